```python
import math
import jax, jax.numpy as jnp
from jax import lax
import numpy as np

D_MODEL = 1024
BATCH = 8
SEQ = 4096
DEPTH = 1

MEM_LEN = 256
EPS = 1e-6
CONV_WIDTH = CONV_HEADS = None
CONV_K = 3
CONV_GROUPS = 8
CONV_DIM = D_MODEL
GM_HEADS = 8
GM_HEAD_DIM = D_MODEL // GM_HEADS
GM_DIM = GM_HEADS * GM_HEAD_DIM
CHUNK = 128
MIX_DIM = CONV_DIM + GM_DIM
IN_DIM = 4 * CONV_DIM + 3 * GM_DIM
X_HEADS = 4
X_HEAD_DIM = D_MODEL // X_HEADS

kernel_name = "hybrid_shortconv_gmlp_memxattn_block"


def rms_norm(x, g):
    xf = x.astype(jnp.float32)
    y = xf * lax.rsqrt(jnp.mean(xf * xf, axis=-1, keepdims=True) + EPS)
    return (y * g.astype(jnp.float32)).astype(x.dtype)


def causal_depthwise_conv(h, w):
    c = h.shape[-1]
    return lax.conv_general_dilated(
        h, w[:, None, :].astype(h.dtype), window_strides=(1,),
        padding=[(CONV_K - 1, 0)], dimension_numbers=("NWC", "WIO", "NWC"),
        feature_group_count=c)


def chunked_spatial_gating(u, v, ln_g, ln_b, ws, bs):
    b, s, _ = v.shape
    n = s // CHUNK
    vh = v.reshape(b, n, CHUNK, GM_HEADS, GM_HEAD_DIM).astype(jnp.float32)
    mu = jnp.mean(vh, axis=-1, keepdims=True)
    var = jnp.mean(jnp.square(vh - mu), axis=-1, keepdims=True)
    vn = (vh - mu) * lax.rsqrt(var + EPS)
    vn = (vn * ln_g.reshape(GM_HEADS, GM_HEAD_DIM).astype(jnp.float32)
          + ln_b.reshape(GM_HEADS, GM_HEAD_DIM).astype(jnp.float32)).astype(v.dtype)
    mask = jnp.tril(jnp.ones((CHUNK, CHUNK), dtype=bool))
    w_c = jnp.where(mask[None], ws, jnp.zeros_like(ws))
    sp = jnp.einsum("hts,bnshc->bnthc", w_c, vn) + bs.T[:, :, None]
    return u * sp.reshape(b, s, GM_DIM)


def mixer_sublayer(h, w_in, conv_w, gm_ln_g, gm_ln_b, gm_ws, gm_bs, w_out):
    proj = h @ w_in
    gb, gc, xa, za, u, v, zb = jnp.split(
        proj, np.cumsum([CONV_DIM] * 4 + [GM_DIM] * 2).tolist(), axis=-1)
    a = gb * causal_depthwise_conv(gc * xa, conv_w)
    a = a * jax.nn.silu(za)
    bo = chunked_spatial_gating(jax.nn.gelu(u), jax.nn.gelu(v),
                                gm_ln_g, gm_ln_b, gm_ws, gm_bs)
    bo = bo * jax.nn.silu(zb)
    return jnp.concatenate([a, bo], axis=-1) @ w_out


def memory_cross_attention(h, m, w_q, w_kv, w_xo):
    b, s, _ = h.shape
    q = (h @ w_q).reshape(b, s, X_HEADS, X_HEAD_DIM)
    k, vv = jnp.split(m @ w_kv, 2, axis=-1)
    k = k.reshape(b, MEM_LEN, X_HEADS, X_HEAD_DIM)
    vv = vv.reshape(b, MEM_LEN, X_HEADS, X_HEAD_DIM)
    scores = jnp.einsum("bshd,bmhd->bhsm", q, k).astype(jnp.float32)
    p = jax.nn.softmax(scores * (1.0 / math.sqrt(X_HEAD_DIM)), axis=-1).astype(vv.dtype)
    o = jnp.einsum("bhsm,bmhd->bshd", p, vv).reshape(b, s, D_MODEL)
    return o @ w_xo


def _fwd_setup_inputs(seed: int = 0) -> dict:
    key = jax.random.key(seed)
    ks = jax.random.split(key, 20)
    f32 = jnp.float32
    L = DEPTH
    nrm = lambda k, shape, scale: jax.random.normal(k, shape, f32) * scale
    return {
        "x": nrm(ks[0], (BATCH, SEQ, D_MODEL), 1.0),
        "mem": nrm(ks[1], (BATCH, MEM_LEN, D_MODEL), 1.0),
        "norm_mix_g": 1.0 + nrm(ks[2], (L, D_MODEL), 0.02),
        "w_in": nrm(ks[3], (L, D_MODEL, IN_DIM), D_MODEL ** -0.5),
        "conv_w": nrm(ks[4], (L, CONV_K, CONV_DIM), CONV_K ** -0.5),
        "gm_ln_g": 1.0 + nrm(ks[5], (L, GM_DIM), 0.02),
        "gm_ln_b": nrm(ks[6], (L, GM_DIM), 0.02),
        "gm_ws": nrm(ks[7], (L, GM_HEADS, CHUNK, CHUNK), 0.5 * CHUNK ** -0.5),
        "gm_bs": 1.0 + nrm(ks[8], (L, GM_HEADS, CHUNK), 0.02),
        "w_out": nrm(ks[9], (L, MIX_DIM, D_MODEL), MIX_DIM ** -0.5),
        "norm_x_g": 1.0 + nrm(ks[10], (L, D_MODEL), 0.02),
        "norm_mem_g": 1.0 + nrm(ks[11], (L, D_MODEL), 0.02),
        "w_q": nrm(ks[12], (L, D_MODEL, D_MODEL), D_MODEL ** -0.5),
        "w_kv": nrm(ks[13], (L, D_MODEL, 2 * D_MODEL), D_MODEL ** -0.5),
        "w_xo": nrm(ks[14], (L, D_MODEL, D_MODEL), D_MODEL ** -0.5),
        "norm_final_g": 1.0 + nrm(ks[15], (D_MODEL,), 0.02),
    }


def _fwd_reference(x, mem, norm_mix_g, w_in, conv_w, gm_ln_g, gm_ln_b, gm_ws, gm_bs,
              w_out, norm_x_g, norm_mem_g, w_q, w_kv, w_xo, norm_final_g):
    for l in range(DEPTH):
        h = rms_norm(x, norm_mix_g[l])
        x = x + mixer_sublayer(h, w_in[l], conv_w[l], gm_ln_g[l], gm_ln_b[l],
                               gm_ws[l], gm_bs[l], w_out[l])
        h = rms_norm(x, norm_x_g[l])
        m = rms_norm(mem, norm_mem_g[l])
        x = x + memory_cross_attention(h, m, w_q[l], w_kv[l], w_xo[l])
    return rms_norm(x, norm_final_g)


import jax as _jax
import jax.numpy as _jnp

TWIN_FORMAT = 'train_step'
FWD_PARAMS = ['x', 'mem', 'norm_mix_g', 'w_in', 'conv_w', 'gm_ln_g', 'gm_ln_b', 'gm_ws', 'gm_bs', 'w_out', 'norm_x_g', 'norm_mem_g', 'w_q', 'w_kv', 'w_xo', 'norm_final_g']
TWIN_WEIGHTS = ['norm_mix_g', 'w_in', 'conv_w', 'gm_ln_g', 'gm_ln_b', 'gm_ws', 'gm_bs', 'w_out', 'norm_x_g', 'norm_mem_g', 'w_q', 'w_kv', 'w_xo', 'norm_final_g']
TWIN_DIFF_INPUT = 'x'
TWIN_INPUTS = ['x', 'mem', 'norm_mix_g', 'w_in', 'conv_w', 'gm_ln_g', 'gm_ln_b', 'gm_ws', 'gm_bs', 'w_out', 'norm_x_g', 'norm_mem_g', 'w_q', 'w_kv', 'w_xo', 'norm_final_g', 'loss_target', 'm_norm_mix_g', 'm_w_in', 'm_conv_w', 'm_gm_ln_g', 'm_gm_ln_b', 'm_gm_ws', 'm_gm_bs', 'm_w_out', 'm_norm_x_g', 'm_norm_mem_g', 'm_w_q', 'm_w_kv', 'm_w_xo', 'm_norm_final_g', 'v_norm_mix_g', 'v_w_in', 'v_conv_w', 'v_gm_ln_g', 'v_gm_ln_b', 'v_gm_ws', 'v_gm_bs', 'v_w_out', 'v_norm_x_g', 'v_norm_mem_g', 'v_w_q', 'v_w_kv', 'v_w_xo', 'v_norm_final_g']
TWIN_OUTPUTS = ['loss', 'grad_x', 'grad_norm_mix_g', 'grad_w_in', 'grad_conv_w', 'grad_gm_ln_g', 'grad_gm_ln_b', 'grad_gm_ws', 'grad_gm_bs', 'grad_w_out', 'grad_norm_x_g', 'grad_norm_mem_g', 'grad_w_q', 'grad_w_kv', 'grad_w_xo', 'grad_norm_final_g', 'delta_norm_mix_g', 'delta_w_in', 'delta_conv_w', 'delta_gm_ln_g', 'delta_gm_ln_b', 'delta_gm_ws', 'delta_gm_bs', 'delta_w_out', 'delta_norm_x_g', 'delta_norm_mem_g', 'delta_w_q', 'delta_w_kv', 'delta_w_xo', 'delta_norm_final_g', 'new_m_norm_mix_g', 'new_m_w_in', 'new_m_conv_w', 'new_m_gm_ln_g', 'new_m_gm_ln_b', 'new_m_gm_ws', 'new_m_gm_bs', 'new_m_w_out', 'new_m_norm_x_g', 'new_m_norm_mem_g', 'new_m_w_q', 'new_m_w_kv', 'new_m_w_xo', 'new_m_norm_final_g', 'new_v_norm_mix_g', 'new_v_w_in', 'new_v_conv_w', 'new_v_gm_ln_g', 'new_v_gm_ln_b', 'new_v_gm_ws', 'new_v_gm_bs', 'new_v_w_out', 'new_v_norm_x_g', 'new_v_norm_mem_g', 'new_v_w_q', 'new_v_w_kv', 'new_v_w_xo', 'new_v_norm_final_g']
TWIN_LEAF_KINDS = {'loss': 'loss', 'grad_x': 'grad_x', 'grad_norm_mix_g': 'grad_w', 'grad_w_in': 'grad_w', 'grad_conv_w': 'grad_w', 'grad_gm_ln_g': 'grad_w', 'grad_gm_ln_b': 'grad_w', 'grad_gm_ws': 'grad_w', 'grad_gm_bs': 'grad_w', 'grad_w_out': 'grad_w', 'grad_norm_x_g': 'grad_w', 'grad_norm_mem_g': 'grad_w', 'grad_w_q': 'grad_w', 'grad_w_kv': 'grad_w', 'grad_w_xo': 'grad_w', 'grad_norm_final_g': 'grad_w', 'delta_norm_mix_g': 'delta_w', 'delta_w_in': 'delta_w', 'delta_conv_w': 'delta_w', 'delta_gm_ln_g': 'delta_w', 'delta_gm_ln_b': 'delta_w', 'delta_gm_ws': 'delta_w', 'delta_gm_bs': 'delta_w', 'delta_w_out': 'delta_w', 'delta_norm_x_g': 'delta_w', 'delta_norm_mem_g': 'delta_w', 'delta_w_q': 'delta_w', 'delta_w_kv': 'delta_w', 'delta_w_xo': 'delta_w', 'delta_norm_final_g': 'delta_w', 'new_m_norm_mix_g': 'new_m', 'new_m_w_in': 'new_m', 'new_m_conv_w': 'new_m', 'new_m_gm_ln_g': 'new_m', 'new_m_gm_ln_b': 'new_m', 'new_m_gm_ws': 'new_m', 'new_m_gm_bs': 'new_m', 'new_m_w_out': 'new_m', 'new_m_norm_x_g': 'new_m', 'new_m_norm_mem_g': 'new_m', 'new_m_w_q': 'new_m', 'new_m_w_kv': 'new_m', 'new_m_w_xo': 'new_m', 'new_m_norm_final_g': 'new_m', 'new_v_norm_mix_g': 'new_v', 'new_v_w_in': 'new_v', 'new_v_conv_w': 'new_v', 'new_v_gm_ln_g': 'new_v', 'new_v_gm_ln_b': 'new_v', 'new_v_gm_ws': 'new_v', 'new_v_gm_bs': 'new_v', 'new_v_w_out': 'new_v', 'new_v_norm_x_g': 'new_v', 'new_v_norm_mem_g': 'new_v', 'new_v_w_q': 'new_v', 'new_v_w_kv': 'new_v', 'new_v_w_xo': 'new_v', 'new_v_norm_final_g': 'new_v'}


def _forward(args):
    return _fwd_reference(*[args[k] for k in FWD_PARAMS])


def _output_shape():
    def fwd():
        inp = _fwd_setup_inputs(0)
        return _fwd_reference(*[inp[k] for k in FWD_PARAMS])
    out = _jax.eval_shape(fwd)
    return out.shape, out.dtype

N_MICROBATCH = 1
ADAM_LR = 0.001
ADAM_B1 = 0.9
ADAM_B2 = 0.999
ADAM_EPS = 1e-08
ADAM_WD = 0.01
ADAM_STEP = 10
PER_EXAMPLE_BATCH_AXIS = {'x': 0, 'mem': 0, 'loss_target': 0}
SHARED_INPUTS = []
_WEIGHT_DTYPES = {'norm_mix_g': _jnp.float32, 'w_in': _jnp.float32, 'conv_w': _jnp.float32, 'gm_ln_g': _jnp.float32, 'gm_ln_b': _jnp.float32, 'gm_ws': _jnp.float32, 'gm_bs': _jnp.float32, 'w_out': _jnp.float32, 'norm_x_g': _jnp.float32, 'norm_mem_g': _jnp.float32, 'w_q': _jnp.float32, 'w_kv': _jnp.float32, 'w_xo': _jnp.float32, 'norm_final_g': _jnp.float32}
MOMENT_SCALE = {'norm_mix_g': 1.477042e-01, 'w_in': 5.714216e-02, 'conv_w': 6.553952e-02, 'gm_ln_g': 1.548761e-02, 'gm_ln_b': 1.642428e-02, 'gm_ws': 3.075285e-02, 'gm_bs': 4.269660e-02, 'w_out': 8.035369e-02, 'norm_x_g': 1.627111e-02, 'norm_mem_g': 2.519844e-02, 'w_q': 1.642874e-02, 'w_kv': 1.657381e-02, 'w_xo': 1.661707e-02, 'norm_final_g': 3.199440e+01}


def _to_microbatches(a, axis):
    t = _jnp.moveaxis(a, axis, 0)
    t = t.reshape((N_MICROBATCH, t.shape[0] // N_MICROBATCH) + t.shape[1:])
    return _jnp.moveaxis(t, 1, axis + 1)


def setup_inputs(seed: int = 0) -> dict:
    inp = _fwd_setup_inputs(seed)
    key = _jax.random.fold_in(_jax.random.key(seed), 7919)
    shape, _ = _output_shape()
    out = dict(inp)
    out["loss_target"] = _jax.random.normal(_jax.random.fold_in(key, 0), shape, _jnp.float32)
    for i, name in enumerate(TWIN_WEIGHTS):
        w = inp[name].astype(_jnp.float32)
        if MOMENT_SCALE is None:
            s = _jnp.sqrt(_jnp.mean(_jnp.square(w)) + 1e-30)
        else:
            s = MOMENT_SCALE[name]
        km, kv = _jax.random.split(_jax.random.fold_in(key, i + 1))
        out[name] = w
        out["m_" + name] = s * _jax.random.normal(km, w.shape, _jnp.float32)
        out["v_" + name] = (s * s) * _jax.random.uniform(kv, w.shape, _jnp.float32, 0.5, 1.5)
    if N_MICROBATCH > 1:
        for name, axis in PER_EXAMPLE_BATCH_AXIS.items():
            out[name] = _to_microbatches(out[name], axis)
    return {'x': out['x'], 'mem': out['mem'], 'norm_mix_g': out['norm_mix_g'], 'w_in': out['w_in'], 'conv_w': out['conv_w'], 'gm_ln_g': out['gm_ln_g'], 'gm_ln_b': out['gm_ln_b'], 'gm_ws': out['gm_ws'], 'gm_bs': out['gm_bs'], 'w_out': out['w_out'], 'norm_x_g': out['norm_x_g'], 'norm_mem_g': out['norm_mem_g'], 'w_q': out['w_q'], 'w_kv': out['w_kv'], 'w_xo': out['w_xo'], 'norm_final_g': out['norm_final_g'], 'loss_target': out['loss_target'], 'm_norm_mix_g': out['m_norm_mix_g'], 'm_w_in': out['m_w_in'], 'm_conv_w': out['m_conv_w'], 'm_gm_ln_g': out['m_gm_ln_g'], 'm_gm_ln_b': out['m_gm_ln_b'], 'm_gm_ws': out['m_gm_ws'], 'm_gm_bs': out['m_gm_bs'], 'm_w_out': out['m_w_out'], 'm_norm_x_g': out['m_norm_x_g'], 'm_norm_mem_g': out['m_norm_mem_g'], 'm_w_q': out['m_w_q'], 'm_w_kv': out['m_w_kv'], 'm_w_xo': out['m_w_xo'], 'm_norm_final_g': out['m_norm_final_g'], 'v_norm_mix_g': out['v_norm_mix_g'], 'v_w_in': out['v_w_in'], 'v_conv_w': out['v_conv_w'], 'v_gm_ln_g': out['v_gm_ln_g'], 'v_gm_ln_b': out['v_gm_ln_b'], 'v_gm_ws': out['v_gm_ws'], 'v_gm_bs': out['v_gm_bs'], 'v_w_out': out['v_w_out'], 'v_norm_x_g': out['v_norm_x_g'], 'v_norm_mem_g': out['v_norm_mem_g'], 'v_w_q': out['v_w_q'], 'v_w_kv': out['v_w_kv'], 'v_w_xo': out['v_w_xo'], 'v_norm_final_g': out['v_norm_final_g']}


def _loss(weights, diff, rest, loss_target):
    with _jax.named_scope("forward"):
        args = {**rest, TWIN_DIFF_INPUT: diff, **{k: w.astype(_WEIGHT_DTYPES[k]) for k, w in weights.items()}}
        y = _forward(args)
    with _jax.named_scope("loss_head"):
        err = _jnp.square(y.astype(_jnp.float32) - loss_target)
        return 0.5 * _jnp.sum(_jnp.mean(err, axis=-1)) if err.ndim else 0.5 * err


def _adamw(w, g, m, v):
    m = ADAM_B1 * m + (1.0 - ADAM_B1) * g
    v = ADAM_B2 * v + (1.0 - ADAM_B2) * _jnp.square(g)
    m_hat = m / (1.0 - ADAM_B1 ** ADAM_STEP)
    v_hat = v / (1.0 - ADAM_B2 ** ADAM_STEP)
    delta = -ADAM_LR * (m_hat / (_jnp.sqrt(v_hat) + ADAM_EPS) + ADAM_WD * w)
    return delta, m, v


def reference(x, mem, norm_mix_g, w_in, conv_w, gm_ln_g, gm_ln_b, gm_ws, gm_bs, w_out, norm_x_g, norm_mem_g, w_q, w_kv, w_xo, norm_final_g, loss_target, m_norm_mix_g, m_w_in, m_conv_w, m_gm_ln_g, m_gm_ln_b, m_gm_ws, m_gm_bs, m_w_out, m_norm_x_g, m_norm_mem_g, m_w_q, m_w_kv, m_w_xo, m_norm_final_g, v_norm_mix_g, v_w_in, v_conv_w, v_gm_ln_g, v_gm_ln_b, v_gm_ws, v_gm_bs, v_w_out, v_norm_x_g, v_norm_mem_g, v_w_q, v_w_kv, v_w_xo, v_norm_final_g):
    given = dict(x=x, mem=mem, norm_mix_g=norm_mix_g, w_in=w_in, conv_w=conv_w, gm_ln_g=gm_ln_g, gm_ln_b=gm_ln_b, gm_ws=gm_ws, gm_bs=gm_bs, w_out=w_out, norm_x_g=norm_x_g, norm_mem_g=norm_mem_g, w_q=w_q, w_kv=w_kv, w_xo=w_xo, norm_final_g=norm_final_g, loss_target=loss_target, m_norm_mix_g=m_norm_mix_g, m_w_in=m_w_in, m_conv_w=m_conv_w, m_gm_ln_g=m_gm_ln_g, m_gm_ln_b=m_gm_ln_b, m_gm_ws=m_gm_ws, m_gm_bs=m_gm_bs, m_w_out=m_w_out, m_norm_x_g=m_norm_x_g, m_norm_mem_g=m_norm_mem_g, m_w_q=m_w_q, m_w_kv=m_w_kv, m_w_xo=m_w_xo, m_norm_final_g=m_norm_final_g, v_norm_mix_g=v_norm_mix_g, v_w_in=v_w_in, v_conv_w=v_conv_w, v_gm_ln_g=v_gm_ln_g, v_gm_ln_b=v_gm_ln_b, v_gm_ws=v_gm_ws, v_gm_bs=v_gm_bs, v_w_out=v_w_out, v_norm_x_g=v_norm_x_g, v_norm_mem_g=v_norm_mem_g, v_w_q=v_w_q, v_w_kv=v_w_kv, v_w_xo=v_w_xo, v_norm_final_g=v_norm_final_g)
    weights = {n: given[n] for n in TWIN_WEIGHTS}
    shared = {n: given[n] for n in SHARED_INPUTS}
    per_example = {n: given[n] for n in ['x', 'mem']}
    grad_fn = _jax.value_and_grad(_loss, argnums=(0, 1))

    def one_microbatch(ex, loss_target):
        ex = dict(ex)
        diff = ex.pop(TWIN_DIFF_INPUT)
        return grad_fn(weights, diff, {**shared, **ex}, loss_target)

    if N_MICROBATCH == 1:
        loss, (grad_w, grad_x) = one_microbatch(per_example, given["loss_target"])
    else:
        def body(carry, xs):
            loss_sum, grad_sum = carry
            l_k, (gw_k, gx_k) = one_microbatch(xs[0], xs[1])
            with _jax.named_scope("update"):
                return (loss_sum + l_k, _jax.tree.map(_jnp.add, grad_sum, gw_k)), gx_k

        init = (_jnp.zeros((), _jnp.float32), _jax.tree.map(_jnp.zeros_like, weights))
        (loss, grad_w), grad_x = _jax.lax.scan(body, init, (per_example, given["loss_target"]))
    with _jax.named_scope("update"):
        delta_w, new_m, new_v = {}, {}, {}
        for n in TWIN_WEIGHTS:
            delta_w[n], new_m[n], new_v[n] = _adamw(weights[n], grad_w[n], given["m_" + n], given["v_" + n])
    return (loss, grad_x, *[grad_w[n] for n in TWIN_WEIGHTS], *[delta_w[n] for n in TWIN_WEIGHTS],
            *[new_m[n] for n in TWIN_WEIGHTS], *[new_v[n] for n in TWIN_WEIGHTS])
```

```python
import functools
import math

import jax
import jax.numpy as jnp
from jax import lax
from jax.experimental import pallas as pl
from jax.experimental.pallas import tpu as pltpu

F32 = jnp.float32
BF16 = jnp.bfloat16
MESH = pl.DeviceIdType.MESH

D = 1024
SLAB = 1024
N_SLAB = 7
IN_DIM = N_SLAB * SLAB
MIX = 2 * SLAB
HEADS = 8
CH = 128
XH = 4
XD = D // XH
EPS = 1e-6
GELU_C = math.sqrt(2.0 / math.pi)
GELU_A = 0.044715
N_CHIP = 4
IN_BLK = IN_DIM // N_CHIP
KV_BLK = 2 * D // N_CHIP

ADAM_LR, ADAM_B1, ADAM_B2, ADAM_EPS, ADAM_WD, ADAM_STEP = 0.001, 0.9, 0.999, 1e-08, 0.01, 10

VMEM_LIMIT = 60 * 1024 * 1024


def _cp(sem=None, vmem=None):
    return pltpu.CompilerParams(dimension_semantics=sem, vmem_limit_bytes=vmem)


def _full(shape, buffers=None):
    n = len(shape)
    if buffers is None:
        return pl.BlockSpec(shape, lambda *_: (0,) * n)
    return pl.BlockSpec(shape, lambda *_: (0,) * n, pipeline_mode=pl.Buffered(buffers))


ANY = pl.BlockSpec(memory_space=pl.ANY)


def _bdot(a, b):
    return jnp.dot(a.astype(BF16), b.astype(BF16), preferred_element_type=F32)


def _bdot_nt(a, b):
    return lax.dot_general(a.astype(BF16), b.astype(BF16), (((1,), (1,)), ((), ())), preferred_element_type=F32)


def _rms(x, g):
    r = lax.rsqrt(jnp.mean(x * x, axis=-1, keepdims=True) + EPS)
    return x * r * g, r


def _rms_bwd(dy, x, r, g):
    gdy = dy * g
    dx = r * gdy - x * (r * r * r) * jnp.mean(x * gdy, axis=-1, keepdims=True)
    dg = jnp.sum(dy * x * r, axis=0, keepdims=True)
    return dx, dg


def _gelu_parts(x):
    x2 = x * x
    t = jnp.tanh(GELU_C * (x + GELU_A * x * x2))
    val = 0.5 * x * (1.0 + t)
    grad = 0.5 * (1.0 + t) + 0.5 * x * (1.0 - t * t) * (GELU_C * (1.0 + 3.0 * GELU_A * x2))
    return val, grad


def _gelu(x):
    return 0.5 * x * (1.0 + jnp.tanh(GELU_C * (x + GELU_A * x * x * x)))


def _sigmoid(z):
    return 1.0 / (1.0 + jnp.exp(-z))


def _cast_shards(arrs):
    n = len(arrs)
    steps = 8

    def body(*refs):
        for i in range(n):
            refs[n + i][...] = refs[i][...].astype(BF16)

    specs = [pl.BlockSpec((a.shape[0] // steps, a.shape[1]), lambda i: (i, 0)) for a in arrs]
    return pl.pallas_call(
        body, grid=(steps,), in_specs=specs, out_specs=specs,
        out_shape=[jax.ShapeDtypeStruct(a.shape, BF16) for a in arrs],
        compiler_params=_cp(("parallel",)), name="cast_shards")(*arrs)


def _norm_in(x, g, tm=512):
    t = x.shape[0]

    def body(x_ref, g_ref, h_ref, ht_ref):
        h, _ = _rms(x_ref[...], g_ref[...])
        h_ref[...] = h.astype(BF16)
        ht_ref[...] = h.T.astype(BF16)

    return pl.pallas_call(
        body, grid=(t // tm,),
        in_specs=[pl.BlockSpec((tm, D), lambda i: (i, 0)), _full((1, D))],
        out_specs=[pl.BlockSpec((tm, D), lambda i: (i, 0)), pl.BlockSpec((D, tm), lambda i: (0, i))],
        out_shape=[jax.ShapeDtypeStruct((t, D), BF16), jax.ShapeDtypeStruct((D, t), BF16)],
        compiler_params=_cp(("parallel",)), name="norm_in")(x, g)


def _proj(h, win_f, tm=1024):
    t = h.shape[0]

    def body(h_ref, w_ref, o_ref):
        o_ref[...] = jnp.dot(h_ref[...], w_ref[...], preferred_element_type=F32).astype(BF16)

    return pl.pallas_call(
        body, grid=(N_CHIP, t // tm),
        in_specs=[pl.BlockSpec((tm, D), lambda j, i: (i, 0)), pl.BlockSpec((None, D, IN_BLK), lambda j, i: (j, 0, 0))],
        out_specs=pl.BlockSpec((tm, IN_BLK), lambda j, i: (i, j)),
        out_shape=jax.ShapeDtypeStruct((t, IN_DIM), BF16),
        compiler_params=_cp(("parallel", "parallel"), VMEM_LIMIT), name="proj")(h, win_f)


def _mixer_fwd(proj, cw8, lng, lnb, wc, bsb, tm=256):
    t = proj.shape[0]
    nch = tm // CH

    def body(p_ref, cw_ref, lng_ref, lnb_ref, wc_ref, bsb_ref, mix_ref, mixt_ref, prev_ref, stage_ref):
        @pl.when(pl.program_id(0) == 0)
        def _():
            prev_ref[...] = jnp.zeros_like(prev_ref)

        rows = lax.broadcasted_iota(jnp.int32, (tm, CH), 0)
        for s in range(HEADS):
            cs = pl.ds(CH * s, CH)

            def slab(k):
                return p_ref[:, pl.ds(k * SLAB + CH * s, CH)].astype(F32)

            gb, gc, xa, za = slab(0), slab(1), slab(2), slab(3)
            cx = gc * xa
            p6 = jnp.broadcast_to(prev_ref[6:7, cs], (tm, CH))
            p7 = jnp.broadcast_to(prev_ref[7:8, cs], (tm, CH))
            c1 = jnp.where(rows == 0, p7, pltpu.roll(cx, 1, 0))
            c2 = jnp.where(rows == 0, p6, jnp.where(rows == 1, p7, pltpu.roll(cx, 2, 0)))
            prev_ref[:, cs] = cx[tm - 8:, :]
            cv = cw_ref[0:1, cs] * c2 + cw_ref[1:2, cs] * c1 + cw_ref[2:3, cs] * cx
            stage_ref[:, cs] = gb * cv * (za * _sigmoid(za))

            u, v, zb = slab(4), slab(5), slab(6)
            ug, vg = _gelu(u), _gelu(v)
            dlt = vg - jnp.mean(vg, axis=-1, keepdims=True)
            vhat = dlt * lax.rsqrt(jnp.mean(dlt * dlt, axis=-1, keepdims=True) + EPS)
            vn = (vhat * lng_ref[:, cs] + lnb_ref[:, cs]).astype(BF16)
            gate = ug * (zb * _sigmoid(zb))
            for c in range(nch):
                rs = slice(CH * c, CH * (c + 1))
                sp = jnp.dot(wc_ref[s], vn[rs], preferred_element_type=F32) + bsb_ref[s]
                stage_ref[rs, pl.ds(SLAB + CH * s, CH)] = gate[rs] * sp

        full = stage_ref[...]
        mix_ref[...] = full.astype(BF16)
        mixt_ref[...] = full.T.astype(BF16)

    return pl.pallas_call(
        body, grid=(t // tm,),
        in_specs=[pl.BlockSpec((tm, IN_DIM), lambda i: (i, 0)), _full((8, D)), _full((1, D)), _full((1, D)),
                  _full((HEADS, CH, CH)), _full((HEADS, CH, CH))],
        out_specs=[pl.BlockSpec((tm, MIX), lambda i: (i, 0)), pl.BlockSpec((MIX, tm), lambda i: (0, i))],
        out_shape=[jax.ShapeDtypeStruct((t, MIX), BF16), jax.ShapeDtypeStruct((MIX, t), BF16)],
        scratch_shapes=[pltpu.VMEM((8, D), F32), pltpu.VMEM((tm, MIX), F32)],
        compiler_params=_cp(("arbitrary",), VMEM_LIMIT), name="mixer_fwd")(proj, cw8, lng, lnb, wc, bsb)


def _mem_fwd(mem, gm, wkv_f):
    n_mem = mem.shape[0]

    def body(mem_ref, gm_ref, w_ref, k_ref, v_ref, mt_ref):
        m, _ = _rms(mem_ref[...], gm_ref[...])
        mb = m.astype(BF16)
        mt_ref[...] = m.T.astype(BF16)
        for j in range(N_CHIP):
            dst = k_ref if j < 2 else v_ref
            dst[:, pl.ds(KV_BLK * (j % 2), KV_BLK)] = jnp.dot(mb, w_ref[j], preferred_element_type=F32).astype(BF16)

    return pl.pallas_call(
        body, out_shape=[jax.ShapeDtypeStruct((n_mem, D), BF16), jax.ShapeDtypeStruct((n_mem, D), BF16),
                         jax.ShapeDtypeStruct((D, n_mem), BF16)],
        compiler_params=_cp(None, VMEM_LIMIT), name="mem_fwd")(mem, gm, wkv_f)


def _tail(x, tgt, mixin, wout, wq, wxo, k, v, g2, g3, tm=256):
    t = x.shape[0]
    n_mem = k.shape[0]
    scale = 1.0 / math.sqrt(XD)

    def body(x_ref, tgt_ref, mix_ref, wout_ref, wq_ref, wxo_ref, k_ref, v_ref, g2_ref, g3_ref,
             loss_ref, dmix_ref, dx1_ref, dx1b_ref, h2t_ref, dq_ref, ot_ref, dx2b_ref, dk_ref, dv_ref, dg2_ref, dg3_ref):
        @pl.when(pl.program_id(0) == 0)
        def _():
            loss_ref[...] = jnp.zeros_like(loss_ref)
            dk_ref[...] = jnp.zeros_like(dk_ref)
            dv_ref[...] = jnp.zeros_like(dv_ref)
            dg2_ref[...] = jnp.zeros_like(dg2_ref)
            dg3_ref[...] = jnp.zeros_like(dg3_ref)

        g2, g3 = g2_ref[...], g3_ref[...]
        x1 = x_ref[...] + jnp.dot(mix_ref[...], wout_ref[...], preferred_element_type=F32)
        h2, r2 = _rms(x1, g2)
        h2t_ref[...] = h2.T.astype(BF16)
        q = _bdot(h2, wq_ref[...]).astype(BF16)
        probs, outs = [], []
        for hd in range(XH):
            hs = pl.ds(XD * hd, XD)
            s = _bdot_nt(q[:, XD * hd:XD * (hd + 1)], k_ref[:, hs]) * scale
            e = jnp.exp(s - jnp.max(s, axis=-1, keepdims=True))
            p = e / jnp.sum(e, axis=-1, keepdims=True)
            probs.append(p)
            outs.append(_bdot(p, v_ref[:, hs]))
        o = jnp.concatenate(outs, axis=-1)
        ot_ref[...] = o.T.astype(BF16)
        x2 = x1 + _bdot(o, wxo_ref[...])
        y, r3 = _rms(x2, g3)
        diff = y - tgt_ref[...]
        row_loss = jnp.sum(diff * diff, axis=-1, keepdims=True)
        loss_ref[...] += jnp.broadcast_to(jnp.sum(row_loss, axis=0, keepdims=True) * (0.5 / D), loss_ref.shape)

        dx2, dg3 = _rms_bwd(diff * (1.0 / D), x2, r3, g3)
        dg3_ref[...] += dg3
        dx2b = dx2.astype(BF16)
        dx2b_ref[...] = dx2b
        do = _bdot_nt(dx2b, wxo_ref[...])
        dqs = []
        for hd in range(XH):
            hs = pl.ds(XD * hd, XD)
            p = probs[hd]
            do_h = do[:, XD * hd:XD * (hd + 1)]
            dv_ref[:, hs] += _bdot(p.T, do_h)
            dp = _bdot_nt(do_h, v_ref[:, hs])
            ds = p * (dp - jnp.sum(dp * p, axis=-1, keepdims=True))
            dqs.append(_bdot(ds, k_ref[:, hs]) * scale)
            dk_ref[:, hs] += _bdot(ds.T, q[:, XD * hd:XD * (hd + 1)]) * scale
        dq = jnp.concatenate(dqs, axis=-1).astype(BF16)
        dq_ref[...] = dq
        dx1n, dg2 = _rms_bwd(_bdot_nt(dq, wq_ref[...]), x1, r2, g2)
        dg2_ref[...] += dg2
        dx1 = dx2 + dx1n
        dx1_ref[...] = dx1
        dx1b = dx1.astype(BF16)
        dx1b_ref[...] = dx1b
        dmix_ref[...] = _bdot_nt(dx1b, wout_ref[...]).astype(BF16)

    tok = lambda w: pl.BlockSpec((tm, w), lambda i: (i, 0))
    tok_t = lambda w: pl.BlockSpec((w, tm), lambda i: (0, i))
    return pl.pallas_call(
        body, grid=(t // tm,),
        in_specs=[tok(D), tok(D), tok(MIX), _full((MIX, D), 1), _full((D, D), 1), _full((D, D), 1),
                  _full((n_mem, D), 1), _full((n_mem, D), 1), _full((1, D)), _full((1, D))],
        out_specs=[_full((8, 128)), tok(MIX), tok(D), tok(D), tok_t(D), tok(D), tok_t(D), tok(D),
                   _full((n_mem, D)), _full((n_mem, D)), _full((1, D)), _full((1, D))],
        out_shape=[jax.ShapeDtypeStruct((8, 128), F32), jax.ShapeDtypeStruct((t, MIX), BF16),
                   jax.ShapeDtypeStruct((t, D), F32), jax.ShapeDtypeStruct((t, D), BF16),
                   jax.ShapeDtypeStruct((D, t), BF16), jax.ShapeDtypeStruct((t, D), BF16),
                   jax.ShapeDtypeStruct((D, t), BF16), jax.ShapeDtypeStruct((t, D), BF16),
                   jax.ShapeDtypeStruct((n_mem, D), F32), jax.ShapeDtypeStruct((n_mem, D), F32),
                   jax.ShapeDtypeStruct((1, D), F32), jax.ShapeDtypeStruct((1, D), F32)],
        compiler_params=_cp(("arbitrary",), VMEM_LIMIT), name="tail")(x, tgt, mixin, wout, wq, wxo, k, v, g2, g3)


def _mem_bwd(mem, gm, dk, dv, wkv_f):
    def body(mem_ref, gm_ref, dk_ref, dv_ref, w_ref, dw_ref, dwb_ref, dgm_ref):
        mem_v = mem_ref[...]
        m, rm = _rms(mem_v, gm_ref[...])
        mt = m.T.astype(BF16)
        dm = jnp.zeros_like(mem_v)
        for j in range(N_CHIP):
            src = dk_ref if j < 2 else dv_ref
            dkv = src[:, pl.ds(KV_BLK * (j % 2), KV_BLK)].astype(BF16)
            dw = jnp.dot(mt, dkv, preferred_element_type=F32)
            dw_ref[j] = dw
            dwb_ref[j] = dw.astype(BF16)
            dm = dm + _bdot_nt(dkv, w_ref[j])
        dgm_ref[...] = jnp.sum(dm * mem_v * rm, axis=0, keepdims=True)

    return pl.pallas_call(
        body, out_shape=[jax.ShapeDtypeStruct((N_CHIP, D, KV_BLK), F32), jax.ShapeDtypeStruct((N_CHIP, D, KV_BLK), BF16),
                         jax.ShapeDtypeStruct((1, D), F32)],
        compiler_params=_cp(None, VMEM_LIMIT), name="mem_bwd")(mem, gm, dk, dv, wkv_f)


def _mixer_bwd(proj, dmix, cw8, lng, lnb, wc, wct, bsb, tm=256):
    t = proj.shape[0]
    nt = t // tm
    nch = tm // CH
    hb = 16

    def body(p_ref, pgc_ref, pxa_ref, dm_ref, cw_ref, lng_ref, lnb_ref, wc_ref, wct_ref, bsb_ref,
             dp_ref, dcw_ref, dlng_ref, dlnb_ref, dwc_ref, dbs_ref, next_ref):
        i = pl.program_id(0)

        @pl.when(i == 0)
        def _():
            next_ref[...] = jnp.zeros_like(next_ref)
            dcw_ref[...] = jnp.zeros_like(dcw_ref)
            dlng_ref[...] = jnp.zeros_like(dlng_ref)
            dlnb_ref[...] = jnp.zeros_like(dlnb_ref)
            dwc_ref[...] = jnp.zeros_like(dwc_ref)
            dbs_ref[...] = jnp.zeros_like(dbs_ref)

        first_tile = i == nt - 1
        rows = lax.broadcasted_iota(jnp.int32, (tm, CH), 0)
        ones8 = jnp.ones((8, CH), BF16)
        for s in range(HEADS):
            cs = pl.ds(CH * s, CH)

            def slab(k):
                return p_ref[:, pl.ds(k * SLAB + CH * s, CH)].astype(F32)

            gb, gc, xa, za = slab(0), slab(1), slab(2), slab(3)
            da = dm_ref[:, cs].astype(F32)
            cx = gc * xa
            cxp = pgc_ref[:, cs].astype(F32) * pxa_ref[:, cs].astype(F32)
            cxp = jnp.where(first_tile, jnp.zeros_like(cxp), cxp)
            p6 = jnp.broadcast_to(cxp[hb - 2:hb - 1, :], (tm, CH))
            p7 = jnp.broadcast_to(cxp[hb - 1:hb, :], (tm, CH))
            c1 = jnp.where(rows == 0, p7, pltpu.roll(cx, 1, 0))
            c2 = jnp.where(rows == 0, p6, jnp.where(rows == 1, p7, pltpu.roll(cx, 2, 0)))
            w0, w1, w2 = cw_ref[0:1, cs], cw_ref[1:2, cs], cw_ref[2:3, cs]
            cv = w0 * c2 + w1 * c1 + w2 * cx
            sg = _sigmoid(za)
            sa = za * sg
            dcv = da * gb * sa
            dp_ref[:, pl.ds(0 * SLAB + CH * s, CH)] = (da * cv * sa).astype(BF16)
            dp_ref[:, pl.ds(3 * SLAB + CH * s, CH)] = (da * gb * cv * (sg * (1.0 + za * (1.0 - sg)))).astype(BF16)
            n0 = jnp.broadcast_to(next_ref[0:1, cs], (tm, CH))
            n1 = jnp.broadcast_to(next_ref[1:2, cs], (tm, CH))
            u1 = jnp.where(rows == tm - 1, n0, pltpu.roll(dcv, tm - 1, 0))
            u2 = jnp.where(rows == tm - 2, n0, jnp.where(rows == tm - 1, n1, pltpu.roll(dcv, tm - 2, 0)))
            next_ref[:, cs] = dcv[0:8, :]
            dcx = w2 * dcv + w1 * u1 + w0 * u2
            dp_ref[:, pl.ds(1 * SLAB + CH * s, CH)] = (dcx * xa).astype(BF16)
            dp_ref[:, pl.ds(2 * SLAB + CH * s, CH)] = (dcx * gc).astype(BF16)
            dcw_ref[0:1, cs] += jnp.sum(dcv * c2, axis=0, keepdims=True)
            dcw_ref[1:2, cs] += jnp.sum(dcv * c1, axis=0, keepdims=True)
            dcw_ref[2:3, cs] += jnp.sum(dcv * cx, axis=0, keepdims=True)

            u, v, zb = slab(4), slab(5), slab(6)
            db = dm_ref[:, pl.ds(SLAB + CH * s, CH)].astype(F32)
            ug, ugrad = _gelu_parts(u)
            vg, vgrad = _gelu_parts(v)
            dlt = vg - jnp.mean(vg, axis=-1, keepdims=True)
            rstd = lax.rsqrt(jnp.mean(dlt * dlt, axis=-1, keepdims=True) + EPS)
            vhat = dlt * rstd
            lg = lng_ref[:, cs]
            vn = (vhat * lg + lnb_ref[:, cs]).astype(BF16)
            sgb = _sigmoid(zb)
            szb = zb * sgb
            sps, dvns = [], []
            dbs = jnp.zeros((8, CH), F32)
            dwc = jnp.zeros((CH, CH), F32)
            for c in range(nch):
                rs = slice(CH * c, CH * (c + 1))
                sp = jnp.dot(wc_ref[s], vn[rs], preferred_element_type=F32) + bsb_ref[s]
                dsp = (db[rs] * ug[rs] * szb[rs]).astype(BF16)
                dbs = dbs + lax.dot_general(ones8, dsp, (((1,), (1,)), ((), ())), preferred_element_type=F32)
                dwc = dwc + lax.dot_general(dsp, vn[rs], (((1,), (1,)), ((), ())), preferred_element_type=F32)
                dvns.append(jnp.dot(wct_ref[s], dsp, preferred_element_type=F32))
                sps.append(sp)
            sp = jnp.concatenate(sps, axis=0)
            dvn = jnp.concatenate(dvns, axis=0)
            dbs_ref[:, cs] += dbs
            dwc_ref[s] += dwc
            dlng_ref[:, cs] += jnp.sum(dvn * vhat, axis=0, keepdims=True)
            dlnb_ref[:, cs] += jnp.sum(dvn, axis=0, keepdims=True)
            dvhat = dvn * lg
            dvg = rstd * (dvhat - jnp.mean(dvhat, axis=-1, keepdims=True)
                          - vhat * jnp.mean(dvhat * vhat, axis=-1, keepdims=True))
            dp_ref[:, pl.ds(4 * SLAB + CH * s, CH)] = (db * sp * szb * ugrad).astype(BF16)
            dp_ref[:, pl.ds(5 * SLAB + CH * s, CH)] = (dvg * vgrad).astype(BF16)
            dp_ref[:, pl.ds(6 * SLAB + CH * s, CH)] = (db * ug * sp * (sgb * (1.0 + zb * (1.0 - sgb)))).astype(BF16)

        @pl.when(i == nt - 1)
        def _():
            tril = lax.broadcasted_iota(jnp.int32, (CH, CH), 0) >= lax.broadcasted_iota(jnp.int32, (CH, CH), 1)
            for s in range(HEADS):
                dwc_ref[s] = jnp.where(tril, dwc_ref[s], 0.0)

    rev = lambda i: nt - 1 - i
    halo = lambda col: pl.BlockSpec((hb, SLAB), lambda i: (jnp.maximum(rev(i) * (tm // hb) - 1, 0), col))
    return pl.pallas_call(
        body, grid=(nt,),
        in_specs=[pl.BlockSpec((tm, IN_DIM), lambda i: (rev(i), 0)), halo(1), halo(2),
                  pl.BlockSpec((tm, MIX), lambda i: (rev(i), 0)), _full((8, D)), _full((1, D)), _full((1, D)),
                  _full((HEADS, CH, CH)), _full((HEADS, CH, CH)), _full((HEADS, CH, CH))],
        out_specs=[pl.BlockSpec((tm, IN_DIM), lambda i: (rev(i), 0)), _full((8, D)), _full((1, D)), _full((1, D)),
                   _full((HEADS, CH, CH)), _full((8, D))],
        out_shape=[jax.ShapeDtypeStruct((t, IN_DIM), BF16), jax.ShapeDtypeStruct((8, D), F32),
                   jax.ShapeDtypeStruct((1, D), F32), jax.ShapeDtypeStruct((1, D), F32),
                   jax.ShapeDtypeStruct((HEADS, CH, CH), F32), jax.ShapeDtypeStruct((8, D), F32)],
        scratch_shapes=[pltpu.VMEM((8, D), F32)],
        compiler_params=_cp(("arbitrary",), VMEM_LIMIT), name="mixer_bwd")(proj, proj, proj, dmix, cw8, lng, lnb, wc, wct, bsb)


def _grad_matmul(at, b, *, by_cols, name, tk=1024):
    m, t = at.shape
    n = b.shape[1]
    nk = t // tk
    bm, bn = (m, n // N_CHIP) if by_cols else (m // N_CHIP, n)

    def body(a_ref, b_ref, o_ref, ob_ref):
        kk = pl.program_id(1)
        part = jnp.dot(a_ref[...], b_ref[...], preferred_element_type=F32)

        @pl.when(kk == 0)
        def _():
            o_ref[...] = part

        @pl.when(kk > 0)
        def _():
            o_ref[...] += part

        @pl.when(kk == nk - 1)
        def _():
            ob_ref[...] = o_ref[...].astype(BF16)

    a_spec = pl.BlockSpec((bm, tk), (lambda j, k: (0, k)) if by_cols else (lambda j, k: (j, k)))
    b_spec = pl.BlockSpec((tk, bn), (lambda j, k: (k, j)) if by_cols else (lambda j, k: (k, 0)))
    o_spec = pl.BlockSpec((None, bm, bn), lambda j, k: (j, 0, 0))
    return pl.pallas_call(
        body, grid=(N_CHIP, nk), in_specs=[a_spec, b_spec], out_specs=[o_spec, o_spec],
        out_shape=[jax.ShapeDtypeStruct((N_CHIP, bm, bn), F32), jax.ShapeDtypeStruct((N_CHIP, bm, bn), BF16)],
        compiler_params=_cp(("parallel", "arbitrary"), VMEM_LIMIT), name=name)(at, b)


def _input_grad(dproj, win_f, x, dx1, g1, tm=512):
    t = x.shape[0]

    def body(dp_ref, w_ref, x_ref, dx1_ref, g_ref, gx_ref, dg_ref):
        @pl.when(pl.program_id(0) == 0)
        def _():
            dg_ref[...] = jnp.zeros_like(dg_ref)

        dh = _bdot_nt(dp_ref[:, pl.ds(0, IN_BLK)], w_ref[0])
        for j in range(1, N_CHIP):
            dh = dh + _bdot_nt(dp_ref[:, pl.ds(IN_BLK * j, IN_BLK)], w_ref[j])
        xv = x_ref[...]
        r = lax.rsqrt(jnp.mean(xv * xv, axis=-1, keepdims=True) + EPS)
        dxn, dg = _rms_bwd(dh, xv, r, g_ref[...])
        gx_ref[...] = dx1_ref[...] + dxn
        dg_ref[...] += dg

    tok = lambda w: pl.BlockSpec((tm, w), lambda i: (i, 0))
    return pl.pallas_call(
        body, grid=(t // tm,),
        in_specs=[tok(IN_DIM), _full((N_CHIP, D, IN_BLK), 1), tok(D), tok(D), _full((1, D))],
        out_specs=[tok(D), _full((1, D))],
        out_shape=[jax.ShapeDtypeStruct((t, D), F32), jax.ShapeDtypeStruct((1, D), F32)],
        compiler_params=_cp(("arbitrary",), VMEM_LIMIT), name="input_grad")(dproj, win_f, x, dx1, g1)


def _coords():
    x, y, c = lax.axis_index("x"), lax.axis_index("y"), lax.axis_index("c")
    chips = [(1 - x, y), (x, 1 - y), (1 - x, 1 - y)]
    return x, y, c, chips


def _allgather_weights(shards, cw8s):
    nw = len(shards)
    halves = [s.shape[0] // 2 for s in shards]

    def body(*refs):
        ins, cw_in = refs[:nw], refs[nw]
        outs, cw_out = refs[nw + 1:2 * nw + 1], refs[2 * nw + 1]
        ici_s, ici_r, d2d_s, d2d_r, loc = refs[2 * nw + 2:]
        x, y, c, chips = _coords()
        b = 2 * x + y
        sib = (x, y, 1 - c)

        def half(w, blk, hc):
            return outs[w].at[blk, pl.ds(hc * halves[w], halves[w])]

        def cw_cols(blk):
            return cw_out.at[:, pl.ds(blk * (D // N_CHIP), D // N_CHIP)]

        local = [pltpu.make_async_copy(ins[w], outs[w].at[b], loc.at[w]) for w in range(nw)]
        local.append(pltpu.make_async_copy(cw_in, cw_cols(b), loc.at[nw]))
        for cp in local:
            cp.start()
        sends = []
        for w in range(nw):
            for j, chip in enumerate(chips):
                sends.append(pltpu.make_async_remote_copy(
                    src_ref=ins[w].at[pl.ds(c * halves[w], halves[w])], dst_ref=half(w, b, c),
                    send_sem=ici_s.at[w, j], recv_sem=ici_r.at[w, j], device_id=(*chip, c), device_id_type=MESH))
        for j, chip in enumerate(chips):
            sends.append(pltpu.make_async_remote_copy(
                src_ref=cw_in, dst_ref=cw_cols(b), send_sem=ici_s.at[nw, j], recv_sem=ici_r.at[nw, j],
                device_id=(*chip, c), device_id_type=MESH))
        for cp in sends:
            cp.start()
        passed = []
        for w in range(nw):
            for j, chip in enumerate(chips):
                bj = 2 * chip[0] + chip[1]
                pltpu.make_async_remote_copy(
                    src_ref=half(w, bj, c), dst_ref=half(w, bj, c), send_sem=ici_s.at[w, j], recv_sem=ici_r.at[w, j],
                    device_id=(*chip, c), device_id_type=MESH).wait_recv()
                fwd = pltpu.make_async_remote_copy(
                    src_ref=half(w, bj, c), dst_ref=half(w, bj, c), send_sem=d2d_s.at[w, j], recv_sem=d2d_r.at[w, j],
                    device_id=sib, device_id_type=MESH)
                fwd.start()
                passed.append(fwd)
        for w in range(nw):
            for j, chip in enumerate(chips):
                bj = 2 * chip[0] + chip[1]
                pltpu.make_async_remote_copy(
                    src_ref=half(w, bj, 1 - c), dst_ref=half(w, bj, 1 - c), send_sem=d2d_s.at[w, j],
                    recv_sem=d2d_r.at[w, j], device_id=sib, device_id_type=MESH).wait_recv()
        for j, chip in enumerate(chips):
            bj = 2 * chip[0] + chip[1]
            pltpu.make_async_remote_copy(
                src_ref=cw_cols(bj), dst_ref=cw_cols(bj), send_sem=ici_s.at[nw, j], recv_sem=ici_r.at[nw, j],
                device_id=(*chip, c), device_id_type=MESH).wait_recv()
        for cp in sends + passed:
            cp.wait_send()
        for cp in local:
            cp.wait()

    out_shape = [jax.ShapeDtypeStruct((N_CHIP,) + s.shape, s.dtype) for s in shards]
    out_shape.append(jax.ShapeDtypeStruct((8, D), F32))
    return pl.pallas_call(
        body, out_shape=out_shape, in_specs=[ANY] * (nw + 1), out_specs=[ANY] * (nw + 1),
        scratch_shapes=[pltpu.SemaphoreType.DMA((nw + 1, 3)), pltpu.SemaphoreType.DMA((nw + 1, 3)),
                        pltpu.SemaphoreType.DMA((nw, 3)), pltpu.SemaphoreType.DMA((nw, 3)),
                        pltpu.SemaphoreType.DMA((nw + 1,))],
        name="allgather_weights")(*shards, cw8s)


def _pair_exchange(grads_b, smalls):
    ng, ns = len(grads_b), len(smalls)
    n = ng + ns

    def body(*refs):
        ins, outs, send, recv = refs[:n], refs[n:2 * n], refs[2 * n], refs[2 * n + 1]
        x, y, c, _ = _coords()
        cps = []
        for i in range(n):
            if i < ng:
                hr = ins[i].shape[1] // 2
                src = ins[i].at[pl.ds(0, N_CHIP), pl.ds((1 - c) * hr, hr)]
            else:
                hr = ins[i].shape[0] // 2
                src = ins[i].at[pl.ds((1 - c) * hr, hr)]
            cps.append(pltpu.make_async_remote_copy(
                src_ref=src, dst_ref=outs[i], send_sem=send.at[i], recv_sem=recv.at[i],
                device_id=(x, y, 1 - c), device_id_type=MESH))
        for cp in cps:
            cp.start()
        for cp in cps:
            cp.wait()

    out_shape = [jax.ShapeDtypeStruct((N_CHIP, g.shape[1] // 2, g.shape[2]), g.dtype) for g in grads_b]
    out_shape += [jax.ShapeDtypeStruct((s.shape[0] // 2, s.shape[1]), s.dtype) for s in smalls]
    return pl.pallas_call(
        body, out_shape=out_shape, in_specs=[ANY] * n, out_specs=[ANY] * n,
        scratch_shapes=[pltpu.SemaphoreType.DMA((n,)), pltpu.SemaphoreType.DMA((n,))],
        name="pair_exchange")(*grads_b, *smalls)


def _pair_sum(c_idx, grads, recvd, smalls, smalls_recvd):
    ng, ns = len(grads), len(smalls)
    halves = [g.shape[1] // 2 for g in grads]

    def body(c_ref, *refs):
        g_in, r_in = refs[:ng], refs[ng:2 * ng]
        s_in, sr_in = refs[2 * ng:2 * ng + ns], refs[2 * ng + ns:2 * ng + 2 * ns]
        o = refs[2 * ng + 2 * ns:]
        for i in range(ng):
            tot = g_in[i][...] + r_in[i][...].astype(F32)
            o[i][...] = tot
            o[ng + i][...] = tot.astype(BF16)
        for i in range(ns):
            o[2 * ng + i][...] = s_in[i][...] + sr_in[i][...]

    in_specs = [pl.BlockSpec((None, None, halves[i], g.shape[2]), lambda b, c: (b, c[0], 0, 0)) for i, g in enumerate(grads)]
    in_specs += [pl.BlockSpec((None, halves[i], g.shape[2]), lambda b, c: (b, 0, 0)) for i, g in enumerate(grads)]
    in_specs += [pl.BlockSpec((None, s.shape[0] // 2, s.shape[1]), lambda b, c: (c[0], 0, 0)) for s in smalls]
    in_specs += [pl.BlockSpec((s.shape[0] // 2, s.shape[1]), lambda b, c: (0, 0)) for s in smalls]
    blk = [pl.BlockSpec((None, halves[i], g.shape[2]), lambda b, c: (b, 0, 0)) for i, g in enumerate(grads)]
    out_specs = blk + blk + [pl.BlockSpec((s.shape[0] // 2, s.shape[1]), lambda b, c: (0, 0)) for s in smalls]
    out_shape = [jax.ShapeDtypeStruct((N_CHIP, halves[i], g.shape[2]), F32) for i, g in enumerate(grads)]
    out_shape += [jax.ShapeDtypeStruct((N_CHIP, halves[i], g.shape[2]), BF16) for i, g in enumerate(grads)]
    out_shape += [jax.ShapeDtypeStruct((s.shape[0] // 2, s.shape[1]), F32) for s in smalls]
    grads4 = [g.reshape(N_CHIP, 2, halves[i], g.shape[2]) for i, g in enumerate(grads)]
    smalls3 = [s.reshape(2, s.shape[0] // 2, s.shape[1]) for s in smalls]
    return pl.pallas_call(
        body, out_shape=out_shape,
        grid_spec=pltpu.PrefetchScalarGridSpec(num_scalar_prefetch=1, grid=(N_CHIP,), in_specs=in_specs, out_specs=out_specs),
        compiler_params=_cp(("arbitrary",), VMEM_LIMIT), name="pair_sum")(c_idx, *grads4, *recvd, *smalls3, *smalls_recvd)


def _chip_exchange(sums_b, smalls):
    ng, ns = len(sums_b), len(smalls)
    n = ng + ns

    def body(*refs):
        ins, outs = refs[:n], refs[n:2 * n]
        send, recv, loc = refs[2 * n:]
        x, y, c, chips = _coords()
        b = 2 * x + y
        local = [pltpu.make_async_copy(ins[ng + i], outs[ng + i].at[b], loc.at[i]) for i in range(ns)]
        for cp in local:
            cp.start()
        cps = []
        for i in range(n):
            for j, chip in enumerate(chips):
                bj = 2 * chip[0] + chip[1]
                src, dst = (ins[i].at[bj], outs[i].at[j]) if i < ng else (ins[i], outs[i].at[b])
                cps.append(pltpu.make_async_remote_copy(
                    src_ref=src, dst_ref=dst, send_sem=send.at[i, j], recv_sem=recv.at[i, j],
                    device_id=(*chip, c), device_id_type=MESH))
        for cp in cps:
            cp.start()
        for i in range(n):
            for j, chip in enumerate(chips):
                bj = 2 * chip[0] + chip[1]
                dst = outs[i].at[j] if i < ng else outs[i].at[bj]
                pltpu.make_async_remote_copy(
                    src_ref=dst, dst_ref=dst, send_sem=send.at[i, j], recv_sem=recv.at[i, j],
                    device_id=(*chip, c), device_id_type=MESH).wait_recv()
        for cp in cps:
            cp.wait_send()
        for cp in local:
            cp.wait()

    out_shape = [jax.ShapeDtypeStruct((3,) + g.shape[1:], g.dtype) for g in sums_b]
    out_shape += [jax.ShapeDtypeStruct((N_CHIP,) + s.shape, s.dtype) for s in smalls]
    return pl.pallas_call(
        body, out_shape=out_shape, in_specs=[ANY] * n, out_specs=[ANY] * n,
        scratch_shapes=[pltpu.SemaphoreType.DMA((n, 3)), pltpu.SemaphoreType.DMA((n, 3)),
                        pltpu.SemaphoreType.DMA((ns,))],
        name="chip_exchange")(*sums_b, *smalls)


def _chip_sum(b_idx, sums, recvd, smalls_slots, steps=4):
    ng, ns = len(sums), len(smalls_slots)

    def body(b_ref, *refs):
        own, rx = refs[:ng], refs[ng:2 * ng]
        sl = refs[2 * ng:2 * ng + ns]
        o = refs[2 * ng + ns:]
        for i in range(ng):
            tot = own[i][...]
            for j in range(3):
                tot = tot + rx[i][j].astype(F32)
            o[i][...] = tot
        for i in range(ns):
            o[ng + i][...] = ((sl[i][0] + sl[i][1]) + sl[i][2]) + sl[i][3]

    def rows(g):
        return g.shape[1] // steps

    in_specs = [pl.BlockSpec((None, rows(g), g.shape[2]), lambda r, b: (b[0], r, 0)) for g in sums]
    in_specs += [pl.BlockSpec((3, rows(g), g.shape[2]), lambda r, b: (0, r, 0)) for g in sums]
    in_specs += [pl.BlockSpec(s.shape, lambda r, b: (0, 0, 0)) for s in smalls_slots]
    out_specs = [pl.BlockSpec((rows(g), g.shape[2]), lambda r, b: (r, 0)) for g in sums]
    out_specs += [pl.BlockSpec(s.shape[1:], lambda r, b: (0, 0)) for s in smalls_slots]
    out_shape = [jax.ShapeDtypeStruct(g.shape[1:], F32) for g in sums]
    out_shape += [jax.ShapeDtypeStruct(s.shape[1:], F32) for s in smalls_slots]
    return pl.pallas_call(
        body, out_shape=out_shape,
        grid_spec=pltpu.PrefetchScalarGridSpec(num_scalar_prefetch=1, grid=(steps,), in_specs=in_specs, out_specs=out_specs),
        compiler_params=_cp(("arbitrary",), VMEM_LIMIT), name="chip_sum")(b_idx, *sums, *recvd, *smalls_slots)


def _pair_gather(halves_in):
    n = len(halves_in)

    def body(*refs):
        ins, outs = refs[:n], refs[n:2 * n]
        send, recv, loc = refs[2 * n:]
        x, y, c, _ = _coords()
        cps, local = [], []
        for i in range(n):
            hr = ins[i].shape[0]
            mine = outs[i].at[pl.ds(c * hr, hr)]
            local.append(pltpu.make_async_copy(ins[i], mine, loc.at[i]))
            cps.append(pltpu.make_async_remote_copy(
                src_ref=ins[i], dst_ref=mine, send_sem=send.at[i], recv_sem=recv.at[i],
                device_id=(x, y, 1 - c), device_id_type=MESH))
        for cp in local + cps:
            cp.start()
        for i in range(n):
            hr = ins[i].shape[0]
            theirs = outs[i].at[pl.ds((1 - c) * hr, hr)]
            pltpu.make_async_remote_copy(
                src_ref=theirs, dst_ref=theirs, send_sem=send.at[i], recv_sem=recv.at[i],
                device_id=(x, y, 1 - c), device_id_type=MESH).wait_recv()
        for cp in cps:
            cp.wait_send()
        for cp in local:
            cp.wait()

    out_shape = [jax.ShapeDtypeStruct((2 * h.shape[0], h.shape[1]), h.dtype) for h in halves_in]
    return pl.pallas_call(
        body, out_shape=out_shape, in_specs=[ANY] * n, out_specs=[ANY] * n,
        scratch_shapes=[pltpu.SemaphoreType.DMA((n,)), pltpu.SemaphoreType.DMA((n,)), pltpu.SemaphoreType.DMA((n,))],
        name="pair_gather")(*halves_in)


def _adamw_math(w, g, m, v):
    m2 = ADAM_B1 * m + (1.0 - ADAM_B1) * g
    v2 = ADAM_B2 * v + (1.0 - ADAM_B2) * (g * g)
    m_hat = m2 / (1.0 - ADAM_B1 ** ADAM_STEP)
    v_hat = v2 / (1.0 - ADAM_B2 ** ADAM_STEP)
    delta = -ADAM_LR * (m_hat / (jnp.sqrt(v_hat) + ADAM_EPS) + ADAM_WD * w)
    return delta, m2, v2


def _adamw_big(w, g, m, v, name, steps=8):
    r, c = w.shape

    def body(w_ref, g_ref, m_ref, v_ref, d_ref, m2_ref, v2_ref):
        d_ref[...], m2_ref[...], v2_ref[...] = _adamw_math(w_ref[...], g_ref[...], m_ref[...], v_ref[...])

    spec = pl.BlockSpec((r // steps, c), lambda i: (i, 0))
    return pl.pallas_call(
        body, grid=(steps,), in_specs=[spec] * 4, out_specs=[spec] * 3,
        out_shape=[jax.ShapeDtypeStruct((r, c), F32)] * 3,
        compiler_params=_cp(("parallel",), VMEM_LIMIT), name=name)(w, g, m, v)


def _adamw_small(groups):
    n = len(groups)

    def body(*refs):
        for i in range(n):
            w_ref, g_ref, m_ref, v_ref = refs[4 * i:4 * i + 4]
            d_ref, m2_ref, v2_ref = refs[4 * n + 3 * i:4 * n + 3 * i + 3]
            d_ref[...], m2_ref[...], v2_ref[...] = _adamw_math(w_ref[...], g_ref[...], m_ref[...], v_ref[...])

    flat = [a for grp in groups for a in grp]
    out_shape = [jax.ShapeDtypeStruct(grp[0].shape, F32) for grp in groups for _ in range(3)]
    outs = pl.pallas_call(body, out_shape=out_shape, name="adamw_small")(*flat)
    return [tuple(outs[3 * i:3 * i + 3]) for i in range(n)]


def kernel(x, mem, norm_mix_g, w_in, conv_w, gm_ln_g, gm_ln_b, gm_ws, gm_bs, w_out, norm_x_g, norm_mem_g, w_q, w_kv, w_xo, norm_final_g, loss_target, m_norm_mix_g, m_w_in, m_conv_w, m_gm_ln_g, m_gm_ln_b, m_gm_ws, m_gm_bs, m_w_out, m_norm_x_g, m_norm_mem_g, m_w_q, m_w_kv, m_w_xo, m_norm_final_g, v_norm_mix_g, v_w_in, v_conv_w, v_gm_ln_g, v_gm_ln_b, v_gm_ws, v_gm_bs, v_w_out, v_norm_x_g, v_norm_mem_g, v_w_q, v_w_kv, v_w_xo, v_norm_final_g):
    t = x.shape[1]
    xi = lax.axis_index("x")
    yi = lax.axis_index("y")
    ci = lax.axis_index("c")
    b_idx = jnp.reshape(2 * xi + yi, (1,)).astype(jnp.int32)
    c_idx = jnp.reshape(ci, (1,)).astype(jnp.int32)

    x2d, mem2d, tgt = x[0], mem[0], loss_target[0]
    big = [w_in[0], w_out[0], w_q[0], w_kv[0], w_xo[0]]
    big_m = [m_w_in[0], m_w_out[0], m_w_q[0], m_w_kv[0], m_w_xo[0]]
    big_v = [v_w_in[0], v_w_out[0], v_w_q[0], v_w_kv[0], v_w_xo[0]]
    g3 = norm_final_g.reshape(1, D)

    def pad8(a):
        return jnp.pad(a, ((0, 8 - a.shape[0]), (0, 0)))

    shards_b = _cast_shards(big)
    win_f, wout_f, wq_f, wkv_f, wxo_f, cw8 = _allgather_weights(shards_b, pad8(conv_w[0]))
    wout2, wq2, wxo2 = wout_f.reshape(MIX, D), wq_f.reshape(D, D), wxo_f.reshape(D, D)

    tril = jnp.tril(jnp.ones((CH, CH), bool))
    wc32 = jnp.where(tril[None], gm_ws[0], 0.0)
    wc = wc32.astype(BF16)
    wct = jnp.swapaxes(wc32, 1, 2).astype(BF16)
    bsb = jnp.broadcast_to(gm_bs[0][:, :, None], (HEADS, CH, CH))

    h, ht = _norm_in(x2d, norm_mix_g)
    proj = _proj(h, win_f)
    mixin, mixt = _mixer_fwd(proj, cw8, gm_ln_g, gm_ln_b, wc, bsb)
    k, v, mt = _mem_fwd(mem2d, norm_mem_g, wkv_f)
    del mt

    (loss_tile, dmix, dx1, dx1b, h2t, dq, ot, dx2b, dk, dv, dg2, dg3) = _tail(
        x2d, tgt, mixin, wout2, wq2, wxo2, k, v, norm_x_g, g3)
    loss = lax.psum(loss_tile[0, 0], ("x", "y", "c"))
    dwkv, dwkv_b, dgm = _mem_bwd(mem2d, norm_mem_g, dk, dv, wkv_f)

    dproj, dcw, dlng, dlnb, dwc, dbs8 = _mixer_bwd(proj, dmix, cw8, gm_ln_g, gm_ln_b, wc, wct, bsb)
    dwxo, dwxo_b = _grad_matmul(ot, dx2b, by_cols=False, name="grad_w_xo")
    dwq, dwq_b = _grad_matmul(h2t, dq, by_cols=False, name="grad_w_q")
    dwout, dwout_b = _grad_matmul(mixt, dx1b, by_cols=False, name="grad_w_out")
    dwin, dwin_b = _grad_matmul(ht, dproj, by_cols=True, name="grad_w_in")
    grad_x, dg1 = _input_grad(dproj, win_f, x2d, dx1, norm_mix_g)

    zero = jnp.zeros((1, D), F32)
    sv = jnp.concatenate([dg1, dg2, dgm, dg3, dlng, dlnb, dbs8[0:1], zero, dcw], axis=0)
    sw = dwc.reshape(HEADS * CH, CH)
    grads = [dwin, dwout, dwq, dwkv, dwxo]
    grads_b = [dwin_b, dwout_b, dwq_b, dwkv_b, dwxo_b]
    rx1 = _pair_exchange(grads_b, [sv, sw])
    ps = _pair_sum(c_idx, grads, rx1[:5], [sv, sw], rx1[5:])
    sums, sums_b, psmall = ps[:5], ps[5:10], ps[10:]
    rx2 = _chip_exchange(sums_b, psmall)
    red = _chip_sum(b_idx, sums, rx2[:5], rx2[5:])
    fin = _pair_gather(red)
    gwin, gwout, gwq, gwkv, gwxo, svf, swf = fin

    names = ["w_in", "w_out", "w_q", "w_kv", "w_xo"]
    big_out = [_adamw_big(w, g, m, v, "adamw_" + nm)
               for w, g, m, v, nm in zip(big, [gwin, gwout, gwq, gwkv, gwxo], big_m, big_v, names)]

    def vec_pack(a1, a2, am, a3, lg, lb, bs):
        return jnp.concatenate([a1, a2, am, a3.reshape(1, D), lg, lb, bs.reshape(1, D), zero], axis=0)

    wv = vec_pack(norm_mix_g, norm_x_g, norm_mem_g, norm_final_g, gm_ln_g, gm_ln_b, gm_bs)
    mv = vec_pack(m_norm_mix_g, m_norm_x_g, m_norm_mem_g, m_norm_final_g, m_gm_ln_g, m_gm_ln_b, m_gm_bs)
    vv = vec_pack(v_norm_mix_g, v_norm_x_g, v_norm_mem_g, v_norm_final_g, v_gm_ln_g, v_gm_ln_b, v_gm_bs)
    gv = svf[0:8]
    gcw = lax.dynamic_slice_in_dim(svf[8:16], (2 * xi + yi) * (D // N_CHIP), D // N_CHIP, axis=1)
    gws = swf
    (dv_, mv_, vv_), (dc_, mc_, vc_), (dws_, mws_, vws_) = _adamw_small([
        (wv, gv, mv, vv),
        (pad8(conv_w[0]), gcw, pad8(m_conv_w[0]), pad8(v_conv_w[0])),
        (gm_ws.reshape(HEADS * CH, CH), gws, m_gm_ws.reshape(HEADS * CH, CH), v_gm_ws.reshape(HEADS * CH, CH))])

    def unpack(vecs, cw, ws, bigs):
        r = lambda i: vecs[i:i + 1]
        return [r(0), bigs[0][None], cw[0:3][None], r(4), r(5), ws.reshape(1, HEADS, CH, CH), vecs[6].reshape(1, HEADS, CH),
                bigs[1][None], r(1), r(2), bigs[2][None], bigs[3][None], bigs[4][None], vecs[3]]

    grads_out = unpack(gv, gcw, gws, [gwin, gwout, gwq, gwkv, gwxo])
    delta_out = unpack(dv_, dc_, dws_, [o[0] for o in big_out])
    m_out = unpack(mv_, mc_, mws_, [o[1] for o in big_out])
    v_out = unpack(vv_, vc_, vws_, [o[2] for o in big_out])
    return (loss, grad_x[None], *grads_out, *delta_out, *m_out, *v_out)
```

```python
import functools
import math

import jax
import jax.numpy as jnp
from jax import lax
from jax.experimental import pallas as pl
from jax.experimental.pallas import tpu as pltpu

F32 = jnp.float32
BF16 = jnp.bfloat16
MESH = pl.DeviceIdType.MESH

D = 1024
SLAB = 1024
N_SLAB = 7
IN_DIM = N_SLAB * SLAB
MIX = 2 * SLAB
HEADS = 8
CH = 128
XH = 4
XD = D // XH
EPS = 1e-6
GELU_C = math.sqrt(2.0 / math.pi)
GELU_A = 0.044715
N_CHIP = 4
IN_BLK = IN_DIM // N_CHIP
KV_BLK = 2 * D // N_CHIP

ADAM_LR, ADAM_B1, ADAM_B2, ADAM_EPS, ADAM_WD, ADAM_STEP = 0.001, 0.9, 0.999, 1e-08, 0.01, 10

VMEM_LIMIT = 60 * 1024 * 1024


def _cp(sem=None, vmem=None):
    return pltpu.CompilerParams(dimension_semantics=sem, vmem_limit_bytes=vmem)


def _full(shape, buffers=None):
    n = len(shape)
    if buffers is None:
        return pl.BlockSpec(shape, lambda *_: (0,) * n)
    return pl.BlockSpec(shape, lambda *_: (0,) * n, pipeline_mode=pl.Buffered(buffers))


ANY = pl.BlockSpec(memory_space=pl.ANY)


def _bdot(a, b):
    return jnp.dot(a.astype(BF16), b.astype(BF16), preferred_element_type=F32)


def _bdot_nt(a, b):
    return lax.dot_general(a.astype(BF16), b.astype(BF16), (((1,), (1,)), ((), ())), preferred_element_type=F32)


def _rms(x, g):
    r = lax.rsqrt(jnp.mean(x * x, axis=-1, keepdims=True) + EPS)
    return x * r * g, r


def _rms_bwd(dy, x, r, g):
    gdy = dy * g
    dx = r * gdy - x * (r * r * r) * jnp.mean(x * gdy, axis=-1, keepdims=True)
    dg = jnp.sum(dy * x * r, axis=0, keepdims=True)
    return dx, dg


def _gelu_parts(x):
    x2 = x * x
    t = jnp.tanh(GELU_C * (x + GELU_A * x * x2))
    val = 0.5 * x * (1.0 + t)
    grad = 0.5 * (1.0 + t) + 0.5 * x * (1.0 - t * t) * (GELU_C * (1.0 + 3.0 * GELU_A * x2))
    return val, grad


def _gelu(x):
    return 0.5 * x * (1.0 + jnp.tanh(GELU_C * (x + GELU_A * x * x * x)))


def _sigmoid(z):
    return 1.0 / (1.0 + jnp.exp(-z))


def _cast_shards(b_idx, arrs):
    n = len(arrs)
    steps = 8

    def body(b_ref, *refs):
        for i in range(n):
            refs[n + i][...] = refs[i][...].astype(BF16)

    in_specs = [pl.BlockSpec((a.shape[0] // steps, a.shape[1]), lambda i, b: (i, 0)) for a in arrs]
    out_specs = [pl.BlockSpec((None, a.shape[0] // steps, a.shape[1]), lambda i, b: (b[0], i, 0)) for a in arrs]
    return pl.pallas_call(
        body, out_shape=[jax.ShapeDtypeStruct((N_CHIP,) + a.shape, BF16) for a in arrs],
        grid_spec=pltpu.PrefetchScalarGridSpec(num_scalar_prefetch=1, grid=(steps,), in_specs=in_specs, out_specs=out_specs),
        compiler_params=_cp(("arbitrary",)), name="cast_shards")(b_idx, *arrs)


def _norm_in(x, g, tm=512):
    t = x.shape[0]

    def body(x_ref, g_ref, h_ref, ht_ref):
        h, _ = _rms(x_ref[...], g_ref[...])
        h_ref[...] = h.astype(BF16)
        ht_ref[...] = h.T.astype(BF16)

    return pl.pallas_call(
        body, grid=(t // tm,),
        in_specs=[pl.BlockSpec((tm, D), lambda i: (i, 0)), _full((1, D))],
        out_specs=[pl.BlockSpec((tm, D), lambda i: (i, 0)), pl.BlockSpec((D, tm), lambda i: (0, i))],
        out_shape=[jax.ShapeDtypeStruct((t, D), BF16), jax.ShapeDtypeStruct((D, t), BF16)],
        compiler_params=_cp(("parallel",)), name="norm_in")(x, g)


def _proj(h, win_f, tm=1024):
    t = h.shape[0]

    def body(h_ref, w_ref, o_ref):
        o_ref[...] = jnp.dot(h_ref[...], w_ref[...], preferred_element_type=F32).astype(BF16)

    return pl.pallas_call(
        body, grid=(N_CHIP, t // tm),
        in_specs=[pl.BlockSpec((tm, D), lambda j, i: (i, 0)), pl.BlockSpec((None, D, IN_BLK), lambda j, i: (j, 0, 0))],
        out_specs=pl.BlockSpec((tm, IN_BLK), lambda j, i: (i, j)),
        out_shape=jax.ShapeDtypeStruct((t, IN_DIM), BF16),
        compiler_params=_cp(("parallel", "parallel"), VMEM_LIMIT), name="proj")(h, win_f)


def _mixer_fwd(proj, cw8, lng, lnb, wc, bsb, tm=256):
    t = proj.shape[0]
    nch = tm // CH

    def body(p_ref, cw_ref, lng_ref, lnb_ref, wc_ref, bsb_ref, mix_ref, mixt_ref, prev_ref, stage_ref):
        @pl.when(pl.program_id(0) == 0)
        def _():
            prev_ref[...] = jnp.zeros_like(prev_ref)

        rows = lax.broadcasted_iota(jnp.int32, (tm, CH), 0)
        for s in range(HEADS):
            cs = pl.ds(CH * s, CH)

            def slab(k):
                return p_ref[:, pl.ds(k * SLAB + CH * s, CH)].astype(F32)

            gb, gc, xa, za = slab(0), slab(1), slab(2), slab(3)
            cx = gc * xa
            p6 = jnp.broadcast_to(prev_ref[6:7, cs], (tm, CH))
            p7 = jnp.broadcast_to(prev_ref[7:8, cs], (tm, CH))
            c1 = jnp.where(rows == 0, p7, pltpu.roll(cx, 1, 0))
            c2 = jnp.where(rows == 0, p6, jnp.where(rows == 1, p7, pltpu.roll(cx, 2, 0)))
            prev_ref[:, cs] = cx[tm - 8:, :]
            cv = cw_ref[0:1, cs] * c2 + cw_ref[1:2, cs] * c1 + cw_ref[2:3, cs] * cx
            stage_ref[:, cs] = gb * cv * (za * _sigmoid(za))

            u, v, zb = slab(4), slab(5), slab(6)
            ug, vg = _gelu(u), _gelu(v)
            dlt = vg - jnp.mean(vg, axis=-1, keepdims=True)
            vhat = dlt * lax.rsqrt(jnp.mean(dlt * dlt, axis=-1, keepdims=True) + EPS)
            vn = (vhat * lng_ref[:, cs] + lnb_ref[:, cs]).astype(BF16)
            gate = ug * (zb * _sigmoid(zb))
            for c in range(nch):
                rs = slice(CH * c, CH * (c + 1))
                sp = jnp.dot(wc_ref[s], vn[rs], preferred_element_type=F32) + bsb_ref[s]
                stage_ref[rs, pl.ds(SLAB + CH * s, CH)] = gate[rs] * sp

        full = stage_ref[...]
        mix_ref[...] = full.astype(BF16)
        mixt_ref[...] = full.T.astype(BF16)

    return pl.pallas_call(
        body, grid=(t // tm,),
        in_specs=[pl.BlockSpec((tm, IN_DIM), lambda i: (i, 0)), _full((8, D)), _full((1, D)), _full((1, D)),
                  _full((HEADS, CH, CH)), _full((HEADS, CH, CH))],
        out_specs=[pl.BlockSpec((tm, MIX), lambda i: (i, 0)), pl.BlockSpec((MIX, tm), lambda i: (0, i))],
        out_shape=[jax.ShapeDtypeStruct((t, MIX), BF16), jax.ShapeDtypeStruct((MIX, t), BF16)],
        scratch_shapes=[pltpu.VMEM((8, D), F32), pltpu.VMEM((tm, MIX), F32)],
        compiler_params=_cp(("arbitrary",), VMEM_LIMIT), name="mixer_fwd")(proj, cw8, lng, lnb, wc, bsb)


def _mem_fwd(mem, gm, wkv_f):
    n_mem = mem.shape[0]

    def body(mem_ref, gm_ref, w_ref, k_ref, v_ref, mt_ref):
        m, _ = _rms(mem_ref[...], gm_ref[...])
        mb = m.astype(BF16)
        mt_ref[...] = m.T.astype(BF16)
        for j in range(N_CHIP):
            dst = k_ref if j < 2 else v_ref
            dst[:, pl.ds(KV_BLK * (j % 2), KV_BLK)] = jnp.dot(mb, w_ref[j], preferred_element_type=F32).astype(BF16)

    return pl.pallas_call(
        body, out_shape=[jax.ShapeDtypeStruct((n_mem, D), BF16), jax.ShapeDtypeStruct((n_mem, D), BF16),
                         jax.ShapeDtypeStruct((D, n_mem), BF16)],
        compiler_params=_cp(None, VMEM_LIMIT), name="mem_fwd")(mem, gm, wkv_f)


def _tail(x, tgt, mixin, wout, wq, wxo, k, v, g2, g3, tm=256):
    t = x.shape[0]
    n_mem = k.shape[0]
    scale = 1.0 / math.sqrt(XD)

    def body(x_ref, tgt_ref, mix_ref, wout_ref, wq_ref, wxo_ref, k_ref, v_ref, g2_ref, g3_ref,
             loss_ref, dmix_ref, dx1_ref, dx1b_ref, h2t_ref, dq_ref, ot_ref, dx2b_ref, dk_ref, dv_ref, dg2_ref, dg3_ref):
        @pl.when(pl.program_id(0) == 0)
        def _():
            loss_ref[...] = jnp.zeros_like(loss_ref)
            dk_ref[...] = jnp.zeros_like(dk_ref)
            dv_ref[...] = jnp.zeros_like(dv_ref)
            dg2_ref[...] = jnp.zeros_like(dg2_ref)
            dg3_ref[...] = jnp.zeros_like(dg3_ref)

        g2, g3 = g2_ref[...], g3_ref[...]
        x1 = x_ref[...] + jnp.dot(mix_ref[...], wout_ref[...], preferred_element_type=F32)
        h2, r2 = _rms(x1, g2)
        h2t_ref[...] = h2.T.astype(BF16)
        q = _bdot(h2, wq_ref[...]).astype(BF16)
        probs, outs = [], []
        for hd in range(XH):
            hs = pl.ds(XD * hd, XD)
            s = _bdot_nt(q[:, XD * hd:XD * (hd + 1)], k_ref[:, hs]) * scale
            e = jnp.exp(s - jnp.max(s, axis=-1, keepdims=True))
            p = e / jnp.sum(e, axis=-1, keepdims=True)
            probs.append(p)
            outs.append(_bdot(p, v_ref[:, hs]))
        o = jnp.concatenate(outs, axis=-1)
        ot_ref[...] = o.T.astype(BF16)
        x2 = x1 + _bdot(o, wxo_ref[...])
        y, r3 = _rms(x2, g3)
        diff = y - tgt_ref[...]
        row_loss = jnp.sum(diff * diff, axis=-1, keepdims=True)
        loss_ref[...] += jnp.broadcast_to(jnp.sum(row_loss, axis=0, keepdims=True) * (0.5 / D), loss_ref.shape)

        dx2, dg3 = _rms_bwd(diff * (1.0 / D), x2, r3, g3)
        dg3_ref[...] += dg3
        dx2b = dx2.astype(BF16)
        dx2b_ref[...] = dx2b
        do = _bdot_nt(dx2b, wxo_ref[...])
        dqs = []
        for hd in range(XH):
            hs = pl.ds(XD * hd, XD)
            p = probs[hd]
            do_h = do[:, XD * hd:XD * (hd + 1)]
            dv_ref[:, hs] += _bdot(p.T, do_h)
            dp = _bdot_nt(do_h, v_ref[:, hs])
            ds = p * (dp - jnp.sum(dp * p, axis=-1, keepdims=True))
            dqs.append(_bdot(ds, k_ref[:, hs]) * scale)
            dk_ref[:, hs] += _bdot(ds.T, q[:, XD * hd:XD * (hd + 1)]) * scale
        dq = jnp.concatenate(dqs, axis=-1).astype(BF16)
        dq_ref[...] = dq
        dx1n, dg2 = _rms_bwd(_bdot_nt(dq, wq_ref[...]), x1, r2, g2)
        dg2_ref[...] += dg2
        dx1 = dx2 + dx1n
        dx1_ref[...] = dx1
        dx1b = dx1.astype(BF16)
        dx1b_ref[...] = dx1b
        dmix_ref[...] = _bdot_nt(dx1b, wout_ref[...]).astype(BF16)

    tok = lambda w: pl.BlockSpec((tm, w), lambda i: (i, 0))
    tok_t = lambda w: pl.BlockSpec((w, tm), lambda i: (0, i))
    return pl.pallas_call(
        body, grid=(t // tm,),
        in_specs=[tok(D), tok(D), tok(MIX), _full((MIX, D), 1), _full((D, D), 1), _full((D, D), 1),
                  _full((n_mem, D), 1), _full((n_mem, D), 1), _full((1, D)), _full((1, D))],
        out_specs=[_full((8, 128)), tok(MIX), tok(D), tok(D), tok_t(D), tok(D), tok_t(D), tok(D),
                   _full((n_mem, D)), _full((n_mem, D)), _full((1, D)), _full((1, D))],
        out_shape=[jax.ShapeDtypeStruct((8, 128), F32), jax.ShapeDtypeStruct((t, MIX), BF16),
                   jax.ShapeDtypeStruct((t, D), F32), jax.ShapeDtypeStruct((t, D), BF16),
                   jax.ShapeDtypeStruct((D, t), BF16), jax.ShapeDtypeStruct((t, D), BF16),
                   jax.ShapeDtypeStruct((D, t), BF16), jax.ShapeDtypeStruct((t, D), BF16),
                   jax.ShapeDtypeStruct((n_mem, D), F32), jax.ShapeDtypeStruct((n_mem, D), F32),
                   jax.ShapeDtypeStruct((1, D), F32), jax.ShapeDtypeStruct((1, D), F32)],
        compiler_params=_cp(("arbitrary",), VMEM_LIMIT), name="tail")(x, tgt, mixin, wout, wq, wxo, k, v, g2, g3)


def _mem_bwd(mem, gm, dk, dv, wkv_f):
    def body(mem_ref, gm_ref, dk_ref, dv_ref, w_ref, dw_ref, dwb_ref, dgm_ref):
        mem_v = mem_ref[...]
        m, rm = _rms(mem_v, gm_ref[...])
        mt = m.T.astype(BF16)
        dm = jnp.zeros_like(mem_v)
        for j in range(N_CHIP):
            src = dk_ref if j < 2 else dv_ref
            dkv = src[:, pl.ds(KV_BLK * (j % 2), KV_BLK)].astype(BF16)
            dw = jnp.dot(mt, dkv, preferred_element_type=F32)
            dw_ref[j] = dw
            dwb_ref[j] = dw.astype(BF16)
            dm = dm + _bdot_nt(dkv, w_ref[j])
        dgm_ref[...] = jnp.sum(dm * mem_v * rm, axis=0, keepdims=True)

    return pl.pallas_call(
        body, out_shape=[jax.ShapeDtypeStruct((N_CHIP, D, KV_BLK), F32), jax.ShapeDtypeStruct((N_CHIP, D, KV_BLK), BF16),
                         jax.ShapeDtypeStruct((1, D), F32)],
        compiler_params=_cp(None, VMEM_LIMIT), name="mem_bwd")(mem, gm, dk, dv, wkv_f)


def _mixer_bwd(proj, dmix, cw8, lng, lnb, wc, wct, bsb, tm=256):
    t = proj.shape[0]
    nt = t // tm
    nch = tm // CH
    hb = 16

    def body(p_ref, pgc_ref, pxa_ref, dm_ref, cw_ref, lng_ref, lnb_ref, wc_ref, wct_ref, bsb_ref,
             dp_ref, dcw_ref, dlng_ref, dlnb_ref, dwc_ref, dbs_ref, next_ref):
        i = pl.program_id(0)

        @pl.when(i == 0)
        def _():
            next_ref[...] = jnp.zeros_like(next_ref)
            dcw_ref[...] = jnp.zeros_like(dcw_ref)
            dlng_ref[...] = jnp.zeros_like(dlng_ref)
            dlnb_ref[...] = jnp.zeros_like(dlnb_ref)
            dwc_ref[...] = jnp.zeros_like(dwc_ref)
            dbs_ref[...] = jnp.zeros_like(dbs_ref)

        first_tile = i == nt - 1
        rows = lax.broadcasted_iota(jnp.int32, (tm, CH), 0)
        ones8 = jnp.ones((8, CH), BF16)
        for s in range(HEADS):
            cs = pl.ds(CH * s, CH)

            def slab(k):
                return p_ref[:, pl.ds(k * SLAB + CH * s, CH)].astype(F32)

            gb, gc, xa, za = slab(0), slab(1), slab(2), slab(3)
            da = dm_ref[:, cs].astype(F32)
            cx = gc * xa
            cxp = pgc_ref[:, cs].astype(F32) * pxa_ref[:, cs].astype(F32)
            cxp = jnp.where(first_tile, jnp.zeros_like(cxp), cxp)
            p6 = jnp.broadcast_to(cxp[hb - 2:hb - 1, :], (tm, CH))
            p7 = jnp.broadcast_to(cxp[hb - 1:hb, :], (tm, CH))
            c1 = jnp.where(rows == 0, p7, pltpu.roll(cx, 1, 0))
            c2 = jnp.where(rows == 0, p6, jnp.where(rows == 1, p7, pltpu.roll(cx, 2, 0)))
            w0, w1, w2 = cw_ref[0:1, cs], cw_ref[1:2, cs], cw_ref[2:3, cs]
            cv = w0 * c2 + w1 * c1 + w2 * cx
            sg = _sigmoid(za)
            sa = za * sg
            dcv = da * gb * sa
            dp_ref[:, pl.ds(0 * SLAB + CH * s, CH)] = (da * cv * sa).astype(BF16)
            dp_ref[:, pl.ds(3 * SLAB + CH * s, CH)] = (da * gb * cv * (sg * (1.0 + za * (1.0 - sg)))).astype(BF16)
            n0 = jnp.broadcast_to(next_ref[0:1, cs], (tm, CH))
            n1 = jnp.broadcast_to(next_ref[1:2, cs], (tm, CH))
            u1 = jnp.where(rows == tm - 1, n0, pltpu.roll(dcv, tm - 1, 0))
            u2 = jnp.where(rows == tm - 2, n0, jnp.where(rows == tm - 1, n1, pltpu.roll(dcv, tm - 2, 0)))
            next_ref[:, cs] = dcv[0:8, :]
            dcx = w2 * dcv + w1 * u1 + w0 * u2
            dp_ref[:, pl.ds(1 * SLAB + CH * s, CH)] = (dcx * xa).astype(BF16)
            dp_ref[:, pl.ds(2 * SLAB + CH * s, CH)] = (dcx * gc).astype(BF16)
            dcw_ref[0:1, cs] += jnp.sum(dcv * c2, axis=0, keepdims=True)
            dcw_ref[1:2, cs] += jnp.sum(dcv * c1, axis=0, keepdims=True)
            dcw_ref[2:3, cs] += jnp.sum(dcv * cx, axis=0, keepdims=True)

            u, v, zb = slab(4), slab(5), slab(6)
            db = dm_ref[:, pl.ds(SLAB + CH * s, CH)].astype(F32)
            ug, ugrad = _gelu_parts(u)
            vg, vgrad = _gelu_parts(v)
            dlt = vg - jnp.mean(vg, axis=-1, keepdims=True)
            rstd = lax.rsqrt(jnp.mean(dlt * dlt, axis=-1, keepdims=True) + EPS)
            vhat = dlt * rstd
            lg = lng_ref[:, cs]
            vn = (vhat * lg + lnb_ref[:, cs]).astype(BF16)
            sgb = _sigmoid(zb)
            szb = zb * sgb
            sps, dvns = [], []
            dbs = jnp.zeros((8, CH), F32)
            dwc = jnp.zeros((CH, CH), F32)
            for c in range(nch):
                rs = slice(CH * c, CH * (c + 1))
                sp = jnp.dot(wc_ref[s], vn[rs], preferred_element_type=F32) + bsb_ref[s]
                dsp = (db[rs] * ug[rs] * szb[rs]).astype(BF16)
                dbs = dbs + lax.dot_general(ones8, dsp, (((1,), (1,)), ((), ())), preferred_element_type=F32)
                dwc = dwc + lax.dot_general(dsp, vn[rs], (((1,), (1,)), ((), ())), preferred_element_type=F32)
                dvns.append(jnp.dot(wct_ref[s], dsp, preferred_element_type=F32))
                sps.append(sp)
            sp = jnp.concatenate(sps, axis=0)
            dvn = jnp.concatenate(dvns, axis=0)
            dbs_ref[:, cs] += dbs
            dwc_ref[s] += dwc
            dlng_ref[:, cs] += jnp.sum(dvn * vhat, axis=0, keepdims=True)
            dlnb_ref[:, cs] += jnp.sum(dvn, axis=0, keepdims=True)
            dvhat = dvn * lg
            dvg = rstd * (dvhat - jnp.mean(dvhat, axis=-1, keepdims=True)
                          - vhat * jnp.mean(dvhat * vhat, axis=-1, keepdims=True))
            dp_ref[:, pl.ds(4 * SLAB + CH * s, CH)] = (db * sp * szb * ugrad).astype(BF16)
            dp_ref[:, pl.ds(5 * SLAB + CH * s, CH)] = (dvg * vgrad).astype(BF16)
            dp_ref[:, pl.ds(6 * SLAB + CH * s, CH)] = (db * ug * sp * (sgb * (1.0 + zb * (1.0 - sgb)))).astype(BF16)

        @pl.when(i == nt - 1)
        def _():
            tril = lax.broadcasted_iota(jnp.int32, (CH, CH), 0) >= lax.broadcasted_iota(jnp.int32, (CH, CH), 1)
            for s in range(HEADS):
                dwc_ref[s] = jnp.where(tril, dwc_ref[s], 0.0)

    rev = lambda i: nt - 1 - i
    halo = lambda col: pl.BlockSpec((hb, SLAB), lambda i: (jnp.maximum(rev(i) * (tm // hb) - 1, 0), col))
    return pl.pallas_call(
        body, grid=(nt,),
        in_specs=[pl.BlockSpec((tm, IN_DIM), lambda i: (rev(i), 0)), halo(1), halo(2),
                  pl.BlockSpec((tm, MIX), lambda i: (rev(i), 0)), _full((8, D)), _full((1, D)), _full((1, D)),
                  _full((HEADS, CH, CH)), _full((HEADS, CH, CH)), _full((HEADS, CH, CH))],
        out_specs=[pl.BlockSpec((tm, IN_DIM), lambda i: (rev(i), 0)), _full((8, D)), _full((1, D)), _full((1, D)),
                   _full((HEADS, CH, CH)), _full((8, D))],
        out_shape=[jax.ShapeDtypeStruct((t, IN_DIM), BF16), jax.ShapeDtypeStruct((8, D), F32),
                   jax.ShapeDtypeStruct((1, D), F32), jax.ShapeDtypeStruct((1, D), F32),
                   jax.ShapeDtypeStruct((HEADS, CH, CH), F32), jax.ShapeDtypeStruct((8, D), F32)],
        scratch_shapes=[pltpu.VMEM((8, D), F32)],
        compiler_params=_cp(("arbitrary",), VMEM_LIMIT), name="mixer_bwd")(proj, proj, proj, dmix, cw8, lng, lnb, wc, wct, bsb)


def _grad_matmul(at, b, *, by_cols, name, tk=1024):
    m, t = at.shape
    n = b.shape[1]
    nk = t // tk
    bm, bn = (m, n // N_CHIP) if by_cols else (m // N_CHIP, n)

    def body(a_ref, b_ref, o_ref, ob_ref):
        kk = pl.program_id(1)
        part = jnp.dot(a_ref[...], b_ref[...], preferred_element_type=F32)

        @pl.when(kk == 0)
        def _():
            o_ref[...] = part

        @pl.when(kk > 0)
        def _():
            o_ref[...] += part

        @pl.when(kk == nk - 1)
        def _():
            ob_ref[...] = o_ref[...].astype(BF16)

    a_spec = pl.BlockSpec((bm, tk), (lambda j, k: (0, k)) if by_cols else (lambda j, k: (j, k)))
    b_spec = pl.BlockSpec((tk, bn), (lambda j, k: (k, j)) if by_cols else (lambda j, k: (k, 0)))
    o_spec = pl.BlockSpec((None, bm, bn), lambda j, k: (j, 0, 0))
    return pl.pallas_call(
        body, grid=(N_CHIP, nk), in_specs=[a_spec, b_spec], out_specs=[o_spec, o_spec],
        out_shape=[jax.ShapeDtypeStruct((N_CHIP, bm, bn), F32), jax.ShapeDtypeStruct((N_CHIP, bm, bn), BF16)],
        compiler_params=_cp(("parallel", "arbitrary"), VMEM_LIMIT), name=name)(at, b)


def _input_grad(dproj, win_f, x, dx1, g1, tm=512):
    t = x.shape[0]

    def body(dp_ref, w_ref, x_ref, dx1_ref, g_ref, gx_ref, dg_ref):
        @pl.when(pl.program_id(0) == 0)
        def _():
            dg_ref[...] = jnp.zeros_like(dg_ref)

        dh = _bdot_nt(dp_ref[:, pl.ds(0, IN_BLK)], w_ref[0])
        for j in range(1, N_CHIP):
            dh = dh + _bdot_nt(dp_ref[:, pl.ds(IN_BLK * j, IN_BLK)], w_ref[j])
        xv = x_ref[...]
        r = lax.rsqrt(jnp.mean(xv * xv, axis=-1, keepdims=True) + EPS)
        dxn, dg = _rms_bwd(dh, xv, r, g_ref[...])
        gx_ref[...] = dx1_ref[...] + dxn
        dg_ref[...] += dg

    tok = lambda w: pl.BlockSpec((tm, w), lambda i: (i, 0))
    return pl.pallas_call(
        body, grid=(t // tm,),
        in_specs=[tok(IN_DIM), _full((N_CHIP, D, IN_BLK), 1), tok(D), tok(D), _full((1, D))],
        out_specs=[tok(D), _full((1, D))],
        out_shape=[jax.ShapeDtypeStruct((t, D), F32), jax.ShapeDtypeStruct((1, D), F32)],
        compiler_params=_cp(("arbitrary",), VMEM_LIMIT), name="input_grad")(dproj, win_f, x, dx1, g1)


def _coords():
    x, y, c = lax.axis_index("x"), lax.axis_index("y"), lax.axis_index("c")
    chips = [(1 - x, y), (x, 1 - y), (1 - x, 1 - y)]
    return x, y, c, chips


def _allgather_weights(fulls, cw8s):
    nw = len(fulls)
    halves = [f.shape[1] // 2 for f in fulls]

    def body(*refs):
        cw_in = refs[nw]
        outs, cw_out = refs[nw + 1:2 * nw + 1], refs[2 * nw + 1]
        ici_s, ici_r, d2d_s, d2d_r, loc = refs[2 * nw + 2:]
        x, y, c, chips = _coords()
        b = 2 * x + y
        sib = (x, y, 1 - c)

        def half(w, blk, hc):
            return outs[w].at[blk, pl.ds(hc * halves[w], halves[w])]

        def cw_cols(blk):
            return cw_out.at[:, pl.ds(blk * (D // N_CHIP), D // N_CHIP)]

        local = [pltpu.make_async_copy(cw_in, cw_cols(b), loc)]
        for cp in local:
            cp.start()
        sends = []
        for w in range(nw):
            for j, chip in enumerate(chips):
                sends.append(pltpu.make_async_remote_copy(
                    src_ref=half(w, b, c), dst_ref=half(w, b, c),
                    send_sem=ici_s.at[w, j], recv_sem=ici_r.at[w, j], device_id=(*chip, c), device_id_type=MESH))
        for j, chip in enumerate(chips):
            sends.append(pltpu.make_async_remote_copy(
                src_ref=cw_in, dst_ref=cw_cols(b), send_sem=ici_s.at[nw, j], recv_sem=ici_r.at[nw, j],
                device_id=(*chip, c), device_id_type=MESH))
        for cp in sends:
            cp.start()
        passed = []
        for w in range(nw):
            for j, chip in enumerate(chips):
                bj = 2 * chip[0] + chip[1]
                pltpu.make_async_remote_copy(
                    src_ref=half(w, bj, c), dst_ref=half(w, bj, c), send_sem=ici_s.at[w, j], recv_sem=ici_r.at[w, j],
                    device_id=(*chip, c), device_id_type=MESH).wait_recv()
                fwd = pltpu.make_async_remote_copy(
                    src_ref=half(w, bj, c), dst_ref=half(w, bj, c), send_sem=d2d_s.at[w, j], recv_sem=d2d_r.at[w, j],
                    device_id=sib, device_id_type=MESH)
                fwd.start()
                passed.append(fwd)
        for w in range(nw):
            for j, chip in enumerate(chips):
                bj = 2 * chip[0] + chip[1]
                pltpu.make_async_remote_copy(
                    src_ref=half(w, bj, 1 - c), dst_ref=half(w, bj, 1 - c), send_sem=d2d_s.at[w, j],
                    recv_sem=d2d_r.at[w, j], device_id=sib, device_id_type=MESH).wait_recv()
        for j, chip in enumerate(chips):
            bj = 2 * chip[0] + chip[1]
            pltpu.make_async_remote_copy(
                src_ref=cw_cols(bj), dst_ref=cw_cols(bj), send_sem=ici_s.at[nw, j], recv_sem=ici_r.at[nw, j],
                device_id=(*chip, c), device_id_type=MESH).wait_recv()
        for cp in sends + passed:
            cp.wait_send()
        for cp in local:
            cp.wait()

    out_shape = [jax.ShapeDtypeStruct(f.shape, f.dtype) for f in fulls]
    out_shape.append(jax.ShapeDtypeStruct((8, D), F32))
    return pl.pallas_call(
        body, out_shape=out_shape, in_specs=[ANY] * (nw + 1), out_specs=[ANY] * (nw + 1),
        input_output_aliases={w: w for w in range(nw)},
        scratch_shapes=[pltpu.SemaphoreType.DMA((nw + 1, 3)), pltpu.SemaphoreType.DMA((nw + 1, 3)),
                        pltpu.SemaphoreType.DMA((nw, 3)), pltpu.SemaphoreType.DMA((nw, 3)),
                        pltpu.SemaphoreType.DMA(())],
        name="allgather_weights")(*fulls, cw8s)


def _pair_exchange(grads_b, smalls):
    ng, ns = len(grads_b), len(smalls)
    n = ng + ns

    def body(*refs):
        ins, outs, send, recv = refs[:n], refs[n:2 * n], refs[2 * n], refs[2 * n + 1]
        x, y, c, _ = _coords()
        cps = []
        for i in range(n):
            if i < ng:
                hr = ins[i].shape[1] // 2
                src = ins[i].at[pl.ds(0, N_CHIP), pl.ds((1 - c) * hr, hr)]
            else:
                hr = ins[i].shape[0] // 2
                src = ins[i].at[pl.ds((1 - c) * hr, hr)]
            cps.append(pltpu.make_async_remote_copy(
                src_ref=src, dst_ref=outs[i], send_sem=send.at[i], recv_sem=recv.at[i],
                device_id=(x, y, 1 - c), device_id_type=MESH))
        for cp in cps:
            cp.start()
        for cp in cps:
            cp.wait()

    out_shape = [jax.ShapeDtypeStruct((N_CHIP, g.shape[1] // 2, g.shape[2]), g.dtype) for g in grads_b]
    out_shape += [jax.ShapeDtypeStruct((s.shape[0] // 2, s.shape[1]), s.dtype) for s in smalls]
    return pl.pallas_call(
        body, out_shape=out_shape, in_specs=[ANY] * n, out_specs=[ANY] * n,
        scratch_shapes=[pltpu.SemaphoreType.DMA((n,)), pltpu.SemaphoreType.DMA((n,))],
        name="pair_exchange")(*grads_b, *smalls)


def _pair_sum(c_idx, grads, recvd, smalls, smalls_recvd):
    ng, ns = len(grads), len(smalls)
    halves = [g.shape[1] // 2 for g in grads]

    def body(c_ref, *refs):
        g_in, r_in = refs[:ng], refs[ng:2 * ng]
        s_in, sr_in = refs[2 * ng:2 * ng + ns], refs[2 * ng + ns:2 * ng + 2 * ns]
        o = refs[2 * ng + 2 * ns:]
        for i in range(ng):
            tot = g_in[i][...] + r_in[i][...].astype(F32)
            o[i][...] = tot
            o[ng + i][...] = tot.astype(BF16)
        for i in range(ns):
            o[2 * ng + i][...] = s_in[i][...] + sr_in[i][...]

    in_specs = [pl.BlockSpec((None, None, halves[i], g.shape[2]), lambda b, c: (b, c[0], 0, 0)) for i, g in enumerate(grads)]
    in_specs += [pl.BlockSpec((None, halves[i], g.shape[2]), lambda b, c: (b, 0, 0)) for i, g in enumerate(grads)]
    in_specs += [pl.BlockSpec((None, s.shape[0] // 2, s.shape[1]), lambda b, c: (c[0], 0, 0)) for s in smalls]
    in_specs += [pl.BlockSpec((s.shape[0] // 2, s.shape[1]), lambda b, c: (0, 0)) for s in smalls]
    blk = [pl.BlockSpec((None, halves[i], g.shape[2]), lambda b, c: (b, 0, 0)) for i, g in enumerate(grads)]
    out_specs = blk + blk + [pl.BlockSpec((s.shape[0] // 2, s.shape[1]), lambda b, c: (0, 0)) for s in smalls]
    out_shape = [jax.ShapeDtypeStruct((N_CHIP, halves[i], g.shape[2]), F32) for i, g in enumerate(grads)]
    out_shape += [jax.ShapeDtypeStruct((N_CHIP, halves[i], g.shape[2]), BF16) for i, g in enumerate(grads)]
    out_shape += [jax.ShapeDtypeStruct((s.shape[0] // 2, s.shape[1]), F32) for s in smalls]
    grads4 = [g.reshape(N_CHIP, 2, halves[i], g.shape[2]) for i, g in enumerate(grads)]
    smalls3 = [s.reshape(2, s.shape[0] // 2, s.shape[1]) for s in smalls]
    return pl.pallas_call(
        body, out_shape=out_shape,
        grid_spec=pltpu.PrefetchScalarGridSpec(num_scalar_prefetch=1, grid=(N_CHIP,), in_specs=in_specs, out_specs=out_specs),
        compiler_params=_cp(("arbitrary",), VMEM_LIMIT), name="pair_sum")(c_idx, *grads4, *recvd, *smalls3, *smalls_recvd)


def _chip_exchange(sums_b, smalls):
    ng, ns = len(sums_b), len(smalls)
    n = ng + ns

    def body(*refs):
        ins, outs = refs[:n], refs[n:2 * n]
        send, recv, loc = refs[2 * n:]
        x, y, c, chips = _coords()
        b = 2 * x + y
        local = [pltpu.make_async_copy(ins[ng + i], outs[ng + i].at[b], loc.at[i]) for i in range(ns)]
        for cp in local:
            cp.start()
        cps = []
        for i in range(n):
            for j, chip in enumerate(chips):
                bj = 2 * chip[0] + chip[1]
                src, dst = (ins[i].at[bj], outs[i].at[j]) if i < ng else (ins[i], outs[i].at[b])
                cps.append(pltpu.make_async_remote_copy(
                    src_ref=src, dst_ref=dst, send_sem=send.at[i, j], recv_sem=recv.at[i, j],
                    device_id=(*chip, c), device_id_type=MESH))
        for cp in cps:
            cp.start()
        for i in range(n):
            for j, chip in enumerate(chips):
                bj = 2 * chip[0] + chip[1]
                dst = outs[i].at[j] if i < ng else outs[i].at[bj]
                pltpu.make_async_remote_copy(
                    src_ref=dst, dst_ref=dst, send_sem=send.at[i, j], recv_sem=recv.at[i, j],
                    device_id=(*chip, c), device_id_type=MESH).wait_recv()
        for cp in cps:
            cp.wait_send()
        for cp in local:
            cp.wait()

    out_shape = [jax.ShapeDtypeStruct((3,) + g.shape[1:], g.dtype) for g in sums_b]
    out_shape += [jax.ShapeDtypeStruct((N_CHIP,) + s.shape, s.dtype) for s in smalls]
    return pl.pallas_call(
        body, out_shape=out_shape, in_specs=[ANY] * n, out_specs=[ANY] * n,
        scratch_shapes=[pltpu.SemaphoreType.DMA((n, 3)), pltpu.SemaphoreType.DMA((n, 3)),
                        pltpu.SemaphoreType.DMA((ns,))],
        name="chip_exchange")(*sums_b, *smalls)


def _chip_sum(bc_idx, sums, recvd, smalls_slots, steps=4):
    ng, ns = len(sums), len(smalls_slots)

    def body(bc_ref, *refs):
        own, rx = refs[:ng], refs[ng:2 * ng]
        sl = refs[2 * ng:2 * ng + ns]
        o = refs[2 * ng + ns:]
        for i in range(ng):
            tot = own[i][...]
            for j in range(3):
                tot = tot + rx[i][j].astype(F32)
            o[i][...] = tot
        for i in range(ns):
            o[ng + i][...] = ((sl[i][0] + sl[i][1]) + sl[i][2]) + sl[i][3]

    def rows(g):
        return g.shape[1] // steps

    in_specs = [pl.BlockSpec((None, rows(g), g.shape[2]), lambda r, bc: (bc[0], r, 0)) for g in sums]
    in_specs += [pl.BlockSpec((3, rows(g), g.shape[2]), lambda r, bc: (0, r, 0)) for g in sums]
    in_specs += [pl.BlockSpec(s.shape, lambda r, bc: (0, 0, 0)) for s in smalls_slots]
    out_specs = [pl.BlockSpec((rows(g), g.shape[2]), lambda r, bc: (bc[1] * steps + r, 0)) for g in sums]
    out_specs += [pl.BlockSpec(s.shape[1:], lambda r, bc: (bc[1], 0)) for s in smalls_slots]
    out_shape = [jax.ShapeDtypeStruct((2 * g.shape[1], g.shape[2]), F32) for g in sums]
    out_shape += [jax.ShapeDtypeStruct((2 * s.shape[1], s.shape[2]), F32) for s in smalls_slots]
    return pl.pallas_call(
        body, out_shape=out_shape,
        grid_spec=pltpu.PrefetchScalarGridSpec(num_scalar_prefetch=1, grid=(steps,), in_specs=in_specs, out_specs=out_specs),
        compiler_params=_cp(("arbitrary",), VMEM_LIMIT), name="chip_sum")(bc_idx, *sums, *recvd, *smalls_slots)


def _pair_gather(arrs):
    n = len(arrs)

    def body(*refs):
        outs = refs[n:2 * n]
        send, recv = refs[2 * n:]
        x, y, c, _ = _coords()
        cps = []
        for i in range(n):
            hr = outs[i].shape[0] // 2
            mine = outs[i].at[pl.ds(c * hr, hr)]
            cps.append(pltpu.make_async_remote_copy(
                src_ref=mine, dst_ref=mine, send_sem=send.at[i], recv_sem=recv.at[i],
                device_id=(x, y, 1 - c), device_id_type=MESH))
        for cp in cps:
            cp.start()
        for i in range(n):
            hr = outs[i].shape[0] // 2
            theirs = outs[i].at[pl.ds((1 - c) * hr, hr)]
            pltpu.make_async_remote_copy(
                src_ref=theirs, dst_ref=theirs, send_sem=send.at[i], recv_sem=recv.at[i],
                device_id=(x, y, 1 - c), device_id_type=MESH).wait_recv()
        for cp in cps:
            cp.wait_send()

    return pl.pallas_call(
        body, out_shape=[jax.ShapeDtypeStruct(a.shape, a.dtype) for a in arrs], in_specs=[ANY] * n, out_specs=[ANY] * n,
        input_output_aliases={i: i for i in range(n)},
        scratch_shapes=[pltpu.SemaphoreType.DMA((n,)), pltpu.SemaphoreType.DMA((n,))],
        name="pair_gather")(*arrs)


def _adamw_math(w, g, m, v):
    m2 = ADAM_B1 * m + (1.0 - ADAM_B1) * g
    v2 = ADAM_B2 * v + (1.0 - ADAM_B2) * (g * g)
    m_hat = m2 / (1.0 - ADAM_B1 ** ADAM_STEP)
    v_hat = v2 / (1.0 - ADAM_B2 ** ADAM_STEP)
    delta = -ADAM_LR * (m_hat / (jnp.sqrt(v_hat) + ADAM_EPS) + ADAM_WD * w)
    return delta, m2, v2


def _adamw_big(w, g, m, v, name, steps=8):
    r, c = w.shape

    def body(w_ref, g_ref, m_ref, v_ref, d_ref, m2_ref, v2_ref):
        d_ref[...], m2_ref[...], v2_ref[...] = _adamw_math(w_ref[...], g_ref[...], m_ref[...], v_ref[...])

    spec = pl.BlockSpec((r // steps, c), lambda i: (i, 0))
    return pl.pallas_call(
        body, grid=(steps,), in_specs=[spec] * 4, out_specs=[spec] * 3,
        out_shape=[jax.ShapeDtypeStruct((r, c), F32)] * 3,
        compiler_params=_cp(("parallel",), VMEM_LIMIT), name=name)(w, g, m, v)


def _adamw_small(groups):
    n = len(groups)

    def body(*refs):
        for i in range(n):
            w_ref, g_ref, m_ref, v_ref = refs[4 * i:4 * i + 4]
            d_ref, m2_ref, v2_ref = refs[4 * n + 3 * i:4 * n + 3 * i + 3]
            d_ref[...], m2_ref[...], v2_ref[...] = _adamw_math(w_ref[...], g_ref[...], m_ref[...], v_ref[...])

    flat = [a for grp in groups for a in grp]
    out_shape = [jax.ShapeDtypeStruct(grp[0].shape, F32) for grp in groups for _ in range(3)]
    outs = pl.pallas_call(body, out_shape=out_shape, name="adamw_small")(*flat)
    return [tuple(outs[3 * i:3 * i + 3]) for i in range(n)]


def kernel(x, mem, norm_mix_g, w_in, conv_w, gm_ln_g, gm_ln_b, gm_ws, gm_bs, w_out, norm_x_g, norm_mem_g, w_q, w_kv, w_xo, norm_final_g, loss_target, m_norm_mix_g, m_w_in, m_conv_w, m_gm_ln_g, m_gm_ln_b, m_gm_ws, m_gm_bs, m_w_out, m_norm_x_g, m_norm_mem_g, m_w_q, m_w_kv, m_w_xo, m_norm_final_g, v_norm_mix_g, v_w_in, v_conv_w, v_gm_ln_g, v_gm_ln_b, v_gm_ws, v_gm_bs, v_w_out, v_norm_x_g, v_norm_mem_g, v_w_q, v_w_kv, v_w_xo, v_norm_final_g):
    t = x.shape[1]
    xi = lax.axis_index("x")
    yi = lax.axis_index("y")
    ci = lax.axis_index("c")
    b_idx = jnp.reshape(2 * xi + yi, (1,)).astype(jnp.int32)
    c_idx = jnp.reshape(ci, (1,)).astype(jnp.int32)

    x2d, mem2d, tgt = x[0], mem[0], loss_target[0]
    big = [w_in[0], w_out[0], w_q[0], w_kv[0], w_xo[0]]
    big_m = [m_w_in[0], m_w_out[0], m_w_q[0], m_w_kv[0], m_w_xo[0]]
    big_v = [v_w_in[0], v_w_out[0], v_w_q[0], v_w_kv[0], v_w_xo[0]]
    g3 = norm_final_g.reshape(1, D)

    def pad8(a):
        return jnp.pad(a, ((0, 8 - a.shape[0]), (0, 0)))

    own_blocks = _cast_shards(b_idx, big)
    win_f, wout_f, wq_f, wkv_f, wxo_f, cw8 = _allgather_weights(own_blocks, pad8(conv_w[0]))
    wout2, wq2, wxo2 = wout_f.reshape(MIX, D), wq_f.reshape(D, D), wxo_f.reshape(D, D)

    tril = jnp.tril(jnp.ones((CH, CH), bool))
    wc32 = jnp.where(tril[None], gm_ws[0], 0.0)
    wc = wc32.astype(BF16)
    wct = jnp.swapaxes(wc32, 1, 2).astype(BF16)
    bsb = jnp.broadcast_to(gm_bs[0][:, :, None], (HEADS, CH, CH))

    h, ht = _norm_in(x2d, norm_mix_g)
    proj = _proj(h, win_f)
    mixin, mixt = _mixer_fwd(proj, cw8, gm_ln_g, gm_ln_b, wc, bsb)
    k, v, mt = _mem_fwd(mem2d, norm_mem_g, wkv_f)
    del mt

    (loss_tile, dmix, dx1, dx1b, h2t, dq, ot, dx2b, dk, dv, dg2, dg3) = _tail(
        x2d, tgt, mixin, wout2, wq2, wxo2, k, v, norm_x_g, g3)
    dwkv, dwkv_b, dgm = _mem_bwd(mem2d, norm_mem_g, dk, dv, wkv_f)

    dproj, dcw, dlng, dlnb, dwc, dbs8 = _mixer_bwd(proj, dmix, cw8, gm_ln_g, gm_ln_b, wc, wct, bsb)
    dwxo, dwxo_b = _grad_matmul(ot, dx2b, by_cols=False, name="grad_w_xo")
    dwq, dwq_b = _grad_matmul(h2t, dq, by_cols=False, name="grad_w_q")
    dwout, dwout_b = _grad_matmul(mixt, dx1b, by_cols=False, name="grad_w_out")
    dwin, dwin_b = _grad_matmul(ht, dproj, by_cols=True, name="grad_w_in")
    grad_x, dg1 = _input_grad(dproj, win_f, x2d, dx1, norm_mix_g)

    zero = jnp.zeros((1, D), F32)
    loss_row = jnp.broadcast_to(loss_tile[0:1, 0:1], (1, D))
    sv = jnp.concatenate([dg1, dg2, dgm, dg3, dlng, dlnb, dbs8[0:1], loss_row, dcw], axis=0)
    sw = dwc.reshape(HEADS * CH, CH)
    grads = [dwin, dwout, dwq, dwkv, dwxo]
    grads_b = [dwin_b, dwout_b, dwq_b, dwkv_b, dwxo_b]
    rx1 = _pair_exchange(grads_b, [sv, sw])
    ps = _pair_sum(c_idx, grads, rx1[:5], [sv, sw], rx1[5:])
    sums, sums_b, psmall = ps[:5], ps[5:10], ps[10:]
    rx2 = _chip_exchange(sums_b, psmall)
    red = _chip_sum(jnp.concatenate([b_idx, c_idx]), sums, rx2[:5], rx2[5:])
    fin = _pair_gather(red)
    gwin, gwout, gwq, gwkv, gwxo, svf, swf = fin

    names = ["w_in", "w_out", "w_q", "w_kv", "w_xo"]
    big_out = [_adamw_big(w, g, m, v, "adamw_" + nm)
               for w, g, m, v, nm in zip(big, [gwin, gwout, gwq, gwkv, gwxo], big_m, big_v, names)]

    def vec_pack(a1, a2, am, a3, lg, lb, bs):
        return jnp.concatenate([a1, a2, am, a3.reshape(1, D), lg, lb, bs.reshape(1, D), zero], axis=0)

    wv = vec_pack(norm_mix_g, norm_x_g, norm_mem_g, norm_final_g, gm_ln_g, gm_ln_b, gm_bs)
    mv = vec_pack(m_norm_mix_g, m_norm_x_g, m_norm_mem_g, m_norm_final_g, m_gm_ln_g, m_gm_ln_b, m_gm_bs)
    vv = vec_pack(v_norm_mix_g, v_norm_x_g, v_norm_mem_g, v_norm_final_g, v_gm_ln_g, v_gm_ln_b, v_gm_bs)
    loss = svf[7, 0]
    gv = svf[0:8]
    gcw =lax.dynamic_slice_in_dim(svf[8:16], (2 * xi + yi) * (D // N_CHIP), D // N_CHIP, axis=1)
    gws = swf
    (dv_, mv_, vv_), (dc_, mc_, vc_), (dws_, mws_, vws_) = _adamw_small([
        (wv, gv, mv, vv),
        (pad8(conv_w[0]), gcw, pad8(m_conv_w[0]), pad8(v_conv_w[0])),
        (gm_ws.reshape(HEADS * CH, CH), gws, m_gm_ws.reshape(HEADS * CH, CH), v_gm_ws.reshape(HEADS * CH, CH))])

    def unpack(vecs, cw, ws, bigs):
        r = lambda i: vecs[i:i + 1]
        return [r(0), bigs[0][None], cw[0:3][None], r(4), r(5), ws.reshape(1, HEADS, CH, CH), vecs[6].reshape(1, HEADS, CH),
                bigs[1][None], r(1), r(2), bigs[2][None], bigs[3][None], bigs[4][None], vecs[3]]

    grads_out = unpack(gv, gcw, gws, [gwin, gwout, gwq, gwkv, gwxo])
    delta_out = unpack(dv_, dc_, dws_, [o[0] for o in big_out])
    m_out = unpack(mv_, mc_, mws_, [o[1] for o in big_out])
    v_out = unpack(vv_, vc_, vws_, [o[2] for o in big_out])
    return (loss, grad_x[None], *grads_out, *delta_out, *m_out, *v_out)
```

```python
import functools
import math

import jax
import jax.numpy as jnp
from jax import lax
from jax.experimental import pallas as pl
from jax.experimental.pallas import tpu as pltpu

F32 = jnp.float32
BF16 = jnp.bfloat16
MESH = pl.DeviceIdType.MESH

D = 1024
SLAB = 1024
N_SLAB = 7
IN_DIM = N_SLAB * SLAB
MIX = 2 * SLAB
HEADS = 8
CH = 128
XH = 4
XD = D // XH
EPS = 1e-6
GELU_C = math.sqrt(2.0 / math.pi)
GELU_A = 0.044715
N_CHIP = 4
IN_BLK = IN_DIM // N_CHIP
KV_BLK = 2 * D // N_CHIP

ADAM_LR, ADAM_B1, ADAM_B2, ADAM_EPS, ADAM_WD, ADAM_STEP = 0.001, 0.9, 0.999, 1e-08, 0.01, 10

VMEM_LIMIT = 60 * 1024 * 1024


def _cp(sem=None, vmem=None):
    return pltpu.CompilerParams(dimension_semantics=sem, vmem_limit_bytes=vmem)


def _full(shape, buffers=None):
    n = len(shape)
    if buffers is None:
        return pl.BlockSpec(shape, lambda *_: (0,) * n)
    return pl.BlockSpec(shape, lambda *_: (0,) * n, pipeline_mode=pl.Buffered(buffers))


ANY = pl.BlockSpec(memory_space=pl.ANY)


def _bdot(a, b):
    return jnp.dot(a.astype(BF16), b.astype(BF16), preferred_element_type=F32)


def _bdot_nt(a, b):
    return lax.dot_general(a.astype(BF16), b.astype(BF16), (((1,), (1,)), ((), ())), preferred_element_type=F32)


def _rms(x, g):
    r = lax.rsqrt(jnp.mean(x * x, axis=-1, keepdims=True) + EPS)
    return x * r * g, r


def _rms_bwd(dy, x, r, g):
    gdy = dy * g
    dx = r * gdy - x * (r * r * r) * jnp.mean(x * gdy, axis=-1, keepdims=True)
    dg = jnp.sum(dy * x * r, axis=0, keepdims=True)
    return dx, dg


def _gelu_parts(x):
    x2 = x * x
    t = jnp.tanh(GELU_C * (x + GELU_A * x * x2))
    val = 0.5 * x * (1.0 + t)
    grad = 0.5 * (1.0 + t) + 0.5 * x * (1.0 - t * t) * (GELU_C * (1.0 + 3.0 * GELU_A * x2))
    return val, grad


def _gelu(x):
    return 0.5 * x * (1.0 + jnp.tanh(GELU_C * (x + GELU_A * x * x * x)))


def _sigmoid(z):
    return 1.0 / (1.0 + jnp.exp(-z))


def _cast_shards(b_idx, arrs):
    n = len(arrs)
    steps = 8

    def body(b_ref, *refs):
        for i in range(n):
            refs[n + i][...] = refs[i][...].astype(BF16)

    in_specs = [pl.BlockSpec((a.shape[0] // steps, a.shape[1]), lambda i, b: (i, 0)) for a in arrs]
    out_specs = [pl.BlockSpec((None, a.shape[0] // steps, a.shape[1]), lambda i, b: (b[0], i, 0)) for a in arrs]
    return pl.pallas_call(
        body, out_shape=[jax.ShapeDtypeStruct((N_CHIP,) + a.shape, BF16) for a in arrs],
        grid_spec=pltpu.PrefetchScalarGridSpec(num_scalar_prefetch=1, grid=(steps,), in_specs=in_specs, out_specs=out_specs),
        compiler_params=_cp(("arbitrary",)), name="cast_shards")(b_idx, *arrs)


def _norm_in(x, g, tm=512):
    t = x.shape[0]

    def body(x_ref, g_ref, h_ref, ht_ref):
        h, _ = _rms(x_ref[...], g_ref[...])
        h_ref[...] = h.astype(BF16)
        ht_ref[...] = h.T.astype(BF16)

    return pl.pallas_call(
        body, grid=(t // tm,),
        in_specs=[pl.BlockSpec((tm, D), lambda i: (i, 0)), _full((1, D))],
        out_specs=[pl.BlockSpec((tm, D), lambda i: (i, 0)), pl.BlockSpec((D, tm), lambda i: (0, i))],
        out_shape=[jax.ShapeDtypeStruct((t, D), BF16), jax.ShapeDtypeStruct((D, t), BF16)],
        compiler_params=_cp(("parallel",)), name="norm_in")(x, g)


def _proj_gather(order, h, win_own, cw8s, tm=1024):
    t = h.shape[0]
    ni = t // tm
    hr = D // 2

    def body(order_ref, h_ref, win_in, cw_in, o_ref, win_f, cw_out, wv, ici_s, ici_r, d2d_s, d2d_r, cw_s, cw_r, loc):
        j, i = pl.program_id(0), pl.program_id(1)
        x, y, c, chips = _coords()
        b = 2 * x + y
        sib = (x, y, 1 - c)
        blks = [2 * chip[0] + chip[1] for chip in chips]

        def half(blk, hc):
            return win_f.at[blk, pl.ds(hc * hr, hr)]

        def cw_cols(blk):
            return cw_out.at[:, pl.ds(blk * (D // N_CHIP), D // N_CHIP)]

        def ici(k, blk):
            return pltpu.make_async_remote_copy(src_ref=half(blk, c), dst_ref=half(blk, c), send_sem=ici_s.at[k],
                                                recv_sem=ici_r.at[k], device_id=(*chips[k], c), device_id_type=MESH)

        def relay(k, blk, hc):
            return pltpu.make_async_remote_copy(src_ref=half(blk, hc), dst_ref=half(blk, hc), send_sem=d2d_s.at[k],
                                                recv_sem=d2d_r.at[k], device_id=sib, device_id_type=MESH)

        def cw_copy(k, blk):
            src = cw_in if blk is None else cw_cols(blk)
            return pltpu.make_async_remote_copy(src_ref=src, dst_ref=cw_cols(b if blk is None else blk), send_sem=cw_s.at[k],
                                                recv_sem=cw_r.at[k], device_id=(*chips[k], c), device_id_type=MESH)

        cw_local = pltpu.make_async_copy(cw_in, cw_cols(b), loc.at[1])

        @pl.when((j == 0) & (i == 0))
        def _():
            cw_local.start()
            for k in range(3):
                ici(k, b).start()
            for k in range(3):
                cw_copy(k, None).start()

        for jj in range(N_CHIP):
            @pl.when((j == jj) & (i == 0))
            def _(jj=jj):
                if jj == 0:
                    blk = b
                else:
                    blk = blks[jj - 1]
                    ici(jj - 1, blk).wait_recv()
                    relay(jj - 1, blk, c).start()
                    relay(jj - 1, blk, 1 - c).wait_recv()
                load = pltpu.make_async_copy(win_f.at[blk], wv, loc.at[0])
                load.start()
                load.wait()

        o_ref[...] = jnp.dot(h_ref[...], wv[...], preferred_element_type=F32).astype(BF16)

        @pl.when((j == N_CHIP - 1) & (i == ni - 1))
        def _():
            for k in range(3):
                cw_copy(k, blks[k]).wait_recv()
            for k in range(3):
                ici(k, b).wait_send()
                relay(k, blks[k], c).wait_send()
                cw_copy(k, None).wait_send()
            cw_local.wait()

    in_specs = [pl.BlockSpec((tm, D), lambda j, i, o: (i, 0)), ANY, ANY]
    out_specs = [pl.BlockSpec((tm, IN_BLK), lambda j, i, o: (i, o[j])), ANY, ANY]
    return pl.pallas_call(
        body, out_shape=[jax.ShapeDtypeStruct((t, IN_DIM), BF16), jax.ShapeDtypeStruct(win_own.shape, BF16),
                         jax.ShapeDtypeStruct((8, D), F32)],
        grid_spec=pltpu.PrefetchScalarGridSpec(
            num_scalar_prefetch=1, grid=(N_CHIP, ni), in_specs=in_specs, out_specs=out_specs,
            scratch_shapes=[pltpu.VMEM((D, IN_BLK), BF16)] + [pltpu.SemaphoreType.DMA((3,))] * 6
            + [pltpu.SemaphoreType.DMA((2,))]),
        input_output_aliases={2: 1},
        compiler_params=_cp(("arbitrary", "arbitrary"), VMEM_LIMIT), name="proj_gather")(order, h, win_own, cw8s)


def _gather_steps(outs, ici_s, ici_r, d2d_s, d2d_r):
    x, y, c, chips = _coords()
    b = 2 * x + y
    sib = (x, y, 1 - c)
    nw = len(outs)

    def half(w, blk, hc):
        hr = outs[w].shape[1] // 2
        return outs[w].at[blk, pl.ds(hc * hr, hr)]

    def ici(w, k, blk):
        return pltpu.make_async_remote_copy(src_ref=half(w, blk, c), dst_ref=half(w, blk, c), send_sem=ici_s.at[w, k],
                                            recv_sem=ici_r.at[w, k], device_id=(*chips[k], c), device_id_type=MESH)

    def relay(w, k, blk, hc):
        return pltpu.make_async_remote_copy(src_ref=half(w, blk, hc), dst_ref=half(w, blk, hc), send_sem=d2d_s.at[w, k],
                                            recv_sem=d2d_r.at[w, k], device_id=sib, device_id_type=MESH)

    def start():
        for w in range(nw):
            for k in range(3):
                ici(w, k, b).start()

    def finish():
        for w in range(nw):
            for k in range(3):
                blk = 2 * chips[k][0] + chips[k][1]
                ici(w, k, blk).wait_recv()
                relay(w, k, blk, c).start()
        for w in range(nw):
            for k in range(3):
                blk = 2 * chips[k][0] + chips[k][1]
                relay(w, k, blk, 1 - c).wait_recv()
        for w in range(nw):
            for k in range(3):
                blk = 2 * chips[k][0] + chips[k][1]
                ici(w, k, b).wait_send()
                relay(w, k, blk, c).wait_send()

    return start, finish


def _mixer_fwd(proj, cw8, lng, lnb, wc, bsb, fulls, tm=256):
    t = proj.shape[0]
    nt = t // tm
    nch = tm // CH
    nw = len(fulls)

    def body(*refs):
        p_ref, cw_ref, lng_ref, lnb_ref, wc_ref, bsb_ref = refs[:6]
        mix_ref, mixt_ref = refs[6 + nw:8 + nw]
        w_outs = refs[8 + nw:8 + 2 * nw]
        prev_ref, stage_ref, ici_s, ici_r, d2d_s, d2d_r = refs[8 + 2 * nw:]
        gather_start, gather_finish = _gather_steps(w_outs, ici_s, ici_r, d2d_s, d2d_r)

        @pl.when(pl.program_id(0) == 0)
        def _():
            gather_start()
            prev_ref[...] = jnp.zeros_like(prev_ref)

        rows = lax.broadcasted_iota(jnp.int32, (tm, CH), 0)
        for s in range(HEADS):
            cs = pl.ds(CH * s, CH)

            def slab(k):
                return p_ref[:, pl.ds(k * SLAB + CH * s, CH)].astype(F32)

            gb, gc, xa, za = slab(0), slab(1), slab(2), slab(3)
            cx = gc * xa
            p6 = jnp.broadcast_to(prev_ref[6:7, cs], (tm, CH))
            p7 = jnp.broadcast_to(prev_ref[7:8, cs], (tm, CH))
            c1 = jnp.where(rows == 0, p7, pltpu.roll(cx, 1, 0))
            c2 = jnp.where(rows == 0, p6, jnp.where(rows == 1, p7, pltpu.roll(cx, 2, 0)))
            prev_ref[:, cs] = cx[tm - 8:, :]
            cv = cw_ref[0:1, cs] * c2 + cw_ref[1:2, cs] * c1 + cw_ref[2:3, cs] * cx
            stage_ref[:, cs] = gb * cv * (za * _sigmoid(za))

            u, v, zb = slab(4), slab(5), slab(6)
            ug, vg = _gelu(u), _gelu(v)
            dlt = vg - jnp.mean(vg, axis=-1, keepdims=True)
            vhat = dlt * lax.rsqrt(jnp.mean(dlt * dlt, axis=-1, keepdims=True) + EPS)
            vn = (vhat * lng_ref[:, cs] + lnb_ref[:, cs]).astype(BF16)
            gate = ug * (zb * _sigmoid(zb))
            for c in range(nch):
                rs = slice(CH * c, CH * (c + 1))
                sp = jnp.dot(wc_ref[s], vn[rs], preferred_element_type=F32) + bsb_ref[s]
                stage_ref[rs, pl.ds(SLAB + CH * s, CH)] = gate[rs] * sp

        full = stage_ref[...]
        mix_ref[...] = full.astype(BF16)
        mixt_ref[...] = full.T.astype(BF16)

        @pl.when(pl.program_id(0) == nt - 1)
        def _():
            gather_finish()

    sems = [pltpu.SemaphoreType.DMA((nw, 3))] * 4
    outs = pl.pallas_call(
        body, grid=(nt,),
        in_specs=[pl.BlockSpec((tm, IN_DIM), lambda i: (i, 0)), _full((8, D)), _full((1, D)), _full((1, D)),
                  _full((HEADS, CH, CH)), _full((HEADS, CH, CH))] + [ANY] * nw,
        out_specs=[pl.BlockSpec((tm, MIX), lambda i: (i, 0)), pl.BlockSpec((MIX, tm), lambda i: (0, i))] + [ANY] * nw,
        out_shape=[jax.ShapeDtypeStruct((t, MIX), BF16), jax.ShapeDtypeStruct((MIX, t), BF16)]
        + [jax.ShapeDtypeStruct(f.shape, f.dtype) for f in fulls],
        input_output_aliases={6 + w: 2 + w for w in range(nw)},
        scratch_shapes=[pltpu.VMEM((8, D), F32), pltpu.VMEM((tm, MIX), F32)] + sems,
        compiler_params=_cp(("arbitrary",), VMEM_LIMIT), name="mixer_fwd")(proj, cw8, lng, lnb, wc, bsb, *fulls)
    return outs[0], outs[1], outs[2:]


def _mem_fwd(mem, gm, wkv_f):
    n_mem = mem.shape[0]

    def body(mem_ref, gm_ref, w_ref, k_ref, v_ref, mt_ref):
        m, _ = _rms(mem_ref[...], gm_ref[...])
        mb = m.astype(BF16)
        mt_ref[...] = m.T.astype(BF16)
        for j in range(N_CHIP):
            dst = k_ref if j < 2 else v_ref
            dst[:, pl.ds(KV_BLK * (j % 2), KV_BLK)] = jnp.dot(mb, w_ref[j], preferred_element_type=F32).astype(BF16)

    return pl.pallas_call(
        body, out_shape=[jax.ShapeDtypeStruct((n_mem, D), BF16), jax.ShapeDtypeStruct((n_mem, D), BF16),
                         jax.ShapeDtypeStruct((D, n_mem), BF16)],
        compiler_params=_cp(None, VMEM_LIMIT), name="mem_fwd")(mem, gm, wkv_f)


def _tail(x, tgt, mixin, wout, wq, wxo, k, v, g2, g3, tm=256):
    t = x.shape[0]
    n_mem = k.shape[0]
    scale = 1.0 / math.sqrt(XD)

    def body(x_ref, tgt_ref, mix_ref, wout_ref, wq_ref, wxo_ref, k_ref, v_ref, g2_ref, g3_ref,
             loss_ref, dmix_ref, dx1_ref, dx1b_ref, h2t_ref, dq_ref, ot_ref, dx2b_ref, dk_ref, dv_ref, dg2_ref, dg3_ref):
        @pl.when(pl.program_id(0) == 0)
        def _():
            loss_ref[...] = jnp.zeros_like(loss_ref)
            dk_ref[...] = jnp.zeros_like(dk_ref)
            dv_ref[...] = jnp.zeros_like(dv_ref)
            dg2_ref[...] = jnp.zeros_like(dg2_ref)
            dg3_ref[...] = jnp.zeros_like(dg3_ref)

        g2, g3 = g2_ref[...], g3_ref[...]
        x1 = x_ref[...] + jnp.dot(mix_ref[...], wout_ref[...], preferred_element_type=F32)
        h2, r2 = _rms(x1, g2)
        h2t_ref[...] = h2.T.astype(BF16)
        q = _bdot(h2, wq_ref[...]).astype(BF16)
        probs, outs = [], []
        for hd in range(XH):
            hs = pl.ds(XD * hd, XD)
            s = _bdot_nt(q[:, XD * hd:XD * (hd + 1)], k_ref[:, hs]) * scale
            e = jnp.exp(s - jnp.max(s, axis=-1, keepdims=True))
            p = e / jnp.sum(e, axis=-1, keepdims=True)
            probs.append(p)
            outs.append(_bdot(p, v_ref[:, hs]))
        o = jnp.concatenate(outs, axis=-1)
        ot_ref[...] = o.T.astype(BF16)
        x2 = x1 + _bdot(o, wxo_ref[...])
        y, r3 = _rms(x2, g3)
        diff = y - tgt_ref[...]
        row_loss = jnp.sum(diff * diff, axis=-1, keepdims=True)
        loss_ref[...] += jnp.broadcast_to(jnp.sum(row_loss, axis=0, keepdims=True) * (0.5 / D), loss_ref.shape)

        dx2, dg3 = _rms_bwd(diff * (1.0 / D), x2, r3, g3)
        dg3_ref[...] += dg3
        dx2b = dx2.astype(BF16)
        dx2b_ref[...] = dx2b
        do = _bdot_nt(dx2b, wxo_ref[...])
        dqs = []
        for hd in range(XH):
            hs = pl.ds(XD * hd, XD)
            p = probs[hd]
            do_h = do[:, XD * hd:XD * (hd + 1)]
            dv_ref[:, hs] += _bdot(p.T, do_h)
            dp = _bdot_nt(do_h, v_ref[:, hs])
            ds = p * (dp - jnp.sum(dp * p, axis=-1, keepdims=True))
            dqs.append(_bdot(ds, k_ref[:, hs]) * scale)
            dk_ref[:, hs] += _bdot(ds.T, q[:, XD * hd:XD * (hd + 1)]) * scale
        dq = jnp.concatenate(dqs, axis=-1).astype(BF16)
        dq_ref[...] = dq
        dx1n, dg2 = _rms_bwd(_bdot_nt(dq, wq_ref[...]), x1, r2, g2)
        dg2_ref[...] += dg2
        dx1 = dx2 + dx1n
        dx1_ref[...] = dx1
        dx1b = dx1.astype(BF16)
        dx1b_ref[...] = dx1b
        dmix_ref[...] = _bdot_nt(dx1b, wout_ref[...]).astype(BF16)

    tok = lambda w: pl.BlockSpec((tm, w), lambda i: (i, 0))
    tok_t = lambda w: pl.BlockSpec((w, tm), lambda i: (0, i))
    return pl.pallas_call(
        body, grid=(t // tm,),
        in_specs=[tok(D), tok(D), tok(MIX), _full((MIX, D), 1), _full((D, D), 1), _full((D, D), 1),
                  _full((n_mem, D), 1), _full((n_mem, D), 1), _full((1, D)), _full((1, D))],
        out_specs=[_full((8, 128)), tok(MIX), tok(D), tok(D), tok_t(D), tok(D), tok_t(D), tok(D),
                   _full((n_mem, D)), _full((n_mem, D)), _full((1, D)), _full((1, D))],
        out_shape=[jax.ShapeDtypeStruct((8, 128), F32), jax.ShapeDtypeStruct((t, MIX), BF16),
                   jax.ShapeDtypeStruct((t, D), F32), jax.ShapeDtypeStruct((t, D), BF16),
                   jax.ShapeDtypeStruct((D, t), BF16), jax.ShapeDtypeStruct((t, D), BF16),
                   jax.ShapeDtypeStruct((D, t), BF16), jax.ShapeDtypeStruct((t, D), BF16),
                   jax.ShapeDtypeStruct((n_mem, D), F32), jax.ShapeDtypeStruct((n_mem, D), F32),
                   jax.ShapeDtypeStruct((1, D), F32), jax.ShapeDtypeStruct((1, D), F32)],
        compiler_params=_cp(("arbitrary",), VMEM_LIMIT), name="tail")(x, tgt, mixin, wout, wq, wxo, k, v, g2, g3)


def _mem_bwd(mem, gm, dk, dv, wkv_f):
    def body(mem_ref, gm_ref, dk_ref, dv_ref, w_ref, dw_ref, dwb_ref, dgm_ref):
        mem_v = mem_ref[...]
        m, rm = _rms(mem_v, gm_ref[...])
        mt = m.T.astype(BF16)
        dm = jnp.zeros_like(mem_v)
        for j in range(N_CHIP):
            src = dk_ref if j < 2 else dv_ref
            dkv = src[:, pl.ds(KV_BLK * (j % 2), KV_BLK)].astype(BF16)
            dw = jnp.dot(mt, dkv, preferred_element_type=F32)
            dw_ref[j] = dw
            dwb_ref[j] = dw.astype(BF16)
            dm = dm + _bdot_nt(dkv, w_ref[j])
        dgm_ref[...] = jnp.sum(dm * mem_v * rm, axis=0, keepdims=True)

    return pl.pallas_call(
        body, out_shape=[jax.ShapeDtypeStruct((N_CHIP, D, KV_BLK), F32), jax.ShapeDtypeStruct((N_CHIP, D, KV_BLK), BF16),
                         jax.ShapeDtypeStruct((1, D), F32)],
        compiler_params=_cp(None, VMEM_LIMIT), name="mem_bwd")(mem, gm, dk, dv, wkv_f)


def _mixer_bwd(proj, dmix, cw8, lng, lnb, wc, wct, bsb, xch, tm=256):
    t = proj.shape[0]
    nt = t // tm
    nch = tm // CH
    hb = 16
    nx = len(xch)
    xch_shape, xch_sems = _exchange_shapes(xch, [])

    def body(*refs):
        p_ref, pgc_ref, pxa_ref, dm_ref, cw_ref, lng_ref, lnb_ref, wc_ref, wct_ref, bsb_ref = refs[:10]
        x_in = refs[10:10 + nx]
        dp_ref, dcw_ref, dlng_ref, dlnb_ref, dwc_ref, dbs_ref = refs[10 + nx:16 + nx]
        x_out = refs[16 + nx:16 + 2 * nx]
        next_ref, send, recv, loc = refs[16 + 2 * nx:]
        exchange_start, exchange_finish = _exchange_steps(x_in, x_out, nx, send, recv, loc)
        i = pl.program_id(0)

        @pl.when(i == 0)
        def _():
            exchange_start()
            next_ref[...] = jnp.zeros_like(next_ref)
            dcw_ref[...] = jnp.zeros_like(dcw_ref)
            dlng_ref[...] = jnp.zeros_like(dlng_ref)
            dlnb_ref[...] = jnp.zeros_like(dlnb_ref)
            dwc_ref[...] = jnp.zeros_like(dwc_ref)
            dbs_ref[...] = jnp.zeros_like(dbs_ref)

        first_tile = i == nt - 1
        rows = lax.broadcasted_iota(jnp.int32, (tm, CH), 0)
        ones8 = jnp.ones((8, CH), BF16)
        for s in range(HEADS):
            cs = pl.ds(CH * s, CH)

            def slab(k):
                return p_ref[:, pl.ds(k * SLAB + CH * s, CH)].astype(F32)

            gb, gc, xa, za = slab(0), slab(1), slab(2), slab(3)
            da = dm_ref[:, cs].astype(F32)
            cx = gc * xa
            cxp = pgc_ref[:, cs].astype(F32) * pxa_ref[:, cs].astype(F32)
            cxp = jnp.where(first_tile, jnp.zeros_like(cxp), cxp)
            p6 = jnp.broadcast_to(cxp[hb - 2:hb - 1, :], (tm, CH))
            p7 = jnp.broadcast_to(cxp[hb - 1:hb, :], (tm, CH))
            c1 = jnp.where(rows == 0, p7, pltpu.roll(cx, 1, 0))
            c2 = jnp.where(rows == 0, p6, jnp.where(rows == 1, p7, pltpu.roll(cx, 2, 0)))
            w0, w1, w2 = cw_ref[0:1, cs], cw_ref[1:2, cs], cw_ref[2:3, cs]
            cv = w0 * c2 + w1 * c1 + w2 * cx
            sg = _sigmoid(za)
            sa = za * sg
            dcv = da * gb * sa
            dp_ref[:, pl.ds(0 * SLAB + CH * s, CH)] = (da * cv * sa).astype(BF16)
            dp_ref[:, pl.ds(3 * SLAB + CH * s, CH)] = (da * gb * cv * (sg * (1.0 + za * (1.0 - sg)))).astype(BF16)
            n0 = jnp.broadcast_to(next_ref[0:1, cs], (tm, CH))
            n1 = jnp.broadcast_to(next_ref[1:2, cs], (tm, CH))
            u1 = jnp.where(rows == tm - 1, n0, pltpu.roll(dcv, tm - 1, 0))
            u2 = jnp.where(rows == tm - 2, n0, jnp.where(rows == tm - 1, n1, pltpu.roll(dcv, tm - 2, 0)))
            next_ref[:, cs] = dcv[0:8, :]
            dcx = w2 * dcv + w1 * u1 + w0 * u2
            dp_ref[:, pl.ds(1 * SLAB + CH * s, CH)] = (dcx * xa).astype(BF16)
            dp_ref[:, pl.ds(2 * SLAB + CH * s, CH)] = (dcx * gc).astype(BF16)
            dcw_ref[0:1, cs] += jnp.sum(dcv * c2, axis=0, keepdims=True)
            dcw_ref[1:2, cs] += jnp.sum(dcv * c1, axis=0, keepdims=True)
            dcw_ref[2:3, cs] += jnp.sum(dcv * cx, axis=0, keepdims=True)

            u, v, zb = slab(4), slab(5), slab(6)
            db = dm_ref[:, pl.ds(SLAB + CH * s, CH)].astype(F32)
            ug, ugrad = _gelu_parts(u)
            vg, vgrad = _gelu_parts(v)
            dlt = vg - jnp.mean(vg, axis=-1, keepdims=True)
            rstd = lax.rsqrt(jnp.mean(dlt * dlt, axis=-1, keepdims=True) + EPS)
            vhat = dlt * rstd
            lg = lng_ref[:, cs]
            vn = (vhat * lg + lnb_ref[:, cs]).astype(BF16)
            sgb = _sigmoid(zb)
            szb = zb * sgb
            sps, dvns = [], []
            dbs = jnp.zeros((8, CH), F32)
            dwc = jnp.zeros((CH, CH), F32)
            for c in range(nch):
                rs = slice(CH * c, CH * (c + 1))
                sp = jnp.dot(wc_ref[s], vn[rs], preferred_element_type=F32) + bsb_ref[s]
                dsp = (db[rs] * ug[rs] * szb[rs]).astype(BF16)
                dbs = dbs + lax.dot_general(ones8, dsp, (((1,), (1,)), ((), ())), preferred_element_type=F32)
                dwc = dwc + lax.dot_general(dsp, vn[rs], (((1,), (1,)), ((), ())), preferred_element_type=F32)
                dvns.append(jnp.dot(wct_ref[s], dsp, preferred_element_type=F32))
                sps.append(sp)
            sp = jnp.concatenate(sps, axis=0)
            dvn = jnp.concatenate(dvns, axis=0)
            dbs_ref[:, cs] += dbs
            dwc_ref[s] += dwc
            dlng_ref[:, cs] += jnp.sum(dvn * vhat, axis=0, keepdims=True)
            dlnb_ref[:, cs] += jnp.sum(dvn, axis=0, keepdims=True)
            dvhat = dvn * lg
            dvg = rstd * (dvhat - jnp.mean(dvhat, axis=-1, keepdims=True)
                          - vhat * jnp.mean(dvhat * vhat, axis=-1, keepdims=True))
            dp_ref[:, pl.ds(4 * SLAB + CH * s, CH)] = (db * sp * szb * ugrad).astype(BF16)
            dp_ref[:, pl.ds(5 * SLAB + CH * s, CH)] = (dvg * vgrad).astype(BF16)
            dp_ref[:, pl.ds(6 * SLAB + CH * s, CH)] = (db * ug * sp * (sgb * (1.0 + zb * (1.0 - sgb)))).astype(BF16)

        @pl.when(i == nt - 1)
        def _():
            tril = lax.broadcasted_iota(jnp.int32, (CH, CH), 0) >= lax.broadcasted_iota(jnp.int32, (CH, CH), 1)
            for s in range(HEADS):
                dwc_ref[s] = jnp.where(tril, dwc_ref[s], 0.0)
            exchange_finish()

    rev = lambda i: nt - 1 - i
    halo = lambda col: pl.BlockSpec((hb, SLAB), lambda i: (jnp.maximum(rev(i) * (tm // hb) - 1, 0), col))
    outs = pl.pallas_call(
        body, grid=(nt,),
        in_specs=[pl.BlockSpec((tm, IN_DIM), lambda i: (rev(i), 0)), halo(1), halo(2),
                  pl.BlockSpec((tm, MIX), lambda i: (rev(i), 0)), _full((8, D)), _full((1, D)), _full((1, D)),
                  _full((HEADS, CH, CH)), _full((HEADS, CH, CH)), _full((HEADS, CH, CH))] + [ANY] * nx,
        out_specs=[pl.BlockSpec((tm, IN_DIM), lambda i: (rev(i), 0)), _full((8, D)), _full((1, D)), _full((1, D)),
                   _full((HEADS, CH, CH)), _full((8, D))] + [ANY] * nx,
        out_shape=[jax.ShapeDtypeStruct((t, IN_DIM), BF16), jax.ShapeDtypeStruct((8, D), F32),
                   jax.ShapeDtypeStruct((1, D), F32), jax.ShapeDtypeStruct((1, D), F32),
                   jax.ShapeDtypeStruct((HEADS, CH, CH), F32), jax.ShapeDtypeStruct((8, D), F32)] + xch_shape,
        scratch_shapes=[pltpu.VMEM((8, D), F32)] + xch_sems,
        compiler_params=_cp(("arbitrary",), VMEM_LIMIT), name="mixer_bwd")(
            proj, proj, proj, dmix, cw8, lng, lnb, wc, wct, bsb, *xch)
    return outs[:6], outs[6:]


def _grad_matmul(at, b, *, by_cols, name, tk=1024):
    m, t = at.shape
    n = b.shape[1]
    nk = t // tk
    bm, bn = (m, n // N_CHIP) if by_cols else (m // N_CHIP, n)

    def body(a_ref, b_ref, o_ref, ob_ref):
        kk = pl.program_id(1)
        part = jnp.dot(a_ref[...], b_ref[...], preferred_element_type=F32)

        @pl.when(kk == 0)
        def _():
            o_ref[...] = part

        @pl.when(kk > 0)
        def _():
            o_ref[...] += part

        @pl.when(kk == nk - 1)
        def _():
            ob_ref[...] = o_ref[...].astype(BF16)

    a_spec = pl.BlockSpec((bm, tk), (lambda j, k: (0, k)) if by_cols else (lambda j, k: (j, k)))
    b_spec = pl.BlockSpec((tk, bn), (lambda j, k: (k, j)) if by_cols else (lambda j, k: (k, 0)))
    o_spec = pl.BlockSpec((None, bm, bn), lambda j, k: (j, 0, 0))
    return pl.pallas_call(
        body, grid=(N_CHIP, nk), in_specs=[a_spec, b_spec], out_specs=[o_spec, o_spec],
        out_shape=[jax.ShapeDtypeStruct((N_CHIP, bm, bn), F32), jax.ShapeDtypeStruct((N_CHIP, bm, bn), BF16)],
        compiler_params=_cp(("parallel", "arbitrary"), VMEM_LIMIT), name=name)(at, b)


def _input_grad(dproj, win_f, x, dx1, g1, xch_big, xch_small, tm=512):
    t = x.shape[0]
    nt = t // tm
    nb, nx = len(xch_big), len(xch_big) + len(xch_small)
    xch_shape, xch_sems = _exchange_shapes(xch_big, xch_small)

    def body(*refs):
        dp_ref, w_ref, x_ref, dx1_ref, g_ref = refs[:5]
        x_in = refs[5:5 + nx]
        gx_ref, dg_ref = refs[5 + nx:7 + nx]
        x_out = refs[7 + nx:7 + 2 * nx]
        send, recv, loc = refs[7 + 2 * nx:]
        exchange_start, exchange_finish = _exchange_steps(x_in, x_out, nb, send, recv, loc)

        @pl.when(pl.program_id(0) == 0)
        def _():
            exchange_start()
            dg_ref[...] = jnp.zeros_like(dg_ref)

        dh = _bdot_nt(dp_ref[:, pl.ds(0, IN_BLK)], w_ref[0])
        for j in range(1, N_CHIP):
            dh = dh + _bdot_nt(dp_ref[:, pl.ds(IN_BLK * j, IN_BLK)], w_ref[j])
        xv = x_ref[...]
        r = lax.rsqrt(jnp.mean(xv * xv, axis=-1, keepdims=True) + EPS)
        dxn, dg = _rms_bwd(dh, xv, r, g_ref[...])
        gx_ref[...] = dx1_ref[...] + dxn
        dg_ref[0:1, :] += dg

        @pl.when(pl.program_id(0) == nt - 1)
        def _():
            exchange_finish()

    tok = lambda w: pl.BlockSpec((tm, w), lambda i: (i, 0))
    outs = pl.pallas_call(
        body, grid=(nt,),
        in_specs=[tok(IN_DIM), _full((N_CHIP, D, IN_BLK), 1), tok(D), tok(D), _full((1, D))] + [ANY] * nx,
        out_specs=[tok(D), _full((8, D))] + [ANY] * nx,
        out_shape=[jax.ShapeDtypeStruct((t, D), F32), jax.ShapeDtypeStruct((8, D), F32)] + xch_shape,
        scratch_shapes=xch_sems,
        compiler_params=_cp(("arbitrary",), VMEM_LIMIT), name="input_grad")(dproj, win_f, x, dx1, g1, *xch_big, *xch_small)
    return outs[0], outs[1], outs[2:]


def _coords():
    x, y, c = lax.axis_index("x"), lax.axis_index("y"), lax.axis_index("c")
    chips = [(1 - x, y), (x, 1 - y), (1 - x, 1 - y)]
    return x, y, c, chips


def _pair_exchange(grads_b, smalls, name):
    ng, ns = len(grads_b), len(smalls)
    n = ng + ns

    def body(*refs):
        ins, outs, send, recv = refs[:n], refs[n:2 * n], refs[2 * n], refs[2 * n + 1]
        x, y, c, _ = _coords()
        cps = []
        for i in range(n):
            if i < ng:
                hr = ins[i].shape[1] // 2
                src = ins[i].at[pl.ds(0, N_CHIP), pl.ds((1 - c) * hr, hr)]
            else:
                hr = ins[i].shape[0] // 2
                src = ins[i].at[pl.ds((1 - c) * hr, hr)]
            cps.append(pltpu.make_async_remote_copy(
                src_ref=src, dst_ref=outs[i], send_sem=send.at[i], recv_sem=recv.at[i],
                device_id=(x, y, 1 - c), device_id_type=MESH))
        for cp in cps:
            cp.start()
        for cp in cps:
            cp.wait()

    out_shape = [jax.ShapeDtypeStruct((N_CHIP, g.shape[1] // 2, g.shape[2]), g.dtype) for g in grads_b]
    out_shape += [jax.ShapeDtypeStruct((s.shape[0] // 2, s.shape[1]), s.dtype) for s in smalls]
    return pl.pallas_call(
        body, out_shape=out_shape, in_specs=[ANY] * n, out_specs=[ANY] * n,
        scratch_shapes=[pltpu.SemaphoreType.DMA((n,)), pltpu.SemaphoreType.DMA((n,))],
        name=name)(*grads_b, *smalls)


def _pair_sum(c_idx, grads, recvd, smalls, smalls_recvd, name):
    ng, ns = len(grads), len(smalls)
    halves = [g.shape[1] // 2 for g in grads]

    def body(c_ref, *refs):
        g_in, r_in = refs[:ng], refs[ng:2 * ng]
        s_in, sr_in = refs[2 * ng:2 * ng + ns], refs[2 * ng + ns:2 * ng + 2 * ns]
        o = refs[2 * ng + 2 * ns:]
        for i in range(ng):
            tot = g_in[i][...] + r_in[i][...].astype(F32)
            o[i][...] = tot
            o[ng + i][...] = tot.astype(BF16)
        for i in range(ns):
            o[2 * ng + i][...] = s_in[i][...] + sr_in[i][...]

    in_specs = [pl.BlockSpec((None, None, halves[i], g.shape[2]), lambda b, c: (b, c[0], 0, 0)) for i, g in enumerate(grads)]
    in_specs += [pl.BlockSpec((None, halves[i], g.shape[2]), lambda b, c: (b, 0, 0)) for i, g in enumerate(grads)]
    in_specs += [pl.BlockSpec((None, s.shape[0] // 2, s.shape[1]), lambda b, c: (c[0], 0, 0)) for s in smalls]
    in_specs += [pl.BlockSpec((s.shape[0] // 2, s.shape[1]), lambda b, c: (0, 0)) for s in smalls]
    blk = [pl.BlockSpec((None, halves[i], g.shape[2]), lambda b, c: (b, 0, 0)) for i, g in enumerate(grads)]
    out_specs = blk + blk + [pl.BlockSpec((s.shape[0] // 2, s.shape[1]), lambda b, c: (0, 0)) for s in smalls]
    out_shape = [jax.ShapeDtypeStruct((N_CHIP, halves[i], g.shape[2]), F32) for i, g in enumerate(grads)]
    out_shape += [jax.ShapeDtypeStruct((N_CHIP, halves[i], g.shape[2]), BF16) for i, g in enumerate(grads)]
    out_shape += [jax.ShapeDtypeStruct((s.shape[0] // 2, s.shape[1]), F32) for s in smalls]
    grads4 = [g.reshape(N_CHIP, 2, halves[i], g.shape[2]) for i, g in enumerate(grads)]
    smalls3 = [s.reshape(2, s.shape[0] // 2, s.shape[1]) for s in smalls]
    return pl.pallas_call(
        body, out_shape=out_shape,
        grid_spec=pltpu.PrefetchScalarGridSpec(num_scalar_prefetch=1, grid=(N_CHIP,), in_specs=in_specs, out_specs=out_specs),
        compiler_params=_cp(("arbitrary",), VMEM_LIMIT), name=name)(c_idx, *grads4, *recvd, *smalls3, *smalls_recvd)


def _exchange_steps(ins, outs, ng, send, recv, loc):
    x, y, c, chips = _coords()
    b = 2 * x + y
    n = len(ins)
    blks = [2 * chip[0] + chip[1] for chip in chips]

    def copy(i, k, arriving):
        if i < ng:
            src, dst = ins[i].at[blks[k]], outs[i].at[k]
        else:
            src, dst = ins[i], outs[i].at[blks[k] if arriving else b]
        return pltpu.make_async_remote_copy(src_ref=dst if arriving else src, dst_ref=dst, send_sem=send.at[i, k],
                                            recv_sem=recv.at[i, k], device_id=(*chips[k], c), device_id_type=MESH)

    local = [pltpu.make_async_copy(ins[i], outs[i].at[b], loc.at[i - ng]) for i in range(ng, n)]

    def start():
        for cp in local:
            cp.start()
        for i in range(n):
            for k in range(3):
                copy(i, k, False).start()

    def finish():
        for i in range(n):
            for k in range(3):
                copy(i, k, True).wait_recv()
        for i in range(n):
            for k in range(3):
                copy(i, k, False).wait_send()
        for cp in local:
            cp.wait()

    return start, finish


def _exchange_shapes(sums_b, smalls):
    n = len(sums_b) + len(smalls)
    out_shape = [jax.ShapeDtypeStruct((3,) + g.shape[1:], g.dtype) for g in sums_b]
    out_shape += [jax.ShapeDtypeStruct((N_CHIP,) + s.shape, s.dtype) for s in smalls]
    sems = [pltpu.SemaphoreType.DMA((n, 3)), pltpu.SemaphoreType.DMA((n, 3)),
            pltpu.SemaphoreType.DMA((max(len(smalls), 1),))]
    return out_shape, sems


def _chip_sum(bc_idx, sums, recvd, smalls_slots, steps=4):
    ng, ns = len(sums), len(smalls_slots)

    def body(bc_ref, *refs):
        own, rx = refs[:ng], refs[ng:2 * ng]
        sl = refs[2 * ng:2 * ng + ns]
        o = refs[2 * ng + ns:]
        for i in range(ng):
            tot = own[i][...]
            for j in range(3):
                tot = tot + rx[i][j].astype(F32)
            o[i][...] = tot
        for i in range(ns):
            o[ng + i][...] = ((sl[i][0] + sl[i][1]) + sl[i][2]) + sl[i][3]

    def rows(g):
        return g.shape[1] // steps

    in_specs = [pl.BlockSpec((None, rows(g), g.shape[2]), lambda r, bc: (bc[0], r, 0)) for g in sums]
    in_specs += [pl.BlockSpec((3, rows(g), g.shape[2]), lambda r, bc: (0, r, 0)) for g in sums]
    in_specs += [pl.BlockSpec(s.shape, lambda r, bc: (0, 0, 0)) for s in smalls_slots]
    out_specs = [pl.BlockSpec((rows(g), g.shape[2]), lambda r, bc: (bc[1] * steps + r, 0)) for g in sums]
    out_specs += [pl.BlockSpec(s.shape[1:], lambda r, bc: (bc[1], 0)) for s in smalls_slots]
    out_shape = [jax.ShapeDtypeStruct((2 * g.shape[1], g.shape[2]), F32) for g in sums]
    out_shape += [jax.ShapeDtypeStruct((2 * s.shape[1], s.shape[2]), F32) for s in smalls_slots]
    return pl.pallas_call(
        body, out_shape=out_shape,
        grid_spec=pltpu.PrefetchScalarGridSpec(num_scalar_prefetch=1, grid=(steps,), in_specs=in_specs, out_specs=out_specs),
        compiler_params=_cp(("arbitrary",), VMEM_LIMIT), name="chip_sum")(bc_idx, *sums, *recvd, *smalls_slots)


def _pair_gather(arrs, last):
    n = len(arrs)
    n_dev = 8

    def body(*refs):
        last_in = refs[n]
        outs = refs[n + 1:2 * n + 1]
        slots = refs[2 * n + 1]
        send, recv, gs, gr, loc = refs[2 * n + 2:]
        x, y, c, _ = _coords()
        me = 4 * x + 2 * y + c
        mine_slot = pltpu.make_async_copy(last_in, slots.at[me], loc)
        mine_slot.start()
        gathers = []
        for r in range(1, n_dev):
            to = (me + r) % n_dev
            gathers.append(pltpu.make_async_remote_copy(
                src_ref=last_in, dst_ref=slots.at[me], send_sem=gs.at[r - 1], recv_sem=gr.at[me],
                device_id=(to // 4, (to // 2) % 2, to % 2), device_id_type=MESH))
        for cp in gathers:
            cp.start()
        cps = []
        for i in range(n):
            hr = outs[i].shape[0] // 2
            mine = outs[i].at[pl.ds(c * hr, hr)]
            cps.append(pltpu.make_async_remote_copy(
                src_ref=mine, dst_ref=mine, send_sem=send.at[i], recv_sem=recv.at[i],
                device_id=(x, y, 1 - c), device_id_type=MESH))
        for cp in cps:
            cp.start()
        for i in range(n):
            hr = outs[i].shape[0] // 2
            theirs = outs[i].at[pl.ds((1 - c) * hr, hr)]
            pltpu.make_async_remote_copy(
                src_ref=theirs, dst_ref=theirs, send_sem=send.at[i], recv_sem=recv.at[i],
                device_id=(x, y, 1 - c), device_id_type=MESH).wait_recv()
        for r in range(1, n_dev):
            frm = (me + r) % n_dev
            pltpu.make_async_remote_copy(
                src_ref=slots.at[frm], dst_ref=slots.at[frm], send_sem=gs.at[r - 1], recv_sem=gr.at[frm],
                device_id=(frm // 4, (frm // 2) % 2, frm % 2), device_id_type=MESH).wait_recv()
        for cp in cps + gathers:
            cp.wait_send()
        mine_slot.wait()

    outs = pl.pallas_call(
        body, out_shape=[jax.ShapeDtypeStruct(a.shape, a.dtype) for a in arrs]
        + [jax.ShapeDtypeStruct((n_dev,) + last.shape, last.dtype)],
        in_specs=[ANY] * (n + 1), out_specs=[ANY] * (n + 1),
        input_output_aliases={i: i for i in range(n)},
        scratch_shapes=[pltpu.SemaphoreType.DMA((n,)), pltpu.SemaphoreType.DMA((n,)),
                        pltpu.SemaphoreType.DMA((n_dev - 1,)), pltpu.SemaphoreType.DMA((n_dev,)),
                        pltpu.SemaphoreType.DMA(())],
        name="pair_gather")(*arrs, last)
    return outs[:n], outs[n]


def _adamw_math(w, g, m, v):
    m2 = ADAM_B1 * m + (1.0 - ADAM_B1) * g
    v2 = ADAM_B2 * v + (1.0 - ADAM_B2) * (g * g)
    m_hat = m2 / (1.0 - ADAM_B1 ** ADAM_STEP)
    v_hat = v2 / (1.0 - ADAM_B2 ** ADAM_STEP)
    delta = -ADAM_LR * (m_hat / (jnp.sqrt(v_hat) + ADAM_EPS) + ADAM_WD * w)
    return delta, m2, v2


def _adamw_big(w, g, m, v, name, steps=8):
    r, c = w.shape

    def body(w_ref, g_ref, m_ref, v_ref, d_ref, m2_ref, v2_ref):
        d_ref[...], m2_ref[...], v2_ref[...] = _adamw_math(w_ref[...], g_ref[...], m_ref[...], v_ref[...])

    spec = pl.BlockSpec((r // steps, c), lambda i: (i, 0))
    return pl.pallas_call(
        body, grid=(steps,), in_specs=[spec] * 4, out_specs=[spec] * 3,
        out_shape=[jax.ShapeDtypeStruct((r, c), F32)] * 3,
        compiler_params=_cp(("parallel",), VMEM_LIMIT), name=name)(w, g, m, v)


def _adamw_small(groups, slots):
    n = len(groups)

    def body(*refs):
        slots_ref, g0_ref = refs[4 * n], refs[4 * n + 1 + 3 * n]
        for i in range(n):
            w_ref, g_ref, m_ref, v_ref = refs[4 * i:4 * i + 4]
            d_ref, m2_ref, v2_ref = refs[4 * n + 1 + 3 * i:4 * n + 1 + 3 * i + 3]
            g = g_ref[...]
            if i == 0:
                for dev in range(slots.shape[0]):
                    g = g + slots_ref[dev]
                g0_ref[...] = g
            d_ref[...], m2_ref[...], v2_ref[...] = _adamw_math(w_ref[...], g, m_ref[...], v_ref[...])

    flat = [a for grp in groups for a in grp]
    out_shape = [jax.ShapeDtypeStruct(grp[0].shape, F32) for grp in groups for _ in range(3)]
    out_shape.append(jax.ShapeDtypeStruct(groups[0][0].shape, F32))
    outs = pl.pallas_call(body, out_shape=out_shape, name="adamw_small")(*flat, slots)
    return [tuple(outs[3 * i:3 * i + 3]) for i in range(n)], outs[3 * n]


def kernel(x, mem, norm_mix_g, w_in, conv_w, gm_ln_g, gm_ln_b, gm_ws, gm_bs, w_out, norm_x_g, norm_mem_g, w_q, w_kv, w_xo, norm_final_g, loss_target, m_norm_mix_g, m_w_in, m_conv_w, m_gm_ln_g, m_gm_ln_b, m_gm_ws, m_gm_bs, m_w_out, m_norm_x_g, m_norm_mem_g, m_w_q, m_w_kv, m_w_xo, m_norm_final_g, v_norm_mix_g, v_w_in, v_conv_w, v_gm_ln_g, v_gm_ln_b, v_gm_ws, v_gm_bs, v_w_out, v_norm_x_g, v_norm_mem_g, v_w_q, v_w_kv, v_w_xo, v_norm_final_g):
    t = x.shape[1]
    xi = lax.axis_index("x")
    yi = lax.axis_index("y")
    ci = lax.axis_index("c")
    b_idx = jnp.reshape(2 * xi + yi, (1,)).astype(jnp.int32)
    c_idx = jnp.reshape(ci, (1,)).astype(jnp.int32)

    x2d, mem2d, tgt = x[0], mem[0], loss_target[0]
    big = [w_in[0], w_out[0], w_q[0], w_kv[0], w_xo[0]]
    big_m = [m_w_in[0], m_w_out[0], m_w_q[0], m_w_kv[0], m_w_xo[0]]
    big_v = [v_w_in[0], v_w_out[0], v_w_q[0], v_w_kv[0], v_w_xo[0]]
    g3 = norm_final_g.reshape(1, D)

    def pad8(a):
        return jnp.pad(a, ((0, 8 - a.shape[0]), (0, 0)))

    own_blocks = _cast_shards(b_idx, big)

    tril = jnp.tril(jnp.ones((CH, CH), bool))
    wc32 = jnp.where(tril[None], gm_ws[0], 0.0)
    wc = wc32.astype(BF16)
    wct = jnp.swapaxes(wc32, 1, 2).astype(BF16)
    bsb = jnp.broadcast_to(gm_bs[0][:, :, None], (HEADS, CH, CH))

    h, ht = _norm_in(x2d, norm_mix_g)
    blk = 2 * xi + yi
    order = jnp.stack([blk, blk ^ 2, blk ^ 1, blk ^ 3]).astype(jnp.int32)
    proj, win_f, cw8 = _proj_gather(order, h, own_blocks[0], pad8(conv_w[0]))
    mixin, mixt, (wout_f, wq_f, wkv_f, wxo_f) = _mixer_fwd(proj, cw8, gm_ln_g, gm_ln_b, wc, bsb, own_blocks[1:])
    wout2, wq2, wxo2 = wout_f.reshape(MIX, D), wq_f.reshape(D, D), wxo_f.reshape(D, D)
    k, v, mt = _mem_fwd(mem2d, norm_mem_g, wkv_f)
    del mt

    (loss_tile, dmix, dx1, dx1b, h2t, dq, ot, dx2b, dk, dv, dg2, dg3) = _tail(
        x2d, tgt, mixin, wout2, wq2, wxo2, k, v, norm_x_g, g3)
    dwkv, dwkv_b, dgm = _mem_bwd(mem2d, norm_mem_g, dk, dv, wkv_f)
    dwxo, dwxo_b = _grad_matmul(ot, dx2b, by_cols=False, name="grad_w_xo")
    dwq, dwq_b = _grad_matmul(h2t, dq, by_cols=False, name="grad_w_q")
    dwout, dwout_b = _grad_matmul(mixt, dx1b, by_cols=False, name="grad_w_out")

    rx1a = _pair_exchange([dwout_b, dwq_b, dwkv_b, dwxo_b], [], "pair_exchange_a")
    ps_a = _pair_sum(c_idx, [dwout, dwq, dwkv, dwxo], rx1a, [], [], "pair_sum_a")
    sums_a, sums_a_b = ps_a[:4], ps_a[4:8]
    (dproj, dcw, dlng, dlnb, dwc, dbs8), rx2a = _mixer_bwd(proj, dmix, cw8, gm_ln_g, gm_ln_b, wc, wct, bsb, sums_a_b)

    dwin, dwin_b = _grad_matmul(ht, dproj, by_cols=True, name="grad_w_in")
    zero = jnp.zeros((1, D), F32)
    loss_row = jnp.broadcast_to(loss_tile[0:1, 0:1], (1, D))
    sv = jnp.concatenate([zero, dg2, dgm, dg3, dlng, dlnb, dbs8[0:1], loss_row, dcw], axis=0)
    sw = dwc.reshape(HEADS * CH, CH)
    rx1b = _pair_exchange([dwin_b], [sv, sw], "pair_exchange_b")
    ps_b = _pair_sum(c_idx, [dwin], rx1b[:1], [sv, sw], rx1b[1:], "pair_sum_b")
    sums_b, sums_b_b, psmall = ps_b[:1], ps_b[1:2], ps_b[2:]
    grad_x, dg1, rx2b = _input_grad(dproj, win_f, x2d, dx1, norm_mix_g, sums_b_b, psmall)

    red = _chip_sum(jnp.concatenate([b_idx, c_idx]), sums_b + sums_a, rx2b[:1] + rx2a, rx2b[1:])
    (gwin, gwout, gwq, gwkv, gwxo, svf, swf), dg1_slots = _pair_gather(red, dg1)

    names = ["w_in", "w_out", "w_q", "w_kv", "w_xo"]
    big_out = [_adamw_big(w, g, m, v, "adamw_" + nm)
               for w, g, m, v, nm in zip(big, [gwin, gwout, gwq, gwkv, gwxo], big_m, big_v, names)]

    def vec_pack(a1, a2, am, a3, lg, lb, bs):
        return jnp.concatenate([a1, a2, am, a3.reshape(1, D), lg, lb, bs.reshape(1, D), zero], axis=0)

    wv = vec_pack(norm_mix_g, norm_x_g, norm_mem_g, norm_final_g, gm_ln_g, gm_ln_b, gm_bs)
    mv = vec_pack(m_norm_mix_g, m_norm_x_g, m_norm_mem_g, m_norm_final_g, m_gm_ln_g, m_gm_ln_b, m_gm_bs)
    vv = vec_pack(v_norm_mix_g, v_norm_x_g, v_norm_mem_g, v_norm_final_g, v_gm_ln_g, v_gm_ln_b, v_gm_bs)
    loss = svf[7, 0]
    gcw = lax.dynamic_slice_in_dim(svf[8:16], blk * (D // N_CHIP), D // N_CHIP, axis=1)
    gws = swf
    ((dv_, mv_, vv_), (dc_, mc_, vc_), (dws_, mws_, vws_)), gv = _adamw_small([
        (wv, svf[0:8], mv, vv),
        (pad8(conv_w[0]), gcw, pad8(m_conv_w[0]), pad8(v_conv_w[0])),
        (gm_ws.reshape(HEADS * CH, CH), gws, m_gm_ws.reshape(HEADS * CH, CH), v_gm_ws.reshape(HEADS * CH, CH))],
        dg1_slots)

    def unpack(vecs, cw, ws, bigs):
        r = lambda i: vecs[i:i + 1]
        return [r(0), bigs[0][None], cw[0:3][None], r(4), r(5), ws.reshape(1, HEADS, CH, CH), vecs[6].reshape(1, HEADS, CH),
                bigs[1][None], r(1), r(2), bigs[2][None], bigs[3][None], bigs[4][None], vecs[3]]

    grads_out = unpack(gv, gcw, gws, [gwin, gwout, gwq, gwkv, gwxo])
    delta_out = unpack(dv_, dc_, dws_, [o[0] for o in big_out])
    m_out = unpack(mv_, mc_, mws_, [o[1] for o in big_out])
    v_out = unpack(vv_, vc_, vws_, [o[2] for o in big_out])
    return (loss, grad_x[None], *grads_out, *delta_out, *m_out, *v_out)
```

```python
import functools
import math

import jax
import jax.numpy as jnp
from jax import lax
from jax.experimental import pallas as pl
from jax.experimental.pallas import tpu as pltpu

F32 = jnp.float32
BF16 = jnp.bfloat16
MESH = pl.DeviceIdType.MESH

D = 1024
SLAB = 1024
N_SLAB = 7
IN_DIM = N_SLAB * SLAB
MIX = 2 * SLAB
HEADS = 8
CH = 128
XH = 4
XD = D // XH
EPS = 1e-6
GELU_C = math.sqrt(2.0 / math.pi)
GELU_A = 0.044715
N_CHIP = 4
IN_BLK = IN_DIM // N_CHIP
KV_BLK = 2 * D // N_CHIP

ADAM_LR, ADAM_B1, ADAM_B2, ADAM_EPS, ADAM_WD, ADAM_STEP = 0.001, 0.9, 0.999, 1e-08, 0.01, 10

VMEM_LIMIT = 60 * 1024 * 1024


def _cp(sem=None, vmem=None):
    return pltpu.CompilerParams(dimension_semantics=sem, vmem_limit_bytes=vmem)


def _full(shape, buffers=None):
    n = len(shape)
    if buffers is None:
        return pl.BlockSpec(shape, lambda *_: (0,) * n)
    return pl.BlockSpec(shape, lambda *_: (0,) * n, pipeline_mode=pl.Buffered(buffers))


ANY = pl.BlockSpec(memory_space=pl.ANY)


def _bdot(a, b):
    return jnp.dot(a.astype(BF16), b.astype(BF16), preferred_element_type=F32)


def _bdot_nt(a, b):
    return lax.dot_general(a.astype(BF16), b.astype(BF16), (((1,), (1,)), ((), ())), preferred_element_type=F32)


def _rms(x, g):
    r = lax.rsqrt(jnp.mean(x * x, axis=-1, keepdims=True) + EPS)
    return x * r * g, r


def _rms_bwd(dy, x, r, g):
    gdy = dy * g
    dx = r * gdy - x * (r * r * r) * jnp.mean(x * gdy, axis=-1, keepdims=True)
    dg = jnp.sum(dy * x * r, axis=0, keepdims=True)
    return dx, dg


def _gelu_parts(x):
    x2 = x * x
    t = jnp.tanh(GELU_C * (x + GELU_A * x * x2))
    val = 0.5 * x * (1.0 + t)
    grad = 0.5 * (1.0 + t) + 0.5 * x * (1.0 - t * t) * (GELU_C * (1.0 + 3.0 * GELU_A * x2))
    return val, grad


def _gelu(x):
    return 0.5 * x * (1.0 + jnp.tanh(GELU_C * (x + GELU_A * x * x * x)))


def _sigmoid(z):
    return 1.0 / (1.0 + jnp.exp(-z))


def _cast_shards(b_idx, arrs):
    n = len(arrs)
    steps = 8

    def body(b_ref, *refs):
        for i in range(n):
            refs[n + i][...] = refs[i][...].astype(BF16)

    in_specs = [pl.BlockSpec((a.shape[0] // steps, a.shape[1]), lambda i, b: (i, 0)) for a in arrs]
    out_specs = [pl.BlockSpec((None, a.shape[0] // steps, a.shape[1]), lambda i, b: (b[0], i, 0)) for a in arrs]
    return pl.pallas_call(
        body, out_shape=[jax.ShapeDtypeStruct((N_CHIP,) + a.shape, BF16) for a in arrs],
        grid_spec=pltpu.PrefetchScalarGridSpec(num_scalar_prefetch=1, grid=(steps,), in_specs=in_specs, out_specs=out_specs),
        compiler_params=_cp(("arbitrary",)), name="cast_shards")(b_idx, *arrs)


def _proj_gather(order, x, g, win_own, cw8s, tm=1024):
    t = x.shape[0]
    ni = t // tm
    hr = D // 2

    def body(order_ref, x_ref, g_ref, win_in, cw_in, o_ref, ht_ref, win_f, cw_out, hbuf, wv,
             ici_s, ici_r, d2d_s, d2d_r, cw_s, cw_r, loc):
        j, i = pl.program_id(0), pl.program_id(1)
        x, y, c, chips = _coords()
        b = 2 * x + y
        sib = (x, y, 1 - c)
        blks = [2 * chip[0] + chip[1] for chip in chips]

        def half(blk, hc):
            return win_f.at[blk, pl.ds(hc * hr, hr)]

        def cw_cols(blk):
            return cw_out.at[:, pl.ds(blk * (D // N_CHIP), D // N_CHIP)]

        def ici(k, blk):
            return pltpu.make_async_remote_copy(src_ref=half(blk, c), dst_ref=half(blk, c), send_sem=ici_s.at[k],
                                                recv_sem=ici_r.at[k], device_id=(*chips[k], c), device_id_type=MESH)

        def relay(k, blk, hc):
            return pltpu.make_async_remote_copy(src_ref=half(blk, hc), dst_ref=half(blk, hc), send_sem=d2d_s.at[k],
                                                recv_sem=d2d_r.at[k], device_id=sib, device_id_type=MESH)

        def cw_copy(k, blk):
            src = cw_in if blk is None else cw_cols(blk)
            return pltpu.make_async_remote_copy(src_ref=src, dst_ref=cw_cols(b if blk is None else blk), send_sem=cw_s.at[k],
                                                recv_sem=cw_r.at[k], device_id=(*chips[k], c), device_id_type=MESH)

        cw_local = pltpu.make_async_copy(cw_in, cw_cols(b), loc.at[1])

        @pl.when((j == 0) & (i == 0))
        def _():
            cw_local.start()
            for k in range(3):
                ici(k, b).start()
            for k in range(3):
                cw_copy(k, None).start()

        for jj in range(N_CHIP):
            @pl.when((j == jj) & (i == 0))
            def _(jj=jj):
                if jj == 0:
                    blk = b
                else:
                    blk = blks[jj - 1]
                    ici(jj - 1, blk).wait_recv()
                    relay(jj - 1, blk, c).start()
                    relay(jj - 1, blk, 1 - c).wait_recv()
                load = pltpu.make_async_copy(win_f.at[blk], wv, loc.at[0])
                load.start()
                load.wait()

        rows = pl.ds(pl.multiple_of(i * tm, tm), tm)

        @pl.when(j == 0)
        def _():
            h, _ = _rms(x_ref[...], g_ref[...])
            hbuf[rows, :] = h.astype(BF16)
            ht_ref[...] = h.T.astype(BF16)

        o_ref[...] = jnp.dot(hbuf[rows, :], wv[...], preferred_element_type=F32).astype(BF16)

        @pl.when((j == N_CHIP - 1) & (i == ni - 1))
        def _():
            for k in range(3):
                cw_copy(k, blks[k]).wait_recv()
            for k in range(3):
                ici(k, b).wait_send()
                relay(k, blks[k], c).wait_send()
                cw_copy(k, None).wait_send()
            cw_local.wait()

    first = lambda j, i: jnp.where(j == 0, i, ni - 1)
    in_specs = [pl.BlockSpec((tm, D), lambda j, i, o: (first(j, i), 0)), pl.BlockSpec((1, D), lambda j, i, o: (0, 0)), ANY, ANY]
    out_specs = [pl.BlockSpec((tm, IN_BLK), lambda j, i, o: (i, o[j])),
                 pl.BlockSpec((D, tm), lambda j, i, o: (0, first(j, i))), ANY, ANY]
    return pl.pallas_call(
        body, out_shape=[jax.ShapeDtypeStruct((t, IN_DIM), BF16), jax.ShapeDtypeStruct((D, t), BF16),
                         jax.ShapeDtypeStruct(win_own.shape, BF16), jax.ShapeDtypeStruct((8, D), F32)],
        grid_spec=pltpu.PrefetchScalarGridSpec(
            num_scalar_prefetch=1, grid=(N_CHIP, ni), in_specs=in_specs, out_specs=out_specs,
            scratch_shapes=[pltpu.VMEM((t, D), BF16), pltpu.VMEM((D, IN_BLK), BF16)]
            + [pltpu.SemaphoreType.DMA((3,))] * 6 + [pltpu.SemaphoreType.DMA((2,))]),
        input_output_aliases={3: 2},
        compiler_params=_cp(("arbitrary", "arbitrary"), VMEM_LIMIT), name="proj_gather")(order, x, g, win_own, cw8s)


def _gather_steps(outs, ici_s, ici_r, d2d_s, d2d_r):
    x, y, c, chips = _coords()
    b = 2 * x + y
    sib = (x, y, 1 - c)
    nw = len(outs)

    def half(w, blk, hc):
        hr = outs[w].shape[1] // 2
        return outs[w].at[blk, pl.ds(hc * hr, hr)]

    def ici(w, k, blk):
        return pltpu.make_async_remote_copy(src_ref=half(w, blk, c), dst_ref=half(w, blk, c), send_sem=ici_s.at[w, k],
                                            recv_sem=ici_r.at[w, k], device_id=(*chips[k], c), device_id_type=MESH)

    def relay(w, k, blk, hc):
        return pltpu.make_async_remote_copy(src_ref=half(w, blk, hc), dst_ref=half(w, blk, hc), send_sem=d2d_s.at[w, k],
                                            recv_sem=d2d_r.at[w, k], device_id=sib, device_id_type=MESH)

    def start():
        for w in range(nw):
            for k in range(3):
                ici(w, k, b).start()

    def finish():
        for w in range(nw):
            for k in range(3):
                blk = 2 * chips[k][0] + chips[k][1]
                ici(w, k, blk).wait_recv()
                relay(w, k, blk, c).start()
        for w in range(nw):
            for k in range(3):
                blk = 2 * chips[k][0] + chips[k][1]
                relay(w, k, blk, 1 - c).wait_recv()
        for w in range(nw):
            for k in range(3):
                blk = 2 * chips[k][0] + chips[k][1]
                ici(w, k, b).wait_send()
                relay(w, k, blk, c).wait_send()

    return start, finish


def _mixer_fwd(proj, cw8, lng, lnb, wc, bsb, fulls, tm=256):
    t = proj.shape[0]
    nt = t // tm
    nch = tm // CH
    nw = len(fulls)

    def body(*refs):
        p_ref, cw_ref, lng_ref, lnb_ref, wc_ref, bsb_ref = refs[:6]
        mix_ref, mixt_ref = refs[6 + nw:8 + nw]
        w_outs = refs[8 + nw:8 + 2 * nw]
        prev_ref, stage_ref, ici_s, ici_r, d2d_s, d2d_r = refs[8 + 2 * nw:]
        gather_start, gather_finish = _gather_steps(w_outs, ici_s, ici_r, d2d_s, d2d_r)

        @pl.when(pl.program_id(0) == 0)
        def _():
            gather_start()
            prev_ref[...] = jnp.zeros_like(prev_ref)

        rows = lax.broadcasted_iota(jnp.int32, (tm, CH), 0)
        for s in range(HEADS):
            cs = pl.ds(CH * s, CH)

            def slab(k):
                return p_ref[:, pl.ds(k * SLAB + CH * s, CH)].astype(F32)

            gb, gc, xa, za = slab(0), slab(1), slab(2), slab(3)
            cx = gc * xa
            p6 = jnp.broadcast_to(prev_ref[6:7, cs], (tm, CH))
            p7 = jnp.broadcast_to(prev_ref[7:8, cs], (tm, CH))
            c1 = jnp.where(rows == 0, p7, pltpu.roll(cx, 1, 0))
            c2 = jnp.where(rows == 0, p6, jnp.where(rows == 1, p7, pltpu.roll(cx, 2, 0)))
            prev_ref[:, cs] = cx[tm - 8:, :]
            cv = cw_ref[0:1, cs] * c2 + cw_ref[1:2, cs] * c1 + cw_ref[2:3, cs] * cx
            stage_ref[:, cs] = gb * cv * (za * _sigmoid(za))

            u, v, zb = slab(4), slab(5), slab(6)
            ug, vg = _gelu(u), _gelu(v)
            dlt = vg - jnp.mean(vg, axis=-1, keepdims=True)
            vhat = dlt * lax.rsqrt(jnp.mean(dlt * dlt, axis=-1, keepdims=True) + EPS)
            vn = (vhat * lng_ref[:, cs] + lnb_ref[:, cs]).astype(BF16)
            gate = ug * (zb * _sigmoid(zb))
            for c in range(nch):
                rs = slice(CH * c, CH * (c + 1))
                sp = jnp.dot(wc_ref[s], vn[rs], preferred_element_type=F32) + bsb_ref[s]
                stage_ref[rs, pl.ds(SLAB + CH * s, CH)] = gate[rs] * sp

        full = stage_ref[...]
        mix_ref[...] = full.astype(BF16)
        mixt_ref[...] = full.T.astype(BF16)

        @pl.when(pl.program_id(0) == nt - 1)
        def _():
            gather_finish()

    sems = [pltpu.SemaphoreType.DMA((nw, 3))] * 4
    outs = pl.pallas_call(
        body, grid=(nt,),
        in_specs=[pl.BlockSpec((tm, IN_DIM), lambda i: (i, 0)), _full((8, D)), _full((1, D)), _full((1, D)),
                  _full((HEADS, CH, CH)), _full((HEADS, CH, CH))] + [ANY] * nw,
        out_specs=[pl.BlockSpec((tm, MIX), lambda i: (i, 0)), pl.BlockSpec((MIX, tm), lambda i: (0, i))] + [ANY] * nw,
        out_shape=[jax.ShapeDtypeStruct((t, MIX), BF16), jax.ShapeDtypeStruct((MIX, t), BF16)]
        + [jax.ShapeDtypeStruct(f.shape, f.dtype) for f in fulls],
        input_output_aliases={6 + w: 2 + w for w in range(nw)},
        scratch_shapes=[pltpu.VMEM((8, D), F32), pltpu.VMEM((tm, MIX), F32)] + sems,
        compiler_params=_cp(("arbitrary",), VMEM_LIMIT), name="mixer_fwd")(proj, cw8, lng, lnb, wc, bsb, *fulls)
    return outs[0], outs[1], outs[2:]


def _mem_fwd(mem, gm, wkv_f):
    n_mem = mem.shape[0]

    def body(mem_ref, gm_ref, w_ref, k_ref, v_ref, mt_ref):
        m, _ = _rms(mem_ref[...], gm_ref[...])
        mb = m.astype(BF16)
        mt_ref[...] = m.T.astype(BF16)
        for j in range(N_CHIP):
            dst = k_ref if j < 2 else v_ref
            dst[:, pl.ds(KV_BLK * (j % 2), KV_BLK)] = jnp.dot(mb, w_ref[j], preferred_element_type=F32).astype(BF16)

    return pl.pallas_call(
        body, out_shape=[jax.ShapeDtypeStruct((n_mem, D), BF16), jax.ShapeDtypeStruct((n_mem, D), BF16),
                         jax.ShapeDtypeStruct((D, n_mem), BF16)],
        compiler_params=_cp(None, VMEM_LIMIT), name="mem_fwd")(mem, gm, wkv_f)


def _tail(x, tgt, mixin, wout, wq, wxo, k, v, g2, g3, tm=512, sub=512):
    t = x.shape[0]
    n_mem = k.shape[0]
    scale = 1.0 / math.sqrt(XD)

    def body(x_ref, tgt_ref, mix_ref, wout_ref, wq_ref, wxo_ref, k_ref, v_ref, g2_ref, g3_ref,
             loss_ref, dmix_ref, dx1b_ref, h2t_ref, dq_ref, ot_ref, dx2b_ref, dk_ref, dv_ref, dg2_ref, dg3_ref):
        @pl.when(pl.program_id(0) == 0)
        def _():
            loss_ref[...] = jnp.zeros_like(loss_ref)
            dk_ref[...] = jnp.zeros_like(dk_ref)
            dv_ref[...] = jnp.zeros_like(dv_ref)
            dg2_ref[...] = jnp.zeros_like(dg2_ref)
            dg3_ref[...] = jnp.zeros_like(dg3_ref)

        g2, g3 = g2_ref[...], g3_ref[...]
        for sb in range(tm // sub):
            rs = pl.ds(sub * sb, sub)
            x1 = x_ref[rs, :] + jnp.dot(mix_ref[rs, :], wout_ref[...], preferred_element_type=F32)
            h2, r2 = _rms(x1, g2)
            h2t_ref[:, rs] = h2.T.astype(BF16)
            q = _bdot(h2, wq_ref[...]).astype(BF16)
            probs, outs = [], []
            for hd in range(XH):
                hs = pl.ds(XD * hd, XD)
                s = _bdot_nt(q[:, XD * hd:XD * (hd + 1)], k_ref[:, hs]) * scale
                e = jnp.exp(s - jnp.max(s, axis=-1, keepdims=True))
                p = e / jnp.sum(e, axis=-1, keepdims=True)
                probs.append(p)
                outs.append(_bdot(p, v_ref[:, hs]))
            o = jnp.concatenate(outs, axis=-1)
            ot_ref[:, rs] = o.T.astype(BF16)
            x2 = x1 + _bdot(o, wxo_ref[...])
            y, r3 = _rms(x2, g3)
            diff = y - tgt_ref[rs, :]
            row_loss = jnp.sum(diff * diff, axis=-1, keepdims=True)
            loss_ref[...] += jnp.broadcast_to(jnp.sum(row_loss, axis=0, keepdims=True) * (0.5 / D), loss_ref.shape)

            dx2, dg3 = _rms_bwd(diff * (1.0 / D), x2, r3, g3)
            dg3_ref[...] += dg3
            dx2b = dx2.astype(BF16)
            dx2b_ref[rs, :] = dx2b
            do = _bdot_nt(dx2b, wxo_ref[...])
            dqs = []
            for hd in range(XH):
                hs = pl.ds(XD * hd, XD)
                p = probs[hd]
                do_h = do[:, XD * hd:XD * (hd + 1)]
                dv_ref[:, hs] += _bdot(p.T, do_h)
                dp = _bdot_nt(do_h, v_ref[:, hs])
                ds = p * (dp - jnp.sum(dp * p, axis=-1, keepdims=True))
                dqs.append(_bdot(ds, k_ref[:, hs]) * scale)
                dk_ref[:, hs] += _bdot(ds.T, q[:, XD * hd:XD * (hd + 1)]) * scale
            dq = jnp.concatenate(dqs, axis=-1).astype(BF16)
            dq_ref[rs, :] = dq
            dx1n, dg2 = _rms_bwd(_bdot_nt(dq, wq_ref[...]), x1, r2, g2)
            dg2_ref[...] += dg2
            dx1b = (dx2 + dx1n).astype(BF16)
            dx1b_ref[rs, :] = dx1b
            dmix_ref[rs, :] = _bdot_nt(dx1b, wout_ref[...]).astype(BF16)

    tok = lambda w: pl.BlockSpec((tm, w), lambda i: (i, 0))
    tok_t = lambda w: pl.BlockSpec((w, tm), lambda i: (0, i))
    return pl.pallas_call(
        body, grid=(t // tm,),
        in_specs=[tok(D), tok(D), tok(MIX), _full((MIX, D), 1), _full((D, D), 1), _full((D, D), 1),
                  _full((n_mem, D), 1), _full((n_mem, D), 1), _full((1, D)), _full((1, D))],
        out_specs=[_full((8, 128)), tok(MIX), tok(D), tok_t(D), tok(D), tok_t(D), tok(D),
                   _full((n_mem, D)), _full((n_mem, D)), _full((1, D)), _full((1, D))],
        out_shape=[jax.ShapeDtypeStruct((8, 128), F32), jax.ShapeDtypeStruct((t, MIX), BF16),
                   jax.ShapeDtypeStruct((t, D), BF16),
                   jax.ShapeDtypeStruct((D, t), BF16), jax.ShapeDtypeStruct((t, D), BF16),
                   jax.ShapeDtypeStruct((D, t), BF16), jax.ShapeDtypeStruct((t, D), BF16),
                   jax.ShapeDtypeStruct((n_mem, D), F32), jax.ShapeDtypeStruct((n_mem, D), F32),
                   jax.ShapeDtypeStruct((1, D), F32), jax.ShapeDtypeStruct((1, D), F32)],
        compiler_params=_cp(("arbitrary",), VMEM_LIMIT), name="tail")(x, tgt, mixin, wout, wq, wxo, k, v, g2, g3)


def _mem_bwd(mem, gm, dk, dv, wkv_f):
    def body(mem_ref, gm_ref, dk_ref, dv_ref, w_ref, dw_ref, dwb_ref, dgm_ref):
        mem_v = mem_ref[...]
        m, rm = _rms(mem_v, gm_ref[...])
        mt = m.T.astype(BF16)
        dm = jnp.zeros_like(mem_v)
        for j in range(N_CHIP):
            src = dk_ref if j < 2 else dv_ref
            dkv = src[:, pl.ds(KV_BLK * (j % 2), KV_BLK)].astype(BF16)
            dw = jnp.dot(mt, dkv, preferred_element_type=F32)
            dw_ref[j] = dw
            dwb_ref[j] = dw.astype(BF16)
            dm = dm + _bdot_nt(dkv, w_ref[j])
        dgm_ref[...] = jnp.sum(dm * mem_v * rm, axis=0, keepdims=True)

    return pl.pallas_call(
        body, out_shape=[jax.ShapeDtypeStruct((N_CHIP, D, KV_BLK), F32), jax.ShapeDtypeStruct((N_CHIP, D, KV_BLK), BF16),
                         jax.ShapeDtypeStruct((1, D), F32)],
        compiler_params=_cp(None, VMEM_LIMIT), name="mem_bwd")(mem, gm, dk, dv, wkv_f)


def _mixer_bwd(proj, dmix, cw8, lng, lnb, wc, wct, bsb, xch, tm=256):
    t = proj.shape[0]
    nt = t // tm
    nch = tm // CH
    hb = 16
    nx = len(xch)
    xch_shape, xch_sems = _exchange_shapes(xch, [])

    def body(*refs):
        p_ref, pgc_ref, pxa_ref, dm_ref, cw_ref, lng_ref, lnb_ref, wc_ref, wct_ref, bsb_ref = refs[:10]
        x_in = refs[10:10 + nx]
        dp_ref, dcw_ref, dlng_ref, dlnb_ref, dwc_ref, dbs_ref = refs[10 + nx:16 + nx]
        x_out = refs[16 + nx:16 + 2 * nx]
        next_ref, send, recv, loc = refs[16 + 2 * nx:]
        exchange_start, exchange_finish = _exchange_steps(x_in, x_out, nx, send, recv, loc)
        i = pl.program_id(0)

        @pl.when(i == 0)
        def _():
            exchange_start()
            next_ref[...] = jnp.zeros_like(next_ref)
            dcw_ref[...] = jnp.zeros_like(dcw_ref)
            dlng_ref[...] = jnp.zeros_like(dlng_ref)
            dlnb_ref[...] = jnp.zeros_like(dlnb_ref)
            dwc_ref[...] = jnp.zeros_like(dwc_ref)
            dbs_ref[...] = jnp.zeros_like(dbs_ref)

        first_tile = i == nt - 1
        rows = lax.broadcasted_iota(jnp.int32, (tm, CH), 0)
        ones8 = jnp.ones((8, CH), BF16)
        for s in range(HEADS):
            cs = pl.ds(CH * s, CH)

            def slab(k):
                return p_ref[:, pl.ds(k * SLAB + CH * s, CH)].astype(F32)

            gb, gc, xa, za = slab(0), slab(1), slab(2), slab(3)
            da = dm_ref[:, cs].astype(F32)
            cx = gc * xa
            cxp = pgc_ref[:, cs].astype(F32) * pxa_ref[:, cs].astype(F32)
            cxp = jnp.where(first_tile, jnp.zeros_like(cxp), cxp)
            p6 = jnp.broadcast_to(cxp[hb - 2:hb - 1, :], (tm, CH))
            p7 = jnp.broadcast_to(cxp[hb - 1:hb, :], (tm, CH))
            c1 = jnp.where(rows == 0, p7, pltpu.roll(cx, 1, 0))
            c2 = jnp.where(rows == 0, p6, jnp.where(rows == 1, p7, pltpu.roll(cx, 2, 0)))
            w0, w1, w2 = cw_ref[0:1, cs], cw_ref[1:2, cs], cw_ref[2:3, cs]
            cv = w0 * c2 + w1 * c1 + w2 * cx
            sg = _sigmoid(za)
            sa = za * sg
            dcv = da * gb * sa
            dp_ref[:, pl.ds(0 * SLAB + CH * s, CH)] = (da * cv * sa).astype(BF16)
            dp_ref[:, pl.ds(3 * SLAB + CH * s, CH)] = (da * gb * cv * (sg * (1.0 + za * (1.0 - sg)))).astype(BF16)
            n0 = jnp.broadcast_to(next_ref[0:1, cs], (tm, CH))
            n1 = jnp.broadcast_to(next_ref[1:2, cs], (tm, CH))
            u1 = jnp.where(rows == tm - 1, n0, pltpu.roll(dcv, tm - 1, 0))
            u2 = jnp.where(rows == tm - 2, n0, jnp.where(rows == tm - 1, n1, pltpu.roll(dcv, tm - 2, 0)))
            next_ref[:, cs] = dcv[0:8, :]
            dcx = w2 * dcv + w1 * u1 + w0 * u2
            dp_ref[:, pl.ds(1 * SLAB + CH * s, CH)] = (dcx * xa).astype(BF16)
            dp_ref[:, pl.ds(2 * SLAB + CH * s, CH)] = (dcx * gc).astype(BF16)
            dcw_ref[0:1, cs] += jnp.sum(dcv * c2, axis=0, keepdims=True)
            dcw_ref[1:2, cs] += jnp.sum(dcv * c1, axis=0, keepdims=True)
            dcw_ref[2:3, cs] += jnp.sum(dcv * cx, axis=0, keepdims=True)

            u, v, zb = slab(4), slab(5), slab(6)
            db = dm_ref[:, pl.ds(SLAB + CH * s, CH)].astype(F32)
            ug, ugrad = _gelu_parts(u)
            vg, vgrad = _gelu_parts(v)
            dlt = vg - jnp.mean(vg, axis=-1, keepdims=True)
            rstd = lax.rsqrt(jnp.mean(dlt * dlt, axis=-1, keepdims=True) + EPS)
            vhat = dlt * rstd
            lg = lng_ref[:, cs]
            vn = (vhat * lg + lnb_ref[:, cs]).astype(BF16)
            sgb = _sigmoid(zb)
            szb = zb * sgb
            sps, dvns = [], []
            dbs = jnp.zeros((8, CH), F32)
            dwc = jnp.zeros((CH, CH), F32)
            for c in range(nch):
                rs = slice(CH * c, CH * (c + 1))
                sp = jnp.dot(wc_ref[s], vn[rs], preferred_element_type=F32) + bsb_ref[s]
                dsp = (db[rs] * ug[rs] * szb[rs]).astype(BF16)
                dbs = dbs + lax.dot_general(ones8, dsp, (((1,), (1,)), ((), ())), preferred_element_type=F32)
                dwc = dwc + lax.dot_general(dsp, vn[rs], (((1,), (1,)), ((), ())), preferred_element_type=F32)
                dvns.append(jnp.dot(wct_ref[s], dsp, preferred_element_type=F32))
                sps.append(sp)
            sp = jnp.concatenate(sps, axis=0)
            dvn = jnp.concatenate(dvns, axis=0)
            dbs_ref[:, cs] += dbs
            dwc_ref[s] += dwc
            dlng_ref[:, cs] += jnp.sum(dvn * vhat, axis=0, keepdims=True)
            dlnb_ref[:, cs] += jnp.sum(dvn, axis=0, keepdims=True)
            dvhat = dvn * lg
            dvg = rstd * (dvhat - jnp.mean(dvhat, axis=-1, keepdims=True)
                          - vhat * jnp.mean(dvhat * vhat, axis=-1, keepdims=True))
            dp_ref[:, pl.ds(4 * SLAB + CH * s, CH)] = (db * sp * szb * ugrad).astype(BF16)
            dp_ref[:, pl.ds(5 * SLAB + CH * s, CH)] = (dvg * vgrad).astype(BF16)
            dp_ref[:, pl.ds(6 * SLAB + CH * s, CH)] = (db * ug * sp * (sgb * (1.0 + zb * (1.0 - sgb)))).astype(BF16)

        @pl.when(i == nt - 1)
        def _():
            tril = lax.broadcasted_iota(jnp.int32, (CH, CH), 0) >= lax.broadcasted_iota(jnp.int32, (CH, CH), 1)
            for s in range(HEADS):
                dwc_ref[s] = jnp.where(tril, dwc_ref[s], 0.0)
            exchange_finish()

    rev = lambda i: nt - 1 - i
    halo = lambda col: pl.BlockSpec((hb, SLAB), lambda i: (jnp.maximum(rev(i) * (tm // hb) - 1, 0), col))
    outs = pl.pallas_call(
        body, grid=(nt,),
        in_specs=[pl.BlockSpec((tm, IN_DIM), lambda i: (rev(i), 0)), halo(1), halo(2),
                  pl.BlockSpec((tm, MIX), lambda i: (rev(i), 0)), _full((8, D)), _full((1, D)), _full((1, D)),
                  _full((HEADS, CH, CH)), _full((HEADS, CH, CH)), _full((HEADS, CH, CH))] + [ANY] * nx,
        out_specs=[pl.BlockSpec((tm, IN_DIM), lambda i: (rev(i), 0)), _full((8, D)), _full((1, D)), _full((1, D)),
                   _full((HEADS, CH, CH)), _full((8, D))] + [ANY] * nx,
        out_shape=[jax.ShapeDtypeStruct((t, IN_DIM), BF16), jax.ShapeDtypeStruct((8, D), F32),
                   jax.ShapeDtypeStruct((1, D), F32), jax.ShapeDtypeStruct((1, D), F32),
                   jax.ShapeDtypeStruct((HEADS, CH, CH), F32), jax.ShapeDtypeStruct((8, D), F32)] + xch_shape,
        scratch_shapes=[pltpu.VMEM((8, D), F32)] + xch_sems,
        compiler_params=_cp(("arbitrary",), VMEM_LIMIT), name="mixer_bwd")(
            proj, proj, proj, dmix, cw8, lng, lnb, wc, wct, bsb, *xch)
    return outs[:6], outs[6:]


def _grad_matmul(at, b, *, by_cols, name, tk=1024):
    m, t = at.shape
    n = b.shape[1]
    nk = t // tk
    nj = N_CHIP if by_cols else 1
    bn = n // nj

    def body(a_ref, b_ref, o_ref, ob_ref):
        kk = pl.program_id(1)
        part = jnp.dot(a_ref[...], b_ref[...], preferred_element_type=F32)

        @pl.when(kk == 0)
        def _():
            o_ref[...] = part

        @pl.when(kk > 0)
        def _():
            o_ref[...] += part

        @pl.when(kk == nk - 1)
        def _():
            ob_ref[...] = o_ref[...].astype(BF16)

    a_spec = pl.BlockSpec((m, tk), lambda j, k: (0, k))
    b_spec = pl.BlockSpec((tk, bn), lambda j, k: (k, j))
    o_spec = pl.BlockSpec((None, m, bn), lambda j, k: (j, 0, 0))
    o32, o16 = pl.pallas_call(
        body, grid=(nj, nk), in_specs=[a_spec, b_spec], out_specs=[o_spec, o_spec],
        out_shape=[jax.ShapeDtypeStruct((nj, m, bn), F32), jax.ShapeDtypeStruct((nj, m, bn), BF16)],
        compiler_params=_cp(("parallel", "arbitrary"), VMEM_LIMIT), name=name)(at, b)
    if by_cols:
        return o32, o16
    return o32.reshape(N_CHIP, m // N_CHIP, n), o16.reshape(N_CHIP, m // N_CHIP, n)


def _input_grad(dproj, win_f, x, dx1, g1, xch_big, xch_small, tm=512):
    t = x.shape[0]
    nt = t // tm
    nb, nx = len(xch_big), len(xch_big) + len(xch_small)
    xch_shape, xch_sems = _exchange_shapes(xch_big, xch_small)

    def body(*refs):
        dp_ref, w_ref, x_ref, dx1_ref, g_ref = refs[:5]
        x_in = refs[5:5 + nx]
        gx_ref, dg_ref = refs[5 + nx:7 + nx]
        x_out = refs[7 + nx:7 + 2 * nx]
        send, recv, loc = refs[7 + 2 * nx:]
        exchange_start, exchange_finish = _exchange_steps(x_in, x_out, nb, send, recv, loc)

        @pl.when(pl.program_id(0) == 0)
        def _():
            exchange_start()
            dg_ref[...] = jnp.zeros_like(dg_ref)

        dh = _bdot_nt(dp_ref[:, pl.ds(0, IN_BLK)], w_ref[0])
        for j in range(1, N_CHIP):
            dh = dh + _bdot_nt(dp_ref[:, pl.ds(IN_BLK * j, IN_BLK)], w_ref[j])
        xv = x_ref[...]
        r = lax.rsqrt(jnp.mean(xv * xv, axis=-1, keepdims=True) + EPS)
        dxn, dg = _rms_bwd(dh, xv, r, g_ref[...])
        gx_ref[...] = dx1_ref[...].astype(F32) + dxn
        dg_ref[0:1, :] += dg

        @pl.when(pl.program_id(0) == nt - 1)
        def _():
            exchange_finish()

    tok = lambda w: pl.BlockSpec((tm, w), lambda i: (i, 0))
    outs = pl.pallas_call(
        body, grid=(nt,),
        in_specs=[tok(IN_DIM), _full((N_CHIP, D, IN_BLK), 1), tok(D), tok(D), _full((1, D))] + [ANY] * nx,
        out_specs=[tok(D), _full((8, D))] + [ANY] * nx,
        out_shape=[jax.ShapeDtypeStruct((t, D), F32), jax.ShapeDtypeStruct((8, D), F32)] + xch_shape,
        scratch_shapes=xch_sems,
        compiler_params=_cp(("arbitrary",), VMEM_LIMIT), name="input_grad")(dproj, win_f, x, dx1, g1, *xch_big, *xch_small)
    return outs[0], outs[1], outs[2:]


def _coords():
    x, y, c = lax.axis_index("x"), lax.axis_index("y"), lax.axis_index("c")
    chips = [(1 - x, y), (x, 1 - y), (1 - x, 1 - y)]
    return x, y, c, chips


def _pair_exchange(grads_b, smalls, name):
    ng, ns = len(grads_b), len(smalls)
    n = ng + ns

    def body(*refs):
        ins, outs, send, recv = refs[:n], refs[n:2 * n], refs[2 * n], refs[2 * n + 1]
        x, y, c, _ = _coords()
        cps = []
        for i in range(n):
            if i < ng:
                hr = ins[i].shape[1] // 2
                src = ins[i].at[pl.ds(0, N_CHIP), pl.ds((1 - c) * hr, hr)]
            else:
                hr = ins[i].shape[0] // 2
                src = ins[i].at[pl.ds((1 - c) * hr, hr)]
            cps.append(pltpu.make_async_remote_copy(
                src_ref=src, dst_ref=outs[i], send_sem=send.at[i], recv_sem=recv.at[i],
                device_id=(x, y, 1 - c), device_id_type=MESH))
        for cp in cps:
            cp.start()
        for cp in cps:
            cp.wait()

    out_shape = [jax.ShapeDtypeStruct((N_CHIP, g.shape[1] // 2, g.shape[2]), g.dtype) for g in grads_b]
    out_shape += [jax.ShapeDtypeStruct((s.shape[0] // 2, s.shape[1]), s.dtype) for s in smalls]
    return pl.pallas_call(
        body, out_shape=out_shape, in_specs=[ANY] * n, out_specs=[ANY] * n,
        scratch_shapes=[pltpu.SemaphoreType.DMA((n,)), pltpu.SemaphoreType.DMA((n,))],
        name=name)(*grads_b, *smalls)


def _pair_sum(c_idx, grads, recvd, smalls, smalls_recvd, name):
    ng, ns = len(grads), len(smalls)
    halves = [g.shape[1] // 2 for g in grads]

    def body(c_ref, *refs):
        g_in, r_in = refs[:ng], refs[ng:2 * ng]
        s_in, sr_in = refs[2 * ng:2 * ng + ns], refs[2 * ng + ns:2 * ng + 2 * ns]
        o = refs[2 * ng + 2 * ns:]
        for i in range(ng):
            tot = g_in[i][...] + r_in[i][...].astype(F32)
            o[i][...] = tot
            o[ng + i][...] = tot.astype(BF16)
        for i in range(ns):
            o[2 * ng + i][...] = s_in[i][...] + sr_in[i][...]

    in_specs = [pl.BlockSpec((None, None, halves[i], g.shape[2]), lambda b, c: (b, c[0], 0, 0)) for i, g in enumerate(grads)]
    in_specs += [pl.BlockSpec((None, halves[i], g.shape[2]), lambda b, c: (b, 0, 0)) for i, g in enumerate(grads)]
    in_specs += [pl.BlockSpec((None, s.shape[0] // 2, s.shape[1]), lambda b, c: (c[0], 0, 0)) for s in smalls]
    in_specs += [pl.BlockSpec((s.shape[0] // 2, s.shape[1]), lambda b, c: (0, 0)) for s in smalls]
    blk = [pl.BlockSpec((None, halves[i], g.shape[2]), lambda b, c: (b, 0, 0)) for i, g in enumerate(grads)]
    out_specs = blk + blk + [pl.BlockSpec((s.shape[0] // 2, s.shape[1]), lambda b, c: (0, 0)) for s in smalls]
    out_shape = [jax.ShapeDtypeStruct((N_CHIP, halves[i], g.shape[2]), F32) for i, g in enumerate(grads)]
    out_shape += [jax.ShapeDtypeStruct((N_CHIP, halves[i], g.shape[2]), BF16) for i, g in enumerate(grads)]
    out_shape += [jax.ShapeDtypeStruct((s.shape[0] // 2, s.shape[1]), F32) for s in smalls]
    grads4 = [g.reshape(N_CHIP, 2, halves[i], g.shape[2]) for i, g in enumerate(grads)]
    smalls3 = [s.reshape(2, s.shape[0] // 2, s.shape[1]) for s in smalls]
    return pl.pallas_call(
        body, out_shape=out_shape,
        grid_spec=pltpu.PrefetchScalarGridSpec(num_scalar_prefetch=1, grid=(N_CHIP,), in_specs=in_specs, out_specs=out_specs),
        compiler_params=_cp(("arbitrary",), VMEM_LIMIT), name=name)(c_idx, *grads4, *recvd, *smalls3, *smalls_recvd)


def _exchange_steps(ins, outs, ng, send, recv, loc):
    x, y, c, chips = _coords()
    b = 2 * x + y
    n = len(ins)
    blks = [2 * chip[0] + chip[1] for chip in chips]

    def copy(i, k, arriving):
        if i < ng:
            src, dst = ins[i].at[blks[k]], outs[i].at[k]
        else:
            src, dst = ins[i], outs[i].at[blks[k] if arriving else b]
        return pltpu.make_async_remote_copy(src_ref=dst if arriving else src, dst_ref=dst, send_sem=send.at[i, k],
                                            recv_sem=recv.at[i, k], device_id=(*chips[k], c), device_id_type=MESH)

    local = [pltpu.make_async_copy(ins[i], outs[i].at[b], loc.at[i - ng]) for i in range(ng, n)]

    def start():
        for cp in local:
            cp.start()
        for i in range(n):
            for k in range(3):
                copy(i, k, False).start()

    def finish():
        for i in range(n):
            for k in range(3):
                copy(i, k, True).wait_recv()
        for i in range(n):
            for k in range(3):
                copy(i, k, False).wait_send()
        for cp in local:
            cp.wait()

    return start, finish


def _exchange_shapes(sums_b, smalls):
    n = len(sums_b) + len(smalls)
    out_shape = [jax.ShapeDtypeStruct((3,) + g.shape[1:], g.dtype) for g in sums_b]
    out_shape += [jax.ShapeDtypeStruct((N_CHIP,) + s.shape, s.dtype) for s in smalls]
    sems = [pltpu.SemaphoreType.DMA((n, 3)), pltpu.SemaphoreType.DMA((n, 3)),
            pltpu.SemaphoreType.DMA((max(len(smalls), 1),))]
    return out_shape, sems


def _chip_sum(bc_idx, sums, recvd, smalls_slots, steps=4):
    ng, ns = len(sums), len(smalls_slots)

    def body(bc_ref, *refs):
        own, rx = refs[:ng], refs[ng:2 * ng]
        sl = refs[2 * ng:2 * ng + ns]
        o = refs[2 * ng + ns:]
        for i in range(ng):
            tot = own[i][...]
            for j in range(3):
                tot = tot + rx[i][j].astype(F32)
            o[i][...] = tot
        for i in range(ns):
            o[ng + i][...] = ((sl[i][0] + sl[i][1]) + sl[i][2]) + sl[i][3]

    def rows(g):
        return g.shape[1] // steps

    in_specs = [pl.BlockSpec((None, rows(g), g.shape[2]), lambda r, bc: (bc[0], r, 0)) for g in sums]
    in_specs += [pl.BlockSpec((3, rows(g), g.shape[2]), lambda r, bc: (0, r, 0)) for g in sums]
    in_specs += [pl.BlockSpec(s.shape, lambda r, bc: (0, 0, 0)) for s in smalls_slots]
    out_specs = [pl.BlockSpec((rows(g), g.shape[2]), lambda r, bc: (bc[1] * steps + r, 0)) for g in sums]
    out_specs += [pl.BlockSpec(s.shape[1:], lambda r, bc: (bc[1], 0)) for s in smalls_slots]
    out_shape = [jax.ShapeDtypeStruct((2 * g.shape[1], g.shape[2]), F32) for g in sums]
    out_shape += [jax.ShapeDtypeStruct((2 * s.shape[1], s.shape[2]), F32) for s in smalls_slots]
    return pl.pallas_call(
        body, out_shape=out_shape,
        grid_spec=pltpu.PrefetchScalarGridSpec(num_scalar_prefetch=1, grid=(steps,), in_specs=in_specs, out_specs=out_specs),
        compiler_params=_cp(("arbitrary",), VMEM_LIMIT), name="chip_sum")(bc_idx, *sums, *recvd, *smalls_slots)


def _pair_gather(arrs, last):
    n = len(arrs)
    n_dev = 8

    def body(*refs):
        last_in = refs[n]
        outs = refs[n + 1:2 * n + 1]
        slots = refs[2 * n + 1]
        send, recv, gs, gr, loc = refs[2 * n + 2:]
        x, y, c, _ = _coords()
        me = 4 * x + 2 * y + c
        mine_slot = pltpu.make_async_copy(last_in, slots.at[me], loc)
        mine_slot.start()
        gathers = []
        for r in range(1, n_dev):
            to = (me + r) % n_dev
            gathers.append(pltpu.make_async_remote_copy(
                src_ref=last_in, dst_ref=slots.at[me], send_sem=gs.at[r - 1], recv_sem=gr.at[me],
                device_id=(to // 4, (to // 2) % 2, to % 2), device_id_type=MESH))
        for cp in gathers:
            cp.start()
        cps = []
        for i in range(n):
            hr = outs[i].shape[0] // 2
            mine = outs[i].at[pl.ds(c * hr, hr)]
            cps.append(pltpu.make_async_remote_copy(
                src_ref=mine, dst_ref=mine, send_sem=send.at[i], recv_sem=recv.at[i],
                device_id=(x, y, 1 - c), device_id_type=MESH))
        for cp in cps:
            cp.start()
        for i in range(n):
            hr = outs[i].shape[0] // 2
            theirs = outs[i].at[pl.ds((1 - c) * hr, hr)]
            pltpu.make_async_remote_copy(
                src_ref=theirs, dst_ref=theirs, send_sem=send.at[i], recv_sem=recv.at[i],
                device_id=(x, y, 1 - c), device_id_type=MESH).wait_recv()
        for r in range(1, n_dev):
            frm = (me + r) % n_dev
            pltpu.make_async_remote_copy(
                src_ref=slots.at[frm], dst_ref=slots.at[frm], send_sem=gs.at[r - 1], recv_sem=gr.at[frm],
                device_id=(frm // 4, (frm // 2) % 2, frm % 2), device_id_type=MESH).wait_recv()
        for cp in cps + gathers:
            cp.wait_send()
        mine_slot.wait()

    outs = pl.pallas_call(
        body, out_shape=[jax.ShapeDtypeStruct(a.shape, a.dtype) for a in arrs]
        + [jax.ShapeDtypeStruct((n_dev,) + last.shape, last.dtype)],
        in_specs=[ANY] * (n + 1), out_specs=[ANY] * (n + 1),
        input_output_aliases={i: i for i in range(n)},
        scratch_shapes=[pltpu.SemaphoreType.DMA((n,)), pltpu.SemaphoreType.DMA((n,)),
                        pltpu.SemaphoreType.DMA((n_dev - 1,)), pltpu.SemaphoreType.DMA((n_dev,)),
                        pltpu.SemaphoreType.DMA(())],
        name="pair_gather")(*arrs, last)
    return outs[:n], outs[n]


def _adamw_math(w, g, m, v):
    m2 = ADAM_B1 * m + (1.0 - ADAM_B1) * g
    v2 = ADAM_B2 * v + (1.0 - ADAM_B2) * (g * g)
    m_hat = m2 / (1.0 - ADAM_B1 ** ADAM_STEP)
    v_hat = v2 / (1.0 - ADAM_B2 ** ADAM_STEP)
    delta = -ADAM_LR * (m_hat / (jnp.sqrt(v_hat) + ADAM_EPS) + ADAM_WD * w)
    return delta, m2, v2


def _adamw_big(w, g, m, v, name, steps=8):
    r, c = w.shape

    def body(w_ref, g_ref, m_ref, v_ref, d_ref, m2_ref, v2_ref):
        d_ref[...], m2_ref[...], v2_ref[...] = _adamw_math(w_ref[...], g_ref[...], m_ref[...], v_ref[...])

    spec = pl.BlockSpec((r // steps, c), lambda i: (i, 0))
    return pl.pallas_call(
        body, grid=(steps,), in_specs=[spec] * 4, out_specs=[spec] * 3,
        out_shape=[jax.ShapeDtypeStruct((r, c), F32)] * 3,
        compiler_params=_cp(("parallel",), VMEM_LIMIT), name=name)(w, g, m, v)


def _adamw_small(groups, slots):
    n = len(groups)

    def body(*refs):
        slots_ref, g0_ref = refs[4 * n], refs[4 * n + 1 + 3 * n]
        for i in range(n):
            w_ref, g_ref, m_ref, v_ref = refs[4 * i:4 * i + 4]
            d_ref, m2_ref, v2_ref = refs[4 * n + 1 + 3 * i:4 * n + 1 + 3 * i + 3]
            g = g_ref[...]
            if i == 0:
                for dev in range(slots.shape[0]):
                    g = g + slots_ref[dev]
                g0_ref[...] = g
            d_ref[...], m2_ref[...], v2_ref[...] = _adamw_math(w_ref[...], g, m_ref[...], v_ref[...])

    flat = [a for grp in groups for a in grp]
    out_shape = [jax.ShapeDtypeStruct(grp[0].shape, F32) for grp in groups for _ in range(3)]
    out_shape.append(jax.ShapeDtypeStruct(groups[0][0].shape, F32))
    outs = pl.pallas_call(body, out_shape=out_shape, name="adamw_small")(*flat, slots)
    return [tuple(outs[3 * i:3 * i + 3]) for i in range(n)], outs[3 * n]


def kernel(x, mem, norm_mix_g, w_in, conv_w, gm_ln_g, gm_ln_b, gm_ws, gm_bs, w_out, norm_x_g, norm_mem_g, w_q, w_kv, w_xo, norm_final_g, loss_target, m_norm_mix_g, m_w_in, m_conv_w, m_gm_ln_g, m_gm_ln_b, m_gm_ws, m_gm_bs, m_w_out, m_norm_x_g, m_norm_mem_g, m_w_q, m_w_kv, m_w_xo, m_norm_final_g, v_norm_mix_g, v_w_in, v_conv_w, v_gm_ln_g, v_gm_ln_b, v_gm_ws, v_gm_bs, v_w_out, v_norm_x_g, v_norm_mem_g, v_w_q, v_w_kv, v_w_xo, v_norm_final_g):
    t = x.shape[1]
    xi = lax.axis_index("x")
    yi = lax.axis_index("y")
    ci = lax.axis_index("c")
    b_idx = jnp.reshape(2 * xi + yi, (1,)).astype(jnp.int32)
    c_idx = jnp.reshape(ci, (1,)).astype(jnp.int32)

    x2d, mem2d, tgt = x[0], mem[0], loss_target[0]
    big = [w_in[0], w_out[0], w_q[0], w_kv[0], w_xo[0]]
    big_m = [m_w_in[0], m_w_out[0], m_w_q[0], m_w_kv[0], m_w_xo[0]]
    big_v = [v_w_in[0], v_w_out[0], v_w_q[0], v_w_kv[0], v_w_xo[0]]
    g3 = norm_final_g.reshape(1, D)

    def pad8(a):
        return jnp.pad(a, ((0, 8 - a.shape[0]), (0, 0)))

    own_blocks = _cast_shards(b_idx, big)

    tril = jnp.tril(jnp.ones((CH, CH), bool))
    wc32 = jnp.where(tril[None], gm_ws[0], 0.0)
    wc = wc32.astype(BF16)
    wct = jnp.swapaxes(wc32, 1, 2).astype(BF16)
    bsb = jnp.broadcast_to(gm_bs[0][:, :, None], (HEADS, CH, CH))

    blk = 2 * xi + yi
    order = jnp.stack([blk, blk ^ 2, blk ^ 1, blk ^ 3]).astype(jnp.int32)
    proj, ht, win_f, cw8 = _proj_gather(order, x2d, norm_mix_g, own_blocks[0], pad8(conv_w[0]))
    mixin, mixt, (wout_f, wq_f, wkv_f, wxo_f) = _mixer_fwd(proj, cw8, gm_ln_g, gm_ln_b, wc, bsb, own_blocks[1:])
    wout2, wq2, wxo2 = wout_f.reshape(MIX, D), wq_f.reshape(D, D), wxo_f.reshape(D, D)
    k, v, mt = _mem_fwd(mem2d, norm_mem_g, wkv_f)
    del mt

    (loss_tile, dmix, dx1b, h2t, dq, ot, dx2b, dk, dv, dg2, dg3) = _tail(
        x2d, tgt, mixin, wout2, wq2, wxo2, k, v, norm_x_g, g3)
    dwkv, dwkv_b, dgm = _mem_bwd(mem2d, norm_mem_g, dk, dv, wkv_f)
    dwxo, dwxo_b = _grad_matmul(ot, dx2b, by_cols=False, name="grad_w_xo")
    dwq, dwq_b = _grad_matmul(h2t, dq, by_cols=False, name="grad_w_q")
    dwout, dwout_b = _grad_matmul(mixt, dx1b, by_cols=False, name="grad_w_out")

    rx1a = _pair_exchange([dwout_b, dwq_b, dwkv_b, dwxo_b], [], "pair_exchange_a")
    ps_a = _pair_sum(c_idx, [dwout, dwq, dwkv, dwxo], rx1a, [], [], "pair_sum_a")
    sums_a, sums_a_b = ps_a[:4], ps_a[4:8]
    (dproj, dcw, dlng, dlnb, dwc, dbs8), rx2a = _mixer_bwd(proj, dmix, cw8, gm_ln_g, gm_ln_b, wc, wct, bsb, sums_a_b)

    dwin, dwin_b = _grad_matmul(ht, dproj, by_cols=True, name="grad_w_in")
    zero = jnp.zeros((1, D), F32)
    loss_row = jnp.broadcast_to(loss_tile[0:1, 0:1], (1, D))
    sv = jnp.concatenate([zero, dg2, dgm, dg3, dlng, dlnb, dbs8[0:1], loss_row, dcw], axis=0)
    sw = dwc.reshape(HEADS * CH, CH)
    rx1b = _pair_exchange([dwin_b], [sv, sw], "pair_exchange_b")
    ps_b = _pair_sum(c_idx, [dwin], rx1b[:1], [sv, sw], rx1b[1:], "pair_sum_b")
    sums_b, sums_b_b, psmall = ps_b[:1], ps_b[1:2], ps_b[2:]
    grad_x, dg1, rx2b = _input_grad(dproj, win_f, x2d, dx1b, norm_mix_g, sums_b_b, psmall)

    red = _chip_sum(jnp.concatenate([b_idx, c_idx]), sums_b + sums_a, rx2b[:1] + rx2a, rx2b[1:])
    (gwin, gwout, gwq, gwkv, gwxo, svf, swf), dg1_slots = _pair_gather(red, dg1)

    names = ["w_in", "w_out", "w_q", "w_kv", "w_xo"]
    big_out = [_adamw_big(w, g, m, v, "adamw_" + nm)
               for w, g, m, v, nm in zip(big, [gwin, gwout, gwq, gwkv, gwxo], big_m, big_v, names)]

    def vec_pack(a1, a2, am, a3, lg, lb, bs):
        return jnp.concatenate([a1, a2, am, a3.reshape(1, D), lg, lb, bs.reshape(1, D), zero], axis=0)

    wv = vec_pack(norm_mix_g, norm_x_g, norm_mem_g, norm_final_g, gm_ln_g, gm_ln_b, gm_bs)
    mv = vec_pack(m_norm_mix_g, m_norm_x_g, m_norm_mem_g, m_norm_final_g, m_gm_ln_g, m_gm_ln_b, m_gm_bs)
    vv = vec_pack(v_norm_mix_g, v_norm_x_g, v_norm_mem_g, v_norm_final_g, v_gm_ln_g, v_gm_ln_b, v_gm_bs)
    loss = svf[7, 0]
    gcw = lax.dynamic_slice_in_dim(svf[8:16], blk * (D // N_CHIP), D // N_CHIP, axis=1)
    gws = swf
    ((dv_, mv_, vv_), (dc_, mc_, vc_), (dws_, mws_, vws_)), gv = _adamw_small([
        (wv, svf[0:8], mv, vv),
        (pad8(conv_w[0]), gcw, pad8(m_conv_w[0]), pad8(v_conv_w[0])),
        (gm_ws.reshape(HEADS * CH, CH), gws, m_gm_ws.reshape(HEADS * CH, CH), v_gm_ws.reshape(HEADS * CH, CH))],
        dg1_slots)

    def unpack(vecs, cw, ws, bigs):
        r = lambda i: vecs[i:i + 1]
        return [r(0), bigs[0][None], cw[0:3][None], r(4), r(5), ws.reshape(1, HEADS, CH, CH), vecs[6].reshape(1, HEADS, CH),
                bigs[1][None], r(1), r(2), bigs[2][None], bigs[3][None], bigs[4][None], vecs[3]]

    grads_out = unpack(gv, gcw, gws, [gwin, gwout, gwq, gwkv, gwxo])
    delta_out = unpack(dv_, dc_, dws_, [o[0] for o in big_out])
    m_out = unpack(mv_, mc_, mws_, [o[1] for o in big_out])
    v_out = unpack(vv_, vc_, vws_, [o[2] for o in big_out])
    return (loss, grad_x[None], *grads_out, *delta_out, *m_out, *v_out)
```

```python
import functools
import math

import jax
import jax.numpy as jnp
from jax import lax
from jax.experimental import pallas as pl
from jax.experimental.pallas import tpu as pltpu

F32 = jnp.float32
BF16 = jnp.bfloat16
MESH = pl.DeviceIdType.MESH

D = 1024
SLAB = 1024
N_SLAB = 7
IN_DIM = N_SLAB * SLAB
MIX = 2 * SLAB
HEADS = 8
CH = 128
XH = 4
XD = D // XH
EPS = 1e-6
GELU_C = math.sqrt(2.0 / math.pi)
GELU_A = 0.044715
N_CHIP = 4
IN_BLK = IN_DIM // N_CHIP
KV_BLK = 2 * D // N_CHIP

ADAM_LR, ADAM_B1, ADAM_B2, ADAM_EPS, ADAM_WD, ADAM_STEP = 0.001, 0.9, 0.999, 1e-08, 0.01, 10

VMEM_LIMIT = 60 * 1024 * 1024


def _cp(sem=None, vmem=None):
    return pltpu.CompilerParams(dimension_semantics=sem, vmem_limit_bytes=vmem)


def _full(shape, buffers=None):
    n = len(shape)
    if buffers is None:
        return pl.BlockSpec(shape, lambda *_: (0,) * n)
    return pl.BlockSpec(shape, lambda *_: (0,) * n, pipeline_mode=pl.Buffered(buffers))


ANY = pl.BlockSpec(memory_space=pl.ANY)


def _bdot(a, b):
    return jnp.dot(a.astype(BF16), b.astype(BF16), preferred_element_type=F32)


def _bdot_nt(a, b):
    return lax.dot_general(a.astype(BF16), b.astype(BF16), (((1,), (1,)), ((), ())), preferred_element_type=F32)


def _rms(x, g):
    r = lax.rsqrt(jnp.mean(x * x, axis=-1, keepdims=True) + EPS)
    return x * r * g, r


def _rms_bwd(dy, x, r, g):
    gdy = dy * g
    dx = r * gdy - x * (r * r * r) * jnp.mean(x * gdy, axis=-1, keepdims=True)
    dg = jnp.sum(dy * x * r, axis=0, keepdims=True)
    return dx, dg


def _gelu_parts(x):
    x2 = x * x
    t = jnp.tanh(GELU_C * (x + GELU_A * x * x2))
    val = 0.5 * x * (1.0 + t)
    grad = 0.5 * (1.0 + t) + 0.5 * x * (1.0 - t * t) * (GELU_C * (1.0 + 3.0 * GELU_A * x2))
    return val, grad


def _gelu(x):
    return 0.5 * x * (1.0 + jnp.tanh(GELU_C * (x + GELU_A * x * x * x)))


def _sigmoid(z):
    return 1.0 / (1.0 + jnp.exp(-z))


def _cast_shards(b_idx, arrs):
    n = len(arrs)
    steps = 8

    def body(b_ref, *refs):
        for i in range(n):
            refs[n + i][...] = refs[i][...].astype(BF16)

    in_specs = [pl.BlockSpec((a.shape[0] // steps, a.shape[1]), lambda i, b: (i, 0)) for a in arrs]
    out_specs = [pl.BlockSpec((None, a.shape[0] // steps, a.shape[1]), lambda i, b: (b[0], i, 0)) for a in arrs]
    return pl.pallas_call(
        body, out_shape=[jax.ShapeDtypeStruct((N_CHIP,) + a.shape, BF16) for a in arrs],
        grid_spec=pltpu.PrefetchScalarGridSpec(num_scalar_prefetch=1, grid=(steps,), in_specs=in_specs, out_specs=out_specs),
        compiler_params=_cp(("arbitrary",)), name="cast_shards")(b_idx, *arrs)


def _proj_gather(order, x, g, win_own, cw8s, more, tm=1024):
    t = x.shape[0]
    ni = t // tm
    hr = D // 2
    nm = len(more)

    def body(*refs):
        order_ref, x_ref, g_ref, win_in, cw_in = refs[:5]
        o_ref, ht_ref, win_f, cw_out = refs[5 + nm:9 + nm]
        more_out = refs[9 + nm:9 + 2 * nm]
        hbuf, wv, ici_s, ici_r, d2d_s, d2d_r, cw_s, cw_r, loc, m_is, m_ir, m_ds, m_dr = refs[9 + 2 * nm:]
        more_start, more_finish = _gather_steps(more_out, m_is, m_ir, m_ds, m_dr)
        j, i = pl.program_id(0), pl.program_id(1)
        x, y, c, chips = _coords()
        b = 2 * x + y
        sib = (x, y, 1 - c)
        blks = [2 * chip[0] + chip[1] for chip in chips]

        def half(blk, hc):
            return win_f.at[blk, pl.ds(hc * hr, hr)]

        def cw_cols(blk):
            return cw_out.at[:, pl.ds(blk * (D // N_CHIP), D // N_CHIP)]

        def ici(k, blk):
            return pltpu.make_async_remote_copy(src_ref=half(blk, c), dst_ref=half(blk, c), send_sem=ici_s.at[k],
                                                recv_sem=ici_r.at[k], device_id=(*chips[k], c), device_id_type=MESH)

        def relay(k, blk, hc):
            return pltpu.make_async_remote_copy(src_ref=half(blk, hc), dst_ref=half(blk, hc), send_sem=d2d_s.at[k],
                                                recv_sem=d2d_r.at[k], device_id=sib, device_id_type=MESH)

        def cw_copy(k, blk):
            src = cw_in if blk is None else cw_cols(blk)
            return pltpu.make_async_remote_copy(src_ref=src, dst_ref=cw_cols(b if blk is None else blk), send_sem=cw_s.at[k],
                                                recv_sem=cw_r.at[k], device_id=(*chips[k], c), device_id_type=MESH)

        cw_local = pltpu.make_async_copy(cw_in, cw_cols(b), loc.at[1])

        @pl.when((j == 0) & (i == 0))
        def _():
            cw_local.start()
            for k in range(3):
                ici(k, b).start()
            for k in range(3):
                cw_copy(k, None).start()
            more_start()

        for jj in range(N_CHIP):
            @pl.when((j == jj) & (i == 0))
            def _(jj=jj):
                if jj == 0:
                    blk = b
                else:
                    blk = blks[jj - 1]
                    ici(jj - 1, blk).wait_recv()
                    relay(jj - 1, blk, c).start()
                    relay(jj - 1, blk, 1 - c).wait_recv()
                load = pltpu.make_async_copy(win_f.at[blk], wv, loc.at[0])
                load.start()
                load.wait()

        rows = pl.ds(pl.multiple_of(i * tm, tm), tm)

        @pl.when(j == 0)
        def _():
            h, _ = _rms(x_ref[...], g_ref[...])
            hbuf[rows, :] = h.astype(BF16)
            ht_ref[...] = h.T.astype(BF16)

        o_ref[...] = jnp.dot(hbuf[rows, :], wv[...], preferred_element_type=F32).astype(BF16)

        @pl.when((j == N_CHIP - 1) & (i == ni - 1))
        def _():
            for k in range(3):
                cw_copy(k, blks[k]).wait_recv()
            for k in range(3):
                ici(k, b).wait_send()
                relay(k, blks[k], c).wait_send()
                cw_copy(k, None).wait_send()
            cw_local.wait()
            more_finish()

    first = lambda j, i: jnp.where(j == 0, i, ni - 1)
    in_specs = [pl.BlockSpec((tm, D), lambda j, i, o: (first(j, i), 0)), pl.BlockSpec((1, D), lambda j, i, o: (0, 0)),
                ANY, ANY] + [ANY] * nm
    out_specs = [pl.BlockSpec((tm, IN_BLK), lambda j, i, o: (i, o[j])),
                 pl.BlockSpec((D, tm), lambda j, i, o: (0, first(j, i))), ANY, ANY] + [ANY] * nm
    outs = pl.pallas_call(
        body, out_shape=[jax.ShapeDtypeStruct((t, IN_DIM), BF16), jax.ShapeDtypeStruct((D, t), BF16),
                         jax.ShapeDtypeStruct(win_own.shape, BF16), jax.ShapeDtypeStruct((8, D), F32)]
        + [jax.ShapeDtypeStruct(f.shape, f.dtype) for f in more],
        grid_spec=pltpu.PrefetchScalarGridSpec(
            num_scalar_prefetch=1, grid=(N_CHIP, ni), in_specs=in_specs, out_specs=out_specs,
            scratch_shapes=[pltpu.VMEM((t, D), BF16), pltpu.VMEM((D, IN_BLK), BF16)]
            + [pltpu.SemaphoreType.DMA((3,))] * 6 + [pltpu.SemaphoreType.DMA((2,))]
            + [pltpu.SemaphoreType.DMA((max(nm, 1), 3))] * 4),
        input_output_aliases={3: 2, **{5 + w: 4 + w for w in range(nm)}},
        compiler_params=_cp(("arbitrary", "arbitrary"), VMEM_LIMIT), name="proj_gather")(order, x, g, win_own, cw8s, *more)
    return outs[0], outs[1], outs[2], outs[3], outs[4:]


def _gather_steps(outs, ici_s, ici_r, d2d_s, d2d_r):
    x, y, c, chips = _coords()
    b = 2 * x + y
    sib = (x, y, 1 - c)
    nw = len(outs)

    def half(w, blk, hc):
        hr = outs[w].shape[1] // 2
        return outs[w].at[blk, pl.ds(hc * hr, hr)]

    def ici(w, k, blk):
        return pltpu.make_async_remote_copy(src_ref=half(w, blk, c), dst_ref=half(w, blk, c), send_sem=ici_s.at[w, k],
                                            recv_sem=ici_r.at[w, k], device_id=(*chips[k], c), device_id_type=MESH)

    def relay(w, k, blk, hc):
        return pltpu.make_async_remote_copy(src_ref=half(w, blk, hc), dst_ref=half(w, blk, hc), send_sem=d2d_s.at[w, k],
                                            recv_sem=d2d_r.at[w, k], device_id=sib, device_id_type=MESH)

    def start():
        for w in range(nw):
            for k in range(3):
                ici(w, k, b).start()

    def finish():
        for w in range(nw):
            for k in range(3):
                blk = 2 * chips[k][0] + chips[k][1]
                ici(w, k, blk).wait_recv()
                relay(w, k, blk, c).start()
        for w in range(nw):
            for k in range(3):
                blk = 2 * chips[k][0] + chips[k][1]
                relay(w, k, blk, 1 - c).wait_recv()
        for w in range(nw):
            for k in range(3):
                blk = 2 * chips[k][0] + chips[k][1]
                ici(w, k, b).wait_send()
                relay(w, k, blk, c).wait_send()

    return start, finish


def _mixer_fwd(proj, cw8, lng, lnb, wc, bsb, fulls, tm=256):
    t = proj.shape[0]
    nt = t // tm
    nch = tm // CH
    nw = len(fulls)

    def body(*refs):
        p_ref, cw_ref, lng_ref, lnb_ref, wc_ref, bsb_ref = refs[:6]
        mix_ref, mixt_ref = refs[6 + nw:8 + nw]
        w_outs = refs[8 + nw:8 + 2 * nw]
        prev_ref, stage_ref, ici_s, ici_r, d2d_s, d2d_r = refs[8 + 2 * nw:]
        gather_start, gather_finish = _gather_steps(w_outs, ici_s, ici_r, d2d_s, d2d_r)

        @pl.when(pl.program_id(0) == 0)
        def _():
            gather_start()
            prev_ref[...] = jnp.zeros_like(prev_ref)

        rows = lax.broadcasted_iota(jnp.int32, (tm, CH), 0)
        for s in range(HEADS):
            cs = pl.ds(CH * s, CH)

            def slab(k):
                return p_ref[:, pl.ds(k * SLAB + CH * s, CH)].astype(F32)

            gb, gc, xa, za = slab(0), slab(1), slab(2), slab(3)
            cx = gc * xa
            p6 = jnp.broadcast_to(prev_ref[6:7, cs], (tm, CH))
            p7 = jnp.broadcast_to(prev_ref[7:8, cs], (tm, CH))
            c1 = jnp.where(rows == 0, p7, pltpu.roll(cx, 1, 0))
            c2 = jnp.where(rows == 0, p6, jnp.where(rows == 1, p7, pltpu.roll(cx, 2, 0)))
            prev_ref[:, cs] = cx[tm - 8:, :]
            cv = cw_ref[0:1, cs] * c2 + cw_ref[1:2, cs] * c1 + cw_ref[2:3, cs] * cx
            stage_ref[:, cs] = gb * cv * (za * _sigmoid(za))

            u, v, zb = slab(4), slab(5), slab(6)
            ug, vg = _gelu(u), _gelu(v)
            dlt = vg - jnp.mean(vg, axis=-1, keepdims=True)
            vhat = dlt * lax.rsqrt(jnp.mean(dlt * dlt, axis=-1, keepdims=True) + EPS)
            vn = (vhat * lng_ref[:, cs] + lnb_ref[:, cs]).astype(BF16)
            gate = ug * (zb * _sigmoid(zb))
            for c in range(nch):
                rs = slice(CH * c, CH * (c + 1))
                sp = jnp.dot(wc_ref[s], vn[rs], preferred_element_type=F32) + bsb_ref[s]
                stage_ref[rs, pl.ds(SLAB + CH * s, CH)] = gate[rs] * sp

        full = stage_ref[...]
        mix_ref[...] = full.astype(BF16)
        mixt_ref[...] = full.T.astype(BF16)

        @pl.when(pl.program_id(0) == nt - 1)
        def _():
            gather_finish()

    sems = [pltpu.SemaphoreType.DMA((nw, 3))] * 4
    outs = pl.pallas_call(
        body, grid=(nt,),
        in_specs=[pl.BlockSpec((tm, IN_DIM), lambda i: (i, 0)), _full((8, D)), _full((1, D)), _full((1, D)),
                  _full((HEADS, CH, CH)), _full((HEADS, CH, CH))] + [ANY] * nw,
        out_specs=[pl.BlockSpec((tm, MIX), lambda i: (i, 0)), pl.BlockSpec((MIX, tm), lambda i: (0, i))] + [ANY] * nw,
        out_shape=[jax.ShapeDtypeStruct((t, MIX), BF16), jax.ShapeDtypeStruct((MIX, t), BF16)]
        + [jax.ShapeDtypeStruct(f.shape, f.dtype) for f in fulls],
        input_output_aliases={6 + w: 2 + w for w in range(nw)},
        scratch_shapes=[pltpu.VMEM((8, D), F32), pltpu.VMEM((tm, MIX), F32)] + sems,
        compiler_params=_cp(("arbitrary",), VMEM_LIMIT), name="mixer_fwd")(proj, cw8, lng, lnb, wc, bsb, *fulls)
    return outs[0], outs[1], outs[2:]


def _mem_fwd(mem, gm, wkv_f):
    n_mem = mem.shape[0]

    def body(mem_ref, gm_ref, w_ref, k_ref, v_ref, mt_ref):
        m, _ = _rms(mem_ref[...], gm_ref[...])
        mb = m.astype(BF16)
        mt_ref[...] = m.T.astype(BF16)
        for j in range(N_CHIP):
            dst = k_ref if j < 2 else v_ref
            dst[:, pl.ds(KV_BLK * (j % 2), KV_BLK)] = jnp.dot(mb, w_ref[j], preferred_element_type=F32).astype(BF16)

    return pl.pallas_call(
        body, out_shape=[jax.ShapeDtypeStruct((n_mem, D), BF16), jax.ShapeDtypeStruct((n_mem, D), BF16),
                         jax.ShapeDtypeStruct((D, n_mem), BF16)],
        compiler_params=_cp(None, VMEM_LIMIT), name="mem_fwd")(mem, gm, wkv_f)


def _tail(x, tgt, mixin, wout, wq, wxo, k, v, g2, g3, tm=512, sub=512):
    t = x.shape[0]
    n_mem = k.shape[0]
    scale = 1.0 / math.sqrt(XD)

    def body(x_ref, tgt_ref, mix_ref, wout_ref, wq_ref, wxo_ref, k_ref, v_ref, g2_ref, g3_ref,
             loss_ref, dmix_ref, dx1b_ref, h2t_ref, dq_ref, ot_ref, dx2b_ref, dk_ref, dv_ref, dg2_ref, dg3_ref):
        @pl.when(pl.program_id(0) == 0)
        def _():
            loss_ref[...] = jnp.zeros_like(loss_ref)
            dk_ref[...] = jnp.zeros_like(dk_ref)
            dv_ref[...] = jnp.zeros_like(dv_ref)
            dg2_ref[...] = jnp.zeros_like(dg2_ref)
            dg3_ref[...] = jnp.zeros_like(dg3_ref)

        g2, g3 = g2_ref[...], g3_ref[...]
        for sb in range(tm // sub):
            rs = pl.ds(sub * sb, sub)
            x1 = x_ref[rs, :] + jnp.dot(mix_ref[rs, :], wout_ref[...], preferred_element_type=F32)
            h2, r2 = _rms(x1, g2)
            h2t_ref[:, rs] = h2.T.astype(BF16)
            q = _bdot(h2, wq_ref[...]).astype(BF16)
            probs, outs = [], []
            for hd in range(XH):
                hs = pl.ds(XD * hd, XD)
                s = _bdot_nt(q[:, XD * hd:XD * (hd + 1)], k_ref[:, hs]) * scale
                e = jnp.exp(s - jnp.max(s, axis=-1, keepdims=True))
                p = e / jnp.sum(e, axis=-1, keepdims=True)
                probs.append(p)
                outs.append(_bdot(p, v_ref[:, hs]))
            o = jnp.concatenate(outs, axis=-1)
            ot_ref[:, rs] = o.T.astype(BF16)
            x2 = x1 + _bdot(o, wxo_ref[...])
            y, r3 = _rms(x2, g3)
            diff = y - tgt_ref[rs, :]
            row_loss = jnp.sum(diff * diff, axis=-1, keepdims=True)
            loss_ref[...] += jnp.broadcast_to(jnp.sum(row_loss, axis=0, keepdims=True) * (0.5 / D), loss_ref.shape)

            dx2, dg3 = _rms_bwd(diff * (1.0 / D), x2, r3, g3)
            dg3_ref[...] += dg3
            dx2b = dx2.astype(BF16)
            dx2b_ref[rs, :] = dx2b
            do = _bdot_nt(dx2b, wxo_ref[...])
            dqs = []
            for hd in range(XH):
                hs = pl.ds(XD * hd, XD)
                p = probs[hd]
                do_h = do[:, XD * hd:XD * (hd + 1)]
                dv_ref[:, hs] += _bdot(p.T, do_h)
                dp = _bdot_nt(do_h, v_ref[:, hs])
                ds = p * (dp - jnp.sum(dp * p, axis=-1, keepdims=True))
                dqs.append(_bdot(ds, k_ref[:, hs]) * scale)
                dk_ref[:, hs] += _bdot(ds.T, q[:, XD * hd:XD * (hd + 1)]) * scale
            dq = jnp.concatenate(dqs, axis=-1).astype(BF16)
            dq_ref[rs, :] = dq
            dx1n, dg2 = _rms_bwd(_bdot_nt(dq, wq_ref[...]), x1, r2, g2)
            dg2_ref[...] += dg2
            dx1b = (dx2 + dx1n).astype(BF16)
            dx1b_ref[rs, :] = dx1b
            dmix_ref[rs, :] = _bdot_nt(dx1b, wout_ref[...]).astype(BF16)

    tok = lambda w: pl.BlockSpec((tm, w), lambda i: (i, 0))
    tok_t = lambda w: pl.BlockSpec((w, tm), lambda i: (0, i))
    return pl.pallas_call(
        body, grid=(t // tm,),
        in_specs=[tok(D), tok(D), tok(MIX), _full((MIX, D), 1), _full((D, D), 1), _full((D, D), 1),
                  _full((n_mem, D), 1), _full((n_mem, D), 1), _full((1, D)), _full((1, D))],
        out_specs=[_full((8, 128)), tok(MIX), tok(D), tok_t(D), tok(D), tok_t(D), tok(D),
                   _full((n_mem, D)), _full((n_mem, D)), _full((1, D)), _full((1, D))],
        out_shape=[jax.ShapeDtypeStruct((8, 128), F32), jax.ShapeDtypeStruct((t, MIX), BF16),
                   jax.ShapeDtypeStruct((t, D), BF16),
                   jax.ShapeDtypeStruct((D, t), BF16), jax.ShapeDtypeStruct((t, D), BF16),
                   jax.ShapeDtypeStruct((D, t), BF16), jax.ShapeDtypeStruct((t, D), BF16),
                   jax.ShapeDtypeStruct((n_mem, D), F32), jax.ShapeDtypeStruct((n_mem, D), F32),
                   jax.ShapeDtypeStruct((1, D), F32), jax.ShapeDtypeStruct((1, D), F32)],
        compiler_params=_cp(("arbitrary",), VMEM_LIMIT), name="tail")(x, tgt, mixin, wout, wq, wxo, k, v, g2, g3)


def _mem_bwd(mem, gm, dk, dv, wkv_f):
    def body(mem_ref, gm_ref, dk_ref, dv_ref, w_ref, dw_ref, dwb_ref, dgm_ref):
        mem_v = mem_ref[...]
        m, rm = _rms(mem_v, gm_ref[...])
        mt = m.T.astype(BF16)
        dm = jnp.zeros_like(mem_v)
        for j in range(N_CHIP):
            src = dk_ref if j < 2 else dv_ref
            dkv = src[:, pl.ds(KV_BLK * (j % 2), KV_BLK)].astype(BF16)
            dw = jnp.dot(mt, dkv, preferred_element_type=F32)
            dw_ref[j] = dw
            dwb_ref[j] = dw.astype(BF16)
            dm = dm + _bdot_nt(dkv, w_ref[j])
        dgm_ref[...] = jnp.sum(dm * mem_v * rm, axis=0, keepdims=True)

    return pl.pallas_call(
        body, out_shape=[jax.ShapeDtypeStruct((N_CHIP, D, KV_BLK), F32), jax.ShapeDtypeStruct((N_CHIP, D, KV_BLK), BF16),
                         jax.ShapeDtypeStruct((1, D), F32)],
        compiler_params=_cp(None, VMEM_LIMIT), name="mem_bwd")(mem, gm, dk, dv, wkv_f)


def _mixer_bwd(proj, dmix, cw8, lng, lnb, wc, wct, bsb, xch, tm=256):
    t = proj.shape[0]
    nt = t // tm
    nch = tm // CH
    hb = 16
    nx = len(xch)
    xch_shape, xch_sems = _exchange_shapes(xch, [])

    def body(*refs):
        p_ref, pgc_ref, pxa_ref, dm_ref, cw_ref, lng_ref, lnb_ref, wc_ref, wct_ref, bsb_ref = refs[:10]
        x_in = refs[10:10 + nx]
        dp_ref, dcw_ref, dlng_ref, dlnb_ref, dwc_ref, dbs_ref = refs[10 + nx:16 + nx]
        x_out = refs[16 + nx:16 + 2 * nx]
        next_ref, send, recv, loc = refs[16 + 2 * nx:]
        exchange_start, exchange_finish = _exchange_steps(x_in, x_out, nx, send, recv, loc)
        i = pl.program_id(0)

        @pl.when(i == 0)
        def _():
            exchange_start()
            next_ref[...] = jnp.zeros_like(next_ref)
            dcw_ref[...] = jnp.zeros_like(dcw_ref)
            dlng_ref[...] = jnp.zeros_like(dlng_ref)
            dlnb_ref[...] = jnp.zeros_like(dlnb_ref)
            dwc_ref[...] = jnp.zeros_like(dwc_ref)
            dbs_ref[...] = jnp.zeros_like(dbs_ref)

        first_tile = i == nt - 1
        rows = lax.broadcasted_iota(jnp.int32, (tm, CH), 0)
        ones8 = jnp.ones((8, CH), BF16)
        for s in range(HEADS):
            cs = pl.ds(CH * s, CH)

            def slab(k):
                return p_ref[:, pl.ds(k * SLAB + CH * s, CH)].astype(F32)

            gb, gc, xa, za = slab(0), slab(1), slab(2), slab(3)
            da = dm_ref[:, cs].astype(F32)
            cx = gc * xa
            cxp = pgc_ref[:, cs].astype(F32) * pxa_ref[:, cs].astype(F32)
            cxp = jnp.where(first_tile, jnp.zeros_like(cxp), cxp)
            p6 = jnp.broadcast_to(cxp[hb - 2:hb - 1, :], (tm, CH))
            p7 = jnp.broadcast_to(cxp[hb - 1:hb, :], (tm, CH))
            c1 = jnp.where(rows == 0, p7, pltpu.roll(cx, 1, 0))
            c2 = jnp.where(rows == 0, p6, jnp.where(rows == 1, p7, pltpu.roll(cx, 2, 0)))
            w0, w1, w2 = cw_ref[0:1, cs], cw_ref[1:2, cs], cw_ref[2:3, cs]
            cv = w0 * c2 + w1 * c1 + w2 * cx
            sg = _sigmoid(za)
            sa = za * sg
            dcv = da * gb * sa
            dp_ref[:, pl.ds(0 * SLAB + CH * s, CH)] = (da * cv * sa).astype(BF16)
            dp_ref[:, pl.ds(3 * SLAB + CH * s, CH)] = (da * gb * cv * (sg * (1.0 + za * (1.0 - sg)))).astype(BF16)
            n0 = jnp.broadcast_to(next_ref[0:1, cs], (tm, CH))
            n1 = jnp.broadcast_to(next_ref[1:2, cs], (tm, CH))
            u1 = jnp.where(rows == tm - 1, n0, pltpu.roll(dcv, tm - 1, 0))
            u2 = jnp.where(rows == tm - 2, n0, jnp.where(rows == tm - 1, n1, pltpu.roll(dcv, tm - 2, 0)))
            next_ref[:, cs] = dcv[0:8, :]
            dcx = w2 * dcv + w1 * u1 + w0 * u2
            dp_ref[:, pl.ds(1 * SLAB + CH * s, CH)] = (dcx * xa).astype(BF16)
            dp_ref[:, pl.ds(2 * SLAB + CH * s, CH)] = (dcx * gc).astype(BF16)
            dcw_ref[0:1, cs] += jnp.sum(dcv * c2, axis=0, keepdims=True)
            dcw_ref[1:2, cs] += jnp.sum(dcv * c1, axis=0, keepdims=True)
            dcw_ref[2:3, cs] += jnp.sum(dcv * cx, axis=0, keepdims=True)

            u, v, zb = slab(4), slab(5), slab(6)
            db = dm_ref[:, pl.ds(SLAB + CH * s, CH)].astype(F32)
            ug, ugrad = _gelu_parts(u)
            vg, vgrad = _gelu_parts(v)
            dlt = vg - jnp.mean(vg, axis=-1, keepdims=True)
            rstd = lax.rsqrt(jnp.mean(dlt * dlt, axis=-1, keepdims=True) + EPS)
            vhat = dlt * rstd
            lg = lng_ref[:, cs]
            vn = (vhat * lg + lnb_ref[:, cs]).astype(BF16)
            sgb = _sigmoid(zb)
            szb = zb * sgb
            sps, dvns = [], []
            dbs = jnp.zeros((8, CH), F32)
            dwc = jnp.zeros((CH, CH), F32)
            for c in range(nch):
                rs = slice(CH * c, CH * (c + 1))
                sp = jnp.dot(wc_ref[s], vn[rs], preferred_element_type=F32) + bsb_ref[s]
                dsp = (db[rs] * ug[rs] * szb[rs]).astype(BF16)
                dbs = dbs + lax.dot_general(ones8, dsp, (((1,), (1,)), ((), ())), preferred_element_type=F32)
                dwc = dwc + lax.dot_general(dsp, vn[rs], (((1,), (1,)), ((), ())), preferred_element_type=F32)
                dvns.append(jnp.dot(wct_ref[s], dsp, preferred_element_type=F32))
                sps.append(sp)
            sp = jnp.concatenate(sps, axis=0)
            dvn = jnp.concatenate(dvns, axis=0)
            dbs_ref[:, cs] += dbs
            dwc_ref[s] += dwc
            dlng_ref[:, cs] += jnp.sum(dvn * vhat, axis=0, keepdims=True)
            dlnb_ref[:, cs] += jnp.sum(dvn, axis=0, keepdims=True)
            dvhat = dvn * lg
            dvg = rstd * (dvhat - jnp.mean(dvhat, axis=-1, keepdims=True)
                          - vhat * jnp.mean(dvhat * vhat, axis=-1, keepdims=True))
            dp_ref[:, pl.ds(4 * SLAB + CH * s, CH)] = (db * sp * szb * ugrad).astype(BF16)
            dp_ref[:, pl.ds(5 * SLAB + CH * s, CH)] = (dvg * vgrad).astype(BF16)
            dp_ref[:, pl.ds(6 * SLAB + CH * s, CH)] = (db * ug * sp * (sgb * (1.0 + zb * (1.0 - sgb)))).astype(BF16)

        @pl.when(i == nt - 1)
        def _():
            tril = lax.broadcasted_iota(jnp.int32, (CH, CH), 0) >= lax.broadcasted_iota(jnp.int32, (CH, CH), 1)
            for s in range(HEADS):
                dwc_ref[s] = jnp.where(tril, dwc_ref[s], 0.0)
            exchange_finish()

    rev = lambda i: nt - 1 - i
    halo = lambda col: pl.BlockSpec((hb, SLAB), lambda i: (jnp.maximum(rev(i) * (tm // hb) - 1, 0), col))
    outs = pl.pallas_call(
        body, grid=(nt,),
        in_specs=[pl.BlockSpec((tm, IN_DIM), lambda i: (rev(i), 0)), halo(1), halo(2),
                  pl.BlockSpec((tm, MIX), lambda i: (rev(i), 0)), _full((8, D)), _full((1, D)), _full((1, D)),
                  _full((HEADS, CH, CH)), _full((HEADS, CH, CH)), _full((HEADS, CH, CH))] + [ANY] * nx,
        out_specs=[pl.BlockSpec((tm, IN_DIM), lambda i: (rev(i), 0)), _full((8, D)), _full((1, D)), _full((1, D)),
                   _full((HEADS, CH, CH)), _full((8, D))] + [ANY] * nx,
        out_shape=[jax.ShapeDtypeStruct((t, IN_DIM), BF16), jax.ShapeDtypeStruct((8, D), F32),
                   jax.ShapeDtypeStruct((1, D), F32), jax.ShapeDtypeStruct((1, D), F32),
                   jax.ShapeDtypeStruct((HEADS, CH, CH), F32), jax.ShapeDtypeStruct((8, D), F32)] + xch_shape,
        scratch_shapes=[pltpu.VMEM((8, D), F32)] + xch_sems,
        compiler_params=_cp(("arbitrary",), VMEM_LIMIT), name="mixer_bwd")(
            proj, proj, proj, dmix, cw8, lng, lnb, wc, wct, bsb, *xch)
    return outs[:6], outs[6:]


def _grad_matmul(at, b, *, by_cols, name, tk=1024):
    m, t = at.shape
    n = b.shape[1]
    nk = t // tk
    nj = N_CHIP if by_cols else 1
    bn = n // nj

    def body(a_ref, b_ref, o_ref, ob_ref):
        kk = pl.program_id(1)
        part = jnp.dot(a_ref[...], b_ref[...], preferred_element_type=F32)

        @pl.when(kk == 0)
        def _():
            o_ref[...] = part

        @pl.when(kk > 0)
        def _():
            o_ref[...] += part

        @pl.when(kk == nk - 1)
        def _():
            ob_ref[...] = o_ref[...].astype(BF16)

    a_spec = pl.BlockSpec((m, tk), lambda j, k: (0, k))
    b_spec = pl.BlockSpec((tk, bn), lambda j, k: (k, j))
    o_spec = pl.BlockSpec((None, m, bn), lambda j, k: (j, 0, 0))
    o32, o16 = pl.pallas_call(
        body, grid=(nj, nk), in_specs=[a_spec, b_spec], out_specs=[o_spec, o_spec],
        out_shape=[jax.ShapeDtypeStruct((nj, m, bn), F32), jax.ShapeDtypeStruct((nj, m, bn), BF16)],
        compiler_params=_cp(("parallel", "arbitrary"), VMEM_LIMIT), name=name)(at, b)
    if by_cols:
        return o32, o16
    return o32.reshape(N_CHIP, m // N_CHIP, n), o16.reshape(N_CHIP, m // N_CHIP, n)


def _input_grad(dproj, win_f, x, dx1, g1, xch_big, xch_small, tm=512):
    t = x.shape[0]
    nt = t // tm
    nb, nx = len(xch_big), len(xch_big) + len(xch_small)
    xch_shape, xch_sems = _exchange_shapes(xch_big, xch_small)

    def body(*refs):
        dp_ref, w_ref, x_ref, dx1_ref, g_ref = refs[:5]
        x_in = refs[5:5 + nx]
        gx_ref, dg_ref = refs[5 + nx:7 + nx]
        x_out = refs[7 + nx:7 + 2 * nx]
        send, recv, loc = refs[7 + 2 * nx:]
        exchange_start, exchange_finish = _exchange_steps(x_in, x_out, nb, send, recv, loc)

        @pl.when(pl.program_id(0) == 0)
        def _():
            exchange_start()
            dg_ref[...] = jnp.zeros_like(dg_ref)

        dh = _bdot_nt(dp_ref[:, pl.ds(0, IN_BLK)], w_ref[0])
        for j in range(1, N_CHIP):
            dh = dh + _bdot_nt(dp_ref[:, pl.ds(IN_BLK * j, IN_BLK)], w_ref[j])
        xv = x_ref[...]
        r = lax.rsqrt(jnp.mean(xv * xv, axis=-1, keepdims=True) + EPS)
        dxn, dg = _rms_bwd(dh, xv, r, g_ref[...])
        gx_ref[...] = dx1_ref[...].astype(F32) + dxn
        dg_ref[0:1, :] += dg

        @pl.when(pl.program_id(0) == nt - 1)
        def _():
            exchange_finish()

    tok = lambda w: pl.BlockSpec((tm, w), lambda i: (i, 0))
    outs = pl.pallas_call(
        body, grid=(nt,),
        in_specs=[tok(IN_DIM), _full((N_CHIP, D, IN_BLK), 1), tok(D), tok(D), _full((1, D))] + [ANY] * nx,
        out_specs=[tok(D), _full((8, D))] + [ANY] * nx,
        out_shape=[jax.ShapeDtypeStruct((t, D), F32), jax.ShapeDtypeStruct((8, D), F32)] + xch_shape,
        scratch_shapes=xch_sems,
        compiler_params=_cp(("arbitrary",), VMEM_LIMIT), name="input_grad")(dproj, win_f, x, dx1, g1, *xch_big, *xch_small)
    return outs[0], outs[1], outs[2:]


def _coords():
    x, y, c = lax.axis_index("x"), lax.axis_index("y"), lax.axis_index("c")
    chips = [(1 - x, y), (x, 1 - y), (1 - x, 1 - y)]
    return x, y, c, chips


def _pair_exchange(grads_b, smalls, name):
    ng, ns = len(grads_b), len(smalls)
    n = ng + ns

    def body(*refs):
        ins, outs, send, recv = refs[:n], refs[n:2 * n], refs[2 * n], refs[2 * n + 1]
        x, y, c, _ = _coords()
        cps = []
        for i in range(n):
            if i < ng:
                hr = ins[i].shape[1] // 2
                src = ins[i].at[pl.ds(0, N_CHIP), pl.ds((1 - c) * hr, hr)]
            else:
                hr = ins[i].shape[0] // 2
                src = ins[i].at[pl.ds((1 - c) * hr, hr)]
            cps.append(pltpu.make_async_remote_copy(
                src_ref=src, dst_ref=outs[i], send_sem=send.at[i], recv_sem=recv.at[i],
                device_id=(x, y, 1 - c), device_id_type=MESH))
        for cp in cps:
            cp.start()
        for cp in cps:
            cp.wait()

    out_shape = [jax.ShapeDtypeStruct((N_CHIP, g.shape[1] // 2, g.shape[2]), g.dtype) for g in grads_b]
    out_shape += [jax.ShapeDtypeStruct((s.shape[0] // 2, s.shape[1]), s.dtype) for s in smalls]
    return pl.pallas_call(
        body, out_shape=out_shape, in_specs=[ANY] * n, out_specs=[ANY] * n,
        scratch_shapes=[pltpu.SemaphoreType.DMA((n,)), pltpu.SemaphoreType.DMA((n,))],
        name=name)(*grads_b, *smalls)


def _pair_sum(c_idx, grads, recvd, smalls, smalls_recvd, name):
    ng, ns = len(grads), len(smalls)
    halves = [g.shape[1] // 2 for g in grads]

    def body(c_ref, *refs):
        g_in, r_in = refs[:ng], refs[ng:2 * ng]
        s_in, sr_in = refs[2 * ng:2 * ng + ns], refs[2 * ng + ns:2 * ng + 2 * ns]
        o = refs[2 * ng + 2 * ns:]
        for i in range(ng):
            tot = g_in[i][...] + r_in[i][...].astype(F32)
            o[i][...] = tot
            o[ng + i][...] = tot.astype(BF16)
        for i in range(ns):
            o[2 * ng + i][...] = s_in[i][...] + sr_in[i][...]

    in_specs = [pl.BlockSpec((None, None, halves[i], g.shape[2]), lambda b, c: (b, c[0], 0, 0)) for i, g in enumerate(grads)]
    in_specs += [pl.BlockSpec((None, halves[i], g.shape[2]), lambda b, c: (b, 0, 0)) for i, g in enumerate(grads)]
    in_specs += [pl.BlockSpec((None, s.shape[0] // 2, s.shape[1]), lambda b, c: (c[0], 0, 0)) for s in smalls]
    in_specs += [pl.BlockSpec((s.shape[0] // 2, s.shape[1]), lambda b, c: (0, 0)) for s in smalls]
    blk = [pl.BlockSpec((None, halves[i], g.shape[2]), lambda b, c: (b, 0, 0)) for i, g in enumerate(grads)]
    out_specs = blk + blk + [pl.BlockSpec((s.shape[0] // 2, s.shape[1]), lambda b, c: (0, 0)) for s in smalls]
    out_shape = [jax.ShapeDtypeStruct((N_CHIP, halves[i], g.shape[2]), F32) for i, g in enumerate(grads)]
    out_shape += [jax.ShapeDtypeStruct((N_CHIP, halves[i], g.shape[2]), BF16) for i, g in enumerate(grads)]
    out_shape += [jax.ShapeDtypeStruct((s.shape[0] // 2, s.shape[1]), F32) for s in smalls]
    grads4 = [g.reshape(N_CHIP, 2, halves[i], g.shape[2]) for i, g in enumerate(grads)]
    smalls3 = [s.reshape(2, s.shape[0] // 2, s.shape[1]) for s in smalls]
    return pl.pallas_call(
        body, out_shape=out_shape,
        grid_spec=pltpu.PrefetchScalarGridSpec(num_scalar_prefetch=1, grid=(N_CHIP,), in_specs=in_specs, out_specs=out_specs),
        compiler_params=_cp(("arbitrary",), VMEM_LIMIT), name=name)(c_idx, *grads4, *recvd, *smalls3, *smalls_recvd)


def _exchange_steps(ins, outs, ng, send, recv, loc):
    x, y, c, chips = _coords()
    b = 2 * x + y
    n = len(ins)
    blks = [2 * chip[0] + chip[1] for chip in chips]

    def copy(i, k, arriving):
        if i < ng:
            src, dst = ins[i].at[blks[k]], outs[i].at[k]
        else:
            src, dst = ins[i], outs[i].at[blks[k] if arriving else b]
        return pltpu.make_async_remote_copy(src_ref=dst if arriving else src, dst_ref=dst, send_sem=send.at[i, k],
                                            recv_sem=recv.at[i, k], device_id=(*chips[k], c), device_id_type=MESH)

    local = [pltpu.make_async_copy(ins[i], outs[i].at[b], loc.at[i - ng]) for i in range(ng, n)]

    def start():
        for cp in local:
            cp.start()
        for i in range(n):
            for k in range(3):
                copy(i, k, False).start()

    def finish():
        for i in range(n):
            for k in range(3):
                copy(i, k, True).wait_recv()
        for i in range(n):
            for k in range(3):
                copy(i, k, False).wait_send()
        for cp in local:
            cp.wait()

    return start, finish


def _exchange_shapes(sums_b, smalls):
    n = len(sums_b) + len(smalls)
    out_shape = [jax.ShapeDtypeStruct((3,) + g.shape[1:], g.dtype) for g in sums_b]
    out_shape += [jax.ShapeDtypeStruct((N_CHIP,) + s.shape, s.dtype) for s in smalls]
    sems = [pltpu.SemaphoreType.DMA((n, 3)), pltpu.SemaphoreType.DMA((n, 3)),
            pltpu.SemaphoreType.DMA((max(len(smalls), 1),))]
    return out_shape, sems


def _chip_sum(bc_idx, sums, recvd, smalls_slots, steps=4):
    ng, ns = len(sums), len(smalls_slots)

    def body(bc_ref, *refs):
        own, rx = refs[:ng], refs[ng:2 * ng]
        sl = refs[2 * ng:2 * ng + ns]
        o = refs[2 * ng + ns:]
        for i in range(ng):
            tot = own[i][...]
            for j in range(3):
                tot = tot + rx[i][j].astype(F32)
            o[i][...] = tot
        for i in range(ns):
            o[ng + i][...] = ((sl[i][0] + sl[i][1]) + sl[i][2]) + sl[i][3]

    def rows(g):
        return g.shape[1] // steps

    in_specs = [pl.BlockSpec((None, rows(g), g.shape[2]), lambda r, bc: (bc[0], r, 0)) for g in sums]
    in_specs += [pl.BlockSpec((3, rows(g), g.shape[2]), lambda r, bc: (0, r, 0)) for g in sums]
    in_specs += [pl.BlockSpec(s.shape, lambda r, bc: (0, 0, 0)) for s in smalls_slots]
    out_specs = [pl.BlockSpec((rows(g), g.shape[2]), lambda r, bc: (bc[1] * steps + r, 0)) for g in sums]
    out_specs += [pl.BlockSpec(s.shape[1:], lambda r, bc: (bc[1], 0)) for s in smalls_slots]
    out_shape = [jax.ShapeDtypeStruct((2 * g.shape[1], g.shape[2]), F32) for g in sums]
    out_shape += [jax.ShapeDtypeStruct((2 * s.shape[1], s.shape[2]), F32) for s in smalls_slots]
    return pl.pallas_call(
        body, out_shape=out_shape,
        grid_spec=pltpu.PrefetchScalarGridSpec(num_scalar_prefetch=1, grid=(steps,), in_specs=in_specs, out_specs=out_specs),
        compiler_params=_cp(("arbitrary",), VMEM_LIMIT), name="chip_sum")(bc_idx, *sums, *recvd, *smalls_slots)


def _pair_gather(arrs, last):
    n = len(arrs)
    n_dev = 8

    def body(*refs):
        last_in = refs[n]
        outs = refs[n + 1:2 * n + 1]
        slots = refs[2 * n + 1]
        send, recv, gs, gr, loc = refs[2 * n + 2:]
        x, y, c, _ = _coords()
        me = 4 * x + 2 * y + c
        mine_slot = pltpu.make_async_copy(last_in, slots.at[me], loc)
        mine_slot.start()
        gathers = []
        for r in range(1, n_dev):
            to = (me + r) % n_dev
            gathers.append(pltpu.make_async_remote_copy(
                src_ref=last_in, dst_ref=slots.at[me], send_sem=gs.at[r - 1], recv_sem=gr.at[me],
                device_id=(to // 4, (to // 2) % 2, to % 2), device_id_type=MESH))
        for cp in gathers:
            cp.start()
        cps = []
        for i in range(n):
            hr = outs[i].shape[0] // 2
            mine = outs[i].at[pl.ds(c * hr, hr)]
            cps.append(pltpu.make_async_remote_copy(
                src_ref=mine, dst_ref=mine, send_sem=send.at[i], recv_sem=recv.at[i],
                device_id=(x, y, 1 - c), device_id_type=MESH))
        for cp in cps:
            cp.start()
        for i in range(n):
            hr = outs[i].shape[0] // 2
            theirs = outs[i].at[pl.ds((1 - c) * hr, hr)]
            pltpu.make_async_remote_copy(
                src_ref=theirs, dst_ref=theirs, send_sem=send.at[i], recv_sem=recv.at[i],
                device_id=(x, y, 1 - c), device_id_type=MESH).wait_recv()
        for r in range(1, n_dev):
            frm = (me + r) % n_dev
            pltpu.make_async_remote_copy(
                src_ref=slots.at[frm], dst_ref=slots.at[frm], send_sem=gs.at[r - 1], recv_sem=gr.at[frm],
                device_id=(frm // 4, (frm // 2) % 2, frm % 2), device_id_type=MESH).wait_recv()
        for cp in cps + gathers:
            cp.wait_send()
        mine_slot.wait()

    outs = pl.pallas_call(
        body, out_shape=[jax.ShapeDtypeStruct(a.shape, a.dtype) for a in arrs]
        + [jax.ShapeDtypeStruct((n_dev,) + last.shape, last.dtype)],
        in_specs=[ANY] * (n + 1), out_specs=[ANY] * (n + 1),
        input_output_aliases={i: i for i in range(n)},
        scratch_shapes=[pltpu.SemaphoreType.DMA((n,)), pltpu.SemaphoreType.DMA((n,)),
                        pltpu.SemaphoreType.DMA((n_dev - 1,)), pltpu.SemaphoreType.DMA((n_dev,)),
                        pltpu.SemaphoreType.DMA(())],
        name="pair_gather")(*arrs, last)
    return outs[:n], outs[n]


def _adamw_math(w, g, m, v):
    m2 = ADAM_B1 * m + (1.0 - ADAM_B1) * g
    v2 = ADAM_B2 * v + (1.0 - ADAM_B2) * (g * g)
    m_hat = m2 / (1.0 - ADAM_B1 ** ADAM_STEP)
    v_hat = v2 / (1.0 - ADAM_B2 ** ADAM_STEP)
    delta = -ADAM_LR * (m_hat / (jnp.sqrt(v_hat) + ADAM_EPS) + ADAM_WD * w)
    return delta, m2, v2


def _adamw_big(w, g, m, v, name, steps=8):
    r, c = w.shape

    def body(w_ref, g_ref, m_ref, v_ref, d_ref, m2_ref, v2_ref):
        d_ref[...], m2_ref[...], v2_ref[...] = _adamw_math(w_ref[...], g_ref[...], m_ref[...], v_ref[...])

    spec = pl.BlockSpec((r // steps, c), lambda i: (i, 0))
    return pl.pallas_call(
        body, grid=(steps,), in_specs=[spec] * 4, out_specs=[spec] * 3,
        out_shape=[jax.ShapeDtypeStruct((r, c), F32)] * 3,
        compiler_params=_cp(("parallel",), VMEM_LIMIT), name=name)(w, g, m, v)


def _adamw_small(groups, slots):
    n = len(groups)

    def body(*refs):
        slots_ref, g0_ref = refs[4 * n], refs[4 * n + 1 + 3 * n]
        for i in range(n):
            w_ref, g_ref, m_ref, v_ref = refs[4 * i:4 * i + 4]
            d_ref, m2_ref, v2_ref = refs[4 * n + 1 + 3 * i:4 * n + 1 + 3 * i + 3]
            g = g_ref[...]
            if i == 0:
                for dev in range(slots.shape[0]):
                    g = g + slots_ref[dev]
                g0_ref[...] = g
            d_ref[...], m2_ref[...], v2_ref[...] = _adamw_math(w_ref[...], g, m_ref[...], v_ref[...])

    flat = [a for grp in groups for a in grp]
    out_shape = [jax.ShapeDtypeStruct(grp[0].shape, F32) for grp in groups for _ in range(3)]
    out_shape.append(jax.ShapeDtypeStruct(groups[0][0].shape, F32))
    outs = pl.pallas_call(body, out_shape=out_shape, name="adamw_small")(*flat, slots)
    return [tuple(outs[3 * i:3 * i + 3]) for i in range(n)], outs[3 * n]


def kernel(x, mem, norm_mix_g, w_in, conv_w, gm_ln_g, gm_ln_b, gm_ws, gm_bs, w_out, norm_x_g, norm_mem_g, w_q, w_kv, w_xo, norm_final_g, loss_target, m_norm_mix_g, m_w_in, m_conv_w, m_gm_ln_g, m_gm_ln_b, m_gm_ws, m_gm_bs, m_w_out, m_norm_x_g, m_norm_mem_g, m_w_q, m_w_kv, m_w_xo, m_norm_final_g, v_norm_mix_g, v_w_in, v_conv_w, v_gm_ln_g, v_gm_ln_b, v_gm_ws, v_gm_bs, v_w_out, v_norm_x_g, v_norm_mem_g, v_w_q, v_w_kv, v_w_xo, v_norm_final_g):
    t = x.shape[1]
    xi = lax.axis_index("x")
    yi = lax.axis_index("y")
    ci = lax.axis_index("c")
    b_idx = jnp.reshape(2 * xi + yi, (1,)).astype(jnp.int32)
    c_idx = jnp.reshape(ci, (1,)).astype(jnp.int32)

    x2d, mem2d, tgt = x[0], mem[0], loss_target[0]
    big = [w_in[0], w_out[0], w_q[0], w_kv[0], w_xo[0]]
    big_m = [m_w_in[0], m_w_out[0], m_w_q[0], m_w_kv[0], m_w_xo[0]]
    big_v = [v_w_in[0], v_w_out[0], v_w_q[0], v_w_kv[0], v_w_xo[0]]
    g3 = norm_final_g.reshape(1, D)

    def pad8(a):
        return jnp.pad(a, ((0, 8 - a.shape[0]), (0, 0)))

    own_blocks = _cast_shards(b_idx, big)

    tril = jnp.tril(jnp.ones((CH, CH), bool))
    wc32 = jnp.where(tril[None], gm_ws[0], 0.0)
    wc = wc32.astype(BF16)
    wct = jnp.swapaxes(wc32, 1, 2).astype(BF16)
    bsb = jnp.broadcast_to(gm_bs[0][:, :, None], (HEADS, CH, CH))

    blk = 2 * xi + yi
    order = jnp.stack([blk, blk ^ 2, blk ^ 1, blk ^ 3]).astype(jnp.int32)
    proj, ht, win_f, cw8, (wout_f, wkv_f) = _proj_gather(
        order, x2d, norm_mix_g, own_blocks[0], pad8(conv_w[0]), [own_blocks[1], own_blocks[3]])
    mixin, mixt, (wq_f, wxo_f) = _mixer_fwd(proj, cw8, gm_ln_g, gm_ln_b, wc, bsb, [own_blocks[2], own_blocks[4]])
    wout2, wq2, wxo2 = wout_f.reshape(MIX, D), wq_f.reshape(D, D), wxo_f.reshape(D, D)
    k, v, mt = _mem_fwd(mem2d, norm_mem_g, wkv_f)
    del mt

    (loss_tile, dmix, dx1b, h2t, dq, ot, dx2b, dk, dv, dg2, dg3) = _tail(
        x2d, tgt, mixin, wout2, wq2, wxo2, k, v, norm_x_g, g3)
    dwkv, dwkv_b, dgm = _mem_bwd(mem2d, norm_mem_g, dk, dv, wkv_f)
    dwxo, dwxo_b = _grad_matmul(ot, dx2b, by_cols=False, name="grad_w_xo")
    dwq, dwq_b = _grad_matmul(h2t, dq, by_cols=False, name="grad_w_q")
    dwout, dwout_b = _grad_matmul(mixt, dx1b, by_cols=False, name="grad_w_out")

    rx1a = _pair_exchange([dwout_b, dwq_b, dwkv_b, dwxo_b], [], "pair_exchange_a")
    ps_a = _pair_sum(c_idx, [dwout, dwq, dwkv, dwxo], rx1a, [], [], "pair_sum_a")
    sums_a, sums_a_b = ps_a[:4], ps_a[4:8]
    (dproj, dcw, dlng, dlnb, dwc, dbs8), rx2a = _mixer_bwd(proj, dmix, cw8, gm_ln_g, gm_ln_b, wc, wct, bsb, sums_a_b)

    dwin, dwin_b = _grad_matmul(ht, dproj, by_cols=True, name="grad_w_in")
    zero = jnp.zeros((1, D), F32)
    loss_row = jnp.broadcast_to(loss_tile[0:1, 0:1], (1, D))
    sv = jnp.concatenate([zero, dg2, dgm, dg3, dlng, dlnb, dbs8[0:1], loss_row, dcw], axis=0)
    sw = dwc.reshape(HEADS * CH, CH)
    rx1b = _pair_exchange([dwin_b], [sv, sw], "pair_exchange_b")
    ps_b = _pair_sum(c_idx, [dwin], rx1b[:1], [sv, sw], rx1b[1:], "pair_sum_b")
    sums_b, sums_b_b, psmall = ps_b[:1], ps_b[1:2], ps_b[2:]
    grad_x, dg1, rx2b = _input_grad(dproj, win_f, x2d, dx1b, norm_mix_g, sums_b_b, psmall)

    red = _chip_sum(jnp.concatenate([b_idx, c_idx]), sums_b + sums_a, rx2b[:1] + rx2a, rx2b[1:])
    (gwin, gwout, gwq, gwkv, gwxo, svf, swf), dg1_slots = _pair_gather(red, dg1)

    names = ["w_in", "w_out", "w_q", "w_kv", "w_xo"]
    big_out = [_adamw_big(w, g, m, v, "adamw_" + nm)
               for w, g, m, v, nm in zip(big, [gwin, gwout, gwq, gwkv, gwxo], big_m, big_v, names)]

    def vec_pack(a1, a2, am, a3, lg, lb, bs):
        return jnp.concatenate([a1, a2, am, a3.reshape(1, D), lg, lb, bs.reshape(1, D), zero], axis=0)

    wv = vec_pack(norm_mix_g, norm_x_g, norm_mem_g, norm_final_g, gm_ln_g, gm_ln_b, gm_bs)
    mv = vec_pack(m_norm_mix_g, m_norm_x_g, m_norm_mem_g, m_norm_final_g, m_gm_ln_g, m_gm_ln_b, m_gm_bs)
    vv = vec_pack(v_norm_mix_g, v_norm_x_g, v_norm_mem_g, v_norm_final_g, v_gm_ln_g, v_gm_ln_b, v_gm_bs)
    loss = svf[7, 0]
    gcw = lax.dynamic_slice_in_dim(svf[8:16], blk * (D // N_CHIP), D // N_CHIP, axis=1)
    gws = swf
    ((dv_, mv_, vv_), (dc_, mc_, vc_), (dws_, mws_, vws_)), gv = _adamw_small([
        (wv, svf[0:8], mv, vv),
        (pad8(conv_w[0]), gcw, pad8(m_conv_w[0]), pad8(v_conv_w[0])),
        (gm_ws.reshape(HEADS * CH, CH), gws, m_gm_ws.reshape(HEADS * CH, CH), v_gm_ws.reshape(HEADS * CH, CH))],
        dg1_slots)

    def unpack(vecs, cw, ws, bigs):
        r = lambda i: vecs[i:i + 1]
        return [r(0), bigs[0][None], cw[0:3][None], r(4), r(5), ws.reshape(1, HEADS, CH, CH), vecs[6].reshape(1, HEADS, CH),
                bigs[1][None], r(1), r(2), bigs[2][None], bigs[3][None], bigs[4][None], vecs[3]]

    grads_out = unpack(gv, gcw, gws, [gwin, gwout, gwq, gwkv, gwxo])
    delta_out = unpack(dv_, dc_, dws_, [o[0] for o in big_out])
    m_out = unpack(mv_, mc_, mws_, [o[1] for o in big_out])
    v_out = unpack(vv_, vc_, vws_, [o[2] for o in big_out])
    return (loss, grad_x[None], *grads_out, *delta_out, *m_out, *v_out)
```

```python
import functools
import math

import jax
import jax.numpy as jnp
from jax import lax
from jax.experimental import pallas as pl
from jax.experimental.pallas import tpu as pltpu

F32 = jnp.float32
BF16 = jnp.bfloat16
MESH = pl.DeviceIdType.MESH

D = 1024
SLAB = 1024
N_SLAB = 7
IN_DIM = N_SLAB * SLAB
MIX = 2 * SLAB
HEADS = 8
CH = 128
XH = 4
XD = D // XH
EPS = 1e-6
GELU_C = math.sqrt(2.0 / math.pi)
GELU_A = 0.044715
N_CHIP = 4
IN_BLK = IN_DIM // N_CHIP
KV_BLK = 2 * D // N_CHIP

ADAM_LR, ADAM_B1, ADAM_B2, ADAM_EPS, ADAM_WD, ADAM_STEP = 0.001, 0.9, 0.999, 1e-08, 0.01, 10

VMEM_LIMIT = 60 * 1024 * 1024


def _cp(sem=None, vmem=None):
    return pltpu.CompilerParams(dimension_semantics=sem, vmem_limit_bytes=vmem)


def _full(shape, buffers=None):
    n = len(shape)
    if buffers is None:
        return pl.BlockSpec(shape, lambda *_: (0,) * n)
    return pl.BlockSpec(shape, lambda *_: (0,) * n, pipeline_mode=pl.Buffered(buffers))


ANY = pl.BlockSpec(memory_space=pl.ANY)


def _bdot(a, b):
    return jnp.dot(a.astype(BF16), b.astype(BF16), preferred_element_type=F32)


def _bdot_nt(a, b):
    return lax.dot_general(a.astype(BF16), b.astype(BF16), (((1,), (1,)), ((), ())), preferred_element_type=F32)


def _rms(x, g):
    r = lax.rsqrt(jnp.mean(x * x, axis=-1, keepdims=True) + EPS)
    return x * r * g, r


def _rms_bwd(dy, x, r, g):
    gdy = dy * g
    dx = r * gdy - x * (r * r * r) * jnp.mean(x * gdy, axis=-1, keepdims=True)
    dg = jnp.sum(dy * x * r, axis=0, keepdims=True)
    return dx, dg


def _gelu_parts(x):
    x2 = x * x
    t = jnp.tanh(GELU_C * (x + GELU_A * x * x2))
    val = 0.5 * x * (1.0 + t)
    grad = 0.5 * (1.0 + t) + 0.5 * x * (1.0 - t * t) * (GELU_C * (1.0 + 3.0 * GELU_A * x2))
    return val, grad


def _gelu(x):
    return 0.5 * x * (1.0 + jnp.tanh(GELU_C * (x + GELU_A * x * x * x)))


def _sigmoid(z):
    return 1.0 / (1.0 + jnp.exp(-z))


def _cast_shards(b_idx, arrs):
    n = len(arrs)
    steps = 8

    def body(b_ref, *refs):
        for i in range(n):
            refs[n + i][...] = refs[i][...].astype(BF16)

    in_specs = [pl.BlockSpec((a.shape[0] // steps, a.shape[1]), lambda i, b: (i, 0)) for a in arrs]
    out_specs = [pl.BlockSpec((None, a.shape[0] // steps, a.shape[1]), lambda i, b: (b[0], i, 0)) for a in arrs]
    return pl.pallas_call(
        body, out_shape=[jax.ShapeDtypeStruct((N_CHIP,) + a.shape, BF16) for a in arrs],
        grid_spec=pltpu.PrefetchScalarGridSpec(num_scalar_prefetch=1, grid=(steps,), in_specs=in_specs, out_specs=out_specs),
        compiler_params=_cp(("arbitrary",)), name="cast_shards")(b_idx, *arrs)


def _proj_gather(order, x, g, win_own, cw8s, more, tm=1024):
    t = x.shape[0]
    ni = t // tm
    hr = D // 2
    nm = len(more)

    def body(*refs):
        order_ref, x_ref, g_ref, win_in, cw_in = refs[:5]
        o_ref, ht_ref, win_f, cw_out = refs[5 + nm:9 + nm]
        more_out = refs[9 + nm:9 + 2 * nm]
        hbuf, wv, ici_s, ici_r, d2d_s, d2d_r, cw_s, cw_r, loc, m_is, m_ir, m_ds, m_dr = refs[9 + 2 * nm:]
        more_start, more_finish = _gather_steps(more_out, m_is, m_ir, m_ds, m_dr)
        j, i = pl.program_id(0), pl.program_id(1)
        x, y, c, chips = _coords()
        b = 2 * x + y
        sib = (x, y, 1 - c)
        blks = [2 * chip[0] + chip[1] for chip in chips]

        def half(blk, hc):
            return win_f.at[blk, pl.ds(hc * hr, hr)]

        def cw_cols(blk):
            return cw_out.at[:, pl.ds(blk * (D // N_CHIP), D // N_CHIP)]

        def ici(k, blk):
            return pltpu.make_async_remote_copy(src_ref=half(blk, c), dst_ref=half(blk, c), send_sem=ici_s.at[k],
                                                recv_sem=ici_r.at[k], device_id=(*chips[k], c), device_id_type=MESH)

        def relay(k, blk, hc):
            return pltpu.make_async_remote_copy(src_ref=half(blk, hc), dst_ref=half(blk, hc), send_sem=d2d_s.at[k],
                                                recv_sem=d2d_r.at[k], device_id=sib, device_id_type=MESH)

        def cw_copy(k, blk):
            src = cw_in if blk is None else cw_cols(blk)
            return pltpu.make_async_remote_copy(src_ref=src, dst_ref=cw_cols(b if blk is None else blk), send_sem=cw_s.at[k],
                                                recv_sem=cw_r.at[k], device_id=(*chips[k], c), device_id_type=MESH)

        cw_local = pltpu.make_async_copy(cw_in, cw_cols(b), loc.at[1])

        @pl.when((j == 0) & (i == 0))
        def _():
            cw_local.start()
            for k in range(3):
                ici(k, b).start()
            for k in range(3):
                cw_copy(k, None).start()
            more_start()

        for jj in range(N_CHIP):
            @pl.when((j == jj) & (i == 0))
            def _(jj=jj):
                if jj == 0:
                    blk = b
                else:
                    blk = blks[jj - 1]
                    ici(jj - 1, blk).wait_recv()
                    relay(jj - 1, blk, c).start()
                    relay(jj - 1, blk, 1 - c).wait_recv()
                load = pltpu.make_async_copy(win_f.at[blk], wv, loc.at[0])
                load.start()
                load.wait()

        rows = pl.ds(pl.multiple_of(i * tm, tm), tm)

        @pl.when(j == 0)
        def _():
            h, _ = _rms(x_ref[...], g_ref[...])
            hbuf[rows, :] = h.astype(BF16)
            ht_ref[...] = h.T.astype(BF16)

        o_ref[...] = jnp.dot(hbuf[rows, :], wv[...], preferred_element_type=F32).astype(BF16)

        @pl.when((j == N_CHIP - 1) & (i == ni - 1))
        def _():
            for k in range(3):
                cw_copy(k, blks[k]).wait_recv()
            for k in range(3):
                ici(k, b).wait_send()
                relay(k, blks[k], c).wait_send()
                cw_copy(k, None).wait_send()
            cw_local.wait()
            more_finish()

    first = lambda j, i: jnp.where(j == 0, i, ni - 1)
    in_specs = [pl.BlockSpec((tm, D), lambda j, i, o: (first(j, i), 0)), pl.BlockSpec((1, D), lambda j, i, o: (0, 0)),
                ANY, ANY] + [ANY] * nm
    out_specs = [pl.BlockSpec((tm, IN_BLK), lambda j, i, o: (i, o[j])),
                 pl.BlockSpec((D, tm), lambda j, i, o: (0, first(j, i))), ANY, ANY] + [ANY] * nm
    outs = pl.pallas_call(
        body, out_shape=[jax.ShapeDtypeStruct((t, IN_DIM), BF16), jax.ShapeDtypeStruct((D, t), BF16),
                         jax.ShapeDtypeStruct(win_own.shape, BF16), jax.ShapeDtypeStruct((8, D), F32)]
        + [jax.ShapeDtypeStruct(f.shape, f.dtype) for f in more],
        grid_spec=pltpu.PrefetchScalarGridSpec(
            num_scalar_prefetch=1, grid=(N_CHIP, ni), in_specs=in_specs, out_specs=out_specs,
            scratch_shapes=[pltpu.VMEM((t, D), BF16), pltpu.VMEM((D, IN_BLK), BF16)]
            + [pltpu.SemaphoreType.DMA((3,))] * 6 + [pltpu.SemaphoreType.DMA((2,))]
            + [pltpu.SemaphoreType.DMA((max(nm, 1), 3))] * 4),
        input_output_aliases={3: 2, **{5 + w: 4 + w for w in range(nm)}},
        compiler_params=_cp(("arbitrary", "arbitrary"), VMEM_LIMIT), name="proj_gather")(order, x, g, win_own, cw8s, *more)
    return outs[0], outs[1], outs[2], outs[3], outs[4:]


def _gather_steps(outs, ici_s, ici_r, d2d_s, d2d_r):
    x, y, c, chips = _coords()
    b = 2 * x + y
    sib = (x, y, 1 - c)
    nw = len(outs)

    def half(w, blk, hc):
        hr = outs[w].shape[1] // 2
        return outs[w].at[blk, pl.ds(hc * hr, hr)]

    def ici(w, k, blk):
        return pltpu.make_async_remote_copy(src_ref=half(w, blk, c), dst_ref=half(w, blk, c), send_sem=ici_s.at[w, k],
                                            recv_sem=ici_r.at[w, k], device_id=(*chips[k], c), device_id_type=MESH)

    def relay(w, k, blk, hc):
        return pltpu.make_async_remote_copy(src_ref=half(w, blk, hc), dst_ref=half(w, blk, hc), send_sem=d2d_s.at[w, k],
                                            recv_sem=d2d_r.at[w, k], device_id=sib, device_id_type=MESH)

    def start():
        for w in range(nw):
            for k in range(3):
                ici(w, k, b).start()

    def finish():
        for w in range(nw):
            for k in range(3):
                blk = 2 * chips[k][0] + chips[k][1]
                ici(w, k, blk).wait_recv()
                relay(w, k, blk, c).start()
        for w in range(nw):
            for k in range(3):
                blk = 2 * chips[k][0] + chips[k][1]
                relay(w, k, blk, 1 - c).wait_recv()
        for w in range(nw):
            for k in range(3):
                blk = 2 * chips[k][0] + chips[k][1]
                ici(w, k, b).wait_send()
                relay(w, k, blk, c).wait_send()

    return start, finish


def _mixer_fwd(proj, cw8, lng, lnb, wc, bsb, fulls, tm=256):
    t = proj.shape[0]
    nt = t // tm
    nch = tm // CH
    nw = len(fulls)

    def body(*refs):
        p_ref, cw_ref, lng_ref, lnb_ref, wc_ref, bsb_ref = refs[:6]
        mix_ref, mixt_ref = refs[6 + nw:8 + nw]
        w_outs = refs[8 + nw:8 + 2 * nw]
        prev_ref, stage_ref, ici_s, ici_r, d2d_s, d2d_r = refs[8 + 2 * nw:]
        gather_start, gather_finish = _gather_steps(w_outs, ici_s, ici_r, d2d_s, d2d_r)

        @pl.when(pl.program_id(0) == 0)
        def _():
            gather_start()
            prev_ref[...] = jnp.zeros_like(prev_ref)

        rows = lax.broadcasted_iota(jnp.int32, (tm, CH), 0)
        for s in range(HEADS):
            cs = pl.ds(CH * s, CH)

            def slab(k):
                return p_ref[:, pl.ds(k * SLAB + CH * s, CH)].astype(F32)

            gb, gc, xa, za = slab(0), slab(1), slab(2), slab(3)
            cx = gc * xa
            p6 = jnp.broadcast_to(prev_ref[6:7, cs], (tm, CH))
            p7 = jnp.broadcast_to(prev_ref[7:8, cs], (tm, CH))
            c1 = jnp.where(rows == 0, p7, pltpu.roll(cx, 1, 0))
            c2 = jnp.where(rows == 0, p6, jnp.where(rows == 1, p7, pltpu.roll(cx, 2, 0)))
            prev_ref[:, cs] = cx[tm - 8:, :]
            cv = cw_ref[0:1, cs] * c2 + cw_ref[1:2, cs] * c1 + cw_ref[2:3, cs] * cx
            stage_ref[:, cs] = gb * cv * (za * _sigmoid(za))

            u, v, zb = slab(4), slab(5), slab(6)
            ug, vg = _gelu(u), _gelu(v)
            dlt = vg - jnp.mean(vg, axis=-1, keepdims=True)
            vhat = dlt * lax.rsqrt(jnp.mean(dlt * dlt, axis=-1, keepdims=True) + EPS)
            vn = (vhat * lng_ref[:, cs] + lnb_ref[:, cs]).astype(BF16)
            gate = ug * (zb * _sigmoid(zb))
            for c in range(nch):
                rs = slice(CH * c, CH * (c + 1))
                sp = jnp.dot(wc_ref[s], vn[rs], preferred_element_type=F32) + bsb_ref[s]
                stage_ref[rs, pl.ds(SLAB + CH * s, CH)] = gate[rs] * sp

        full = stage_ref[...]
        mix_ref[...] = full.astype(BF16)
        mixt_ref[...] = full.T.astype(BF16)

        @pl.when(pl.program_id(0) == nt - 1)
        def _():
            gather_finish()

    sems = [pltpu.SemaphoreType.DMA((nw, 3))] * 4
    outs = pl.pallas_call(
        body, grid=(nt,),
        in_specs=[pl.BlockSpec((tm, IN_DIM), lambda i: (i, 0)), _full((8, D)), _full((1, D)), _full((1, D)),
                  _full((HEADS, CH, CH)), _full((HEADS, CH, CH))] + [ANY] * nw,
        out_specs=[pl.BlockSpec((tm, MIX), lambda i: (i, 0)), pl.BlockSpec((MIX, tm), lambda i: (0, i))] + [ANY] * nw,
        out_shape=[jax.ShapeDtypeStruct((t, MIX), BF16), jax.ShapeDtypeStruct((MIX, t), BF16)]
        + [jax.ShapeDtypeStruct(f.shape, f.dtype) for f in fulls],
        input_output_aliases={6 + w: 2 + w for w in range(nw)},
        scratch_shapes=[pltpu.VMEM((8, D), F32), pltpu.VMEM((tm, MIX), F32)] + sems,
        compiler_params=_cp(("arbitrary",), VMEM_LIMIT), name="mixer_fwd")(proj, cw8, lng, lnb, wc, bsb, *fulls)
    return outs[0], outs[1], outs[2:]


def _mem_fwd(mem, gm, wkv_f):
    n_mem = mem.shape[0]

    def body(mem_ref, gm_ref, w_ref, k_ref, v_ref, mt_ref):
        m, _ = _rms(mem_ref[...], gm_ref[...])
        mb = m.astype(BF16)
        mt_ref[...] = m.T.astype(BF16)
        for j in range(N_CHIP):
            dst = k_ref if j < 2 else v_ref
            dst[:, pl.ds(KV_BLK * (j % 2), KV_BLK)] = jnp.dot(mb, w_ref[j], preferred_element_type=F32).astype(BF16)

    return pl.pallas_call(
        body, out_shape=[jax.ShapeDtypeStruct((n_mem, D), BF16), jax.ShapeDtypeStruct((n_mem, D), BF16),
                         jax.ShapeDtypeStruct((D, n_mem), BF16)],
        compiler_params=_cp(None, VMEM_LIMIT), name="mem_fwd")(mem, gm, wkv_f)


def _tail(x, tgt, mixin, wout, wq, wxo, k, v, g2, g3, tm=512, sub=512):
    t = x.shape[0]
    n_mem = k.shape[0]
    scale = 1.0 / math.sqrt(XD)

    def body(x_ref, tgt_ref, mix_ref, wout_ref, wq_ref, wxo_ref, k_ref, v_ref, g2_ref, g3_ref,
             loss_ref, dmix_ref, dx1b_ref, h2t_ref, dq_ref, ot_ref, dx2b_ref, dk_ref, dv_ref, dg2_ref, dg3_ref):
        @pl.when(pl.program_id(0) == 0)
        def _():
            loss_ref[...] = jnp.zeros_like(loss_ref)
            dk_ref[...] = jnp.zeros_like(dk_ref)
            dv_ref[...] = jnp.zeros_like(dv_ref)
            dg2_ref[...] = jnp.zeros_like(dg2_ref)
            dg3_ref[...] = jnp.zeros_like(dg3_ref)

        g2, g3 = g2_ref[...], g3_ref[...]
        for sb in range(tm // sub):
            rs = pl.ds(sub * sb, sub)
            x1 = x_ref[rs, :] + jnp.dot(mix_ref[rs, :], wout_ref[...], preferred_element_type=F32)
            h2, r2 = _rms(x1, g2)
            h2t_ref[:, rs] = h2.T.astype(BF16)
            q = _bdot(h2, wq_ref[...]).astype(BF16)
            probs, outs = [], []
            for hd in range(XH):
                hs = pl.ds(XD * hd, XD)
                s = _bdot_nt(q[:, XD * hd:XD * (hd + 1)], k_ref[:, hs]) * scale
                e = jnp.exp(s - jnp.max(s, axis=-1, keepdims=True))
                p = e / jnp.sum(e, axis=-1, keepdims=True)
                probs.append(p)
                outs.append(_bdot(p, v_ref[:, hs]))
            o = jnp.concatenate(outs, axis=-1)
            ot_ref[:, rs] = o.T.astype(BF16)
            x2 = x1 + _bdot(o, wxo_ref[...])
            y, r3 = _rms(x2, g3)
            diff = y - tgt_ref[rs, :]
            row_loss = jnp.sum(diff * diff, axis=-1, keepdims=True)
            loss_ref[...] += jnp.broadcast_to(jnp.sum(row_loss, axis=0, keepdims=True) * (0.5 / D), loss_ref.shape)

            dx2, dg3 = _rms_bwd(diff * (1.0 / D), x2, r3, g3)
            dg3_ref[...] += dg3
            dx2b = dx2.astype(BF16)
            dx2b_ref[rs, :] = dx2b
            do = _bdot_nt(dx2b, wxo_ref[...])
            dqs = []
            for hd in range(XH):
                hs = pl.ds(XD * hd, XD)
                p = probs[hd]
                do_h = do[:, XD * hd:XD * (hd + 1)]
                dv_ref[:, hs] += _bdot(p.T, do_h)
                dp = _bdot_nt(do_h, v_ref[:, hs])
                ds = p * (dp - jnp.sum(dp * p, axis=-1, keepdims=True))
                dqs.append(_bdot(ds, k_ref[:, hs]) * scale)
                dk_ref[:, hs] += _bdot(ds.T, q[:, XD * hd:XD * (hd + 1)]) * scale
            dq = jnp.concatenate(dqs, axis=-1).astype(BF16)
            dq_ref[rs, :] = dq
            dx1n, dg2 = _rms_bwd(_bdot_nt(dq, wq_ref[...]), x1, r2, g2)
            dg2_ref[...] += dg2
            dx1b = (dx2 + dx1n).astype(BF16)
            dx1b_ref[rs, :] = dx1b
            dmix_ref[rs, :] = _bdot_nt(dx1b, wout_ref[...]).astype(BF16)

    tok = lambda w: pl.BlockSpec((tm, w), lambda i: (i, 0))
    tok_t = lambda w: pl.BlockSpec((w, tm), lambda i: (0, i))
    return pl.pallas_call(
        body, grid=(t // tm,),
        in_specs=[tok(D), tok(D), tok(MIX), _full((MIX, D), 1), _full((D, D), 1), _full((D, D), 1),
                  _full((n_mem, D), 1), _full((n_mem, D), 1), _full((1, D)), _full((1, D))],
        out_specs=[_full((8, 128)), tok(MIX), tok(D), tok_t(D), tok(D), tok_t(D), tok(D),
                   _full((n_mem, D)), _full((n_mem, D)), _full((1, D)), _full((1, D))],
        out_shape=[jax.ShapeDtypeStruct((8, 128), F32), jax.ShapeDtypeStruct((t, MIX), BF16),
                   jax.ShapeDtypeStruct((t, D), BF16),
                   jax.ShapeDtypeStruct((D, t), BF16), jax.ShapeDtypeStruct((t, D), BF16),
                   jax.ShapeDtypeStruct((D, t), BF16), jax.ShapeDtypeStruct((t, D), BF16),
                   jax.ShapeDtypeStruct((n_mem, D), F32), jax.ShapeDtypeStruct((n_mem, D), F32),
                   jax.ShapeDtypeStruct((1, D), F32), jax.ShapeDtypeStruct((1, D), F32)],
        compiler_params=_cp(("arbitrary",), VMEM_LIMIT), name="tail")(x, tgt, mixin, wout, wq, wxo, k, v, g2, g3)


def _mem_bwd(mem, gm, dk, dv, wkv_f):
    def body(mem_ref, gm_ref, dk_ref, dv_ref, w_ref, dw_ref, dwb_ref, dgm_ref):
        mem_v = mem_ref[...]
        m, rm = _rms(mem_v, gm_ref[...])
        mt = m.T.astype(BF16)
        dm = jnp.zeros_like(mem_v)
        for j in range(N_CHIP):
            src = dk_ref if j < 2 else dv_ref
            dkv = src[:, pl.ds(KV_BLK * (j % 2), KV_BLK)].astype(BF16)
            dw = jnp.dot(mt, dkv, preferred_element_type=F32)
            dw_ref[j] = dw
            dwb_ref[j] = dw.astype(BF16)
            dm = dm + _bdot_nt(dkv, w_ref[j])
        dgm_ref[...] = jnp.sum(dm * mem_v * rm, axis=0, keepdims=True)

    return pl.pallas_call(
        body, out_shape=[jax.ShapeDtypeStruct((N_CHIP, D, KV_BLK), F32), jax.ShapeDtypeStruct((N_CHIP, D, KV_BLK), BF16),
                         jax.ShapeDtypeStruct((1, D), F32)],
        compiler_params=_cp(None, VMEM_LIMIT), name="mem_bwd")(mem, gm, dk, dv, wkv_f)


def _mixer_bwd(proj, dmix, cw8, lng, lnb, wc, wct, bsb, xch, tm=256):
    t = proj.shape[0]
    nt = t // tm
    nch = tm // CH
    hb = 16
    nx = len(xch)
    xch_shape, xch_sems = _exchange_shapes(xch, [])

    def body(*refs):
        p_ref, pgc_ref, pxa_ref, dm_ref, cw_ref, lng_ref, lnb_ref, wc_ref, wct_ref, bsb_ref = refs[:10]
        x_in = refs[10:10 + nx]
        dp_ref, dcw_ref, dlng_ref, dlnb_ref, dwc_ref, dbs_ref = refs[10 + nx:16 + nx]
        x_out = refs[16 + nx:16 + 2 * nx]
        next_ref, send, recv, loc = refs[16 + 2 * nx:]
        exchange_start, exchange_finish = _exchange_steps(x_in, x_out, nx, send, recv, loc)
        i = pl.program_id(0)

        @pl.when(i == 0)
        def _():
            exchange_start()
            next_ref[...] = jnp.zeros_like(next_ref)
            dcw_ref[...] = jnp.zeros_like(dcw_ref)
            dlng_ref[...] = jnp.zeros_like(dlng_ref)
            dlnb_ref[...] = jnp.zeros_like(dlnb_ref)
            dwc_ref[...] = jnp.zeros_like(dwc_ref)
            dbs_ref[...] = jnp.zeros_like(dbs_ref)

        first_tile = i == nt - 1
        rows = lax.broadcasted_iota(jnp.int32, (tm, CH), 0)
        ones8 = jnp.ones((8, CH), BF16)
        for s in range(HEADS):
            cs = pl.ds(CH * s, CH)

            def slab(k):
                return p_ref[:, pl.ds(k * SLAB + CH * s, CH)].astype(F32)

            gb, gc, xa, za = slab(0), slab(1), slab(2), slab(3)
            da = dm_ref[:, cs].astype(F32)
            cx = gc * xa
            cxp = pgc_ref[:, cs].astype(F32) * pxa_ref[:, cs].astype(F32)
            cxp = jnp.where(first_tile, jnp.zeros_like(cxp), cxp)
            p6 = jnp.broadcast_to(cxp[hb - 2:hb - 1, :], (tm, CH))
            p7 = jnp.broadcast_to(cxp[hb - 1:hb, :], (tm, CH))
            c1 = jnp.where(rows == 0, p7, pltpu.roll(cx, 1, 0))
            c2 = jnp.where(rows == 0, p6, jnp.where(rows == 1, p7, pltpu.roll(cx, 2, 0)))
            w0, w1, w2 = cw_ref[0:1, cs], cw_ref[1:2, cs], cw_ref[2:3, cs]
            cv = w0 * c2 + w1 * c1 + w2 * cx
            sg = _sigmoid(za)
            sa = za * sg
            dcv = da * gb * sa
            dp_ref[:, pl.ds(0 * SLAB + CH * s, CH)] = (da * cv * sa).astype(BF16)
            dp_ref[:, pl.ds(3 * SLAB + CH * s, CH)] = (da * gb * cv * (sg * (1.0 + za * (1.0 - sg)))).astype(BF16)
            n0 = jnp.broadcast_to(next_ref[0:1, cs], (tm, CH))
            n1 = jnp.broadcast_to(next_ref[1:2, cs], (tm, CH))
            u1 = jnp.where(rows == tm - 1, n0, pltpu.roll(dcv, tm - 1, 0))
            u2 = jnp.where(rows == tm - 2, n0, jnp.where(rows == tm - 1, n1, pltpu.roll(dcv, tm - 2, 0)))
            next_ref[:, cs] = dcv[0:8, :]
            dcx = w2 * dcv + w1 * u1 + w0 * u2
            dp_ref[:, pl.ds(1 * SLAB + CH * s, CH)] = (dcx * xa).astype(BF16)
            dp_ref[:, pl.ds(2 * SLAB + CH * s, CH)] = (dcx * gc).astype(BF16)
            dcw_ref[0:1, cs] += jnp.sum(dcv * c2, axis=0, keepdims=True)
            dcw_ref[1:2, cs] += jnp.sum(dcv * c1, axis=0, keepdims=True)
            dcw_ref[2:3, cs] += jnp.sum(dcv * cx, axis=0, keepdims=True)

            u, v, zb = slab(4), slab(5), slab(6)
            db = dm_ref[:, pl.ds(SLAB + CH * s, CH)].astype(F32)
            ug, ugrad = _gelu_parts(u)
            vg, vgrad = _gelu_parts(v)
            dlt = vg - jnp.mean(vg, axis=-1, keepdims=True)
            rstd = lax.rsqrt(jnp.mean(dlt * dlt, axis=-1, keepdims=True) + EPS)
            vhat = dlt * rstd
            lg = lng_ref[:, cs]
            vn = (vhat * lg + lnb_ref[:, cs]).astype(BF16)
            sgb = _sigmoid(zb)
            szb = zb * sgb
            sps, dvns = [], []
            dbs = jnp.zeros((8, CH), F32)
            dwc = jnp.zeros((CH, CH), F32)
            for c in range(nch):
                rs = slice(CH * c, CH * (c + 1))
                sp = jnp.dot(wc_ref[s], vn[rs], preferred_element_type=F32) + bsb_ref[s]
                dsp = (db[rs] * ug[rs] * szb[rs]).astype(BF16)
                dbs = dbs + lax.dot_general(ones8, dsp, (((1,), (1,)), ((), ())), preferred_element_type=F32)
                dwc = dwc + lax.dot_general(dsp, vn[rs], (((1,), (1,)), ((), ())), preferred_element_type=F32)
                dvns.append(jnp.dot(wct_ref[s], dsp, preferred_element_type=F32))
                sps.append(sp)
            sp = jnp.concatenate(sps, axis=0)
            dvn = jnp.concatenate(dvns, axis=0)
            dbs_ref[:, cs] += dbs
            dwc_ref[s] += dwc
            dlng_ref[:, cs] += jnp.sum(dvn * vhat, axis=0, keepdims=True)
            dlnb_ref[:, cs] += jnp.sum(dvn, axis=0, keepdims=True)
            dvhat = dvn * lg
            dvg = rstd * (dvhat - jnp.mean(dvhat, axis=-1, keepdims=True)
                          - vhat * jnp.mean(dvhat * vhat, axis=-1, keepdims=True))
            dp_ref[:, pl.ds(4 * SLAB + CH * s, CH)] = (db * sp * szb * ugrad).astype(BF16)
            dp_ref[:, pl.ds(5 * SLAB + CH * s, CH)] = (dvg * vgrad).astype(BF16)
            dp_ref[:, pl.ds(6 * SLAB + CH * s, CH)] = (db * ug * sp * (sgb * (1.0 + zb * (1.0 - sgb)))).astype(BF16)

        @pl.when(i == nt - 1)
        def _():
            tril = lax.broadcasted_iota(jnp.int32, (CH, CH), 0) >= lax.broadcasted_iota(jnp.int32, (CH, CH), 1)
            for s in range(HEADS):
                dwc_ref[s] = jnp.where(tril, dwc_ref[s], 0.0)
            exchange_finish()

    rev = lambda i: nt - 1 - i
    halo = lambda col: pl.BlockSpec((hb, SLAB), lambda i: (jnp.maximum(rev(i) * (tm // hb) - 1, 0), col))
    outs = pl.pallas_call(
        body, grid=(nt,),
        in_specs=[pl.BlockSpec((tm, IN_DIM), lambda i: (rev(i), 0)), halo(1), halo(2),
                  pl.BlockSpec((tm, MIX), lambda i: (rev(i), 0)), _full((8, D)), _full((1, D)), _full((1, D)),
                  _full((HEADS, CH, CH)), _full((HEADS, CH, CH)), _full((HEADS, CH, CH))] + [ANY] * nx,
        out_specs=[pl.BlockSpec((tm, IN_DIM), lambda i: (rev(i), 0)), _full((8, D)), _full((1, D)), _full((1, D)),
                   _full((HEADS, CH, CH)), _full((8, D))] + [ANY] * nx,
        out_shape=[jax.ShapeDtypeStruct((t, IN_DIM), BF16), jax.ShapeDtypeStruct((8, D), F32),
                   jax.ShapeDtypeStruct((1, D), F32), jax.ShapeDtypeStruct((1, D), F32),
                   jax.ShapeDtypeStruct((HEADS, CH, CH), F32), jax.ShapeDtypeStruct((8, D), F32)] + xch_shape,
        scratch_shapes=[pltpu.VMEM((8, D), F32)] + xch_sems,
        compiler_params=_cp(("arbitrary",), VMEM_LIMIT), name="mixer_bwd")(
            proj, proj, proj, dmix, cw8, lng, lnb, wc, wct, bsb, *xch)
    return outs[:6], outs[6:]


def _grad_matmul(at, b, *, by_cols, name, tk=1024):
    m, t = at.shape
    n = b.shape[1]
    nk = t // tk
    nj = N_CHIP if by_cols else 1
    bn = n // nj

    def body(a_ref, b_ref, o_ref, ob_ref):
        kk = pl.program_id(1)
        part = jnp.dot(a_ref[...], b_ref[...], preferred_element_type=F32)

        @pl.when(kk == 0)
        def _():
            o_ref[...] = part

        @pl.when(kk > 0)
        def _():
            o_ref[...] += part

        @pl.when(kk == nk - 1)
        def _():
            ob_ref[...] = o_ref[...].astype(BF16)

    a_spec = pl.BlockSpec((m, tk), lambda j, k: (0, k))
    b_spec = pl.BlockSpec((tk, bn), lambda j, k: (k, j))
    o_spec = pl.BlockSpec((None, m, bn), lambda j, k: (j, 0, 0))
    o32, o16 = pl.pallas_call(
        body, grid=(nj, nk), in_specs=[a_spec, b_spec], out_specs=[o_spec, o_spec],
        out_shape=[jax.ShapeDtypeStruct((nj, m, bn), F32), jax.ShapeDtypeStruct((nj, m, bn), BF16)],
        compiler_params=_cp(("parallel", "arbitrary"), VMEM_LIMIT), name=name)(at, b)
    if by_cols:
        return o32, o16
    return o32.reshape(N_CHIP, m // N_CHIP, n), o16.reshape(N_CHIP, m // N_CHIP, n)


def _input_grad(dproj, win_f, x, dx1, g1, after, tm=512):
    t = x.shape[0]
    nt = t // tm

    def body(dp_ref, w_ref, x_ref, dx1_ref, g_ref, after_ref, gx_ref, dg_ref):
        @pl.when(pl.program_id(0) == 0)
        def _():
            dg_ref[...] = jnp.zeros_like(dg_ref)

        dh = _bdot_nt(dp_ref[:, pl.ds(0, IN_BLK)], w_ref[0])
        for j in range(1, N_CHIP):
            dh = dh + _bdot_nt(dp_ref[:, pl.ds(IN_BLK * j, IN_BLK)], w_ref[j])
        xv = x_ref[...]
        r = lax.rsqrt(jnp.mean(xv * xv, axis=-1, keepdims=True) + EPS)
        dxn, dg = _rms_bwd(dh, xv, r, g_ref[...])
        gx_ref[...] = dx1_ref[...].astype(F32) + dxn
        dg_ref[0:1, :] += dg

    tok = lambda w: pl.BlockSpec((tm, w), lambda i: (i, 0))
    return pl.pallas_call(
        body, grid=(nt,),
        in_specs=[tok(IN_DIM), _full((N_CHIP, D, IN_BLK), 1), tok(D), tok(D), _full((1, D)), ANY],
        out_specs=[tok(D), _full((8, D))],
        out_shape=[jax.ShapeDtypeStruct((t, D), F32), jax.ShapeDtypeStruct((8, D), F32)],
        compiler_params=_cp(("arbitrary",), VMEM_LIMIT), name="input_grad")(dproj, win_f, x, dx1, g1, after)


def _coords():
    x, y, c = lax.axis_index("x"), lax.axis_index("y"), lax.axis_index("c")
    chips = [(1 - x, y), (x, 1 - y), (1 - x, 1 - y)]
    return x, y, c, chips


def _pair_exchange(grads_b, smalls, name):
    ng, ns = len(grads_b), len(smalls)
    n = ng + ns

    def body(*refs):
        ins, outs, send, recv = refs[:n], refs[n:2 * n], refs[2 * n], refs[2 * n + 1]
        x, y, c, _ = _coords()
        cps = []
        for i in range(n):
            if i < ng:
                hr = ins[i].shape[1] // 2
                src = ins[i].at[pl.ds(0, N_CHIP), pl.ds((1 - c) * hr, hr)]
            else:
                hr = ins[i].shape[0] // 2
                src = ins[i].at[pl.ds((1 - c) * hr, hr)]
            cps.append(pltpu.make_async_remote_copy(
                src_ref=src, dst_ref=outs[i], send_sem=send.at[i], recv_sem=recv.at[i],
                device_id=(x, y, 1 - c), device_id_type=MESH))
        for cp in cps:
            cp.start()
        for cp in cps:
            cp.wait()

    out_shape = [jax.ShapeDtypeStruct((N_CHIP, g.shape[1] // 2, g.shape[2]), g.dtype) for g in grads_b]
    out_shape += [jax.ShapeDtypeStruct((s.shape[0] // 2, s.shape[1]), s.dtype) for s in smalls]
    return pl.pallas_call(
        body, out_shape=out_shape, in_specs=[ANY] * n, out_specs=[ANY] * n,
        scratch_shapes=[pltpu.SemaphoreType.DMA((n,)), pltpu.SemaphoreType.DMA((n,))],
        name=name)(*grads_b, *smalls)


def _pair_sum(c_idx, grads, recvd, smalls, smalls_recvd, name):
    ng, ns = len(grads), len(smalls)
    halves = [g.shape[1] // 2 for g in grads]

    def body(c_ref, *refs):
        g_in, r_in = refs[:ng], refs[ng:2 * ng]
        s_in, sr_in = refs[2 * ng:2 * ng + ns], refs[2 * ng + ns:2 * ng + 2 * ns]
        o = refs[2 * ng + 2 * ns:]
        for i in range(ng):
            tot = g_in[i][...] + r_in[i][...].astype(F32)
            o[i][...] = tot
            o[ng + i][...] = tot.astype(BF16)
        for i in range(ns):
            o[2 * ng + i][...] = s_in[i][...] + sr_in[i][...]

    in_specs = [pl.BlockSpec((None, None, halves[i], g.shape[2]), lambda b, c: (b, c[0], 0, 0)) for i, g in enumerate(grads)]
    in_specs += [pl.BlockSpec((None, halves[i], g.shape[2]), lambda b, c: (b, 0, 0)) for i, g in enumerate(grads)]
    in_specs += [pl.BlockSpec((None, s.shape[0] // 2, s.shape[1]), lambda b, c: (c[0], 0, 0)) for s in smalls]
    in_specs += [pl.BlockSpec((s.shape[0] // 2, s.shape[1]), lambda b, c: (0, 0)) for s in smalls]
    blk = [pl.BlockSpec((None, halves[i], g.shape[2]), lambda b, c: (b, 0, 0)) for i, g in enumerate(grads)]
    out_specs = blk + blk + [pl.BlockSpec((s.shape[0] // 2, s.shape[1]), lambda b, c: (0, 0)) for s in smalls]
    out_shape = [jax.ShapeDtypeStruct((N_CHIP, halves[i], g.shape[2]), F32) for i, g in enumerate(grads)]
    out_shape += [jax.ShapeDtypeStruct((N_CHIP, halves[i], g.shape[2]), BF16) for i, g in enumerate(grads)]
    out_shape += [jax.ShapeDtypeStruct((s.shape[0] // 2, s.shape[1]), F32) for s in smalls]
    grads4 = [g.reshape(N_CHIP, 2, halves[i], g.shape[2]) for i, g in enumerate(grads)]
    smalls3 = [s.reshape(2, s.shape[0] // 2, s.shape[1]) for s in smalls]
    return pl.pallas_call(
        body, out_shape=out_shape,
        grid_spec=pltpu.PrefetchScalarGridSpec(num_scalar_prefetch=1, grid=(N_CHIP,), in_specs=in_specs, out_specs=out_specs),
        compiler_params=_cp(("arbitrary",), VMEM_LIMIT), name=name)(c_idx, *grads4, *recvd, *smalls3, *smalls_recvd)


def _exchange_steps(ins, outs, ng, send, recv, loc):
    x, y, c, chips = _coords()
    b = 2 * x + y
    n = len(ins)
    blks = [2 * chip[0] + chip[1] for chip in chips]

    def copy(i, k, arriving):
        if i < ng:
            src, dst = ins[i].at[blks[k]], outs[i].at[k]
        else:
            src, dst = ins[i], outs[i].at[blks[k] if arriving else b]
        return pltpu.make_async_remote_copy(src_ref=dst if arriving else src, dst_ref=dst, send_sem=send.at[i, k],
                                            recv_sem=recv.at[i, k], device_id=(*chips[k], c), device_id_type=MESH)

    local = [pltpu.make_async_copy(ins[i], outs[i].at[b], loc.at[i - ng]) for i in range(ng, n)]

    def start():
        for cp in local:
            cp.start()
        for i in range(n):
            for k in range(3):
                copy(i, k, False).start()

    def finish():
        for i in range(n):
            for k in range(3):
                copy(i, k, True).wait_recv()
        for i in range(n):
            for k in range(3):
                copy(i, k, False).wait_send()
        for cp in local:
            cp.wait()

    return start, finish


def _exchange_shapes(sums_b, smalls):
    n = len(sums_b) + len(smalls)
    out_shape = [jax.ShapeDtypeStruct((3,) + g.shape[1:], g.dtype) for g in sums_b]
    out_shape += [jax.ShapeDtypeStruct((N_CHIP,) + s.shape, s.dtype) for s in smalls]
    sems = [pltpu.SemaphoreType.DMA((n, 3)), pltpu.SemaphoreType.DMA((n, 3)),
            pltpu.SemaphoreType.DMA((max(len(smalls), 1),))]
    return out_shape, sems


_HBM = pl.BlockSpec(memory_space=pltpu.HBM)
_SEM = pl.BlockSpec(memory_space=pltpu.SEMAPHORE)


def _split_copies(ins, lands, ng, send, recv, arriving):
    x, y, c, chips = _coords()
    b = 2 * x + y
    copies = []
    for i in range(len(ins)):
        for k in range(3):
            blk = 2 * chips[k][0] + chips[k][1]
            src, dst, got = (ins[i].at[blk], lands[i].at[k], lands[i].at[k]) if i < ng else (ins[i], lands[i].at[b], lands[i].at[blk])
            sems = dict(send_sem=send.at[3 * i + k], recv_sem=recv.at[3 * i + k], device_id=(*chips[k], c), device_id_type=MESH)
            if arriving:
                copies.append(pltpu.make_async_remote_copy(src_ref=got, dst_ref=got, **sems))
            else:
                copies.append(pltpu.make_async_remote_copy(src_ref=src, dst_ref=dst, **sems))
    return copies


def _exchange_begin(sums_b, smalls, name):
    ng, n = len(sums_b), len(sums_b) + len(smalls)
    srcs = list(sums_b) + list(smalls)
    lands = [lax.empty((3,) + g.shape[1:], g.dtype) for g in sums_b] + [lax.empty((N_CHIP,) + s.shape, s.dtype) for s in smalls]

    def body(*refs):
        ins, land_refs = refs[:n], refs[n:2 * n]
        send, recv = refs[2 * n], refs[2 * n + 1]
        token = refs[4 * n + 2]
        for cp in _split_copies(ins, land_refs, ng, send, recv, False):
            cp.start()
        token[...] = jnp.zeros_like(token)

    hbm = lambda a: pltpu.HBM(a.shape, a.dtype)
    outs = pl.pallas_call(
        body, name=name,
        out_shape=(pltpu.SemaphoreType.DMA((3 * n,)), pltpu.SemaphoreType.DMA((3 * n,)), *[hbm(a) for a in srcs + lands],
                   jax.ShapeDtypeStruct((8, 128), F32)),
        in_specs=[_HBM] * (2 * n), out_specs=(_SEM, _SEM, *[_HBM] * (2 * n), pl.BlockSpec(memory_space=pltpu.VMEM)),
        input_output_aliases={i: 2 + i for i in range(2 * n)},
        compiler_params=pltpu.CompilerParams(has_side_effects=pltpu.SideEffectType.DATAFLOW_SIDE_EFFECTING),
    )(*[pltpu.with_memory_space_constraint(a, pltpu.HBM) for a in srcs + lands])
    return outs[0], outs[1], list(outs[2:2 + n]), list(outs[2 + n:2 + 2 * n]), outs[2 + 2 * n]


def _exchange_end(send, recv, srcs, lands, ng, after, name):
    n = len(srcs)

    def body(*refs):
        ins, land_refs = refs[:n], refs[n:2 * n]
        send_ref, recv_ref = refs[2 * n], refs[2 * n + 1]
        for cp in _split_copies(ins, land_refs, ng, send_ref, recv_ref, False):
            cp.wait_send()
        for cp in _split_copies(ins, land_refs, ng, send_ref, recv_ref, True):
            cp.wait_recv()

    hbm = lambda a: pltpu.HBM(a.shape, a.dtype)
    outs = pl.pallas_call(
        body, name=name, out_shape=tuple(hbm(a) for a in list(srcs) + list(lands)),
        in_specs=[_HBM] * (2 * n) + [_SEM, _SEM, ANY], out_specs=tuple([_HBM] * (2 * n)),
        input_output_aliases={i: i for i in range(2 * n)},
        compiler_params=pltpu.CompilerParams(has_side_effects=pltpu.SideEffectType.DATAFLOW_SIDE_EFFECTING),
    )(*srcs, *lands, send, recv, after)
    return list(outs[n:])


def _chip_sum(bc_idx, sums, recvd, smalls_slots, smalls_own, steps=4):
    ng, ns = len(sums), len(smalls_slots)

    def body(bc_ref, *refs):
        own, rx = refs[:ng], refs[ng:2 * ng]
        sl = refs[2 * ng:2 * ng + ns]
        sl_own = refs[2 * ng + ns:2 * ng + 2 * ns]
        o = refs[2 * ng + 2 * ns:]
        for i in range(ng):
            tot = own[i][...]
            for j in range(3):
                tot = tot + rx[i][j].astype(F32)
            o[i][...] = tot
        for i in range(ns):
            term = [jnp.where(bc_ref[0] == kk, sl_own[i][...], sl[i][kk]) for kk in range(N_CHIP)]
            o[ng + i][...] = ((term[0] + term[1]) + term[2]) + term[3]

    def rows(g):
        return g.shape[1] // steps

    in_specs = [pl.BlockSpec((None, rows(g), g.shape[2]), lambda r, bc: (bc[0], r, 0)) for g in sums]
    in_specs += [pl.BlockSpec((3, rows(g), g.shape[2]), lambda r, bc: (0, r, 0)) for g in sums]
    in_specs += [pl.BlockSpec(s.shape, lambda r, bc: (0, 0, 0)) for s in smalls_slots]
    in_specs += [pl.BlockSpec(s.shape[1:], lambda r, bc: (0, 0)) for s in smalls_slots]
    out_specs = [pl.BlockSpec((rows(g), g.shape[2]), lambda r, bc: (bc[1] * steps + r, 0)) for g in sums]
    out_specs += [pl.BlockSpec(s.shape[1:], lambda r, bc: (bc[1], 0)) for s in smalls_slots]
    out_shape = [jax.ShapeDtypeStruct((2 * g.shape[1], g.shape[2]), F32) for g in sums]
    out_shape += [jax.ShapeDtypeStruct((2 * s.shape[1], s.shape[2]), F32) for s in smalls_slots]
    return pl.pallas_call(
        body, out_shape=out_shape,
        grid_spec=pltpu.PrefetchScalarGridSpec(num_scalar_prefetch=1, grid=(steps,), in_specs=in_specs, out_specs=out_specs),
        compiler_params=_cp(("arbitrary",), VMEM_LIMIT), name="chip_sum")(bc_idx, *sums, *recvd, *smalls_slots, *smalls_own)


def _pair_gather(arrs, last):
    n = len(arrs)
    n_dev = 8

    def body(*refs):
        last_in = refs[n]
        outs = refs[n + 1:2 * n + 1]
        slots = refs[2 * n + 1]
        send, recv, gs, gr, loc = refs[2 * n + 2:]
        x, y, c, _ = _coords()
        me = 4 * x + 2 * y + c
        mine_slot = pltpu.make_async_copy(last_in, slots.at[me], loc)
        mine_slot.start()
        gathers = []
        for r in range(1, n_dev):
            to = (me + r) % n_dev
            gathers.append(pltpu.make_async_remote_copy(
                src_ref=last_in, dst_ref=slots.at[me], send_sem=gs.at[r - 1], recv_sem=gr.at[me],
                device_id=(to // 4, (to // 2) % 2, to % 2), device_id_type=MESH))
        for cp in gathers:
            cp.start()
        cps = []
        for i in range(n):
            hr = outs[i].shape[0] // 2
            mine = outs[i].at[pl.ds(c * hr, hr)]
            cps.append(pltpu.make_async_remote_copy(
                src_ref=mine, dst_ref=mine, send_sem=send.at[i], recv_sem=recv.at[i],
                device_id=(x, y, 1 - c), device_id_type=MESH))
        for cp in cps:
            cp.start()
        for i in range(n):
            hr = outs[i].shape[0] // 2
            theirs = outs[i].at[pl.ds((1 - c) * hr, hr)]
            pltpu.make_async_remote_copy(
                src_ref=theirs, dst_ref=theirs, send_sem=send.at[i], recv_sem=recv.at[i],
                device_id=(x, y, 1 - c), device_id_type=MESH).wait_recv()
        for r in range(1, n_dev):
            frm = (me + r) % n_dev
            pltpu.make_async_remote_copy(
                src_ref=slots.at[frm], dst_ref=slots.at[frm], send_sem=gs.at[r - 1], recv_sem=gr.at[frm],
                device_id=(frm // 4, (frm // 2) % 2, frm % 2), device_id_type=MESH).wait_recv()
        for cp in cps + gathers:
            cp.wait_send()
        mine_slot.wait()

    outs = pl.pallas_call(
        body, out_shape=[jax.ShapeDtypeStruct(a.shape, a.dtype) for a in arrs]
        + [jax.ShapeDtypeStruct((n_dev,) + last.shape, last.dtype)],
        in_specs=[ANY] * (n + 1), out_specs=[ANY] * (n + 1),
        input_output_aliases={i: i for i in range(n)},
        scratch_shapes=[pltpu.SemaphoreType.DMA((n,)), pltpu.SemaphoreType.DMA((n,)),
                        pltpu.SemaphoreType.DMA((n_dev - 1,)), pltpu.SemaphoreType.DMA((n_dev,)),
                        pltpu.SemaphoreType.DMA(())],
        name="pair_gather")(*arrs, last)
    return outs[:n], outs[n]


def _adamw_math(w, g, m, v):
    m2 = ADAM_B1 * m + (1.0 - ADAM_B1) * g
    v2 = ADAM_B2 * v + (1.0 - ADAM_B2) * (g * g)
    m_hat = m2 / (1.0 - ADAM_B1 ** ADAM_STEP)
    v_hat = v2 / (1.0 - ADAM_B2 ** ADAM_STEP)
    delta = -ADAM_LR * (m_hat / (jnp.sqrt(v_hat) + ADAM_EPS) + ADAM_WD * w)
    return delta, m2, v2


def _adamw_big(w, g, m, v, name, steps=8):
    r, c = w.shape

    def body(w_ref, g_ref, m_ref, v_ref, d_ref, m2_ref, v2_ref):
        d_ref[...], m2_ref[...], v2_ref[...] = _adamw_math(w_ref[...], g_ref[...], m_ref[...], v_ref[...])

    spec = pl.BlockSpec((r // steps, c), lambda i: (i, 0))
    return pl.pallas_call(
        body, grid=(steps,), in_specs=[spec] * 4, out_specs=[spec] * 3,
        out_shape=[jax.ShapeDtypeStruct((r, c), F32)] * 3,
        compiler_params=_cp(("parallel",), VMEM_LIMIT), name=name)(w, g, m, v)


def _adamw_small(groups, slots):
    n = len(groups)

    def body(*refs):
        slots_ref, g0_ref = refs[4 * n], refs[4 * n + 1 + 3 * n]
        for i in range(n):
            w_ref, g_ref, m_ref, v_ref = refs[4 * i:4 * i + 4]
            d_ref, m2_ref, v2_ref = refs[4 * n + 1 + 3 * i:4 * n + 1 + 3 * i + 3]
            g = g_ref[...]
            if i == 0:
                for dev in range(slots.shape[0]):
                    g = g + slots_ref[dev]
                g0_ref[...] = g
            d_ref[...], m2_ref[...], v2_ref[...] = _adamw_math(w_ref[...], g, m_ref[...], v_ref[...])

    flat = [a for grp in groups for a in grp]
    out_shape = [jax.ShapeDtypeStruct(grp[0].shape, F32) for grp in groups for _ in range(3)]
    out_shape.append(jax.ShapeDtypeStruct(groups[0][0].shape, F32))
    outs = pl.pallas_call(body, out_shape=out_shape, name="adamw_small")(*flat, slots)
    return [tuple(outs[3 * i:3 * i + 3]) for i in range(n)], outs[3 * n]


def kernel(x, mem, norm_mix_g, w_in, conv_w, gm_ln_g, gm_ln_b, gm_ws, gm_bs, w_out, norm_x_g, norm_mem_g, w_q, w_kv, w_xo, norm_final_g, loss_target, m_norm_mix_g, m_w_in, m_conv_w, m_gm_ln_g, m_gm_ln_b, m_gm_ws, m_gm_bs, m_w_out, m_norm_x_g, m_norm_mem_g, m_w_q, m_w_kv, m_w_xo, m_norm_final_g, v_norm_mix_g, v_w_in, v_conv_w, v_gm_ln_g, v_gm_ln_b, v_gm_ws, v_gm_bs, v_w_out, v_norm_x_g, v_norm_mem_g, v_w_q, v_w_kv, v_w_xo, v_norm_final_g):
    t = x.shape[1]
    xi = lax.axis_index("x")
    yi = lax.axis_index("y")
    ci = lax.axis_index("c")
    b_idx = jnp.reshape(2 * xi + yi, (1,)).astype(jnp.int32)
    c_idx = jnp.reshape(ci, (1,)).astype(jnp.int32)

    x2d, mem2d, tgt = x[0], mem[0], loss_target[0]
    big = [w_in[0], w_out[0], w_q[0], w_kv[0], w_xo[0]]
    big_m = [m_w_in[0], m_w_out[0], m_w_q[0], m_w_kv[0], m_w_xo[0]]
    big_v = [v_w_in[0], v_w_out[0], v_w_q[0], v_w_kv[0], v_w_xo[0]]
    g3 = norm_final_g.reshape(1, D)

    def pad8(a):
        return jnp.pad(a, ((0, 8 - a.shape[0]), (0, 0)))

    own_blocks = _cast_shards(b_idx, big)

    tril = jnp.tril(jnp.ones((CH, CH), bool))
    wc32 = jnp.where(tril[None], gm_ws[0], 0.0)
    wc = wc32.astype(BF16)
    wct = jnp.swapaxes(wc32, 1, 2).astype(BF16)
    bsb = jnp.broadcast_to(gm_bs[0][:, :, None], (HEADS, CH, CH))

    blk = 2 * xi + yi
    order = jnp.stack([blk, blk ^ 2, blk ^ 1, blk ^ 3]).astype(jnp.int32)
    proj, ht, win_f, cw8, (wout_f, wkv_f) = _proj_gather(
        order, x2d, norm_mix_g, own_blocks[0], pad8(conv_w[0]), [own_blocks[1], own_blocks[3]])
    mixin, mixt, (wq_f, wxo_f) = _mixer_fwd(proj, cw8, gm_ln_g, gm_ln_b, wc, bsb, [own_blocks[2], own_blocks[4]])
    wout2, wq2, wxo2 = wout_f.reshape(MIX, D), wq_f.reshape(D, D), wxo_f.reshape(D, D)
    k, v, mt = _mem_fwd(mem2d, norm_mem_g, wkv_f)
    del mt

    (loss_tile, dmix, dx1b, h2t, dq, ot, dx2b, dk, dv, dg2, dg3) = _tail(
        x2d, tgt, mixin, wout2, wq2, wxo2, k, v, norm_x_g, g3)
    dwkv, dwkv_b, dgm = _mem_bwd(mem2d, norm_mem_g, dk, dv, wkv_f)
    dwxo, dwxo_b = _grad_matmul(ot, dx2b, by_cols=False, name="grad_w_xo")
    dwq, dwq_b = _grad_matmul(h2t, dq, by_cols=False, name="grad_w_q")
    dwout, dwout_b = _grad_matmul(mixt, dx1b, by_cols=False, name="grad_w_out")

    rx1a = _pair_exchange([dwout_b, dwq_b, dwkv_b, dwxo_b], [], "pair_exchange_a")
    ps_a = _pair_sum(c_idx, [dwout, dwq, dwkv, dwxo], rx1a, [], [], "pair_sum_a")
    sums_a, sums_a_b = ps_a[:4], ps_a[4:8]
    (dproj, dcw, dlng, dlnb, dwc, dbs8), rx2a = _mixer_bwd(proj, dmix, cw8, gm_ln_g, gm_ln_b, wc, wct, bsb, sums_a_b)

    dwin, dwin_b = _grad_matmul(ht, dproj, by_cols=True, name="grad_w_in")
    zero = jnp.zeros((1, D), F32)
    loss_row = jnp.broadcast_to(loss_tile[0:1, 0:1], (1, D))
    sv = jnp.concatenate([zero, dg2, dgm, dg3, dlng, dlnb, dbs8[0:1], loss_row, dcw], axis=0)
    sw = dwc.reshape(HEADS * CH, CH)
    rx1b = _pair_exchange([dwin_b], [sv, sw], "pair_exchange_b")
    ps_b = _pair_sum(c_idx, [dwin], rx1b[:1], [sv, sw], rx1b[1:], "pair_sum_b")
    sums_b, sums_b_b, psmall = ps_b[:1], ps_b[1:2], ps_b[2:]
    send, recv, src_thru, land_thru, token = _exchange_begin(sums_b_b, psmall, "exchange_b_begin")
    grad_x, dg1 = _input_grad(dproj, win_f, x2d, dx1b, norm_mix_g, token)
    rx2b = _exchange_end(send, recv, src_thru, land_thru, 1, dg1, "exchange_b_end")

    red = _chip_sum(jnp.concatenate([b_idx, c_idx]), list(sums_b) + list(sums_a), list(rx2b[:1]) + list(rx2a),
                    list(rx2b[1:]), list(psmall))
    (gwin, gwout, gwq, gwkv, gwxo, svf, swf), dg1_slots = _pair_gather(red, dg1)

    names = ["w_in", "w_out", "w_q", "w_kv", "w_xo"]
    big_out = [_adamw_big(w, g, m, v, "adamw_" + nm)
               for w, g, m, v, nm in zip(big, [gwin, gwout, gwq, gwkv, gwxo], big_m, big_v, names)]

    def vec_pack(a1, a2, am, a3, lg, lb, bs):
        return jnp.concatenate([a1, a2, am, a3.reshape(1, D), lg, lb, bs.reshape(1, D), zero], axis=0)

    wv = vec_pack(norm_mix_g, norm_x_g, norm_mem_g, norm_final_g, gm_ln_g, gm_ln_b, gm_bs)
    mv = vec_pack(m_norm_mix_g, m_norm_x_g, m_norm_mem_g, m_norm_final_g, m_gm_ln_g, m_gm_ln_b, m_gm_bs)
    vv = vec_pack(v_norm_mix_g, v_norm_x_g, v_norm_mem_g, v_norm_final_g, v_gm_ln_g, v_gm_ln_b, v_gm_bs)
    loss = svf[7, 0]
    gcw = lax.dynamic_slice_in_dim(svf[8:16], blk * (D // N_CHIP), D // N_CHIP, axis=1)
    gws = swf
    ((dv_, mv_, vv_), (dc_, mc_, vc_), (dws_, mws_, vws_)), gv = _adamw_small([
        (wv, svf[0:8], mv, vv),
        (pad8(conv_w[0]), gcw, pad8(m_conv_w[0]), pad8(v_conv_w[0])),
        (gm_ws.reshape(HEADS * CH, CH), gws, m_gm_ws.reshape(HEADS * CH, CH), v_gm_ws.reshape(HEADS * CH, CH))],
        dg1_slots)

    def unpack(vecs, cw, ws, bigs):
        r = lambda i: vecs[i:i + 1]
        return [r(0), bigs[0][None], cw[0:3][None], r(4), r(5), ws.reshape(1, HEADS, CH, CH), vecs[6].reshape(1, HEADS, CH),
                bigs[1][None], r(1), r(2), bigs[2][None], bigs[3][None], bigs[4][None], vecs[3]]

    grads_out = unpack(gv, gcw, gws, [gwin, gwout, gwq, gwkv, gwxo])
    delta_out = unpack(dv_, dc_, dws_, [o[0] for o in big_out])
    m_out = unpack(mv_, mc_, mws_, [o[1] for o in big_out])
    v_out = unpack(vv_, vc_, vws_, [o[2] for o in big_out])
    return (loss, grad_x[None], *grads_out, *delta_out, *m_out, *v_out)
```

```python
import functools
import math

import jax
import jax.numpy as jnp
from jax import lax
from jax.experimental import pallas as pl
from jax.experimental.pallas import tpu as pltpu

F32 = jnp.float32
BF16 = jnp.bfloat16
MESH = pl.DeviceIdType.MESH

D = 1024
SLAB = 1024
N_SLAB = 7
IN_DIM = N_SLAB * SLAB
MIX = 2 * SLAB
HEADS = 8
CH = 128
XH = 4
XD = D // XH
EPS = 1e-6
GELU_C = math.sqrt(2.0 / math.pi)
GELU_A = 0.044715
N_CHIP = 4
IN_BLK = IN_DIM // N_CHIP
KV_BLK = 2 * D // N_CHIP

ADAM_LR, ADAM_B1, ADAM_B2, ADAM_EPS, ADAM_WD, ADAM_STEP = 0.001, 0.9, 0.999, 1e-08, 0.01, 10

VMEM_LIMIT = 60 * 1024 * 1024


def _cp(sem=None, vmem=None):
    return pltpu.CompilerParams(dimension_semantics=sem, vmem_limit_bytes=vmem)


def _full(shape, buffers=None):
    n = len(shape)
    if buffers is None:
        return pl.BlockSpec(shape, lambda *_: (0,) * n)
    return pl.BlockSpec(shape, lambda *_: (0,) * n, pipeline_mode=pl.Buffered(buffers))


ANY = pl.BlockSpec(memory_space=pl.ANY)


def _bdot(a, b):
    return jnp.dot(a.astype(BF16), b.astype(BF16), preferred_element_type=F32)


def _bdot_nt(a, b):
    return lax.dot_general(a.astype(BF16), b.astype(BF16), (((1,), (1,)), ((), ())), preferred_element_type=F32)


def _rms(x, g):
    r = lax.rsqrt(jnp.mean(x * x, axis=-1, keepdims=True) + EPS)
    return x * r * g, r


def _rms_bwd(dy, x, r, g):
    gdy = dy * g
    dx = r * gdy - x * (r * r * r) * jnp.mean(x * gdy, axis=-1, keepdims=True)
    dg = jnp.sum(dy * x * r, axis=0, keepdims=True)
    return dx, dg


def _gelu_parts(x):
    x2 = x * x
    t = jnp.tanh(GELU_C * (x + GELU_A * x * x2))
    val = 0.5 * x * (1.0 + t)
    grad = 0.5 * (1.0 + t) + 0.5 * x * (1.0 - t * t) * (GELU_C * (1.0 + 3.0 * GELU_A * x2))
    return val, grad


def _gelu(x):
    return 0.5 * x * (1.0 + jnp.tanh(GELU_C * (x + GELU_A * x * x * x)))


def _sigmoid(z):
    return 1.0 / (1.0 + jnp.exp(-z))


def _cast_shards(b_idx, arrs):
    n = len(arrs)
    steps = 8

    def body(b_ref, *refs):
        for i in range(n):
            refs[n + i][...] = refs[i][...].astype(BF16)

    in_specs = [pl.BlockSpec((a.shape[0] // steps, a.shape[1]), lambda i, b: (i, 0)) for a in arrs]
    out_specs = [pl.BlockSpec((None, a.shape[0] // steps, a.shape[1]), lambda i, b: (b[0], i, 0)) for a in arrs]
    return pl.pallas_call(
        body, out_shape=[jax.ShapeDtypeStruct((N_CHIP,) + a.shape, BF16) for a in arrs],
        grid_spec=pltpu.PrefetchScalarGridSpec(num_scalar_prefetch=1, grid=(steps,), in_specs=in_specs, out_specs=out_specs),
        compiler_params=_cp(("arbitrary",)), name="cast_shards")(b_idx, *arrs)


def _proj_gather(order, x, g, win_own, cw8s, more, tm=1024):
    t = x.shape[0]
    ni = t // tm
    hr = D // 2
    nm = len(more)

    def body(*refs):
        order_ref, x_ref, g_ref, win_in, cw_in = refs[:5]
        o_ref, ht_ref, win_f, cw_out = refs[5 + nm:9 + nm]
        more_out = refs[9 + nm:9 + 2 * nm]
        hbuf, wv, ici_s, ici_r, d2d_s, d2d_r, cw_s, cw_r, loc, m_is, m_ir, m_ds, m_dr = refs[9 + 2 * nm:]
        more_start, more_finish = _gather_steps(more_out, m_is, m_ir, m_ds, m_dr)
        j, i = pl.program_id(0), pl.program_id(1)
        x, y, c, chips = _coords()
        b = 2 * x + y
        sib = (x, y, 1 - c)
        blks = [2 * chip[0] + chip[1] for chip in chips]

        def half(blk, hc):
            return win_f.at[blk, pl.ds(hc * hr, hr)]

        def cw_cols(blk):
            return cw_out.at[:, pl.ds(blk * (D // N_CHIP), D // N_CHIP)]

        def ici(k, blk):
            return pltpu.make_async_remote_copy(src_ref=half(blk, c), dst_ref=half(blk, c), send_sem=ici_s.at[k],
                                                recv_sem=ici_r.at[k], device_id=(*chips[k], c), device_id_type=MESH)

        def relay(k, blk, hc):
            return pltpu.make_async_remote_copy(src_ref=half(blk, hc), dst_ref=half(blk, hc), send_sem=d2d_s.at[k],
                                                recv_sem=d2d_r.at[k], device_id=sib, device_id_type=MESH)

        def cw_copy(k, blk):
            src = cw_in if blk is None else cw_cols(blk)
            return pltpu.make_async_remote_copy(src_ref=src, dst_ref=cw_cols(b if blk is None else blk), send_sem=cw_s.at[k],
                                                recv_sem=cw_r.at[k], device_id=(*chips[k], c), device_id_type=MESH)

        cw_local = pltpu.make_async_copy(cw_in, cw_cols(b), loc.at[1])

        @pl.when((j == 0) & (i == 0))
        def _():
            cw_local.start()
            for k in range(3):
                ici(k, b).start()
            for k in range(3):
                cw_copy(k, None).start()
            more_start()

        for jj in range(N_CHIP):
            @pl.when((j == jj) & (i == 0))
            def _(jj=jj):
                if jj == 0:
                    blk = b
                else:
                    blk = blks[jj - 1]
                    ici(jj - 1, blk).wait_recv()
                    relay(jj - 1, blk, c).start()
                    relay(jj - 1, blk, 1 - c).wait_recv()
                load = pltpu.make_async_copy(win_f.at[blk], wv, loc.at[0])
                load.start()
                load.wait()

        rows = pl.ds(pl.multiple_of(i * tm, tm), tm)

        @pl.when(j == 0)
        def _():
            h, _ = _rms(x_ref[...], g_ref[...])
            hbuf[rows, :] = h.astype(BF16)
            ht_ref[...] = h.T.astype(BF16)

        o_ref[...] = jnp.dot(hbuf[rows, :], wv[...], preferred_element_type=F32).astype(BF16)

        @pl.when((j == N_CHIP - 1) & (i == ni - 1))
        def _():
            for k in range(3):
                cw_copy(k, blks[k]).wait_recv()
            for k in range(3):
                ici(k, b).wait_send()
                relay(k, blks[k], c).wait_send()
                cw_copy(k, None).wait_send()
            cw_local.wait()
            more_finish()

    first = lambda j, i: jnp.where(j == 0, i, ni - 1)
    in_specs = [pl.BlockSpec((tm, D), lambda j, i, o: (first(j, i), 0)), pl.BlockSpec((1, D), lambda j, i, o: (0, 0)),
                ANY, ANY] + [ANY] * nm
    out_specs = [pl.BlockSpec((tm, IN_BLK), lambda j, i, o: (i, o[j])),
                 pl.BlockSpec((D, tm), lambda j, i, o: (0, first(j, i))), ANY, ANY] + [ANY] * nm
    outs = pl.pallas_call(
        body, out_shape=[jax.ShapeDtypeStruct((t, IN_DIM), BF16), jax.ShapeDtypeStruct((D, t), BF16),
                         jax.ShapeDtypeStruct(win_own.shape, BF16), jax.ShapeDtypeStruct((8, D), F32)]
        + [jax.ShapeDtypeStruct(f.shape, f.dtype) for f in more],
        grid_spec=pltpu.PrefetchScalarGridSpec(
            num_scalar_prefetch=1, grid=(N_CHIP, ni), in_specs=in_specs, out_specs=out_specs,
            scratch_shapes=[pltpu.VMEM((t, D), BF16), pltpu.VMEM((D, IN_BLK), BF16)]
            + [pltpu.SemaphoreType.DMA((3,))] * 6 + [pltpu.SemaphoreType.DMA((2,))]
            + [pltpu.SemaphoreType.DMA((max(nm, 1), 3))] * 4),
        input_output_aliases={3: 2, **{5 + w: 4 + w for w in range(nm)}},
        compiler_params=_cp(("arbitrary", "arbitrary"), VMEM_LIMIT), name="proj_gather")(order, x, g, win_own, cw8s, *more)
    return outs[0], outs[1], outs[2], outs[3], outs[4:]


def _gather_steps(outs, ici_s, ici_r, d2d_s, d2d_r):
    x, y, c, chips = _coords()
    b = 2 * x + y
    sib = (x, y, 1 - c)
    nw = len(outs)

    def half(w, blk, hc):
        hr = outs[w].shape[1] // 2
        return outs[w].at[blk, pl.ds(hc * hr, hr)]

    def ici(w, k, blk):
        return pltpu.make_async_remote_copy(src_ref=half(w, blk, c), dst_ref=half(w, blk, c), send_sem=ici_s.at[w, k],
                                            recv_sem=ici_r.at[w, k], device_id=(*chips[k], c), device_id_type=MESH)

    def relay(w, k, blk, hc):
        return pltpu.make_async_remote_copy(src_ref=half(w, blk, hc), dst_ref=half(w, blk, hc), send_sem=d2d_s.at[w, k],
                                            recv_sem=d2d_r.at[w, k], device_id=sib, device_id_type=MESH)

    def start():
        for w in range(nw):
            for k in range(3):
                ici(w, k, b).start()

    def finish():
        for w in range(nw):
            for k in range(3):
                blk = 2 * chips[k][0] + chips[k][1]
                ici(w, k, blk).wait_recv()
                relay(w, k, blk, c).start()
        for w in range(nw):
            for k in range(3):
                blk = 2 * chips[k][0] + chips[k][1]
                relay(w, k, blk, 1 - c).wait_recv()
        for w in range(nw):
            for k in range(3):
                blk = 2 * chips[k][0] + chips[k][1]
                ici(w, k, b).wait_send()
                relay(w, k, blk, c).wait_send()

    return start, finish


def _mixer_fwd(proj, cw8, lng, lnb, wc, bsb, fulls, tm=256):
    t = proj.shape[0]
    nt = t // tm
    nch = tm // CH
    nw = len(fulls)

    def body(*refs):
        p_ref, cw_ref, lng_ref, lnb_ref, wc_ref, bsb_ref = refs[:6]
        mix_ref, mixt_ref = refs[6 + nw:8 + nw]
        w_outs = refs[8 + nw:8 + 2 * nw]
        prev_ref, stage_ref, ici_s, ici_r, d2d_s, d2d_r = refs[8 + 2 * nw:]
        gather_start, gather_finish = _gather_steps(w_outs, ici_s, ici_r, d2d_s, d2d_r)

        @pl.when(pl.program_id(0) == 0)
        def _():
            gather_start()
            prev_ref[...] = jnp.zeros_like(prev_ref)

        rows = lax.broadcasted_iota(jnp.int32, (tm, CH), 0)
        for s in range(HEADS):
            cs = pl.ds(CH * s, CH)

            def slab(k):
                return p_ref[:, pl.ds(k * SLAB + CH * s, CH)].astype(F32)

            gb, gc, xa, za = slab(0), slab(1), slab(2), slab(3)
            cx = gc * xa
            p6 = jnp.broadcast_to(prev_ref[6:7, cs], (tm, CH))
            p7 = jnp.broadcast_to(prev_ref[7:8, cs], (tm, CH))
            c1 = jnp.where(rows == 0, p7, pltpu.roll(cx, 1, 0))
            c2 = jnp.where(rows == 0, p6, jnp.where(rows == 1, p7, pltpu.roll(cx, 2, 0)))
            prev_ref[:, cs] = cx[tm - 8:, :]
            cv = cw_ref[0:1, cs] * c2 + cw_ref[1:2, cs] * c1 + cw_ref[2:3, cs] * cx
            stage_ref[:, cs] = gb * cv * (za * _sigmoid(za))

            u, v, zb = slab(4), slab(5), slab(6)
            ug, vg = _gelu(u), _gelu(v)
            dlt = vg - jnp.mean(vg, axis=-1, keepdims=True)
            vhat = dlt * lax.rsqrt(jnp.mean(dlt * dlt, axis=-1, keepdims=True) + EPS)
            vn = (vhat * lng_ref[:, cs] + lnb_ref[:, cs]).astype(BF16)
            gate = ug * (zb * _sigmoid(zb))
            for c in range(nch):
                rs = slice(CH * c, CH * (c + 1))
                sp = jnp.dot(wc_ref[s], vn[rs], preferred_element_type=F32) + bsb_ref[s]
                stage_ref[rs, pl.ds(SLAB + CH * s, CH)] = gate[rs] * sp

        full = stage_ref[...]
        mix_ref[...] = full.astype(BF16)
        mixt_ref[...] = full.T.astype(BF16)

        @pl.when(pl.program_id(0) == nt - 1)
        def _():
            gather_finish()

    sems = [pltpu.SemaphoreType.DMA((nw, 3))] * 4
    outs = pl.pallas_call(
        body, grid=(nt,),
        in_specs=[pl.BlockSpec((tm, IN_DIM), lambda i: (i, 0)), _full((8, D)), _full((1, D)), _full((1, D)),
                  _full((HEADS, CH, CH)), _full((HEADS, CH, CH))] + [ANY] * nw,
        out_specs=[pl.BlockSpec((tm, MIX), lambda i: (i, 0)), pl.BlockSpec((MIX, tm), lambda i: (0, i))] + [ANY] * nw,
        out_shape=[jax.ShapeDtypeStruct((t, MIX), BF16), jax.ShapeDtypeStruct((MIX, t), BF16)]
        + [jax.ShapeDtypeStruct(f.shape, f.dtype) for f in fulls],
        input_output_aliases={6 + w: 2 + w for w in range(nw)},
        scratch_shapes=[pltpu.VMEM((8, D), F32), pltpu.VMEM((tm, MIX), F32)] + sems,
        compiler_params=_cp(("arbitrary",), VMEM_LIMIT), name="mixer_fwd")(proj, cw8, lng, lnb, wc, bsb, *fulls)
    return outs[0], outs[1], outs[2:]


def _mem_fwd(mem, gm, wkv_f):
    n_mem = mem.shape[0]

    def body(mem_ref, gm_ref, w_ref, k_ref, v_ref, mt_ref):
        m, _ = _rms(mem_ref[...], gm_ref[...])
        mb = m.astype(BF16)
        mt_ref[...] = m.T.astype(BF16)
        for j in range(N_CHIP):
            dst = k_ref if j < 2 else v_ref
            dst[:, pl.ds(KV_BLK * (j % 2), KV_BLK)] = jnp.dot(mb, w_ref[j], preferred_element_type=F32).astype(BF16)

    return pl.pallas_call(
        body, out_shape=[jax.ShapeDtypeStruct((n_mem, D), BF16), jax.ShapeDtypeStruct((n_mem, D), BF16),
                         jax.ShapeDtypeStruct((D, n_mem), BF16)],
        compiler_params=_cp(None, VMEM_LIMIT), name="mem_fwd")(mem, gm, wkv_f)


def _tail(x, tgt, mixin, wout, wq, wxo, k, v, g2, g3, tm=512, sub=512):
    t = x.shape[0]
    n_mem = k.shape[0]
    scale = 1.0 / math.sqrt(XD)

    def body(x_ref, tgt_ref, mix_ref, wout_ref, wq_ref, wxo_ref, k_ref, v_ref, g2_ref, g3_ref,
             loss_ref, dmix_ref, dx1b_ref, h2t_ref, dq_ref, ot_ref, dx2b_ref, dk_ref, dv_ref, dg2_ref, dg3_ref):
        @pl.when(pl.program_id(0) == 0)
        def _():
            loss_ref[...] = jnp.zeros_like(loss_ref)
            dk_ref[...] = jnp.zeros_like(dk_ref)
            dv_ref[...] = jnp.zeros_like(dv_ref)
            dg2_ref[...] = jnp.zeros_like(dg2_ref)
            dg3_ref[...] = jnp.zeros_like(dg3_ref)

        g2, g3 = g2_ref[...], g3_ref[...]
        for sb in range(tm // sub):
            rs = pl.ds(sub * sb, sub)
            x1 = x_ref[rs, :] + jnp.dot(mix_ref[rs, :], wout_ref[...], preferred_element_type=F32)
            h2, r2 = _rms(x1, g2)
            h2t_ref[:, rs] = h2.T.astype(BF16)
            q = _bdot(h2, wq_ref[...]).astype(BF16)
            probs, outs = [], []
            for hd in range(XH):
                hs = pl.ds(XD * hd, XD)
                s = _bdot_nt(q[:, XD * hd:XD * (hd + 1)], k_ref[:, hs]) * scale
                e = jnp.exp(s - jnp.max(s, axis=-1, keepdims=True))
                p = e / jnp.sum(e, axis=-1, keepdims=True)
                probs.append(p)
                outs.append(_bdot(p, v_ref[:, hs]))
            o = jnp.concatenate(outs, axis=-1)
            ot_ref[:, rs] = o.T.astype(BF16)
            x2 = x1 + _bdot(o, wxo_ref[...])
            y, r3 = _rms(x2, g3)
            diff = y - tgt_ref[rs, :]
            row_loss = jnp.sum(diff * diff, axis=-1, keepdims=True)
            loss_ref[...] += jnp.broadcast_to(jnp.sum(row_loss, axis=0, keepdims=True) * (0.5 / D), loss_ref.shape)

            dx2, dg3 = _rms_bwd(diff * (1.0 / D), x2, r3, g3)
            dg3_ref[...] += dg3
            dx2b = dx2.astype(BF16)
            dx2b_ref[rs, :] = dx2b
            do = _bdot_nt(dx2b, wxo_ref[...])
            dqs = []
            for hd in range(XH):
                hs = pl.ds(XD * hd, XD)
                p = probs[hd]
                do_h = do[:, XD * hd:XD * (hd + 1)]
                dv_ref[:, hs] += _bdot(p.T, do_h)
                dp = _bdot_nt(do_h, v_ref[:, hs])
                ds = p * (dp - jnp.sum(dp * p, axis=-1, keepdims=True))
                dqs.append(_bdot(ds, k_ref[:, hs]) * scale)
                dk_ref[:, hs] += _bdot(ds.T, q[:, XD * hd:XD * (hd + 1)]) * scale
            dq = jnp.concatenate(dqs, axis=-1).astype(BF16)
            dq_ref[rs, :] = dq
            dx1n, dg2 = _rms_bwd(_bdot_nt(dq, wq_ref[...]), x1, r2, g2)
            dg2_ref[...] += dg2
            dx1b = (dx2 + dx1n).astype(BF16)
            dx1b_ref[rs, :] = dx1b
            dmix_ref[rs, :] = _bdot_nt(dx1b, wout_ref[...]).astype(BF16)

    tok = lambda w: pl.BlockSpec((tm, w), lambda i: (i, 0))
    tok_t = lambda w: pl.BlockSpec((w, tm), lambda i: (0, i))
    return pl.pallas_call(
        body, grid=(t // tm,),
        in_specs=[tok(D), tok(D), tok(MIX), _full((MIX, D), 1), _full((D, D), 1), _full((D, D), 1),
                  _full((n_mem, D), 1), _full((n_mem, D), 1), _full((1, D)), _full((1, D))],
        out_specs=[_full((8, 128)), tok(MIX), tok(D), tok_t(D), tok(D), tok_t(D), tok(D),
                   _full((n_mem, D)), _full((n_mem, D)), _full((1, D)), _full((1, D))],
        out_shape=[jax.ShapeDtypeStruct((8, 128), F32), jax.ShapeDtypeStruct((t, MIX), BF16),
                   jax.ShapeDtypeStruct((t, D), BF16),
                   jax.ShapeDtypeStruct((D, t), BF16), jax.ShapeDtypeStruct((t, D), BF16),
                   jax.ShapeDtypeStruct((D, t), BF16), jax.ShapeDtypeStruct((t, D), BF16),
                   jax.ShapeDtypeStruct((n_mem, D), F32), jax.ShapeDtypeStruct((n_mem, D), F32),
                   jax.ShapeDtypeStruct((1, D), F32), jax.ShapeDtypeStruct((1, D), F32)],
        compiler_params=_cp(("arbitrary",), VMEM_LIMIT), name="tail")(x, tgt, mixin, wout, wq, wxo, k, v, g2, g3)


def _mem_bwd(mem, gm, dk, dv, wkv_f):
    def body(mem_ref, gm_ref, dk_ref, dv_ref, w_ref, dw_ref, dwb_ref, dgm_ref):
        mem_v = mem_ref[...]
        m, rm = _rms(mem_v, gm_ref[...])
        mt = m.T.astype(BF16)
        dm = jnp.zeros_like(mem_v)
        for j in range(N_CHIP):
            src = dk_ref if j < 2 else dv_ref
            dkv = src[:, pl.ds(KV_BLK * (j % 2), KV_BLK)].astype(BF16)
            dw = jnp.dot(mt, dkv, preferred_element_type=F32)
            dw_ref[j] = dw
            dwb_ref[j] = dw.astype(BF16)
            dm = dm + _bdot_nt(dkv, w_ref[j])
        dgm_ref[...] = jnp.sum(dm * mem_v * rm, axis=0, keepdims=True)

    return pl.pallas_call(
        body, out_shape=[jax.ShapeDtypeStruct((N_CHIP, D, KV_BLK), F32), jax.ShapeDtypeStruct((N_CHIP, D, KV_BLK), BF16),
                         jax.ShapeDtypeStruct((1, D), F32)],
        compiler_params=_cp(None, VMEM_LIMIT), name="mem_bwd")(mem, gm, dk, dv, wkv_f)


def _mixer_bwd(proj, dmix, cw8, lng, lnb, wc, wct, bsb, win_f, x, dx1, g1, after, tm=256):
    t = proj.shape[0]
    nt = t // tm
    nch = tm // CH
    hb = 16
    pair = 2 * CH

    def body(p_ref, pgc_ref, pxa_ref, dm_ref, cw_ref, lng_ref, lnb_ref, wc_ref, wct_ref, bsb_ref, w_ref, x_ref,
             dx1_ref, g1_ref, after_ref, dp_ref, dcw_ref, dlng_ref, dlnb_ref, dwc_ref, dbs_ref, gx_ref, dg1_ref,
             next_ref, dh_ref):
        i = pl.program_id(0)

        @pl.when(i == 0)
        def _():
            next_ref[...] = jnp.zeros_like(next_ref)
            dcw_ref[...] = jnp.zeros_like(dcw_ref)
            dlng_ref[...] = jnp.zeros_like(dlng_ref)
            dlnb_ref[...] = jnp.zeros_like(dlnb_ref)
            dwc_ref[...] = jnp.zeros_like(dwc_ref)
            dbs_ref[...] = jnp.zeros_like(dbs_ref)
            dg1_ref[...] = jnp.zeros_like(dg1_ref)

        first_tile = i == nt - 1
        rows = lax.broadcasted_iota(jnp.int32, (tm, CH), 0)
        ones8 = jnp.ones((8, CH), BF16)
        for s in range(HEADS):
            cs = pl.ds(CH * s, CH)

            def slab(k):
                return p_ref[:, pl.ds(k * SLAB + CH * s, CH)].astype(F32)

            gb, gc, xa, za = slab(0), slab(1), slab(2), slab(3)
            da = dm_ref[:, cs].astype(F32)
            cx = gc * xa
            cxp = pgc_ref[:, cs].astype(F32) * pxa_ref[:, cs].astype(F32)
            cxp = jnp.where(first_tile, jnp.zeros_like(cxp), cxp)
            p6 = jnp.broadcast_to(cxp[hb - 2:hb - 1, :], (tm, CH))
            p7 = jnp.broadcast_to(cxp[hb - 1:hb, :], (tm, CH))
            c1 = jnp.where(rows == 0, p7, pltpu.roll(cx, 1, 0))
            c2 = jnp.where(rows == 0, p6, jnp.where(rows == 1, p7, pltpu.roll(cx, 2, 0)))
            w0, w1, w2 = cw_ref[0:1, cs], cw_ref[1:2, cs], cw_ref[2:3, cs]
            cv = w0 * c2 + w1 * c1 + w2 * cx
            sg = _sigmoid(za)
            sa = za * sg
            dcv = da * gb * sa
            dp_ref[:, pl.ds(0 * SLAB + CH * s, CH)] = (da * cv * sa).astype(BF16)
            dp_ref[:, pl.ds(3 * SLAB + CH * s, CH)] = (da * gb * cv * (sg * (1.0 + za * (1.0 - sg)))).astype(BF16)
            n0 = jnp.broadcast_to(next_ref[0:1, cs], (tm, CH))
            n1 = jnp.broadcast_to(next_ref[1:2, cs], (tm, CH))
            u1 = jnp.where(rows == tm - 1, n0, pltpu.roll(dcv, tm - 1, 0))
            u2 = jnp.where(rows == tm - 2, n0, jnp.where(rows == tm - 1, n1, pltpu.roll(dcv, tm - 2, 0)))
            next_ref[:, cs] = dcv[0:8, :]
            dcx = w2 * dcv + w1 * u1 + w0 * u2
            dp_ref[:, pl.ds(1 * SLAB + CH * s, CH)] = (dcx * xa).astype(BF16)
            dp_ref[:, pl.ds(2 * SLAB + CH * s, CH)] = (dcx * gc).astype(BF16)
            dcw_ref[0:1, cs] += jnp.sum(dcv * c2, axis=0, keepdims=True)
            dcw_ref[1:2, cs] += jnp.sum(dcv * c1, axis=0, keepdims=True)
            dcw_ref[2:3, cs] += jnp.sum(dcv * cx, axis=0, keepdims=True)

            u, v, zb = slab(4), slab(5), slab(6)
            db = dm_ref[:, pl.ds(SLAB + CH * s, CH)].astype(F32)
            ug, ugrad = _gelu_parts(u)
            vg, vgrad = _gelu_parts(v)
            dlt = vg - jnp.mean(vg, axis=-1, keepdims=True)
            rstd = lax.rsqrt(jnp.mean(dlt * dlt, axis=-1, keepdims=True) + EPS)
            vhat = dlt * rstd
            lg = lng_ref[:, cs]
            vn = (vhat * lg + lnb_ref[:, cs]).astype(BF16)
            sgb = _sigmoid(zb)
            szb = zb * sgb
            sps, dvns = [], []
            dbs = jnp.zeros((8, CH), F32)
            dwc = jnp.zeros((CH, CH), F32)
            for c in range(nch):
                rs = slice(CH * c, CH * (c + 1))
                sp = jnp.dot(wc_ref[s], vn[rs], preferred_element_type=F32) + bsb_ref[s]
                dsp = (db[rs] * ug[rs] * szb[rs]).astype(BF16)
                dbs = dbs + lax.dot_general(ones8, dsp, (((1,), (1,)), ((), ())), preferred_element_type=F32)
                dwc = dwc + lax.dot_general(dsp, vn[rs], (((1,), (1,)), ((), ())), preferred_element_type=F32)
                dvns.append(jnp.dot(wct_ref[s], dsp, preferred_element_type=F32))
                sps.append(sp)
            sp = jnp.concatenate(sps, axis=0)
            dvn = jnp.concatenate(dvns, axis=0)
            dbs_ref[:, cs] += dbs
            dwc_ref[s] += dwc
            dlng_ref[:, cs] += jnp.sum(dvn * vhat, axis=0, keepdims=True)
            dlnb_ref[:, cs] += jnp.sum(dvn, axis=0, keepdims=True)
            dvhat = dvn * lg
            dvg = rstd * (dvhat - jnp.mean(dvhat, axis=-1, keepdims=True)
                          - vhat * jnp.mean(dvhat * vhat, axis=-1, keepdims=True))
            dp_ref[:, pl.ds(4 * SLAB + CH * s, CH)] = (db * sp * szb * ugrad).astype(BF16)
            dp_ref[:, pl.ds(5 * SLAB + CH * s, CH)] = (dvg * vgrad).astype(BF16)
            dp_ref[:, pl.ds(6 * SLAB + CH * s, CH)] = (db * ug * sp * (sgb * (1.0 + zb * (1.0 - sgb)))).astype(BF16)

            if s % 2 == 1:
                part = None
                for k in range(N_SLAB):
                    col = k * SLAB + pair * (s // 2)
                    blk, off = divmod(col, IN_BLK)
                    term = lax.dot_general(dp_ref[:, pl.ds(col, pair)], w_ref[blk, :, pl.ds(off, pair)],
                                           (((1,), (1,)), ((), ())), preferred_element_type=F32)
                    part = term if part is None else part + term
                if s == 1:
                    dh_ref[...] = part
                else:
                    dh_ref[...] += part

        xv = x_ref[...]
        r = lax.rsqrt(jnp.mean(xv * xv, axis=-1, keepdims=True) + EPS)
        dxn, dg = _rms_bwd(dh_ref[...], xv, r, g1_ref[...])
        gx_ref[...] = dx1_ref[...].astype(F32) + dxn
        dg1_ref[0:1, :] += dg

        @pl.when(i == nt - 1)
        def _():
            tril = lax.broadcasted_iota(jnp.int32, (CH, CH), 0) >= lax.broadcasted_iota(jnp.int32, (CH, CH), 1)
            for s in range(HEADS):
                dwc_ref[s] = jnp.where(tril, dwc_ref[s], 0.0)

    rev = lambda i: nt - 1 - i
    halo = lambda col: pl.BlockSpec((hb, SLAB), lambda i: (jnp.maximum(rev(i) * (tm // hb) - 1, 0), col))
    tok = lambda w: pl.BlockSpec((tm, w), lambda i: (rev(i), 0))
    return pl.pallas_call(
        body, grid=(nt,),
        in_specs=[tok(IN_DIM), halo(1), halo(2), tok(MIX), _full((8, D)), _full((1, D)), _full((1, D)),
                  _full((HEADS, CH, CH)), _full((HEADS, CH, CH)), _full((HEADS, CH, CH)),
                  _full((N_CHIP, D, IN_BLK), 1), tok(D), tok(D), _full((1, D)), ANY],
        out_specs=[tok(IN_DIM), _full((8, D)), _full((1, D)), _full((1, D)), _full((HEADS, CH, CH)), _full((8, D)),
                   tok(D), _full((8, D))],
        out_shape=[jax.ShapeDtypeStruct((t, IN_DIM), BF16), jax.ShapeDtypeStruct((8, D), F32),
                   jax.ShapeDtypeStruct((1, D), F32), jax.ShapeDtypeStruct((1, D), F32),
                   jax.ShapeDtypeStruct((HEADS, CH, CH), F32), jax.ShapeDtypeStruct((8, D), F32),
                   jax.ShapeDtypeStruct((t, D), F32), jax.ShapeDtypeStruct((8, D), F32)],
        scratch_shapes=[pltpu.VMEM((8, D), F32), pltpu.VMEM((tm, D), F32)],
        compiler_params=_cp(("arbitrary",), VMEM_LIMIT), name="mixer_bwd")(
            proj, proj, proj, dmix, cw8, lng, lnb, wc, wct, bsb, win_f, x, dx1, g1, after)


def _grad_matmul(at, b, *, by_cols, name, tk=1024):
    m, t = at.shape
    n = b.shape[1]
    nk = t // tk
    nj = N_CHIP if by_cols else 1
    bn = n // nj

    def body(a_ref, b_ref, o_ref, ob_ref):
        kk = pl.program_id(1)
        part = jnp.dot(a_ref[...], b_ref[...], preferred_element_type=F32)

        @pl.when(kk == 0)
        def _():
            o_ref[...] = part

        @pl.when(kk > 0)
        def _():
            o_ref[...] += part

        @pl.when(kk == nk - 1)
        def _():
            ob_ref[...] = o_ref[...].astype(BF16)

    a_spec = pl.BlockSpec((m, tk), lambda j, k: (0, k))
    b_spec = pl.BlockSpec((tk, bn), lambda j, k: (k, j))
    o_spec = pl.BlockSpec((None, m, bn), lambda j, k: (j, 0, 0))
    o32, o16 = pl.pallas_call(
        body, grid=(nj, nk), in_specs=[a_spec, b_spec], out_specs=[o_spec, o_spec],
        out_shape=[jax.ShapeDtypeStruct((nj, m, bn), F32), jax.ShapeDtypeStruct((nj, m, bn), BF16)],
        compiler_params=_cp(("parallel", "arbitrary"), VMEM_LIMIT), name=name)(at, b)
    if by_cols:
        return o32, o16
    return o32.reshape(N_CHIP, m // N_CHIP, n), o16.reshape(N_CHIP, m // N_CHIP, n)


def _coords():
    x, y, c = lax.axis_index("x"), lax.axis_index("y"), lax.axis_index("c")
    chips = [(1 - x, y), (x, 1 - y), (1 - x, 1 - y)]
    return x, y, c, chips


def _pair_exchange(grads_b, smalls, name):
    ng, ns = len(grads_b), len(smalls)
    n = ng + ns

    def body(*refs):
        ins, outs, send, recv = refs[:n], refs[n:2 * n], refs[2 * n], refs[2 * n + 1]
        x, y, c, _ = _coords()
        cps = []
        for i in range(n):
            if i < ng:
                hr = ins[i].shape[1] // 2
                src = ins[i].at[pl.ds(0, N_CHIP), pl.ds((1 - c) * hr, hr)]
            else:
                hr = ins[i].shape[0] // 2
                src = ins[i].at[pl.ds((1 - c) * hr, hr)]
            cps.append(pltpu.make_async_remote_copy(
                src_ref=src, dst_ref=outs[i], send_sem=send.at[i], recv_sem=recv.at[i],
                device_id=(x, y, 1 - c), device_id_type=MESH))
        for cp in cps:
            cp.start()
        for cp in cps:
            cp.wait()

    out_shape = [jax.ShapeDtypeStruct((N_CHIP, g.shape[1] // 2, g.shape[2]), g.dtype) for g in grads_b]
    out_shape += [jax.ShapeDtypeStruct((s.shape[0] // 2, s.shape[1]), s.dtype) for s in smalls]
    return pl.pallas_call(
        body, out_shape=out_shape, in_specs=[ANY] * n, out_specs=[ANY] * n,
        scratch_shapes=[pltpu.SemaphoreType.DMA((n,)), pltpu.SemaphoreType.DMA((n,))],
        name=name)(*grads_b, *smalls)


def _pair_sum(c_idx, grads, recvd, smalls, smalls_recvd, name):
    ng, ns = len(grads), len(smalls)
    halves = [g.shape[1] // 2 for g in grads]

    def body(c_ref, *refs):
        g_in, r_in = refs[:ng], refs[ng:2 * ng]
        s_in, sr_in = refs[2 * ng:2 * ng + ns], refs[2 * ng + ns:2 * ng + 2 * ns]
        o = refs[2 * ng + 2 * ns:]
        for i in range(ng):
            tot = g_in[i][...] + r_in[i][...].astype(F32)
            o[i][...] = tot
            o[ng + i][...] = tot.astype(BF16)
        for i in range(ns):
            o[2 * ng + i][...] = s_in[i][...] + sr_in[i][...]

    in_specs = [pl.BlockSpec((None, None, halves[i], g.shape[2]), lambda b, c: (b, c[0], 0, 0)) for i, g in enumerate(grads)]
    in_specs += [pl.BlockSpec((None, halves[i], g.shape[2]), lambda b, c: (b, 0, 0)) for i, g in enumerate(grads)]
    in_specs += [pl.BlockSpec((None, s.shape[0] // 2, s.shape[1]), lambda b, c: (c[0], 0, 0)) for s in smalls]
    in_specs += [pl.BlockSpec((s.shape[0] // 2, s.shape[1]), lambda b, c: (0, 0)) for s in smalls]
    blk = [pl.BlockSpec((None, halves[i], g.shape[2]), lambda b, c: (b, 0, 0)) for i, g in enumerate(grads)]
    out_specs = blk + blk + [pl.BlockSpec((s.shape[0] // 2, s.shape[1]), lambda b, c: (0, 0)) for s in smalls]
    out_shape = [jax.ShapeDtypeStruct((N_CHIP, halves[i], g.shape[2]), F32) for i, g in enumerate(grads)]
    out_shape += [jax.ShapeDtypeStruct((N_CHIP, halves[i], g.shape[2]), BF16) for i, g in enumerate(grads)]
    out_shape += [jax.ShapeDtypeStruct((s.shape[0] // 2, s.shape[1]), F32) for s in smalls]
    grads4 = [g.reshape(N_CHIP, 2, halves[i], g.shape[2]) for i, g in enumerate(grads)]
    smalls3 = [s.reshape(2, s.shape[0] // 2, s.shape[1]) for s in smalls]
    return pl.pallas_call(
        body, out_shape=out_shape,
        grid_spec=pltpu.PrefetchScalarGridSpec(num_scalar_prefetch=1, grid=(N_CHIP,), in_specs=in_specs, out_specs=out_specs),
        compiler_params=_cp(("arbitrary",), VMEM_LIMIT), name=name)(c_idx, *grads4, *recvd, *smalls3, *smalls_recvd)


_HBM = pl.BlockSpec(memory_space=pltpu.HBM)
_SEM = pl.BlockSpec(memory_space=pltpu.SEMAPHORE)


def _split_copies(ins, lands, ng, send, recv, arriving):
    x, y, c, chips = _coords()
    b = 2 * x + y
    copies = []
    for i in range(len(ins)):
        for k in range(3):
            blk = 2 * chips[k][0] + chips[k][1]
            src, dst, got = (ins[i].at[blk], lands[i].at[k], lands[i].at[k]) if i < ng else (ins[i], lands[i].at[b], lands[i].at[blk])
            sems = dict(send_sem=send.at[3 * i + k], recv_sem=recv.at[3 * i + k], device_id=(*chips[k], c), device_id_type=MESH)
            if arriving:
                copies.append(pltpu.make_async_remote_copy(src_ref=got, dst_ref=got, **sems))
            else:
                copies.append(pltpu.make_async_remote_copy(src_ref=src, dst_ref=dst, **sems))
    return copies


def _exchange_begin(sums_b, smalls, name):
    ng, n = len(sums_b), len(sums_b) + len(smalls)
    srcs = list(sums_b) + list(smalls)
    lands = [lax.empty((3,) + g.shape[1:], g.dtype) for g in sums_b] + [lax.empty((N_CHIP,) + s.shape, s.dtype) for s in smalls]

    def body(*refs):
        ins, land_refs = refs[:n], refs[n:2 * n]
        send, recv = refs[2 * n], refs[2 * n + 1]
        token = refs[4 * n + 2]
        for cp in _split_copies(ins, land_refs, ng, send, recv, False):
            cp.start()
        token[...] = jnp.zeros_like(token)

    hbm = lambda a: pltpu.HBM(a.shape, a.dtype)
    outs = pl.pallas_call(
        body, name=name,
        out_shape=(pltpu.SemaphoreType.DMA((3 * n,)), pltpu.SemaphoreType.DMA((3 * n,)), *[hbm(a) for a in srcs + lands],
                   jax.ShapeDtypeStruct((8, 128), F32)),
        in_specs=[_HBM] * (2 * n), out_specs=(_SEM, _SEM, *[_HBM] * (2 * n), pl.BlockSpec(memory_space=pltpu.VMEM)),
        input_output_aliases={i: 2 + i for i in range(2 * n)},
        compiler_params=pltpu.CompilerParams(has_side_effects=pltpu.SideEffectType.DATAFLOW_SIDE_EFFECTING),
    )(*[pltpu.with_memory_space_constraint(a, pltpu.HBM) for a in srcs + lands])
    return outs[0], outs[1], list(outs[2:2 + n]), list(outs[2 + n:2 + 2 * n]), outs[2 + 2 * n]


def _exchange_end(send, recv, srcs, lands, ng, after, name):
    n = len(srcs)

    def body(*refs):
        ins, land_refs = refs[:n], refs[n:2 * n]
        send_ref, recv_ref = refs[2 * n], refs[2 * n + 1]
        for cp in _split_copies(ins, land_refs, ng, send_ref, recv_ref, False):
            cp.wait_send()
        for cp in _split_copies(ins, land_refs, ng, send_ref, recv_ref, True):
            cp.wait_recv()

    hbm = lambda a: pltpu.HBM(a.shape, a.dtype)
    outs = pl.pallas_call(
        body, name=name, out_shape=tuple(hbm(a) for a in list(srcs) + list(lands)),
        in_specs=[_HBM] * (2 * n) + [_SEM, _SEM, ANY], out_specs=tuple([_HBM] * (2 * n)),
        input_output_aliases={i: i for i in range(2 * n)},
        compiler_params=pltpu.CompilerParams(has_side_effects=pltpu.SideEffectType.DATAFLOW_SIDE_EFFECTING),
    )(*srcs, *lands, send, recv, after)
    return list(outs[n:])


def _chip_sum(bc_idx, sums, recvd, smalls_slots, smalls_own, name, steps=4):
    ng, ns = len(sums), len(smalls_slots)

    def body(bc_ref, *refs):
        own, rx = refs[:ng], refs[ng:2 * ng]
        sl = refs[2 * ng:2 * ng + ns]
        sl_own = refs[2 * ng + ns:2 * ng + 2 * ns]
        o = refs[2 * ng + 2 * ns:]
        for i in range(ng):
            tot = own[i][...]
            for j in range(3):
                tot = tot + rx[i][j].astype(F32)
            o[i][...] = tot
        for i in range(ns):
            term = [jnp.where(bc_ref[0] == kk, sl_own[i][...], sl[i][kk]) for kk in range(N_CHIP)]
            o[ng + i][...] = ((term[0] + term[1]) + term[2]) + term[3]

    def rows(g):
        return g.shape[1] // steps

    in_specs = [pl.BlockSpec((None, rows(g), g.shape[2]), lambda r, bc: (bc[0], r, 0)) for g in sums]
    in_specs += [pl.BlockSpec((3, rows(g), g.shape[2]), lambda r, bc: (0, r, 0)) for g in sums]
    in_specs += [pl.BlockSpec(s.shape, lambda r, bc: (0, 0, 0)) for s in smalls_slots]
    in_specs += [pl.BlockSpec(s.shape[1:], lambda r, bc: (0, 0)) for s in smalls_slots]
    out_specs = [pl.BlockSpec((rows(g), g.shape[2]), lambda r, bc: (bc[1] * steps + r, 0)) for g in sums]
    out_specs += [pl.BlockSpec(s.shape[1:], lambda r, bc: (bc[1], 0)) for s in smalls_slots]
    out_shape = [jax.ShapeDtypeStruct((2 * g.shape[1], g.shape[2]), F32) for g in sums]
    out_shape += [jax.ShapeDtypeStruct((2 * s.shape[1], s.shape[2]), F32) for s in smalls_slots]
    return pl.pallas_call(
        body, out_shape=out_shape,
        grid_spec=pltpu.PrefetchScalarGridSpec(num_scalar_prefetch=1, grid=(steps,), in_specs=in_specs, out_specs=out_specs),
        compiler_params=_cp(("arbitrary",), VMEM_LIMIT), name=name)(bc_idx, *sums, *recvd, *smalls_slots, *smalls_own)


def _pair_gather(arrs, last, name):
    n = len(arrs)
    n_dev = 8
    n_last = 0 if last is None else 1

    def body(*refs):
        outs = refs[n + n_last:2 * n + n_last]
        send, recv, gs, gr, loc = refs[2 * n + 2 * n_last:]
        x, y, c, _ = _coords()
        me = 4 * x + 2 * y + c
        gathers, arrivals, local = [], [], []
        if last is not None:
            last_in, slots = refs[n], refs[2 * n + 1]
            local.append(pltpu.make_async_copy(last_in, slots.at[me], loc))
            for r in range(1, n_dev):
                to = (me + r) % n_dev
                peer = dict(device_id=(to // 4, (to // 2) % 2, to % 2), device_id_type=MESH)
                gathers.append(pltpu.make_async_remote_copy(
                    src_ref=last_in, dst_ref=slots.at[me], send_sem=gs.at[r - 1], recv_sem=gr.at[me], **peer))
                arrivals.append(pltpu.make_async_remote_copy(
                    src_ref=slots.at[to], dst_ref=slots.at[to], send_sem=gs.at[r - 1], recv_sem=gr.at[to], **peer))
        cps = []
        for i in range(n):
            hr = outs[i].shape[0] // 2
            mine = outs[i].at[pl.ds(c * hr, hr)]
            cps.append(pltpu.make_async_remote_copy(
                src_ref=mine, dst_ref=mine, send_sem=send.at[i], recv_sem=recv.at[i],
                device_id=(x, y, 1 - c), device_id_type=MESH))
        for cp in local + gathers + cps:
            cp.start()
        for i in range(n):
            hr = outs[i].shape[0] // 2
            theirs = outs[i].at[pl.ds((1 - c) * hr, hr)]
            pltpu.make_async_remote_copy(
                src_ref=theirs, dst_ref=theirs, send_sem=send.at[i], recv_sem=recv.at[i],
                device_id=(x, y, 1 - c), device_id_type=MESH).wait_recv()
        for cp in arrivals:
            cp.wait_recv()
        for cp in cps + gathers:
            cp.wait_send()
        for cp in local:
            cp.wait()

    out_shape = [jax.ShapeDtypeStruct(a.shape, a.dtype) for a in arrs]
    operands = list(arrs)
    if last is not None:
        out_shape.append(jax.ShapeDtypeStruct((n_dev,) + last.shape, last.dtype))
        operands.append(last)
    outs = pl.pallas_call(
        body, out_shape=out_shape, in_specs=[ANY] * (n + n_last), out_specs=[ANY] * (n + n_last),
        input_output_aliases={i: i for i in range(n)},
        scratch_shapes=[pltpu.SemaphoreType.DMA((n,)), pltpu.SemaphoreType.DMA((n,)),
                        pltpu.SemaphoreType.DMA((n_dev - 1,)), pltpu.SemaphoreType.DMA((n_dev,)),
                        pltpu.SemaphoreType.DMA(())],
        name=name)(*operands)
    return list(outs[:n]), (outs[n] if last is not None else None)


def _adamw_math(w, g, m, v):
    m2 = ADAM_B1 * m + (1.0 - ADAM_B1) * g
    v2 = ADAM_B2 * v + (1.0 - ADAM_B2) * (g * g)
    m_hat = m2 / (1.0 - ADAM_B1 ** ADAM_STEP)
    v_hat = v2 / (1.0 - ADAM_B2 ** ADAM_STEP)
    delta = -ADAM_LR * (m_hat / (jnp.sqrt(v_hat) + ADAM_EPS) + ADAM_WD * w)
    return delta, m2, v2


def _adamw_big(w, g, m, v, name, steps=8):
    r, c = w.shape

    def body(w_ref, g_ref, m_ref, v_ref, d_ref, m2_ref, v2_ref):
        d_ref[...], m2_ref[...], v2_ref[...] = _adamw_math(w_ref[...], g_ref[...], m_ref[...], v_ref[...])

    spec = pl.BlockSpec((r // steps, c), lambda i: (i, 0))
    return pl.pallas_call(
        body, grid=(steps,), in_specs=[spec] * 4, out_specs=[spec] * 3,
        out_shape=[jax.ShapeDtypeStruct((r, c), F32)] * 3,
        compiler_params=_cp(("parallel",), VMEM_LIMIT), name=name)(w, g, m, v)


def _adamw_small(groups, slots):
    n = len(groups)

    def body(*refs):
        slots_ref, g0_ref = refs[4 * n], refs[4 * n + 1 + 3 * n]
        for i in range(n):
            w_ref, g_ref, m_ref, v_ref = refs[4 * i:4 * i + 4]
            d_ref, m2_ref, v2_ref = refs[4 * n + 1 + 3 * i:4 * n + 1 + 3 * i + 3]
            g = g_ref[...]
            if i == 0:
                for dev in range(slots.shape[0]):
                    g = g + slots_ref[dev]
                g0_ref[...] = g
            d_ref[...], m2_ref[...], v2_ref[...] = _adamw_math(w_ref[...], g, m_ref[...], v_ref[...])

    flat = [a for grp in groups for a in grp]
    out_shape = [jax.ShapeDtypeStruct(grp[0].shape, F32) for grp in groups for _ in range(3)]
    out_shape.append(jax.ShapeDtypeStruct(groups[0][0].shape, F32))
    outs = pl.pallas_call(body, out_shape=out_shape, name="adamw_small")(*flat, slots)
    return [tuple(outs[3 * i:3 * i + 3]) for i in range(n)], outs[3 * n]


def kernel(x, mem, norm_mix_g, w_in, conv_w, gm_ln_g, gm_ln_b, gm_ws, gm_bs, w_out, norm_x_g, norm_mem_g, w_q, w_kv, w_xo, norm_final_g, loss_target, m_norm_mix_g, m_w_in, m_conv_w, m_gm_ln_g, m_gm_ln_b, m_gm_ws, m_gm_bs, m_w_out, m_norm_x_g, m_norm_mem_g, m_w_q, m_w_kv, m_w_xo, m_norm_final_g, v_norm_mix_g, v_w_in, v_conv_w, v_gm_ln_g, v_gm_ln_b, v_gm_ws, v_gm_bs, v_w_out, v_norm_x_g, v_norm_mem_g, v_w_q, v_w_kv, v_w_xo, v_norm_final_g):
    t = x.shape[1]
    xi = lax.axis_index("x")
    yi = lax.axis_index("y")
    ci = lax.axis_index("c")
    b_idx = jnp.reshape(2 * xi + yi, (1,)).astype(jnp.int32)
    c_idx = jnp.reshape(ci, (1,)).astype(jnp.int32)

    x2d, mem2d, tgt = x[0], mem[0], loss_target[0]
    big = [w_in[0], w_out[0], w_q[0], w_kv[0], w_xo[0]]
    big_m = [m_w_in[0], m_w_out[0], m_w_q[0], m_w_kv[0], m_w_xo[0]]
    big_v = [v_w_in[0], v_w_out[0], v_w_q[0], v_w_kv[0], v_w_xo[0]]
    g3 = norm_final_g.reshape(1, D)

    def pad8(a):
        return jnp.pad(a, ((0, 8 - a.shape[0]), (0, 0)))

    own_blocks = _cast_shards(b_idx, big)

    tril = jnp.tril(jnp.ones((CH, CH), bool))
    wc32 = jnp.where(tril[None], gm_ws[0], 0.0)
    wc = wc32.astype(BF16)
    wct = jnp.swapaxes(wc32, 1, 2).astype(BF16)
    bsb = jnp.broadcast_to(gm_bs[0][:, :, None], (HEADS, CH, CH))

    blk = 2 * xi + yi
    order = jnp.stack([blk, blk ^ 2, blk ^ 1, blk ^ 3]).astype(jnp.int32)
    proj, ht, win_f, cw8, (wout_f, wkv_f) = _proj_gather(
        order, x2d, norm_mix_g, own_blocks[0], pad8(conv_w[0]), [own_blocks[1], own_blocks[3]])
    mixin, mixt, (wq_f, wxo_f) = _mixer_fwd(proj, cw8, gm_ln_g, gm_ln_b, wc, bsb, [own_blocks[2], own_blocks[4]])
    wout2, wq2, wxo2 = wout_f.reshape(MIX, D), wq_f.reshape(D, D), wxo_f.reshape(D, D)
    k, v, mt = _mem_fwd(mem2d, norm_mem_g, wkv_f)
    del mt

    (loss_tile, dmix, dx1b, h2t, dq, ot, dx2b, dk, dv, dg2, dg3) = _tail(
        x2d, tgt, mixin, wout2, wq2, wxo2, k, v, norm_x_g, g3)
    dwkv, dwkv_b, dgm = _mem_bwd(mem2d, norm_mem_g, dk, dv, wkv_f)
    dwxo, dwxo_b = _grad_matmul(ot, dx2b, by_cols=False, name="grad_w_xo", tk=2048)
    dwq, dwq_b = _grad_matmul(h2t, dq, by_cols=False, name="grad_w_q", tk=2048)
    dwout, dwout_b = _grad_matmul(mixt, dx1b, by_cols=False, name="grad_w_out")

    bc_idx = jnp.concatenate([b_idx, c_idx])
    rx1a = _pair_exchange([dwout_b, dwq_b, dwkv_b, dwxo_b], [], "pair_exchange_a")
    ps_a = _pair_sum(c_idx, [dwout, dwq, dwkv, dwxo], rx1a, [], [], "pair_sum_a")
    sums_a, sums_a_b = list(ps_a[:4]), list(ps_a[4:8])
    send_a, recv_a, src_a, land_a, token_a = _exchange_begin(sums_a_b, [], "exchange_a_begin")
    dproj, dcw, dlng, dlnb, dwc, dbs8, grad_x, dg1 = _mixer_bwd(
        proj, dmix, cw8, gm_ln_g, gm_ln_b, wc, wct, bsb, win_f, x2d, dx1b, norm_mix_g, token_a)

    dwin, dwin_b = _grad_matmul(ht, dproj, by_cols=True, name="grad_w_in", tk=2048)
    zero = jnp.zeros((1, D), F32)
    loss_row = jnp.broadcast_to(loss_tile[0:1, 0:1], (1, D))
    sv = jnp.concatenate([zero, dg2, dgm, dg3, dlng, dlnb, dbs8[0:1], loss_row, dcw], axis=0)
    sw = dwc.reshape(HEADS * CH, CH)
    rx1b = _pair_exchange([dwin_b], [sv, sw], "pair_exchange_b")
    ps_b = _pair_sum(c_idx, [dwin], rx1b[:1], [sv, sw], rx1b[1:], "pair_sum_b")
    sums_b, sums_b_b, psmall = list(ps_b[:1]), list(ps_b[1:2]), list(ps_b[2:])
    send_b, recv_b, src_b, land_b, token_b = _exchange_begin(sums_b_b, psmall, "exchange_b_begin")

    rx2a = _exchange_end(send_a, recv_a, src_a, land_a, 4, token_b, "exchange_a_end")
    red_a = _chip_sum(bc_idx, sums_a, rx2a, [], [], "chip_sum_a")
    (gwout, gwq, gwkv, gwxo), dg1_slots = _pair_gather(red_a, dg1, "pair_gather_a")
    names = ["w_out", "w_q", "w_kv", "w_xo"]
    out_a = [_adamw_big(w, g, m, v, "adamw_" + nm)
             for w, g, m, v, nm in zip(big[1:], [gwout, gwq, gwkv, gwxo], big_m[1:], big_v[1:], names)]

    rx2b = _exchange_end(send_b, recv_b, src_b, land_b, 1, out_a[-1][0], "exchange_b_end")
    red_b = _chip_sum(bc_idx, sums_b, rx2b[:1], rx2b[1:], psmall, "chip_sum_b")
    (gwin, svf, swf), _ = _pair_gather(red_b, None, "pair_gather_b")
    big_out = [_adamw_big(big[0], gwin, big_m[0], big_v[0], "adamw_w_in")] + out_a

    def vec_pack(a1, a2, am, a3, lg, lb, bs):
        return jnp.concatenate([a1, a2, am, a3.reshape(1, D), lg, lb, bs.reshape(1, D), zero], axis=0)

    wv = vec_pack(norm_mix_g, norm_x_g, norm_mem_g, norm_final_g, gm_ln_g, gm_ln_b, gm_bs)
    mv = vec_pack(m_norm_mix_g, m_norm_x_g, m_norm_mem_g, m_norm_final_g, m_gm_ln_g, m_gm_ln_b, m_gm_bs)
    vv = vec_pack(v_norm_mix_g, v_norm_x_g, v_norm_mem_g, v_norm_final_g, v_gm_ln_g, v_gm_ln_b, v_gm_bs)
    loss = svf[7, 0]
    gcw = lax.dynamic_slice_in_dim(svf[8:16], blk * (D // N_CHIP), D // N_CHIP, axis=1)
    gws = swf
    ((dv_, mv_, vv_), (dc_, mc_, vc_), (dws_, mws_, vws_)), gv = _adamw_small([
        (wv, svf[0:8], mv, vv),
        (pad8(conv_w[0]), gcw, pad8(m_conv_w[0]), pad8(v_conv_w[0])),
        (gm_ws.reshape(HEADS * CH, CH), gws, m_gm_ws.reshape(HEADS * CH, CH), v_gm_ws.reshape(HEADS * CH, CH))],
        dg1_slots)

    def unpack(vecs, cw, ws, bigs):
        r = lambda i: vecs[i:i + 1]
        return [r(0), bigs[0][None], cw[0:3][None], r(4), r(5), ws.reshape(1, HEADS, CH, CH), vecs[6].reshape(1, HEADS, CH),
                bigs[1][None], r(1), r(2), bigs[2][None], bigs[3][None], bigs[4][None], vecs[3]]

    grads_out = unpack(gv, gcw, gws, [gwin, gwout, gwq, gwkv, gwxo])
    delta_out = unpack(dv_, dc_, dws_, [o[0] for o in big_out])
    m_out = unpack(mv_, mc_, mws_, [o[1] for o in big_out])
    v_out = unpack(vv_, vc_, vws_, [o[2] for o in big_out])
    return (loss, grad_x[None], *grads_out, *delta_out, *m_out, *v_out)
```

```python
import functools
import math

import jax
import jax.numpy as jnp
from jax import lax
from jax.experimental import pallas as pl
from jax.experimental.pallas import tpu as pltpu

F32 = jnp.float32
BF16 = jnp.bfloat16
MESH = pl.DeviceIdType.MESH

D = 1024
SLAB = 1024
N_SLAB = 7
IN_DIM = N_SLAB * SLAB
MIX = 2 * SLAB
HEADS = 8
CH = 128
XH = 4
XD = D // XH
EPS = 1e-6
GELU_C = math.sqrt(2.0 / math.pi)
GELU_A = 0.044715
N_CHIP = 4
IN_BLK = IN_DIM // N_CHIP
KV_BLK = 2 * D // N_CHIP

ADAM_LR, ADAM_B1, ADAM_B2, ADAM_EPS, ADAM_WD, ADAM_STEP = 0.001, 0.9, 0.999, 1e-08, 0.01, 10

VMEM_LIMIT = 60 * 1024 * 1024


def _cp(sem=None, vmem=None):
    return pltpu.CompilerParams(dimension_semantics=sem, vmem_limit_bytes=vmem)


def _full(shape, buffers=None):
    n = len(shape)
    if buffers is None:
        return pl.BlockSpec(shape, lambda *_: (0,) * n)
    return pl.BlockSpec(shape, lambda *_: (0,) * n, pipeline_mode=pl.Buffered(buffers))


ANY = pl.BlockSpec(memory_space=pl.ANY)


def _bdot(a, b):
    return jnp.dot(a.astype(BF16), b.astype(BF16), preferred_element_type=F32)


def _bdot_nt(a, b):
    return lax.dot_general(a.astype(BF16), b.astype(BF16), (((1,), (1,)), ((), ())), preferred_element_type=F32)


def _rms(x, g):
    r = lax.rsqrt(jnp.mean(x * x, axis=-1, keepdims=True) + EPS)
    return x * r * g, r


def _rms_bwd(dy, x, r, g):
    gdy = dy * g
    dx = r * gdy - x * (r * r * r) * jnp.mean(x * gdy, axis=-1, keepdims=True)
    dg = jnp.sum(dy * x * r, axis=0, keepdims=True)
    return dx, dg


def _gelu_parts(x):
    x2 = x * x
    t = jnp.tanh(GELU_C * (x + GELU_A * x * x2))
    val = 0.5 * x * (1.0 + t)
    grad = 0.5 * (1.0 + t) + 0.5 * x * (1.0 - t * t) * (GELU_C * (1.0 + 3.0 * GELU_A * x2))
    return val, grad


def _gelu(x):
    return 0.5 * x * (1.0 + jnp.tanh(GELU_C * (x + GELU_A * x * x * x)))


def _sigmoid(z):
    return 1.0 / (1.0 + jnp.exp(-z))


def _cast_shards(b_idx, arrs):
    n = len(arrs)
    steps = 8

    def body(b_ref, *refs):
        for i in range(n):
            refs[n + i][...] = refs[i][...].astype(BF16)

    in_specs = [pl.BlockSpec((a.shape[0] // steps, a.shape[1]), lambda i, b: (i, 0)) for a in arrs]
    out_specs = [pl.BlockSpec((None, a.shape[0] // steps, a.shape[1]), lambda i, b: (b[0], i, 0)) for a in arrs]
    return pl.pallas_call(
        body, out_shape=[jax.ShapeDtypeStruct((N_CHIP,) + a.shape, BF16) for a in arrs],
        grid_spec=pltpu.PrefetchScalarGridSpec(num_scalar_prefetch=1, grid=(steps,), in_specs=in_specs, out_specs=out_specs),
        compiler_params=_cp(("arbitrary",)), name="cast_shards")(b_idx, *arrs)


def _proj_gather(order, x, g, win_own, cw8s, more, tm=1024):
    t = x.shape[0]
    ni = t // tm
    hr = D // 2
    nm = len(more)

    def body(*refs):
        order_ref, x_ref, g_ref, win_in, cw_in = refs[:5]
        o_ref, ht_ref, win_f, cw_out = refs[5 + nm:9 + nm]
        more_out = refs[9 + nm:9 + 2 * nm]
        hbuf, wv, ici_s, ici_r, d2d_s, d2d_r, cw_s, cw_r, loc, m_is, m_ir, m_ds, m_dr = refs[9 + 2 * nm:]
        more_start, more_finish = _gather_steps(more_out, m_is, m_ir, m_ds, m_dr)
        j, i = pl.program_id(0), pl.program_id(1)
        x, y, c, chips = _coords()
        b = 2 * x + y
        sib = (x, y, 1 - c)
        blks = [2 * chip[0] + chip[1] for chip in chips]

        def half(blk, hc):
            return win_f.at[blk, pl.ds(hc * hr, hr)]

        def cw_cols(blk):
            return cw_out.at[:, pl.ds(blk * (D // N_CHIP), D // N_CHIP)]

        def ici(k, blk):
            return pltpu.make_async_remote_copy(src_ref=half(blk, c), dst_ref=half(blk, c), send_sem=ici_s.at[k],
                                                recv_sem=ici_r.at[k], device_id=(*chips[k], c), device_id_type=MESH)

        def relay(k, blk, hc):
            return pltpu.make_async_remote_copy(src_ref=half(blk, hc), dst_ref=half(blk, hc), send_sem=d2d_s.at[k],
                                                recv_sem=d2d_r.at[k], device_id=sib, device_id_type=MESH)

        def cw_copy(k, blk):
            src = cw_in if blk is None else cw_cols(blk)
            return pltpu.make_async_remote_copy(src_ref=src, dst_ref=cw_cols(b if blk is None else blk), send_sem=cw_s.at[k],
                                                recv_sem=cw_r.at[k], device_id=(*chips[k], c), device_id_type=MESH)

        cw_local = pltpu.make_async_copy(cw_in, cw_cols(b), loc.at[1])

        @pl.when((j == 0) & (i == 0))
        def _():
            cw_local.start()
            for k in range(3):
                ici(k, b).start()
            for k in range(3):
                cw_copy(k, None).start()
            more_start()

        for jj in range(N_CHIP):
            @pl.when((j == jj) & (i == 0))
            def _(jj=jj):
                if jj == 0:
                    blk = b
                else:
                    blk = blks[jj - 1]
                    ici(jj - 1, blk).wait_recv()
                    relay(jj - 1, blk, c).start()
                    relay(jj - 1, blk, 1 - c).wait_recv()
                load = pltpu.make_async_copy(win_f.at[blk], wv, loc.at[0])
                load.start()
                load.wait()

        rows = pl.ds(pl.multiple_of(i * tm, tm), tm)

        @pl.when(j == 0)
        def _():
            h, _ = _rms(x_ref[...], g_ref[...])
            hbuf[rows, :] = h.astype(BF16)
            ht_ref[...] = h.T.astype(BF16)

        o_ref[...] = jnp.dot(hbuf[rows, :], wv[...], preferred_element_type=F32).astype(BF16)

        @pl.when((j == N_CHIP - 1) & (i == ni - 1))
        def _():
            for k in range(3):
                cw_copy(k, blks[k]).wait_recv()
            for k in range(3):
                ici(k, b).wait_send()
                relay(k, blks[k], c).wait_send()
                cw_copy(k, None).wait_send()
            cw_local.wait()
            more_finish()

    first = lambda j, i: jnp.where(j == 0, i, ni - 1)
    in_specs = [pl.BlockSpec((tm, D), lambda j, i, o: (first(j, i), 0)), pl.BlockSpec((1, D), lambda j, i, o: (0, 0)),
                ANY, ANY] + [ANY] * nm
    out_specs = [pl.BlockSpec((tm, IN_BLK), lambda j, i, o: (i, o[j])),
                 pl.BlockSpec((D, tm), lambda j, i, o: (0, first(j, i))), ANY, ANY] + [ANY] * nm
    outs = pl.pallas_call(
        body, out_shape=[jax.ShapeDtypeStruct((t, IN_DIM), BF16), jax.ShapeDtypeStruct((D, t), BF16),
                         jax.ShapeDtypeStruct(win_own.shape, BF16), jax.ShapeDtypeStruct((8, D), F32)]
        + [jax.ShapeDtypeStruct(f.shape, f.dtype) for f in more],
        grid_spec=pltpu.PrefetchScalarGridSpec(
            num_scalar_prefetch=1, grid=(N_CHIP, ni), in_specs=in_specs, out_specs=out_specs,
            scratch_shapes=[pltpu.VMEM((t, D), BF16), pltpu.VMEM((D, IN_BLK), BF16)]
            + [pltpu.SemaphoreType.DMA((3,))] * 6 + [pltpu.SemaphoreType.DMA((2,))]
            + [pltpu.SemaphoreType.DMA((max(nm, 1), 3))] * 4),
        input_output_aliases={3: 2, **{5 + w: 4 + w for w in range(nm)}},
        compiler_params=_cp(("arbitrary", "arbitrary"), VMEM_LIMIT), name="proj_gather")(order, x, g, win_own, cw8s, *more)
    return outs[0], outs[1], outs[2], outs[3], outs[4:]


def _gather_steps(outs, ici_s, ici_r, d2d_s, d2d_r):
    x, y, c, chips = _coords()
    b = 2 * x + y
    sib = (x, y, 1 - c)
    nw = len(outs)

    def half(w, blk, hc):
        hr = outs[w].shape[1] // 2
        return outs[w].at[blk, pl.ds(hc * hr, hr)]

    def ici(w, k, blk):
        return pltpu.make_async_remote_copy(src_ref=half(w, blk, c), dst_ref=half(w, blk, c), send_sem=ici_s.at[w, k],
                                            recv_sem=ici_r.at[w, k], device_id=(*chips[k], c), device_id_type=MESH)

    def relay(w, k, blk, hc):
        return pltpu.make_async_remote_copy(src_ref=half(w, blk, hc), dst_ref=half(w, blk, hc), send_sem=d2d_s.at[w, k],
                                            recv_sem=d2d_r.at[w, k], device_id=sib, device_id_type=MESH)

    def start():
        for w in range(nw):
            for k in range(3):
                ici(w, k, b).start()

    def finish():
        for w in range(nw):
            for k in range(3):
                blk = 2 * chips[k][0] + chips[k][1]
                ici(w, k, blk).wait_recv()
                relay(w, k, blk, c).start()
        for w in range(nw):
            for k in range(3):
                blk = 2 * chips[k][0] + chips[k][1]
                relay(w, k, blk, 1 - c).wait_recv()
        for w in range(nw):
            for k in range(3):
                blk = 2 * chips[k][0] + chips[k][1]
                ici(w, k, b).wait_send()
                relay(w, k, blk, c).wait_send()

    return start, finish


def _mixer_fwd(proj, cw8, lng, lnb, wc, bsb, fulls, tm=256):
    t = proj.shape[0]
    nt = t // tm
    nch = tm // CH
    nw = len(fulls)

    def body(*refs):
        p_ref, cw_ref, lng_ref, lnb_ref, wc_ref, bsb_ref = refs[:6]
        mix_ref, mixt_ref = refs[6 + nw:8 + nw]
        w_outs = refs[8 + nw:8 + 2 * nw]
        prev_ref, stage_ref, ici_s, ici_r, d2d_s, d2d_r = refs[8 + 2 * nw:]
        gather_start, gather_finish = _gather_steps(w_outs, ici_s, ici_r, d2d_s, d2d_r)

        @pl.when(pl.program_id(0) == 0)
        def _():
            gather_start()
            prev_ref[...] = jnp.zeros_like(prev_ref)

        rows = lax.broadcasted_iota(jnp.int32, (tm, CH), 0)
        for s in range(HEADS):
            cs = pl.ds(CH * s, CH)

            def slab(k):
                return p_ref[:, pl.ds(k * SLAB + CH * s, CH)].astype(F32)

            gb, gc, xa, za = slab(0), slab(1), slab(2), slab(3)
            cx = gc * xa
            p6 = jnp.broadcast_to(prev_ref[6:7, cs], (tm, CH))
            p7 = jnp.broadcast_to(prev_ref[7:8, cs], (tm, CH))
            c1 = jnp.where(rows == 0, p7, pltpu.roll(cx, 1, 0))
            c2 = jnp.where(rows == 0, p6, jnp.where(rows == 1, p7, pltpu.roll(cx, 2, 0)))
            prev_ref[:, cs] = cx[tm - 8:, :]
            cv = cw_ref[0:1, cs] * c2 + cw_ref[1:2, cs] * c1 + cw_ref[2:3, cs] * cx
            stage_ref[:, cs] = gb * cv * (za * _sigmoid(za))

            u, v, zb = slab(4), slab(5), slab(6)
            ug, vg = _gelu(u), _gelu(v)
            dlt = vg - jnp.mean(vg, axis=-1, keepdims=True)
            vhat = dlt * lax.rsqrt(jnp.mean(dlt * dlt, axis=-1, keepdims=True) + EPS)
            vn = (vhat * lng_ref[:, cs] + lnb_ref[:, cs]).astype(BF16)
            gate = ug * (zb * _sigmoid(zb))
            for c in range(nch):
                rs = slice(CH * c, CH * (c + 1))
                sp = jnp.dot(wc_ref[s], vn[rs], preferred_element_type=F32) + bsb_ref[s]
                stage_ref[rs, pl.ds(SLAB + CH * s, CH)] = gate[rs] * sp

        full = stage_ref[...]
        mix_ref[...] = full.astype(BF16)
        mixt_ref[...] = full.T.astype(BF16)

        @pl.when(pl.program_id(0) == nt - 1)
        def _():
            gather_finish()

    sems = [pltpu.SemaphoreType.DMA((nw, 3))] * 4
    outs = pl.pallas_call(
        body, grid=(nt,),
        in_specs=[pl.BlockSpec((tm, IN_DIM), lambda i: (i, 0)), _full((8, D)), _full((1, D)), _full((1, D)),
                  _full((HEADS, CH, CH)), _full((HEADS, CH, CH))] + [ANY] * nw,
        out_specs=[pl.BlockSpec((tm, MIX), lambda i: (i, 0)), pl.BlockSpec((MIX, tm), lambda i: (0, i))] + [ANY] * nw,
        out_shape=[jax.ShapeDtypeStruct((t, MIX), BF16), jax.ShapeDtypeStruct((MIX, t), BF16)]
        + [jax.ShapeDtypeStruct(f.shape, f.dtype) for f in fulls],
        input_output_aliases={6 + w: 2 + w for w in range(nw)},
        scratch_shapes=[pltpu.VMEM((8, D), F32), pltpu.VMEM((tm, MIX), F32)] + sems,
        compiler_params=_cp(("arbitrary",), VMEM_LIMIT), name="mixer_fwd")(proj, cw8, lng, lnb, wc, bsb, *fulls)
    return outs[0], outs[1], outs[2:]


def _mem_fwd(mem, gm, wkv_f):
    n_mem = mem.shape[0]

    def body(mem_ref, gm_ref, w_ref, k_ref, v_ref, mt_ref):
        m, _ = _rms(mem_ref[...], gm_ref[...])
        mb = m.astype(BF16)
        mt_ref[...] = m.T.astype(BF16)
        for j in range(N_CHIP):
            dst = k_ref if j < 2 else v_ref
            dst[:, pl.ds(KV_BLK * (j % 2), KV_BLK)] = jnp.dot(mb, w_ref[j], preferred_element_type=F32).astype(BF16)

    return pl.pallas_call(
        body, out_shape=[jax.ShapeDtypeStruct((n_mem, D), BF16), jax.ShapeDtypeStruct((n_mem, D), BF16),
                         jax.ShapeDtypeStruct((D, n_mem), BF16)],
        compiler_params=_cp(None, VMEM_LIMIT), name="mem_fwd")(mem, gm, wkv_f)


def _tail(x, tgt, mixin, wout, wq, wxo, k, v, g2, g3, tm=512, sub=512):
    t = x.shape[0]
    n_mem = k.shape[0]
    scale = 1.0 / math.sqrt(XD)

    def body(x_ref, tgt_ref, mix_ref, wout_ref, wq_ref, wxo_ref, k_ref, v_ref, g2_ref, g3_ref,
             loss_ref, dmix_ref, dx1b_ref, h2t_ref, dq_ref, ot_ref, dx2b_ref, dk_ref, dv_ref, dg2_ref, dg3_ref):
        @pl.when(pl.program_id(0) == 0)
        def _():
            loss_ref[...] = jnp.zeros_like(loss_ref)
            dk_ref[...] = jnp.zeros_like(dk_ref)
            dv_ref[...] = jnp.zeros_like(dv_ref)
            dg2_ref[...] = jnp.zeros_like(dg2_ref)
            dg3_ref[...] = jnp.zeros_like(dg3_ref)

        g2, g3 = g2_ref[...], g3_ref[...]
        for sb in range(tm // sub):
            rs = pl.ds(sub * sb, sub)
            x1 = x_ref[rs, :] + jnp.dot(mix_ref[rs, :], wout_ref[...], preferred_element_type=F32)
            h2, r2 = _rms(x1, g2)
            h2t_ref[:, rs] = h2.T.astype(BF16)
            q = _bdot(h2, wq_ref[...]).astype(BF16)
            probs, outs = [], []
            for hd in range(XH):
                hs = pl.ds(XD * hd, XD)
                s = _bdot_nt(q[:, XD * hd:XD * (hd + 1)], k_ref[:, hs]) * scale
                e = jnp.exp(s - jnp.max(s, axis=-1, keepdims=True))
                p = e / jnp.sum(e, axis=-1, keepdims=True)
                probs.append(p)
                outs.append(_bdot(p, v_ref[:, hs]))
            o = jnp.concatenate(outs, axis=-1)
            ot_ref[:, rs] = o.T.astype(BF16)
            x2 = x1 + _bdot(o, wxo_ref[...])
            y, r3 = _rms(x2, g3)
            diff = y - tgt_ref[rs, :]
            row_loss = jnp.sum(diff * diff, axis=-1, keepdims=True)
            loss_ref[...] += jnp.broadcast_to(jnp.sum(row_loss, axis=0, keepdims=True) * (0.5 / D), loss_ref.shape)

            dx2, dg3 = _rms_bwd(diff * (1.0 / D), x2, r3, g3)
            dg3_ref[...] += dg3
            dx2b = dx2.astype(BF16)
            dx2b_ref[rs, :] = dx2b
            do = _bdot_nt(dx2b, wxo_ref[...])
            dqs = []
            for hd in range(XH):
                hs = pl.ds(XD * hd, XD)
                p = probs[hd]
                do_h = do[:, XD * hd:XD * (hd + 1)]
                dv_ref[:, hs] += _bdot(p.T, do_h)
                dp = _bdot_nt(do_h, v_ref[:, hs])
                ds = p * (dp - jnp.sum(dp * p, axis=-1, keepdims=True))
                dqs.append(_bdot(ds, k_ref[:, hs]) * scale)
                dk_ref[:, hs] += _bdot(ds.T, q[:, XD * hd:XD * (hd + 1)]) * scale
            dq = jnp.concatenate(dqs, axis=-1).astype(BF16)
            dq_ref[rs, :] = dq
            dx1n, dg2 = _rms_bwd(_bdot_nt(dq, wq_ref[...]), x1, r2, g2)
            dg2_ref[...] += dg2
            dx1b = (dx2 + dx1n).astype(BF16)
            dx1b_ref[rs, :] = dx1b
            dmix_ref[rs, :] = _bdot_nt(dx1b, wout_ref[...]).astype(BF16)

    tok = lambda w: pl.BlockSpec((tm, w), lambda i: (i, 0))
    tok_t = lambda w: pl.BlockSpec((w, tm), lambda i: (0, i))
    return pl.pallas_call(
        body, grid=(t // tm,),
        in_specs=[tok(D), tok(D), tok(MIX), _full((MIX, D), 1), _full((D, D), 1), _full((D, D), 1),
                  _full((n_mem, D), 1), _full((n_mem, D), 1), _full((1, D)), _full((1, D))],
        out_specs=[_full((8, 128)), tok(MIX), tok(D), tok_t(D), tok(D), tok_t(D), tok(D),
                   _full((n_mem, D)), _full((n_mem, D)), _full((1, D)), _full((1, D))],
        out_shape=[jax.ShapeDtypeStruct((8, 128), F32), jax.ShapeDtypeStruct((t, MIX), BF16),
                   jax.ShapeDtypeStruct((t, D), BF16),
                   jax.ShapeDtypeStruct((D, t), BF16), jax.ShapeDtypeStruct((t, D), BF16),
                   jax.ShapeDtypeStruct((D, t), BF16), jax.ShapeDtypeStruct((t, D), BF16),
                   jax.ShapeDtypeStruct((n_mem, D), F32), jax.ShapeDtypeStruct((n_mem, D), F32),
                   jax.ShapeDtypeStruct((1, D), F32), jax.ShapeDtypeStruct((1, D), F32)],
        compiler_params=_cp(("arbitrary",), VMEM_LIMIT), name="tail")(x, tgt, mixin, wout, wq, wxo, k, v, g2, g3)


def _mem_bwd(mem, gm, dk, dv, wkv_f):
    def body(mem_ref, gm_ref, dk_ref, dv_ref, w_ref, dw_ref, dwb_ref, dgm_ref):
        mem_v = mem_ref[...]
        m, rm = _rms(mem_v, gm_ref[...])
        mt = m.T.astype(BF16)
        dm = jnp.zeros_like(mem_v)
        for j in range(N_CHIP):
            src = dk_ref if j < 2 else dv_ref
            dkv = src[:, pl.ds(KV_BLK * (j % 2), KV_BLK)].astype(BF16)
            dw = jnp.dot(mt, dkv, preferred_element_type=F32)
            dw_ref[j] = dw
            dwb_ref[j] = dw.astype(BF16)
            dm = dm + _bdot_nt(dkv, w_ref[j])
        dgm_ref[...] = jnp.sum(dm * mem_v * rm, axis=0, keepdims=True)

    return pl.pallas_call(
        body, out_shape=[jax.ShapeDtypeStruct((N_CHIP, D, KV_BLK), F32), jax.ShapeDtypeStruct((N_CHIP, D, KV_BLK), BF16),
                         jax.ShapeDtypeStruct((1, D), F32)],
        compiler_params=_cp(None, VMEM_LIMIT), name="mem_bwd")(mem, gm, dk, dv, wkv_f)


def _mixer_bwd(proj, dmix, cw8, lng, lnb, wc, wct, bsb, win_f, x, dx1, g1, after, tm=256):
    t = proj.shape[0]
    nt = t // tm
    nch = tm // CH
    hb = 16
    pair = 2 * CH

    def body(p_ref, pgc_ref, pxa_ref, dm_ref, cw_ref, lng_ref, lnb_ref, wc_ref, wct_ref, bsb_ref, w_ref, x_ref,
             dx1_ref, g1_ref, after_ref, dp_ref, dcw_ref, dlng_ref, dlnb_ref, dwc_ref, dbs_ref, gx_ref, dg1_ref,
             next_ref, dh_ref):
        i = pl.program_id(0)

        @pl.when(i == 0)
        def _():
            next_ref[...] = jnp.zeros_like(next_ref)
            dcw_ref[...] = jnp.zeros_like(dcw_ref)
            dlng_ref[...] = jnp.zeros_like(dlng_ref)
            dlnb_ref[...] = jnp.zeros_like(dlnb_ref)
            dwc_ref[...] = jnp.zeros_like(dwc_ref)
            dbs_ref[...] = jnp.zeros_like(dbs_ref)
            dg1_ref[...] = jnp.zeros_like(dg1_ref)

        first_tile = i == nt - 1
        rows = lax.broadcasted_iota(jnp.int32, (tm, CH), 0)
        ones8 = jnp.ones((8, CH), BF16)
        for s in range(HEADS):
            cs = pl.ds(CH * s, CH)

            def slab(k):
                return p_ref[:, pl.ds(k * SLAB + CH * s, CH)].astype(F32)

            gb, gc, xa, za = slab(0), slab(1), slab(2), slab(3)
            da = dm_ref[:, cs].astype(F32)
            cx = gc * xa
            cxp = pgc_ref[:, cs].astype(F32) * pxa_ref[:, cs].astype(F32)
            cxp = jnp.where(first_tile, jnp.zeros_like(cxp), cxp)
            p6 = jnp.broadcast_to(cxp[hb - 2:hb - 1, :], (tm, CH))
            p7 = jnp.broadcast_to(cxp[hb - 1:hb, :], (tm, CH))
            c1 = jnp.where(rows == 0, p7, pltpu.roll(cx, 1, 0))
            c2 = jnp.where(rows == 0, p6, jnp.where(rows == 1, p7, pltpu.roll(cx, 2, 0)))
            w0, w1, w2 = cw_ref[0:1, cs], cw_ref[1:2, cs], cw_ref[2:3, cs]
            cv = w0 * c2 + w1 * c1 + w2 * cx
            sg = _sigmoid(za)
            sa = za * sg
            dcv = da * gb * sa
            dp_ref[:, pl.ds(0 * SLAB + CH * s, CH)] = (da * cv * sa).astype(BF16)
            dp_ref[:, pl.ds(3 * SLAB + CH * s, CH)] = (da * gb * cv * (sg * (1.0 + za * (1.0 - sg)))).astype(BF16)
            n0 = jnp.broadcast_to(next_ref[0:1, cs], (tm, CH))
            n1 = jnp.broadcast_to(next_ref[1:2, cs], (tm, CH))
            u1 = jnp.where(rows == tm - 1, n0, pltpu.roll(dcv, tm - 1, 0))
            u2 = jnp.where(rows == tm - 2, n0, jnp.where(rows == tm - 1, n1, pltpu.roll(dcv, tm - 2, 0)))
            next_ref[:, cs] = dcv[0:8, :]
            dcx = w2 * dcv + w1 * u1 + w0 * u2
            dp_ref[:, pl.ds(1 * SLAB + CH * s, CH)] = (dcx * xa).astype(BF16)
            dp_ref[:, pl.ds(2 * SLAB + CH * s, CH)] = (dcx * gc).astype(BF16)
            dcw_ref[0:1, cs] += jnp.sum(dcv * c2, axis=0, keepdims=True)
            dcw_ref[1:2, cs] += jnp.sum(dcv * c1, axis=0, keepdims=True)
            dcw_ref[2:3, cs] += jnp.sum(dcv * cx, axis=0, keepdims=True)

            u, v, zb = slab(4), slab(5), slab(6)
            db = dm_ref[:, pl.ds(SLAB + CH * s, CH)].astype(F32)
            ug, ugrad = _gelu_parts(u)
            vg, vgrad = _gelu_parts(v)
            dlt = vg - jnp.mean(vg, axis=-1, keepdims=True)
            rstd = lax.rsqrt(jnp.mean(dlt * dlt, axis=-1, keepdims=True) + EPS)
            vhat = dlt * rstd
            lg = lng_ref[:, cs]
            vn = (vhat * lg + lnb_ref[:, cs]).astype(BF16)
            sgb = _sigmoid(zb)
            szb = zb * sgb
            sps, dvns = [], []
            dbs = jnp.zeros((8, CH), F32)
            dwc = jnp.zeros((CH, CH), F32)
            for c in range(nch):
                rs = slice(CH * c, CH * (c + 1))
                sp = jnp.dot(wc_ref[s], vn[rs], preferred_element_type=F32) + bsb_ref[s]
                dsp = (db[rs] * ug[rs] * szb[rs]).astype(BF16)
                dbs = dbs + lax.dot_general(ones8, dsp, (((1,), (1,)), ((), ())), preferred_element_type=F32)
                dwc = dwc + lax.dot_general(dsp, vn[rs], (((1,), (1,)), ((), ())), preferred_element_type=F32)
                dvns.append(jnp.dot(wct_ref[s], dsp, preferred_element_type=F32))
                sps.append(sp)
            sp = jnp.concatenate(sps, axis=0)
            dvn = jnp.concatenate(dvns, axis=0)
            dbs_ref[:, cs] += dbs
            dwc_ref[s] += dwc
            dlng_ref[:, cs] += jnp.sum(dvn * vhat, axis=0, keepdims=True)
            dlnb_ref[:, cs] += jnp.sum(dvn, axis=0, keepdims=True)
            dvhat = dvn * lg
            dvg = rstd * (dvhat - jnp.mean(dvhat, axis=-1, keepdims=True)
                          - vhat * jnp.mean(dvhat * vhat, axis=-1, keepdims=True))
            dp_ref[:, pl.ds(4 * SLAB + CH * s, CH)] = (db * sp * szb * ugrad).astype(BF16)
            dp_ref[:, pl.ds(5 * SLAB + CH * s, CH)] = (dvg * vgrad).astype(BF16)
            dp_ref[:, pl.ds(6 * SLAB + CH * s, CH)] = (db * ug * sp * (sgb * (1.0 + zb * (1.0 - sgb)))).astype(BF16)

            if s % 2 == 1:
                part = None
                for k in range(N_SLAB):
                    col = k * SLAB + pair * (s // 2)
                    blk, off = divmod(col, IN_BLK)
                    term = lax.dot_general(dp_ref[:, pl.ds(col, pair)], w_ref[blk, :, pl.ds(off, pair)],
                                           (((1,), (1,)), ((), ())), preferred_element_type=F32)
                    part = term if part is None else part + term
                if s == 1:
                    dh_ref[...] = part
                else:
                    dh_ref[...] += part

        xv = x_ref[...]
        r = lax.rsqrt(jnp.mean(xv * xv, axis=-1, keepdims=True) + EPS)
        dxn, dg = _rms_bwd(dh_ref[...], xv, r, g1_ref[...])
        gx_ref[...] = dx1_ref[...].astype(F32) + dxn
        dg1_ref[0:1, :] += dg

        @pl.when(i == nt - 1)
        def _():
            tril = lax.broadcasted_iota(jnp.int32, (CH, CH), 0) >= lax.broadcasted_iota(jnp.int32, (CH, CH), 1)
            for s in range(HEADS):
                dwc_ref[s] = jnp.where(tril, dwc_ref[s], 0.0)

    rev = lambda i: nt - 1 - i
    halo = lambda col: pl.BlockSpec((hb, SLAB), lambda i: (jnp.maximum(rev(i) * (tm // hb) - 1, 0), col))
    tok = lambda w: pl.BlockSpec((tm, w), lambda i: (rev(i), 0))
    return pl.pallas_call(
        body, grid=(nt,),
        in_specs=[tok(IN_DIM), halo(1), halo(2), tok(MIX), _full((8, D)), _full((1, D)), _full((1, D)),
                  _full((HEADS, CH, CH)), _full((HEADS, CH, CH)), _full((HEADS, CH, CH)),
                  _full((N_CHIP, D, IN_BLK), 1), tok(D), tok(D), _full((1, D)), ANY],
        out_specs=[tok(IN_DIM), _full((8, D)), _full((1, D)), _full((1, D)), _full((HEADS, CH, CH)), _full((8, D)),
                   tok(D), _full((8, D))],
        out_shape=[jax.ShapeDtypeStruct((t, IN_DIM), BF16), jax.ShapeDtypeStruct((8, D), F32),
                   jax.ShapeDtypeStruct((1, D), F32), jax.ShapeDtypeStruct((1, D), F32),
                   jax.ShapeDtypeStruct((HEADS, CH, CH), F32), jax.ShapeDtypeStruct((8, D), F32),
                   jax.ShapeDtypeStruct((t, D), F32), jax.ShapeDtypeStruct((8, D), F32)],
        scratch_shapes=[pltpu.VMEM((8, D), F32), pltpu.VMEM((tm, D), F32)],
        compiler_params=_cp(("arbitrary",), VMEM_LIMIT), name="mixer_bwd")(
            proj, proj, proj, dmix, cw8, lng, lnb, wc, wct, bsb, win_f, x, dx1, g1, after)


def _grad_matmul(at, b, *, by_cols, name, tk=1024):
    m, t = at.shape
    n = b.shape[1]
    nk = t // tk
    nj = N_CHIP if by_cols else 1
    bn = n // nj

    def body(a_ref, b_ref, o_ref, ob_ref):
        kk = pl.program_id(1)
        part = jnp.dot(a_ref[...], b_ref[...], preferred_element_type=F32)

        @pl.when(kk == 0)
        def _():
            o_ref[...] = part

        @pl.when(kk > 0)
        def _():
            o_ref[...] += part

        @pl.when(kk == nk - 1)
        def _():
            ob_ref[...] = o_ref[...].astype(BF16)

    a_spec = pl.BlockSpec((m, tk), lambda j, k: (0, k))
    b_spec = pl.BlockSpec((tk, bn), lambda j, k: (k, j))
    o_spec = pl.BlockSpec((None, m, bn), lambda j, k: (j, 0, 0))
    o32, o16 = pl.pallas_call(
        body, grid=(nj, nk), in_specs=[a_spec, b_spec], out_specs=[o_spec, o_spec],
        out_shape=[jax.ShapeDtypeStruct((nj, m, bn), F32), jax.ShapeDtypeStruct((nj, m, bn), BF16)],
        compiler_params=_cp(("parallel", "arbitrary"), VMEM_LIMIT), name=name)(at, b)
    if by_cols:
        return o32, o16
    return o32.reshape(N_CHIP, m // N_CHIP, n), o16.reshape(N_CHIP, m // N_CHIP, n)


def _coords():
    x, y, c = lax.axis_index("x"), lax.axis_index("y"), lax.axis_index("c")
    chips = [(1 - x, y), (x, 1 - y), (1 - x, 1 - y)]
    return x, y, c, chips


def _pair_exchange(grads_b, smalls, name):
    ng, ns = len(grads_b), len(smalls)
    n = ng + ns

    def body(*refs):
        ins, outs, send, recv = refs[:n], refs[n:2 * n], refs[2 * n], refs[2 * n + 1]
        x, y, c, _ = _coords()
        cps = []
        for i in range(n):
            if i < ng:
                hr = ins[i].shape[1] // 2
                src = ins[i].at[pl.ds(0, N_CHIP), pl.ds((1 - c) * hr, hr)]
            else:
                hr = ins[i].shape[0] // 2
                src = ins[i].at[pl.ds((1 - c) * hr, hr)]
            cps.append(pltpu.make_async_remote_copy(
                src_ref=src, dst_ref=outs[i], send_sem=send.at[i], recv_sem=recv.at[i],
                device_id=(x, y, 1 - c), device_id_type=MESH))
        for cp in cps:
            cp.start()
        for cp in cps:
            cp.wait()

    out_shape = [jax.ShapeDtypeStruct((N_CHIP, g.shape[1] // 2, g.shape[2]), g.dtype) for g in grads_b]
    out_shape += [jax.ShapeDtypeStruct((s.shape[0] // 2, s.shape[1]), s.dtype) for s in smalls]
    return pl.pallas_call(
        body, out_shape=out_shape, in_specs=[ANY] * n, out_specs=[ANY] * n,
        scratch_shapes=[pltpu.SemaphoreType.DMA((n,)), pltpu.SemaphoreType.DMA((n,))],
        name=name)(*grads_b, *smalls)


def _pair_sum(c_idx, grads, recvd, smalls, smalls_recvd, name):
    ng, ns = len(grads), len(smalls)
    halves = [g.shape[1] // 2 for g in grads]

    def body(c_ref, *refs):
        g_in, r_in = refs[:ng], refs[ng:2 * ng]
        s_in, sr_in = refs[2 * ng:2 * ng + ns], refs[2 * ng + ns:2 * ng + 2 * ns]
        o = refs[2 * ng + 2 * ns:]
        for i in range(ng):
            tot = g_in[i][...] + r_in[i][...].astype(F32)
            o[i][...] = tot
            o[ng + i][...] = tot.astype(BF16)
        for i in range(ns):
            o[2 * ng + i][...] = s_in[i][...] + sr_in[i][...]

    in_specs = [pl.BlockSpec((None, None, halves[i], g.shape[2]), lambda b, c: (b, c[0], 0, 0)) for i, g in enumerate(grads)]
    in_specs += [pl.BlockSpec((None, halves[i], g.shape[2]), lambda b, c: (b, 0, 0)) for i, g in enumerate(grads)]
    in_specs += [pl.BlockSpec((None, s.shape[0] // 2, s.shape[1]), lambda b, c: (c[0], 0, 0)) for s in smalls]
    in_specs += [pl.BlockSpec((s.shape[0] // 2, s.shape[1]), lambda b, c: (0, 0)) for s in smalls]
    blk = [pl.BlockSpec((None, halves[i], g.shape[2]), lambda b, c: (b, 0, 0)) for i, g in enumerate(grads)]
    out_specs = blk + blk + [pl.BlockSpec((s.shape[0] // 2, s.shape[1]), lambda b, c: (0, 0)) for s in smalls]
    out_shape = [jax.ShapeDtypeStruct((N_CHIP, halves[i], g.shape[2]), F32) for i, g in enumerate(grads)]
    out_shape += [jax.ShapeDtypeStruct((N_CHIP, halves[i], g.shape[2]), BF16) for i, g in enumerate(grads)]
    out_shape += [jax.ShapeDtypeStruct((s.shape[0] // 2, s.shape[1]), F32) for s in smalls]
    grads4 = [g.reshape(N_CHIP, 2, halves[i], g.shape[2]) for i, g in enumerate(grads)]
    smalls3 = [s.reshape(2, s.shape[0] // 2, s.shape[1]) for s in smalls]
    return pl.pallas_call(
        body, out_shape=out_shape,
        grid_spec=pltpu.PrefetchScalarGridSpec(num_scalar_prefetch=1, grid=(N_CHIP,), in_specs=in_specs, out_specs=out_specs),
        compiler_params=_cp(("arbitrary",), VMEM_LIMIT), name=name)(c_idx, *grads4, *recvd, *smalls3, *smalls_recvd)


_HBM = pl.BlockSpec(memory_space=pltpu.HBM)
_SEM = pl.BlockSpec(memory_space=pltpu.SEMAPHORE)


def _split_copies(ins, lands, ng, send, recv, arriving):
    x, y, c, chips = _coords()
    b = 2 * x + y
    copies = []
    for i in range(len(ins)):
        for k in range(3):
            blk = 2 * chips[k][0] + chips[k][1]
            src, dst, got = (ins[i].at[blk], lands[i].at[k], lands[i].at[k]) if i < ng else (ins[i], lands[i].at[b], lands[i].at[blk])
            sems = dict(send_sem=send.at[3 * i + k], recv_sem=recv.at[3 * i + k], device_id=(*chips[k], c), device_id_type=MESH)
            if arriving:
                copies.append(pltpu.make_async_remote_copy(src_ref=got, dst_ref=got, **sems))
            else:
                copies.append(pltpu.make_async_remote_copy(src_ref=src, dst_ref=dst, **sems))
    return copies


def _exchange_begin(sums_b, smalls, name):
    ng, n = len(sums_b), len(sums_b) + len(smalls)
    srcs = list(sums_b) + list(smalls)
    lands = [lax.empty((3,) + g.shape[1:], g.dtype) for g in sums_b] + [lax.empty((N_CHIP,) + s.shape, s.dtype) for s in smalls]

    def body(*refs):
        ins, land_refs = refs[:n], refs[n:2 * n]
        send, recv = refs[2 * n], refs[2 * n + 1]
        token = refs[4 * n + 2]
        for cp in _split_copies(ins, land_refs, ng, send, recv, False):
            cp.start()
        token[...] = jnp.zeros_like(token)

    hbm = lambda a: pltpu.HBM(a.shape, a.dtype)
    outs = pl.pallas_call(
        body, name=name,
        out_shape=(pltpu.SemaphoreType.DMA((3 * n,)), pltpu.SemaphoreType.DMA((3 * n,)), *[hbm(a) for a in srcs + lands],
                   jax.ShapeDtypeStruct((8, 128), F32)),
        in_specs=[_HBM] * (2 * n), out_specs=(_SEM, _SEM, *[_HBM] * (2 * n), pl.BlockSpec(memory_space=pltpu.VMEM)),
        input_output_aliases={i: 2 + i for i in range(2 * n)},
        compiler_params=pltpu.CompilerParams(has_side_effects=pltpu.SideEffectType.DATAFLOW_SIDE_EFFECTING),
    )(*[pltpu.with_memory_space_constraint(a, pltpu.HBM) for a in srcs + lands])
    return outs[0], outs[1], list(outs[2:2 + n]), list(outs[2 + n:2 + 2 * n]), outs[2 + 2 * n]


def _exchange_end(send, recv, srcs, lands, ng, after, name):
    n = len(srcs)

    def body(*refs):
        ins, land_refs = refs[:n], refs[n:2 * n]
        send_ref, recv_ref = refs[2 * n], refs[2 * n + 1]
        for cp in _split_copies(ins, land_refs, ng, send_ref, recv_ref, False):
            cp.wait_send()
        for cp in _split_copies(ins, land_refs, ng, send_ref, recv_ref, True):
            cp.wait_recv()

    hbm = lambda a: pltpu.HBM(a.shape, a.dtype)
    outs = pl.pallas_call(
        body, name=name, out_shape=tuple(hbm(a) for a in list(srcs) + list(lands)),
        in_specs=[_HBM] * (2 * n) + [_SEM, _SEM, ANY], out_specs=tuple([_HBM] * (2 * n)),
        input_output_aliases={i: i for i in range(2 * n)},
        compiler_params=pltpu.CompilerParams(has_side_effects=pltpu.SideEffectType.DATAFLOW_SIDE_EFFECTING),
    )(*srcs, *lands, send, recv, after)
    return list(outs[n:])


def _chip_sum(bc_idx, sums, recvd, smalls_slots, smalls_own, name, steps=4):
    ng, ns = len(sums), len(smalls_slots)

    def body(bc_ref, *refs):
        own, rx = refs[:ng], refs[ng:2 * ng]
        sl = refs[2 * ng:2 * ng + ns]
        sl_own = refs[2 * ng + ns:2 * ng + 2 * ns]
        o = refs[2 * ng + 2 * ns:]
        for i in range(ng):
            tot = own[i][...]
            for j in range(3):
                tot = tot + rx[i][j].astype(F32)
            o[i][...] = tot
        for i in range(ns):
            term = [jnp.where(bc_ref[0] == kk, sl_own[i][...], sl[i][kk]) for kk in range(N_CHIP)]
            o[ng + i][...] = ((term[0] + term[1]) + term[2]) + term[3]

    def rows(g):
        return g.shape[1] // steps

    in_specs = [pl.BlockSpec((None, rows(g), g.shape[2]), lambda r, bc: (bc[0], r, 0)) for g in sums]
    in_specs += [pl.BlockSpec((3, rows(g), g.shape[2]), lambda r, bc: (0, r, 0)) for g in sums]
    in_specs += [pl.BlockSpec(s.shape, lambda r, bc: (0, 0, 0)) for s in smalls_slots]
    in_specs += [pl.BlockSpec(s.shape[1:], lambda r, bc: (0, 0)) for s in smalls_slots]
    out_specs = [pl.BlockSpec((rows(g), g.shape[2]), lambda r, bc: (bc[1] * steps + r, 0)) for g in sums]
    out_specs += [pl.BlockSpec(s.shape[1:], lambda r, bc: (bc[1], 0)) for s in smalls_slots]
    out_shape = [jax.ShapeDtypeStruct((2 * g.shape[1], g.shape[2]), F32) for g in sums]
    out_shape += [jax.ShapeDtypeStruct((2 * s.shape[1], s.shape[2]), F32) for s in smalls_slots]
    return pl.pallas_call(
        body, out_shape=out_shape,
        grid_spec=pltpu.PrefetchScalarGridSpec(num_scalar_prefetch=1, grid=(steps,), in_specs=in_specs, out_specs=out_specs),
        compiler_params=_cp(("arbitrary",), VMEM_LIMIT), name=name)(bc_idx, *sums, *recvd, *smalls_slots, *smalls_own)


def _pair_gather(arrs, name):
    n = len(arrs)

    def body(*refs):
        outs = refs[n:2 * n]
        send, recv = refs[2 * n:]
        x, y, c, _ = _coords()
        cps = []
        for i in range(n):
            hr = outs[i].shape[0] // 2
            mine = outs[i].at[pl.ds(c * hr, hr)]
            cps.append(pltpu.make_async_remote_copy(
                src_ref=mine, dst_ref=mine, send_sem=send.at[i], recv_sem=recv.at[i],
                device_id=(x, y, 1 - c), device_id_type=MESH))
        for cp in cps:
            cp.start()
        for i in range(n):
            hr = outs[i].shape[0] // 2
            theirs = outs[i].at[pl.ds((1 - c) * hr, hr)]
            pltpu.make_async_remote_copy(
                src_ref=theirs, dst_ref=theirs, send_sem=send.at[i], recv_sem=recv.at[i],
                device_id=(x, y, 1 - c), device_id_type=MESH).wait_recv()
        for cp in cps:
            cp.wait_send()

    return list(pl.pallas_call(
        body, out_shape=[jax.ShapeDtypeStruct(a.shape, a.dtype) for a in arrs], in_specs=[ANY] * n, out_specs=[ANY] * n,
        input_output_aliases={i: i for i in range(n)},
        scratch_shapes=[pltpu.SemaphoreType.DMA((n,)), pltpu.SemaphoreType.DMA((n,))],
        name=name)(*arrs))


def _adamw_math(w, g, m, v):
    m2 = ADAM_B1 * m + (1.0 - ADAM_B1) * g
    v2 = ADAM_B2 * v + (1.0 - ADAM_B2) * (g * g)
    m_hat = m2 / (1.0 - ADAM_B1 ** ADAM_STEP)
    v_hat = v2 / (1.0 - ADAM_B2 ** ADAM_STEP)
    delta = -ADAM_LR * (m_hat / (jnp.sqrt(v_hat) + ADAM_EPS) + ADAM_WD * w)
    return delta, m2, v2


def _adamw_big(w, g, m, v, name, steps=8):
    r, c = w.shape

    def body(w_ref, g_ref, m_ref, v_ref, d_ref, m2_ref, v2_ref):
        d_ref[...], m2_ref[...], v2_ref[...] = _adamw_math(w_ref[...], g_ref[...], m_ref[...], v_ref[...])

    spec = pl.BlockSpec((r // steps, c), lambda i: (i, 0))
    return pl.pallas_call(
        body, grid=(steps,), in_specs=[spec] * 4, out_specs=[spec] * 3,
        out_shape=[jax.ShapeDtypeStruct((r, c), F32)] * 3,
        compiler_params=_cp(("parallel",), VMEM_LIMIT), name=name)(w, g, m, v)


def _adamw_small(groups):
    n = len(groups)

    def body(*refs):
        for i in range(n):
            w_ref, g_ref, m_ref, v_ref = refs[4 * i:4 * i + 4]
            d_ref, m2_ref, v2_ref = refs[4 * n + 3 * i:4 * n + 3 * i + 3]
            d_ref[...], m2_ref[...], v2_ref[...] = _adamw_math(w_ref[...], g_ref[...], m_ref[...], v_ref[...])

    flat = [a for grp in groups for a in grp]
    out_shape = [jax.ShapeDtypeStruct(grp[0].shape, F32) for grp in groups for _ in range(3)]
    outs = pl.pallas_call(body, out_shape=out_shape, name="adamw_small")(*flat)
    return [tuple(outs[3 * i:3 * i + 3]) for i in range(n)]


def kernel(x, mem, norm_mix_g, w_in, conv_w, gm_ln_g, gm_ln_b, gm_ws, gm_bs, w_out, norm_x_g, norm_mem_g, w_q, w_kv, w_xo, norm_final_g, loss_target, m_norm_mix_g, m_w_in, m_conv_w, m_gm_ln_g, m_gm_ln_b, m_gm_ws, m_gm_bs, m_w_out, m_norm_x_g, m_norm_mem_g, m_w_q, m_w_kv, m_w_xo, m_norm_final_g, v_norm_mix_g, v_w_in, v_conv_w, v_gm_ln_g, v_gm_ln_b, v_gm_ws, v_gm_bs, v_w_out, v_norm_x_g, v_norm_mem_g, v_w_q, v_w_kv, v_w_xo, v_norm_final_g):
    t = x.shape[1]
    xi = lax.axis_index("x")
    yi = lax.axis_index("y")
    ci = lax.axis_index("c")
    b_idx = jnp.reshape(2 * xi + yi, (1,)).astype(jnp.int32)
    c_idx = jnp.reshape(ci, (1,)).astype(jnp.int32)

    x2d, mem2d, tgt = x[0], mem[0], loss_target[0]
    big = [w_in[0], w_out[0], w_q[0], w_kv[0], w_xo[0]]
    big_m = [m_w_in[0], m_w_out[0], m_w_q[0], m_w_kv[0], m_w_xo[0]]
    big_v = [v_w_in[0], v_w_out[0], v_w_q[0], v_w_kv[0], v_w_xo[0]]
    g3 = norm_final_g.reshape(1, D)

    def pad8(a):
        return jnp.pad(a, ((0, 8 - a.shape[0]), (0, 0)))

    own_blocks = _cast_shards(b_idx, big)

    tril = jnp.tril(jnp.ones((CH, CH), bool))
    wc32 = jnp.where(tril[None], gm_ws[0], 0.0)
    wc = wc32.astype(BF16)
    wct = jnp.swapaxes(wc32, 1, 2).astype(BF16)
    bsb = jnp.broadcast_to(gm_bs[0][:, :, None], (HEADS, CH, CH))

    blk = 2 * xi + yi
    order = jnp.stack([blk, blk ^ 2, blk ^ 1, blk ^ 3]).astype(jnp.int32)
    proj, ht, win_f, cw8, (wout_f, wkv_f) = _proj_gather(
        order, x2d, norm_mix_g, own_blocks[0], pad8(conv_w[0]), [own_blocks[1], own_blocks[3]])
    mixin, mixt, (wq_f, wxo_f) = _mixer_fwd(proj, cw8, gm_ln_g, gm_ln_b, wc, bsb, [own_blocks[2], own_blocks[4]])
    wout2, wq2, wxo2 = wout_f.reshape(MIX, D), wq_f.reshape(D, D), wxo_f.reshape(D, D)
    k, v, mt = _mem_fwd(mem2d, norm_mem_g, wkv_f)
    del mt

    (loss_tile, dmix, dx1b, h2t, dq, ot, dx2b, dk, dv, dg2, dg3) = _tail(
        x2d, tgt, mixin, wout2, wq2, wxo2, k, v, norm_x_g, g3)
    dwkv, dwkv_b, dgm = _mem_bwd(mem2d, norm_mem_g, dk, dv, wkv_f)
    dwxo, dwxo_b = _grad_matmul(ot, dx2b, by_cols=False, name="grad_w_xo", tk=2048)
    dwq, dwq_b = _grad_matmul(h2t, dq, by_cols=False, name="grad_w_q", tk=2048)
    dwout, dwout_b = _grad_matmul(mixt, dx1b, by_cols=False, name="grad_w_out")

    bc_idx = jnp.concatenate([b_idx, c_idx])
    rx1a = _pair_exchange([dwout_b, dwq_b, dwkv_b, dwxo_b], [], "pair_exchange_a")
    ps_a = _pair_sum(c_idx, [dwout, dwq, dwkv, dwxo], rx1a, [], [], "pair_sum_a")
    sums_a, sums_a_b = list(ps_a[:4]), list(ps_a[4:8])
    send_a, recv_a, src_a, land_a, token_a = _exchange_begin(sums_a_b, [], "exchange_a_begin")
    dproj, dcw, dlng, dlnb, dwc, dbs8, grad_x, dg1 = _mixer_bwd(
        proj, dmix, cw8, gm_ln_g, gm_ln_b, wc, wct, bsb, win_f, x2d, dx1b, norm_mix_g, token_a)

    dwin, dwin_b = _grad_matmul(ht, dproj, by_cols=True, name="grad_w_in", tk=2048)
    zero = jnp.zeros((1, D), F32)
    loss_row = jnp.broadcast_to(loss_tile[0:1, 0:1], (1, D))
    sv = jnp.concatenate([dg1[0:1], dg2, dgm, dg3, dlng, dlnb, dbs8[0:1], loss_row, dcw], axis=0)
    sw = dwc.reshape(HEADS * CH, CH)
    rx1b = _pair_exchange([dwin_b], [sv, sw], "pair_exchange_b")
    ps_b = _pair_sum(c_idx, [dwin], rx1b[:1], [sv, sw], rx1b[1:], "pair_sum_b")
    sums_b, sums_b_b, psmall = list(ps_b[:1]), list(ps_b[1:2]), list(ps_b[2:])
    send_b, recv_b, src_b, land_b, token_b = _exchange_begin(sums_b_b, psmall, "exchange_b_begin")

    rx2a = _exchange_end(send_a, recv_a, src_a, land_a, 4, token_b, "exchange_a_end")
    red_a = _chip_sum(bc_idx, sums_a, rx2a, [], [], "chip_sum_a")
    gwout, gwq, gwkv, gwxo = _pair_gather(red_a, "pair_gather_a")
    names = ["w_out", "w_q", "w_kv", "w_xo"]
    out_a = [_adamw_big(w, g, m, v, "adamw_" + nm)
             for w, g, m, v, nm in zip(big[1:], [gwout, gwq, gwkv, gwxo], big_m[1:], big_v[1:], names)]

    rx2b = _exchange_end(send_b, recv_b, src_b, land_b, 1, out_a[-1][0], "exchange_b_end")
    red_b = _chip_sum(bc_idx, sums_b, rx2b[:1], rx2b[1:], psmall, "chip_sum_b")
    gwin, svf, swf = _pair_gather(red_b, "pair_gather_b")
    big_out = [_adamw_big(big[0], gwin, big_m[0], big_v[0], "adamw_w_in")] + out_a

    def vec_pack(a1, a2, am, a3, lg, lb, bs):
        return jnp.concatenate([a1, a2, am, a3.reshape(1, D), lg, lb, bs.reshape(1, D), zero], axis=0)

    wv = vec_pack(norm_mix_g, norm_x_g, norm_mem_g, norm_final_g, gm_ln_g, gm_ln_b, gm_bs)
    mv = vec_pack(m_norm_mix_g, m_norm_x_g, m_norm_mem_g, m_norm_final_g, m_gm_ln_g, m_gm_ln_b, m_gm_bs)
    vv = vec_pack(v_norm_mix_g, v_norm_x_g, v_norm_mem_g, v_norm_final_g, v_gm_ln_g, v_gm_ln_b, v_gm_bs)
    loss = svf[7, 0]
    gcw = lax.dynamic_slice_in_dim(svf[8:16], blk * (D // N_CHIP), D // N_CHIP, axis=1)
    gws = swf
    gv = svf[0:8]
    (dv_, mv_, vv_), (dc_, mc_, vc_), (dws_, mws_, vws_) = _adamw_small([
        (wv, gv, mv, vv),
        (pad8(conv_w[0]), gcw, pad8(m_conv_w[0]), pad8(v_conv_w[0])),
        (gm_ws.reshape(HEADS * CH, CH), gws, m_gm_ws.reshape(HEADS * CH, CH), v_gm_ws.reshape(HEADS * CH, CH))])

    def unpack(vecs, cw, ws, bigs):
        r = lambda i: vecs[i:i + 1]
        return [r(0), bigs[0][None], cw[0:3][None], r(4), r(5), ws.reshape(1, HEADS, CH, CH), vecs[6].reshape(1, HEADS, CH),
                bigs[1][None], r(1), r(2), bigs[2][None], bigs[3][None], bigs[4][None], vecs[3]]

    grads_out = unpack(gv, gcw, gws, [gwin, gwout, gwq, gwkv, gwxo])
    delta_out = unpack(dv_, dc_, dws_, [o[0] for o in big_out])
    m_out = unpack(mv_, mc_, mws_, [o[1] for o in big_out])
    v_out = unpack(vv_, vc_, vws_, [o[2] for o in big_out])
    return (loss, grad_x[None], *grads_out, *delta_out, *m_out, *v_out)
```

```python
import functools
import math

import jax
import jax.numpy as jnp
from jax import lax
from jax.experimental import pallas as pl
from jax.experimental.pallas import tpu as pltpu

F32 = jnp.float32
BF16 = jnp.bfloat16
MESH = pl.DeviceIdType.MESH

D = 1024
SLAB = 1024
N_SLAB = 7
IN_DIM = N_SLAB * SLAB
MIX = 2 * SLAB
HEADS = 8
CH = 128
XH = 4
XD = D // XH
EPS = 1e-6
GELU_C = math.sqrt(2.0 / math.pi)
GELU_A = 0.044715
N_CHIP = 4
IN_BLK = IN_DIM // N_CHIP
KV_BLK = 2 * D // N_CHIP

ADAM_LR, ADAM_B1, ADAM_B2, ADAM_EPS, ADAM_WD, ADAM_STEP = 0.001, 0.9, 0.999, 1e-08, 0.01, 10

VMEM_LIMIT = 60 * 1024 * 1024


def _cp(sem=None, vmem=None):
    return pltpu.CompilerParams(dimension_semantics=sem, vmem_limit_bytes=vmem)


def _full(shape, buffers=None):
    n = len(shape)
    if buffers is None:
        return pl.BlockSpec(shape, lambda *_: (0,) * n)
    return pl.BlockSpec(shape, lambda *_: (0,) * n, pipeline_mode=pl.Buffered(buffers))


ANY = pl.BlockSpec(memory_space=pl.ANY)


def _bdot(a, b):
    return jnp.dot(a.astype(BF16), b.astype(BF16), preferred_element_type=F32)


def _bdot_nt(a, b):
    return lax.dot_general(a.astype(BF16), b.astype(BF16), (((1,), (1,)), ((), ())), preferred_element_type=F32)


def _rms(x, g):
    r = lax.rsqrt(jnp.mean(x * x, axis=-1, keepdims=True) + EPS)
    return x * r * g, r


def _rms_bwd(dy, x, r, g):
    gdy = dy * g
    dx = r * gdy - x * (r * r * r) * jnp.mean(x * gdy, axis=-1, keepdims=True)
    dg = jnp.sum(dy * x * r, axis=0, keepdims=True)
    return dx, dg


def _gelu_parts(x):
    x2 = x * x
    t = jnp.tanh(GELU_C * (x + GELU_A * x * x2))
    val = 0.5 * x * (1.0 + t)
    grad = 0.5 * (1.0 + t) + 0.5 * x * (1.0 - t * t) * (GELU_C * (1.0 + 3.0 * GELU_A * x2))
    return val, grad


def _gelu(x):
    return 0.5 * x * (1.0 + jnp.tanh(GELU_C * (x + GELU_A * x * x * x)))


def _sigmoid(z):
    return 1.0 / (1.0 + jnp.exp(-z))


def _cast_shards(b_idx, arrs):
    n = len(arrs)
    steps = 8

    def body(b_ref, *refs):
        for i in range(n):
            refs[n + i][...] = refs[i][...].astype(BF16)

    in_specs = [pl.BlockSpec((a.shape[0] // steps, a.shape[1]), lambda i, b: (i, 0)) for a in arrs]
    out_specs = [pl.BlockSpec((None, a.shape[0] // steps, a.shape[1]), lambda i, b: (b[0], i, 0)) for a in arrs]
    return pl.pallas_call(
        body, out_shape=[jax.ShapeDtypeStruct((N_CHIP,) + a.shape, BF16) for a in arrs],
        grid_spec=pltpu.PrefetchScalarGridSpec(num_scalar_prefetch=1, grid=(steps,), in_specs=in_specs, out_specs=out_specs),
        compiler_params=_cp(("arbitrary",)), name="cast_shards")(b_idx, *arrs)


def _proj_gather(order, x, g, win_own, cw8s, more, tm=1024):
    t = x.shape[0]
    ni = t // tm
    hr = D // 2
    nm = len(more)

    def body(*refs):
        order_ref, x_ref, g_ref, win_in, cw_in = refs[:5]
        o_ref, ht_ref, win_f, cw_out = refs[5 + nm:9 + nm]
        more_out = refs[9 + nm:9 + 2 * nm]
        hbuf, wv, ici_s, ici_r, d2d_s, d2d_r, cw_s, cw_r, loc, m_is, m_ir, m_ds, m_dr = refs[9 + 2 * nm:]
        more_start, more_finish = _gather_steps(more_out, m_is, m_ir, m_ds, m_dr)
        j, i = pl.program_id(0), pl.program_id(1)
        x, y, c, chips = _coords()
        b = 2 * x + y
        sib = (x, y, 1 - c)
        blks = [2 * chip[0] + chip[1] for chip in chips]

        def half(blk, hc):
            return win_f.at[blk, pl.ds(hc * hr, hr)]

        def cw_cols(blk):
            return cw_out.at[:, pl.ds(blk * (D // N_CHIP), D // N_CHIP)]

        def ici(k, blk):
            return pltpu.make_async_remote_copy(src_ref=half(blk, c), dst_ref=half(blk, c), send_sem=ici_s.at[k],
                                                recv_sem=ici_r.at[k], device_id=(*chips[k], c), device_id_type=MESH)

        def relay(k, blk, hc):
            return pltpu.make_async_remote_copy(src_ref=half(blk, hc), dst_ref=half(blk, hc), send_sem=d2d_s.at[k],
                                                recv_sem=d2d_r.at[k], device_id=sib, device_id_type=MESH)

        def cw_copy(k, blk):
            src = cw_in if blk is None else cw_cols(blk)
            return pltpu.make_async_remote_copy(src_ref=src, dst_ref=cw_cols(b if blk is None else blk), send_sem=cw_s.at[k],
                                                recv_sem=cw_r.at[k], device_id=(*chips[k], c), device_id_type=MESH)

        cw_local = pltpu.make_async_copy(cw_in, cw_cols(b), loc.at[1])

        @pl.when((j == 0) & (i == 0))
        def _():
            cw_local.start()
            for k in range(3):
                ici(k, b).start()
            for k in range(3):
                cw_copy(k, None).start()
            more_start()

        for jj in range(N_CHIP):
            @pl.when((j == jj) & (i == 0))
            def _(jj=jj):
                if jj == 0:
                    blk = b
                else:
                    blk = blks[jj - 1]
                    ici(jj - 1, blk).wait_recv()
                    relay(jj - 1, blk, c).start()
                    relay(jj - 1, blk, 1 - c).wait_recv()
                load = pltpu.make_async_copy(win_f.at[blk], wv, loc.at[0])
                load.start()
                load.wait()

        rows = pl.ds(pl.multiple_of(i * tm, tm), tm)

        @pl.when(j == 0)
        def _():
            h, _ = _rms(x_ref[...], g_ref[...])
            hbuf[rows, :] = h.astype(BF16)
            ht_ref[...] = h.T.astype(BF16)

        o_ref[...] = jnp.dot(hbuf[rows, :], wv[...], preferred_element_type=F32).astype(BF16)

        @pl.when((j == N_CHIP - 1) & (i == ni - 1))
        def _():
            for k in range(3):
                cw_copy(k, blks[k]).wait_recv()
            for k in range(3):
                ici(k, b).wait_send()
                relay(k, blks[k], c).wait_send()
                cw_copy(k, None).wait_send()
            cw_local.wait()
            more_finish()

    first = lambda j, i: jnp.where(j == 0, i, ni - 1)
    in_specs = [pl.BlockSpec((tm, D), lambda j, i, o: (first(j, i), 0)), pl.BlockSpec((1, D), lambda j, i, o: (0, 0)),
                ANY, ANY] + [ANY] * nm
    out_specs = [pl.BlockSpec((tm, IN_BLK), lambda j, i, o: (i, o[j])),
                 pl.BlockSpec((D, tm), lambda j, i, o: (0, first(j, i))), ANY, ANY] + [ANY] * nm
    outs = pl.pallas_call(
        body, out_shape=[jax.ShapeDtypeStruct((t, IN_DIM), BF16), jax.ShapeDtypeStruct((D, t), BF16),
                         jax.ShapeDtypeStruct(win_own.shape, BF16), jax.ShapeDtypeStruct((8, D), F32)]
        + [jax.ShapeDtypeStruct(f.shape, f.dtype) for f in more],
        grid_spec=pltpu.PrefetchScalarGridSpec(
            num_scalar_prefetch=1, grid=(N_CHIP, ni), in_specs=in_specs, out_specs=out_specs,
            scratch_shapes=[pltpu.VMEM((t, D), BF16), pltpu.VMEM((D, IN_BLK), BF16)]
            + [pltpu.SemaphoreType.DMA((3,))] * 6 + [pltpu.SemaphoreType.DMA((2,))]
            + [pltpu.SemaphoreType.DMA((max(nm, 1), 3))] * 4),
        input_output_aliases={3: 2, **{5 + w: 4 + w for w in range(nm)}},
        compiler_params=_cp(("arbitrary", "arbitrary"), VMEM_LIMIT), name="proj_gather")(order, x, g, win_own, cw8s, *more)
    return outs[0], outs[1], outs[2], outs[3], outs[4:]


def _gather_steps(outs, ici_s, ici_r, d2d_s, d2d_r):
    x, y, c, chips = _coords()
    b = 2 * x + y
    sib = (x, y, 1 - c)
    nw = len(outs)

    def half(w, blk, hc):
        hr = outs[w].shape[1] // 2
        return outs[w].at[blk, pl.ds(hc * hr, hr)]

    def ici(w, k, blk):
        return pltpu.make_async_remote_copy(src_ref=half(w, blk, c), dst_ref=half(w, blk, c), send_sem=ici_s.at[w, k],
                                            recv_sem=ici_r.at[w, k], device_id=(*chips[k], c), device_id_type=MESH)

    def relay(w, k, blk, hc):
        return pltpu.make_async_remote_copy(src_ref=half(w, blk, hc), dst_ref=half(w, blk, hc), send_sem=d2d_s.at[w, k],
                                            recv_sem=d2d_r.at[w, k], device_id=sib, device_id_type=MESH)

    def start():
        for w in range(nw):
            for k in range(3):
                ici(w, k, b).start()

    def finish():
        for w in range(nw):
            for k in range(3):
                blk = 2 * chips[k][0] + chips[k][1]
                ici(w, k, blk).wait_recv()
                relay(w, k, blk, c).start()
        for w in range(nw):
            for k in range(3):
                blk = 2 * chips[k][0] + chips[k][1]
                relay(w, k, blk, 1 - c).wait_recv()
        for w in range(nw):
            for k in range(3):
                blk = 2 * chips[k][0] + chips[k][1]
                ici(w, k, b).wait_send()
                relay(w, k, blk, c).wait_send()

    return start, finish


def _mixer_fwd(proj, cw8, lng, lnb, wc, bsb, fulls, tm=256):
    t = proj.shape[0]
    nt = t // tm
    nch = tm // CH
    nw = len(fulls)

    def body(*refs):
        p_ref, cw_ref, lng_ref, lnb_ref, wc_ref, bsb_ref = refs[:6]
        mix_ref, mixt_ref = refs[6 + nw:8 + nw]
        w_outs = refs[8 + nw:8 + 2 * nw]
        prev_ref, stage_ref, ici_s, ici_r, d2d_s, d2d_r = refs[8 + 2 * nw:]
        gather_start, gather_finish = _gather_steps(w_outs, ici_s, ici_r, d2d_s, d2d_r)

        @pl.when(pl.program_id(0) == 0)
        def _():
            gather_start()
            prev_ref[...] = jnp.zeros_like(prev_ref)

        rows = lax.broadcasted_iota(jnp.int32, (tm, CH), 0)
        for s in range(HEADS):
            cs = pl.ds(CH * s, CH)

            def slab(k):
                return p_ref[:, pl.ds(k * SLAB + CH * s, CH)].astype(F32)

            gb, gc, xa, za = slab(0), slab(1), slab(2), slab(3)
            cx = gc * xa
            p6 = jnp.broadcast_to(prev_ref[6:7, cs], (tm, CH))
            p7 = jnp.broadcast_to(prev_ref[7:8, cs], (tm, CH))
            c1 = jnp.where(rows == 0, p7, pltpu.roll(cx, 1, 0))
            c2 = jnp.where(rows == 0, p6, jnp.where(rows == 1, p7, pltpu.roll(cx, 2, 0)))
            prev_ref[:, cs] = cx[tm - 8:, :]
            cv = cw_ref[0:1, cs] * c2 + cw_ref[1:2, cs] * c1 + cw_ref[2:3, cs] * cx
            stage_ref[:, cs] = gb * cv * (za * _sigmoid(za))

            u, v, zb = slab(4), slab(5), slab(6)
            ug, vg = _gelu(u), _gelu(v)
            dlt = vg - jnp.mean(vg, axis=-1, keepdims=True)
            vhat = dlt * lax.rsqrt(jnp.mean(dlt * dlt, axis=-1, keepdims=True) + EPS)
            vn = (vhat * lng_ref[:, cs] + lnb_ref[:, cs]).astype(BF16)
            gate = ug * (zb * _sigmoid(zb))
            for c in range(nch):
                rs = slice(CH * c, CH * (c + 1))
                sp = jnp.dot(wc_ref[s], vn[rs], preferred_element_type=F32) + bsb_ref[s]
                stage_ref[rs, pl.ds(SLAB + CH * s, CH)] = gate[rs] * sp

        full = stage_ref[...]
        mix_ref[...] = full.astype(BF16)
        mixt_ref[...] = full.T.astype(BF16)

        @pl.when(pl.program_id(0) == nt - 1)
        def _():
            gather_finish()

    sems = [pltpu.SemaphoreType.DMA((nw, 3))] * 4
    outs = pl.pallas_call(
        body, grid=(nt,),
        in_specs=[pl.BlockSpec((tm, IN_DIM), lambda i: (i, 0)), _full((8, D)), _full((1, D)), _full((1, D)),
                  _full((HEADS, CH, CH)), _full((HEADS, CH, CH))] + [ANY] * nw,
        out_specs=[pl.BlockSpec((tm, MIX), lambda i: (i, 0)), pl.BlockSpec((MIX, tm), lambda i: (0, i))] + [ANY] * nw,
        out_shape=[jax.ShapeDtypeStruct((t, MIX), BF16), jax.ShapeDtypeStruct((MIX, t), BF16)]
        + [jax.ShapeDtypeStruct(f.shape, f.dtype) for f in fulls],
        input_output_aliases={6 + w: 2 + w for w in range(nw)},
        scratch_shapes=[pltpu.VMEM((8, D), F32), pltpu.VMEM((tm, MIX), F32)] + sems,
        compiler_params=_cp(("arbitrary",), VMEM_LIMIT), name="mixer_fwd")(proj, cw8, lng, lnb, wc, bsb, *fulls)
    return outs[0], outs[1], outs[2:]


def _mem_fwd(mem, gm, wkv_f):
    n_mem = mem.shape[0]

    def body(mem_ref, gm_ref, w_ref, k_ref, v_ref, mt_ref):
        m, _ = _rms(mem_ref[...], gm_ref[...])
        mb = m.astype(BF16)
        mt_ref[...] = m.T.astype(BF16)
        for j in range(N_CHIP):
            dst = k_ref if j < 2 else v_ref
            dst[:, pl.ds(KV_BLK * (j % 2), KV_BLK)] = jnp.dot(mb, w_ref[j], preferred_element_type=F32).astype(BF16)

    return pl.pallas_call(
        body, out_shape=[jax.ShapeDtypeStruct((n_mem, D), BF16), jax.ShapeDtypeStruct((n_mem, D), BF16),
                         jax.ShapeDtypeStruct((D, n_mem), BF16)],
        compiler_params=_cp(None, VMEM_LIMIT), name="mem_fwd")(mem, gm, wkv_f)


def _tail(x, tgt, mixin, wout, wq, wxo, k, v, g2, g3, tm=512, sub=512):
    t = x.shape[0]
    n_mem = k.shape[0]
    scale = 1.0 / math.sqrt(XD)

    def body(x_ref, tgt_ref, mix_ref, wout_ref, wq_ref, wxo_ref, k_ref, v_ref, g2_ref, g3_ref,
             loss_ref, dmix_ref, dx1b_ref, h2t_ref, dq_ref, ot_ref, dx2b_ref, dk_ref, dv_ref, dg2_ref, dg3_ref):
        @pl.when(pl.program_id(0) == 0)
        def _():
            loss_ref[...] = jnp.zeros_like(loss_ref)
            dk_ref[...] = jnp.zeros_like(dk_ref)
            dv_ref[...] = jnp.zeros_like(dv_ref)
            dg2_ref[...] = jnp.zeros_like(dg2_ref)
            dg3_ref[...] = jnp.zeros_like(dg3_ref)

        g2, g3 = g2_ref[...], g3_ref[...]
        for sb in range(tm // sub):
            rs = pl.ds(sub * sb, sub)
            x1 = x_ref[rs, :] + jnp.dot(mix_ref[rs, :], wout_ref[...], preferred_element_type=F32)
            h2, r2 = _rms(x1, g2)
            h2t_ref[:, rs] = h2.T.astype(BF16)
            q = _bdot(h2, wq_ref[...]).astype(BF16)
            probs, outs = [], []
            for hd in range(XH):
                hs = pl.ds(XD * hd, XD)
                s = _bdot_nt(q[:, XD * hd:XD * (hd + 1)], k_ref[:, hs]) * scale
                e = jnp.exp(s - jnp.max(s, axis=-1, keepdims=True))
                p = e / jnp.sum(e, axis=-1, keepdims=True)
                probs.append(p)
                outs.append(_bdot(p, v_ref[:, hs]))
            o = jnp.concatenate(outs, axis=-1)
            ot_ref[:, rs] = o.T.astype(BF16)
            x2 = x1 + _bdot(o, wxo_ref[...])
            y, r3 = _rms(x2, g3)
            diff = y - tgt_ref[rs, :]
            row_loss = jnp.sum(diff * diff, axis=-1, keepdims=True)
            loss_ref[...] += jnp.broadcast_to(jnp.sum(row_loss, axis=0, keepdims=True) * (0.5 / D), loss_ref.shape)

            dx2, dg3 = _rms_bwd(diff * (1.0 / D), x2, r3, g3)
            dg3_ref[...] += dg3
            dx2b = dx2.astype(BF16)
            dx2b_ref[rs, :] = dx2b
            do = _bdot_nt(dx2b, wxo_ref[...])
            dqs = []
            for hd in range(XH):
                hs = pl.ds(XD * hd, XD)
                p = probs[hd]
                do_h = do[:, XD * hd:XD * (hd + 1)]
                dv_ref[:, hs] += _bdot(p.T, do_h)
                dp = _bdot_nt(do_h, v_ref[:, hs])
                ds = p * (dp - jnp.sum(dp * p, axis=-1, keepdims=True))
                dqs.append(_bdot(ds, k_ref[:, hs]) * scale)
                dk_ref[:, hs] += _bdot(ds.T, q[:, XD * hd:XD * (hd + 1)]) * scale
            dq = jnp.concatenate(dqs, axis=-1).astype(BF16)
            dq_ref[rs, :] = dq
            dx1n, dg2 = _rms_bwd(_bdot_nt(dq, wq_ref[...]), x1, r2, g2)
            dg2_ref[...] += dg2
            dx1b = (dx2 + dx1n).astype(BF16)
            dx1b_ref[rs, :] = dx1b
            dmix_ref[rs, :] = _bdot_nt(dx1b, wout_ref[...]).astype(BF16)

    tok = lambda w: pl.BlockSpec((tm, w), lambda i: (i, 0))
    tok_t = lambda w: pl.BlockSpec((w, tm), lambda i: (0, i))
    return pl.pallas_call(
        body, grid=(t // tm,),
        in_specs=[tok(D), tok(D), tok(MIX), _full((MIX, D), 1), _full((D, D), 1), _full((D, D), 1),
                  _full((n_mem, D), 1), _full((n_mem, D), 1), _full((1, D)), _full((1, D))],
        out_specs=[_full((8, 128)), tok(MIX), tok(D), tok_t(D), tok(D), tok_t(D), tok(D),
                   _full((n_mem, D)), _full((n_mem, D)), _full((1, D)), _full((1, D))],
        out_shape=[jax.ShapeDtypeStruct((8, 128), F32), jax.ShapeDtypeStruct((t, MIX), BF16),
                   jax.ShapeDtypeStruct((t, D), BF16),
                   jax.ShapeDtypeStruct((D, t), BF16), jax.ShapeDtypeStruct((t, D), BF16),
                   jax.ShapeDtypeStruct((D, t), BF16), jax.ShapeDtypeStruct((t, D), BF16),
                   jax.ShapeDtypeStruct((n_mem, D), F32), jax.ShapeDtypeStruct((n_mem, D), F32),
                   jax.ShapeDtypeStruct((1, D), F32), jax.ShapeDtypeStruct((1, D), F32)],
        compiler_params=_cp(("arbitrary",), VMEM_LIMIT), name="tail")(x, tgt, mixin, wout, wq, wxo, k, v, g2, g3)


def _mem_bwd(mem, gm, dk, dv, wkv_f):
    def body(mem_ref, gm_ref, dk_ref, dv_ref, w_ref, dw_ref, dwb_ref, dgm_ref):
        mem_v = mem_ref[...]
        m, rm = _rms(mem_v, gm_ref[...])
        mt = m.T.astype(BF16)
        dm = jnp.zeros_like(mem_v)
        for j in range(N_CHIP):
            src = dk_ref if j < 2 else dv_ref
            dkv = src[:, pl.ds(KV_BLK * (j % 2), KV_BLK)].astype(BF16)
            dw = jnp.dot(mt, dkv, preferred_element_type=F32)
            dw_ref[j] = dw
            dwb_ref[j] = dw.astype(BF16)
            dm = dm + _bdot_nt(dkv, w_ref[j])
        dgm_ref[...] = jnp.sum(dm * mem_v * rm, axis=0, keepdims=True)

    return pl.pallas_call(
        body, out_shape=[jax.ShapeDtypeStruct((N_CHIP, D, KV_BLK), F32), jax.ShapeDtypeStruct((N_CHIP, D, KV_BLK), BF16),
                         jax.ShapeDtypeStruct((1, D), F32)],
        compiler_params=_cp(None, VMEM_LIMIT), name="mem_bwd")(mem, gm, dk, dv, wkv_f)


def _mixer_bwd(proj, dmix, cw8, lng, lnb, wc, wct, bsb, win_f, x, dx1, g1, after, tm=256):
    t = proj.shape[0]
    nt = t // tm
    nch = tm // CH
    hb = 16
    pair = 2 * CH

    def body(p_ref, pgc_ref, pxa_ref, dm_ref, cw_ref, lng_ref, lnb_ref, wc_ref, wct_ref, bsb_ref, w_ref, x_ref,
             dx1_ref, g1_ref, after_ref, dp_ref, dcw_ref, dlng_ref, dlnb_ref, dwc_ref, dbs_ref, gx_ref, dg1_ref,
             next_ref, dh_ref):
        i = pl.program_id(0)

        @pl.when(i == 0)
        def _():
            next_ref[...] = jnp.zeros_like(next_ref)
            dcw_ref[...] = jnp.zeros_like(dcw_ref)
            dlng_ref[...] = jnp.zeros_like(dlng_ref)
            dlnb_ref[...] = jnp.zeros_like(dlnb_ref)
            dwc_ref[...] = jnp.zeros_like(dwc_ref)
            dbs_ref[...] = jnp.zeros_like(dbs_ref)
            dg1_ref[...] = jnp.zeros_like(dg1_ref)

        first_tile = i == nt - 1
        rows = lax.broadcasted_iota(jnp.int32, (tm, CH), 0)
        ones8 = jnp.ones((8, CH), BF16)
        for s in range(HEADS):
            cs = pl.ds(CH * s, CH)

            def slab(k):
                return p_ref[:, pl.ds(k * SLAB + CH * s, CH)].astype(F32)

            gb, gc, xa, za = slab(0), slab(1), slab(2), slab(3)
            da = dm_ref[:, cs].astype(F32)
            cx = gc * xa
            cxp = pgc_ref[:, cs].astype(F32) * pxa_ref[:, cs].astype(F32)
            cxp = jnp.where(first_tile, jnp.zeros_like(cxp), cxp)
            p6 = jnp.broadcast_to(cxp[hb - 2:hb - 1, :], (tm, CH))
            p7 = jnp.broadcast_to(cxp[hb - 1:hb, :], (tm, CH))
            c1 = jnp.where(rows == 0, p7, pltpu.roll(cx, 1, 0))
            c2 = jnp.where(rows == 0, p6, jnp.where(rows == 1, p7, pltpu.roll(cx, 2, 0)))
            w0, w1, w2 = cw_ref[0:1, cs], cw_ref[1:2, cs], cw_ref[2:3, cs]
            cv = w0 * c2 + w1 * c1 + w2 * cx
            sg = _sigmoid(za)
            sa = za * sg
            dcv = da * gb * sa
            dp_ref[:, pl.ds(0 * SLAB + CH * s, CH)] = (da * cv * sa).astype(BF16)
            dp_ref[:, pl.ds(3 * SLAB + CH * s, CH)] = (da * gb * cv * (sg * (1.0 + za * (1.0 - sg)))).astype(BF16)
            n0 = jnp.broadcast_to(next_ref[0:1, cs], (tm, CH))
            n1 = jnp.broadcast_to(next_ref[1:2, cs], (tm, CH))
            u1 = jnp.where(rows == tm - 1, n0, pltpu.roll(dcv, tm - 1, 0))
            u2 = jnp.where(rows == tm - 2, n0, jnp.where(rows == tm - 1, n1, pltpu.roll(dcv, tm - 2, 0)))
            next_ref[:, cs] = dcv[0:8, :]
            dcx = w2 * dcv + w1 * u1 + w0 * u2
            dp_ref[:, pl.ds(1 * SLAB + CH * s, CH)] = (dcx * xa).astype(BF16)
            dp_ref[:, pl.ds(2 * SLAB + CH * s, CH)] = (dcx * gc).astype(BF16)
            dcw_ref[0:1, cs] += jnp.sum(dcv * c2, axis=0, keepdims=True)
            dcw_ref[1:2, cs] += jnp.sum(dcv * c1, axis=0, keepdims=True)
            dcw_ref[2:3, cs] += jnp.sum(dcv * cx, axis=0, keepdims=True)

            u, v, zb = slab(4), slab(5), slab(6)
            db = dm_ref[:, pl.ds(SLAB + CH * s, CH)].astype(F32)
            ug, ugrad = _gelu_parts(u)
            vg, vgrad = _gelu_parts(v)
            dlt = vg - jnp.mean(vg, axis=-1, keepdims=True)
            rstd = lax.rsqrt(jnp.mean(dlt * dlt, axis=-1, keepdims=True) + EPS)
            vhat = dlt * rstd
            lg = lng_ref[:, cs]
            vn = (vhat * lg + lnb_ref[:, cs]).astype(BF16)
            sgb = _sigmoid(zb)
            szb = zb * sgb
            sps, dvns = [], []
            dbs = jnp.zeros((8, CH), F32)
            dwc = jnp.zeros((CH, CH), F32)
            for c in range(nch):
                rs = slice(CH * c, CH * (c + 1))
                sp = jnp.dot(wc_ref[s], vn[rs], preferred_element_type=F32) + bsb_ref[s]
                dsp = (db[rs] * ug[rs] * szb[rs]).astype(BF16)
                dbs = dbs + lax.dot_general(ones8, dsp, (((1,), (1,)), ((), ())), preferred_element_type=F32)
                dwc = dwc + lax.dot_general(dsp, vn[rs], (((1,), (1,)), ((), ())), preferred_element_type=F32)
                dvns.append(jnp.dot(wct_ref[s], dsp, preferred_element_type=F32))
                sps.append(sp)
            sp = jnp.concatenate(sps, axis=0)
            dvn = jnp.concatenate(dvns, axis=0)
            dbs_ref[:, cs] += dbs
            dwc_ref[s] += dwc
            dlng_ref[:, cs] += jnp.sum(dvn * vhat, axis=0, keepdims=True)
            dlnb_ref[:, cs] += jnp.sum(dvn, axis=0, keepdims=True)
            dvhat = dvn * lg
            dvg = rstd * (dvhat - jnp.mean(dvhat, axis=-1, keepdims=True)
                          - vhat * jnp.mean(dvhat * vhat, axis=-1, keepdims=True))
            dp_ref[:, pl.ds(4 * SLAB + CH * s, CH)] = (db * sp * szb * ugrad).astype(BF16)
            dp_ref[:, pl.ds(5 * SLAB + CH * s, CH)] = (dvg * vgrad).astype(BF16)
            dp_ref[:, pl.ds(6 * SLAB + CH * s, CH)] = (db * ug * sp * (sgb * (1.0 + zb * (1.0 - sgb)))).astype(BF16)

            if s % 2 == 1:
                part = None
                for k in range(N_SLAB):
                    col = k * SLAB + pair * (s // 2)
                    blk, off = divmod(col, IN_BLK)
                    term = lax.dot_general(dp_ref[:, pl.ds(col, pair)], w_ref[blk, :, pl.ds(off, pair)],
                                           (((1,), (1,)), ((), ())), preferred_element_type=F32)
                    part = term if part is None else part + term
                if s == 1:
                    dh_ref[...] = part
                else:
                    dh_ref[...] += part

        xv = x_ref[...]
        r = lax.rsqrt(jnp.mean(xv * xv, axis=-1, keepdims=True) + EPS)
        dxn, dg = _rms_bwd(dh_ref[...], xv, r, g1_ref[...])
        gx_ref[...] = dx1_ref[...].astype(F32) + dxn
        dg1_ref[0:1, :] += dg

        @pl.when(i == nt - 1)
        def _():
            tril = lax.broadcasted_iota(jnp.int32, (CH, CH), 0) >= lax.broadcasted_iota(jnp.int32, (CH, CH), 1)
            for s in range(HEADS):
                dwc_ref[s] = jnp.where(tril, dwc_ref[s], 0.0)

    rev = lambda i: nt - 1 - i
    halo = lambda col: pl.BlockSpec((hb, SLAB), lambda i: (jnp.maximum(rev(i) * (tm // hb) - 1, 0), col))
    tok = lambda w: pl.BlockSpec((tm, w), lambda i: (rev(i), 0))
    return pl.pallas_call(
        body, grid=(nt,),
        in_specs=[tok(IN_DIM), halo(1), halo(2), tok(MIX), _full((8, D)), _full((1, D)), _full((1, D)),
                  _full((HEADS, CH, CH)), _full((HEADS, CH, CH)), _full((HEADS, CH, CH)),
                  _full((N_CHIP, D, IN_BLK), 1), tok(D), tok(D), _full((1, D)), ANY],
        out_specs=[tok(IN_DIM), _full((8, D)), _full((1, D)), _full((1, D)), _full((HEADS, CH, CH)), _full((8, D)),
                   tok(D), _full((8, D))],
        out_shape=[jax.ShapeDtypeStruct((t, IN_DIM), BF16), jax.ShapeDtypeStruct((8, D), F32),
                   jax.ShapeDtypeStruct((1, D), F32), jax.ShapeDtypeStruct((1, D), F32),
                   jax.ShapeDtypeStruct((HEADS, CH, CH), F32), jax.ShapeDtypeStruct((8, D), F32),
                   jax.ShapeDtypeStruct((t, D), F32), jax.ShapeDtypeStruct((8, D), F32)],
        scratch_shapes=[pltpu.VMEM((8, D), F32), pltpu.VMEM((tm, D), F32)],
        compiler_params=_cp(("arbitrary",), VMEM_LIMIT), name="mixer_bwd")(
            proj, proj, proj, dmix, cw8, lng, lnb, wc, wct, bsb, win_f, x, dx1, g1, after)


def _grad_matmul(at, b, after, *, by_cols, name, tk=1024):
    m, t = at.shape
    n = b.shape[1]
    nk = t // tk
    nj = N_CHIP if by_cols else 1
    bn = n // nj

    def body(a_ref, b_ref, after_ref, o_ref, ob_ref):
        kk = pl.program_id(1)
        part = jnp.dot(a_ref[...], b_ref[...], preferred_element_type=F32)

        @pl.when(kk == 0)
        def _():
            o_ref[...] = part

        @pl.when(kk > 0)
        def _():
            o_ref[...] += part

        @pl.when(kk == nk - 1)
        def _():
            ob_ref[...] = o_ref[...].astype(BF16)

    a_spec = pl.BlockSpec((m, tk), lambda j, k: (0, k))
    b_spec = pl.BlockSpec((tk, bn), lambda j, k: (k, j))
    o_spec = pl.BlockSpec((None, m, bn), lambda j, k: (j, 0, 0))
    o32, o16 = pl.pallas_call(
        body, grid=(nj, nk), in_specs=[a_spec, b_spec, ANY], out_specs=[o_spec, o_spec],
        out_shape=[jax.ShapeDtypeStruct((nj, m, bn), F32), jax.ShapeDtypeStruct((nj, m, bn), BF16)],
        compiler_params=_cp(("parallel", "arbitrary"), VMEM_LIMIT), name=name)(at, b, after)
    if by_cols:
        return o32, o16
    return o32.reshape(N_CHIP, m // N_CHIP, n), o16.reshape(N_CHIP, m // N_CHIP, n)


def _coords():
    x, y, c = lax.axis_index("x"), lax.axis_index("y"), lax.axis_index("c")
    chips = [(1 - x, y), (x, 1 - y), (1 - x, 1 - y)]
    return x, y, c, chips


def _pair_exchange(grads_b, smalls, name):
    ng, ns = len(grads_b), len(smalls)
    n = ng + ns

    def body(*refs):
        ins, outs, send, recv = refs[:n], refs[n:2 * n], refs[2 * n], refs[2 * n + 1]
        x, y, c, _ = _coords()
        cps = []
        for i in range(n):
            if i < ng:
                hr = ins[i].shape[1] // 2
                src = ins[i].at[pl.ds(0, N_CHIP), pl.ds((1 - c) * hr, hr)]
            else:
                hr = ins[i].shape[0] // 2
                src = ins[i].at[pl.ds((1 - c) * hr, hr)]
            cps.append(pltpu.make_async_remote_copy(
                src_ref=src, dst_ref=outs[i], send_sem=send.at[i], recv_sem=recv.at[i],
                device_id=(x, y, 1 - c), device_id_type=MESH))
        for cp in cps:
            cp.start()
        for cp in cps:
            cp.wait()

    out_shape = [jax.ShapeDtypeStruct((N_CHIP, g.shape[1] // 2, g.shape[2]), g.dtype) for g in grads_b]
    out_shape += [jax.ShapeDtypeStruct((s.shape[0] // 2, s.shape[1]), s.dtype) for s in smalls]
    return pl.pallas_call(
        body, out_shape=out_shape, in_specs=[ANY] * n, out_specs=[ANY] * n,
        scratch_shapes=[pltpu.SemaphoreType.DMA((n,)), pltpu.SemaphoreType.DMA((n,))],
        name=name)(*grads_b, *smalls)


def _pair_sum(c_idx, grads, recvd, smalls, smalls_recvd, name):
    ng, ns = len(grads), len(smalls)
    halves = [g.shape[1] // 2 for g in grads]

    def body(c_ref, *refs):
        g_in, r_in = refs[:ng], refs[ng:2 * ng]
        s_in, sr_in = refs[2 * ng:2 * ng + ns], refs[2 * ng + ns:2 * ng + 2 * ns]
        o = refs[2 * ng + 2 * ns:]
        for i in range(ng):
            tot = g_in[i][...] + r_in[i][...].astype(F32)
            o[i][...] = tot
            o[ng + i][...] = tot.astype(BF16)
        for i in range(ns):
            o[2 * ng + i][...] = s_in[i][...] + sr_in[i][...]

    in_specs = [pl.BlockSpec((None, None, halves[i], g.shape[2]), lambda b, c: (b, c[0], 0, 0)) for i, g in enumerate(grads)]
    in_specs += [pl.BlockSpec((None, halves[i], g.shape[2]), lambda b, c: (b, 0, 0)) for i, g in enumerate(grads)]
    in_specs += [pl.BlockSpec((None, s.shape[0] // 2, s.shape[1]), lambda b, c: (c[0], 0, 0)) for s in smalls]
    in_specs += [pl.BlockSpec((s.shape[0] // 2, s.shape[1]), lambda b, c: (0, 0)) for s in smalls]
    blk = [pl.BlockSpec((None, halves[i], g.shape[2]), lambda b, c: (b, 0, 0)) for i, g in enumerate(grads)]
    out_specs = blk + blk + [pl.BlockSpec((s.shape[0] // 2, s.shape[1]), lambda b, c: (0, 0)) for s in smalls]
    out_shape = [jax.ShapeDtypeStruct((N_CHIP, halves[i], g.shape[2]), F32) for i, g in enumerate(grads)]
    out_shape += [jax.ShapeDtypeStruct((N_CHIP, halves[i], g.shape[2]), BF16) for i, g in enumerate(grads)]
    out_shape += [jax.ShapeDtypeStruct((s.shape[0] // 2, s.shape[1]), F32) for s in smalls]
    grads4 = [g.reshape(N_CHIP, 2, halves[i], g.shape[2]) for i, g in enumerate(grads)]
    smalls3 = [s.reshape(2, s.shape[0] // 2, s.shape[1]) for s in smalls]
    return pl.pallas_call(
        body, out_shape=out_shape,
        grid_spec=pltpu.PrefetchScalarGridSpec(num_scalar_prefetch=1, grid=(N_CHIP,), in_specs=in_specs, out_specs=out_specs),
        compiler_params=_cp(("arbitrary",), VMEM_LIMIT), name=name)(c_idx, *grads4, *recvd, *smalls3, *smalls_recvd)


_HBM = pl.BlockSpec(memory_space=pltpu.HBM)
_SEM = pl.BlockSpec(memory_space=pltpu.SEMAPHORE)


def _split_copies(ins, lands, ng, send, recv, arriving):
    x, y, c, chips = _coords()
    b = 2 * x + y
    copies = []
    for i in range(len(ins)):
        for k in range(3):
            blk = 2 * chips[k][0] + chips[k][1]
            src, dst, got = (ins[i].at[blk], lands[i].at[k], lands[i].at[k]) if i < ng else (ins[i], lands[i].at[b], lands[i].at[blk])
            sems = dict(send_sem=send.at[3 * i + k], recv_sem=recv.at[3 * i + k], device_id=(*chips[k], c), device_id_type=MESH)
            if arriving:
                copies.append(pltpu.make_async_remote_copy(src_ref=got, dst_ref=got, **sems))
            else:
                copies.append(pltpu.make_async_remote_copy(src_ref=src, dst_ref=dst, **sems))
    return copies


def _exchange_begin(sums_b, smalls, name):
    ng, n = len(sums_b), len(sums_b) + len(smalls)
    srcs = list(sums_b) + list(smalls)
    lands = [lax.empty((3,) + g.shape[1:], g.dtype) for g in sums_b] + [lax.empty((N_CHIP,) + s.shape, s.dtype) for s in smalls]

    def body(*refs):
        ins, land_refs = refs[:n], refs[n:2 * n]
        send, recv = refs[2 * n], refs[2 * n + 1]
        token = refs[4 * n + 2]
        for cp in _split_copies(ins, land_refs, ng, send, recv, False):
            cp.start()
        token[...] = jnp.zeros_like(token)

    hbm = lambda a: pltpu.HBM(a.shape, a.dtype)
    outs = pl.pallas_call(
        body, name=name,
        out_shape=(pltpu.SemaphoreType.DMA((3 * n,)), pltpu.SemaphoreType.DMA((3 * n,)), *[hbm(a) for a in srcs + lands],
                   jax.ShapeDtypeStruct((8, 128), F32)),
        in_specs=[_HBM] * (2 * n), out_specs=(_SEM, _SEM, *[_HBM] * (2 * n), pl.BlockSpec(memory_space=pltpu.VMEM)),
        input_output_aliases={i: 2 + i for i in range(2 * n)},
        compiler_params=pltpu.CompilerParams(has_side_effects=pltpu.SideEffectType.DATAFLOW_SIDE_EFFECTING),
    )(*[pltpu.with_memory_space_constraint(a, pltpu.HBM) for a in srcs + lands])
    return outs[0], outs[1], list(outs[2:2 + n]), list(outs[2 + n:2 + 2 * n]), outs[2 + 2 * n]


def _exchange_end(send, recv, srcs, lands, ng, after, name):
    n = len(srcs)
    after = list(after)

    def body(*refs):
        ins, land_refs = refs[:n], refs[n:2 * n]
        send_ref, recv_ref = refs[2 * n], refs[2 * n + 1]
        for cp in _split_copies(ins, land_refs, ng, send_ref, recv_ref, False):
            cp.wait_send()
        for cp in _split_copies(ins, land_refs, ng, send_ref, recv_ref, True):
            cp.wait_recv()

    hbm = lambda a: pltpu.HBM(a.shape, a.dtype)
    outs = pl.pallas_call(
        body, name=name, out_shape=tuple(hbm(a) for a in list(srcs) + list(lands)),
        in_specs=[_HBM] * (2 * n) + [_SEM, _SEM] + [ANY] * len(after), out_specs=tuple([_HBM] * (2 * n)),
        input_output_aliases={i: i for i in range(2 * n)},
        compiler_params=pltpu.CompilerParams(has_side_effects=pltpu.SideEffectType.DATAFLOW_SIDE_EFFECTING),
    )(*srcs, *lands, send, recv, *after)
    return list(outs[n:])


def _chip_sum(bc_idx, sums, recvd, smalls_slots, smalls_own, name, steps=4):
    ng, ns = len(sums), len(smalls_slots)

    def body(bc_ref, *refs):
        own, rx = refs[:ng], refs[ng:2 * ng]
        sl = refs[2 * ng:2 * ng + ns]
        sl_own = refs[2 * ng + ns:2 * ng + 2 * ns]
        o = refs[2 * ng + 2 * ns:]
        for i in range(ng):
            tot = own[i][...]
            for j in range(3):
                tot = tot + rx[i][j].astype(F32)
            o[i][...] = tot
        for i in range(ns):
            term = [jnp.where(bc_ref[0] == kk, sl_own[i][...], sl[i][kk]) for kk in range(N_CHIP)]
            o[ng + i][...] = ((term[0] + term[1]) + term[2]) + term[3]

    def rows(g):
        return g.shape[1] // steps

    in_specs = [pl.BlockSpec((None, rows(g), g.shape[2]), lambda r, bc: (bc[0], r, 0)) for g in sums]
    in_specs += [pl.BlockSpec((3, rows(g), g.shape[2]), lambda r, bc: (0, r, 0)) for g in sums]
    in_specs += [pl.BlockSpec(s.shape, lambda r, bc: (0, 0, 0)) for s in smalls_slots]
    in_specs += [pl.BlockSpec(s.shape[1:], lambda r, bc: (0, 0)) for s in smalls_slots]
    out_specs = [pl.BlockSpec((rows(g), g.shape[2]), lambda r, bc: (bc[1] * steps + r, 0)) for g in sums]
    out_specs += [pl.BlockSpec(s.shape[1:], lambda r, bc: (bc[1], 0)) for s in smalls_slots]
    out_shape = [jax.ShapeDtypeStruct((2 * g.shape[1], g.shape[2]), F32) for g in sums]
    out_shape += [jax.ShapeDtypeStruct((2 * s.shape[1], s.shape[2]), F32) for s in smalls_slots]
    return pl.pallas_call(
        body, out_shape=out_shape,
        grid_spec=pltpu.PrefetchScalarGridSpec(num_scalar_prefetch=1, grid=(steps,), in_specs=in_specs, out_specs=out_specs),
        compiler_params=_cp(("arbitrary",), VMEM_LIMIT), name=name)(bc_idx, *sums, *recvd, *smalls_slots, *smalls_own)


def _pair_gather(arrs, name):
    n = len(arrs)

    def body(*refs):
        outs = refs[n:2 * n]
        send, recv = refs[2 * n:]
        x, y, c, _ = _coords()
        cps = []
        for i in range(n):
            hr = outs[i].shape[0] // 2
            mine = outs[i].at[pl.ds(c * hr, hr)]
            cps.append(pltpu.make_async_remote_copy(
                src_ref=mine, dst_ref=mine, send_sem=send.at[i], recv_sem=recv.at[i],
                device_id=(x, y, 1 - c), device_id_type=MESH))
        for cp in cps:
            cp.start()
        for i in range(n):
            hr = outs[i].shape[0] // 2
            theirs = outs[i].at[pl.ds((1 - c) * hr, hr)]
            pltpu.make_async_remote_copy(
                src_ref=theirs, dst_ref=theirs, send_sem=send.at[i], recv_sem=recv.at[i],
                device_id=(x, y, 1 - c), device_id_type=MESH).wait_recv()
        for cp in cps:
            cp.wait_send()

    return list(pl.pallas_call(
        body, out_shape=[jax.ShapeDtypeStruct(a.shape, a.dtype) for a in arrs], in_specs=[ANY] * n, out_specs=[ANY] * n,
        input_output_aliases={i: i for i in range(n)},
        scratch_shapes=[pltpu.SemaphoreType.DMA((n,)), pltpu.SemaphoreType.DMA((n,))],
        name=name)(*arrs))


def _adamw_math(w, g, m, v):
    m2 = ADAM_B1 * m + (1.0 - ADAM_B1) * g
    v2 = ADAM_B2 * v + (1.0 - ADAM_B2) * (g * g)
    m_hat = m2 / (1.0 - ADAM_B1 ** ADAM_STEP)
    v_hat = v2 / (1.0 - ADAM_B2 ** ADAM_STEP)
    delta = -ADAM_LR * (m_hat / (jnp.sqrt(v_hat) + ADAM_EPS) + ADAM_WD * w)
    return delta, m2, v2


def _adamw_big(w, g, m, v, name, steps=8):
    r, c = w.shape

    def body(w_ref, g_ref, m_ref, v_ref, d_ref, m2_ref, v2_ref, g2_ref):
        gv = g_ref[...]
        d_ref[...], m2_ref[...], v2_ref[...] = _adamw_math(w_ref[...], gv, m_ref[...], v_ref[...])
        g2_ref[...] = gv

    spec = pl.BlockSpec((r // steps, c), lambda i: (i, 0))
    return pl.pallas_call(
        body, grid=(steps,), in_specs=[spec] * 4, out_specs=[spec] * 4,
        out_shape=[jax.ShapeDtypeStruct((r, c), F32)] * 4,
        compiler_params=_cp(("parallel",), VMEM_LIMIT), name=name)(w, g, m, v)


def _adamw_small(groups):
    n = len(groups)

    def body(*refs):
        for i in range(n):
            w_ref, g_ref, m_ref, v_ref = refs[4 * i:4 * i + 4]
            d_ref, m2_ref, v2_ref = refs[4 * n + 3 * i:4 * n + 3 * i + 3]
            d_ref[...], m2_ref[...], v2_ref[...] = _adamw_math(w_ref[...], g_ref[...], m_ref[...], v_ref[...])

    flat = [a for grp in groups for a in grp]
    out_shape = [jax.ShapeDtypeStruct(grp[0].shape, F32) for grp in groups for _ in range(3)]
    outs = pl.pallas_call(body, out_shape=out_shape, name="adamw_small")(*flat)
    return [tuple(outs[3 * i:3 * i + 3]) for i in range(n)]


def kernel(x, mem, norm_mix_g, w_in, conv_w, gm_ln_g, gm_ln_b, gm_ws, gm_bs, w_out, norm_x_g, norm_mem_g, w_q, w_kv, w_xo, norm_final_g, loss_target, m_norm_mix_g, m_w_in, m_conv_w, m_gm_ln_g, m_gm_ln_b, m_gm_ws, m_gm_bs, m_w_out, m_norm_x_g, m_norm_mem_g, m_w_q, m_w_kv, m_w_xo, m_norm_final_g, v_norm_mix_g, v_w_in, v_conv_w, v_gm_ln_g, v_gm_ln_b, v_gm_ws, v_gm_bs, v_w_out, v_norm_x_g, v_norm_mem_g, v_w_q, v_w_kv, v_w_xo, v_norm_final_g):
    t = x.shape[1]
    xi = lax.axis_index("x")
    yi = lax.axis_index("y")
    ci = lax.axis_index("c")
    b_idx = jnp.reshape(2 * xi + yi, (1,)).astype(jnp.int32)
    c_idx = jnp.reshape(ci, (1,)).astype(jnp.int32)

    x2d, mem2d, tgt = x[0], mem[0], loss_target[0]
    big = [w_in[0], w_out[0], w_q[0], w_kv[0], w_xo[0]]
    big_m = [m_w_in[0], m_w_out[0], m_w_q[0], m_w_kv[0], m_w_xo[0]]
    big_v = [v_w_in[0], v_w_out[0], v_w_q[0], v_w_kv[0], v_w_xo[0]]
    g3 = norm_final_g.reshape(1, D)

    def pad8(a):
        return jnp.pad(a, ((0, 8 - a.shape[0]), (0, 0)))

    own_blocks = _cast_shards(b_idx, big)

    tril = jnp.tril(jnp.ones((CH, CH), bool))
    wc32 = jnp.where(tril[None], gm_ws[0], 0.0)
    wc = wc32.astype(BF16)
    wct = jnp.swapaxes(wc32, 1, 2).astype(BF16)
    bsb = jnp.broadcast_to(gm_bs[0][:, :, None], (HEADS, CH, CH))

    blk = 2 * xi + yi
    order = jnp.stack([blk, blk ^ 2, blk ^ 1, blk ^ 3]).astype(jnp.int32)
    proj, ht, win_f, cw8, (wout_f, wkv_f) = _proj_gather(
        order, x2d, norm_mix_g, own_blocks[0], pad8(conv_w[0]), [own_blocks[1], own_blocks[3]])
    mixin, mixt, (wq_f, wxo_f) = _mixer_fwd(proj, cw8, gm_ln_g, gm_ln_b, wc, bsb, [own_blocks[2], own_blocks[4]])
    wout2, wq2, wxo2 = wout_f.reshape(MIX, D), wq_f.reshape(D, D), wxo_f.reshape(D, D)
    k, v, mt = _mem_fwd(mem2d, norm_mem_g, wkv_f)
    del mt

    (loss_tile, dmix, dx1b, h2t, dq, ot, dx2b, dk, dv, dg2, dg3) = _tail(
        x2d, tgt, mixin, wout2, wq2, wxo2, k, v, norm_x_g, g3)
    dwkv, dwkv_b, dgm = _mem_bwd(mem2d, norm_mem_g, dk, dv, wkv_f)
    dproj, dcw, dlng, dlnb, dwc, dbs8, grad_x, dg1 = _mixer_bwd(
        proj, dmix, cw8, gm_ln_g, gm_ln_b, wc, wct, bsb, win_f, x2d, dx1b, norm_mix_g, dgm)

    bc_idx = jnp.concatenate([b_idx, c_idx])
    dwin, dwin_b = _grad_matmul(ht, dproj, dgm, by_cols=True, name="grad_w_in", tk=2048)
    zero = jnp.zeros((1, D), F32)
    loss_row = jnp.broadcast_to(loss_tile[0:1, 0:1], (1, D))
    sv = jnp.concatenate([dg1[0:1], dg2, dgm, dg3, dlng, dlnb, dbs8[0:1], loss_row, dcw], axis=0)
    sw = dwc.reshape(HEADS * CH, CH)
    rx1b = _pair_exchange([dwin_b], [sv, sw], "pair_exchange_b")
    ps_b = _pair_sum(c_idx, [dwin], rx1b[:1], [sv, sw], rx1b[1:], "pair_sum_b")
    sums_b, sums_b_b, psmall = list(ps_b[:1]), list(ps_b[1:2]), list(ps_b[2:])
    send_b, recv_b, src_b, land_b, token_b = _exchange_begin(sums_b_b, psmall, "exchange_b_begin")

    dwxo, dwxo_b = _grad_matmul(ot, dx2b, token_b, by_cols=False, name="grad_w_xo", tk=2048)
    dwq, dwq_b = _grad_matmul(h2t, dq, token_b, by_cols=False, name="grad_w_q", tk=2048)
    dwout, dwout_b = _grad_matmul(mixt, dx1b, token_b, by_cols=False, name="grad_w_out")
    rx1a = _pair_exchange([dwout_b, dwq_b, dwkv_b, dwxo_b], [], "pair_exchange_a")
    ps_a = _pair_sum(c_idx, [dwout, dwq, dwkv, dwxo], rx1a, [], [], "pair_sum_a")
    sums_a, sums_a_b = list(ps_a[:4]), list(ps_a[4:8])
    send_a, recv_a, src_a, land_a, token_a = _exchange_begin(sums_a_b, [], "exchange_a_begin")

    rx2b = _exchange_end(send_b, recv_b, src_b, land_b, 1, [token_a], "exchange_b_end")
    red_b = _chip_sum(bc_idx, sums_b, rx2b[:1], rx2b[1:], psmall, "chip_sum_b")
    gwin, svf, swf = _pair_gather(red_b, "pair_gather_b")
    out_b = _adamw_big(big[0], gwin, big_m[0], big_v[0], "adamw_w_in")

    def vec_pack(a1, a2, am, a3, lg, lb, bs):
        return jnp.concatenate([a1, a2, am, a3.reshape(1, D), lg, lb, bs.reshape(1, D), zero], axis=0)

    wv = vec_pack(norm_mix_g, norm_x_g, norm_mem_g, norm_final_g, gm_ln_g, gm_ln_b, gm_bs)
    mv = vec_pack(m_norm_mix_g, m_norm_x_g, m_norm_mem_g, m_norm_final_g, m_gm_ln_g, m_gm_ln_b, m_gm_bs)
    vv = vec_pack(v_norm_mix_g, v_norm_x_g, v_norm_mem_g, v_norm_final_g, v_gm_ln_g, v_gm_ln_b, v_gm_bs)
    loss = svf[7, 0]
    gcw = lax.dynamic_slice_in_dim(svf[8:16], blk * (D // N_CHIP), D // N_CHIP, axis=1)
    gws = swf
    gv = svf[0:8]
    (dv_, mv_, vv_), (dc_, mc_, vc_), (dws_, mws_, vws_) = _adamw_small([
        (wv, gv, mv, vv),
        (pad8(conv_w[0]), gcw, pad8(m_conv_w[0]), pad8(v_conv_w[0])),
        (gm_ws.reshape(HEADS * CH, CH), gws, m_gm_ws.reshape(HEADS * CH, CH), v_gm_ws.reshape(HEADS * CH, CH))])

    rx2a = _exchange_end(send_a, recv_a, src_a, land_a, 4, [out_b[0], dv_], "exchange_a_end")
    red_a = _chip_sum(bc_idx, sums_a, rx2a, [], [], "chip_sum_a")
    g_a = _pair_gather(red_a, "pair_gather_a")
    names = ["w_out", "w_q", "w_kv", "w_xo"]
    out_a = [_adamw_big(w, g, m, v, "adamw_" + nm) for w, g, m, v, nm in zip(big[1:], g_a, big_m[1:], big_v[1:], names)]
    big_out = [out_b] + out_a

    def unpack(vecs, cw, ws, bigs):
        r = lambda i: vecs[i:i + 1]
        return [r(0), bigs[0][None], cw[0:3][None], r(4), r(5), ws.reshape(1, HEADS, CH, CH), vecs[6].reshape(1, HEADS, CH),
                bigs[1][None], r(1), r(2), bigs[2][None], bigs[3][None], bigs[4][None], vecs[3]]

    grads_out = unpack(gv, gcw, gws, [o[3] for o in big_out])
    delta_out = unpack(dv_, dc_, dws_, [o[0] for o in big_out])
    m_out = unpack(mv_, mc_, mws_, [o[1] for o in big_out])
    v_out = unpack(vv_, vc_, vws_, [o[2] for o in big_out])
    return (loss, grad_x[None], *grads_out, *delta_out, *m_out, *v_out)
```

```python
import functools
import math

import jax
import jax.numpy as jnp
from jax import lax
from jax.experimental import pallas as pl
from jax.experimental.pallas import tpu as pltpu

F32 = jnp.float32
BF16 = jnp.bfloat16
MESH = pl.DeviceIdType.MESH

D = 1024
SLAB = 1024
N_SLAB = 7
IN_DIM = N_SLAB * SLAB
MIX = 2 * SLAB
HEADS = 8
CH = 128
XH = 4
XD = D // XH
EPS = 1e-6
GELU_C = math.sqrt(2.0 / math.pi)
GELU_A = 0.044715
N_CHIP = 4
IN_BLK = IN_DIM // N_CHIP
KV_BLK = 2 * D // N_CHIP

ADAM_LR, ADAM_B1, ADAM_B2, ADAM_EPS, ADAM_WD, ADAM_STEP = 0.001, 0.9, 0.999, 1e-08, 0.01, 10

VMEM_LIMIT = 60 * 1024 * 1024


def _cp(sem=None, vmem=None):
    return pltpu.CompilerParams(dimension_semantics=sem, vmem_limit_bytes=vmem)


def _full(shape, buffers=None):
    n = len(shape)
    if buffers is None:
        return pl.BlockSpec(shape, lambda *_: (0,) * n)
    return pl.BlockSpec(shape, lambda *_: (0,) * n, pipeline_mode=pl.Buffered(buffers))


ANY = pl.BlockSpec(memory_space=pl.ANY)


def _bdot(a, b):
    return jnp.dot(a.astype(BF16), b.astype(BF16), preferred_element_type=F32)


def _bdot_nt(a, b):
    return lax.dot_general(a.astype(BF16), b.astype(BF16), (((1,), (1,)), ((), ())), preferred_element_type=F32)


def _rms(x, g):
    r = lax.rsqrt(jnp.mean(x * x, axis=-1, keepdims=True) + EPS)
    return x * r * g, r


def _rms_bwd(dy, x, r, g):
    gdy = dy * g
    dx = r * gdy - x * (r * r * r) * jnp.mean(x * gdy, axis=-1, keepdims=True)
    dg = jnp.sum(dy * x * r, axis=0, keepdims=True)
    return dx, dg


def _gelu_parts(x):
    x2 = x * x
    t = jnp.tanh(GELU_C * (x + GELU_A * x * x2))
    val = 0.5 * x * (1.0 + t)
    grad = 0.5 * (1.0 + t) + 0.5 * x * (1.0 - t * t) * (GELU_C * (1.0 + 3.0 * GELU_A * x2))
    return val, grad


def _gelu(x):
    return 0.5 * x * (1.0 + jnp.tanh(GELU_C * (x + GELU_A * x * x * x)))


def _sigmoid(z):
    return 1.0 / (1.0 + jnp.exp(-z))


def _cast_shards(b_idx, arrs):
    n = len(arrs)
    steps = 8

    def body(b_ref, *refs):
        for i in range(n):
            refs[n + i][...] = refs[i][...].astype(BF16)

    in_specs = [pl.BlockSpec((a.shape[0] // steps, a.shape[1]), lambda i, b: (i, 0)) for a in arrs]
    out_specs = [pl.BlockSpec((None, a.shape[0] // steps, a.shape[1]), lambda i, b: (b[0], i, 0)) for a in arrs]
    return pl.pallas_call(
        body, out_shape=[jax.ShapeDtypeStruct((N_CHIP,) + a.shape, BF16) for a in arrs],
        grid_spec=pltpu.PrefetchScalarGridSpec(num_scalar_prefetch=1, grid=(steps,), in_specs=in_specs, out_specs=out_specs),
        compiler_params=_cp(("arbitrary",)), name="cast_shards")(b_idx, *arrs)


def _proj_gather(order, x, g, win_own, cw8s, more, tm=1024):
    t = x.shape[0]
    ni = t // tm
    hr = D // 2
    nm = len(more)

    def body(*refs):
        order_ref, x_ref, g_ref, win_in, cw_in = refs[:5]
        o_ref, ht_ref, win_f, cw_out = refs[5 + nm:9 + nm]
        more_out = refs[9 + nm:9 + 2 * nm]
        hbuf, wv, ici_s, ici_r, d2d_s, d2d_r, cw_s, cw_r, loc, m_is, m_ir, m_ds, m_dr = refs[9 + 2 * nm:]
        more_start, more_pass_on, more_finish = _gather_steps(more_out, m_is, m_ir, m_ds, m_dr)
        j, i = pl.program_id(0), pl.program_id(1)
        x, y, c, chips = _coords()
        b = 2 * x + y
        sib = (x, y, 1 - c)
        blks = [2 * chip[0] + chip[1] for chip in chips]

        def half(blk, hc):
            return win_f.at[blk, pl.ds(hc * hr, hr)]

        def cw_cols(blk):
            return cw_out.at[:, pl.ds(blk * (D // N_CHIP), D // N_CHIP)]

        def ici(k, blk):
            return pltpu.make_async_remote_copy(src_ref=half(blk, c), dst_ref=half(blk, c), send_sem=ici_s.at[k],
                                                recv_sem=ici_r.at[k], device_id=(*chips[k], c), device_id_type=MESH)

        def relay(k, blk, hc):
            return pltpu.make_async_remote_copy(src_ref=half(blk, hc), dst_ref=half(blk, hc), send_sem=d2d_s.at[k],
                                                recv_sem=d2d_r.at[k], device_id=sib, device_id_type=MESH)

        def cw_copy(k, blk):
            src = cw_in if blk is None else cw_cols(blk)
            return pltpu.make_async_remote_copy(src_ref=src, dst_ref=cw_cols(b if blk is None else blk), send_sem=cw_s.at[k],
                                                recv_sem=cw_r.at[k], device_id=(*chips[k], c), device_id_type=MESH)

        cw_local = pltpu.make_async_copy(cw_in, cw_cols(b), loc.at[1])

        @pl.when((j == 0) & (i == 0))
        def _():
            cw_local.start()
            for k in range(3):
                ici(k, b).start()
            for k in range(3):
                cw_copy(k, None).start()
            more_start()

        for jj in range(N_CHIP):
            @pl.when((j == jj) & (i == 0))
            def _(jj=jj):
                if jj == 0:
                    blk = b
                else:
                    blk = blks[jj - 1]
                    ici(jj - 1, blk).wait_recv()
                    relay(jj - 1, blk, c).start()
                    relay(jj - 1, blk, 1 - c).wait_recv()
                load = pltpu.make_async_copy(win_f.at[blk], wv, loc.at[0])
                load.start()
                load.wait()

        rows = pl.ds(pl.multiple_of(i * tm, tm), tm)

        @pl.when(j == 0)
        def _():
            h, _ = _rms(x_ref[...], g_ref[...])
            hbuf[rows, :] = h.astype(BF16)
            ht_ref[...] = h.T.astype(BF16)

        @pl.when((j == N_CHIP - 1) & (i == ni - 1))
        def _():
            more_pass_on()

        o_ref[...] = jnp.dot(hbuf[rows, :], wv[...], preferred_element_type=F32).astype(BF16)

        @pl.when((j == N_CHIP - 1) & (i == ni - 1))
        def _():
            for k in range(3):
                cw_copy(k, blks[k]).wait_recv()
            for k in range(3):
                ici(k, b).wait_send()
                relay(k, blks[k], c).wait_send()
                cw_copy(k, None).wait_send()
            cw_local.wait()
            more_finish()

    first = lambda j, i: jnp.where(j == 0, i, ni - 1)
    in_specs = [pl.BlockSpec((tm, D), lambda j, i, o: (first(j, i), 0)), pl.BlockSpec((1, D), lambda j, i, o: (0, 0)),
                ANY, ANY] + [ANY] * nm
    out_specs = [pl.BlockSpec((tm, IN_BLK), lambda j, i, o: (i, o[j])),
                 pl.BlockSpec((D, tm), lambda j, i, o: (0, first(j, i))), ANY, ANY] + [ANY] * nm
    outs = pl.pallas_call(
        body, out_shape=[jax.ShapeDtypeStruct((t, IN_DIM), BF16), jax.ShapeDtypeStruct((D, t), BF16),
                         jax.ShapeDtypeStruct(win_own.shape, BF16), jax.ShapeDtypeStruct((8, D), F32)]
        + [jax.ShapeDtypeStruct(f.shape, f.dtype) for f in more],
        grid_spec=pltpu.PrefetchScalarGridSpec(
            num_scalar_prefetch=1, grid=(N_CHIP, ni), in_specs=in_specs, out_specs=out_specs,
            scratch_shapes=[pltpu.VMEM((t, D), BF16), pltpu.VMEM((D, IN_BLK), BF16)]
            + [pltpu.SemaphoreType.DMA((3,))] * 6 + [pltpu.SemaphoreType.DMA((2,))]
            + [pltpu.SemaphoreType.DMA((max(nm, 1), 3))] * 4),
        input_output_aliases={3: 2, **{5 + w: 4 + w for w in range(nm)}},
        compiler_params=_cp(("arbitrary", "arbitrary"), VMEM_LIMIT), name="proj_gather")(order, x, g, win_own, cw8s, *more)
    return outs[0], outs[1], outs[2], outs[3], outs[4:]


def _gather_steps(outs, ici_s, ici_r, d2d_s, d2d_r):
    x, y, c, chips = _coords()
    b = 2 * x + y
    sib = (x, y, 1 - c)
    nw = len(outs)

    def half(w, blk, hc):
        hr = outs[w].shape[1] // 2
        return outs[w].at[blk, pl.ds(hc * hr, hr)]

    def ici(w, k, blk):
        return pltpu.make_async_remote_copy(src_ref=half(w, blk, c), dst_ref=half(w, blk, c), send_sem=ici_s.at[w, k],
                                            recv_sem=ici_r.at[w, k], device_id=(*chips[k], c), device_id_type=MESH)

    def relay(w, k, blk, hc):
        return pltpu.make_async_remote_copy(src_ref=half(w, blk, hc), dst_ref=half(w, blk, hc), send_sem=d2d_s.at[w, k],
                                            recv_sem=d2d_r.at[w, k], device_id=sib, device_id_type=MESH)

    def start():
        for w in range(nw):
            for k in range(3):
                ici(w, k, b).start()

    def pass_on():
        for w in range(nw):
            for k in range(3):
                blk = 2 * chips[k][0] + chips[k][1]
                ici(w, k, blk).wait_recv()
                relay(w, k, blk, c).start()

    def finish():
        for w in range(nw):
            for k in range(3):
                blk = 2 * chips[k][0] + chips[k][1]
                relay(w, k, blk, 1 - c).wait_recv()
        for w in range(nw):
            for k in range(3):
                blk = 2 * chips[k][0] + chips[k][1]
                ici(w, k, b).wait_send()
                relay(w, k, blk, c).wait_send()

    return start, pass_on, finish


def _mixer_fwd(proj, cw8, lng, lnb, wc, bsb, fulls, tm=256):
    t = proj.shape[0]
    nt = t // tm
    nch = tm // CH
    nw = len(fulls)

    def body(*refs):
        p_ref, cw_ref, lng_ref, lnb_ref, wc_ref, bsb_ref = refs[:6]
        mix_ref, mixt_ref = refs[6 + nw:8 + nw]
        w_outs = refs[8 + nw:8 + 2 * nw]
        prev_ref, stage_ref, ici_s, ici_r, d2d_s, d2d_r = refs[8 + 2 * nw:]
        gather_start, gather_pass_on, gather_finish = _gather_steps(w_outs, ici_s, ici_r, d2d_s, d2d_r)

        @pl.when(pl.program_id(0) == 0)
        def _():
            gather_start()
            prev_ref[...] = jnp.zeros_like(prev_ref)

        @pl.when(pl.program_id(0) == nt - 1)
        def _():
            gather_pass_on()

        rows = lax.broadcasted_iota(jnp.int32, (tm, CH), 0)
        for s in range(HEADS):
            cs = pl.ds(CH * s, CH)

            def slab(k):
                return p_ref[:, pl.ds(k * SLAB + CH * s, CH)].astype(F32)

            gb, gc, xa, za = slab(0), slab(1), slab(2), slab(3)
            cx = gc * xa
            p6 = jnp.broadcast_to(prev_ref[6:7, cs], (tm, CH))
            p7 = jnp.broadcast_to(prev_ref[7:8, cs], (tm, CH))
            c1 = jnp.where(rows == 0, p7, pltpu.roll(cx, 1, 0))
            c2 = jnp.where(rows == 0, p6, jnp.where(rows == 1, p7, pltpu.roll(cx, 2, 0)))
            prev_ref[:, cs] = cx[tm - 8:, :]
            cv = cw_ref[0:1, cs] * c2 + cw_ref[1:2, cs] * c1 + cw_ref[2:3, cs] * cx
            stage_ref[:, cs] = gb * cv * (za * _sigmoid(za))

            u, v, zb = slab(4), slab(5), slab(6)
            ug, vg = _gelu(u), _gelu(v)
            dlt = vg - jnp.mean(vg, axis=-1, keepdims=True)
            vhat = dlt * lax.rsqrt(jnp.mean(dlt * dlt, axis=-1, keepdims=True) + EPS)
            vn = (vhat * lng_ref[:, cs] + lnb_ref[:, cs]).astype(BF16)
            gate = ug * (zb * _sigmoid(zb))
            for c in range(nch):
                rs = slice(CH * c, CH * (c + 1))
                sp = jnp.dot(wc_ref[s], vn[rs], preferred_element_type=F32) + bsb_ref[s]
                stage_ref[rs, pl.ds(SLAB + CH * s, CH)] = gate[rs] * sp

        full = stage_ref[...]
        mix_ref[...] = full.astype(BF16)
        mixt_ref[...] = full.T.astype(BF16)

        @pl.when(pl.program_id(0) == nt - 1)
        def _():
            gather_finish()

    sems = [pltpu.SemaphoreType.DMA((nw, 3))] * 4
    outs = pl.pallas_call(
        body, grid=(nt,),
        in_specs=[pl.BlockSpec((tm, IN_DIM), lambda i: (i, 0)), _full((8, D)), _full((1, D)), _full((1, D)),
                  _full((HEADS, CH, CH)), _full((HEADS, CH, CH))] + [ANY] * nw,
        out_specs=[pl.BlockSpec((tm, MIX), lambda i: (i, 0)), pl.BlockSpec((MIX, tm), lambda i: (0, i))] + [ANY] * nw,
        out_shape=[jax.ShapeDtypeStruct((t, MIX), BF16), jax.ShapeDtypeStruct((MIX, t), BF16)]
        + [jax.ShapeDtypeStruct(f.shape, f.dtype) for f in fulls],
        input_output_aliases={6 + w: 2 + w for w in range(nw)},
        scratch_shapes=[pltpu.VMEM((8, D), F32), pltpu.VMEM((tm, MIX), F32)] + sems,
        compiler_params=_cp(("arbitrary",), VMEM_LIMIT), name="mixer_fwd")(proj, cw8, lng, lnb, wc, bsb, *fulls)
    return outs[0], outs[1], outs[2:]


def _mem_fwd(mem, gm, wkv_f):
    n_mem = mem.shape[0]

    def body(mem_ref, gm_ref, w_ref, k_ref, v_ref, mt_ref):
        m, _ = _rms(mem_ref[...], gm_ref[...])
        mb = m.astype(BF16)
        mt_ref[...] = m.T.astype(BF16)
        for j in range(N_CHIP):
            dst = k_ref if j < 2 else v_ref
            dst[:, pl.ds(KV_BLK * (j % 2), KV_BLK)] = jnp.dot(mb, w_ref[j], preferred_element_type=F32).astype(BF16)

    return pl.pallas_call(
        body, out_shape=[jax.ShapeDtypeStruct((n_mem, D), BF16), jax.ShapeDtypeStruct((n_mem, D), BF16),
                         jax.ShapeDtypeStruct((D, n_mem), BF16)],
        compiler_params=_cp(None, VMEM_LIMIT), name="mem_fwd")(mem, gm, wkv_f)


def _tail(x, tgt, mixin, wout, wq, wxo, k, v, g2, g3, tm=512, sub=512):
    t = x.shape[0]
    n_mem = k.shape[0]
    scale = 1.0 / math.sqrt(XD)

    def body(x_ref, tgt_ref, mix_ref, wout_ref, wq_ref, wxo_ref, k_ref, v_ref, g2_ref, g3_ref,
             loss_ref, dmix_ref, dx1b_ref, h2t_ref, dq_ref, ot_ref, dx2b_ref, dk_ref, dv_ref, dg2_ref, dg3_ref):
        @pl.when(pl.program_id(0) == 0)
        def _():
            loss_ref[...] = jnp.zeros_like(loss_ref)
            dk_ref[...] = jnp.zeros_like(dk_ref)
            dv_ref[...] = jnp.zeros_like(dv_ref)
            dg2_ref[...] = jnp.zeros_like(dg2_ref)
            dg3_ref[...] = jnp.zeros_like(dg3_ref)

        g2, g3 = g2_ref[...], g3_ref[...]
        for sb in range(tm // sub):
            rs = pl.ds(sub * sb, sub)
            x1 = x_ref[rs, :] + jnp.dot(mix_ref[rs, :], wout_ref[...], preferred_element_type=F32)
            h2, r2 = _rms(x1, g2)
            h2t_ref[:, rs] = h2.T.astype(BF16)
            q = _bdot(h2, wq_ref[...]).astype(BF16)
            probs, outs = [], []
            for hd in range(XH):
                hs = pl.ds(XD * hd, XD)
                s = _bdot_nt(q[:, XD * hd:XD * (hd + 1)], k_ref[:, hs]) * scale
                e = jnp.exp(s - jnp.max(s, axis=-1, keepdims=True))
                p = e / jnp.sum(e, axis=-1, keepdims=True)
                probs.append(p)
                outs.append(_bdot(p, v_ref[:, hs]))
            o = jnp.concatenate(outs, axis=-1)
            ot_ref[:, rs] = o.T.astype(BF16)
            x2 = x1 + _bdot(o, wxo_ref[...])
            y, r3 = _rms(x2, g3)
            diff = y - tgt_ref[rs, :]
            row_loss = jnp.sum(diff * diff, axis=-1, keepdims=True)
            loss_ref[...] += jnp.broadcast_to(jnp.sum(row_loss, axis=0, keepdims=True) * (0.5 / D), loss_ref.shape)

            dx2, dg3 = _rms_bwd(diff * (1.0 / D), x2, r3, g3)
            dg3_ref[...] += dg3
            dx2b = dx2.astype(BF16)
            dx2b_ref[rs, :] = dx2b
            do = _bdot_nt(dx2b, wxo_ref[...])
            dqs = []
            for hd in range(XH):
                hs = pl.ds(XD * hd, XD)
                p = probs[hd]
                do_h = do[:, XD * hd:XD * (hd + 1)]
                dv_ref[:, hs] += _bdot(p.T, do_h)
                dp = _bdot_nt(do_h, v_ref[:, hs])
                ds = p * (dp - jnp.sum(dp * p, axis=-1, keepdims=True))
                dqs.append(_bdot(ds, k_ref[:, hs]) * scale)
                dk_ref[:, hs] += _bdot(ds.T, q[:, XD * hd:XD * (hd + 1)]) * scale
            dq = jnp.concatenate(dqs, axis=-1).astype(BF16)
            dq_ref[rs, :] = dq
            dx1n, dg2 = _rms_bwd(_bdot_nt(dq, wq_ref[...]), x1, r2, g2)
            dg2_ref[...] += dg2
            dx1b = (dx2 + dx1n).astype(BF16)
            dx1b_ref[rs, :] = dx1b
            dmix_ref[rs, :] = _bdot_nt(dx1b, wout_ref[...]).astype(BF16)

    tok = lambda w: pl.BlockSpec((tm, w), lambda i: (i, 0))
    tok_t = lambda w: pl.BlockSpec((w, tm), lambda i: (0, i))
    return pl.pallas_call(
        body, grid=(t // tm,),
        in_specs=[tok(D), tok(D), tok(MIX), _full((MIX, D), 1), _full((D, D), 1), _full((D, D), 1),
                  _full((n_mem, D), 1), _full((n_mem, D), 1), _full((1, D)), _full((1, D))],
        out_specs=[_full((8, 128)), tok(MIX), tok(D), tok_t(D), tok(D), tok_t(D), tok(D),
                   _full((n_mem, D)), _full((n_mem, D)), _full((1, D)), _full((1, D))],
        out_shape=[jax.ShapeDtypeStruct((8, 128), F32), jax.ShapeDtypeStruct((t, MIX), BF16),
                   jax.ShapeDtypeStruct((t, D), BF16),
                   jax.ShapeDtypeStruct((D, t), BF16), jax.ShapeDtypeStruct((t, D), BF16),
                   jax.ShapeDtypeStruct((D, t), BF16), jax.ShapeDtypeStruct((t, D), BF16),
                   jax.ShapeDtypeStruct((n_mem, D), F32), jax.ShapeDtypeStruct((n_mem, D), F32),
                   jax.ShapeDtypeStruct((1, D), F32), jax.ShapeDtypeStruct((1, D), F32)],
        compiler_params=_cp(("arbitrary",), VMEM_LIMIT), name="tail")(x, tgt, mixin, wout, wq, wxo, k, v, g2, g3)


def _mem_bwd(mem, gm, dk, dv, wkv_f):
    def body(mem_ref, gm_ref, dk_ref, dv_ref, w_ref, dw_ref, dwb_ref, dgm_ref):
        mem_v = mem_ref[...]
        m, rm = _rms(mem_v, gm_ref[...])
        mt = m.T.astype(BF16)
        dm = jnp.zeros_like(mem_v)
        for j in range(N_CHIP):
            src = dk_ref if j < 2 else dv_ref
            dkv = src[:, pl.ds(KV_BLK * (j % 2), KV_BLK)].astype(BF16)
            dw = jnp.dot(mt, dkv, preferred_element_type=F32)
            dw_ref[j] = dw
            dwb_ref[j] = dw.astype(BF16)
            dm = dm + _bdot_nt(dkv, w_ref[j])
        dgm_ref[...] = jnp.sum(dm * mem_v * rm, axis=0, keepdims=True)

    return pl.pallas_call(
        body, out_shape=[jax.ShapeDtypeStruct((N_CHIP, D, KV_BLK), F32), jax.ShapeDtypeStruct((N_CHIP, D, KV_BLK), BF16),
                         jax.ShapeDtypeStruct((1, D), F32)],
        compiler_params=_cp(None, VMEM_LIMIT), name="mem_bwd")(mem, gm, dk, dv, wkv_f)


def _mixer_bwd(proj, dmix, cw8, lng, lnb, wc, wct, bsb, win_f, x, dx1, g1, after, tm=256):
    t = proj.shape[0]
    nt = t // tm
    nch = tm // CH
    hb = 16
    pair = 2 * CH

    def body(p_ref, pgc_ref, pxa_ref, dm_ref, cw_ref, lng_ref, lnb_ref, wc_ref, wct_ref, bsb_ref, w_ref, x_ref,
             dx1_ref, g1_ref, after_ref, dp_ref, dcw_ref, dlng_ref, dlnb_ref, dwc_ref, dbs_ref, gx_ref, dg1_ref,
             next_ref, dh_ref):
        i = pl.program_id(0)

        @pl.when(i == 0)
        def _():
            next_ref[...] = jnp.zeros_like(next_ref)
            dcw_ref[...] = jnp.zeros_like(dcw_ref)
            dlng_ref[...] = jnp.zeros_like(dlng_ref)
            dlnb_ref[...] = jnp.zeros_like(dlnb_ref)
            dwc_ref[...] = jnp.zeros_like(dwc_ref)
            dbs_ref[...] = jnp.zeros_like(dbs_ref)
            dg1_ref[...] = jnp.zeros_like(dg1_ref)

        first_tile = i == nt - 1
        rows = lax.broadcasted_iota(jnp.int32, (tm, CH), 0)
        ones8 = jnp.ones((8, CH), BF16)
        for s in range(HEADS):
            cs = pl.ds(CH * s, CH)

            def slab(k):
                return p_ref[:, pl.ds(k * SLAB + CH * s, CH)].astype(F32)

            gb, gc, xa, za = slab(0), slab(1), slab(2), slab(3)
            da = dm_ref[:, cs].astype(F32)
            cx = gc * xa
            cxp = pgc_ref[:, cs].astype(F32) * pxa_ref[:, cs].astype(F32)
            cxp = jnp.where(first_tile, jnp.zeros_like(cxp), cxp)
            p6 = jnp.broadcast_to(cxp[hb - 2:hb - 1, :], (tm, CH))
            p7 = jnp.broadcast_to(cxp[hb - 1:hb, :], (tm, CH))
            c1 = jnp.where(rows == 0, p7, pltpu.roll(cx, 1, 0))
            c2 = jnp.where(rows == 0, p6, jnp.where(rows == 1, p7, pltpu.roll(cx, 2, 0)))
            w0, w1, w2 = cw_ref[0:1, cs], cw_ref[1:2, cs], cw_ref[2:3, cs]
            cv = w0 * c2 + w1 * c1 + w2 * cx
            sg = _sigmoid(za)
            sa = za * sg
            dcv = da * gb * sa
            dp_ref[:, pl.ds(0 * SLAB + CH * s, CH)] = (da * cv * sa).astype(BF16)
            dp_ref[:, pl.ds(3 * SLAB + CH * s, CH)] = (da * gb * cv * (sg * (1.0 + za * (1.0 - sg)))).astype(BF16)
            n0 = jnp.broadcast_to(next_ref[0:1, cs], (tm, CH))
            n1 = jnp.broadcast_to(next_ref[1:2, cs], (tm, CH))
            u1 = jnp.where(rows == tm - 1, n0, pltpu.roll(dcv, tm - 1, 0))
            u2 = jnp.where(rows == tm - 2, n0, jnp.where(rows == tm - 1, n1, pltpu.roll(dcv, tm - 2, 0)))
            next_ref[:, cs] = dcv[0:8, :]
            dcx = w2 * dcv + w1 * u1 + w0 * u2
            dp_ref[:, pl.ds(1 * SLAB + CH * s, CH)] = (dcx * xa).astype(BF16)
            dp_ref[:, pl.ds(2 * SLAB + CH * s, CH)] = (dcx * gc).astype(BF16)
            dcw_ref[0:1, cs] += jnp.sum(dcv * c2, axis=0, keepdims=True)
            dcw_ref[1:2, cs] += jnp.sum(dcv * c1, axis=0, keepdims=True)
            dcw_ref[2:3, cs] += jnp.sum(dcv * cx, axis=0, keepdims=True)

            u, v, zb = slab(4), slab(5), slab(6)
            db = dm_ref[:, pl.ds(SLAB + CH * s, CH)].astype(F32)
            ug, ugrad = _gelu_parts(u)
            vg, vgrad = _gelu_parts(v)
            dlt = vg - jnp.mean(vg, axis=-1, keepdims=True)
            rstd = lax.rsqrt(jnp.mean(dlt * dlt, axis=-1, keepdims=True) + EPS)
            vhat = dlt * rstd
            lg = lng_ref[:, cs]
            vn = (vhat * lg + lnb_ref[:, cs]).astype(BF16)
            sgb = _sigmoid(zb)
            szb = zb * sgb
            sps, dvns = [], []
            dbs = jnp.zeros((8, CH), F32)
            dwc = jnp.zeros((CH, CH), F32)
            for c in range(nch):
                rs = slice(CH * c, CH * (c + 1))
                sp = jnp.dot(wc_ref[s], vn[rs], preferred_element_type=F32) + bsb_ref[s]
                dsp = (db[rs] * ug[rs] * szb[rs]).astype(BF16)
                dbs = dbs + lax.dot_general(ones8, dsp, (((1,), (1,)), ((), ())), preferred_element_type=F32)
                dwc = dwc + lax.dot_general(dsp, vn[rs], (((1,), (1,)), ((), ())), preferred_element_type=F32)
                dvns.append(jnp.dot(wct_ref[s], dsp, preferred_element_type=F32))
                sps.append(sp)
            sp = jnp.concatenate(sps, axis=0)
            dvn = jnp.concatenate(dvns, axis=0)
            dbs_ref[:, cs] += dbs
            dwc_ref[s] += dwc
            dlng_ref[:, cs] += jnp.sum(dvn * vhat, axis=0, keepdims=True)
            dlnb_ref[:, cs] += jnp.sum(dvn, axis=0, keepdims=True)
            dvhat = dvn * lg
            dvg = rstd * (dvhat - jnp.mean(dvhat, axis=-1, keepdims=True)
                          - vhat * jnp.mean(dvhat * vhat, axis=-1, keepdims=True))
            dp_ref[:, pl.ds(4 * SLAB + CH * s, CH)] = (db * sp * szb * ugrad).astype(BF16)
            dp_ref[:, pl.ds(5 * SLAB + CH * s, CH)] = (dvg * vgrad).astype(BF16)
            dp_ref[:, pl.ds(6 * SLAB + CH * s, CH)] = (db * ug * sp * (sgb * (1.0 + zb * (1.0 - sgb)))).astype(BF16)

            if s % 2 == 1:
                part = None
                for k in range(N_SLAB):
                    col = k * SLAB + pair * (s // 2)
                    blk, off = divmod(col, IN_BLK)
                    term = lax.dot_general(dp_ref[:, pl.ds(col, pair)], w_ref[blk, :, pl.ds(off, pair)],
                                           (((1,), (1,)), ((), ())), preferred_element_type=F32)
                    part = term if part is None else part + term
                if s == 1:
                    dh_ref[...] = part
                else:
                    dh_ref[...] += part

        xv = x_ref[...]
        r = lax.rsqrt(jnp.mean(xv * xv, axis=-1, keepdims=True) + EPS)
        dxn, dg = _rms_bwd(dh_ref[...], xv, r, g1_ref[...])
        gx_ref[...] = dx1_ref[...].astype(F32) + dxn
        dg1_ref[0:1, :] += dg

        @pl.when(i == nt - 1)
        def _():
            tril = lax.broadcasted_iota(jnp.int32, (CH, CH), 0) >= lax.broadcasted_iota(jnp.int32, (CH, CH), 1)
            for s in range(HEADS):
                dwc_ref[s] = jnp.where(tril, dwc_ref[s], 0.0)

    rev = lambda i: nt - 1 - i
    halo = lambda col: pl.BlockSpec((hb, SLAB), lambda i: (jnp.maximum(rev(i) * (tm // hb) - 1, 0), col))
    tok = lambda w: pl.BlockSpec((tm, w), lambda i: (rev(i), 0))
    return pl.pallas_call(
        body, grid=(nt,),
        in_specs=[tok(IN_DIM), halo(1), halo(2), tok(MIX), _full((8, D)), _full((1, D)), _full((1, D)),
                  _full((HEADS, CH, CH)), _full((HEADS, CH, CH)), _full((HEADS, CH, CH)),
                  _full((N_CHIP, D, IN_BLK), 1), tok(D), tok(D), _full((1, D)), ANY],
        out_specs=[tok(IN_DIM), _full((8, D)), _full((1, D)), _full((1, D)), _full((HEADS, CH, CH)), _full((8, D)),
                   tok(D), _full((8, D))],
        out_shape=[jax.ShapeDtypeStruct((t, IN_DIM), BF16), jax.ShapeDtypeStruct((8, D), F32),
                   jax.ShapeDtypeStruct((1, D), F32), jax.ShapeDtypeStruct((1, D), F32),
                   jax.ShapeDtypeStruct((HEADS, CH, CH), F32), jax.ShapeDtypeStruct((8, D), F32),
                   jax.ShapeDtypeStruct((t, D), F32), jax.ShapeDtypeStruct((8, D), F32)],
        scratch_shapes=[pltpu.VMEM((8, D), F32), pltpu.VMEM((tm, D), F32)],
        compiler_params=_cp(("arbitrary",), VMEM_LIMIT), name="mixer_bwd")(
            proj, proj, proj, dmix, cw8, lng, lnb, wc, wct, bsb, win_f, x, dx1, g1, after)


def _grad_matmul(at, b, after, *, by_cols, name, tk=1024):
    m, t = at.shape
    n = b.shape[1]
    nk = t // tk
    nj = N_CHIP if by_cols else 1
    bn = n // nj

    def body(a_ref, b_ref, after_ref, o_ref, ob_ref):
        kk = pl.program_id(1)
        part = jnp.dot(a_ref[...], b_ref[...], preferred_element_type=F32)

        @pl.when(kk == 0)
        def _():
            o_ref[...] = part

        @pl.when(kk > 0)
        def _():
            o_ref[...] += part

        @pl.when(kk == nk - 1)
        def _():
            ob_ref[...] = o_ref[...].astype(BF16)

    a_spec = pl.BlockSpec((m, tk), lambda j, k: (0, k))
    b_spec = pl.BlockSpec((tk, bn), lambda j, k: (k, j))
    o_spec = pl.BlockSpec((None, m, bn), lambda j, k: (j, 0, 0))
    o32, o16 = pl.pallas_call(
        body, grid=(nj, nk), in_specs=[a_spec, b_spec, ANY], out_specs=[o_spec, o_spec],
        out_shape=[jax.ShapeDtypeStruct((nj, m, bn), F32), jax.ShapeDtypeStruct((nj, m, bn), BF16)],
        compiler_params=_cp(("parallel", "arbitrary"), VMEM_LIMIT), name=name)(at, b, after)
    if by_cols:
        return o32, o16
    return o32.reshape(N_CHIP, m // N_CHIP, n), o16.reshape(N_CHIP, m // N_CHIP, n)


def _coords():
    x, y, c = lax.axis_index("x"), lax.axis_index("y"), lax.axis_index("c")
    chips = [(1 - x, y), (x, 1 - y), (1 - x, 1 - y)]
    return x, y, c, chips


def _pair_exchange(grads_b, smalls, name):
    ng, ns = len(grads_b), len(smalls)
    n = ng + ns

    def body(*refs):
        ins, outs, send, recv = refs[:n], refs[n:2 * n], refs[2 * n], refs[2 * n + 1]
        x, y, c, _ = _coords()
        cps = []
        for i in range(n):
            if i < ng:
                hr = ins[i].shape[1] // 2
                src = ins[i].at[pl.ds(0, N_CHIP), pl.ds((1 - c) * hr, hr)]
            else:
                hr = ins[i].shape[0] // 2
                src = ins[i].at[pl.ds((1 - c) * hr, hr)]
            cps.append(pltpu.make_async_remote_copy(
                src_ref=src, dst_ref=outs[i], send_sem=send.at[i], recv_sem=recv.at[i],
                device_id=(x, y, 1 - c), device_id_type=MESH))
        for cp in cps:
            cp.start()
        for cp in cps:
            cp.wait()

    out_shape = [jax.ShapeDtypeStruct((N_CHIP, g.shape[1] // 2, g.shape[2]), g.dtype) for g in grads_b]
    out_shape += [jax.ShapeDtypeStruct((s.shape[0] // 2, s.shape[1]), s.dtype) for s in smalls]
    return pl.pallas_call(
        body, out_shape=out_shape, in_specs=[ANY] * n, out_specs=[ANY] * n,
        scratch_shapes=[pltpu.SemaphoreType.DMA((n,)), pltpu.SemaphoreType.DMA((n,))],
        name=name)(*grads_b, *smalls)


def _pair_sum(c_idx, grads, recvd, smalls, smalls_recvd, name):
    ng, ns = len(grads), len(smalls)
    halves = [g.shape[1] // 2 for g in grads]

    def body(c_ref, *refs):
        g_in, r_in = refs[:ng], refs[ng:2 * ng]
        s_in, sr_in = refs[2 * ng:2 * ng + ns], refs[2 * ng + ns:2 * ng + 2 * ns]
        o = refs[2 * ng + 2 * ns:]
        for i in range(ng):
            tot = g_in[i][...] + r_in[i][...].astype(F32)
            o[i][...] = tot
            o[ng + i][...] = tot.astype(BF16)
        for i in range(ns):
            o[2 * ng + i][...] = s_in[i][...] + sr_in[i][...]

    in_specs = [pl.BlockSpec((None, None, halves[i], g.shape[2]), lambda b, c: (b, c[0], 0, 0)) for i, g in enumerate(grads)]
    in_specs += [pl.BlockSpec((None, halves[i], g.shape[2]), lambda b, c: (b, 0, 0)) for i, g in enumerate(grads)]
    in_specs += [pl.BlockSpec((None, s.shape[0] // 2, s.shape[1]), lambda b, c: (c[0], 0, 0)) for s in smalls]
    in_specs += [pl.BlockSpec((s.shape[0] // 2, s.shape[1]), lambda b, c: (0, 0)) for s in smalls]
    blk = [pl.BlockSpec((None, halves[i], g.shape[2]), lambda b, c: (b, 0, 0)) for i, g in enumerate(grads)]
    out_specs = blk + blk + [pl.BlockSpec((s.shape[0] // 2, s.shape[1]), lambda b, c: (0, 0)) for s in smalls]
    out_shape = [jax.ShapeDtypeStruct((N_CHIP, halves[i], g.shape[2]), F32) for i, g in enumerate(grads)]
    out_shape += [jax.ShapeDtypeStruct((N_CHIP, halves[i], g.shape[2]), BF16) for i, g in enumerate(grads)]
    out_shape += [jax.ShapeDtypeStruct((s.shape[0] // 2, s.shape[1]), F32) for s in smalls]
    grads4 = [g.reshape(N_CHIP, 2, halves[i], g.shape[2]) for i, g in enumerate(grads)]
    smalls3 = [s.reshape(2, s.shape[0] // 2, s.shape[1]) for s in smalls]
    return pl.pallas_call(
        body, out_shape=out_shape,
        grid_spec=pltpu.PrefetchScalarGridSpec(num_scalar_prefetch=1, grid=(N_CHIP,), in_specs=in_specs, out_specs=out_specs),
        compiler_params=_cp(("arbitrary",), VMEM_LIMIT), name=name)(c_idx, *grads4, *recvd, *smalls3, *smalls_recvd)


_HBM = pl.BlockSpec(memory_space=pltpu.HBM)
_SEM = pl.BlockSpec(memory_space=pltpu.SEMAPHORE)


def _split_copies(ins, lands, ng, send, recv, arriving):
    x, y, c, chips = _coords()
    b = 2 * x + y
    copies = []
    for i in range(len(ins)):
        for k in range(3):
            blk = 2 * chips[k][0] + chips[k][1]
            src, dst, got = (ins[i].at[blk], lands[i].at[k], lands[i].at[k]) if i < ng else (ins[i], lands[i].at[b], lands[i].at[blk])
            sems = dict(send_sem=send.at[3 * i + k], recv_sem=recv.at[3 * i + k], device_id=(*chips[k], c), device_id_type=MESH)
            if arriving:
                copies.append(pltpu.make_async_remote_copy(src_ref=got, dst_ref=got, **sems))
            else:
                copies.append(pltpu.make_async_remote_copy(src_ref=src, dst_ref=dst, **sems))
    return copies


def _exchange_begin(sums_b, smalls, name):
    ng, n = len(sums_b), len(sums_b) + len(smalls)
    srcs = list(sums_b) + list(smalls)
    lands = [lax.empty((3,) + g.shape[1:], g.dtype) for g in sums_b] + [lax.empty((N_CHIP,) + s.shape, s.dtype) for s in smalls]

    def body(*refs):
        ins, land_refs = refs[:n], refs[n:2 * n]
        send, recv = refs[2 * n], refs[2 * n + 1]
        token = refs[4 * n + 2]
        for cp in _split_copies(ins, land_refs, ng, send, recv, False):
            cp.start()
        token[...] = jnp.zeros_like(token)

    hbm = lambda a: pltpu.HBM(a.shape, a.dtype)
    outs = pl.pallas_call(
        body, name=name,
        out_shape=(pltpu.SemaphoreType.DMA((3 * n,)), pltpu.SemaphoreType.DMA((3 * n,)), *[hbm(a) for a in srcs + lands],
                   jax.ShapeDtypeStruct((8, 128), F32)),
        in_specs=[_HBM] * (2 * n), out_specs=(_SEM, _SEM, *[_HBM] * (2 * n), pl.BlockSpec(memory_space=pltpu.VMEM)),
        input_output_aliases={i: 2 + i for i in range(2 * n)},
        compiler_params=pltpu.CompilerParams(has_side_effects=pltpu.SideEffectType.DATAFLOW_SIDE_EFFECTING),
    )(*[pltpu.with_memory_space_constraint(a, pltpu.HBM) for a in srcs + lands])
    return outs[0], outs[1], list(outs[2:2 + n]), list(outs[2 + n:2 + 2 * n]), outs[2 + 2 * n]


def _exchange_end(send, recv, srcs, lands, ng, after, name):
    n = len(srcs)
    after = list(after)

    def body(*refs):
        ins, land_refs = refs[:n], refs[n:2 * n]
        send_ref, recv_ref = refs[2 * n], refs[2 * n + 1]
        for cp in _split_copies(ins, land_refs, ng, send_ref, recv_ref, False):
            cp.wait_send()
        for cp in _split_copies(ins, land_refs, ng, send_ref, recv_ref, True):
            cp.wait_recv()

    hbm = lambda a: pltpu.HBM(a.shape, a.dtype)
    outs = pl.pallas_call(
        body, name=name, out_shape=tuple(hbm(a) for a in list(srcs) + list(lands)),
        in_specs=[_HBM] * (2 * n) + [_SEM, _SEM] + [ANY] * len(after), out_specs=tuple([_HBM] * (2 * n)),
        input_output_aliases={i: i for i in range(2 * n)},
        compiler_params=pltpu.CompilerParams(has_side_effects=pltpu.SideEffectType.DATAFLOW_SIDE_EFFECTING),
    )(*srcs, *lands, send, recv, *after)
    return list(outs[n:])


def _chip_sum(bc_idx, sums, recvd, smalls_slots, smalls_own, name, steps=4):
    ng, ns = len(sums), len(smalls_slots)

    def body(bc_ref, *refs):
        own, rx = refs[:ng], refs[ng:2 * ng]
        sl = refs[2 * ng:2 * ng + ns]
        sl_own = refs[2 * ng + ns:2 * ng + 2 * ns]
        o = refs[2 * ng + 2 * ns:]
        for i in range(ng):
            tot = own[i][...]
            for j in range(3):
                tot = tot + rx[i][j].astype(F32)
            o[i][...] = tot
        for i in range(ns):
            term = [jnp.where(bc_ref[0] == kk, sl_own[i][...], sl[i][kk]) for kk in range(N_CHIP)]
            o[ng + i][...] = ((term[0] + term[1]) + term[2]) + term[3]

    def rows(g):
        return g.shape[1] // steps

    in_specs = [pl.BlockSpec((None, rows(g), g.shape[2]), lambda r, bc: (bc[0], r, 0)) for g in sums]
    in_specs += [pl.BlockSpec((3, rows(g), g.shape[2]), lambda r, bc: (0, r, 0)) for g in sums]
    in_specs += [pl.BlockSpec(s.shape, lambda r, bc: (0, 0, 0)) for s in smalls_slots]
    in_specs += [pl.BlockSpec(s.shape[1:], lambda r, bc: (0, 0)) for s in smalls_slots]
    out_specs = [pl.BlockSpec((rows(g), g.shape[2]), lambda r, bc: (bc[1] * steps + r, 0)) for g in sums]
    out_specs += [pl.BlockSpec(s.shape[1:], lambda r, bc: (bc[1], 0)) for s in smalls_slots]
    out_shape = [jax.ShapeDtypeStruct((2 * g.shape[1], g.shape[2]), F32) for g in sums]
    out_shape += [jax.ShapeDtypeStruct((2 * s.shape[1], s.shape[2]), F32) for s in smalls_slots]
    return pl.pallas_call(
        body, out_shape=out_shape,
        grid_spec=pltpu.PrefetchScalarGridSpec(num_scalar_prefetch=1, grid=(steps,), in_specs=in_specs, out_specs=out_specs),
        compiler_params=_cp(("arbitrary",), VMEM_LIMIT), name=name)(bc_idx, *sums, *recvd, *smalls_slots, *smalls_own)


def _pair_gather(arrs, name):
    n = len(arrs)

    def body(*refs):
        outs = refs[n:2 * n]
        send, recv = refs[2 * n:]
        x, y, c, _ = _coords()
        cps = []
        for i in range(n):
            hr = outs[i].shape[0] // 2
            mine = outs[i].at[pl.ds(c * hr, hr)]
            cps.append(pltpu.make_async_remote_copy(
                src_ref=mine, dst_ref=mine, send_sem=send.at[i], recv_sem=recv.at[i],
                device_id=(x, y, 1 - c), device_id_type=MESH))
        for cp in cps:
            cp.start()
        for i in range(n):
            hr = outs[i].shape[0] // 2
            theirs = outs[i].at[pl.ds((1 - c) * hr, hr)]
            pltpu.make_async_remote_copy(
                src_ref=theirs, dst_ref=theirs, send_sem=send.at[i], recv_sem=recv.at[i],
                device_id=(x, y, 1 - c), device_id_type=MESH).wait_recv()
        for cp in cps:
            cp.wait_send()

    return list(pl.pallas_call(
        body, out_shape=[jax.ShapeDtypeStruct(a.shape, a.dtype) for a in arrs], in_specs=[ANY] * n, out_specs=[ANY] * n,
        input_output_aliases={i: i for i in range(n)},
        scratch_shapes=[pltpu.SemaphoreType.DMA((n,)), pltpu.SemaphoreType.DMA((n,))],
        name=name)(*arrs))


def _adamw_math(w, g, m, v):
    m2 = ADAM_B1 * m + (1.0 - ADAM_B1) * g
    v2 = ADAM_B2 * v + (1.0 - ADAM_B2) * (g * g)
    m_hat = m2 / (1.0 - ADAM_B1 ** ADAM_STEP)
    v_hat = v2 / (1.0 - ADAM_B2 ** ADAM_STEP)
    delta = -ADAM_LR * (m_hat / (jnp.sqrt(v_hat) + ADAM_EPS) + ADAM_WD * w)
    return delta, m2, v2


def _adamw_big(w, g, m, v, name, steps=8):
    r, c = w.shape

    def body(w_ref, g_ref, m_ref, v_ref, d_ref, m2_ref, v2_ref, g2_ref):
        gv = g_ref[...]
        d_ref[...], m2_ref[...], v2_ref[...] = _adamw_math(w_ref[...], gv, m_ref[...], v_ref[...])
        g2_ref[...] = gv

    spec = pl.BlockSpec((r // steps, c), lambda i: (i, 0))
    return pl.pallas_call(
        body, grid=(steps,), in_specs=[spec] * 4, out_specs=[spec] * 4,
        out_shape=[jax.ShapeDtypeStruct((r, c), F32)] * 4,
        compiler_params=_cp(("parallel",), VMEM_LIMIT), name=name)(w, g, m, v)


def _adamw_small(groups):
    n = len(groups)

    def body(*refs):
        for i in range(n):
            w_ref, g_ref, m_ref, v_ref = refs[4 * i:4 * i + 4]
            d_ref, m2_ref, v2_ref = refs[4 * n + 3 * i:4 * n + 3 * i + 3]
            d_ref[...], m2_ref[...], v2_ref[...] = _adamw_math(w_ref[...], g_ref[...], m_ref[...], v_ref[...])

    flat = [a for grp in groups for a in grp]
    out_shape = [jax.ShapeDtypeStruct(grp[0].shape, F32) for grp in groups for _ in range(3)]
    outs = pl.pallas_call(body, out_shape=out_shape, name="adamw_small")(*flat)
    return [tuple(outs[3 * i:3 * i + 3]) for i in range(n)]


def kernel(x, mem, norm_mix_g, w_in, conv_w, gm_ln_g, gm_ln_b, gm_ws, gm_bs, w_out, norm_x_g, norm_mem_g, w_q, w_kv, w_xo, norm_final_g, loss_target, m_norm_mix_g, m_w_in, m_conv_w, m_gm_ln_g, m_gm_ln_b, m_gm_ws, m_gm_bs, m_w_out, m_norm_x_g, m_norm_mem_g, m_w_q, m_w_kv, m_w_xo, m_norm_final_g, v_norm_mix_g, v_w_in, v_conv_w, v_gm_ln_g, v_gm_ln_b, v_gm_ws, v_gm_bs, v_w_out, v_norm_x_g, v_norm_mem_g, v_w_q, v_w_kv, v_w_xo, v_norm_final_g):
    t = x.shape[1]
    xi = lax.axis_index("x")
    yi = lax.axis_index("y")
    ci = lax.axis_index("c")
    b_idx = jnp.reshape(2 * xi + yi, (1,)).astype(jnp.int32)
    c_idx = jnp.reshape(ci, (1,)).astype(jnp.int32)

    x2d, mem2d, tgt = x[0], mem[0], loss_target[0]
    big = [w_in[0], w_out[0], w_q[0], w_kv[0], w_xo[0]]
    big_m = [m_w_in[0], m_w_out[0], m_w_q[0], m_w_kv[0], m_w_xo[0]]
    big_v = [v_w_in[0], v_w_out[0], v_w_q[0], v_w_kv[0], v_w_xo[0]]
    g3 = norm_final_g.reshape(1, D)

    def pad8(a):
        return jnp.pad(a, ((0, 8 - a.shape[0]), (0, 0)))

    own_blocks = _cast_shards(b_idx, big)

    tril = jnp.tril(jnp.ones((CH, CH), bool))
    wc32 = jnp.where(tril[None], gm_ws[0], 0.0)
    wc = wc32.astype(BF16)
    wct = jnp.swapaxes(wc32, 1, 2).astype(BF16)
    bsb = jnp.broadcast_to(gm_bs[0][:, :, None], (HEADS, CH, CH))

    blk = 2 * xi + yi
    order = jnp.stack([blk, blk ^ 2, blk ^ 1, blk ^ 3]).astype(jnp.int32)
    proj, ht, win_f, cw8, (wout_f,) = _proj_gather(
        order, x2d, norm_mix_g, own_blocks[0], pad8(conv_w[0]), [own_blocks[1]])
    mixin, mixt, (wq_f, wkv_f, wxo_f) = _mixer_fwd(proj, cw8, gm_ln_g, gm_ln_b, wc, bsb, own_blocks[2:])
    wout2, wq2, wxo2 = wout_f.reshape(MIX, D), wq_f.reshape(D, D), wxo_f.reshape(D, D)
    k, v, mt = _mem_fwd(mem2d, norm_mem_g, wkv_f)
    del mt

    (loss_tile, dmix, dx1b, h2t, dq, ot, dx2b, dk, dv, dg2, dg3) = _tail(
        x2d, tgt, mixin, wout2, wq2, wxo2, k, v, norm_x_g, g3)
    dwkv, dwkv_b, dgm = _mem_bwd(mem2d, norm_mem_g, dk, dv, wkv_f)
    dproj, dcw, dlng, dlnb, dwc, dbs8, grad_x, dg1 = _mixer_bwd(
        proj, dmix, cw8, gm_ln_g, gm_ln_b, wc, wct, bsb, win_f, x2d, dx1b, norm_mix_g, dgm)

    bc_idx = jnp.concatenate([b_idx, c_idx])
    dwin, dwin_b = _grad_matmul(ht, dproj, dgm, by_cols=True, name="grad_w_in", tk=2048)
    zero = jnp.zeros((1, D), F32)
    loss_row = jnp.broadcast_to(loss_tile[0:1, 0:1], (1, D))
    sv = jnp.concatenate([dg1[0:1], dg2, dgm, dg3, dlng, dlnb, dbs8[0:1], loss_row, dcw], axis=0)
    sw = dwc.reshape(HEADS * CH, CH)
    rx1b = _pair_exchange([dwin_b], [sv, sw], "pair_exchange_b")
    ps_b = _pair_sum(c_idx, [dwin], rx1b[:1], [sv, sw], rx1b[1:], "pair_sum_b")
    sums_b, sums_b_b, psmall = list(ps_b[:1]), list(ps_b[1:2]), list(ps_b[2:])
    send_b, recv_b, src_b, land_b, token_b = _exchange_begin(sums_b_b, psmall, "exchange_b_begin")

    dwxo, dwxo_b = _grad_matmul(ot, dx2b, token_b, by_cols=False, name="grad_w_xo", tk=2048)
    dwq, dwq_b = _grad_matmul(h2t, dq, token_b, by_cols=False, name="grad_w_q", tk=2048)
    dwout, dwout_b = _grad_matmul(mixt, dx1b, token_b, by_cols=False, name="grad_w_out")
    rx1a = _pair_exchange([dwout_b, dwq_b, dwkv_b, dwxo_b], [], "pair_exchange_a")
    ps_a = _pair_sum(c_idx, [dwout, dwq, dwkv, dwxo], rx1a, [], [], "pair_sum_a")
    sums_a, sums_a_b = list(ps_a[:4]), list(ps_a[4:8])
    send_a, recv_a, src_a, land_a, token_a = _exchange_begin(sums_a_b, [], "exchange_a_begin")

    rx2b = _exchange_end(send_b, recv_b, src_b, land_b, 1, [token_a], "exchange_b_end")
    red_b = _chip_sum(bc_idx, sums_b, rx2b[:1], rx2b[1:], psmall, "chip_sum_b")
    gwin, svf, swf = _pair_gather(red_b, "pair_gather_b")
    out_b = _adamw_big(big[0], gwin, big_m[0], big_v[0], "adamw_w_in")

    def vec_pack(a1, a2, am, a3, lg, lb, bs):
        return jnp.concatenate([a1, a2, am, a3.reshape(1, D), lg, lb, bs.reshape(1, D), zero], axis=0)

    wv = vec_pack(norm_mix_g, norm_x_g, norm_mem_g, norm_final_g, gm_ln_g, gm_ln_b, gm_bs)
    mv = vec_pack(m_norm_mix_g, m_norm_x_g, m_norm_mem_g, m_norm_final_g, m_gm_ln_g, m_gm_ln_b, m_gm_bs)
    vv = vec_pack(v_norm_mix_g, v_norm_x_g, v_norm_mem_g, v_norm_final_g, v_gm_ln_g, v_gm_ln_b, v_gm_bs)
    loss = svf[7, 0]
    gcw = lax.dynamic_slice_in_dim(svf[8:16], blk * (D // N_CHIP), D // N_CHIP, axis=1)
    gws = swf
    gv = svf[0:8]
    (dv_, mv_, vv_), (dc_, mc_, vc_), (dws_, mws_, vws_) = _adamw_small([
        (wv, gv, mv, vv),
        (pad8(conv_w[0]), gcw, pad8(m_conv_w[0]), pad8(v_conv_w[0])),
        (gm_ws.reshape(HEADS * CH, CH), gws, m_gm_ws.reshape(HEADS * CH, CH), v_gm_ws.reshape(HEADS * CH, CH))])

    rx2a = _exchange_end(send_a, recv_a, src_a, land_a, 4, [out_b[0], dv_], "exchange_a_end")
    red_a = _chip_sum(bc_idx, sums_a, rx2a, [], [], "chip_sum_a")
    g_a = _pair_gather(red_a, "pair_gather_a")
    names = ["w_out", "w_q", "w_kv", "w_xo"]
    out_a = [_adamw_big(w, g, m, v, "adamw_" + nm) for w, g, m, v, nm in zip(big[1:], g_a, big_m[1:], big_v[1:], names)]
    big_out = [out_b] + out_a

    def unpack(vecs, cw, ws, bigs):
        r = lambda i: vecs[i:i + 1]
        return [r(0), bigs[0][None], cw[0:3][None], r(4), r(5), ws.reshape(1, HEADS, CH, CH), vecs[6].reshape(1, HEADS, CH),
                bigs[1][None], r(1), r(2), bigs[2][None], bigs[3][None], bigs[4][None], vecs[3]]

    grads_out = unpack(gv, gcw, gws, [o[3] for o in big_out])
    delta_out = unpack(dv_, dc_, dws_, [o[0] for o in big_out])
    m_out = unpack(mv_, mc_, mws_, [o[1] for o in big_out])
    v_out = unpack(vv_, vc_, vws_, [o[2] for o in big_out])
    return (loss, grad_x[None], *grads_out, *delta_out, *m_out, *v_out)
```

```python
import functools
import math

import jax
import jax.numpy as jnp
from jax import lax
from jax.experimental import pallas as pl
from jax.experimental.pallas import tpu as pltpu

F32 = jnp.float32
BF16 = jnp.bfloat16
MESH = pl.DeviceIdType.MESH

D = 1024
SLAB = 1024
N_SLAB = 7
IN_DIM = N_SLAB * SLAB
MIX = 2 * SLAB
HEADS = 8
CH = 128
XH = 4
XD = D // XH
EPS = 1e-6
GELU_C = math.sqrt(2.0 / math.pi)
GELU_A = 0.044715
N_CHIP = 4
IN_BLK = IN_DIM // N_CHIP
KV_BLK = 2 * D // N_CHIP

ADAM_LR, ADAM_B1, ADAM_B2, ADAM_EPS, ADAM_WD, ADAM_STEP = 0.001, 0.9, 0.999, 1e-08, 0.01, 10

VMEM_LIMIT = 60 * 1024 * 1024


def _cp(sem=None, vmem=None):
    return pltpu.CompilerParams(dimension_semantics=sem, vmem_limit_bytes=vmem)


def _full(shape, buffers=None):
    n = len(shape)
    if buffers is None:
        return pl.BlockSpec(shape, lambda *_: (0,) * n)
    return pl.BlockSpec(shape, lambda *_: (0,) * n, pipeline_mode=pl.Buffered(buffers))


ANY = pl.BlockSpec(memory_space=pl.ANY)


def _bdot(a, b):
    return jnp.dot(a.astype(BF16), b.astype(BF16), preferred_element_type=F32)


def _bdot_nt(a, b):
    return lax.dot_general(a.astype(BF16), b.astype(BF16), (((1,), (1,)), ((), ())), preferred_element_type=F32)


def _rms(x, g):
    r = lax.rsqrt(jnp.mean(x * x, axis=-1, keepdims=True) + EPS)
    return x * r * g, r


def _rms_bwd(dy, x, r, g):
    gdy = dy * g
    dx = r * gdy - x * (r * r * r) * jnp.mean(x * gdy, axis=-1, keepdims=True)
    dg = jnp.sum(dy * x * r, axis=0, keepdims=True)
    return dx, dg


def _gelu_parts(x):
    x2 = x * x
    t = jnp.tanh(GELU_C * (x + GELU_A * x * x2))
    val = 0.5 * x * (1.0 + t)
    grad = 0.5 * (1.0 + t) + 0.5 * x * (1.0 - t * t) * (GELU_C * (1.0 + 3.0 * GELU_A * x2))
    return val, grad


def _gelu(x):
    return 0.5 * x * (1.0 + jnp.tanh(GELU_C * (x + GELU_A * x * x * x)))


def _sigmoid(z):
    return 1.0 / (1.0 + jnp.exp(-z))


def _cast_shards(b_idx, arrs):
    n = len(arrs)
    steps = 8

    def body(b_ref, *refs):
        for i in range(n):
            refs[n + i][...] = refs[i][...].astype(BF16)

    in_specs = [pl.BlockSpec((a.shape[0] // steps, a.shape[1]), lambda i, b: (i, 0)) for a in arrs]
    out_specs = [pl.BlockSpec((None, a.shape[0] // steps, a.shape[1]), lambda i, b: (b[0], i, 0)) for a in arrs]
    return pl.pallas_call(
        body, out_shape=[jax.ShapeDtypeStruct((N_CHIP,) + a.shape, BF16) for a in arrs],
        grid_spec=pltpu.PrefetchScalarGridSpec(num_scalar_prefetch=1, grid=(steps,), in_specs=in_specs, out_specs=out_specs),
        compiler_params=_cp(("arbitrary",)), name="cast_shards")(b_idx, *arrs)


def _proj_gather(order, x, g, win_own, cw8s, more, tm=1024):
    t = x.shape[0]
    ni = t // tm
    hr = D // 2
    nm = len(more)

    def body(*refs):
        order_ref, x_ref, g_ref, win_in, cw_in = refs[:5]
        o_ref, ht_ref, win_f, cw_out = refs[5 + nm:9 + nm]
        more_out = refs[9 + nm:9 + 2 * nm]
        hbuf, wv, ici_s, ici_r, d2d_s, d2d_r, cw_s, cw_r, loc, m_is, m_ir, m_ds, m_dr = refs[9 + 2 * nm:]
        more_start, more_pass_on, more_finish = _gather_steps(more_out, m_is, m_ir, m_ds, m_dr)
        j, i = pl.program_id(0), pl.program_id(1)
        x, y, c, chips = _coords()
        b = 2 * x + y
        sib = (x, y, 1 - c)
        blks = [2 * chip[0] + chip[1] for chip in chips]

        def half(blk, hc):
            return win_f.at[blk, pl.ds(hc * hr, hr)]

        def cw_cols(blk):
            return cw_out.at[:, pl.ds(blk * (D // N_CHIP), D // N_CHIP)]

        def ici(k, blk):
            return pltpu.make_async_remote_copy(src_ref=half(blk, c), dst_ref=half(blk, c), send_sem=ici_s.at[k],
                                                recv_sem=ici_r.at[k], device_id=(*chips[k], c), device_id_type=MESH)

        def relay(k, blk, hc):
            return pltpu.make_async_remote_copy(src_ref=half(blk, hc), dst_ref=half(blk, hc), send_sem=d2d_s.at[k],
                                                recv_sem=d2d_r.at[k], device_id=sib, device_id_type=MESH)

        def cw_copy(k, blk):
            src = cw_in if blk is None else cw_cols(blk)
            return pltpu.make_async_remote_copy(src_ref=src, dst_ref=cw_cols(b if blk is None else blk), send_sem=cw_s.at[k],
                                                recv_sem=cw_r.at[k], device_id=(*chips[k], c), device_id_type=MESH)

        cw_local = pltpu.make_async_copy(cw_in, cw_cols(b), loc.at[1])

        @pl.when((j == 0) & (i == 0))
        def _():
            cw_local.start()
            for k in range(3):
                ici(k, b).start()
            for k in range(3):
                cw_copy(k, None).start()
            more_start()

        for jj in range(N_CHIP):
            @pl.when((j == jj) & (i == 0))
            def _(jj=jj):
                if jj == 0:
                    blk = b
                else:
                    blk = blks[jj - 1]
                    ici(jj - 1, blk).wait_recv()
                    relay(jj - 1, blk, c).start()
                    relay(jj - 1, blk, 1 - c).wait_recv()
                load = pltpu.make_async_copy(win_f.at[blk], wv, loc.at[0])
                load.start()
                load.wait()

        rows = pl.ds(pl.multiple_of(i * tm, tm), tm)

        @pl.when(j == 0)
        def _():
            h, _ = _rms(x_ref[...], g_ref[...])
            hbuf[rows, :] = h.astype(BF16)
            ht_ref[...] = h.T.astype(BF16)

        @pl.when((j == N_CHIP - 1) & (i == ni - 1))
        def _():
            more_pass_on()

        o_ref[...] = jnp.dot(hbuf[rows, :], wv[...], preferred_element_type=F32).astype(BF16)

        @pl.when((j == N_CHIP - 1) & (i == ni - 1))
        def _():
            for k in range(3):
                cw_copy(k, blks[k]).wait_recv()
            for k in range(3):
                ici(k, b).wait_send()
                relay(k, blks[k], c).wait_send()
                cw_copy(k, None).wait_send()
            cw_local.wait()
            more_finish()

    first = lambda j, i: jnp.where(j == 0, i, ni - 1)
    in_specs = [pl.BlockSpec((tm, D), lambda j, i, o: (first(j, i), 0)), pl.BlockSpec((1, D), lambda j, i, o: (0, 0)),
                ANY, ANY] + [ANY] * nm
    out_specs = [pl.BlockSpec((tm, IN_BLK), lambda j, i, o: (i, o[j])),
                 pl.BlockSpec((D, tm), lambda j, i, o: (0, first(j, i))), ANY, ANY] + [ANY] * nm
    outs = pl.pallas_call(
        body, out_shape=[jax.ShapeDtypeStruct((t, IN_DIM), BF16), jax.ShapeDtypeStruct((D, t), BF16),
                         jax.ShapeDtypeStruct(win_own.shape, BF16), jax.ShapeDtypeStruct((8, D), F32)]
        + [jax.ShapeDtypeStruct(f.shape, f.dtype) for f in more],
        grid_spec=pltpu.PrefetchScalarGridSpec(
            num_scalar_prefetch=1, grid=(N_CHIP, ni), in_specs=in_specs, out_specs=out_specs,
            scratch_shapes=[pltpu.VMEM((t, D), BF16), pltpu.VMEM((D, IN_BLK), BF16)]
            + [pltpu.SemaphoreType.DMA((3,))] * 6 + [pltpu.SemaphoreType.DMA((2,))]
            + [pltpu.SemaphoreType.DMA((max(nm, 1), 3))] * 4),
        input_output_aliases={3: 2, **{5 + w: 4 + w for w in range(nm)}},
        compiler_params=_cp(("arbitrary", "arbitrary"), VMEM_LIMIT), name="proj_gather")(order, x, g, win_own, cw8s, *more)
    return outs[0], outs[1], outs[2], outs[3], outs[4:]


def _gather_steps(outs, ici_s, ici_r, d2d_s, d2d_r):
    x, y, c, chips = _coords()
    b = 2 * x + y
    sib = (x, y, 1 - c)
    nw = len(outs)

    def half(w, blk, hc):
        hr = outs[w].shape[1] // 2
        return outs[w].at[blk, pl.ds(hc * hr, hr)]

    def ici(w, k, blk):
        return pltpu.make_async_remote_copy(src_ref=half(w, blk, c), dst_ref=half(w, blk, c), send_sem=ici_s.at[w, k],
                                            recv_sem=ici_r.at[w, k], device_id=(*chips[k], c), device_id_type=MESH)

    def relay(w, k, blk, hc):
        return pltpu.make_async_remote_copy(src_ref=half(w, blk, hc), dst_ref=half(w, blk, hc), send_sem=d2d_s.at[w, k],
                                            recv_sem=d2d_r.at[w, k], device_id=sib, device_id_type=MESH)

    def start():
        for w in range(nw):
            for k in range(3):
                ici(w, k, b).start()

    def pass_on():
        for w in range(nw):
            for k in range(3):
                blk = 2 * chips[k][0] + chips[k][1]
                ici(w, k, blk).wait_recv()
                relay(w, k, blk, c).start()

    def finish():
        for w in range(nw):
            for k in range(3):
                blk = 2 * chips[k][0] + chips[k][1]
                relay(w, k, blk, 1 - c).wait_recv()
        for w in range(nw):
            for k in range(3):
                blk = 2 * chips[k][0] + chips[k][1]
                ici(w, k, b).wait_send()
                relay(w, k, blk, c).wait_send()

    return start, pass_on, finish


def _mixer_fwd(proj, cw8, lng, lnb, wc, bsb, fulls, tm=256):
    t = proj.shape[0]
    nt = t // tm
    nch = tm // CH
    nw = len(fulls)

    def body(*refs):
        p_ref, cw_ref, lng_ref, lnb_ref, wc_ref, bsb_ref = refs[:6]
        mix_ref, mixt_ref, sav_ref = refs[6 + nw:9 + nw]
        w_outs = refs[9 + nw:9 + 2 * nw]
        prev_ref, stage_ref, ici_s, ici_r, d2d_s, d2d_r = refs[9 + 2 * nw:]
        gather_start, gather_pass_on, gather_finish = _gather_steps(w_outs, ici_s, ici_r, d2d_s, d2d_r)

        @pl.when(pl.program_id(0) == 0)
        def _():
            gather_start()
            prev_ref[...] = jnp.zeros_like(prev_ref)

        @pl.when(pl.program_id(0) == nt - 1)
        def _():
            gather_pass_on()

        rows = lax.broadcasted_iota(jnp.int32, (tm, CH), 0)
        for s in range(HEADS):
            cs = pl.ds(CH * s, CH)

            def slab(k):
                return p_ref[:, pl.ds(k * SLAB + CH * s, CH)].astype(F32)

            gb, gc, xa, za = slab(0), slab(1), slab(2), slab(3)
            cx = gc * xa
            p6 = jnp.broadcast_to(prev_ref[6:7, cs], (tm, CH))
            p7 = jnp.broadcast_to(prev_ref[7:8, cs], (tm, CH))
            c1 = jnp.where(rows == 0, p7, pltpu.roll(cx, 1, 0))
            c2 = jnp.where(rows == 0, p6, jnp.where(rows == 1, p7, pltpu.roll(cx, 2, 0)))
            prev_ref[:, cs] = cx[tm - 8:, :]
            cv = cw_ref[0:1, cs] * c2 + cw_ref[1:2, cs] * c1 + cw_ref[2:3, cs] * cx
            stage_ref[:, cs] = gb * cv * (za * _sigmoid(za))

            u, v, zb = slab(4), slab(5), slab(6)
            ug, ugrad = _gelu_parts(u)
            vg, vgrad = _gelu_parts(v)
            dlt = vg - jnp.mean(vg, axis=-1, keepdims=True)
            rstd = lax.rsqrt(jnp.mean(dlt * dlt, axis=-1, keepdims=True) + EPS)
            vhat = dlt * rstd
            sav_ref[:, pl.ds(0 * SLAB + CH * s, CH)] = ug.astype(BF16)
            sav_ref[:, pl.ds(1 * SLAB + CH * s, CH)] = ugrad.astype(BF16)
            sav_ref[:, pl.ds(2 * SLAB + CH * s, CH)] = (vgrad * rstd).astype(BF16)
            sav_ref[:, pl.ds(3 * SLAB + CH * s, CH)] = vhat.astype(BF16)
            vn = (vhat * lng_ref[:, cs] + lnb_ref[:, cs]).astype(BF16)
            gate = ug * (zb * _sigmoid(zb))
            for c in range(nch):
                rs = slice(CH * c, CH * (c + 1))
                sp = jnp.dot(wc_ref[s], vn[rs], preferred_element_type=F32) + bsb_ref[s]
                stage_ref[rs, pl.ds(SLAB + CH * s, CH)] = gate[rs] * sp

        full = stage_ref[...]
        mix_ref[...] = full.astype(BF16)
        mixt_ref[...] = full.T.astype(BF16)

        @pl.when(pl.program_id(0) == nt - 1)
        def _():
            gather_finish()

    sems = [pltpu.SemaphoreType.DMA((nw, 3))] * 4
    outs = pl.pallas_call(
        body, grid=(nt,),
        in_specs=[pl.BlockSpec((tm, IN_DIM), lambda i: (i, 0)), _full((8, D)), _full((1, D)), _full((1, D)),
                  _full((HEADS, CH, CH)), _full((HEADS, CH, CH))] + [ANY] * nw,
        out_specs=[pl.BlockSpec((tm, MIX), lambda i: (i, 0)), pl.BlockSpec((MIX, tm), lambda i: (0, i)),
                   pl.BlockSpec((tm, 4 * SLAB), lambda i: (i, 0))] + [ANY] * nw,
        out_shape=[jax.ShapeDtypeStruct((t, MIX), BF16), jax.ShapeDtypeStruct((MIX, t), BF16),
                   jax.ShapeDtypeStruct((t, 4 * SLAB), BF16)]
        + [jax.ShapeDtypeStruct(f.shape, f.dtype) for f in fulls],
        input_output_aliases={6 + w: 3 + w for w in range(nw)},
        scratch_shapes=[pltpu.VMEM((8, D), F32), pltpu.VMEM((tm, MIX), F32)] + sems,
        compiler_params=_cp(("arbitrary",), VMEM_LIMIT), name="mixer_fwd")(proj, cw8, lng, lnb, wc, bsb, *fulls)
    return outs[0], outs[1], outs[2], outs[3:]


def _mem_fwd(mem, gm, wkv_f):
    n_mem = mem.shape[0]

    def body(mem_ref, gm_ref, w_ref, k_ref, v_ref, mt_ref):
        m, _ = _rms(mem_ref[...], gm_ref[...])
        mb = m.astype(BF16)
        mt_ref[...] = m.T.astype(BF16)
        for j in range(N_CHIP):
            dst = k_ref if j < 2 else v_ref
            dst[:, pl.ds(KV_BLK * (j % 2), KV_BLK)] = jnp.dot(mb, w_ref[j], preferred_element_type=F32).astype(BF16)

    return pl.pallas_call(
        body, out_shape=[jax.ShapeDtypeStruct((n_mem, D), BF16), jax.ShapeDtypeStruct((n_mem, D), BF16),
                         jax.ShapeDtypeStruct((D, n_mem), BF16)],
        compiler_params=_cp(None, VMEM_LIMIT), name="mem_fwd")(mem, gm, wkv_f)


def _tail(x, tgt, mixin, wout, wq, wxo, k, v, g2, g3, tm=512, sub=512):
    t = x.shape[0]
    n_mem = k.shape[0]
    scale = 1.0 / math.sqrt(XD)

    def body(x_ref, tgt_ref, mix_ref, wout_ref, wq_ref, wxo_ref, k_ref, v_ref, g2_ref, g3_ref,
             loss_ref, dmix_ref, dx1b_ref, h2t_ref, dq_ref, ot_ref, dx2b_ref, dk_ref, dv_ref, dg2_ref, dg3_ref):
        @pl.when(pl.program_id(0) == 0)
        def _():
            loss_ref[...] = jnp.zeros_like(loss_ref)
            dk_ref[...] = jnp.zeros_like(dk_ref)
            dv_ref[...] = jnp.zeros_like(dv_ref)
            dg2_ref[...] = jnp.zeros_like(dg2_ref)
            dg3_ref[...] = jnp.zeros_like(dg3_ref)

        g2, g3 = g2_ref[...], g3_ref[...]
        for sb in range(tm // sub):
            rs = pl.ds(sub * sb, sub)
            x1 = x_ref[rs, :] + jnp.dot(mix_ref[rs, :], wout_ref[...], preferred_element_type=F32)
            h2, r2 = _rms(x1, g2)
            h2t_ref[:, rs] = h2.T.astype(BF16)
            q = _bdot(h2, wq_ref[...]).astype(BF16)
            probs, outs = [], []
            for hd in range(XH):
                hs = pl.ds(XD * hd, XD)
                s = _bdot_nt(q[:, XD * hd:XD * (hd + 1)], k_ref[:, hs]) * scale
                e = jnp.exp(s - jnp.max(s, axis=-1, keepdims=True))
                p = e / jnp.sum(e, axis=-1, keepdims=True)
                probs.append(p)
                outs.append(_bdot(p, v_ref[:, hs]))
            o = jnp.concatenate(outs, axis=-1)
            ot_ref[:, rs] = o.T.astype(BF16)
            x2 = x1 + _bdot(o, wxo_ref[...])
            y, r3 = _rms(x2, g3)
            diff = y - tgt_ref[rs, :]
            row_loss = jnp.sum(diff * diff, axis=-1, keepdims=True)
            loss_ref[...] += jnp.broadcast_to(jnp.sum(row_loss, axis=0, keepdims=True) * (0.5 / D), loss_ref.shape)

            dx2, dg3 = _rms_bwd(diff * (1.0 / D), x2, r3, g3)
            dg3_ref[...] += dg3
            dx2b = dx2.astype(BF16)
            dx2b_ref[rs, :] = dx2b
            do = _bdot_nt(dx2b, wxo_ref[...])
            dqs = []
            for hd in range(XH):
                hs = pl.ds(XD * hd, XD)
                p = probs[hd]
                do_h = do[:, XD * hd:XD * (hd + 1)]
                dv_ref[:, hs] += _bdot(p.T, do_h)
                dp = _bdot_nt(do_h, v_ref[:, hs])
                ds = p * (dp - jnp.sum(dp * p, axis=-1, keepdims=True))
                dqs.append(_bdot(ds, k_ref[:, hs]) * scale)
                dk_ref[:, hs] += _bdot(ds.T, q[:, XD * hd:XD * (hd + 1)]) * scale
            dq = jnp.concatenate(dqs, axis=-1).astype(BF16)
            dq_ref[rs, :] = dq
            dx1n, dg2 = _rms_bwd(_bdot_nt(dq, wq_ref[...]), x1, r2, g2)
            dg2_ref[...] += dg2
            dx1b = (dx2 + dx1n).astype(BF16)
            dx1b_ref[rs, :] = dx1b
            dmix_ref[rs, :] = _bdot_nt(dx1b, wout_ref[...]).astype(BF16)

    tok = lambda w: pl.BlockSpec((tm, w), lambda i: (i, 0))
    tok_t = lambda w: pl.BlockSpec((w, tm), lambda i: (0, i))
    return pl.pallas_call(
        body, grid=(t // tm,),
        in_specs=[tok(D), tok(D), tok(MIX), _full((MIX, D), 1), _full((D, D), 1), _full((D, D), 1),
                  _full((n_mem, D), 1), _full((n_mem, D), 1), _full((1, D)), _full((1, D))],
        out_specs=[_full((8, 128)), tok(MIX), tok(D), tok_t(D), tok(D), tok_t(D), tok(D),
                   _full((n_mem, D)), _full((n_mem, D)), _full((1, D)), _full((1, D))],
        out_shape=[jax.ShapeDtypeStruct((8, 128), F32), jax.ShapeDtypeStruct((t, MIX), BF16),
                   jax.ShapeDtypeStruct((t, D), BF16),
                   jax.ShapeDtypeStruct((D, t), BF16), jax.ShapeDtypeStruct((t, D), BF16),
                   jax.ShapeDtypeStruct((D, t), BF16), jax.ShapeDtypeStruct((t, D), BF16),
                   jax.ShapeDtypeStruct((n_mem, D), F32), jax.ShapeDtypeStruct((n_mem, D), F32),
                   jax.ShapeDtypeStruct((1, D), F32), jax.ShapeDtypeStruct((1, D), F32)],
        compiler_params=_cp(("arbitrary",), VMEM_LIMIT), name="tail")(x, tgt, mixin, wout, wq, wxo, k, v, g2, g3)


def _mem_bwd(mem, gm, dk, dv, wkv_f):
    def body(mem_ref, gm_ref, dk_ref, dv_ref, w_ref, dw_ref, dwb_ref, dgm_ref):
        mem_v = mem_ref[...]
        m, rm = _rms(mem_v, gm_ref[...])
        mt = m.T.astype(BF16)
        dm = jnp.zeros_like(mem_v)
        for j in range(N_CHIP):
            src = dk_ref if j < 2 else dv_ref
            dkv = src[:, pl.ds(KV_BLK * (j % 2), KV_BLK)].astype(BF16)
            dw = jnp.dot(mt, dkv, preferred_element_type=F32)
            dw_ref[j] = dw
            dwb_ref[j] = dw.astype(BF16)
            dm = dm + _bdot_nt(dkv, w_ref[j])
        dgm_ref[...] = jnp.sum(dm * mem_v * rm, axis=0, keepdims=True)

    return pl.pallas_call(
        body, out_shape=[jax.ShapeDtypeStruct((N_CHIP, D, KV_BLK), F32), jax.ShapeDtypeStruct((N_CHIP, D, KV_BLK), BF16),
                         jax.ShapeDtypeStruct((1, D), F32)],
        compiler_params=_cp(None, VMEM_LIMIT), name="mem_bwd")(mem, gm, dk, dv, wkv_f)


def _mixer_bwd(proj, sav, dmix, cw8, lng, lnb, wc, wct, bsb, win_f, x, dx1, g1, tm=256):
    t = proj.shape[0]
    nt = t // tm
    nch = tm // CH
    hb = 16
    pair = 2 * CH

    def body(p_ref, pgc_ref, pxa_ref, sav_ref, dm_ref, cw_ref, lng_ref, lnb_ref, wc_ref, wct_ref, bsb_ref, w_ref, x_ref,
             dx1_ref, g1_ref, dp_ref, dcw_ref, dlng_ref, dlnb_ref, dwc_ref, dbs_ref, gx_ref, dg1_ref,
             next_ref, dh_ref):
        i = pl.program_id(0)

        @pl.when(i == 0)
        def _():
            next_ref[...] = jnp.zeros_like(next_ref)
            dcw_ref[...] = jnp.zeros_like(dcw_ref)
            dlng_ref[...] = jnp.zeros_like(dlng_ref)
            dlnb_ref[...] = jnp.zeros_like(dlnb_ref)
            dwc_ref[...] = jnp.zeros_like(dwc_ref)
            dbs_ref[...] = jnp.zeros_like(dbs_ref)
            dg1_ref[...] = jnp.zeros_like(dg1_ref)

        first_tile = i == nt - 1
        rows = lax.broadcasted_iota(jnp.int32, (tm, CH), 0)
        ones8 = jnp.ones((8, CH), BF16)
        for s in range(HEADS):
            cs = pl.ds(CH * s, CH)

            def slab(k):
                return p_ref[:, pl.ds(k * SLAB + CH * s, CH)].astype(F32)

            gb, gc, xa, za = slab(0), slab(1), slab(2), slab(3)
            da = dm_ref[:, cs].astype(F32)
            cx = gc * xa
            cxp = pgc_ref[:, cs].astype(F32) * pxa_ref[:, cs].astype(F32)
            cxp = jnp.where(first_tile, jnp.zeros_like(cxp), cxp)
            p6 = jnp.broadcast_to(cxp[hb - 2:hb - 1, :], (tm, CH))
            p7 = jnp.broadcast_to(cxp[hb - 1:hb, :], (tm, CH))
            c1 = jnp.where(rows == 0, p7, pltpu.roll(cx, 1, 0))
            c2 = jnp.where(rows == 0, p6, jnp.where(rows == 1, p7, pltpu.roll(cx, 2, 0)))
            w0, w1, w2 = cw_ref[0:1, cs], cw_ref[1:2, cs], cw_ref[2:3, cs]
            cv = w0 * c2 + w1 * c1 + w2 * cx
            sg = _sigmoid(za)
            sa = za * sg
            dcv = da * gb * sa
            dp_ref[:, pl.ds(0 * SLAB + CH * s, CH)] = (da * cv * sa).astype(BF16)
            dp_ref[:, pl.ds(3 * SLAB + CH * s, CH)] = (da * gb * cv * (sg * (1.0 + za * (1.0 - sg)))).astype(BF16)
            n0 = jnp.broadcast_to(next_ref[0:1, cs], (tm, CH))
            n1 = jnp.broadcast_to(next_ref[1:2, cs], (tm, CH))
            u1 = jnp.where(rows == tm - 1, n0, pltpu.roll(dcv, tm - 1, 0))
            u2 = jnp.where(rows == tm - 2, n0, jnp.where(rows == tm - 1, n1, pltpu.roll(dcv, tm - 2, 0)))
            next_ref[:, cs] = dcv[0:8, :]
            dcx = w2 * dcv + w1 * u1 + w0 * u2
            dp_ref[:, pl.ds(1 * SLAB + CH * s, CH)] = (dcx * xa).astype(BF16)
            dp_ref[:, pl.ds(2 * SLAB + CH * s, CH)] = (dcx * gc).astype(BF16)
            dcw_ref[0:1, cs] += jnp.sum(dcv * c2, axis=0, keepdims=True)
            dcw_ref[1:2, cs] += jnp.sum(dcv * c1, axis=0, keepdims=True)
            dcw_ref[2:3, cs] += jnp.sum(dcv * cx, axis=0, keepdims=True)

            zb = slab(6)
            db = dm_ref[:, pl.ds(SLAB + CH * s, CH)].astype(F32)
            ug = sav_ref[:, pl.ds(0 * SLAB + CH * s, CH)].astype(F32)
            ugrad = sav_ref[:, pl.ds(1 * SLAB + CH * s, CH)].astype(F32)
            vgrad_r = sav_ref[:, pl.ds(2 * SLAB + CH * s, CH)].astype(F32)
            vhat = sav_ref[:, pl.ds(3 * SLAB + CH * s, CH)].astype(F32)
            lg = lng_ref[:, cs]
            vn = (vhat * lg + lnb_ref[:, cs]).astype(BF16)
            sgb = _sigmoid(zb)
            szb = zb * sgb
            sps, dvns = [], []
            dbs = jnp.zeros((8, CH), F32)
            dwc = jnp.zeros((CH, CH), F32)
            for c in range(nch):
                rs = slice(CH * c, CH * (c + 1))
                sp = jnp.dot(wc_ref[s], vn[rs], preferred_element_type=F32) + bsb_ref[s]
                dsp = (db[rs] * ug[rs] * szb[rs]).astype(BF16)
                dbs = dbs + lax.dot_general(ones8, dsp, (((1,), (1,)), ((), ())), preferred_element_type=F32)
                dwc = dwc + lax.dot_general(dsp, vn[rs], (((1,), (1,)), ((), ())), preferred_element_type=F32)
                dvns.append(jnp.dot(wct_ref[s], dsp, preferred_element_type=F32))
                sps.append(sp)
            sp = jnp.concatenate(sps, axis=0)
            dvn = jnp.concatenate(dvns, axis=0)
            dbs_ref[:, cs] += dbs
            dwc_ref[s] += dwc
            dlng_ref[:, cs] += jnp.sum(dvn * vhat, axis=0, keepdims=True)
            dlnb_ref[:, cs] += jnp.sum(dvn, axis=0, keepdims=True)
            dvhat = dvn * lg
            dvg_r = (dvhat - jnp.mean(dvhat, axis=-1, keepdims=True)
                     - vhat * jnp.mean(dvhat * vhat, axis=-1, keepdims=True))
            dp_ref[:, pl.ds(4 * SLAB + CH * s, CH)] = (db * sp * szb * ugrad).astype(BF16)
            dp_ref[:, pl.ds(5 * SLAB + CH * s, CH)] = (dvg_r * vgrad_r).astype(BF16)
            dp_ref[:, pl.ds(6 * SLAB + CH * s, CH)] = (db * ug * sp * (sgb * (1.0 + zb * (1.0 - sgb)))).astype(BF16)

            if s % 2 == 1:
                part = None
                for k in range(N_SLAB):
                    col = k * SLAB + pair * (s // 2)
                    blk, off = divmod(col, IN_BLK)
                    term = lax.dot_general(dp_ref[:, pl.ds(col, pair)], w_ref[blk, :, pl.ds(off, pair)],
                                           (((1,), (1,)), ((), ())), preferred_element_type=F32)
                    part = term if part is None else part + term
                if s == 1:
                    dh_ref[...] = part
                else:
                    dh_ref[...] += part

        xv = x_ref[...]
        r = lax.rsqrt(jnp.mean(xv * xv, axis=-1, keepdims=True) + EPS)
        dxn, dg = _rms_bwd(dh_ref[...], xv, r, g1_ref[...])
        gx_ref[...] = dx1_ref[...].astype(F32) + dxn
        dg1_ref[0:1, :] += dg

        @pl.when(i == nt - 1)
        def _():
            tril = lax.broadcasted_iota(jnp.int32, (CH, CH), 0) >= lax.broadcasted_iota(jnp.int32, (CH, CH), 1)
            for s in range(HEADS):
                dwc_ref[s] = jnp.where(tril, dwc_ref[s], 0.0)

    rev = lambda i: nt - 1 - i
    halo = lambda col: pl.BlockSpec((hb, SLAB), lambda i: (jnp.maximum(rev(i) * (tm // hb) - 1, 0), col))
    tok = lambda w: pl.BlockSpec((tm, w), lambda i: (rev(i), 0))
    return pl.pallas_call(
        body, grid=(nt,),
        in_specs=[tok(IN_DIM), halo(1), halo(2), tok(4 * SLAB), tok(MIX), _full((8, D)), _full((1, D)), _full((1, D)),
                  _full((HEADS, CH, CH)), _full((HEADS, CH, CH)), _full((HEADS, CH, CH)),
                  _full((N_CHIP, D, IN_BLK), 1), tok(D), tok(D), _full((1, D))],
        out_specs=[tok(IN_DIM), _full((8, D)), _full((1, D)), _full((1, D)), _full((HEADS, CH, CH)), _full((8, D)),
                   tok(D), _full((8, D))],
        out_shape=[jax.ShapeDtypeStruct((t, IN_DIM), BF16), jax.ShapeDtypeStruct((8, D), F32),
                   jax.ShapeDtypeStruct((1, D), F32), jax.ShapeDtypeStruct((1, D), F32),
                   jax.ShapeDtypeStruct((HEADS, CH, CH), F32), jax.ShapeDtypeStruct((8, D), F32),
                   jax.ShapeDtypeStruct((t, D), F32), jax.ShapeDtypeStruct((8, D), F32)],
        scratch_shapes=[pltpu.VMEM((8, D), F32), pltpu.VMEM((tm, D), F32)],
        compiler_params=_cp(("arbitrary",), VMEM_LIMIT), name="mixer_bwd")(
            proj, proj, proj, sav, dmix, cw8, lng, lnb, wc, wct, bsb, win_f, x, dx1, g1)


def _grad_matmul(at, b, after, *, by_cols, name, tk=1024):
    m, t = at.shape
    n = b.shape[1]
    nk = t // tk
    nj = N_CHIP if by_cols else 1
    bn = n // nj

    def body(a_ref, b_ref, after_ref, o_ref, ob_ref):
        kk = pl.program_id(1)
        part = jnp.dot(a_ref[...], b_ref[...], preferred_element_type=F32)

        @pl.when(kk == 0)
        def _():
            o_ref[...] = part

        @pl.when(kk > 0)
        def _():
            o_ref[...] += part

        @pl.when(kk == nk - 1)
        def _():
            ob_ref[...] = o_ref[...].astype(BF16)

    a_spec = pl.BlockSpec((m, tk), lambda j, k: (0, k))
    b_spec = pl.BlockSpec((tk, bn), lambda j, k: (k, j))
    o_spec = pl.BlockSpec((None, m, bn), lambda j, k: (j, 0, 0))
    o32, o16 = pl.pallas_call(
        body, grid=(nj, nk), in_specs=[a_spec, b_spec, ANY], out_specs=[o_spec, o_spec],
        out_shape=[jax.ShapeDtypeStruct((nj, m, bn), F32), jax.ShapeDtypeStruct((nj, m, bn), BF16)],
        compiler_params=_cp(("parallel", "arbitrary"), VMEM_LIMIT), name=name)(at, b, after)
    if by_cols:
        return o32, o16
    return o32.reshape(N_CHIP, m // N_CHIP, n), o16.reshape(N_CHIP, m // N_CHIP, n)


def _coords():
    x, y, c = lax.axis_index("x"), lax.axis_index("y"), lax.axis_index("c")
    chips = [(1 - x, y), (x, 1 - y), (1 - x, 1 - y)]
    return x, y, c, chips


def _pair_exchange(grads_b, smalls, name):
    ng, ns = len(grads_b), len(smalls)
    n = ng + ns

    def body(*refs):
        ins, outs, send, recv = refs[:n], refs[n:2 * n], refs[2 * n], refs[2 * n + 1]
        x, y, c, _ = _coords()
        cps = []
        for i in range(n):
            if i < ng:
                hr = ins[i].shape[1] // 2
                src = ins[i].at[pl.ds(0, N_CHIP), pl.ds((1 - c) * hr, hr)]
            else:
                hr = ins[i].shape[0] // 2
                src = ins[i].at[pl.ds((1 - c) * hr, hr)]
            cps.append(pltpu.make_async_remote_copy(
                src_ref=src, dst_ref=outs[i], send_sem=send.at[i], recv_sem=recv.at[i],
                device_id=(x, y, 1 - c), device_id_type=MESH))
        for cp in cps:
            cp.start()
        for cp in cps:
            cp.wait()

    out_shape = [jax.ShapeDtypeStruct((N_CHIP, g.shape[1] // 2, g.shape[2]), g.dtype) for g in grads_b]
    out_shape += [jax.ShapeDtypeStruct((s.shape[0] // 2, s.shape[1]), s.dtype) for s in smalls]
    return pl.pallas_call(
        body, out_shape=out_shape, in_specs=[ANY] * n, out_specs=[ANY] * n,
        scratch_shapes=[pltpu.SemaphoreType.DMA((n,)), pltpu.SemaphoreType.DMA((n,))],
        name=name)(*grads_b, *smalls)


def _pair_sum(c_idx, grads, recvd, smalls, smalls_recvd, name):
    ng, ns = len(grads), len(smalls)
    halves = [g.shape[1] // 2 for g in grads]

    def body(c_ref, *refs):
        g_in, r_in = refs[:ng], refs[ng:2 * ng]
        s_in, sr_in = refs[2 * ng:2 * ng + ns], refs[2 * ng + ns:2 * ng + 2 * ns]
        o = refs[2 * ng + 2 * ns:]
        for i in range(ng):
            tot = g_in[i][...] + r_in[i][...].astype(F32)
            o[i][...] = tot
            o[ng + i][...] = tot.astype(BF16)
        for i in range(ns):
            o[2 * ng + i][...] = s_in[i][...] + sr_in[i][...]

    in_specs = [pl.BlockSpec((None, None, halves[i], g.shape[2]), lambda b, c: (b, c[0], 0, 0)) for i, g in enumerate(grads)]
    in_specs += [pl.BlockSpec((None, halves[i], g.shape[2]), lambda b, c: (b, 0, 0)) for i, g in enumerate(grads)]
    in_specs += [pl.BlockSpec((None, s.shape[0] // 2, s.shape[1]), lambda b, c: (c[0], 0, 0)) for s in smalls]
    in_specs += [pl.BlockSpec((s.shape[0] // 2, s.shape[1]), lambda b, c: (0, 0)) for s in smalls]
    blk = [pl.BlockSpec((None, halves[i], g.shape[2]), lambda b, c: (b, 0, 0)) for i, g in enumerate(grads)]
    out_specs = blk + blk + [pl.BlockSpec((s.shape[0] // 2, s.shape[1]), lambda b, c: (0, 0)) for s in smalls]
    out_shape = [jax.ShapeDtypeStruct((N_CHIP, halves[i], g.shape[2]), F32) for i, g in enumerate(grads)]
    out_shape += [jax.ShapeDtypeStruct((N_CHIP, halves[i], g.shape[2]), BF16) for i, g in enumerate(grads)]
    out_shape += [jax.ShapeDtypeStruct((s.shape[0] // 2, s.shape[1]), F32) for s in smalls]
    grads4 = [g.reshape(N_CHIP, 2, halves[i], g.shape[2]) for i, g in enumerate(grads)]
    smalls3 = [s.reshape(2, s.shape[0] // 2, s.shape[1]) for s in smalls]
    return pl.pallas_call(
        body, out_shape=out_shape,
        grid_spec=pltpu.PrefetchScalarGridSpec(num_scalar_prefetch=1, grid=(N_CHIP,), in_specs=in_specs, out_specs=out_specs),
        compiler_params=_cp(("arbitrary",), VMEM_LIMIT), name=name)(c_idx, *grads4, *recvd, *smalls3, *smalls_recvd)


_HBM = pl.BlockSpec(memory_space=pltpu.HBM)
_SEM = pl.BlockSpec(memory_space=pltpu.SEMAPHORE)


def _split_copies(ins, lands, ng, send, recv, arriving):
    x, y, c, chips = _coords()
    b = 2 * x + y
    copies = []
    for i in range(len(ins)):
        for k in range(3):
            blk = 2 * chips[k][0] + chips[k][1]
            src, dst, got = (ins[i].at[blk], lands[i].at[k], lands[i].at[k]) if i < ng else (ins[i], lands[i].at[b], lands[i].at[blk])
            sems = dict(send_sem=send.at[3 * i + k], recv_sem=recv.at[3 * i + k], device_id=(*chips[k], c), device_id_type=MESH)
            if arriving:
                copies.append(pltpu.make_async_remote_copy(src_ref=got, dst_ref=got, **sems))
            else:
                copies.append(pltpu.make_async_remote_copy(src_ref=src, dst_ref=dst, **sems))
    return copies


def _exchange_begin(sums_b, smalls, name):
    ng, n = len(sums_b), len(sums_b) + len(smalls)
    srcs = list(sums_b) + list(smalls)
    lands = [lax.empty((3,) + g.shape[1:], g.dtype) for g in sums_b] + [lax.empty((N_CHIP,) + s.shape, s.dtype) for s in smalls]

    def body(*refs):
        ins, land_refs = refs[:n], refs[n:2 * n]
        send, recv = refs[2 * n], refs[2 * n + 1]
        token = refs[4 * n + 2]
        for cp in _split_copies(ins, land_refs, ng, send, recv, False):
            cp.start()
        token[...] = jnp.zeros_like(token)

    hbm = lambda a: pltpu.HBM(a.shape, a.dtype)
    outs = pl.pallas_call(
        body, name=name,
        out_shape=(pltpu.SemaphoreType.DMA((3 * n,)), pltpu.SemaphoreType.DMA((3 * n,)), *[hbm(a) for a in srcs + lands],
                   jax.ShapeDtypeStruct((8, 128), F32)),
        in_specs=[_HBM] * (2 * n), out_specs=(_SEM, _SEM, *[_HBM] * (2 * n), pl.BlockSpec(memory_space=pltpu.VMEM)),
        input_output_aliases={i: 2 + i for i in range(2 * n)},
        compiler_params=pltpu.CompilerParams(has_side_effects=pltpu.SideEffectType.DATAFLOW_SIDE_EFFECTING),
    )(*[pltpu.with_memory_space_constraint(a, pltpu.HBM) for a in srcs + lands])
    return outs[0], outs[1], list(outs[2:2 + n]), list(outs[2 + n:2 + 2 * n]), outs[2 + 2 * n]


def _exchange_end(send, recv, srcs, lands, ng, after, name):
    n = len(srcs)
    after = list(after)

    def body(*refs):
        ins, land_refs = refs[:n], refs[n:2 * n]
        send_ref, recv_ref = refs[2 * n], refs[2 * n + 1]
        for cp in _split_copies(ins, land_refs, ng, send_ref, recv_ref, False):
            cp.wait_send()
        for cp in _split_copies(ins, land_refs, ng, send_ref, recv_ref, True):
            cp.wait_recv()

    hbm = lambda a: pltpu.HBM(a.shape, a.dtype)
    outs = pl.pallas_call(
        body, name=name, out_shape=tuple(hbm(a) for a in list(srcs) + list(lands)),
        in_specs=[_HBM] * (2 * n) + [_SEM, _SEM] + [ANY] * len(after), out_specs=tuple([_HBM] * (2 * n)),
        input_output_aliases={i: i for i in range(2 * n)},
        compiler_params=pltpu.CompilerParams(has_side_effects=pltpu.SideEffectType.DATAFLOW_SIDE_EFFECTING),
    )(*srcs, *lands, send, recv, *after)
    return list(outs[n:])


def _chip_sum(bc_idx, sums, recvd, smalls_slots, smalls_own, name, steps=4):
    ng, ns = len(sums), len(smalls_slots)

    def body(bc_ref, *refs):
        own, rx = refs[:ng], refs[ng:2 * ng]
        sl = refs[2 * ng:2 * ng + ns]
        sl_own = refs[2 * ng + ns:2 * ng + 2 * ns]
        o = refs[2 * ng + 2 * ns:]
        for i in range(ng):
            tot = own[i][...]
            for j in range(3):
                tot = tot + rx[i][j].astype(F32)
            o[i][...] = tot
        for i in range(ns):
            term = [jnp.where(bc_ref[0] == kk, sl_own[i][...], sl[i][kk]) for kk in range(N_CHIP)]
            o[ng + i][...] = ((term[0] + term[1]) + term[2]) + term[3]

    def rows(g):
        return g.shape[1] // steps

    in_specs = [pl.BlockSpec((None, rows(g), g.shape[2]), lambda r, bc: (bc[0], r, 0)) for g in sums]
    in_specs += [pl.BlockSpec((3, rows(g), g.shape[2]), lambda r, bc: (0, r, 0)) for g in sums]
    in_specs += [pl.BlockSpec(s.shape, lambda r, bc: (0, 0, 0)) for s in smalls_slots]
    in_specs += [pl.BlockSpec(s.shape[1:], lambda r, bc: (0, 0)) for s in smalls_slots]
    out_specs = [pl.BlockSpec((rows(g), g.shape[2]), lambda r, bc: (bc[1] * steps + r, 0)) for g in sums]
    out_specs += [pl.BlockSpec(s.shape[1:], lambda r, bc: (bc[1], 0)) for s in smalls_slots]
    out_shape = [jax.ShapeDtypeStruct((2 * g.shape[1], g.shape[2]), F32) for g in sums]
    out_shape += [jax.ShapeDtypeStruct((2 * s.shape[1], s.shape[2]), F32) for s in smalls_slots]
    return pl.pallas_call(
        body, out_shape=out_shape,
        grid_spec=pltpu.PrefetchScalarGridSpec(num_scalar_prefetch=1, grid=(steps,), in_specs=in_specs, out_specs=out_specs),
        compiler_params=_cp(("arbitrary",), VMEM_LIMIT), name=name)(bc_idx, *sums, *recvd, *smalls_slots, *smalls_own)


def _pair_gather(arrs, name):
    n = len(arrs)

    def body(*refs):
        outs = refs[n:2 * n]
        send, recv = refs[2 * n:]
        x, y, c, _ = _coords()
        cps = []
        for i in range(n):
            hr = outs[i].shape[0] // 2
            mine = outs[i].at[pl.ds(c * hr, hr)]
            cps.append(pltpu.make_async_remote_copy(
                src_ref=mine, dst_ref=mine, send_sem=send.at[i], recv_sem=recv.at[i],
                device_id=(x, y, 1 - c), device_id_type=MESH))
        for cp in cps:
            cp.start()
        for i in range(n):
            hr = outs[i].shape[0] // 2
            theirs = outs[i].at[pl.ds((1 - c) * hr, hr)]
            pltpu.make_async_remote_copy(
                src_ref=theirs, dst_ref=theirs, send_sem=send.at[i], recv_sem=recv.at[i],
                device_id=(x, y, 1 - c), device_id_type=MESH).wait_recv()
        for cp in cps:
            cp.wait_send()

    return list(pl.pallas_call(
        body, out_shape=[jax.ShapeDtypeStruct(a.shape, a.dtype) for a in arrs], in_specs=[ANY] * n, out_specs=[ANY] * n,
        input_output_aliases={i: i for i in range(n)},
        scratch_shapes=[pltpu.SemaphoreType.DMA((n,)), pltpu.SemaphoreType.DMA((n,))],
        name=name)(*arrs))


def _adamw_math(w, g, m, v):
    m2 = ADAM_B1 * m + (1.0 - ADAM_B1) * g
    v2 = ADAM_B2 * v + (1.0 - ADAM_B2) * (g * g)
    m_hat = m2 / (1.0 - ADAM_B1 ** ADAM_STEP)
    v_hat = v2 / (1.0 - ADAM_B2 ** ADAM_STEP)
    delta = -ADAM_LR * (m_hat / (jnp.sqrt(v_hat) + ADAM_EPS) + ADAM_WD * w)
    return delta, m2, v2


def _adamw_big(w, g, m, v, name, steps=8):
    r, c = w.shape

    def body(w_ref, g_ref, m_ref, v_ref, d_ref, m2_ref, v2_ref, g2_ref):
        gv = g_ref[...]
        d_ref[...], m2_ref[...], v2_ref[...] = _adamw_math(w_ref[...], gv, m_ref[...], v_ref[...])
        g2_ref[...] = gv

    spec = pl.BlockSpec((r // steps, c), lambda i: (i, 0))
    return pl.pallas_call(
        body, grid=(steps,), in_specs=[spec] * 4, out_specs=[spec] * 4,
        out_shape=[jax.ShapeDtypeStruct((r, c), F32)] * 4,
        compiler_params=_cp(("parallel",), VMEM_LIMIT), name=name)(w, g, m, v)


def _adamw_small(groups):
    n = len(groups)

    def body(*refs):
        for i in range(n):
            w_ref, g_ref, m_ref, v_ref = refs[4 * i:4 * i + 4]
            d_ref, m2_ref, v2_ref = refs[4 * n + 3 * i:4 * n + 3 * i + 3]
            d_ref[...], m2_ref[...], v2_ref[...] = _adamw_math(w_ref[...], g_ref[...], m_ref[...], v_ref[...])

    flat = [a for grp in groups for a in grp]
    out_shape = [jax.ShapeDtypeStruct(grp[0].shape, F32) for grp in groups for _ in range(3)]
    outs = pl.pallas_call(body, out_shape=out_shape, name="adamw_small")(*flat)
    return [tuple(outs[3 * i:3 * i + 3]) for i in range(n)]


def kernel(x, mem, norm_mix_g, w_in, conv_w, gm_ln_g, gm_ln_b, gm_ws, gm_bs, w_out, norm_x_g, norm_mem_g, w_q, w_kv, w_xo, norm_final_g, loss_target, m_norm_mix_g, m_w_in, m_conv_w, m_gm_ln_g, m_gm_ln_b, m_gm_ws, m_gm_bs, m_w_out, m_norm_x_g, m_norm_mem_g, m_w_q, m_w_kv, m_w_xo, m_norm_final_g, v_norm_mix_g, v_w_in, v_conv_w, v_gm_ln_g, v_gm_ln_b, v_gm_ws, v_gm_bs, v_w_out, v_norm_x_g, v_norm_mem_g, v_w_q, v_w_kv, v_w_xo, v_norm_final_g):
    t = x.shape[1]
    xi = lax.axis_index("x")
    yi = lax.axis_index("y")
    ci = lax.axis_index("c")
    b_idx = jnp.reshape(2 * xi + yi, (1,)).astype(jnp.int32)
    c_idx = jnp.reshape(ci, (1,)).astype(jnp.int32)

    x2d, mem2d, tgt = x[0], mem[0], loss_target[0]
    big = [w_in[0], w_out[0], w_q[0], w_kv[0], w_xo[0]]
    big_m = [m_w_in[0], m_w_out[0], m_w_q[0], m_w_kv[0], m_w_xo[0]]
    big_v = [v_w_in[0], v_w_out[0], v_w_q[0], v_w_kv[0], v_w_xo[0]]
    g3 = norm_final_g.reshape(1, D)

    def pad8(a):
        return jnp.pad(a, ((0, 8 - a.shape[0]), (0, 0)))

    own_blocks = _cast_shards(b_idx, big)

    tril = jnp.tril(jnp.ones((CH, CH), bool))
    wc32 = jnp.where(tril[None], gm_ws[0], 0.0)
    wc = wc32.astype(BF16)
    wct = jnp.swapaxes(wc32, 1, 2).astype(BF16)
    bsb = jnp.broadcast_to(gm_bs[0][:, :, None], (HEADS, CH, CH))

    blk = 2 * xi + yi
    order = jnp.stack([blk, blk ^ 2, blk ^ 1, blk ^ 3]).astype(jnp.int32)
    proj, ht, win_f, cw8, (wout_f,) = _proj_gather(
        order, x2d, norm_mix_g, own_blocks[0], pad8(conv_w[0]), [own_blocks[1]])
    mixin, mixt, sav, (wq_f, wkv_f, wxo_f) = _mixer_fwd(proj, cw8, gm_ln_g, gm_ln_b, wc, bsb, own_blocks[2:])
    wout2, wq2, wxo2 = wout_f.reshape(MIX, D), wq_f.reshape(D, D), wxo_f.reshape(D, D)
    k, v, mt = _mem_fwd(mem2d, norm_mem_g, wkv_f)
    del mt

    (loss_tile, dmix, dx1b, h2t, dq, ot, dx2b, dk, dv, dg2, dg3) = _tail(
        x2d, tgt, mixin, wout2, wq2, wxo2, k, v, norm_x_g, g3)
    dwkv, dwkv_b, dgm = _mem_bwd(mem2d, norm_mem_g, dk, dv, wkv_f)
    dproj, dcw, dlng, dlnb, dwc, dbs8, grad_x, dg1 = _mixer_bwd(
        proj, sav, dmix, cw8, gm_ln_g, gm_ln_b, wc, wct, bsb, win_f, x2d, dx1b, norm_mix_g)

    bc_idx = jnp.concatenate([b_idx, c_idx])
    dwin, dwin_b = _grad_matmul(ht, dproj, dgm, by_cols=True, name="grad_w_in", tk=2048)
    zero = jnp.zeros((1, D), F32)
    loss_row = jnp.broadcast_to(loss_tile[0:1, 0:1], (1, D))
    sv = jnp.concatenate([dg1[0:1], dg2, dgm, dg3, dlng, dlnb, dbs8[0:1], loss_row, dcw], axis=0)
    sw = dwc.reshape(HEADS * CH, CH)
    rx1b = _pair_exchange([dwin_b], [sv, sw], "pair_exchange_b")
    ps_b = _pair_sum(c_idx, [dwin], rx1b[:1], [sv, sw], rx1b[1:], "pair_sum_b")
    sums_b, sums_b_b, psmall = list(ps_b[:1]), list(ps_b[1:2]), list(ps_b[2:])
    send_b, recv_b, src_b, land_b, token_b = _exchange_begin(sums_b_b, psmall, "exchange_b_begin")

    dwxo, dwxo_b = _grad_matmul(ot, dx2b, token_b, by_cols=False, name="grad_w_xo", tk=2048)
    dwq, dwq_b = _grad_matmul(h2t, dq, token_b, by_cols=False, name="grad_w_q", tk=2048)
    dwout, dwout_b = _grad_matmul(mixt, dx1b, token_b, by_cols=False, name="grad_w_out")
    rx1a = _pair_exchange([dwout_b, dwq_b, dwkv_b, dwxo_b], [], "pair_exchange_a")
    ps_a = _pair_sum(c_idx, [dwout, dwq, dwkv, dwxo], rx1a, [], [], "pair_sum_a")
    sums_a, sums_a_b = list(ps_a[:4]), list(ps_a[4:8])
    send_a, recv_a, src_a, land_a, token_a = _exchange_begin(sums_a_b, [], "exchange_a_begin")

    rx2b = _exchange_end(send_b, recv_b, src_b, land_b, 1, [token_a], "exchange_b_end")
    red_b = _chip_sum(bc_idx, sums_b, rx2b[:1], rx2b[1:], psmall, "chip_sum_b")
    gwin, svf, swf = _pair_gather(red_b, "pair_gather_b")
    out_b = _adamw_big(big[0], gwin, big_m[0], big_v[0], "adamw_w_in")

    def vec_pack(a1, a2, am, a3, lg, lb, bs):
        return jnp.concatenate([a1, a2, am, a3.reshape(1, D), lg, lb, bs.reshape(1, D), zero], axis=0)

    wv = vec_pack(norm_mix_g, norm_x_g, norm_mem_g, norm_final_g, gm_ln_g, gm_ln_b, gm_bs)
    mv = vec_pack(m_norm_mix_g, m_norm_x_g, m_norm_mem_g, m_norm_final_g, m_gm_ln_g, m_gm_ln_b, m_gm_bs)
    vv = vec_pack(v_norm_mix_g, v_norm_x_g, v_norm_mem_g, v_norm_final_g, v_gm_ln_g, v_gm_ln_b, v_gm_bs)
    loss = svf[7, 0]
    gcw = lax.dynamic_slice_in_dim(svf[8:16], blk * (D // N_CHIP), D // N_CHIP, axis=1)
    gws = swf
    gv = svf[0:8]
    (dv_, mv_, vv_), (dc_, mc_, vc_), (dws_, mws_, vws_) = _adamw_small([
        (wv, gv, mv, vv),
        (pad8(conv_w[0]), gcw, pad8(m_conv_w[0]), pad8(v_conv_w[0])),
        (gm_ws.reshape(HEADS * CH, CH), gws, m_gm_ws.reshape(HEADS * CH, CH), v_gm_ws.reshape(HEADS * CH, CH))])

    rx2a = _exchange_end(send_a, recv_a, src_a, land_a, 4, [out_b[0], dv_], "exchange_a_end")
    red_a = _chip_sum(bc_idx, sums_a, rx2a, [], [], "chip_sum_a")
    g_a = _pair_gather(red_a, "pair_gather_a")
    names = ["w_out", "w_q", "w_kv", "w_xo"]
    out_a = [_adamw_big(w, g, m, v, "adamw_" + nm) for w, g, m, v, nm in zip(big[1:], g_a, big_m[1:], big_v[1:], names)]
    big_out = [out_b] + out_a

    def unpack(vecs, cw, ws, bigs):
        r = lambda i: vecs[i:i + 1]
        return [r(0), bigs[0][None], cw[0:3][None], r(4), r(5), ws.reshape(1, HEADS, CH, CH), vecs[6].reshape(1, HEADS, CH),
                bigs[1][None], r(1), r(2), bigs[2][None], bigs[3][None], bigs[4][None], vecs[3]]

    grads_out = unpack(gv, gcw, gws, [o[3] for o in big_out])
    delta_out = unpack(dv_, dc_, dws_, [o[0] for o in big_out])
    m_out = unpack(mv_, mc_, mws_, [o[1] for o in big_out])
    v_out = unpack(vv_, vc_, vws_, [o[2] for o in big_out])
    return (loss, grad_x[None], *grads_out, *delta_out, *m_out, *v_out)
```

```python
import functools
import math

import jax
import jax.numpy as jnp
from jax import lax
from jax.experimental import pallas as pl
from jax.experimental.pallas import tpu as pltpu

F32 = jnp.float32
BF16 = jnp.bfloat16
MESH = pl.DeviceIdType.MESH

D = 1024
SLAB = 1024
N_SLAB = 7
IN_DIM = N_SLAB * SLAB
MIX = 2 * SLAB
HEADS = 8
CH = 128
XH = 4
XD = D // XH
EPS = 1e-6
GELU_C = math.sqrt(2.0 / math.pi)
GELU_A = 0.044715
N_CHIP = 4
IN_BLK = IN_DIM // N_CHIP
KV_BLK = 2 * D // N_CHIP

ADAM_LR, ADAM_B1, ADAM_B2, ADAM_EPS, ADAM_WD, ADAM_STEP = 0.001, 0.9, 0.999, 1e-08, 0.01, 10

VMEM_LIMIT = 60 * 1024 * 1024


def _cp(sem=None, vmem=None):
    return pltpu.CompilerParams(dimension_semantics=sem, vmem_limit_bytes=vmem)


def _full(shape, buffers=None):
    n = len(shape)
    if buffers is None:
        return pl.BlockSpec(shape, lambda *_: (0,) * n)
    return pl.BlockSpec(shape, lambda *_: (0,) * n, pipeline_mode=pl.Buffered(buffers))


ANY = pl.BlockSpec(memory_space=pl.ANY)


def _bdot(a, b):
    return jnp.dot(a.astype(BF16), b.astype(BF16), preferred_element_type=F32)


def _bdot_nt(a, b):
    return lax.dot_general(a.astype(BF16), b.astype(BF16), (((1,), (1,)), ((), ())), preferred_element_type=F32)


def _rms(x, g):
    r = lax.rsqrt(jnp.mean(x * x, axis=-1, keepdims=True) + EPS)
    return x * r * g, r


def _rms_bwd(dy, x, r, g):
    gdy = dy * g
    dx = r * gdy - x * (r * r * r) * jnp.mean(x * gdy, axis=-1, keepdims=True)
    dg = jnp.sum(dy * x * r, axis=0, keepdims=True)
    return dx, dg


def _gelu_parts(x):
    x2 = x * x
    t = jnp.tanh(GELU_C * (x + GELU_A * x * x2))
    val = 0.5 * x * (1.0 + t)
    grad = 0.5 * (1.0 + t) + 0.5 * x * (1.0 - t * t) * (GELU_C * (1.0 + 3.0 * GELU_A * x2))
    return val, grad


def _gelu(x):
    return 0.5 * x * (1.0 + jnp.tanh(GELU_C * (x + GELU_A * x * x * x)))


def _sigmoid(z):
    return 1.0 / (1.0 + jnp.exp(-z))


def _cast_shards(b_idx, arrs):
    n = len(arrs)
    steps = 8

    def body(b_ref, *refs):
        for i in range(n):
            refs[n + i][...] = refs[i][...].astype(BF16)

    in_specs = [pl.BlockSpec((a.shape[0] // steps, a.shape[1]), lambda i, b: (i, 0)) for a in arrs]
    out_specs = [pl.BlockSpec((None, a.shape[0] // steps, a.shape[1]), lambda i, b: (b[0], i, 0)) for a in arrs]
    return pl.pallas_call(
        body, out_shape=[jax.ShapeDtypeStruct((N_CHIP,) + a.shape, BF16) for a in arrs],
        grid_spec=pltpu.PrefetchScalarGridSpec(num_scalar_prefetch=1, grid=(steps,), in_specs=in_specs, out_specs=out_specs),
        compiler_params=_cp(("arbitrary",)), name="cast_shards")(b_idx, *arrs)


def _proj_gather(order, x, g, win_own, cw8s, more, tm=1024):
    t = x.shape[0]
    ni = t // tm
    hr = D // 2
    nm = len(more)

    def body(*refs):
        order_ref, x_ref, g_ref, win_in, cw_in = refs[:5]
        o_ref, ht_ref, win_f, cw_out = refs[5 + nm:9 + nm]
        more_out = refs[9 + nm:9 + 2 * nm]
        hbuf, wv, cw_s, cw_r, loc = refs[9 + 2 * nm:14 + 2 * nm]
        g_in = _Gather([win_f], *refs[14 + 2 * nm:18 + 2 * nm])
        g_more = _Gather(more_out, *refs[18 + 2 * nm:22 + 2 * nm])
        j, i = pl.program_id(0), pl.program_id(1)
        x, y, c, chips = _coords()
        b = 2 * x + y
        blks = [2 * chip[0] + chip[1] for chip in chips]

        def cw_cols(blk):
            return cw_out.at[:, pl.ds(blk * (D // N_CHIP), D // N_CHIP)]

        def cw_copy(k, blk):
            src = cw_in if blk is None else cw_cols(blk)
            return pltpu.make_async_remote_copy(src_ref=src, dst_ref=cw_cols(b if blk is None else blk), send_sem=cw_s.at[k],
                                                recv_sem=cw_r.at[k], device_id=(*chips[k], c), device_id_type=MESH)

        cw_local = pltpu.make_async_copy(cw_in, cw_cols(b), loc.at[1])

        def load(blk):
            cp = pltpu.make_async_copy(win_f.at[blk], wv, loc.at[0])
            cp.start()
            cp.wait()

        @pl.when((j == 0) & (i == 0))
        def _():
            g_in.start()
            cw_local.start()
            for k in range(3):
                cw_copy(k, None).start()
            load(b)

        @pl.when((j == 1) & (i == 0))
        def _():
            g_in.hop()
            g_more.start()
            g_in.near_ready()
            load(g_in.bx)

        @pl.when((j == 2) & (i == 0))
        def _():
            load(g_in.by)

        @pl.when((j == 3) & (i == 0))
        def _():
            g_in.far()
            g_more.hop()
            g_in.far_ready()
            load(g_in.bd)

        rows = pl.ds(pl.multiple_of(i * tm, tm), tm)

        @pl.when(j == 0)
        def _():
            h, _ = _rms(x_ref[...], g_ref[...])
            hbuf[rows, :] = h.astype(BF16)
            ht_ref[...] = h.T.astype(BF16)

        @pl.when((j == N_CHIP - 1) & (i == ni - 1))
        def _():
            g_more.far()

        o_ref[...] = jnp.dot(hbuf[rows, :], wv[...], preferred_element_type=F32).astype(BF16)

        @pl.when((j == N_CHIP - 1) & (i == ni - 1))
        def _():
            for k in range(3):
                cw_copy(k, blks[k]).wait_recv()
            for k in range(3):
                cw_copy(k, None).wait_send()
            cw_local.wait()
            g_more.near_ready()
            g_more.far_ready()
            g_in.drain()
            g_more.drain()

    first = lambda j, i: jnp.where(j == 0, i, ni - 1)
    in_specs = [pl.BlockSpec((tm, D), lambda j, i, o: (first(j, i), 0)), pl.BlockSpec((1, D), lambda j, i, o: (0, 0)),
                ANY, ANY] + [ANY] * nm
    out_specs = [pl.BlockSpec((tm, IN_BLK), lambda j, i, o: (i, o[j])),
                 pl.BlockSpec((D, tm), lambda j, i, o: (0, first(j, i))), ANY, ANY] + [ANY] * nm
    outs = pl.pallas_call(
        body, out_shape=[jax.ShapeDtypeStruct((t, IN_DIM), BF16), jax.ShapeDtypeStruct((D, t), BF16),
                         jax.ShapeDtypeStruct(win_own.shape, BF16), jax.ShapeDtypeStruct((8, D), F32)]
        + [jax.ShapeDtypeStruct(f.shape, f.dtype) for f in more],
        grid_spec=pltpu.PrefetchScalarGridSpec(
            num_scalar_prefetch=1, grid=(N_CHIP, ni), in_specs=in_specs, out_specs=out_specs,
            scratch_shapes=[pltpu.VMEM((t, D), BF16), pltpu.VMEM((D, IN_BLK), BF16)]
            + [pltpu.SemaphoreType.DMA((3,))] * 2 + [pltpu.SemaphoreType.DMA((2,))] + _gather_sems(1) + _gather_sems(nm)),
        input_output_aliases={3: 2, **{5 + w: 4 + w for w in range(nm)}},
        compiler_params=_cp(("arbitrary", "arbitrary"), VMEM_LIMIT), name="proj_gather")(order, x, g, win_own, cw8s, *more)
    return outs[0], outs[1], outs[2], outs[3], outs[4:]


class _Gather:
    def __init__(self, outs, ici_s, ici_r, d2d_s, d2d_r):
        x, y, c, _ = _coords()
        self.outs, self.c = outs, c
        self.sems = ici_s, ici_r, d2d_s, d2d_r
        self.b, self.bx, self.by, self.bd = 2 * x + y, 2 * (1 - x) + y, 2 * x + (1 - y), 2 * (1 - x) + (1 - y)
        self.xn, self.yn, self.sib = (1 - x, y, c), (x, 1 - y, c), (x, y, 1 - c)

    def piece(self, w, blk, hc, quarter=None):
        hr = self.outs[w].shape[1] // 2
        if quarter is None:
            return self.outs[w].at[blk, pl.ds(hc * hr, hr)]
        return self.outs[w].at[blk, pl.ds(hc * hr + quarter * (hr // 2), hr // 2)]

    def ici(self, w, k, ref, to):
        return pltpu.make_async_remote_copy(src_ref=ref, dst_ref=ref, send_sem=self.sems[0].at[w, k],
                                            recv_sem=self.sems[1].at[w, k], device_id=to, device_id_type=MESH)

    def d2d(self, w, k, ref):
        return pltpu.make_async_remote_copy(src_ref=ref, dst_ref=ref, send_sem=self.sems[2].at[w, k],
                                            recv_sem=self.sems[3].at[w, k], device_id=self.sib, device_id_type=MESH)

    def start(self):
        for w in range(len(self.outs)):
            mine = self.piece(w, self.b, self.c)
            self.ici(w, 0, mine, self.xn).start()
            self.ici(w, 1, mine, self.yn).start()

    def hop(self):
        c = self.c
        for w in range(len(self.outs)):
            self.ici(w, 0, self.piece(w, self.bx, c), self.xn).wait_recv()
            self.ici(w, 1, self.piece(w, self.by, c), self.yn).wait_recv()
            self.ici(w, 2, self.piece(w, self.bx, c, 0), self.yn).start()
            self.ici(w, 3, self.piece(w, self.by, c, 1), self.xn).start()
            self.d2d(w, 0, self.piece(w, self.bx, c)).start()
            self.d2d(w, 1, self.piece(w, self.by, c)).start()

    def near_ready(self):
        for w in range(len(self.outs)):
            self.d2d(w, 0, self.piece(w, self.bx, 1 - self.c)).wait_recv()
            self.d2d(w, 1, self.piece(w, self.by, 1 - self.c)).wait_recv()

    def far(self):
        c = self.c
        for w in range(len(self.outs)):
            self.ici(w, 2, self.piece(w, self.bd, c, 0), self.yn).wait_recv()
            self.ici(w, 3, self.piece(w, self.bd, c, 1), self.xn).wait_recv()
            self.d2d(w, 2, self.piece(w, self.bd, c, 0)).start()
            self.d2d(w, 3, self.piece(w, self.bd, c, 1)).start()

    def far_ready(self):
        for w in range(len(self.outs)):
            self.d2d(w, 2, self.piece(w, self.bd, 1 - self.c, 0)).wait_recv()
            self.d2d(w, 3, self.piece(w, self.bd, 1 - self.c, 1)).wait_recv()

    def drain(self):
        c = self.c
        for w in range(len(self.outs)):
            mine = self.piece(w, self.b, c)
            self.ici(w, 0, mine, self.xn).wait_send()
            self.ici(w, 1, mine, self.yn).wait_send()
            self.ici(w, 2, self.piece(w, self.bx, c, 0), self.yn).wait_send()
            self.ici(w, 3, self.piece(w, self.by, c, 1), self.xn).wait_send()
            self.d2d(w, 0, self.piece(w, self.bx, c)).wait_send()
            self.d2d(w, 1, self.piece(w, self.by, c)).wait_send()
            self.d2d(w, 2, self.piece(w, self.bd, c, 0)).wait_send()
            self.d2d(w, 3, self.piece(w, self.bd, c, 1)).wait_send()


def _gather_sems(nw):
    return [pltpu.SemaphoreType.DMA((max(nw, 1), 4))] * 4


def _mixer_fwd(proj, cw8, lng, lnb, wc, bsb, fulls, tm=256):
    t = proj.shape[0]
    nt = t // tm
    nch = tm // CH
    nw = len(fulls)

    def body(*refs):
        p_ref, cw_ref, lng_ref, lnb_ref, wc_ref, bsb_ref = refs[:6]
        mix_ref, mixt_ref = refs[6 + nw:8 + nw]
        w_outs = refs[8 + nw:8 + 2 * nw]
        prev_ref, stage_ref = refs[8 + 2 * nw:10 + 2 * nw]
        gather = _Gather(w_outs, *refs[10 + 2 * nw:])

        @pl.when(pl.program_id(0) == 0)
        def _():
            gather.start()
            prev_ref[...] = jnp.zeros_like(prev_ref)

        @pl.when(pl.program_id(0) == nt // 2)
        def _():
            gather.hop()

        @pl.when(pl.program_id(0) == nt - 1)
        def _():
            gather.far()

        rows = lax.broadcasted_iota(jnp.int32, (tm, CH), 0)
        for s in range(HEADS):
            cs = pl.ds(CH * s, CH)

            def slab(k):
                return p_ref[:, pl.ds(k * SLAB + CH * s, CH)].astype(F32)

            gb, gc, xa, za = slab(0), slab(1), slab(2), slab(3)
            cx = gc * xa
            p6 = jnp.broadcast_to(prev_ref[6:7, cs], (tm, CH))
            p7 = jnp.broadcast_to(prev_ref[7:8, cs], (tm, CH))
            c1 = jnp.where(rows == 0, p7, pltpu.roll(cx, 1, 0))
            c2 = jnp.where(rows == 0, p6, jnp.where(rows == 1, p7, pltpu.roll(cx, 2, 0)))
            prev_ref[:, cs] = cx[tm - 8:, :]
            cv = cw_ref[0:1, cs] * c2 + cw_ref[1:2, cs] * c1 + cw_ref[2:3, cs] * cx
            stage_ref[:, cs] = gb * cv * (za * _sigmoid(za))

            u, v, zb = slab(4), slab(5), slab(6)
            ug, vg = _gelu(u), _gelu(v)
            dlt = vg - jnp.mean(vg, axis=-1, keepdims=True)
            vhat = dlt * lax.rsqrt(jnp.mean(dlt * dlt, axis=-1, keepdims=True) + EPS)
            vn = (vhat * lng_ref[:, cs] + lnb_ref[:, cs]).astype(BF16)
            gate = ug * (zb * _sigmoid(zb))
            for c in range(nch):
                rs = slice(CH * c, CH * (c + 1))
                sp = jnp.dot(wc_ref[s], vn[rs], preferred_element_type=F32) + bsb_ref[s]
                stage_ref[rs, pl.ds(SLAB + CH * s, CH)] = gate[rs] * sp

        full = stage_ref[...]
        mix_ref[...] = full.astype(BF16)
        mixt_ref[...] = full.T.astype(BF16)

        @pl.when(pl.program_id(0) == nt - 1)
        def _():
            gather.near_ready()
            gather.far_ready()
            gather.drain()

    sems = _gather_sems(nw)
    outs = pl.pallas_call(
        body, grid=(nt,),
        in_specs=[pl.BlockSpec((tm, IN_DIM), lambda i: (i, 0)), _full((8, D)), _full((1, D)), _full((1, D)),
                  _full((HEADS, CH, CH)), _full((HEADS, CH, CH))] + [ANY] * nw,
        out_specs=[pl.BlockSpec((tm, MIX), lambda i: (i, 0)), pl.BlockSpec((MIX, tm), lambda i: (0, i))] + [ANY] * nw,
        out_shape=[jax.ShapeDtypeStruct((t, MIX), BF16), jax.ShapeDtypeStruct((MIX, t), BF16)]
        + [jax.ShapeDtypeStruct(f.shape, f.dtype) for f in fulls],
        input_output_aliases={6 + w: 2 + w for w in range(nw)},
        scratch_shapes=[pltpu.VMEM((8, D), F32), pltpu.VMEM((tm, MIX), F32)] + sems,
        compiler_params=_cp(("arbitrary",), VMEM_LIMIT), name="mixer_fwd")(proj, cw8, lng, lnb, wc, bsb, *fulls)
    return outs[0], outs[1], outs[2:]


def _mem_fwd(mem, gm, wkv_f):
    n_mem = mem.shape[0]

    def body(mem_ref, gm_ref, w_ref, k_ref, v_ref, mt_ref):
        m, _ = _rms(mem_ref[...], gm_ref[...])
        mb = m.astype(BF16)
        mt_ref[...] = m.T.astype(BF16)
        for j in range(N_CHIP):
            dst = k_ref if j < 2 else v_ref
            dst[:, pl.ds(KV_BLK * (j % 2), KV_BLK)] = jnp.dot(mb, w_ref[j], preferred_element_type=F32).astype(BF16)

    return pl.pallas_call(
        body, out_shape=[jax.ShapeDtypeStruct((n_mem, D), BF16), jax.ShapeDtypeStruct((n_mem, D), BF16),
                         jax.ShapeDtypeStruct((D, n_mem), BF16)],
        compiler_params=_cp(None, VMEM_LIMIT), name="mem_fwd")(mem, gm, wkv_f)


def _tail(x, tgt, mixin, wout, wq, wxo, k, v, g2, g3, tm=512, sub=512):
    t = x.shape[0]
    n_mem = k.shape[0]
    scale = 1.0 / math.sqrt(XD)

    def body(x_ref, tgt_ref, mix_ref, wout_ref, wq_ref, wxo_ref, k_ref, v_ref, g2_ref, g3_ref,
             loss_ref, dmix_ref, dx1b_ref, h2t_ref, dq_ref, ot_ref, dx2b_ref, dk_ref, dv_ref, dg2_ref, dg3_ref):
        @pl.when(pl.program_id(0) == 0)
        def _():
            loss_ref[...] = jnp.zeros_like(loss_ref)
            dk_ref[...] = jnp.zeros_like(dk_ref)
            dv_ref[...] = jnp.zeros_like(dv_ref)
            dg2_ref[...] = jnp.zeros_like(dg2_ref)
            dg3_ref[...] = jnp.zeros_like(dg3_ref)

        g2, g3 = g2_ref[...], g3_ref[...]
        for sb in range(tm // sub):
            rs = pl.ds(sub * sb, sub)
            x1 = x_ref[rs, :] + jnp.dot(mix_ref[rs, :], wout_ref[...], preferred_element_type=F32)
            h2, r2 = _rms(x1, g2)
            h2t_ref[:, rs] = h2.T.astype(BF16)
            q = _bdot(h2, wq_ref[...]).astype(BF16)
            probs, outs = [], []
            for hd in range(XH):
                hs = pl.ds(XD * hd, XD)
                s = _bdot_nt(q[:, XD * hd:XD * (hd + 1)], k_ref[:, hs]) * scale
                e = jnp.exp(s - jnp.max(s, axis=-1, keepdims=True))
                p = e / jnp.sum(e, axis=-1, keepdims=True)
                probs.append(p)
                outs.append(_bdot(p, v_ref[:, hs]))
            o = jnp.concatenate(outs, axis=-1)
            ot_ref[:, rs] = o.T.astype(BF16)
            x2 = x1 + _bdot(o, wxo_ref[...])
            y, r3 = _rms(x2, g3)
            diff = y - tgt_ref[rs, :]
            row_loss = jnp.sum(diff * diff, axis=-1, keepdims=True)
            loss_ref[...] += jnp.broadcast_to(jnp.sum(row_loss, axis=0, keepdims=True) * (0.5 / D), loss_ref.shape)

            dx2, dg3 = _rms_bwd(diff * (1.0 / D), x2, r3, g3)
            dg3_ref[...] += dg3
            dx2b = dx2.astype(BF16)
            dx2b_ref[rs, :] = dx2b
            do = _bdot_nt(dx2b, wxo_ref[...])
            dqs = []
            for hd in range(XH):
                hs = pl.ds(XD * hd, XD)
                p = probs[hd]
                do_h = do[:, XD * hd:XD * (hd + 1)]
                dv_ref[:, hs] += _bdot(p.T, do_h)
                dp = _bdot_nt(do_h, v_ref[:, hs])
                ds = p * (dp - jnp.sum(dp * p, axis=-1, keepdims=True))
                dqs.append(_bdot(ds, k_ref[:, hs]) * scale)
                dk_ref[:, hs] += _bdot(ds.T, q[:, XD * hd:XD * (hd + 1)]) * scale
            dq = jnp.concatenate(dqs, axis=-1).astype(BF16)
            dq_ref[rs, :] = dq
            dx1n, dg2 = _rms_bwd(_bdot_nt(dq, wq_ref[...]), x1, r2, g2)
            dg2_ref[...] += dg2
            dx1b = (dx2 + dx1n).astype(BF16)
            dx1b_ref[rs, :] = dx1b
            dmix_ref[rs, :] = _bdot_nt(dx1b, wout_ref[...]).astype(BF16)

    tok = lambda w: pl.BlockSpec((tm, w), lambda i: (i, 0))
    tok_t = lambda w: pl.BlockSpec((w, tm), lambda i: (0, i))
    return pl.pallas_call(
        body, grid=(t // tm,),
        in_specs=[tok(D), tok(D), tok(MIX), _full((MIX, D), 1), _full((D, D), 1), _full((D, D), 1),
                  _full((n_mem, D), 1), _full((n_mem, D), 1), _full((1, D)), _full((1, D))],
        out_specs=[_full((8, 128)), tok(MIX), tok(D), tok_t(D), tok(D), tok_t(D), tok(D),
                   _full((n_mem, D)), _full((n_mem, D)), _full((1, D)), _full((1, D))],
        out_shape=[jax.ShapeDtypeStruct((8, 128), F32), jax.ShapeDtypeStruct((t, MIX), BF16),
                   jax.ShapeDtypeStruct((t, D), BF16),
                   jax.ShapeDtypeStruct((D, t), BF16), jax.ShapeDtypeStruct((t, D), BF16),
                   jax.ShapeDtypeStruct((D, t), BF16), jax.ShapeDtypeStruct((t, D), BF16),
                   jax.ShapeDtypeStruct((n_mem, D), F32), jax.ShapeDtypeStruct((n_mem, D), F32),
                   jax.ShapeDtypeStruct((1, D), F32), jax.ShapeDtypeStruct((1, D), F32)],
        compiler_params=_cp(("arbitrary",), VMEM_LIMIT), name="tail")(x, tgt, mixin, wout, wq, wxo, k, v, g2, g3)


def _mem_bwd(mem, gm, dk, dv, wkv_f):
    def body(mem_ref, gm_ref, dk_ref, dv_ref, w_ref, dw_ref, dwb_ref, dgm_ref):
        mem_v = mem_ref[...]
        m, rm = _rms(mem_v, gm_ref[...])
        mt = m.T.astype(BF16)
        dm = jnp.zeros_like(mem_v)
        for j in range(N_CHIP):
            src = dk_ref if j < 2 else dv_ref
            dkv = src[:, pl.ds(KV_BLK * (j % 2), KV_BLK)].astype(BF16)
            dw = jnp.dot(mt, dkv, preferred_element_type=F32)
            dw_ref[j] = dw
            dwb_ref[j] = dw.astype(BF16)
            dm = dm + _bdot_nt(dkv, w_ref[j])
        dgm_ref[...] = jnp.sum(dm * mem_v * rm, axis=0, keepdims=True)

    return pl.pallas_call(
        body, out_shape=[jax.ShapeDtypeStruct((N_CHIP, D, KV_BLK), F32), jax.ShapeDtypeStruct((N_CHIP, D, KV_BLK), BF16),
                         jax.ShapeDtypeStruct((1, D), F32)],
        compiler_params=_cp(None, VMEM_LIMIT), name="mem_bwd")(mem, gm, dk, dv, wkv_f)


def _mixer_bwd(proj, dmix, cw8, lng, lnb, wc, wct, bsb, win_f, x, dx1, g1, tm=256):
    t = proj.shape[0]
    nt = t // tm
    nch = tm // CH
    hb = 16
    pair = 2 * CH

    def body(p_ref, pgc_ref, pxa_ref, dm_ref, cw_ref, lng_ref, lnb_ref, wc_ref, wct_ref, bsb_ref, w_ref, x_ref,
             dx1_ref, g1_ref, dp_ref, dcw_ref, dlng_ref, dlnb_ref, dwc_ref, dbs_ref, gx_ref, dg1_ref,
             next_ref, dh_ref):
        i = pl.program_id(0)

        @pl.when(i == 0)
        def _():
            next_ref[...] = jnp.zeros_like(next_ref)
            dcw_ref[...] = jnp.zeros_like(dcw_ref)
            dlng_ref[...] = jnp.zeros_like(dlng_ref)
            dlnb_ref[...] = jnp.zeros_like(dlnb_ref)
            dwc_ref[...] = jnp.zeros_like(dwc_ref)
            dbs_ref[...] = jnp.zeros_like(dbs_ref)
            dg1_ref[...] = jnp.zeros_like(dg1_ref)

        first_tile = i == nt - 1
        rows = lax.broadcasted_iota(jnp.int32, (tm, CH), 0)
        ones8 = jnp.ones((8, CH), BF16)
        for s in range(HEADS):
            cs = pl.ds(CH * s, CH)

            def slab(k):
                return p_ref[:, pl.ds(k * SLAB + CH * s, CH)].astype(F32)

            gb, gc, xa, za = slab(0), slab(1), slab(2), slab(3)
            da = dm_ref[:, cs].astype(F32)
            cx = gc * xa
            cxp = pgc_ref[:, cs].astype(F32) * pxa_ref[:, cs].astype(F32)
            cxp = jnp.where(first_tile, jnp.zeros_like(cxp), cxp)
            p6 = jnp.broadcast_to(cxp[hb - 2:hb - 1, :], (tm, CH))
            p7 = jnp.broadcast_to(cxp[hb - 1:hb, :], (tm, CH))
            c1 = jnp.where(rows == 0, p7, pltpu.roll(cx, 1, 0))
            c2 = jnp.where(rows == 0, p6, jnp.where(rows == 1, p7, pltpu.roll(cx, 2, 0)))
            w0, w1, w2 = cw_ref[0:1, cs], cw_ref[1:2, cs], cw_ref[2:3, cs]
            cv = w0 * c2 + w1 * c1 + w2 * cx
            sg = _sigmoid(za)
            sa = za * sg
            dcv = da * gb * sa
            dp_ref[:, pl.ds(0 * SLAB + CH * s, CH)] = (da * cv * sa).astype(BF16)
            dp_ref[:, pl.ds(3 * SLAB + CH * s, CH)] = (da * gb * cv * (sg * (1.0 + za * (1.0 - sg)))).astype(BF16)
            n0 = jnp.broadcast_to(next_ref[0:1, cs], (tm, CH))
            n1 = jnp.broadcast_to(next_ref[1:2, cs], (tm, CH))
            u1 = jnp.where(rows == tm - 1, n0, pltpu.roll(dcv, tm - 1, 0))
            u2 = jnp.where(rows == tm - 2, n0, jnp.where(rows == tm - 1, n1, pltpu.roll(dcv, tm - 2, 0)))
            next_ref[:, cs] = dcv[0:8, :]
            dcx = w2 * dcv + w1 * u1 + w0 * u2
            dp_ref[:, pl.ds(1 * SLAB + CH * s, CH)] = (dcx * xa).astype(BF16)
            dp_ref[:, pl.ds(2 * SLAB + CH * s, CH)] = (dcx * gc).astype(BF16)
            dcw_ref[0:1, cs] += jnp.sum(dcv * c2, axis=0, keepdims=True)
            dcw_ref[1:2, cs] += jnp.sum(dcv * c1, axis=0, keepdims=True)
            dcw_ref[2:3, cs] += jnp.sum(dcv * cx, axis=0, keepdims=True)

            u, v, zb = slab(4), slab(5), slab(6)
            db = dm_ref[:, pl.ds(SLAB + CH * s, CH)].astype(F32)
            ug, ugrad = _gelu_parts(u)
            vg, vgrad = _gelu_parts(v)
            dlt = vg - jnp.mean(vg, axis=-1, keepdims=True)
            rstd = lax.rsqrt(jnp.mean(dlt * dlt, axis=-1, keepdims=True) + EPS)
            vhat = dlt * rstd
            lg = lng_ref[:, cs]
            vn = (vhat * lg + lnb_ref[:, cs]).astype(BF16)
            sgb = _sigmoid(zb)
            szb = zb * sgb
            sps, dvns = [], []
            dbs = jnp.zeros((8, CH), F32)
            dwc = jnp.zeros((CH, CH), F32)
            for c in range(nch):
                rs = slice(CH * c, CH * (c + 1))
                sp = jnp.dot(wc_ref[s], vn[rs], preferred_element_type=F32) + bsb_ref[s]
                dsp = (db[rs] * ug[rs] * szb[rs]).astype(BF16)
                dbs = dbs + lax.dot_general(ones8, dsp, (((1,), (1,)), ((), ())), preferred_element_type=F32)
                dwc = dwc + lax.dot_general(dsp, vn[rs], (((1,), (1,)), ((), ())), preferred_element_type=F32)
                dvns.append(jnp.dot(wct_ref[s], dsp, preferred_element_type=F32))
                sps.append(sp)
            sp = jnp.concatenate(sps, axis=0)
            dvn = jnp.concatenate(dvns, axis=0)
            dbs_ref[:, cs] += dbs
            dwc_ref[s] += dwc
            dlng_ref[:, cs] += jnp.sum(dvn * vhat, axis=0, keepdims=True)
            dlnb_ref[:, cs] += jnp.sum(dvn, axis=0, keepdims=True)
            dvhat = dvn * lg
            dvg = rstd * (dvhat - jnp.mean(dvhat, axis=-1, keepdims=True)
                          - vhat * jnp.mean(dvhat * vhat, axis=-1, keepdims=True))
            dp_ref[:, pl.ds(4 * SLAB + CH * s, CH)] = (db * sp * szb * ugrad).astype(BF16)
            dp_ref[:, pl.ds(5 * SLAB + CH * s, CH)] = (dvg * vgrad).astype(BF16)
            dp_ref[:, pl.ds(6 * SLAB + CH * s, CH)] = (db * ug * sp * (sgb * (1.0 + zb * (1.0 - sgb)))).astype(BF16)

            if s % 2 == 1:
                part = None
                for k in range(N_SLAB):
                    col = k * SLAB + pair * (s // 2)
                    blk, off = divmod(col, IN_BLK)
                    term = lax.dot_general(dp_ref[:, pl.ds(col, pair)], w_ref[blk, :, pl.ds(off, pair)],
                                           (((1,), (1,)), ((), ())), preferred_element_type=F32)
                    part = term if part is None else part + term
                if s == 1:
                    dh_ref[...] = part
                else:
                    dh_ref[...] += part

        xv = x_ref[...]
        r = lax.rsqrt(jnp.mean(xv * xv, axis=-1, keepdims=True) + EPS)
        dxn, dg = _rms_bwd(dh_ref[...], xv, r, g1_ref[...])
        gx_ref[...] = dx1_ref[...].astype(F32) + dxn
        dg1_ref[0:1, :] += dg

        @pl.when(i == nt - 1)
        def _():
            tril = lax.broadcasted_iota(jnp.int32, (CH, CH), 0) >= lax.broadcasted_iota(jnp.int32, (CH, CH), 1)
            for s in range(HEADS):
                dwc_ref[s] = jnp.where(tril, dwc_ref[s], 0.0)

    rev = lambda i: nt - 1 - i
    halo = lambda col: pl.BlockSpec((hb, SLAB), lambda i: (jnp.maximum(rev(i) * (tm // hb) - 1, 0), col))
    tok = lambda w: pl.BlockSpec((tm, w), lambda i: (rev(i), 0))
    return pl.pallas_call(
        body, grid=(nt,),
        in_specs=[tok(IN_DIM), halo(1), halo(2), tok(MIX), _full((8, D)), _full((1, D)), _full((1, D)),
                  _full((HEADS, CH, CH)), _full((HEADS, CH, CH)), _full((HEADS, CH, CH)),
                  _full((N_CHIP, D, IN_BLK), 1), tok(D), tok(D), _full((1, D))],
        out_specs=[tok(IN_DIM), _full((8, D)), _full((1, D)), _full((1, D)), _full((HEADS, CH, CH)), _full((8, D)),
                   tok(D), _full((8, D))],
        out_shape=[jax.ShapeDtypeStruct((t, IN_DIM), BF16), jax.ShapeDtypeStruct((8, D), F32),
                   jax.ShapeDtypeStruct((1, D), F32), jax.ShapeDtypeStruct((1, D), F32),
                   jax.ShapeDtypeStruct((HEADS, CH, CH), F32), jax.ShapeDtypeStruct((8, D), F32),
                   jax.ShapeDtypeStruct((t, D), F32), jax.ShapeDtypeStruct((8, D), F32)],
        scratch_shapes=[pltpu.VMEM((8, D), F32), pltpu.VMEM((tm, D), F32)],
        compiler_params=_cp(("arbitrary",), VMEM_LIMIT), name="mixer_bwd")(
            proj, proj, proj, dmix, cw8, lng, lnb, wc, wct, bsb, win_f, x, dx1, g1)


def _grad_matmul(at, b, after, *, by_cols, name, tk=1024):
    m, t = at.shape
    n = b.shape[1]
    nk = t // tk
    nj = N_CHIP if by_cols else 1
    bn = n // nj

    def body(a_ref, b_ref, after_ref, o_ref, ob_ref):
        kk = pl.program_id(1)
        part = jnp.dot(a_ref[...], b_ref[...], preferred_element_type=F32)

        @pl.when(kk == 0)
        def _():
            o_ref[...] = part

        @pl.when(kk > 0)
        def _():
            o_ref[...] += part

        @pl.when(kk == nk - 1)
        def _():
            ob_ref[...] = o_ref[...].astype(BF16)

    a_spec = pl.BlockSpec((m, tk), lambda j, k: (0, k))
    b_spec = pl.BlockSpec((tk, bn), lambda j, k: (k, j))
    o_spec = pl.BlockSpec((None, m, bn), lambda j, k: (j, 0, 0))
    o32, o16 = pl.pallas_call(
        body, grid=(nj, nk), in_specs=[a_spec, b_spec, ANY], out_specs=[o_spec, o_spec],
        out_shape=[jax.ShapeDtypeStruct((nj, m, bn), F32), jax.ShapeDtypeStruct((nj, m, bn), BF16)],
        compiler_params=_cp(("parallel", "arbitrary"), VMEM_LIMIT), name=name)(at, b, after)
    if by_cols:
        return o32, o16
    return o32.reshape(N_CHIP, m // N_CHIP, n), o16.reshape(N_CHIP, m // N_CHIP, n)


def _coords():
    x, y, c = lax.axis_index("x"), lax.axis_index("y"), lax.axis_index("c")
    chips = [(1 - x, y), (x, 1 - y), (1 - x, 1 - y)]
    return x, y, c, chips


def _pair_exchange(grads_b, smalls, name):
    ng, ns = len(grads_b), len(smalls)
    n = ng + ns

    def body(*refs):
        ins, outs, send, recv = refs[:n], refs[n:2 * n], refs[2 * n], refs[2 * n + 1]
        x, y, c, _ = _coords()
        cps = []
        for i in range(n):
            if i < ng:
                hr = ins[i].shape[1] // 2
                src = ins[i].at[pl.ds(0, N_CHIP), pl.ds((1 - c) * hr, hr)]
            else:
                hr = ins[i].shape[0] // 2
                src = ins[i].at[pl.ds((1 - c) * hr, hr)]
            cps.append(pltpu.make_async_remote_copy(
                src_ref=src, dst_ref=outs[i], send_sem=send.at[i], recv_sem=recv.at[i],
                device_id=(x, y, 1 - c), device_id_type=MESH))
        for cp in cps:
            cp.start()
        for cp in cps:
            cp.wait()

    out_shape = [jax.ShapeDtypeStruct((N_CHIP, g.shape[1] // 2, g.shape[2]), g.dtype) for g in grads_b]
    out_shape += [jax.ShapeDtypeStruct((s.shape[0] // 2, s.shape[1]), s.dtype) for s in smalls]
    return pl.pallas_call(
        body, out_shape=out_shape, in_specs=[ANY] * n, out_specs=[ANY] * n,
        scratch_shapes=[pltpu.SemaphoreType.DMA((n,)), pltpu.SemaphoreType.DMA((n,))],
        name=name)(*grads_b, *smalls)


def _pair_sum(c_idx, grads, recvd, smalls, smalls_recvd, name):
    ng, ns = len(grads), len(smalls)
    halves = [g.shape[1] // 2 for g in grads]

    def body(c_ref, *refs):
        g_in, r_in = refs[:ng], refs[ng:2 * ng]
        s_in, sr_in = refs[2 * ng:2 * ng + ns], refs[2 * ng + ns:2 * ng + 2 * ns]
        o = refs[2 * ng + 2 * ns:]
        for i in range(ng):
            tot = g_in[i][...] + r_in[i][...].astype(F32)
            o[i][...] = tot
            o[ng + i][...] = tot.astype(BF16)
        for i in range(ns):
            o[2 * ng + i][...] = s_in[i][...] + sr_in[i][...]

    in_specs = [pl.BlockSpec((None, None, halves[i], g.shape[2]), lambda b, c: (b, c[0], 0, 0)) for i, g in enumerate(grads)]
    in_specs += [pl.BlockSpec((None, halves[i], g.shape[2]), lambda b, c: (b, 0, 0)) for i, g in enumerate(grads)]
    in_specs += [pl.BlockSpec((None, s.shape[0] // 2, s.shape[1]), lambda b, c: (c[0], 0, 0)) for s in smalls]
    in_specs += [pl.BlockSpec((s.shape[0] // 2, s.shape[1]), lambda b, c: (0, 0)) for s in smalls]
    blk = [pl.BlockSpec((None, halves[i], g.shape[2]), lambda b, c: (b, 0, 0)) for i, g in enumerate(grads)]
    out_specs = blk + blk + [pl.BlockSpec((s.shape[0] // 2, s.shape[1]), lambda b, c: (0, 0)) for s in smalls]
    out_shape = [jax.ShapeDtypeStruct((N_CHIP, halves[i], g.shape[2]), F32) for i, g in enumerate(grads)]
    out_shape += [jax.ShapeDtypeStruct((N_CHIP, halves[i], g.shape[2]), BF16) for i, g in enumerate(grads)]
    out_shape += [jax.ShapeDtypeStruct((s.shape[0] // 2, s.shape[1]), F32) for s in smalls]
    grads4 = [g.reshape(N_CHIP, 2, halves[i], g.shape[2]) for i, g in enumerate(grads)]
    smalls3 = [s.reshape(2, s.shape[0] // 2, s.shape[1]) for s in smalls]
    return pl.pallas_call(
        body, out_shape=out_shape,
        grid_spec=pltpu.PrefetchScalarGridSpec(num_scalar_prefetch=1, grid=(N_CHIP,), in_specs=in_specs, out_specs=out_specs),
        compiler_params=_cp(("arbitrary",), VMEM_LIMIT), name=name)(c_idx, *grads4, *recvd, *smalls3, *smalls_recvd)


_HBM = pl.BlockSpec(memory_space=pltpu.HBM)
_SEM = pl.BlockSpec(memory_space=pltpu.SEMAPHORE)


def _split_copies(ins, lands, ng, send, recv, arriving):
    x, y, c, chips = _coords()
    b = 2 * x + y
    copies = []
    for i in range(len(ins)):
        for k in range(3):
            blk = 2 * chips[k][0] + chips[k][1]
            src, dst, got = (ins[i].at[blk], lands[i].at[k], lands[i].at[k]) if i < ng else (ins[i], lands[i].at[b], lands[i].at[blk])
            sems = dict(send_sem=send.at[3 * i + k], recv_sem=recv.at[3 * i + k], device_id=(*chips[k], c), device_id_type=MESH)
            if arriving:
                copies.append(pltpu.make_async_remote_copy(src_ref=got, dst_ref=got, **sems))
            else:
                copies.append(pltpu.make_async_remote_copy(src_ref=src, dst_ref=dst, **sems))
    return copies


def _exchange_begin(sums_b, smalls, name):
    ng, n = len(sums_b), len(sums_b) + len(smalls)
    srcs = list(sums_b) + list(smalls)
    lands = [lax.empty((3,) + g.shape[1:], g.dtype) for g in sums_b] + [lax.empty((N_CHIP,) + s.shape, s.dtype) for s in smalls]

    def body(*refs):
        ins, land_refs = refs[:n], refs[n:2 * n]
        send, recv = refs[2 * n], refs[2 * n + 1]
        token = refs[4 * n + 2]
        for cp in _split_copies(ins, land_refs, ng, send, recv, False):
            cp.start()
        token[...] = jnp.zeros_like(token)

    hbm = lambda a: pltpu.HBM(a.shape, a.dtype)
    outs = pl.pallas_call(
        body, name=name,
        out_shape=(pltpu.SemaphoreType.DMA((3 * n,)), pltpu.SemaphoreType.DMA((3 * n,)), *[hbm(a) for a in srcs + lands],
                   jax.ShapeDtypeStruct((8, 128), F32)),
        in_specs=[_HBM] * (2 * n), out_specs=(_SEM, _SEM, *[_HBM] * (2 * n), pl.BlockSpec(memory_space=pltpu.VMEM)),
        input_output_aliases={i: 2 + i for i in range(2 * n)},
        compiler_params=pltpu.CompilerParams(has_side_effects=pltpu.SideEffectType.DATAFLOW_SIDE_EFFECTING),
    )(*[pltpu.with_memory_space_constraint(a, pltpu.HBM) for a in srcs + lands])
    return outs[0], outs[1], list(outs[2:2 + n]), list(outs[2 + n:2 + 2 * n]), outs[2 + 2 * n]


def _exchange_end(send, recv, srcs, lands, ng, after, name):
    n = len(srcs)
    after = list(after)

    def body(*refs):
        ins, land_refs = refs[:n], refs[n:2 * n]
        send_ref, recv_ref = refs[2 * n], refs[2 * n + 1]
        for cp in _split_copies(ins, land_refs, ng, send_ref, recv_ref, False):
            cp.wait_send()
        for cp in _split_copies(ins, land_refs, ng, send_ref, recv_ref, True):
            cp.wait_recv()

    hbm = lambda a: pltpu.HBM(a.shape, a.dtype)
    outs = pl.pallas_call(
        body, name=name, out_shape=tuple(hbm(a) for a in list(srcs) + list(lands)),
        in_specs=[_HBM] * (2 * n) + [_SEM, _SEM] + [ANY] * len(after), out_specs=tuple([_HBM] * (2 * n)),
        input_output_aliases={i: i for i in range(2 * n)},
        compiler_params=pltpu.CompilerParams(has_side_effects=pltpu.SideEffectType.DATAFLOW_SIDE_EFFECTING),
    )(*srcs, *lands, send, recv, *after)
    return list(outs[n:])


def _chip_sum(bc_idx, sums, recvd, smalls_slots, smalls_own, name, steps=4):
    ng, ns = len(sums), len(smalls_slots)

    def body(bc_ref, *refs):
        own, rx = refs[:ng], refs[ng:2 * ng]
        sl = refs[2 * ng:2 * ng + ns]
        sl_own = refs[2 * ng + ns:2 * ng + 2 * ns]
        o = refs[2 * ng + 2 * ns:]
        for i in range(ng):
            tot = own[i][...]
            for j in range(3):
                tot = tot + rx[i][j].astype(F32)
            o[i][...] = tot
        for i in range(ns):
            term = [jnp.where(bc_ref[0] == kk, sl_own[i][...], sl[i][kk]) for kk in range(N_CHIP)]
            o[ng + i][...] = ((term[0] + term[1]) + term[2]) + term[3]

    def rows(g):
        return g.shape[1] // steps

    in_specs = [pl.BlockSpec((None, rows(g), g.shape[2]), lambda r, bc: (bc[0], r, 0)) for g in sums]
    in_specs += [pl.BlockSpec((3, rows(g), g.shape[2]), lambda r, bc: (0, r, 0)) for g in sums]
    in_specs += [pl.BlockSpec(s.shape, lambda r, bc: (0, 0, 0)) for s in smalls_slots]
    in_specs += [pl.BlockSpec(s.shape[1:], lambda r, bc: (0, 0)) for s in smalls_slots]
    out_specs = [pl.BlockSpec((rows(g), g.shape[2]), lambda r, bc: (bc[1] * steps + r, 0)) for g in sums]
    out_specs += [pl.BlockSpec(s.shape[1:], lambda r, bc: (bc[1], 0)) for s in smalls_slots]
    out_shape = [jax.ShapeDtypeStruct((2 * g.shape[1], g.shape[2]), F32) for g in sums]
    out_shape += [jax.ShapeDtypeStruct((2 * s.shape[1], s.shape[2]), F32) for s in smalls_slots]
    return pl.pallas_call(
        body, out_shape=out_shape,
        grid_spec=pltpu.PrefetchScalarGridSpec(num_scalar_prefetch=1, grid=(steps,), in_specs=in_specs, out_specs=out_specs),
        compiler_params=_cp(("arbitrary",), VMEM_LIMIT), name=name)(bc_idx, *sums, *recvd, *smalls_slots, *smalls_own)


def _pair_gather(arrs, name):
    n = len(arrs)

    def body(*refs):
        outs = refs[n:2 * n]
        send, recv = refs[2 * n:]
        x, y, c, _ = _coords()
        cps = []
        for i in range(n):
            hr = outs[i].shape[0] // 2
            mine = outs[i].at[pl.ds(c * hr, hr)]
            cps.append(pltpu.make_async_remote_copy(
                src_ref=mine, dst_ref=mine, send_sem=send.at[i], recv_sem=recv.at[i],
                device_id=(x, y, 1 - c), device_id_type=MESH))
        for cp in cps:
            cp.start()
        for i in range(n):
            hr = outs[i].shape[0] // 2
            theirs = outs[i].at[pl.ds((1 - c) * hr, hr)]
            pltpu.make_async_remote_copy(
                src_ref=theirs, dst_ref=theirs, send_sem=send.at[i], recv_sem=recv.at[i],
                device_id=(x, y, 1 - c), device_id_type=MESH).wait_recv()
        for cp in cps:
            cp.wait_send()

    return list(pl.pallas_call(
        body, out_shape=[jax.ShapeDtypeStruct(a.shape, a.dtype) for a in arrs], in_specs=[ANY] * n, out_specs=[ANY] * n,
        input_output_aliases={i: i for i in range(n)},
        scratch_shapes=[pltpu.SemaphoreType.DMA((n,)), pltpu.SemaphoreType.DMA((n,))],
        name=name)(*arrs))


def _adamw_math(w, g, m, v):
    m2 = ADAM_B1 * m + (1.0 - ADAM_B1) * g
    v2 = ADAM_B2 * v + (1.0 - ADAM_B2) * (g * g)
    m_hat = m2 / (1.0 - ADAM_B1 ** ADAM_STEP)
    v_hat = v2 / (1.0 - ADAM_B2 ** ADAM_STEP)
    delta = -ADAM_LR * (m_hat / (jnp.sqrt(v_hat) + ADAM_EPS) + ADAM_WD * w)
    return delta, m2, v2


def _adamw_big(w, g, m, v, name, steps=8):
    r, c = w.shape

    def body(w_ref, g_ref, m_ref, v_ref, d_ref, m2_ref, v2_ref, g2_ref):
        gv = g_ref[...]
        d_ref[...], m2_ref[...], v2_ref[...] = _adamw_math(w_ref[...], gv, m_ref[...], v_ref[...])
        g2_ref[...] = gv

    spec = pl.BlockSpec((r // steps, c), lambda i: (i, 0))
    return pl.pallas_call(
        body, grid=(steps,), in_specs=[spec] * 4, out_specs=[spec] * 4,
        out_shape=[jax.ShapeDtypeStruct((r, c), F32)] * 4,
        compiler_params=_cp(("parallel",), VMEM_LIMIT), name=name)(w, g, m, v)


def _adamw_small(groups):
    n = len(groups)

    def body(*refs):
        for i in range(n):
            w_ref, g_ref, m_ref, v_ref = refs[4 * i:4 * i + 4]
            d_ref, m2_ref, v2_ref = refs[4 * n + 3 * i:4 * n + 3 * i + 3]
            d_ref[...], m2_ref[...], v2_ref[...] = _adamw_math(w_ref[...], g_ref[...], m_ref[...], v_ref[...])

    flat = [a for grp in groups for a in grp]
    out_shape = [jax.ShapeDtypeStruct(grp[0].shape, F32) for grp in groups for _ in range(3)]
    outs = pl.pallas_call(body, out_shape=out_shape, name="adamw_small")(*flat)
    return [tuple(outs[3 * i:3 * i + 3]) for i in range(n)]


def kernel(x, mem, norm_mix_g, w_in, conv_w, gm_ln_g, gm_ln_b, gm_ws, gm_bs, w_out, norm_x_g, norm_mem_g, w_q, w_kv, w_xo, norm_final_g, loss_target, m_norm_mix_g, m_w_in, m_conv_w, m_gm_ln_g, m_gm_ln_b, m_gm_ws, m_gm_bs, m_w_out, m_norm_x_g, m_norm_mem_g, m_w_q, m_w_kv, m_w_xo, m_norm_final_g, v_norm_mix_g, v_w_in, v_conv_w, v_gm_ln_g, v_gm_ln_b, v_gm_ws, v_gm_bs, v_w_out, v_norm_x_g, v_norm_mem_g, v_w_q, v_w_kv, v_w_xo, v_norm_final_g):
    t = x.shape[1]
    xi = lax.axis_index("x")
    yi = lax.axis_index("y")
    ci = lax.axis_index("c")
    b_idx = jnp.reshape(2 * xi + yi, (1,)).astype(jnp.int32)
    c_idx = jnp.reshape(ci, (1,)).astype(jnp.int32)

    x2d, mem2d, tgt = x[0], mem[0], loss_target[0]
    big = [w_in[0], w_out[0], w_q[0], w_kv[0], w_xo[0]]
    big_m = [m_w_in[0], m_w_out[0], m_w_q[0], m_w_kv[0], m_w_xo[0]]
    big_v = [v_w_in[0], v_w_out[0], v_w_q[0], v_w_kv[0], v_w_xo[0]]
    g3 = norm_final_g.reshape(1, D)

    def pad8(a):
        return jnp.pad(a, ((0, 8 - a.shape[0]), (0, 0)))

    own_blocks = _cast_shards(b_idx, big)

    tril = jnp.tril(jnp.ones((CH, CH), bool))
    wc32 = jnp.where(tril[None], gm_ws[0], 0.0)
    wc = wc32.astype(BF16)
    wct = jnp.swapaxes(wc32, 1, 2).astype(BF16)
    bsb = jnp.broadcast_to(gm_bs[0][:, :, None], (HEADS, CH, CH))

    blk = 2 * xi + yi
    order = jnp.stack([blk, blk ^ 2, blk ^ 1, blk ^ 3]).astype(jnp.int32)
    proj, ht, win_f, cw8, (wout_f,) = _proj_gather(
        order, x2d, norm_mix_g, own_blocks[0], pad8(conv_w[0]), [own_blocks[1]])
    mixin, mixt, (wq_f, wkv_f, wxo_f) = _mixer_fwd(proj, cw8, gm_ln_g, gm_ln_b, wc, bsb, own_blocks[2:])
    wout2, wq2, wxo2 = wout_f.reshape(MIX, D), wq_f.reshape(D, D), wxo_f.reshape(D, D)
    k, v, mt = _mem_fwd(mem2d, norm_mem_g, wkv_f)
    del mt

    (loss_tile, dmix, dx1b, h2t, dq, ot, dx2b, dk, dv, dg2, dg3) = _tail(
        x2d, tgt, mixin, wout2, wq2, wxo2, k, v, norm_x_g, g3)
    dwkv, dwkv_b, dgm = _mem_bwd(mem2d, norm_mem_g, dk, dv, wkv_f)
    dproj, dcw, dlng, dlnb, dwc, dbs8, grad_x, dg1 = _mixer_bwd(
        proj, dmix, cw8, gm_ln_g, gm_ln_b, wc, wct, bsb, win_f, x2d, dx1b, norm_mix_g)

    bc_idx = jnp.concatenate([b_idx, c_idx])
    dwin, dwin_b = _grad_matmul(ht, dproj, dgm, by_cols=True, name="grad_w_in", tk=2048)
    zero = jnp.zeros((1, D), F32)
    loss_row = jnp.broadcast_to(loss_tile[0:1, 0:1], (1, D))
    sv = jnp.concatenate([dg1[0:1], dg2, dgm, dg3, dlng, dlnb, dbs8[0:1], loss_row, dcw], axis=0)
    sw = dwc.reshape(HEADS * CH, CH)
    rx1b = _pair_exchange([dwin_b], [sv, sw], "pair_exchange_b")
    ps_b = _pair_sum(c_idx, [dwin], rx1b[:1], [sv, sw], rx1b[1:], "pair_sum_b")
    sums_b, sums_b_b, psmall = list(ps_b[:1]), list(ps_b[1:2]), list(ps_b[2:])
    send_b, recv_b, src_b, land_b, token_b = _exchange_begin(sums_b_b, psmall, "exchange_b_begin")

    dwxo, dwxo_b = _grad_matmul(ot, dx2b, token_b, by_cols=False, name="grad_w_xo", tk=2048)
    dwq, dwq_b = _grad_matmul(h2t, dq, token_b, by_cols=False, name="grad_w_q", tk=2048)
    dwout, dwout_b = _grad_matmul(mixt, dx1b, token_b, by_cols=False, name="grad_w_out")
    rx1a = _pair_exchange([dwout_b, dwq_b, dwkv_b, dwxo_b], [], "pair_exchange_a")
    ps_a = _pair_sum(c_idx, [dwout, dwq, dwkv, dwxo], rx1a, [], [], "pair_sum_a")
    sums_a, sums_a_b = list(ps_a[:4]), list(ps_a[4:8])
    send_a, recv_a, src_a, land_a, token_a = _exchange_begin(sums_a_b, [], "exchange_a_begin")

    rx2b = _exchange_end(send_b, recv_b, src_b, land_b, 1, [token_a], "exchange_b_end")
    red_b = _chip_sum(bc_idx, sums_b, rx2b[:1], rx2b[1:], psmall, "chip_sum_b")
    gwin, svf, swf = _pair_gather(red_b, "pair_gather_b")
    out_b = _adamw_big(big[0], gwin, big_m[0], big_v[0], "adamw_w_in")

    def vec_pack(a1, a2, am, a3, lg, lb, bs):
        return jnp.concatenate([a1, a2, am, a3.reshape(1, D), lg, lb, bs.reshape(1, D), zero], axis=0)

    wv = vec_pack(norm_mix_g, norm_x_g, norm_mem_g, norm_final_g, gm_ln_g, gm_ln_b, gm_bs)
    mv = vec_pack(m_norm_mix_g, m_norm_x_g, m_norm_mem_g, m_norm_final_g, m_gm_ln_g, m_gm_ln_b, m_gm_bs)
    vv = vec_pack(v_norm_mix_g, v_norm_x_g, v_norm_mem_g, v_norm_final_g, v_gm_ln_g, v_gm_ln_b, v_gm_bs)
    loss = svf[7, 0]
    gcw = lax.dynamic_slice_in_dim(svf[8:16], blk * (D // N_CHIP), D // N_CHIP, axis=1)
    gws = swf
    gv = svf[0:8]
    (dv_, mv_, vv_), (dc_, mc_, vc_), (dws_, mws_, vws_) = _adamw_small([
        (wv, gv, mv, vv),
        (pad8(conv_w[0]), gcw, pad8(m_conv_w[0]), pad8(v_conv_w[0])),
        (gm_ws.reshape(HEADS * CH, CH), gws, m_gm_ws.reshape(HEADS * CH, CH), v_gm_ws.reshape(HEADS * CH, CH))])

    rx2a = _exchange_end(send_a, recv_a, src_a, land_a, 4, [out_b[0], dv_], "exchange_a_end")
    red_a = _chip_sum(bc_idx, sums_a, rx2a, [], [], "chip_sum_a")
    g_a = _pair_gather(red_a, "pair_gather_a")
    names = ["w_out", "w_q", "w_kv", "w_xo"]
    out_a = [_adamw_big(w, g, m, v, "adamw_" + nm) for w, g, m, v, nm in zip(big[1:], g_a, big_m[1:], big_v[1:], names)]
    big_out = [out_b] + out_a

    def unpack(vecs, cw, ws, bigs):
        r = lambda i: vecs[i:i + 1]
        return [r(0), bigs[0][None], cw[0:3][None], r(4), r(5), ws.reshape(1, HEADS, CH, CH), vecs[6].reshape(1, HEADS, CH),
                bigs[1][None], r(1), r(2), bigs[2][None], bigs[3][None], bigs[4][None], vecs[3]]

    grads_out = unpack(gv, gcw, gws, [o[3] for o in big_out])
    delta_out = unpack(dv_, dc_, dws_, [o[0] for o in big_out])
    m_out = unpack(mv_, mc_, mws_, [o[1] for o in big_out])
    v_out = unpack(vv_, vc_, vws_, [o[2] for o in big_out])
    return (loss, grad_x[None], *grads_out, *delta_out, *m_out, *v_out)
```

```python
import functools
import math

import jax
import jax.numpy as jnp
from jax import lax
from jax.experimental import pallas as pl
from jax.experimental.pallas import tpu as pltpu

F32 = jnp.float32
BF16 = jnp.bfloat16
MESH = pl.DeviceIdType.MESH

D = 1024
SLAB = 1024
N_SLAB = 7
IN_DIM = N_SLAB * SLAB
MIX = 2 * SLAB
HEADS = 8
CH = 128
XH = 4
XD = D // XH
EPS = 1e-6
GELU_C = math.sqrt(2.0 / math.pi)
GELU_A = 0.044715
N_CHIP = 4
IN_BLK = IN_DIM // N_CHIP
KV_BLK = 2 * D // N_CHIP

ADAM_LR, ADAM_B1, ADAM_B2, ADAM_EPS, ADAM_WD, ADAM_STEP = 0.001, 0.9, 0.999, 1e-08, 0.01, 10

VMEM_LIMIT = 60 * 1024 * 1024


def _cp(sem=None, vmem=None):
    return pltpu.CompilerParams(dimension_semantics=sem, vmem_limit_bytes=vmem)


def _full(shape, buffers=None):
    n = len(shape)
    if buffers is None:
        return pl.BlockSpec(shape, lambda *_: (0,) * n)
    return pl.BlockSpec(shape, lambda *_: (0,) * n, pipeline_mode=pl.Buffered(buffers))


ANY = pl.BlockSpec(memory_space=pl.ANY)


def _bdot(a, b):
    return jnp.dot(a.astype(BF16), b.astype(BF16), preferred_element_type=F32)


def _bdot_nt(a, b):
    return lax.dot_general(a.astype(BF16), b.astype(BF16), (((1,), (1,)), ((), ())), preferred_element_type=F32)


def _rms(x, g):
    r = lax.rsqrt(jnp.mean(x * x, axis=-1, keepdims=True) + EPS)
    return x * r * g, r


def _rms_bwd(dy, x, r, g):
    gdy = dy * g
    dx = r * gdy - x * (r * r * r) * jnp.mean(x * gdy, axis=-1, keepdims=True)
    dg = jnp.sum(dy * x * r, axis=0, keepdims=True)
    return dx, dg


def _gelu_parts(x):
    x2 = x * x
    t = jnp.tanh(GELU_C * (x + GELU_A * x * x2))
    val = 0.5 * x * (1.0 + t)
    grad = 0.5 * (1.0 + t) + 0.5 * x * (1.0 - t * t) * (GELU_C * (1.0 + 3.0 * GELU_A * x2))
    return val, grad


def _gelu(x):
    return 0.5 * x * (1.0 + jnp.tanh(GELU_C * (x + GELU_A * x * x * x)))


def _sigmoid(z):
    return 1.0 / (1.0 + jnp.exp(-z))


def _cast_shards(b_idx, arrs):
    n = len(arrs)
    steps = 8

    def body(b_ref, *refs):
        for i in range(n):
            refs[n + i][...] = refs[i][...].astype(BF16)

    in_specs = [pl.BlockSpec((a.shape[0] // steps, a.shape[1]), lambda i, b: (i, 0)) for a in arrs]
    out_specs = [pl.BlockSpec((None, a.shape[0] // steps, a.shape[1]), lambda i, b: (b[0], i, 0)) for a in arrs]
    return pl.pallas_call(
        body, out_shape=[jax.ShapeDtypeStruct((N_CHIP,) + a.shape, BF16) for a in arrs],
        grid_spec=pltpu.PrefetchScalarGridSpec(num_scalar_prefetch=1, grid=(steps,), in_specs=in_specs, out_specs=out_specs),
        compiler_params=_cp(("arbitrary",)), name="cast_shards")(b_idx, *arrs)


def _proj_gather(order, x, g, win_own, cw8s, more, tm=1024):
    t = x.shape[0]
    ni = t // tm
    nm = len(more)

    def body(*refs):
        order_ref, x_ref, g_ref, win_in, cw_in = refs[:5]
        o_ref, ht_ref, win_f, cw_out = refs[5 + nm:9 + nm]
        more_out = refs[9 + nm:9 + 2 * nm]
        hbuf, wv, cw_s, cw_r, loc = refs[9 + 2 * nm:14 + 2 * nm]
        g_in = _Gather([win_f], *refs[14 + 2 * nm:18 + 2 * nm])
        g_more = _Gather(more_out, *refs[18 + 2 * nm:22 + 2 * nm])
        j, i = pl.program_id(0), pl.program_id(1)
        x, y, c, chips = _coords()
        b = 2 * x + y
        blks = [2 * chip[0] + chip[1] for chip in chips]

        def cw_cols(blk):
            return cw_out.at[:, pl.ds(blk * (D // N_CHIP), D // N_CHIP)]

        def cw_copy(k, blk):
            src = cw_in if blk is None else cw_cols(blk)
            return pltpu.make_async_remote_copy(src_ref=src, dst_ref=cw_cols(b if blk is None else blk), send_sem=cw_s.at[k],
                                                recv_sem=cw_r.at[k], device_id=(*chips[k], c), device_id_type=MESH)

        cw_local = pltpu.make_async_copy(cw_in, cw_cols(b), loc.at[1])

        def load(blk, slot):
            return pltpu.make_async_copy(win_f.at[blk], wv.at[slot], loc.at[2 + slot])

        @pl.when((j == 0) & (i == 0))
        def _():
            g_in.start()
            cw_local.start()
            for k in range(3):
                cw_copy(k, None).start()
            load(b, 0).start()
            load(b, 0).wait()

        @pl.when((j == 1) & (i == 0))
        def _():
            g_in.hop()
            g_more.start()
            g_in.near_ready()
            load(g_in.bx, 1).start()
            load(g_in.by, 0).start()
            load(g_in.bx, 1).wait()

        @pl.when((j == 2) & (i == 0))
        def _():
            load(g_in.by, 0).wait()
            g_in.far()
            g_in.far_ready()
            load(g_in.bd, 1).start()

        @pl.when((j == 3) & (i == 0))
        def _():
            load(g_in.bd, 1).wait()
            g_more.hop()

        rows = pl.ds(pl.multiple_of(i * tm, tm), tm)

        @pl.when(j == 0)
        def _():
            h, _ = _rms(x_ref[...], g_ref[...])
            hbuf[rows, :] = h.astype(BF16)
            ht_ref[...] = h.T.astype(BF16)

        @pl.when((j == N_CHIP - 1) & (i == ni - 1))
        def _():
            g_more.far()

        o_ref[...] = jnp.dot(hbuf[rows, :], wv[lax.rem(j, 2)], preferred_element_type=F32).astype(BF16)

        @pl.when((j == N_CHIP - 1) & (i == ni - 1))
        def _():
            for k in range(3):
                cw_copy(k, blks[k]).wait_recv()
            for k in range(3):
                cw_copy(k, None).wait_send()
            cw_local.wait()
            g_more.near_ready()
            g_more.far_ready()
            g_in.drain()
            g_more.drain()

    first = lambda j, i: jnp.where(j == 0, i, ni - 1)
    in_specs = [pl.BlockSpec((tm, D), lambda j, i, o: (first(j, i), 0)), pl.BlockSpec((1, D), lambda j, i, o: (0, 0)),
                ANY, ANY] + [ANY] * nm
    out_specs = [pl.BlockSpec((tm, IN_BLK), lambda j, i, o: (i, o[j])),
                 pl.BlockSpec((D, tm), lambda j, i, o: (0, first(j, i))), ANY, ANY] + [ANY] * nm
    outs = pl.pallas_call(
        body, out_shape=[jax.ShapeDtypeStruct((t, IN_DIM), BF16), jax.ShapeDtypeStruct((D, t), BF16),
                         jax.ShapeDtypeStruct(win_own.shape, BF16), jax.ShapeDtypeStruct((8, D), F32)]
        + [jax.ShapeDtypeStruct(f.shape, f.dtype) for f in more],
        grid_spec=pltpu.PrefetchScalarGridSpec(
            num_scalar_prefetch=1, grid=(N_CHIP, ni), in_specs=in_specs, out_specs=out_specs,
            scratch_shapes=[pltpu.VMEM((t, D), BF16), pltpu.VMEM((2, D, IN_BLK), BF16)]
            + [pltpu.SemaphoreType.DMA((3,))] * 2 + [pltpu.SemaphoreType.DMA((4,))] + _gather_sems(1) + _gather_sems(nm)),
        input_output_aliases={3: 2, **{5 + w: 4 + w for w in range(nm)}},
        compiler_params=_cp(("arbitrary", "arbitrary"), VMEM_LIMIT), name="proj_gather")(order, x, g, win_own, cw8s, *more)
    return outs[0], outs[1], outs[2], outs[3], outs[4:]


class _Gather:
    def __init__(self, outs, ici_s, ici_r, d2d_s, d2d_r):
        x, y, c, _ = _coords()
        self.outs, self.c = outs, c
        self.sems = ici_s, ici_r, d2d_s, d2d_r
        self.b, self.bx, self.by, self.bd = 2 * x + y, 2 * (1 - x) + y, 2 * x + (1 - y), 2 * (1 - x) + (1 - y)
        self.xn, self.yn, self.sib = (1 - x, y, c), (x, 1 - y, c), (x, y, 1 - c)

    def piece(self, w, blk, hc, quarter=None):
        hr = self.outs[w].shape[1] // 2
        if quarter is None:
            return self.outs[w].at[blk, pl.ds(hc * hr, hr)]
        return self.outs[w].at[blk, pl.ds(hc * hr + quarter * (hr // 2), hr // 2)]

    def ici(self, w, k, ref, to):
        return pltpu.make_async_remote_copy(src_ref=ref, dst_ref=ref, send_sem=self.sems[0].at[w, k],
                                            recv_sem=self.sems[1].at[w, k], device_id=to, device_id_type=MESH)

    def d2d(self, w, k, ref):
        return pltpu.make_async_remote_copy(src_ref=ref, dst_ref=ref, send_sem=self.sems[2].at[w, k],
                                            recv_sem=self.sems[3].at[w, k], device_id=self.sib, device_id_type=MESH)

    def start(self):
        for w in range(len(self.outs)):
            mine = self.piece(w, self.b, self.c)
            self.ici(w, 0, mine, self.xn).start()
            self.ici(w, 1, mine, self.yn).start()

    def hop(self):
        c = self.c
        for w in range(len(self.outs)):
            self.ici(w, 0, self.piece(w, self.bx, c), self.xn).wait_recv()
            self.ici(w, 1, self.piece(w, self.by, c), self.yn).wait_recv()
            self.ici(w, 2, self.piece(w, self.bx, c, 0), self.yn).start()
            self.ici(w, 3, self.piece(w, self.by, c, 1), self.xn).start()
            self.d2d(w, 0, self.piece(w, self.bx, c)).start()
            self.d2d(w, 1, self.piece(w, self.by, c)).start()

    def near_ready(self):
        for w in range(len(self.outs)):
            self.d2d(w, 0, self.piece(w, self.bx, 1 - self.c)).wait_recv()
            self.d2d(w, 1, self.piece(w, self.by, 1 - self.c)).wait_recv()

    def far(self):
        c = self.c
        for w in range(len(self.outs)):
            self.ici(w, 2, self.piece(w, self.bd, c, 0), self.yn).wait_recv()
            self.ici(w, 3, self.piece(w, self.bd, c, 1), self.xn).wait_recv()
            self.d2d(w, 2, self.piece(w, self.bd, c, 0)).start()
            self.d2d(w, 3, self.piece(w, self.bd, c, 1)).start()

    def far_ready(self):
        for w in range(len(self.outs)):
            self.d2d(w, 2, self.piece(w, self.bd, 1 - self.c, 0)).wait_recv()
            self.d2d(w, 3, self.piece(w, self.bd, 1 - self.c, 1)).wait_recv()

    def drain(self):
        c = self.c
        for w in range(len(self.outs)):
            mine = self.piece(w, self.b, c)
            self.ici(w, 0, mine, self.xn).wait_send()
            self.ici(w, 1, mine, self.yn).wait_send()
            self.ici(w, 2, self.piece(w, self.bx, c, 0), self.yn).wait_send()
            self.ici(w, 3, self.piece(w, self.by, c, 1), self.xn).wait_send()
            self.d2d(w, 0, self.piece(w, self.bx, c)).wait_send()
            self.d2d(w, 1, self.piece(w, self.by, c)).wait_send()
            self.d2d(w, 2, self.piece(w, self.bd, c, 0)).wait_send()
            self.d2d(w, 3, self.piece(w, self.bd, c, 1)).wait_send()


def _gather_sems(nw):
    return [pltpu.SemaphoreType.DMA((max(nw, 1), 4))] * 4


def _mixer_fwd(proj, cw8, lng, lnb, wc, bsb, fulls, tm=256):
    t = proj.shape[0]
    nt = t // tm
    nch = tm // CH
    nw = len(fulls)

    def body(*refs):
        p_ref, cw_ref, lng_ref, lnb_ref, wc_ref, bsb_ref = refs[:6]
        mix_ref, mixt_ref = refs[6 + nw:8 + nw]
        w_outs = refs[8 + nw:8 + 2 * nw]
        prev_ref, stage_ref = refs[8 + 2 * nw:10 + 2 * nw]
        gather = _Gather(w_outs, *refs[10 + 2 * nw:])

        @pl.when(pl.program_id(0) == 0)
        def _():
            gather.start()
            prev_ref[...] = jnp.zeros_like(prev_ref)

        @pl.when(pl.program_id(0) == nt // 2)
        def _():
            gather.hop()

        @pl.when(pl.program_id(0) == nt - 1)
        def _():
            gather.far()

        rows = lax.broadcasted_iota(jnp.int32, (tm, CH), 0)
        for s in range(HEADS):
            cs = pl.ds(CH * s, CH)

            def slab(k):
                return p_ref[:, pl.ds(k * SLAB + CH * s, CH)].astype(F32)

            gb, gc, xa, za = slab(0), slab(1), slab(2), slab(3)
            cx = gc * xa
            p6 = jnp.broadcast_to(prev_ref[6:7, cs], (tm, CH))
            p7 = jnp.broadcast_to(prev_ref[7:8, cs], (tm, CH))
            c1 = jnp.where(rows == 0, p7, pltpu.roll(cx, 1, 0))
            c2 = jnp.where(rows == 0, p6, jnp.where(rows == 1, p7, pltpu.roll(cx, 2, 0)))
            prev_ref[:, cs] = cx[tm - 8:, :]
            cv = cw_ref[0:1, cs] * c2 + cw_ref[1:2, cs] * c1 + cw_ref[2:3, cs] * cx
            stage_ref[:, cs] = gb * cv * (za * _sigmoid(za))

            u, v, zb = slab(4), slab(5), slab(6)
            ug, vg = _gelu(u), _gelu(v)
            dlt = vg - jnp.mean(vg, axis=-1, keepdims=True)
            vhat = dlt * lax.rsqrt(jnp.mean(dlt * dlt, axis=-1, keepdims=True) + EPS)
            vn = (vhat * lng_ref[:, cs] + lnb_ref[:, cs]).astype(BF16)
            gate = ug * (zb * _sigmoid(zb))
            for c in range(nch):
                rs = slice(CH * c, CH * (c + 1))
                sp = jnp.dot(wc_ref[s], vn[rs], preferred_element_type=F32) + bsb_ref[s]
                stage_ref[rs, pl.ds(SLAB + CH * s, CH)] = gate[rs] * sp

        full = stage_ref[...]
        mix_ref[...] = full.astype(BF16)
        mixt_ref[...] = full.T.astype(BF16)

        @pl.when(pl.program_id(0) == nt - 1)
        def _():
            gather.near_ready()
            gather.far_ready()
            gather.drain()

    sems = _gather_sems(nw)
    outs = pl.pallas_call(
        body, grid=(nt,),
        in_specs=[pl.BlockSpec((tm, IN_DIM), lambda i: (i, 0)), _full((8, D)), _full((1, D)), _full((1, D)),
                  _full((HEADS, CH, CH)), _full((HEADS, CH, CH))] + [ANY] * nw,
        out_specs=[pl.BlockSpec((tm, MIX), lambda i: (i, 0)), pl.BlockSpec((MIX, tm), lambda i: (0, i))] + [ANY] * nw,
        out_shape=[jax.ShapeDtypeStruct((t, MIX), BF16), jax.ShapeDtypeStruct((MIX, t), BF16)]
        + [jax.ShapeDtypeStruct(f.shape, f.dtype) for f in fulls],
        input_output_aliases={6 + w: 2 + w for w in range(nw)},
        scratch_shapes=[pltpu.VMEM((8, D), F32), pltpu.VMEM((tm, MIX), F32)] + sems,
        compiler_params=_cp(("arbitrary",), VMEM_LIMIT), name="mixer_fwd")(proj, cw8, lng, lnb, wc, bsb, *fulls)
    return outs[0], outs[1], outs[2:]


def _mem_fwd(mem, gm, wkv_f):
    n_mem = mem.shape[0]

    def body(mem_ref, gm_ref, w_ref, k_ref, v_ref, mt_ref):
        m, _ = _rms(mem_ref[...], gm_ref[...])
        mb = m.astype(BF16)
        mt_ref[...] = m.T.astype(BF16)
        for j in range(N_CHIP):
            dst = k_ref if j < 2 else v_ref
            dst[:, pl.ds(KV_BLK * (j % 2), KV_BLK)] = jnp.dot(mb, w_ref[j], preferred_element_type=F32).astype(BF16)

    return pl.pallas_call(
        body, out_shape=[jax.ShapeDtypeStruct((n_mem, D), BF16), jax.ShapeDtypeStruct((n_mem, D), BF16),
                         jax.ShapeDtypeStruct((D, n_mem), BF16)],
        compiler_params=_cp(None, VMEM_LIMIT), name="mem_fwd")(mem, gm, wkv_f)


def _tail(x, tgt, mixin, wout, wq, wxo, k, v, g2, g3, tm=512, sub=512):
    t = x.shape[0]
    n_mem = k.shape[0]
    scale = 1.0 / math.sqrt(XD)

    def body(x_ref, tgt_ref, mix_ref, wout_ref, wq_ref, wxo_ref, k_ref, v_ref, g2_ref, g3_ref,
             loss_ref, dmix_ref, dx1b_ref, h2t_ref, dq_ref, ot_ref, dx2b_ref, dk_ref, dv_ref, dg2_ref, dg3_ref):
        @pl.when(pl.program_id(0) == 0)
        def _():
            loss_ref[...] = jnp.zeros_like(loss_ref)
            dk_ref[...] = jnp.zeros_like(dk_ref)
            dv_ref[...] = jnp.zeros_like(dv_ref)
            dg2_ref[...] = jnp.zeros_like(dg2_ref)
            dg3_ref[...] = jnp.zeros_like(dg3_ref)

        g2, g3 = g2_ref[...], g3_ref[...]
        for sb in range(tm // sub):
            rs = pl.ds(sub * sb, sub)
            x1 = x_ref[rs, :] + jnp.dot(mix_ref[rs, :], wout_ref[...], preferred_element_type=F32)
            h2, r2 = _rms(x1, g2)
            h2t_ref[:, rs] = h2.T.astype(BF16)
            q = _bdot(h2, wq_ref[...]).astype(BF16)
            probs, outs = [], []
            for hd in range(XH):
                hs = pl.ds(XD * hd, XD)
                s = _bdot_nt(q[:, XD * hd:XD * (hd + 1)], k_ref[:, hs]) * scale
                e = jnp.exp(s - jnp.max(s, axis=-1, keepdims=True))
                p = e / jnp.sum(e, axis=-1, keepdims=True)
                probs.append(p)
                outs.append(_bdot(p, v_ref[:, hs]))
            o = jnp.concatenate(outs, axis=-1)
            ot_ref[:, rs] = o.T.astype(BF16)
            x2 = x1 + _bdot(o, wxo_ref[...])
            y, r3 = _rms(x2, g3)
            diff = y - tgt_ref[rs, :]
            row_loss = jnp.sum(diff * diff, axis=-1, keepdims=True)
            loss_ref[...] += jnp.broadcast_to(jnp.sum(row_loss, axis=0, keepdims=True) * (0.5 / D), loss_ref.shape)

            dx2, dg3 = _rms_bwd(diff * (1.0 / D), x2, r3, g3)
            dg3_ref[...] += dg3
            dx2b = dx2.astype(BF16)
            dx2b_ref[rs, :] = dx2b
            do = _bdot_nt(dx2b, wxo_ref[...])
            dqs = []
            for hd in range(XH):
                hs = pl.ds(XD * hd, XD)
                p = probs[hd]
                do_h = do[:, XD * hd:XD * (hd + 1)]
                dv_ref[:, hs] += _bdot(p.T, do_h)
                dp = _bdot_nt(do_h, v_ref[:, hs])
                ds = p * (dp - jnp.sum(dp * p, axis=-1, keepdims=True))
                dqs.append(_bdot(ds, k_ref[:, hs]) * scale)
                dk_ref[:, hs] += _bdot(ds.T, q[:, XD * hd:XD * (hd + 1)]) * scale
            dq = jnp.concatenate(dqs, axis=-1).astype(BF16)
            dq_ref[rs, :] = dq
            dx1n, dg2 = _rms_bwd(_bdot_nt(dq, wq_ref[...]), x1, r2, g2)
            dg2_ref[...] += dg2
            dx1b = (dx2 + dx1n).astype(BF16)
            dx1b_ref[rs, :] = dx1b
            dmix_ref[rs, :] = _bdot_nt(dx1b, wout_ref[...]).astype(BF16)

    tok = lambda w: pl.BlockSpec((tm, w), lambda i: (i, 0))
    tok_t = lambda w: pl.BlockSpec((w, tm), lambda i: (0, i))
    return pl.pallas_call(
        body, grid=(t // tm,),
        in_specs=[tok(D), tok(D), tok(MIX), _full((MIX, D), 1), _full((D, D), 1), _full((D, D), 1),
                  _full((n_mem, D), 1), _full((n_mem, D), 1), _full((1, D)), _full((1, D))],
        out_specs=[_full((8, 128)), tok(MIX), tok(D), tok_t(D), tok(D), tok_t(D), tok(D),
                   _full((n_mem, D)), _full((n_mem, D)), _full((1, D)), _full((1, D))],
        out_shape=[jax.ShapeDtypeStruct((8, 128), F32), jax.ShapeDtypeStruct((t, MIX), BF16),
                   jax.ShapeDtypeStruct((t, D), BF16),
                   jax.ShapeDtypeStruct((D, t), BF16), jax.ShapeDtypeStruct((t, D), BF16),
                   jax.ShapeDtypeStruct((D, t), BF16), jax.ShapeDtypeStruct((t, D), BF16),
                   jax.ShapeDtypeStruct((n_mem, D), F32), jax.ShapeDtypeStruct((n_mem, D), F32),
                   jax.ShapeDtypeStruct((1, D), F32), jax.ShapeDtypeStruct((1, D), F32)],
        compiler_params=_cp(("arbitrary",), VMEM_LIMIT), name="tail")(x, tgt, mixin, wout, wq, wxo, k, v, g2, g3)


def _mem_bwd(mem, gm, dk, dv, wkv_f):
    def body(mem_ref, gm_ref, dk_ref, dv_ref, w_ref, dw_ref, dwb_ref, dgm_ref):
        mem_v = mem_ref[...]
        m, rm = _rms(mem_v, gm_ref[...])
        mt = m.T.astype(BF16)
        dm = jnp.zeros_like(mem_v)
        for j in range(N_CHIP):
            src = dk_ref if j < 2 else dv_ref
            dkv = src[:, pl.ds(KV_BLK * (j % 2), KV_BLK)].astype(BF16)
            dw = jnp.dot(mt, dkv, preferred_element_type=F32)
            dw_ref[j] = dw
            dwb_ref[j] = dw.astype(BF16)
            dm = dm + _bdot_nt(dkv, w_ref[j])
        dgm_ref[...] = jnp.sum(dm * mem_v * rm, axis=0, keepdims=True)

    return pl.pallas_call(
        body, out_shape=[jax.ShapeDtypeStruct((N_CHIP, D, KV_BLK), F32), jax.ShapeDtypeStruct((N_CHIP, D, KV_BLK), BF16),
                         jax.ShapeDtypeStruct((1, D), F32)],
        compiler_params=_cp(None, VMEM_LIMIT), name="mem_bwd")(mem, gm, dk, dv, wkv_f)


def _mixer_bwd(proj, dmix, cw8, lng, lnb, wc, wct, bsb, win_f, x, dx1, g1, tm=256):
    t = proj.shape[0]
    nt = t // tm
    nch = tm // CH
    hb = 16
    pair = 2 * CH

    def body(p_ref, pgc_ref, pxa_ref, dm_ref, cw_ref, lng_ref, lnb_ref, wc_ref, wct_ref, bsb_ref, w_ref, x_ref,
             dx1_ref, g1_ref, dp_ref, dcw_ref, dlng_ref, dlnb_ref, dwc_ref, dbs_ref, gx_ref, dg1_ref,
             next_ref, dh_ref):
        i = pl.program_id(0)

        @pl.when(i == 0)
        def _():
            next_ref[...] = jnp.zeros_like(next_ref)
            dcw_ref[...] = jnp.zeros_like(dcw_ref)
            dlng_ref[...] = jnp.zeros_like(dlng_ref)
            dlnb_ref[...] = jnp.zeros_like(dlnb_ref)
            dwc_ref[...] = jnp.zeros_like(dwc_ref)
            dbs_ref[...] = jnp.zeros_like(dbs_ref)
            dg1_ref[...] = jnp.zeros_like(dg1_ref)

        first_tile = i == nt - 1
        rows = lax.broadcasted_iota(jnp.int32, (tm, CH), 0)
        ones8 = jnp.ones((8, CH), BF16)
        for s in range(HEADS):
            cs = pl.ds(CH * s, CH)

            def slab(k):
                return p_ref[:, pl.ds(k * SLAB + CH * s, CH)].astype(F32)

            gb, gc, xa, za = slab(0), slab(1), slab(2), slab(3)
            da = dm_ref[:, cs].astype(F32)
            cx = gc * xa
            cxp = pgc_ref[:, cs].astype(F32) * pxa_ref[:, cs].astype(F32)
            cxp = jnp.where(first_tile, jnp.zeros_like(cxp), cxp)
            p6 = jnp.broadcast_to(cxp[hb - 2:hb - 1, :], (tm, CH))
            p7 = jnp.broadcast_to(cxp[hb - 1:hb, :], (tm, CH))
            c1 = jnp.where(rows == 0, p7, pltpu.roll(cx, 1, 0))
            c2 = jnp.where(rows == 0, p6, jnp.where(rows == 1, p7, pltpu.roll(cx, 2, 0)))
            w0, w1, w2 = cw_ref[0:1, cs], cw_ref[1:2, cs], cw_ref[2:3, cs]
            cv = w0 * c2 + w1 * c1 + w2 * cx
            sg = _sigmoid(za)
            sa = za * sg
            dcv = da * gb * sa
            dp_ref[:, pl.ds(0 * SLAB + CH * s, CH)] = (da * cv * sa).astype(BF16)
            dp_ref[:, pl.ds(3 * SLAB + CH * s, CH)] = (da * gb * cv * (sg * (1.0 + za * (1.0 - sg)))).astype(BF16)
            n0 = jnp.broadcast_to(next_ref[0:1, cs], (tm, CH))
            n1 = jnp.broadcast_to(next_ref[1:2, cs], (tm, CH))
            u1 = jnp.where(rows == tm - 1, n0, pltpu.roll(dcv, tm - 1, 0))
            u2 = jnp.where(rows == tm - 2, n0, jnp.where(rows == tm - 1, n1, pltpu.roll(dcv, tm - 2, 0)))
            next_ref[:, cs] = dcv[0:8, :]
            dcx = w2 * dcv + w1 * u1 + w0 * u2
            dp_ref[:, pl.ds(1 * SLAB + CH * s, CH)] = (dcx * xa).astype(BF16)
            dp_ref[:, pl.ds(2 * SLAB + CH * s, CH)] = (dcx * gc).astype(BF16)
            dcw_ref[0:1, cs] += jnp.sum(dcv * c2, axis=0, keepdims=True)
            dcw_ref[1:2, cs] += jnp.sum(dcv * c1, axis=0, keepdims=True)
            dcw_ref[2:3, cs] += jnp.sum(dcv * cx, axis=0, keepdims=True)

            u, v, zb = slab(4), slab(5), slab(6)
            db = dm_ref[:, pl.ds(SLAB + CH * s, CH)].astype(F32)
            ug, ugrad = _gelu_parts(u)
            vg, vgrad = _gelu_parts(v)
            dlt = vg - jnp.mean(vg, axis=-1, keepdims=True)
            rstd = lax.rsqrt(jnp.mean(dlt * dlt, axis=-1, keepdims=True) + EPS)
            vhat = dlt * rstd
            lg = lng_ref[:, cs]
            vn = (vhat * lg + lnb_ref[:, cs]).astype(BF16)
            sgb = _sigmoid(zb)
            szb = zb * sgb
            sps, dvns = [], []
            dbs = jnp.zeros((8, CH), F32)
            dwc = jnp.zeros((CH, CH), F32)
            for c in range(nch):
                rs = slice(CH * c, CH * (c + 1))
                sp = jnp.dot(wc_ref[s], vn[rs], preferred_element_type=F32) + bsb_ref[s]
                dsp = (db[rs] * ug[rs] * szb[rs]).astype(BF16)
                dbs = dbs + lax.dot_general(ones8, dsp, (((1,), (1,)), ((), ())), preferred_element_type=F32)
                dwc = dwc + lax.dot_general(dsp, vn[rs], (((1,), (1,)), ((), ())), preferred_element_type=F32)
                dvns.append(jnp.dot(wct_ref[s], dsp, preferred_element_type=F32))
                sps.append(sp)
            sp = jnp.concatenate(sps, axis=0)
            dvn = jnp.concatenate(dvns, axis=0)
            dbs_ref[:, cs] += dbs
            dwc_ref[s] += dwc
            dlng_ref[:, cs] += jnp.sum(dvn * vhat, axis=0, keepdims=True)
            dlnb_ref[:, cs] += jnp.sum(dvn, axis=0, keepdims=True)
            dvhat = dvn * lg
            dvg = rstd * (dvhat - jnp.mean(dvhat, axis=-1, keepdims=True)
                          - vhat * jnp.mean(dvhat * vhat, axis=-1, keepdims=True))
            dp_ref[:, pl.ds(4 * SLAB + CH * s, CH)] = (db * sp * szb * ugrad).astype(BF16)
            dp_ref[:, pl.ds(5 * SLAB + CH * s, CH)] = (dvg * vgrad).astype(BF16)
            dp_ref[:, pl.ds(6 * SLAB + CH * s, CH)] = (db * ug * sp * (sgb * (1.0 + zb * (1.0 - sgb)))).astype(BF16)

            if s % 2 == 1:
                part = None
                for k in range(N_SLAB):
                    col = k * SLAB + pair * (s // 2)
                    blk, off = divmod(col, IN_BLK)
                    term = lax.dot_general(dp_ref[:, pl.ds(col, pair)], w_ref[blk, :, pl.ds(off, pair)],
                                           (((1,), (1,)), ((), ())), preferred_element_type=F32)
                    part = term if part is None else part + term
                if s == 1:
                    dh_ref[...] = part
                else:
                    dh_ref[...] += part

        xv = x_ref[...]
        r = lax.rsqrt(jnp.mean(xv * xv, axis=-1, keepdims=True) + EPS)
        dxn, dg = _rms_bwd(dh_ref[...], xv, r, g1_ref[...])
        gx_ref[...] = dx1_ref[...].astype(F32) + dxn
        dg1_ref[0:1, :] += dg

        @pl.when(i == nt - 1)
        def _():
            tril = lax.broadcasted_iota(jnp.int32, (CH, CH), 0) >= lax.broadcasted_iota(jnp.int32, (CH, CH), 1)
            for s in range(HEADS):
                dwc_ref[s] = jnp.where(tril, dwc_ref[s], 0.0)

    rev = lambda i: nt - 1 - i
    halo = lambda col: pl.BlockSpec((hb, SLAB), lambda i: (jnp.maximum(rev(i) * (tm // hb) - 1, 0), col))
    tok = lambda w: pl.BlockSpec((tm, w), lambda i: (rev(i), 0))
    return pl.pallas_call(
        body, grid=(nt,),
        in_specs=[tok(IN_DIM), halo(1), halo(2), tok(MIX), _full((8, D)), _full((1, D)), _full((1, D)),
                  _full((HEADS, CH, CH)), _full((HEADS, CH, CH)), _full((HEADS, CH, CH)),
                  _full((N_CHIP, D, IN_BLK), 1), tok(D), tok(D), _full((1, D))],
        out_specs=[tok(IN_DIM), _full((8, D)), _full((1, D)), _full((1, D)), _full((HEADS, CH, CH)), _full((8, D)),
                   tok(D), _full((8, D))],
        out_shape=[jax.ShapeDtypeStruct((t, IN_DIM), BF16), jax.ShapeDtypeStruct((8, D), F32),
                   jax.ShapeDtypeStruct((1, D), F32), jax.ShapeDtypeStruct((1, D), F32),
                   jax.ShapeDtypeStruct((HEADS, CH, CH), F32), jax.ShapeDtypeStruct((8, D), F32),
                   jax.ShapeDtypeStruct((t, D), F32), jax.ShapeDtypeStruct((8, D), F32)],
        scratch_shapes=[pltpu.VMEM((8, D), F32), pltpu.VMEM((tm, D), F32)],
        compiler_params=_cp(("arbitrary",), VMEM_LIMIT), name="mixer_bwd")(
            proj, proj, proj, dmix, cw8, lng, lnb, wc, wct, bsb, win_f, x, dx1, g1)


def _grad_matmul(at, b, after, *, by_cols, name, tk=1024):
    m, t = at.shape
    n = b.shape[1]
    nk = t // tk
    nj = N_CHIP if by_cols else 1
    bn = n // nj

    def body(a_ref, b_ref, after_ref, o_ref, ob_ref):
        kk = pl.program_id(1)
        part = jnp.dot(a_ref[...], b_ref[...], preferred_element_type=F32)

        @pl.when(kk == 0)
        def _():
            o_ref[...] = part

        @pl.when(kk > 0)
        def _():
            o_ref[...] += part

        @pl.when(kk == nk - 1)
        def _():
            ob_ref[...] = o_ref[...].astype(BF16)

    a_spec = pl.BlockSpec((m, tk), lambda j, k: (0, k))
    b_spec = pl.BlockSpec((tk, bn), lambda j, k: (k, j))
    o_spec = pl.BlockSpec((None, m, bn), lambda j, k: (j, 0, 0))
    o32, o16 = pl.pallas_call(
        body, grid=(nj, nk), in_specs=[a_spec, b_spec, ANY], out_specs=[o_spec, o_spec],
        out_shape=[jax.ShapeDtypeStruct((nj, m, bn), F32), jax.ShapeDtypeStruct((nj, m, bn), BF16)],
        compiler_params=_cp(("parallel", "arbitrary"), VMEM_LIMIT), name=name)(at, b, after)
    if by_cols:
        return o32, o16
    return o32.reshape(N_CHIP, m // N_CHIP, n), o16.reshape(N_CHIP, m // N_CHIP, n)


def _coords():
    x, y, c = lax.axis_index("x"), lax.axis_index("y"), lax.axis_index("c")
    chips = [(1 - x, y), (x, 1 - y), (1 - x, 1 - y)]
    return x, y, c, chips


def _pair_exchange(grads_b, smalls, name):
    ng, ns = len(grads_b), len(smalls)
    n = ng + ns

    def body(*refs):
        ins, outs, send, recv = refs[:n], refs[n:2 * n], refs[2 * n], refs[2 * n + 1]
        x, y, c, _ = _coords()
        cps = []
        for i in range(n):
            if i < ng:
                hr = ins[i].shape[1] // 2
                src = ins[i].at[pl.ds(0, N_CHIP), pl.ds((1 - c) * hr, hr)]
            else:
                hr = ins[i].shape[0] // 2
                src = ins[i].at[pl.ds((1 - c) * hr, hr)]
            cps.append(pltpu.make_async_remote_copy(
                src_ref=src, dst_ref=outs[i], send_sem=send.at[i], recv_sem=recv.at[i],
                device_id=(x, y, 1 - c), device_id_type=MESH))
        for cp in cps:
            cp.start()
        for cp in cps:
            cp.wait()

    out_shape = [jax.ShapeDtypeStruct((N_CHIP, g.shape[1] // 2, g.shape[2]), g.dtype) for g in grads_b]
    out_shape += [jax.ShapeDtypeStruct((s.shape[0] // 2, s.shape[1]), s.dtype) for s in smalls]
    return pl.pallas_call(
        body, out_shape=out_shape, in_specs=[ANY] * n, out_specs=[ANY] * n,
        scratch_shapes=[pltpu.SemaphoreType.DMA((n,)), pltpu.SemaphoreType.DMA((n,))],
        name=name)(*grads_b, *smalls)


def _pair_sum(c_idx, grads, recvd, smalls, smalls_recvd, name):
    ng, ns = len(grads), len(smalls)
    halves = [g.shape[1] // 2 for g in grads]

    def body(c_ref, *refs):
        g_in, r_in = refs[:ng], refs[ng:2 * ng]
        s_in, sr_in = refs[2 * ng:2 * ng + ns], refs[2 * ng + ns:2 * ng + 2 * ns]
        o = refs[2 * ng + 2 * ns:]
        for i in range(ng):
            tot = g_in[i][...] + r_in[i][...].astype(F32)
            o[i][...] = tot
            o[ng + i][...] = tot.astype(BF16)
        for i in range(ns):
            o[2 * ng + i][...] = s_in[i][...] + sr_in[i][...]

    in_specs = [pl.BlockSpec((None, None, halves[i], g.shape[2]), lambda b, c: (b, c[0], 0, 0)) for i, g in enumerate(grads)]
    in_specs += [pl.BlockSpec((None, halves[i], g.shape[2]), lambda b, c: (b, 0, 0)) for i, g in enumerate(grads)]
    in_specs += [pl.BlockSpec((None, s.shape[0] // 2, s.shape[1]), lambda b, c: (c[0], 0, 0)) for s in smalls]
    in_specs += [pl.BlockSpec((s.shape[0] // 2, s.shape[1]), lambda b, c: (0, 0)) for s in smalls]
    blk = [pl.BlockSpec((None, halves[i], g.shape[2]), lambda b, c: (b, 0, 0)) for i, g in enumerate(grads)]
    out_specs = blk + blk + [pl.BlockSpec((s.shape[0] // 2, s.shape[1]), lambda b, c: (0, 0)) for s in smalls]
    out_shape = [jax.ShapeDtypeStruct((N_CHIP, halves[i], g.shape[2]), F32) for i, g in enumerate(grads)]
    out_shape += [jax.ShapeDtypeStruct((N_CHIP, halves[i], g.shape[2]), BF16) for i, g in enumerate(grads)]
    out_shape += [jax.ShapeDtypeStruct((s.shape[0] // 2, s.shape[1]), F32) for s in smalls]
    grads4 = [g.reshape(N_CHIP, 2, halves[i], g.shape[2]) for i, g in enumerate(grads)]
    smalls3 = [s.reshape(2, s.shape[0] // 2, s.shape[1]) for s in smalls]
    return pl.pallas_call(
        body, out_shape=out_shape,
        grid_spec=pltpu.PrefetchScalarGridSpec(num_scalar_prefetch=1, grid=(N_CHIP,), in_specs=in_specs, out_specs=out_specs),
        compiler_params=_cp(("arbitrary",), VMEM_LIMIT), name=name)(c_idx, *grads4, *recvd, *smalls3, *smalls_recvd)


_HBM = pl.BlockSpec(memory_space=pltpu.HBM)
_SEM = pl.BlockSpec(memory_space=pltpu.SEMAPHORE)


def _split_copies(ins, lands, ng, send, recv, arriving):
    x, y, c, chips = _coords()
    b = 2 * x + y
    copies = []
    for i in range(len(ins)):
        for k in range(3):
            blk = 2 * chips[k][0] + chips[k][1]
            src, dst, got = (ins[i].at[blk], lands[i].at[k], lands[i].at[k]) if i < ng else (ins[i], lands[i].at[b], lands[i].at[blk])
            sems = dict(send_sem=send.at[3 * i + k], recv_sem=recv.at[3 * i + k], device_id=(*chips[k], c), device_id_type=MESH)
            if arriving:
                copies.append(pltpu.make_async_remote_copy(src_ref=got, dst_ref=got, **sems))
            else:
                copies.append(pltpu.make_async_remote_copy(src_ref=src, dst_ref=dst, **sems))
    return copies


def _exchange_begin(sums_b, smalls, name):
    ng, n = len(sums_b), len(sums_b) + len(smalls)
    srcs = list(sums_b) + list(smalls)
    lands = [lax.empty((3,) + g.shape[1:], g.dtype) for g in sums_b] + [lax.empty((N_CHIP,) + s.shape, s.dtype) for s in smalls]

    def body(*refs):
        ins, land_refs = refs[:n], refs[n:2 * n]
        send, recv = refs[2 * n], refs[2 * n + 1]
        token = refs[4 * n + 2]
        for cp in _split_copies(ins, land_refs, ng, send, recv, False):
            cp.start()
        token[...] = jnp.zeros_like(token)

    hbm = lambda a: pltpu.HBM(a.shape, a.dtype)
    outs = pl.pallas_call(
        body, name=name,
        out_shape=(pltpu.SemaphoreType.DMA((3 * n,)), pltpu.SemaphoreType.DMA((3 * n,)), *[hbm(a) for a in srcs + lands],
                   jax.ShapeDtypeStruct((8, 128), F32)),
        in_specs=[_HBM] * (2 * n), out_specs=(_SEM, _SEM, *[_HBM] * (2 * n), pl.BlockSpec(memory_space=pltpu.VMEM)),
        input_output_aliases={i: 2 + i for i in range(2 * n)},
        compiler_params=pltpu.CompilerParams(has_side_effects=pltpu.SideEffectType.DATAFLOW_SIDE_EFFECTING),
    )(*[pltpu.with_memory_space_constraint(a, pltpu.HBM) for a in srcs + lands])
    return outs[0], outs[1], list(outs[2:2 + n]), list(outs[2 + n:2 + 2 * n]), outs[2 + 2 * n]


def _exchange_end(send, recv, srcs, lands, ng, after, name):
    n = len(srcs)
    after = list(after)

    def body(*refs):
        ins, land_refs = refs[:n], refs[n:2 * n]
        send_ref, recv_ref = refs[2 * n], refs[2 * n + 1]
        for cp in _split_copies(ins, land_refs, ng, send_ref, recv_ref, False):
            cp.wait_send()
        for cp in _split_copies(ins, land_refs, ng, send_ref, recv_ref, True):
            cp.wait_recv()

    hbm = lambda a: pltpu.HBM(a.shape, a.dtype)
    outs = pl.pallas_call(
        body, name=name, out_shape=tuple(hbm(a) for a in list(srcs) + list(lands)),
        in_specs=[_HBM] * (2 * n) + [_SEM, _SEM] + [ANY] * len(after), out_specs=tuple([_HBM] * (2 * n)),
        input_output_aliases={i: i for i in range(2 * n)},
        compiler_params=pltpu.CompilerParams(has_side_effects=pltpu.SideEffectType.DATAFLOW_SIDE_EFFECTING),
    )(*srcs, *lands, send, recv, *after)
    return list(outs[n:])


def _chip_sum(bc_idx, sums, recvd, smalls_slots, smalls_own, name, steps=4):
    ng, ns = len(sums), len(smalls_slots)

    def body(bc_ref, *refs):
        own, rx = refs[:ng], refs[ng:2 * ng]
        sl = refs[2 * ng:2 * ng + ns]
        sl_own = refs[2 * ng + ns:2 * ng + 2 * ns]
        o = refs[2 * ng + 2 * ns:]
        for i in range(ng):
            tot = own[i][...]
            for j in range(3):
                tot = tot + rx[i][j].astype(F32)
            o[i][...] = tot
        for i in range(ns):
            term = [jnp.where(bc_ref[0] == kk, sl_own[i][...], sl[i][kk]) for kk in range(N_CHIP)]
            o[ng + i][...] = ((term[0] + term[1]) + term[2]) + term[3]

    def rows(g):
        return g.shape[1] // steps

    in_specs = [pl.BlockSpec((None, rows(g), g.shape[2]), lambda r, bc: (bc[0], r, 0)) for g in sums]
    in_specs += [pl.BlockSpec((3, rows(g), g.shape[2]), lambda r, bc: (0, r, 0)) for g in sums]
    in_specs += [pl.BlockSpec(s.shape, lambda r, bc: (0, 0, 0)) for s in smalls_slots]
    in_specs += [pl.BlockSpec(s.shape[1:], lambda r, bc: (0, 0)) for s in smalls_slots]
    out_specs = [pl.BlockSpec((rows(g), g.shape[2]), lambda r, bc: (bc[1] * steps + r, 0)) for g in sums]
    out_specs += [pl.BlockSpec(s.shape[1:], lambda r, bc: (bc[1], 0)) for s in smalls_slots]
    out_shape = [jax.ShapeDtypeStruct((2 * g.shape[1], g.shape[2]), F32) for g in sums]
    out_shape += [jax.ShapeDtypeStruct((2 * s.shape[1], s.shape[2]), F32) for s in smalls_slots]
    return pl.pallas_call(
        body, out_shape=out_shape,
        grid_spec=pltpu.PrefetchScalarGridSpec(num_scalar_prefetch=1, grid=(steps,), in_specs=in_specs, out_specs=out_specs),
        compiler_params=_cp(("arbitrary",), VMEM_LIMIT), name=name)(bc_idx, *sums, *recvd, *smalls_slots, *smalls_own)


def _pair_gather(arrs, name):
    n = len(arrs)

    def body(*refs):
        outs = refs[n:2 * n]
        send, recv = refs[2 * n:]
        x, y, c, _ = _coords()
        cps = []
        for i in range(n):
            hr = outs[i].shape[0] // 2
            mine = outs[i].at[pl.ds(c * hr, hr)]
            cps.append(pltpu.make_async_remote_copy(
                src_ref=mine, dst_ref=mine, send_sem=send.at[i], recv_sem=recv.at[i],
                device_id=(x, y, 1 - c), device_id_type=MESH))
        for cp in cps:
            cp.start()
        for i in range(n):
            hr = outs[i].shape[0] // 2
            theirs = outs[i].at[pl.ds((1 - c) * hr, hr)]
            pltpu.make_async_remote_copy(
                src_ref=theirs, dst_ref=theirs, send_sem=send.at[i], recv_sem=recv.at[i],
                device_id=(x, y, 1 - c), device_id_type=MESH).wait_recv()
        for cp in cps:
            cp.wait_send()

    return list(pl.pallas_call(
        body, out_shape=[jax.ShapeDtypeStruct(a.shape, a.dtype) for a in arrs], in_specs=[ANY] * n, out_specs=[ANY] * n,
        input_output_aliases={i: i for i in range(n)},
        scratch_shapes=[pltpu.SemaphoreType.DMA((n,)), pltpu.SemaphoreType.DMA((n,))],
        name=name)(*arrs))


def _adamw_math(w, g, m, v):
    m2 = ADAM_B1 * m + (1.0 - ADAM_B1) * g
    v2 = ADAM_B2 * v + (1.0 - ADAM_B2) * (g * g)
    m_hat = m2 / (1.0 - ADAM_B1 ** ADAM_STEP)
    v_hat = v2 / (1.0 - ADAM_B2 ** ADAM_STEP)
    delta = -ADAM_LR * (m_hat / (jnp.sqrt(v_hat) + ADAM_EPS) + ADAM_WD * w)
    return delta, m2, v2


def _adamw_big(w, g, m, v, name, steps=8):
    r, c = w.shape

    def body(w_ref, g_ref, m_ref, v_ref, d_ref, m2_ref, v2_ref, g2_ref):
        gv = g_ref[...]
        d_ref[...], m2_ref[...], v2_ref[...] = _adamw_math(w_ref[...], gv, m_ref[...], v_ref[...])
        g2_ref[...] = gv

    spec = pl.BlockSpec((r // steps, c), lambda i: (i, 0))
    return pl.pallas_call(
        body, grid=(steps,), in_specs=[spec] * 4, out_specs=[spec] * 4,
        out_shape=[jax.ShapeDtypeStruct((r, c), F32)] * 4,
        compiler_params=_cp(("parallel",), VMEM_LIMIT), name=name)(w, g, m, v)


def _adamw_small(groups):
    n = len(groups)

    def body(*refs):
        for i in range(n):
            w_ref, g_ref, m_ref, v_ref = refs[4 * i:4 * i + 4]
            d_ref, m2_ref, v2_ref = refs[4 * n + 3 * i:4 * n + 3 * i + 3]
            d_ref[...], m2_ref[...], v2_ref[...] = _adamw_math(w_ref[...], g_ref[...], m_ref[...], v_ref[...])

    flat = [a for grp in groups for a in grp]
    out_shape = [jax.ShapeDtypeStruct(grp[0].shape, F32) for grp in groups for _ in range(3)]
    outs = pl.pallas_call(body, out_shape=out_shape, name="adamw_small")(*flat)
    return [tuple(outs[3 * i:3 * i + 3]) for i in range(n)]


def kernel(x, mem, norm_mix_g, w_in, conv_w, gm_ln_g, gm_ln_b, gm_ws, gm_bs, w_out, norm_x_g, norm_mem_g, w_q, w_kv, w_xo, norm_final_g, loss_target, m_norm_mix_g, m_w_in, m_conv_w, m_gm_ln_g, m_gm_ln_b, m_gm_ws, m_gm_bs, m_w_out, m_norm_x_g, m_norm_mem_g, m_w_q, m_w_kv, m_w_xo, m_norm_final_g, v_norm_mix_g, v_w_in, v_conv_w, v_gm_ln_g, v_gm_ln_b, v_gm_ws, v_gm_bs, v_w_out, v_norm_x_g, v_norm_mem_g, v_w_q, v_w_kv, v_w_xo, v_norm_final_g):
    t = x.shape[1]
    xi = lax.axis_index("x")
    yi = lax.axis_index("y")
    ci = lax.axis_index("c")
    b_idx = jnp.reshape(2 * xi + yi, (1,)).astype(jnp.int32)
    c_idx = jnp.reshape(ci, (1,)).astype(jnp.int32)

    x2d, mem2d, tgt = x[0], mem[0], loss_target[0]
    big = [w_in[0], w_out[0], w_q[0], w_kv[0], w_xo[0]]
    big_m = [m_w_in[0], m_w_out[0], m_w_q[0], m_w_kv[0], m_w_xo[0]]
    big_v = [v_w_in[0], v_w_out[0], v_w_q[0], v_w_kv[0], v_w_xo[0]]
    g3 = norm_final_g.reshape(1, D)

    def pad8(a):
        return jnp.pad(a, ((0, 8 - a.shape[0]), (0, 0)))

    own_blocks = _cast_shards(b_idx, big)

    tril = jnp.tril(jnp.ones((CH, CH), bool))
    wc32 = jnp.where(tril[None], gm_ws[0], 0.0)
    wc = wc32.astype(BF16)
    wct = jnp.swapaxes(wc32, 1, 2).astype(BF16)
    bsb = jnp.broadcast_to(gm_bs[0][:, :, None], (HEADS, CH, CH))

    blk = 2 * xi + yi
    order = jnp.stack([blk, blk ^ 2, blk ^ 1, blk ^ 3]).astype(jnp.int32)
    proj, ht, win_f, cw8, (wout_f,) = _proj_gather(
        order, x2d, norm_mix_g, own_blocks[0], pad8(conv_w[0]), [own_blocks[1]])
    mixin, mixt, (wq_f, wkv_f, wxo_f) = _mixer_fwd(proj, cw8, gm_ln_g, gm_ln_b, wc, bsb, own_blocks[2:])
    wout2, wq2, wxo2 = wout_f.reshape(MIX, D), wq_f.reshape(D, D), wxo_f.reshape(D, D)
    k, v, mt = _mem_fwd(mem2d, norm_mem_g, wkv_f)
    del mt

    (loss_tile, dmix, dx1b, h2t, dq, ot, dx2b, dk, dv, dg2, dg3) = _tail(
        x2d, tgt, mixin, wout2, wq2, wxo2, k, v, norm_x_g, g3)
    dwkv, dwkv_b, dgm = _mem_bwd(mem2d, norm_mem_g, dk, dv, wkv_f)
    dproj, dcw, dlng, dlnb, dwc, dbs8, grad_x, dg1 = _mixer_bwd(
        proj, dmix, cw8, gm_ln_g, gm_ln_b, wc, wct, bsb, win_f, x2d, dx1b, norm_mix_g)

    bc_idx = jnp.concatenate([b_idx, c_idx])
    dwin, dwin_b = _grad_matmul(ht, dproj, dgm, by_cols=True, name="grad_w_in", tk=2048)
    zero = jnp.zeros((1, D), F32)
    loss_row = jnp.broadcast_to(loss_tile[0:1, 0:1], (1, D))
    sv = jnp.concatenate([dg1[0:1], dg2, dgm, dg3, dlng, dlnb, dbs8[0:1], loss_row, dcw], axis=0)
    sw = dwc.reshape(HEADS * CH, CH)
    rx1b = _pair_exchange([dwin_b], [sv, sw], "pair_exchange_b")
    ps_b = _pair_sum(c_idx, [dwin], rx1b[:1], [sv, sw], rx1b[1:], "pair_sum_b")
    sums_b, sums_b_b, psmall = list(ps_b[:1]), list(ps_b[1:2]), list(ps_b[2:])
    send_b, recv_b, src_b, land_b, token_b = _exchange_begin(sums_b_b, psmall, "exchange_b_begin")

    dwxo, dwxo_b = _grad_matmul(ot, dx2b, token_b, by_cols=False, name="grad_w_xo", tk=2048)
    dwq, dwq_b = _grad_matmul(h2t, dq, token_b, by_cols=False, name="grad_w_q", tk=2048)
    dwout, dwout_b = _grad_matmul(mixt, dx1b, token_b, by_cols=False, name="grad_w_out")
    rx1a = _pair_exchange([dwout_b, dwq_b, dwkv_b, dwxo_b], [], "pair_exchange_a")
    ps_a = _pair_sum(c_idx, [dwout, dwq, dwkv, dwxo], rx1a, [], [], "pair_sum_a")
    sums_a, sums_a_b = list(ps_a[:4]), list(ps_a[4:8])
    send_a, recv_a, src_a, land_a, token_a = _exchange_begin(sums_a_b, [], "exchange_a_begin")

    rx2b = _exchange_end(send_b, recv_b, src_b, land_b, 1, [token_a], "exchange_b_end")
    red_b = _chip_sum(bc_idx, sums_b, rx2b[:1], rx2b[1:], psmall, "chip_sum_b")
    gwin, svf, swf = _pair_gather(red_b, "pair_gather_b")
    out_b = _adamw_big(big[0], gwin, big_m[0], big_v[0], "adamw_w_in")

    def vec_pack(a1, a2, am, a3, lg, lb, bs):
        return jnp.concatenate([a1, a2, am, a3.reshape(1, D), lg, lb, bs.reshape(1, D), zero], axis=0)

    wv = vec_pack(norm_mix_g, norm_x_g, norm_mem_g, norm_final_g, gm_ln_g, gm_ln_b, gm_bs)
    mv = vec_pack(m_norm_mix_g, m_norm_x_g, m_norm_mem_g, m_norm_final_g, m_gm_ln_g, m_gm_ln_b, m_gm_bs)
    vv = vec_pack(v_norm_mix_g, v_norm_x_g, v_norm_mem_g, v_norm_final_g, v_gm_ln_g, v_gm_ln_b, v_gm_bs)
    loss = svf[7, 0]
    gcw = lax.dynamic_slice_in_dim(svf[8:16], blk * (D // N_CHIP), D // N_CHIP, axis=1)
    gws = swf
    gv = svf[0:8]
    (dv_, mv_, vv_), (dc_, mc_, vc_), (dws_, mws_, vws_) = _adamw_small([
        (wv, gv, mv, vv),
        (pad8(conv_w[0]), gcw, pad8(m_conv_w[0]), pad8(v_conv_w[0])),
        (gm_ws.reshape(HEADS * CH, CH), gws, m_gm_ws.reshape(HEADS * CH, CH), v_gm_ws.reshape(HEADS * CH, CH))])

    rx2a = _exchange_end(send_a, recv_a, src_a, land_a, 4, [out_b[0], dv_], "exchange_a_end")
    red_a = _chip_sum(bc_idx, sums_a, rx2a, [], [], "chip_sum_a")
    g_a = _pair_gather(red_a, "pair_gather_a")
    names = ["w_out", "w_q", "w_kv", "w_xo"]
    out_a = [_adamw_big(w, g, m, v, "adamw_" + nm) for w, g, m, v, nm in zip(big[1:], g_a, big_m[1:], big_v[1:], names)]
    big_out = [out_b] + out_a

    def unpack(vecs, cw, ws, bigs):
        r = lambda i: vecs[i:i + 1]
        return [r(0), bigs[0][None], cw[0:3][None], r(4), r(5), ws.reshape(1, HEADS, CH, CH), vecs[6].reshape(1, HEADS, CH),
                bigs[1][None], r(1), r(2), bigs[2][None], bigs[3][None], bigs[4][None], vecs[3]]

    grads_out = unpack(gv, gcw, gws, [o[3] for o in big_out])
    delta_out = unpack(dv_, dc_, dws_, [o[0] for o in big_out])
    m_out = unpack(mv_, mc_, mws_, [o[1] for o in big_out])
    v_out = unpack(vv_, vc_, vws_, [o[2] for o in big_out])
    return (loss, grad_x[None], *grads_out, *delta_out, *m_out, *v_out)
```

```python
import functools
import math

import jax
import jax.numpy as jnp
from jax import lax
from jax.experimental import pallas as pl
from jax.experimental.pallas import tpu as pltpu

F32 = jnp.float32
BF16 = jnp.bfloat16
MESH = pl.DeviceIdType.MESH

D = 1024
SLAB = 1024
N_SLAB = 7
IN_DIM = N_SLAB * SLAB
MIX = 2 * SLAB
HEADS = 8
CH = 128
XH = 4
XD = D // XH
EPS = 1e-6
GELU_C = math.sqrt(2.0 / math.pi)
GELU_A = 0.044715
N_CHIP = 4
IN_BLK = IN_DIM // N_CHIP
KV_BLK = 2 * D // N_CHIP

ADAM_LR, ADAM_B1, ADAM_B2, ADAM_EPS, ADAM_WD, ADAM_STEP = 0.001, 0.9, 0.999, 1e-08, 0.01, 10

VMEM_LIMIT = 60 * 1024 * 1024


def _cp(sem=None, vmem=None):
    return pltpu.CompilerParams(dimension_semantics=sem, vmem_limit_bytes=vmem)


def _full(shape, buffers=None):
    n = len(shape)
    if buffers is None:
        return pl.BlockSpec(shape, lambda *_: (0,) * n)
    return pl.BlockSpec(shape, lambda *_: (0,) * n, pipeline_mode=pl.Buffered(buffers))


ANY = pl.BlockSpec(memory_space=pl.ANY)


def _bdot(a, b):
    return jnp.dot(a.astype(BF16), b.astype(BF16), preferred_element_type=F32)


def _bdot_nt(a, b):
    return lax.dot_general(a.astype(BF16), b.astype(BF16), (((1,), (1,)), ((), ())), preferred_element_type=F32)


def _bdot_tn(a, b):
    return lax.dot_general(a.astype(BF16), b.astype(BF16), (((0,), (0,)), ((), ())), preferred_element_type=F32)


def _rms(x, g):
    r = lax.rsqrt(jnp.mean(x * x, axis=-1, keepdims=True) + EPS)
    return x * r * g, r


def _rms_bwd(dy, x, r, g):
    gdy = dy * g
    dx = r * gdy - x * (r * r * r) * jnp.mean(x * gdy, axis=-1, keepdims=True)
    dg = jnp.sum(dy * x * r, axis=0, keepdims=True)
    return dx, dg


def _gelu_parts(x):
    x2 = x * x
    t = jnp.tanh(GELU_C * (x + GELU_A * x * x2))
    val = 0.5 * x * (1.0 + t)
    grad = 0.5 * (1.0 + t) + 0.5 * x * (1.0 - t * t) * (GELU_C * (1.0 + 3.0 * GELU_A * x2))
    return val, grad


def _gelu(x):
    return 0.5 * x * (1.0 + jnp.tanh(GELU_C * (x + GELU_A * x * x * x)))


def _sigmoid(z):
    return 1.0 / (1.0 + jnp.exp(-z))


def _cast_shards(b_idx, arrs):
    n = len(arrs)
    steps = 8

    def body(b_ref, *refs):
        for i in range(n):
            refs[n + i][...] = refs[i][...].astype(BF16)

    in_specs = [pl.BlockSpec((a.shape[0] // steps, a.shape[1]), lambda i, b: (i, 0)) for a in arrs]
    out_specs = [pl.BlockSpec((None, a.shape[0] // steps, a.shape[1]), lambda i, b: (b[0], i, 0)) for a in arrs]
    return pl.pallas_call(
        body, out_shape=[jax.ShapeDtypeStruct((N_CHIP,) + a.shape, BF16) for a in arrs],
        grid_spec=pltpu.PrefetchScalarGridSpec(num_scalar_prefetch=1, grid=(steps,), in_specs=in_specs, out_specs=out_specs),
        compiler_params=_cp(("arbitrary",)), name="cast_shards")(b_idx, *arrs)


def _proj_gather(order, x, g, win_own, cw8s, more, tm=1024):
    t = x.shape[0]
    ni = t // tm
    nm = len(more)

    def body(*refs):
        order_ref, x_ref, g_ref, win_in, cw_in = refs[:5]
        o_ref, hb_ref, win_f, cw_out = refs[5 + nm:9 + nm]
        more_out = refs[9 + nm:9 + 2 * nm]
        hbuf, wv, cw_s, cw_r, loc = refs[9 + 2 * nm:14 + 2 * nm]
        g_in = _Gather([win_f], *refs[14 + 2 * nm:18 + 2 * nm])
        g_more = _Gather(more_out, *refs[18 + 2 * nm:22 + 2 * nm])
        j, i = pl.program_id(0), pl.program_id(1)
        x, y, c, chips = _coords()
        b = 2 * x + y
        blks = [2 * chip[0] + chip[1] for chip in chips]

        def cw_cols(blk):
            return cw_out.at[:, pl.ds(blk * (D // N_CHIP), D // N_CHIP)]

        def cw_copy(k, blk):
            src = cw_in if blk is None else cw_cols(blk)
            return pltpu.make_async_remote_copy(src_ref=src, dst_ref=cw_cols(b if blk is None else blk), send_sem=cw_s.at[k],
                                                recv_sem=cw_r.at[k], device_id=(*chips[k], c), device_id_type=MESH)

        cw_local = pltpu.make_async_copy(cw_in, cw_cols(b), loc.at[1])

        def load(blk, slot):
            return pltpu.make_async_copy(win_f.at[blk], wv.at[slot], loc.at[2 + slot])

        @pl.when((j == 0) & (i == 0))
        def _():
            g_in.start()
            cw_local.start()
            for k in range(3):
                cw_copy(k, None).start()
            load(b, 0).start()
            load(b, 0).wait()

        @pl.when((j == 1) & (i == 0))
        def _():
            g_in.hop()
            g_more.start()
            g_in.near_ready()
            load(g_in.bx, 1).start()
            load(g_in.by, 0).start()
            load(g_in.bx, 1).wait()

        @pl.when((j == 2) & (i == 0))
        def _():
            load(g_in.by, 0).wait()
            g_in.far()
            g_in.far_ready()
            load(g_in.bd, 1).start()

        @pl.when((j == 3) & (i == 0))
        def _():
            load(g_in.bd, 1).wait()
            g_more.hop()

        rows = pl.ds(pl.multiple_of(i * tm, tm), tm)

        @pl.when(j == 0)
        def _():
            h, _ = _rms(x_ref[...], g_ref[...])
            hbuf[rows, :] = h.astype(BF16)
            hb_ref[...] = h.astype(BF16)

        @pl.when((j == N_CHIP - 1) & (i == ni - 1))
        def _():
            g_more.far()

        o_ref[...] = jnp.dot(hbuf[rows, :], wv[lax.rem(j, 2)], preferred_element_type=F32).astype(BF16)

        @pl.when((j == N_CHIP - 1) & (i == ni - 1))
        def _():
            for k in range(3):
                cw_copy(k, blks[k]).wait_recv()
            for k in range(3):
                cw_copy(k, None).wait_send()
            cw_local.wait()
            g_more.near_ready()
            g_more.far_ready()
            g_in.drain()
            g_more.drain()

    first = lambda j, i: jnp.where(j == 0, i, ni - 1)
    in_specs = [pl.BlockSpec((tm, D), lambda j, i, o: (first(j, i), 0)), pl.BlockSpec((1, D), lambda j, i, o: (0, 0)),
                ANY, ANY] + [ANY] * nm
    out_specs = [pl.BlockSpec((tm, IN_BLK), lambda j, i, o: (i, o[j])),
                 pl.BlockSpec((tm, D), lambda j, i, o: (first(j, i), 0)), ANY, ANY] + [ANY] * nm
    outs = pl.pallas_call(
        body, out_shape=[jax.ShapeDtypeStruct((t, IN_DIM), BF16), jax.ShapeDtypeStruct((t, D), BF16),
                         jax.ShapeDtypeStruct(win_own.shape, BF16), jax.ShapeDtypeStruct((8, D), F32)]
        + [jax.ShapeDtypeStruct(f.shape, f.dtype) for f in more],
        grid_spec=pltpu.PrefetchScalarGridSpec(
            num_scalar_prefetch=1, grid=(N_CHIP, ni), in_specs=in_specs, out_specs=out_specs,
            scratch_shapes=[pltpu.VMEM((t, D), BF16), pltpu.VMEM((2, D, IN_BLK), BF16)]
            + [pltpu.SemaphoreType.DMA((3,))] * 2 + [pltpu.SemaphoreType.DMA((4,))] + _gather_sems(1) + _gather_sems(nm)),
        input_output_aliases={3: 2, **{5 + w: 4 + w for w in range(nm)}},
        compiler_params=_cp(("arbitrary", "arbitrary"), VMEM_LIMIT), name="proj_gather")(order, x, g, win_own, cw8s, *more)
    return outs[0], outs[1], outs[2], outs[3], outs[4:]


class _Gather:
    def __init__(self, outs, ici_s, ici_r, d2d_s, d2d_r):
        x, y, c, _ = _coords()
        self.outs, self.c = outs, c
        self.sems = ici_s, ici_r, d2d_s, d2d_r
        self.b, self.bx, self.by, self.bd = 2 * x + y, 2 * (1 - x) + y, 2 * x + (1 - y), 2 * (1 - x) + (1 - y)
        self.xn, self.yn, self.sib = (1 - x, y, c), (x, 1 - y, c), (x, y, 1 - c)

    def piece(self, w, blk, hc, quarter=None):
        hr = self.outs[w].shape[1] // 2
        if quarter is None:
            return self.outs[w].at[blk, pl.ds(hc * hr, hr)]
        return self.outs[w].at[blk, pl.ds(hc * hr + quarter * (hr // 2), hr // 2)]

    def ici(self, w, k, ref, to):
        return pltpu.make_async_remote_copy(src_ref=ref, dst_ref=ref, send_sem=self.sems[0].at[w, k],
                                            recv_sem=self.sems[1].at[w, k], device_id=to, device_id_type=MESH)

    def d2d(self, w, k, ref):
        return pltpu.make_async_remote_copy(src_ref=ref, dst_ref=ref, send_sem=self.sems[2].at[w, k],
                                            recv_sem=self.sems[3].at[w, k], device_id=self.sib, device_id_type=MESH)

    def start(self):
        for w in range(len(self.outs)):
            mine = self.piece(w, self.b, self.c)
            self.ici(w, 0, mine, self.xn).start()
            self.ici(w, 1, mine, self.yn).start()

    def hop(self):
        c = self.c
        for w in range(len(self.outs)):
            self.ici(w, 0, self.piece(w, self.bx, c), self.xn).wait_recv()
            self.ici(w, 1, self.piece(w, self.by, c), self.yn).wait_recv()
            self.ici(w, 2, self.piece(w, self.bx, c, 0), self.yn).start()
            self.ici(w, 3, self.piece(w, self.by, c, 1), self.xn).start()
            self.d2d(w, 0, self.piece(w, self.bx, c)).start()
            self.d2d(w, 1, self.piece(w, self.by, c)).start()

    def near_ready(self):
        for w in range(len(self.outs)):
            self.d2d(w, 0, self.piece(w, self.bx, 1 - self.c)).wait_recv()
            self.d2d(w, 1, self.piece(w, self.by, 1 - self.c)).wait_recv()

    def far(self):
        c = self.c
        for w in range(len(self.outs)):
            self.ici(w, 2, self.piece(w, self.bd, c, 0), self.yn).wait_recv()
            self.ici(w, 3, self.piece(w, self.bd, c, 1), self.xn).wait_recv()
            self.d2d(w, 2, self.piece(w, self.bd, c, 0)).start()
            self.d2d(w, 3, self.piece(w, self.bd, c, 1)).start()

    def far_ready(self):
        for w in range(len(self.outs)):
            self.d2d(w, 2, self.piece(w, self.bd, 1 - self.c, 0)).wait_recv()
            self.d2d(w, 3, self.piece(w, self.bd, 1 - self.c, 1)).wait_recv()

    def drain(self):
        c = self.c
        for w in range(len(self.outs)):
            mine = self.piece(w, self.b, c)
            self.ici(w, 0, mine, self.xn).wait_send()
            self.ici(w, 1, mine, self.yn).wait_send()
            self.ici(w, 2, self.piece(w, self.bx, c, 0), self.yn).wait_send()
            self.ici(w, 3, self.piece(w, self.by, c, 1), self.xn).wait_send()
            self.d2d(w, 0, self.piece(w, self.bx, c)).wait_send()
            self.d2d(w, 1, self.piece(w, self.by, c)).wait_send()
            self.d2d(w, 2, self.piece(w, self.bd, c, 0)).wait_send()
            self.d2d(w, 3, self.piece(w, self.bd, c, 1)).wait_send()


def _gather_sems(nw):
    return [pltpu.SemaphoreType.DMA((max(nw, 1), 4))] * 4


def _mixer_fwd(proj, cw8, lng, lnb, wc, bsb, fulls, tm=256):
    t = proj.shape[0]
    nt = t // tm
    nch = tm // CH
    nw = len(fulls)

    def body(*refs):
        p_ref, cw_ref, lng_ref, lnb_ref, wc_ref, bsb_ref = refs[:6]
        mix_ref = refs[6 + nw]
        w_outs = refs[7 + nw:7 + 2 * nw]
        prev_ref = refs[7 + 2 * nw]
        gather = _Gather(w_outs, *refs[8 + 2 * nw:])

        @pl.when(pl.program_id(0) == 0)
        def _():
            gather.start()
            prev_ref[...] = jnp.zeros_like(prev_ref)

        @pl.when(pl.program_id(0) == nt // 2)
        def _():
            gather.hop()

        @pl.when(pl.program_id(0) == nt - 1)
        def _():
            gather.far()

        rows = lax.broadcasted_iota(jnp.int32, (tm, CH), 0)
        for s in range(HEADS):
            cs = pl.ds(CH * s, CH)

            def slab(k):
                return p_ref[:, pl.ds(k * SLAB + CH * s, CH)].astype(F32)

            gb, gc, xa, za = slab(0), slab(1), slab(2), slab(3)
            cx = gc * xa
            p6 = jnp.broadcast_to(prev_ref[6:7, cs], (tm, CH))
            p7 = jnp.broadcast_to(prev_ref[7:8, cs], (tm, CH))
            c1 = jnp.where(rows == 0, p7, pltpu.roll(cx, 1, 0))
            c2 = jnp.where(rows == 0, p6, jnp.where(rows == 1, p7, pltpu.roll(cx, 2, 0)))
            prev_ref[:, cs] = cx[tm - 8:, :]
            cv = cw_ref[0:1, cs] * c2 + cw_ref[1:2, cs] * c1 + cw_ref[2:3, cs] * cx
            mix_ref[:, cs] = (gb * cv * (za * _sigmoid(za))).astype(BF16)

            u, v, zb = slab(4), slab(5), slab(6)
            ug, vg = _gelu(u), _gelu(v)
            dlt = vg - jnp.mean(vg, axis=-1, keepdims=True)
            vhat = dlt * lax.rsqrt(jnp.mean(dlt * dlt, axis=-1, keepdims=True) + EPS)
            vn = (vhat * lng_ref[:, cs] + lnb_ref[:, cs]).astype(BF16)
            gate = ug * (zb * _sigmoid(zb))
            for c in range(nch):
                rs = slice(CH * c, CH * (c + 1))
                sp = jnp.dot(wc_ref[s], vn[rs], preferred_element_type=F32) + bsb_ref[s]
                mix_ref[rs, pl.ds(SLAB + CH * s, CH)] = (gate[rs] * sp).astype(BF16)

        @pl.when(pl.program_id(0) == nt - 1)
        def _():
            gather.near_ready()
            gather.far_ready()
            gather.drain()

    sems = _gather_sems(nw)
    outs = pl.pallas_call(
        body, grid=(nt,),
        in_specs=[pl.BlockSpec((tm, IN_DIM), lambda i: (i, 0)), _full((8, D)), _full((1, D)), _full((1, D)),
                  _full((HEADS, CH, CH)), _full((HEADS, CH, CH))] + [ANY] * nw,
        out_specs=[pl.BlockSpec((tm, MIX), lambda i: (i, 0))] + [ANY] * nw,
        out_shape=[jax.ShapeDtypeStruct((t, MIX), BF16)] + [jax.ShapeDtypeStruct(f.shape, f.dtype) for f in fulls],
        input_output_aliases={6 + w: 1 + w for w in range(nw)},
        scratch_shapes=[pltpu.VMEM((8, D), F32)] + sems,
        compiler_params=_cp(("arbitrary",), VMEM_LIMIT), name="mixer_fwd")(proj, cw8, lng, lnb, wc, bsb, *fulls)
    return outs[0], outs[1:]


def _mem_fwd(mem, gm, wkv_f):
    n_mem = mem.shape[0]

    def body(mem_ref, gm_ref, w_ref, k_ref, v_ref):
        m, _ = _rms(mem_ref[...], gm_ref[...])
        mb = m.astype(BF16)
        for j in range(N_CHIP):
            dst = k_ref if j < 2 else v_ref
            dst[:, pl.ds(KV_BLK * (j % 2), KV_BLK)] = jnp.dot(mb, w_ref[j], preferred_element_type=F32).astype(BF16)

    return pl.pallas_call(
        body, out_shape=[jax.ShapeDtypeStruct((n_mem, D), BF16), jax.ShapeDtypeStruct((n_mem, D), BF16)],
        compiler_params=_cp(None, VMEM_LIMIT), name="mem_fwd")(mem, gm, wkv_f)


def _tail(x, tgt, mixin, wout, wq, wxo, k, v, g2, g3, tm=512, sub=512):
    t = x.shape[0]
    n_mem = k.shape[0]
    scale = 1.0 / math.sqrt(XD)

    def body(x_ref, tgt_ref, mix_ref, wout_ref, wq_ref, wxo_ref, k_ref, v_ref, g2_ref, g3_ref,
             loss_ref, dmix_ref, dx1b_ref, h2_ref, dq_ref, o_ref, dx2b_ref, dk_ref, dv_ref, dg2_ref, dg3_ref):
        @pl.when(pl.program_id(0) == 0)
        def _():
            loss_ref[...] = jnp.zeros_like(loss_ref)
            dk_ref[...] = jnp.zeros_like(dk_ref)
            dv_ref[...] = jnp.zeros_like(dv_ref)
            dg2_ref[...] = jnp.zeros_like(dg2_ref)
            dg3_ref[...] = jnp.zeros_like(dg3_ref)

        g2, g3 = g2_ref[...], g3_ref[...]
        for sb in range(tm // sub):
            rs = pl.ds(sub * sb, sub)
            x1 = x_ref[rs, :] + jnp.dot(mix_ref[rs, :], wout_ref[...], preferred_element_type=F32)
            h2, r2 = _rms(x1, g2)
            h2b = h2.astype(BF16)
            h2_ref[rs, :] = h2b
            q = jnp.dot(h2b, wq_ref[...], preferred_element_type=F32).astype(BF16)
            probs, outs = [], []
            for hd in range(XH):
                hs = pl.ds(XD * hd, XD)
                s = _bdot_nt(q[:, XD * hd:XD * (hd + 1)], k_ref[:, hs]) * scale
                e = jnp.exp(s - jnp.max(s, axis=-1, keepdims=True))
                p = e / jnp.sum(e, axis=-1, keepdims=True)
                probs.append(p)
                outs.append(_bdot(p, v_ref[:, hs]))
            ob = jnp.concatenate(outs, axis=-1).astype(BF16)
            o_ref[rs, :] = ob
            x2 = x1 + jnp.dot(ob, wxo_ref[...], preferred_element_type=F32)
            y, r3 = _rms(x2, g3)
            diff = y - tgt_ref[rs, :]
            row_loss = jnp.sum(diff * diff, axis=-1, keepdims=True)
            loss_ref[...] += jnp.broadcast_to(jnp.sum(row_loss, axis=0, keepdims=True) * (0.5 / D), loss_ref.shape)

            dx2, dg3 = _rms_bwd(diff * (1.0 / D), x2, r3, g3)
            dg3_ref[...] += dg3
            dx2b = dx2.astype(BF16)
            dx2b_ref[rs, :] = dx2b
            do = _bdot_nt(dx2b, wxo_ref[...])
            dqs = []
            for hd in range(XH):
                hs = pl.ds(XD * hd, XD)
                p = probs[hd]
                do_h = do[:, XD * hd:XD * (hd + 1)]
                dv_ref[:, hs] += _bdot_tn(p, do_h)
                dp = _bdot_nt(do_h, v_ref[:, hs])
                ds = p * (dp - jnp.sum(dp * p, axis=-1, keepdims=True))
                dqs.append(_bdot(ds, k_ref[:, hs]) * scale)
                dk_ref[:, hs] += _bdot_tn(ds, q[:, XD * hd:XD * (hd + 1)]) * scale
            dq = jnp.concatenate(dqs, axis=-1).astype(BF16)
            dq_ref[rs, :] = dq
            dx1n, dg2 = _rms_bwd(_bdot_nt(dq, wq_ref[...]), x1, r2, g2)
            dg2_ref[...] += dg2
            dx1b = (dx2 + dx1n).astype(BF16)
            dx1b_ref[rs, :] = dx1b
            dmix_ref[rs, :] = _bdot_nt(dx1b, wout_ref[...]).astype(BF16)

    tok = lambda w: pl.BlockSpec((tm, w), lambda i: (i, 0))
    return pl.pallas_call(
        body, grid=(t // tm,),
        in_specs=[tok(D), tok(D), tok(MIX), _full((MIX, D), 1), _full((D, D), 1), _full((D, D), 1),
                  _full((n_mem, D), 1), _full((n_mem, D), 1), _full((1, D)), _full((1, D))],
        out_specs=[_full((8, 128)), tok(MIX), tok(D), tok(D), tok(D), tok(D), tok(D),
                   _full((n_mem, D)), _full((n_mem, D)), _full((1, D)), _full((1, D))],
        out_shape=[jax.ShapeDtypeStruct((8, 128), F32), jax.ShapeDtypeStruct((t, MIX), BF16),
                   jax.ShapeDtypeStruct((t, D), BF16),
                   jax.ShapeDtypeStruct((t, D), BF16), jax.ShapeDtypeStruct((t, D), BF16),
                   jax.ShapeDtypeStruct((t, D), BF16), jax.ShapeDtypeStruct((t, D), BF16),
                   jax.ShapeDtypeStruct((n_mem, D), F32), jax.ShapeDtypeStruct((n_mem, D), F32),
                   jax.ShapeDtypeStruct((1, D), F32), jax.ShapeDtypeStruct((1, D), F32)],
        compiler_params=_cp(("arbitrary",), VMEM_LIMIT), name="tail")(x, tgt, mixin, wout, wq, wxo, k, v, g2, g3)


def _mem_bwd(mem, gm, dk, dv, wkv_f):
    def body(mem_ref, gm_ref, dk_ref, dv_ref, w_ref, dw_ref, dwb_ref, dgm_ref):
        mem_v = mem_ref[...]
        m, rm = _rms(mem_v, gm_ref[...])
        mb = m.astype(BF16)
        dm = jnp.zeros_like(mem_v)
        for j in range(N_CHIP):
            src = dk_ref if j < 2 else dv_ref
            dkv = src[:, pl.ds(KV_BLK * (j % 2), KV_BLK)].astype(BF16)
            dw = _bdot_tn(mb, dkv)
            dw_ref[j] = dw
            dwb_ref[j] = dw.astype(BF16)
            dm = dm + _bdot_nt(dkv, w_ref[j])
        dgm_ref[...] = jnp.sum(dm * mem_v * rm, axis=0, keepdims=True)

    return pl.pallas_call(
        body, out_shape=[jax.ShapeDtypeStruct((N_CHIP, D, KV_BLK), F32), jax.ShapeDtypeStruct((N_CHIP, D, KV_BLK), BF16),
                         jax.ShapeDtypeStruct((1, D), F32)],
        compiler_params=_cp(None, VMEM_LIMIT), name="mem_bwd")(mem, gm, dk, dv, wkv_f)


def _mixer_bwd(proj, dmix, cw8, lng, lnb, wc, wct, bsb, win_f, x, dx1, g1, tm=256):
    t = proj.shape[0]
    nt = t // tm
    nch = tm // CH
    hb = 16
    pair = 2 * CH

    def body(p_ref, pgc_ref, pxa_ref, dm_ref, cw_ref, lng_ref, lnb_ref, wc_ref, wct_ref, bsb_ref, w_ref, x_ref,
             dx1_ref, g1_ref, dp_ref, dcw_ref, dlng_ref, dlnb_ref, dwc_ref, dbs_ref, gx_ref, dg1_ref,
             next_ref, dh_ref):
        i = pl.program_id(0)

        @pl.when(i == 0)
        def _():
            next_ref[...] = jnp.zeros_like(next_ref)
            dcw_ref[...] = jnp.zeros_like(dcw_ref)
            dlng_ref[...] = jnp.zeros_like(dlng_ref)
            dlnb_ref[...] = jnp.zeros_like(dlnb_ref)
            dwc_ref[...] = jnp.zeros_like(dwc_ref)
            dbs_ref[...] = jnp.zeros_like(dbs_ref)
            dg1_ref[...] = jnp.zeros_like(dg1_ref)

        first_tile = i == nt - 1
        rows = lax.broadcasted_iota(jnp.int32, (tm, CH), 0)
        ones8 = jnp.ones((8, CH), BF16)
        for s in range(HEADS):
            cs = pl.ds(CH * s, CH)

            def slab(k):
                return p_ref[:, pl.ds(k * SLAB + CH * s, CH)].astype(F32)

            gb, gc, xa, za = slab(0), slab(1), slab(2), slab(3)
            da = dm_ref[:, cs].astype(F32)
            cx = gc * xa
            cxp = pgc_ref[:, cs].astype(F32) * pxa_ref[:, cs].astype(F32)
            cxp = jnp.where(first_tile, jnp.zeros_like(cxp), cxp)
            p6 = jnp.broadcast_to(cxp[hb - 2:hb - 1, :], (tm, CH))
            p7 = jnp.broadcast_to(cxp[hb - 1:hb, :], (tm, CH))
            c1 = jnp.where(rows == 0, p7, pltpu.roll(cx, 1, 0))
            c2 = jnp.where(rows == 0, p6, jnp.where(rows == 1, p7, pltpu.roll(cx, 2, 0)))
            w0, w1, w2 = cw_ref[0:1, cs], cw_ref[1:2, cs], cw_ref[2:3, cs]
            cv = w0 * c2 + w1 * c1 + w2 * cx
            sg = _sigmoid(za)
            sa = za * sg
            dcv = da * gb * sa
            dp_ref[:, pl.ds(0 * SLAB + CH * s, CH)] = (da * cv * sa).astype(BF16)
            dp_ref[:, pl.ds(3 * SLAB + CH * s, CH)] = (da * gb * cv * (sg * (1.0 + za * (1.0 - sg)))).astype(BF16)
            n0 = jnp.broadcast_to(next_ref[0:1, cs], (tm, CH))
            n1 = jnp.broadcast_to(next_ref[1:2, cs], (tm, CH))
            u1 = jnp.where(rows == tm - 1, n0, pltpu.roll(dcv, tm - 1, 0))
            u2 = jnp.where(rows == tm - 2, n0, jnp.where(rows == tm - 1, n1, pltpu.roll(dcv, tm - 2, 0)))
            next_ref[:, cs] = dcv[0:8, :]
            dcx = w2 * dcv + w1 * u1 + w0 * u2
            dp_ref[:, pl.ds(1 * SLAB + CH * s, CH)] = (dcx * xa).astype(BF16)
            dp_ref[:, pl.ds(2 * SLAB + CH * s, CH)] = (dcx * gc).astype(BF16)
            dcw_ref[0:1, cs] += jnp.sum(dcv * c2, axis=0, keepdims=True)
            dcw_ref[1:2, cs] += jnp.sum(dcv * c1, axis=0, keepdims=True)
            dcw_ref[2:3, cs] += jnp.sum(dcv * cx, axis=0, keepdims=True)

            u, v, zb = slab(4), slab(5), slab(6)
            db = dm_ref[:, pl.ds(SLAB + CH * s, CH)].astype(F32)
            ug, ugrad = _gelu_parts(u)
            vg, vgrad = _gelu_parts(v)
            dlt = vg - jnp.mean(vg, axis=-1, keepdims=True)
            rstd = lax.rsqrt(jnp.mean(dlt * dlt, axis=-1, keepdims=True) + EPS)
            vhat = dlt * rstd
            lg = lng_ref[:, cs]
            vn = (vhat * lg + lnb_ref[:, cs]).astype(BF16)
            sgb = _sigmoid(zb)
            szb = zb * sgb
            sps, dvns = [], []
            dbs = jnp.zeros((8, CH), F32)
            dwc = jnp.zeros((CH, CH), F32)
            for c in range(nch):
                rs = slice(CH * c, CH * (c + 1))
                sp = jnp.dot(wc_ref[s], vn[rs], preferred_element_type=F32) + bsb_ref[s]
                dsp = (db[rs] * ug[rs] * szb[rs]).astype(BF16)
                dbs = dbs + lax.dot_general(ones8, dsp, (((1,), (1,)), ((), ())), preferred_element_type=F32)
                dwc = dwc + lax.dot_general(dsp, vn[rs], (((1,), (1,)), ((), ())), preferred_element_type=F32)
                dvns.append(jnp.dot(wct_ref[s], dsp, preferred_element_type=F32))
                sps.append(sp)
            sp = jnp.concatenate(sps, axis=0)
            dvn = jnp.concatenate(dvns, axis=0)
            dbs_ref[:, cs] += dbs
            dwc_ref[s] += dwc
            dlng_ref[:, cs] += jnp.sum(dvn * vhat, axis=0, keepdims=True)
            dlnb_ref[:, cs] += jnp.sum(dvn, axis=0, keepdims=True)
            dvhat = dvn * lg
            dvg = rstd * (dvhat - jnp.mean(dvhat, axis=-1, keepdims=True)
                          - vhat * jnp.mean(dvhat * vhat, axis=-1, keepdims=True))
            dp_ref[:, pl.ds(4 * SLAB + CH * s, CH)] = (db * sp * szb * ugrad).astype(BF16)
            dp_ref[:, pl.ds(5 * SLAB + CH * s, CH)] = (dvg * vgrad).astype(BF16)
            dp_ref[:, pl.ds(6 * SLAB + CH * s, CH)] = (db * ug * sp * (sgb * (1.0 + zb * (1.0 - sgb)))).astype(BF16)

            if s % 2 == 1:
                part = None
                for k in range(N_SLAB):
                    col = k * SLAB + pair * (s // 2)
                    blk, off = divmod(col, IN_BLK)
                    term = lax.dot_general(dp_ref[:, pl.ds(col, pair)], w_ref[blk, :, pl.ds(off, pair)],
                                           (((1,), (1,)), ((), ())), preferred_element_type=F32)
                    part = term if part is None else part + term
                if s == 1:
                    dh_ref[...] = part
                else:
                    dh_ref[...] += part

        xv = x_ref[...]
        r = lax.rsqrt(jnp.mean(xv * xv, axis=-1, keepdims=True) + EPS)
        dxn, dg = _rms_bwd(dh_ref[...], xv, r, g1_ref[...])
        gx_ref[...] = dx1_ref[...].astype(F32) + dxn
        dg1_ref[0:1, :] += dg

        @pl.when(i == nt - 1)
        def _():
            tril = lax.broadcasted_iota(jnp.int32, (CH, CH), 0) >= lax.broadcasted_iota(jnp.int32, (CH, CH), 1)
            for s in range(HEADS):
                dwc_ref[s] = jnp.where(tril, dwc_ref[s], 0.0)

    rev = lambda i: nt - 1 - i
    halo = lambda col: pl.BlockSpec((hb, SLAB), lambda i: (jnp.maximum(rev(i) * (tm // hb) - 1, 0), col))
    tok = lambda w: pl.BlockSpec((tm, w), lambda i: (rev(i), 0))
    return pl.pallas_call(
        body, grid=(nt,),
        in_specs=[tok(IN_DIM), halo(1), halo(2), tok(MIX), _full((8, D)), _full((1, D)), _full((1, D)),
                  _full((HEADS, CH, CH)), _full((HEADS, CH, CH)), _full((HEADS, CH, CH)),
                  _full((N_CHIP, D, IN_BLK), 1), tok(D), tok(D), _full((1, D))],
        out_specs=[tok(IN_DIM), _full((8, D)), _full((1, D)), _full((1, D)), _full((HEADS, CH, CH)), _full((8, D)),
                   tok(D), _full((8, D))],
        out_shape=[jax.ShapeDtypeStruct((t, IN_DIM), BF16), jax.ShapeDtypeStruct((8, D), F32),
                   jax.ShapeDtypeStruct((1, D), F32), jax.ShapeDtypeStruct((1, D), F32),
                   jax.ShapeDtypeStruct((HEADS, CH, CH), F32), jax.ShapeDtypeStruct((8, D), F32),
                   jax.ShapeDtypeStruct((t, D), F32), jax.ShapeDtypeStruct((8, D), F32)],
        scratch_shapes=[pltpu.VMEM((8, D), F32), pltpu.VMEM((tm, D), F32)],
        compiler_params=_cp(("arbitrary",), VMEM_LIMIT), name="mixer_bwd")(
            proj, proj, proj, dmix, cw8, lng, lnb, wc, wct, bsb, win_f, x, dx1, g1)


def _grad_matmul(a, b, after, *, by_cols, name, tk=1024):
    t, m = a.shape
    n = b.shape[1]
    nk = t // tk
    nj = N_CHIP if by_cols else 1
    bn = n // nj

    def body(a_ref, b_ref, after_ref, o_ref, ob_ref):
        kk = pl.program_id(1)
        part = lax.dot_general(a_ref[...], b_ref[...], (((0,), (0,)), ((), ())), preferred_element_type=F32)

        @pl.when(kk == 0)
        def _():
            o_ref[...] = part

        @pl.when(kk > 0)
        def _():
            o_ref[...] += part

        @pl.when(kk == nk - 1)
        def _():
            ob_ref[...] = o_ref[...].astype(BF16)

    a_spec = pl.BlockSpec((tk, m), lambda j, k: (k, 0))
    b_spec = pl.BlockSpec((tk, bn), lambda j, k: (k, j))
    o_spec = pl.BlockSpec((None, m, bn), lambda j, k: (j, 0, 0))
    o32, o16 = pl.pallas_call(
        body, grid=(nj, nk), in_specs=[a_spec, b_spec, ANY], out_specs=[o_spec, o_spec],
        out_shape=[jax.ShapeDtypeStruct((nj, m, bn), F32), jax.ShapeDtypeStruct((nj, m, bn), BF16)],
        compiler_params=_cp(("parallel", "arbitrary"), VMEM_LIMIT), name=name)(a, b, after)
    if by_cols:
        return o32, o16
    return o32.reshape(N_CHIP, m // N_CHIP, n), o16.reshape(N_CHIP, m // N_CHIP, n)


def _coords():
    x, y, c = lax.axis_index("x"), lax.axis_index("y"), lax.axis_index("c")
    chips = [(1 - x, y), (x, 1 - y), (1 - x, 1 - y)]
    return x, y, c, chips


def _pair_exchange(grads_b, smalls, name):
    ng, ns = len(grads_b), len(smalls)
    n = ng + ns

    def body(*refs):
        ins, outs, send, recv = refs[:n], refs[n:2 * n], refs[2 * n], refs[2 * n + 1]
        x, y, c, _ = _coords()
        cps = []
        for i in range(n):
            if i < ng:
                hr = ins[i].shape[1] // 2
                src = ins[i].at[pl.ds(0, N_CHIP), pl.ds((1 - c) * hr, hr)]
            else:
                hr = ins[i].shape[0] // 2
                src = ins[i].at[pl.ds((1 - c) * hr, hr)]
            cps.append(pltpu.make_async_remote_copy(
                src_ref=src, dst_ref=outs[i], send_sem=send.at[i], recv_sem=recv.at[i],
                device_id=(x, y, 1 - c), device_id_type=MESH))
        for cp in cps:
            cp.start()
        for cp in cps:
            cp.wait()

    out_shape = [jax.ShapeDtypeStruct((N_CHIP, g.shape[1] // 2, g.shape[2]), g.dtype) for g in grads_b]
    out_shape += [jax.ShapeDtypeStruct((s.shape[0] // 2, s.shape[1]), s.dtype) for s in smalls]
    return pl.pallas_call(
        body, out_shape=out_shape, in_specs=[ANY] * n, out_specs=[ANY] * n,
        scratch_shapes=[pltpu.SemaphoreType.DMA((n,)), pltpu.SemaphoreType.DMA((n,))],
        name=name)(*grads_b, *smalls)


def _pair_sum(c_idx, grads, recvd, smalls, smalls_recvd, name):
    ng, ns = len(grads), len(smalls)
    halves = [g.shape[1] // 2 for g in grads]

    def body(c_ref, *refs):
        g_in, r_in = refs[:ng], refs[ng:2 * ng]
        s_in, sr_in = refs[2 * ng:2 * ng + ns], refs[2 * ng + ns:2 * ng + 2 * ns]
        o = refs[2 * ng + 2 * ns:]
        for i in range(ng):
            tot = g_in[i][...] + r_in[i][...].astype(F32)
            o[i][...] = tot
            o[ng + i][...] = tot.astype(BF16)
        for i in range(ns):
            o[2 * ng + i][...] = s_in[i][...] + sr_in[i][...]

    in_specs = [pl.BlockSpec((None, None, halves[i], g.shape[2]), lambda b, c: (b, c[0], 0, 0)) for i, g in enumerate(grads)]
    in_specs += [pl.BlockSpec((None, halves[i], g.shape[2]), lambda b, c: (b, 0, 0)) for i, g in enumerate(grads)]
    in_specs += [pl.BlockSpec((None, s.shape[0] // 2, s.shape[1]), lambda b, c: (c[0], 0, 0)) for s in smalls]
    in_specs += [pl.BlockSpec((s.shape[0] // 2, s.shape[1]), lambda b, c: (0, 0)) for s in smalls]
    blk = [pl.BlockSpec((None, halves[i], g.shape[2]), lambda b, c: (b, 0, 0)) for i, g in enumerate(grads)]
    out_specs = blk + blk + [pl.BlockSpec((s.shape[0] // 2, s.shape[1]), lambda b, c: (0, 0)) for s in smalls]
    out_shape = [jax.ShapeDtypeStruct((N_CHIP, halves[i], g.shape[2]), F32) for i, g in enumerate(grads)]
    out_shape += [jax.ShapeDtypeStruct((N_CHIP, halves[i], g.shape[2]), BF16) for i, g in enumerate(grads)]
    out_shape += [jax.ShapeDtypeStruct((s.shape[0] // 2, s.shape[1]), F32) for s in smalls]
    grads4 = [g.reshape(N_CHIP, 2, halves[i], g.shape[2]) for i, g in enumerate(grads)]
    smalls3 = [s.reshape(2, s.shape[0] // 2, s.shape[1]) for s in smalls]
    return pl.pallas_call(
        body, out_shape=out_shape,
        grid_spec=pltpu.PrefetchScalarGridSpec(num_scalar_prefetch=1, grid=(N_CHIP,), in_specs=in_specs, out_specs=out_specs),
        compiler_params=_cp(("arbitrary",), VMEM_LIMIT), name=name)(c_idx, *grads4, *recvd, *smalls3, *smalls_recvd)


_HBM = pl.BlockSpec(memory_space=pltpu.HBM)
_SEM = pl.BlockSpec(memory_space=pltpu.SEMAPHORE)


def _split_copies(ins, lands, ng, send, recv, arriving):
    x, y, c, chips = _coords()
    b = 2 * x + y
    copies = []
    for i in range(len(ins)):
        for k in range(3):
            blk = 2 * chips[k][0] + chips[k][1]
            src, dst, got = (ins[i].at[blk], lands[i].at[k], lands[i].at[k]) if i < ng else (ins[i], lands[i].at[b], lands[i].at[blk])
            sems = dict(send_sem=send.at[3 * i + k], recv_sem=recv.at[3 * i + k], device_id=(*chips[k], c), device_id_type=MESH)
            if arriving:
                copies.append(pltpu.make_async_remote_copy(src_ref=got, dst_ref=got, **sems))
            else:
                copies.append(pltpu.make_async_remote_copy(src_ref=src, dst_ref=dst, **sems))
    return copies


def _exchange_begin(sums_b, smalls, name):
    ng, n = len(sums_b), len(sums_b) + len(smalls)
    srcs = list(sums_b) + list(smalls)
    lands = [lax.empty((3,) + g.shape[1:], g.dtype) for g in sums_b] + [lax.empty((N_CHIP,) + s.shape, s.dtype) for s in smalls]

    def body(*refs):
        ins, land_refs = refs[:n], refs[n:2 * n]
        send, recv = refs[2 * n], refs[2 * n + 1]
        token = refs[4 * n + 2]
        for cp in _split_copies(ins, land_refs, ng, send, recv, False):
            cp.start()
        token[...] = jnp.zeros_like(token)

    hbm = lambda a: pltpu.HBM(a.shape, a.dtype)
    outs = pl.pallas_call(
        body, name=name,
        out_shape=(pltpu.SemaphoreType.DMA((3 * n,)), pltpu.SemaphoreType.DMA((3 * n,)), *[hbm(a) for a in srcs + lands],
                   jax.ShapeDtypeStruct((8, 128), F32)),
        in_specs=[_HBM] * (2 * n), out_specs=(_SEM, _SEM, *[_HBM] * (2 * n), pl.BlockSpec(memory_space=pltpu.VMEM)),
        input_output_aliases={i: 2 + i for i in range(2 * n)},
        compiler_params=pltpu.CompilerParams(has_side_effects=pltpu.SideEffectType.DATAFLOW_SIDE_EFFECTING),
    )(*[pltpu.with_memory_space_constraint(a, pltpu.HBM) for a in srcs + lands])
    return outs[0], outs[1], list(outs[2:2 + n]), list(outs[2 + n:2 + 2 * n]), outs[2 + 2 * n]


def _exchange_end(send, recv, srcs, lands, ng, after, name):
    n = len(srcs)
    after = list(after)

    def body(*refs):
        ins, land_refs = refs[:n], refs[n:2 * n]
        send_ref, recv_ref = refs[2 * n], refs[2 * n + 1]
        for cp in _split_copies(ins, land_refs, ng, send_ref, recv_ref, False):
            cp.wait_send()
        for cp in _split_copies(ins, land_refs, ng, send_ref, recv_ref, True):
            cp.wait_recv()

    hbm = lambda a: pltpu.HBM(a.shape, a.dtype)
    outs = pl.pallas_call(
        body, name=name, out_shape=tuple(hbm(a) for a in list(srcs) + list(lands)),
        in_specs=[_HBM] * (2 * n) + [_SEM, _SEM] + [ANY] * len(after), out_specs=tuple([_HBM] * (2 * n)),
        input_output_aliases={i: i for i in range(2 * n)},
        compiler_params=pltpu.CompilerParams(has_side_effects=pltpu.SideEffectType.DATAFLOW_SIDE_EFFECTING),
    )(*srcs, *lands, send, recv, *after)
    return list(outs[n:])


def _chip_sum(bc_idx, sums, recvd, smalls_slots, smalls_own, name, steps=4):
    ng, ns = len(sums), len(smalls_slots)

    def body(bc_ref, *refs):
        own, rx = refs[:ng], refs[ng:2 * ng]
        sl = refs[2 * ng:2 * ng + ns]
        sl_own = refs[2 * ng + ns:2 * ng + 2 * ns]
        o = refs[2 * ng + 2 * ns:]
        for i in range(ng):
            tot = own[i][...]
            for j in range(3):
                tot = tot + rx[i][j].astype(F32)
            o[i][...] = tot
        for i in range(ns):
            term = [jnp.where(bc_ref[0] == kk, sl_own[i][...], sl[i][kk]) for kk in range(N_CHIP)]
            o[ng + i][...] = ((term[0] + term[1]) + term[2]) + term[3]

    def rows(g):
        return g.shape[1] // steps

    in_specs = [pl.BlockSpec((None, rows(g), g.shape[2]), lambda r, bc: (bc[0], r, 0)) for g in sums]
    in_specs += [pl.BlockSpec((3, rows(g), g.shape[2]), lambda r, bc: (0, r, 0)) for g in sums]
    in_specs += [pl.BlockSpec(s.shape, lambda r, bc: (0, 0, 0)) for s in smalls_slots]
    in_specs += [pl.BlockSpec(s.shape[1:], lambda r, bc: (0, 0)) for s in smalls_slots]
    out_specs = [pl.BlockSpec((rows(g), g.shape[2]), lambda r, bc: (bc[1] * steps + r, 0)) for g in sums]
    out_specs += [pl.BlockSpec(s.shape[1:], lambda r, bc: (bc[1], 0)) for s in smalls_slots]
    out_shape = [jax.ShapeDtypeStruct((2 * g.shape[1], g.shape[2]), F32) for g in sums]
    out_shape += [jax.ShapeDtypeStruct((2 * s.shape[1], s.shape[2]), F32) for s in smalls_slots]
    return pl.pallas_call(
        body, out_shape=out_shape,
        grid_spec=pltpu.PrefetchScalarGridSpec(num_scalar_prefetch=1, grid=(steps,), in_specs=in_specs, out_specs=out_specs),
        compiler_params=_cp(("arbitrary",), VMEM_LIMIT), name=name)(bc_idx, *sums, *recvd, *smalls_slots, *smalls_own)


def _pair_gather(arrs, name):
    n = len(arrs)

    def body(*refs):
        outs = refs[n:2 * n]
        send, recv = refs[2 * n:]
        x, y, c, _ = _coords()
        cps = []
        for i in range(n):
            hr = outs[i].shape[0] // 2
            mine = outs[i].at[pl.ds(c * hr, hr)]
            cps.append(pltpu.make_async_remote_copy(
                src_ref=mine, dst_ref=mine, send_sem=send.at[i], recv_sem=recv.at[i],
                device_id=(x, y, 1 - c), device_id_type=MESH))
        for cp in cps:
            cp.start()
        for i in range(n):
            hr = outs[i].shape[0] // 2
            theirs = outs[i].at[pl.ds((1 - c) * hr, hr)]
            pltpu.make_async_remote_copy(
                src_ref=theirs, dst_ref=theirs, send_sem=send.at[i], recv_sem=recv.at[i],
                device_id=(x, y, 1 - c), device_id_type=MESH).wait_recv()
        for cp in cps:
            cp.wait_send()

    return list(pl.pallas_call(
        body, out_shape=[jax.ShapeDtypeStruct(a.shape, a.dtype) for a in arrs], in_specs=[ANY] * n, out_specs=[ANY] * n,
        input_output_aliases={i: i for i in range(n)},
        scratch_shapes=[pltpu.SemaphoreType.DMA((n,)), pltpu.SemaphoreType.DMA((n,))],
        name=name)(*arrs))


def _adamw_math(w, g, m, v):
    m2 = ADAM_B1 * m + (1.0 - ADAM_B1) * g
    v2 = ADAM_B2 * v + (1.0 - ADAM_B2) * (g * g)
    m_hat = m2 / (1.0 - ADAM_B1 ** ADAM_STEP)
    v_hat = v2 / (1.0 - ADAM_B2 ** ADAM_STEP)
    delta = -ADAM_LR * (m_hat / (jnp.sqrt(v_hat) + ADAM_EPS) + ADAM_WD * w)
    return delta, m2, v2


def _adamw_big(w, g, m, v, name, steps=8):
    r, c = w.shape

    def body(w_ref, g_ref, m_ref, v_ref, d_ref, m2_ref, v2_ref, g2_ref):
        gv = g_ref[...]
        d_ref[...], m2_ref[...], v2_ref[...] = _adamw_math(w_ref[...], gv, m_ref[...], v_ref[...])
        g2_ref[...] = gv

    spec = pl.BlockSpec((r // steps, c), lambda i: (i, 0))
    return pl.pallas_call(
        body, grid=(steps,), in_specs=[spec] * 4, out_specs=[spec] * 4,
        out_shape=[jax.ShapeDtypeStruct((r, c), F32)] * 4,
        compiler_params=_cp(("parallel",), VMEM_LIMIT), name=name)(w, g, m, v)


def _adamw_small(groups):
    n = len(groups)

    def body(*refs):
        for i in range(n):
            w_ref, g_ref, m_ref, v_ref = refs[4 * i:4 * i + 4]
            d_ref, m2_ref, v2_ref = refs[4 * n + 3 * i:4 * n + 3 * i + 3]
            d_ref[...], m2_ref[...], v2_ref[...] = _adamw_math(w_ref[...], g_ref[...], m_ref[...], v_ref[...])

    flat = [a for grp in groups for a in grp]
    out_shape = [jax.ShapeDtypeStruct(grp[0].shape, F32) for grp in groups for _ in range(3)]
    outs = pl.pallas_call(body, out_shape=out_shape, name="adamw_small")(*flat)
    return [tuple(outs[3 * i:3 * i + 3]) for i in range(n)]


def kernel(x, mem, norm_mix_g, w_in, conv_w, gm_ln_g, gm_ln_b, gm_ws, gm_bs, w_out, norm_x_g, norm_mem_g, w_q, w_kv, w_xo, norm_final_g, loss_target, m_norm_mix_g, m_w_in, m_conv_w, m_gm_ln_g, m_gm_ln_b, m_gm_ws, m_gm_bs, m_w_out, m_norm_x_g, m_norm_mem_g, m_w_q, m_w_kv, m_w_xo, m_norm_final_g, v_norm_mix_g, v_w_in, v_conv_w, v_gm_ln_g, v_gm_ln_b, v_gm_ws, v_gm_bs, v_w_out, v_norm_x_g, v_norm_mem_g, v_w_q, v_w_kv, v_w_xo, v_norm_final_g):
    t = x.shape[1]
    xi = lax.axis_index("x")
    yi = lax.axis_index("y")
    ci = lax.axis_index("c")
    b_idx = jnp.reshape(2 * xi + yi, (1,)).astype(jnp.int32)
    c_idx = jnp.reshape(ci, (1,)).astype(jnp.int32)

    x2d, mem2d, tgt = x[0], mem[0], loss_target[0]
    big = [w_in[0], w_out[0], w_q[0], w_kv[0], w_xo[0]]
    big_m = [m_w_in[0], m_w_out[0], m_w_q[0], m_w_kv[0], m_w_xo[0]]
    big_v = [v_w_in[0], v_w_out[0], v_w_q[0], v_w_kv[0], v_w_xo[0]]
    g3 = norm_final_g.reshape(1, D)

    def pad8(a):
        return jnp.pad(a, ((0, 8 - a.shape[0]), (0, 0)))

    own_blocks = _cast_shards(b_idx, big)

    tril = jnp.tril(jnp.ones((CH, CH), bool))
    wc32 = jnp.where(tril[None], gm_ws[0], 0.0)
    wc = wc32.astype(BF16)
    wct = jnp.swapaxes(wc32, 1, 2).astype(BF16)
    bsb = jnp.broadcast_to(gm_bs[0][:, :, None], (HEADS, CH, CH))

    blk = 2 * xi + yi
    order = jnp.stack([blk, blk ^ 2, blk ^ 1, blk ^ 3]).astype(jnp.int32)
    proj, hb, win_f, cw8, (wout_f,) = _proj_gather(
        order, x2d, norm_mix_g, own_blocks[0], pad8(conv_w[0]), [own_blocks[1]])
    mixin, (wq_f, wkv_f, wxo_f) = _mixer_fwd(proj, cw8, gm_ln_g, gm_ln_b, wc, bsb, own_blocks[2:])
    wout2, wq2, wxo2 = wout_f.reshape(MIX, D), wq_f.reshape(D, D), wxo_f.reshape(D, D)
    k, v = _mem_fwd(mem2d, norm_mem_g, wkv_f)

    (loss_tile, dmix, dx1b, h2b, dq, ob, dx2b, dk, dv, dg2, dg3) = _tail(
        x2d, tgt, mixin, wout2, wq2, wxo2, k, v, norm_x_g, g3)
    dwkv, dwkv_b, dgm = _mem_bwd(mem2d, norm_mem_g, dk, dv, wkv_f)
    dproj, dcw, dlng, dlnb, dwc, dbs8, grad_x, dg1 = _mixer_bwd(
        proj, dmix, cw8, gm_ln_g, gm_ln_b, wc, wct, bsb, win_f, x2d, dx1b, norm_mix_g)

    bc_idx = jnp.concatenate([b_idx, c_idx])
    dwin, dwin_b = _grad_matmul(hb, dproj, dgm, by_cols=True, name="grad_w_in", tk=2048)
    zero = jnp.zeros((1, D), F32)
    loss_row = jnp.broadcast_to(loss_tile[0:1, 0:1], (1, D))
    sv = jnp.concatenate([dg1[0:1], dg2, dgm, dg3, dlng, dlnb, dbs8[0:1], loss_row, dcw], axis=0)
    sw = dwc.reshape(HEADS * CH, CH)
    rx1b = _pair_exchange([dwin_b], [sv, sw], "pair_exchange_b")
    ps_b = _pair_sum(c_idx, [dwin], rx1b[:1], [sv, sw], rx1b[1:], "pair_sum_b")
    sums_b, sums_b_b, psmall = list(ps_b[:1]), list(ps_b[1:2]), list(ps_b[2:])
    send_b, recv_b, src_b, land_b, token_b = _exchange_begin(sums_b_b, psmall, "exchange_b_begin")

    dwxo, dwxo_b = _grad_matmul(ob, dx2b, token_b, by_cols=False, name="grad_w_xo", tk=2048)
    dwq, dwq_b = _grad_matmul(h2b, dq, token_b, by_cols=False, name="grad_w_q", tk=2048)
    dwout, dwout_b = _grad_matmul(mixin, dx1b, token_b, by_cols=False, name="grad_w_out")
    rx1a = _pair_exchange([dwout_b, dwq_b, dwkv_b, dwxo_b], [], "pair_exchange_a")
    ps_a = _pair_sum(c_idx, [dwout, dwq, dwkv, dwxo], rx1a, [], [], "pair_sum_a")
    sums_a, sums_a_b = list(ps_a[:4]), list(ps_a[4:8])
    send_a, recv_a, src_a, land_a, token_a = _exchange_begin(sums_a_b, [], "exchange_a_begin")

    rx2b = _exchange_end(send_b, recv_b, src_b, land_b, 1, [token_a], "exchange_b_end")
    red_b = _chip_sum(bc_idx, sums_b, rx2b[:1], rx2b[1:], psmall, "chip_sum_b")
    gwin, svf, swf = _pair_gather(red_b, "pair_gather_b")
    out_b = _adamw_big(big[0], gwin, big_m[0], big_v[0], "adamw_w_in")

    def vec_pack(a1, a2, am, a3, lg, lb, bs):
        return jnp.concatenate([a1, a2, am, a3.reshape(1, D), lg, lb, bs.reshape(1, D), zero], axis=0)

    wv = vec_pack(norm_mix_g, norm_x_g, norm_mem_g, norm_final_g, gm_ln_g, gm_ln_b, gm_bs)
    mv = vec_pack(m_norm_mix_g, m_norm_x_g, m_norm_mem_g, m_norm_final_g, m_gm_ln_g, m_gm_ln_b, m_gm_bs)
    vv = vec_pack(v_norm_mix_g, v_norm_x_g, v_norm_mem_g, v_norm_final_g, v_gm_ln_g, v_gm_ln_b, v_gm_bs)
    loss = svf[7, 0]
    gcw = lax.dynamic_slice_in_dim(svf[8:16], blk * (D // N_CHIP), D // N_CHIP, axis=1)
    gws = swf
    gv = svf[0:8]
    (dv_, mv_, vv_), (dc_, mc_, vc_), (dws_, mws_, vws_) = _adamw_small([
        (wv, gv, mv, vv),
        (pad8(conv_w[0]), gcw, pad8(m_conv_w[0]), pad8(v_conv_w[0])),
        (gm_ws.reshape(HEADS * CH, CH), gws, m_gm_ws.reshape(HEADS * CH, CH), v_gm_ws.reshape(HEADS * CH, CH))])

    rx2a = _exchange_end(send_a, recv_a, src_a, land_a, 4, [out_b[0], dv_], "exchange_a_end")
    red_a = _chip_sum(bc_idx, sums_a, rx2a, [], [], "chip_sum_a")
    g_a = _pair_gather(red_a, "pair_gather_a")
    names = ["w_out", "w_q", "w_kv", "w_xo"]
    out_a = [_adamw_big(w, g, m, v, "adamw_" + nm) for w, g, m, v, nm in zip(big[1:], g_a, big_m[1:], big_v[1:], names)]
    big_out = [out_b] + out_a

    def unpack(vecs, cw, ws, bigs):
        r = lambda i: vecs[i:i + 1]
        return [r(0), bigs[0][None], cw[0:3][None], r(4), r(5), ws.reshape(1, HEADS, CH, CH), vecs[6].reshape(1, HEADS, CH),
                bigs[1][None], r(1), r(2), bigs[2][None], bigs[3][None], bigs[4][None], vecs[3]]

    grads_out = unpack(gv, gcw, gws, [o[3] for o in big_out])
    delta_out = unpack(dv_, dc_, dws_, [o[0] for o in big_out])
    m_out = unpack(mv_, mc_, mws_, [o[1] for o in big_out])
    v_out = unpack(vv_, vc_, vws_, [o[2] for o in big_out])
    return (loss, grad_x[None], *grads_out, *delta_out, *m_out, *v_out)
```

```python
import functools
import math

import jax
import jax.numpy as jnp
from jax import lax
from jax.experimental import pallas as pl
from jax.experimental.pallas import tpu as pltpu

F32 = jnp.float32
BF16 = jnp.bfloat16
MESH = pl.DeviceIdType.MESH

D = 1024
SLAB = 1024
N_SLAB = 7
IN_DIM = N_SLAB * SLAB
MIX = 2 * SLAB
HEADS = 8
CH = 128
XH = 4
XD = D // XH
EPS = 1e-6
GELU_C = math.sqrt(2.0 / math.pi)
GELU_A = 0.044715
N_CHIP = 4
IN_BLK = IN_DIM // N_CHIP
KV_BLK = 2 * D // N_CHIP

ADAM_LR, ADAM_B1, ADAM_B2, ADAM_EPS, ADAM_WD, ADAM_STEP = 0.001, 0.9, 0.999, 1e-08, 0.01, 10

VMEM_LIMIT = 60 * 1024 * 1024


def _cp(sem=None, vmem=None):
    return pltpu.CompilerParams(dimension_semantics=sem, vmem_limit_bytes=vmem)


def _full(shape, buffers=None):
    n = len(shape)
    if buffers is None:
        return pl.BlockSpec(shape, lambda *_: (0,) * n)
    return pl.BlockSpec(shape, lambda *_: (0,) * n, pipeline_mode=pl.Buffered(buffers))


ANY = pl.BlockSpec(memory_space=pl.ANY)


def _bdot(a, b):
    return jnp.dot(a.astype(BF16), b.astype(BF16), preferred_element_type=F32)


def _bdot_nt(a, b):
    return lax.dot_general(a.astype(BF16), b.astype(BF16), (((1,), (1,)), ((), ())), preferred_element_type=F32)


def _bdot_tn(a, b):
    return lax.dot_general(a.astype(BF16), b.astype(BF16), (((0,), (0,)), ((), ())), preferred_element_type=F32)


def _rms(x, g):
    r = lax.rsqrt(jnp.mean(x * x, axis=-1, keepdims=True) + EPS)
    return x * r * g, r


def _rms_bwd(dy, x, r, g):
    gdy = dy * g
    dx = r * gdy - x * (r * r * r) * jnp.mean(x * gdy, axis=-1, keepdims=True)
    dg = jnp.sum(dy * x * r, axis=0, keepdims=True)
    return dx, dg


def _gelu_parts(x):
    x2 = x * x
    t = jnp.tanh(GELU_C * (x + GELU_A * x * x2))
    val = 0.5 * x * (1.0 + t)
    grad = 0.5 * (1.0 + t) + 0.5 * x * (1.0 - t * t) * (GELU_C * (1.0 + 3.0 * GELU_A * x2))
    return val, grad


def _gelu(x):
    return 0.5 * x * (1.0 + jnp.tanh(GELU_C * (x + GELU_A * x * x * x)))


def _sigmoid(z):
    return 1.0 / (1.0 + jnp.exp(-z))


def _cast_shards(b_idx, arrs):
    n = len(arrs)
    steps = 8

    def body(b_ref, *refs):
        for i in range(n):
            refs[n + i][...] = refs[i][...].astype(BF16)

    in_specs = [pl.BlockSpec((a.shape[0] // steps, a.shape[1]), lambda i, b: (i, 0)) for a in arrs]
    out_specs = [pl.BlockSpec((None, a.shape[0] // steps, a.shape[1]), lambda i, b: (b[0], i, 0)) for a in arrs]
    return pl.pallas_call(
        body, out_shape=[jax.ShapeDtypeStruct((N_CHIP,) + a.shape, BF16) for a in arrs],
        grid_spec=pltpu.PrefetchScalarGridSpec(num_scalar_prefetch=1, grid=(steps,), in_specs=in_specs, out_specs=out_specs),
        compiler_params=_cp(("arbitrary",)), name="cast_shards")(b_idx, *arrs)


def _proj_gather(order, x, g, win_own, cw8s, more, tm=1024):
    t = x.shape[0]
    ni = t // tm
    nm = len(more)

    def body(*refs):
        order_ref, x_ref, g_ref, win_in, cw_in = refs[:5]
        o_ref, hb_ref, win_f, cw_out = refs[5 + nm:9 + nm]
        more_out = refs[9 + nm:9 + 2 * nm]
        hbuf, wv, cw_s, cw_r, loc = refs[9 + 2 * nm:14 + 2 * nm]
        g_in = _Gather([win_f], *refs[14 + 2 * nm:18 + 2 * nm])
        g_more = _Gather(more_out, *refs[18 + 2 * nm:22 + 2 * nm])
        j, i = pl.program_id(0), pl.program_id(1)
        x, y, c, chips = _coords()
        b = 2 * x + y
        blks = [2 * chip[0] + chip[1] for chip in chips]

        def cw_cols(blk):
            return cw_out.at[:, pl.ds(blk * (D // N_CHIP), D // N_CHIP)]

        def cw_copy(k, blk):
            src = cw_in if blk is None else cw_cols(blk)
            return pltpu.make_async_remote_copy(src_ref=src, dst_ref=cw_cols(b if blk is None else blk), send_sem=cw_s.at[k],
                                                recv_sem=cw_r.at[k], device_id=(*chips[k], c), device_id_type=MESH)

        cw_local = pltpu.make_async_copy(cw_in, cw_cols(b), loc.at[1])

        def load(blk, slot):
            return pltpu.make_async_copy(win_f.at[blk], wv.at[slot], loc.at[2 + slot])

        @pl.when((j == 0) & (i == 0))
        def _():
            g_in.start()
            cw_local.start()
            for k in range(3):
                cw_copy(k, None).start()
            load(b, 0).start()
            load(b, 0).wait()

        @pl.when((j == 1) & (i == 0))
        def _():
            g_in.hop()
            g_more.start()
            g_in.near_ready()
            load(g_in.bx, 1).start()
            load(g_in.by, 0).start()
            load(g_in.bx, 1).wait()

        @pl.when((j == 2) & (i == 0))
        def _():
            load(g_in.by, 0).wait()
            g_in.far()
            g_in.far_ready()
            load(g_in.bd, 1).start()

        @pl.when((j == 3) & (i == 0))
        def _():
            load(g_in.bd, 1).wait()
            g_more.hop()

        rows = pl.ds(pl.multiple_of(i * tm, tm), tm)

        @pl.when(j == 0)
        def _():
            h, _ = _rms(x_ref[...], g_ref[...])
            hbuf[rows, :] = h.astype(BF16)
            hb_ref[...] = h.astype(BF16)

        @pl.when((j == N_CHIP - 1) & (i == ni - 1))
        def _():
            g_more.far()

        o_ref[...] = jnp.dot(hbuf[rows, :], wv[lax.rem(j, 2)], preferred_element_type=F32).astype(BF16)

        @pl.when((j == N_CHIP - 1) & (i == ni - 1))
        def _():
            for k in range(3):
                cw_copy(k, blks[k]).wait_recv()
            for k in range(3):
                cw_copy(k, None).wait_send()
            cw_local.wait()
            g_more.near_ready()
            g_more.far_ready()
            g_in.drain()
            g_more.drain()

    first = lambda j, i: jnp.where(j == 0, i, ni - 1)
    in_specs = [pl.BlockSpec((tm, D), lambda j, i, o: (first(j, i), 0)), pl.BlockSpec((1, D), lambda j, i, o: (0, 0)),
                ANY, ANY] + [ANY] * nm
    out_specs = [pl.BlockSpec((tm, IN_BLK), lambda j, i, o: (i, o[j])),
                 pl.BlockSpec((tm, D), lambda j, i, o: (first(j, i), 0)), ANY, ANY] + [ANY] * nm
    outs = pl.pallas_call(
        body, out_shape=[jax.ShapeDtypeStruct((t, IN_DIM), BF16), jax.ShapeDtypeStruct((t, D), BF16),
                         jax.ShapeDtypeStruct(win_own.shape, BF16), jax.ShapeDtypeStruct((8, D), F32)]
        + [jax.ShapeDtypeStruct(f.shape, f.dtype) for f in more],
        grid_spec=pltpu.PrefetchScalarGridSpec(
            num_scalar_prefetch=1, grid=(N_CHIP, ni), in_specs=in_specs, out_specs=out_specs,
            scratch_shapes=[pltpu.VMEM((t, D), BF16), pltpu.VMEM((2, D, IN_BLK), BF16)]
            + [pltpu.SemaphoreType.DMA((3,))] * 2 + [pltpu.SemaphoreType.DMA((4,))] + _gather_sems(1) + _gather_sems(nm)),
        input_output_aliases={3: 2, **{5 + w: 4 + w for w in range(nm)}},
        compiler_params=_cp(("arbitrary", "arbitrary"), VMEM_LIMIT), name="proj_gather")(order, x, g, win_own, cw8s, *more)
    return outs[0], outs[1], outs[2], outs[3], outs[4:]


class _Gather:
    def __init__(self, outs, ici_s, ici_r, d2d_s, d2d_r):
        x, y, c, _ = _coords()
        self.outs, self.c = outs, c
        self.sems = ici_s, ici_r, d2d_s, d2d_r
        self.b, self.bx, self.by, self.bd = 2 * x + y, 2 * (1 - x) + y, 2 * x + (1 - y), 2 * (1 - x) + (1 - y)
        self.xn, self.yn, self.sib = (1 - x, y, c), (x, 1 - y, c), (x, y, 1 - c)

    def piece(self, w, blk, hc, quarter=None):
        hr = self.outs[w].shape[1] // 2
        if quarter is None:
            return self.outs[w].at[blk, pl.ds(hc * hr, hr)]
        return self.outs[w].at[blk, pl.ds(hc * hr + quarter * (hr // 2), hr // 2)]

    def ici(self, w, k, ref, to):
        return pltpu.make_async_remote_copy(src_ref=ref, dst_ref=ref, send_sem=self.sems[0].at[w, k],
                                            recv_sem=self.sems[1].at[w, k], device_id=to, device_id_type=MESH)

    def d2d(self, w, k, ref):
        return pltpu.make_async_remote_copy(src_ref=ref, dst_ref=ref, send_sem=self.sems[2].at[w, k],
                                            recv_sem=self.sems[3].at[w, k], device_id=self.sib, device_id_type=MESH)

    def start(self):
        for w in range(len(self.outs)):
            mine = self.piece(w, self.b, self.c)
            self.ici(w, 0, mine, self.xn).start()
            self.ici(w, 1, mine, self.yn).start()

    def hop(self):
        c = self.c
        for w in range(len(self.outs)):
            self.ici(w, 0, self.piece(w, self.bx, c), self.xn).wait_recv()
            self.ici(w, 1, self.piece(w, self.by, c), self.yn).wait_recv()
            self.ici(w, 2, self.piece(w, self.bx, c, 0), self.yn).start()
            self.ici(w, 3, self.piece(w, self.by, c, 1), self.xn).start()
            self.d2d(w, 0, self.piece(w, self.bx, c)).start()
            self.d2d(w, 1, self.piece(w, self.by, c)).start()

    def near_ready(self):
        for w in range(len(self.outs)):
            self.d2d(w, 0, self.piece(w, self.bx, 1 - self.c)).wait_recv()
            self.d2d(w, 1, self.piece(w, self.by, 1 - self.c)).wait_recv()

    def far(self):
        c = self.c
        for w in range(len(self.outs)):
            self.ici(w, 2, self.piece(w, self.bd, c, 0), self.yn).wait_recv()
            self.ici(w, 3, self.piece(w, self.bd, c, 1), self.xn).wait_recv()
            self.d2d(w, 2, self.piece(w, self.bd, c, 0)).start()
            self.d2d(w, 3, self.piece(w, self.bd, c, 1)).start()

    def far_ready(self):
        for w in range(len(self.outs)):
            self.d2d(w, 2, self.piece(w, self.bd, 1 - self.c, 0)).wait_recv()
            self.d2d(w, 3, self.piece(w, self.bd, 1 - self.c, 1)).wait_recv()

    def drain(self):
        c = self.c
        for w in range(len(self.outs)):
            mine = self.piece(w, self.b, c)
            self.ici(w, 0, mine, self.xn).wait_send()
            self.ici(w, 1, mine, self.yn).wait_send()
            self.ici(w, 2, self.piece(w, self.bx, c, 0), self.yn).wait_send()
            self.ici(w, 3, self.piece(w, self.by, c, 1), self.xn).wait_send()
            self.d2d(w, 0, self.piece(w, self.bx, c)).wait_send()
            self.d2d(w, 1, self.piece(w, self.by, c)).wait_send()
            self.d2d(w, 2, self.piece(w, self.bd, c, 0)).wait_send()
            self.d2d(w, 3, self.piece(w, self.bd, c, 1)).wait_send()


def _gather_sems(nw):
    return [pltpu.SemaphoreType.DMA((max(nw, 1), 4))] * 4


def _mixer_fwd(proj, cw8, lng, lnb, wc, bsb, fulls, tm=256):
    t = proj.shape[0]
    nt = t // tm
    nch = tm // CH
    nw = len(fulls)

    def body(*refs):
        p_ref, cw_ref, lng_ref, lnb_ref, wc_ref, bsb_ref = refs[:6]
        mix_ref = refs[6 + nw]
        w_outs = refs[7 + nw:7 + 2 * nw]
        prev_ref = refs[7 + 2 * nw]
        gather = _Gather(w_outs, *refs[8 + 2 * nw:])

        @pl.when(pl.program_id(0) == 0)
        def _():
            gather.start()
            prev_ref[...] = jnp.zeros_like(prev_ref)

        @pl.when(pl.program_id(0) == nt // 2)
        def _():
            gather.hop()

        @pl.when(pl.program_id(0) == nt - 1)
        def _():
            gather.far()

        rows = lax.broadcasted_iota(jnp.int32, (tm, CH), 0)
        for s in range(HEADS):
            cs = pl.ds(CH * s, CH)

            def slab(k):
                return p_ref[:, pl.ds(k * SLAB + CH * s, CH)].astype(F32)

            gb, gc, xa, za = slab(0), slab(1), slab(2), slab(3)
            cx = gc * xa
            p6 = jnp.broadcast_to(prev_ref[6:7, cs], (tm, CH))
            p7 = jnp.broadcast_to(prev_ref[7:8, cs], (tm, CH))
            c1 = jnp.where(rows == 0, p7, pltpu.roll(cx, 1, 0))
            c2 = jnp.where(rows == 0, p6, jnp.where(rows == 1, p7, pltpu.roll(cx, 2, 0)))
            prev_ref[:, cs] = cx[tm - 8:, :]
            cv = cw_ref[0:1, cs] * c2 + cw_ref[1:2, cs] * c1 + cw_ref[2:3, cs] * cx
            mix_ref[:, cs] = (gb * cv * (za * _sigmoid(za))).astype(BF16)

            u, v, zb = slab(4), slab(5), slab(6)
            ug, vg = _gelu(u), _gelu(v)
            dlt = vg - jnp.mean(vg, axis=-1, keepdims=True)
            vhat = dlt * lax.rsqrt(jnp.mean(dlt * dlt, axis=-1, keepdims=True) + EPS)
            vn = (vhat * lng_ref[:, cs] + lnb_ref[:, cs]).astype(BF16)
            gate = ug * (zb * _sigmoid(zb))
            for c in range(nch):
                rs = slice(CH * c, CH * (c + 1))
                sp = jnp.dot(wc_ref[s], vn[rs], preferred_element_type=F32) + bsb_ref[s]
                mix_ref[rs, pl.ds(SLAB + CH * s, CH)] = (gate[rs] * sp).astype(BF16)

        @pl.when(pl.program_id(0) == nt - 1)
        def _():
            gather.near_ready()
            gather.far_ready()
            gather.drain()

    sems = _gather_sems(nw)
    outs = pl.pallas_call(
        body, grid=(nt,),
        in_specs=[pl.BlockSpec((tm, IN_DIM), lambda i: (i, 0)), _full((8, D)), _full((1, D)), _full((1, D)),
                  _full((HEADS, CH, CH)), _full((HEADS, CH, CH))] + [ANY] * nw,
        out_specs=[pl.BlockSpec((tm, MIX), lambda i: (i, 0))] + [ANY] * nw,
        out_shape=[jax.ShapeDtypeStruct((t, MIX), BF16)] + [jax.ShapeDtypeStruct(f.shape, f.dtype) for f in fulls],
        input_output_aliases={6 + w: 1 + w for w in range(nw)},
        scratch_shapes=[pltpu.VMEM((8, D), F32)] + sems,
        compiler_params=_cp(("arbitrary",), VMEM_LIMIT), name="mixer_fwd")(proj, cw8, lng, lnb, wc, bsb, *fulls)
    return outs[0], outs[1:]


def _mem_fwd(mem, gm, wkv_f):
    n_mem = mem.shape[0]

    def body(mem_ref, gm_ref, w_ref, k_ref, v_ref):
        m, _ = _rms(mem_ref[...], gm_ref[...])
        mb = m.astype(BF16)
        for j in range(N_CHIP):
            dst = k_ref if j < 2 else v_ref
            dst[:, pl.ds(KV_BLK * (j % 2), KV_BLK)] = jnp.dot(mb, w_ref[j], preferred_element_type=F32).astype(BF16)

    return pl.pallas_call(
        body, out_shape=[jax.ShapeDtypeStruct((n_mem, D), BF16), jax.ShapeDtypeStruct((n_mem, D), BF16)],
        compiler_params=_cp(None, VMEM_LIMIT), name="mem_fwd")(mem, gm, wkv_f)


def _tail(x, tgt, mixin, wout, wq, wxo, k, v, g2, g3, tm=512, sub=512):
    t = x.shape[0]
    n_mem = k.shape[0]
    scale = 1.0 / math.sqrt(XD)

    def body(x_ref, tgt_ref, mix_ref, wout_ref, wq_ref, wxo_ref, k_ref, v_ref, g2_ref, g3_ref,
             loss_ref, dmix_ref, dx1b_ref, h2_ref, dq_ref, o_ref, dx2b_ref, dk_ref, dv_ref, dg2_ref, dg3_ref):
        @pl.when(pl.program_id(0) == 0)
        def _():
            loss_ref[...] = jnp.zeros_like(loss_ref)
            dk_ref[...] = jnp.zeros_like(dk_ref)
            dv_ref[...] = jnp.zeros_like(dv_ref)
            dg2_ref[...] = jnp.zeros_like(dg2_ref)
            dg3_ref[...] = jnp.zeros_like(dg3_ref)

        g2, g3 = g2_ref[...], g3_ref[...]
        for sb in range(tm // sub):
            rs = pl.ds(sub * sb, sub)
            x1 = x_ref[rs, :] + jnp.dot(mix_ref[rs, :], wout_ref[...], preferred_element_type=F32)
            h2, r2 = _rms(x1, g2)
            h2b = h2.astype(BF16)
            h2_ref[rs, :] = h2b
            q = jnp.dot(h2b, wq_ref[...], preferred_element_type=F32).astype(BF16)
            probs, outs = [], []
            for hd in range(XH):
                hs = pl.ds(XD * hd, XD)
                s = _bdot_nt(q[:, XD * hd:XD * (hd + 1)], k_ref[:, hs]) * scale
                e = jnp.exp(s - jnp.max(s, axis=-1, keepdims=True))
                p = e / jnp.sum(e, axis=-1, keepdims=True)
                probs.append(p)
                outs.append(_bdot(p, v_ref[:, hs]))
            ob = jnp.concatenate(outs, axis=-1).astype(BF16)
            o_ref[rs, :] = ob
            x2 = x1 + jnp.dot(ob, wxo_ref[...], preferred_element_type=F32)
            y, r3 = _rms(x2, g3)
            diff = y - tgt_ref[rs, :]
            row_loss = jnp.sum(diff * diff, axis=-1, keepdims=True)
            loss_ref[...] += jnp.broadcast_to(jnp.sum(row_loss, axis=0, keepdims=True) * (0.5 / D), loss_ref.shape)

            dx2, dg3 = _rms_bwd(diff * (1.0 / D), x2, r3, g3)
            dg3_ref[...] += dg3
            dx2b = dx2.astype(BF16)
            dx2b_ref[rs, :] = dx2b
            do = _bdot_nt(dx2b, wxo_ref[...])
            dqs = []
            for hd in range(XH):
                hs = pl.ds(XD * hd, XD)
                p = probs[hd]
                do_h = do[:, XD * hd:XD * (hd + 1)]
                dv_ref[:, hs] += _bdot_tn(p, do_h)
                dp = _bdot_nt(do_h, v_ref[:, hs])
                ds = p * (dp - jnp.sum(dp * p, axis=-1, keepdims=True))
                dqs.append(_bdot(ds, k_ref[:, hs]) * scale)
                dk_ref[:, hs] += _bdot_tn(ds, q[:, XD * hd:XD * (hd + 1)]) * scale
            dq = jnp.concatenate(dqs, axis=-1).astype(BF16)
            dq_ref[rs, :] = dq
            dx1n, dg2 = _rms_bwd(_bdot_nt(dq, wq_ref[...]), x1, r2, g2)
            dg2_ref[...] += dg2
            dx1b = (dx2 + dx1n).astype(BF16)
            dx1b_ref[rs, :] = dx1b
            dmix_ref[rs, :] = _bdot_nt(dx1b, wout_ref[...]).astype(BF16)

    tok = lambda w: pl.BlockSpec((tm, w), lambda i: (i, 0))
    return pl.pallas_call(
        body, grid=(t // tm,),
        in_specs=[tok(D), tok(D), tok(MIX), _full((MIX, D), 1), _full((D, D), 1), _full((D, D), 1),
                  _full((n_mem, D), 1), _full((n_mem, D), 1), _full((1, D)), _full((1, D))],
        out_specs=[_full((8, 128)), tok(MIX), tok(D), tok(D), tok(D), tok(D), tok(D),
                   _full((n_mem, D)), _full((n_mem, D)), _full((1, D)), _full((1, D))],
        out_shape=[jax.ShapeDtypeStruct((8, 128), F32), jax.ShapeDtypeStruct((t, MIX), BF16),
                   jax.ShapeDtypeStruct((t, D), BF16),
                   jax.ShapeDtypeStruct((t, D), BF16), jax.ShapeDtypeStruct((t, D), BF16),
                   jax.ShapeDtypeStruct((t, D), BF16), jax.ShapeDtypeStruct((t, D), BF16),
                   jax.ShapeDtypeStruct((n_mem, D), F32), jax.ShapeDtypeStruct((n_mem, D), F32),
                   jax.ShapeDtypeStruct((1, D), F32), jax.ShapeDtypeStruct((1, D), F32)],
        compiler_params=_cp(("arbitrary",), VMEM_LIMIT), name="tail")(x, tgt, mixin, wout, wq, wxo, k, v, g2, g3)


def _mem_bwd(mem, gm, dk, dv, wkv_f):
    def body(mem_ref, gm_ref, dk_ref, dv_ref, w_ref, dw_ref, dwb_ref, dgm_ref):
        mem_v = mem_ref[...]
        m, rm = _rms(mem_v, gm_ref[...])
        mb = m.astype(BF16)
        dm = jnp.zeros_like(mem_v)
        for j in range(N_CHIP):
            src = dk_ref if j < 2 else dv_ref
            dkv = src[:, pl.ds(KV_BLK * (j % 2), KV_BLK)].astype(BF16)
            dw = _bdot_tn(mb, dkv)
            dw_ref[j] = dw
            dwb_ref[j] = dw.astype(BF16)
            dm = dm + _bdot_nt(dkv, w_ref[j])
        dgm_ref[...] = jnp.sum(dm * mem_v * rm, axis=0, keepdims=True)

    return pl.pallas_call(
        body, out_shape=[jax.ShapeDtypeStruct((N_CHIP, D, KV_BLK), F32), jax.ShapeDtypeStruct((N_CHIP, D, KV_BLK), BF16),
                         jax.ShapeDtypeStruct((1, D), F32)],
        compiler_params=_cp(None, VMEM_LIMIT), name="mem_bwd")(mem, gm, dk, dv, wkv_f)


def _mixer_bwd(proj, dmix, cw8, lng, lnb, wc, wct, bsb, win_f, x, dx1, g1, tm=256):
    t = proj.shape[0]
    nt = t // tm
    nch = tm // CH
    hb = 16
    pair = 2 * CH

    def body(p_ref, pgc_ref, pxa_ref, dm_ref, cw_ref, lng_ref, lnb_ref, wc_ref, wct_ref, bsb_ref, w_ref, x_ref,
             dx1_ref, g1_ref, dp_ref, dcw_ref, dlng_ref, dlnb_ref, dwc_ref, dbs_ref, gx_ref, dg1_ref,
             next_ref, dh_ref):
        i = pl.program_id(0)

        @pl.when(i == 0)
        def _():
            next_ref[...] = jnp.zeros_like(next_ref)
            dcw_ref[...] = jnp.zeros_like(dcw_ref)
            dlng_ref[...] = jnp.zeros_like(dlng_ref)
            dlnb_ref[...] = jnp.zeros_like(dlnb_ref)
            dwc_ref[...] = jnp.zeros_like(dwc_ref)
            dbs_ref[...] = jnp.zeros_like(dbs_ref)
            dg1_ref[...] = jnp.zeros_like(dg1_ref)

        first_tile = i == nt - 1
        rows = lax.broadcasted_iota(jnp.int32, (tm, CH), 0)
        ones8 = jnp.ones((8, CH), BF16)
        for s in range(HEADS):
            cs = pl.ds(CH * s, CH)

            def slab(k):
                return p_ref[:, pl.ds(k * SLAB + CH * s, CH)].astype(F32)

            gb, gc, xa, za = slab(0), slab(1), slab(2), slab(3)
            da = dm_ref[:, cs].astype(F32)
            cx = gc * xa
            cxp = pgc_ref[:, cs].astype(F32) * pxa_ref[:, cs].astype(F32)
            cxp = jnp.where(first_tile, jnp.zeros_like(cxp), cxp)
            p6 = jnp.broadcast_to(cxp[hb - 2:hb - 1, :], (tm, CH))
            p7 = jnp.broadcast_to(cxp[hb - 1:hb, :], (tm, CH))
            c1 = jnp.where(rows == 0, p7, pltpu.roll(cx, 1, 0))
            c2 = jnp.where(rows == 0, p6, jnp.where(rows == 1, p7, pltpu.roll(cx, 2, 0)))
            w0, w1, w2 = cw_ref[0:1, cs], cw_ref[1:2, cs], cw_ref[2:3, cs]
            cv = w0 * c2 + w1 * c1 + w2 * cx
            sg = _sigmoid(za)
            sa = za * sg
            dcv = da * gb * sa
            dp_ref[:, pl.ds(0 * SLAB + CH * s, CH)] = (da * cv * sa).astype(BF16)
            dp_ref[:, pl.ds(3 * SLAB + CH * s, CH)] = (da * gb * cv * (sg * (1.0 + za * (1.0 - sg)))).astype(BF16)
            n0 = jnp.broadcast_to(next_ref[0:1, cs], (tm, CH))
            n1 = jnp.broadcast_to(next_ref[1:2, cs], (tm, CH))
            u1 = jnp.where(rows == tm - 1, n0, pltpu.roll(dcv, tm - 1, 0))
            u2 = jnp.where(rows == tm - 2, n0, jnp.where(rows == tm - 1, n1, pltpu.roll(dcv, tm - 2, 0)))
            next_ref[:, cs] = dcv[0:8, :]
            dcx = w2 * dcv + w1 * u1 + w0 * u2
            dp_ref[:, pl.ds(1 * SLAB + CH * s, CH)] = (dcx * xa).astype(BF16)
            dp_ref[:, pl.ds(2 * SLAB + CH * s, CH)] = (dcx * gc).astype(BF16)
            dcw_ref[0:1, cs] += jnp.sum(dcv * c2, axis=0, keepdims=True)
            dcw_ref[1:2, cs] += jnp.sum(dcv * c1, axis=0, keepdims=True)
            dcw_ref[2:3, cs] += jnp.sum(dcv * cx, axis=0, keepdims=True)

            u, v, zb = slab(4), slab(5), slab(6)
            db = dm_ref[:, pl.ds(SLAB + CH * s, CH)].astype(F32)
            ug, ugrad = _gelu_parts(u)
            vg, vgrad = _gelu_parts(v)
            dlt = vg - jnp.mean(vg, axis=-1, keepdims=True)
            rstd = lax.rsqrt(jnp.mean(dlt * dlt, axis=-1, keepdims=True) + EPS)
            vhat = dlt * rstd
            lg = lng_ref[:, cs]
            vn = (vhat * lg + lnb_ref[:, cs]).astype(BF16)
            sgb = _sigmoid(zb)
            szb = zb * sgb
            sps, dvns = [], []
            dbs = jnp.zeros((8, CH), F32)
            dwc = jnp.zeros((CH, CH), F32)
            for c in range(nch):
                rs = slice(CH * c, CH * (c + 1))
                sp = jnp.dot(wc_ref[s], vn[rs], preferred_element_type=F32) + bsb_ref[s]
                dsp = (db[rs] * ug[rs] * szb[rs]).astype(BF16)
                dbs = dbs + lax.dot_general(ones8, dsp, (((1,), (1,)), ((), ())), preferred_element_type=F32)
                dwc = dwc + lax.dot_general(dsp, vn[rs], (((1,), (1,)), ((), ())), preferred_element_type=F32)
                dvns.append(jnp.dot(wct_ref[s], dsp, preferred_element_type=F32))
                sps.append(sp)
            sp = jnp.concatenate(sps, axis=0)
            dvn = jnp.concatenate(dvns, axis=0)
            dbs_ref[:, cs] += dbs
            dwc_ref[s] += dwc
            dlng_ref[:, cs] += jnp.sum(dvn * vhat, axis=0, keepdims=True)
            dlnb_ref[:, cs] += jnp.sum(dvn, axis=0, keepdims=True)
            dvhat = dvn * lg
            dvg = rstd * (dvhat - jnp.mean(dvhat, axis=-1, keepdims=True)
                          - vhat * jnp.mean(dvhat * vhat, axis=-1, keepdims=True))
            dp_ref[:, pl.ds(4 * SLAB + CH * s, CH)] = (db * sp * szb * ugrad).astype(BF16)
            dp_ref[:, pl.ds(5 * SLAB + CH * s, CH)] = (dvg * vgrad).astype(BF16)
            dp_ref[:, pl.ds(6 * SLAB + CH * s, CH)] = (db * ug * sp * (sgb * (1.0 + zb * (1.0 - sgb)))).astype(BF16)

            if s % 2 == 1:
                part = None
                for k in range(N_SLAB):
                    col = k * SLAB + pair * (s // 2)
                    blk, off = divmod(col, IN_BLK)
                    term = lax.dot_general(dp_ref[:, pl.ds(col, pair)], w_ref[blk, :, pl.ds(off, pair)],
                                           (((1,), (1,)), ((), ())), preferred_element_type=F32)
                    part = term if part is None else part + term
                if s == 1:
                    dh_ref[...] = part
                else:
                    dh_ref[...] += part

        xv = x_ref[...]
        r = lax.rsqrt(jnp.mean(xv * xv, axis=-1, keepdims=True) + EPS)
        dxn, dg = _rms_bwd(dh_ref[...], xv, r, g1_ref[...])
        gx_ref[...] = dx1_ref[...].astype(F32) + dxn
        dg1_ref[0:1, :] += dg

        @pl.when(i == nt - 1)
        def _():
            tril = lax.broadcasted_iota(jnp.int32, (CH, CH), 0) >= lax.broadcasted_iota(jnp.int32, (CH, CH), 1)
            for s in range(HEADS):
                dwc_ref[s] = jnp.where(tril, dwc_ref[s], 0.0)

    rev = lambda i: nt - 1 - i
    halo = lambda col: pl.BlockSpec((hb, SLAB), lambda i: (jnp.maximum(rev(i) * (tm // hb) - 1, 0), col))
    tok = lambda w: pl.BlockSpec((tm, w), lambda i: (rev(i), 0))
    return pl.pallas_call(
        body, grid=(nt,),
        in_specs=[tok(IN_DIM), halo(1), halo(2), tok(MIX), _full((8, D)), _full((1, D)), _full((1, D)),
                  _full((HEADS, CH, CH)), _full((HEADS, CH, CH)), _full((HEADS, CH, CH)),
                  _full((N_CHIP, D, IN_BLK), 1), tok(D), tok(D), _full((1, D))],
        out_specs=[tok(IN_DIM), _full((8, D)), _full((1, D)), _full((1, D)), _full((HEADS, CH, CH)), _full((8, D)),
                   tok(D), _full((8, D))],
        out_shape=[jax.ShapeDtypeStruct((t, IN_DIM), BF16), jax.ShapeDtypeStruct((8, D), F32),
                   jax.ShapeDtypeStruct((1, D), F32), jax.ShapeDtypeStruct((1, D), F32),
                   jax.ShapeDtypeStruct((HEADS, CH, CH), F32), jax.ShapeDtypeStruct((8, D), F32),
                   jax.ShapeDtypeStruct((t, D), F32), jax.ShapeDtypeStruct((8, D), F32)],
        scratch_shapes=[pltpu.VMEM((8, D), F32), pltpu.VMEM((tm, D), F32)],
        compiler_params=_cp(("arbitrary",), VMEM_LIMIT), name="mixer_bwd")(
            proj, proj, proj, dmix, cw8, lng, lnb, wc, wct, bsb, win_f, x, dx1, g1)


def _grad_matmul(a, b, after, *, by_cols, name, tk=1024):
    t, m = a.shape
    n = b.shape[1]
    nk = t // tk
    nj = N_CHIP if by_cols else 1
    bn = n // nj

    def body(a_ref, b_ref, after_ref, o_ref, ob_ref):
        kk = pl.program_id(1)
        part = lax.dot_general(a_ref[...], b_ref[...], (((0,), (0,)), ((), ())), preferred_element_type=F32)

        @pl.when(kk == 0)
        def _():
            o_ref[...] = part

        @pl.when(kk > 0)
        def _():
            o_ref[...] += part

        @pl.when(kk == nk - 1)
        def _():
            ob_ref[...] = o_ref[...].astype(BF16)

    a_spec = pl.BlockSpec((tk, m), lambda j, k: (k, 0))
    b_spec = pl.BlockSpec((tk, bn), lambda j, k: (k, j))
    o_spec = pl.BlockSpec((None, m, bn), lambda j, k: (j, 0, 0))
    o32, o16 = pl.pallas_call(
        body, grid=(nj, nk), in_specs=[a_spec, b_spec, ANY], out_specs=[o_spec, o_spec],
        out_shape=[jax.ShapeDtypeStruct((nj, m, bn), F32), jax.ShapeDtypeStruct((nj, m, bn), BF16)],
        compiler_params=_cp(("parallel", "arbitrary"), VMEM_LIMIT), name=name)(a, b, after)
    if by_cols:
        return o32, o16
    return o32.reshape(N_CHIP, m // N_CHIP, n), o16.reshape(N_CHIP, m // N_CHIP, n)


def _coords():
    x, y, c = lax.axis_index("x"), lax.axis_index("y"), lax.axis_index("c")
    chips = [(1 - x, y), (x, 1 - y), (1 - x, 1 - y)]
    return x, y, c, chips


def _pair_exchange(grads_b, smalls, name):
    ng, ns = len(grads_b), len(smalls)
    n = ng + ns

    def body(*refs):
        ins, outs, send, recv = refs[:n], refs[n:2 * n], refs[2 * n], refs[2 * n + 1]
        x, y, c, _ = _coords()
        cps = []
        for i in range(n):
            if i < ng:
                hr = ins[i].shape[1] // 2
                src = ins[i].at[pl.ds(0, N_CHIP), pl.ds((1 - c) * hr, hr)]
            else:
                hr = ins[i].shape[0] // 2
                src = ins[i].at[pl.ds((1 - c) * hr, hr)]
            cps.append(pltpu.make_async_remote_copy(
                src_ref=src, dst_ref=outs[i], send_sem=send.at[i], recv_sem=recv.at[i],
                device_id=(x, y, 1 - c), device_id_type=MESH))
        for cp in cps:
            cp.start()
        for cp in cps:
            cp.wait()

    out_shape = [jax.ShapeDtypeStruct((N_CHIP, g.shape[1] // 2, g.shape[2]), g.dtype) for g in grads_b]
    out_shape += [jax.ShapeDtypeStruct((s.shape[0] // 2, s.shape[1]), s.dtype) for s in smalls]
    return pl.pallas_call(
        body, out_shape=out_shape, in_specs=[ANY] * n, out_specs=[ANY] * n,
        scratch_shapes=[pltpu.SemaphoreType.DMA((n,)), pltpu.SemaphoreType.DMA((n,))],
        name=name)(*grads_b, *smalls)


def _pair_sum(c_idx, grads, recvd, smalls, smalls_recvd, name):
    ng, ns = len(grads), len(smalls)
    halves = [g.shape[1] // 2 for g in grads]

    def body(c_ref, *refs):
        g_in, r_in = refs[:ng], refs[ng:2 * ng]
        s_in, sr_in = refs[2 * ng:2 * ng + ns], refs[2 * ng + ns:2 * ng + 2 * ns]
        o = refs[2 * ng + 2 * ns:]
        for i in range(ng):
            tot = g_in[i][...] + r_in[i][...].astype(F32)
            o[i][...] = tot
            o[ng + i][...] = tot.astype(BF16)
        for i in range(ns):
            o[2 * ng + i][...] = s_in[i][...] + sr_in[i][...]

    steps = 4
    in_specs = [pl.BlockSpec((None, None, halves[i] // steps, g.shape[2]), lambda b, r, c: (b, c[0], r, 0))
                for i, g in enumerate(grads)]
    in_specs += [pl.BlockSpec((None, halves[i] // steps, g.shape[2]), lambda b, r, c: (b, r, 0)) for i, g in enumerate(grads)]
    in_specs += [pl.BlockSpec((None, s.shape[0] // 2, s.shape[1]), lambda b, r, c: (c[0], 0, 0)) for s in smalls]
    in_specs += [pl.BlockSpec((s.shape[0] // 2, s.shape[1]), lambda b, r, c: (0, 0)) for s in smalls]
    blk = [pl.BlockSpec((None, halves[i] // steps, g.shape[2]), lambda b, r, c: (b, r, 0)) for i, g in enumerate(grads)]
    out_specs = blk + blk + [pl.BlockSpec((s.shape[0] // 2, s.shape[1]), lambda b, r, c: (0, 0)) for s in smalls]
    out_shape = [jax.ShapeDtypeStruct((N_CHIP, halves[i], g.shape[2]), F32) for i, g in enumerate(grads)]
    out_shape += [jax.ShapeDtypeStruct((N_CHIP, halves[i], g.shape[2]), BF16) for i, g in enumerate(grads)]
    out_shape += [jax.ShapeDtypeStruct((s.shape[0] // 2, s.shape[1]), F32) for s in smalls]
    grads4 = [g.reshape(N_CHIP, 2, halves[i], g.shape[2]) for i, g in enumerate(grads)]
    smalls3 = [s.reshape(2, s.shape[0] // 2, s.shape[1]) for s in smalls]
    return pl.pallas_call(
        body, out_shape=out_shape,
        grid_spec=pltpu.PrefetchScalarGridSpec(num_scalar_prefetch=1, grid=(N_CHIP, steps), in_specs=in_specs,
                                               out_specs=out_specs),
        compiler_params=_cp(("arbitrary", "arbitrary"), VMEM_LIMIT), name=name)(c_idx, *grads4, *recvd, *smalls3, *smalls_recvd)


_HBM = pl.BlockSpec(memory_space=pltpu.HBM)
_SEM = pl.BlockSpec(memory_space=pltpu.SEMAPHORE)


def _split_copies(ins, lands, ng, send, recv, arriving):
    x, y, c, chips = _coords()
    b = 2 * x + y
    copies = []
    for i in range(len(ins)):
        for k in range(3):
            blk = 2 * chips[k][0] + chips[k][1]
            src, dst, got = (ins[i].at[blk], lands[i].at[k], lands[i].at[k]) if i < ng else (ins[i], lands[i].at[b], lands[i].at[blk])
            sems = dict(send_sem=send.at[3 * i + k], recv_sem=recv.at[3 * i + k], device_id=(*chips[k], c), device_id_type=MESH)
            if arriving:
                copies.append(pltpu.make_async_remote_copy(src_ref=got, dst_ref=got, **sems))
            else:
                copies.append(pltpu.make_async_remote_copy(src_ref=src, dst_ref=dst, **sems))
    return copies


def _exchange_begin(sums_b, smalls, name):
    ng, n = len(sums_b), len(sums_b) + len(smalls)
    srcs = list(sums_b) + list(smalls)
    lands = [lax.empty((3,) + g.shape[1:], g.dtype) for g in sums_b] + [lax.empty((N_CHIP,) + s.shape, s.dtype) for s in smalls]

    def body(*refs):
        ins, land_refs = refs[:n], refs[n:2 * n]
        send, recv = refs[2 * n], refs[2 * n + 1]
        token = refs[4 * n + 2]
        for cp in _split_copies(ins, land_refs, ng, send, recv, False):
            cp.start()
        token[...] = jnp.zeros_like(token)

    hbm = lambda a: pltpu.HBM(a.shape, a.dtype)
    outs = pl.pallas_call(
        body, name=name,
        out_shape=(pltpu.SemaphoreType.DMA((3 * n,)), pltpu.SemaphoreType.DMA((3 * n,)), *[hbm(a) for a in srcs + lands],
                   jax.ShapeDtypeStruct((8, 128), F32)),
        in_specs=[_HBM] * (2 * n), out_specs=(_SEM, _SEM, *[_HBM] * (2 * n), pl.BlockSpec(memory_space=pltpu.VMEM)),
        input_output_aliases={i: 2 + i for i in range(2 * n)},
        compiler_params=pltpu.CompilerParams(has_side_effects=pltpu.SideEffectType.DATAFLOW_SIDE_EFFECTING),
    )(*[pltpu.with_memory_space_constraint(a, pltpu.HBM) for a in srcs + lands])
    return outs[0], outs[1], list(outs[2:2 + n]), list(outs[2 + n:2 + 2 * n]), outs[2 + 2 * n]


def _exchange_end(send, recv, srcs, lands, ng, after, name):
    n = len(srcs)
    after = list(after)

    def body(*refs):
        ins, land_refs = refs[:n], refs[n:2 * n]
        send_ref, recv_ref = refs[2 * n], refs[2 * n + 1]
        for cp in _split_copies(ins, land_refs, ng, send_ref, recv_ref, False):
            cp.wait_send()
        for cp in _split_copies(ins, land_refs, ng, send_ref, recv_ref, True):
            cp.wait_recv()

    hbm = lambda a: pltpu.HBM(a.shape, a.dtype)
    outs = pl.pallas_call(
        body, name=name, out_shape=tuple(hbm(a) for a in list(srcs) + list(lands)),
        in_specs=[_HBM] * (2 * n) + [_SEM, _SEM] + [ANY] * len(after), out_specs=tuple([_HBM] * (2 * n)),
        input_output_aliases={i: i for i in range(2 * n)},
        compiler_params=pltpu.CompilerParams(has_side_effects=pltpu.SideEffectType.DATAFLOW_SIDE_EFFECTING),
    )(*srcs, *lands, send, recv, *after)
    return list(outs[n:])


def _chip_sum(bc_idx, sums, recvd, smalls_slots, smalls_own, name, steps=8):
    ng, ns = len(sums), len(smalls_slots)

    def body(bc_ref, *refs):
        own, rx = refs[:ng], refs[ng:2 * ng]
        sl = refs[2 * ng:2 * ng + ns]
        sl_own = refs[2 * ng + ns:2 * ng + 2 * ns]
        o = refs[2 * ng + 2 * ns:]
        for i in range(ng):
            tot = own[i][...]
            for j in range(3):
                tot = tot + rx[i][j].astype(F32)
            o[i][...] = tot
        for i in range(ns):
            term = [jnp.where(bc_ref[0] == kk, sl_own[i][...], sl[i][kk]) for kk in range(N_CHIP)]
            o[ng + i][...] = ((term[0] + term[1]) + term[2]) + term[3]

    def rows(g):
        return g.shape[1] // steps

    in_specs = [pl.BlockSpec((None, rows(g), g.shape[2]), lambda r, bc: (bc[0], r, 0)) for g in sums]
    in_specs += [pl.BlockSpec((3, rows(g), g.shape[2]), lambda r, bc: (0, r, 0)) for g in sums]
    in_specs += [pl.BlockSpec(s.shape, lambda r, bc: (0, 0, 0)) for s in smalls_slots]
    in_specs += [pl.BlockSpec(s.shape[1:], lambda r, bc: (0, 0)) for s in smalls_slots]
    out_specs = [pl.BlockSpec((rows(g), g.shape[2]), lambda r, bc: (bc[1] * steps + r, 0)) for g in sums]
    out_specs += [pl.BlockSpec(s.shape[1:], lambda r, bc: (bc[1], 0)) for s in smalls_slots]
    out_shape = [jax.ShapeDtypeStruct((2 * g.shape[1], g.shape[2]), F32) for g in sums]
    out_shape += [jax.ShapeDtypeStruct((2 * s.shape[1], s.shape[2]), F32) for s in smalls_slots]
    return pl.pallas_call(
        body, out_shape=out_shape,
        grid_spec=pltpu.PrefetchScalarGridSpec(num_scalar_prefetch=1, grid=(steps,), in_specs=in_specs, out_specs=out_specs),
        compiler_params=_cp(("arbitrary",), VMEM_LIMIT), name=name)(bc_idx, *sums, *recvd, *smalls_slots, *smalls_own)


def _pair_gather(arrs, name):
    n = len(arrs)

    def body(*refs):
        outs = refs[n:2 * n]
        send, recv = refs[2 * n:]
        x, y, c, _ = _coords()
        cps = []
        for i in range(n):
            hr = outs[i].shape[0] // 2
            mine = outs[i].at[pl.ds(c * hr, hr)]
            cps.append(pltpu.make_async_remote_copy(
                src_ref=mine, dst_ref=mine, send_sem=send.at[i], recv_sem=recv.at[i],
                device_id=(x, y, 1 - c), device_id_type=MESH))
        for cp in cps:
            cp.start()
        for i in range(n):
            hr = outs[i].shape[0] // 2
            theirs = outs[i].at[pl.ds((1 - c) * hr, hr)]
            pltpu.make_async_remote_copy(
                src_ref=theirs, dst_ref=theirs, send_sem=send.at[i], recv_sem=recv.at[i],
                device_id=(x, y, 1 - c), device_id_type=MESH).wait_recv()
        for cp in cps:
            cp.wait_send()

    return list(pl.pallas_call(
        body, out_shape=[jax.ShapeDtypeStruct(a.shape, a.dtype) for a in arrs], in_specs=[ANY] * n, out_specs=[ANY] * n,
        input_output_aliases={i: i for i in range(n)},
        scratch_shapes=[pltpu.SemaphoreType.DMA((n,)), pltpu.SemaphoreType.DMA((n,))],
        name=name)(*arrs))


def _adamw_math(w, g, m, v):
    m2 = ADAM_B1 * m + (1.0 - ADAM_B1) * g
    v2 = ADAM_B2 * v + (1.0 - ADAM_B2) * (g * g)
    m_hat = m2 / (1.0 - ADAM_B1 ** ADAM_STEP)
    v_hat = v2 / (1.0 - ADAM_B2 ** ADAM_STEP)
    delta = -ADAM_LR * (m_hat / (jnp.sqrt(v_hat) + ADAM_EPS) + ADAM_WD * w)
    return delta, m2, v2


def _adamw_big(ws, gs, ms, vs, name, steps=8):
    n = len(ws)

    def body(*refs):
        for i in range(n):
            w_ref, g_ref, m_ref, v_ref = (refs[k * n + i] for k in range(4))
            d_ref, m2_ref, v2_ref, g2_ref = (refs[(4 + k) * n + i] for k in range(4))
            gv = g_ref[...]
            d_ref[...], m2_ref[...], v2_ref[...] = _adamw_math(w_ref[...], gv, m_ref[...], v_ref[...])
            g2_ref[...] = gv

    specs = [pl.BlockSpec((w.shape[0] // steps, w.shape[1]), lambda i: (i, 0)) for w in ws]
    shapes = [jax.ShapeDtypeStruct(w.shape, F32) for w in ws]
    outs = pl.pallas_call(
        body, grid=(steps,), in_specs=specs * 4, out_specs=specs * 4, out_shape=shapes * 4,
        compiler_params=_cp(("parallel",), VMEM_LIMIT), name=name)(*ws, *gs, *ms, *vs)
    return [tuple(outs[k * n + i] for k in range(4)) for i in range(n)]


def _adamw_small(groups):
    n = len(groups)

    def body(*refs):
        for i in range(n):
            w_ref, g_ref, m_ref, v_ref = refs[4 * i:4 * i + 4]
            d_ref, m2_ref, v2_ref = refs[4 * n + 3 * i:4 * n + 3 * i + 3]
            d_ref[...], m2_ref[...], v2_ref[...] = _adamw_math(w_ref[...], g_ref[...], m_ref[...], v_ref[...])

    flat = [a for grp in groups for a in grp]
    out_shape = [jax.ShapeDtypeStruct(grp[0].shape, F32) for grp in groups for _ in range(3)]
    outs = pl.pallas_call(body, out_shape=out_shape, name="adamw_small")(*flat)
    return [tuple(outs[3 * i:3 * i + 3]) for i in range(n)]


def kernel(x, mem, norm_mix_g, w_in, conv_w, gm_ln_g, gm_ln_b, gm_ws, gm_bs, w_out, norm_x_g, norm_mem_g, w_q, w_kv, w_xo, norm_final_g, loss_target, m_norm_mix_g, m_w_in, m_conv_w, m_gm_ln_g, m_gm_ln_b, m_gm_ws, m_gm_bs, m_w_out, m_norm_x_g, m_norm_mem_g, m_w_q, m_w_kv, m_w_xo, m_norm_final_g, v_norm_mix_g, v_w_in, v_conv_w, v_gm_ln_g, v_gm_ln_b, v_gm_ws, v_gm_bs, v_w_out, v_norm_x_g, v_norm_mem_g, v_w_q, v_w_kv, v_w_xo, v_norm_final_g):
    t = x.shape[1]
    xi = lax.axis_index("x")
    yi = lax.axis_index("y")
    ci = lax.axis_index("c")
    b_idx = jnp.reshape(2 * xi + yi, (1,)).astype(jnp.int32)
    c_idx = jnp.reshape(ci, (1,)).astype(jnp.int32)

    x2d, mem2d, tgt = x[0], mem[0], loss_target[0]
    big = [w_in[0], w_out[0], w_q[0], w_kv[0], w_xo[0]]
    big_m = [m_w_in[0], m_w_out[0], m_w_q[0], m_w_kv[0], m_w_xo[0]]
    big_v = [v_w_in[0], v_w_out[0], v_w_q[0], v_w_kv[0], v_w_xo[0]]
    g3 = norm_final_g.reshape(1, D)

    def pad8(a):
        return jnp.pad(a, ((0, 8 - a.shape[0]), (0, 0)))

    own_blocks = _cast_shards(b_idx, big)

    tril = jnp.tril(jnp.ones((CH, CH), bool))
    wc32 = jnp.where(tril[None], gm_ws[0], 0.0)
    wc = wc32.astype(BF16)
    wct = jnp.swapaxes(wc32, 1, 2).astype(BF16)
    bsb = jnp.broadcast_to(gm_bs[0][:, :, None], (HEADS, CH, CH))

    blk = 2 * xi + yi
    order = jnp.stack([blk, blk ^ 2, blk ^ 1, blk ^ 3]).astype(jnp.int32)
    proj, hb, win_f, cw8, (wout_f,) = _proj_gather(
        order, x2d, norm_mix_g, own_blocks[0], pad8(conv_w[0]), [own_blocks[1]])
    mixin, (wq_f, wkv_f, wxo_f) = _mixer_fwd(proj, cw8, gm_ln_g, gm_ln_b, wc, bsb, own_blocks[2:])
    wout2, wq2, wxo2 = wout_f.reshape(MIX, D), wq_f.reshape(D, D), wxo_f.reshape(D, D)
    k, v = _mem_fwd(mem2d, norm_mem_g, wkv_f)

    (loss_tile, dmix, dx1b, h2b, dq, ob, dx2b, dk, dv, dg2, dg3) = _tail(
        x2d, tgt, mixin, wout2, wq2, wxo2, k, v, norm_x_g, g3)
    dwkv, dwkv_b, dgm = _mem_bwd(mem2d, norm_mem_g, dk, dv, wkv_f)
    dproj, dcw, dlng, dlnb, dwc, dbs8, grad_x, dg1 = _mixer_bwd(
        proj, dmix, cw8, gm_ln_g, gm_ln_b, wc, wct, bsb, win_f, x2d, dx1b, norm_mix_g)

    bc_idx = jnp.concatenate([b_idx, c_idx])
    dwin, dwin_b = _grad_matmul(hb, dproj, dgm, by_cols=True, name="grad_w_in", tk=2048)
    zero = jnp.zeros((1, D), F32)
    loss_row = jnp.broadcast_to(loss_tile[0:1, 0:1], (1, D))
    sv = jnp.concatenate([dg1[0:1], dg2, dgm, dg3, dlng, dlnb, dbs8[0:1], loss_row, dcw], axis=0)
    sw = dwc.reshape(HEADS * CH, CH)
    rx1b = _pair_exchange([dwin_b], [sv, sw], "pair_exchange_b")
    ps_b = _pair_sum(c_idx, [dwin], rx1b[:1], [sv, sw], rx1b[1:], "pair_sum_b")
    sums_b, sums_b_b, psmall = list(ps_b[:1]), list(ps_b[1:2]), list(ps_b[2:])
    send_b, recv_b, src_b, land_b, token_b = _exchange_begin(sums_b_b, psmall, "exchange_b_begin")

    dwxo, dwxo_b = _grad_matmul(ob, dx2b, token_b, by_cols=False, name="grad_w_xo", tk=2048)
    dwq, dwq_b = _grad_matmul(h2b, dq, token_b, by_cols=False, name="grad_w_q", tk=2048)
    dwout, dwout_b = _grad_matmul(mixin, dx1b, token_b, by_cols=False, name="grad_w_out")
    rx1a = _pair_exchange([dwout_b, dwq_b, dwkv_b, dwxo_b], [], "pair_exchange_a")
    ps_a = _pair_sum(c_idx, [dwout, dwq, dwkv, dwxo], rx1a, [], [], "pair_sum_a")
    sums_a, sums_a_b = list(ps_a[:4]), list(ps_a[4:8])
    send_a, recv_a, src_a, land_a, token_a = _exchange_begin(sums_a_b, [], "exchange_a_begin")

    rx2b = _exchange_end(send_b, recv_b, src_b, land_b, 1, [token_a], "exchange_b_end")
    red_b = _chip_sum(bc_idx, sums_b, rx2b[:1], rx2b[1:], psmall, "chip_sum_b")
    gwin, svf, swf = _pair_gather(red_b, "pair_gather_b")
    out_b = _adamw_big(big[:1], [gwin], big_m[:1], big_v[:1], "adamw_w_in")[0]

    def vec_pack(a1, a2, am, a3, lg, lb, bs):
        return jnp.concatenate([a1, a2, am, a3.reshape(1, D), lg, lb, bs.reshape(1, D), zero], axis=0)

    wv = vec_pack(norm_mix_g, norm_x_g, norm_mem_g, norm_final_g, gm_ln_g, gm_ln_b, gm_bs)
    mv = vec_pack(m_norm_mix_g, m_norm_x_g, m_norm_mem_g, m_norm_final_g, m_gm_ln_g, m_gm_ln_b, m_gm_bs)
    vv = vec_pack(v_norm_mix_g, v_norm_x_g, v_norm_mem_g, v_norm_final_g, v_gm_ln_g, v_gm_ln_b, v_gm_bs)
    loss = svf[7, 0]
    gcw = lax.dynamic_slice_in_dim(svf[8:16], blk * (D // N_CHIP), D // N_CHIP, axis=1)
    gws = swf
    gv = svf[0:8]
    (dv_, mv_, vv_), (dc_, mc_, vc_), (dws_, mws_, vws_) = _adamw_small([
        (wv, gv, mv, vv),
        (pad8(conv_w[0]), gcw, pad8(m_conv_w[0]), pad8(v_conv_w[0])),
        (gm_ws.reshape(HEADS * CH, CH), gws, m_gm_ws.reshape(HEADS * CH, CH), v_gm_ws.reshape(HEADS * CH, CH))])

    rx2a = _exchange_end(send_a, recv_a, src_a, land_a, 4, [out_b[0], dv_], "exchange_a_end")
    red_a = _chip_sum(bc_idx, sums_a, rx2a, [], [], "chip_sum_a")
    g_a = _pair_gather(red_a, "pair_gather_a")
    big_out = [out_b] + _adamw_big(big[1:], g_a, big_m[1:], big_v[1:], "adamw_rest")

    def unpack(vecs, cw, ws, bigs):
        r = lambda i: vecs[i:i + 1]
        return [r(0), bigs[0][None], cw[0:3][None], r(4), r(5), ws.reshape(1, HEADS, CH, CH), vecs[6].reshape(1, HEADS, CH),
                bigs[1][None], r(1), r(2), bigs[2][None], bigs[3][None], bigs[4][None], vecs[3]]

    grads_out = unpack(gv, gcw, gws, [o[3] for o in big_out])
    delta_out = unpack(dv_, dc_, dws_, [o[0] for o in big_out])
    m_out = unpack(mv_, mc_, mws_, [o[1] for o in big_out])
    v_out = unpack(vv_, vc_, vws_, [o[2] for o in big_out])
    return (loss, grad_x[None], *grads_out, *delta_out, *m_out, *v_out)
```

```python
import functools
import math

import jax
import jax.numpy as jnp
from jax import lax
from jax.experimental import pallas as pl
from jax.experimental.pallas import tpu as pltpu

F32 = jnp.float32
BF16 = jnp.bfloat16
MESH = pl.DeviceIdType.MESH

D = 1024
SLAB = 1024
N_SLAB = 7
IN_DIM = N_SLAB * SLAB
MIX = 2 * SLAB
HEADS = 8
CH = 128
XH = 4
XD = D // XH
EPS = 1e-6
GELU_C = math.sqrt(2.0 / math.pi)
GELU_A = 0.044715
N_CHIP = 4
IN_BLK = IN_DIM // N_CHIP
KV_BLK = 2 * D // N_CHIP

ADAM_LR, ADAM_B1, ADAM_B2, ADAM_EPS, ADAM_WD, ADAM_STEP = 0.001, 0.9, 0.999, 1e-08, 0.01, 10

VMEM_LIMIT = 60 * 1024 * 1024


def _cp(sem=None, vmem=None):
    return pltpu.CompilerParams(dimension_semantics=sem, vmem_limit_bytes=vmem)


def _full(shape, buffers=None):
    n = len(shape)
    if buffers is None:
        return pl.BlockSpec(shape, lambda *_: (0,) * n)
    return pl.BlockSpec(shape, lambda *_: (0,) * n, pipeline_mode=pl.Buffered(buffers))


ANY = pl.BlockSpec(memory_space=pl.ANY)


def _bdot(a, b):
    return jnp.dot(a.astype(BF16), b.astype(BF16), preferred_element_type=F32)


def _bdot_nt(a, b):
    return lax.dot_general(a.astype(BF16), b.astype(BF16), (((1,), (1,)), ((), ())), preferred_element_type=F32)


def _bdot_tn(a, b):
    return lax.dot_general(a.astype(BF16), b.astype(BF16), (((0,), (0,)), ((), ())), preferred_element_type=F32)


def _rms(x, g):
    r = lax.rsqrt(jnp.mean(x * x, axis=-1, keepdims=True) + EPS)
    return x * r * g, r


def _rms_bwd(dy, x, r, g):
    gdy = dy * g
    dx = r * gdy - x * (r * r * r) * jnp.mean(x * gdy, axis=-1, keepdims=True)
    dg = jnp.sum(dy * x * r, axis=0, keepdims=True)
    return dx, dg


def _gelu_parts(x):
    x2 = x * x
    t = jnp.tanh(GELU_C * (x + GELU_A * x * x2))
    val = 0.5 * x * (1.0 + t)
    grad = 0.5 * (1.0 + t) + 0.5 * x * (1.0 - t * t) * (GELU_C * (1.0 + 3.0 * GELU_A * x2))
    return val, grad


def _gelu(x):
    return 0.5 * x * (1.0 + jnp.tanh(GELU_C * (x + GELU_A * x * x * x)))


def _sigmoid(z):
    return 1.0 / (1.0 + jnp.exp(-z))


def _cast_shards(b_idx, arrs):
    n = len(arrs)
    steps = 8

    def body(b_ref, *refs):
        for i in range(n):
            refs[n + i][...] = refs[i][...].astype(BF16)

    in_specs = [pl.BlockSpec((a.shape[0] // steps, a.shape[1]), lambda i, b: (i, 0)) for a in arrs]
    out_specs = [pl.BlockSpec((None, a.shape[0] // steps, a.shape[1]), lambda i, b: (b[0], i, 0)) for a in arrs]
    return pl.pallas_call(
        body, out_shape=[jax.ShapeDtypeStruct((N_CHIP,) + a.shape, BF16) for a in arrs],
        grid_spec=pltpu.PrefetchScalarGridSpec(num_scalar_prefetch=1, grid=(steps,), in_specs=in_specs, out_specs=out_specs),
        compiler_params=_cp(("arbitrary",)), name="cast_shards")(b_idx, *arrs)


def _proj_gather(order, x, g, win_own, cw8s, more, tm=1024):
    t = x.shape[0]
    ni = t // tm
    nm = len(more)

    def body(*refs):
        order_ref, x_ref, g_ref, win_in, cw_in = refs[:5]
        o_ref, hb_ref, win_f, cw_out = refs[5 + nm:9 + nm]
        more_out = refs[9 + nm:9 + 2 * nm]
        hbuf, wv, cw_s, cw_r, loc = refs[9 + 2 * nm:14 + 2 * nm]
        g_in = _Gather([win_f], *refs[14 + 2 * nm:18 + 2 * nm])
        g_more = _Gather(more_out, *refs[18 + 2 * nm:22 + 2 * nm])
        j, i = pl.program_id(0), pl.program_id(1)
        x, y, c, chips = _coords()
        b = 2 * x + y
        blks = [2 * chip[0] + chip[1] for chip in chips]

        def cw_cols(blk):
            return cw_out.at[:, pl.ds(blk * (D // N_CHIP), D // N_CHIP)]

        def cw_copy(k, blk):
            src = cw_in if blk is None else cw_cols(blk)
            return pltpu.make_async_remote_copy(src_ref=src, dst_ref=cw_cols(b if blk is None else blk), send_sem=cw_s.at[k],
                                                recv_sem=cw_r.at[k], device_id=(*chips[k], c), device_id_type=MESH)

        cw_local = pltpu.make_async_copy(cw_in, cw_cols(b), loc.at[1])

        def load(blk, slot):
            return pltpu.make_async_copy(win_f.at[blk], wv.at[slot], loc.at[2 + slot])

        @pl.when((j == 0) & (i == 0))
        def _():
            g_in.start()
            cw_local.start()
            for k in range(3):
                cw_copy(k, None).start()
            load(b, 0).start()
            load(b, 0).wait()

        @pl.when((j == 1) & (i == 0))
        def _():
            g_in.hop()
            g_more.start()
            g_in.near_ready()
            load(g_in.bx, 1).start()
            load(g_in.by, 0).start()
            load(g_in.bx, 1).wait()

        @pl.when((j == 2) & (i == 0))
        def _():
            load(g_in.by, 0).wait()
            g_in.far()
            g_in.far_ready()
            load(g_in.bd, 1).start()

        @pl.when((j == 3) & (i == 0))
        def _():
            load(g_in.bd, 1).wait()
            g_more.hop()

        rows = pl.ds(pl.multiple_of(i * tm, tm), tm)

        @pl.when(j == 0)
        def _():
            h, _ = _rms(x_ref[...], g_ref[...])
            hbuf[rows, :] = h.astype(BF16)
            hb_ref[...] = h.astype(BF16)

        @pl.when((j == N_CHIP - 1) & (i == ni - 1))
        def _():
            g_more.far()

        o_ref[...] = jnp.dot(hbuf[rows, :], wv[lax.rem(j, 2)], preferred_element_type=F32).astype(BF16)

        @pl.when((j == N_CHIP - 1) & (i == ni - 1))
        def _():
            for k in range(3):
                cw_copy(k, blks[k]).wait_recv()
            for k in range(3):
                cw_copy(k, None).wait_send()
            cw_local.wait()
            g_more.near_ready()
            g_more.far_ready()
            g_in.drain()
            g_more.drain()

    first = lambda j, i: jnp.where(j == 0, i, ni - 1)
    in_specs = [pl.BlockSpec((tm, D), lambda j, i, o: (first(j, i), 0)), pl.BlockSpec((1, D), lambda j, i, o: (0, 0)),
                ANY, ANY] + [ANY] * nm
    out_specs = [pl.BlockSpec((tm, IN_BLK), lambda j, i, o: (i, o[j])),
                 pl.BlockSpec((tm, D), lambda j, i, o: (first(j, i), 0)), ANY, ANY] + [ANY] * nm
    outs = pl.pallas_call(
        body, out_shape=[jax.ShapeDtypeStruct((t, IN_DIM), BF16), jax.ShapeDtypeStruct((t, D), BF16),
                         jax.ShapeDtypeStruct(win_own.shape, BF16), jax.ShapeDtypeStruct((8, D), F32)]
        + [jax.ShapeDtypeStruct(f.shape, f.dtype) for f in more],
        grid_spec=pltpu.PrefetchScalarGridSpec(
            num_scalar_prefetch=1, grid=(N_CHIP, ni), in_specs=in_specs, out_specs=out_specs,
            scratch_shapes=[pltpu.VMEM((t, D), BF16), pltpu.VMEM((2, D, IN_BLK), BF16)]
            + [pltpu.SemaphoreType.DMA((3,))] * 2 + [pltpu.SemaphoreType.DMA((4,))] + _gather_sems(1) + _gather_sems(nm)),
        input_output_aliases={3: 2, **{5 + w: 4 + w for w in range(nm)}},
        compiler_params=_cp(("arbitrary", "arbitrary"), VMEM_LIMIT), name="proj_gather")(order, x, g, win_own, cw8s, *more)
    return outs[0], outs[1], outs[2], outs[3], outs[4:]


class _Gather:
    def __init__(self, outs, ici_s, ici_r, d2d_s, d2d_r):
        x, y, c, _ = _coords()
        self.outs, self.c = outs, c
        self.sems = ici_s, ici_r, d2d_s, d2d_r
        self.b, self.bx, self.by, self.bd = 2 * x + y, 2 * (1 - x) + y, 2 * x + (1 - y), 2 * (1 - x) + (1 - y)
        self.xn, self.yn, self.sib = (1 - x, y, c), (x, 1 - y, c), (x, y, 1 - c)

    def piece(self, w, blk, hc, quarter=None):
        hr = self.outs[w].shape[1] // 2
        if quarter is None:
            return self.outs[w].at[blk, pl.ds(hc * hr, hr)]
        return self.outs[w].at[blk, pl.ds(hc * hr + quarter * (hr // 2), hr // 2)]

    def ici(self, w, k, ref, to):
        return pltpu.make_async_remote_copy(src_ref=ref, dst_ref=ref, send_sem=self.sems[0].at[w, k],
                                            recv_sem=self.sems[1].at[w, k], device_id=to, device_id_type=MESH)

    def d2d(self, w, k, ref):
        return pltpu.make_async_remote_copy(src_ref=ref, dst_ref=ref, send_sem=self.sems[2].at[w, k],
                                            recv_sem=self.sems[3].at[w, k], device_id=self.sib, device_id_type=MESH)

    def start(self):
        for w in range(len(self.outs)):
            mine = self.piece(w, self.b, self.c)
            self.ici(w, 0, mine, self.xn).start()
            self.ici(w, 1, mine, self.yn).start()

    def hop(self):
        c = self.c
        for w in range(len(self.outs)):
            self.ici(w, 0, self.piece(w, self.bx, c), self.xn).wait_recv()
            self.ici(w, 1, self.piece(w, self.by, c), self.yn).wait_recv()
            self.ici(w, 2, self.piece(w, self.bx, c, 0), self.yn).start()
            self.ici(w, 3, self.piece(w, self.by, c, 1), self.xn).start()
            self.d2d(w, 0, self.piece(w, self.bx, c)).start()
            self.d2d(w, 1, self.piece(w, self.by, c)).start()

    def near_ready(self):
        for w in range(len(self.outs)):
            self.d2d(w, 0, self.piece(w, self.bx, 1 - self.c)).wait_recv()
            self.d2d(w, 1, self.piece(w, self.by, 1 - self.c)).wait_recv()

    def far(self):
        c = self.c
        for w in range(len(self.outs)):
            self.ici(w, 2, self.piece(w, self.bd, c, 0), self.yn).wait_recv()
            self.ici(w, 3, self.piece(w, self.bd, c, 1), self.xn).wait_recv()
            self.d2d(w, 2, self.piece(w, self.bd, c, 0)).start()
            self.d2d(w, 3, self.piece(w, self.bd, c, 1)).start()

    def far_ready(self):
        for w in range(len(self.outs)):
            self.d2d(w, 2, self.piece(w, self.bd, 1 - self.c, 0)).wait_recv()
            self.d2d(w, 3, self.piece(w, self.bd, 1 - self.c, 1)).wait_recv()

    def drain(self):
        c = self.c
        for w in range(len(self.outs)):
            mine = self.piece(w, self.b, c)
            self.ici(w, 0, mine, self.xn).wait_send()
            self.ici(w, 1, mine, self.yn).wait_send()
            self.ici(w, 2, self.piece(w, self.bx, c, 0), self.yn).wait_send()
            self.ici(w, 3, self.piece(w, self.by, c, 1), self.xn).wait_send()
            self.d2d(w, 0, self.piece(w, self.bx, c)).wait_send()
            self.d2d(w, 1, self.piece(w, self.by, c)).wait_send()
            self.d2d(w, 2, self.piece(w, self.bd, c, 0)).wait_send()
            self.d2d(w, 3, self.piece(w, self.bd, c, 1)).wait_send()


def _gather_sems(nw):
    return [pltpu.SemaphoreType.DMA((max(nw, 1), 4))] * 4


def _mixer_fwd(proj, cw8, lng, lnb, wc, bsb, fulls, tm=256):
    t = proj.shape[0]
    nt = t // tm
    nch = tm // CH
    nw = len(fulls)

    def body(*refs):
        p_ref, cw_ref, lng_ref, lnb_ref, wc_ref, bsb_ref = refs[:6]
        mix_ref = refs[6 + nw]
        w_outs = refs[7 + nw:7 + 2 * nw]
        prev_ref = refs[7 + 2 * nw]
        gather = _Gather(w_outs, *refs[8 + 2 * nw:])

        @pl.when(pl.program_id(0) == 0)
        def _():
            gather.start()
            prev_ref[...] = jnp.zeros_like(prev_ref)

        @pl.when(pl.program_id(0) == nt // 2)
        def _():
            gather.hop()

        @pl.when(pl.program_id(0) == nt - 1)
        def _():
            gather.far()

        rows = lax.broadcasted_iota(jnp.int32, (tm, CH), 0)
        for s in range(HEADS):
            cs = pl.ds(CH * s, CH)

            def slab(k):
                return p_ref[:, pl.ds(k * SLAB + CH * s, CH)].astype(F32)

            gb, gc, xa, za = slab(0), slab(1), slab(2), slab(3)
            cx = gc * xa
            p6 = jnp.broadcast_to(prev_ref[6:7, cs], (tm, CH))
            p7 = jnp.broadcast_to(prev_ref[7:8, cs], (tm, CH))
            c1 = jnp.where(rows == 0, p7, pltpu.roll(cx, 1, 0))
            c2 = jnp.where(rows == 0, p6, jnp.where(rows == 1, p7, pltpu.roll(cx, 2, 0)))
            prev_ref[:, cs] = cx[tm - 8:, :]
            cv = cw_ref[0:1, cs] * c2 + cw_ref[1:2, cs] * c1 + cw_ref[2:3, cs] * cx
            mix_ref[:, cs] = (gb * cv * (za * _sigmoid(za))).astype(BF16)

            u, v, zb = slab(4), slab(5), slab(6)
            ug, vg = _gelu(u), _gelu(v)
            dlt = vg - jnp.mean(vg, axis=-1, keepdims=True)
            vhat = dlt * lax.rsqrt(jnp.mean(dlt * dlt, axis=-1, keepdims=True) + EPS)
            vn = (vhat * lng_ref[:, cs] + lnb_ref[:, cs]).astype(BF16)
            gate = ug * (zb * _sigmoid(zb))
            for c in range(nch):
                rs = slice(CH * c, CH * (c + 1))
                sp = jnp.dot(wc_ref[s], vn[rs], preferred_element_type=F32) + bsb_ref[s]
                mix_ref[rs, pl.ds(SLAB + CH * s, CH)] = (gate[rs] * sp).astype(BF16)

        @pl.when(pl.program_id(0) == nt - 1)
        def _():
            gather.near_ready()
            gather.far_ready()
            gather.drain()

    sems = _gather_sems(nw)
    outs = pl.pallas_call(
        body, grid=(nt,),
        in_specs=[pl.BlockSpec((tm, IN_DIM), lambda i: (i, 0)), _full((8, D)), _full((1, D)), _full((1, D)),
                  _full((HEADS, CH, CH)), _full((HEADS, CH, CH))] + [ANY] * nw,
        out_specs=[pl.BlockSpec((tm, MIX), lambda i: (i, 0))] + [ANY] * nw,
        out_shape=[jax.ShapeDtypeStruct((t, MIX), BF16)] + [jax.ShapeDtypeStruct(f.shape, f.dtype) for f in fulls],
        input_output_aliases={6 + w: 1 + w for w in range(nw)},
        scratch_shapes=[pltpu.VMEM((8, D), F32)] + sems,
        compiler_params=_cp(("arbitrary",), VMEM_LIMIT), name="mixer_fwd")(proj, cw8, lng, lnb, wc, bsb, *fulls)
    return outs[0], outs[1:]


def _mem_fwd(mem, gm, wkv_f):
    n_mem = mem.shape[0]

    def body(mem_ref, gm_ref, w_ref, k_ref, v_ref):
        m, _ = _rms(mem_ref[...], gm_ref[...])
        mb = m.astype(BF16)
        for j in range(N_CHIP):
            dst = k_ref if j < 2 else v_ref
            dst[:, pl.ds(KV_BLK * (j % 2), KV_BLK)] = jnp.dot(mb, w_ref[j], preferred_element_type=F32).astype(BF16)

    return pl.pallas_call(
        body, out_shape=[jax.ShapeDtypeStruct((n_mem, D), BF16), jax.ShapeDtypeStruct((n_mem, D), BF16)],
        compiler_params=_cp(None, VMEM_LIMIT), name="mem_fwd")(mem, gm, wkv_f)


def _tail(x, tgt, mixin, wout, wq, wxo, k, v, g2, g3, tm=512, sub=512):
    t = x.shape[0]
    n_mem = k.shape[0]
    scale = 1.0 / math.sqrt(XD)

    def body(x_ref, tgt_ref, mix_ref, wout_ref, wq_ref, wxo_ref, k_ref, v_ref, g2_ref, g3_ref,
             loss_ref, dmix_ref, dx1b_ref, h2_ref, dq_ref, o_ref, dx2b_ref, dk_ref, dv_ref, dg2_ref, dg3_ref):
        @pl.when(pl.program_id(0) == 0)
        def _():
            loss_ref[...] = jnp.zeros_like(loss_ref)
            dk_ref[...] = jnp.zeros_like(dk_ref)
            dv_ref[...] = jnp.zeros_like(dv_ref)
            dg2_ref[...] = jnp.zeros_like(dg2_ref)
            dg3_ref[...] = jnp.zeros_like(dg3_ref)

        g2, g3 = g2_ref[...], g3_ref[...]
        for sb in range(tm // sub):
            rs = pl.ds(sub * sb, sub)
            x1 = x_ref[rs, :] + jnp.dot(mix_ref[rs, :], wout_ref[...], preferred_element_type=F32)
            h2, r2 = _rms(x1, g2)
            h2b = h2.astype(BF16)
            h2_ref[rs, :] = h2b
            q = jnp.dot(h2b, wq_ref[...], preferred_element_type=F32).astype(BF16)
            probs, outs = [], []
            for hd in range(XH):
                hs = pl.ds(XD * hd, XD)
                s = _bdot_nt(q[:, XD * hd:XD * (hd + 1)], k_ref[:, hs]) * scale
                e = jnp.exp(s - jnp.max(s, axis=-1, keepdims=True))
                p = e / jnp.sum(e, axis=-1, keepdims=True)
                probs.append(p)
                outs.append(_bdot(p, v_ref[:, hs]))
            ob = jnp.concatenate(outs, axis=-1).astype(BF16)
            o_ref[rs, :] = ob
            x2 = x1 + jnp.dot(ob, wxo_ref[...], preferred_element_type=F32)
            y, r3 = _rms(x2, g3)
            diff = y - tgt_ref[rs, :]
            row_loss = jnp.sum(diff * diff, axis=-1, keepdims=True)
            loss_ref[...] += jnp.broadcast_to(jnp.sum(row_loss, axis=0, keepdims=True) * (0.5 / D), loss_ref.shape)

            dx2, dg3 = _rms_bwd(diff * (1.0 / D), x2, r3, g3)
            dg3_ref[...] += dg3
            dx2b = dx2.astype(BF16)
            dx2b_ref[rs, :] = dx2b
            do = _bdot_nt(dx2b, wxo_ref[...])
            dqs = []
            for hd in range(XH):
                hs = pl.ds(XD * hd, XD)
                p = probs[hd]
                do_h = do[:, XD * hd:XD * (hd + 1)]
                dv_ref[:, hs] += _bdot_tn(p, do_h)
                dp = _bdot_nt(do_h, v_ref[:, hs])
                ds = p * (dp - jnp.sum(dp * p, axis=-1, keepdims=True))
                dqs.append(_bdot(ds, k_ref[:, hs]) * scale)
                dk_ref[:, hs] += _bdot_tn(ds, q[:, XD * hd:XD * (hd + 1)]) * scale
            dq = jnp.concatenate(dqs, axis=-1).astype(BF16)
            dq_ref[rs, :] = dq
            dx1n, dg2 = _rms_bwd(_bdot_nt(dq, wq_ref[...]), x1, r2, g2)
            dg2_ref[...] += dg2
            dx1b = (dx2 + dx1n).astype(BF16)
            dx1b_ref[rs, :] = dx1b
            dmix_ref[rs, :] = _bdot_nt(dx1b, wout_ref[...]).astype(BF16)

    tok = lambda w: pl.BlockSpec((tm, w), lambda i: (i, 0))
    return pl.pallas_call(
        body, grid=(t // tm,),
        in_specs=[tok(D), tok(D), tok(MIX), _full((MIX, D), 1), _full((D, D), 1), _full((D, D), 1),
                  _full((n_mem, D), 1), _full((n_mem, D), 1), _full((1, D)), _full((1, D))],
        out_specs=[_full((8, 128)), tok(MIX), tok(D), tok(D), tok(D), tok(D), tok(D),
                   _full((n_mem, D)), _full((n_mem, D)), _full((1, D)), _full((1, D))],
        out_shape=[jax.ShapeDtypeStruct((8, 128), F32), jax.ShapeDtypeStruct((t, MIX), BF16),
                   jax.ShapeDtypeStruct((t, D), BF16),
                   jax.ShapeDtypeStruct((t, D), BF16), jax.ShapeDtypeStruct((t, D), BF16),
                   jax.ShapeDtypeStruct((t, D), BF16), jax.ShapeDtypeStruct((t, D), BF16),
                   jax.ShapeDtypeStruct((n_mem, D), F32), jax.ShapeDtypeStruct((n_mem, D), F32),
                   jax.ShapeDtypeStruct((1, D), F32), jax.ShapeDtypeStruct((1, D), F32)],
        compiler_params=_cp(("arbitrary",), VMEM_LIMIT), name="tail")(x, tgt, mixin, wout, wq, wxo, k, v, g2, g3)


def _mem_bwd(mem, gm, dk, dv, wkv_f):
    def body(mem_ref, gm_ref, dk_ref, dv_ref, w_ref, dw_ref, dwb_ref, dgm_ref):
        mem_v = mem_ref[...]
        m, rm = _rms(mem_v, gm_ref[...])
        mb = m.astype(BF16)
        dm = jnp.zeros_like(mem_v)
        for j in range(N_CHIP):
            src = dk_ref if j < 2 else dv_ref
            dkv = src[:, pl.ds(KV_BLK * (j % 2), KV_BLK)].astype(BF16)
            dw = _bdot_tn(mb, dkv)
            dw_ref[j] = dw
            dwb_ref[j] = dw.astype(BF16)
            dm = dm + _bdot_nt(dkv, w_ref[j])
        dgm_ref[...] = jnp.sum(dm * mem_v * rm, axis=0, keepdims=True)

    return pl.pallas_call(
        body, out_shape=[jax.ShapeDtypeStruct((N_CHIP, D, KV_BLK), F32), jax.ShapeDtypeStruct((N_CHIP, D, KV_BLK), BF16),
                         jax.ShapeDtypeStruct((1, D), F32)],
        compiler_params=_cp(None, VMEM_LIMIT), name="mem_bwd")(mem, gm, dk, dv, wkv_f)


def _mixer_bwd(proj, dmix, cw8, lng, lnb, wc, wct, bsb, win_f, x, dx1, g1, tm=256):
    t = proj.shape[0]
    nt = t // tm
    nch = tm // CH
    hb = 16
    pair = 2 * CH

    def body(p_ref, pgc_ref, pxa_ref, dm_ref, cw_ref, lng_ref, lnb_ref, wc_ref, wct_ref, bsb_ref, w_ref, x_ref,
             dx1_ref, g1_ref, dp_ref, dcw_ref, dlng_ref, dlnb_ref, dwc_ref, dbs_ref, gx_ref, dg1_ref,
             next_ref, dh_ref):
        i = pl.program_id(0)

        @pl.when(i == 0)
        def _():
            next_ref[...] = jnp.zeros_like(next_ref)
            dcw_ref[...] = jnp.zeros_like(dcw_ref)
            dlng_ref[...] = jnp.zeros_like(dlng_ref)
            dlnb_ref[...] = jnp.zeros_like(dlnb_ref)
            dwc_ref[...] = jnp.zeros_like(dwc_ref)
            dbs_ref[...] = jnp.zeros_like(dbs_ref)
            dg1_ref[...] = jnp.zeros_like(dg1_ref)

        first_tile = i == nt - 1
        rows = lax.broadcasted_iota(jnp.int32, (tm, CH), 0)
        ones8 = jnp.ones((8, CH), BF16)
        for s in range(HEADS):
            cs = pl.ds(CH * s, CH)

            def slab(k):
                return p_ref[:, pl.ds(k * SLAB + CH * s, CH)].astype(F32)

            gb, gc, xa, za = slab(0), slab(1), slab(2), slab(3)
            da = dm_ref[:, cs].astype(F32)
            cx = gc * xa
            cxp = pgc_ref[:, cs].astype(F32) * pxa_ref[:, cs].astype(F32)
            cxp = jnp.where(first_tile, jnp.zeros_like(cxp), cxp)
            p6 = jnp.broadcast_to(cxp[hb - 2:hb - 1, :], (tm, CH))
            p7 = jnp.broadcast_to(cxp[hb - 1:hb, :], (tm, CH))
            c1 = jnp.where(rows == 0, p7, pltpu.roll(cx, 1, 0))
            c2 = jnp.where(rows == 0, p6, jnp.where(rows == 1, p7, pltpu.roll(cx, 2, 0)))
            w0, w1, w2 = cw_ref[0:1, cs], cw_ref[1:2, cs], cw_ref[2:3, cs]
            cv = w0 * c2 + w1 * c1 + w2 * cx
            sg = _sigmoid(za)
            sa = za * sg
            dcv = da * gb * sa
            dp_ref[:, pl.ds(0 * SLAB + CH * s, CH)] = (da * cv * sa).astype(BF16)
            dp_ref[:, pl.ds(3 * SLAB + CH * s, CH)] = (da * gb * cv * (sg * (1.0 + za * (1.0 - sg)))).astype(BF16)
            n0 = jnp.broadcast_to(next_ref[0:1, cs], (tm, CH))
            n1 = jnp.broadcast_to(next_ref[1:2, cs], (tm, CH))
            u1 = jnp.where(rows == tm - 1, n0, pltpu.roll(dcv, tm - 1, 0))
            u2 = jnp.where(rows == tm - 2, n0, jnp.where(rows == tm - 1, n1, pltpu.roll(dcv, tm - 2, 0)))
            next_ref[:, cs] = dcv[0:8, :]
            dcx = w2 * dcv + w1 * u1 + w0 * u2
            dp_ref[:, pl.ds(1 * SLAB + CH * s, CH)] = (dcx * xa).astype(BF16)
            dp_ref[:, pl.ds(2 * SLAB + CH * s, CH)] = (dcx * gc).astype(BF16)
            dcw_ref[0:1, cs] += jnp.sum(dcv * c2, axis=0, keepdims=True)
            dcw_ref[1:2, cs] += jnp.sum(dcv * c1, axis=0, keepdims=True)
            dcw_ref[2:3, cs] += jnp.sum(dcv * cx, axis=0, keepdims=True)

            u, v, zb = slab(4), slab(5), slab(6)
            db = dm_ref[:, pl.ds(SLAB + CH * s, CH)].astype(F32)
            ug, ugrad = _gelu_parts(u)
            vg, vgrad = _gelu_parts(v)
            dlt = vg - jnp.mean(vg, axis=-1, keepdims=True)
            rstd = lax.rsqrt(jnp.mean(dlt * dlt, axis=-1, keepdims=True) + EPS)
            vhat = dlt * rstd
            lg = lng_ref[:, cs]
            vn = (vhat * lg + lnb_ref[:, cs]).astype(BF16)
            sgb = _sigmoid(zb)
            szb = zb * sgb
            sps, dvns = [], []
            dbs = jnp.zeros((8, CH), F32)
            dwc = jnp.zeros((CH, CH), F32)
            for c in range(nch):
                rs = slice(CH * c, CH * (c + 1))
                sp = jnp.dot(wc_ref[s], vn[rs], preferred_element_type=F32) + bsb_ref[s]
                dsp = (db[rs] * ug[rs] * szb[rs]).astype(BF16)
                dbs = dbs + lax.dot_general(ones8, dsp, (((1,), (1,)), ((), ())), preferred_element_type=F32)
                dwc = dwc + lax.dot_general(dsp, vn[rs], (((1,), (1,)), ((), ())), preferred_element_type=F32)
                dvns.append(jnp.dot(wct_ref[s], dsp, preferred_element_type=F32))
                sps.append(sp)
            sp = jnp.concatenate(sps, axis=0)
            dvn = jnp.concatenate(dvns, axis=0)
            dbs_ref[:, cs] += dbs
            dwc_ref[s] += dwc
            dlng_ref[:, cs] += jnp.sum(dvn * vhat, axis=0, keepdims=True)
            dlnb_ref[:, cs] += jnp.sum(dvn, axis=0, keepdims=True)
            dvhat = dvn * lg
            dvg = rstd * (dvhat - jnp.mean(dvhat, axis=-1, keepdims=True)
                          - vhat * jnp.mean(dvhat * vhat, axis=-1, keepdims=True))
            dp_ref[:, pl.ds(4 * SLAB + CH * s, CH)] = (db * sp * szb * ugrad).astype(BF16)
            dp_ref[:, pl.ds(5 * SLAB + CH * s, CH)] = (dvg * vgrad).astype(BF16)
            dp_ref[:, pl.ds(6 * SLAB + CH * s, CH)] = (db * ug * sp * (sgb * (1.0 + zb * (1.0 - sgb)))).astype(BF16)

            if s % 2 == 1:
                part = None
                for k in range(N_SLAB):
                    col = k * SLAB + pair * (s // 2)
                    blk, off = divmod(col, IN_BLK)
                    term = lax.dot_general(dp_ref[:, pl.ds(col, pair)], w_ref[blk, :, pl.ds(off, pair)],
                                           (((1,), (1,)), ((), ())), preferred_element_type=F32)
                    part = term if part is None else part + term
                if s == 1:
                    dh_ref[...] = part
                else:
                    dh_ref[...] += part

        xv = x_ref[...]
        r = lax.rsqrt(jnp.mean(xv * xv, axis=-1, keepdims=True) + EPS)
        dxn, dg = _rms_bwd(dh_ref[...], xv, r, g1_ref[...])
        gx_ref[...] = dx1_ref[...].astype(F32) + dxn
        dg1_ref[0:1, :] += dg

        @pl.when(i == nt - 1)
        def _():
            tril = lax.broadcasted_iota(jnp.int32, (CH, CH), 0) >= lax.broadcasted_iota(jnp.int32, (CH, CH), 1)
            for s in range(HEADS):
                dwc_ref[s] = jnp.where(tril, dwc_ref[s], 0.0)

    rev = lambda i: nt - 1 - i
    halo = lambda col: pl.BlockSpec((hb, SLAB), lambda i: (jnp.maximum(rev(i) * (tm // hb) - 1, 0), col))
    tok = lambda w: pl.BlockSpec((tm, w), lambda i: (rev(i), 0))
    return pl.pallas_call(
        body, grid=(nt,),
        in_specs=[tok(IN_DIM), halo(1), halo(2), tok(MIX), _full((8, D)), _full((1, D)), _full((1, D)),
                  _full((HEADS, CH, CH)), _full((HEADS, CH, CH)), _full((HEADS, CH, CH)),
                  _full((N_CHIP, D, IN_BLK), 1), tok(D), tok(D), _full((1, D))],
        out_specs=[tok(IN_DIM), _full((8, D)), _full((1, D)), _full((1, D)), _full((HEADS, CH, CH)), _full((8, D)),
                   tok(D), _full((8, D))],
        out_shape=[jax.ShapeDtypeStruct((t, IN_DIM), BF16), jax.ShapeDtypeStruct((8, D), F32),
                   jax.ShapeDtypeStruct((1, D), F32), jax.ShapeDtypeStruct((1, D), F32),
                   jax.ShapeDtypeStruct((HEADS, CH, CH), F32), jax.ShapeDtypeStruct((8, D), F32),
                   jax.ShapeDtypeStruct((t, D), F32), jax.ShapeDtypeStruct((8, D), F32)],
        scratch_shapes=[pltpu.VMEM((8, D), F32), pltpu.VMEM((tm, D), F32)],
        compiler_params=_cp(("arbitrary",), VMEM_LIMIT), name="mixer_bwd")(
            proj, proj, proj, dmix, cw8, lng, lnb, wc, wct, bsb, win_f, x, dx1, g1)


def _grad_matmul(a, b, after, *, by_cols, name, tk=1024):
    t, m = a.shape
    n = b.shape[1]
    nk = t // tk
    nj = N_CHIP if by_cols else 1
    bn = n // nj

    def body(a_ref, b_ref, after_ref, o_ref, ob_ref):
        kk = pl.program_id(1)
        part = lax.dot_general(a_ref[...], b_ref[...], (((0,), (0,)), ((), ())), preferred_element_type=F32)

        @pl.when(kk == 0)
        def _():
            o_ref[...] = part

        @pl.when(kk > 0)
        def _():
            o_ref[...] += part

        @pl.when(kk == nk - 1)
        def _():
            ob_ref[...] = o_ref[...].astype(BF16)

    a_spec = pl.BlockSpec((tk, m), lambda j, k: (k, 0))
    b_spec = pl.BlockSpec((tk, bn), lambda j, k: (k, j))
    o_spec = pl.BlockSpec((None, m, bn), lambda j, k: (j, 0, 0))
    o32, o16 = pl.pallas_call(
        body, grid=(nj, nk), in_specs=[a_spec, b_spec, ANY], out_specs=[o_spec, o_spec],
        out_shape=[jax.ShapeDtypeStruct((nj, m, bn), F32), jax.ShapeDtypeStruct((nj, m, bn), BF16)],
        compiler_params=_cp(("parallel", "arbitrary"), VMEM_LIMIT), name=name)(a, b, after)
    if by_cols:
        return o32, o16
    return o32.reshape(N_CHIP, m // N_CHIP, n), o16.reshape(N_CHIP, m // N_CHIP, n)


def _coords():
    x, y, c = lax.axis_index("x"), lax.axis_index("y"), lax.axis_index("c")
    chips = [(1 - x, y), (x, 1 - y), (1 - x, 1 - y)]
    return x, y, c, chips


def _pair_exchange(grads_b, smalls, name):
    ng, ns = len(grads_b), len(smalls)
    n = ng + ns

    def body(*refs):
        ins, outs, send, recv = refs[:n], refs[n:2 * n], refs[2 * n], refs[2 * n + 1]
        x, y, c, _ = _coords()
        cps = []
        for i in range(n):
            if i < ng:
                hr = ins[i].shape[1] // 2
                src = ins[i].at[pl.ds(0, N_CHIP), pl.ds((1 - c) * hr, hr)]
            else:
                hr = ins[i].shape[0] // 2
                src = ins[i].at[pl.ds((1 - c) * hr, hr)]
            cps.append(pltpu.make_async_remote_copy(
                src_ref=src, dst_ref=outs[i], send_sem=send.at[i], recv_sem=recv.at[i],
                device_id=(x, y, 1 - c), device_id_type=MESH))
        for cp in cps:
            cp.start()
        for cp in cps:
            cp.wait()

    out_shape = [jax.ShapeDtypeStruct((N_CHIP, g.shape[1] // 2, g.shape[2]), g.dtype) for g in grads_b]
    out_shape += [jax.ShapeDtypeStruct((s.shape[0] // 2, s.shape[1]), s.dtype) for s in smalls]
    return pl.pallas_call(
        body, out_shape=out_shape, in_specs=[ANY] * n, out_specs=[ANY] * n,
        scratch_shapes=[pltpu.SemaphoreType.DMA((n,)), pltpu.SemaphoreType.DMA((n,))],
        name=name)(*grads_b, *smalls)


def _pair_sum(c_idx, grads, recvd, smalls, smalls_recvd, name):
    ng, ns = len(grads), len(smalls)
    halves = [g.shape[1] // 2 for g in grads]

    def body(c_ref, *refs):
        g_in, r_in = refs[:ng], refs[ng:2 * ng]
        s_in, sr_in = refs[2 * ng:2 * ng + ns], refs[2 * ng + ns:2 * ng + 2 * ns]
        o = refs[2 * ng + 2 * ns:]
        for i in range(ng):
            tot = g_in[i][...] + r_in[i][...].astype(F32)
            o[i][...] = tot
            o[ng + i][...] = tot.astype(BF16)
        for i in range(ns):
            o[2 * ng + i][...] = s_in[i][...] + sr_in[i][...]

    steps = 1
    in_specs = [pl.BlockSpec((None, None, halves[i] // steps, g.shape[2]), lambda b, r, c: (b, c[0], r, 0))
                for i, g in enumerate(grads)]
    in_specs += [pl.BlockSpec((None, halves[i] // steps, g.shape[2]), lambda b, r, c: (b, r, 0)) for i, g in enumerate(grads)]
    in_specs += [pl.BlockSpec((None, s.shape[0] // 2, s.shape[1]), lambda b, r, c: (c[0], 0, 0)) for s in smalls]
    in_specs += [pl.BlockSpec((s.shape[0] // 2, s.shape[1]), lambda b, r, c: (0, 0)) for s in smalls]
    blk = [pl.BlockSpec((None, halves[i] // steps, g.shape[2]), lambda b, r, c: (b, r, 0)) for i, g in enumerate(grads)]
    out_specs = blk + blk + [pl.BlockSpec((s.shape[0] // 2, s.shape[1]), lambda b, r, c: (0, 0)) for s in smalls]
    out_shape = [jax.ShapeDtypeStruct((N_CHIP, halves[i], g.shape[2]), F32) for i, g in enumerate(grads)]
    out_shape += [jax.ShapeDtypeStruct((N_CHIP, halves[i], g.shape[2]), BF16) for i, g in enumerate(grads)]
    out_shape += [jax.ShapeDtypeStruct((s.shape[0] // 2, s.shape[1]), F32) for s in smalls]
    grads4 = [g.reshape(N_CHIP, 2, halves[i], g.shape[2]) for i, g in enumerate(grads)]
    smalls3 = [s.reshape(2, s.shape[0] // 2, s.shape[1]) for s in smalls]
    return pl.pallas_call(
        body, out_shape=out_shape,
        grid_spec=pltpu.PrefetchScalarGridSpec(num_scalar_prefetch=1, grid=(N_CHIP, steps), in_specs=in_specs,
                                               out_specs=out_specs),
        compiler_params=_cp(("arbitrary", "arbitrary"), VMEM_LIMIT), name=name)(c_idx, *grads4, *recvd, *smalls3, *smalls_recvd)


_HBM = pl.BlockSpec(memory_space=pltpu.HBM)
_SEM = pl.BlockSpec(memory_space=pltpu.SEMAPHORE)


def _split_copies(ins, lands, ng, send, recv, arriving):
    x, y, c, chips = _coords()
    b = 2 * x + y
    copies = []
    for i in range(len(ins)):
        for k in range(3):
            blk = 2 * chips[k][0] + chips[k][1]
            src, dst, got = (ins[i].at[blk], lands[i].at[k], lands[i].at[k]) if i < ng else (ins[i], lands[i].at[b], lands[i].at[blk])
            sems = dict(send_sem=send.at[3 * i + k], recv_sem=recv.at[3 * i + k], device_id=(*chips[k], c), device_id_type=MESH)
            if arriving:
                copies.append(pltpu.make_async_remote_copy(src_ref=got, dst_ref=got, **sems))
            else:
                copies.append(pltpu.make_async_remote_copy(src_ref=src, dst_ref=dst, **sems))
    return copies


def _exchange_begin(sums_b, smalls, name):
    ng, n = len(sums_b), len(sums_b) + len(smalls)
    srcs = list(sums_b) + list(smalls)
    lands = [lax.empty((3,) + g.shape[1:], g.dtype) for g in sums_b] + [lax.empty((N_CHIP,) + s.shape, s.dtype) for s in smalls]

    def body(*refs):
        ins, land_refs = refs[:n], refs[n:2 * n]
        send, recv = refs[2 * n], refs[2 * n + 1]
        token = refs[4 * n + 2]
        for cp in _split_copies(ins, land_refs, ng, send, recv, False):
            cp.start()
        token[...] = jnp.zeros_like(token)

    hbm = lambda a: pltpu.HBM(a.shape, a.dtype)
    outs = pl.pallas_call(
        body, name=name,
        out_shape=(pltpu.SemaphoreType.DMA((3 * n,)), pltpu.SemaphoreType.DMA((3 * n,)), *[hbm(a) for a in srcs + lands],
                   jax.ShapeDtypeStruct((8, 128), F32)),
        in_specs=[_HBM] * (2 * n), out_specs=(_SEM, _SEM, *[_HBM] * (2 * n), pl.BlockSpec(memory_space=pltpu.VMEM)),
        input_output_aliases={i: 2 + i for i in range(2 * n)},
        compiler_params=pltpu.CompilerParams(has_side_effects=pltpu.SideEffectType.DATAFLOW_SIDE_EFFECTING),
    )(*[pltpu.with_memory_space_constraint(a, pltpu.HBM) for a in srcs + lands])
    return outs[0], outs[1], list(outs[2:2 + n]), list(outs[2 + n:2 + 2 * n]), outs[2 + 2 * n]


def _exchange_end(send, recv, srcs, lands, ng, after, name):
    n = len(srcs)
    after = list(after)

    def body(*refs):
        ins, land_refs = refs[:n], refs[n:2 * n]
        send_ref, recv_ref = refs[2 * n], refs[2 * n + 1]
        for cp in _split_copies(ins, land_refs, ng, send_ref, recv_ref, False):
            cp.wait_send()
        for cp in _split_copies(ins, land_refs, ng, send_ref, recv_ref, True):
            cp.wait_recv()

    hbm = lambda a: pltpu.HBM(a.shape, a.dtype)
    outs = pl.pallas_call(
        body, name=name, out_shape=tuple(hbm(a) for a in list(srcs) + list(lands)),
        in_specs=[_HBM] * (2 * n) + [_SEM, _SEM] + [ANY] * len(after), out_specs=tuple([_HBM] * (2 * n)),
        input_output_aliases={i: i for i in range(2 * n)},
        compiler_params=pltpu.CompilerParams(has_side_effects=pltpu.SideEffectType.DATAFLOW_SIDE_EFFECTING),
    )(*srcs, *lands, send, recv, *after)
    return list(outs[n:])


def _chip_sum(bc_idx, sums, recvd, smalls_slots, smalls_own, name, steps=4):
    ng, ns = len(sums), len(smalls_slots)

    def body(bc_ref, *refs):
        own, rx = refs[:ng], refs[ng:2 * ng]
        sl = refs[2 * ng:2 * ng + ns]
        sl_own = refs[2 * ng + ns:2 * ng + 2 * ns]
        o = refs[2 * ng + 2 * ns:]
        for i in range(ng):
            tot = own[i][...]
            for j in range(3):
                tot = tot + rx[i][j].astype(F32)
            o[i][...] = tot
        for i in range(ns):
            term = [jnp.where(bc_ref[0] == kk, sl_own[i][...], sl[i][kk]) for kk in range(N_CHIP)]
            o[ng + i][...] = ((term[0] + term[1]) + term[2]) + term[3]

    def rows(g):
        return g.shape[1] // steps

    in_specs = [pl.BlockSpec((None, rows(g), g.shape[2]), lambda r, bc: (bc[0], r, 0)) for g in sums]
    in_specs += [pl.BlockSpec((3, rows(g), g.shape[2]), lambda r, bc: (0, r, 0)) for g in sums]
    in_specs += [pl.BlockSpec(s.shape, lambda r, bc: (0, 0, 0)) for s in smalls_slots]
    in_specs += [pl.BlockSpec(s.shape[1:], lambda r, bc: (0, 0)) for s in smalls_slots]
    out_specs = [pl.BlockSpec((rows(g), g.shape[2]), lambda r, bc: (bc[1] * steps + r, 0)) for g in sums]
    out_specs += [pl.BlockSpec(s.shape[1:], lambda r, bc: (bc[1], 0)) for s in smalls_slots]
    out_shape = [jax.ShapeDtypeStruct((2 * g.shape[1], g.shape[2]), F32) for g in sums]
    out_shape += [jax.ShapeDtypeStruct((2 * s.shape[1], s.shape[2]), F32) for s in smalls_slots]
    return pl.pallas_call(
        body, out_shape=out_shape,
        grid_spec=pltpu.PrefetchScalarGridSpec(num_scalar_prefetch=1, grid=(steps,), in_specs=in_specs, out_specs=out_specs),
        compiler_params=_cp(("arbitrary",), VMEM_LIMIT), name=name)(bc_idx, *sums, *recvd, *smalls_slots, *smalls_own)


def _pair_gather(arrs, name):
    n = len(arrs)

    def body(*refs):
        outs = refs[n:2 * n]
        send, recv = refs[2 * n:]
        x, y, c, _ = _coords()
        cps = []
        for i in range(n):
            hr = outs[i].shape[0] // 2
            mine = outs[i].at[pl.ds(c * hr, hr)]
            cps.append(pltpu.make_async_remote_copy(
                src_ref=mine, dst_ref=mine, send_sem=send.at[i], recv_sem=recv.at[i],
                device_id=(x, y, 1 - c), device_id_type=MESH))
        for cp in cps:
            cp.start()
        for i in range(n):
            hr = outs[i].shape[0] // 2
            theirs = outs[i].at[pl.ds((1 - c) * hr, hr)]
            pltpu.make_async_remote_copy(
                src_ref=theirs, dst_ref=theirs, send_sem=send.at[i], recv_sem=recv.at[i],
                device_id=(x, y, 1 - c), device_id_type=MESH).wait_recv()
        for cp in cps:
            cp.wait_send()

    return list(pl.pallas_call(
        body, out_shape=[jax.ShapeDtypeStruct(a.shape, a.dtype) for a in arrs], in_specs=[ANY] * n, out_specs=[ANY] * n,
        input_output_aliases={i: i for i in range(n)},
        scratch_shapes=[pltpu.SemaphoreType.DMA((n,)), pltpu.SemaphoreType.DMA((n,))],
        name=name)(*arrs))


def _adamw_math(w, g, m, v):
    m2 = ADAM_B1 * m + (1.0 - ADAM_B1) * g
    v2 = ADAM_B2 * v + (1.0 - ADAM_B2) * (g * g)
    m_hat = m2 / (1.0 - ADAM_B1 ** ADAM_STEP)
    v_hat = v2 / (1.0 - ADAM_B2 ** ADAM_STEP)
    delta = -ADAM_LR * (m_hat / (jnp.sqrt(v_hat) + ADAM_EPS) + ADAM_WD * w)
    return delta, m2, v2


def _adamw_big(ws, gs, ms, vs, name, steps=8):
    n = len(ws)

    def body(*refs):
        for i in range(n):
            w_ref, g_ref, m_ref, v_ref = (refs[k * n + i] for k in range(4))
            d_ref, m2_ref, v2_ref, g2_ref = (refs[(4 + k) * n + i] for k in range(4))
            gv = g_ref[...]
            d_ref[...], m2_ref[...], v2_ref[...] = _adamw_math(w_ref[...], gv, m_ref[...], v_ref[...])
            g2_ref[...] = gv

    specs = [pl.BlockSpec((w.shape[0] // steps, w.shape[1]), lambda i: (i, 0)) for w in ws]
    shapes = [jax.ShapeDtypeStruct(w.shape, F32) for w in ws]
    outs = pl.pallas_call(
        body, grid=(steps,), in_specs=specs * 4, out_specs=specs * 4, out_shape=shapes * 4,
        compiler_params=_cp(("parallel",), VMEM_LIMIT), name=name)(*ws, *gs, *ms, *vs)
    return [tuple(outs[k * n + i] for k in range(4)) for i in range(n)]


def _adamw_small(groups):
    n = len(groups)

    def body(*refs):
        for i in range(n):
            w_ref, g_ref, m_ref, v_ref = refs[4 * i:4 * i + 4]
            d_ref, m2_ref, v2_ref = refs[4 * n + 3 * i:4 * n + 3 * i + 3]
            d_ref[...], m2_ref[...], v2_ref[...] = _adamw_math(w_ref[...], g_ref[...], m_ref[...], v_ref[...])

    flat = [a for grp in groups for a in grp]
    out_shape = [jax.ShapeDtypeStruct(grp[0].shape, F32) for grp in groups for _ in range(3)]
    outs = pl.pallas_call(body, out_shape=out_shape, name="adamw_small")(*flat)
    return [tuple(outs[3 * i:3 * i + 3]) for i in range(n)]


def kernel(x, mem, norm_mix_g, w_in, conv_w, gm_ln_g, gm_ln_b, gm_ws, gm_bs, w_out, norm_x_g, norm_mem_g, w_q, w_kv, w_xo, norm_final_g, loss_target, m_norm_mix_g, m_w_in, m_conv_w, m_gm_ln_g, m_gm_ln_b, m_gm_ws, m_gm_bs, m_w_out, m_norm_x_g, m_norm_mem_g, m_w_q, m_w_kv, m_w_xo, m_norm_final_g, v_norm_mix_g, v_w_in, v_conv_w, v_gm_ln_g, v_gm_ln_b, v_gm_ws, v_gm_bs, v_w_out, v_norm_x_g, v_norm_mem_g, v_w_q, v_w_kv, v_w_xo, v_norm_final_g):
    t = x.shape[1]
    xi = lax.axis_index("x")
    yi = lax.axis_index("y")
    ci = lax.axis_index("c")
    b_idx = jnp.reshape(2 * xi + yi, (1,)).astype(jnp.int32)
    c_idx = jnp.reshape(ci, (1,)).astype(jnp.int32)

    x2d, mem2d, tgt = x[0], mem[0], loss_target[0]
    big = [w_in[0], w_out[0], w_q[0], w_kv[0], w_xo[0]]
    big_m = [m_w_in[0], m_w_out[0], m_w_q[0], m_w_kv[0], m_w_xo[0]]
    big_v = [v_w_in[0], v_w_out[0], v_w_q[0], v_w_kv[0], v_w_xo[0]]
    g3 = norm_final_g.reshape(1, D)

    def pad8(a):
        return jnp.pad(a, ((0, 8 - a.shape[0]), (0, 0)))

    own_blocks = _cast_shards(b_idx, big)

    tril = jnp.tril(jnp.ones((CH, CH), bool))
    wc32 = jnp.where(tril[None], gm_ws[0], 0.0)
    wc = wc32.astype(BF16)
    wct = jnp.swapaxes(wc32, 1, 2).astype(BF16)
    bsb = jnp.broadcast_to(gm_bs[0][:, :, None], (HEADS, CH, CH))

    blk = 2 * xi + yi
    order = jnp.stack([blk, blk ^ 2, blk ^ 1, blk ^ 3]).astype(jnp.int32)
    proj, hb, win_f, cw8, (wout_f,) = _proj_gather(
        order, x2d, norm_mix_g, own_blocks[0], pad8(conv_w[0]), [own_blocks[1]])
    mixin, (wq_f, wkv_f, wxo_f) = _mixer_fwd(proj, cw8, gm_ln_g, gm_ln_b, wc, bsb, own_blocks[2:])
    wout2, wq2, wxo2 = wout_f.reshape(MIX, D), wq_f.reshape(D, D), wxo_f.reshape(D, D)
    k, v = _mem_fwd(mem2d, norm_mem_g, wkv_f)

    (loss_tile, dmix, dx1b, h2b, dq, ob, dx2b, dk, dv, dg2, dg3) = _tail(
        x2d, tgt, mixin, wout2, wq2, wxo2, k, v, norm_x_g, g3)
    dwkv, dwkv_b, dgm = _mem_bwd(mem2d, norm_mem_g, dk, dv, wkv_f)
    dproj, dcw, dlng, dlnb, dwc, dbs8, grad_x, dg1 = _mixer_bwd(
        proj, dmix, cw8, gm_ln_g, gm_ln_b, wc, wct, bsb, win_f, x2d, dx1b, norm_mix_g)

    bc_idx = jnp.concatenate([b_idx, c_idx])
    dwin, dwin_b = _grad_matmul(hb, dproj, dgm, by_cols=True, name="grad_w_in", tk=2048)
    zero = jnp.zeros((1, D), F32)
    loss_row = jnp.broadcast_to(loss_tile[0:1, 0:1], (1, D))
    sv = jnp.concatenate([dg1[0:1], dg2, dgm, dg3, dlng, dlnb, dbs8[0:1], loss_row, dcw], axis=0)
    sw = dwc.reshape(HEADS * CH, CH)
    rx1b = _pair_exchange([dwin_b], [sv, sw], "pair_exchange_b")
    ps_b = _pair_sum(c_idx, [dwin], rx1b[:1], [sv, sw], rx1b[1:], "pair_sum_b")
    sums_b, sums_b_b, psmall = list(ps_b[:1]), list(ps_b[1:2]), list(ps_b[2:])
    send_b, recv_b, src_b, land_b, token_b = _exchange_begin(sums_b_b, psmall, "exchange_b_begin")

    dwxo, dwxo_b = _grad_matmul(ob, dx2b, token_b, by_cols=False, name="grad_w_xo", tk=2048)
    dwq, dwq_b = _grad_matmul(h2b, dq, token_b, by_cols=False, name="grad_w_q", tk=2048)
    dwout, dwout_b = _grad_matmul(mixin, dx1b, token_b, by_cols=False, name="grad_w_out")
    rx1a = _pair_exchange([dwout_b, dwq_b, dwkv_b, dwxo_b], [], "pair_exchange_a")
    ps_a = _pair_sum(c_idx, [dwout, dwq, dwkv, dwxo], rx1a, [], [], "pair_sum_a")
    sums_a, sums_a_b = list(ps_a[:4]), list(ps_a[4:8])
    send_a, recv_a, src_a, land_a, token_a = _exchange_begin(sums_a_b, [], "exchange_a_begin")

    rx2b = _exchange_end(send_b, recv_b, src_b, land_b, 1, [token_a], "exchange_b_end")
    red_b = _chip_sum(bc_idx, sums_b, rx2b[:1], rx2b[1:], psmall, "chip_sum_b")
    gwin, svf, swf = _pair_gather(red_b, "pair_gather_b")
    out_b = _adamw_big(big[:1], [gwin], big_m[:1], big_v[:1], "adamw_w_in")[0]

    def vec_pack(a1, a2, am, a3, lg, lb, bs):
        return jnp.concatenate([a1, a2, am, a3.reshape(1, D), lg, lb, bs.reshape(1, D), zero], axis=0)

    wv = vec_pack(norm_mix_g, norm_x_g, norm_mem_g, norm_final_g, gm_ln_g, gm_ln_b, gm_bs)
    mv = vec_pack(m_norm_mix_g, m_norm_x_g, m_norm_mem_g, m_norm_final_g, m_gm_ln_g, m_gm_ln_b, m_gm_bs)
    vv = vec_pack(v_norm_mix_g, v_norm_x_g, v_norm_mem_g, v_norm_final_g, v_gm_ln_g, v_gm_ln_b, v_gm_bs)
    loss = svf[7, 0]
    gcw = lax.dynamic_slice_in_dim(svf[8:16], blk * (D // N_CHIP), D // N_CHIP, axis=1)
    gws = swf
    gv = svf[0:8]
    (dv_, mv_, vv_), (dc_, mc_, vc_), (dws_, mws_, vws_) = _adamw_small([
        (wv, gv, mv, vv),
        (pad8(conv_w[0]), gcw, pad8(m_conv_w[0]), pad8(v_conv_w[0])),
        (gm_ws.reshape(HEADS * CH, CH), gws, m_gm_ws.reshape(HEADS * CH, CH), v_gm_ws.reshape(HEADS * CH, CH))])

    rx2a = _exchange_end(send_a, recv_a, src_a, land_a, 4, [out_b[0], dv_], "exchange_a_end")
    red_a = _chip_sum(bc_idx, sums_a, rx2a, [], [], "chip_sum_a")
    g_a = _pair_gather(red_a, "pair_gather_a")
    big_out = [out_b] + _adamw_big(big[1:], g_a, big_m[1:], big_v[1:], "adamw_rest")

    def unpack(vecs, cw, ws, bigs):
        r = lambda i: vecs[i:i + 1]
        return [r(0), bigs[0][None], cw[0:3][None], r(4), r(5), ws.reshape(1, HEADS, CH, CH), vecs[6].reshape(1, HEADS, CH),
                bigs[1][None], r(1), r(2), bigs[2][None], bigs[3][None], bigs[4][None], vecs[3]]

    grads_out = unpack(gv, gcw, gws, [o[3] for o in big_out])
    delta_out = unpack(dv_, dc_, dws_, [o[0] for o in big_out])
    m_out = unpack(mv_, mc_, mws_, [o[1] for o in big_out])
    v_out = unpack(vv_, vc_, vws_, [o[2] for o in big_out])
    return (loss, grad_x[None], *grads_out, *delta_out, *m_out, *v_out)
```

```python
import functools
import math

import jax
import jax.numpy as jnp
from jax import lax
from jax.experimental import pallas as pl
from jax.experimental.pallas import tpu as pltpu

F32 = jnp.float32
BF16 = jnp.bfloat16
MESH = pl.DeviceIdType.MESH

D = 1024
SLAB = 1024
N_SLAB = 7
IN_DIM = N_SLAB * SLAB
MIX = 2 * SLAB
HEADS = 8
CH = 128
XH = 4
XD = D // XH
EPS = 1e-6
GELU_C = math.sqrt(2.0 / math.pi)
GELU_A = 0.044715
N_CHIP = 4
IN_BLK = IN_DIM // N_CHIP
KV_BLK = 2 * D // N_CHIP

ADAM_LR, ADAM_B1, ADAM_B2, ADAM_EPS, ADAM_WD, ADAM_STEP = 0.001, 0.9, 0.999, 1e-08, 0.01, 10

VMEM_LIMIT = 60 * 1024 * 1024


def _cp(sem=None, vmem=None):
    return pltpu.CompilerParams(dimension_semantics=sem, vmem_limit_bytes=vmem)


def _full(shape, buffers=None):
    n = len(shape)
    if buffers is None:
        return pl.BlockSpec(shape, lambda *_: (0,) * n)
    return pl.BlockSpec(shape, lambda *_: (0,) * n, pipeline_mode=pl.Buffered(buffers))


ANY = pl.BlockSpec(memory_space=pl.ANY)


def _bdot(a, b):
    return jnp.dot(a.astype(BF16), b.astype(BF16), preferred_element_type=F32)


def _bdot_nt(a, b):
    return lax.dot_general(a.astype(BF16), b.astype(BF16), (((1,), (1,)), ((), ())), preferred_element_type=F32)


def _bdot_tn(a, b):
    return lax.dot_general(a.astype(BF16), b.astype(BF16), (((0,), (0,)), ((), ())), preferred_element_type=F32)


def _rms(x, g):
    r = lax.rsqrt(jnp.mean(x * x, axis=-1, keepdims=True) + EPS)
    return x * r * g, r


def _rms_bwd(dy, x, r, g):
    gdy = dy * g
    dx = r * gdy - x * (r * r * r) * jnp.mean(x * gdy, axis=-1, keepdims=True)
    dg = jnp.sum(dy * x * r, axis=0, keepdims=True)
    return dx, dg


def _gelu_parts(x):
    x2 = x * x
    t = jnp.tanh(GELU_C * (x + GELU_A * x * x2))
    val = 0.5 * x * (1.0 + t)
    grad = 0.5 * (1.0 + t) + 0.5 * x * (1.0 - t * t) * (GELU_C * (1.0 + 3.0 * GELU_A * x2))
    return val, grad


def _gelu(x):
    return 0.5 * x * (1.0 + jnp.tanh(GELU_C * (x + GELU_A * x * x * x)))


def _sigmoid(z):
    return 1.0 / (1.0 + jnp.exp(-z))


def _cast_shards(b_idx, arrs):
    n = len(arrs)
    steps = 8

    def body(b_ref, *refs):
        for i in range(n):
            refs[n + i][...] = refs[i][...].astype(BF16)

    in_specs = [pl.BlockSpec((a.shape[0] // steps, a.shape[1]), lambda i, b: (i, 0)) for a in arrs]
    out_specs = [pl.BlockSpec((None, a.shape[0] // steps, a.shape[1]), lambda i, b: (b[0], i, 0)) for a in arrs]
    return pl.pallas_call(
        body, out_shape=[jax.ShapeDtypeStruct((N_CHIP,) + a.shape, BF16) for a in arrs],
        grid_spec=pltpu.PrefetchScalarGridSpec(num_scalar_prefetch=1, grid=(steps,), in_specs=in_specs, out_specs=out_specs),
        compiler_params=_cp(("arbitrary",)), name="cast_shards")(b_idx, *arrs)


def _proj_gather(order, x, g, win_own, cw8s, more, tm=1024):
    t = x.shape[0]
    ni = t // tm
    nm = len(more)

    def body(*refs):
        order_ref, x_ref, g_ref, win_in, cw_in = refs[:5]
        o_ref, hb_ref, win_f, cw_out = refs[5 + nm:9 + nm]
        more_out = refs[9 + nm:9 + 2 * nm]
        hbuf, wv, cw_s, cw_r, loc = refs[9 + 2 * nm:14 + 2 * nm]
        g_in = _Gather([win_f], *refs[14 + 2 * nm:18 + 2 * nm])
        g_more = _Gather(more_out, *refs[18 + 2 * nm:22 + 2 * nm])
        j, i = pl.program_id(0), pl.program_id(1)
        x, y, c, chips = _coords()
        b = 2 * x + y
        blks = [2 * chip[0] + chip[1] for chip in chips]

        def cw_cols(blk):
            return cw_out.at[:, pl.ds(blk * (D // N_CHIP), D // N_CHIP)]

        def cw_copy(k, blk):
            src = cw_in if blk is None else cw_cols(blk)
            return pltpu.make_async_remote_copy(src_ref=src, dst_ref=cw_cols(b if blk is None else blk), send_sem=cw_s.at[k],
                                                recv_sem=cw_r.at[k], device_id=(*chips[k], c), device_id_type=MESH)

        cw_local = pltpu.make_async_copy(cw_in, cw_cols(b), loc.at[1])

        def load(blk, slot):
            return pltpu.make_async_copy(win_f.at[blk], wv.at[slot], loc.at[2 + slot])

        @pl.when((j == 0) & (i == 0))
        def _():
            g_in.start()
            cw_local.start()
            for k in range(3):
                cw_copy(k, None).start()
            load(b, 0).start()
            load(b, 0).wait()

        @pl.when((j == 1) & (i == 0))
        def _():
            g_in.hop()
            g_more.start()
            g_in.near_ready()
            load(g_in.bx, 1).start()
            load(g_in.by, 0).start()
            load(g_in.bx, 1).wait()

        @pl.when((j == 2) & (i == 0))
        def _():
            load(g_in.by, 0).wait()
            g_in.far()
            g_in.far_ready()
            load(g_in.bd, 1).start()

        @pl.when((j == 3) & (i == 0))
        def _():
            load(g_in.bd, 1).wait()
            g_more.hop()

        rows = pl.ds(pl.multiple_of(i * tm, tm), tm)

        @pl.when(j == 0)
        def _():
            h, _ = _rms(x_ref[...], g_ref[...])
            hbuf[rows, :] = h.astype(BF16)
            hb_ref[...] = h.astype(BF16)

        @pl.when((j == N_CHIP - 1) & (i == ni - 1))
        def _():
            g_more.far()

        o_ref[...] = jnp.dot(hbuf[rows, :], wv[lax.rem(j, 2)], preferred_element_type=F32).astype(BF16)

        @pl.when((j == N_CHIP - 1) & (i == ni - 1))
        def _():
            for k in range(3):
                cw_copy(k, blks[k]).wait_recv()
            for k in range(3):
                cw_copy(k, None).wait_send()
            cw_local.wait()
            g_more.near_ready()
            g_more.far_ready()
            g_in.drain()
            g_more.drain()

    first = lambda j, i: jnp.where(j == 0, i, ni - 1)
    in_specs = [pl.BlockSpec((tm, D), lambda j, i, o: (first(j, i), 0)), pl.BlockSpec((1, D), lambda j, i, o: (0, 0)),
                ANY, ANY] + [ANY] * nm
    out_specs = [pl.BlockSpec((tm, IN_BLK), lambda j, i, o: (i, o[j])),
                 pl.BlockSpec((tm, D), lambda j, i, o: (first(j, i), 0)), ANY, ANY] + [ANY] * nm
    outs = pl.pallas_call(
        body, out_shape=[jax.ShapeDtypeStruct((t, IN_DIM), BF16), jax.ShapeDtypeStruct((t, D), BF16),
                         jax.ShapeDtypeStruct(win_own.shape, BF16), jax.ShapeDtypeStruct((8, D), F32)]
        + [jax.ShapeDtypeStruct(f.shape, f.dtype) for f in more],
        grid_spec=pltpu.PrefetchScalarGridSpec(
            num_scalar_prefetch=1, grid=(N_CHIP, ni), in_specs=in_specs, out_specs=out_specs,
            scratch_shapes=[pltpu.VMEM((t, D), BF16), pltpu.VMEM((2, D, IN_BLK), BF16)]
            + [pltpu.SemaphoreType.DMA((3,))] * 2 + [pltpu.SemaphoreType.DMA((4,))] + _gather_sems(1) + _gather_sems(nm)),
        input_output_aliases={3: 2, **{5 + w: 4 + w for w in range(nm)}},
        compiler_params=_cp(("arbitrary", "arbitrary"), VMEM_LIMIT), name="proj_gather")(order, x, g, win_own, cw8s, *more)
    return outs[0], outs[1], outs[2], outs[3], outs[4:]


class _Gather:
    def __init__(self, outs, ici_s, ici_r, d2d_s, d2d_r):
        x, y, c, _ = _coords()
        self.outs, self.c = outs, c
        self.sems = ici_s, ici_r, d2d_s, d2d_r
        self.b, self.bx, self.by, self.bd = 2 * x + y, 2 * (1 - x) + y, 2 * x + (1 - y), 2 * (1 - x) + (1 - y)
        self.xn, self.yn, self.sib = (1 - x, y, c), (x, 1 - y, c), (x, y, 1 - c)

    def piece(self, w, blk, hc, quarter=None):
        hr = self.outs[w].shape[1] // 2
        if quarter is None:
            return self.outs[w].at[blk, pl.ds(hc * hr, hr)]
        return self.outs[w].at[blk, pl.ds(hc * hr + quarter * (hr // 2), hr // 2)]

    def ici(self, w, k, ref, to):
        return pltpu.make_async_remote_copy(src_ref=ref, dst_ref=ref, send_sem=self.sems[0].at[w, k],
                                            recv_sem=self.sems[1].at[w, k], device_id=to, device_id_type=MESH)

    def d2d(self, w, k, ref):
        return pltpu.make_async_remote_copy(src_ref=ref, dst_ref=ref, send_sem=self.sems[2].at[w, k],
                                            recv_sem=self.sems[3].at[w, k], device_id=self.sib, device_id_type=MESH)

    def start(self):
        for w in range(len(self.outs)):
            mine = self.piece(w, self.b, self.c)
            self.ici(w, 0, mine, self.xn).start()
            self.ici(w, 1, mine, self.yn).start()

    def hop(self):
        c = self.c
        for w in range(len(self.outs)):
            self.ici(w, 0, self.piece(w, self.bx, c), self.xn).wait_recv()
            self.ici(w, 1, self.piece(w, self.by, c), self.yn).wait_recv()
            self.ici(w, 2, self.piece(w, self.bx, c, 0), self.yn).start()
            self.ici(w, 3, self.piece(w, self.by, c, 1), self.xn).start()
            self.d2d(w, 0, self.piece(w, self.bx, c)).start()
            self.d2d(w, 1, self.piece(w, self.by, c)).start()

    def near_ready(self):
        for w in range(len(self.outs)):
            self.d2d(w, 0, self.piece(w, self.bx, 1 - self.c)).wait_recv()
            self.d2d(w, 1, self.piece(w, self.by, 1 - self.c)).wait_recv()

    def far(self):
        c = self.c
        for w in range(len(self.outs)):
            self.ici(w, 2, self.piece(w, self.bd, c, 0), self.yn).wait_recv()
            self.ici(w, 3, self.piece(w, self.bd, c, 1), self.xn).wait_recv()
            self.d2d(w, 2, self.piece(w, self.bd, c, 0)).start()
            self.d2d(w, 3, self.piece(w, self.bd, c, 1)).start()

    def far_ready(self):
        for w in range(len(self.outs)):
            self.d2d(w, 2, self.piece(w, self.bd, 1 - self.c, 0)).wait_recv()
            self.d2d(w, 3, self.piece(w, self.bd, 1 - self.c, 1)).wait_recv()

    def drain(self):
        c = self.c
        for w in range(len(self.outs)):
            mine = self.piece(w, self.b, c)
            self.ici(w, 0, mine, self.xn).wait_send()
            self.ici(w, 1, mine, self.yn).wait_send()
            self.ici(w, 2, self.piece(w, self.bx, c, 0), self.yn).wait_send()
            self.ici(w, 3, self.piece(w, self.by, c, 1), self.xn).wait_send()
            self.d2d(w, 0, self.piece(w, self.bx, c)).wait_send()
            self.d2d(w, 1, self.piece(w, self.by, c)).wait_send()
            self.d2d(w, 2, self.piece(w, self.bd, c, 0)).wait_send()
            self.d2d(w, 3, self.piece(w, self.bd, c, 1)).wait_send()


def _gather_sems(nw):
    return [pltpu.SemaphoreType.DMA((max(nw, 1), 4))] * 4


def _mixer_fwd(proj, cw8, lng, lnb, wc, bsb, fulls, tm=256):
    t = proj.shape[0]
    nt = t // tm
    nch = tm // CH
    nw = len(fulls)

    def body(*refs):
        p_ref, cw_ref, lng_ref, lnb_ref, wc_ref, bsb_ref = refs[:6]
        mix_ref = refs[6 + nw]
        w_outs = refs[7 + nw:7 + 2 * nw]
        prev_ref = refs[7 + 2 * nw]
        gather = _Gather(w_outs, *refs[8 + 2 * nw:])

        @pl.when(pl.program_id(0) == 0)
        def _():
            gather.start()
            prev_ref[...] = jnp.zeros_like(prev_ref)

        @pl.when(pl.program_id(0) == nt // 2)
        def _():
            gather.hop()

        @pl.when(pl.program_id(0) == nt - 1)
        def _():
            gather.far()

        rows = lax.broadcasted_iota(jnp.int32, (tm, CH), 0)
        for s in range(HEADS):
            cs = pl.ds(CH * s, CH)

            def slab(k):
                return p_ref[:, pl.ds(k * SLAB + CH * s, CH)].astype(F32)

            gb, gc, xa, za = slab(0), slab(1), slab(2), slab(3)
            cx = gc * xa
            p6 = jnp.broadcast_to(prev_ref[6:7, cs], (tm, CH))
            p7 = jnp.broadcast_to(prev_ref[7:8, cs], (tm, CH))
            c1 = jnp.where(rows == 0, p7, pltpu.roll(cx, 1, 0))
            c2 = jnp.where(rows == 0, p6, jnp.where(rows == 1, p7, pltpu.roll(cx, 2, 0)))
            prev_ref[:, cs] = cx[tm - 8:, :]
            cv = cw_ref[0:1, cs] * c2 + cw_ref[1:2, cs] * c1 + cw_ref[2:3, cs] * cx
            mix_ref[:, cs] = (gb * cv * (za * _sigmoid(za))).astype(BF16)

            u, v, zb = slab(4), slab(5), slab(6)
            ug, vg = _gelu(u), _gelu(v)
            dlt = vg - jnp.mean(vg, axis=-1, keepdims=True)
            vhat = dlt * lax.rsqrt(jnp.mean(dlt * dlt, axis=-1, keepdims=True) + EPS)
            vn = (vhat * lng_ref[:, cs] + lnb_ref[:, cs]).astype(BF16)
            gate = ug * (zb * _sigmoid(zb))
            for c in range(nch):
                rs = slice(CH * c, CH * (c + 1))
                sp = jnp.dot(wc_ref[s], vn[rs], preferred_element_type=F32) + bsb_ref[s]
                mix_ref[rs, pl.ds(SLAB + CH * s, CH)] = (gate[rs] * sp).astype(BF16)

        @pl.when(pl.program_id(0) == nt - 1)
        def _():
            gather.near_ready()
            gather.far_ready()
            gather.drain()

    sems = _gather_sems(nw)
    outs = pl.pallas_call(
        body, grid=(nt,),
        in_specs=[pl.BlockSpec((tm, IN_DIM), lambda i: (i, 0)), _full((8, D)), _full((1, D)), _full((1, D)),
                  _full((HEADS, CH, CH)), _full((HEADS, CH, CH))] + [ANY] * nw,
        out_specs=[pl.BlockSpec((tm, MIX), lambda i: (i, 0))] + [ANY] * nw,
        out_shape=[jax.ShapeDtypeStruct((t, MIX), BF16)] + [jax.ShapeDtypeStruct(f.shape, f.dtype) for f in fulls],
        input_output_aliases={6 + w: 1 + w for w in range(nw)},
        scratch_shapes=[pltpu.VMEM((8, D), F32)] + sems,
        compiler_params=_cp(("arbitrary",), VMEM_LIMIT), name="mixer_fwd")(proj, cw8, lng, lnb, wc, bsb, *fulls)
    return outs[0], outs[1:]


def _mem_fwd(mem, gm, wkv_f):
    n_mem = mem.shape[0]

    def body(mem_ref, gm_ref, w_ref, k_ref, v_ref):
        m, _ = _rms(mem_ref[...], gm_ref[...])
        mb = m.astype(BF16)
        for j in range(N_CHIP):
            dst = k_ref if j < 2 else v_ref
            dst[:, pl.ds(KV_BLK * (j % 2), KV_BLK)] = jnp.dot(mb, w_ref[j], preferred_element_type=F32).astype(BF16)

    return pl.pallas_call(
        body, out_shape=[jax.ShapeDtypeStruct((n_mem, D), BF16), jax.ShapeDtypeStruct((n_mem, D), BF16)],
        compiler_params=_cp(None, VMEM_LIMIT), name="mem_fwd")(mem, gm, wkv_f)


def _tail(x, tgt, mixin, wout, wq, wxo, k, v, g2, g3, tm=512, sub=512):
    t = x.shape[0]
    n_mem = k.shape[0]
    scale = 1.0 / math.sqrt(XD)

    def body(x_ref, tgt_ref, mix_ref, wout_ref, wq_ref, wxo_ref, k_ref, v_ref, g2_ref, g3_ref,
             loss_ref, dmix_ref, dx1b_ref, h2_ref, dq_ref, o_ref, dx2b_ref, dk_ref, dv_ref, dg2_ref, dg3_ref):
        @pl.when(pl.program_id(0) == 0)
        def _():
            loss_ref[...] = jnp.zeros_like(loss_ref)
            dk_ref[...] = jnp.zeros_like(dk_ref)
            dv_ref[...] = jnp.zeros_like(dv_ref)
            dg2_ref[...] = jnp.zeros_like(dg2_ref)
            dg3_ref[...] = jnp.zeros_like(dg3_ref)

        g2, g3 = g2_ref[...], g3_ref[...]
        for sb in range(tm // sub):
            rs = pl.ds(sub * sb, sub)
            x1 = x_ref[rs, :] + jnp.dot(mix_ref[rs, :], wout_ref[...], preferred_element_type=F32)
            h2, r2 = _rms(x1, g2)
            h2b = h2.astype(BF16)
            h2_ref[rs, :] = h2b
            q = jnp.dot(h2b, wq_ref[...], preferred_element_type=F32).astype(BF16)
            probs, outs = [], []
            for hd in range(XH):
                hs = pl.ds(XD * hd, XD)
                s = _bdot_nt(q[:, XD * hd:XD * (hd + 1)], k_ref[:, hs]) * scale
                e = jnp.exp(s - jnp.max(s, axis=-1, keepdims=True))
                p = e / jnp.sum(e, axis=-1, keepdims=True)
                probs.append(p)
                outs.append(_bdot(p, v_ref[:, hs]))
            ob = jnp.concatenate(outs, axis=-1).astype(BF16)
            o_ref[rs, :] = ob
            x2 = x1 + jnp.dot(ob, wxo_ref[...], preferred_element_type=F32)
            y, r3 = _rms(x2, g3)
            diff = y - tgt_ref[rs, :]
            row_loss = jnp.sum(diff * diff, axis=-1, keepdims=True)
            loss_ref[...] += jnp.broadcast_to(jnp.sum(row_loss, axis=0, keepdims=True) * (0.5 / D), loss_ref.shape)

            dx2, dg3 = _rms_bwd(diff * (1.0 / D), x2, r3, g3)
            dg3_ref[...] += dg3
            dx2b = dx2.astype(BF16)
            dx2b_ref[rs, :] = dx2b
            do = _bdot_nt(dx2b, wxo_ref[...])
            dqs = []
            for hd in range(XH):
                hs = pl.ds(XD * hd, XD)
                p = probs[hd]
                do_h = do[:, XD * hd:XD * (hd + 1)]
                dv_ref[:, hs] += _bdot_tn(p, do_h)
                dp = _bdot_nt(do_h, v_ref[:, hs])
                ds = p * (dp - jnp.sum(dp * p, axis=-1, keepdims=True))
                dqs.append(_bdot(ds, k_ref[:, hs]) * scale)
                dk_ref[:, hs] += _bdot_tn(ds, q[:, XD * hd:XD * (hd + 1)]) * scale
            dq = jnp.concatenate(dqs, axis=-1).astype(BF16)
            dq_ref[rs, :] = dq
            dx1n, dg2 = _rms_bwd(_bdot_nt(dq, wq_ref[...]), x1, r2, g2)
            dg2_ref[...] += dg2
            dx1b = (dx2 + dx1n).astype(BF16)
            dx1b_ref[rs, :] = dx1b
            dmix_ref[rs, :] = _bdot_nt(dx1b, wout_ref[...]).astype(BF16)

    tok = lambda w: pl.BlockSpec((tm, w), lambda i: (i, 0))
    return pl.pallas_call(
        body, grid=(t // tm,),
        in_specs=[tok(D), tok(D), tok(MIX), _full((MIX, D), 1), _full((D, D), 1), _full((D, D), 1),
                  _full((n_mem, D), 1), _full((n_mem, D), 1), _full((1, D)), _full((1, D))],
        out_specs=[_full((8, 128)), tok(MIX), tok(D), tok(D), tok(D), tok(D), tok(D),
                   _full((n_mem, D)), _full((n_mem, D)), _full((1, D)), _full((1, D))],
        out_shape=[jax.ShapeDtypeStruct((8, 128), F32), jax.ShapeDtypeStruct((t, MIX), BF16),
                   jax.ShapeDtypeStruct((t, D), BF16),
                   jax.ShapeDtypeStruct((t, D), BF16), jax.ShapeDtypeStruct((t, D), BF16),
                   jax.ShapeDtypeStruct((t, D), BF16), jax.ShapeDtypeStruct((t, D), BF16),
                   jax.ShapeDtypeStruct((n_mem, D), F32), jax.ShapeDtypeStruct((n_mem, D), F32),
                   jax.ShapeDtypeStruct((1, D), F32), jax.ShapeDtypeStruct((1, D), F32)],
        compiler_params=_cp(("arbitrary",), VMEM_LIMIT), name="tail")(x, tgt, mixin, wout, wq, wxo, k, v, g2, g3)


def _mem_bwd(mem, gm, dk, dv, wkv_f):
    def body(mem_ref, gm_ref, dk_ref, dv_ref, w_ref, dw_ref, dwb_ref, dgm_ref):
        mem_v = mem_ref[...]
        m, rm = _rms(mem_v, gm_ref[...])
        mb = m.astype(BF16)
        dm = jnp.zeros_like(mem_v)
        for j in range(N_CHIP):
            src = dk_ref if j < 2 else dv_ref
            dkv = src[:, pl.ds(KV_BLK * (j % 2), KV_BLK)].astype(BF16)
            dw = _bdot_tn(mb, dkv)
            dw_ref[j] = dw
            dwb_ref[j] = dw.astype(BF16)
            dm = dm + _bdot_nt(dkv, w_ref[j])
        dgm_ref[...] = jnp.sum(dm * mem_v * rm, axis=0, keepdims=True)

    return pl.pallas_call(
        body, out_shape=[jax.ShapeDtypeStruct((N_CHIP, D, KV_BLK), F32), jax.ShapeDtypeStruct((N_CHIP, D, KV_BLK), BF16),
                         jax.ShapeDtypeStruct((1, D), F32)],
        compiler_params=_cp(None, VMEM_LIMIT), name="mem_bwd")(mem, gm, dk, dv, wkv_f)


def _mixer_bwd(proj, dmix, cw8, lng, lnb, wc, wct, bsb, win_f, x, dx1, g1, tm=256):
    t = proj.shape[0]
    nt = t // tm
    nch = tm // CH
    hb = 16
    pair = 2 * CH

    def body(p_ref, pgc_ref, pxa_ref, dm_ref, cw_ref, lng_ref, lnb_ref, wc_ref, wct_ref, bsb_ref, w_ref, x_ref,
             dx1_ref, g1_ref, dp_ref, dcw_ref, dlng_ref, dlnb_ref, dwc_ref, dbs_ref, gx_ref, dg1_ref,
             next_ref, dh_ref):
        i = pl.program_id(0)

        @pl.when(i == 0)
        def _():
            next_ref[...] = jnp.zeros_like(next_ref)
            dcw_ref[...] = jnp.zeros_like(dcw_ref)
            dlng_ref[...] = jnp.zeros_like(dlng_ref)
            dlnb_ref[...] = jnp.zeros_like(dlnb_ref)
            dwc_ref[...] = jnp.zeros_like(dwc_ref)
            dbs_ref[...] = jnp.zeros_like(dbs_ref)
            dg1_ref[...] = jnp.zeros_like(dg1_ref)

        first_tile = i == nt - 1
        rows = lax.broadcasted_iota(jnp.int32, (tm, CH), 0)
        ones8 = jnp.ones((8, CH), BF16)
        for s in range(HEADS):
            cs = pl.ds(CH * s, CH)

            def slab(k):
                return p_ref[:, pl.ds(k * SLAB + CH * s, CH)].astype(F32)

            gb, gc, xa, za = slab(0), slab(1), slab(2), slab(3)
            da = dm_ref[:, cs].astype(F32)
            cx = gc * xa
            cxp = pgc_ref[:, cs].astype(F32) * pxa_ref[:, cs].astype(F32)
            cxp = jnp.where(first_tile, jnp.zeros_like(cxp), cxp)
            p6 = jnp.broadcast_to(cxp[hb - 2:hb - 1, :], (tm, CH))
            p7 = jnp.broadcast_to(cxp[hb - 1:hb, :], (tm, CH))
            c1 = jnp.where(rows == 0, p7, pltpu.roll(cx, 1, 0))
            c2 = jnp.where(rows == 0, p6, jnp.where(rows == 1, p7, pltpu.roll(cx, 2, 0)))
            w0, w1, w2 = cw_ref[0:1, cs], cw_ref[1:2, cs], cw_ref[2:3, cs]
            cv = w0 * c2 + w1 * c1 + w2 * cx
            sg = _sigmoid(za)
            sa = za * sg
            dcv = da * gb * sa
            dp_ref[:, pl.ds(0 * SLAB + CH * s, CH)] = (da * cv * sa).astype(BF16)
            dp_ref[:, pl.ds(3 * SLAB + CH * s, CH)] = (da * gb * cv * (sg * (1.0 + za * (1.0 - sg)))).astype(BF16)
            n0 = jnp.broadcast_to(next_ref[0:1, cs], (tm, CH))
            n1 = jnp.broadcast_to(next_ref[1:2, cs], (tm, CH))
            u1 = jnp.where(rows == tm - 1, n0, pltpu.roll(dcv, tm - 1, 0))
            u2 = jnp.where(rows == tm - 2, n0, jnp.where(rows == tm - 1, n1, pltpu.roll(dcv, tm - 2, 0)))
            next_ref[:, cs] = dcv[0:8, :]
            dcx = w2 * dcv + w1 * u1 + w0 * u2
            dp_ref[:, pl.ds(1 * SLAB + CH * s, CH)] = (dcx * xa).astype(BF16)
            dp_ref[:, pl.ds(2 * SLAB + CH * s, CH)] = (dcx * gc).astype(BF16)
            dcw_ref[0:1, cs] += jnp.sum(dcv * c2, axis=0, keepdims=True)
            dcw_ref[1:2, cs] += jnp.sum(dcv * c1, axis=0, keepdims=True)
            dcw_ref[2:3, cs] += jnp.sum(dcv * cx, axis=0, keepdims=True)

            u, v, zb = slab(4), slab(5), slab(6)
            db = dm_ref[:, pl.ds(SLAB + CH * s, CH)].astype(F32)
            ug, ugrad = _gelu_parts(u)
            vg, vgrad = _gelu_parts(v)
            dlt = vg - jnp.mean(vg, axis=-1, keepdims=True)
            rstd = lax.rsqrt(jnp.mean(dlt * dlt, axis=-1, keepdims=True) + EPS)
            vhat = dlt * rstd
            lg = lng_ref[:, cs]
            vn = (vhat * lg + lnb_ref[:, cs]).astype(BF16)
            sgb = _sigmoid(zb)
            szb = zb * sgb
            sps, dvns = [], []
            dbs = jnp.zeros((8, CH), F32)
            dwc = jnp.zeros((CH, CH), F32)
            for c in range(nch):
                rs = slice(CH * c, CH * (c + 1))
                sp = jnp.dot(wc_ref[s], vn[rs], preferred_element_type=F32) + bsb_ref[s]
                dsp = (db[rs] * ug[rs] * szb[rs]).astype(BF16)
                dbs = dbs + lax.dot_general(ones8, dsp, (((1,), (1,)), ((), ())), preferred_element_type=F32)
                dwc = dwc + lax.dot_general(dsp, vn[rs], (((1,), (1,)), ((), ())), preferred_element_type=F32)
                dvns.append(jnp.dot(wct_ref[s], dsp, preferred_element_type=F32))
                sps.append(sp)
            sp = jnp.concatenate(sps, axis=0)
            dvn = jnp.concatenate(dvns, axis=0)
            dbs_ref[:, cs] += dbs
            dwc_ref[s] += dwc
            dlng_ref[:, cs] += jnp.sum(dvn * vhat, axis=0, keepdims=True)
            dlnb_ref[:, cs] += jnp.sum(dvn, axis=0, keepdims=True)
            dvhat = dvn * lg
            dvg = rstd * (dvhat - jnp.mean(dvhat, axis=-1, keepdims=True)
                          - vhat * jnp.mean(dvhat * vhat, axis=-1, keepdims=True))
            dp_ref[:, pl.ds(4 * SLAB + CH * s, CH)] = (db * sp * szb * ugrad).astype(BF16)
            dp_ref[:, pl.ds(5 * SLAB + CH * s, CH)] = (dvg * vgrad).astype(BF16)
            dp_ref[:, pl.ds(6 * SLAB + CH * s, CH)] = (db * ug * sp * (sgb * (1.0 + zb * (1.0 - sgb)))).astype(BF16)

            if s % 2 == 1:
                part = None
                for k in range(N_SLAB):
                    col = k * SLAB + pair * (s // 2)
                    blk, off = divmod(col, IN_BLK)
                    term = lax.dot_general(dp_ref[:, pl.ds(col, pair)], w_ref[blk, :, pl.ds(off, pair)],
                                           (((1,), (1,)), ((), ())), preferred_element_type=F32)
                    part = term if part is None else part + term
                if s == 1:
                    dh_ref[...] = part
                else:
                    dh_ref[...] += part

        xv = x_ref[...]
        r = lax.rsqrt(jnp.mean(xv * xv, axis=-1, keepdims=True) + EPS)
        dxn, dg = _rms_bwd(dh_ref[...], xv, r, g1_ref[...])
        gx_ref[...] = dx1_ref[...].astype(F32) + dxn
        dg1_ref[0:1, :] += dg

        @pl.when(i == nt - 1)
        def _():
            tril = lax.broadcasted_iota(jnp.int32, (CH, CH), 0) >= lax.broadcasted_iota(jnp.int32, (CH, CH), 1)
            for s in range(HEADS):
                dwc_ref[s] = jnp.where(tril, dwc_ref[s], 0.0)

    rev = lambda i: nt - 1 - i
    halo = lambda col: pl.BlockSpec((hb, SLAB), lambda i: (jnp.maximum(rev(i) * (tm // hb) - 1, 0), col))
    tok = lambda w: pl.BlockSpec((tm, w), lambda i: (rev(i), 0))
    return pl.pallas_call(
        body, grid=(nt,),
        in_specs=[tok(IN_DIM), halo(1), halo(2), tok(MIX), _full((8, D)), _full((1, D)), _full((1, D)),
                  _full((HEADS, CH, CH)), _full((HEADS, CH, CH)), _full((HEADS, CH, CH)),
                  _full((N_CHIP, D, IN_BLK), 1), tok(D), tok(D), _full((1, D))],
        out_specs=[tok(IN_DIM), _full((8, D)), _full((1, D)), _full((1, D)), _full((HEADS, CH, CH)), _full((8, D)),
                   tok(D), _full((8, D))],
        out_shape=[jax.ShapeDtypeStruct((t, IN_DIM), BF16), jax.ShapeDtypeStruct((8, D), F32),
                   jax.ShapeDtypeStruct((1, D), F32), jax.ShapeDtypeStruct((1, D), F32),
                   jax.ShapeDtypeStruct((HEADS, CH, CH), F32), jax.ShapeDtypeStruct((8, D), F32),
                   jax.ShapeDtypeStruct((t, D), F32), jax.ShapeDtypeStruct((8, D), F32)],
        scratch_shapes=[pltpu.VMEM((8, D), F32), pltpu.VMEM((tm, D), F32)],
        compiler_params=_cp(("arbitrary",), VMEM_LIMIT), name="mixer_bwd")(
            proj, proj, proj, dmix, cw8, lng, lnb, wc, wct, bsb, win_f, x, dx1, g1)


def _grad_matmul(a, b, after, *, by_cols, name, tk=1024):
    t, m = a.shape
    n = b.shape[1]
    nk = t // tk
    nj = N_CHIP if by_cols else 1
    bn = n // nj

    def body(a_ref, b_ref, after_ref, o_ref, ob_ref):
        kk = pl.program_id(1)
        part = lax.dot_general(a_ref[...], b_ref[...], (((0,), (0,)), ((), ())), preferred_element_type=F32)

        @pl.when(kk == 0)
        def _():
            o_ref[...] = part

        @pl.when(kk > 0)
        def _():
            o_ref[...] += part

        @pl.when(kk == nk - 1)
        def _():
            ob_ref[...] = o_ref[...].astype(BF16)

    a_spec = pl.BlockSpec((tk, m), lambda j, k: (k, 0))
    b_spec = pl.BlockSpec((tk, bn), lambda j, k: (k, j))
    o_spec = pl.BlockSpec((None, m, bn), lambda j, k: (j, 0, 0))
    o32, o16 = pl.pallas_call(
        body, grid=(nj, nk), in_specs=[a_spec, b_spec, ANY], out_specs=[o_spec, o_spec],
        out_shape=[jax.ShapeDtypeStruct((nj, m, bn), F32), jax.ShapeDtypeStruct((nj, m, bn), BF16)],
        compiler_params=_cp(("parallel", "arbitrary"), VMEM_LIMIT), name=name)(a, b, after)
    if by_cols:
        return o32, o16
    return o32.reshape(N_CHIP, m // N_CHIP, n), o16.reshape(N_CHIP, m // N_CHIP, n)


def _coords():
    x, y, c = lax.axis_index("x"), lax.axis_index("y"), lax.axis_index("c")
    chips = [(1 - x, y), (x, 1 - y), (1 - x, 1 - y)]
    return x, y, c, chips


def _pair_reduce(c_idx, grads, grads_b, smalls, name):
    ng, ns = len(grads), len(smalls)
    halves = [g.shape[1] // 2 for g in grads]

    def body(c_ref, *refs):
        g_in, gb_any = refs[:ng], refs[ng:2 * ng]
        s_own, s_any = refs[2 * ng:2 * ng + ns], refs[2 * ng + ns:2 * ng + 2 * ns]
        o = refs[2 * ng + 2 * ns:4 * ng + 3 * ns]
        lands = refs[4 * ng + 3 * ns:5 * ng + 4 * ns]
        send, recv = refs[5 * ng + 4 * ns:]
        x, y, c, _ = _coords()
        j = pl.program_id(0)

        def big(i, blk):
            return pltpu.make_async_remote_copy(
                src_ref=gb_any[i].at[blk, pl.ds((1 - c) * halves[i], halves[i])], dst_ref=lands[i].at[blk],
                send_sem=send.at[i, blk], recv_sem=recv.at[i, blk], device_id=(x, y, 1 - c), device_id_type=MESH)

        def small(i):
            return pltpu.make_async_remote_copy(
                src_ref=s_any[i].at[1 - c], dst_ref=lands[ng + i],
                send_sem=send.at[ng + i, 0], recv_sem=recv.at[ng + i, 0], device_id=(x, y, 1 - c), device_id_type=MESH)

        @pl.when(j == 0)
        def _():
            for blk in range(N_CHIP):
                for i in range(ng):
                    big(i, blk).start()
            for i in range(ns):
                small(i).start()

        for i in range(ng):
            big(i, j).wait_recv()
            tot = g_in[i][...] + lands[i][j].astype(F32)
            o[i][...] = tot
            o[ng + i][...] = tot.astype(BF16)

        @pl.when(j == N_CHIP - 1)
        def _():
            for i in range(ns):
                small(i).wait_recv()
                o[2 * ng + i][...] = s_own[i][...] + lands[ng + i][...]
                small(i).wait_send()
            for blk in range(N_CHIP):
                for i in range(ng):
                    big(i, blk).wait_send()

    in_specs = [pl.BlockSpec((None, None, halves[i], g.shape[2]), lambda b, c: (b, c[0], 0, 0)) for i, g in enumerate(grads)]
    in_specs += [ANY] * ng
    in_specs += [pl.BlockSpec((None, s.shape[0] // 2, s.shape[1]), lambda b, c: (c[0], 0, 0)) for s in smalls]
    in_specs += [ANY] * ns
    blk = [pl.BlockSpec((None, halves[i], g.shape[2]), lambda b, c: (b, 0, 0)) for i, g in enumerate(grads)]
    out_specs = blk + blk + [pl.BlockSpec((s.shape[0] // 2, s.shape[1]), lambda b, c: (0, 0)) for s in smalls]
    out_shape = [jax.ShapeDtypeStruct((N_CHIP, halves[i], g.shape[2]), F32) for i, g in enumerate(grads)]
    out_shape += [jax.ShapeDtypeStruct((N_CHIP, halves[i], g.shape[2]), BF16) for i, g in enumerate(grads)]
    out_shape += [jax.ShapeDtypeStruct((s.shape[0] // 2, s.shape[1]), F32) for s in smalls]
    scratch = [pltpu.VMEM((N_CHIP, halves[i], g.shape[2]), BF16) for i, g in enumerate(grads)]
    scratch += [pltpu.VMEM((s.shape[0] // 2, s.shape[1]), F32) for s in smalls]
    scratch += [pltpu.SemaphoreType.DMA((ng + ns, N_CHIP)), pltpu.SemaphoreType.DMA((ng + ns, N_CHIP))]
    grads4 = [g.reshape(N_CHIP, 2, halves[i], g.shape[2]) for i, g in enumerate(grads)]
    smalls3 = [s.reshape(2, s.shape[0] // 2, s.shape[1]) for s in smalls]
    return pl.pallas_call(
        body, out_shape=out_shape,
        grid_spec=pltpu.PrefetchScalarGridSpec(num_scalar_prefetch=1, grid=(N_CHIP,), in_specs=in_specs,
                                               out_specs=out_specs, scratch_shapes=scratch),
        compiler_params=_cp(("arbitrary",), VMEM_LIMIT), name=name)(c_idx, *grads4, *grads_b, *smalls3, *smalls3)


_HBM = pl.BlockSpec(memory_space=pltpu.HBM)
_SEM = pl.BlockSpec(memory_space=pltpu.SEMAPHORE)


def _split_copies(ins, lands, ng, send, recv, arriving):
    x, y, c, chips = _coords()
    b = 2 * x + y
    copies = []
    for i in range(len(ins)):
        for k in range(3):
            blk = 2 * chips[k][0] + chips[k][1]
            src, dst, got = (ins[i].at[blk], lands[i].at[k], lands[i].at[k]) if i < ng else (ins[i], lands[i].at[b], lands[i].at[blk])
            sems = dict(send_sem=send.at[3 * i + k], recv_sem=recv.at[3 * i + k], device_id=(*chips[k], c), device_id_type=MESH)
            if arriving:
                copies.append(pltpu.make_async_remote_copy(src_ref=got, dst_ref=got, **sems))
            else:
                copies.append(pltpu.make_async_remote_copy(src_ref=src, dst_ref=dst, **sems))
    return copies


def _exchange_begin(sums_b, smalls, name):
    ng, n = len(sums_b), len(sums_b) + len(smalls)
    srcs = list(sums_b) + list(smalls)
    lands = [lax.empty((3,) + g.shape[1:], g.dtype) for g in sums_b] + [lax.empty((N_CHIP,) + s.shape, s.dtype) for s in smalls]

    def body(*refs):
        ins, land_refs = refs[:n], refs[n:2 * n]
        send, recv = refs[2 * n], refs[2 * n + 1]
        token = refs[4 * n + 2]
        for cp in _split_copies(ins, land_refs, ng, send, recv, False):
            cp.start()
        token[...] = jnp.zeros_like(token)

    hbm = lambda a: pltpu.HBM(a.shape, a.dtype)
    outs = pl.pallas_call(
        body, name=name,
        out_shape=(pltpu.SemaphoreType.DMA((3 * n,)), pltpu.SemaphoreType.DMA((3 * n,)), *[hbm(a) for a in srcs + lands],
                   jax.ShapeDtypeStruct((8, 128), F32)),
        in_specs=[_HBM] * (2 * n), out_specs=(_SEM, _SEM, *[_HBM] * (2 * n), pl.BlockSpec(memory_space=pltpu.VMEM)),
        input_output_aliases={i: 2 + i for i in range(2 * n)},
        compiler_params=pltpu.CompilerParams(has_side_effects=pltpu.SideEffectType.DATAFLOW_SIDE_EFFECTING),
    )(*[pltpu.with_memory_space_constraint(a, pltpu.HBM) for a in srcs + lands])
    return outs[0], outs[1], list(outs[2:2 + n]), list(outs[2 + n:2 + 2 * n]), outs[2 + 2 * n]


def _exchange_end(send, recv, srcs, lands, ng, after, name):
    n = len(srcs)
    after = list(after)

    def body(*refs):
        ins, land_refs = refs[:n], refs[n:2 * n]
        send_ref, recv_ref = refs[2 * n], refs[2 * n + 1]
        for cp in _split_copies(ins, land_refs, ng, send_ref, recv_ref, False):
            cp.wait_send()
        for cp in _split_copies(ins, land_refs, ng, send_ref, recv_ref, True):
            cp.wait_recv()

    hbm = lambda a: pltpu.HBM(a.shape, a.dtype)
    outs = pl.pallas_call(
        body, name=name, out_shape=tuple(hbm(a) for a in list(srcs) + list(lands)),
        in_specs=[_HBM] * (2 * n) + [_SEM, _SEM] + [ANY] * len(after), out_specs=tuple([_HBM] * (2 * n)),
        input_output_aliases={i: i for i in range(2 * n)},
        compiler_params=pltpu.CompilerParams(has_side_effects=pltpu.SideEffectType.DATAFLOW_SIDE_EFFECTING),
    )(*srcs, *lands, send, recv, *after)
    return list(outs[n:])


def _chip_sum(bc_idx, sums, recvd, smalls_slots, smalls_own, name, steps=4):
    ng, ns = len(sums), len(smalls_slots)

    def body(bc_ref, *refs):
        own, rx = refs[:ng], refs[ng:2 * ng]
        sl = refs[2 * ng:2 * ng + ns]
        sl_own = refs[2 * ng + ns:2 * ng + 2 * ns]
        o = refs[2 * ng + 2 * ns:]
        for i in range(ng):
            tot = own[i][...]
            for j in range(3):
                tot = tot + rx[i][j].astype(F32)
            o[i][...] = tot
        for i in range(ns):
            term = [jnp.where(bc_ref[0] == kk, sl_own[i][...], sl[i][kk]) for kk in range(N_CHIP)]
            o[ng + i][...] = ((term[0] + term[1]) + term[2]) + term[3]

    def rows(g):
        return g.shape[1] // steps

    in_specs = [pl.BlockSpec((None, rows(g), g.shape[2]), lambda r, bc: (bc[0], r, 0)) for g in sums]
    in_specs += [pl.BlockSpec((3, rows(g), g.shape[2]), lambda r, bc: (0, r, 0)) for g in sums]
    in_specs += [pl.BlockSpec(s.shape, lambda r, bc: (0, 0, 0)) for s in smalls_slots]
    in_specs += [pl.BlockSpec(s.shape[1:], lambda r, bc: (0, 0)) for s in smalls_slots]
    out_specs = [pl.BlockSpec((rows(g), g.shape[2]), lambda r, bc: (bc[1] * steps + r, 0)) for g in sums]
    out_specs += [pl.BlockSpec(s.shape[1:], lambda r, bc: (bc[1], 0)) for s in smalls_slots]
    out_shape = [jax.ShapeDtypeStruct((2 * g.shape[1], g.shape[2]), F32) for g in sums]
    out_shape += [jax.ShapeDtypeStruct((2 * s.shape[1], s.shape[2]), F32) for s in smalls_slots]
    return pl.pallas_call(
        body, out_shape=out_shape,
        grid_spec=pltpu.PrefetchScalarGridSpec(num_scalar_prefetch=1, grid=(steps,), in_specs=in_specs, out_specs=out_specs),
        compiler_params=_cp(("arbitrary",), VMEM_LIMIT), name=name)(bc_idx, *sums, *recvd, *smalls_slots, *smalls_own)


def _pair_gather(arrs, name):
    n = len(arrs)

    def body(*refs):
        outs = refs[n:2 * n]
        send, recv = refs[2 * n:]
        x, y, c, _ = _coords()
        cps = []
        for i in range(n):
            hr = outs[i].shape[0] // 2
            mine = outs[i].at[pl.ds(c * hr, hr)]
            cps.append(pltpu.make_async_remote_copy(
                src_ref=mine, dst_ref=mine, send_sem=send.at[i], recv_sem=recv.at[i],
                device_id=(x, y, 1 - c), device_id_type=MESH))
        for cp in cps:
            cp.start()
        for i in range(n):
            hr = outs[i].shape[0] // 2
            theirs = outs[i].at[pl.ds((1 - c) * hr, hr)]
            pltpu.make_async_remote_copy(
                src_ref=theirs, dst_ref=theirs, send_sem=send.at[i], recv_sem=recv.at[i],
                device_id=(x, y, 1 - c), device_id_type=MESH).wait_recv()
        for cp in cps:
            cp.wait_send()

    return list(pl.pallas_call(
        body, out_shape=[jax.ShapeDtypeStruct(a.shape, a.dtype) for a in arrs], in_specs=[ANY] * n, out_specs=[ANY] * n,
        input_output_aliases={i: i for i in range(n)},
        scratch_shapes=[pltpu.SemaphoreType.DMA((n,)), pltpu.SemaphoreType.DMA((n,))],
        name=name)(*arrs))


def _adamw_math(w, g, m, v):
    m2 = ADAM_B1 * m + (1.0 - ADAM_B1) * g
    v2 = ADAM_B2 * v + (1.0 - ADAM_B2) * (g * g)
    m_hat = m2 / (1.0 - ADAM_B1 ** ADAM_STEP)
    v_hat = v2 / (1.0 - ADAM_B2 ** ADAM_STEP)
    delta = -ADAM_LR * (m_hat / (jnp.sqrt(v_hat) + ADAM_EPS) + ADAM_WD * w)
    return delta, m2, v2


def _adamw_big(ws, gs, ms, vs, name, steps=8):
    n = len(ws)

    def body(*refs):
        for i in range(n):
            w_ref, g_ref, m_ref, v_ref = (refs[k * n + i] for k in range(4))
            d_ref, m2_ref, v2_ref, g2_ref = (refs[(4 + k) * n + i] for k in range(4))
            gv = g_ref[...]
            d_ref[...], m2_ref[...], v2_ref[...] = _adamw_math(w_ref[...], gv, m_ref[...], v_ref[...])
            g2_ref[...] = gv

    specs = [pl.BlockSpec((w.shape[0] // steps, w.shape[1]), lambda i: (i, 0)) for w in ws]
    shapes = [jax.ShapeDtypeStruct(w.shape, F32) for w in ws]
    outs = pl.pallas_call(
        body, grid=(steps,), in_specs=specs * 4, out_specs=specs * 4, out_shape=shapes * 4,
        compiler_params=_cp(("parallel",), VMEM_LIMIT), name=name)(*ws, *gs, *ms, *vs)
    return [tuple(outs[k * n + i] for k in range(4)) for i in range(n)]


def _adamw_small(groups):
    n = len(groups)

    def body(*refs):
        for i in range(n):
            w_ref, g_ref, m_ref, v_ref = refs[4 * i:4 * i + 4]
            d_ref, m2_ref, v2_ref = refs[4 * n + 3 * i:4 * n + 3 * i + 3]
            d_ref[...], m2_ref[...], v2_ref[...] = _adamw_math(w_ref[...], g_ref[...], m_ref[...], v_ref[...])

    flat = [a for grp in groups for a in grp]
    out_shape = [jax.ShapeDtypeStruct(grp[0].shape, F32) for grp in groups for _ in range(3)]
    outs = pl.pallas_call(body, out_shape=out_shape, name="adamw_small")(*flat)
    return [tuple(outs[3 * i:3 * i + 3]) for i in range(n)]


def kernel(x, mem, norm_mix_g, w_in, conv_w, gm_ln_g, gm_ln_b, gm_ws, gm_bs, w_out, norm_x_g, norm_mem_g, w_q, w_kv, w_xo, norm_final_g, loss_target, m_norm_mix_g, m_w_in, m_conv_w, m_gm_ln_g, m_gm_ln_b, m_gm_ws, m_gm_bs, m_w_out, m_norm_x_g, m_norm_mem_g, m_w_q, m_w_kv, m_w_xo, m_norm_final_g, v_norm_mix_g, v_w_in, v_conv_w, v_gm_ln_g, v_gm_ln_b, v_gm_ws, v_gm_bs, v_w_out, v_norm_x_g, v_norm_mem_g, v_w_q, v_w_kv, v_w_xo, v_norm_final_g):
    t = x.shape[1]
    xi = lax.axis_index("x")
    yi = lax.axis_index("y")
    ci = lax.axis_index("c")
    b_idx = jnp.reshape(2 * xi + yi, (1,)).astype(jnp.int32)
    c_idx = jnp.reshape(ci, (1,)).astype(jnp.int32)

    x2d, mem2d, tgt = x[0], mem[0], loss_target[0]
    big = [w_in[0], w_out[0], w_q[0], w_kv[0], w_xo[0]]
    big_m = [m_w_in[0], m_w_out[0], m_w_q[0], m_w_kv[0], m_w_xo[0]]
    big_v = [v_w_in[0], v_w_out[0], v_w_q[0], v_w_kv[0], v_w_xo[0]]
    g3 = norm_final_g.reshape(1, D)

    def pad8(a):
        return jnp.pad(a, ((0, 8 - a.shape[0]), (0, 0)))

    own_blocks = _cast_shards(b_idx, big)

    tril = jnp.tril(jnp.ones((CH, CH), bool))
    wc32 = jnp.where(tril[None], gm_ws[0], 0.0)
    wc = wc32.astype(BF16)
    wct = jnp.swapaxes(wc32, 1, 2).astype(BF16)
    bsb = jnp.broadcast_to(gm_bs[0][:, :, None], (HEADS, CH, CH))

    blk = 2 * xi + yi
    order = jnp.stack([blk, blk ^ 2, blk ^ 1, blk ^ 3]).astype(jnp.int32)
    proj, hb, win_f, cw8, (wout_f,) = _proj_gather(
        order, x2d, norm_mix_g, own_blocks[0], pad8(conv_w[0]), [own_blocks[1]])
    mixin, (wq_f, wkv_f, wxo_f) = _mixer_fwd(proj, cw8, gm_ln_g, gm_ln_b, wc, bsb, own_blocks[2:])
    wout2, wq2, wxo2 = wout_f.reshape(MIX, D), wq_f.reshape(D, D), wxo_f.reshape(D, D)
    k, v = _mem_fwd(mem2d, norm_mem_g, wkv_f)

    (loss_tile, dmix, dx1b, h2b, dq, ob, dx2b, dk, dv, dg2, dg3) = _tail(
        x2d, tgt, mixin, wout2, wq2, wxo2, k, v, norm_x_g, g3)
    dwkv, dwkv_b, dgm = _mem_bwd(mem2d, norm_mem_g, dk, dv, wkv_f)
    dproj, dcw, dlng, dlnb, dwc, dbs8, grad_x, dg1 = _mixer_bwd(
        proj, dmix, cw8, gm_ln_g, gm_ln_b, wc, wct, bsb, win_f, x2d, dx1b, norm_mix_g)

    bc_idx = jnp.concatenate([b_idx, c_idx])
    dwin, dwin_b = _grad_matmul(hb, dproj, dgm, by_cols=True, name="grad_w_in", tk=2048)
    zero = jnp.zeros((1, D), F32)
    loss_row = jnp.broadcast_to(loss_tile[0:1, 0:1], (1, D))
    sv = jnp.concatenate([dg1[0:1], dg2, dgm, dg3, dlng, dlnb, dbs8[0:1], loss_row, dcw], axis=0)
    sw = dwc.reshape(HEADS * CH, CH)
    ps_b = _pair_reduce(c_idx, [dwin], [dwin_b], [sv, sw], "pair_reduce_b")
    sums_b, sums_b_b, psmall = list(ps_b[:1]), list(ps_b[1:2]), list(ps_b[2:])
    send_b, recv_b, src_b, land_b, token_b = _exchange_begin(sums_b_b, psmall, "exchange_b_begin")

    dwxo, dwxo_b = _grad_matmul(ob, dx2b, token_b, by_cols=False, name="grad_w_xo", tk=2048)
    dwq, dwq_b = _grad_matmul(h2b, dq, token_b, by_cols=False, name="grad_w_q", tk=2048)
    dwout, dwout_b = _grad_matmul(mixin, dx1b, token_b, by_cols=False, name="grad_w_out")
    ps_a = _pair_reduce(c_idx, [dwout, dwq, dwkv, dwxo], [dwout_b, dwq_b, dwkv_b, dwxo_b], [], "pair_reduce_a")
    sums_a, sums_a_b = list(ps_a[:4]), list(ps_a[4:8])
    send_a, recv_a, src_a, land_a, token_a = _exchange_begin(sums_a_b, [], "exchange_a_begin")

    rx2b = _exchange_end(send_b, recv_b, src_b, land_b, 1, [token_a], "exchange_b_end")
    red_b = _chip_sum(bc_idx, sums_b, rx2b[:1], rx2b[1:], psmall, "chip_sum_b")
    gwin, svf, swf = _pair_gather(red_b, "pair_gather_b")
    out_b = _adamw_big(big[:1], [gwin], big_m[:1], big_v[:1], "adamw_w_in")[0]

    def vec_pack(a1, a2, am, a3, lg, lb, bs):
        return jnp.concatenate([a1, a2, am, a3.reshape(1, D), lg, lb, bs.reshape(1, D), zero], axis=0)

    wv = vec_pack(norm_mix_g, norm_x_g, norm_mem_g, norm_final_g, gm_ln_g, gm_ln_b, gm_bs)
    mv = vec_pack(m_norm_mix_g, m_norm_x_g, m_norm_mem_g, m_norm_final_g, m_gm_ln_g, m_gm_ln_b, m_gm_bs)
    vv = vec_pack(v_norm_mix_g, v_norm_x_g, v_norm_mem_g, v_norm_final_g, v_gm_ln_g, v_gm_ln_b, v_gm_bs)
    loss = svf[7, 0]
    gcw = lax.dynamic_slice_in_dim(svf[8:16], blk * (D // N_CHIP), D // N_CHIP, axis=1)
    gws = swf
    gv = svf[0:8]
    (dv_, mv_, vv_), (dc_, mc_, vc_), (dws_, mws_, vws_) = _adamw_small([
        (wv, gv, mv, vv),
        (pad8(conv_w[0]), gcw, pad8(m_conv_w[0]), pad8(v_conv_w[0])),
        (gm_ws.reshape(HEADS * CH, CH), gws, m_gm_ws.reshape(HEADS * CH, CH), v_gm_ws.reshape(HEADS * CH, CH))])

    rx2a = _exchange_end(send_a, recv_a, src_a, land_a, 4, [out_b[0], dv_], "exchange_a_end")
    red_a = _chip_sum(bc_idx, sums_a, rx2a, [], [], "chip_sum_a")
    g_a = _pair_gather(red_a, "pair_gather_a")
    big_out = [out_b] + _adamw_big(big[1:], g_a, big_m[1:], big_v[1:], "adamw_rest")

    def unpack(vecs, cw, ws, bigs):
        r = lambda i: vecs[i:i + 1]
        return [r(0), bigs[0][None], cw[0:3][None], r(4), r(5), ws.reshape(1, HEADS, CH, CH), vecs[6].reshape(1, HEADS, CH),
                bigs[1][None], r(1), r(2), bigs[2][None], bigs[3][None], bigs[4][None], vecs[3]]

    grads_out = unpack(gv, gcw, gws, [o[3] for o in big_out])
    delta_out = unpack(dv_, dc_, dws_, [o[0] for o in big_out])
    m_out = unpack(mv_, mc_, mws_, [o[1] for o in big_out])
    v_out = unpack(vv_, vc_, vws_, [o[2] for o in big_out])
    return (loss, grad_x[None], *grads_out, *delta_out, *m_out, *v_out)
```

```python
import functools
import math

import jax
import jax.numpy as jnp
from jax import lax
from jax.experimental import pallas as pl
from jax.experimental.pallas import tpu as pltpu

F32 = jnp.float32
BF16 = jnp.bfloat16
MESH = pl.DeviceIdType.MESH

D = 1024
SLAB = 1024
N_SLAB = 7
IN_DIM = N_SLAB * SLAB
MIX = 2 * SLAB
HEADS = 8
CH = 128
XH = 4
XD = D // XH
EPS = 1e-6
GELU_C = math.sqrt(2.0 / math.pi)
GELU_A = 0.044715
N_CHIP = 4
IN_BLK = IN_DIM // N_CHIP
KV_BLK = 2 * D // N_CHIP

ADAM_LR, ADAM_B1, ADAM_B2, ADAM_EPS, ADAM_WD, ADAM_STEP = 0.001, 0.9, 0.999, 1e-08, 0.01, 10

VMEM_LIMIT = 60 * 1024 * 1024


def _cp(sem=None, vmem=None):
    return pltpu.CompilerParams(dimension_semantics=sem, vmem_limit_bytes=vmem)


def _full(shape, buffers=None):
    n = len(shape)
    if buffers is None:
        return pl.BlockSpec(shape, lambda *_: (0,) * n)
    return pl.BlockSpec(shape, lambda *_: (0,) * n, pipeline_mode=pl.Buffered(buffers))


ANY = pl.BlockSpec(memory_space=pl.ANY)


def _bdot(a, b):
    return jnp.dot(a.astype(BF16), b.astype(BF16), preferred_element_type=F32)


def _bdot_nt(a, b):
    return lax.dot_general(a.astype(BF16), b.astype(BF16), (((1,), (1,)), ((), ())), preferred_element_type=F32)


def _bdot_tn(a, b):
    return lax.dot_general(a.astype(BF16), b.astype(BF16), (((0,), (0,)), ((), ())), preferred_element_type=F32)


def _rms(x, g):
    r = lax.rsqrt(jnp.mean(x * x, axis=-1, keepdims=True) + EPS)
    return x * r * g, r


def _rms_bwd(dy, x, r, g):
    gdy = dy * g
    dx = r * gdy - x * (r * r * r) * jnp.mean(x * gdy, axis=-1, keepdims=True)
    dg = jnp.sum(dy * x * r, axis=0, keepdims=True)
    return dx, dg


def _gelu_parts(x):
    x2 = x * x
    t = jnp.tanh(GELU_C * (x + GELU_A * x * x2))
    val = 0.5 * x * (1.0 + t)
    grad = 0.5 * (1.0 + t) + 0.5 * x * (1.0 - t * t) * (GELU_C * (1.0 + 3.0 * GELU_A * x2))
    return val, grad


def _gelu(x):
    return 0.5 * x * (1.0 + jnp.tanh(GELU_C * (x + GELU_A * x * x * x)))


def _sigmoid(z):
    return 1.0 / (1.0 + jnp.exp(-z))


def _cast_shards(b_idx, arrs):
    n = len(arrs)
    steps = 8

    def body(b_ref, *refs):
        for i in range(n):
            refs[n + i][...] = refs[i][...].astype(BF16)

    in_specs = [pl.BlockSpec((a.shape[0] // steps, a.shape[1]), lambda i, b: (i, 0)) for a in arrs]
    out_specs = [pl.BlockSpec((None, a.shape[0] // steps, a.shape[1]), lambda i, b: (b[0], i, 0)) for a in arrs]
    return pl.pallas_call(
        body, out_shape=[jax.ShapeDtypeStruct((N_CHIP,) + a.shape, BF16) for a in arrs],
        grid_spec=pltpu.PrefetchScalarGridSpec(num_scalar_prefetch=1, grid=(steps,), in_specs=in_specs, out_specs=out_specs),
        compiler_params=_cp(("arbitrary",)), name="cast_shards")(b_idx, *arrs)


def _proj_gather(order, x, g, win_own, cw8s, more, tm=1024):
    t = x.shape[0]
    ni = t // tm
    nm = len(more)

    def body(*refs):
        order_ref, x_ref, g_ref, win_in, cw_in = refs[:5]
        o_ref, hb_ref, win_f, cw_out = refs[5 + nm:9 + nm]
        more_out = refs[9 + nm:9 + 2 * nm]
        hbuf, wv, cw_s, cw_r, loc = refs[9 + 2 * nm:14 + 2 * nm]
        g_in = _Gather([win_f], *refs[14 + 2 * nm:18 + 2 * nm])
        g_more = _Gather(more_out, *refs[18 + 2 * nm:22 + 2 * nm])
        j, i = pl.program_id(0), pl.program_id(1)
        x, y, c, chips = _coords()
        b = 2 * x + y
        blks = [2 * chip[0] + chip[1] for chip in chips]

        def cw_cols(blk):
            return cw_out.at[:, pl.ds(blk * (D // N_CHIP), D // N_CHIP)]

        def cw_copy(k, blk):
            src = cw_in if blk is None else cw_cols(blk)
            return pltpu.make_async_remote_copy(src_ref=src, dst_ref=cw_cols(b if blk is None else blk), send_sem=cw_s.at[k],
                                                recv_sem=cw_r.at[k], device_id=(*chips[k], c), device_id_type=MESH)

        cw_local = pltpu.make_async_copy(cw_in, cw_cols(b), loc.at[1])

        def load(blk, slot):
            return pltpu.make_async_copy(win_f.at[blk], wv.at[slot], loc.at[2 + slot])

        @pl.when((j == 0) & (i == 0))
        def _():
            g_in.start()
            cw_local.start()
            for k in range(3):
                cw_copy(k, None).start()
            load(b, 0).start()
            load(b, 0).wait()

        @pl.when((j == 1) & (i == 0))
        def _():
            g_in.hop()
            g_more.start()
            g_in.near_ready()
            load(g_in.bx, 1).start()
            load(g_in.by, 0).start()
            load(g_in.bx, 1).wait()

        @pl.when((j == 2) & (i == 0))
        def _():
            load(g_in.by, 0).wait()
            g_in.far()
            g_in.far_ready()
            load(g_in.bd, 1).start()

        @pl.when((j == 3) & (i == 0))
        def _():
            load(g_in.bd, 1).wait()
            g_more.hop()

        rows = pl.ds(pl.multiple_of(i * tm, tm), tm)

        @pl.when(j == 0)
        def _():
            h, _ = _rms(x_ref[...], g_ref[...])
            hbuf[rows, :] = h.astype(BF16)
            hb_ref[...] = h.astype(BF16)

        @pl.when((j == N_CHIP - 1) & (i == ni - 1))
        def _():
            g_more.far()

        o_ref[...] = jnp.dot(hbuf[rows, :], wv[lax.rem(j, 2)], preferred_element_type=F32).astype(BF16)

        @pl.when((j == N_CHIP - 1) & (i == ni - 1))
        def _():
            for k in range(3):
                cw_copy(k, blks[k]).wait_recv()
            for k in range(3):
                cw_copy(k, None).wait_send()
            cw_local.wait()
            g_more.near_ready()
            g_more.far_ready()
            g_in.drain()
            g_more.drain()

    first = lambda j, i: jnp.where(j == 0, i, ni - 1)
    in_specs = [pl.BlockSpec((tm, D), lambda j, i, o: (first(j, i), 0)), pl.BlockSpec((1, D), lambda j, i, o: (0, 0)),
                ANY, ANY] + [ANY] * nm
    out_specs = [pl.BlockSpec((tm, IN_BLK), lambda j, i, o: (i, o[j])),
                 pl.BlockSpec((tm, D), lambda j, i, o: (first(j, i), 0)), ANY, ANY] + [ANY] * nm
    outs = pl.pallas_call(
        body, out_shape=[jax.ShapeDtypeStruct((t, IN_DIM), BF16), jax.ShapeDtypeStruct((t, D), BF16),
                         jax.ShapeDtypeStruct(win_own.shape, BF16), jax.ShapeDtypeStruct((8, D), F32)]
        + [jax.ShapeDtypeStruct(f.shape, f.dtype) for f in more],
        grid_spec=pltpu.PrefetchScalarGridSpec(
            num_scalar_prefetch=1, grid=(N_CHIP, ni), in_specs=in_specs, out_specs=out_specs,
            scratch_shapes=[pltpu.VMEM((t, D), BF16), pltpu.VMEM((2, D, IN_BLK), BF16)]
            + [pltpu.SemaphoreType.DMA((3,))] * 2 + [pltpu.SemaphoreType.DMA((4,))] + _gather_sems(1) + _gather_sems(nm)),
        input_output_aliases={3: 2, **{5 + w: 4 + w for w in range(nm)}},
        compiler_params=_cp(("arbitrary", "arbitrary"), VMEM_LIMIT), name="proj_gather")(order, x, g, win_own, cw8s, *more)
    return outs[0], outs[1], outs[2], outs[3], outs[4:]


class _Gather:
    def __init__(self, outs, ici_s, ici_r, d2d_s, d2d_r):
        x, y, c, _ = _coords()
        self.outs, self.c = outs, c
        self.sems = ici_s, ici_r, d2d_s, d2d_r
        self.b, self.bx, self.by, self.bd = 2 * x + y, 2 * (1 - x) + y, 2 * x + (1 - y), 2 * (1 - x) + (1 - y)
        self.xn, self.yn, self.sib = (1 - x, y, c), (x, 1 - y, c), (x, y, 1 - c)

    def piece(self, w, blk, hc, quarter=None):
        hr = self.outs[w].shape[1] // 2
        if quarter is None:
            return self.outs[w].at[blk, pl.ds(hc * hr, hr)]
        return self.outs[w].at[blk, pl.ds(hc * hr + quarter * (hr // 2), hr // 2)]

    def ici(self, w, k, ref, to):
        return pltpu.make_async_remote_copy(src_ref=ref, dst_ref=ref, send_sem=self.sems[0].at[w, k],
                                            recv_sem=self.sems[1].at[w, k], device_id=to, device_id_type=MESH)

    def d2d(self, w, k, ref):
        return pltpu.make_async_remote_copy(src_ref=ref, dst_ref=ref, send_sem=self.sems[2].at[w, k],
                                            recv_sem=self.sems[3].at[w, k], device_id=self.sib, device_id_type=MESH)

    def start(self):
        for w in range(len(self.outs)):
            mine = self.piece(w, self.b, self.c)
            self.ici(w, 0, mine, self.xn).start()
            self.ici(w, 1, mine, self.yn).start()

    def hop(self):
        c = self.c
        for w in range(len(self.outs)):
            self.ici(w, 0, self.piece(w, self.bx, c), self.xn).wait_recv()
            self.ici(w, 1, self.piece(w, self.by, c), self.yn).wait_recv()
            self.ici(w, 2, self.piece(w, self.bx, c, 0), self.yn).start()
            self.ici(w, 3, self.piece(w, self.by, c, 1), self.xn).start()
            self.d2d(w, 0, self.piece(w, self.bx, c)).start()
            self.d2d(w, 1, self.piece(w, self.by, c)).start()

    def near_ready(self):
        for w in range(len(self.outs)):
            self.d2d(w, 0, self.piece(w, self.bx, 1 - self.c)).wait_recv()
            self.d2d(w, 1, self.piece(w, self.by, 1 - self.c)).wait_recv()

    def far(self):
        c = self.c
        for w in range(len(self.outs)):
            self.ici(w, 2, self.piece(w, self.bd, c, 0), self.yn).wait_recv()
            self.ici(w, 3, self.piece(w, self.bd, c, 1), self.xn).wait_recv()
            self.d2d(w, 2, self.piece(w, self.bd, c, 0)).start()
            self.d2d(w, 3, self.piece(w, self.bd, c, 1)).start()

    def far_ready(self):
        for w in range(len(self.outs)):
            self.d2d(w, 2, self.piece(w, self.bd, 1 - self.c, 0)).wait_recv()
            self.d2d(w, 3, self.piece(w, self.bd, 1 - self.c, 1)).wait_recv()

    def drain(self):
        c = self.c
        for w in range(len(self.outs)):
            mine = self.piece(w, self.b, c)
            self.ici(w, 0, mine, self.xn).wait_send()
            self.ici(w, 1, mine, self.yn).wait_send()
            self.ici(w, 2, self.piece(w, self.bx, c, 0), self.yn).wait_send()
            self.ici(w, 3, self.piece(w, self.by, c, 1), self.xn).wait_send()
            self.d2d(w, 0, self.piece(w, self.bx, c)).wait_send()
            self.d2d(w, 1, self.piece(w, self.by, c)).wait_send()
            self.d2d(w, 2, self.piece(w, self.bd, c, 0)).wait_send()
            self.d2d(w, 3, self.piece(w, self.bd, c, 1)).wait_send()


def _gather_sems(nw):
    return [pltpu.SemaphoreType.DMA((max(nw, 1), 4))] * 4


def _mixer_fwd(proj, cw8, lng, lnb, wc, bsb, fulls, tm=256):
    t = proj.shape[0]
    nt = t // tm
    nch = tm // CH
    nw = len(fulls)

    def body(*refs):
        p_ref, cw_ref, lng_ref, lnb_ref, wc_ref, bsb_ref = refs[:6]
        mix_ref = refs[6 + nw]
        w_outs = refs[7 + nw:7 + 2 * nw]
        prev_ref = refs[7 + 2 * nw]
        gather = _Gather(w_outs, *refs[8 + 2 * nw:])

        @pl.when(pl.program_id(0) == 0)
        def _():
            gather.start()
            prev_ref[...] = jnp.zeros_like(prev_ref)

        @pl.when(pl.program_id(0) == nt // 2)
        def _():
            gather.hop()

        @pl.when(pl.program_id(0) == nt - 1)
        def _():
            gather.far()

        rows = lax.broadcasted_iota(jnp.int32, (tm, CH), 0)
        for s in range(HEADS):
            cs = pl.ds(CH * s, CH)

            def slab(k):
                return p_ref[:, pl.ds(k * SLAB + CH * s, CH)].astype(F32)

            gb, gc, xa, za = slab(0), slab(1), slab(2), slab(3)
            cx = gc * xa
            p6 = jnp.broadcast_to(prev_ref[6:7, cs], (tm, CH))
            p7 = jnp.broadcast_to(prev_ref[7:8, cs], (tm, CH))
            c1 = jnp.where(rows == 0, p7, pltpu.roll(cx, 1, 0))
            c2 = jnp.where(rows == 0, p6, jnp.where(rows == 1, p7, pltpu.roll(cx, 2, 0)))
            prev_ref[:, cs] = cx[tm - 8:, :]
            cv = cw_ref[0:1, cs] * c2 + cw_ref[1:2, cs] * c1 + cw_ref[2:3, cs] * cx
            mix_ref[:, cs] = (gb * cv * (za * _sigmoid(za))).astype(BF16)

            u, v, zb = slab(4), slab(5), slab(6)
            ug, vg = _gelu(u), _gelu(v)
            dlt = vg - jnp.mean(vg, axis=-1, keepdims=True)
            vhat = dlt * lax.rsqrt(jnp.mean(dlt * dlt, axis=-1, keepdims=True) + EPS)
            vn = (vhat * lng_ref[:, cs] + lnb_ref[:, cs]).astype(BF16)
            gate = ug * (zb * _sigmoid(zb))
            for c in range(nch):
                rs = slice(CH * c, CH * (c + 1))
                sp = jnp.dot(wc_ref[s], vn[rs], preferred_element_type=F32) + bsb_ref[s]
                mix_ref[rs, pl.ds(SLAB + CH * s, CH)] = (gate[rs] * sp).astype(BF16)

        @pl.when(pl.program_id(0) == nt - 1)
        def _():
            gather.near_ready()
            gather.far_ready()
            gather.drain()

    sems = _gather_sems(nw)
    outs = pl.pallas_call(
        body, grid=(nt,),
        in_specs=[pl.BlockSpec((tm, IN_DIM), lambda i: (i, 0)), _full((8, D)), _full((1, D)), _full((1, D)),
                  _full((HEADS, CH, CH)), _full((HEADS, CH, CH))] + [ANY] * nw,
        out_specs=[pl.BlockSpec((tm, MIX), lambda i: (i, 0))] + [ANY] * nw,
        out_shape=[jax.ShapeDtypeStruct((t, MIX), BF16)] + [jax.ShapeDtypeStruct(f.shape, f.dtype) for f in fulls],
        input_output_aliases={6 + w: 1 + w for w in range(nw)},
        scratch_shapes=[pltpu.VMEM((8, D), F32)] + sems,
        compiler_params=_cp(("arbitrary",), VMEM_LIMIT), name="mixer_fwd")(proj, cw8, lng, lnb, wc, bsb, *fulls)
    return outs[0], outs[1:]


def _mem_fwd(mem, gm, wkv_f):
    n_mem = mem.shape[0]

    def body(mem_ref, gm_ref, w_ref, k_ref, v_ref):
        m, _ = _rms(mem_ref[...], gm_ref[...])
        mb = m.astype(BF16)
        for j in range(N_CHIP):
            dst = k_ref if j < 2 else v_ref
            dst[:, pl.ds(KV_BLK * (j % 2), KV_BLK)] = jnp.dot(mb, w_ref[j], preferred_element_type=F32).astype(BF16)

    return pl.pallas_call(
        body, out_shape=[jax.ShapeDtypeStruct((n_mem, D), BF16), jax.ShapeDtypeStruct((n_mem, D), BF16)],
        compiler_params=_cp(None, VMEM_LIMIT), name="mem_fwd")(mem, gm, wkv_f)


def _tail(x, tgt, mixin, wout, wq, wxo, k, v, g2, g3, tm=512, sub=512):
    t = x.shape[0]
    n_mem = k.shape[0]
    scale = 1.0 / math.sqrt(XD)

    def body(x_ref, tgt_ref, mix_ref, wout_ref, wq_ref, wxo_ref, k_ref, v_ref, g2_ref, g3_ref,
             loss_ref, dmix_ref, dx1b_ref, h2_ref, dq_ref, o_ref, dx2b_ref, dk_ref, dv_ref, dg2_ref, dg3_ref):
        @pl.when(pl.program_id(0) == 0)
        def _():
            loss_ref[...] = jnp.zeros_like(loss_ref)
            dk_ref[...] = jnp.zeros_like(dk_ref)
            dv_ref[...] = jnp.zeros_like(dv_ref)
            dg2_ref[...] = jnp.zeros_like(dg2_ref)
            dg3_ref[...] = jnp.zeros_like(dg3_ref)

        g2, g3 = g2_ref[...], g3_ref[...]
        for sb in range(tm // sub):
            rs = pl.ds(sub * sb, sub)
            x1 = x_ref[rs, :] + jnp.dot(mix_ref[rs, :], wout_ref[...], preferred_element_type=F32)
            h2, r2 = _rms(x1, g2)
            h2b = h2.astype(BF16)
            h2_ref[rs, :] = h2b
            q = jnp.dot(h2b, wq_ref[...], preferred_element_type=F32).astype(BF16)
            probs, outs = [], []
            for hd in range(XH):
                hs = pl.ds(XD * hd, XD)
                s = _bdot_nt(q[:, XD * hd:XD * (hd + 1)], k_ref[:, hs]) * scale
                e = jnp.exp(s - jnp.max(s, axis=-1, keepdims=True))
                p = e / jnp.sum(e, axis=-1, keepdims=True)
                probs.append(p)
                outs.append(_bdot(p, v_ref[:, hs]))
            ob = jnp.concatenate(outs, axis=-1).astype(BF16)
            o_ref[rs, :] = ob
            x2 = x1 + jnp.dot(ob, wxo_ref[...], preferred_element_type=F32)
            y, r3 = _rms(x2, g3)
            diff = y - tgt_ref[rs, :]
            row_loss = jnp.sum(diff * diff, axis=-1, keepdims=True)
            loss_ref[...] += jnp.broadcast_to(jnp.sum(row_loss, axis=0, keepdims=True) * (0.5 / D), loss_ref.shape)

            dx2, dg3 = _rms_bwd(diff * (1.0 / D), x2, r3, g3)
            dg3_ref[...] += dg3
            dx2b = dx2.astype(BF16)
            dx2b_ref[rs, :] = dx2b
            do = _bdot_nt(dx2b, wxo_ref[...])
            dqs = []
            for hd in range(XH):
                hs = pl.ds(XD * hd, XD)
                p = probs[hd]
                do_h = do[:, XD * hd:XD * (hd + 1)]
                dv_ref[:, hs] += _bdot_tn(p, do_h)
                dp = _bdot_nt(do_h, v_ref[:, hs])
                ds = p * (dp - jnp.sum(dp * p, axis=-1, keepdims=True))
                dqs.append(_bdot(ds, k_ref[:, hs]) * scale)
                dk_ref[:, hs] += _bdot_tn(ds, q[:, XD * hd:XD * (hd + 1)]) * scale
            dq = jnp.concatenate(dqs, axis=-1).astype(BF16)
            dq_ref[rs, :] = dq
            dx1n, dg2 = _rms_bwd(_bdot_nt(dq, wq_ref[...]), x1, r2, g2)
            dg2_ref[...] += dg2
            dx1b = (dx2 + dx1n).astype(BF16)
            dx1b_ref[rs, :] = dx1b
            dmix_ref[rs, :] = _bdot_nt(dx1b, wout_ref[...]).astype(BF16)

    tok = lambda w: pl.BlockSpec((tm, w), lambda i: (i, 0))
    return pl.pallas_call(
        body, grid=(t // tm,),
        in_specs=[tok(D), tok(D), tok(MIX), _full((MIX, D), 1), _full((D, D), 1), _full((D, D), 1),
                  _full((n_mem, D), 1), _full((n_mem, D), 1), _full((1, D)), _full((1, D))],
        out_specs=[_full((8, 128)), tok(MIX), tok(D), tok(D), tok(D), tok(D), tok(D),
                   _full((n_mem, D)), _full((n_mem, D)), _full((1, D)), _full((1, D))],
        out_shape=[jax.ShapeDtypeStruct((8, 128), F32), jax.ShapeDtypeStruct((t, MIX), BF16),
                   jax.ShapeDtypeStruct((t, D), BF16),
                   jax.ShapeDtypeStruct((t, D), BF16), jax.ShapeDtypeStruct((t, D), BF16),
                   jax.ShapeDtypeStruct((t, D), BF16), jax.ShapeDtypeStruct((t, D), BF16),
                   jax.ShapeDtypeStruct((n_mem, D), F32), jax.ShapeDtypeStruct((n_mem, D), F32),
                   jax.ShapeDtypeStruct((1, D), F32), jax.ShapeDtypeStruct((1, D), F32)],
        compiler_params=_cp(("arbitrary",), VMEM_LIMIT), name="tail")(x, tgt, mixin, wout, wq, wxo, k, v, g2, g3)


def _mem_bwd(mem, gm, dk, dv, wkv_f):
    def body(mem_ref, gm_ref, dk_ref, dv_ref, w_ref, dw_ref, dwb_ref, dgm_ref):
        mem_v = mem_ref[...]
        m, rm = _rms(mem_v, gm_ref[...])
        mb = m.astype(BF16)
        dm = jnp.zeros_like(mem_v)
        for j in range(N_CHIP):
            src = dk_ref if j < 2 else dv_ref
            dkv = src[:, pl.ds(KV_BLK * (j % 2), KV_BLK)].astype(BF16)
            dw = _bdot_tn(mb, dkv)
            dw_ref[j] = dw
            dwb_ref[j] = dw.astype(BF16)
            dm = dm + _bdot_nt(dkv, w_ref[j])
        dgm_ref[...] = jnp.sum(dm * mem_v * rm, axis=0, keepdims=True)

    return pl.pallas_call(
        body, out_shape=[jax.ShapeDtypeStruct((N_CHIP, D, KV_BLK), F32), jax.ShapeDtypeStruct((N_CHIP, D, KV_BLK), BF16),
                         jax.ShapeDtypeStruct((1, D), F32)],
        compiler_params=_cp(None, VMEM_LIMIT), name="mem_bwd")(mem, gm, dk, dv, wkv_f)


def _mixer_bwd(proj, dmix, cw8, lng, lnb, wc, wct, bsb, win_f, x, dx1, g1, tm=256):
    t = proj.shape[0]
    nt = t // tm
    nch = tm // CH
    hb = 16
    pair = 2 * CH

    def body(p_ref, pgc_ref, pxa_ref, dm_ref, cw_ref, lng_ref, lnb_ref, wc_ref, wct_ref, bsb_ref, w_ref, x_ref,
             dx1_ref, g1_ref, dp_ref, dcw_ref, dlng_ref, dlnb_ref, dwc_ref, dbs_ref, gx_ref, dg1_ref,
             next_ref, dh_ref):
        i = pl.program_id(0)

        @pl.when(i == 0)
        def _():
            next_ref[...] = jnp.zeros_like(next_ref)
            dcw_ref[...] = jnp.zeros_like(dcw_ref)
            dlng_ref[...] = jnp.zeros_like(dlng_ref)
            dlnb_ref[...] = jnp.zeros_like(dlnb_ref)
            dwc_ref[...] = jnp.zeros_like(dwc_ref)
            dbs_ref[...] = jnp.zeros_like(dbs_ref)
            dg1_ref[...] = jnp.zeros_like(dg1_ref)

        first_tile = i == nt - 1
        rows = lax.broadcasted_iota(jnp.int32, (tm, CH), 0)
        ones8 = jnp.ones((8, CH), BF16)
        for s in range(HEADS):
            cs = pl.ds(CH * s, CH)

            def slab(k):
                return p_ref[:, pl.ds(k * SLAB + CH * s, CH)].astype(F32)

            gb, gc, xa, za = slab(0), slab(1), slab(2), slab(3)
            da = dm_ref[:, cs].astype(F32)
            cx = gc * xa
            cxp = pgc_ref[:, cs].astype(F32) * pxa_ref[:, cs].astype(F32)
            cxp = jnp.where(first_tile, jnp.zeros_like(cxp), cxp)
            p6 = jnp.broadcast_to(cxp[hb - 2:hb - 1, :], (tm, CH))
            p7 = jnp.broadcast_to(cxp[hb - 1:hb, :], (tm, CH))
            c1 = jnp.where(rows == 0, p7, pltpu.roll(cx, 1, 0))
            c2 = jnp.where(rows == 0, p6, jnp.where(rows == 1, p7, pltpu.roll(cx, 2, 0)))
            w0, w1, w2 = cw_ref[0:1, cs], cw_ref[1:2, cs], cw_ref[2:3, cs]
            cv = w0 * c2 + w1 * c1 + w2 * cx
            sg = _sigmoid(za)
            sa = za * sg
            dcv = da * gb * sa
            dp_ref[:, pl.ds(0 * SLAB + CH * s, CH)] = (da * cv * sa).astype(BF16)
            dp_ref[:, pl.ds(3 * SLAB + CH * s, CH)] = (da * gb * cv * (sg * (1.0 + za * (1.0 - sg)))).astype(BF16)
            n0 = jnp.broadcast_to(next_ref[0:1, cs], (tm, CH))
            n1 = jnp.broadcast_to(next_ref[1:2, cs], (tm, CH))
            u1 = jnp.where(rows == tm - 1, n0, pltpu.roll(dcv, tm - 1, 0))
            u2 = jnp.where(rows == tm - 2, n0, jnp.where(rows == tm - 1, n1, pltpu.roll(dcv, tm - 2, 0)))
            next_ref[:, cs] = dcv[0:8, :]
            dcx = w2 * dcv + w1 * u1 + w0 * u2
            dp_ref[:, pl.ds(1 * SLAB + CH * s, CH)] = (dcx * xa).astype(BF16)
            dp_ref[:, pl.ds(2 * SLAB + CH * s, CH)] = (dcx * gc).astype(BF16)
            dcw_ref[0:1, cs] += jnp.sum(dcv * c2, axis=0, keepdims=True)
            dcw_ref[1:2, cs] += jnp.sum(dcv * c1, axis=0, keepdims=True)
            dcw_ref[2:3, cs] += jnp.sum(dcv * cx, axis=0, keepdims=True)

            u, v, zb = slab(4), slab(5), slab(6)
            db = dm_ref[:, pl.ds(SLAB + CH * s, CH)].astype(F32)
            ug, ugrad = _gelu_parts(u)
            vg, vgrad = _gelu_parts(v)
            dlt = vg - jnp.mean(vg, axis=-1, keepdims=True)
            rstd = lax.rsqrt(jnp.mean(dlt * dlt, axis=-1, keepdims=True) + EPS)
            vhat = dlt * rstd
            lg = lng_ref[:, cs]
            vn = (vhat * lg + lnb_ref[:, cs]).astype(BF16)
            sgb = _sigmoid(zb)
            szb = zb * sgb
            sps, dvns = [], []
            dbs = jnp.zeros((8, CH), F32)
            dwc = jnp.zeros((CH, CH), F32)
            for c in range(nch):
                rs = slice(CH * c, CH * (c + 1))
                sp = jnp.dot(wc_ref[s], vn[rs], preferred_element_type=F32) + bsb_ref[s]
                dsp = (db[rs] * ug[rs] * szb[rs]).astype(BF16)
                dbs = dbs + lax.dot_general(ones8, dsp, (((1,), (1,)), ((), ())), preferred_element_type=F32)
                dwc = dwc + lax.dot_general(dsp, vn[rs], (((1,), (1,)), ((), ())), preferred_element_type=F32)
                dvns.append(jnp.dot(wct_ref[s], dsp, preferred_element_type=F32))
                sps.append(sp)
            sp = jnp.concatenate(sps, axis=0)
            dvn = jnp.concatenate(dvns, axis=0)
            dbs_ref[:, cs] += dbs
            dwc_ref[s] += dwc
            dlng_ref[:, cs] += jnp.sum(dvn * vhat, axis=0, keepdims=True)
            dlnb_ref[:, cs] += jnp.sum(dvn, axis=0, keepdims=True)
            dvhat = dvn * lg
            dvg = rstd * (dvhat - jnp.mean(dvhat, axis=-1, keepdims=True)
                          - vhat * jnp.mean(dvhat * vhat, axis=-1, keepdims=True))
            dp_ref[:, pl.ds(4 * SLAB + CH * s, CH)] = (db * sp * szb * ugrad).astype(BF16)
            dp_ref[:, pl.ds(5 * SLAB + CH * s, CH)] = (dvg * vgrad).astype(BF16)
            dp_ref[:, pl.ds(6 * SLAB + CH * s, CH)] = (db * ug * sp * (sgb * (1.0 + zb * (1.0 - sgb)))).astype(BF16)

            if s % 2 == 1:
                part = None
                for k in range(N_SLAB):
                    col = k * SLAB + pair * (s // 2)
                    blk, off = divmod(col, IN_BLK)
                    term = lax.dot_general(dp_ref[:, pl.ds(col, pair)], w_ref[blk, :, pl.ds(off, pair)],
                                           (((1,), (1,)), ((), ())), preferred_element_type=F32)
                    part = term if part is None else part + term
                if s == 1:
                    dh_ref[...] = part
                else:
                    dh_ref[...] += part

        xv = x_ref[...]
        r = lax.rsqrt(jnp.mean(xv * xv, axis=-1, keepdims=True) + EPS)
        dxn, dg = _rms_bwd(dh_ref[...], xv, r, g1_ref[...])
        gx_ref[...] = dx1_ref[...].astype(F32) + dxn
        dg1_ref[0:1, :] += dg

        @pl.when(i == nt - 1)
        def _():
            tril = lax.broadcasted_iota(jnp.int32, (CH, CH), 0) >= lax.broadcasted_iota(jnp.int32, (CH, CH), 1)
            for s in range(HEADS):
                dwc_ref[s] = jnp.where(tril, dwc_ref[s], 0.0)

    rev = lambda i: nt - 1 - i
    halo = lambda col: pl.BlockSpec((hb, SLAB), lambda i: (jnp.maximum(rev(i) * (tm // hb) - 1, 0), col))
    tok = lambda w: pl.BlockSpec((tm, w), lambda i: (rev(i), 0))
    return pl.pallas_call(
        body, grid=(nt,),
        in_specs=[tok(IN_DIM), halo(1), halo(2), tok(MIX), _full((8, D)), _full((1, D)), _full((1, D)),
                  _full((HEADS, CH, CH)), _full((HEADS, CH, CH)), _full((HEADS, CH, CH)),
                  _full((N_CHIP, D, IN_BLK), 1), tok(D), tok(D), _full((1, D))],
        out_specs=[tok(IN_DIM), _full((8, D)), _full((1, D)), _full((1, D)), _full((HEADS, CH, CH)), _full((8, D)),
                   tok(D), _full((8, D))],
        out_shape=[jax.ShapeDtypeStruct((t, IN_DIM), BF16), jax.ShapeDtypeStruct((8, D), F32),
                   jax.ShapeDtypeStruct((1, D), F32), jax.ShapeDtypeStruct((1, D), F32),
                   jax.ShapeDtypeStruct((HEADS, CH, CH), F32), jax.ShapeDtypeStruct((8, D), F32),
                   jax.ShapeDtypeStruct((t, D), F32), jax.ShapeDtypeStruct((8, D), F32)],
        scratch_shapes=[pltpu.VMEM((8, D), F32), pltpu.VMEM((tm, D), F32)],
        compiler_params=_cp(("arbitrary",), VMEM_LIMIT), name="mixer_bwd")(
            proj, proj, proj, dmix, cw8, lng, lnb, wc, wct, bsb, win_f, x, dx1, g1)


def _grad_matmul(a, b, after, *, by_cols, name, tk=1024):
    t, m = a.shape
    n = b.shape[1]
    nk = t // tk
    nj = N_CHIP if by_cols else 1
    bn = n // nj

    def body(a_ref, b_ref, after_ref, o_ref, ob_ref):
        kk = pl.program_id(1)
        part = lax.dot_general(a_ref[...], b_ref[...], (((0,), (0,)), ((), ())), preferred_element_type=F32)

        @pl.when(kk == 0)
        def _():
            o_ref[...] = part

        @pl.when(kk > 0)
        def _():
            o_ref[...] += part

        @pl.when(kk == nk - 1)
        def _():
            ob_ref[...] = o_ref[...].astype(BF16)

    a_spec = pl.BlockSpec((tk, m), lambda j, k: (k, 0))
    b_spec = pl.BlockSpec((tk, bn), lambda j, k: (k, j))
    o_spec = pl.BlockSpec((None, m, bn), lambda j, k: (j, 0, 0))
    o32, o16 = pl.pallas_call(
        body, grid=(nj, nk), in_specs=[a_spec, b_spec, ANY], out_specs=[o_spec, o_spec],
        out_shape=[jax.ShapeDtypeStruct((nj, m, bn), F32), jax.ShapeDtypeStruct((nj, m, bn), BF16)],
        compiler_params=_cp(("parallel", "arbitrary"), VMEM_LIMIT), name=name)(a, b, after)
    if by_cols:
        return o32, o16
    return o32.reshape(N_CHIP, m // N_CHIP, n), o16.reshape(N_CHIP, m // N_CHIP, n)


def _coords():
    x, y, c = lax.axis_index("x"), lax.axis_index("y"), lax.axis_index("c")
    chips = [(1 - x, y), (x, 1 - y), (1 - x, 1 - y)]
    return x, y, c, chips


def _pair_reduce(c_idx, grads, grads_b, smalls, name):
    ng, ns = len(grads), len(smalls)
    halves = [g.shape[1] // 2 for g in grads]

    def body(c_ref, *refs):
        g_in, gb_any = refs[:ng], refs[ng:2 * ng]
        s_own, s_any = refs[2 * ng:2 * ng + ns], refs[2 * ng + ns:2 * ng + 2 * ns]
        o = refs[2 * ng + 2 * ns:4 * ng + 3 * ns]
        lands = refs[4 * ng + 3 * ns:5 * ng + 4 * ns]
        send, recv = refs[5 * ng + 4 * ns:]
        x, y, c, _ = _coords()
        j = pl.program_id(0)

        def big(i, blk):
            return pltpu.make_async_remote_copy(
                src_ref=gb_any[i].at[blk, pl.ds((1 - c) * halves[i], halves[i])], dst_ref=lands[i].at[blk],
                send_sem=send.at[i, blk], recv_sem=recv.at[i, blk], device_id=(x, y, 1 - c), device_id_type=MESH)

        def small(i):
            return pltpu.make_async_remote_copy(
                src_ref=s_any[i].at[1 - c], dst_ref=lands[ng + i],
                send_sem=send.at[ng + i, 0], recv_sem=recv.at[ng + i, 0], device_id=(x, y, 1 - c), device_id_type=MESH)

        @pl.when(j == 0)
        def _():
            for blk in range(N_CHIP):
                for i in range(ng):
                    big(i, blk).start()
            for i in range(ns):
                small(i).start()

        for i in range(ng):
            big(i, j).wait_recv()
            tot = g_in[i][...] + lands[i][j].astype(F32)
            o[i][...] = tot
            o[ng + i][...] = tot.astype(BF16)

        @pl.when(j == N_CHIP - 1)
        def _():
            for i in range(ns):
                small(i).wait_recv()
                o[2 * ng + i][...] = s_own[i][...] + lands[ng + i][...]
                small(i).wait_send()
            for blk in range(N_CHIP):
                for i in range(ng):
                    big(i, blk).wait_send()

    in_specs = [pl.BlockSpec((None, None, halves[i], g.shape[2]), lambda b, c: (b, c[0], 0, 0)) for i, g in enumerate(grads)]
    in_specs += [ANY] * ng
    in_specs += [pl.BlockSpec((None, s.shape[0] // 2, s.shape[1]), lambda b, c: (c[0], 0, 0)) for s in smalls]
    in_specs += [ANY] * ns
    blk = [pl.BlockSpec((None, halves[i], g.shape[2]), lambda b, c: (b, 0, 0)) for i, g in enumerate(grads)]
    out_specs = blk + blk + [pl.BlockSpec((s.shape[0] // 2, s.shape[1]), lambda b, c: (0, 0)) for s in smalls]
    out_shape = [jax.ShapeDtypeStruct((N_CHIP, halves[i], g.shape[2]), F32) for i, g in enumerate(grads)]
    out_shape += [jax.ShapeDtypeStruct((N_CHIP, halves[i], g.shape[2]), BF16) for i, g in enumerate(grads)]
    out_shape += [jax.ShapeDtypeStruct((s.shape[0] // 2, s.shape[1]), F32) for s in smalls]
    scratch = [pltpu.VMEM((N_CHIP, halves[i], g.shape[2]), BF16) for i, g in enumerate(grads)]
    scratch += [pltpu.VMEM((s.shape[0] // 2, s.shape[1]), F32) for s in smalls]
    scratch += [pltpu.SemaphoreType.DMA((ng + ns, N_CHIP)), pltpu.SemaphoreType.DMA((ng + ns, N_CHIP))]
    grads4 = [g.reshape(N_CHIP, 2, halves[i], g.shape[2]) for i, g in enumerate(grads)]
    smalls3 = [s.reshape(2, s.shape[0] // 2, s.shape[1]) for s in smalls]
    return pl.pallas_call(
        body, out_shape=out_shape,
        grid_spec=pltpu.PrefetchScalarGridSpec(num_scalar_prefetch=1, grid=(N_CHIP,), in_specs=in_specs,
                                               out_specs=out_specs, scratch_shapes=scratch),
        compiler_params=_cp(("arbitrary",), VMEM_LIMIT), name=name)(c_idx, *grads4, *grads_b, *smalls3, *smalls3)


_HBM = pl.BlockSpec(memory_space=pltpu.HBM)
_SEM = pl.BlockSpec(memory_space=pltpu.SEMAPHORE)


def _split_copies(ins, lands, ng, send, recv, arriving):
    x, y, c, chips = _coords()
    b = 2 * x + y
    copies = []
    for i in range(len(ins)):
        for k in range(3):
            blk = 2 * chips[k][0] + chips[k][1]
            src, dst, got = (ins[i].at[blk], lands[i].at[k], lands[i].at[k]) if i < ng else (ins[i], lands[i].at[b], lands[i].at[blk])
            sems = dict(send_sem=send.at[3 * i + k], recv_sem=recv.at[3 * i + k], device_id=(*chips[k], c), device_id_type=MESH)
            if arriving:
                copies.append(pltpu.make_async_remote_copy(src_ref=got, dst_ref=got, **sems))
            else:
                copies.append(pltpu.make_async_remote_copy(src_ref=src, dst_ref=dst, **sems))
    return copies


def _exchange_begin(sums_b, smalls, name):
    ng, n = len(sums_b), len(sums_b) + len(smalls)
    srcs = list(sums_b) + list(smalls)
    lands = [lax.empty((3,) + g.shape[1:], g.dtype) for g in sums_b] + [lax.empty((N_CHIP,) + s.shape, s.dtype) for s in smalls]

    def body(*refs):
        ins, land_refs = refs[:n], refs[n:2 * n]
        send, recv = refs[2 * n], refs[2 * n + 1]
        token = refs[4 * n + 2]
        for cp in _split_copies(ins, land_refs, ng, send, recv, False):
            cp.start()
        token[...] = jnp.zeros_like(token)

    hbm = lambda a: pltpu.HBM(a.shape, a.dtype)
    outs = pl.pallas_call(
        body, name=name,
        out_shape=(pltpu.SemaphoreType.DMA((3 * n,)), pltpu.SemaphoreType.DMA((3 * n,)), *[hbm(a) for a in srcs + lands],
                   jax.ShapeDtypeStruct((8, 128), F32)),
        in_specs=[_HBM] * (2 * n), out_specs=(_SEM, _SEM, *[_HBM] * (2 * n), pl.BlockSpec(memory_space=pltpu.VMEM)),
        input_output_aliases={i: 2 + i for i in range(2 * n)},
        compiler_params=pltpu.CompilerParams(has_side_effects=pltpu.SideEffectType.DATAFLOW_SIDE_EFFECTING),
    )(*[pltpu.with_memory_space_constraint(a, pltpu.HBM) for a in srcs + lands])
    return outs[0], outs[1], list(outs[2:2 + n]), list(outs[2 + n:2 + 2 * n]), outs[2 + 2 * n]


def _exchange_end(send, recv, srcs, lands, ng, after, name):
    n = len(srcs)
    after = list(after)

    def body(*refs):
        ins, land_refs = refs[:n], refs[n:2 * n]
        send_ref, recv_ref = refs[2 * n], refs[2 * n + 1]
        for cp in _split_copies(ins, land_refs, ng, send_ref, recv_ref, False):
            cp.wait_send()
        for cp in _split_copies(ins, land_refs, ng, send_ref, recv_ref, True):
            cp.wait_recv()

    hbm = lambda a: pltpu.HBM(a.shape, a.dtype)
    outs = pl.pallas_call(
        body, name=name, out_shape=tuple(hbm(a) for a in list(srcs) + list(lands)),
        in_specs=[_HBM] * (2 * n) + [_SEM, _SEM] + [ANY] * len(after), out_specs=tuple([_HBM] * (2 * n)),
        input_output_aliases={i: i for i in range(2 * n)},
        compiler_params=pltpu.CompilerParams(has_side_effects=pltpu.SideEffectType.DATAFLOW_SIDE_EFFECTING),
    )(*srcs, *lands, send, recv, *after)
    return list(outs[n:])


def _chip_reduce(bc_idx, sums, recvd, smalls_slots, smalls_own, name, steps=4):
    ng, ns = len(sums), len(smalls_slots)
    n = ng + ns
    assert steps >= 2
    halves = [g.shape[1] for g in sums] + [s.shape[1] for s in smalls_slots]
    rows = [g.shape[1] // steps for g in sums]

    def body(bc_ref, *refs):
        own, rx = refs[:ng], refs[ng:2 * ng]
        sl = refs[2 * ng:2 * ng + ns]
        sl_own = refs[2 * ng + ns:2 * ng + 2 * ns]
        o = refs[2 * ng + 2 * ns:2 * ng + 2 * ns + n]
        tiles = refs[2 * ng + 2 * ns + n:2 * ng + 2 * ns + 2 * n]
        keep, send, recv = refs[2 * ng + 2 * ns + 2 * n:]
        x, y, c, _ = _coords()
        sibling = dict(device_id=(x, y, 1 - c), device_id_type=MESH)
        r = pl.program_id(0)

        def writes(i, step, slot):
            dst = o[i].at[pl.ds(c * halves[i] + step * rows[i], rows[i])]
            return (pltpu.make_async_copy(tiles[i].at[slot], dst, keep.at[i, slot]),
                    pltpu.make_async_remote_copy(src_ref=tiles[i].at[slot], dst_ref=dst, send_sem=send.at[i, slot],
                                                 recv_sem=recv.at[i, step], **sibling))

        def small_writes(i):
            dst = o[i].at[pl.ds(c * halves[i], halves[i])]
            return (pltpu.make_async_copy(tiles[i], dst, keep.at[i, 0]),
                    pltpu.make_async_remote_copy(src_ref=tiles[i], dst_ref=dst, send_sem=send.at[i, 0],
                                                 recv_sem=recv.at[i, 0], **sibling))

        def arriving(i, step, nrows):
            dst = o[i].at[pl.ds((1 - c) * halves[i] + step * nrows, nrows)]
            return pltpu.make_async_remote_copy(src_ref=dst, dst_ref=dst, send_sem=send.at[i, 0], recv_sem=recv.at[i, step],
                                                **sibling)

        def finish(step, slot):
            for i in range(ng):
                local, remote = writes(i, step, slot)
                local.wait()
                remote.wait_send()

        @pl.when(r >= 2)
        def _():
            finish(r - 2, r % 2)

        for i in range(ng):
            tot = own[i][...]
            for j in range(3):
                tot = tot + rx[i][j].astype(F32)
            tiles[i][r % 2] = tot
            for cp in writes(i, r, r % 2):
                cp.start()

        @pl.when(r == 0)
        def _():
            for i in range(ns):
                term = [jnp.where(bc_ref[0] == kk, sl_own[i][...], sl[i][kk]) for kk in range(N_CHIP)]
                tiles[ng + i][...] = ((term[0] + term[1]) + term[2]) + term[3]
                for cp in small_writes(ng + i):
                    cp.start()

        @pl.when(r == steps - 1)
        def _():
            finish(steps - 2, (steps - 2) % 2)
            finish(steps - 1, (steps - 1) % 2)
            for i in range(ns):
                local, remote = small_writes(ng + i)
                local.wait()
                remote.wait_send()
                arriving(ng + i, 0, halves[ng + i]).wait_recv()
            for i in range(ng):
                for step in range(steps):
                    arriving(i, step, rows[i]).wait_recv()

    in_specs = [pl.BlockSpec((None, rows[i], g.shape[2]), lambda r, bc: (bc[0], r, 0)) for i, g in enumerate(sums)]
    in_specs += [pl.BlockSpec((3, rows[i], g.shape[2]), lambda r, bc: (0, r, 0)) for i, g in enumerate(sums)]
    in_specs += [pl.BlockSpec(s.shape, lambda r, bc: (0, 0, 0)) for s in smalls_slots]
    in_specs += [pl.BlockSpec(s.shape[1:], lambda r, bc: (0, 0)) for s in smalls_slots]
    out_shape = [jax.ShapeDtypeStruct((2 * g.shape[1], g.shape[2]), F32) for g in sums]
    out_shape += [jax.ShapeDtypeStruct((2 * s.shape[1], s.shape[2]), F32) for s in smalls_slots]
    scratch = [pltpu.VMEM((2, rows[i], g.shape[2]), F32) for i, g in enumerate(sums)]
    scratch += [pltpu.VMEM(s.shape[1:], F32) for s in smalls_slots]
    scratch += [pltpu.SemaphoreType.DMA((n, 2)), pltpu.SemaphoreType.DMA((n, 2)), pltpu.SemaphoreType.DMA((n, steps))]
    return list(pl.pallas_call(
        body, out_shape=out_shape,
        grid_spec=pltpu.PrefetchScalarGridSpec(num_scalar_prefetch=1, grid=(steps,), in_specs=in_specs,
                                               out_specs=[ANY] * n, scratch_shapes=scratch),
        compiler_params=_cp(("arbitrary",), VMEM_LIMIT), name=name)(bc_idx, *sums, *recvd, *smalls_slots, *smalls_own))


def _adamw_math(w, g, m, v):
    m2 = ADAM_B1 * m + (1.0 - ADAM_B1) * g
    v2 = ADAM_B2 * v + (1.0 - ADAM_B2) * (g * g)
    m_hat = m2 / (1.0 - ADAM_B1 ** ADAM_STEP)
    v_hat = v2 / (1.0 - ADAM_B2 ** ADAM_STEP)
    delta = -ADAM_LR * (m_hat / (jnp.sqrt(v_hat) + ADAM_EPS) + ADAM_WD * w)
    return delta, m2, v2


def _adamw_big(ws, gs, ms, vs, name, steps=8):
    n = len(ws)

    def body(*refs):
        for i in range(n):
            w_ref, g_ref, m_ref, v_ref = (refs[k * n + i] for k in range(4))
            d_ref, m2_ref, v2_ref, g2_ref = (refs[(4 + k) * n + i] for k in range(4))
            gv = g_ref[...]
            d_ref[...], m2_ref[...], v2_ref[...] = _adamw_math(w_ref[...], gv, m_ref[...], v_ref[...])
            g2_ref[...] = gv

    specs = [pl.BlockSpec((w.shape[0] // steps, w.shape[1]), lambda i: (i, 0)) for w in ws]
    shapes = [jax.ShapeDtypeStruct(w.shape, F32) for w in ws]
    outs = pl.pallas_call(
        body, grid=(steps,), in_specs=specs * 4, out_specs=specs * 4, out_shape=shapes * 4,
        compiler_params=_cp(("parallel",), VMEM_LIMIT), name=name)(*ws, *gs, *ms, *vs)
    return [tuple(outs[k * n + i] for k in range(4)) for i in range(n)]


def _adamw_small(groups):
    n = len(groups)

    def body(*refs):
        for i in range(n):
            w_ref, g_ref, m_ref, v_ref = refs[4 * i:4 * i + 4]
            d_ref, m2_ref, v2_ref = refs[4 * n + 3 * i:4 * n + 3 * i + 3]
            d_ref[...], m2_ref[...], v2_ref[...] = _adamw_math(w_ref[...], g_ref[...], m_ref[...], v_ref[...])

    flat = [a for grp in groups for a in grp]
    out_shape = [jax.ShapeDtypeStruct(grp[0].shape, F32) for grp in groups for _ in range(3)]
    outs = pl.pallas_call(body, out_shape=out_shape, name="adamw_small")(*flat)
    return [tuple(outs[3 * i:3 * i + 3]) for i in range(n)]


def kernel(x, mem, norm_mix_g, w_in, conv_w, gm_ln_g, gm_ln_b, gm_ws, gm_bs, w_out, norm_x_g, norm_mem_g, w_q, w_kv, w_xo, norm_final_g, loss_target, m_norm_mix_g, m_w_in, m_conv_w, m_gm_ln_g, m_gm_ln_b, m_gm_ws, m_gm_bs, m_w_out, m_norm_x_g, m_norm_mem_g, m_w_q, m_w_kv, m_w_xo, m_norm_final_g, v_norm_mix_g, v_w_in, v_conv_w, v_gm_ln_g, v_gm_ln_b, v_gm_ws, v_gm_bs, v_w_out, v_norm_x_g, v_norm_mem_g, v_w_q, v_w_kv, v_w_xo, v_norm_final_g):
    t = x.shape[1]
    xi = lax.axis_index("x")
    yi = lax.axis_index("y")
    ci = lax.axis_index("c")
    b_idx = jnp.reshape(2 * xi + yi, (1,)).astype(jnp.int32)
    c_idx = jnp.reshape(ci, (1,)).astype(jnp.int32)

    x2d, mem2d, tgt = x[0], mem[0], loss_target[0]
    big = [w_in[0], w_out[0], w_q[0], w_kv[0], w_xo[0]]
    big_m = [m_w_in[0], m_w_out[0], m_w_q[0], m_w_kv[0], m_w_xo[0]]
    big_v = [v_w_in[0], v_w_out[0], v_w_q[0], v_w_kv[0], v_w_xo[0]]
    g3 = norm_final_g.reshape(1, D)

    def pad8(a):
        return jnp.pad(a, ((0, 8 - a.shape[0]), (0, 0)))

    own_blocks = _cast_shards(b_idx, big)

    tril = jnp.tril(jnp.ones((CH, CH), bool))
    wc32 = jnp.where(tril[None], gm_ws[0], 0.0)
    wc = wc32.astype(BF16)
    wct = jnp.swapaxes(wc32, 1, 2).astype(BF16)
    bsb = jnp.broadcast_to(gm_bs[0][:, :, None], (HEADS, CH, CH))

    blk = 2 * xi + yi
    order = jnp.stack([blk, blk ^ 2, blk ^ 1, blk ^ 3]).astype(jnp.int32)
    proj, hb, win_f, cw8, (wout_f,) = _proj_gather(
        order, x2d, norm_mix_g, own_blocks[0], pad8(conv_w[0]), [own_blocks[1]])
    mixin, (wq_f, wkv_f, wxo_f) = _mixer_fwd(proj, cw8, gm_ln_g, gm_ln_b, wc, bsb, own_blocks[2:])
    wout2, wq2, wxo2 = wout_f.reshape(MIX, D), wq_f.reshape(D, D), wxo_f.reshape(D, D)
    k, v = _mem_fwd(mem2d, norm_mem_g, wkv_f)

    (loss_tile, dmix, dx1b, h2b, dq, ob, dx2b, dk, dv, dg2, dg3) = _tail(
        x2d, tgt, mixin, wout2, wq2, wxo2, k, v, norm_x_g, g3)
    dwkv, dwkv_b, dgm = _mem_bwd(mem2d, norm_mem_g, dk, dv, wkv_f)
    dproj, dcw, dlng, dlnb, dwc, dbs8, grad_x, dg1 = _mixer_bwd(
        proj, dmix, cw8, gm_ln_g, gm_ln_b, wc, wct, bsb, win_f, x2d, dx1b, norm_mix_g)

    bc_idx = jnp.concatenate([b_idx, c_idx])
    dwin, dwin_b = _grad_matmul(hb, dproj, dgm, by_cols=True, name="grad_w_in", tk=2048)
    zero = jnp.zeros((1, D), F32)
    loss_row = jnp.broadcast_to(loss_tile[0:1, 0:1], (1, D))
    sv = jnp.concatenate([dg1[0:1], dg2, dgm, dg3, dlng, dlnb, dbs8[0:1], loss_row, dcw], axis=0)
    sw = dwc.reshape(HEADS * CH, CH)
    ps_b = _pair_reduce(c_idx, [dwin], [dwin_b], [sv, sw], "pair_reduce_b")
    sums_b, sums_b_b, psmall = list(ps_b[:1]), list(ps_b[1:2]), list(ps_b[2:])
    send_b, recv_b, src_b, land_b, token_b = _exchange_begin(sums_b_b, psmall, "exchange_b_begin")

    dwxo, dwxo_b = _grad_matmul(ob, dx2b, token_b, by_cols=False, name="grad_w_xo", tk=2048)
    dwq, dwq_b = _grad_matmul(h2b, dq, token_b, by_cols=False, name="grad_w_q", tk=2048)
    dwout, dwout_b = _grad_matmul(mixin, dx1b, token_b, by_cols=False, name="grad_w_out")
    ps_a = _pair_reduce(c_idx, [dwout, dwq, dwkv, dwxo], [dwout_b, dwq_b, dwkv_b, dwxo_b], [], "pair_reduce_a")
    sums_a, sums_a_b = list(ps_a[:4]), list(ps_a[4:8])
    send_a, recv_a, src_a, land_a, token_a = _exchange_begin(sums_a_b, [], "exchange_a_begin")

    rx2b = _exchange_end(send_b, recv_b, src_b, land_b, 1, [token_a], "exchange_b_end")
    gwin, svf, swf = _chip_reduce(bc_idx, sums_b, rx2b[:1], rx2b[1:], psmall, "chip_reduce_b")
    out_b = _adamw_big(big[:1], [gwin], big_m[:1], big_v[:1], "adamw_w_in")[0]

    def vec_pack(a1, a2, am, a3, lg, lb, bs):
        return jnp.concatenate([a1, a2, am, a3.reshape(1, D), lg, lb, bs.reshape(1, D), zero], axis=0)

    wv = vec_pack(norm_mix_g, norm_x_g, norm_mem_g, norm_final_g, gm_ln_g, gm_ln_b, gm_bs)
    mv = vec_pack(m_norm_mix_g, m_norm_x_g, m_norm_mem_g, m_norm_final_g, m_gm_ln_g, m_gm_ln_b, m_gm_bs)
    vv = vec_pack(v_norm_mix_g, v_norm_x_g, v_norm_mem_g, v_norm_final_g, v_gm_ln_g, v_gm_ln_b, v_gm_bs)
    loss = svf[7, 0]
    gcw = lax.dynamic_slice_in_dim(svf[8:16], blk * (D // N_CHIP), D // N_CHIP, axis=1)
    gws = swf
    gv = svf[0:8]
    (dv_, mv_, vv_), (dc_, mc_, vc_), (dws_, mws_, vws_) = _adamw_small([
        (wv, gv, mv, vv),
        (pad8(conv_w[0]), gcw, pad8(m_conv_w[0]), pad8(v_conv_w[0])),
        (gm_ws.reshape(HEADS * CH, CH), gws, m_gm_ws.reshape(HEADS * CH, CH), v_gm_ws.reshape(HEADS * CH, CH))])

    rx2a = _exchange_end(send_a, recv_a, src_a, land_a, 4, [out_b[0], dv_], "exchange_a_end")
    g_a = _chip_reduce(bc_idx, sums_a, rx2a, [], [], "chip_reduce_a")
    big_out = [out_b] + _adamw_big(big[1:], g_a, big_m[1:], big_v[1:], "adamw_rest")

    def unpack(vecs, cw, ws, bigs):
        r = lambda i: vecs[i:i + 1]
        return [r(0), bigs[0][None], cw[0:3][None], r(4), r(5), ws.reshape(1, HEADS, CH, CH), vecs[6].reshape(1, HEADS, CH),
                bigs[1][None], r(1), r(2), bigs[2][None], bigs[3][None], bigs[4][None], vecs[3]]

    grads_out = unpack(gv, gcw, gws, [o[3] for o in big_out])
    delta_out = unpack(dv_, dc_, dws_, [o[0] for o in big_out])
    m_out = unpack(mv_, mc_, mws_, [o[1] for o in big_out])
    v_out = unpack(vv_, vc_, vws_, [o[2] for o in big_out])
    return (loss, grad_x[None], *grads_out, *delta_out, *m_out, *v_out)
```

```python
import functools
import math

import jax
import jax.numpy as jnp
from jax import lax
from jax.experimental import pallas as pl
from jax.experimental.pallas import tpu as pltpu

F32 = jnp.float32
BF16 = jnp.bfloat16
MESH = pl.DeviceIdType.MESH

D = 1024
SLAB = 1024
N_SLAB = 7
IN_DIM = N_SLAB * SLAB
MIX = 2 * SLAB
HEADS = 8
CH = 128
XH = 4
XD = D // XH
EPS = 1e-6
GELU_C = math.sqrt(2.0 / math.pi)
GELU_A = 0.044715
N_CHIP = 4
IN_BLK = IN_DIM // N_CHIP
KV_BLK = 2 * D // N_CHIP

ADAM_LR, ADAM_B1, ADAM_B2, ADAM_EPS, ADAM_WD, ADAM_STEP = 0.001, 0.9, 0.999, 1e-08, 0.01, 10

VMEM_LIMIT = 60 * 1024 * 1024


def _cp(sem=None, vmem=None):
    return pltpu.CompilerParams(dimension_semantics=sem, vmem_limit_bytes=vmem)


def _full(shape, buffers=None):
    n = len(shape)
    if buffers is None:
        return pl.BlockSpec(shape, lambda *_: (0,) * n)
    return pl.BlockSpec(shape, lambda *_: (0,) * n, pipeline_mode=pl.Buffered(buffers))


ANY = pl.BlockSpec(memory_space=pl.ANY)


def _bdot(a, b):
    return jnp.dot(a.astype(BF16), b.astype(BF16), preferred_element_type=F32)


def _bdot_nt(a, b):
    return lax.dot_general(a.astype(BF16), b.astype(BF16), (((1,), (1,)), ((), ())), preferred_element_type=F32)


def _bdot_tn(a, b):
    return lax.dot_general(a.astype(BF16), b.astype(BF16), (((0,), (0,)), ((), ())), preferred_element_type=F32)


def _rms(x, g):
    r = lax.rsqrt(jnp.mean(x * x, axis=-1, keepdims=True) + EPS)
    return x * r * g, r


def _rms_bwd(dy, x, r, g):
    gdy = dy * g
    dx = r * gdy - x * (r * r * r) * jnp.mean(x * gdy, axis=-1, keepdims=True)
    dg = jnp.sum(dy * x * r, axis=0, keepdims=True)
    return dx, dg


def _gelu_parts(x):
    x2 = x * x
    t = jnp.tanh(GELU_C * (x + GELU_A * x * x2))
    val = 0.5 * x * (1.0 + t)
    grad = 0.5 * (1.0 + t) + 0.5 * x * (1.0 - t * t) * (GELU_C * (1.0 + 3.0 * GELU_A * x2))
    return val, grad


def _gelu(x):
    return 0.5 * x * (1.0 + jnp.tanh(GELU_C * (x + GELU_A * x * x * x)))


def _sigmoid(z):
    return 1.0 / (1.0 + jnp.exp(-z))


def _cast_shards(b_idx, arrs):
    n = len(arrs)
    steps = 8

    def body(b_ref, *refs):
        for i in range(n):
            refs[n + i][...] = refs[i][...].astype(BF16)

    in_specs = [pl.BlockSpec((a.shape[0] // steps, a.shape[1]), lambda i, b: (i, 0)) for a in arrs]
    out_specs = [pl.BlockSpec((None, a.shape[0] // steps, a.shape[1]), lambda i, b: (b[0], i, 0)) for a in arrs]
    return pl.pallas_call(
        body, out_shape=[jax.ShapeDtypeStruct((N_CHIP,) + a.shape, BF16) for a in arrs],
        grid_spec=pltpu.PrefetchScalarGridSpec(num_scalar_prefetch=1, grid=(steps,), in_specs=in_specs, out_specs=out_specs),
        compiler_params=_cp(("arbitrary",)), name="cast_shards")(b_idx, *arrs)


def _proj_gather(order, x, g, win_own, cw8s, more, tm=1024):
    t = x.shape[0]
    ni = t // tm
    nm = len(more)

    def body(*refs):
        order_ref, x_ref, g_ref, win_in, cw_in = refs[:5]
        o_ref, hb_ref, win_f, cw_out = refs[5 + nm:9 + nm]
        more_out = refs[9 + nm:9 + 2 * nm]
        hbuf, wv, cw_s, cw_r, loc = refs[9 + 2 * nm:14 + 2 * nm]
        g_in = _Gather([win_f], *refs[14 + 2 * nm:18 + 2 * nm])
        g_more = _Gather(more_out, *refs[18 + 2 * nm:22 + 2 * nm])
        j, i = pl.program_id(0), pl.program_id(1)
        x, y, c, chips = _coords()
        b = 2 * x + y
        blks = [2 * chip[0] + chip[1] for chip in chips]

        def cw_cols(blk):
            return cw_out.at[:, pl.ds(blk * (D // N_CHIP), D // N_CHIP)]

        def cw_copy(k, blk):
            src = cw_in if blk is None else cw_cols(blk)
            return pltpu.make_async_remote_copy(src_ref=src, dst_ref=cw_cols(b if blk is None else blk), send_sem=cw_s.at[k],
                                                recv_sem=cw_r.at[k], device_id=(*chips[k], c), device_id_type=MESH)

        cw_local = pltpu.make_async_copy(cw_in, cw_cols(b), loc.at[1])

        def load(blk, slot):
            return pltpu.make_async_copy(win_f.at[blk], wv.at[slot], loc.at[2 + slot])

        @pl.when((j == 0) & (i == 0))
        def _():
            g_in.start()
            cw_local.start()
            for k in range(3):
                cw_copy(k, None).start()
            load(b, 0).start()
            load(b, 0).wait()

        @pl.when((j == 1) & (i == 0))
        def _():
            g_in.hop()
            g_more.start()
            g_in.near_ready()
            load(g_in.bx, 1).start()
            load(g_in.by, 0).start()
            load(g_in.bx, 1).wait()

        @pl.when((j == 2) & (i == 0))
        def _():
            load(g_in.by, 0).wait()
            g_in.far()
            g_in.far_ready()
            load(g_in.bd, 1).start()

        @pl.when((j == 3) & (i == 0))
        def _():
            load(g_in.bd, 1).wait()
            g_more.hop()

        rows = pl.ds(pl.multiple_of(i * tm, tm), tm)

        @pl.when(j == 0)
        def _():
            h, _ = _rms(x_ref[...], g_ref[...])
            hbuf[rows, :] = h.astype(BF16)
            hb_ref[...] = h.astype(BF16)

        @pl.when((j == N_CHIP - 1) & (i == ni - 1))
        def _():
            g_more.far()

        o_ref[...] = jnp.dot(hbuf[rows, :], wv[lax.rem(j, 2)], preferred_element_type=F32).astype(BF16)

        @pl.when((j == N_CHIP - 1) & (i == ni - 1))
        def _():
            for k in range(3):
                cw_copy(k, blks[k]).wait_recv()
            for k in range(3):
                cw_copy(k, None).wait_send()
            cw_local.wait()
            g_more.near_ready()
            g_more.far_ready()
            g_in.drain()
            g_more.drain()

    first = lambda j, i: jnp.where(j == 0, i, ni - 1)
    in_specs = [pl.BlockSpec((tm, D), lambda j, i, o: (first(j, i), 0)), pl.BlockSpec((1, D), lambda j, i, o: (0, 0)),
                ANY, ANY] + [ANY] * nm
    out_specs = [pl.BlockSpec((tm, IN_BLK), lambda j, i, o: (i, o[j])),
                 pl.BlockSpec((tm, D), lambda j, i, o: (first(j, i), 0)), ANY, ANY] + [ANY] * nm
    outs = pl.pallas_call(
        body, out_shape=[jax.ShapeDtypeStruct((t, IN_DIM), BF16), jax.ShapeDtypeStruct((t, D), BF16),
                         jax.ShapeDtypeStruct(win_own.shape, BF16), jax.ShapeDtypeStruct((8, D), F32)]
        + [jax.ShapeDtypeStruct(f.shape, f.dtype) for f in more],
        grid_spec=pltpu.PrefetchScalarGridSpec(
            num_scalar_prefetch=1, grid=(N_CHIP, ni), in_specs=in_specs, out_specs=out_specs,
            scratch_shapes=[pltpu.VMEM((t, D), BF16), pltpu.VMEM((2, D, IN_BLK), BF16)]
            + [pltpu.SemaphoreType.DMA((3,))] * 2 + [pltpu.SemaphoreType.DMA((4,))] + _gather_sems(1) + _gather_sems(nm)),
        input_output_aliases={3: 2, **{5 + w: 4 + w for w in range(nm)}},
        compiler_params=_cp(("arbitrary", "arbitrary"), VMEM_LIMIT), name="proj_gather")(order, x, g, win_own, cw8s, *more)
    return outs[0], outs[1], outs[2], outs[3], outs[4:]


class _Gather:
    def __init__(self, outs, ici_s, ici_r, d2d_s, d2d_r):
        x, y, c, _ = _coords()
        self.outs, self.c = outs, c
        self.sems = ici_s, ici_r, d2d_s, d2d_r
        self.b, self.bx, self.by, self.bd = 2 * x + y, 2 * (1 - x) + y, 2 * x + (1 - y), 2 * (1 - x) + (1 - y)
        self.xn, self.yn, self.sib = (1 - x, y, c), (x, 1 - y, c), (x, y, 1 - c)

    def piece(self, w, blk, hc, quarter=None):
        hr = self.outs[w].shape[1] // 2
        if quarter is None:
            return self.outs[w].at[blk, pl.ds(hc * hr, hr)]
        return self.outs[w].at[blk, pl.ds(hc * hr + quarter * (hr // 2), hr // 2)]

    def ici(self, w, k, ref, to):
        return pltpu.make_async_remote_copy(src_ref=ref, dst_ref=ref, send_sem=self.sems[0].at[w, k],
                                            recv_sem=self.sems[1].at[w, k], device_id=to, device_id_type=MESH)

    def d2d(self, w, k, ref):
        return pltpu.make_async_remote_copy(src_ref=ref, dst_ref=ref, send_sem=self.sems[2].at[w, k],
                                            recv_sem=self.sems[3].at[w, k], device_id=self.sib, device_id_type=MESH)

    def start(self):
        for w in range(len(self.outs)):
            mine = self.piece(w, self.b, self.c)
            self.ici(w, 0, mine, self.xn).start()
            self.ici(w, 1, mine, self.yn).start()

    def hop(self):
        c = self.c
        for w in range(len(self.outs)):
            self.ici(w, 0, self.piece(w, self.bx, c), self.xn).wait_recv()
            self.ici(w, 1, self.piece(w, self.by, c), self.yn).wait_recv()
            self.ici(w, 2, self.piece(w, self.bx, c, 0), self.yn).start()
            self.ici(w, 3, self.piece(w, self.by, c, 1), self.xn).start()
            self.d2d(w, 0, self.piece(w, self.bx, c)).start()
            self.d2d(w, 1, self.piece(w, self.by, c)).start()

    def near_ready(self):
        for w in range(len(self.outs)):
            self.d2d(w, 0, self.piece(w, self.bx, 1 - self.c)).wait_recv()
            self.d2d(w, 1, self.piece(w, self.by, 1 - self.c)).wait_recv()

    def far(self):
        c = self.c
        for w in range(len(self.outs)):
            self.ici(w, 2, self.piece(w, self.bd, c, 0), self.yn).wait_recv()
            self.ici(w, 3, self.piece(w, self.bd, c, 1), self.xn).wait_recv()
            self.d2d(w, 2, self.piece(w, self.bd, c, 0)).start()
            self.d2d(w, 3, self.piece(w, self.bd, c, 1)).start()

    def far_ready(self):
        for w in range(len(self.outs)):
            self.d2d(w, 2, self.piece(w, self.bd, 1 - self.c, 0)).wait_recv()
            self.d2d(w, 3, self.piece(w, self.bd, 1 - self.c, 1)).wait_recv()

    def drain(self):
        c = self.c
        for w in range(len(self.outs)):
            mine = self.piece(w, self.b, c)
            self.ici(w, 0, mine, self.xn).wait_send()
            self.ici(w, 1, mine, self.yn).wait_send()
            self.ici(w, 2, self.piece(w, self.bx, c, 0), self.yn).wait_send()
            self.ici(w, 3, self.piece(w, self.by, c, 1), self.xn).wait_send()
            self.d2d(w, 0, self.piece(w, self.bx, c)).wait_send()
            self.d2d(w, 1, self.piece(w, self.by, c)).wait_send()
            self.d2d(w, 2, self.piece(w, self.bd, c, 0)).wait_send()
            self.d2d(w, 3, self.piece(w, self.bd, c, 1)).wait_send()


def _gather_sems(nw):
    return [pltpu.SemaphoreType.DMA((max(nw, 1), 4))] * 4


def _mixer_fwd(proj, cw8, lng, lnb, wc, bsb, fulls, tm=256):
    t = proj.shape[0]
    nt = t // tm
    nch = tm // CH
    nw = len(fulls)

    def body(*refs):
        p_ref, cw_ref, lng_ref, lnb_ref, wc_ref, bsb_ref = refs[:6]
        mix_ref = refs[6 + nw]
        w_outs = refs[7 + nw:7 + 2 * nw]
        prev_ref = refs[7 + 2 * nw]
        gather = _Gather(w_outs, *refs[8 + 2 * nw:])

        @pl.when(pl.program_id(0) == 0)
        def _():
            gather.start()
            prev_ref[...] = jnp.zeros_like(prev_ref)

        @pl.when(pl.program_id(0) == nt // 2)
        def _():
            gather.hop()

        @pl.when(pl.program_id(0) == nt - 1)
        def _():
            gather.far()

        rows = lax.broadcasted_iota(jnp.int32, (tm, CH), 0)
        for s in range(HEADS):
            cs = pl.ds(CH * s, CH)

            def slab(k):
                return p_ref[:, pl.ds(k * SLAB + CH * s, CH)].astype(F32)

            gb, gc, xa, za = slab(0), slab(1), slab(2), slab(3)
            cx = gc * xa
            p6 = jnp.broadcast_to(prev_ref[6:7, cs], (tm, CH))
            p7 = jnp.broadcast_to(prev_ref[7:8, cs], (tm, CH))
            c1 = jnp.where(rows == 0, p7, pltpu.roll(cx, 1, 0))
            c2 = jnp.where(rows == 0, p6, jnp.where(rows == 1, p7, pltpu.roll(cx, 2, 0)))
            prev_ref[:, cs] = cx[tm - 8:, :]
            cv = cw_ref[0:1, cs] * c2 + cw_ref[1:2, cs] * c1 + cw_ref[2:3, cs] * cx
            mix_ref[:, cs] = (gb * cv * (za * _sigmoid(za))).astype(BF16)

            u, v, zb = slab(4), slab(5), slab(6)
            ug, vg = _gelu(u), _gelu(v)
            dlt = vg - jnp.mean(vg, axis=-1, keepdims=True)
            vhat = dlt * lax.rsqrt(jnp.mean(dlt * dlt, axis=-1, keepdims=True) + EPS)
            vn = (vhat * lng_ref[:, cs] + lnb_ref[:, cs]).astype(BF16)
            gate = ug * (zb * _sigmoid(zb))
            for c in range(nch):
                rs = slice(CH * c, CH * (c + 1))
                sp = jnp.dot(wc_ref[s], vn[rs], preferred_element_type=F32) + bsb_ref[s]
                mix_ref[rs, pl.ds(SLAB + CH * s, CH)] = (gate[rs] * sp).astype(BF16)

        @pl.when(pl.program_id(0) == nt - 1)
        def _():
            gather.near_ready()
            gather.far_ready()
            gather.drain()

    sems = _gather_sems(nw)
    outs = pl.pallas_call(
        body, grid=(nt,),
        in_specs=[pl.BlockSpec((tm, IN_DIM), lambda i: (i, 0)), _full((8, D)), _full((1, D)), _full((1, D)),
                  _full((HEADS, CH, CH)), _full((HEADS, CH, CH))] + [ANY] * nw,
        out_specs=[pl.BlockSpec((tm, MIX), lambda i: (i, 0))] + [ANY] * nw,
        out_shape=[jax.ShapeDtypeStruct((t, MIX), BF16)] + [jax.ShapeDtypeStruct(f.shape, f.dtype) for f in fulls],
        input_output_aliases={6 + w: 1 + w for w in range(nw)},
        scratch_shapes=[pltpu.VMEM((8, D), F32)] + sems,
        compiler_params=_cp(("arbitrary",), VMEM_LIMIT), name="mixer_fwd")(proj, cw8, lng, lnb, wc, bsb, *fulls)
    return outs[0], outs[1:]


def _mem_fwd(mem, gm, wkv_f):
    n_mem = mem.shape[0]

    def body(mem_ref, gm_ref, w_ref, k_ref, v_ref):
        m, _ = _rms(mem_ref[...], gm_ref[...])
        mb = m.astype(BF16)
        for j in range(N_CHIP):
            dst = k_ref if j < 2 else v_ref
            dst[:, pl.ds(KV_BLK * (j % 2), KV_BLK)] = jnp.dot(mb, w_ref[j], preferred_element_type=F32).astype(BF16)

    return pl.pallas_call(
        body, out_shape=[jax.ShapeDtypeStruct((n_mem, D), BF16), jax.ShapeDtypeStruct((n_mem, D), BF16)],
        compiler_params=_cp(None, VMEM_LIMIT), name="mem_fwd")(mem, gm, wkv_f)


def _tail(x, tgt, mixin, wout, wq, wxo, k, v, g2, g3, tm=512, sub=512):
    t = x.shape[0]
    n_mem = k.shape[0]
    scale = 1.0 / math.sqrt(XD)

    def body(x_ref, tgt_ref, mix_ref, wout_ref, wq_ref, wxo_ref, k_ref, v_ref, g2_ref, g3_ref,
             loss_ref, dmix_ref, dx1b_ref, h2_ref, dq_ref, o_ref, dx2b_ref, dk_ref, dv_ref, dg2_ref, dg3_ref):
        @pl.when(pl.program_id(0) == 0)
        def _():
            loss_ref[...] = jnp.zeros_like(loss_ref)
            dk_ref[...] = jnp.zeros_like(dk_ref)
            dv_ref[...] = jnp.zeros_like(dv_ref)
            dg2_ref[...] = jnp.zeros_like(dg2_ref)
            dg3_ref[...] = jnp.zeros_like(dg3_ref)

        g2, g3 = g2_ref[...], g3_ref[...]
        for sb in range(tm // sub):
            rs = pl.ds(sub * sb, sub)
            x1 = x_ref[rs, :] + jnp.dot(mix_ref[rs, :], wout_ref[...], preferred_element_type=F32)
            h2, r2 = _rms(x1, g2)
            h2b = h2.astype(BF16)
            h2_ref[rs, :] = h2b
            q = jnp.dot(h2b, wq_ref[...], preferred_element_type=F32).astype(BF16)
            probs, outs = [], []
            for hd in range(XH):
                hs = pl.ds(XD * hd, XD)
                s = _bdot_nt(q[:, XD * hd:XD * (hd + 1)], k_ref[:, hs]) * scale
                e = jnp.exp(s - jnp.max(s, axis=-1, keepdims=True))
                p = e / jnp.sum(e, axis=-1, keepdims=True)
                probs.append(p)
                outs.append(_bdot(p, v_ref[:, hs]))
            ob = jnp.concatenate(outs, axis=-1).astype(BF16)
            o_ref[rs, :] = ob
            x2 = x1 + jnp.dot(ob, wxo_ref[...], preferred_element_type=F32)
            y, r3 = _rms(x2, g3)
            diff = y - tgt_ref[rs, :]
            row_loss = jnp.sum(diff * diff, axis=-1, keepdims=True)
            loss_ref[...] += jnp.broadcast_to(jnp.sum(row_loss, axis=0, keepdims=True) * (0.5 / D), loss_ref.shape)

            dx2, dg3 = _rms_bwd(diff * (1.0 / D), x2, r3, g3)
            dg3_ref[...] += dg3
            dx2b = dx2.astype(BF16)
            dx2b_ref[rs, :] = dx2b
            do = _bdot_nt(dx2b, wxo_ref[...])
            dqs = []
            for hd in range(XH):
                hs = pl.ds(XD * hd, XD)
                p = probs[hd]
                do_h = do[:, XD * hd:XD * (hd + 1)]
                dv_ref[:, hs] += _bdot_tn(p, do_h)
                dp = _bdot_nt(do_h, v_ref[:, hs])
                ds = p * (dp - jnp.sum(dp * p, axis=-1, keepdims=True))
                dqs.append(_bdot(ds, k_ref[:, hs]) * scale)
                dk_ref[:, hs] += _bdot_tn(ds, q[:, XD * hd:XD * (hd + 1)]) * scale
            dq = jnp.concatenate(dqs, axis=-1).astype(BF16)
            dq_ref[rs, :] = dq
            dx1n, dg2 = _rms_bwd(_bdot_nt(dq, wq_ref[...]), x1, r2, g2)
            dg2_ref[...] += dg2
            dx1b = (dx2 + dx1n).astype(BF16)
            dx1b_ref[rs, :] = dx1b
            dmix_ref[rs, :] = _bdot_nt(dx1b, wout_ref[...]).astype(BF16)

    tok = lambda w: pl.BlockSpec((tm, w), lambda i: (i, 0))
    return pl.pallas_call(
        body, grid=(t // tm,),
        in_specs=[tok(D), tok(D), tok(MIX), _full((MIX, D), 1), _full((D, D), 1), _full((D, D), 1),
                  _full((n_mem, D), 1), _full((n_mem, D), 1), _full((1, D)), _full((1, D))],
        out_specs=[_full((8, 128)), tok(MIX), tok(D), tok(D), tok(D), tok(D), tok(D),
                   _full((n_mem, D)), _full((n_mem, D)), _full((1, D)), _full((1, D))],
        out_shape=[jax.ShapeDtypeStruct((8, 128), F32), jax.ShapeDtypeStruct((t, MIX), BF16),
                   jax.ShapeDtypeStruct((t, D), BF16),
                   jax.ShapeDtypeStruct((t, D), BF16), jax.ShapeDtypeStruct((t, D), BF16),
                   jax.ShapeDtypeStruct((t, D), BF16), jax.ShapeDtypeStruct((t, D), BF16),
                   jax.ShapeDtypeStruct((n_mem, D), F32), jax.ShapeDtypeStruct((n_mem, D), F32),
                   jax.ShapeDtypeStruct((1, D), F32), jax.ShapeDtypeStruct((1, D), F32)],
        compiler_params=_cp(("arbitrary",), VMEM_LIMIT), name="tail")(x, tgt, mixin, wout, wq, wxo, k, v, g2, g3)


def _mem_bwd(mem, gm, dk, dv, wkv_f):
    def body(mem_ref, gm_ref, dk_ref, dv_ref, w_ref, dw_ref, dwb_ref, dgm_ref):
        mem_v = mem_ref[...]
        m, rm = _rms(mem_v, gm_ref[...])
        mb = m.astype(BF16)
        dm = jnp.zeros_like(mem_v)
        for j in range(N_CHIP):
            src = dk_ref if j < 2 else dv_ref
            dkv = src[:, pl.ds(KV_BLK * (j % 2), KV_BLK)].astype(BF16)
            dw = _bdot_tn(mb, dkv)
            dw_ref[j] = dw
            dwb_ref[j] = dw.astype(BF16)
            dm = dm + _bdot_nt(dkv, w_ref[j])
        dgm_ref[...] = jnp.sum(dm * mem_v * rm, axis=0, keepdims=True)

    return pl.pallas_call(
        body, out_shape=[jax.ShapeDtypeStruct((N_CHIP, D, KV_BLK), F32), jax.ShapeDtypeStruct((N_CHIP, D, KV_BLK), BF16),
                         jax.ShapeDtypeStruct((1, D), F32)],
        compiler_params=_cp(None, VMEM_LIMIT), name="mem_bwd")(mem, gm, dk, dv, wkv_f)


def _mixer_bwd(proj, dmix, cw8, lng, lnb, wc, wct, bsb, win_f, x, dx1, g1, after, tm=256):
    t = proj.shape[0]
    nt = t // tm
    nch = tm // CH
    hb = 16
    pair = 2 * CH

    def body(p_ref, pgc_ref, pxa_ref, dm_ref, cw_ref, lng_ref, lnb_ref, wc_ref, wct_ref, bsb_ref, w_ref, x_ref,
             dx1_ref, g1_ref, after_ref, dp_ref, dcw_ref, dlng_ref, dlnb_ref, dwc_ref, dbs_ref, gx_ref, dg1_ref,
             next_ref, dh_ref):
        i = pl.program_id(0)

        @pl.when(i == 0)
        def _():
            next_ref[...] = jnp.zeros_like(next_ref)
            dcw_ref[...] = jnp.zeros_like(dcw_ref)
            dlng_ref[...] = jnp.zeros_like(dlng_ref)
            dlnb_ref[...] = jnp.zeros_like(dlnb_ref)
            dwc_ref[...] = jnp.zeros_like(dwc_ref)
            dbs_ref[...] = jnp.zeros_like(dbs_ref)
            dg1_ref[...] = jnp.zeros_like(dg1_ref)

        first_tile = i == nt - 1
        rows = lax.broadcasted_iota(jnp.int32, (tm, CH), 0)
        ones8 = jnp.ones((8, CH), BF16)
        for s in range(HEADS):
            cs = pl.ds(CH * s, CH)

            def slab(k):
                return p_ref[:, pl.ds(k * SLAB + CH * s, CH)].astype(F32)

            gb, gc, xa, za = slab(0), slab(1), slab(2), slab(3)
            da = dm_ref[:, cs].astype(F32)
            cx = gc * xa
            cxp = pgc_ref[:, cs].astype(F32) * pxa_ref[:, cs].astype(F32)
            cxp = jnp.where(first_tile, jnp.zeros_like(cxp), cxp)
            p6 = jnp.broadcast_to(cxp[hb - 2:hb - 1, :], (tm, CH))
            p7 = jnp.broadcast_to(cxp[hb - 1:hb, :], (tm, CH))
            c1 = jnp.where(rows == 0, p7, pltpu.roll(cx, 1, 0))
            c2 = jnp.where(rows == 0, p6, jnp.where(rows == 1, p7, pltpu.roll(cx, 2, 0)))
            w0, w1, w2 = cw_ref[0:1, cs], cw_ref[1:2, cs], cw_ref[2:3, cs]
            cv = w0 * c2 + w1 * c1 + w2 * cx
            sg = _sigmoid(za)
            sa = za * sg
            dcv = da * gb * sa
            dp_ref[:, pl.ds(0 * SLAB + CH * s, CH)] = (da * cv * sa).astype(BF16)
            dp_ref[:, pl.ds(3 * SLAB + CH * s, CH)] = (da * gb * cv * (sg * (1.0 + za * (1.0 - sg)))).astype(BF16)
            n0 = jnp.broadcast_to(next_ref[0:1, cs], (tm, CH))
            n1 = jnp.broadcast_to(next_ref[1:2, cs], (tm, CH))
            u1 = jnp.where(rows == tm - 1, n0, pltpu.roll(dcv, tm - 1, 0))
            u2 = jnp.where(rows == tm - 2, n0, jnp.where(rows == tm - 1, n1, pltpu.roll(dcv, tm - 2, 0)))
            next_ref[:, cs] = dcv[0:8, :]
            dcx = w2 * dcv + w1 * u1 + w0 * u2
            dp_ref[:, pl.ds(1 * SLAB + CH * s, CH)] = (dcx * xa).astype(BF16)
            dp_ref[:, pl.ds(2 * SLAB + CH * s, CH)] = (dcx * gc).astype(BF16)
            dcw_ref[0:1, cs] += jnp.sum(dcv * c2, axis=0, keepdims=True)
            dcw_ref[1:2, cs] += jnp.sum(dcv * c1, axis=0, keepdims=True)
            dcw_ref[2:3, cs] += jnp.sum(dcv * cx, axis=0, keepdims=True)

            u, v, zb = slab(4), slab(5), slab(6)
            db = dm_ref[:, pl.ds(SLAB + CH * s, CH)].astype(F32)
            ug, ugrad = _gelu_parts(u)
            vg, vgrad = _gelu_parts(v)
            dlt = vg - jnp.mean(vg, axis=-1, keepdims=True)
            rstd = lax.rsqrt(jnp.mean(dlt * dlt, axis=-1, keepdims=True) + EPS)
            vhat = dlt * rstd
            lg = lng_ref[:, cs]
            vn = (vhat * lg + lnb_ref[:, cs]).astype(BF16)
            sgb = _sigmoid(zb)
            szb = zb * sgb
            sps, dvns = [], []
            dbs = jnp.zeros((8, CH), F32)
            dwc = jnp.zeros((CH, CH), F32)
            for c in range(nch):
                rs = slice(CH * c, CH * (c + 1))
                sp = jnp.dot(wc_ref[s], vn[rs], preferred_element_type=F32) + bsb_ref[s]
                dsp = (db[rs] * ug[rs] * szb[rs]).astype(BF16)
                dbs = dbs + lax.dot_general(ones8, dsp, (((1,), (1,)), ((), ())), preferred_element_type=F32)
                dwc = dwc + lax.dot_general(dsp, vn[rs], (((1,), (1,)), ((), ())), preferred_element_type=F32)
                dvns.append(jnp.dot(wct_ref[s], dsp, preferred_element_type=F32))
                sps.append(sp)
            sp = jnp.concatenate(sps, axis=0)
            dvn = jnp.concatenate(dvns, axis=0)
            dbs_ref[:, cs] += dbs
            dwc_ref[s] += dwc
            dlng_ref[:, cs] += jnp.sum(dvn * vhat, axis=0, keepdims=True)
            dlnb_ref[:, cs] += jnp.sum(dvn, axis=0, keepdims=True)
            dvhat = dvn * lg
            dvg = rstd * (dvhat - jnp.mean(dvhat, axis=-1, keepdims=True)
                          - vhat * jnp.mean(dvhat * vhat, axis=-1, keepdims=True))
            dp_ref[:, pl.ds(4 * SLAB + CH * s, CH)] = (db * sp * szb * ugrad).astype(BF16)
            dp_ref[:, pl.ds(5 * SLAB + CH * s, CH)] = (dvg * vgrad).astype(BF16)
            dp_ref[:, pl.ds(6 * SLAB + CH * s, CH)] = (db * ug * sp * (sgb * (1.0 + zb * (1.0 - sgb)))).astype(BF16)

            if s % 2 == 1:
                part = None
                for k in range(N_SLAB):
                    col = k * SLAB + pair * (s // 2)
                    blk, off = divmod(col, IN_BLK)
                    term = lax.dot_general(dp_ref[:, pl.ds(col, pair)], w_ref[blk, :, pl.ds(off, pair)],
                                           (((1,), (1,)), ((), ())), preferred_element_type=F32)
                    part = term if part is None else part + term
                if s == 1:
                    dh_ref[...] = part
                else:
                    dh_ref[...] += part

        xv = x_ref[...]
        r = lax.rsqrt(jnp.mean(xv * xv, axis=-1, keepdims=True) + EPS)
        dxn, dg = _rms_bwd(dh_ref[...], xv, r, g1_ref[...])
        gx_ref[...] = dx1_ref[...].astype(F32) + dxn
        dg1_ref[0:1, :] += dg

        @pl.when(i == nt - 1)
        def _():
            tril = lax.broadcasted_iota(jnp.int32, (CH, CH), 0) >= lax.broadcasted_iota(jnp.int32, (CH, CH), 1)
            for s in range(HEADS):
                dwc_ref[s] = jnp.where(tril, dwc_ref[s], 0.0)

    rev = lambda i: nt - 1 - i
    halo = lambda col: pl.BlockSpec((hb, SLAB), lambda i: (jnp.maximum(rev(i) * (tm // hb) - 1, 0), col))
    tok = lambda w: pl.BlockSpec((tm, w), lambda i: (rev(i), 0))
    return pl.pallas_call(
        body, grid=(nt,),
        in_specs=[tok(IN_DIM), halo(1), halo(2), tok(MIX), _full((8, D)), _full((1, D)), _full((1, D)),
                  _full((HEADS, CH, CH)), _full((HEADS, CH, CH)), _full((HEADS, CH, CH)),
                  _full((N_CHIP, D, IN_BLK), 1), tok(D), tok(D), _full((1, D)), ANY],
        out_specs=[tok(IN_DIM), _full((8, D)), _full((1, D)), _full((1, D)), _full((HEADS, CH, CH)), _full((8, D)),
                   tok(D), _full((8, D))],
        out_shape=[jax.ShapeDtypeStruct((t, IN_DIM), BF16), jax.ShapeDtypeStruct((8, D), F32),
                   jax.ShapeDtypeStruct((1, D), F32), jax.ShapeDtypeStruct((1, D), F32),
                   jax.ShapeDtypeStruct((HEADS, CH, CH), F32), jax.ShapeDtypeStruct((8, D), F32),
                   jax.ShapeDtypeStruct((t, D), F32), jax.ShapeDtypeStruct((8, D), F32)],
        scratch_shapes=[pltpu.VMEM((8, D), F32), pltpu.VMEM((tm, D), F32)],
        compiler_params=_cp(("arbitrary",), VMEM_LIMIT), name="mixer_bwd")(
            proj, proj, proj, dmix, cw8, lng, lnb, wc, wct, bsb, win_f, x, dx1, g1, after)


def _grad_matmul(a, b, after, *, by_cols, name, tk=1024, part=(0, 1)):
    t, m = a.shape
    n = b.shape[1]
    nk = t // tk
    nj = N_CHIP if by_cols else 1
    p, parts = part
    assert by_cols or parts == 1
    bn = n // nj // parts

    def body(a_ref, b_ref, after_ref, o_ref, ob_ref):
        kk = pl.program_id(1)
        part = lax.dot_general(a_ref[...], b_ref[...], (((0,), (0,)), ((), ())), preferred_element_type=F32)

        @pl.when(kk == 0)
        def _():
            o_ref[...] = part

        @pl.when(kk > 0)
        def _():
            o_ref[...] += part

        @pl.when(kk == nk - 1)
        def _():
            ob_ref[...] = o_ref[...].astype(BF16)

    a_spec = pl.BlockSpec((tk, m), lambda j, k: (k, 0))
    b_spec = pl.BlockSpec((tk, bn), lambda j, k: (k, j * parts + p))
    o_spec = pl.BlockSpec((None, m, bn), lambda j, k: (j, 0, 0))
    o32, o16 = pl.pallas_call(
        body, grid=(nj, nk), in_specs=[a_spec, b_spec, ANY], out_specs=[o_spec, o_spec],
        out_shape=[jax.ShapeDtypeStruct((nj, m, bn), F32), jax.ShapeDtypeStruct((nj, m, bn), BF16)],
        compiler_params=_cp(("parallel", "arbitrary"), VMEM_LIMIT), name=name)(a, b, after)
    if by_cols:
        return o32, o16
    return o32.reshape(N_CHIP, m // N_CHIP, n), o16.reshape(N_CHIP, m // N_CHIP, n)


def _coords():
    x, y, c = lax.axis_index("x"), lax.axis_index("y"), lax.axis_index("c")
    chips = [(1 - x, y), (x, 1 - y), (1 - x, 1 - y)]
    return x, y, c, chips


def _pair_reduce(c_idx, grads, grads_b, smalls, name):
    ng, ns = len(grads), len(smalls)
    halves = [g.shape[1] // 2 for g in grads]

    def body(c_ref, *refs):
        g_in, gb_any = refs[:ng], refs[ng:2 * ng]
        s_own, s_any = refs[2 * ng:2 * ng + ns], refs[2 * ng + ns:2 * ng + 2 * ns]
        o = refs[2 * ng + 2 * ns:4 * ng + 3 * ns]
        lands = refs[4 * ng + 3 * ns:5 * ng + 4 * ns]
        send, recv = refs[5 * ng + 4 * ns:]
        x, y, c, _ = _coords()
        j = pl.program_id(0)

        def big(i, blk):
            return pltpu.make_async_remote_copy(
                src_ref=gb_any[i].at[blk, pl.ds((1 - c) * halves[i], halves[i])], dst_ref=lands[i].at[blk],
                send_sem=send.at[i, blk], recv_sem=recv.at[i, blk], device_id=(x, y, 1 - c), device_id_type=MESH)

        def small(i):
            return pltpu.make_async_remote_copy(
                src_ref=s_any[i].at[1 - c], dst_ref=lands[ng + i],
                send_sem=send.at[ng + i, 0], recv_sem=recv.at[ng + i, 0], device_id=(x, y, 1 - c), device_id_type=MESH)

        @pl.when(j == 0)
        def _():
            for blk in range(N_CHIP):
                for i in range(ng):
                    big(i, blk).start()
            for i in range(ns):
                small(i).start()

        for i in range(ng):
            big(i, j).wait_recv()
            tot = g_in[i][...] + lands[i][j].astype(F32)
            o[i][...] = tot
            o[ng + i][...] = tot.astype(BF16)

        @pl.when(j == N_CHIP - 1)
        def _():
            for i in range(ns):
                small(i).wait_recv()
                o[2 * ng + i][...] = s_own[i][...] + lands[ng + i][...]
                small(i).wait_send()
            for blk in range(N_CHIP):
                for i in range(ng):
                    big(i, blk).wait_send()

    in_specs = [pl.BlockSpec((None, None, halves[i], g.shape[2]), lambda b, c: (b, c[0], 0, 0)) for i, g in enumerate(grads)]
    in_specs += [ANY] * ng
    in_specs += [pl.BlockSpec((None, s.shape[0] // 2, s.shape[1]), lambda b, c: (c[0], 0, 0)) for s in smalls]
    in_specs += [ANY] * ns
    blk = [pl.BlockSpec((None, halves[i], g.shape[2]), lambda b, c: (b, 0, 0)) for i, g in enumerate(grads)]
    out_specs = blk + blk + [pl.BlockSpec((s.shape[0] // 2, s.shape[1]), lambda b, c: (0, 0)) for s in smalls]
    out_shape = [jax.ShapeDtypeStruct((N_CHIP, halves[i], g.shape[2]), F32) for i, g in enumerate(grads)]
    out_shape += [jax.ShapeDtypeStruct((N_CHIP, halves[i], g.shape[2]), BF16) for i, g in enumerate(grads)]
    out_shape += [jax.ShapeDtypeStruct((s.shape[0] // 2, s.shape[1]), F32) for s in smalls]
    scratch = [pltpu.VMEM((N_CHIP, halves[i], g.shape[2]), BF16) for i, g in enumerate(grads)]
    scratch += [pltpu.VMEM((s.shape[0] // 2, s.shape[1]), F32) for s in smalls]
    scratch += [pltpu.SemaphoreType.DMA((ng + ns, N_CHIP)), pltpu.SemaphoreType.DMA((ng + ns, N_CHIP))]
    grads4 = [g.reshape(N_CHIP, 2, halves[i], g.shape[2]) for i, g in enumerate(grads)]
    smalls3 = [s.reshape(2, s.shape[0] // 2, s.shape[1]) for s in smalls]
    return pl.pallas_call(
        body, out_shape=out_shape,
        grid_spec=pltpu.PrefetchScalarGridSpec(num_scalar_prefetch=1, grid=(N_CHIP,), in_specs=in_specs,
                                               out_specs=out_specs, scratch_shapes=scratch),
        compiler_params=_cp(("arbitrary",), VMEM_LIMIT), name=name)(c_idx, *grads4, *grads_b, *smalls3, *smalls3)


_HBM = pl.BlockSpec(memory_space=pltpu.HBM)
_SEM = pl.BlockSpec(memory_space=pltpu.SEMAPHORE)


def _split_copies(ins, lands, ng, send, recv, arriving):
    x, y, c, chips = _coords()
    b = 2 * x + y
    copies = []
    for i in range(len(ins)):
        for k in range(3):
            blk = 2 * chips[k][0] + chips[k][1]
            src, dst, got = (ins[i].at[blk], lands[i].at[k], lands[i].at[k]) if i < ng else (ins[i], lands[i].at[b], lands[i].at[blk])
            sems = dict(send_sem=send.at[3 * i + k], recv_sem=recv.at[3 * i + k], device_id=(*chips[k], c), device_id_type=MESH)
            if arriving:
                copies.append(pltpu.make_async_remote_copy(src_ref=got, dst_ref=got, **sems))
            else:
                copies.append(pltpu.make_async_remote_copy(src_ref=src, dst_ref=dst, **sems))
    return copies


def _exchange_begin(sums_b, smalls, name):
    ng, n = len(sums_b), len(sums_b) + len(smalls)
    srcs = list(sums_b) + list(smalls)
    lands = [lax.empty((3,) + g.shape[1:], g.dtype) for g in sums_b] + [lax.empty((N_CHIP,) + s.shape, s.dtype) for s in smalls]

    def body(*refs):
        ins, land_refs = refs[:n], refs[n:2 * n]
        send, recv = refs[2 * n], refs[2 * n + 1]
        token = refs[4 * n + 2]
        for cp in _split_copies(ins, land_refs, ng, send, recv, False):
            cp.start()
        token[...] = jnp.zeros_like(token)

    hbm = lambda a: pltpu.HBM(a.shape, a.dtype)
    outs = pl.pallas_call(
        body, name=name,
        out_shape=(pltpu.SemaphoreType.DMA((3 * n,)), pltpu.SemaphoreType.DMA((3 * n,)), *[hbm(a) for a in srcs + lands],
                   jax.ShapeDtypeStruct((8, 128), F32)),
        in_specs=[_HBM] * (2 * n), out_specs=(_SEM, _SEM, *[_HBM] * (2 * n), pl.BlockSpec(memory_space=pltpu.VMEM)),
        input_output_aliases={i: 2 + i for i in range(2 * n)},
        compiler_params=pltpu.CompilerParams(has_side_effects=pltpu.SideEffectType.DATAFLOW_SIDE_EFFECTING),
    )(*[pltpu.with_memory_space_constraint(a, pltpu.HBM) for a in srcs + lands])
    return outs[0], outs[1], list(outs[2:2 + n]), list(outs[2 + n:2 + 2 * n]), outs[2 + 2 * n]


def _exchange_end(send, recv, srcs, lands, ng, after, name):
    n = len(srcs)
    after = list(after)

    def body(*refs):
        ins, land_refs = refs[:n], refs[n:2 * n]
        send_ref, recv_ref = refs[2 * n], refs[2 * n + 1]
        for cp in _split_copies(ins, land_refs, ng, send_ref, recv_ref, False):
            cp.wait_send()
        for cp in _split_copies(ins, land_refs, ng, send_ref, recv_ref, True):
            cp.wait_recv()

    hbm = lambda a: pltpu.HBM(a.shape, a.dtype)
    outs = pl.pallas_call(
        body, name=name, out_shape=tuple(hbm(a) for a in list(srcs) + list(lands)),
        in_specs=[_HBM] * (2 * n) + [_SEM, _SEM] + [ANY] * len(after), out_specs=tuple([_HBM] * (2 * n)),
        input_output_aliases={i: i for i in range(2 * n)},
        compiler_params=pltpu.CompilerParams(has_side_effects=pltpu.SideEffectType.DATAFLOW_SIDE_EFFECTING),
    )(*srcs, *lands, send, recv, *after)
    return list(outs[n:])


def _chip_reduce(bc_idx, sums, recvd, smalls_slots, smalls_own, name, steps=4):
    ng, ns = len(sums), len(smalls_slots)
    n = ng + ns
    assert steps >= 2
    halves = [g.shape[1] for g in sums] + [s.shape[1] for s in smalls_slots]
    rows = [g.shape[1] // steps for g in sums]

    def body(bc_ref, *refs):
        own, rx = refs[:ng], refs[ng:2 * ng]
        sl = refs[2 * ng:2 * ng + ns]
        sl_own = refs[2 * ng + ns:2 * ng + 2 * ns]
        o = refs[2 * ng + 2 * ns:2 * ng + 2 * ns + n]
        tiles = refs[2 * ng + 2 * ns + n:2 * ng + 2 * ns + 2 * n]
        keep, send, recv = refs[2 * ng + 2 * ns + 2 * n:]
        x, y, c, _ = _coords()
        sibling = dict(device_id=(x, y, 1 - c), device_id_type=MESH)
        r = pl.program_id(0)

        def writes(i, step, slot):
            dst = o[i].at[pl.ds(c * halves[i] + step * rows[i], rows[i])]
            return (pltpu.make_async_copy(tiles[i].at[slot], dst, keep.at[i, slot]),
                    pltpu.make_async_remote_copy(src_ref=tiles[i].at[slot], dst_ref=dst, send_sem=send.at[i, slot],
                                                 recv_sem=recv.at[i, step], **sibling))

        def small_writes(i):
            dst = o[i].at[pl.ds(c * halves[i], halves[i])]
            return (pltpu.make_async_copy(tiles[i], dst, keep.at[i, 0]),
                    pltpu.make_async_remote_copy(src_ref=tiles[i], dst_ref=dst, send_sem=send.at[i, 0],
                                                 recv_sem=recv.at[i, 0], **sibling))

        def arriving(i, step, nrows):
            dst = o[i].at[pl.ds((1 - c) * halves[i] + step * nrows, nrows)]
            return pltpu.make_async_remote_copy(src_ref=dst, dst_ref=dst, send_sem=send.at[i, 0], recv_sem=recv.at[i, step],
                                                **sibling)

        def finish(step, slot):
            for i in range(ng):
                local, remote = writes(i, step, slot)
                local.wait()
                remote.wait_send()

        @pl.when(r >= 2)
        def _():
            finish(r - 2, r % 2)

        for i in range(ng):
            tot = own[i][...]
            for j in range(3):
                tot = tot + rx[i][j].astype(F32)
            tiles[i][r % 2] = tot
            for cp in writes(i, r, r % 2):
                cp.start()

        @pl.when(r == 0)
        def _():
            for i in range(ns):
                term = [jnp.where(bc_ref[0] == kk, sl_own[i][...], sl[i][kk]) for kk in range(N_CHIP)]
                tiles[ng + i][...] = ((term[0] + term[1]) + term[2]) + term[3]
                for cp in small_writes(ng + i):
                    cp.start()

        @pl.when(r == steps - 1)
        def _():
            finish(steps - 2, (steps - 2) % 2)
            finish(steps - 1, (steps - 1) % 2)
            for i in range(ns):
                local, remote = small_writes(ng + i)
                local.wait()
                remote.wait_send()
                arriving(ng + i, 0, halves[ng + i]).wait_recv()
            for i in range(ng):
                for step in range(steps):
                    arriving(i, step, rows[i]).wait_recv()

    in_specs = [pl.BlockSpec((None, rows[i], g.shape[2]), lambda r, bc: (bc[0], r, 0)) for i, g in enumerate(sums)]
    in_specs += [pl.BlockSpec((3, rows[i], g.shape[2]), lambda r, bc: (0, r, 0)) for i, g in enumerate(sums)]
    in_specs += [pl.BlockSpec(s.shape, lambda r, bc: (0, 0, 0)) for s in smalls_slots]
    in_specs += [pl.BlockSpec(s.shape[1:], lambda r, bc: (0, 0)) for s in smalls_slots]
    out_shape = [jax.ShapeDtypeStruct((2 * g.shape[1], g.shape[2]), F32) for g in sums]
    out_shape += [jax.ShapeDtypeStruct((2 * s.shape[1], s.shape[2]), F32) for s in smalls_slots]
    scratch = [pltpu.VMEM((2, rows[i], g.shape[2]), F32) for i, g in enumerate(sums)]
    scratch += [pltpu.VMEM(s.shape[1:], F32) for s in smalls_slots]
    scratch += [pltpu.SemaphoreType.DMA((n, 2)), pltpu.SemaphoreType.DMA((n, 2)), pltpu.SemaphoreType.DMA((n, steps))]
    return list(pl.pallas_call(
        body, out_shape=out_shape,
        grid_spec=pltpu.PrefetchScalarGridSpec(num_scalar_prefetch=1, grid=(steps,), in_specs=in_specs,
                                               out_specs=[ANY] * n, scratch_shapes=scratch),
        compiler_params=_cp(("arbitrary",), VMEM_LIMIT), name=name)(bc_idx, *sums, *recvd, *smalls_slots, *smalls_own))


def _adamw_math(w, g, m, v):
    m2 = ADAM_B1 * m + (1.0 - ADAM_B1) * g
    v2 = ADAM_B2 * v + (1.0 - ADAM_B2) * (g * g)
    m_hat = m2 / (1.0 - ADAM_B1 ** ADAM_STEP)
    v_hat = v2 / (1.0 - ADAM_B2 ** ADAM_STEP)
    delta = -ADAM_LR * (m_hat / (jnp.sqrt(v_hat) + ADAM_EPS) + ADAM_WD * w)
    return delta, m2, v2


def _adamw_big(ws, gs, ms, vs, name, steps=8):
    n = len(ws)

    def body(*refs):
        for i in range(n):
            w_ref, g_ref, m_ref, v_ref = (refs[k * n + i] for k in range(4))
            d_ref, m2_ref, v2_ref, g2_ref = (refs[(4 + k) * n + i] for k in range(4))
            gv = g_ref[...]
            d_ref[...], m2_ref[...], v2_ref[...] = _adamw_math(w_ref[...], gv, m_ref[...], v_ref[...])
            g2_ref[...] = gv

    specs = [pl.BlockSpec((w.shape[0] // steps, w.shape[1]), lambda i: (i, 0)) for w in ws]
    shapes = [jax.ShapeDtypeStruct(w.shape, F32) for w in ws]
    outs = pl.pallas_call(
        body, grid=(steps,), in_specs=specs * 4, out_specs=specs * 4, out_shape=shapes * 4,
        compiler_params=_cp(("parallel",), VMEM_LIMIT), name=name)(*ws, *gs, *ms, *vs)
    return [tuple(outs[k * n + i] for k in range(4)) for i in range(n)]


def _adamw_cols(w, g, m, v, part, prev, name, steps=8):
    p, parts = part
    rows, cols = w.shape[0] // steps, w.shape[1] // parts
    n_prev = 0 if prev is None else 4

    def body(w_ref, g_ref, m_ref, v_ref, *rest):
        d_ref, m2_ref, v2_ref, g2_ref = rest[n_prev:]
        gv = g_ref[...]
        d_ref[...], m2_ref[...], v2_ref[...] = _adamw_math(w_ref[...], gv, m_ref[...], v_ref[...])
        g2_ref[...] = gv

    at_p = pl.BlockSpec((rows, cols), lambda i: (i, p))
    return tuple(pl.pallas_call(
        body, grid=(steps,), in_specs=[at_p, pl.BlockSpec((rows, cols), lambda i: (i, 0)), at_p, at_p] + [ANY] * n_prev,
        out_specs=[at_p] * 4, out_shape=[jax.ShapeDtypeStruct(w.shape, F32)] * 4,
        input_output_aliases={4 + k: k for k in range(n_prev)},
        compiler_params=_cp(("parallel",), VMEM_LIMIT), name=name)(w, g, m, v, *(prev or ())))


def _adamw_small(groups):
    n = len(groups)

    def body(*refs):
        for i in range(n):
            w_ref, g_ref, m_ref, v_ref = refs[4 * i:4 * i + 4]
            d_ref, m2_ref, v2_ref = refs[4 * n + 3 * i:4 * n + 3 * i + 3]
            d_ref[...], m2_ref[...], v2_ref[...] = _adamw_math(w_ref[...], g_ref[...], m_ref[...], v_ref[...])

    flat = [a for grp in groups for a in grp]
    out_shape = [jax.ShapeDtypeStruct(grp[0].shape, F32) for grp in groups for _ in range(3)]
    outs = pl.pallas_call(body, out_shape=out_shape, name="adamw_small")(*flat)
    return [tuple(outs[3 * i:3 * i + 3]) for i in range(n)]


def kernel(x, mem, norm_mix_g, w_in, conv_w, gm_ln_g, gm_ln_b, gm_ws, gm_bs, w_out, norm_x_g, norm_mem_g, w_q, w_kv, w_xo, norm_final_g, loss_target, m_norm_mix_g, m_w_in, m_conv_w, m_gm_ln_g, m_gm_ln_b, m_gm_ws, m_gm_bs, m_w_out, m_norm_x_g, m_norm_mem_g, m_w_q, m_w_kv, m_w_xo, m_norm_final_g, v_norm_mix_g, v_w_in, v_conv_w, v_gm_ln_g, v_gm_ln_b, v_gm_ws, v_gm_bs, v_w_out, v_norm_x_g, v_norm_mem_g, v_w_q, v_w_kv, v_w_xo, v_norm_final_g):
    t = x.shape[1]
    xi = lax.axis_index("x")
    yi = lax.axis_index("y")
    ci = lax.axis_index("c")
    b_idx = jnp.reshape(2 * xi + yi, (1,)).astype(jnp.int32)
    c_idx = jnp.reshape(ci, (1,)).astype(jnp.int32)

    x2d, mem2d, tgt = x[0], mem[0], loss_target[0]
    big = [w_in[0], w_out[0], w_q[0], w_kv[0], w_xo[0]]
    big_m = [m_w_in[0], m_w_out[0], m_w_q[0], m_w_kv[0], m_w_xo[0]]
    big_v = [v_w_in[0], v_w_out[0], v_w_q[0], v_w_kv[0], v_w_xo[0]]
    g3 = norm_final_g.reshape(1, D)

    def pad8(a):
        return jnp.pad(a, ((0, 8 - a.shape[0]), (0, 0)))

    own_blocks = _cast_shards(b_idx, big)

    tril = jnp.tril(jnp.ones((CH, CH), bool))
    wc32 = jnp.where(tril[None], gm_ws[0], 0.0)
    wc = wc32.astype(BF16)
    wct = jnp.swapaxes(wc32, 1, 2).astype(BF16)
    bsb = jnp.broadcast_to(gm_bs[0][:, :, None], (HEADS, CH, CH))

    blk = 2 * xi + yi
    order = jnp.stack([blk, blk ^ 2, blk ^ 1, blk ^ 3]).astype(jnp.int32)
    proj, hb, win_f, cw8, (wout_f,) = _proj_gather(
        order, x2d, norm_mix_g, own_blocks[0], pad8(conv_w[0]), [own_blocks[1]])
    mixin, (wq_f, wkv_f, wxo_f) = _mixer_fwd(proj, cw8, gm_ln_g, gm_ln_b, wc, bsb, own_blocks[2:])
    wout2, wq2, wxo2 = wout_f.reshape(MIX, D), wq_f.reshape(D, D), wxo_f.reshape(D, D)
    k, v = _mem_fwd(mem2d, norm_mem_g, wkv_f)

    (loss_tile, dmix, dx1b, h2b, dq, ob, dx2b, dk, dv, dg2, dg3) = _tail(
        x2d, tgt, mixin, wout2, wq2, wxo2, k, v, norm_x_g, g3)
    dwkv, dwkv_b, dgm = _mem_bwd(mem2d, norm_mem_g, dk, dv, wkv_f)
    bc_idx = jnp.concatenate([b_idx, c_idx])

    dwxo, dwxo_b = _grad_matmul(ob, dx2b, dgm, by_cols=False, name="grad_w_xo", tk=2048)
    dwq, dwq_b = _grad_matmul(h2b, dq, dgm, by_cols=False, name="grad_w_q", tk=2048)
    dwout, dwout_b = _grad_matmul(mixin, dx1b, dgm, by_cols=False, name="grad_w_out")
    ps_a = _pair_reduce(c_idx, [dwout, dwq, dwkv, dwxo], [dwout_b, dwq_b, dwkv_b, dwxo_b], [], "pair_reduce_a")
    sums_a, sums_a_b = list(ps_a[:4]), list(ps_a[4:8])
    send_a, recv_a, src_a, land_a, token_a = _exchange_begin(sums_a_b, [], "exchange_a_begin")

    dproj, dcw, dlng, dlnb, dwc, dbs8, grad_x, dg1 = _mixer_bwd(
        proj, dmix, cw8, gm_ln_g, gm_ln_b, wc, wct, bsb, win_f, x2d, dx1b, norm_mix_g, token_a)

    zero = jnp.zeros((1, D), F32)
    loss_row = jnp.broadcast_to(loss_tile[0:1, 0:1], (1, D))
    sv = jnp.concatenate([dg1[0:1], dg2, dgm, dg3, dlng, dlnb, dbs8[0:1], loss_row, dcw], axis=0)
    sw = dwc.reshape(HEADS * CH, CH)
    dwin1, dwin1_b = _grad_matmul(hb, dproj, token_a, by_cols=True, name="grad_w_in_1", tk=2048, part=(0, 2))
    ps_b1 = _pair_reduce(c_idx, [dwin1], [dwin1_b], [sv, sw], "pair_reduce_b1")
    sums_b1, psmall = list(ps_b1[:1]), list(ps_b1[2:])
    send_b1, recv_b1, src_b1, land_b1, token_b1 = _exchange_begin(list(ps_b1[1:2]), psmall, "exchange_b1_begin")
    dwin2, dwin2_b = _grad_matmul(hb, dproj, token_b1, by_cols=True, name="grad_w_in_2", tk=2048, part=(1, 2))
    ps_b2 = _pair_reduce(c_idx, [dwin2], [dwin2_b], [], "pair_reduce_b2")
    sums_b2 = list(ps_b2[:1])
    send_b2, recv_b2, src_b2, land_b2, token_b2 = _exchange_begin(list(ps_b2[1:2]), [], "exchange_b2_begin")

    rx2a = _exchange_end(send_a, recv_a, src_a, land_a, 4, [token_b2], "exchange_a_end")
    g_a = _chip_reduce(bc_idx, sums_a, rx2a, [], [], "chip_reduce_a")
    out_a = _adamw_big(big[1:], g_a, big_m[1:], big_v[1:], "adamw_rest")

    rx2b1 = _exchange_end(send_b1, recv_b1, src_b1, land_b1, 1, [out_a[0][0]], "exchange_b1_end")
    gwin1, svf, swf = _chip_reduce(bc_idx, sums_b1, rx2b1[:1], rx2b1[1:], psmall, "chip_reduce_b1")
    out_b1 = _adamw_cols(big[0], gwin1, big_m[0], big_v[0], (0, 2), None, "adamw_w_in_1")

    def vec_pack(a1, a2, am, a3, lg, lb, bs):
        return jnp.concatenate([a1, a2, am, a3.reshape(1, D), lg, lb, bs.reshape(1, D), zero], axis=0)

    wv = vec_pack(norm_mix_g, norm_x_g, norm_mem_g, norm_final_g, gm_ln_g, gm_ln_b, gm_bs)
    mv = vec_pack(m_norm_mix_g, m_norm_x_g, m_norm_mem_g, m_norm_final_g, m_gm_ln_g, m_gm_ln_b, m_gm_bs)
    vv = vec_pack(v_norm_mix_g, v_norm_x_g, v_norm_mem_g, v_norm_final_g, v_gm_ln_g, v_gm_ln_b, v_gm_bs)
    loss = svf[7, 0]
    gcw = lax.dynamic_slice_in_dim(svf[8:16], blk * (D // N_CHIP), D // N_CHIP, axis=1)
    gws = swf
    gv = svf[0:8]
    (dv_, mv_, vv_), (dc_, mc_, vc_), (dws_, mws_, vws_) = _adamw_small([
        (wv, gv, mv, vv),
        (pad8(conv_w[0]), gcw, pad8(m_conv_w[0]), pad8(v_conv_w[0])),
        (gm_ws.reshape(HEADS * CH, CH), gws, m_gm_ws.reshape(HEADS * CH, CH), v_gm_ws.reshape(HEADS * CH, CH))])

    rx2b2 = _exchange_end(send_b2, recv_b2, src_b2, land_b2, 1, [out_b1[0], dv_], "exchange_b2_end")
    (gwin2,) = _chip_reduce(bc_idx, sums_b2, rx2b2, [], [], "chip_reduce_b2")
    out_b = _adamw_cols(big[0], gwin2, big_m[0], big_v[0], (1, 2), out_b1, "adamw_w_in_2")
    big_out = [out_b] + out_a

    def unpack(vecs, cw, ws, bigs):
        r = lambda i: vecs[i:i + 1]
        return [r(0), bigs[0][None], cw[0:3][None], r(4), r(5), ws.reshape(1, HEADS, CH, CH), vecs[6].reshape(1, HEADS, CH),
                bigs[1][None], r(1), r(2), bigs[2][None], bigs[3][None], bigs[4][None], vecs[3]]

    grads_out = unpack(gv, gcw, gws, [o[3] for o in big_out])
    delta_out = unpack(dv_, dc_, dws_, [o[0] for o in big_out])
    m_out = unpack(mv_, mc_, mws_, [o[1] for o in big_out])
    v_out = unpack(vv_, vc_, vws_, [o[2] for o in big_out])
    return (loss, grad_x[None], *grads_out, *delta_out, *m_out, *v_out)
```

```python
import functools
import math

import jax
import jax.numpy as jnp
from jax import lax
from jax.experimental import pallas as pl
from jax.experimental.pallas import tpu as pltpu

F32 = jnp.float32
BF16 = jnp.bfloat16
MESH = pl.DeviceIdType.MESH

D = 1024
SLAB = 1024
N_SLAB = 7
IN_DIM = N_SLAB * SLAB
MIX = 2 * SLAB
HEADS = 8
CH = 128
XH = 4
XD = D // XH
EPS = 1e-6
GELU_C = math.sqrt(2.0 / math.pi)
GELU_A = 0.044715
N_CHIP = 4
IN_BLK = IN_DIM // N_CHIP
IN_PIECE = 256
N_PIECE = IN_BLK // IN_PIECE
KV_BLK = 2 * D // N_CHIP

ADAM_LR, ADAM_B1, ADAM_B2, ADAM_EPS, ADAM_WD, ADAM_STEP = 0.001, 0.9, 0.999, 1e-08, 0.01, 10

VMEM_LIMIT = 60 * 1024 * 1024


def _cp(sem=None, vmem=None):
    return pltpu.CompilerParams(dimension_semantics=sem, vmem_limit_bytes=vmem)


def _full(shape, buffers=None):
    n = len(shape)
    if buffers is None:
        return pl.BlockSpec(shape, lambda *_: (0,) * n)
    return pl.BlockSpec(shape, lambda *_: (0,) * n, pipeline_mode=pl.Buffered(buffers))


ANY = pl.BlockSpec(memory_space=pl.ANY)


def _bdot(a, b):
    return jnp.dot(a.astype(BF16), b.astype(BF16), preferred_element_type=F32)


def _bdot_nt(a, b):
    return lax.dot_general(a.astype(BF16), b.astype(BF16), (((1,), (1,)), ((), ())), preferred_element_type=F32)


def _bdot_tn(a, b):
    return lax.dot_general(a.astype(BF16), b.astype(BF16), (((0,), (0,)), ((), ())), preferred_element_type=F32)


def _rms(x, g):
    r = lax.rsqrt(jnp.mean(x * x, axis=-1, keepdims=True) + EPS)
    return x * r * g, r


def _rms_bwd(dy, x, r, g):
    gdy = dy * g
    dx = r * gdy - x * (r * r * r) * jnp.mean(x * gdy, axis=-1, keepdims=True)
    dg = jnp.sum(dy * x * r, axis=0, keepdims=True)
    return dx, dg


def _gelu_parts(x):
    x2 = x * x
    t = jnp.tanh(GELU_C * (x + GELU_A * x * x2))
    val = 0.5 * x * (1.0 + t)
    grad = 0.5 * (1.0 + t) + 0.5 * x * (1.0 - t * t) * (GELU_C * (1.0 + 3.0 * GELU_A * x2))
    return val, grad


def _gelu(x):
    return 0.5 * x * (1.0 + jnp.tanh(GELU_C * (x + GELU_A * x * x * x)))


def _sigmoid(z):
    return 1.0 / (1.0 + jnp.exp(-z))


def _cast_shards(b_idx, arrs):
    n = len(arrs)
    steps = 8

    def body(b_ref, *refs):
        for p in range(N_PIECE):
            refs[n][p] = refs[0][:, pl.ds(p * IN_PIECE, IN_PIECE)].astype(BF16)
        for i in range(1, n):
            refs[n + i][...] = refs[i][...].astype(BF16)

    rows = [a.shape[0] // steps for a in arrs]
    in_specs = [pl.BlockSpec((rows[i], a.shape[1]), lambda i, b: (i, 0)) for i, a in enumerate(arrs)]
    out_specs = [pl.BlockSpec((None, N_PIECE, rows[0], IN_PIECE), lambda i, b: (b[0], 0, i, 0))]
    out_specs += [pl.BlockSpec((None, rows[i], a.shape[1]), lambda i, b: (b[0], i, 0)) for i, a in enumerate(arrs) if i > 0]
    out_shape = [jax.ShapeDtypeStruct((N_CHIP, N_PIECE, arrs[0].shape[0], IN_PIECE), BF16)]
    out_shape += [jax.ShapeDtypeStruct((N_CHIP,) + a.shape, BF16) for a in arrs[1:]]
    return pl.pallas_call(
        body, out_shape=out_shape,
        grid_spec=pltpu.PrefetchScalarGridSpec(num_scalar_prefetch=1, grid=(steps,), in_specs=in_specs, out_specs=out_specs),
        compiler_params=_cp(("arbitrary",)), name="cast_shards")(b_idx, *arrs)


def _proj_gather(seq, x, g, win_own, cw8s, more, tm=1024):
    t = x.shape[0]
    ni = t // tm
    nm = len(more)
    steps = N_CHIP * N_PIECE
    near0, far0 = N_PIECE, 3 * N_PIECE

    def body(*refs):
        seq_ref, x_any, g_ref, win_in, cw_in = refs[:5]
        o_ref, hb_any, win_f, cw_out = refs[5 + nm:9 + nm]
        more_out = refs[9 + nm:9 + 2 * nm]
        hbuf, xbuf, wv, cw_s, cw_r, loc = refs[9 + 2 * nm:15 + 2 * nm]
        g_in = _Gather([win_f.at[:, p] for p in range(N_PIECE)], *refs[15 + 2 * nm:19 + 2 * nm])
        g_more = _Gather(more_out, *refs[19 + 2 * nm:23 + 2 * nm])
        s = pl.program_id(0)
        x, y, c, chips = _coords()
        b = 2 * x + y
        blks = [2 * chip[0] + chip[1] for chip in chips]

        def cw_cols(blk):
            return cw_out.at[:, pl.ds(blk * (D // N_CHIP), D // N_CHIP)]

        def cw_copy(k, blk):
            src = cw_in if blk is None else cw_cols(blk)
            return pltpu.make_async_remote_copy(src_ref=src, dst_ref=cw_cols(b if blk is None else blk), send_sem=cw_s.at[k],
                                                recv_sem=cw_r.at[k], device_id=(*chips[k], c), device_id_type=MESH)

        cw_local = pltpu.make_async_copy(cw_in, cw_cols(b), loc.at[1])
        hb_copy = pltpu.make_async_copy(hbuf, hb_any, loc.at[0])

        def load(step):
            slot = lax.rem(step, 2)
            return pltpu.make_async_copy(win_f.at[seq_ref[0, step], seq_ref[1, step]], wv.at[slot], loc.at[2 + slot])

        def chunk(i):
            return pltpu.make_async_copy(x_any.at[pl.ds(i * tm, tm)], xbuf.at[i % 2], loc.at[4 + i % 2])

        def first():
            g_in.start()
            cw_local.start()
            for k in range(3):
                cw_copy(k, None).start()
            load(0).start()
            chunk(0).start()
            for i in range(ni):
                if i + 1 < ni:
                    chunk(i + 1).start()
                chunk(i).wait()
                h, _ = _rms(xbuf[i % 2], g_ref[...])
                hbuf[pl.ds(i * tm, tm), :] = h.astype(BF16)
            hb_copy.start()

        events = {step: [] for step in range(steps)}
        events[0].append(first)
        for p in range(N_PIECE):
            events[2 * p + 2].append(functools.partial(g_in.hop, [p]))
            events[near0 + 2 * p - 1].append(functools.partial(g_in.near_ready, [p]))
            events[far0 + p - 2].append(functools.partial(g_in.far, [p]))
            events[far0 + p - 1].append(functools.partial(g_in.far_ready, [p]))
        events[2 * N_PIECE + 1].append(g_more.start)
        for step, todo in events.items():
            if todo:
                @pl.when(s == step)
                def _(todo=todo):
                    for do in todo:
                        do()

        @pl.when(s + 1 < steps)
        def _():
            load(s + 1).start()

        load(s).wait()
        for i in range(ni):
            rows = pl.ds(i * tm, tm)
            o_ref[rows, :] = jnp.dot(hbuf[rows, :], wv[lax.rem(s, 2)], preferred_element_type=F32).astype(BF16)

        @pl.when(s == steps - 1)
        def _():
            g_more.hop()
            g_more.far()
            for k in range(3):
                cw_copy(k, blks[k]).wait_recv()
            for k in range(3):
                cw_copy(k, None).wait_send()
            cw_local.wait()
            hb_copy.wait()
            g_more.near_ready()
            g_more.far_ready()
            g_in.drain()
            g_more.drain()

    in_specs = [ANY, pl.BlockSpec((1, D), lambda s, q: (0, 0)), ANY, ANY] + [ANY] * nm
    out_specs = [pl.BlockSpec((t, IN_PIECE), lambda s, q: (0, q[2, s])), ANY, ANY, ANY] + [ANY] * nm
    outs = pl.pallas_call(
        body, out_shape=[jax.ShapeDtypeStruct((t, IN_DIM), BF16), jax.ShapeDtypeStruct((t, D), BF16),
                         jax.ShapeDtypeStruct(win_own.shape, BF16), jax.ShapeDtypeStruct((8, D), F32)]
        + [jax.ShapeDtypeStruct(f.shape, f.dtype) for f in more],
        grid_spec=pltpu.PrefetchScalarGridSpec(
            num_scalar_prefetch=1, grid=(steps,), in_specs=in_specs, out_specs=out_specs,
            scratch_shapes=[pltpu.VMEM((t, D), BF16), pltpu.VMEM((2, tm, D), F32), pltpu.VMEM((2, D, IN_PIECE), BF16)]
            + [pltpu.SemaphoreType.DMA((3,))] * 2 + [pltpu.SemaphoreType.DMA((6,))]
            + _gather_sems(N_PIECE) + _gather_sems(nm)),
        input_output_aliases={3: 2, **{5 + w: 4 + w for w in range(nm)}},
        compiler_params=_cp(("arbitrary",), VMEM_LIMIT), name="proj_gather")(seq, x, g, win_own, cw8s, *more)
    return outs[0], outs[1], outs[2], outs[3], outs[4:]


class _Gather:
    def __init__(self, outs, ici_s, ici_r, d2d_s, d2d_r):
        x, y, c, _ = _coords()
        self.outs, self.c = outs, c
        self.sems = ici_s, ici_r, d2d_s, d2d_r
        self.b, self.bx, self.by, self.bd = 2 * x + y, 2 * (1 - x) + y, 2 * x + (1 - y), 2 * (1 - x) + (1 - y)
        self.xn, self.yn, self.sib = (1 - x, y, c), (x, 1 - y, c), (x, y, 1 - c)

    def piece(self, w, blk, hc, quarter=None):
        hr = self.outs[w].shape[1] // 2
        if quarter is None:
            return self.outs[w].at[blk, pl.ds(hc * hr, hr)]
        return self.outs[w].at[blk, pl.ds(hc * hr + quarter * (hr // 2), hr // 2)]

    def ici(self, w, k, ref, to):
        return pltpu.make_async_remote_copy(src_ref=ref, dst_ref=ref, send_sem=self.sems[0].at[w, k],
                                            recv_sem=self.sems[1].at[w, k], device_id=to, device_id_type=MESH)

    def d2d(self, w, k, ref):
        return pltpu.make_async_remote_copy(src_ref=ref, dst_ref=ref, send_sem=self.sems[2].at[w, k],
                                            recv_sem=self.sems[3].at[w, k], device_id=self.sib, device_id_type=MESH)

    def all(self):
        return range(len(self.outs))

    def start(self):
        for w in self.all():
            mine = self.piece(w, self.b, self.c)
            self.ici(w, 0, mine, self.xn).start()
            self.ici(w, 1, mine, self.yn).start()

    def hop(self, ws=None):
        c = self.c
        for w in ws or self.all():
            self.ici(w, 0, self.piece(w, self.bx, c), self.xn).wait_recv()
            self.ici(w, 1, self.piece(w, self.by, c), self.yn).wait_recv()
            self.ici(w, 2, self.piece(w, self.bx, c, 0), self.yn).start()
            self.ici(w, 3, self.piece(w, self.by, c, 1), self.xn).start()
            self.d2d(w, 0, self.piece(w, self.bx, c)).start()
            self.d2d(w, 1, self.piece(w, self.by, c)).start()

    def near_ready(self, ws=None):
        for w in ws or self.all():
            self.d2d(w, 0, self.piece(w, self.bx, 1 - self.c)).wait_recv()
            self.d2d(w, 1, self.piece(w, self.by, 1 - self.c)).wait_recv()

    def far(self, ws=None):
        c = self.c
        for w in ws or self.all():
            self.ici(w, 2, self.piece(w, self.bd, c, 0), self.yn).wait_recv()
            self.ici(w, 3, self.piece(w, self.bd, c, 1), self.xn).wait_recv()
            self.d2d(w, 2, self.piece(w, self.bd, c, 0)).start()
            self.d2d(w, 3, self.piece(w, self.bd, c, 1)).start()

    def far_ready(self, ws=None):
        for w in ws or self.all():
            self.d2d(w, 2, self.piece(w, self.bd, 1 - self.c, 0)).wait_recv()
            self.d2d(w, 3, self.piece(w, self.bd, 1 - self.c, 1)).wait_recv()

    def drain(self):
        c = self.c
        for w in self.all():
            mine = self.piece(w, self.b, c)
            self.ici(w, 0, mine, self.xn).wait_send()
            self.ici(w, 1, mine, self.yn).wait_send()
            self.ici(w, 2, self.piece(w, self.bx, c, 0), self.yn).wait_send()
            self.ici(w, 3, self.piece(w, self.by, c, 1), self.xn).wait_send()
            self.d2d(w, 0, self.piece(w, self.bx, c)).wait_send()
            self.d2d(w, 1, self.piece(w, self.by, c)).wait_send()
            self.d2d(w, 2, self.piece(w, self.bd, c, 0)).wait_send()
            self.d2d(w, 3, self.piece(w, self.bd, c, 1)).wait_send()


def _gather_sems(nw):
    return [pltpu.SemaphoreType.DMA((max(nw, 1), 4))] * 4


def _mixer_fwd(proj, cw8, lng, lnb, wc, bsb, fulls, tm=256):
    t = proj.shape[0]
    nt = t // tm
    nch = tm // CH
    nw = len(fulls)

    def body(*refs):
        p_ref, cw_ref, lng_ref, lnb_ref, wc_ref, bsb_ref = refs[:6]
        mix_ref = refs[6 + nw]
        w_outs = refs[7 + nw:7 + 2 * nw]
        prev_ref = refs[7 + 2 * nw]
        gather = _Gather(w_outs, *refs[8 + 2 * nw:])

        @pl.when(pl.program_id(0) == 0)
        def _():
            gather.start()
            prev_ref[...] = jnp.zeros_like(prev_ref)

        @pl.when(pl.program_id(0) == nt // 2)
        def _():
            gather.hop()

        @pl.when(pl.program_id(0) == nt - 1)
        def _():
            gather.far()

        rows = lax.broadcasted_iota(jnp.int32, (tm, CH), 0)
        for s in range(HEADS):
            cs = pl.ds(CH * s, CH)

            def slab(k):
                return p_ref[:, pl.ds(k * SLAB + CH * s, CH)].astype(F32)

            gb, gc, xa, za = slab(0), slab(1), slab(2), slab(3)
            cx = gc * xa
            p6 = jnp.broadcast_to(prev_ref[6:7, cs], (tm, CH))
            p7 = jnp.broadcast_to(prev_ref[7:8, cs], (tm, CH))
            c1 = jnp.where(rows == 0, p7, pltpu.roll(cx, 1, 0))
            c2 = jnp.where(rows == 0, p6, jnp.where(rows == 1, p7, pltpu.roll(cx, 2, 0)))
            prev_ref[:, cs] = cx[tm - 8:, :]
            cv = cw_ref[0:1, cs] * c2 + cw_ref[1:2, cs] * c1 + cw_ref[2:3, cs] * cx
            mix_ref[:, cs] = (gb * cv * (za * _sigmoid(za))).astype(BF16)

            u, v, zb = slab(4), slab(5), slab(6)
            ug, vg = _gelu(u), _gelu(v)
            dlt = vg - jnp.mean(vg, axis=-1, keepdims=True)
            vhat = dlt * lax.rsqrt(jnp.mean(dlt * dlt, axis=-1, keepdims=True) + EPS)
            vn = (vhat * lng_ref[:, cs] + lnb_ref[:, cs]).astype(BF16)
            gate = ug * (zb * _sigmoid(zb))
            for c in range(nch):
                rs = slice(CH * c, CH * (c + 1))
                sp = jnp.dot(wc_ref[s], vn[rs], preferred_element_type=F32) + bsb_ref[s]
                mix_ref[rs, pl.ds(SLAB + CH * s, CH)] = (gate[rs] * sp).astype(BF16)

        @pl.when(pl.program_id(0) == nt - 1)
        def _():
            gather.near_ready()
            gather.far_ready()
            gather.drain()

    sems = _gather_sems(nw)
    outs = pl.pallas_call(
        body, grid=(nt,),
        in_specs=[pl.BlockSpec((tm, IN_DIM), lambda i: (i, 0)), _full((8, D)), _full((1, D)), _full((1, D)),
                  _full((HEADS, CH, CH)), _full((HEADS, CH, CH))] + [ANY] * nw,
        out_specs=[pl.BlockSpec((tm, MIX), lambda i: (i, 0))] + [ANY] * nw,
        out_shape=[jax.ShapeDtypeStruct((t, MIX), BF16)] + [jax.ShapeDtypeStruct(f.shape, f.dtype) for f in fulls],
        input_output_aliases={6 + w: 1 + w for w in range(nw)},
        scratch_shapes=[pltpu.VMEM((8, D), F32)] + sems,
        compiler_params=_cp(("arbitrary",), VMEM_LIMIT), name="mixer_fwd")(proj, cw8, lng, lnb, wc, bsb, *fulls)
    return outs[0], outs[1:]


def _mem_fwd(mem, gm, wkv_f):
    n_mem = mem.shape[0]

    def body(mem_ref, gm_ref, w_ref, k_ref, v_ref):
        m, _ = _rms(mem_ref[...], gm_ref[...])
        mb = m.astype(BF16)
        for j in range(N_CHIP):
            dst = k_ref if j < 2 else v_ref
            dst[:, pl.ds(KV_BLK * (j % 2), KV_BLK)] = jnp.dot(mb, w_ref[j], preferred_element_type=F32).astype(BF16)

    return pl.pallas_call(
        body, out_shape=[jax.ShapeDtypeStruct((n_mem, D), BF16), jax.ShapeDtypeStruct((n_mem, D), BF16)],
        compiler_params=_cp(None, VMEM_LIMIT), name="mem_fwd")(mem, gm, wkv_f)


def _tail(x, tgt, mixin, wout, wq, wxo, k, v, g2, g3, tm=512, sub=512):
    t = x.shape[0]
    n_mem = k.shape[0]
    scale = 1.0 / math.sqrt(XD)

    def body(x_ref, tgt_ref, mix_ref, wout_ref, wq_ref, wxo_ref, k_ref, v_ref, g2_ref, g3_ref,
             loss_ref, dmix_ref, dx1b_ref, h2_ref, dq_ref, o_ref, dx2b_ref, dk_ref, dv_ref, dg2_ref, dg3_ref):
        @pl.when(pl.program_id(0) == 0)
        def _():
            loss_ref[...] = jnp.zeros_like(loss_ref)
            dk_ref[...] = jnp.zeros_like(dk_ref)
            dv_ref[...] = jnp.zeros_like(dv_ref)
            dg2_ref[...] = jnp.zeros_like(dg2_ref)
            dg3_ref[...] = jnp.zeros_like(dg3_ref)

        g2, g3 = g2_ref[...], g3_ref[...]
        for sb in range(tm // sub):
            rs = pl.ds(sub * sb, sub)
            x1 = x_ref[rs, :] + jnp.dot(mix_ref[rs, :], wout_ref[...], preferred_element_type=F32)
            h2, r2 = _rms(x1, g2)
            h2b = h2.astype(BF16)
            h2_ref[rs, :] = h2b
            q = jnp.dot(h2b, wq_ref[...], preferred_element_type=F32).astype(BF16)
            probs, outs = [], []
            for hd in range(XH):
                hs = pl.ds(XD * hd, XD)
                s = _bdot_nt(q[:, XD * hd:XD * (hd + 1)], k_ref[:, hs]) * scale
                e = jnp.exp(s - jnp.max(s, axis=-1, keepdims=True))
                p = e / jnp.sum(e, axis=-1, keepdims=True)
                probs.append(p)
                outs.append(_bdot(p, v_ref[:, hs]))
            ob = jnp.concatenate(outs, axis=-1).astype(BF16)
            o_ref[rs, :] = ob
            x2 = x1 + jnp.dot(ob, wxo_ref[...], preferred_element_type=F32)
            y, r3 = _rms(x2, g3)
            diff = y - tgt_ref[rs, :]
            row_loss = jnp.sum(diff * diff, axis=-1, keepdims=True)
            loss_ref[...] += jnp.broadcast_to(jnp.sum(row_loss, axis=0, keepdims=True) * (0.5 / D), loss_ref.shape)

            dx2, dg3 = _rms_bwd(diff * (1.0 / D), x2, r3, g3)
            dg3_ref[...] += dg3
            dx2b = dx2.astype(BF16)
            dx2b_ref[rs, :] = dx2b
            do = _bdot_nt(dx2b, wxo_ref[...])
            dqs = []
            for hd in range(XH):
                hs = pl.ds(XD * hd, XD)
                p = probs[hd]
                do_h = do[:, XD * hd:XD * (hd + 1)]
                dv_ref[:, hs] += _bdot_tn(p, do_h)
                dp = _bdot_nt(do_h, v_ref[:, hs])
                ds = p * (dp - jnp.sum(dp * p, axis=-1, keepdims=True))
                dqs.append(_bdot(ds, k_ref[:, hs]) * scale)
                dk_ref[:, hs] += _bdot_tn(ds, q[:, XD * hd:XD * (hd + 1)]) * scale
            dq = jnp.concatenate(dqs, axis=-1).astype(BF16)
            dq_ref[rs, :] = dq
            dx1n, dg2 = _rms_bwd(_bdot_nt(dq, wq_ref[...]), x1, r2, g2)
            dg2_ref[...] += dg2
            dx1b = (dx2 + dx1n).astype(BF16)
            dx1b_ref[rs, :] = dx1b
            dmix_ref[rs, :] = _bdot_nt(dx1b, wout_ref[...]).astype(BF16)

    tok = lambda w: pl.BlockSpec((tm, w), lambda i: (i, 0))
    return pl.pallas_call(
        body, grid=(t // tm,),
        in_specs=[tok(D), tok(D), tok(MIX), _full((MIX, D), 1), _full((D, D), 1), _full((D, D), 1),
                  _full((n_mem, D), 1), _full((n_mem, D), 1), _full((1, D)), _full((1, D))],
        out_specs=[_full((8, 128)), tok(MIX), tok(D), tok(D), tok(D), tok(D), tok(D),
                   _full((n_mem, D)), _full((n_mem, D)), _full((1, D)), _full((1, D))],
        out_shape=[jax.ShapeDtypeStruct((8, 128), F32), jax.ShapeDtypeStruct((t, MIX), BF16),
                   jax.ShapeDtypeStruct((t, D), BF16),
                   jax.ShapeDtypeStruct((t, D), BF16), jax.ShapeDtypeStruct((t, D), BF16),
                   jax.ShapeDtypeStruct((t, D), BF16), jax.ShapeDtypeStruct((t, D), BF16),
                   jax.ShapeDtypeStruct((n_mem, D), F32), jax.ShapeDtypeStruct((n_mem, D), F32),
                   jax.ShapeDtypeStruct((1, D), F32), jax.ShapeDtypeStruct((1, D), F32)],
        compiler_params=_cp(("arbitrary",), VMEM_LIMIT), name="tail")(x, tgt, mixin, wout, wq, wxo, k, v, g2, g3)


def _mem_bwd(mem, gm, dk, dv, wkv_f):
    def body(mem_ref, gm_ref, dk_ref, dv_ref, w_ref, dw_ref, dwb_ref, dgm_ref):
        mem_v = mem_ref[...]
        m, rm = _rms(mem_v, gm_ref[...])
        mb = m.astype(BF16)
        dm = jnp.zeros_like(mem_v)
        for j in range(N_CHIP):
            src = dk_ref if j < 2 else dv_ref
            dkv = src[:, pl.ds(KV_BLK * (j % 2), KV_BLK)].astype(BF16)
            dw = _bdot_tn(mb, dkv)
            dw_ref[j] = dw
            dwb_ref[j] = dw.astype(BF16)
            dm = dm + _bdot_nt(dkv, w_ref[j])
        dgm_ref[...] = jnp.sum(dm * mem_v * rm, axis=0, keepdims=True)

    return pl.pallas_call(
        body, out_shape=[jax.ShapeDtypeStruct((N_CHIP, D, KV_BLK), F32), jax.ShapeDtypeStruct((N_CHIP, D, KV_BLK), BF16),
                         jax.ShapeDtypeStruct((1, D), F32)],
        compiler_params=_cp(None, VMEM_LIMIT), name="mem_bwd")(mem, gm, dk, dv, wkv_f)


def _mixer_bwd(proj, dmix, cw8, lng, lnb, wc, wct, bsb, win_f, x, dx1, g1, tm=256):
    t = proj.shape[0]
    nt = t // tm
    nch = tm // CH
    hb = 16
    pair = 2 * CH
    assert pair == IN_PIECE

    def body(p_ref, pgc_ref, pxa_ref, dm_ref, cw_ref, lng_ref, lnb_ref, wc_ref, wct_ref, bsb_ref, w_ref, x_ref,
             dx1_ref, g1_ref, dp_ref, dcw_ref, dlng_ref, dlnb_ref, dwc_ref, dbs_ref, gx_ref, dg1_ref,
             next_ref, dh_ref):
        i = pl.program_id(0)

        @pl.when(i == 0)
        def _():
            next_ref[...] = jnp.zeros_like(next_ref)
            dcw_ref[...] = jnp.zeros_like(dcw_ref)
            dlng_ref[...] = jnp.zeros_like(dlng_ref)
            dlnb_ref[...] = jnp.zeros_like(dlnb_ref)
            dwc_ref[...] = jnp.zeros_like(dwc_ref)
            dbs_ref[...] = jnp.zeros_like(dbs_ref)
            dg1_ref[...] = jnp.zeros_like(dg1_ref)

        first_tile = i == nt - 1
        rows = lax.broadcasted_iota(jnp.int32, (tm, CH), 0)
        ones8 = jnp.ones((8, CH), BF16)
        for s in range(HEADS):
            cs = pl.ds(CH * s, CH)

            def slab(k):
                return p_ref[:, pl.ds(k * SLAB + CH * s, CH)].astype(F32)

            gb, gc, xa, za = slab(0), slab(1), slab(2), slab(3)
            da = dm_ref[:, cs].astype(F32)
            cx = gc * xa
            cxp = pgc_ref[:, cs].astype(F32) * pxa_ref[:, cs].astype(F32)
            cxp = jnp.where(first_tile, jnp.zeros_like(cxp), cxp)
            p6 = jnp.broadcast_to(cxp[hb - 2:hb - 1, :], (tm, CH))
            p7 = jnp.broadcast_to(cxp[hb - 1:hb, :], (tm, CH))
            c1 = jnp.where(rows == 0, p7, pltpu.roll(cx, 1, 0))
            c2 = jnp.where(rows == 0, p6, jnp.where(rows == 1, p7, pltpu.roll(cx, 2, 0)))
            w0, w1, w2 = cw_ref[0:1, cs], cw_ref[1:2, cs], cw_ref[2:3, cs]
            cv = w0 * c2 + w1 * c1 + w2 * cx
            sg = _sigmoid(za)
            sa = za * sg
            dcv = da * gb * sa
            dp_ref[:, pl.ds(0 * SLAB + CH * s, CH)] = (da * cv * sa).astype(BF16)
            dp_ref[:, pl.ds(3 * SLAB + CH * s, CH)] = (da * gb * cv * (sg * (1.0 + za * (1.0 - sg)))).astype(BF16)
            n0 = jnp.broadcast_to(next_ref[0:1, cs], (tm, CH))
            n1 = jnp.broadcast_to(next_ref[1:2, cs], (tm, CH))
            u1 = jnp.where(rows == tm - 1, n0, pltpu.roll(dcv, tm - 1, 0))
            u2 = jnp.where(rows == tm - 2, n0, jnp.where(rows == tm - 1, n1, pltpu.roll(dcv, tm - 2, 0)))
            next_ref[:, cs] = dcv[0:8, :]
            dcx = w2 * dcv + w1 * u1 + w0 * u2
            dp_ref[:, pl.ds(1 * SLAB + CH * s, CH)] = (dcx * xa).astype(BF16)
            dp_ref[:, pl.ds(2 * SLAB + CH * s, CH)] = (dcx * gc).astype(BF16)
            dcw_ref[0:1, cs] += jnp.sum(dcv * c2, axis=0, keepdims=True)
            dcw_ref[1:2, cs] += jnp.sum(dcv * c1, axis=0, keepdims=True)
            dcw_ref[2:3, cs] += jnp.sum(dcv * cx, axis=0, keepdims=True)

            u, v, zb = slab(4), slab(5), slab(6)
            db = dm_ref[:, pl.ds(SLAB + CH * s, CH)].astype(F32)
            ug, ugrad = _gelu_parts(u)
            vg, vgrad = _gelu_parts(v)
            dlt = vg - jnp.mean(vg, axis=-1, keepdims=True)
            rstd = lax.rsqrt(jnp.mean(dlt * dlt, axis=-1, keepdims=True) + EPS)
            vhat = dlt * rstd
            lg = lng_ref[:, cs]
            vn = (vhat * lg + lnb_ref[:, cs]).astype(BF16)
            sgb = _sigmoid(zb)
            szb = zb * sgb
            sps, dvns = [], []
            dbs = jnp.zeros((8, CH), F32)
            dwc = jnp.zeros((CH, CH), F32)
            for c in range(nch):
                rs = slice(CH * c, CH * (c + 1))
                sp = jnp.dot(wc_ref[s], vn[rs], preferred_element_type=F32) + bsb_ref[s]
                dsp = (db[rs] * ug[rs] * szb[rs]).astype(BF16)
                dbs = dbs + lax.dot_general(ones8, dsp, (((1,), (1,)), ((), ())), preferred_element_type=F32)
                dwc = dwc + lax.dot_general(dsp, vn[rs], (((1,), (1,)), ((), ())), preferred_element_type=F32)
                dvns.append(jnp.dot(wct_ref[s], dsp, preferred_element_type=F32))
                sps.append(sp)
            sp = jnp.concatenate(sps, axis=0)
            dvn = jnp.concatenate(dvns, axis=0)
            dbs_ref[:, cs] += dbs
            dwc_ref[s] += dwc
            dlng_ref[:, cs] += jnp.sum(dvn * vhat, axis=0, keepdims=True)
            dlnb_ref[:, cs] += jnp.sum(dvn, axis=0, keepdims=True)
            dvhat = dvn * lg
            dvg = rstd * (dvhat - jnp.mean(dvhat, axis=-1, keepdims=True)
                          - vhat * jnp.mean(dvhat * vhat, axis=-1, keepdims=True))
            dp_ref[:, pl.ds(4 * SLAB + CH * s, CH)] = (db * sp * szb * ugrad).astype(BF16)
            dp_ref[:, pl.ds(5 * SLAB + CH * s, CH)] = (dvg * vgrad).astype(BF16)
            dp_ref[:, pl.ds(6 * SLAB + CH * s, CH)] = (db * ug * sp * (sgb * (1.0 + zb * (1.0 - sgb)))).astype(BF16)

            if s % 2 == 1:
                part = None
                for k in range(N_SLAB):
                    col = k * SLAB + pair * (s // 2)
                    blk, off = divmod(col, IN_BLK)
                    term = lax.dot_general(dp_ref[:, pl.ds(col, pair)], w_ref[blk, off // IN_PIECE],
                                           (((1,), (1,)), ((), ())), preferred_element_type=F32)
                    part = term if part is None else part + term
                if s == 1:
                    dh_ref[...] = part
                else:
                    dh_ref[...] += part

        xv = x_ref[...]
        r = lax.rsqrt(jnp.mean(xv * xv, axis=-1, keepdims=True) + EPS)
        dxn, dg = _rms_bwd(dh_ref[...], xv, r, g1_ref[...])
        gx_ref[...] = dx1_ref[...].astype(F32) + dxn
        dg1_ref[0:1, :] += dg

        @pl.when(i == nt - 1)
        def _():
            tril = lax.broadcasted_iota(jnp.int32, (CH, CH), 0) >= lax.broadcasted_iota(jnp.int32, (CH, CH), 1)
            for s in range(HEADS):
                dwc_ref[s] = jnp.where(tril, dwc_ref[s], 0.0)

    rev = lambda i: nt - 1 - i
    halo = lambda col: pl.BlockSpec((hb, SLAB), lambda i: (jnp.maximum(rev(i) * (tm // hb) - 1, 0), col))
    tok = lambda w: pl.BlockSpec((tm, w), lambda i: (rev(i), 0))
    return pl.pallas_call(
        body, grid=(nt,),
        in_specs=[tok(IN_DIM), halo(1), halo(2), tok(MIX), _full((8, D)), _full((1, D)), _full((1, D)),
                  _full((HEADS, CH, CH)), _full((HEADS, CH, CH)), _full((HEADS, CH, CH)),
                  _full((N_CHIP, N_PIECE, D, IN_PIECE), 1), tok(D), tok(D), _full((1, D))],
        out_specs=[tok(IN_DIM), _full((8, D)), _full((1, D)), _full((1, D)), _full((HEADS, CH, CH)), _full((8, D)),
                   tok(D), _full((8, D))],
        out_shape=[jax.ShapeDtypeStruct((t, IN_DIM), BF16), jax.ShapeDtypeStruct((8, D), F32),
                   jax.ShapeDtypeStruct((1, D), F32), jax.ShapeDtypeStruct((1, D), F32),
                   jax.ShapeDtypeStruct((HEADS, CH, CH), F32), jax.ShapeDtypeStruct((8, D), F32),
                   jax.ShapeDtypeStruct((t, D), F32), jax.ShapeDtypeStruct((8, D), F32)],
        scratch_shapes=[pltpu.VMEM((8, D), F32), pltpu.VMEM((tm, D), F32)],
        compiler_params=_cp(("arbitrary",), VMEM_LIMIT), name="mixer_bwd")(
            proj, proj, proj, dmix, cw8, lng, lnb, wc, wct, bsb, win_f, x, dx1, g1)


def _grad_matmul(a, b, after, *, by_cols, name, tk=1024):
    t, m = a.shape
    n = b.shape[1]
    nk = t // tk
    nj = N_CHIP if by_cols else 1
    bn = n // nj

    def body(a_ref, b_ref, after_ref, o_ref, ob_ref):
        kk = pl.program_id(1)
        part = lax.dot_general(a_ref[...], b_ref[...], (((0,), (0,)), ((), ())), preferred_element_type=F32)

        @pl.when(kk == 0)
        def _():
            o_ref[...] = part

        @pl.when(kk > 0)
        def _():
            o_ref[...] += part

        @pl.when(kk == nk - 1)
        def _():
            ob_ref[...] = o_ref[...].astype(BF16)

    a_spec = pl.BlockSpec((tk, m), lambda j, k: (k, 0))
    b_spec = pl.BlockSpec((tk, bn), lambda j, k: (k, j))
    o_spec = pl.BlockSpec((None, m, bn), lambda j, k: (j, 0, 0))
    o32, o16 = pl.pallas_call(
        body, grid=(nj, nk), in_specs=[a_spec, b_spec, ANY], out_specs=[o_spec, o_spec],
        out_shape=[jax.ShapeDtypeStruct((nj, m, bn), F32), jax.ShapeDtypeStruct((nj, m, bn), BF16)],
        compiler_params=_cp(("parallel", "arbitrary"), VMEM_LIMIT), name=name)(a, b, after)
    if by_cols:
        return o32, o16
    return o32.reshape(N_CHIP, m // N_CHIP, n), o16.reshape(N_CHIP, m // N_CHIP, n)


def _coords():
    x, y, c = lax.axis_index("x"), lax.axis_index("y"), lax.axis_index("c")
    chips = [(1 - x, y), (x, 1 - y), (1 - x, 1 - y)]
    return x, y, c, chips


def _pair_reduce(c_idx, grads, grads_b, smalls, name):
    ng, ns = len(grads), len(smalls)
    halves = [g.shape[1] // 2 for g in grads]

    def body(c_ref, *refs):
        g_in, gb_any = refs[:ng], refs[ng:2 * ng]
        s_own, s_any = refs[2 * ng:2 * ng + ns], refs[2 * ng + ns:2 * ng + 2 * ns]
        o = refs[2 * ng + 2 * ns:4 * ng + 3 * ns]
        lands = refs[4 * ng + 3 * ns:5 * ng + 4 * ns]
        send, recv = refs[5 * ng + 4 * ns:]
        x, y, c, _ = _coords()
        j = pl.program_id(0)

        def big(i, blk):
            return pltpu.make_async_remote_copy(
                src_ref=gb_any[i].at[blk, pl.ds((1 - c) * halves[i], halves[i])], dst_ref=lands[i].at[blk],
                send_sem=send.at[i, blk], recv_sem=recv.at[i, blk], device_id=(x, y, 1 - c), device_id_type=MESH)

        def small(i):
            return pltpu.make_async_remote_copy(
                src_ref=s_any[i].at[1 - c], dst_ref=lands[ng + i],
                send_sem=send.at[ng + i, 0], recv_sem=recv.at[ng + i, 0], device_id=(x, y, 1 - c), device_id_type=MESH)

        @pl.when(j == 0)
        def _():
            for blk in range(N_CHIP):
                for i in range(ng):
                    big(i, blk).start()
            for i in range(ns):
                small(i).start()

        for i in range(ng):
            big(i, j).wait_recv()
            tot = g_in[i][...] + lands[i][j].astype(F32)
            o[i][...] = tot
            o[ng + i][...] = tot.astype(BF16)

        @pl.when(j == N_CHIP - 1)
        def _():
            for i in range(ns):
                small(i).wait_recv()
                o[2 * ng + i][...] = s_own[i][...] + lands[ng + i][...]
                small(i).wait_send()
            for blk in range(N_CHIP):
                for i in range(ng):
                    big(i, blk).wait_send()

    in_specs = [pl.BlockSpec((None, None, halves[i], g.shape[2]), lambda b, c: (b, c[0], 0, 0)) for i, g in enumerate(grads)]
    in_specs += [ANY] * ng
    in_specs += [pl.BlockSpec((None, s.shape[0] // 2, s.shape[1]), lambda b, c: (c[0], 0, 0)) for s in smalls]
    in_specs += [ANY] * ns
    blk = [pl.BlockSpec((None, halves[i], g.shape[2]), lambda b, c: (b, 0, 0)) for i, g in enumerate(grads)]
    out_specs = blk + blk + [pl.BlockSpec((s.shape[0] // 2, s.shape[1]), lambda b, c: (0, 0)) for s in smalls]
    out_shape = [jax.ShapeDtypeStruct((N_CHIP, halves[i], g.shape[2]), F32) for i, g in enumerate(grads)]
    out_shape += [jax.ShapeDtypeStruct((N_CHIP, halves[i], g.shape[2]), BF16) for i, g in enumerate(grads)]
    out_shape += [jax.ShapeDtypeStruct((s.shape[0] // 2, s.shape[1]), F32) for s in smalls]
    scratch = [pltpu.VMEM((N_CHIP, halves[i], g.shape[2]), BF16) for i, g in enumerate(grads)]
    scratch += [pltpu.VMEM((s.shape[0] // 2, s.shape[1]), F32) for s in smalls]
    scratch += [pltpu.SemaphoreType.DMA((ng + ns, N_CHIP)), pltpu.SemaphoreType.DMA((ng + ns, N_CHIP))]
    grads4 = [g.reshape(N_CHIP, 2, halves[i], g.shape[2]) for i, g in enumerate(grads)]
    smalls3 = [s.reshape(2, s.shape[0] // 2, s.shape[1]) for s in smalls]
    return pl.pallas_call(
        body, out_shape=out_shape,
        grid_spec=pltpu.PrefetchScalarGridSpec(num_scalar_prefetch=1, grid=(N_CHIP,), in_specs=in_specs,
                                               out_specs=out_specs, scratch_shapes=scratch),
        compiler_params=_cp(("arbitrary",), VMEM_LIMIT), name=name)(c_idx, *grads4, *grads_b, *smalls3, *smalls3)


_HBM = pl.BlockSpec(memory_space=pltpu.HBM)
_SEM = pl.BlockSpec(memory_space=pltpu.SEMAPHORE)


def _split_copies(ins, lands, ng, send, recv, arriving):
    x, y, c, chips = _coords()
    b = 2 * x + y
    copies = []
    for i in range(len(ins)):
        for k in range(3):
            blk = 2 * chips[k][0] + chips[k][1]
            src, dst, got = (ins[i].at[blk], lands[i].at[k], lands[i].at[k]) if i < ng else (ins[i], lands[i].at[b], lands[i].at[blk])
            sems = dict(send_sem=send.at[3 * i + k], recv_sem=recv.at[3 * i + k], device_id=(*chips[k], c), device_id_type=MESH)
            if arriving:
                copies.append(pltpu.make_async_remote_copy(src_ref=got, dst_ref=got, **sems))
            else:
                copies.append(pltpu.make_async_remote_copy(src_ref=src, dst_ref=dst, **sems))
    return copies


def _exchange_begin(sums_b, smalls, name):
    ng, n = len(sums_b), len(sums_b) + len(smalls)
    srcs = list(sums_b) + list(smalls)
    lands = [lax.empty((3,) + g.shape[1:], g.dtype) for g in sums_b] + [lax.empty((N_CHIP,) + s.shape, s.dtype) for s in smalls]

    def body(*refs):
        ins, land_refs = refs[:n], refs[n:2 * n]
        send, recv = refs[2 * n], refs[2 * n + 1]
        token = refs[4 * n + 2]
        for cp in _split_copies(ins, land_refs, ng, send, recv, False):
            cp.start()
        token[...] = jnp.zeros_like(token)

    hbm = lambda a: pltpu.HBM(a.shape, a.dtype)
    outs = pl.pallas_call(
        body, name=name,
        out_shape=(pltpu.SemaphoreType.DMA((3 * n,)), pltpu.SemaphoreType.DMA((3 * n,)), *[hbm(a) for a in srcs + lands],
                   jax.ShapeDtypeStruct((8, 128), F32)),
        in_specs=[_HBM] * (2 * n), out_specs=(_SEM, _SEM, *[_HBM] * (2 * n), pl.BlockSpec(memory_space=pltpu.VMEM)),
        input_output_aliases={i: 2 + i for i in range(2 * n)},
        compiler_params=pltpu.CompilerParams(has_side_effects=pltpu.SideEffectType.DATAFLOW_SIDE_EFFECTING),
    )(*[pltpu.with_memory_space_constraint(a, pltpu.HBM) for a in srcs + lands])
    return outs[0], outs[1], list(outs[2:2 + n]), list(outs[2 + n:2 + 2 * n]), outs[2 + 2 * n]


def _exchange_end(send, recv, srcs, lands, ng, after, name):
    n = len(srcs)
    after = list(after)

    def body(*refs):
        ins, land_refs = refs[:n], refs[n:2 * n]
        send_ref, recv_ref = refs[2 * n], refs[2 * n + 1]
        for cp in _split_copies(ins, land_refs, ng, send_ref, recv_ref, False):
            cp.wait_send()
        for cp in _split_copies(ins, land_refs, ng, send_ref, recv_ref, True):
            cp.wait_recv()

    hbm = lambda a: pltpu.HBM(a.shape, a.dtype)
    outs = pl.pallas_call(
        body, name=name, out_shape=tuple(hbm(a) for a in list(srcs) + list(lands)),
        in_specs=[_HBM] * (2 * n) + [_SEM, _SEM] + [ANY] * len(after), out_specs=tuple([_HBM] * (2 * n)),
        input_output_aliases={i: i for i in range(2 * n)},
        compiler_params=pltpu.CompilerParams(has_side_effects=pltpu.SideEffectType.DATAFLOW_SIDE_EFFECTING),
    )(*srcs, *lands, send, recv, *after)
    return list(outs[n:])


def _chip_reduce(bc_idx, sums, recvd, smalls_slots, smalls_own, name, steps=4):
    ng, ns = len(sums), len(smalls_slots)
    n = ng + ns
    assert steps >= 2
    halves = [g.shape[1] for g in sums] + [s.shape[1] for s in smalls_slots]
    rows = [g.shape[1] // steps for g in sums]

    def body(bc_ref, *refs):
        own, rx = refs[:ng], refs[ng:2 * ng]
        sl = refs[2 * ng:2 * ng + ns]
        sl_own = refs[2 * ng + ns:2 * ng + 2 * ns]
        o = refs[2 * ng + 2 * ns:2 * ng + 2 * ns + n]
        tiles = refs[2 * ng + 2 * ns + n:2 * ng + 2 * ns + 2 * n]
        keep, send, recv = refs[2 * ng + 2 * ns + 2 * n:]
        x, y, c, _ = _coords()
        sibling = dict(device_id=(x, y, 1 - c), device_id_type=MESH)
        r = pl.program_id(0)

        def writes(i, step, slot):
            dst = o[i].at[pl.ds(c * halves[i] + step * rows[i], rows[i])]
            return (pltpu.make_async_copy(tiles[i].at[slot], dst, keep.at[i, slot]),
                    pltpu.make_async_remote_copy(src_ref=tiles[i].at[slot], dst_ref=dst, send_sem=send.at[i, slot],
                                                 recv_sem=recv.at[i, step], **sibling))

        def small_writes(i):
            dst = o[i].at[pl.ds(c * halves[i], halves[i])]
            return (pltpu.make_async_copy(tiles[i], dst, keep.at[i, 0]),
                    pltpu.make_async_remote_copy(src_ref=tiles[i], dst_ref=dst, send_sem=send.at[i, 0],
                                                 recv_sem=recv.at[i, 0], **sibling))

        def arriving(i, step, nrows):
            dst = o[i].at[pl.ds((1 - c) * halves[i] + step * nrows, nrows)]
            return pltpu.make_async_remote_copy(src_ref=dst, dst_ref=dst, send_sem=send.at[i, 0], recv_sem=recv.at[i, step],
                                                **sibling)

        def finish(step, slot):
            for i in range(ng):
                local, remote = writes(i, step, slot)
                local.wait()
                remote.wait_send()

        @pl.when(r >= 2)
        def _():
            finish(r - 2, r % 2)

        for i in range(ng):
            tot = own[i][...]
            for j in range(3):
                tot = tot + rx[i][j].astype(F32)
            tiles[i][r % 2] = tot
            for cp in writes(i, r, r % 2):
                cp.start()

        @pl.when(r == 0)
        def _():
            for i in range(ns):
                term = [jnp.where(bc_ref[0] == kk, sl_own[i][...], sl[i][kk]) for kk in range(N_CHIP)]
                tiles[ng + i][...] = ((term[0] + term[1]) + term[2]) + term[3]
                for cp in small_writes(ng + i):
                    cp.start()

        @pl.when(r == steps - 1)
        def _():
            finish(steps - 2, (steps - 2) % 2)
            finish(steps - 1, (steps - 1) % 2)
            for i in range(ns):
                local, remote = small_writes(ng + i)
                local.wait()
                remote.wait_send()
                arriving(ng + i, 0, halves[ng + i]).wait_recv()
            for i in range(ng):
                for step in range(steps):
                    arriving(i, step, rows[i]).wait_recv()

    in_specs = [pl.BlockSpec((None, rows[i], g.shape[2]), lambda r, bc: (bc[0], r, 0)) for i, g in enumerate(sums)]
    in_specs += [pl.BlockSpec((3, rows[i], g.shape[2]), lambda r, bc: (0, r, 0)) for i, g in enumerate(sums)]
    in_specs += [pl.BlockSpec(s.shape, lambda r, bc: (0, 0, 0)) for s in smalls_slots]
    in_specs += [pl.BlockSpec(s.shape[1:], lambda r, bc: (0, 0)) for s in smalls_slots]
    out_shape = [jax.ShapeDtypeStruct((2 * g.shape[1], g.shape[2]), F32) for g in sums]
    out_shape += [jax.ShapeDtypeStruct((2 * s.shape[1], s.shape[2]), F32) for s in smalls_slots]
    scratch = [pltpu.VMEM((2, rows[i], g.shape[2]), F32) for i, g in enumerate(sums)]
    scratch += [pltpu.VMEM(s.shape[1:], F32) for s in smalls_slots]
    scratch += [pltpu.SemaphoreType.DMA((n, 2)), pltpu.SemaphoreType.DMA((n, 2)), pltpu.SemaphoreType.DMA((n, steps))]
    return list(pl.pallas_call(
        body, out_shape=out_shape,
        grid_spec=pltpu.PrefetchScalarGridSpec(num_scalar_prefetch=1, grid=(steps,), in_specs=in_specs,
                                               out_specs=[ANY] * n, scratch_shapes=scratch),
        compiler_params=_cp(("arbitrary",), VMEM_LIMIT), name=name)(bc_idx, *sums, *recvd, *smalls_slots, *smalls_own))


def _adamw_math(w, g, m, v):
    m2 = ADAM_B1 * m + (1.0 - ADAM_B1) * g
    v2 = ADAM_B2 * v + (1.0 - ADAM_B2) * (g * g)
    m_hat = m2 / (1.0 - ADAM_B1 ** ADAM_STEP)
    v_hat = v2 / (1.0 - ADAM_B2 ** ADAM_STEP)
    delta = -ADAM_LR * (m_hat / (jnp.sqrt(v_hat) + ADAM_EPS) + ADAM_WD * w)
    return delta, m2, v2


def _adamw_big(ws, gs, ms, vs, name, steps=8):
    n = len(ws)

    def body(*refs):
        for i in range(n):
            w_ref, g_ref, m_ref, v_ref = (refs[k * n + i] for k in range(4))
            d_ref, m2_ref, v2_ref, g2_ref = (refs[(4 + k) * n + i] for k in range(4))
            gv = g_ref[...]
            d_ref[...], m2_ref[...], v2_ref[...] = _adamw_math(w_ref[...], gv, m_ref[...], v_ref[...])
            g2_ref[...] = gv

    specs = [pl.BlockSpec((w.shape[0] // steps, w.shape[1]), lambda i: (i, 0)) for w in ws]
    shapes = [jax.ShapeDtypeStruct(w.shape, F32) for w in ws]
    outs = pl.pallas_call(
        body, grid=(steps,), in_specs=specs * 4, out_specs=specs * 4, out_shape=shapes * 4,
        compiler_params=_cp(("parallel",), VMEM_LIMIT), name=name)(*ws, *gs, *ms, *vs)
    return [tuple(outs[k * n + i] for k in range(4)) for i in range(n)]


def _adamw_small(groups):
    n = len(groups)

    def body(*refs):
        for i in range(n):
            w_ref, g_ref, m_ref, v_ref = refs[4 * i:4 * i + 4]
            d_ref, m2_ref, v2_ref = refs[4 * n + 3 * i:4 * n + 3 * i + 3]
            d_ref[...], m2_ref[...], v2_ref[...] = _adamw_math(w_ref[...], g_ref[...], m_ref[...], v_ref[...])

    flat = [a for grp in groups for a in grp]
    out_shape = [jax.ShapeDtypeStruct(grp[0].shape, F32) for grp in groups for _ in range(3)]
    outs = pl.pallas_call(body, out_shape=out_shape, name="adamw_small")(*flat)
    return [tuple(outs[3 * i:3 * i + 3]) for i in range(n)]


def kernel(x, mem, norm_mix_g, w_in, conv_w, gm_ln_g, gm_ln_b, gm_ws, gm_bs, w_out, norm_x_g, norm_mem_g, w_q, w_kv, w_xo, norm_final_g, loss_target, m_norm_mix_g, m_w_in, m_conv_w, m_gm_ln_g, m_gm_ln_b, m_gm_ws, m_gm_bs, m_w_out, m_norm_x_g, m_norm_mem_g, m_w_q, m_w_kv, m_w_xo, m_norm_final_g, v_norm_mix_g, v_w_in, v_conv_w, v_gm_ln_g, v_gm_ln_b, v_gm_ws, v_gm_bs, v_w_out, v_norm_x_g, v_norm_mem_g, v_w_q, v_w_kv, v_w_xo, v_norm_final_g):
    t = x.shape[1]
    xi = lax.axis_index("x")
    yi = lax.axis_index("y")
    ci = lax.axis_index("c")
    b_idx = jnp.reshape(2 * xi + yi, (1,)).astype(jnp.int32)
    c_idx = jnp.reshape(ci, (1,)).astype(jnp.int32)

    x2d, mem2d, tgt = x[0], mem[0], loss_target[0]
    big = [w_in[0], w_out[0], w_q[0], w_kv[0], w_xo[0]]
    big_m = [m_w_in[0], m_w_out[0], m_w_q[0], m_w_kv[0], m_w_xo[0]]
    big_v = [v_w_in[0], v_w_out[0], v_w_q[0], v_w_kv[0], v_w_xo[0]]
    g3 = norm_final_g.reshape(1, D)

    def pad8(a):
        return jnp.pad(a, ((0, 8 - a.shape[0]), (0, 0)))

    own_blocks = _cast_shards(b_idx, big)

    tril = jnp.tril(jnp.ones((CH, CH), bool))
    wc32 = jnp.where(tril[None], gm_ws[0], 0.0)
    wc = wc32.astype(BF16)
    wct = jnp.swapaxes(wc32, 1, 2).astype(BF16)
    bsb = jnp.broadcast_to(gm_bs[0][:, :, None], (HEADS, CH, CH))

    blk = 2 * xi + yi
    near = [blk ^ (2 >> (k % 2)) for k in range(2 * N_PIECE)]
    seq_blk = jnp.stack([blk] * N_PIECE + near + [blk ^ 3] * N_PIECE)
    seq_piece = jnp.asarray(list(range(N_PIECE)) + [k // 2 for k in range(2 * N_PIECE)] + list(range(N_PIECE)))
    seq = jnp.stack([seq_blk, seq_piece, seq_blk * N_PIECE + seq_piece]).astype(jnp.int32)
    proj, hb, win_f, cw8, (wq_f,) = _proj_gather(
        seq, x2d, norm_mix_g, own_blocks[0], pad8(conv_w[0]), [own_blocks[2]])
    mixin, (wout_f, wkv_f, wxo_f) = _mixer_fwd(
        proj, cw8, gm_ln_g, gm_ln_b, wc, bsb, [own_blocks[1], own_blocks[3], own_blocks[4]])
    wout2, wq2, wxo2 = wout_f.reshape(MIX, D), wq_f.reshape(D, D), wxo_f.reshape(D, D)
    k, v = _mem_fwd(mem2d, norm_mem_g, wkv_f)

    (loss_tile, dmix, dx1b, h2b, dq, ob, dx2b, dk, dv, dg2, dg3) = _tail(
        x2d, tgt, mixin, wout2, wq2, wxo2, k, v, norm_x_g, g3)
    dwkv, dwkv_b, dgm = _mem_bwd(mem2d, norm_mem_g, dk, dv, wkv_f)
    dproj, dcw, dlng, dlnb, dwc, dbs8, grad_x, dg1 = _mixer_bwd(
        proj, dmix, cw8, gm_ln_g, gm_ln_b, wc, wct, bsb, win_f, x2d, dx1b, norm_mix_g)

    bc_idx = jnp.concatenate([b_idx, c_idx])
    dwin, dwin_b = _grad_matmul(hb, dproj, dgm, by_cols=True, name="grad_w_in", tk=2048)
    zero = jnp.zeros((1, D), F32)
    loss_row = jnp.broadcast_to(loss_tile[0:1, 0:1], (1, D))
    sv = jnp.concatenate([dg1[0:1], dg2, dgm, dg3, dlng, dlnb, dbs8[0:1], loss_row, dcw], axis=0)
    sw = dwc.reshape(HEADS * CH, CH)
    ps_b = _pair_reduce(c_idx, [dwin], [dwin_b], [sv, sw], "pair_reduce_b")
    sums_b, sums_b_b, psmall = list(ps_b[:1]), list(ps_b[1:2]), list(ps_b[2:])
    send_b, recv_b, src_b, land_b, token_b = _exchange_begin(sums_b_b, psmall, "exchange_b_begin")

    dwxo, dwxo_b = _grad_matmul(ob, dx2b, token_b, by_cols=False, name="grad_w_xo", tk=2048)
    dwq, dwq_b = _grad_matmul(h2b, dq, token_b, by_cols=False, name="grad_w_q", tk=2048)
    dwout, dwout_b = _grad_matmul(mixin, dx1b, token_b, by_cols=False, name="grad_w_out")
    ps_a = _pair_reduce(c_idx, [dwout, dwq, dwkv, dwxo], [dwout_b, dwq_b, dwkv_b, dwxo_b], [], "pair_reduce_a")
    sums_a, sums_a_b = list(ps_a[:4]), list(ps_a[4:8])
    send_a, recv_a, src_a, land_a, token_a = _exchange_begin(sums_a_b, [], "exchange_a_begin")

    rx2b = _exchange_end(send_b, recv_b, src_b, land_b, 1, [token_a], "exchange_b_end")
    gwin, svf, swf = _chip_reduce(bc_idx, sums_b, rx2b[:1], rx2b[1:], psmall, "chip_reduce_b")
    out_b = _adamw_big(big[:1], [gwin], big_m[:1], big_v[:1], "adamw_w_in")[0]

    def vec_pack(a1, a2, am, a3, lg, lb, bs):
        return jnp.concatenate([a1, a2, am, a3.reshape(1, D), lg, lb, bs.reshape(1, D), zero], axis=0)

    wv = vec_pack(norm_mix_g, norm_x_g, norm_mem_g, norm_final_g, gm_ln_g, gm_ln_b, gm_bs)
    mv = vec_pack(m_norm_mix_g, m_norm_x_g, m_norm_mem_g, m_norm_final_g, m_gm_ln_g, m_gm_ln_b, m_gm_bs)
    vv = vec_pack(v_norm_mix_g, v_norm_x_g, v_norm_mem_g, v_norm_final_g, v_gm_ln_g, v_gm_ln_b, v_gm_bs)
    loss = svf[7, 0]
    gcw = lax.dynamic_slice_in_dim(svf[8:16], blk * (D // N_CHIP), D // N_CHIP, axis=1)
    gws = swf
    gv = svf[0:8]
    (dv_, mv_, vv_), (dc_, mc_, vc_), (dws_, mws_, vws_) = _adamw_small([
        (wv, gv, mv, vv),
        (pad8(conv_w[0]), gcw, pad8(m_conv_w[0]), pad8(v_conv_w[0])),
        (gm_ws.reshape(HEADS * CH, CH), gws, m_gm_ws.reshape(HEADS * CH, CH), v_gm_ws.reshape(HEADS * CH, CH))])

    rx2a = _exchange_end(send_a, recv_a, src_a, land_a, 4, [out_b[0], dv_], "exchange_a_end")
    g_a = _chip_reduce(bc_idx, sums_a, rx2a, [], [], "chip_reduce_a")
    big_out = [out_b] + _adamw_big(big[1:], g_a, big_m[1:], big_v[1:], "adamw_rest")

    def unpack(vecs, cw, ws, bigs):
        r = lambda i: vecs[i:i + 1]
        return [r(0), bigs[0][None], cw[0:3][None], r(4), r(5), ws.reshape(1, HEADS, CH, CH), vecs[6].reshape(1, HEADS, CH),
                bigs[1][None], r(1), r(2), bigs[2][None], bigs[3][None], bigs[4][None], vecs[3]]

    grads_out = unpack(gv, gcw, gws, [o[3] for o in big_out])
    delta_out = unpack(dv_, dc_, dws_, [o[0] for o in big_out])
    m_out = unpack(mv_, mc_, mws_, [o[1] for o in big_out])
    v_out = unpack(vv_, vc_, vws_, [o[2] for o in big_out])
    return (loss, grad_x[None], *grads_out, *delta_out, *m_out, *v_out)
```

```python
import functools
import math

import jax
import jax.numpy as jnp
from jax import lax
from jax.experimental import pallas as pl
from jax.experimental.pallas import tpu as pltpu

F32 = jnp.float32
BF16 = jnp.bfloat16
MESH = pl.DeviceIdType.MESH

D = 1024
SLAB = 1024
N_SLAB = 7
IN_DIM = N_SLAB * SLAB
MIX = 2 * SLAB
HEADS = 8
CH = 128
XH = 4
XD = D // XH
EPS = 1e-6
GELU_C = math.sqrt(2.0 / math.pi)
GELU_A = 0.044715
N_CHIP = 4
IN_BLK = IN_DIM // N_CHIP
IN_PIECE = 256
N_PIECE = IN_BLK // IN_PIECE
KV_BLK = 2 * D // N_CHIP

ADAM_LR, ADAM_B1, ADAM_B2, ADAM_EPS, ADAM_WD, ADAM_STEP = 0.001, 0.9, 0.999, 1e-08, 0.01, 10

VMEM_LIMIT = 60 * 1024 * 1024


def _cp(sem=None, vmem=None):
    return pltpu.CompilerParams(dimension_semantics=sem, vmem_limit_bytes=vmem)


def _full(shape, buffers=None):
    n = len(shape)
    if buffers is None:
        return pl.BlockSpec(shape, lambda *_: (0,) * n)
    return pl.BlockSpec(shape, lambda *_: (0,) * n, pipeline_mode=pl.Buffered(buffers))


ANY = pl.BlockSpec(memory_space=pl.ANY)


def _bdot(a, b):
    return jnp.dot(a.astype(BF16), b.astype(BF16), preferred_element_type=F32)


def _bdot_nt(a, b):
    return lax.dot_general(a.astype(BF16), b.astype(BF16), (((1,), (1,)), ((), ())), preferred_element_type=F32)


def _bdot_tn(a, b):
    return lax.dot_general(a.astype(BF16), b.astype(BF16), (((0,), (0,)), ((), ())), preferred_element_type=F32)


def _rms(x, g):
    r = lax.rsqrt(jnp.mean(x * x, axis=-1, keepdims=True) + EPS)
    return x * r * g, r


def _rms_bwd(dy, x, r, g):
    gdy = dy * g
    dx = r * gdy - x * (r * r * r) * jnp.mean(x * gdy, axis=-1, keepdims=True)
    dg = jnp.sum(dy * x * r, axis=0, keepdims=True)
    return dx, dg


def _gelu_parts(x):
    x2 = x * x
    t = jnp.tanh(GELU_C * (x + GELU_A * x * x2))
    val = 0.5 * x * (1.0 + t)
    grad = 0.5 * (1.0 + t) + 0.5 * x * (1.0 - t * t) * (GELU_C * (1.0 + 3.0 * GELU_A * x2))
    return val, grad


def _gelu(x):
    return 0.5 * x * (1.0 + jnp.tanh(GELU_C * (x + GELU_A * x * x * x)))


def _sigmoid(z):
    return 1.0 / (1.0 + jnp.exp(-z))


def _cast_shards(b_idx, arrs):
    n = len(arrs)
    steps = 8

    def body(b_ref, *refs):
        for p in range(N_PIECE):
            refs[n][p] = refs[0][:, pl.ds(p * IN_PIECE, IN_PIECE)].astype(BF16)
        for i in range(1, n):
            refs[n + i][...] = refs[i][...].astype(BF16)

    rows = [a.shape[0] // steps for a in arrs]
    in_specs = [pl.BlockSpec((rows[i], a.shape[1]), lambda i, b: (i, 0)) for i, a in enumerate(arrs)]
    out_specs = [pl.BlockSpec((None, N_PIECE, rows[0], IN_PIECE), lambda i, b: (b[0], 0, i, 0))]
    out_specs += [pl.BlockSpec((None, rows[i], a.shape[1]), lambda i, b: (b[0], i, 0)) for i, a in enumerate(arrs) if i > 0]
    out_shape = [jax.ShapeDtypeStruct((N_CHIP, N_PIECE, arrs[0].shape[0], IN_PIECE), BF16)]
    out_shape += [jax.ShapeDtypeStruct((N_CHIP,) + a.shape, BF16) for a in arrs[1:]]
    return pl.pallas_call(
        body, out_shape=out_shape,
        grid_spec=pltpu.PrefetchScalarGridSpec(num_scalar_prefetch=1, grid=(steps,), in_specs=in_specs, out_specs=out_specs),
        compiler_params=_cp(("arbitrary",)), name="cast_shards")(b_idx, *arrs)


def _proj_gather(roles, x, g, win_own, cw8s, more, tm=1024):
    t = x.shape[0]
    ni = t // tm
    nm = len(more)
    chunks = ((0, 4), (4, N_PIECE))
    work = ((0, 0), (0, 1), (1, 0), (2, 0), (1, 1), (2, 1), (3, 0), (3, 1))
    wide = max(p1 - p0 for p0, p1 in chunks) * IN_PIECE
    hop_at = (2, 3, 5, 8, 11, 14, 17)
    far_at = (18, 20, 21, 23, 24, 25, 27)
    more_at = hop_at[-1] + 1

    def body(*refs):
        role_ref, x_any, g_ref, win_in, cw_in = refs[:5]
        o_any, hb_any, win_f, cw_out = refs[5 + nm:9 + nm]
        more_out = refs[9 + nm:9 + 2 * nm]
        hbuf, xbuf, wv, obuf, cw_s, cw_r, loc, wsem, osem = refs[9 + 2 * nm:18 + 2 * nm]
        g_in = _Gather([win_f.at[:, p] for p in range(N_PIECE)], *refs[18 + 2 * nm:22 + 2 * nm])
        g_more = _Gather(more_out, *refs[22 + 2 * nm:26 + 2 * nm])
        x, y, c, chips = _coords()
        b = 2 * x + y
        blks = [2 * chip[0] + chip[1] for chip in chips]

        def cw_cols(blk):
            return cw_out.at[:, pl.ds(blk * (D // N_CHIP), D // N_CHIP)]

        def cw_copy(k, blk):
            src = cw_in if blk is None else cw_cols(blk)
            return pltpu.make_async_remote_copy(src_ref=src, dst_ref=cw_cols(b if blk is None else blk), send_sem=cw_s.at[k],
                                                recv_sem=cw_r.at[k], device_id=(*chips[k], c), device_id_type=MESH)

        cw_local = pltpu.make_async_copy(cw_in, cw_cols(b), loc.at[1])
        hb_copy = pltpu.make_async_copy(hbuf, hb_any, loc.at[0])

        def x_rows(i):
            return pltpu.make_async_copy(x_any.at[pl.ds(i * tm, tm)], xbuf.at[i % 2], loc.at[2 + i % 2])

        def loads(wi):
            role, (p0, p1) = work[wi][0], chunks[work[wi][1]]
            return [pltpu.make_async_copy(win_f.at[role_ref[role], p], wv.at[wi % 2, :, pl.ds((p - p0) * IN_PIECE, IN_PIECE)],
                                          wsem.at[wi % 2, p - p0]) for p in range(p0, p1)]

        def out_copy(wi, i, n):
            role, (p0, p1) = work[wi][0], chunks[work[wi][1]]
            col = pl.multiple_of(role_ref[role] * IN_BLK + p0 * IN_PIECE, IN_PIECE)
            width = (p1 - p0) * IN_PIECE
            return pltpu.make_async_copy(obuf.at[n % 2, :, pl.ds(0, width)],
                                         o_any.at[pl.ds(i * tm, tm), pl.ds(col, width)], osem.at[n % 2])

        done = {g_in.hop: set(), g_in.near_ready: set(), g_in.far: set(), g_in.far_ready: set()}

        def once(step, ps):
            ps = [p for p in ps if p not in done[step]]
            done[step].update(ps)
            if ps:
                step(ps)

        def ensure(wi):
            role, ps = work[wi][0], list(range(*chunks[work[wi][1]]))
            if role in (1, 2):
                once(g_in.hop, ps)
                once(g_in.near_ready, ps)
            elif role == 3:
                once(g_in.far, ps)
                once(g_in.far_ready, ps)

        g_in.start()
        cw_local.start()
        for k in range(3):
            cw_copy(k, None).start()
        for cp in loads(0):
            cp.start()
        x_rows(0).start()

        n = 0
        for wi in range(len(work)):
            width = (chunks[work[wi][1]][1] - chunks[work[wi][1]][0]) * IN_PIECE
            for i in range(ni):
                rows = pl.ds(i * tm, tm)
                if wi == 0:
                    if i + 1 < ni:
                        x_rows(i + 1).start()
                    x_rows(i).wait()
                    h, _ = _rms(xbuf[i % 2], g_ref[...])
                    hbuf[rows, :] = h.astype(BF16)
                if i == 0:
                    for cp in loads(wi):
                        cp.wait()
                for p in range(N_PIECE):
                    if hop_at[p] == n:
                        once(g_in.hop, [p])
                    if far_at[p] == n:
                        once(g_in.far, [p])
                if more_at == n:
                    g_more.start()
                if i == ni - 1 and wi + 1 < len(work):
                    ensure(wi + 1)
                    for cp in loads(wi + 1):
                        cp.start()
                if n >= 2:
                    prev = n - 2
                    out_copy(prev // ni, prev % ni, prev).wait()
                obuf[n % 2, :, pl.ds(0, width)] = jnp.dot(
                    hbuf[rows, :], wv[wi % 2, :, pl.ds(0, width)], preferred_element_type=F32).astype(BF16)
                out_copy(wi, i, n).start()
                n += 1
            if wi == 0:
                hb_copy.start()

        g_more.hop()
        g_more.far()
        for prev in (n - 2, n - 1):
            out_copy(prev // ni, prev % ni, prev).wait()
        for k in range(3):
            cw_copy(k, blks[k]).wait_recv()
        for k in range(3):
            cw_copy(k, None).wait_send()
        cw_local.wait()
        hb_copy.wait()
        g_more.near_ready()
        g_more.far_ready()
        g_in.drain()
        g_more.drain()

    smem = pl.BlockSpec(memory_space=pltpu.SMEM)
    outs = pl.pallas_call(
        body, out_shape=[jax.ShapeDtypeStruct((t, IN_DIM), BF16), jax.ShapeDtypeStruct((t, D), BF16),
                         jax.ShapeDtypeStruct(win_own.shape, BF16), jax.ShapeDtypeStruct((8, D), F32)]
        + [jax.ShapeDtypeStruct(f.shape, f.dtype) for f in more],
        in_specs=[smem, ANY, pl.BlockSpec(memory_space=pltpu.VMEM), ANY, ANY] + [ANY] * nm,
        out_specs=[ANY, ANY, ANY, ANY] + [ANY] * nm,
        scratch_shapes=[pltpu.VMEM((t, D), BF16), pltpu.VMEM((2, tm, D), F32), pltpu.VMEM((2, D, wide), BF16),
                        pltpu.VMEM((2, tm, wide), BF16)]
        + [pltpu.SemaphoreType.DMA((3,))] * 2 + [pltpu.SemaphoreType.DMA((4,)), pltpu.SemaphoreType.DMA((2, 4)),
                                                 pltpu.SemaphoreType.DMA((2,))]
        + _gather_sems(N_PIECE) + _gather_sems(nm),
        input_output_aliases={3: 2, **{5 + w: 4 + w for w in range(nm)}},
        compiler_params=_cp(None, VMEM_LIMIT), name="proj_gather")(roles, x, g, win_own, cw8s, *more)
    return outs[0], outs[1], outs[2], outs[3], outs[4:]


class _Gather:
    def __init__(self, outs, ici_s, ici_r, d2d_s, d2d_r):
        x, y, c, _ = _coords()
        self.outs, self.c = outs, c
        self.sems = ici_s, ici_r, d2d_s, d2d_r
        self.b, self.bx, self.by, self.bd = 2 * x + y, 2 * (1 - x) + y, 2 * x + (1 - y), 2 * (1 - x) + (1 - y)
        self.xn, self.yn, self.sib = (1 - x, y, c), (x, 1 - y, c), (x, y, 1 - c)

    def piece(self, w, blk, hc, quarter=None):
        hr = self.outs[w].shape[1] // 2
        if quarter is None:
            return self.outs[w].at[blk, pl.ds(hc * hr, hr)]
        return self.outs[w].at[blk, pl.ds(hc * hr + quarter * (hr // 2), hr // 2)]

    def ici(self, w, k, ref, to):
        return pltpu.make_async_remote_copy(src_ref=ref, dst_ref=ref, send_sem=self.sems[0].at[w, k],
                                            recv_sem=self.sems[1].at[w, k], device_id=to, device_id_type=MESH)

    def d2d(self, w, k, ref):
        return pltpu.make_async_remote_copy(src_ref=ref, dst_ref=ref, send_sem=self.sems[2].at[w, k],
                                            recv_sem=self.sems[3].at[w, k], device_id=self.sib, device_id_type=MESH)

    def all(self):
        return range(len(self.outs))

    def start(self):
        for w in self.all():
            mine = self.piece(w, self.b, self.c)
            self.ici(w, 0, mine, self.xn).start()
            self.ici(w, 1, mine, self.yn).start()

    def hop(self, ws=None):
        c = self.c
        for w in ws or self.all():
            self.ici(w, 0, self.piece(w, self.bx, c), self.xn).wait_recv()
            self.ici(w, 1, self.piece(w, self.by, c), self.yn).wait_recv()
            self.ici(w, 2, self.piece(w, self.bx, c, 0), self.yn).start()
            self.ici(w, 3, self.piece(w, self.by, c, 1), self.xn).start()
            self.d2d(w, 0, self.piece(w, self.bx, c)).start()
            self.d2d(w, 1, self.piece(w, self.by, c)).start()

    def near_ready(self, ws=None):
        for w in ws or self.all():
            self.d2d(w, 0, self.piece(w, self.bx, 1 - self.c)).wait_recv()
            self.d2d(w, 1, self.piece(w, self.by, 1 - self.c)).wait_recv()

    def far(self, ws=None):
        c = self.c
        for w in ws or self.all():
            self.ici(w, 2, self.piece(w, self.bd, c, 0), self.yn).wait_recv()
            self.ici(w, 3, self.piece(w, self.bd, c, 1), self.xn).wait_recv()
            self.d2d(w, 2, self.piece(w, self.bd, c, 0)).start()
            self.d2d(w, 3, self.piece(w, self.bd, c, 1)).start()

    def far_ready(self, ws=None):
        for w in ws or self.all():
            self.d2d(w, 2, self.piece(w, self.bd, 1 - self.c, 0)).wait_recv()
            self.d2d(w, 3, self.piece(w, self.bd, 1 - self.c, 1)).wait_recv()

    def drain(self):
        c = self.c
        for w in self.all():
            mine = self.piece(w, self.b, c)
            self.ici(w, 0, mine, self.xn).wait_send()
            self.ici(w, 1, mine, self.yn).wait_send()
            self.ici(w, 2, self.piece(w, self.bx, c, 0), self.yn).wait_send()
            self.ici(w, 3, self.piece(w, self.by, c, 1), self.xn).wait_send()
            self.d2d(w, 0, self.piece(w, self.bx, c)).wait_send()
            self.d2d(w, 1, self.piece(w, self.by, c)).wait_send()
            self.d2d(w, 2, self.piece(w, self.bd, c, 0)).wait_send()
            self.d2d(w, 3, self.piece(w, self.bd, c, 1)).wait_send()


def _gather_sems(nw):
    return [pltpu.SemaphoreType.DMA((max(nw, 1), 4))] * 4


def _mixer_fwd(proj, cw8, lng, lnb, wc, bsb, fulls, tm=256):
    t = proj.shape[0]
    nt = t // tm
    nch = tm // CH
    nw = len(fulls)

    def body(*refs):
        p_ref, cw_ref, lng_ref, lnb_ref, wc_ref, bsb_ref = refs[:6]
        mix_ref = refs[6 + nw]
        w_outs = refs[7 + nw:7 + 2 * nw]
        prev_ref = refs[7 + 2 * nw]
        gather = _Gather(w_outs, *refs[8 + 2 * nw:])

        @pl.when(pl.program_id(0) == 0)
        def _():
            gather.start()
            prev_ref[...] = jnp.zeros_like(prev_ref)

        @pl.when(pl.program_id(0) == nt // 2)
        def _():
            gather.hop()

        @pl.when(pl.program_id(0) == nt - 1)
        def _():
            gather.far()

        rows = lax.broadcasted_iota(jnp.int32, (tm, CH), 0)
        for s in range(HEADS):
            cs = pl.ds(CH * s, CH)

            def slab(k):
                return p_ref[:, pl.ds(k * SLAB + CH * s, CH)].astype(F32)

            gb, gc, xa, za = slab(0), slab(1), slab(2), slab(3)
            cx = gc * xa
            p6 = jnp.broadcast_to(prev_ref[6:7, cs], (tm, CH))
            p7 = jnp.broadcast_to(prev_ref[7:8, cs], (tm, CH))
            c1 = jnp.where(rows == 0, p7, pltpu.roll(cx, 1, 0))
            c2 = jnp.where(rows == 0, p6, jnp.where(rows == 1, p7, pltpu.roll(cx, 2, 0)))
            prev_ref[:, cs] = cx[tm - 8:, :]
            cv = cw_ref[0:1, cs] * c2 + cw_ref[1:2, cs] * c1 + cw_ref[2:3, cs] * cx
            mix_ref[:, cs] = (gb * cv * (za * _sigmoid(za))).astype(BF16)

            u, v, zb = slab(4), slab(5), slab(6)
            ug, vg = _gelu(u), _gelu(v)
            dlt = vg - jnp.mean(vg, axis=-1, keepdims=True)
            vhat = dlt * lax.rsqrt(jnp.mean(dlt * dlt, axis=-1, keepdims=True) + EPS)
            vn = (vhat * lng_ref[:, cs] + lnb_ref[:, cs]).astype(BF16)
            gate = ug * (zb * _sigmoid(zb))
            for c in range(nch):
                rs = slice(CH * c, CH * (c + 1))
                sp = jnp.dot(wc_ref[s], vn[rs], preferred_element_type=F32) + bsb_ref[s]
                mix_ref[rs, pl.ds(SLAB + CH * s, CH)] = (gate[rs] * sp).astype(BF16)

        @pl.when(pl.program_id(0) == nt - 1)
        def _():
            gather.near_ready()
            gather.far_ready()
            gather.drain()

    sems = _gather_sems(nw)
    outs = pl.pallas_call(
        body, grid=(nt,),
        in_specs=[pl.BlockSpec((tm, IN_DIM), lambda i: (i, 0)), _full((8, D)), _full((1, D)), _full((1, D)),
                  _full((HEADS, CH, CH)), _full((HEADS, CH, CH))] + [ANY] * nw,
        out_specs=[pl.BlockSpec((tm, MIX), lambda i: (i, 0))] + [ANY] * nw,
        out_shape=[jax.ShapeDtypeStruct((t, MIX), BF16)] + [jax.ShapeDtypeStruct(f.shape, f.dtype) for f in fulls],
        input_output_aliases={6 + w: 1 + w for w in range(nw)},
        scratch_shapes=[pltpu.VMEM((8, D), F32)] + sems,
        compiler_params=_cp(("arbitrary",), VMEM_LIMIT), name="mixer_fwd")(proj, cw8, lng, lnb, wc, bsb, *fulls)
    return outs[0], outs[1:]


def _mem_fwd(mem, gm, wkv_f):
    n_mem = mem.shape[0]

    def body(mem_ref, gm_ref, w_ref, k_ref, v_ref):
        m, _ = _rms(mem_ref[...], gm_ref[...])
        mb = m.astype(BF16)
        for j in range(N_CHIP):
            dst = k_ref if j < 2 else v_ref
            dst[:, pl.ds(KV_BLK * (j % 2), KV_BLK)] = jnp.dot(mb, w_ref[j], preferred_element_type=F32).astype(BF16)

    return pl.pallas_call(
        body, out_shape=[jax.ShapeDtypeStruct((n_mem, D), BF16), jax.ShapeDtypeStruct((n_mem, D), BF16)],
        compiler_params=_cp(None, VMEM_LIMIT), name="mem_fwd")(mem, gm, wkv_f)


def _tail(x, tgt, mixin, wout, wq, wxo, k, v, g2, g3, tm=512, sub=512):
    t = x.shape[0]
    n_mem = k.shape[0]
    scale = 1.0 / math.sqrt(XD)

    def body(x_ref, tgt_ref, mix_ref, wout_ref, wq_ref, wxo_ref, k_ref, v_ref, g2_ref, g3_ref,
             loss_ref, dmix_ref, dx1b_ref, h2_ref, dq_ref, o_ref, dx2b_ref, dk_ref, dv_ref, dg2_ref, dg3_ref):
        @pl.when(pl.program_id(0) == 0)
        def _():
            loss_ref[...] = jnp.zeros_like(loss_ref)
            dk_ref[...] = jnp.zeros_like(dk_ref)
            dv_ref[...] = jnp.zeros_like(dv_ref)
            dg2_ref[...] = jnp.zeros_like(dg2_ref)
            dg3_ref[...] = jnp.zeros_like(dg3_ref)

        g2, g3 = g2_ref[...], g3_ref[...]
        for sb in range(tm // sub):
            rs = pl.ds(sub * sb, sub)
            x1 = x_ref[rs, :] + jnp.dot(mix_ref[rs, :], wout_ref[...], preferred_element_type=F32)
            h2, r2 = _rms(x1, g2)
            h2b = h2.astype(BF16)
            h2_ref[rs, :] = h2b
            q = jnp.dot(h2b, wq_ref[...], preferred_element_type=F32).astype(BF16)
            probs, outs = [], []
            for hd in range(XH):
                hs = pl.ds(XD * hd, XD)
                s = _bdot_nt(q[:, XD * hd:XD * (hd + 1)], k_ref[:, hs]) * scale
                e = jnp.exp(s - jnp.max(s, axis=-1, keepdims=True))
                p = e / jnp.sum(e, axis=-1, keepdims=True)
                probs.append(p)
                outs.append(_bdot(p, v_ref[:, hs]))
            ob = jnp.concatenate(outs, axis=-1).astype(BF16)
            o_ref[rs, :] = ob
            x2 = x1 + jnp.dot(ob, wxo_ref[...], preferred_element_type=F32)
            y, r3 = _rms(x2, g3)
            diff = y - tgt_ref[rs, :]
            row_loss = jnp.sum(diff * diff, axis=-1, keepdims=True)
            loss_ref[...] += jnp.broadcast_to(jnp.sum(row_loss, axis=0, keepdims=True) * (0.5 / D), loss_ref.shape)

            dx2, dg3 = _rms_bwd(diff * (1.0 / D), x2, r3, g3)
            dg3_ref[...] += dg3
            dx2b = dx2.astype(BF16)
            dx2b_ref[rs, :] = dx2b
            do = _bdot_nt(dx2b, wxo_ref[...])
            dqs = []
            for hd in range(XH):
                hs = pl.ds(XD * hd, XD)
                p = probs[hd]
                do_h = do[:, XD * hd:XD * (hd + 1)]
                dv_ref[:, hs] += _bdot_tn(p, do_h)
                dp = _bdot_nt(do_h, v_ref[:, hs])
                ds = p * (dp - jnp.sum(dp * p, axis=-1, keepdims=True))
                dqs.append(_bdot(ds, k_ref[:, hs]) * scale)
                dk_ref[:, hs] += _bdot_tn(ds, q[:, XD * hd:XD * (hd + 1)]) * scale
            dq = jnp.concatenate(dqs, axis=-1).astype(BF16)
            dq_ref[rs, :] = dq
            dx1n, dg2 = _rms_bwd(_bdot_nt(dq, wq_ref[...]), x1, r2, g2)
            dg2_ref[...] += dg2
            dx1b = (dx2 + dx1n).astype(BF16)
            dx1b_ref[rs, :] = dx1b
            dmix_ref[rs, :] = _bdot_nt(dx1b, wout_ref[...]).astype(BF16)

    tok = lambda w: pl.BlockSpec((tm, w), lambda i: (i, 0))
    return pl.pallas_call(
        body, grid=(t // tm,),
        in_specs=[tok(D), tok(D), tok(MIX), _full((MIX, D), 1), _full((D, D), 1), _full((D, D), 1),
                  _full((n_mem, D), 1), _full((n_mem, D), 1), _full((1, D)), _full((1, D))],
        out_specs=[_full((8, 128)), tok(MIX), tok(D), tok(D), tok(D), tok(D), tok(D),
                   _full((n_mem, D)), _full((n_mem, D)), _full((1, D)), _full((1, D))],
        out_shape=[jax.ShapeDtypeStruct((8, 128), F32), jax.ShapeDtypeStruct((t, MIX), BF16),
                   jax.ShapeDtypeStruct((t, D), BF16),
                   jax.ShapeDtypeStruct((t, D), BF16), jax.ShapeDtypeStruct((t, D), BF16),
                   jax.ShapeDtypeStruct((t, D), BF16), jax.ShapeDtypeStruct((t, D), BF16),
                   jax.ShapeDtypeStruct((n_mem, D), F32), jax.ShapeDtypeStruct((n_mem, D), F32),
                   jax.ShapeDtypeStruct((1, D), F32), jax.ShapeDtypeStruct((1, D), F32)],
        compiler_params=_cp(("arbitrary",), VMEM_LIMIT), name="tail")(x, tgt, mixin, wout, wq, wxo, k, v, g2, g3)


def _mem_bwd(mem, gm, dk, dv, wkv_f):
    def body(mem_ref, gm_ref, dk_ref, dv_ref, w_ref, dw_ref, dwb_ref, dgm_ref):
        mem_v = mem_ref[...]
        m, rm = _rms(mem_v, gm_ref[...])
        mb = m.astype(BF16)
        dm = jnp.zeros_like(mem_v)
        for j in range(N_CHIP):
            src = dk_ref if j < 2 else dv_ref
            dkv = src[:, pl.ds(KV_BLK * (j % 2), KV_BLK)].astype(BF16)
            dw = _bdot_tn(mb, dkv)
            dw_ref[j] = dw
            dwb_ref[j] = dw.astype(BF16)
            dm = dm + _bdot_nt(dkv, w_ref[j])
        dgm_ref[...] = jnp.sum(dm * mem_v * rm, axis=0, keepdims=True)

    return pl.pallas_call(
        body, out_shape=[jax.ShapeDtypeStruct((N_CHIP, D, KV_BLK), F32), jax.ShapeDtypeStruct((N_CHIP, D, KV_BLK), BF16),
                         jax.ShapeDtypeStruct((1, D), F32)],
        compiler_params=_cp(None, VMEM_LIMIT), name="mem_bwd")(mem, gm, dk, dv, wkv_f)


def _mixer_bwd(proj, dmix, cw8, lng, lnb, wc, wct, bsb, win_f, x, dx1, g1, tm=256):
    t = proj.shape[0]
    nt = t // tm
    nch = tm // CH
    hb = 16
    pair = 2 * CH
    assert pair == IN_PIECE

    def body(p_ref, pgc_ref, pxa_ref, dm_ref, cw_ref, lng_ref, lnb_ref, wc_ref, wct_ref, bsb_ref, w_ref, x_ref,
             dx1_ref, g1_ref, dp_ref, dcw_ref, dlng_ref, dlnb_ref, dwc_ref, dbs_ref, gx_ref, dg1_ref,
             next_ref, dh_ref):
        i = pl.program_id(0)

        @pl.when(i == 0)
        def _():
            next_ref[...] = jnp.zeros_like(next_ref)
            dcw_ref[...] = jnp.zeros_like(dcw_ref)
            dlng_ref[...] = jnp.zeros_like(dlng_ref)
            dlnb_ref[...] = jnp.zeros_like(dlnb_ref)
            dwc_ref[...] = jnp.zeros_like(dwc_ref)
            dbs_ref[...] = jnp.zeros_like(dbs_ref)
            dg1_ref[...] = jnp.zeros_like(dg1_ref)

        first_tile = i == nt - 1
        rows = lax.broadcasted_iota(jnp.int32, (tm, CH), 0)
        ones8 = jnp.ones((8, CH), BF16)
        for s in range(HEADS):
            cs = pl.ds(CH * s, CH)

            def slab(k):
                return p_ref[:, pl.ds(k * SLAB + CH * s, CH)].astype(F32)

            gb, gc, xa, za = slab(0), slab(1), slab(2), slab(3)
            da = dm_ref[:, cs].astype(F32)
            cx = gc * xa
            cxp = pgc_ref[:, cs].astype(F32) * pxa_ref[:, cs].astype(F32)
            cxp = jnp.where(first_tile, jnp.zeros_like(cxp), cxp)
            p6 = jnp.broadcast_to(cxp[hb - 2:hb - 1, :], (tm, CH))
            p7 = jnp.broadcast_to(cxp[hb - 1:hb, :], (tm, CH))
            c1 = jnp.where(rows == 0, p7, pltpu.roll(cx, 1, 0))
            c2 = jnp.where(rows == 0, p6, jnp.where(rows == 1, p7, pltpu.roll(cx, 2, 0)))
            w0, w1, w2 = cw_ref[0:1, cs], cw_ref[1:2, cs], cw_ref[2:3, cs]
            cv = w0 * c2 + w1 * c1 + w2 * cx
            sg = _sigmoid(za)
            sa = za * sg
            dcv = da * gb * sa
            dp_ref[:, pl.ds(0 * SLAB + CH * s, CH)] = (da * cv * sa).astype(BF16)
            dp_ref[:, pl.ds(3 * SLAB + CH * s, CH)] = (da * gb * cv * (sg * (1.0 + za * (1.0 - sg)))).astype(BF16)
            n0 = jnp.broadcast_to(next_ref[0:1, cs], (tm, CH))
            n1 = jnp.broadcast_to(next_ref[1:2, cs], (tm, CH))
            u1 = jnp.where(rows == tm - 1, n0, pltpu.roll(dcv, tm - 1, 0))
            u2 = jnp.where(rows == tm - 2, n0, jnp.where(rows == tm - 1, n1, pltpu.roll(dcv, tm - 2, 0)))
            next_ref[:, cs] = dcv[0:8, :]
            dcx = w2 * dcv + w1 * u1 + w0 * u2
            dp_ref[:, pl.ds(1 * SLAB + CH * s, CH)] = (dcx * xa).astype(BF16)
            dp_ref[:, pl.ds(2 * SLAB + CH * s, CH)] = (dcx * gc).astype(BF16)
            dcw_ref[0:1, cs] += jnp.sum(dcv * c2, axis=0, keepdims=True)
            dcw_ref[1:2, cs] += jnp.sum(dcv * c1, axis=0, keepdims=True)
            dcw_ref[2:3, cs] += jnp.sum(dcv * cx, axis=0, keepdims=True)

            u, v, zb = slab(4), slab(5), slab(6)
            db = dm_ref[:, pl.ds(SLAB + CH * s, CH)].astype(F32)
            ug, ugrad = _gelu_parts(u)
            vg, vgrad = _gelu_parts(v)
            dlt = vg - jnp.mean(vg, axis=-1, keepdims=True)
            rstd = lax.rsqrt(jnp.mean(dlt * dlt, axis=-1, keepdims=True) + EPS)
            vhat = dlt * rstd
            lg = lng_ref[:, cs]
            vn = (vhat * lg + lnb_ref[:, cs]).astype(BF16)
            sgb = _sigmoid(zb)
            szb = zb * sgb
            sps, dvns = [], []
            dbs = jnp.zeros((8, CH), F32)
            dwc = jnp.zeros((CH, CH), F32)
            for c in range(nch):
                rs = slice(CH * c, CH * (c + 1))
                sp = jnp.dot(wc_ref[s], vn[rs], preferred_element_type=F32) + bsb_ref[s]
                dsp = (db[rs] * ug[rs] * szb[rs]).astype(BF16)
                dbs = dbs + lax.dot_general(ones8, dsp, (((1,), (1,)), ((), ())), preferred_element_type=F32)
                dwc = dwc + lax.dot_general(dsp, vn[rs], (((1,), (1,)), ((), ())), preferred_element_type=F32)
                dvns.append(jnp.dot(wct_ref[s], dsp, preferred_element_type=F32))
                sps.append(sp)
            sp = jnp.concatenate(sps, axis=0)
            dvn = jnp.concatenate(dvns, axis=0)
            dbs_ref[:, cs] += dbs
            dwc_ref[s] += dwc
            dlng_ref[:, cs] += jnp.sum(dvn * vhat, axis=0, keepdims=True)
            dlnb_ref[:, cs] += jnp.sum(dvn, axis=0, keepdims=True)
            dvhat = dvn * lg
            dvg = rstd * (dvhat - jnp.mean(dvhat, axis=-1, keepdims=True)
                          - vhat * jnp.mean(dvhat * vhat, axis=-1, keepdims=True))
            dp_ref[:, pl.ds(4 * SLAB + CH * s, CH)] = (db * sp * szb * ugrad).astype(BF16)
            dp_ref[:, pl.ds(5 * SLAB + CH * s, CH)] = (dvg * vgrad).astype(BF16)
            dp_ref[:, pl.ds(6 * SLAB + CH * s, CH)] = (db * ug * sp * (sgb * (1.0 + zb * (1.0 - sgb)))).astype(BF16)

            if s % 2 == 1:
                part = None
                for k in range(N_SLAB):
                    col = k * SLAB + pair * (s // 2)
                    blk, off = divmod(col, IN_BLK)
                    term = lax.dot_general(dp_ref[:, pl.ds(col, pair)], w_ref[blk, off // IN_PIECE],
                                           (((1,), (1,)), ((), ())), preferred_element_type=F32)
                    part = term if part is None else part + term
                if s == 1:
                    dh_ref[...] = part
                else:
                    dh_ref[...] += part

        xv = x_ref[...]
        r = lax.rsqrt(jnp.mean(xv * xv, axis=-1, keepdims=True) + EPS)
        dxn, dg = _rms_bwd(dh_ref[...], xv, r, g1_ref[...])
        gx_ref[...] = dx1_ref[...].astype(F32) + dxn
        dg1_ref[0:1, :] += dg

        @pl.when(i == nt - 1)
        def _():
            tril = lax.broadcasted_iota(jnp.int32, (CH, CH), 0) >= lax.broadcasted_iota(jnp.int32, (CH, CH), 1)
            for s in range(HEADS):
                dwc_ref[s] = jnp.where(tril, dwc_ref[s], 0.0)

    rev = lambda i: nt - 1 - i
    halo = lambda col: pl.BlockSpec((hb, SLAB), lambda i: (jnp.maximum(rev(i) * (tm // hb) - 1, 0), col))
    tok = lambda w: pl.BlockSpec((tm, w), lambda i: (rev(i), 0))
    return pl.pallas_call(
        body, grid=(nt,),
        in_specs=[tok(IN_DIM), halo(1), halo(2), tok(MIX), _full((8, D)), _full((1, D)), _full((1, D)),
                  _full((HEADS, CH, CH)), _full((HEADS, CH, CH)), _full((HEADS, CH, CH)),
                  _full((N_CHIP, N_PIECE, D, IN_PIECE), 1), tok(D), tok(D), _full((1, D))],
        out_specs=[tok(IN_DIM), _full((8, D)), _full((1, D)), _full((1, D)), _full((HEADS, CH, CH)), _full((8, D)),
                   tok(D), _full((8, D))],
        out_shape=[jax.ShapeDtypeStruct((t, IN_DIM), BF16), jax.ShapeDtypeStruct((8, D), F32),
                   jax.ShapeDtypeStruct((1, D), F32), jax.ShapeDtypeStruct((1, D), F32),
                   jax.ShapeDtypeStruct((HEADS, CH, CH), F32), jax.ShapeDtypeStruct((8, D), F32),
                   jax.ShapeDtypeStruct((t, D), F32), jax.ShapeDtypeStruct((8, D), F32)],
        scratch_shapes=[pltpu.VMEM((8, D), F32), pltpu.VMEM((tm, D), F32)],
        compiler_params=_cp(("arbitrary",), VMEM_LIMIT), name="mixer_bwd")(
            proj, proj, proj, dmix, cw8, lng, lnb, wc, wct, bsb, win_f, x, dx1, g1)


def _grad_matmul(a, b, after, *, by_cols, name, tk=1024):
    t, m = a.shape
    n = b.shape[1]
    nk = t // tk
    nj = N_CHIP if by_cols else 1
    bn = n // nj

    def body(a_ref, b_ref, after_ref, o_ref, ob_ref):
        kk = pl.program_id(1)
        part = lax.dot_general(a_ref[...], b_ref[...], (((0,), (0,)), ((), ())), preferred_element_type=F32)

        @pl.when(kk == 0)
        def _():
            o_ref[...] = part

        @pl.when(kk > 0)
        def _():
            o_ref[...] += part

        @pl.when(kk == nk - 1)
        def _():
            ob_ref[...] = o_ref[...].astype(BF16)

    a_spec = pl.BlockSpec((tk, m), lambda j, k: (k, 0))
    b_spec = pl.BlockSpec((tk, bn), lambda j, k: (k, j))
    o_spec = pl.BlockSpec((None, m, bn), lambda j, k: (j, 0, 0))
    o32, o16 = pl.pallas_call(
        body, grid=(nj, nk), in_specs=[a_spec, b_spec, ANY], out_specs=[o_spec, o_spec],
        out_shape=[jax.ShapeDtypeStruct((nj, m, bn), F32), jax.ShapeDtypeStruct((nj, m, bn), BF16)],
        compiler_params=_cp(("parallel", "arbitrary"), VMEM_LIMIT), name=name)(a, b, after)
    if by_cols:
        return o32, o16
    return o32.reshape(N_CHIP, m // N_CHIP, n), o16.reshape(N_CHIP, m // N_CHIP, n)


def _coords():
    x, y, c = lax.axis_index("x"), lax.axis_index("y"), lax.axis_index("c")
    chips = [(1 - x, y), (x, 1 - y), (1 - x, 1 - y)]
    return x, y, c, chips


def _pair_reduce(c_idx, grads, grads_b, smalls, name):
    ng, ns = len(grads), len(smalls)
    halves = [g.shape[1] // 2 for g in grads]

    def body(c_ref, *refs):
        g_in, gb_any = refs[:ng], refs[ng:2 * ng]
        s_own, s_any = refs[2 * ng:2 * ng + ns], refs[2 * ng + ns:2 * ng + 2 * ns]
        o = refs[2 * ng + 2 * ns:4 * ng + 3 * ns]
        lands = refs[4 * ng + 3 * ns:5 * ng + 4 * ns]
        send, recv = refs[5 * ng + 4 * ns:]
        x, y, c, _ = _coords()
        j = pl.program_id(0)

        def big(i, blk):
            return pltpu.make_async_remote_copy(
                src_ref=gb_any[i].at[blk, pl.ds((1 - c) * halves[i], halves[i])], dst_ref=lands[i].at[blk],
                send_sem=send.at[i, blk], recv_sem=recv.at[i, blk], device_id=(x, y, 1 - c), device_id_type=MESH)

        def small(i):
            return pltpu.make_async_remote_copy(
                src_ref=s_any[i].at[1 - c], dst_ref=lands[ng + i],
                send_sem=send.at[ng + i, 0], recv_sem=recv.at[ng + i, 0], device_id=(x, y, 1 - c), device_id_type=MESH)

        @pl.when(j == 0)
        def _():
            for blk in range(N_CHIP):
                for i in range(ng):
                    big(i, blk).start()
            for i in range(ns):
                small(i).start()

        for i in range(ng):
            big(i, j).wait_recv()
            tot = g_in[i][...] + lands[i][j].astype(F32)
            o[i][...] = tot
            o[ng + i][...] = tot.astype(BF16)

        @pl.when(j == N_CHIP - 1)
        def _():
            for i in range(ns):
                small(i).wait_recv()
                o[2 * ng + i][...] = s_own[i][...] + lands[ng + i][...]
                small(i).wait_send()
            for blk in range(N_CHIP):
                for i in range(ng):
                    big(i, blk).wait_send()

    in_specs = [pl.BlockSpec((None, None, halves[i], g.shape[2]), lambda b, c: (b, c[0], 0, 0)) for i, g in enumerate(grads)]
    in_specs += [ANY] * ng
    in_specs += [pl.BlockSpec((None, s.shape[0] // 2, s.shape[1]), lambda b, c: (c[0], 0, 0)) for s in smalls]
    in_specs += [ANY] * ns
    blk = [pl.BlockSpec((None, halves[i], g.shape[2]), lambda b, c: (b, 0, 0)) for i, g in enumerate(grads)]
    out_specs = blk + blk + [pl.BlockSpec((s.shape[0] // 2, s.shape[1]), lambda b, c: (0, 0)) for s in smalls]
    out_shape = [jax.ShapeDtypeStruct((N_CHIP, halves[i], g.shape[2]), F32) for i, g in enumerate(grads)]
    out_shape += [jax.ShapeDtypeStruct((N_CHIP, halves[i], g.shape[2]), BF16) for i, g in enumerate(grads)]
    out_shape += [jax.ShapeDtypeStruct((s.shape[0] // 2, s.shape[1]), F32) for s in smalls]
    scratch = [pltpu.VMEM((N_CHIP, halves[i], g.shape[2]), BF16) for i, g in enumerate(grads)]
    scratch += [pltpu.VMEM((s.shape[0] // 2, s.shape[1]), F32) for s in smalls]
    scratch += [pltpu.SemaphoreType.DMA((ng + ns, N_CHIP)), pltpu.SemaphoreType.DMA((ng + ns, N_CHIP))]
    grads4 = [g.reshape(N_CHIP, 2, halves[i], g.shape[2]) for i, g in enumerate(grads)]
    smalls3 = [s.reshape(2, s.shape[0] // 2, s.shape[1]) for s in smalls]
    return pl.pallas_call(
        body, out_shape=out_shape,
        grid_spec=pltpu.PrefetchScalarGridSpec(num_scalar_prefetch=1, grid=(N_CHIP,), in_specs=in_specs,
                                               out_specs=out_specs, scratch_shapes=scratch),
        compiler_params=_cp(("arbitrary",), VMEM_LIMIT), name=name)(c_idx, *grads4, *grads_b, *smalls3, *smalls3)


_HBM = pl.BlockSpec(memory_space=pltpu.HBM)
_SEM = pl.BlockSpec(memory_space=pltpu.SEMAPHORE)


def _split_copies(ins, lands, ng, send, recv, arriving):
    x, y, c, chips = _coords()
    b = 2 * x + y
    copies = []
    for i in range(len(ins)):
        for k in range(3):
            blk = 2 * chips[k][0] + chips[k][1]
            src, dst, got = (ins[i].at[blk], lands[i].at[k], lands[i].at[k]) if i < ng else (ins[i], lands[i].at[b], lands[i].at[blk])
            sems = dict(send_sem=send.at[3 * i + k], recv_sem=recv.at[3 * i + k], device_id=(*chips[k], c), device_id_type=MESH)
            if arriving:
                copies.append(pltpu.make_async_remote_copy(src_ref=got, dst_ref=got, **sems))
            else:
                copies.append(pltpu.make_async_remote_copy(src_ref=src, dst_ref=dst, **sems))
    return copies


def _exchange_begin(sums_b, smalls, name):
    ng, n = len(sums_b), len(sums_b) + len(smalls)
    srcs = list(sums_b) + list(smalls)
    lands = [lax.empty((3,) + g.shape[1:], g.dtype) for g in sums_b] + [lax.empty((N_CHIP,) + s.shape, s.dtype) for s in smalls]

    def body(*refs):
        ins, land_refs = refs[:n], refs[n:2 * n]
        send, recv = refs[2 * n], refs[2 * n + 1]
        token = refs[4 * n + 2]
        for cp in _split_copies(ins, land_refs, ng, send, recv, False):
            cp.start()
        token[...] = jnp.zeros_like(token)

    hbm = lambda a: pltpu.HBM(a.shape, a.dtype)
    outs = pl.pallas_call(
        body, name=name,
        out_shape=(pltpu.SemaphoreType.DMA((3 * n,)), pltpu.SemaphoreType.DMA((3 * n,)), *[hbm(a) for a in srcs + lands],
                   jax.ShapeDtypeStruct((8, 128), F32)),
        in_specs=[_HBM] * (2 * n), out_specs=(_SEM, _SEM, *[_HBM] * (2 * n), pl.BlockSpec(memory_space=pltpu.VMEM)),
        input_output_aliases={i: 2 + i for i in range(2 * n)},
        compiler_params=pltpu.CompilerParams(has_side_effects=pltpu.SideEffectType.DATAFLOW_SIDE_EFFECTING),
    )(*[pltpu.with_memory_space_constraint(a, pltpu.HBM) for a in srcs + lands])
    return outs[0], outs[1], list(outs[2:2 + n]), list(outs[2 + n:2 + 2 * n]), outs[2 + 2 * n]


def _exchange_end(send, recv, srcs, lands, ng, after, name):
    n = len(srcs)
    after = list(after)

    def body(*refs):
        ins, land_refs = refs[:n], refs[n:2 * n]
        send_ref, recv_ref = refs[2 * n], refs[2 * n + 1]
        for cp in _split_copies(ins, land_refs, ng, send_ref, recv_ref, False):
            cp.wait_send()
        for cp in _split_copies(ins, land_refs, ng, send_ref, recv_ref, True):
            cp.wait_recv()

    hbm = lambda a: pltpu.HBM(a.shape, a.dtype)
    outs = pl.pallas_call(
        body, name=name, out_shape=tuple(hbm(a) for a in list(srcs) + list(lands)),
        in_specs=[_HBM] * (2 * n) + [_SEM, _SEM] + [ANY] * len(after), out_specs=tuple([_HBM] * (2 * n)),
        input_output_aliases={i: i for i in range(2 * n)},
        compiler_params=pltpu.CompilerParams(has_side_effects=pltpu.SideEffectType.DATAFLOW_SIDE_EFFECTING),
    )(*srcs, *lands, send, recv, *after)
    return list(outs[n:])


def _chip_reduce(bc_idx, sums, recvd, smalls_slots, smalls_own, name, steps=4):
    ng, ns = len(sums), len(smalls_slots)
    n = ng + ns
    assert steps >= 2
    halves = [g.shape[1] for g in sums] + [s.shape[1] for s in smalls_slots]
    rows = [g.shape[1] // steps for g in sums]

    def body(bc_ref, *refs):
        own, rx = refs[:ng], refs[ng:2 * ng]
        sl = refs[2 * ng:2 * ng + ns]
        sl_own = refs[2 * ng + ns:2 * ng + 2 * ns]
        o = refs[2 * ng + 2 * ns:2 * ng + 2 * ns + n]
        tiles = refs[2 * ng + 2 * ns + n:2 * ng + 2 * ns + 2 * n]
        keep, send, recv = refs[2 * ng + 2 * ns + 2 * n:]
        x, y, c, _ = _coords()
        sibling = dict(device_id=(x, y, 1 - c), device_id_type=MESH)
        r = pl.program_id(0)

        def writes(i, step, slot):
            dst = o[i].at[pl.ds(c * halves[i] + step * rows[i], rows[i])]
            return (pltpu.make_async_copy(tiles[i].at[slot], dst, keep.at[i, slot]),
                    pltpu.make_async_remote_copy(src_ref=tiles[i].at[slot], dst_ref=dst, send_sem=send.at[i, slot],
                                                 recv_sem=recv.at[i, step], **sibling))

        def small_writes(i):
            dst = o[i].at[pl.ds(c * halves[i], halves[i])]
            return (pltpu.make_async_copy(tiles[i], dst, keep.at[i, 0]),
                    pltpu.make_async_remote_copy(src_ref=tiles[i], dst_ref=dst, send_sem=send.at[i, 0],
                                                 recv_sem=recv.at[i, 0], **sibling))

        def arriving(i, step, nrows):
            dst = o[i].at[pl.ds((1 - c) * halves[i] + step * nrows, nrows)]
            return pltpu.make_async_remote_copy(src_ref=dst, dst_ref=dst, send_sem=send.at[i, 0], recv_sem=recv.at[i, step],
                                                **sibling)

        def finish(step, slot):
            for i in range(ng):
                local, remote = writes(i, step, slot)
                local.wait()
                remote.wait_send()

        @pl.when(r >= 2)
        def _():
            finish(r - 2, r % 2)

        for i in range(ng):
            tot = own[i][...]
            for j in range(3):
                tot = tot + rx[i][j].astype(F32)
            tiles[i][r % 2] = tot
            for cp in writes(i, r, r % 2):
                cp.start()

        @pl.when(r == 0)
        def _():
            for i in range(ns):
                term = [jnp.where(bc_ref[0] == kk, sl_own[i][...], sl[i][kk]) for kk in range(N_CHIP)]
                tiles[ng + i][...] = ((term[0] + term[1]) + term[2]) + term[3]
                for cp in small_writes(ng + i):
                    cp.start()

        @pl.when(r == steps - 1)
        def _():
            finish(steps - 2, (steps - 2) % 2)
            finish(steps - 1, (steps - 1) % 2)
            for i in range(ns):
                local, remote = small_writes(ng + i)
                local.wait()
                remote.wait_send()
                arriving(ng + i, 0, halves[ng + i]).wait_recv()
            for i in range(ng):
                for step in range(steps):
                    arriving(i, step, rows[i]).wait_recv()

    in_specs = [pl.BlockSpec((None, rows[i], g.shape[2]), lambda r, bc: (bc[0], r, 0)) for i, g in enumerate(sums)]
    in_specs += [pl.BlockSpec((3, rows[i], g.shape[2]), lambda r, bc: (0, r, 0)) for i, g in enumerate(sums)]
    in_specs += [pl.BlockSpec(s.shape, lambda r, bc: (0, 0, 0)) for s in smalls_slots]
    in_specs += [pl.BlockSpec(s.shape[1:], lambda r, bc: (0, 0)) for s in smalls_slots]
    out_shape = [jax.ShapeDtypeStruct((2 * g.shape[1], g.shape[2]), F32) for g in sums]
    out_shape += [jax.ShapeDtypeStruct((2 * s.shape[1], s.shape[2]), F32) for s in smalls_slots]
    scratch = [pltpu.VMEM((2, rows[i], g.shape[2]), F32) for i, g in enumerate(sums)]
    scratch += [pltpu.VMEM(s.shape[1:], F32) for s in smalls_slots]
    scratch += [pltpu.SemaphoreType.DMA((n, 2)), pltpu.SemaphoreType.DMA((n, 2)), pltpu.SemaphoreType.DMA((n, steps))]
    return list(pl.pallas_call(
        body, out_shape=out_shape,
        grid_spec=pltpu.PrefetchScalarGridSpec(num_scalar_prefetch=1, grid=(steps,), in_specs=in_specs,
                                               out_specs=[ANY] * n, scratch_shapes=scratch),
        compiler_params=_cp(("arbitrary",), VMEM_LIMIT), name=name)(bc_idx, *sums, *recvd, *smalls_slots, *smalls_own))


def _adamw_math(w, g, m, v):
    m2 = ADAM_B1 * m + (1.0 - ADAM_B1) * g
    v2 = ADAM_B2 * v + (1.0 - ADAM_B2) * (g * g)
    m_hat = m2 / (1.0 - ADAM_B1 ** ADAM_STEP)
    v_hat = v2 / (1.0 - ADAM_B2 ** ADAM_STEP)
    delta = -ADAM_LR * (m_hat / (jnp.sqrt(v_hat) + ADAM_EPS) + ADAM_WD * w)
    return delta, m2, v2


def _adamw_big(ws, gs, ms, vs, name, steps=8):
    n = len(ws)

    def body(*refs):
        for i in range(n):
            w_ref, g_ref, m_ref, v_ref = (refs[k * n + i] for k in range(4))
            d_ref, m2_ref, v2_ref, g2_ref = (refs[(4 + k) * n + i] for k in range(4))
            gv = g_ref[...]
            d_ref[...], m2_ref[...], v2_ref[...] = _adamw_math(w_ref[...], gv, m_ref[...], v_ref[...])
            g2_ref[...] = gv

    specs = [pl.BlockSpec((w.shape[0] // steps, w.shape[1]), lambda i: (i, 0)) for w in ws]
    shapes = [jax.ShapeDtypeStruct(w.shape, F32) for w in ws]
    outs = pl.pallas_call(
        body, grid=(steps,), in_specs=specs * 4, out_specs=specs * 4, out_shape=shapes * 4,
        compiler_params=_cp(("parallel",), VMEM_LIMIT), name=name)(*ws, *gs, *ms, *vs)
    return [tuple(outs[k * n + i] for k in range(4)) for i in range(n)]


def _adamw_small(groups):
    n = len(groups)

    def body(*refs):
        for i in range(n):
            w_ref, g_ref, m_ref, v_ref = refs[4 * i:4 * i + 4]
            d_ref, m2_ref, v2_ref = refs[4 * n + 3 * i:4 * n + 3 * i + 3]
            d_ref[...], m2_ref[...], v2_ref[...] = _adamw_math(w_ref[...], g_ref[...], m_ref[...], v_ref[...])

    flat = [a for grp in groups for a in grp]
    out_shape = [jax.ShapeDtypeStruct(grp[0].shape, F32) for grp in groups for _ in range(3)]
    outs = pl.pallas_call(body, out_shape=out_shape, name="adamw_small")(*flat)
    return [tuple(outs[3 * i:3 * i + 3]) for i in range(n)]


def kernel(x, mem, norm_mix_g, w_in, conv_w, gm_ln_g, gm_ln_b, gm_ws, gm_bs, w_out, norm_x_g, norm_mem_g, w_q, w_kv, w_xo, norm_final_g, loss_target, m_norm_mix_g, m_w_in, m_conv_w, m_gm_ln_g, m_gm_ln_b, m_gm_ws, m_gm_bs, m_w_out, m_norm_x_g, m_norm_mem_g, m_w_q, m_w_kv, m_w_xo, m_norm_final_g, v_norm_mix_g, v_w_in, v_conv_w, v_gm_ln_g, v_gm_ln_b, v_gm_ws, v_gm_bs, v_w_out, v_norm_x_g, v_norm_mem_g, v_w_q, v_w_kv, v_w_xo, v_norm_final_g):
    t = x.shape[1]
    xi = lax.axis_index("x")
    yi = lax.axis_index("y")
    ci = lax.axis_index("c")
    b_idx = jnp.reshape(2 * xi + yi, (1,)).astype(jnp.int32)
    c_idx = jnp.reshape(ci, (1,)).astype(jnp.int32)

    x2d, mem2d, tgt = x[0], mem[0], loss_target[0]
    big = [w_in[0], w_out[0], w_q[0], w_kv[0], w_xo[0]]
    big_m = [m_w_in[0], m_w_out[0], m_w_q[0], m_w_kv[0], m_w_xo[0]]
    big_v = [v_w_in[0], v_w_out[0], v_w_q[0], v_w_kv[0], v_w_xo[0]]
    g3 = norm_final_g.reshape(1, D)

    def pad8(a):
        return jnp.pad(a, ((0, 8 - a.shape[0]), (0, 0)))

    own_blocks = _cast_shards(b_idx, big)

    tril = jnp.tril(jnp.ones((CH, CH), bool))
    wc32 = jnp.where(tril[None], gm_ws[0], 0.0)
    wc = wc32.astype(BF16)
    wct = jnp.swapaxes(wc32, 1, 2).astype(BF16)
    bsb = jnp.broadcast_to(gm_bs[0][:, :, None], (HEADS, CH, CH))

    blk = 2 * xi + yi
    roles = jnp.stack([blk, blk ^ 2, blk ^ 1, blk ^ 3]).astype(jnp.int32)
    proj, hb, win_f, cw8, (wq_f,) = _proj_gather(
        roles, x2d, norm_mix_g, own_blocks[0], pad8(conv_w[0]), [own_blocks[2]])
    mixin, (wout_f, wkv_f, wxo_f) = _mixer_fwd(
        proj, cw8, gm_ln_g, gm_ln_b, wc, bsb, [own_blocks[1], own_blocks[3], own_blocks[4]])
    wout2, wq2, wxo2 = wout_f.reshape(MIX, D), wq_f.reshape(D, D), wxo_f.reshape(D, D)
    k, v = _mem_fwd(mem2d, norm_mem_g, wkv_f)

    (loss_tile, dmix, dx1b, h2b, dq, ob, dx2b, dk, dv, dg2, dg3) = _tail(
        x2d, tgt, mixin, wout2, wq2, wxo2, k, v, norm_x_g, g3)
    dwkv, dwkv_b, dgm = _mem_bwd(mem2d, norm_mem_g, dk, dv, wkv_f)
    dproj, dcw, dlng, dlnb, dwc, dbs8, grad_x, dg1 = _mixer_bwd(
        proj, dmix, cw8, gm_ln_g, gm_ln_b, wc, wct, bsb, win_f, x2d, dx1b, norm_mix_g)

    bc_idx = jnp.concatenate([b_idx, c_idx])
    dwin, dwin_b = _grad_matmul(hb, dproj, dgm, by_cols=True, name="grad_w_in", tk=2048)
    zero = jnp.zeros((1, D), F32)
    loss_row = jnp.broadcast_to(loss_tile[0:1, 0:1], (1, D))
    sv = jnp.concatenate([dg1[0:1], dg2, dgm, dg3, dlng, dlnb, dbs8[0:1], loss_row, dcw], axis=0)
    sw = dwc.reshape(HEADS * CH, CH)
    ps_b = _pair_reduce(c_idx, [dwin], [dwin_b], [sv, sw], "pair_reduce_b")
    sums_b, sums_b_b, psmall = list(ps_b[:1]), list(ps_b[1:2]), list(ps_b[2:])
    send_b, recv_b, src_b, land_b, token_b = _exchange_begin(sums_b_b, psmall, "exchange_b_begin")

    dwxo, dwxo_b = _grad_matmul(ob, dx2b, token_b, by_cols=False, name="grad_w_xo", tk=2048)
    dwq, dwq_b = _grad_matmul(h2b, dq, token_b, by_cols=False, name="grad_w_q", tk=2048)
    dwout, dwout_b = _grad_matmul(mixin, dx1b, token_b, by_cols=False, name="grad_w_out")
    ps_a = _pair_reduce(c_idx, [dwout, dwq, dwkv, dwxo], [dwout_b, dwq_b, dwkv_b, dwxo_b], [], "pair_reduce_a")
    sums_a, sums_a_b = list(ps_a[:4]), list(ps_a[4:8])
    send_a, recv_a, src_a, land_a, token_a = _exchange_begin(sums_a_b, [], "exchange_a_begin")

    rx2b = _exchange_end(send_b, recv_b, src_b, land_b, 1, [token_a], "exchange_b_end")
    gwin, svf, swf = _chip_reduce(bc_idx, sums_b, rx2b[:1], rx2b[1:], psmall, "chip_reduce_b")
    out_b = _adamw_big(big[:1], [gwin], big_m[:1], big_v[:1], "adamw_w_in")[0]

    def vec_pack(a1, a2, am, a3, lg, lb, bs):
        return jnp.concatenate([a1, a2, am, a3.reshape(1, D), lg, lb, bs.reshape(1, D), zero], axis=0)

    wv = vec_pack(norm_mix_g, norm_x_g, norm_mem_g, norm_final_g, gm_ln_g, gm_ln_b, gm_bs)
    mv = vec_pack(m_norm_mix_g, m_norm_x_g, m_norm_mem_g, m_norm_final_g, m_gm_ln_g, m_gm_ln_b, m_gm_bs)
    vv = vec_pack(v_norm_mix_g, v_norm_x_g, v_norm_mem_g, v_norm_final_g, v_gm_ln_g, v_gm_ln_b, v_gm_bs)
    loss = svf[7, 0]
    gcw = lax.dynamic_slice_in_dim(svf[8:16], blk * (D // N_CHIP), D // N_CHIP, axis=1)
    gws = swf
    gv = svf[0:8]
    (dv_, mv_, vv_), (dc_, mc_, vc_), (dws_, mws_, vws_) = _adamw_small([
        (wv, gv, mv, vv),
        (pad8(conv_w[0]), gcw, pad8(m_conv_w[0]), pad8(v_conv_w[0])),
        (gm_ws.reshape(HEADS * CH, CH), gws, m_gm_ws.reshape(HEADS * CH, CH), v_gm_ws.reshape(HEADS * CH, CH))])

    rx2a = _exchange_end(send_a, recv_a, src_a, land_a, 4, [out_b[0], dv_], "exchange_a_end")
    g_a = _chip_reduce(bc_idx, sums_a, rx2a, [], [], "chip_reduce_a")
    big_out = [out_b] + _adamw_big(big[1:], g_a, big_m[1:], big_v[1:], "adamw_rest")

    def unpack(vecs, cw, ws, bigs):
        r = lambda i: vecs[i:i + 1]
        return [r(0), bigs[0][None], cw[0:3][None], r(4), r(5), ws.reshape(1, HEADS, CH, CH), vecs[6].reshape(1, HEADS, CH),
                bigs[1][None], r(1), r(2), bigs[2][None], bigs[3][None], bigs[4][None], vecs[3]]

    grads_out = unpack(gv, gcw, gws, [o[3] for o in big_out])
    delta_out = unpack(dv_, dc_, dws_, [o[0] for o in big_out])
    m_out = unpack(mv_, mc_, mws_, [o[1] for o in big_out])
    v_out = unpack(vv_, vc_, vws_, [o[2] for o in big_out])
    return (loss, grad_x[None], *grads_out, *delta_out, *m_out, *v_out)
```

```python
import functools
import math

import jax
import jax.numpy as jnp
from jax import lax
from jax.experimental import pallas as pl
from jax.experimental.pallas import tpu as pltpu

F32 = jnp.float32
BF16 = jnp.bfloat16
MESH = pl.DeviceIdType.MESH

D = 1024
SLAB = 1024
N_SLAB = 7
IN_DIM = N_SLAB * SLAB
MIX = 2 * SLAB
HEADS = 8
CH = 128
XH = 4
XD = D // XH
EPS = 1e-6
GELU_C = math.sqrt(2.0 / math.pi)
GELU_A = 0.044715
N_CHIP = 4
IN_BLK = IN_DIM // N_CHIP
IN_PIECE = 256
N_PIECE = IN_BLK // IN_PIECE
KV_BLK = 2 * D // N_CHIP

ADAM_LR, ADAM_B1, ADAM_B2, ADAM_EPS, ADAM_WD, ADAM_STEP = 0.001, 0.9, 0.999, 1e-08, 0.01, 10

VMEM_LIMIT = 60 * 1024 * 1024


def _cp(sem=None, vmem=None):
    return pltpu.CompilerParams(dimension_semantics=sem, vmem_limit_bytes=vmem)


def _full(shape, buffers=None):
    n = len(shape)
    if buffers is None:
        return pl.BlockSpec(shape, lambda *_: (0,) * n)
    return pl.BlockSpec(shape, lambda *_: (0,) * n, pipeline_mode=pl.Buffered(buffers))


ANY = pl.BlockSpec(memory_space=pl.ANY)


def _bdot(a, b):
    return jnp.dot(a.astype(BF16), b.astype(BF16), preferred_element_type=F32)


def _bdot_nt(a, b):
    return lax.dot_general(a.astype(BF16), b.astype(BF16), (((1,), (1,)), ((), ())), preferred_element_type=F32)


def _bdot_tn(a, b):
    return lax.dot_general(a.astype(BF16), b.astype(BF16), (((0,), (0,)), ((), ())), preferred_element_type=F32)


def _rms(x, g):
    r = lax.rsqrt(jnp.mean(x * x, axis=-1, keepdims=True) + EPS)
    return x * r * g, r


def _rms_bwd(dy, x, r, g):
    gdy = dy * g
    dx = r * gdy - x * (r * r * r) * jnp.mean(x * gdy, axis=-1, keepdims=True)
    dg = jnp.sum(dy * x * r, axis=0, keepdims=True)
    return dx, dg


def _gelu_parts(x):
    x2 = x * x
    t = jnp.tanh(GELU_C * (x + GELU_A * x * x2))
    val = 0.5 * x * (1.0 + t)
    grad = 0.5 * (1.0 + t) + 0.5 * x * (1.0 - t * t) * (GELU_C * (1.0 + 3.0 * GELU_A * x2))
    return val, grad


def _gelu(x):
    return 0.5 * x * (1.0 + jnp.tanh(GELU_C * (x + GELU_A * x * x * x)))


def _sigmoid(z):
    return 1.0 / (1.0 + jnp.exp(-z))


def _cast_shards(b_idx, arrs):
    n = len(arrs)
    steps = 8

    def body(b_ref, *refs):
        for p in range(N_PIECE):
            refs[n][p] = refs[0][:, pl.ds(p * IN_PIECE, IN_PIECE)].astype(BF16)
        for i in range(1, n):
            refs[n + i][...] = refs[i][...].astype(BF16)

    rows = [a.shape[0] // steps for a in arrs]
    in_specs = [pl.BlockSpec((rows[i], a.shape[1]), lambda i, b: (i, 0)) for i, a in enumerate(arrs)]
    out_specs = [pl.BlockSpec((None, N_PIECE, rows[0], IN_PIECE), lambda i, b: (b[0], 0, i, 0))]
    out_specs += [pl.BlockSpec((None, rows[i], a.shape[1]), lambda i, b: (b[0], i, 0)) for i, a in enumerate(arrs) if i > 0]
    out_shape = [jax.ShapeDtypeStruct((N_CHIP, N_PIECE, arrs[0].shape[0], IN_PIECE), BF16)]
    out_shape += [jax.ShapeDtypeStruct((N_CHIP,) + a.shape, BF16) for a in arrs[1:]]
    return pl.pallas_call(
        body, out_shape=out_shape,
        grid_spec=pltpu.PrefetchScalarGridSpec(num_scalar_prefetch=1, grid=(steps,), in_specs=in_specs, out_specs=out_specs),
        compiler_params=_cp(("arbitrary",)), name="cast_shards")(b_idx, *arrs)


def _proj_gather(roles, x, g, win_own, cw8s, more, tm=1024):
    t = x.shape[0]
    ni = t // tm
    nm = len(more)
    chunks = ((0, 4), (4, N_PIECE))
    work = ((0, 0), (0, 1), (1, 0), (2, 0), (1, 1), (2, 1), (3, 0), (3, 1))
    wide = max(p1 - p0 for p0, p1 in chunks) * IN_PIECE
    send_at = (0, 0, 1, 2, 4, 6, 9)
    hop_at = (2, 3, 5, 8, 11, 14, 17)
    far_at = (18, 20, 21, 23, 24, 25, 27)
    more_at = hop_at[-1] + 1

    def body(*refs):
        role_ref, x_any, g_ref, win_in, cw_in = refs[:5]
        o_any, hb_any, win_f, cw_out = refs[5 + nm:9 + nm]
        more_out = refs[9 + nm:9 + 2 * nm]
        hbuf, xbuf, wv, obuf, cw_s, cw_r, loc, wsem, osem = refs[9 + 2 * nm:18 + 2 * nm]
        g_in = _Gather([win_f.at[:, p] for p in range(N_PIECE)], *refs[18 + 2 * nm:22 + 2 * nm])
        g_more = _Gather(more_out, *refs[22 + 2 * nm:26 + 2 * nm])
        x, y, c, chips = _coords()
        b = 2 * x + y
        blks = [2 * chip[0] + chip[1] for chip in chips]

        def cw_cols(blk):
            return cw_out.at[:, pl.ds(blk * (D // N_CHIP), D // N_CHIP)]

        def cw_copy(k, blk):
            src = cw_in if blk is None else cw_cols(blk)
            return pltpu.make_async_remote_copy(src_ref=src, dst_ref=cw_cols(b if blk is None else blk), send_sem=cw_s.at[k],
                                                recv_sem=cw_r.at[k], device_id=(*chips[k], c), device_id_type=MESH)

        cw_local = pltpu.make_async_copy(cw_in, cw_cols(b), loc.at[1])
        hb_copy = pltpu.make_async_copy(hbuf, hb_any, loc.at[0])

        def x_rows(i):
            return pltpu.make_async_copy(x_any.at[pl.ds(i * tm, tm)], xbuf.at[i % 2], loc.at[2 + i % 2])

        def loads(wi):
            role, (p0, p1) = work[wi][0], chunks[work[wi][1]]
            return [pltpu.make_async_copy(win_f.at[role_ref[role], p], wv.at[wi % 2, :, pl.ds((p - p0) * IN_PIECE, IN_PIECE)],
                                          wsem.at[wi % 2, p - p0]) for p in range(p0, p1)]

        def out_copy(wi, i, n):
            role, (p0, p1) = work[wi][0], chunks[work[wi][1]]
            col = pl.multiple_of(role_ref[role] * IN_BLK + p0 * IN_PIECE, IN_PIECE)
            width = (p1 - p0) * IN_PIECE
            return pltpu.make_async_copy(obuf.at[n % 2, :, pl.ds(0, width)],
                                         o_any.at[pl.ds(i * tm, tm), pl.ds(col, width)], osem.at[n % 2])

        done = {g_in.hop: set(), g_in.near_ready: set(), g_in.far: set(), g_in.far_ready: set()}

        def once(step, ps):
            ps = [p for p in ps if p not in done[step]]
            done[step].update(ps)
            if ps:
                step(ps)

        def ensure(wi):
            role, ps = work[wi][0], list(range(*chunks[work[wi][1]]))
            if role in (1, 2):
                once(g_in.hop, ps)
                once(g_in.near_ready, ps)
            elif role == 3:
                once(g_in.far, ps)
                once(g_in.far_ready, ps)

        cw_local.start()
        for k in range(3):
            cw_copy(k, None).start()
        for cp in loads(0):
            cp.start()
        x_rows(0).start()

        n = 0
        for wi in range(len(work)):
            width = (chunks[work[wi][1]][1] - chunks[work[wi][1]][0]) * IN_PIECE
            for i in range(ni):
                rows = pl.ds(i * tm, tm)
                for p in range(N_PIECE):
                    if send_at[p] == n:
                        g_in.start([p])
                    if hop_at[p] == n:
                        once(g_in.hop, [p])
                    if far_at[p] == n:
                        once(g_in.far, [p])
                if more_at == n:
                    g_more.start()
                if wi == 0:
                    if i + 1 < ni:
                        x_rows(i + 1).start()
                    x_rows(i).wait()
                    h, _ = _rms(xbuf[i % 2], g_ref[...])
                    hbuf[rows, :] = h.astype(BF16)
                if i == 0:
                    for cp in loads(wi):
                        cp.wait()
                if i == ni - 1 and wi + 1 < len(work):
                    ensure(wi + 1)
                    for cp in loads(wi + 1):
                        cp.start()
                if n >= 2:
                    prev = n - 2
                    out_copy(prev // ni, prev % ni, prev).wait()
                obuf[n % 2, :, pl.ds(0, width)] = jnp.dot(
                    hbuf[rows, :], wv[wi % 2, :, pl.ds(0, width)], preferred_element_type=F32).astype(BF16)
                out_copy(wi, i, n).start()
                n += 1
            if wi == 0:
                hb_copy.start()

        g_more.hop()
        g_more.far()
        for prev in (n - 2, n - 1):
            out_copy(prev // ni, prev % ni, prev).wait()
        for k in range(3):
            cw_copy(k, blks[k]).wait_recv()
        for k in range(3):
            cw_copy(k, None).wait_send()
        cw_local.wait()
        hb_copy.wait()
        g_more.near_ready()
        g_more.far_ready()
        g_in.drain()
        g_more.drain()

    smem = pl.BlockSpec(memory_space=pltpu.SMEM)
    outs = pl.pallas_call(
        body, out_shape=[jax.ShapeDtypeStruct((t, IN_DIM), BF16), jax.ShapeDtypeStruct((t, D), BF16),
                         jax.ShapeDtypeStruct(win_own.shape, BF16), jax.ShapeDtypeStruct((8, D), F32)]
        + [jax.ShapeDtypeStruct(f.shape, f.dtype) for f in more],
        in_specs=[smem, ANY, pl.BlockSpec(memory_space=pltpu.VMEM), ANY, ANY] + [ANY] * nm,
        out_specs=[ANY, ANY, ANY, ANY] + [ANY] * nm,
        scratch_shapes=[pltpu.VMEM((t, D), BF16), pltpu.VMEM((2, tm, D), F32), pltpu.VMEM((2, D, wide), BF16),
                        pltpu.VMEM((2, tm, wide), BF16)]
        + [pltpu.SemaphoreType.DMA((3,))] * 2 + [pltpu.SemaphoreType.DMA((4,)), pltpu.SemaphoreType.DMA((2, 4)),
                                                 pltpu.SemaphoreType.DMA((2,))]
        + _gather_sems(N_PIECE) + _gather_sems(nm),
        input_output_aliases={3: 2, **{5 + w: 4 + w for w in range(nm)}},
        compiler_params=_cp(None, VMEM_LIMIT), name="proj_gather")(roles, x, g, win_own, cw8s, *more)
    return outs[0], outs[1], outs[2], outs[3], outs[4:]


class _Gather:
    def __init__(self, outs, ici_s, ici_r, d2d_s, d2d_r):
        x, y, c, _ = _coords()
        self.outs, self.c = outs, c
        self.sems = ici_s, ici_r, d2d_s, d2d_r
        self.b, self.bx, self.by, self.bd = 2 * x + y, 2 * (1 - x) + y, 2 * x + (1 - y), 2 * (1 - x) + (1 - y)
        self.xn, self.yn, self.sib = (1 - x, y, c), (x, 1 - y, c), (x, y, 1 - c)

    def piece(self, w, blk, hc, quarter=None):
        hr = self.outs[w].shape[1] // 2
        if quarter is None:
            return self.outs[w].at[blk, pl.ds(hc * hr, hr)]
        return self.outs[w].at[blk, pl.ds(hc * hr + quarter * (hr // 2), hr // 2)]

    def ici(self, w, k, ref, to):
        return pltpu.make_async_remote_copy(src_ref=ref, dst_ref=ref, send_sem=self.sems[0].at[w, k],
                                            recv_sem=self.sems[1].at[w, k], device_id=to, device_id_type=MESH)

    def d2d(self, w, k, ref):
        return pltpu.make_async_remote_copy(src_ref=ref, dst_ref=ref, send_sem=self.sems[2].at[w, k],
                                            recv_sem=self.sems[3].at[w, k], device_id=self.sib, device_id_type=MESH)

    def all(self):
        return range(len(self.outs))

    def start(self, ws=None):
        for w in ws or self.all():
            mine = self.piece(w, self.b, self.c)
            self.ici(w, 0, mine, self.xn).start()
            self.ici(w, 1, mine, self.yn).start()

    def hop(self, ws=None):
        c = self.c
        for w in ws or self.all():
            self.ici(w, 0, self.piece(w, self.bx, c), self.xn).wait_recv()
            self.ici(w, 1, self.piece(w, self.by, c), self.yn).wait_recv()
            self.ici(w, 2, self.piece(w, self.bx, c, 0), self.yn).start()
            self.ici(w, 3, self.piece(w, self.by, c, 1), self.xn).start()
            self.d2d(w, 0, self.piece(w, self.bx, c)).start()
            self.d2d(w, 1, self.piece(w, self.by, c)).start()

    def near_ready(self, ws=None):
        for w in ws or self.all():
            self.d2d(w, 0, self.piece(w, self.bx, 1 - self.c)).wait_recv()
            self.d2d(w, 1, self.piece(w, self.by, 1 - self.c)).wait_recv()

    def far(self, ws=None):
        c = self.c
        for w in ws or self.all():
            self.ici(w, 2, self.piece(w, self.bd, c, 0), self.yn).wait_recv()
            self.ici(w, 3, self.piece(w, self.bd, c, 1), self.xn).wait_recv()
            self.d2d(w, 2, self.piece(w, self.bd, c, 0)).start()
            self.d2d(w, 3, self.piece(w, self.bd, c, 1)).start()

    def far_ready(self, ws=None):
        for w in ws or self.all():
            self.d2d(w, 2, self.piece(w, self.bd, 1 - self.c, 0)).wait_recv()
            self.d2d(w, 3, self.piece(w, self.bd, 1 - self.c, 1)).wait_recv()

    def drain(self):
        c = self.c
        for w in self.all():
            mine = self.piece(w, self.b, c)
            self.ici(w, 0, mine, self.xn).wait_send()
            self.ici(w, 1, mine, self.yn).wait_send()
            self.ici(w, 2, self.piece(w, self.bx, c, 0), self.yn).wait_send()
            self.ici(w, 3, self.piece(w, self.by, c, 1), self.xn).wait_send()
            self.d2d(w, 0, self.piece(w, self.bx, c)).wait_send()
            self.d2d(w, 1, self.piece(w, self.by, c)).wait_send()
            self.d2d(w, 2, self.piece(w, self.bd, c, 0)).wait_send()
            self.d2d(w, 3, self.piece(w, self.bd, c, 1)).wait_send()


def _gather_sems(nw):
    return [pltpu.SemaphoreType.DMA((max(nw, 1), 4))] * 4


def _mixer_fwd(proj, cw8, lng, lnb, wc, bsb, fulls, tm=256):
    t = proj.shape[0]
    nt = t // tm
    nch = tm // CH
    nw = len(fulls)

    def body(*refs):
        p_ref, cw_ref, lng_ref, lnb_ref, wc_ref, bsb_ref = refs[:6]
        mix_ref = refs[6 + nw]
        w_outs = refs[7 + nw:7 + 2 * nw]
        prev_ref = refs[7 + 2 * nw]
        gather = _Gather(w_outs, *refs[8 + 2 * nw:])

        @pl.when(pl.program_id(0) == 0)
        def _():
            gather.start()
            prev_ref[...] = jnp.zeros_like(prev_ref)

        @pl.when(pl.program_id(0) == nt // 2)
        def _():
            gather.hop()

        @pl.when(pl.program_id(0) == nt - 1)
        def _():
            gather.far()

        rows = lax.broadcasted_iota(jnp.int32, (tm, CH), 0)
        for s in range(HEADS):
            cs = pl.ds(CH * s, CH)

            def slab(k):
                return p_ref[:, pl.ds(k * SLAB + CH * s, CH)].astype(F32)

            gb, gc, xa, za = slab(0), slab(1), slab(2), slab(3)
            cx = gc * xa
            p6 = jnp.broadcast_to(prev_ref[6:7, cs], (tm, CH))
            p7 = jnp.broadcast_to(prev_ref[7:8, cs], (tm, CH))
            c1 = jnp.where(rows == 0, p7, pltpu.roll(cx, 1, 0))
            c2 = jnp.where(rows == 0, p6, jnp.where(rows == 1, p7, pltpu.roll(cx, 2, 0)))
            prev_ref[:, cs] = cx[tm - 8:, :]
            cv = cw_ref[0:1, cs] * c2 + cw_ref[1:2, cs] * c1 + cw_ref[2:3, cs] * cx
            mix_ref[:, cs] = (gb * cv * (za * _sigmoid(za))).astype(BF16)

            u, v, zb = slab(4), slab(5), slab(6)
            ug, vg = _gelu(u), _gelu(v)
            dlt = vg - jnp.mean(vg, axis=-1, keepdims=True)
            vhat = dlt * lax.rsqrt(jnp.mean(dlt * dlt, axis=-1, keepdims=True) + EPS)
            vn = (vhat * lng_ref[:, cs] + lnb_ref[:, cs]).astype(BF16)
            gate = ug * (zb * _sigmoid(zb))
            for c in range(nch):
                rs = slice(CH * c, CH * (c + 1))
                sp = jnp.dot(wc_ref[s], vn[rs], preferred_element_type=F32) + bsb_ref[s]
                mix_ref[rs, pl.ds(SLAB + CH * s, CH)] = (gate[rs] * sp).astype(BF16)

        @pl.when(pl.program_id(0) == nt - 1)
        def _():
            gather.near_ready()
            gather.far_ready()
            gather.drain()

    sems = _gather_sems(nw)
    outs = pl.pallas_call(
        body, grid=(nt,),
        in_specs=[pl.BlockSpec((tm, IN_DIM), lambda i: (i, 0)), _full((8, D)), _full((1, D)), _full((1, D)),
                  _full((HEADS, CH, CH)), _full((HEADS, CH, CH))] + [ANY] * nw,
        out_specs=[pl.BlockSpec((tm, MIX), lambda i: (i, 0))] + [ANY] * nw,
        out_shape=[jax.ShapeDtypeStruct((t, MIX), BF16)] + [jax.ShapeDtypeStruct(f.shape, f.dtype) for f in fulls],
        input_output_aliases={6 + w: 1 + w for w in range(nw)},
        scratch_shapes=[pltpu.VMEM((8, D), F32)] + sems,
        compiler_params=_cp(("arbitrary",), VMEM_LIMIT), name="mixer_fwd")(proj, cw8, lng, lnb, wc, bsb, *fulls)
    return outs[0], outs[1:]


def _mem_fwd(mem, gm, wkv_f):
    n_mem = mem.shape[0]

    def body(mem_ref, gm_ref, w_ref, k_ref, v_ref):
        m, _ = _rms(mem_ref[...], gm_ref[...])
        mb = m.astype(BF16)
        for j in range(N_CHIP):
            dst = k_ref if j < 2 else v_ref
            dst[:, pl.ds(KV_BLK * (j % 2), KV_BLK)] = jnp.dot(mb, w_ref[j], preferred_element_type=F32).astype(BF16)

    return pl.pallas_call(
        body, out_shape=[jax.ShapeDtypeStruct((n_mem, D), BF16), jax.ShapeDtypeStruct((n_mem, D), BF16)],
        compiler_params=_cp(None, VMEM_LIMIT), name="mem_fwd")(mem, gm, wkv_f)


def _tail(x, tgt, mixin, wout, wq, wxo, k, v, g2, g3, tm=512, sub=512):
    t = x.shape[0]
    n_mem = k.shape[0]
    scale = 1.0 / math.sqrt(XD)

    def body(x_ref, tgt_ref, mix_ref, wout_ref, wq_ref, wxo_ref, k_ref, v_ref, g2_ref, g3_ref,
             loss_ref, dmix_ref, dx1b_ref, h2_ref, dq_ref, o_ref, dx2b_ref, dk_ref, dv_ref, dg2_ref, dg3_ref):
        @pl.when(pl.program_id(0) == 0)
        def _():
            loss_ref[...] = jnp.zeros_like(loss_ref)
            dk_ref[...] = jnp.zeros_like(dk_ref)
            dv_ref[...] = jnp.zeros_like(dv_ref)
            dg2_ref[...] = jnp.zeros_like(dg2_ref)
            dg3_ref[...] = jnp.zeros_like(dg3_ref)

        g2, g3 = g2_ref[...], g3_ref[...]
        for sb in range(tm // sub):
            rs = pl.ds(sub * sb, sub)
            x1 = x_ref[rs, :] + jnp.dot(mix_ref[rs, :], wout_ref[...], preferred_element_type=F32)
            h2, r2 = _rms(x1, g2)
            h2b = h2.astype(BF16)
            h2_ref[rs, :] = h2b
            q = jnp.dot(h2b, wq_ref[...], preferred_element_type=F32).astype(BF16)
            probs, outs = [], []
            for hd in range(XH):
                hs = pl.ds(XD * hd, XD)
                s = _bdot_nt(q[:, XD * hd:XD * (hd + 1)], k_ref[:, hs]) * scale
                e = jnp.exp(s - jnp.max(s, axis=-1, keepdims=True))
                p = e / jnp.sum(e, axis=-1, keepdims=True)
                probs.append(p)
                outs.append(_bdot(p, v_ref[:, hs]))
            ob = jnp.concatenate(outs, axis=-1).astype(BF16)
            o_ref[rs, :] = ob
            x2 = x1 + jnp.dot(ob, wxo_ref[...], preferred_element_type=F32)
            y, r3 = _rms(x2, g3)
            diff = y - tgt_ref[rs, :]
            row_loss = jnp.sum(diff * diff, axis=-1, keepdims=True)
            loss_ref[...] += jnp.broadcast_to(jnp.sum(row_loss, axis=0, keepdims=True) * (0.5 / D), loss_ref.shape)

            dx2, dg3 = _rms_bwd(diff * (1.0 / D), x2, r3, g3)
            dg3_ref[...] += dg3
            dx2b = dx2.astype(BF16)
            dx2b_ref[rs, :] = dx2b
            do = _bdot_nt(dx2b, wxo_ref[...])
            dqs = []
            for hd in range(XH):
                hs = pl.ds(XD * hd, XD)
                p = probs[hd]
                do_h = do[:, XD * hd:XD * (hd + 1)]
                dv_ref[:, hs] += _bdot_tn(p, do_h)
                dp = _bdot_nt(do_h, v_ref[:, hs])
                ds = p * (dp - jnp.sum(dp * p, axis=-1, keepdims=True))
                dqs.append(_bdot(ds, k_ref[:, hs]) * scale)
                dk_ref[:, hs] += _bdot_tn(ds, q[:, XD * hd:XD * (hd + 1)]) * scale
            dq = jnp.concatenate(dqs, axis=-1).astype(BF16)
            dq_ref[rs, :] = dq
            dx1n, dg2 = _rms_bwd(_bdot_nt(dq, wq_ref[...]), x1, r2, g2)
            dg2_ref[...] += dg2
            dx1b = (dx2 + dx1n).astype(BF16)
            dx1b_ref[rs, :] = dx1b
            dmix_ref[rs, :] = _bdot_nt(dx1b, wout_ref[...]).astype(BF16)

    tok = lambda w: pl.BlockSpec((tm, w), lambda i: (i, 0))
    return pl.pallas_call(
        body, grid=(t // tm,),
        in_specs=[tok(D), tok(D), tok(MIX), _full((MIX, D), 1), _full((D, D), 1), _full((D, D), 1),
                  _full((n_mem, D), 1), _full((n_mem, D), 1), _full((1, D)), _full((1, D))],
        out_specs=[_full((8, 128)), tok(MIX), tok(D), tok(D), tok(D), tok(D), tok(D),
                   _full((n_mem, D)), _full((n_mem, D)), _full((1, D)), _full((1, D))],
        out_shape=[jax.ShapeDtypeStruct((8, 128), F32), jax.ShapeDtypeStruct((t, MIX), BF16),
                   jax.ShapeDtypeStruct((t, D), BF16),
                   jax.ShapeDtypeStruct((t, D), BF16), jax.ShapeDtypeStruct((t, D), BF16),
                   jax.ShapeDtypeStruct((t, D), BF16), jax.ShapeDtypeStruct((t, D), BF16),
                   jax.ShapeDtypeStruct((n_mem, D), F32), jax.ShapeDtypeStruct((n_mem, D), F32),
                   jax.ShapeDtypeStruct((1, D), F32), jax.ShapeDtypeStruct((1, D), F32)],
        compiler_params=_cp(("arbitrary",), VMEM_LIMIT), name="tail")(x, tgt, mixin, wout, wq, wxo, k, v, g2, g3)


def _mem_bwd(mem, gm, dk, dv, wkv_f):
    def body(mem_ref, gm_ref, dk_ref, dv_ref, w_ref, dw_ref, dwb_ref, dgm_ref):
        mem_v = mem_ref[...]
        m, rm = _rms(mem_v, gm_ref[...])
        mb = m.astype(BF16)
        dm = jnp.zeros_like(mem_v)
        for j in range(N_CHIP):
            src = dk_ref if j < 2 else dv_ref
            dkv = src[:, pl.ds(KV_BLK * (j % 2), KV_BLK)].astype(BF16)
            dw = _bdot_tn(mb, dkv)
            dw_ref[j] = dw
            dwb_ref[j] = dw.astype(BF16)
            dm = dm + _bdot_nt(dkv, w_ref[j])
        dgm_ref[...] = jnp.sum(dm * mem_v * rm, axis=0, keepdims=True)

    return pl.pallas_call(
        body, out_shape=[jax.ShapeDtypeStruct((N_CHIP, D, KV_BLK), F32), jax.ShapeDtypeStruct((N_CHIP, D, KV_BLK), BF16),
                         jax.ShapeDtypeStruct((1, D), F32)],
        compiler_params=_cp(None, VMEM_LIMIT), name="mem_bwd")(mem, gm, dk, dv, wkv_f)


def _mixer_bwd(proj, dmix, cw8, lng, lnb, wc, wct, bsb, win_f, x, dx1, g1, tm=256):
    t = proj.shape[0]
    nt = t // tm
    nch = tm // CH
    hb = 16
    pair = 2 * CH
    assert pair == IN_PIECE

    def body(p_ref, pgc_ref, pxa_ref, dm_ref, cw_ref, lng_ref, lnb_ref, wc_ref, wct_ref, bsb_ref, w_ref, x_ref,
             dx1_ref, g1_ref, dp_ref, dcw_ref, dlng_ref, dlnb_ref, dwc_ref, dbs_ref, gx_ref, dg1_ref,
             next_ref, dh_ref):
        i = pl.program_id(0)

        @pl.when(i == 0)
        def _():
            next_ref[...] = jnp.zeros_like(next_ref)
            dcw_ref[...] = jnp.zeros_like(dcw_ref)
            dlng_ref[...] = jnp.zeros_like(dlng_ref)
            dlnb_ref[...] = jnp.zeros_like(dlnb_ref)
            dwc_ref[...] = jnp.zeros_like(dwc_ref)
            dbs_ref[...] = jnp.zeros_like(dbs_ref)
            dg1_ref[...] = jnp.zeros_like(dg1_ref)

        first_tile = i == nt - 1
        rows = lax.broadcasted_iota(jnp.int32, (tm, CH), 0)
        ones8 = jnp.ones((8, CH), BF16)
        for s in range(HEADS):
            cs = pl.ds(CH * s, CH)

            def slab(k):
                return p_ref[:, pl.ds(k * SLAB + CH * s, CH)].astype(F32)

            gb, gc, xa, za = slab(0), slab(1), slab(2), slab(3)
            da = dm_ref[:, cs].astype(F32)
            cx = gc * xa
            cxp = pgc_ref[:, cs].astype(F32) * pxa_ref[:, cs].astype(F32)
            cxp = jnp.where(first_tile, jnp.zeros_like(cxp), cxp)
            p6 = jnp.broadcast_to(cxp[hb - 2:hb - 1, :], (tm, CH))
            p7 = jnp.broadcast_to(cxp[hb - 1:hb, :], (tm, CH))
            c1 = jnp.where(rows == 0, p7, pltpu.roll(cx, 1, 0))
            c2 = jnp.where(rows == 0, p6, jnp.where(rows == 1, p7, pltpu.roll(cx, 2, 0)))
            w0, w1, w2 = cw_ref[0:1, cs], cw_ref[1:2, cs], cw_ref[2:3, cs]
            cv = w0 * c2 + w1 * c1 + w2 * cx
            sg = _sigmoid(za)
            sa = za * sg
            dcv = da * gb * sa
            dp_ref[:, pl.ds(0 * SLAB + CH * s, CH)] = (da * cv * sa).astype(BF16)
            dp_ref[:, pl.ds(3 * SLAB + CH * s, CH)] = (da * gb * cv * (sg * (1.0 + za * (1.0 - sg)))).astype(BF16)
            n0 = jnp.broadcast_to(next_ref[0:1, cs], (tm, CH))
            n1 = jnp.broadcast_to(next_ref[1:2, cs], (tm, CH))
            u1 = jnp.where(rows == tm - 1, n0, pltpu.roll(dcv, tm - 1, 0))
            u2 = jnp.where(rows == tm - 2, n0, jnp.where(rows == tm - 1, n1, pltpu.roll(dcv, tm - 2, 0)))
            next_ref[:, cs] = dcv[0:8, :]
            dcx = w2 * dcv + w1 * u1 + w0 * u2
            dp_ref[:, pl.ds(1 * SLAB + CH * s, CH)] = (dcx * xa).astype(BF16)
            dp_ref[:, pl.ds(2 * SLAB + CH * s, CH)] = (dcx * gc).astype(BF16)
            dcw_ref[0:1, cs] += jnp.sum(dcv * c2, axis=0, keepdims=True)
            dcw_ref[1:2, cs] += jnp.sum(dcv * c1, axis=0, keepdims=True)
            dcw_ref[2:3, cs] += jnp.sum(dcv * cx, axis=0, keepdims=True)

            u, v, zb = slab(4), slab(5), slab(6)
            db = dm_ref[:, pl.ds(SLAB + CH * s, CH)].astype(F32)
            ug, ugrad = _gelu_parts(u)
            vg, vgrad = _gelu_parts(v)
            dlt = vg - jnp.mean(vg, axis=-1, keepdims=True)
            rstd = lax.rsqrt(jnp.mean(dlt * dlt, axis=-1, keepdims=True) + EPS)
            vhat = dlt * rstd
            lg = lng_ref[:, cs]
            vn = (vhat * lg + lnb_ref[:, cs]).astype(BF16)
            sgb = _sigmoid(zb)
            szb = zb * sgb
            sps, dvns = [], []
            dbs = jnp.zeros((8, CH), F32)
            dwc = jnp.zeros((CH, CH), F32)
            for c in range(nch):
                rs = slice(CH * c, CH * (c + 1))
                sp = jnp.dot(wc_ref[s], vn[rs], preferred_element_type=F32) + bsb_ref[s]
                dsp = (db[rs] * ug[rs] * szb[rs]).astype(BF16)
                dbs = dbs + lax.dot_general(ones8, dsp, (((1,), (1,)), ((), ())), preferred_element_type=F32)
                dwc = dwc + lax.dot_general(dsp, vn[rs], (((1,), (1,)), ((), ())), preferred_element_type=F32)
                dvns.append(jnp.dot(wct_ref[s], dsp, preferred_element_type=F32))
                sps.append(sp)
            sp = jnp.concatenate(sps, axis=0)
            dvn = jnp.concatenate(dvns, axis=0)
            dbs_ref[:, cs] += dbs
            dwc_ref[s] += dwc
            dlng_ref[:, cs] += jnp.sum(dvn * vhat, axis=0, keepdims=True)
            dlnb_ref[:, cs] += jnp.sum(dvn, axis=0, keepdims=True)
            dvhat = dvn * lg
            dvg = rstd * (dvhat - jnp.mean(dvhat, axis=-1, keepdims=True)
                          - vhat * jnp.mean(dvhat * vhat, axis=-1, keepdims=True))
            dp_ref[:, pl.ds(4 * SLAB + CH * s, CH)] = (db * sp * szb * ugrad).astype(BF16)
            dp_ref[:, pl.ds(5 * SLAB + CH * s, CH)] = (dvg * vgrad).astype(BF16)
            dp_ref[:, pl.ds(6 * SLAB + CH * s, CH)] = (db * ug * sp * (sgb * (1.0 + zb * (1.0 - sgb)))).astype(BF16)

            if s % 2 == 1:
                part = None
                for k in range(N_SLAB):
                    col = k * SLAB + pair * (s // 2)
                    blk, off = divmod(col, IN_BLK)
                    term = lax.dot_general(dp_ref[:, pl.ds(col, pair)], w_ref[blk, off // IN_PIECE],
                                           (((1,), (1,)), ((), ())), preferred_element_type=F32)
                    part = term if part is None else part + term
                if s == 1:
                    dh_ref[...] = part
                else:
                    dh_ref[...] += part

        xv = x_ref[...]
        r = lax.rsqrt(jnp.mean(xv * xv, axis=-1, keepdims=True) + EPS)
        dxn, dg = _rms_bwd(dh_ref[...], xv, r, g1_ref[...])
        gx_ref[...] = dx1_ref[...].astype(F32) + dxn
        dg1_ref[0:1, :] += dg

        @pl.when(i == nt - 1)
        def _():
            tril = lax.broadcasted_iota(jnp.int32, (CH, CH), 0) >= lax.broadcasted_iota(jnp.int32, (CH, CH), 1)
            for s in range(HEADS):
                dwc_ref[s] = jnp.where(tril, dwc_ref[s], 0.0)

    rev = lambda i: nt - 1 - i
    halo = lambda col: pl.BlockSpec((hb, SLAB), lambda i: (jnp.maximum(rev(i) * (tm // hb) - 1, 0), col))
    tok = lambda w: pl.BlockSpec((tm, w), lambda i: (rev(i), 0))
    return pl.pallas_call(
        body, grid=(nt,),
        in_specs=[tok(IN_DIM), halo(1), halo(2), tok(MIX), _full((8, D)), _full((1, D)), _full((1, D)),
                  _full((HEADS, CH, CH)), _full((HEADS, CH, CH)), _full((HEADS, CH, CH)),
                  _full((N_CHIP, N_PIECE, D, IN_PIECE), 1), tok(D), tok(D), _full((1, D))],
        out_specs=[tok(IN_DIM), _full((8, D)), _full((1, D)), _full((1, D)), _full((HEADS, CH, CH)), _full((8, D)),
                   tok(D), _full((8, D))],
        out_shape=[jax.ShapeDtypeStruct((t, IN_DIM), BF16), jax.ShapeDtypeStruct((8, D), F32),
                   jax.ShapeDtypeStruct((1, D), F32), jax.ShapeDtypeStruct((1, D), F32),
                   jax.ShapeDtypeStruct((HEADS, CH, CH), F32), jax.ShapeDtypeStruct((8, D), F32),
                   jax.ShapeDtypeStruct((t, D), F32), jax.ShapeDtypeStruct((8, D), F32)],
        scratch_shapes=[pltpu.VMEM((8, D), F32), pltpu.VMEM((tm, D), F32)],
        compiler_params=_cp(("arbitrary",), VMEM_LIMIT), name="mixer_bwd")(
            proj, proj, proj, dmix, cw8, lng, lnb, wc, wct, bsb, win_f, x, dx1, g1)


def _grad_matmul(a, b, after, *, by_cols, name, tk=1024):
    t, m = a.shape
    n = b.shape[1]
    nk = t // tk
    nj = N_CHIP if by_cols else 1
    bn = n // nj

    def body(a_ref, b_ref, after_ref, o_ref, ob_ref):
        kk = pl.program_id(1)
        part = lax.dot_general(a_ref[...], b_ref[...], (((0,), (0,)), ((), ())), preferred_element_type=F32)

        @pl.when(kk == 0)
        def _():
            o_ref[...] = part

        @pl.when(kk > 0)
        def _():
            o_ref[...] += part

        @pl.when(kk == nk - 1)
        def _():
            ob_ref[...] = o_ref[...].astype(BF16)

    a_spec = pl.BlockSpec((tk, m), lambda j, k: (k, 0))
    b_spec = pl.BlockSpec((tk, bn), lambda j, k: (k, j))
    o_spec = pl.BlockSpec((None, m, bn), lambda j, k: (j, 0, 0))
    o32, o16 = pl.pallas_call(
        body, grid=(nj, nk), in_specs=[a_spec, b_spec, ANY], out_specs=[o_spec, o_spec],
        out_shape=[jax.ShapeDtypeStruct((nj, m, bn), F32), jax.ShapeDtypeStruct((nj, m, bn), BF16)],
        compiler_params=_cp(("parallel", "arbitrary"), VMEM_LIMIT), name=name)(a, b, after)
    if by_cols:
        return o32, o16
    return o32.reshape(N_CHIP, m // N_CHIP, n), o16.reshape(N_CHIP, m // N_CHIP, n)


def _coords():
    x, y, c = lax.axis_index("x"), lax.axis_index("y"), lax.axis_index("c")
    chips = [(1 - x, y), (x, 1 - y), (1 - x, 1 - y)]
    return x, y, c, chips


def _pair_reduce(c_idx, grads, grads_b, smalls, name):
    ng, ns = len(grads), len(smalls)
    halves = [g.shape[1] // 2 for g in grads]

    def body(c_ref, *refs):
        g_in, gb_any = refs[:ng], refs[ng:2 * ng]
        s_own, s_any = refs[2 * ng:2 * ng + ns], refs[2 * ng + ns:2 * ng + 2 * ns]
        o = refs[2 * ng + 2 * ns:4 * ng + 3 * ns]
        lands = refs[4 * ng + 3 * ns:5 * ng + 4 * ns]
        send, recv = refs[5 * ng + 4 * ns:]
        x, y, c, _ = _coords()
        j = pl.program_id(0)

        def big(i, blk):
            return pltpu.make_async_remote_copy(
                src_ref=gb_any[i].at[blk, pl.ds((1 - c) * halves[i], halves[i])], dst_ref=lands[i].at[blk],
                send_sem=send.at[i, blk], recv_sem=recv.at[i, blk], device_id=(x, y, 1 - c), device_id_type=MESH)

        def small(i):
            return pltpu.make_async_remote_copy(
                src_ref=s_any[i].at[1 - c], dst_ref=lands[ng + i],
                send_sem=send.at[ng + i, 0], recv_sem=recv.at[ng + i, 0], device_id=(x, y, 1 - c), device_id_type=MESH)

        @pl.when(j == 0)
        def _():
            for blk in range(N_CHIP):
                for i in range(ng):
                    big(i, blk).start()
            for i in range(ns):
                small(i).start()

        for i in range(ng):
            big(i, j).wait_recv()
            tot = g_in[i][...] + lands[i][j].astype(F32)
            o[i][...] = tot
            o[ng + i][...] = tot.astype(BF16)

        @pl.when(j == N_CHIP - 1)
        def _():
            for i in range(ns):
                small(i).wait_recv()
                o[2 * ng + i][...] = s_own[i][...] + lands[ng + i][...]
                small(i).wait_send()
            for blk in range(N_CHIP):
                for i in range(ng):
                    big(i, blk).wait_send()

    in_specs = [pl.BlockSpec((None, None, halves[i], g.shape[2]), lambda b, c: (b, c[0], 0, 0)) for i, g in enumerate(grads)]
    in_specs += [ANY] * ng
    in_specs += [pl.BlockSpec((None, s.shape[0] // 2, s.shape[1]), lambda b, c: (c[0], 0, 0)) for s in smalls]
    in_specs += [ANY] * ns
    blk = [pl.BlockSpec((None, halves[i], g.shape[2]), lambda b, c: (b, 0, 0)) for i, g in enumerate(grads)]
    out_specs = blk + blk + [pl.BlockSpec((s.shape[0] // 2, s.shape[1]), lambda b, c: (0, 0)) for s in smalls]
    out_shape = [jax.ShapeDtypeStruct((N_CHIP, halves[i], g.shape[2]), F32) for i, g in enumerate(grads)]
    out_shape += [jax.ShapeDtypeStruct((N_CHIP, halves[i], g.shape[2]), BF16) for i, g in enumerate(grads)]
    out_shape += [jax.ShapeDtypeStruct((s.shape[0] // 2, s.shape[1]), F32) for s in smalls]
    scratch = [pltpu.VMEM((N_CHIP, halves[i], g.shape[2]), BF16) for i, g in enumerate(grads)]
    scratch += [pltpu.VMEM((s.shape[0] // 2, s.shape[1]), F32) for s in smalls]
    scratch += [pltpu.SemaphoreType.DMA((ng + ns, N_CHIP)), pltpu.SemaphoreType.DMA((ng + ns, N_CHIP))]
    grads4 = [g.reshape(N_CHIP, 2, halves[i], g.shape[2]) for i, g in enumerate(grads)]
    smalls3 = [s.reshape(2, s.shape[0] // 2, s.shape[1]) for s in smalls]
    return pl.pallas_call(
        body, out_shape=out_shape,
        grid_spec=pltpu.PrefetchScalarGridSpec(num_scalar_prefetch=1, grid=(N_CHIP,), in_specs=in_specs,
                                               out_specs=out_specs, scratch_shapes=scratch),
        compiler_params=_cp(("arbitrary",), VMEM_LIMIT), name=name)(c_idx, *grads4, *grads_b, *smalls3, *smalls3)


_HBM = pl.BlockSpec(memory_space=pltpu.HBM)
_SEM = pl.BlockSpec(memory_space=pltpu.SEMAPHORE)


def _split_copies(ins, lands, ng, send, recv, arriving):
    x, y, c, chips = _coords()
    b = 2 * x + y
    copies = []
    for i in range(len(ins)):
        for k in range(3):
            blk = 2 * chips[k][0] + chips[k][1]
            src, dst, got = (ins[i].at[blk], lands[i].at[k], lands[i].at[k]) if i < ng else (ins[i], lands[i].at[b], lands[i].at[blk])
            sems = dict(send_sem=send.at[3 * i + k], recv_sem=recv.at[3 * i + k], device_id=(*chips[k], c), device_id_type=MESH)
            if arriving:
                copies.append(pltpu.make_async_remote_copy(src_ref=got, dst_ref=got, **sems))
            else:
                copies.append(pltpu.make_async_remote_copy(src_ref=src, dst_ref=dst, **sems))
    return copies


def _exchange_begin(sums_b, smalls, name):
    ng, n = len(sums_b), len(sums_b) + len(smalls)
    srcs = list(sums_b) + list(smalls)
    lands = [lax.empty((3,) + g.shape[1:], g.dtype) for g in sums_b] + [lax.empty((N_CHIP,) + s.shape, s.dtype) for s in smalls]

    def body(*refs):
        ins, land_refs = refs[:n], refs[n:2 * n]
        send, recv = refs[2 * n], refs[2 * n + 1]
        token = refs[4 * n + 2]
        for cp in _split_copies(ins, land_refs, ng, send, recv, False):
            cp.start()
        token[...] = jnp.zeros_like(token)

    hbm = lambda a: pltpu.HBM(a.shape, a.dtype)
    outs = pl.pallas_call(
        body, name=name,
        out_shape=(pltpu.SemaphoreType.DMA((3 * n,)), pltpu.SemaphoreType.DMA((3 * n,)), *[hbm(a) for a in srcs + lands],
                   jax.ShapeDtypeStruct((8, 128), F32)),
        in_specs=[_HBM] * (2 * n), out_specs=(_SEM, _SEM, *[_HBM] * (2 * n), pl.BlockSpec(memory_space=pltpu.VMEM)),
        input_output_aliases={i: 2 + i for i in range(2 * n)},
        compiler_params=pltpu.CompilerParams(has_side_effects=pltpu.SideEffectType.DATAFLOW_SIDE_EFFECTING),
    )(*[pltpu.with_memory_space_constraint(a, pltpu.HBM) for a in srcs + lands])
    return outs[0], outs[1], list(outs[2:2 + n]), list(outs[2 + n:2 + 2 * n]), outs[2 + 2 * n]


def _exchange_end(send, recv, srcs, lands, ng, after, name):
    n = len(srcs)
    after = list(after)

    def body(*refs):
        ins, land_refs = refs[:n], refs[n:2 * n]
        send_ref, recv_ref = refs[2 * n], refs[2 * n + 1]
        for cp in _split_copies(ins, land_refs, ng, send_ref, recv_ref, False):
            cp.wait_send()
        for cp in _split_copies(ins, land_refs, ng, send_ref, recv_ref, True):
            cp.wait_recv()

    hbm = lambda a: pltpu.HBM(a.shape, a.dtype)
    outs = pl.pallas_call(
        body, name=name, out_shape=tuple(hbm(a) for a in list(srcs) + list(lands)),
        in_specs=[_HBM] * (2 * n) + [_SEM, _SEM] + [ANY] * len(after), out_specs=tuple([_HBM] * (2 * n)),
        input_output_aliases={i: i for i in range(2 * n)},
        compiler_params=pltpu.CompilerParams(has_side_effects=pltpu.SideEffectType.DATAFLOW_SIDE_EFFECTING),
    )(*srcs, *lands, send, recv, *after)
    return list(outs[n:])


def _chip_reduce(bc_idx, sums, recvd, smalls_slots, smalls_own, name, steps=4):
    ng, ns = len(sums), len(smalls_slots)
    n = ng + ns
    assert steps >= 2
    halves = [g.shape[1] for g in sums] + [s.shape[1] for s in smalls_slots]
    rows = [g.shape[1] // steps for g in sums]

    def body(bc_ref, *refs):
        own, rx = refs[:ng], refs[ng:2 * ng]
        sl = refs[2 * ng:2 * ng + ns]
        sl_own = refs[2 * ng + ns:2 * ng + 2 * ns]
        o = refs[2 * ng + 2 * ns:2 * ng + 2 * ns + n]
        tiles = refs[2 * ng + 2 * ns + n:2 * ng + 2 * ns + 2 * n]
        keep, send, recv = refs[2 * ng + 2 * ns + 2 * n:]
        x, y, c, _ = _coords()
        sibling = dict(device_id=(x, y, 1 - c), device_id_type=MESH)
        r = pl.program_id(0)

        def writes(i, step, slot):
            dst = o[i].at[pl.ds(c * halves[i] + step * rows[i], rows[i])]
            return (pltpu.make_async_copy(tiles[i].at[slot], dst, keep.at[i, slot]),
                    pltpu.make_async_remote_copy(src_ref=tiles[i].at[slot], dst_ref=dst, send_sem=send.at[i, slot],
                                                 recv_sem=recv.at[i, step], **sibling))

        def small_writes(i):
            dst = o[i].at[pl.ds(c * halves[i], halves[i])]
            return (pltpu.make_async_copy(tiles[i], dst, keep.at[i, 0]),
                    pltpu.make_async_remote_copy(src_ref=tiles[i], dst_ref=dst, send_sem=send.at[i, 0],
                                                 recv_sem=recv.at[i, 0], **sibling))

        def arriving(i, step, nrows):
            dst = o[i].at[pl.ds((1 - c) * halves[i] + step * nrows, nrows)]
            return pltpu.make_async_remote_copy(src_ref=dst, dst_ref=dst, send_sem=send.at[i, 0], recv_sem=recv.at[i, step],
                                                **sibling)

        def finish(step, slot):
            for i in range(ng):
                local, remote = writes(i, step, slot)
                local.wait()
                remote.wait_send()

        @pl.when(r >= 2)
        def _():
            finish(r - 2, r % 2)

        for i in range(ng):
            tot = own[i][...]
            for j in range(3):
                tot = tot + rx[i][j].astype(F32)
            tiles[i][r % 2] = tot
            for cp in writes(i, r, r % 2):
                cp.start()

        @pl.when(r == 0)
        def _():
            for i in range(ns):
                term = [jnp.where(bc_ref[0] == kk, sl_own[i][...], sl[i][kk]) for kk in range(N_CHIP)]
                tiles[ng + i][...] = ((term[0] + term[1]) + term[2]) + term[3]
                for cp in small_writes(ng + i):
                    cp.start()

        @pl.when(r == steps - 1)
        def _():
            finish(steps - 2, (steps - 2) % 2)
            finish(steps - 1, (steps - 1) % 2)
            for i in range(ns):
                local, remote = small_writes(ng + i)
                local.wait()
                remote.wait_send()
                arriving(ng + i, 0, halves[ng + i]).wait_recv()
            for i in range(ng):
                for step in range(steps):
                    arriving(i, step, rows[i]).wait_recv()

    in_specs = [pl.BlockSpec((None, rows[i], g.shape[2]), lambda r, bc: (bc[0], r, 0)) for i, g in enumerate(sums)]
    in_specs += [pl.BlockSpec((3, rows[i], g.shape[2]), lambda r, bc: (0, r, 0)) for i, g in enumerate(sums)]
    in_specs += [pl.BlockSpec(s.shape, lambda r, bc: (0, 0, 0)) for s in smalls_slots]
    in_specs += [pl.BlockSpec(s.shape[1:], lambda r, bc: (0, 0)) for s in smalls_slots]
    out_shape = [jax.ShapeDtypeStruct((2 * g.shape[1], g.shape[2]), F32) for g in sums]
    out_shape += [jax.ShapeDtypeStruct((2 * s.shape[1], s.shape[2]), F32) for s in smalls_slots]
    scratch = [pltpu.VMEM((2, rows[i], g.shape[2]), F32) for i, g in enumerate(sums)]
    scratch += [pltpu.VMEM(s.shape[1:], F32) for s in smalls_slots]
    scratch += [pltpu.SemaphoreType.DMA((n, 2)), pltpu.SemaphoreType.DMA((n, 2)), pltpu.SemaphoreType.DMA((n, steps))]
    return list(pl.pallas_call(
        body, out_shape=out_shape,
        grid_spec=pltpu.PrefetchScalarGridSpec(num_scalar_prefetch=1, grid=(steps,), in_specs=in_specs,
                                               out_specs=[ANY] * n, scratch_shapes=scratch),
        compiler_params=_cp(("arbitrary",), VMEM_LIMIT), name=name)(bc_idx, *sums, *recvd, *smalls_slots, *smalls_own))


def _adamw_math(w, g, m, v):
    m2 = ADAM_B1 * m + (1.0 - ADAM_B1) * g
    v2 = ADAM_B2 * v + (1.0 - ADAM_B2) * (g * g)
    m_hat = m2 / (1.0 - ADAM_B1 ** ADAM_STEP)
    v_hat = v2 / (1.0 - ADAM_B2 ** ADAM_STEP)
    delta = -ADAM_LR * (m_hat / (jnp.sqrt(v_hat) + ADAM_EPS) + ADAM_WD * w)
    return delta, m2, v2


def _adamw_big(ws, gs, ms, vs, name, steps=8):
    n = len(ws)

    def body(*refs):
        for i in range(n):
            w_ref, g_ref, m_ref, v_ref = (refs[k * n + i] for k in range(4))
            d_ref, m2_ref, v2_ref, g2_ref = (refs[(4 + k) * n + i] for k in range(4))
            gv = g_ref[...]
            d_ref[...], m2_ref[...], v2_ref[...] = _adamw_math(w_ref[...], gv, m_ref[...], v_ref[...])
            g2_ref[...] = gv

    specs = [pl.BlockSpec((w.shape[0] // steps, w.shape[1]), lambda i: (i, 0)) for w in ws]
    shapes = [jax.ShapeDtypeStruct(w.shape, F32) for w in ws]
    outs = pl.pallas_call(
        body, grid=(steps,), in_specs=specs * 4, out_specs=specs * 4, out_shape=shapes * 4,
        compiler_params=_cp(("parallel",), VMEM_LIMIT), name=name)(*ws, *gs, *ms, *vs)
    return [tuple(outs[k * n + i] for k in range(4)) for i in range(n)]


def _adamw_small(groups):
    n = len(groups)

    def body(*refs):
        for i in range(n):
            w_ref, g_ref, m_ref, v_ref = refs[4 * i:4 * i + 4]
            d_ref, m2_ref, v2_ref = refs[4 * n + 3 * i:4 * n + 3 * i + 3]
            d_ref[...], m2_ref[...], v2_ref[...] = _adamw_math(w_ref[...], g_ref[...], m_ref[...], v_ref[...])

    flat = [a for grp in groups for a in grp]
    out_shape = [jax.ShapeDtypeStruct(grp[0].shape, F32) for grp in groups for _ in range(3)]
    outs = pl.pallas_call(body, out_shape=out_shape, name="adamw_small")(*flat)
    return [tuple(outs[3 * i:3 * i + 3]) for i in range(n)]


def kernel(x, mem, norm_mix_g, w_in, conv_w, gm_ln_g, gm_ln_b, gm_ws, gm_bs, w_out, norm_x_g, norm_mem_g, w_q, w_kv, w_xo, norm_final_g, loss_target, m_norm_mix_g, m_w_in, m_conv_w, m_gm_ln_g, m_gm_ln_b, m_gm_ws, m_gm_bs, m_w_out, m_norm_x_g, m_norm_mem_g, m_w_q, m_w_kv, m_w_xo, m_norm_final_g, v_norm_mix_g, v_w_in, v_conv_w, v_gm_ln_g, v_gm_ln_b, v_gm_ws, v_gm_bs, v_w_out, v_norm_x_g, v_norm_mem_g, v_w_q, v_w_kv, v_w_xo, v_norm_final_g):
    t = x.shape[1]
    xi = lax.axis_index("x")
    yi = lax.axis_index("y")
    ci = lax.axis_index("c")
    b_idx = jnp.reshape(2 * xi + yi, (1,)).astype(jnp.int32)
    c_idx = jnp.reshape(ci, (1,)).astype(jnp.int32)

    x2d, mem2d, tgt = x[0], mem[0], loss_target[0]
    big = [w_in[0], w_out[0], w_q[0], w_kv[0], w_xo[0]]
    big_m = [m_w_in[0], m_w_out[0], m_w_q[0], m_w_kv[0], m_w_xo[0]]
    big_v = [v_w_in[0], v_w_out[0], v_w_q[0], v_w_kv[0], v_w_xo[0]]
    g3 = norm_final_g.reshape(1, D)

    def pad8(a):
        return jnp.pad(a, ((0, 8 - a.shape[0]), (0, 0)))

    own_blocks = _cast_shards(b_idx, big)

    tril = jnp.tril(jnp.ones((CH, CH), bool))
    wc32 = jnp.where(tril[None], gm_ws[0], 0.0)
    wc = wc32.astype(BF16)
    wct = jnp.swapaxes(wc32, 1, 2).astype(BF16)
    bsb = jnp.broadcast_to(gm_bs[0][:, :, None], (HEADS, CH, CH))

    blk = 2 * xi + yi
    roles = jnp.stack([blk, blk ^ 2, blk ^ 1, blk ^ 3]).astype(jnp.int32)
    proj, hb, win_f, cw8, (wq_f,) = _proj_gather(
        roles, x2d, norm_mix_g, own_blocks[0], pad8(conv_w[0]), [own_blocks[2]])
    mixin, (wout_f, wkv_f, wxo_f) = _mixer_fwd(
        proj, cw8, gm_ln_g, gm_ln_b, wc, bsb, [own_blocks[1], own_blocks[3], own_blocks[4]])
    wout2, wq2, wxo2 = wout_f.reshape(MIX, D), wq_f.reshape(D, D), wxo_f.reshape(D, D)
    k, v = _mem_fwd(mem2d, norm_mem_g, wkv_f)

    (loss_tile, dmix, dx1b, h2b, dq, ob, dx2b, dk, dv, dg2, dg3) = _tail(
        x2d, tgt, mixin, wout2, wq2, wxo2, k, v, norm_x_g, g3)
    dwkv, dwkv_b, dgm = _mem_bwd(mem2d, norm_mem_g, dk, dv, wkv_f)
    dproj, dcw, dlng, dlnb, dwc, dbs8, grad_x, dg1 = _mixer_bwd(
        proj, dmix, cw8, gm_ln_g, gm_ln_b, wc, wct, bsb, win_f, x2d, dx1b, norm_mix_g)

    bc_idx = jnp.concatenate([b_idx, c_idx])
    dwin, dwin_b = _grad_matmul(hb, dproj, dgm, by_cols=True, name="grad_w_in", tk=2048)
    zero = jnp.zeros((1, D), F32)
    loss_row = jnp.broadcast_to(loss_tile[0:1, 0:1], (1, D))
    sv = jnp.concatenate([dg1[0:1], dg2, dgm, dg3, dlng, dlnb, dbs8[0:1], loss_row, dcw], axis=0)
    sw = dwc.reshape(HEADS * CH, CH)
    ps_b = _pair_reduce(c_idx, [dwin], [dwin_b], [sv, sw], "pair_reduce_b")
    sums_b, sums_b_b, psmall = list(ps_b[:1]), list(ps_b[1:2]), list(ps_b[2:])
    send_b, recv_b, src_b, land_b, token_b = _exchange_begin(sums_b_b, psmall, "exchange_b_begin")

    dwxo, dwxo_b = _grad_matmul(ob, dx2b, token_b, by_cols=False, name="grad_w_xo", tk=2048)
    dwq, dwq_b = _grad_matmul(h2b, dq, token_b, by_cols=False, name="grad_w_q", tk=2048)
    dwout, dwout_b = _grad_matmul(mixin, dx1b, token_b, by_cols=False, name="grad_w_out")
    ps_a = _pair_reduce(c_idx, [dwout, dwq, dwkv, dwxo], [dwout_b, dwq_b, dwkv_b, dwxo_b], [], "pair_reduce_a")
    sums_a, sums_a_b = list(ps_a[:4]), list(ps_a[4:8])
    send_a, recv_a, src_a, land_a, token_a = _exchange_begin(sums_a_b, [], "exchange_a_begin")

    rx2b = _exchange_end(send_b, recv_b, src_b, land_b, 1, [token_a], "exchange_b_end")
    gwin, svf, swf = _chip_reduce(bc_idx, sums_b, rx2b[:1], rx2b[1:], psmall, "chip_reduce_b")
    out_b = _adamw_big(big[:1], [gwin], big_m[:1], big_v[:1], "adamw_w_in")[0]

    def vec_pack(a1, a2, am, a3, lg, lb, bs):
        return jnp.concatenate([a1, a2, am, a3.reshape(1, D), lg, lb, bs.reshape(1, D), zero], axis=0)

    wv = vec_pack(norm_mix_g, norm_x_g, norm_mem_g, norm_final_g, gm_ln_g, gm_ln_b, gm_bs)
    mv = vec_pack(m_norm_mix_g, m_norm_x_g, m_norm_mem_g, m_norm_final_g, m_gm_ln_g, m_gm_ln_b, m_gm_bs)
    vv = vec_pack(v_norm_mix_g, v_norm_x_g, v_norm_mem_g, v_norm_final_g, v_gm_ln_g, v_gm_ln_b, v_gm_bs)
    loss = svf[7, 0]
    gcw = lax.dynamic_slice_in_dim(svf[8:16], blk * (D // N_CHIP), D // N_CHIP, axis=1)
    gws = swf
    gv = svf[0:8]
    (dv_, mv_, vv_), (dc_, mc_, vc_), (dws_, mws_, vws_) = _adamw_small([
        (wv, gv, mv, vv),
        (pad8(conv_w[0]), gcw, pad8(m_conv_w[0]), pad8(v_conv_w[0])),
        (gm_ws.reshape(HEADS * CH, CH), gws, m_gm_ws.reshape(HEADS * CH, CH), v_gm_ws.reshape(HEADS * CH, CH))])

    rx2a = _exchange_end(send_a, recv_a, src_a, land_a, 4, [out_b[0], dv_], "exchange_a_end")
    g_a = _chip_reduce(bc_idx, sums_a, rx2a, [], [], "chip_reduce_a")
    big_out = [out_b] + _adamw_big(big[1:], g_a, big_m[1:], big_v[1:], "adamw_rest")

    def unpack(vecs, cw, ws, bigs):
        r = lambda i: vecs[i:i + 1]
        return [r(0), bigs[0][None], cw[0:3][None], r(4), r(5), ws.reshape(1, HEADS, CH, CH), vecs[6].reshape(1, HEADS, CH),
                bigs[1][None], r(1), r(2), bigs[2][None], bigs[3][None], bigs[4][None], vecs[3]]

    grads_out = unpack(gv, gcw, gws, [o[3] for o in big_out])
    delta_out = unpack(dv_, dc_, dws_, [o[0] for o in big_out])
    m_out = unpack(mv_, mc_, mws_, [o[1] for o in big_out])
    v_out = unpack(vv_, vc_, vws_, [o[2] for o in big_out])
    return (loss, grad_x[None], *grads_out, *delta_out, *m_out, *v_out)
```

```python
import functools
import math

import jax
import jax.numpy as jnp
from jax import lax
from jax.experimental import pallas as pl
from jax.experimental.pallas import tpu as pltpu

F32 = jnp.float32
BF16 = jnp.bfloat16
MESH = pl.DeviceIdType.MESH

D = 1024
SLAB = 1024
N_SLAB = 7
IN_DIM = N_SLAB * SLAB
MIX = 2 * SLAB
HEADS = 8
CH = 128
XH = 4
XD = D // XH
EPS = 1e-6
GELU_C = math.sqrt(2.0 / math.pi)
GELU_A = 0.044715
N_CHIP = 4
IN_BLK = IN_DIM // N_CHIP
IN_PIECE = 256
N_PIECE = IN_BLK // IN_PIECE
KV_BLK = 2 * D // N_CHIP

ADAM_LR, ADAM_B1, ADAM_B2, ADAM_EPS, ADAM_WD, ADAM_STEP = 0.001, 0.9, 0.999, 1e-08, 0.01, 10

VMEM_LIMIT = 60 * 1024 * 1024


def _cp(sem=None, vmem=None):
    return pltpu.CompilerParams(dimension_semantics=sem, vmem_limit_bytes=vmem)


def _full(shape, buffers=None):
    n = len(shape)
    if buffers is None:
        return pl.BlockSpec(shape, lambda *_: (0,) * n)
    return pl.BlockSpec(shape, lambda *_: (0,) * n, pipeline_mode=pl.Buffered(buffers))


ANY = pl.BlockSpec(memory_space=pl.ANY)


def _bdot(a, b):
    return jnp.dot(a.astype(BF16), b.astype(BF16), preferred_element_type=F32)


def _bdot_nt(a, b):
    return lax.dot_general(a.astype(BF16), b.astype(BF16), (((1,), (1,)), ((), ())), preferred_element_type=F32)


def _bdot_tn(a, b):
    return lax.dot_general(a.astype(BF16), b.astype(BF16), (((0,), (0,)), ((), ())), preferred_element_type=F32)


def _rms(x, g):
    r = lax.rsqrt(jnp.mean(x * x, axis=-1, keepdims=True) + EPS)
    return x * r * g, r


def _rms_bwd(dy, x, r, g):
    gdy = dy * g
    dx = r * gdy - x * (r * r * r) * jnp.mean(x * gdy, axis=-1, keepdims=True)
    dg = jnp.sum(dy * x * r, axis=0, keepdims=True)
    return dx, dg


def _gelu_parts(x):
    x2 = x * x
    t = jnp.tanh(GELU_C * (x + GELU_A * x * x2))
    val = 0.5 * x * (1.0 + t)
    grad = 0.5 * (1.0 + t) + 0.5 * x * (1.0 - t * t) * (GELU_C * (1.0 + 3.0 * GELU_A * x2))
    return val, grad


def _gelu(x):
    return 0.5 * x * (1.0 + jnp.tanh(GELU_C * (x + GELU_A * x * x * x)))


def _sigmoid(z):
    return 1.0 / (1.0 + jnp.exp(-z))


def _cast_shards(b_idx, arrs):
    n = len(arrs)
    steps = 8

    def body(b_ref, *refs):
        for p in range(N_PIECE):
            refs[n][p] = refs[0][:, pl.ds(p * IN_PIECE, IN_PIECE)].astype(BF16)
        for i in range(1, n):
            refs[n + i][...] = refs[i][...].astype(BF16)

    rows = [a.shape[0] // steps for a in arrs]
    in_specs = [pl.BlockSpec((rows[i], a.shape[1]), lambda i, b: (i, 0)) for i, a in enumerate(arrs)]
    out_specs = [pl.BlockSpec((None, N_PIECE, rows[0], IN_PIECE), lambda i, b: (b[0], 0, i, 0))]
    out_specs += [pl.BlockSpec((None, rows[i], a.shape[1]), lambda i, b: (b[0], i, 0)) for i, a in enumerate(arrs) if i > 0]
    out_shape = [jax.ShapeDtypeStruct((N_CHIP, N_PIECE, arrs[0].shape[0], IN_PIECE), BF16)]
    out_shape += [jax.ShapeDtypeStruct((N_CHIP,) + a.shape, BF16) for a in arrs[1:]]
    return pl.pallas_call(
        body, out_shape=out_shape,
        grid_spec=pltpu.PrefetchScalarGridSpec(num_scalar_prefetch=1, grid=(steps,), in_specs=in_specs, out_specs=out_specs),
        compiler_params=_cp(("arbitrary",)), name="cast_shards")(b_idx, *arrs)


def _proj_gather(seq, x, g, win_own, cw8s, more, tm=1024):
    t = x.shape[0]
    ni = t // tm
    nm = len(more)
    steps = N_CHIP * N_PIECE
    near0, far0 = N_PIECE, 3 * N_PIECE

    def body(*refs):
        seq_ref, x_any, g_ref, win_in, cw_in = refs[:5]
        o_ref, hb_any, win_f, cw_out = refs[5 + nm:9 + nm]
        more_out = refs[9 + nm:9 + 2 * nm]
        hbuf, xbuf, wv, cw_s, cw_r, loc = refs[9 + 2 * nm:15 + 2 * nm]
        g_in = _Gather([win_f.at[:, p] for p in range(N_PIECE)], *refs[15 + 2 * nm:19 + 2 * nm])
        g_more = _Gather(more_out, *refs[19 + 2 * nm:23 + 2 * nm])
        s = pl.program_id(0)
        x, y, c, chips = _coords()
        b = 2 * x + y
        blks = [2 * chip[0] + chip[1] for chip in chips]

        def cw_cols(blk):
            return cw_out.at[:, pl.ds(blk * (D // N_CHIP), D // N_CHIP)]

        def cw_copy(k, blk):
            src = cw_in if blk is None else cw_cols(blk)
            return pltpu.make_async_remote_copy(src_ref=src, dst_ref=cw_cols(b if blk is None else blk), send_sem=cw_s.at[k],
                                                recv_sem=cw_r.at[k], device_id=(*chips[k], c), device_id_type=MESH)

        cw_local = pltpu.make_async_copy(cw_in, cw_cols(b), loc.at[1])
        hb_copy = pltpu.make_async_copy(hbuf, hb_any, loc.at[0])

        def load(step):
            slot = lax.rem(step, 2)
            return pltpu.make_async_copy(win_f.at[seq_ref[0, step], seq_ref[1, step]], wv.at[slot], loc.at[2 + slot])

        def chunk(i):
            return pltpu.make_async_copy(x_any.at[pl.ds(i * tm, tm)], xbuf.at[i % 2], loc.at[4 + i % 2])

        def first():
            g_in.start()
            cw_local.start()
            for k in range(3):
                cw_copy(k, None).start()
            load(0).start()
            chunk(0).start()
            for i in range(ni):
                if i + 1 < ni:
                    chunk(i + 1).start()
                chunk(i).wait()
                h, _ = _rms(xbuf[i % 2], g_ref[...])
                hbuf[pl.ds(i * tm, tm), :] = h.astype(BF16)
            hb_copy.start()

        events = {step: [] for step in range(steps)}
        events[0].append(first)
        for p in range(N_PIECE):
            events[2 * p + 2].append(functools.partial(g_in.hop, [p]))
            events[near0 + 2 * p - 1].append(functools.partial(g_in.near_ready, [p]))
            events[far0 + p - 2].append(functools.partial(g_in.far, [p]))
            events[far0 + p - 1].append(functools.partial(g_in.far_ready, [p]))
        events[2 * N_PIECE + 1].append(g_more.start)
        for step, todo in events.items():
            if todo:
                @pl.when(s == step)
                def _(todo=todo):
                    for do in todo:
                        do()

        @pl.when(s + 1 < steps)
        def _():
            load(s + 1).start()

        load(s).wait()
        for i in range(ni):
            rows = pl.ds(i * tm, tm)
            o_ref[rows, :] = jnp.dot(hbuf[rows, :], wv[lax.rem(s, 2)], preferred_element_type=F32).astype(BF16)

        @pl.when(s == steps - 1)
        def _():
            g_more.hop()
            g_more.far()
            for k in range(3):
                cw_copy(k, blks[k]).wait_recv()
            for k in range(3):
                cw_copy(k, None).wait_send()
            cw_local.wait()
            hb_copy.wait()
            g_more.near_ready()
            g_more.far_ready()
            g_in.drain()
            g_more.drain()

    in_specs = [ANY, pl.BlockSpec((1, D), lambda s, q: (0, 0)), ANY, ANY] + [ANY] * nm
    out_specs = [pl.BlockSpec((t, IN_PIECE), lambda s, q: (0, q[2, s])), ANY, ANY, ANY] + [ANY] * nm
    outs = pl.pallas_call(
        body, out_shape=[jax.ShapeDtypeStruct((t, IN_DIM), BF16), jax.ShapeDtypeStruct((t, D), BF16),
                         jax.ShapeDtypeStruct(win_own.shape, BF16), jax.ShapeDtypeStruct((8, D), F32)]
        + [jax.ShapeDtypeStruct(f.shape, f.dtype) for f in more],
        grid_spec=pltpu.PrefetchScalarGridSpec(
            num_scalar_prefetch=1, grid=(steps,), in_specs=in_specs, out_specs=out_specs,
            scratch_shapes=[pltpu.VMEM((t, D), BF16), pltpu.VMEM((2, tm, D), F32), pltpu.VMEM((2, D, IN_PIECE), BF16)]
            + [pltpu.SemaphoreType.DMA((3,))] * 2 + [pltpu.SemaphoreType.DMA((6,))]
            + _gather_sems(N_PIECE) + _gather_sems(nm)),
        input_output_aliases={3: 2, **{5 + w: 4 + w for w in range(nm)}},
        compiler_params=_cp(("arbitrary",), VMEM_LIMIT), name="proj_gather")(seq, x, g, win_own, cw8s, *more)
    return outs[0], outs[1], outs[2], outs[3], outs[4:]


class _Gather:
    def __init__(self, outs, ici_s, ici_r, d2d_s, d2d_r):
        x, y, c, _ = _coords()
        self.outs, self.c = outs, c
        self.sems = ici_s, ici_r, d2d_s, d2d_r
        self.b, self.bx, self.by, self.bd = 2 * x + y, 2 * (1 - x) + y, 2 * x + (1 - y), 2 * (1 - x) + (1 - y)
        self.xn, self.yn, self.sib = (1 - x, y, c), (x, 1 - y, c), (x, y, 1 - c)

    def piece(self, w, blk, hc, quarter=None):
        hr = self.outs[w].shape[1] // 2
        if quarter is None:
            return self.outs[w].at[blk, pl.ds(hc * hr, hr)]
        return self.outs[w].at[blk, pl.ds(hc * hr + quarter * (hr // 2), hr // 2)]

    def ici(self, w, k, ref, to):
        return pltpu.make_async_remote_copy(src_ref=ref, dst_ref=ref, send_sem=self.sems[0].at[w, k],
                                            recv_sem=self.sems[1].at[w, k], device_id=to, device_id_type=MESH)

    def d2d(self, w, k, ref):
        return pltpu.make_async_remote_copy(src_ref=ref, dst_ref=ref, send_sem=self.sems[2].at[w, k],
                                            recv_sem=self.sems[3].at[w, k], device_id=self.sib, device_id_type=MESH)

    def all(self):
        return range(len(self.outs))

    def start(self):
        for w in self.all():
            mine = self.piece(w, self.b, self.c)
            self.ici(w, 0, mine, self.xn).start()
            self.ici(w, 1, mine, self.yn).start()

    def hop(self, ws=None):
        c = self.c
        for w in ws or self.all():
            self.ici(w, 0, self.piece(w, self.bx, c), self.xn).wait_recv()
            self.ici(w, 1, self.piece(w, self.by, c), self.yn).wait_recv()
            self.ici(w, 2, self.piece(w, self.bx, c, 0), self.yn).start()
            self.ici(w, 3, self.piece(w, self.by, c, 1), self.xn).start()
            self.d2d(w, 0, self.piece(w, self.bx, c)).start()
            self.d2d(w, 1, self.piece(w, self.by, c)).start()

    def near_ready(self, ws=None):
        for w in ws or self.all():
            self.d2d(w, 0, self.piece(w, self.bx, 1 - self.c)).wait_recv()
            self.d2d(w, 1, self.piece(w, self.by, 1 - self.c)).wait_recv()

    def far(self, ws=None):
        c = self.c
        for w in ws or self.all():
            self.ici(w, 2, self.piece(w, self.bd, c, 0), self.yn).wait_recv()
            self.ici(w, 3, self.piece(w, self.bd, c, 1), self.xn).wait_recv()
            self.d2d(w, 2, self.piece(w, self.bd, c, 0)).start()
            self.d2d(w, 3, self.piece(w, self.bd, c, 1)).start()

    def far_ready(self, ws=None):
        for w in ws or self.all():
            self.d2d(w, 2, self.piece(w, self.bd, 1 - self.c, 0)).wait_recv()
            self.d2d(w, 3, self.piece(w, self.bd, 1 - self.c, 1)).wait_recv()

    def drain(self):
        c = self.c
        for w in self.all():
            mine = self.piece(w, self.b, c)
            self.ici(w, 0, mine, self.xn).wait_send()
            self.ici(w, 1, mine, self.yn).wait_send()
            self.ici(w, 2, self.piece(w, self.bx, c, 0), self.yn).wait_send()
            self.ici(w, 3, self.piece(w, self.by, c, 1), self.xn).wait_send()
            self.d2d(w, 0, self.piece(w, self.bx, c)).wait_send()
            self.d2d(w, 1, self.piece(w, self.by, c)).wait_send()
            self.d2d(w, 2, self.piece(w, self.bd, c, 0)).wait_send()
            self.d2d(w, 3, self.piece(w, self.bd, c, 1)).wait_send()


def _gather_sems(nw):
    return [pltpu.SemaphoreType.DMA((max(nw, 1), 4))] * 4


def _mixer_fwd(proj, cw8, lng, lnb, wc, bsb, fulls, tm=256):
    t = proj.shape[0]
    nt = t // tm
    nch = tm // CH
    nw = len(fulls)

    def body(*refs):
        p_ref, cw_ref, lng_ref, lnb_ref, wc_ref, bsb_ref = refs[:6]
        mix_ref = refs[6 + nw]
        w_outs = refs[7 + nw:7 + 2 * nw]
        prev_ref = refs[7 + 2 * nw]
        gather = _Gather(w_outs, *refs[8 + 2 * nw:])

        @pl.when(pl.program_id(0) == 0)
        def _():
            gather.start()
            prev_ref[...] = jnp.zeros_like(prev_ref)

        @pl.when(pl.program_id(0) == nt // 2)
        def _():
            gather.hop()

        @pl.when(pl.program_id(0) == nt - 1)
        def _():
            gather.far()

        rows = lax.broadcasted_iota(jnp.int32, (tm, CH), 0)
        for s in range(HEADS):
            cs = pl.ds(CH * s, CH)

            def slab(k):
                return p_ref[:, pl.ds(k * SLAB + CH * s, CH)].astype(F32)

            gb, gc, xa, za = slab(0), slab(1), slab(2), slab(3)
            cx = gc * xa
            p6 = jnp.broadcast_to(prev_ref[6:7, cs], (tm, CH))
            p7 = jnp.broadcast_to(prev_ref[7:8, cs], (tm, CH))
            c1 = jnp.where(rows == 0, p7, pltpu.roll(cx, 1, 0))
            c2 = jnp.where(rows == 0, p6, jnp.where(rows == 1, p7, pltpu.roll(cx, 2, 0)))
            prev_ref[:, cs] = cx[tm - 8:, :]
            cv = cw_ref[0:1, cs] * c2 + cw_ref[1:2, cs] * c1 + cw_ref[2:3, cs] * cx
            mix_ref[:, cs] = (gb * cv * (za * _sigmoid(za))).astype(BF16)

            u, v, zb = slab(4), slab(5), slab(6)
            ug, vg = _gelu(u), _gelu(v)
            dlt = vg - jnp.mean(vg, axis=-1, keepdims=True)
            vhat = dlt * lax.rsqrt(jnp.mean(dlt * dlt, axis=-1, keepdims=True) + EPS)
            vn = (vhat * lng_ref[:, cs] + lnb_ref[:, cs]).astype(BF16)
            gate = ug * (zb * _sigmoid(zb))
            for c in range(nch):
                rs = slice(CH * c, CH * (c + 1))
                sp = jnp.dot(wc_ref[s], vn[rs], preferred_element_type=F32) + bsb_ref[s]
                mix_ref[rs, pl.ds(SLAB + CH * s, CH)] = (gate[rs] * sp).astype(BF16)

        @pl.when(pl.program_id(0) == nt - 1)
        def _():
            gather.near_ready()
            gather.far_ready()
            gather.drain()

    sems = _gather_sems(nw)
    outs = pl.pallas_call(
        body, grid=(nt,),
        in_specs=[pl.BlockSpec((tm, IN_DIM), lambda i: (i, 0)), _full((8, D)), _full((1, D)), _full((1, D)),
                  _full((HEADS, CH, CH)), _full((HEADS, CH, CH))] + [ANY] * nw,
        out_specs=[pl.BlockSpec((tm, MIX), lambda i: (i, 0))] + [ANY] * nw,
        out_shape=[jax.ShapeDtypeStruct((t, MIX), BF16)] + [jax.ShapeDtypeStruct(f.shape, f.dtype) for f in fulls],
        input_output_aliases={6 + w: 1 + w for w in range(nw)},
        scratch_shapes=[pltpu.VMEM((8, D), F32)] + sems,
        compiler_params=_cp(("arbitrary",), VMEM_LIMIT), name="mixer_fwd")(proj, cw8, lng, lnb, wc, bsb, *fulls)
    return outs[0], outs[1:]


def _mem_fwd(mem, gm, wkv_f):
    n_mem = mem.shape[0]

    def body(mem_ref, gm_ref, w_ref, k_ref, v_ref):
        m, _ = _rms(mem_ref[...], gm_ref[...])
        mb = m.astype(BF16)
        for j in range(N_CHIP):
            dst = k_ref if j < 2 else v_ref
            dst[:, pl.ds(KV_BLK * (j % 2), KV_BLK)] = jnp.dot(mb, w_ref[j], preferred_element_type=F32).astype(BF16)

    return pl.pallas_call(
        body, out_shape=[jax.ShapeDtypeStruct((n_mem, D), BF16), jax.ShapeDtypeStruct((n_mem, D), BF16)],
        compiler_params=_cp(None, VMEM_LIMIT), name="mem_fwd")(mem, gm, wkv_f)


def _tail(x, tgt, mixin, wout, wq, wxo, k, v, g2, g3, tm=512, sub=512):
    t = x.shape[0]
    n_mem = k.shape[0]
    scale = 1.0 / math.sqrt(XD)

    def body(x_ref, tgt_ref, mix_ref, wout_ref, wq_ref, wxo_ref, k_ref, v_ref, g2_ref, g3_ref,
             loss_ref, dmix_ref, dx1b_ref, h2_ref, dq_ref, o_ref, dx2b_ref, dk_ref, dv_ref, dg2_ref, dg3_ref):
        @pl.when(pl.program_id(0) == 0)
        def _():
            loss_ref[...] = jnp.zeros_like(loss_ref)
            dk_ref[...] = jnp.zeros_like(dk_ref)
            dv_ref[...] = jnp.zeros_like(dv_ref)
            dg2_ref[...] = jnp.zeros_like(dg2_ref)
            dg3_ref[...] = jnp.zeros_like(dg3_ref)

        g2, g3 = g2_ref[...], g3_ref[...]
        for sb in range(tm // sub):
            rs = pl.ds(sub * sb, sub)
            x1 = x_ref[rs, :] + jnp.dot(mix_ref[rs, :], wout_ref[...], preferred_element_type=F32)
            h2, r2 = _rms(x1, g2)
            h2b = h2.astype(BF16)
            h2_ref[rs, :] = h2b
            q = jnp.dot(h2b, wq_ref[...], preferred_element_type=F32).astype(BF16)
            probs, outs = [], []
            for hd in range(XH):
                hs = pl.ds(XD * hd, XD)
                s = _bdot_nt(q[:, XD * hd:XD * (hd + 1)], k_ref[:, hs]) * scale
                e = jnp.exp(s - jnp.max(s, axis=-1, keepdims=True))
                p = e / jnp.sum(e, axis=-1, keepdims=True)
                probs.append(p)
                outs.append(_bdot(p, v_ref[:, hs]))
            ob = jnp.concatenate(outs, axis=-1).astype(BF16)
            o_ref[rs, :] = ob
            x2 = x1 + jnp.dot(ob, wxo_ref[...], preferred_element_type=F32)
            y, r3 = _rms(x2, g3)
            diff = y - tgt_ref[rs, :]
            row_loss = jnp.sum(diff * diff, axis=-1, keepdims=True)
            loss_ref[...] += jnp.broadcast_to(jnp.sum(row_loss, axis=0, keepdims=True) * (0.5 / D), loss_ref.shape)

            dx2, dg3 = _rms_bwd(diff * (1.0 / D), x2, r3, g3)
            dg3_ref[...] += dg3
            dx2b = dx2.astype(BF16)
            dx2b_ref[rs, :] = dx2b
            do = _bdot_nt(dx2b, wxo_ref[...])
            dqs = []
            for hd in range(XH):
                hs = pl.ds(XD * hd, XD)
                p = probs[hd]
                do_h = do[:, XD * hd:XD * (hd + 1)]
                dv_ref[:, hs] += _bdot_tn(p, do_h)
                dp = _bdot_nt(do_h, v_ref[:, hs])
                ds = p * (dp - jnp.sum(dp * p, axis=-1, keepdims=True))
                dqs.append(_bdot(ds, k_ref[:, hs]) * scale)
                dk_ref[:, hs] += _bdot_tn(ds, q[:, XD * hd:XD * (hd + 1)]) * scale
            dq = jnp.concatenate(dqs, axis=-1).astype(BF16)
            dq_ref[rs, :] = dq
            dx1n, dg2 = _rms_bwd(_bdot_nt(dq, wq_ref[...]), x1, r2, g2)
            dg2_ref[...] += dg2
            dx1b = (dx2 + dx1n).astype(BF16)
            dx1b_ref[rs, :] = dx1b
            dmix_ref[rs, :] = _bdot_nt(dx1b, wout_ref[...]).astype(BF16)

    tok = lambda w: pl.BlockSpec((tm, w), lambda i: (i, 0))
    return pl.pallas_call(
        body, grid=(t // tm,),
        in_specs=[tok(D), tok(D), tok(MIX), _full((MIX, D), 1), _full((D, D), 1), _full((D, D), 1),
                  _full((n_mem, D), 1), _full((n_mem, D), 1), _full((1, D)), _full((1, D))],
        out_specs=[_full((8, 128)), tok(MIX), tok(D), tok(D), tok(D), tok(D), tok(D),
                   _full((n_mem, D)), _full((n_mem, D)), _full((1, D)), _full((1, D))],
        out_shape=[jax.ShapeDtypeStruct((8, 128), F32), jax.ShapeDtypeStruct((t, MIX), BF16),
                   jax.ShapeDtypeStruct((t, D), BF16),
                   jax.ShapeDtypeStruct((t, D), BF16), jax.ShapeDtypeStruct((t, D), BF16),
                   jax.ShapeDtypeStruct((t, D), BF16), jax.ShapeDtypeStruct((t, D), BF16),
                   jax.ShapeDtypeStruct((n_mem, D), F32), jax.ShapeDtypeStruct((n_mem, D), F32),
                   jax.ShapeDtypeStruct((1, D), F32), jax.ShapeDtypeStruct((1, D), F32)],
        compiler_params=_cp(("arbitrary",), VMEM_LIMIT), name="tail")(x, tgt, mixin, wout, wq, wxo, k, v, g2, g3)


def _mem_bwd(mem, gm, dk, dv, wkv_f):
    def body(mem_ref, gm_ref, dk_ref, dv_ref, w_ref, dw_ref, dwb_ref, dgm_ref):
        mem_v = mem_ref[...]
        m, rm = _rms(mem_v, gm_ref[...])
        mb = m.astype(BF16)
        dm = jnp.zeros_like(mem_v)
        for j in range(N_CHIP):
            src = dk_ref if j < 2 else dv_ref
            dkv = src[:, pl.ds(KV_BLK * (j % 2), KV_BLK)].astype(BF16)
            dw = _bdot_tn(mb, dkv)
            dw_ref[j] = dw
            dwb_ref[j] = dw.astype(BF16)
            dm = dm + _bdot_nt(dkv, w_ref[j])
        dgm_ref[...] = jnp.sum(dm * mem_v * rm, axis=0, keepdims=True)

    return pl.pallas_call(
        body, out_shape=[jax.ShapeDtypeStruct((N_CHIP, D, KV_BLK), F32), jax.ShapeDtypeStruct((N_CHIP, D, KV_BLK), BF16),
                         jax.ShapeDtypeStruct((1, D), F32)],
        compiler_params=_cp(None, VMEM_LIMIT), name="mem_bwd")(mem, gm, dk, dv, wkv_f)


def _mixer_bwd(proj, dmix, cw8, lng, lnb, wc, wct, bsb, win_f, x, dx1, g1, tm=256):
    t = proj.shape[0]
    nt = t // tm
    nch = tm // CH
    hb = 16
    pair = 2 * CH
    assert pair == IN_PIECE

    def body(p_ref, pgc_ref, pxa_ref, dm_ref, cw_ref, lng_ref, lnb_ref, wc_ref, wct_ref, bsb_ref, w_ref, x_ref,
             dx1_ref, g1_ref, dp_ref, dcw_ref, dlng_ref, dlnb_ref, dwc_ref, dbs_ref, gx_ref, dg1_ref,
             next_ref, dh_ref):
        i = pl.program_id(0)

        @pl.when(i == 0)
        def _():
            next_ref[...] = jnp.zeros_like(next_ref)
            dcw_ref[...] = jnp.zeros_like(dcw_ref)
            dlng_ref[...] = jnp.zeros_like(dlng_ref)
            dlnb_ref[...] = jnp.zeros_like(dlnb_ref)
            dwc_ref[...] = jnp.zeros_like(dwc_ref)
            dbs_ref[...] = jnp.zeros_like(dbs_ref)
            dg1_ref[...] = jnp.zeros_like(dg1_ref)

        first_tile = i == nt - 1
        rows = lax.broadcasted_iota(jnp.int32, (tm, CH), 0)
        ones8 = jnp.ones((8, CH), BF16)
        for s in range(HEADS):
            cs = pl.ds(CH * s, CH)

            def slab(k):
                return p_ref[:, pl.ds(k * SLAB + CH * s, CH)].astype(F32)

            gb, gc, xa, za = slab(0), slab(1), slab(2), slab(3)
            da = dm_ref[:, cs].astype(F32)
            cx = gc * xa
            cxp = pgc_ref[:, cs].astype(F32) * pxa_ref[:, cs].astype(F32)
            cxp = jnp.where(first_tile, jnp.zeros_like(cxp), cxp)
            p6 = jnp.broadcast_to(cxp[hb - 2:hb - 1, :], (tm, CH))
            p7 = jnp.broadcast_to(cxp[hb - 1:hb, :], (tm, CH))
            c1 = jnp.where(rows == 0, p7, pltpu.roll(cx, 1, 0))
            c2 = jnp.where(rows == 0, p6, jnp.where(rows == 1, p7, pltpu.roll(cx, 2, 0)))
            w0, w1, w2 = cw_ref[0:1, cs], cw_ref[1:2, cs], cw_ref[2:3, cs]
            cv = w0 * c2 + w1 * c1 + w2 * cx
            sg = _sigmoid(za)
            sa = za * sg
            dcv = da * gb * sa
            dp_ref[:, pl.ds(0 * SLAB + CH * s, CH)] = (da * cv * sa).astype(BF16)
            dp_ref[:, pl.ds(3 * SLAB + CH * s, CH)] = (da * gb * cv * (sg * (1.0 + za * (1.0 - sg)))).astype(BF16)
            n0 = jnp.broadcast_to(next_ref[0:1, cs], (tm, CH))
            n1 = jnp.broadcast_to(next_ref[1:2, cs], (tm, CH))
            u1 = jnp.where(rows == tm - 1, n0, pltpu.roll(dcv, tm - 1, 0))
            u2 = jnp.where(rows == tm - 2, n0, jnp.where(rows == tm - 1, n1, pltpu.roll(dcv, tm - 2, 0)))
            next_ref[:, cs] = dcv[0:8, :]
            dcx = w2 * dcv + w1 * u1 + w0 * u2
            dp_ref[:, pl.ds(1 * SLAB + CH * s, CH)] = (dcx * xa).astype(BF16)
            dp_ref[:, pl.ds(2 * SLAB + CH * s, CH)] = (dcx * gc).astype(BF16)
            dcw_ref[0:1, cs] += jnp.sum(dcv * c2, axis=0, keepdims=True)
            dcw_ref[1:2, cs] += jnp.sum(dcv * c1, axis=0, keepdims=True)
            dcw_ref[2:3, cs] += jnp.sum(dcv * cx, axis=0, keepdims=True)

            u, v, zb = slab(4), slab(5), slab(6)
            db = dm_ref[:, pl.ds(SLAB + CH * s, CH)].astype(F32)
            ug, ugrad = _gelu_parts(u)
            vg, vgrad = _gelu_parts(v)
            dlt = vg - jnp.mean(vg, axis=-1, keepdims=True)
            rstd = lax.rsqrt(jnp.mean(dlt * dlt, axis=-1, keepdims=True) + EPS)
            vhat = dlt * rstd
            lg = lng_ref[:, cs]
            vn = (vhat * lg + lnb_ref[:, cs]).astype(BF16)
            sgb = _sigmoid(zb)
            szb = zb * sgb
            sps, dvns = [], []
            dbs = jnp.zeros((8, CH), F32)
            dwc = jnp.zeros((CH, CH), F32)
            for c in range(nch):
                rs = slice(CH * c, CH * (c + 1))
                sp = jnp.dot(wc_ref[s], vn[rs], preferred_element_type=F32) + bsb_ref[s]
                dsp = (db[rs] * ug[rs] * szb[rs]).astype(BF16)
                dbs = dbs + lax.dot_general(ones8, dsp, (((1,), (1,)), ((), ())), preferred_element_type=F32)
                dwc = dwc + lax.dot_general(dsp, vn[rs], (((1,), (1,)), ((), ())), preferred_element_type=F32)
                dvns.append(jnp.dot(wct_ref[s], dsp, preferred_element_type=F32))
                sps.append(sp)
            sp = jnp.concatenate(sps, axis=0)
            dvn = jnp.concatenate(dvns, axis=0)
            dbs_ref[:, cs] += dbs
            dwc_ref[s] += dwc
            dlng_ref[:, cs] += jnp.sum(dvn * vhat, axis=0, keepdims=True)
            dlnb_ref[:, cs] += jnp.sum(dvn, axis=0, keepdims=True)
            dvhat = dvn * lg
            dvg = rstd * (dvhat - jnp.mean(dvhat, axis=-1, keepdims=True)
                          - vhat * jnp.mean(dvhat * vhat, axis=-1, keepdims=True))
            dp_ref[:, pl.ds(4 * SLAB + CH * s, CH)] = (db * sp * szb * ugrad).astype(BF16)
            dp_ref[:, pl.ds(5 * SLAB + CH * s, CH)] = (dvg * vgrad).astype(BF16)
            dp_ref[:, pl.ds(6 * SLAB + CH * s, CH)] = (db * ug * sp * (sgb * (1.0 + zb * (1.0 - sgb)))).astype(BF16)

            if s % 2 == 1:
                part = None
                for k in range(N_SLAB):
                    col = k * SLAB + pair * (s // 2)
                    blk, off = divmod(col, IN_BLK)
                    term = lax.dot_general(dp_ref[:, pl.ds(col, pair)], w_ref[blk, off // IN_PIECE],
                                           (((1,), (1,)), ((), ())), preferred_element_type=F32)
                    part = term if part is None else part + term
                if s == 1:
                    dh_ref[...] = part
                else:
                    dh_ref[...] += part

        xv = x_ref[...]
        r = lax.rsqrt(jnp.mean(xv * xv, axis=-1, keepdims=True) + EPS)
        dxn, dg = _rms_bwd(dh_ref[...], xv, r, g1_ref[...])
        gx_ref[...] = dx1_ref[...].astype(F32) + dxn
        dg1_ref[0:1, :] += dg

        @pl.when(i == nt - 1)
        def _():
            tril = lax.broadcasted_iota(jnp.int32, (CH, CH), 0) >= lax.broadcasted_iota(jnp.int32, (CH, CH), 1)
            for s in range(HEADS):
                dwc_ref[s] = jnp.where(tril, dwc_ref[s], 0.0)

    rev = lambda i: nt - 1 - i
    halo = lambda col: pl.BlockSpec((hb, SLAB), lambda i: (jnp.maximum(rev(i) * (tm // hb) - 1, 0), col))
    tok = lambda w: pl.BlockSpec((tm, w), lambda i: (rev(i), 0))
    return pl.pallas_call(
        body, grid=(nt,),
        in_specs=[tok(IN_DIM), halo(1), halo(2), tok(MIX), _full((8, D)), _full((1, D)), _full((1, D)),
                  _full((HEADS, CH, CH)), _full((HEADS, CH, CH)), _full((HEADS, CH, CH)),
                  _full((N_CHIP, N_PIECE, D, IN_PIECE), 1), tok(D), tok(D), _full((1, D))],
        out_specs=[tok(IN_DIM), _full((8, D)), _full((1, D)), _full((1, D)), _full((HEADS, CH, CH)), _full((8, D)),
                   tok(D), _full((8, D))],
        out_shape=[jax.ShapeDtypeStruct((t, IN_DIM), BF16), jax.ShapeDtypeStruct((8, D), F32),
                   jax.ShapeDtypeStruct((1, D), F32), jax.ShapeDtypeStruct((1, D), F32),
                   jax.ShapeDtypeStruct((HEADS, CH, CH), F32), jax.ShapeDtypeStruct((8, D), F32),
                   jax.ShapeDtypeStruct((t, D), F32), jax.ShapeDtypeStruct((8, D), F32)],
        scratch_shapes=[pltpu.VMEM((8, D), F32), pltpu.VMEM((tm, D), F32)],
        compiler_params=_cp(("arbitrary",), VMEM_LIMIT), name="mixer_bwd")(
            proj, proj, proj, dmix, cw8, lng, lnb, wc, wct, bsb, win_f, x, dx1, g1)


def _grad_matmul(a, b, after, *, by_cols, name, tk=1024):
    t, m = a.shape
    n = b.shape[1]
    nk = t // tk
    nj = N_CHIP if by_cols else 1
    bn = n // nj

    def body(a_ref, b_ref, after_ref, o_ref, ob_ref):
        kk = pl.program_id(1)
        part = lax.dot_general(a_ref[...], b_ref[...], (((0,), (0,)), ((), ())), preferred_element_type=F32)

        @pl.when(kk == 0)
        def _():
            o_ref[...] = part

        @pl.when(kk > 0)
        def _():
            o_ref[...] += part

        @pl.when(kk == nk - 1)
        def _():
            ob_ref[...] = o_ref[...].astype(BF16)

    a_spec = pl.BlockSpec((tk, m), lambda j, k: (k, 0))
    b_spec = pl.BlockSpec((tk, bn), lambda j, k: (k, j))
    o_spec = pl.BlockSpec((None, m, bn), lambda j, k: (j, 0, 0))
    o32, o16 = pl.pallas_call(
        body, grid=(nj, nk), in_specs=[a_spec, b_spec, ANY], out_specs=[o_spec, o_spec],
        out_shape=[jax.ShapeDtypeStruct((nj, m, bn), F32), jax.ShapeDtypeStruct((nj, m, bn), BF16)],
        compiler_params=_cp(("parallel", "arbitrary"), VMEM_LIMIT), name=name)(a, b, after)
    if by_cols:
        return o32, o16
    return o32.reshape(N_CHIP, m // N_CHIP, n), o16.reshape(N_CHIP, m // N_CHIP, n)


def _coords():
    x, y, c = lax.axis_index("x"), lax.axis_index("y"), lax.axis_index("c")
    chips = [(1 - x, y), (x, 1 - y), (1 - x, 1 - y)]
    return x, y, c, chips


def _pair_reduce(c_idx, grads, grads_b, smalls, name):
    ng, ns = len(grads), len(smalls)
    halves = [g.shape[1] // 2 for g in grads]

    def body(c_ref, *refs):
        g_in, gb_any = refs[:ng], refs[ng:2 * ng]
        s_own, s_any = refs[2 * ng:2 * ng + ns], refs[2 * ng + ns:2 * ng + 2 * ns]
        o = refs[2 * ng + 2 * ns:4 * ng + 3 * ns]
        lands = refs[4 * ng + 3 * ns:5 * ng + 4 * ns]
        send, recv = refs[5 * ng + 4 * ns:]
        x, y, c, _ = _coords()
        j = pl.program_id(0)

        def big(i, blk):
            return pltpu.make_async_remote_copy(
                src_ref=gb_any[i].at[blk, pl.ds((1 - c) * halves[i], halves[i])], dst_ref=lands[i].at[blk],
                send_sem=send.at[i, blk], recv_sem=recv.at[i, blk], device_id=(x, y, 1 - c), device_id_type=MESH)

        def small(i):
            return pltpu.make_async_remote_copy(
                src_ref=s_any[i].at[1 - c], dst_ref=lands[ng + i],
                send_sem=send.at[ng + i, 0], recv_sem=recv.at[ng + i, 0], device_id=(x, y, 1 - c), device_id_type=MESH)

        @pl.when(j == 0)
        def _():
            for blk in range(N_CHIP):
                for i in range(ng):
                    big(i, blk).start()
            for i in range(ns):
                small(i).start()

        for i in range(ng):
            big(i, j).wait_recv()
            tot = g_in[i][...] + lands[i][j].astype(F32)
            o[i][...] = tot
            o[ng + i][...] = tot.astype(BF16)

        @pl.when(j == N_CHIP - 1)
        def _():
            for i in range(ns):
                small(i).wait_recv()
                o[2 * ng + i][...] = s_own[i][...] + lands[ng + i][...]
                small(i).wait_send()
            for blk in range(N_CHIP):
                for i in range(ng):
                    big(i, blk).wait_send()

    in_specs = [pl.BlockSpec((None, None, halves[i], g.shape[2]), lambda b, c: (b, c[0], 0, 0)) for i, g in enumerate(grads)]
    in_specs += [ANY] * ng
    in_specs += [pl.BlockSpec((None, s.shape[0] // 2, s.shape[1]), lambda b, c: (c[0], 0, 0)) for s in smalls]
    in_specs += [ANY] * ns
    blk = [pl.BlockSpec((None, halves[i], g.shape[2]), lambda b, c: (b, 0, 0)) for i, g in enumerate(grads)]
    out_specs = blk + blk + [pl.BlockSpec((s.shape[0] // 2, s.shape[1]), lambda b, c: (0, 0)) for s in smalls]
    out_shape = [jax.ShapeDtypeStruct((N_CHIP, halves[i], g.shape[2]), F32) for i, g in enumerate(grads)]
    out_shape += [jax.ShapeDtypeStruct((N_CHIP, halves[i], g.shape[2]), BF16) for i, g in enumerate(grads)]
    out_shape += [jax.ShapeDtypeStruct((s.shape[0] // 2, s.shape[1]), F32) for s in smalls]
    scratch = [pltpu.VMEM((N_CHIP, halves[i], g.shape[2]), BF16) for i, g in enumerate(grads)]
    scratch += [pltpu.VMEM((s.shape[0] // 2, s.shape[1]), F32) for s in smalls]
    scratch += [pltpu.SemaphoreType.DMA((ng + ns, N_CHIP)), pltpu.SemaphoreType.DMA((ng + ns, N_CHIP))]
    grads4 = [g.reshape(N_CHIP, 2, halves[i], g.shape[2]) for i, g in enumerate(grads)]
    smalls3 = [s.reshape(2, s.shape[0] // 2, s.shape[1]) for s in smalls]
    return pl.pallas_call(
        body, out_shape=out_shape,
        grid_spec=pltpu.PrefetchScalarGridSpec(num_scalar_prefetch=1, grid=(N_CHIP,), in_specs=in_specs,
                                               out_specs=out_specs, scratch_shapes=scratch),
        compiler_params=_cp(("arbitrary",), VMEM_LIMIT), name=name)(c_idx, *grads4, *grads_b, *smalls3, *smalls3)


def _grad_matmul_pair(c_idx, a, b, smalls, after, *, name, tk=2048):
    t, m = a.shape
    bn = b.shape[1] // N_CHIP
    nk = t // tk
    hr = m // 2
    ns = len(smalls)

    def body(c_ref, a_ref, b_ref, *refs):
        s_own, s_any = refs[:ns], refs[ns:2 * ns]
        o32, o16 = refs[2 * ns + 1], refs[2 * ns + 2]
        o_small = refs[2 * ns + 3:3 * ns + 3]
        acc, keep, tb, land, st32, st16 = refs[3 * ns + 3:3 * ns + 9]
        s_land, s_stage = refs[3 * ns + 9:4 * ns + 9], refs[4 * ns + 9:5 * ns + 9]
        send, recv, loc = refs[5 * ns + 9:]
        x, y, c, _ = _coords()
        sibling = dict(device_id=(x, y, 1 - c), device_id_type=MESH)
        j, kk = pl.program_id(0), pl.program_id(1)
        mine = pl.ds(pl.multiple_of(c * hr, hr), hr)
        theirs = pl.ds(pl.multiple_of((1 - c) * hr, hr), hr)

        def to_sibling(blk):
            return pltpu.make_async_remote_copy(src_ref=tb, dst_ref=land.at[blk], send_sem=send.at[blk],
                                                recv_sem=recv.at[blk], **sibling)

        def small(i):
            return pltpu.make_async_remote_copy(src_ref=s_any[i].at[1 - c], dst_ref=s_land[i], send_sem=send.at[N_CHIP + i],
                                                recv_sem=recv.at[N_CHIP + i], **sibling)

        def written(blk):
            return (pltpu.make_async_copy(st32, o32.at[blk], loc.at[0]), pltpu.make_async_copy(st16, o16.at[blk], loc.at[1]))

        def finish(blk):
            to_sibling(blk).wait_recv()

            @pl.when(blk > 0)
            def _():
                for cp in written(blk - 1):
                    cp.wait()

            tot = keep[...] + land[blk].astype(F32)
            st32[...] = tot
            st16[...] = tot.astype(BF16)
            for cp in written(blk):
                cp.start()

        @pl.when((j == 0) & (kk == 0))
        def _():
            for i in range(ns):
                small(i).start()

        @pl.when((j > 0) & (kk == 0))
        def _():
            finish(j - 1)

        part = lax.dot_general(a_ref[...], b_ref[...], (((0,), (0,)), ((), ())), preferred_element_type=F32)

        @pl.when(kk == 0)
        def _():
            acc[...] = part

        @pl.when(kk > 0)
        def _():
            acc[...] += part

        @pl.when(kk == nk - 1)
        def _():
            @pl.when(j > 0)
            def _():
                to_sibling(j - 1).wait_send()

            tb[...] = acc[theirs, :].astype(BF16)
            to_sibling(j).start()
            keep[...] = acc[mine, :]

        @pl.when((j == N_CHIP - 1) & (kk == nk - 1))
        def _():
            finish(j)
            for i in range(ns):
                small(i).wait_recv()
                s_stage[i][...] = s_own[i][...] + s_land[i][...]
                cp = pltpu.make_async_copy(s_stage[i], o_small[i], loc.at[2 + i])
                cp.start()
                cp.wait()
                small(i).wait_send()
            for cp in written(j):
                cp.wait()
            to_sibling(j).wait_send()

    halves = [(s.shape[0] // 2, s.shape[1]) for s in smalls]
    in_specs = [pl.BlockSpec((tk, m), lambda j, k, c: (k, 0)), pl.BlockSpec((tk, bn), lambda j, k, c: (k, j))]
    in_specs += [pl.BlockSpec((None,) + h, lambda j, k, c: (c[0], 0, 0)) for h in halves] + [ANY] * ns + [ANY]
    out_shape = [jax.ShapeDtypeStruct((N_CHIP, hr, bn), F32), jax.ShapeDtypeStruct((N_CHIP, hr, bn), BF16)]
    out_shape += [jax.ShapeDtypeStruct(h, F32) for h in halves]
    scratch = [pltpu.VMEM((m, bn), F32), pltpu.VMEM((hr, bn), F32), pltpu.VMEM((hr, bn), BF16),
               pltpu.VMEM((N_CHIP, hr, bn), BF16), pltpu.VMEM((hr, bn), F32), pltpu.VMEM((hr, bn), BF16)]
    scratch += [pltpu.VMEM(h, F32) for h in halves] * 2
    scratch += [pltpu.SemaphoreType.DMA((N_CHIP + ns,)), pltpu.SemaphoreType.DMA((N_CHIP + ns,)),
                pltpu.SemaphoreType.DMA((2 + ns,))]
    smalls3 = [s.reshape((2,) + h) for s, h in zip(smalls, halves)]
    outs = pl.pallas_call(
        body, out_shape=out_shape,
        grid_spec=pltpu.PrefetchScalarGridSpec(num_scalar_prefetch=1, grid=(N_CHIP, nk), in_specs=in_specs,
                                               out_specs=[ANY] * (2 + ns), scratch_shapes=scratch),
        compiler_params=_cp(("arbitrary", "arbitrary"), VMEM_LIMIT), name=name)(c_idx, a, b, *smalls3, *smalls3, after)
    return outs[0], outs[1], list(outs[2:])


_HBM = pl.BlockSpec(memory_space=pltpu.HBM)
_SEM = pl.BlockSpec(memory_space=pltpu.SEMAPHORE)


def _split_copies(ins, lands, ng, send, recv, arriving):
    x, y, c, chips = _coords()
    b = 2 * x + y
    copies = []
    for i in range(len(ins)):
        for k in range(3):
            blk = 2 * chips[k][0] + chips[k][1]
            src, dst, got = (ins[i].at[blk], lands[i].at[k], lands[i].at[k]) if i < ng else (ins[i], lands[i].at[b], lands[i].at[blk])
            sems = dict(send_sem=send.at[3 * i + k], recv_sem=recv.at[3 * i + k], device_id=(*chips[k], c), device_id_type=MESH)
            if arriving:
                copies.append(pltpu.make_async_remote_copy(src_ref=got, dst_ref=got, **sems))
            else:
                copies.append(pltpu.make_async_remote_copy(src_ref=src, dst_ref=dst, **sems))
    return copies


def _exchange_begin(sums_b, smalls, name):
    ng, n = len(sums_b), len(sums_b) + len(smalls)
    srcs = list(sums_b) + list(smalls)
    lands = [lax.empty((3,) + g.shape[1:], g.dtype) for g in sums_b] + [lax.empty((N_CHIP,) + s.shape, s.dtype) for s in smalls]

    def body(*refs):
        ins, land_refs = refs[:n], refs[n:2 * n]
        send, recv = refs[2 * n], refs[2 * n + 1]
        token = refs[4 * n + 2]
        for cp in _split_copies(ins, land_refs, ng, send, recv, False):
            cp.start()
        token[...] = jnp.zeros_like(token)

    hbm = lambda a: pltpu.HBM(a.shape, a.dtype)
    outs = pl.pallas_call(
        body, name=name,
        out_shape=(pltpu.SemaphoreType.DMA((3 * n,)), pltpu.SemaphoreType.DMA((3 * n,)), *[hbm(a) for a in srcs + lands],
                   jax.ShapeDtypeStruct((8, 128), F32)),
        in_specs=[_HBM] * (2 * n), out_specs=(_SEM, _SEM, *[_HBM] * (2 * n), pl.BlockSpec(memory_space=pltpu.VMEM)),
        input_output_aliases={i: 2 + i for i in range(2 * n)},
        compiler_params=pltpu.CompilerParams(has_side_effects=pltpu.SideEffectType.DATAFLOW_SIDE_EFFECTING),
    )(*[pltpu.with_memory_space_constraint(a, pltpu.HBM) for a in srcs + lands])
    return outs[0], outs[1], list(outs[2:2 + n]), list(outs[2 + n:2 + 2 * n]), outs[2 + 2 * n]


def _exchange_end(send, recv, srcs, lands, ng, after, name):
    n = len(srcs)
    after = list(after)

    def body(*refs):
        ins, land_refs = refs[:n], refs[n:2 * n]
        send_ref, recv_ref = refs[2 * n], refs[2 * n + 1]
        for cp in _split_copies(ins, land_refs, ng, send_ref, recv_ref, False):
            cp.wait_send()
        for cp in _split_copies(ins, land_refs, ng, send_ref, recv_ref, True):
            cp.wait_recv()

    hbm = lambda a: pltpu.HBM(a.shape, a.dtype)
    outs = pl.pallas_call(
        body, name=name, out_shape=tuple(hbm(a) for a in list(srcs) + list(lands)),
        in_specs=[_HBM] * (2 * n) + [_SEM, _SEM] + [ANY] * len(after), out_specs=tuple([_HBM] * (2 * n)),
        input_output_aliases={i: i for i in range(2 * n)},
        compiler_params=pltpu.CompilerParams(has_side_effects=pltpu.SideEffectType.DATAFLOW_SIDE_EFFECTING),
    )(*srcs, *lands, send, recv, *after)
    return list(outs[n:])


def _chip_reduce(bc_idx, sums, recvd, smalls_slots, smalls_own, name, steps=4):
    ng, ns = len(sums), len(smalls_slots)
    n = ng + ns
    assert steps >= 2
    halves = [g.shape[1] for g in sums] + [s.shape[1] for s in smalls_slots]
    rows = [g.shape[1] // steps for g in sums]

    def body(bc_ref, *refs):
        own, rx = refs[:ng], refs[ng:2 * ng]
        sl = refs[2 * ng:2 * ng + ns]
        sl_own = refs[2 * ng + ns:2 * ng + 2 * ns]
        o = refs[2 * ng + 2 * ns:2 * ng + 2 * ns + n]
        tiles = refs[2 * ng + 2 * ns + n:2 * ng + 2 * ns + 2 * n]
        keep, send, recv = refs[2 * ng + 2 * ns + 2 * n:]
        x, y, c, _ = _coords()
        sibling = dict(device_id=(x, y, 1 - c), device_id_type=MESH)
        r = pl.program_id(0)

        def writes(i, step, slot):
            dst = o[i].at[pl.ds(c * halves[i] + step * rows[i], rows[i])]
            return (pltpu.make_async_copy(tiles[i].at[slot], dst, keep.at[i, slot]),
                    pltpu.make_async_remote_copy(src_ref=tiles[i].at[slot], dst_ref=dst, send_sem=send.at[i, slot],
                                                 recv_sem=recv.at[i, step], **sibling))

        def small_writes(i):
            dst = o[i].at[pl.ds(c * halves[i], halves[i])]
            return (pltpu.make_async_copy(tiles[i], dst, keep.at[i, 0]),
                    pltpu.make_async_remote_copy(src_ref=tiles[i], dst_ref=dst, send_sem=send.at[i, 0],
                                                 recv_sem=recv.at[i, 0], **sibling))

        def arriving(i, step, nrows):
            dst = o[i].at[pl.ds((1 - c) * halves[i] + step * nrows, nrows)]
            return pltpu.make_async_remote_copy(src_ref=dst, dst_ref=dst, send_sem=send.at[i, 0], recv_sem=recv.at[i, step],
                                                **sibling)

        def finish(step, slot):
            for i in range(ng):
                local, remote = writes(i, step, slot)
                local.wait()
                remote.wait_send()

        @pl.when(r >= 2)
        def _():
            finish(r - 2, r % 2)

        for i in range(ng):
            tot = own[i][...]
            for j in range(3):
                tot = tot + rx[i][j].astype(F32)
            tiles[i][r % 2] = tot
            for cp in writes(i, r, r % 2):
                cp.start()

        @pl.when(r == 0)
        def _():
            for i in range(ns):
                term = [jnp.where(bc_ref[0] == kk, sl_own[i][...], sl[i][kk]) for kk in range(N_CHIP)]
                tiles[ng + i][...] = ((term[0] + term[1]) + term[2]) + term[3]
                for cp in small_writes(ng + i):
                    cp.start()

        @pl.when(r == steps - 1)
        def _():
            finish(steps - 2, (steps - 2) % 2)
            finish(steps - 1, (steps - 1) % 2)
            for i in range(ns):
                local, remote = small_writes(ng + i)
                local.wait()
                remote.wait_send()
                arriving(ng + i, 0, halves[ng + i]).wait_recv()
            for i in range(ng):
                for step in range(steps):
                    arriving(i, step, rows[i]).wait_recv()

    in_specs = [pl.BlockSpec((None, rows[i], g.shape[2]), lambda r, bc: (bc[0], r, 0)) for i, g in enumerate(sums)]
    in_specs += [pl.BlockSpec((3, rows[i], g.shape[2]), lambda r, bc: (0, r, 0)) for i, g in enumerate(sums)]
    in_specs += [pl.BlockSpec(s.shape, lambda r, bc: (0, 0, 0)) for s in smalls_slots]
    in_specs += [pl.BlockSpec(s.shape[1:], lambda r, bc: (0, 0)) for s in smalls_slots]
    out_shape = [jax.ShapeDtypeStruct((2 * g.shape[1], g.shape[2]), F32) for g in sums]
    out_shape += [jax.ShapeDtypeStruct((2 * s.shape[1], s.shape[2]), F32) for s in smalls_slots]
    scratch = [pltpu.VMEM((2, rows[i], g.shape[2]), F32) for i, g in enumerate(sums)]
    scratch += [pltpu.VMEM(s.shape[1:], F32) for s in smalls_slots]
    scratch += [pltpu.SemaphoreType.DMA((n, 2)), pltpu.SemaphoreType.DMA((n, 2)), pltpu.SemaphoreType.DMA((n, steps))]
    return list(pl.pallas_call(
        body, out_shape=out_shape,
        grid_spec=pltpu.PrefetchScalarGridSpec(num_scalar_prefetch=1, grid=(steps,), in_specs=in_specs,
                                               out_specs=[ANY] * n, scratch_shapes=scratch),
        compiler_params=_cp(("arbitrary",), VMEM_LIMIT), name=name)(bc_idx, *sums, *recvd, *smalls_slots, *smalls_own))


def _adamw_math(w, g, m, v):
    m2 = ADAM_B1 * m + (1.0 - ADAM_B1) * g
    v2 = ADAM_B2 * v + (1.0 - ADAM_B2) * (g * g)
    m_hat = m2 / (1.0 - ADAM_B1 ** ADAM_STEP)
    v_hat = v2 / (1.0 - ADAM_B2 ** ADAM_STEP)
    delta = -ADAM_LR * (m_hat / (jnp.sqrt(v_hat) + ADAM_EPS) + ADAM_WD * w)
    return delta, m2, v2


def _adamw_big(ws, gs, ms, vs, name, steps=8):
    n = len(ws)

    def body(*refs):
        for i in range(n):
            w_ref, g_ref, m_ref, v_ref = (refs[k * n + i] for k in range(4))
            d_ref, m2_ref, v2_ref, g2_ref = (refs[(4 + k) * n + i] for k in range(4))
            gv = g_ref[...]
            d_ref[...], m2_ref[...], v2_ref[...] = _adamw_math(w_ref[...], gv, m_ref[...], v_ref[...])
            g2_ref[...] = gv

    specs = [pl.BlockSpec((w.shape[0] // steps, w.shape[1]), lambda i: (i, 0)) for w in ws]
    shapes = [jax.ShapeDtypeStruct(w.shape, F32) for w in ws]
    outs = pl.pallas_call(
        body, grid=(steps,), in_specs=specs * 4, out_specs=specs * 4, out_shape=shapes * 4,
        compiler_params=_cp(("parallel",), VMEM_LIMIT), name=name)(*ws, *gs, *ms, *vs)
    return [tuple(outs[k * n + i] for k in range(4)) for i in range(n)]


def _adamw_small(groups):
    n = len(groups)

    def body(*refs):
        for i in range(n):
            w_ref, g_ref, m_ref, v_ref = refs[4 * i:4 * i + 4]
            d_ref, m2_ref, v2_ref = refs[4 * n + 3 * i:4 * n + 3 * i + 3]
            d_ref[...], m2_ref[...], v2_ref[...] = _adamw_math(w_ref[...], g_ref[...], m_ref[...], v_ref[...])

    flat = [a for grp in groups for a in grp]
    out_shape = [jax.ShapeDtypeStruct(grp[0].shape, F32) for grp in groups for _ in range(3)]
    outs = pl.pallas_call(body, out_shape=out_shape, name="adamw_small")(*flat)
    return [tuple(outs[3 * i:3 * i + 3]) for i in range(n)]


def kernel(x, mem, norm_mix_g, w_in, conv_w, gm_ln_g, gm_ln_b, gm_ws, gm_bs, w_out, norm_x_g, norm_mem_g, w_q, w_kv, w_xo, norm_final_g, loss_target, m_norm_mix_g, m_w_in, m_conv_w, m_gm_ln_g, m_gm_ln_b, m_gm_ws, m_gm_bs, m_w_out, m_norm_x_g, m_norm_mem_g, m_w_q, m_w_kv, m_w_xo, m_norm_final_g, v_norm_mix_g, v_w_in, v_conv_w, v_gm_ln_g, v_gm_ln_b, v_gm_ws, v_gm_bs, v_w_out, v_norm_x_g, v_norm_mem_g, v_w_q, v_w_kv, v_w_xo, v_norm_final_g):
    t = x.shape[1]
    xi = lax.axis_index("x")
    yi = lax.axis_index("y")
    ci = lax.axis_index("c")
    b_idx = jnp.reshape(2 * xi + yi, (1,)).astype(jnp.int32)
    c_idx = jnp.reshape(ci, (1,)).astype(jnp.int32)

    x2d, mem2d, tgt = x[0], mem[0], loss_target[0]
    big = [w_in[0], w_out[0], w_q[0], w_kv[0], w_xo[0]]
    big_m = [m_w_in[0], m_w_out[0], m_w_q[0], m_w_kv[0], m_w_xo[0]]
    big_v = [v_w_in[0], v_w_out[0], v_w_q[0], v_w_kv[0], v_w_xo[0]]
    g3 = norm_final_g.reshape(1, D)

    def pad8(a):
        return jnp.pad(a, ((0, 8 - a.shape[0]), (0, 0)))

    own_blocks = _cast_shards(b_idx, big)

    tril = jnp.tril(jnp.ones((CH, CH), bool))
    wc32 = jnp.where(tril[None], gm_ws[0], 0.0)
    wc = wc32.astype(BF16)
    wct = jnp.swapaxes(wc32, 1, 2).astype(BF16)
    bsb = jnp.broadcast_to(gm_bs[0][:, :, None], (HEADS, CH, CH))

    blk = 2 * xi + yi
    near = [blk ^ (2 >> (k % 2)) for k in range(2 * N_PIECE)]
    seq_blk = jnp.stack([blk] * N_PIECE + near + [blk ^ 3] * N_PIECE)
    seq_piece = jnp.asarray(list(range(N_PIECE)) + [k // 2 for k in range(2 * N_PIECE)] + list(range(N_PIECE)))
    seq = jnp.stack([seq_blk, seq_piece, seq_blk * N_PIECE + seq_piece]).astype(jnp.int32)
    proj, hb, win_f, cw8, (wq_f,) = _proj_gather(
        seq, x2d, norm_mix_g, own_blocks[0], pad8(conv_w[0]), [own_blocks[2]])
    mixin, (wout_f, wkv_f, wxo_f) = _mixer_fwd(
        proj, cw8, gm_ln_g, gm_ln_b, wc, bsb, [own_blocks[1], own_blocks[3], own_blocks[4]])
    wout2, wq2, wxo2 = wout_f.reshape(MIX, D), wq_f.reshape(D, D), wxo_f.reshape(D, D)
    k, v = _mem_fwd(mem2d, norm_mem_g, wkv_f)

    (loss_tile, dmix, dx1b, h2b, dq, ob, dx2b, dk, dv, dg2, dg3) = _tail(
        x2d, tgt, mixin, wout2, wq2, wxo2, k, v, norm_x_g, g3)
    dwkv, dwkv_b, dgm = _mem_bwd(mem2d, norm_mem_g, dk, dv, wkv_f)
    dproj, dcw, dlng, dlnb, dwc, dbs8, grad_x, dg1 = _mixer_bwd(
        proj, dmix, cw8, gm_ln_g, gm_ln_b, wc, wct, bsb, win_f, x2d, dx1b, norm_mix_g)

    bc_idx = jnp.concatenate([b_idx, c_idx])
    zero = jnp.zeros((1, D), F32)
    loss_row = jnp.broadcast_to(loss_tile[0:1, 0:1], (1, D))
    sv = jnp.concatenate([dg1[0:1], dg2, dgm, dg3, dlng, dlnb, dbs8[0:1], loss_row, dcw], axis=0)
    sw = dwc.reshape(HEADS * CH, CH)
    dwin_sum, dwin_sum_b, psmall = _grad_matmul_pair(c_idx, hb, dproj, [sv, sw], dgm, name="grad_w_in")
    sums_b = [dwin_sum]
    send_b, recv_b, src_b, land_b, token_b = _exchange_begin([dwin_sum_b], psmall, "exchange_b_begin")

    dwxo, dwxo_b = _grad_matmul(ob, dx2b, token_b, by_cols=False, name="grad_w_xo", tk=2048)
    dwq, dwq_b = _grad_matmul(h2b, dq, token_b, by_cols=False, name="grad_w_q", tk=2048)
    dwout, dwout_b = _grad_matmul(mixin, dx1b, token_b, by_cols=False, name="grad_w_out")
    ps_a = _pair_reduce(c_idx, [dwout, dwq, dwkv, dwxo], [dwout_b, dwq_b, dwkv_b, dwxo_b], [], "pair_reduce_a")
    sums_a, sums_a_b = list(ps_a[:4]), list(ps_a[4:8])
    send_a, recv_a, src_a, land_a, token_a = _exchange_begin(sums_a_b, [], "exchange_a_begin")

    rx2b = _exchange_end(send_b, recv_b, src_b, land_b, 1, [token_a], "exchange_b_end")
    gwin, svf, swf = _chip_reduce(bc_idx, sums_b, rx2b[:1], rx2b[1:], psmall, "chip_reduce_b")
    out_b = _adamw_big(big[:1], [gwin], big_m[:1], big_v[:1], "adamw_w_in")[0]

    def vec_pack(a1, a2, am, a3, lg, lb, bs):
        return jnp.concatenate([a1, a2, am, a3.reshape(1, D), lg, lb, bs.reshape(1, D), zero], axis=0)

    wv = vec_pack(norm_mix_g, norm_x_g, norm_mem_g, norm_final_g, gm_ln_g, gm_ln_b, gm_bs)
    mv = vec_pack(m_norm_mix_g, m_norm_x_g, m_norm_mem_g, m_norm_final_g, m_gm_ln_g, m_gm_ln_b, m_gm_bs)
    vv = vec_pack(v_norm_mix_g, v_norm_x_g, v_norm_mem_g, v_norm_final_g, v_gm_ln_g, v_gm_ln_b, v_gm_bs)
    loss = svf[7, 0]
    gcw = lax.dynamic_slice_in_dim(svf[8:16], blk * (D // N_CHIP), D // N_CHIP, axis=1)
    gws = swf
    gv = svf[0:8]
    (dv_, mv_, vv_), (dc_, mc_, vc_), (dws_, mws_, vws_) = _adamw_small([
        (wv, gv, mv, vv),
        (pad8(conv_w[0]), gcw, pad8(m_conv_w[0]), pad8(v_conv_w[0])),
        (gm_ws.reshape(HEADS * CH, CH), gws, m_gm_ws.reshape(HEADS * CH, CH), v_gm_ws.reshape(HEADS * CH, CH))])

    rx2a = _exchange_end(send_a, recv_a, src_a, land_a, 4, [out_b[0], dv_], "exchange_a_end")
    g_a = _chip_reduce(bc_idx, sums_a, rx2a, [], [], "chip_reduce_a")
    big_out = [out_b] + _adamw_big(big[1:], g_a, big_m[1:], big_v[1:], "adamw_rest")

    def unpack(vecs, cw, ws, bigs):
        r = lambda i: vecs[i:i + 1]
        return [r(0), bigs[0][None], cw[0:3][None], r(4), r(5), ws.reshape(1, HEADS, CH, CH), vecs[6].reshape(1, HEADS, CH),
                bigs[1][None], r(1), r(2), bigs[2][None], bigs[3][None], bigs[4][None], vecs[3]]

    grads_out = unpack(gv, gcw, gws, [o[3] for o in big_out])
    delta_out = unpack(dv_, dc_, dws_, [o[0] for o in big_out])
    m_out = unpack(mv_, mc_, mws_, [o[1] for o in big_out])
    v_out = unpack(vv_, vc_, vws_, [o[2] for o in big_out])
    return (loss, grad_x[None], *grads_out, *delta_out, *m_out, *v_out)
```

```python
import functools
import math

import jax
import jax.numpy as jnp
from jax import lax
from jax.experimental import pallas as pl
from jax.experimental.pallas import tpu as pltpu

F32 = jnp.float32
BF16 = jnp.bfloat16
MESH = pl.DeviceIdType.MESH

D = 1024
SLAB = 1024
N_SLAB = 7
IN_DIM = N_SLAB * SLAB
MIX = 2 * SLAB
HEADS = 8
CH = 128
XH = 4
XD = D // XH
EPS = 1e-6
GELU_C = math.sqrt(2.0 / math.pi)
GELU_A = 0.044715
N_CHIP = 4
IN_BLK = IN_DIM // N_CHIP
IN_PIECE = 256
N_PIECE = IN_BLK // IN_PIECE
KV_BLK = 2 * D // N_CHIP

ADAM_LR, ADAM_B1, ADAM_B2, ADAM_EPS, ADAM_WD, ADAM_STEP = 0.001, 0.9, 0.999, 1e-08, 0.01, 10

VMEM_LIMIT = 60 * 1024 * 1024


def _cp(sem=None, vmem=None):
    return pltpu.CompilerParams(dimension_semantics=sem, vmem_limit_bytes=vmem)


def _full(shape, buffers=None):
    n = len(shape)
    if buffers is None:
        return pl.BlockSpec(shape, lambda *_: (0,) * n)
    return pl.BlockSpec(shape, lambda *_: (0,) * n, pipeline_mode=pl.Buffered(buffers))


ANY = pl.BlockSpec(memory_space=pl.ANY)


def _bdot(a, b):
    return jnp.dot(a.astype(BF16), b.astype(BF16), preferred_element_type=F32)


def _bdot_nt(a, b):
    return lax.dot_general(a.astype(BF16), b.astype(BF16), (((1,), (1,)), ((), ())), preferred_element_type=F32)


def _bdot_tn(a, b):
    return lax.dot_general(a.astype(BF16), b.astype(BF16), (((0,), (0,)), ((), ())), preferred_element_type=F32)


def _rms(x, g):
    r = lax.rsqrt(jnp.mean(x * x, axis=-1, keepdims=True) + EPS)
    return x * r * g, r


def _rms_bwd(dy, x, r, g):
    gdy = dy * g
    dx = r * gdy - x * (r * r * r) * jnp.mean(x * gdy, axis=-1, keepdims=True)
    dg = jnp.sum(dy * x * r, axis=0, keepdims=True)
    return dx, dg


def _gelu_parts(x):
    x2 = x * x
    t = jnp.tanh(GELU_C * (x + GELU_A * x * x2))
    val = 0.5 * x * (1.0 + t)
    grad = 0.5 * (1.0 + t) + 0.5 * x * (1.0 - t * t) * (GELU_C * (1.0 + 3.0 * GELU_A * x2))
    return val, grad


def _gelu(x):
    return 0.5 * x * (1.0 + jnp.tanh(GELU_C * (x + GELU_A * x * x * x)))


def _sigmoid(z):
    return 1.0 / (1.0 + jnp.exp(-z))


def _cast_shards(b_idx, arrs):
    n = len(arrs)
    steps = 8

    def body(b_ref, *refs):
        for p in range(N_PIECE):
            refs[n][p] = refs[0][:, pl.ds(p * IN_PIECE, IN_PIECE)].astype(BF16)
        for i in range(1, n):
            refs[n + i][...] = refs[i][...].astype(BF16)

    rows = [a.shape[0] // steps for a in arrs]
    in_specs = [pl.BlockSpec((rows[i], a.shape[1]), lambda i, b: (i, 0)) for i, a in enumerate(arrs)]
    out_specs = [pl.BlockSpec((None, N_PIECE, rows[0], IN_PIECE), lambda i, b: (b[0], 0, i, 0))]
    out_specs += [pl.BlockSpec((None, rows[i], a.shape[1]), lambda i, b: (b[0], i, 0)) for i, a in enumerate(arrs) if i > 0]
    out_shape = [jax.ShapeDtypeStruct((N_CHIP, N_PIECE, arrs[0].shape[0], IN_PIECE), BF16)]
    out_shape += [jax.ShapeDtypeStruct((N_CHIP,) + a.shape, BF16) for a in arrs[1:]]
    return pl.pallas_call(
        body, out_shape=out_shape,
        grid_spec=pltpu.PrefetchScalarGridSpec(num_scalar_prefetch=1, grid=(steps,), in_specs=in_specs, out_specs=out_specs),
        compiler_params=_cp(("arbitrary",)), name="cast_shards")(b_idx, *arrs)


def _proj_gather(seq, x, g, win_own, cw8s, more, tm=1024):
    t = x.shape[0]
    ni = t // tm
    nm = len(more)
    steps = N_CHIP * N_PIECE
    near0, far0 = N_PIECE, 3 * N_PIECE

    def body(*refs):
        seq_ref, x_any, g_ref, win_in, cw_in = refs[:5]
        o_ref, hb_any, win_f, cw_out = refs[5 + nm:9 + nm]
        more_out = refs[9 + nm:9 + 2 * nm]
        hbuf, xbuf, wv, cw_s, cw_r, loc = refs[9 + 2 * nm:15 + 2 * nm]
        g_in = _Gather([win_f.at[:, p] for p in range(N_PIECE)], *refs[15 + 2 * nm:19 + 2 * nm])
        g_more = _Gather(more_out, *refs[19 + 2 * nm:23 + 2 * nm])
        s = pl.program_id(0)
        x, y, c, chips = _coords()
        b = 2 * x + y
        blks = [2 * chip[0] + chip[1] for chip in chips]

        def cw_cols(blk):
            return cw_out.at[:, pl.ds(blk * (D // N_CHIP), D // N_CHIP)]

        def cw_copy(k, blk):
            src = cw_in if blk is None else cw_cols(blk)
            return pltpu.make_async_remote_copy(src_ref=src, dst_ref=cw_cols(b if blk is None else blk), send_sem=cw_s.at[k],
                                                recv_sem=cw_r.at[k], device_id=(*chips[k], c), device_id_type=MESH)

        cw_local = pltpu.make_async_copy(cw_in, cw_cols(b), loc.at[1])
        hb_copy = pltpu.make_async_copy(hbuf, hb_any, loc.at[0])

        def load(step):
            slot = lax.rem(step, 2)
            return pltpu.make_async_copy(win_f.at[seq_ref[0, step], seq_ref[1, step]], wv.at[slot], loc.at[2 + slot])

        def chunk(i):
            return pltpu.make_async_copy(x_any.at[pl.ds(i * tm, tm)], xbuf.at[i % 2], loc.at[4 + i % 2])

        def first():
            g_in.start()
            cw_local.start()
            for k in range(3):
                cw_copy(k, None).start()
            load(0).start()
            chunk(0).start()
            for i in range(ni):
                if i + 1 < ni:
                    chunk(i + 1).start()
                chunk(i).wait()
                h, _ = _rms(xbuf[i % 2], g_ref[...])
                hbuf[pl.ds(i * tm, tm), :] = h.astype(BF16)
            hb_copy.start()

        events = {step: [] for step in range(steps)}
        events[0].append(first)
        for p in range(N_PIECE):
            events[2 * p + 2].append(functools.partial(g_in.hop, [p]))
            events[near0 + 2 * p - 1].append(functools.partial(g_in.near_ready, [p]))
            events[far0 + p - 2].append(functools.partial(g_in.far, [p]))
            events[far0 + p - 1].append(functools.partial(g_in.far_ready, [p]))
        events[2 * N_PIECE + 1].append(g_more.start)
        for step, todo in events.items():
            if todo:
                @pl.when(s == step)
                def _(todo=todo):
                    for do in todo:
                        do()

        @pl.when(s + 1 < steps)
        def _():
            load(s + 1).start()

        load(s).wait()
        for i in range(ni):
            rows = pl.ds(i * tm, tm)
            o_ref[rows, :] = jnp.dot(hbuf[rows, :], wv[lax.rem(s, 2)], preferred_element_type=F32).astype(BF16)

        @pl.when(s == steps - 1)
        def _():
            g_more.hop()
            g_more.far()
            for k in range(3):
                cw_copy(k, blks[k]).wait_recv()
            for k in range(3):
                cw_copy(k, None).wait_send()
            cw_local.wait()
            hb_copy.wait()
            g_more.near_ready()
            g_more.far_ready()
            g_in.drain()
            g_more.drain()

    in_specs = [ANY, pl.BlockSpec((1, D), lambda s, q: (0, 0)), ANY, ANY] + [ANY] * nm
    out_specs = [pl.BlockSpec((t, IN_PIECE), lambda s, q: (0, q[2, s])), ANY, ANY, ANY] + [ANY] * nm
    outs = pl.pallas_call(
        body, out_shape=[jax.ShapeDtypeStruct((t, IN_DIM), BF16), jax.ShapeDtypeStruct((t, D), BF16),
                         jax.ShapeDtypeStruct(win_own.shape, BF16), jax.ShapeDtypeStruct((8, D), F32)]
        + [jax.ShapeDtypeStruct(f.shape, f.dtype) for f in more],
        grid_spec=pltpu.PrefetchScalarGridSpec(
            num_scalar_prefetch=1, grid=(steps,), in_specs=in_specs, out_specs=out_specs,
            scratch_shapes=[pltpu.VMEM((t, D), BF16), pltpu.VMEM((2, tm, D), F32), pltpu.VMEM((2, D, IN_PIECE), BF16)]
            + [pltpu.SemaphoreType.DMA((3,))] * 2 + [pltpu.SemaphoreType.DMA((6,))]
            + _gather_sems(N_PIECE) + _gather_sems(nm)),
        input_output_aliases={3: 2, **{5 + w: 4 + w for w in range(nm)}},
        compiler_params=_cp(("arbitrary",), VMEM_LIMIT), name="proj_gather")(seq, x, g, win_own, cw8s, *more)
    return outs[0], outs[1], outs[2], outs[3], outs[4:]


class _Gather:
    def __init__(self, outs, ici_s, ici_r, d2d_s, d2d_r):
        x, y, c, _ = _coords()
        self.outs, self.c = outs, c
        self.sems = ici_s, ici_r, d2d_s, d2d_r
        self.b, self.bx, self.by, self.bd = 2 * x + y, 2 * (1 - x) + y, 2 * x + (1 - y), 2 * (1 - x) + (1 - y)
        self.xn, self.yn, self.sib = (1 - x, y, c), (x, 1 - y, c), (x, y, 1 - c)

    def piece(self, w, blk, hc, quarter=None):
        hr = self.outs[w].shape[1] // 2
        if quarter is None:
            return self.outs[w].at[blk, pl.ds(hc * hr, hr)]
        return self.outs[w].at[blk, pl.ds(hc * hr + quarter * (hr // 2), hr // 2)]

    def ici(self, w, k, ref, to):
        return pltpu.make_async_remote_copy(src_ref=ref, dst_ref=ref, send_sem=self.sems[0].at[w, k],
                                            recv_sem=self.sems[1].at[w, k], device_id=to, device_id_type=MESH)

    def d2d(self, w, k, ref):
        return pltpu.make_async_remote_copy(src_ref=ref, dst_ref=ref, send_sem=self.sems[2].at[w, k],
                                            recv_sem=self.sems[3].at[w, k], device_id=self.sib, device_id_type=MESH)

    def all(self):
        return range(len(self.outs))

    def start(self):
        for w in self.all():
            mine = self.piece(w, self.b, self.c)
            self.ici(w, 0, mine, self.xn).start()
            self.ici(w, 1, mine, self.yn).start()

    def hop(self, ws=None):
        c = self.c
        for w in ws or self.all():
            self.ici(w, 0, self.piece(w, self.bx, c), self.xn).wait_recv()
            self.ici(w, 1, self.piece(w, self.by, c), self.yn).wait_recv()
            self.ici(w, 2, self.piece(w, self.bx, c, 0), self.yn).start()
            self.ici(w, 3, self.piece(w, self.by, c, 1), self.xn).start()
            self.d2d(w, 0, self.piece(w, self.bx, c)).start()
            self.d2d(w, 1, self.piece(w, self.by, c)).start()

    def near_ready(self, ws=None):
        for w in ws or self.all():
            self.d2d(w, 0, self.piece(w, self.bx, 1 - self.c)).wait_recv()
            self.d2d(w, 1, self.piece(w, self.by, 1 - self.c)).wait_recv()

    def far(self, ws=None):
        c = self.c
        for w in ws or self.all():
            self.ici(w, 2, self.piece(w, self.bd, c, 0), self.yn).wait_recv()
            self.ici(w, 3, self.piece(w, self.bd, c, 1), self.xn).wait_recv()
            self.d2d(w, 2, self.piece(w, self.bd, c, 0)).start()
            self.d2d(w, 3, self.piece(w, self.bd, c, 1)).start()

    def far_ready(self, ws=None):
        for w in ws or self.all():
            self.d2d(w, 2, self.piece(w, self.bd, 1 - self.c, 0)).wait_recv()
            self.d2d(w, 3, self.piece(w, self.bd, 1 - self.c, 1)).wait_recv()

    def drain(self):
        c = self.c
        for w in self.all():
            mine = self.piece(w, self.b, c)
            self.ici(w, 0, mine, self.xn).wait_send()
            self.ici(w, 1, mine, self.yn).wait_send()
            self.ici(w, 2, self.piece(w, self.bx, c, 0), self.yn).wait_send()
            self.ici(w, 3, self.piece(w, self.by, c, 1), self.xn).wait_send()
            self.d2d(w, 0, self.piece(w, self.bx, c)).wait_send()
            self.d2d(w, 1, self.piece(w, self.by, c)).wait_send()
            self.d2d(w, 2, self.piece(w, self.bd, c, 0)).wait_send()
            self.d2d(w, 3, self.piece(w, self.bd, c, 1)).wait_send()


def _gather_sems(nw):
    return [pltpu.SemaphoreType.DMA((max(nw, 1), 4))] * 4


def _mixer_fwd(proj, cw8, lng, lnb, wc, bsb, fulls, tm=256):
    t = proj.shape[0]
    nt = t // tm
    nch = tm // CH
    nw = len(fulls)

    def body(*refs):
        p_ref, cw_ref, lng_ref, lnb_ref, wc_ref, bsb_ref = refs[:6]
        mix_ref = refs[6 + nw]
        w_outs = refs[7 + nw:7 + 2 * nw]
        prev_ref = refs[7 + 2 * nw]
        gather = _Gather(w_outs, *refs[8 + 2 * nw:])

        @pl.when(pl.program_id(0) == 0)
        def _():
            gather.start()
            prev_ref[...] = jnp.zeros_like(prev_ref)

        @pl.when(pl.program_id(0) == nt // 2)
        def _():
            gather.hop()

        @pl.when(pl.program_id(0) == nt - 1)
        def _():
            gather.far()

        rows = lax.broadcasted_iota(jnp.int32, (tm, CH), 0)
        for s in range(HEADS):
            cs = pl.ds(CH * s, CH)

            def slab(k):
                return p_ref[:, pl.ds(k * SLAB + CH * s, CH)].astype(F32)

            gb, gc, xa, za = slab(0), slab(1), slab(2), slab(3)
            cx = gc * xa
            p6 = jnp.broadcast_to(prev_ref[6:7, cs], (tm, CH))
            p7 = jnp.broadcast_to(prev_ref[7:8, cs], (tm, CH))
            c1 = jnp.where(rows == 0, p7, pltpu.roll(cx, 1, 0))
            c2 = jnp.where(rows == 0, p6, jnp.where(rows == 1, p7, pltpu.roll(cx, 2, 0)))
            prev_ref[:, cs] = cx[tm - 8:, :]
            cv = cw_ref[0:1, cs] * c2 + cw_ref[1:2, cs] * c1 + cw_ref[2:3, cs] * cx
            mix_ref[:, cs] = (gb * cv * (za * _sigmoid(za))).astype(BF16)

            u, v, zb = slab(4), slab(5), slab(6)
            ug, vg = _gelu(u), _gelu(v)
            dlt = vg - jnp.mean(vg, axis=-1, keepdims=True)
            vhat = dlt * lax.rsqrt(jnp.mean(dlt * dlt, axis=-1, keepdims=True) + EPS)
            vn = (vhat * lng_ref[:, cs] + lnb_ref[:, cs]).astype(BF16)
            gate = ug * (zb * _sigmoid(zb))
            for c in range(nch):
                rs = slice(CH * c, CH * (c + 1))
                sp = jnp.dot(wc_ref[s], vn[rs], preferred_element_type=F32) + bsb_ref[s]
                mix_ref[rs, pl.ds(SLAB + CH * s, CH)] = (gate[rs] * sp).astype(BF16)

        @pl.when(pl.program_id(0) == nt - 1)
        def _():
            gather.near_ready()
            gather.far_ready()
            gather.drain()

    sems = _gather_sems(nw)
    outs = pl.pallas_call(
        body, grid=(nt,),
        in_specs=[pl.BlockSpec((tm, IN_DIM), lambda i: (i, 0)), _full((8, D)), _full((1, D)), _full((1, D)),
                  _full((HEADS, CH, CH)), _full((HEADS, CH, CH))] + [ANY] * nw,
        out_specs=[pl.BlockSpec((tm, MIX), lambda i: (i, 0))] + [ANY] * nw,
        out_shape=[jax.ShapeDtypeStruct((t, MIX), BF16)] + [jax.ShapeDtypeStruct(f.shape, f.dtype) for f in fulls],
        input_output_aliases={6 + w: 1 + w for w in range(nw)},
        scratch_shapes=[pltpu.VMEM((8, D), F32)] + sems,
        compiler_params=_cp(("arbitrary",), VMEM_LIMIT), name="mixer_fwd")(proj, cw8, lng, lnb, wc, bsb, *fulls)
    return outs[0], outs[1:]


def _mem_fwd(mem, gm, wkv_f):
    n_mem = mem.shape[0]

    def body(mem_ref, gm_ref, w_ref, k_ref, v_ref):
        m, _ = _rms(mem_ref[...], gm_ref[...])
        mb = m.astype(BF16)
        for j in range(N_CHIP):
            dst = k_ref if j < 2 else v_ref
            dst[:, pl.ds(KV_BLK * (j % 2), KV_BLK)] = jnp.dot(mb, w_ref[j], preferred_element_type=F32).astype(BF16)

    return pl.pallas_call(
        body, out_shape=[jax.ShapeDtypeStruct((n_mem, D), BF16), jax.ShapeDtypeStruct((n_mem, D), BF16)],
        compiler_params=_cp(None, VMEM_LIMIT), name="mem_fwd")(mem, gm, wkv_f)


def _tail(x, tgt, mixin, wout, wq, wxo, k, v, g2, g3, tm=512, sub=512):
    t = x.shape[0]
    n_mem = k.shape[0]
    scale = 1.0 / math.sqrt(XD)

    def body(x_ref, tgt_ref, mix_ref, wout_ref, wq_ref, wxo_ref, k_ref, v_ref, g2_ref, g3_ref,
             loss_ref, dmix_ref, dx1b_ref, h2_ref, dq_ref, o_ref, dx2b_ref, dk_ref, dv_ref, dg2_ref, dg3_ref):
        @pl.when(pl.program_id(0) == 0)
        def _():
            loss_ref[...] = jnp.zeros_like(loss_ref)
            dk_ref[...] = jnp.zeros_like(dk_ref)
            dv_ref[...] = jnp.zeros_like(dv_ref)
            dg2_ref[...] = jnp.zeros_like(dg2_ref)
            dg3_ref[...] = jnp.zeros_like(dg3_ref)

        g2, g3 = g2_ref[...], g3_ref[...]
        for sb in range(tm // sub):
            rs = pl.ds(sub * sb, sub)
            x1 = x_ref[rs, :] + jnp.dot(mix_ref[rs, :], wout_ref[...], preferred_element_type=F32)
            h2, r2 = _rms(x1, g2)
            h2b = h2.astype(BF16)
            h2_ref[rs, :] = h2b
            q = jnp.dot(h2b, wq_ref[...], preferred_element_type=F32).astype(BF16)
            probs, outs = [], []
            for hd in range(XH):
                hs = pl.ds(XD * hd, XD)
                s = _bdot_nt(q[:, XD * hd:XD * (hd + 1)], k_ref[:, hs]) * scale
                e = jnp.exp(s - jnp.max(s, axis=-1, keepdims=True))
                p = e / jnp.sum(e, axis=-1, keepdims=True)
                probs.append(p)
                outs.append(_bdot(p, v_ref[:, hs]))
            ob = jnp.concatenate(outs, axis=-1).astype(BF16)
            o_ref[rs, :] = ob
            x2 = x1 + jnp.dot(ob, wxo_ref[...], preferred_element_type=F32)
            y, r3 = _rms(x2, g3)
            diff = y - tgt_ref[rs, :]
            row_loss = jnp.sum(diff * diff, axis=-1, keepdims=True)
            loss_ref[...] += jnp.broadcast_to(jnp.sum(row_loss, axis=0, keepdims=True) * (0.5 / D), loss_ref.shape)

            dx2, dg3 = _rms_bwd(diff * (1.0 / D), x2, r3, g3)
            dg3_ref[...] += dg3
            dx2b = dx2.astype(BF16)
            dx2b_ref[rs, :] = dx2b
            do = _bdot_nt(dx2b, wxo_ref[...])
            dqs = []
            for hd in range(XH):
                hs = pl.ds(XD * hd, XD)
                p = probs[hd]
                do_h = do[:, XD * hd:XD * (hd + 1)]
                dv_ref[:, hs] += _bdot_tn(p, do_h)
                dp = _bdot_nt(do_h, v_ref[:, hs])
                ds = p * (dp - jnp.sum(dp * p, axis=-1, keepdims=True))
                dqs.append(_bdot(ds, k_ref[:, hs]) * scale)
                dk_ref[:, hs] += _bdot_tn(ds, q[:, XD * hd:XD * (hd + 1)]) * scale
            dq = jnp.concatenate(dqs, axis=-1).astype(BF16)
            dq_ref[rs, :] = dq
            dx1n, dg2 = _rms_bwd(_bdot_nt(dq, wq_ref[...]), x1, r2, g2)
            dg2_ref[...] += dg2
            dx1b = (dx2 + dx1n).astype(BF16)
            dx1b_ref[rs, :] = dx1b
            dmix_ref[rs, :] = _bdot_nt(dx1b, wout_ref[...]).astype(BF16)

    tok = lambda w: pl.BlockSpec((tm, w), lambda i: (i, 0))
    return pl.pallas_call(
        body, grid=(t // tm,),
        in_specs=[tok(D), tok(D), tok(MIX), _full((MIX, D), 1), _full((D, D), 1), _full((D, D), 1),
                  _full((n_mem, D), 1), _full((n_mem, D), 1), _full((1, D)), _full((1, D))],
        out_specs=[_full((8, 128)), tok(MIX), tok(D), tok(D), tok(D), tok(D), tok(D),
                   _full((n_mem, D)), _full((n_mem, D)), _full((1, D)), _full((1, D))],
        out_shape=[jax.ShapeDtypeStruct((8, 128), F32), jax.ShapeDtypeStruct((t, MIX), BF16),
                   jax.ShapeDtypeStruct((t, D), BF16),
                   jax.ShapeDtypeStruct((t, D), BF16), jax.ShapeDtypeStruct((t, D), BF16),
                   jax.ShapeDtypeStruct((t, D), BF16), jax.ShapeDtypeStruct((t, D), BF16),
                   jax.ShapeDtypeStruct((n_mem, D), F32), jax.ShapeDtypeStruct((n_mem, D), F32),
                   jax.ShapeDtypeStruct((1, D), F32), jax.ShapeDtypeStruct((1, D), F32)],
        compiler_params=_cp(("arbitrary",), VMEM_LIMIT), name="tail")(x, tgt, mixin, wout, wq, wxo, k, v, g2, g3)


def _mem_bwd(mem, gm, dk, dv, wkv_f):
    def body(mem_ref, gm_ref, dk_ref, dv_ref, w_ref, dw_ref, dwb_ref, dgm_ref):
        mem_v = mem_ref[...]
        m, rm = _rms(mem_v, gm_ref[...])
        mb = m.astype(BF16)
        dm = jnp.zeros_like(mem_v)
        for j in range(N_CHIP):
            src = dk_ref if j < 2 else dv_ref
            dkv = src[:, pl.ds(KV_BLK * (j % 2), KV_BLK)].astype(BF16)
            dw = _bdot_tn(mb, dkv)
            dw_ref[j] = dw
            dwb_ref[j] = dw.astype(BF16)
            dm = dm + _bdot_nt(dkv, w_ref[j])
        dgm_ref[...] = jnp.sum(dm * mem_v * rm, axis=0, keepdims=True)

    return pl.pallas_call(
        body, out_shape=[jax.ShapeDtypeStruct((N_CHIP, D, KV_BLK), F32), jax.ShapeDtypeStruct((N_CHIP, D, KV_BLK), BF16),
                         jax.ShapeDtypeStruct((1, D), F32)],
        compiler_params=_cp(None, VMEM_LIMIT), name="mem_bwd")(mem, gm, dk, dv, wkv_f)


def _mixer_bwd(proj, dmix, cw8, lng, lnb, wc, wct, bsb, win_f, x, dx1, g1, tm=256):
    t = proj.shape[0]
    nt = t // tm
    nch = tm // CH
    hb = 16
    pair = 2 * CH
    assert pair == IN_PIECE

    def body(p_ref, pgc_ref, pxa_ref, dm_ref, cw_ref, lng_ref, lnb_ref, wc_ref, wct_ref, bsb_ref, w_ref, x_ref,
             dx1_ref, g1_ref, dp_ref, dcw_ref, dlng_ref, dlnb_ref, dwc_ref, dbs_ref, gx_ref, dg1_ref,
             next_ref, dh_ref):
        i = pl.program_id(0)

        @pl.when(i == 0)
        def _():
            next_ref[...] = jnp.zeros_like(next_ref)
            dcw_ref[...] = jnp.zeros_like(dcw_ref)
            dlng_ref[...] = jnp.zeros_like(dlng_ref)
            dlnb_ref[...] = jnp.zeros_like(dlnb_ref)
            dwc_ref[...] = jnp.zeros_like(dwc_ref)
            dbs_ref[...] = jnp.zeros_like(dbs_ref)
            dg1_ref[...] = jnp.zeros_like(dg1_ref)

        first_tile = i == nt - 1
        rows = lax.broadcasted_iota(jnp.int32, (tm, CH), 0)
        ones8 = jnp.ones((8, CH), BF16)
        for s in range(HEADS):
            cs = pl.ds(CH * s, CH)

            def slab(k):
                return p_ref[:, pl.ds(k * SLAB + CH * s, CH)].astype(F32)

            gb, gc, xa, za = slab(0), slab(1), slab(2), slab(3)
            da = dm_ref[:, cs].astype(F32)
            cx = gc * xa
            cxp = pgc_ref[:, cs].astype(F32) * pxa_ref[:, cs].astype(F32)
            cxp = jnp.where(first_tile, jnp.zeros_like(cxp), cxp)
            p6 = jnp.broadcast_to(cxp[hb - 2:hb - 1, :], (tm, CH))
            p7 = jnp.broadcast_to(cxp[hb - 1:hb, :], (tm, CH))
            c1 = jnp.where(rows == 0, p7, pltpu.roll(cx, 1, 0))
            c2 = jnp.where(rows == 0, p6, jnp.where(rows == 1, p7, pltpu.roll(cx, 2, 0)))
            w0, w1, w2 = cw_ref[0:1, cs], cw_ref[1:2, cs], cw_ref[2:3, cs]
            cv = w0 * c2 + w1 * c1 + w2 * cx
            sg = _sigmoid(za)
            sa = za * sg
            dcv = da * gb * sa
            dp_ref[:, pl.ds(0 * SLAB + CH * s, CH)] = (da * cv * sa).astype(BF16)
            dp_ref[:, pl.ds(3 * SLAB + CH * s, CH)] = (da * gb * cv * (sg * (1.0 + za * (1.0 - sg)))).astype(BF16)
            n0 = jnp.broadcast_to(next_ref[0:1, cs], (tm, CH))
            n1 = jnp.broadcast_to(next_ref[1:2, cs], (tm, CH))
            u1 = jnp.where(rows == tm - 1, n0, pltpu.roll(dcv, tm - 1, 0))
            u2 = jnp.where(rows == tm - 2, n0, jnp.where(rows == tm - 1, n1, pltpu.roll(dcv, tm - 2, 0)))
            next_ref[:, cs] = dcv[0:8, :]
            dcx = w2 * dcv + w1 * u1 + w0 * u2
            dp_ref[:, pl.ds(1 * SLAB + CH * s, CH)] = (dcx * xa).astype(BF16)
            dp_ref[:, pl.ds(2 * SLAB + CH * s, CH)] = (dcx * gc).astype(BF16)
            dcw_ref[0:1, cs] += jnp.sum(dcv * c2, axis=0, keepdims=True)
            dcw_ref[1:2, cs] += jnp.sum(dcv * c1, axis=0, keepdims=True)
            dcw_ref[2:3, cs] += jnp.sum(dcv * cx, axis=0, keepdims=True)

            u, v, zb = slab(4), slab(5), slab(6)
            db = dm_ref[:, pl.ds(SLAB + CH * s, CH)].astype(F32)
            ug, ugrad = _gelu_parts(u)
            vg, vgrad = _gelu_parts(v)
            dlt = vg - jnp.mean(vg, axis=-1, keepdims=True)
            rstd = lax.rsqrt(jnp.mean(dlt * dlt, axis=-1, keepdims=True) + EPS)
            vhat = dlt * rstd
            lg = lng_ref[:, cs]
            vn = (vhat * lg + lnb_ref[:, cs]).astype(BF16)
            sgb = _sigmoid(zb)
            szb = zb * sgb
            sps, dvns = [], []
            dbs = jnp.zeros((8, CH), F32)
            dwc = jnp.zeros((CH, CH), F32)
            for c in range(nch):
                rs = slice(CH * c, CH * (c + 1))
                sp = jnp.dot(wc_ref[s], vn[rs], preferred_element_type=F32) + bsb_ref[s]
                dsp = (db[rs] * ug[rs] * szb[rs]).astype(BF16)
                dbs = dbs + lax.dot_general(ones8, dsp, (((1,), (1,)), ((), ())), preferred_element_type=F32)
                dwc = dwc + lax.dot_general(dsp, vn[rs], (((1,), (1,)), ((), ())), preferred_element_type=F32)
                dvns.append(jnp.dot(wct_ref[s], dsp, preferred_element_type=F32))
                sps.append(sp)
            sp = jnp.concatenate(sps, axis=0)
            dvn = jnp.concatenate(dvns, axis=0)
            dbs_ref[:, cs] += dbs
            dwc_ref[s] += dwc
            dlng_ref[:, cs] += jnp.sum(dvn * vhat, axis=0, keepdims=True)
            dlnb_ref[:, cs] += jnp.sum(dvn, axis=0, keepdims=True)
            dvhat = dvn * lg
            dvg = rstd * (dvhat - jnp.mean(dvhat, axis=-1, keepdims=True)
                          - vhat * jnp.mean(dvhat * vhat, axis=-1, keepdims=True))
            dp_ref[:, pl.ds(4 * SLAB + CH * s, CH)] = (db * sp * szb * ugrad).astype(BF16)
            dp_ref[:, pl.ds(5 * SLAB + CH * s, CH)] = (dvg * vgrad).astype(BF16)
            dp_ref[:, pl.ds(6 * SLAB + CH * s, CH)] = (db * ug * sp * (sgb * (1.0 + zb * (1.0 - sgb)))).astype(BF16)

            if s % 2 == 1:
                part = None
                for k in range(N_SLAB):
                    col = k * SLAB + pair * (s // 2)
                    blk, off = divmod(col, IN_BLK)
                    term = lax.dot_general(dp_ref[:, pl.ds(col, pair)], w_ref[blk, off // IN_PIECE],
                                           (((1,), (1,)), ((), ())), preferred_element_type=F32)
                    part = term if part is None else part + term
                if s == 1:
                    dh_ref[...] = part
                else:
                    dh_ref[...] += part

        xv = x_ref[...]
        r = lax.rsqrt(jnp.mean(xv * xv, axis=-1, keepdims=True) + EPS)
        dxn, dg = _rms_bwd(dh_ref[...], xv, r, g1_ref[...])
        gx_ref[...] = dx1_ref[...].astype(F32) + dxn
        dg1_ref[0:1, :] += dg

        @pl.when(i == nt - 1)
        def _():
            tril = lax.broadcasted_iota(jnp.int32, (CH, CH), 0) >= lax.broadcasted_iota(jnp.int32, (CH, CH), 1)
            for s in range(HEADS):
                dwc_ref[s] = jnp.where(tril, dwc_ref[s], 0.0)

    rev = lambda i: nt - 1 - i
    halo = lambda col: pl.BlockSpec((hb, SLAB), lambda i: (jnp.maximum(rev(i) * (tm // hb) - 1, 0), col))
    tok = lambda w: pl.BlockSpec((tm, w), lambda i: (rev(i), 0))
    return pl.pallas_call(
        body, grid=(nt,),
        in_specs=[tok(IN_DIM), halo(1), halo(2), tok(MIX), _full((8, D)), _full((1, D)), _full((1, D)),
                  _full((HEADS, CH, CH)), _full((HEADS, CH, CH)), _full((HEADS, CH, CH)),
                  _full((N_CHIP, N_PIECE, D, IN_PIECE), 1), tok(D), tok(D), _full((1, D))],
        out_specs=[tok(IN_DIM), _full((8, D)), _full((1, D)), _full((1, D)), _full((HEADS, CH, CH)), _full((8, D)),
                   tok(D), _full((8, D))],
        out_shape=[jax.ShapeDtypeStruct((t, IN_DIM), BF16), jax.ShapeDtypeStruct((8, D), F32),
                   jax.ShapeDtypeStruct((1, D), F32), jax.ShapeDtypeStruct((1, D), F32),
                   jax.ShapeDtypeStruct((HEADS, CH, CH), F32), jax.ShapeDtypeStruct((8, D), F32),
                   jax.ShapeDtypeStruct((t, D), F32), jax.ShapeDtypeStruct((8, D), F32)],
        scratch_shapes=[pltpu.VMEM((8, D), F32), pltpu.VMEM((tm, D), F32)],
        compiler_params=_cp(("arbitrary",), VMEM_LIMIT), name="mixer_bwd")(
            proj, proj, proj, dmix, cw8, lng, lnb, wc, wct, bsb, win_f, x, dx1, g1)


def _grad_matmul(a, b, after, *, by_cols, name, tk=1024):
    t, m = a.shape
    n = b.shape[1]
    nk = t // tk
    nj = N_CHIP if by_cols else 1
    bn = n // nj

    def body(a_ref, b_ref, after_ref, o_ref, ob_ref):
        kk = pl.program_id(1)
        part = lax.dot_general(a_ref[...], b_ref[...], (((0,), (0,)), ((), ())), preferred_element_type=F32)

        @pl.when(kk == 0)
        def _():
            o_ref[...] = part

        @pl.when(kk > 0)
        def _():
            o_ref[...] += part

        @pl.when(kk == nk - 1)
        def _():
            ob_ref[...] = o_ref[...].astype(BF16)

    a_spec = pl.BlockSpec((tk, m), lambda j, k: (k, 0))
    b_spec = pl.BlockSpec((tk, bn), lambda j, k: (k, j))
    o_spec = pl.BlockSpec((None, m, bn), lambda j, k: (j, 0, 0))
    o32, o16 = pl.pallas_call(
        body, grid=(nj, nk), in_specs=[a_spec, b_spec, ANY], out_specs=[o_spec, o_spec],
        out_shape=[jax.ShapeDtypeStruct((nj, m, bn), F32), jax.ShapeDtypeStruct((nj, m, bn), BF16)],
        compiler_params=_cp(("parallel", "arbitrary"), VMEM_LIMIT), name=name)(a, b, after)
    if by_cols:
        return o32, o16
    return o32.reshape(N_CHIP, m // N_CHIP, n), o16.reshape(N_CHIP, m // N_CHIP, n)


def _coords():
    x, y, c = lax.axis_index("x"), lax.axis_index("y"), lax.axis_index("c")
    chips = [(1 - x, y), (x, 1 - y), (1 - x, 1 - y)]
    return x, y, c, chips


def _pair_reduce(c_idx, grads, grads_b, smalls, name):
    ng, ns = len(grads), len(smalls)
    halves = [g.shape[1] // 2 for g in grads]

    def body(c_ref, *refs):
        g_in, gb_any = refs[:ng], refs[ng:2 * ng]
        s_own, s_any = refs[2 * ng:2 * ng + ns], refs[2 * ng + ns:2 * ng + 2 * ns]
        o = refs[2 * ng + 2 * ns:4 * ng + 3 * ns]
        lands = refs[4 * ng + 3 * ns:5 * ng + 4 * ns]
        send, recv = refs[5 * ng + 4 * ns:]
        x, y, c, _ = _coords()
        j = pl.program_id(0)

        def big(i, blk):
            return pltpu.make_async_remote_copy(
                src_ref=gb_any[i].at[blk, pl.ds((1 - c) * halves[i], halves[i])], dst_ref=lands[i].at[blk],
                send_sem=send.at[i, blk], recv_sem=recv.at[i, blk], device_id=(x, y, 1 - c), device_id_type=MESH)

        def small(i):
            return pltpu.make_async_remote_copy(
                src_ref=s_any[i].at[1 - c], dst_ref=lands[ng + i],
                send_sem=send.at[ng + i, 0], recv_sem=recv.at[ng + i, 0], device_id=(x, y, 1 - c), device_id_type=MESH)

        @pl.when(j == 0)
        def _():
            for blk in range(N_CHIP):
                for i in range(ng):
                    big(i, blk).start()
            for i in range(ns):
                small(i).start()

        for i in range(ng):
            big(i, j).wait_recv()
            tot = g_in[i][...] + lands[i][j].astype(F32)
            o[i][...] = tot
            o[ng + i][...] = tot.astype(BF16)

        @pl.when(j == N_CHIP - 1)
        def _():
            for i in range(ns):
                small(i).wait_recv()
                o[2 * ng + i][...] = s_own[i][...] + lands[ng + i][...]
                small(i).wait_send()
            for blk in range(N_CHIP):
                for i in range(ng):
                    big(i, blk).wait_send()

    in_specs = [pl.BlockSpec((None, None, halves[i], g.shape[2]), lambda b, c: (b, c[0], 0, 0)) for i, g in enumerate(grads)]
    in_specs += [ANY] * ng
    in_specs += [pl.BlockSpec((None, s.shape[0] // 2, s.shape[1]), lambda b, c: (c[0], 0, 0)) for s in smalls]
    in_specs += [ANY] * ns
    blk = [pl.BlockSpec((None, halves[i], g.shape[2]), lambda b, c: (b, 0, 0)) for i, g in enumerate(grads)]
    out_specs = blk + blk + [pl.BlockSpec((s.shape[0] // 2, s.shape[1]), lambda b, c: (0, 0)) for s in smalls]
    out_shape = [jax.ShapeDtypeStruct((N_CHIP, halves[i], g.shape[2]), F32) for i, g in enumerate(grads)]
    out_shape += [jax.ShapeDtypeStruct((N_CHIP, halves[i], g.shape[2]), BF16) for i, g in enumerate(grads)]
    out_shape += [jax.ShapeDtypeStruct((s.shape[0] // 2, s.shape[1]), F32) for s in smalls]
    scratch = [pltpu.VMEM((N_CHIP, halves[i], g.shape[2]), BF16) for i, g in enumerate(grads)]
    scratch += [pltpu.VMEM((s.shape[0] // 2, s.shape[1]), F32) for s in smalls]
    scratch += [pltpu.SemaphoreType.DMA((ng + ns, N_CHIP)), pltpu.SemaphoreType.DMA((ng + ns, N_CHIP))]
    grads4 = [g.reshape(N_CHIP, 2, halves[i], g.shape[2]) for i, g in enumerate(grads)]
    smalls3 = [s.reshape(2, s.shape[0] // 2, s.shape[1]) for s in smalls]
    return pl.pallas_call(
        body, out_shape=out_shape,
        grid_spec=pltpu.PrefetchScalarGridSpec(num_scalar_prefetch=1, grid=(N_CHIP,), in_specs=in_specs,
                                               out_specs=out_specs, scratch_shapes=scratch),
        compiler_params=_cp(("arbitrary",), VMEM_LIMIT), name=name)(c_idx, *grads4, *grads_b, *smalls3, *smalls3)


def _grad_matmul_pair(c_idx, a, b, smalls, after, *, name, tk=2048):
    t, m = a.shape
    bn = b.shape[1] // N_CHIP
    nk = t // tk
    hr = m // 2
    ns = len(smalls)

    def body(c_ref, a_ref, b_ref, *refs):
        s_own, s_any = refs[:ns], refs[ns:2 * ns]
        o32, o16 = refs[2 * ns + 1], refs[2 * ns + 2]
        o_small = refs[2 * ns + 3:3 * ns + 3]
        acc, tb, land, st16 = refs[3 * ns + 3:3 * ns + 7]
        s_land, s_stage = refs[3 * ns + 7:4 * ns + 7], refs[4 * ns + 7:5 * ns + 7]
        send, recv, loc = refs[5 * ns + 7:]
        x, y, c, _ = _coords()
        sibling = dict(device_id=(x, y, 1 - c), device_id_type=MESH)
        j, kk = pl.program_id(0), pl.program_id(1)
        mine = pl.ds(pl.multiple_of(c * hr, hr), hr)
        theirs = pl.ds(pl.multiple_of((1 - c) * hr, hr), hr)

        def to_sibling(blk):
            return pltpu.make_async_remote_copy(src_ref=tb, dst_ref=land.at[blk], send_sem=send.at[blk],
                                                recv_sem=recv.at[blk], **sibling)

        def small(i):
            return pltpu.make_async_remote_copy(src_ref=s_any[i].at[1 - c], dst_ref=s_land[i], send_sem=send.at[N_CHIP + i],
                                                recv_sem=recv.at[N_CHIP + i], **sibling)

        def written(blk):
            return (pltpu.make_async_copy(acc.at[blk % 2, mine], o32.at[blk], loc.at[0]),
                    pltpu.make_async_copy(st16, o16.at[blk], loc.at[1]))

        def finish(blk):
            to_sibling(blk).wait_recv()

            @pl.when(blk > 0)
            def _():
                for cp in written(blk - 1):
                    cp.wait()

            tot = acc[blk % 2, mine, :] + land[blk].astype(F32)
            acc[blk % 2, mine, :] = tot
            st16[...] = tot.astype(BF16)
            for cp in written(blk):
                cp.start()

        def small_out(i):
            return pltpu.make_async_copy(s_stage[i], o_small[i], loc.at[2 + i])

        @pl.when((j == 0) & (kk == 0))
        def _():
            for i in range(ns):
                small(i).start()

        @pl.when((j == 1) & (kk == 0))
        def _():
            for i in range(ns):
                small(i).wait_recv()
                s_stage[i][...] = s_own[i][...] + s_land[i][...]
                small_out(i).start()

        @pl.when((j > 0) & (kk == 0))
        def _():
            finish(j - 1)

        part = lax.dot_general(a_ref[...], b_ref[...], (((0,), (0,)), ((), ())), preferred_element_type=F32)
        slot = lax.rem(j, 2)

        @pl.when(kk == 0)
        def _():
            acc[slot] = part

        @pl.when(kk > 0)
        def _():
            acc[slot] += part

        @pl.when(kk == nk - 1)
        def _():
            @pl.when(j > 0)
            def _():
                to_sibling(j - 1).wait_send()

            tb[...] = acc[slot, theirs, :].astype(BF16)
            to_sibling(j).start()

        @pl.when((j == N_CHIP - 1) & (kk == nk - 1))
        def _():
            finish(j)
            for i in range(ns):
                small_out(i).wait()
                small(i).wait_send()
            for cp in written(j):
                cp.wait()
            to_sibling(j).wait_send()

    halves = [(s.shape[0] // 2, s.shape[1]) for s in smalls]
    in_specs = [pl.BlockSpec((tk, m), lambda j, k, c: (k, 0)), pl.BlockSpec((tk, bn), lambda j, k, c: (k, j))]
    in_specs += [pl.BlockSpec((None,) + h, lambda j, k, c: (c[0], 0, 0)) for h in halves] + [ANY] * ns + [ANY]
    out_shape = [jax.ShapeDtypeStruct((N_CHIP, hr, bn), F32), jax.ShapeDtypeStruct((N_CHIP, hr, bn), BF16)]
    out_shape += [jax.ShapeDtypeStruct(h, F32) for h in halves]
    scratch = [pltpu.VMEM((2, m, bn), F32), pltpu.VMEM((hr, bn), BF16),
               pltpu.VMEM((N_CHIP, hr, bn), BF16), pltpu.VMEM((hr, bn), BF16)]
    scratch += [pltpu.VMEM(h, F32) for h in halves] * 2
    scratch += [pltpu.SemaphoreType.DMA((N_CHIP + ns,)), pltpu.SemaphoreType.DMA((N_CHIP + ns,)),
                pltpu.SemaphoreType.DMA((2 + ns,))]
    smalls3 = [s.reshape((2,) + h) for s, h in zip(smalls, halves)]
    outs = pl.pallas_call(
        body, out_shape=out_shape,
        grid_spec=pltpu.PrefetchScalarGridSpec(num_scalar_prefetch=1, grid=(N_CHIP, nk), in_specs=in_specs,
                                               out_specs=[ANY] * (2 + ns), scratch_shapes=scratch),
        compiler_params=_cp(("arbitrary", "arbitrary"), VMEM_LIMIT), name=name)(c_idx, a, b, *smalls3, *smalls3, after)
    return outs[0], outs[1], list(outs[2:])


_HBM = pl.BlockSpec(memory_space=pltpu.HBM)
_SEM = pl.BlockSpec(memory_space=pltpu.SEMAPHORE)


def _split_copies(ins, lands, ng, send, recv, arriving):
    x, y, c, chips = _coords()
    b = 2 * x + y
    copies = []
    for i in range(len(ins)):
        for k in range(3):
            blk = 2 * chips[k][0] + chips[k][1]
            src, dst, got = (ins[i].at[blk], lands[i].at[k], lands[i].at[k]) if i < ng else (ins[i], lands[i].at[b], lands[i].at[blk])
            sems = dict(send_sem=send.at[3 * i + k], recv_sem=recv.at[3 * i + k], device_id=(*chips[k], c), device_id_type=MESH)
            if arriving:
                copies.append(pltpu.make_async_remote_copy(src_ref=got, dst_ref=got, **sems))
            else:
                copies.append(pltpu.make_async_remote_copy(src_ref=src, dst_ref=dst, **sems))
    return copies


def _exchange_begin(sums_b, smalls, name):
    ng, n = len(sums_b), len(sums_b) + len(smalls)
    srcs = list(sums_b) + list(smalls)
    lands = [lax.empty((3,) + g.shape[1:], g.dtype) for g in sums_b] + [lax.empty((N_CHIP,) + s.shape, s.dtype) for s in smalls]

    def body(*refs):
        ins, land_refs = refs[:n], refs[n:2 * n]
        send, recv = refs[2 * n], refs[2 * n + 1]
        token = refs[4 * n + 2]
        for cp in _split_copies(ins, land_refs, ng, send, recv, False):
            cp.start()
        token[...] = jnp.zeros_like(token)

    hbm = lambda a: pltpu.HBM(a.shape, a.dtype)
    outs = pl.pallas_call(
        body, name=name,
        out_shape=(pltpu.SemaphoreType.DMA((3 * n,)), pltpu.SemaphoreType.DMA((3 * n,)), *[hbm(a) for a in srcs + lands],
                   jax.ShapeDtypeStruct((8, 128), F32)),
        in_specs=[_HBM] * (2 * n), out_specs=(_SEM, _SEM, *[_HBM] * (2 * n), pl.BlockSpec(memory_space=pltpu.VMEM)),
        input_output_aliases={i: 2 + i for i in range(2 * n)},
        compiler_params=pltpu.CompilerParams(has_side_effects=pltpu.SideEffectType.DATAFLOW_SIDE_EFFECTING),
    )(*[pltpu.with_memory_space_constraint(a, pltpu.HBM) for a in srcs + lands])
    return outs[0], outs[1], list(outs[2:2 + n]), list(outs[2 + n:2 + 2 * n]), outs[2 + 2 * n]


def _exchange_end(send, recv, srcs, lands, ng, after, name):
    n = len(srcs)
    after = list(after)

    def body(*refs):
        ins, land_refs = refs[:n], refs[n:2 * n]
        send_ref, recv_ref = refs[2 * n], refs[2 * n + 1]
        for cp in _split_copies(ins, land_refs, ng, send_ref, recv_ref, False):
            cp.wait_send()
        for cp in _split_copies(ins, land_refs, ng, send_ref, recv_ref, True):
            cp.wait_recv()

    hbm = lambda a: pltpu.HBM(a.shape, a.dtype)
    outs = pl.pallas_call(
        body, name=name, out_shape=tuple(hbm(a) for a in list(srcs) + list(lands)),
        in_specs=[_HBM] * (2 * n) + [_SEM, _SEM] + [ANY] * len(after), out_specs=tuple([_HBM] * (2 * n)),
        input_output_aliases={i: i for i in range(2 * n)},
        compiler_params=pltpu.CompilerParams(has_side_effects=pltpu.SideEffectType.DATAFLOW_SIDE_EFFECTING),
    )(*srcs, *lands, send, recv, *after)
    return list(outs[n:])


def _chip_reduce(bc_idx, sums, recvd, smalls_slots, smalls_own, name, steps=4):
    ng, ns = len(sums), len(smalls_slots)
    n = ng + ns
    assert steps >= 2
    halves = [g.shape[1] for g in sums] + [s.shape[1] for s in smalls_slots]
    rows = [g.shape[1] // steps for g in sums]

    def body(bc_ref, *refs):
        own, rx = refs[:ng], refs[ng:2 * ng]
        sl = refs[2 * ng:2 * ng + ns]
        sl_own = refs[2 * ng + ns:2 * ng + 2 * ns]
        o = refs[2 * ng + 2 * ns:2 * ng + 2 * ns + n]
        tiles = refs[2 * ng + 2 * ns + n:2 * ng + 2 * ns + 2 * n]
        keep, send, recv = refs[2 * ng + 2 * ns + 2 * n:]
        x, y, c, _ = _coords()
        sibling = dict(device_id=(x, y, 1 - c), device_id_type=MESH)
        r = pl.program_id(0)

        def writes(i, step, slot):
            dst = o[i].at[pl.ds(c * halves[i] + step * rows[i], rows[i])]
            return (pltpu.make_async_copy(tiles[i].at[slot], dst, keep.at[i, slot]),
                    pltpu.make_async_remote_copy(src_ref=tiles[i].at[slot], dst_ref=dst, send_sem=send.at[i, slot],
                                                 recv_sem=recv.at[i, step], **sibling))

        def small_writes(i):
            dst = o[i].at[pl.ds(c * halves[i], halves[i])]
            return (pltpu.make_async_copy(tiles[i], dst, keep.at[i, 0]),
                    pltpu.make_async_remote_copy(src_ref=tiles[i], dst_ref=dst, send_sem=send.at[i, 0],
                                                 recv_sem=recv.at[i, 0], **sibling))

        def arriving(i, step, nrows):
            dst = o[i].at[pl.ds((1 - c) * halves[i] + step * nrows, nrows)]
            return pltpu.make_async_remote_copy(src_ref=dst, dst_ref=dst, send_sem=send.at[i, 0], recv_sem=recv.at[i, step],
                                                **sibling)

        def finish(step, slot):
            for i in range(ng):
                local, remote = writes(i, step, slot)
                local.wait()
                remote.wait_send()

        @pl.when(r >= 2)
        def _():
            finish(r - 2, r % 2)

        for i in range(ng):
            tot = own[i][...]
            for j in range(3):
                tot = tot + rx[i][j].astype(F32)
            tiles[i][r % 2] = tot
            for cp in writes(i, r, r % 2):
                cp.start()

        @pl.when(r == 0)
        def _():
            for i in range(ns):
                term = [jnp.where(bc_ref[0] == kk, sl_own[i][...], sl[i][kk]) for kk in range(N_CHIP)]
                tiles[ng + i][...] = ((term[0] + term[1]) + term[2]) + term[3]
                for cp in small_writes(ng + i):
                    cp.start()

        @pl.when(r == steps - 1)
        def _():
            finish(steps - 2, (steps - 2) % 2)
            finish(steps - 1, (steps - 1) % 2)
            for i in range(ns):
                local, remote = small_writes(ng + i)
                local.wait()
                remote.wait_send()
                arriving(ng + i, 0, halves[ng + i]).wait_recv()
            for i in range(ng):
                for step in range(steps):
                    arriving(i, step, rows[i]).wait_recv()

    in_specs = [pl.BlockSpec((None, rows[i], g.shape[2]), lambda r, bc: (bc[0], r, 0)) for i, g in enumerate(sums)]
    in_specs += [pl.BlockSpec((3, rows[i], g.shape[2]), lambda r, bc: (0, r, 0)) for i, g in enumerate(sums)]
    in_specs += [pl.BlockSpec(s.shape, lambda r, bc: (0, 0, 0)) for s in smalls_slots]
    in_specs += [pl.BlockSpec(s.shape[1:], lambda r, bc: (0, 0)) for s in smalls_slots]
    out_shape = [jax.ShapeDtypeStruct((2 * g.shape[1], g.shape[2]), F32) for g in sums]
    out_shape += [jax.ShapeDtypeStruct((2 * s.shape[1], s.shape[2]), F32) for s in smalls_slots]
    scratch = [pltpu.VMEM((2, rows[i], g.shape[2]), F32) for i, g in enumerate(sums)]
    scratch += [pltpu.VMEM(s.shape[1:], F32) for s in smalls_slots]
    scratch += [pltpu.SemaphoreType.DMA((n, 2)), pltpu.SemaphoreType.DMA((n, 2)), pltpu.SemaphoreType.DMA((n, steps))]
    return list(pl.pallas_call(
        body, out_shape=out_shape,
        grid_spec=pltpu.PrefetchScalarGridSpec(num_scalar_prefetch=1, grid=(steps,), in_specs=in_specs,
                                               out_specs=[ANY] * n, scratch_shapes=scratch),
        compiler_params=_cp(("arbitrary",), VMEM_LIMIT), name=name)(bc_idx, *sums, *recvd, *smalls_slots, *smalls_own))


def _adamw_math(w, g, m, v):
    m2 = ADAM_B1 * m + (1.0 - ADAM_B1) * g
    v2 = ADAM_B2 * v + (1.0 - ADAM_B2) * (g * g)
    m_hat = m2 / (1.0 - ADAM_B1 ** ADAM_STEP)
    v_hat = v2 / (1.0 - ADAM_B2 ** ADAM_STEP)
    delta = -ADAM_LR * (m_hat / (jnp.sqrt(v_hat) + ADAM_EPS) + ADAM_WD * w)
    return delta, m2, v2


def _adamw_big(ws, gs, ms, vs, name, steps=8):
    n = len(ws)

    def body(*refs):
        for i in range(n):
            w_ref, g_ref, m_ref, v_ref = (refs[k * n + i] for k in range(4))
            d_ref, m2_ref, v2_ref, g2_ref = (refs[(4 + k) * n + i] for k in range(4))
            gv = g_ref[...]
            d_ref[...], m2_ref[...], v2_ref[...] = _adamw_math(w_ref[...], gv, m_ref[...], v_ref[...])
            g2_ref[...] = gv

    specs = [pl.BlockSpec((w.shape[0] // steps, w.shape[1]), lambda i: (i, 0)) for w in ws]
    shapes = [jax.ShapeDtypeStruct(w.shape, F32) for w in ws]
    outs = pl.pallas_call(
        body, grid=(steps,), in_specs=specs * 4, out_specs=specs * 4, out_shape=shapes * 4,
        compiler_params=_cp(("parallel",), VMEM_LIMIT), name=name)(*ws, *gs, *ms, *vs)
    return [tuple(outs[k * n + i] for k in range(4)) for i in range(n)]


def _adamw_small(groups):
    n = len(groups)

    def body(*refs):
        for i in range(n):
            w_ref, g_ref, m_ref, v_ref = refs[4 * i:4 * i + 4]
            d_ref, m2_ref, v2_ref = refs[4 * n + 3 * i:4 * n + 3 * i + 3]
            d_ref[...], m2_ref[...], v2_ref[...] = _adamw_math(w_ref[...], g_ref[...], m_ref[...], v_ref[...])

    flat = [a for grp in groups for a in grp]
    out_shape = [jax.ShapeDtypeStruct(grp[0].shape, F32) for grp in groups for _ in range(3)]
    outs = pl.pallas_call(body, out_shape=out_shape, name="adamw_small")(*flat)
    return [tuple(outs[3 * i:3 * i + 3]) for i in range(n)]


def kernel(x, mem, norm_mix_g, w_in, conv_w, gm_ln_g, gm_ln_b, gm_ws, gm_bs, w_out, norm_x_g, norm_mem_g, w_q, w_kv, w_xo, norm_final_g, loss_target, m_norm_mix_g, m_w_in, m_conv_w, m_gm_ln_g, m_gm_ln_b, m_gm_ws, m_gm_bs, m_w_out, m_norm_x_g, m_norm_mem_g, m_w_q, m_w_kv, m_w_xo, m_norm_final_g, v_norm_mix_g, v_w_in, v_conv_w, v_gm_ln_g, v_gm_ln_b, v_gm_ws, v_gm_bs, v_w_out, v_norm_x_g, v_norm_mem_g, v_w_q, v_w_kv, v_w_xo, v_norm_final_g):
    t = x.shape[1]
    xi = lax.axis_index("x")
    yi = lax.axis_index("y")
    ci = lax.axis_index("c")
    b_idx = jnp.reshape(2 * xi + yi, (1,)).astype(jnp.int32)
    c_idx = jnp.reshape(ci, (1,)).astype(jnp.int32)

    x2d, mem2d, tgt = x[0], mem[0], loss_target[0]
    big = [w_in[0], w_out[0], w_q[0], w_kv[0], w_xo[0]]
    big_m = [m_w_in[0], m_w_out[0], m_w_q[0], m_w_kv[0], m_w_xo[0]]
    big_v = [v_w_in[0], v_w_out[0], v_w_q[0], v_w_kv[0], v_w_xo[0]]
    g3 = norm_final_g.reshape(1, D)

    def pad8(a):
        return jnp.pad(a, ((0, 8 - a.shape[0]), (0, 0)))

    own_blocks = _cast_shards(b_idx, big)

    tril = jnp.tril(jnp.ones((CH, CH), bool))
    wc32 = jnp.where(tril[None], gm_ws[0], 0.0)
    wc = wc32.astype(BF16)
    wct = jnp.swapaxes(wc32, 1, 2).astype(BF16)
    bsb = jnp.broadcast_to(gm_bs[0][:, :, None], (HEADS, CH, CH))

    blk = 2 * xi + yi
    near = [blk ^ (2 >> (k % 2)) for k in range(2 * N_PIECE)]
    seq_blk = jnp.stack([blk] * N_PIECE + near + [blk ^ 3] * N_PIECE)
    seq_piece = jnp.asarray(list(range(N_PIECE)) + [k // 2 for k in range(2 * N_PIECE)] + list(range(N_PIECE)))
    seq = jnp.stack([seq_blk, seq_piece, seq_blk * N_PIECE + seq_piece]).astype(jnp.int32)
    proj, hb, win_f, cw8, (wq_f,) = _proj_gather(
        seq, x2d, norm_mix_g, own_blocks[0], pad8(conv_w[0]), [own_blocks[2]])
    mixin, (wout_f, wkv_f, wxo_f) = _mixer_fwd(
        proj, cw8, gm_ln_g, gm_ln_b, wc, bsb, [own_blocks[1], own_blocks[3], own_blocks[4]])
    wout2, wq2, wxo2 = wout_f.reshape(MIX, D), wq_f.reshape(D, D), wxo_f.reshape(D, D)
    k, v = _mem_fwd(mem2d, norm_mem_g, wkv_f)

    (loss_tile, dmix, dx1b, h2b, dq, ob, dx2b, dk, dv, dg2, dg3) = _tail(
        x2d, tgt, mixin, wout2, wq2, wxo2, k, v, norm_x_g, g3)
    dwkv, dwkv_b, dgm = _mem_bwd(mem2d, norm_mem_g, dk, dv, wkv_f)
    dproj, dcw, dlng, dlnb, dwc, dbs8, grad_x, dg1 = _mixer_bwd(
        proj, dmix, cw8, gm_ln_g, gm_ln_b, wc, wct, bsb, win_f, x2d, dx1b, norm_mix_g)

    bc_idx = jnp.concatenate([b_idx, c_idx])
    zero = jnp.zeros((1, D), F32)
    loss_row = jnp.broadcast_to(loss_tile[0:1, 0:1], (1, D))
    sv = jnp.concatenate([dg1[0:1], dg2, dgm, dg3, dlng, dlnb, dbs8[0:1], loss_row, dcw], axis=0)
    sw = dwc.reshape(HEADS * CH, CH)
    dwin_sum, dwin_sum_b, psmall = _grad_matmul_pair(c_idx, hb, dproj, [sv, sw], dgm, name="grad_w_in")
    sums_b = [dwin_sum]
    send_b, recv_b, src_b, land_b, token_b = _exchange_begin([dwin_sum_b], psmall, "exchange_b_begin")

    dwxo, dwxo_b = _grad_matmul(ob, dx2b, token_b, by_cols=False, name="grad_w_xo", tk=2048)
    dwq, dwq_b = _grad_matmul(h2b, dq, token_b, by_cols=False, name="grad_w_q", tk=2048)
    dwout, dwout_b = _grad_matmul(mixin, dx1b, token_b, by_cols=False, name="grad_w_out")
    ps_a = _pair_reduce(c_idx, [dwout, dwq, dwkv, dwxo], [dwout_b, dwq_b, dwkv_b, dwxo_b], [], "pair_reduce_a")
    sums_a, sums_a_b = list(ps_a[:4]), list(ps_a[4:8])
    send_a, recv_a, src_a, land_a, token_a = _exchange_begin(sums_a_b, [], "exchange_a_begin")

    rx2b = _exchange_end(send_b, recv_b, src_b, land_b, 1, [token_a], "exchange_b_end")
    gwin, svf, swf = _chip_reduce(bc_idx, sums_b, rx2b[:1], rx2b[1:], psmall, "chip_reduce_b")
    out_b = _adamw_big(big[:1], [gwin], big_m[:1], big_v[:1], "adamw_w_in")[0]

    def vec_pack(a1, a2, am, a3, lg, lb, bs):
        return jnp.concatenate([a1, a2, am, a3.reshape(1, D), lg, lb, bs.reshape(1, D), zero], axis=0)

    wv = vec_pack(norm_mix_g, norm_x_g, norm_mem_g, norm_final_g, gm_ln_g, gm_ln_b, gm_bs)
    mv = vec_pack(m_norm_mix_g, m_norm_x_g, m_norm_mem_g, m_norm_final_g, m_gm_ln_g, m_gm_ln_b, m_gm_bs)
    vv = vec_pack(v_norm_mix_g, v_norm_x_g, v_norm_mem_g, v_norm_final_g, v_gm_ln_g, v_gm_ln_b, v_gm_bs)
    loss = svf[7, 0]
    gcw = lax.dynamic_slice_in_dim(svf[8:16], blk * (D // N_CHIP), D // N_CHIP, axis=1)
    gws = swf
    gv = svf[0:8]
    (dv_, mv_, vv_), (dc_, mc_, vc_), (dws_, mws_, vws_) = _adamw_small([
        (wv, gv, mv, vv),
        (pad8(conv_w[0]), gcw, pad8(m_conv_w[0]), pad8(v_conv_w[0])),
        (gm_ws.reshape(HEADS * CH, CH), gws, m_gm_ws.reshape(HEADS * CH, CH), v_gm_ws.reshape(HEADS * CH, CH))])

    rx2a = _exchange_end(send_a, recv_a, src_a, land_a, 4, [out_b[0], dv_], "exchange_a_end")
    g_a = _chip_reduce(bc_idx, sums_a, rx2a, [], [], "chip_reduce_a")
    big_out = [out_b] + _adamw_big(big[1:], g_a, big_m[1:], big_v[1:], "adamw_rest")

    def unpack(vecs, cw, ws, bigs):
        r = lambda i: vecs[i:i + 1]
        return [r(0), bigs[0][None], cw[0:3][None], r(4), r(5), ws.reshape(1, HEADS, CH, CH), vecs[6].reshape(1, HEADS, CH),
                bigs[1][None], r(1), r(2), bigs[2][None], bigs[3][None], bigs[4][None], vecs[3]]

    grads_out = unpack(gv, gcw, gws, [o[3] for o in big_out])
    delta_out = unpack(dv_, dc_, dws_, [o[0] for o in big_out])
    m_out = unpack(mv_, mc_, mws_, [o[1] for o in big_out])
    v_out = unpack(vv_, vc_, vws_, [o[2] for o in big_out])
    return (loss, grad_x[None], *grads_out, *delta_out, *m_out, *v_out)
```

```python
import functools
import math

import jax
import jax.numpy as jnp
from jax import lax
from jax.experimental import pallas as pl
from jax.experimental.pallas import tpu as pltpu

F32 = jnp.float32
BF16 = jnp.bfloat16
MESH = pl.DeviceIdType.MESH

D = 1024
SLAB = 1024
N_SLAB = 7
IN_DIM = N_SLAB * SLAB
MIX = 2 * SLAB
HEADS = 8
CH = 128
XH = 4
XD = D // XH
EPS = 1e-6
GELU_C = math.sqrt(2.0 / math.pi)
GELU_A = 0.044715
N_CHIP = 4
IN_BLK = IN_DIM // N_CHIP
IN_PIECE = 256
N_PIECE = IN_BLK // IN_PIECE
KV_BLK = 2 * D // N_CHIP

ADAM_LR, ADAM_B1, ADAM_B2, ADAM_EPS, ADAM_WD, ADAM_STEP = 0.001, 0.9, 0.999, 1e-08, 0.01, 10

VMEM_LIMIT = 60 * 1024 * 1024


def _cp(sem=None, vmem=None):
    return pltpu.CompilerParams(dimension_semantics=sem, vmem_limit_bytes=vmem)


def _full(shape, buffers=None):
    n = len(shape)
    if buffers is None:
        return pl.BlockSpec(shape, lambda *_: (0,) * n)
    return pl.BlockSpec(shape, lambda *_: (0,) * n, pipeline_mode=pl.Buffered(buffers))


ANY = pl.BlockSpec(memory_space=pl.ANY)


def _bdot(a, b):
    return jnp.dot(a.astype(BF16), b.astype(BF16), preferred_element_type=F32)


def _bdot_nt(a, b):
    return lax.dot_general(a.astype(BF16), b.astype(BF16), (((1,), (1,)), ((), ())), preferred_element_type=F32)


def _bdot_tn(a, b):
    return lax.dot_general(a.astype(BF16), b.astype(BF16), (((0,), (0,)), ((), ())), preferred_element_type=F32)


def _rms(x, g):
    r = lax.rsqrt(jnp.mean(x * x, axis=-1, keepdims=True) + EPS)
    return x * r * g, r


def _rms_bwd(dy, x, r, g):
    gdy = dy * g
    dx = r * gdy - x * (r * r * r) * jnp.mean(x * gdy, axis=-1, keepdims=True)
    dg = jnp.sum(dy * x * r, axis=0, keepdims=True)
    return dx, dg


def _gelu_parts(x):
    x2 = x * x
    t = jnp.tanh(GELU_C * (x + GELU_A * x * x2))
    val = 0.5 * x * (1.0 + t)
    grad = 0.5 * (1.0 + t) + 0.5 * x * (1.0 - t * t) * (GELU_C * (1.0 + 3.0 * GELU_A * x2))
    return val, grad


def _gelu(x):
    return 0.5 * x * (1.0 + jnp.tanh(GELU_C * (x + GELU_A * x * x * x)))


def _sigmoid(z):
    return 1.0 / (1.0 + jnp.exp(-z))


def _cast_shards(b_idx, arrs):
    n = len(arrs)
    steps = 8

    def body(b_ref, *refs):
        for p in range(N_PIECE):
            refs[n][p] = refs[0][:, pl.ds(p * IN_PIECE, IN_PIECE)].astype(BF16)
        for i in range(1, n):
            refs[n + i][...] = refs[i][...].astype(BF16)

    rows = [a.shape[0] // steps for a in arrs]
    in_specs = [pl.BlockSpec((rows[i], a.shape[1]), lambda i, b: (i, 0)) for i, a in enumerate(arrs)]
    out_specs = [pl.BlockSpec((None, N_PIECE, rows[0], IN_PIECE), lambda i, b: (b[0], 0, i, 0))]
    out_specs += [pl.BlockSpec((None, rows[i], a.shape[1]), lambda i, b: (b[0], i, 0)) for i, a in enumerate(arrs) if i > 0]
    out_shape = [jax.ShapeDtypeStruct((N_CHIP, N_PIECE, arrs[0].shape[0], IN_PIECE), BF16)]
    out_shape += [jax.ShapeDtypeStruct((N_CHIP,) + a.shape, BF16) for a in arrs[1:]]
    return pl.pallas_call(
        body, out_shape=out_shape,
        grid_spec=pltpu.PrefetchScalarGridSpec(num_scalar_prefetch=1, grid=(steps,), in_specs=in_specs, out_specs=out_specs),
        compiler_params=_cp(("arbitrary",)), name="cast_shards")(b_idx, *arrs)


def _proj_gather(seq, x, g, win_own, cw8s, more, tm=1024):
    t = x.shape[0]
    ni = t // tm
    nm = len(more)
    steps = N_CHIP * N_PIECE
    near0, far0 = N_PIECE, 3 * N_PIECE

    def body(*refs):
        seq_ref, x_any, g_ref, win_in, cw_in = refs[:5]
        o_ref, hb_any, win_f, cw_out = refs[5 + nm:9 + nm]
        more_out = refs[9 + nm:9 + 2 * nm]
        hbuf, xbuf, wv, cw_s, cw_r, loc = refs[9 + 2 * nm:15 + 2 * nm]
        g_in = _Gather([win_f.at[:, p] for p in range(N_PIECE)], *refs[15 + 2 * nm:19 + 2 * nm])
        g_more = _Gather(more_out, *refs[19 + 2 * nm:23 + 2 * nm])
        s = pl.program_id(0)
        x, y, c, chips = _coords()
        b = 2 * x + y
        blks = [2 * chip[0] + chip[1] for chip in chips]

        def cw_cols(blk):
            return cw_out.at[:, pl.ds(blk * (D // N_CHIP), D // N_CHIP)]

        def cw_copy(k, blk):
            src = cw_in if blk is None else cw_cols(blk)
            return pltpu.make_async_remote_copy(src_ref=src, dst_ref=cw_cols(b if blk is None else blk), send_sem=cw_s.at[k],
                                                recv_sem=cw_r.at[k], device_id=(*chips[k], c), device_id_type=MESH)

        cw_local = pltpu.make_async_copy(cw_in, cw_cols(b), loc.at[1])
        hb_copy = pltpu.make_async_copy(hbuf, hb_any, loc.at[0])

        def load(step):
            slot = lax.rem(step, 2)
            return pltpu.make_async_copy(win_f.at[seq_ref[0, step], seq_ref[1, step]], wv.at[slot], loc.at[2 + slot])

        def chunk(i):
            return pltpu.make_async_copy(x_any.at[pl.ds(i * tm, tm)], xbuf.at[i % 2], loc.at[4 + i % 2])

        def first():
            g_in.start()
            cw_local.start()
            for k in range(3):
                cw_copy(k, None).start()
            load(0).start()
            chunk(0).start()
            for i in range(ni):
                if i + 1 < ni:
                    chunk(i + 1).start()
                chunk(i).wait()
                h, _ = _rms(xbuf[i % 2], g_ref[...])
                hbuf[pl.ds(i * tm, tm), :] = h.astype(BF16)
            hb_copy.start()

        events = {step: [] for step in range(steps)}
        events[0].append(first)
        for p in range(N_PIECE):
            events[2 * p + 2].append(functools.partial(g_in.hop, [p]))
            events[near0 + 2 * p - 1].append(functools.partial(g_in.near_ready, [p]))
            events[far0 + p - 2].append(functools.partial(g_in.far, [p]))
            events[far0 + p - 1].append(functools.partial(g_in.far_ready, [p]))
        events[2 * N_PIECE + 1].append(g_more.start)
        for step, todo in events.items():
            if todo:
                @pl.when(s == step)
                def _(todo=todo):
                    for do in todo:
                        do()

        @pl.when(s + 1 < steps)
        def _():
            load(s + 1).start()

        load(s).wait()
        for i in range(ni):
            rows = pl.ds(i * tm, tm)
            o_ref[rows, :] = jnp.dot(hbuf[rows, :], wv[lax.rem(s, 2)], preferred_element_type=F32).astype(BF16)

        @pl.when(s == steps - 1)
        def _():
            g_more.hop()
            g_more.far()
            for k in range(3):
                cw_copy(k, blks[k]).wait_recv()
            for k in range(3):
                cw_copy(k, None).wait_send()
            cw_local.wait()
            hb_copy.wait()
            g_more.near_ready()
            g_more.far_ready()
            g_in.drain()
            g_more.drain()

    in_specs = [ANY, pl.BlockSpec((1, D), lambda s, q: (0, 0)), ANY, ANY] + [ANY] * nm
    out_specs = [pl.BlockSpec((t, IN_PIECE), lambda s, q: (0, q[2, s])), ANY, ANY, ANY] + [ANY] * nm
    outs = pl.pallas_call(
        body, out_shape=[jax.ShapeDtypeStruct((t, IN_DIM), BF16), jax.ShapeDtypeStruct((t, D), BF16),
                         jax.ShapeDtypeStruct(win_own.shape, BF16), jax.ShapeDtypeStruct((8, D), F32)]
        + [jax.ShapeDtypeStruct(f.shape, f.dtype) for f in more],
        grid_spec=pltpu.PrefetchScalarGridSpec(
            num_scalar_prefetch=1, grid=(steps,), in_specs=in_specs, out_specs=out_specs,
            scratch_shapes=[pltpu.VMEM((t, D), BF16), pltpu.VMEM((2, tm, D), F32), pltpu.VMEM((2, D, IN_PIECE), BF16)]
            + [pltpu.SemaphoreType.DMA((3,))] * 2 + [pltpu.SemaphoreType.DMA((6,))]
            + _gather_sems(N_PIECE) + _gather_sems(nm)),
        input_output_aliases={3: 2, **{5 + w: 4 + w for w in range(nm)}},
        compiler_params=_cp(("arbitrary",), VMEM_LIMIT), name="proj_gather")(seq, x, g, win_own, cw8s, *more)
    return outs[0], outs[1], outs[2], outs[3], outs[4:]


class _Gather:
    def __init__(self, outs, ici_s, ici_r, d2d_s, d2d_r):
        x, y, c, _ = _coords()
        self.outs, self.c = outs, c
        self.sems = ici_s, ici_r, d2d_s, d2d_r
        self.b, self.bx, self.by, self.bd = 2 * x + y, 2 * (1 - x) + y, 2 * x + (1 - y), 2 * (1 - x) + (1 - y)
        self.xn, self.yn, self.sib = (1 - x, y, c), (x, 1 - y, c), (x, y, 1 - c)

    def piece(self, w, blk, hc, quarter=None):
        hr = self.outs[w].shape[1] // 2
        if quarter is None:
            return self.outs[w].at[blk, pl.ds(hc * hr, hr)]
        return self.outs[w].at[blk, pl.ds(hc * hr + quarter * (hr // 2), hr // 2)]

    def ici(self, w, k, ref, to):
        return pltpu.make_async_remote_copy(src_ref=ref, dst_ref=ref, send_sem=self.sems[0].at[w, k],
                                            recv_sem=self.sems[1].at[w, k], device_id=to, device_id_type=MESH)

    def d2d(self, w, k, ref):
        return pltpu.make_async_remote_copy(src_ref=ref, dst_ref=ref, send_sem=self.sems[2].at[w, k],
                                            recv_sem=self.sems[3].at[w, k], device_id=self.sib, device_id_type=MESH)

    def all(self):
        return range(len(self.outs))

    def start(self):
        for w in self.all():
            mine = self.piece(w, self.b, self.c)
            self.ici(w, 0, mine, self.xn).start()
            self.ici(w, 1, mine, self.yn).start()

    def hop(self, ws=None):
        c = self.c
        for w in ws or self.all():
            self.ici(w, 0, self.piece(w, self.bx, c), self.xn).wait_recv()
            self.ici(w, 1, self.piece(w, self.by, c), self.yn).wait_recv()
            self.ici(w, 2, self.piece(w, self.bx, c, 0), self.yn).start()
            self.ici(w, 3, self.piece(w, self.by, c, 1), self.xn).start()
            self.d2d(w, 0, self.piece(w, self.bx, c)).start()
            self.d2d(w, 1, self.piece(w, self.by, c)).start()

    def near_ready(self, ws=None):
        for w in ws or self.all():
            self.d2d(w, 0, self.piece(w, self.bx, 1 - self.c)).wait_recv()
            self.d2d(w, 1, self.piece(w, self.by, 1 - self.c)).wait_recv()

    def far(self, ws=None):
        c = self.c
        for w in ws or self.all():
            self.ici(w, 2, self.piece(w, self.bd, c, 0), self.yn).wait_recv()
            self.ici(w, 3, self.piece(w, self.bd, c, 1), self.xn).wait_recv()
            self.d2d(w, 2, self.piece(w, self.bd, c, 0)).start()
            self.d2d(w, 3, self.piece(w, self.bd, c, 1)).start()

    def far_ready(self, ws=None):
        for w in ws or self.all():
            self.d2d(w, 2, self.piece(w, self.bd, 1 - self.c, 0)).wait_recv()
            self.d2d(w, 3, self.piece(w, self.bd, 1 - self.c, 1)).wait_recv()

    def drain(self):
        c = self.c
        for w in self.all():
            mine = self.piece(w, self.b, c)
            self.ici(w, 0, mine, self.xn).wait_send()
            self.ici(w, 1, mine, self.yn).wait_send()
            self.ici(w, 2, self.piece(w, self.bx, c, 0), self.yn).wait_send()
            self.ici(w, 3, self.piece(w, self.by, c, 1), self.xn).wait_send()
            self.d2d(w, 0, self.piece(w, self.bx, c)).wait_send()
            self.d2d(w, 1, self.piece(w, self.by, c)).wait_send()
            self.d2d(w, 2, self.piece(w, self.bd, c, 0)).wait_send()
            self.d2d(w, 3, self.piece(w, self.bd, c, 1)).wait_send()


def _gather_sems(nw):
    return [pltpu.SemaphoreType.DMA((max(nw, 1), 4))] * 4


def _mixer_fwd(proj, cw8, lng, lnb, wc, bsb, fulls, tm=256):
    t = proj.shape[0]
    nt = t // tm
    nch = tm // CH
    nw = len(fulls)

    def body(*refs):
        p_ref, cw_ref, lng_ref, lnb_ref, wc_ref, bsb_ref = refs[:6]
        mix_ref = refs[6 + nw]
        w_outs = refs[7 + nw:7 + 2 * nw]
        prev_ref = refs[7 + 2 * nw]
        gather = _Gather(w_outs, *refs[8 + 2 * nw:])

        @pl.when(pl.program_id(0) == 0)
        def _():
            gather.start()
            prev_ref[...] = jnp.zeros_like(prev_ref)

        @pl.when(pl.program_id(0) == nt // 2)
        def _():
            gather.hop()

        @pl.when(pl.program_id(0) == nt - 1)
        def _():
            gather.far()

        rows = lax.broadcasted_iota(jnp.int32, (tm, CH), 0)
        for s in range(HEADS):
            cs = pl.ds(CH * s, CH)

            def slab(k):
                return p_ref[:, pl.ds(k * SLAB + CH * s, CH)].astype(F32)

            gb, gc, xa, za = slab(0), slab(1), slab(2), slab(3)
            cx = gc * xa
            p6 = jnp.broadcast_to(prev_ref[6:7, cs], (tm, CH))
            p7 = jnp.broadcast_to(prev_ref[7:8, cs], (tm, CH))
            c1 = jnp.where(rows == 0, p7, pltpu.roll(cx, 1, 0))
            c2 = jnp.where(rows == 0, p6, jnp.where(rows == 1, p7, pltpu.roll(cx, 2, 0)))
            prev_ref[:, cs] = cx[tm - 8:, :]
            cv = cw_ref[0:1, cs] * c2 + cw_ref[1:2, cs] * c1 + cw_ref[2:3, cs] * cx
            mix_ref[:, cs] = (gb * cv * (za * _sigmoid(za))).astype(BF16)

            u, v, zb = slab(4), slab(5), slab(6)
            ug, vg = _gelu(u), _gelu(v)
            dlt = vg - jnp.mean(vg, axis=-1, keepdims=True)
            vhat = dlt * lax.rsqrt(jnp.mean(dlt * dlt, axis=-1, keepdims=True) + EPS)
            vn = (vhat * lng_ref[:, cs] + lnb_ref[:, cs]).astype(BF16)
            gate = ug * (zb * _sigmoid(zb))
            for c in range(nch):
                rs = slice(CH * c, CH * (c + 1))
                sp = jnp.dot(wc_ref[s], vn[rs], preferred_element_type=F32) + bsb_ref[s]
                mix_ref[rs, pl.ds(SLAB + CH * s, CH)] = (gate[rs] * sp).astype(BF16)

        @pl.when(pl.program_id(0) == nt - 1)
        def _():
            gather.near_ready()
            gather.far_ready()
            gather.drain()

    sems = _gather_sems(nw)
    outs = pl.pallas_call(
        body, grid=(nt,),
        in_specs=[pl.BlockSpec((tm, IN_DIM), lambda i: (i, 0)), _full((8, D)), _full((1, D)), _full((1, D)),
                  _full((HEADS, CH, CH)), _full((HEADS, CH, CH))] + [ANY] * nw,
        out_specs=[pl.BlockSpec((tm, MIX), lambda i: (i, 0))] + [ANY] * nw,
        out_shape=[jax.ShapeDtypeStruct((t, MIX), BF16)] + [jax.ShapeDtypeStruct(f.shape, f.dtype) for f in fulls],
        input_output_aliases={6 + w: 1 + w for w in range(nw)},
        scratch_shapes=[pltpu.VMEM((8, D), F32)] + sems,
        compiler_params=_cp(("arbitrary",), VMEM_LIMIT), name="mixer_fwd")(proj, cw8, lng, lnb, wc, bsb, *fulls)
    return outs[0], outs[1:]


def _mem_fwd(mem, gm, wkv_f):
    n_mem = mem.shape[0]

    def body(mem_ref, gm_ref, w_ref, k_ref, v_ref):
        m, _ = _rms(mem_ref[...], gm_ref[...])
        mb = m.astype(BF16)
        for j in range(N_CHIP):
            dst = k_ref if j < 2 else v_ref
            dst[:, pl.ds(KV_BLK * (j % 2), KV_BLK)] = jnp.dot(mb, w_ref[j], preferred_element_type=F32).astype(BF16)

    return pl.pallas_call(
        body, out_shape=[jax.ShapeDtypeStruct((n_mem, D), BF16), jax.ShapeDtypeStruct((n_mem, D), BF16)],
        compiler_params=_cp(None, VMEM_LIMIT), name="mem_fwd")(mem, gm, wkv_f)


def _tail(x, tgt, mixin, wout, wq, wxo, k, v, g2, g3, tm=512, sub=512):
    t = x.shape[0]
    n_mem = k.shape[0]
    scale = 1.0 / math.sqrt(XD)

    def body(x_ref, tgt_ref, mix_ref, wout_ref, wq_ref, wxo_ref, k_ref, v_ref, g2_ref, g3_ref,
             loss_ref, dmix_ref, dx1b_ref, h2_ref, dq_ref, o_ref, dx2b_ref, dk_ref, dv_ref, dg2_ref, dg3_ref):
        @pl.when(pl.program_id(0) == 0)
        def _():
            loss_ref[...] = jnp.zeros_like(loss_ref)
            dk_ref[...] = jnp.zeros_like(dk_ref)
            dv_ref[...] = jnp.zeros_like(dv_ref)
            dg2_ref[...] = jnp.zeros_like(dg2_ref)
            dg3_ref[...] = jnp.zeros_like(dg3_ref)

        g2, g3 = g2_ref[...], g3_ref[...]
        for sb in range(tm // sub):
            rs = pl.ds(sub * sb, sub)
            x1 = x_ref[rs, :] + jnp.dot(mix_ref[rs, :], wout_ref[...], preferred_element_type=F32)
            h2, r2 = _rms(x1, g2)
            h2b = h2.astype(BF16)
            h2_ref[rs, :] = h2b
            q = jnp.dot(h2b, wq_ref[...], preferred_element_type=F32).astype(BF16)
            probs, outs = [], []
            for hd in range(XH):
                hs = pl.ds(XD * hd, XD)
                s = _bdot_nt(q[:, XD * hd:XD * (hd + 1)], k_ref[:, hs]) * scale
                e = jnp.exp(s - jnp.max(s, axis=-1, keepdims=True))
                p = e / jnp.sum(e, axis=-1, keepdims=True)
                probs.append(p)
                outs.append(_bdot(p, v_ref[:, hs]))
            ob = jnp.concatenate(outs, axis=-1).astype(BF16)
            o_ref[rs, :] = ob
            x2 = x1 + jnp.dot(ob, wxo_ref[...], preferred_element_type=F32)
            y, r3 = _rms(x2, g3)
            diff = y - tgt_ref[rs, :]
            row_loss = jnp.sum(diff * diff, axis=-1, keepdims=True)
            loss_ref[...] += jnp.broadcast_to(jnp.sum(row_loss, axis=0, keepdims=True) * (0.5 / D), loss_ref.shape)

            dx2, dg3 = _rms_bwd(diff * (1.0 / D), x2, r3, g3)
            dg3_ref[...] += dg3
            dx2b = dx2.astype(BF16)
            dx2b_ref[rs, :] = dx2b
            do = _bdot_nt(dx2b, wxo_ref[...])
            dqs = []
            for hd in range(XH):
                hs = pl.ds(XD * hd, XD)
                p = probs[hd]
                do_h = do[:, XD * hd:XD * (hd + 1)]
                dv_ref[:, hs] += _bdot_tn(p, do_h)
                dp = _bdot_nt(do_h, v_ref[:, hs])
                ds = p * (dp - jnp.sum(dp * p, axis=-1, keepdims=True))
                dqs.append(_bdot(ds, k_ref[:, hs]) * scale)
                dk_ref[:, hs] += _bdot_tn(ds, q[:, XD * hd:XD * (hd + 1)]) * scale
            dq = jnp.concatenate(dqs, axis=-1).astype(BF16)
            dq_ref[rs, :] = dq
            dx1n, dg2 = _rms_bwd(_bdot_nt(dq, wq_ref[...]), x1, r2, g2)
            dg2_ref[...] += dg2
            dx1b = (dx2 + dx1n).astype(BF16)
            dx1b_ref[rs, :] = dx1b
            dmix_ref[rs, :] = _bdot_nt(dx1b, wout_ref[...]).astype(BF16)

    tok = lambda w: pl.BlockSpec((tm, w), lambda i: (i, 0))
    return pl.pallas_call(
        body, grid=(t // tm,),
        in_specs=[tok(D), tok(D), tok(MIX), _full((MIX, D), 1), _full((D, D), 1), _full((D, D), 1),
                  _full((n_mem, D), 1), _full((n_mem, D), 1), _full((1, D)), _full((1, D))],
        out_specs=[_full((8, 128)), tok(MIX), tok(D), tok(D), tok(D), tok(D), tok(D),
                   _full((n_mem, D)), _full((n_mem, D)), _full((1, D)), _full((1, D))],
        out_shape=[jax.ShapeDtypeStruct((8, 128), F32), jax.ShapeDtypeStruct((t, MIX), BF16),
                   jax.ShapeDtypeStruct((t, D), BF16),
                   jax.ShapeDtypeStruct((t, D), BF16), jax.ShapeDtypeStruct((t, D), BF16),
                   jax.ShapeDtypeStruct((t, D), BF16), jax.ShapeDtypeStruct((t, D), BF16),
                   jax.ShapeDtypeStruct((n_mem, D), F32), jax.ShapeDtypeStruct((n_mem, D), F32),
                   jax.ShapeDtypeStruct((1, D), F32), jax.ShapeDtypeStruct((1, D), F32)],
        compiler_params=_cp(("arbitrary",), VMEM_LIMIT), name="tail")(x, tgt, mixin, wout, wq, wxo, k, v, g2, g3)


def _mem_bwd(mem, gm, dk, dv, wkv_f):
    def body(mem_ref, gm_ref, dk_ref, dv_ref, w_ref, dw_ref, dwb_ref, dgm_ref):
        mem_v = mem_ref[...]
        m, rm = _rms(mem_v, gm_ref[...])
        mb = m.astype(BF16)
        dm = jnp.zeros_like(mem_v)
        for j in range(N_CHIP):
            src = dk_ref if j < 2 else dv_ref
            dkv = src[:, pl.ds(KV_BLK * (j % 2), KV_BLK)].astype(BF16)
            dw = _bdot_tn(mb, dkv)
            dw_ref[j] = dw
            dwb_ref[j] = dw.astype(BF16)
            dm = dm + _bdot_nt(dkv, w_ref[j])
        dgm_ref[...] = jnp.sum(dm * mem_v * rm, axis=0, keepdims=True)

    return pl.pallas_call(
        body, out_shape=[jax.ShapeDtypeStruct((N_CHIP, D, KV_BLK), F32), jax.ShapeDtypeStruct((N_CHIP, D, KV_BLK), BF16),
                         jax.ShapeDtypeStruct((1, D), F32)],
        compiler_params=_cp(None, VMEM_LIMIT), name="mem_bwd")(mem, gm, dk, dv, wkv_f)


def _mixer_bwd(proj, dmix, cw8, lng, lnb, wc, wct, bsb, win_f, x, dx1, g1, tm=256):
    t = proj.shape[0]
    nt = t // tm
    nch = tm // CH
    hb = 16
    pair = 2 * CH
    assert pair == IN_PIECE

    def body(p_ref, pgc_ref, pxa_ref, dm_ref, cw_ref, lng_ref, lnb_ref, wc_ref, wct_ref, bsb_ref, w_ref, x_ref,
             dx1_ref, g1_ref, dp_ref, dcw_ref, dlng_ref, dlnb_ref, dwc_ref, dbs_ref, gx_ref, dg1_ref,
             next_ref, dh_ref):
        i = pl.program_id(0)

        @pl.when(i == 0)
        def _():
            next_ref[...] = jnp.zeros_like(next_ref)
            dcw_ref[...] = jnp.zeros_like(dcw_ref)
            dlng_ref[...] = jnp.zeros_like(dlng_ref)
            dlnb_ref[...] = jnp.zeros_like(dlnb_ref)
            dwc_ref[...] = jnp.zeros_like(dwc_ref)
            dbs_ref[...] = jnp.zeros_like(dbs_ref)
            dg1_ref[...] = jnp.zeros_like(dg1_ref)

        first_tile = i == nt - 1
        rows = lax.broadcasted_iota(jnp.int32, (tm, CH), 0)
        ones8 = jnp.ones((8, CH), BF16)
        for s in range(HEADS):
            cs = pl.ds(CH * s, CH)

            def slab(k):
                return p_ref[:, pl.ds(k * SLAB + CH * s, CH)].astype(F32)

            gb, gc, xa, za = slab(0), slab(1), slab(2), slab(3)
            da = dm_ref[:, cs].astype(F32)
            cx = gc * xa
            cxp = pgc_ref[:, cs].astype(F32) * pxa_ref[:, cs].astype(F32)
            cxp = jnp.where(first_tile, jnp.zeros_like(cxp), cxp)
            p6 = jnp.broadcast_to(cxp[hb - 2:hb - 1, :], (tm, CH))
            p7 = jnp.broadcast_to(cxp[hb - 1:hb, :], (tm, CH))
            c1 = jnp.where(rows == 0, p7, pltpu.roll(cx, 1, 0))
            c2 = jnp.where(rows == 0, p6, jnp.where(rows == 1, p7, pltpu.roll(cx, 2, 0)))
            w0, w1, w2 = cw_ref[0:1, cs], cw_ref[1:2, cs], cw_ref[2:3, cs]
            cv = w0 * c2 + w1 * c1 + w2 * cx
            sg = _sigmoid(za)
            sa = za * sg
            dcv = da * gb * sa
            dp_ref[:, pl.ds(0 * SLAB + CH * s, CH)] = (da * cv * sa).astype(BF16)
            dp_ref[:, pl.ds(3 * SLAB + CH * s, CH)] = (da * gb * cv * (sg * (1.0 + za * (1.0 - sg)))).astype(BF16)
            n0 = jnp.broadcast_to(next_ref[0:1, cs], (tm, CH))
            n1 = jnp.broadcast_to(next_ref[1:2, cs], (tm, CH))
            u1 = jnp.where(rows == tm - 1, n0, pltpu.roll(dcv, tm - 1, 0))
            u2 = jnp.where(rows == tm - 2, n0, jnp.where(rows == tm - 1, n1, pltpu.roll(dcv, tm - 2, 0)))
            next_ref[:, cs] = dcv[0:8, :]
            dcx = w2 * dcv + w1 * u1 + w0 * u2
            dp_ref[:, pl.ds(1 * SLAB + CH * s, CH)] = (dcx * xa).astype(BF16)
            dp_ref[:, pl.ds(2 * SLAB + CH * s, CH)] = (dcx * gc).astype(BF16)
            dcw_ref[0:1, cs] += jnp.sum(dcv * c2, axis=0, keepdims=True)
            dcw_ref[1:2, cs] += jnp.sum(dcv * c1, axis=0, keepdims=True)
            dcw_ref[2:3, cs] += jnp.sum(dcv * cx, axis=0, keepdims=True)

            u, v, zb = slab(4), slab(5), slab(6)
            db = dm_ref[:, pl.ds(SLAB + CH * s, CH)].astype(F32)
            ug, ugrad = _gelu_parts(u)
            vg, vgrad = _gelu_parts(v)
            dlt = vg - jnp.mean(vg, axis=-1, keepdims=True)
            rstd = lax.rsqrt(jnp.mean(dlt * dlt, axis=-1, keepdims=True) + EPS)
            vhat = dlt * rstd
            lg = lng_ref[:, cs]
            vn = (vhat * lg + lnb_ref[:, cs]).astype(BF16)
            sgb = _sigmoid(zb)
            szb = zb * sgb
            sps, dvns = [], []
            dbs = jnp.zeros((8, CH), F32)
            dwc = jnp.zeros((CH, CH), F32)
            for c in range(nch):
                rs = slice(CH * c, CH * (c + 1))
                sp = jnp.dot(wc_ref[s], vn[rs], preferred_element_type=F32) + bsb_ref[s]
                dsp = (db[rs] * ug[rs] * szb[rs]).astype(BF16)
                dbs = dbs + lax.dot_general(ones8, dsp, (((1,), (1,)), ((), ())), preferred_element_type=F32)
                dwc = dwc + lax.dot_general(dsp, vn[rs], (((1,), (1,)), ((), ())), preferred_element_type=F32)
                dvns.append(jnp.dot(wct_ref[s], dsp, preferred_element_type=F32))
                sps.append(sp)
            sp = jnp.concatenate(sps, axis=0)
            dvn = jnp.concatenate(dvns, axis=0)
            dbs_ref[:, cs] += dbs
            dwc_ref[s] += dwc
            dlng_ref[:, cs] += jnp.sum(dvn * vhat, axis=0, keepdims=True)
            dlnb_ref[:, cs] += jnp.sum(dvn, axis=0, keepdims=True)
            dvhat = dvn * lg
            dvg = rstd * (dvhat - jnp.mean(dvhat, axis=-1, keepdims=True)
                          - vhat * jnp.mean(dvhat * vhat, axis=-1, keepdims=True))
            dp_ref[:, pl.ds(4 * SLAB + CH * s, CH)] = (db * sp * szb * ugrad).astype(BF16)
            dp_ref[:, pl.ds(5 * SLAB + CH * s, CH)] = (dvg * vgrad).astype(BF16)
            dp_ref[:, pl.ds(6 * SLAB + CH * s, CH)] = (db * ug * sp * (sgb * (1.0 + zb * (1.0 - sgb)))).astype(BF16)

            if s % 2 == 1:
                part = None
                for k in range(N_SLAB):
                    col = k * SLAB + pair * (s // 2)
                    blk, off = divmod(col, IN_BLK)
                    term = lax.dot_general(dp_ref[:, pl.ds(col, pair)], w_ref[blk, off // IN_PIECE],
                                           (((1,), (1,)), ((), ())), preferred_element_type=F32)
                    part = term if part is None else part + term
                if s == 1:
                    dh_ref[...] = part
                else:
                    dh_ref[...] += part

        xv = x_ref[...]
        r = lax.rsqrt(jnp.mean(xv * xv, axis=-1, keepdims=True) + EPS)
        dxn, dg = _rms_bwd(dh_ref[...], xv, r, g1_ref[...])
        gx_ref[...] = dx1_ref[...].astype(F32) + dxn
        dg1_ref[0:1, :] += dg

        @pl.when(i == nt - 1)
        def _():
            tril = lax.broadcasted_iota(jnp.int32, (CH, CH), 0) >= lax.broadcasted_iota(jnp.int32, (CH, CH), 1)
            for s in range(HEADS):
                dwc_ref[s] = jnp.where(tril, dwc_ref[s], 0.0)

    rev = lambda i: nt - 1 - i
    halo = lambda col: pl.BlockSpec((hb, SLAB), lambda i: (jnp.maximum(rev(i) * (tm // hb) - 1, 0), col))
    tok = lambda w: pl.BlockSpec((tm, w), lambda i: (rev(i), 0))
    return pl.pallas_call(
        body, grid=(nt,),
        in_specs=[tok(IN_DIM), halo(1), halo(2), tok(MIX), _full((8, D)), _full((1, D)), _full((1, D)),
                  _full((HEADS, CH, CH)), _full((HEADS, CH, CH)), _full((HEADS, CH, CH)),
                  _full((N_CHIP, N_PIECE, D, IN_PIECE), 1), tok(D), tok(D), _full((1, D))],
        out_specs=[tok(IN_DIM), _full((8, D)), _full((1, D)), _full((1, D)), _full((HEADS, CH, CH)), _full((8, D)),
                   tok(D), _full((8, D))],
        out_shape=[jax.ShapeDtypeStruct((t, IN_DIM), BF16), jax.ShapeDtypeStruct((8, D), F32),
                   jax.ShapeDtypeStruct((1, D), F32), jax.ShapeDtypeStruct((1, D), F32),
                   jax.ShapeDtypeStruct((HEADS, CH, CH), F32), jax.ShapeDtypeStruct((8, D), F32),
                   jax.ShapeDtypeStruct((t, D), F32), jax.ShapeDtypeStruct((8, D), F32)],
        scratch_shapes=[pltpu.VMEM((8, D), F32), pltpu.VMEM((tm, D), F32)],
        compiler_params=_cp(("arbitrary",), VMEM_LIMIT), name="mixer_bwd")(
            proj, proj, proj, dmix, cw8, lng, lnb, wc, wct, bsb, win_f, x, dx1, g1)


def _grad_matmul(a, b, after, *, by_cols, name, tk=1024):
    t, m = a.shape
    n = b.shape[1]
    nk = t // tk
    nj = N_CHIP if by_cols else 1
    bn = n // nj

    def body(a_ref, b_ref, after_ref, o_ref, ob_ref):
        kk = pl.program_id(1)
        part = lax.dot_general(a_ref[...], b_ref[...], (((0,), (0,)), ((), ())), preferred_element_type=F32)

        @pl.when(kk == 0)
        def _():
            o_ref[...] = part

        @pl.when(kk > 0)
        def _():
            o_ref[...] += part

        @pl.when(kk == nk - 1)
        def _():
            ob_ref[...] = o_ref[...].astype(BF16)

    a_spec = pl.BlockSpec((tk, m), lambda j, k: (k, 0))
    b_spec = pl.BlockSpec((tk, bn), lambda j, k: (k, j))
    o_spec = pl.BlockSpec((None, m, bn), lambda j, k: (j, 0, 0))
    o32, o16 = pl.pallas_call(
        body, grid=(nj, nk), in_specs=[a_spec, b_spec, ANY], out_specs=[o_spec, o_spec],
        out_shape=[jax.ShapeDtypeStruct((nj, m, bn), F32), jax.ShapeDtypeStruct((nj, m, bn), BF16)],
        compiler_params=_cp(("parallel", "arbitrary"), VMEM_LIMIT), name=name)(a, b, after)
    if by_cols:
        return o32, o16
    return o32.reshape(N_CHIP, m // N_CHIP, n), o16.reshape(N_CHIP, m // N_CHIP, n)


def _coords():
    x, y, c = lax.axis_index("x"), lax.axis_index("y"), lax.axis_index("c")
    chips = [(1 - x, y), (x, 1 - y), (1 - x, 1 - y)]
    return x, y, c, chips


def _pair_reduce(c_idx, grads, grads_b, smalls, name):
    ng, ns = len(grads), len(smalls)
    halves = [g.shape[1] // 2 for g in grads]

    def body(c_ref, *refs):
        g_in, gb_any = refs[:ng], refs[ng:2 * ng]
        s_own, s_any = refs[2 * ng:2 * ng + ns], refs[2 * ng + ns:2 * ng + 2 * ns]
        o = refs[2 * ng + 2 * ns:4 * ng + 3 * ns]
        lands = refs[4 * ng + 3 * ns:5 * ng + 4 * ns]
        send, recv = refs[5 * ng + 4 * ns:]
        x, y, c, _ = _coords()
        j = pl.program_id(0)

        def big(i, blk):
            return pltpu.make_async_remote_copy(
                src_ref=gb_any[i].at[blk, pl.ds((1 - c) * halves[i], halves[i])], dst_ref=lands[i].at[blk],
                send_sem=send.at[i, blk], recv_sem=recv.at[i, blk], device_id=(x, y, 1 - c), device_id_type=MESH)

        def small(i):
            return pltpu.make_async_remote_copy(
                src_ref=s_any[i].at[1 - c], dst_ref=lands[ng + i],
                send_sem=send.at[ng + i, 0], recv_sem=recv.at[ng + i, 0], device_id=(x, y, 1 - c), device_id_type=MESH)

        @pl.when(j == 0)
        def _():
            for blk in range(N_CHIP):
                for i in range(ng):
                    big(i, blk).start()
            for i in range(ns):
                small(i).start()

        for i in range(ng):
            big(i, j).wait_recv()
            tot = g_in[i][...] + lands[i][j].astype(F32)
            o[i][...] = tot
            o[ng + i][...] = tot.astype(BF16)

        @pl.when(j == N_CHIP - 1)
        def _():
            for i in range(ns):
                small(i).wait_recv()
                o[2 * ng + i][...] = s_own[i][...] + lands[ng + i][...]
                small(i).wait_send()
            for blk in range(N_CHIP):
                for i in range(ng):
                    big(i, blk).wait_send()

    in_specs = [pl.BlockSpec((None, None, halves[i], g.shape[2]), lambda b, c: (b, c[0], 0, 0)) for i, g in enumerate(grads)]
    in_specs += [ANY] * ng
    in_specs += [pl.BlockSpec((None, s.shape[0] // 2, s.shape[1]), lambda b, c: (c[0], 0, 0)) for s in smalls]
    in_specs += [ANY] * ns
    blk = [pl.BlockSpec((None, halves[i], g.shape[2]), lambda b, c: (b, 0, 0)) for i, g in enumerate(grads)]
    out_specs = blk + blk + [pl.BlockSpec((s.shape[0] // 2, s.shape[1]), lambda b, c: (0, 0)) for s in smalls]
    out_shape = [jax.ShapeDtypeStruct((N_CHIP, halves[i], g.shape[2]), F32) for i, g in enumerate(grads)]
    out_shape += [jax.ShapeDtypeStruct((N_CHIP, halves[i], g.shape[2]), BF16) for i, g in enumerate(grads)]
    out_shape += [jax.ShapeDtypeStruct((s.shape[0] // 2, s.shape[1]), F32) for s in smalls]
    scratch = [pltpu.VMEM((N_CHIP, halves[i], g.shape[2]), BF16) for i, g in enumerate(grads)]
    scratch += [pltpu.VMEM((s.shape[0] // 2, s.shape[1]), F32) for s in smalls]
    scratch += [pltpu.SemaphoreType.DMA((ng + ns, N_CHIP)), pltpu.SemaphoreType.DMA((ng + ns, N_CHIP))]
    grads4 = [g.reshape(N_CHIP, 2, halves[i], g.shape[2]) for i, g in enumerate(grads)]
    smalls3 = [s.reshape(2, s.shape[0] // 2, s.shape[1]) for s in smalls]
    return pl.pallas_call(
        body, out_shape=out_shape,
        grid_spec=pltpu.PrefetchScalarGridSpec(num_scalar_prefetch=1, grid=(N_CHIP,), in_specs=in_specs,
                                               out_specs=out_specs, scratch_shapes=scratch),
        compiler_params=_cp(("arbitrary",), VMEM_LIMIT), name=name)(c_idx, *grads4, *grads_b, *smalls3, *smalls3)


def _grad_matmul_pair(c_idx, a, b, smalls, after, *, name, tk=2048):
    t, m = a.shape
    bn = b.shape[1] // N_CHIP
    nk = t // tk
    hr = m // 2
    ns = len(smalls)

    def body(c_ref, a_ref, b_ref, *refs):
        s_own, s_any = refs[:ns], refs[ns:2 * ns]
        o32, o16 = refs[2 * ns + 1], refs[2 * ns + 2]
        o_small = refs[2 * ns + 3:3 * ns + 3]
        acc, tb, land, st16 = refs[3 * ns + 3:3 * ns + 7]
        s_land, s_stage = refs[3 * ns + 7:4 * ns + 7], refs[4 * ns + 7:5 * ns + 7]
        send, recv, loc = refs[5 * ns + 7:]
        x, y, c, _ = _coords()
        sibling = dict(device_id=(x, y, 1 - c), device_id_type=MESH)
        j, kk = pl.program_id(0), pl.program_id(1)
        mine = pl.ds(pl.multiple_of(c * hr, hr), hr)
        theirs = pl.ds(pl.multiple_of((1 - c) * hr, hr), hr)

        def to_sibling(blk):
            return pltpu.make_async_remote_copy(src_ref=tb, dst_ref=land.at[blk], send_sem=send.at[blk],
                                                recv_sem=recv.at[blk], **sibling)

        def small(i):
            return pltpu.make_async_remote_copy(src_ref=s_any[i].at[1 - c], dst_ref=s_land[i], send_sem=send.at[N_CHIP + i],
                                                recv_sem=recv.at[N_CHIP + i], **sibling)

        def written(blk):
            return (pltpu.make_async_copy(acc.at[blk % 2, mine], o32.at[blk], loc.at[0]),
                    pltpu.make_async_copy(st16, o16.at[blk], loc.at[1]))

        def finish(blk):
            to_sibling(blk).wait_recv()

            @pl.when(blk > 0)
            def _():
                for cp in written(blk - 1):
                    cp.wait()

            tot = acc[blk % 2, mine, :] + land[blk].astype(F32)
            acc[blk % 2, mine, :] = tot
            st16[...] = tot.astype(BF16)
            for cp in written(blk):
                cp.start()

        def small_out(i):
            return pltpu.make_async_copy(s_stage[i], o_small[i], loc.at[2 + i])

        @pl.when((j == 0) & (kk == 0))
        def _():
            for i in range(ns):
                small(i).start()

        @pl.when((j == 1) & (kk == 0))
        def _():
            for i in range(ns):
                small(i).wait_recv()
                s_stage[i][...] = s_own[i][...] + s_land[i][...]
                small_out(i).start()

        @pl.when((j > 0) & (kk == 0))
        def _():
            finish(j - 1)

        part = lax.dot_general(a_ref[...], b_ref[...], (((0,), (0,)), ((), ())), preferred_element_type=F32)
        slot = lax.rem(j, 2)

        @pl.when(kk == 0)
        def _():
            acc[slot] = part

        @pl.when(kk > 0)
        def _():
            acc[slot] += part

        @pl.when(kk == nk - 1)
        def _():
            @pl.when(j > 0)
            def _():
                to_sibling(j - 1).wait_send()

            tb[...] = acc[slot, theirs, :].astype(BF16)
            to_sibling(j).start()

        @pl.when((j == N_CHIP - 1) & (kk == nk - 1))
        def _():
            finish(j)
            for i in range(ns):
                small_out(i).wait()
                small(i).wait_send()
            for cp in written(j):
                cp.wait()
            to_sibling(j).wait_send()

    halves = [(s.shape[0] // 2, s.shape[1]) for s in smalls]
    in_specs = [pl.BlockSpec((tk, m), lambda j, k, c: (k, 0)), pl.BlockSpec((tk, bn), lambda j, k, c: (k, j))]
    in_specs += [pl.BlockSpec((None,) + h, lambda j, k, c: (c[0], 0, 0)) for h in halves] + [ANY] * ns + [ANY]
    out_shape = [jax.ShapeDtypeStruct((N_CHIP, hr, bn), F32), jax.ShapeDtypeStruct((N_CHIP, hr, bn), BF16)]
    out_shape += [jax.ShapeDtypeStruct(h, F32) for h in halves]
    scratch = [pltpu.VMEM((2, m, bn), F32), pltpu.VMEM((hr, bn), BF16),
               pltpu.VMEM((N_CHIP, hr, bn), BF16), pltpu.VMEM((hr, bn), BF16)]
    scratch += [pltpu.VMEM(h, F32) for h in halves] * 2
    scratch += [pltpu.SemaphoreType.DMA((N_CHIP + ns,)), pltpu.SemaphoreType.DMA((N_CHIP + ns,)),
                pltpu.SemaphoreType.DMA((2 + ns,))]
    smalls3 = [s.reshape((2,) + h) for s, h in zip(smalls, halves)]
    outs = pl.pallas_call(
        body, out_shape=out_shape,
        grid_spec=pltpu.PrefetchScalarGridSpec(num_scalar_prefetch=1, grid=(N_CHIP, nk), in_specs=in_specs,
                                               out_specs=[ANY] * (2 + ns), scratch_shapes=scratch),
        compiler_params=_cp(("arbitrary", "arbitrary"), VMEM_LIMIT), name=name)(c_idx, a, b, *smalls3, *smalls3, after)
    return outs[0], outs[1], list(outs[2:])


_HBM = pl.BlockSpec(memory_space=pltpu.HBM)
_SEM = pl.BlockSpec(memory_space=pltpu.SEMAPHORE)


def _split_copies(ins, lands, ng, send, recv, arriving):
    x, y, c, chips = _coords()
    b = 2 * x + y
    copies = []
    for i in range(len(ins)):
        for k in range(3):
            blk = 2 * chips[k][0] + chips[k][1]
            src, dst, got = (ins[i].at[blk], lands[i].at[k], lands[i].at[k]) if i < ng else (ins[i], lands[i].at[b], lands[i].at[blk])
            sems = dict(send_sem=send.at[3 * i + k], recv_sem=recv.at[3 * i + k], device_id=(*chips[k], c), device_id_type=MESH)
            if arriving:
                copies.append(pltpu.make_async_remote_copy(src_ref=got, dst_ref=got, **sems))
            else:
                copies.append(pltpu.make_async_remote_copy(src_ref=src, dst_ref=dst, **sems))
    return copies


def _exchange_begin(sums_b, smalls, name):
    ng, n = len(sums_b), len(sums_b) + len(smalls)
    srcs = list(sums_b) + list(smalls)
    lands = [lax.empty((3,) + g.shape[1:], g.dtype) for g in sums_b] + [lax.empty((N_CHIP,) + s.shape, s.dtype) for s in smalls]

    def body(*refs):
        ins, land_refs = refs[:n], refs[n:2 * n]
        send, recv = refs[2 * n], refs[2 * n + 1]
        token = refs[4 * n + 2]
        for cp in _split_copies(ins, land_refs, ng, send, recv, False):
            cp.start()
        token[...] = jnp.zeros_like(token)

    hbm = lambda a: pltpu.HBM(a.shape, a.dtype)
    outs = pl.pallas_call(
        body, name=name,
        out_shape=(pltpu.SemaphoreType.DMA((3 * n,)), pltpu.SemaphoreType.DMA((3 * n,)), *[hbm(a) for a in srcs + lands],
                   jax.ShapeDtypeStruct((8, 128), F32)),
        in_specs=[_HBM] * (2 * n), out_specs=(_SEM, _SEM, *[_HBM] * (2 * n), pl.BlockSpec(memory_space=pltpu.VMEM)),
        input_output_aliases={i: 2 + i for i in range(2 * n)},
        compiler_params=pltpu.CompilerParams(has_side_effects=pltpu.SideEffectType.DATAFLOW_SIDE_EFFECTING),
    )(*[pltpu.with_memory_space_constraint(a, pltpu.HBM) for a in srcs + lands])
    return outs[0], outs[1], list(outs[2:2 + n]), list(outs[2 + n:2 + 2 * n]), outs[2 + 2 * n]


def _exchange_end(send, recv, srcs, lands, ng, which, after, name):
    n = len(srcs)
    after = list(after)

    def body(*refs):
        ins, land_refs = refs[:n], refs[n:2 * n]
        send_ref, recv_ref = refs[2 * n], refs[2 * n + 1]
        outgoing = _split_copies(ins, land_refs, ng, send_ref, recv_ref, False)
        arriving = _split_copies(ins, land_refs, ng, send_ref, recv_ref, True)
        for i in which:
            for cp in outgoing[3 * i:3 * i + 3]:
                cp.wait_send()
        for i in which:
            for cp in arriving[3 * i:3 * i + 3]:
                cp.wait_recv()

    hbm = lambda a: pltpu.HBM(a.shape, a.dtype)
    outs = pl.pallas_call(
        body, name=name, out_shape=tuple(hbm(a) for a in list(srcs) + list(lands)),
        in_specs=[_HBM] * (2 * n) + [_SEM, _SEM] + [ANY] * len(after), out_specs=tuple([_HBM] * (2 * n)),
        input_output_aliases={i: i for i in range(2 * n)},
        compiler_params=pltpu.CompilerParams(has_side_effects=pltpu.SideEffectType.DATAFLOW_SIDE_EFFECTING),
    )(*srcs, *lands, send, recv, *after)
    return list(outs[:n]), list(outs[n:])


def _chip_reduce(bc_idx, sums, recvd, smalls_slots, smalls_own, name, steps=4):
    ng, ns = len(sums), len(smalls_slots)
    n = ng + ns
    assert steps >= 2
    halves = [g.shape[1] for g in sums] + [s.shape[1] for s in smalls_slots]
    rows = [g.shape[1] // steps for g in sums]

    def body(bc_ref, *refs):
        own, rx = refs[:ng], refs[ng:2 * ng]
        sl = refs[2 * ng:2 * ng + ns]
        sl_own = refs[2 * ng + ns:2 * ng + 2 * ns]
        o = refs[2 * ng + 2 * ns:2 * ng + 2 * ns + n]
        tiles = refs[2 * ng + 2 * ns + n:2 * ng + 2 * ns + 2 * n]
        keep, send, recv = refs[2 * ng + 2 * ns + 2 * n:]
        x, y, c, _ = _coords()
        sibling = dict(device_id=(x, y, 1 - c), device_id_type=MESH)
        r = pl.program_id(0)

        def writes(i, step, slot):
            dst = o[i].at[pl.ds(c * halves[i] + step * rows[i], rows[i])]
            return (pltpu.make_async_copy(tiles[i].at[slot], dst, keep.at[i, slot]),
                    pltpu.make_async_remote_copy(src_ref=tiles[i].at[slot], dst_ref=dst, send_sem=send.at[i, slot],
                                                 recv_sem=recv.at[i, step], **sibling))

        def small_writes(i):
            dst = o[i].at[pl.ds(c * halves[i], halves[i])]
            return (pltpu.make_async_copy(tiles[i], dst, keep.at[i, 0]),
                    pltpu.make_async_remote_copy(src_ref=tiles[i], dst_ref=dst, send_sem=send.at[i, 0],
                                                 recv_sem=recv.at[i, 0], **sibling))

        def arriving(i, step, nrows):
            dst = o[i].at[pl.ds((1 - c) * halves[i] + step * nrows, nrows)]
            return pltpu.make_async_remote_copy(src_ref=dst, dst_ref=dst, send_sem=send.at[i, 0], recv_sem=recv.at[i, step],
                                                **sibling)

        def finish(step, slot):
            for i in range(ng):
                local, remote = writes(i, step, slot)
                local.wait()
                remote.wait_send()

        @pl.when(r >= 2)
        def _():
            finish(r - 2, r % 2)

        for i in range(ng):
            tot = own[i][...]
            for j in range(3):
                tot = tot + rx[i][j].astype(F32)
            tiles[i][r % 2] = tot
            for cp in writes(i, r, r % 2):
                cp.start()

        @pl.when(r == 0)
        def _():
            for i in range(ns):
                term = [jnp.where(bc_ref[0] == kk, sl_own[i][...], sl[i][kk]) for kk in range(N_CHIP)]
                tiles[ng + i][...] = ((term[0] + term[1]) + term[2]) + term[3]
                for cp in small_writes(ng + i):
                    cp.start()

        @pl.when(r == steps - 1)
        def _():
            finish(steps - 2, (steps - 2) % 2)
            finish(steps - 1, (steps - 1) % 2)
            for i in range(ns):
                local, remote = small_writes(ng + i)
                local.wait()
                remote.wait_send()
                arriving(ng + i, 0, halves[ng + i]).wait_recv()
            for i in range(ng):
                for step in range(steps):
                    arriving(i, step, rows[i]).wait_recv()

    in_specs = [pl.BlockSpec((None, rows[i], g.shape[2]), lambda r, bc: (bc[0], r, 0)) for i, g in enumerate(sums)]
    in_specs += [pl.BlockSpec((3, rows[i], g.shape[2]), lambda r, bc: (0, r, 0)) for i, g in enumerate(sums)]
    in_specs += [pl.BlockSpec(s.shape, lambda r, bc: (0, 0, 0)) for s in smalls_slots]
    in_specs += [pl.BlockSpec(s.shape[1:], lambda r, bc: (0, 0)) for s in smalls_slots]
    out_shape = [jax.ShapeDtypeStruct((2 * g.shape[1], g.shape[2]), F32) for g in sums]
    out_shape += [jax.ShapeDtypeStruct((2 * s.shape[1], s.shape[2]), F32) for s in smalls_slots]
    scratch = [pltpu.VMEM((2, rows[i], g.shape[2]), F32) for i, g in enumerate(sums)]
    scratch += [pltpu.VMEM(s.shape[1:], F32) for s in smalls_slots]
    scratch += [pltpu.SemaphoreType.DMA((n, 2)), pltpu.SemaphoreType.DMA((n, 2)), pltpu.SemaphoreType.DMA((n, steps))]
    return list(pl.pallas_call(
        body, out_shape=out_shape,
        grid_spec=pltpu.PrefetchScalarGridSpec(num_scalar_prefetch=1, grid=(steps,), in_specs=in_specs,
                                               out_specs=[ANY] * n, scratch_shapes=scratch),
        compiler_params=_cp(("arbitrary",), VMEM_LIMIT), name=name)(bc_idx, *sums, *recvd, *smalls_slots, *smalls_own))


def _adamw_math(w, g, m, v):
    m2 = ADAM_B1 * m + (1.0 - ADAM_B1) * g
    v2 = ADAM_B2 * v + (1.0 - ADAM_B2) * (g * g)
    m_hat = m2 / (1.0 - ADAM_B1 ** ADAM_STEP)
    v_hat = v2 / (1.0 - ADAM_B2 ** ADAM_STEP)
    delta = -ADAM_LR * (m_hat / (jnp.sqrt(v_hat) + ADAM_EPS) + ADAM_WD * w)
    return delta, m2, v2


def _adamw_big(ws, gs, ms, vs, name, steps=8):
    n = len(ws)

    def body(*refs):
        for i in range(n):
            w_ref, g_ref, m_ref, v_ref = (refs[k * n + i] for k in range(4))
            d_ref, m2_ref, v2_ref, g2_ref = (refs[(4 + k) * n + i] for k in range(4))
            gv = g_ref[...]
            d_ref[...], m2_ref[...], v2_ref[...] = _adamw_math(w_ref[...], gv, m_ref[...], v_ref[...])
            g2_ref[...] = gv

    specs = [pl.BlockSpec((w.shape[0] // steps, w.shape[1]), lambda i: (i, 0)) for w in ws]
    shapes = [jax.ShapeDtypeStruct(w.shape, F32) for w in ws]
    outs = pl.pallas_call(
        body, grid=(steps,), in_specs=specs * 4, out_specs=specs * 4, out_shape=shapes * 4,
        compiler_params=_cp(("parallel",), VMEM_LIMIT), name=name)(*ws, *gs, *ms, *vs)
    return [tuple(outs[k * n + i] for k in range(4)) for i in range(n)]


def _adamw_small(groups):
    n = len(groups)

    def body(*refs):
        for i in range(n):
            w_ref, g_ref, m_ref, v_ref = refs[4 * i:4 * i + 4]
            d_ref, m2_ref, v2_ref = refs[4 * n + 3 * i:4 * n + 3 * i + 3]
            d_ref[...], m2_ref[...], v2_ref[...] = _adamw_math(w_ref[...], g_ref[...], m_ref[...], v_ref[...])

    flat = [a for grp in groups for a in grp]
    out_shape = [jax.ShapeDtypeStruct(grp[0].shape, F32) for grp in groups for _ in range(3)]
    outs = pl.pallas_call(body, out_shape=out_shape, name="adamw_small")(*flat)
    return [tuple(outs[3 * i:3 * i + 3]) for i in range(n)]


def kernel(x, mem, norm_mix_g, w_in, conv_w, gm_ln_g, gm_ln_b, gm_ws, gm_bs, w_out, norm_x_g, norm_mem_g, w_q, w_kv, w_xo, norm_final_g, loss_target, m_norm_mix_g, m_w_in, m_conv_w, m_gm_ln_g, m_gm_ln_b, m_gm_ws, m_gm_bs, m_w_out, m_norm_x_g, m_norm_mem_g, m_w_q, m_w_kv, m_w_xo, m_norm_final_g, v_norm_mix_g, v_w_in, v_conv_w, v_gm_ln_g, v_gm_ln_b, v_gm_ws, v_gm_bs, v_w_out, v_norm_x_g, v_norm_mem_g, v_w_q, v_w_kv, v_w_xo, v_norm_final_g):
    t = x.shape[1]
    xi = lax.axis_index("x")
    yi = lax.axis_index("y")
    ci = lax.axis_index("c")
    b_idx = jnp.reshape(2 * xi + yi, (1,)).astype(jnp.int32)
    c_idx = jnp.reshape(ci, (1,)).astype(jnp.int32)

    x2d, mem2d, tgt = x[0], mem[0], loss_target[0]
    big = [w_in[0], w_out[0], w_q[0], w_kv[0], w_xo[0]]
    big_m = [m_w_in[0], m_w_out[0], m_w_q[0], m_w_kv[0], m_w_xo[0]]
    big_v = [v_w_in[0], v_w_out[0], v_w_q[0], v_w_kv[0], v_w_xo[0]]
    g3 = norm_final_g.reshape(1, D)

    def pad8(a):
        return jnp.pad(a, ((0, 8 - a.shape[0]), (0, 0)))

    own_blocks = _cast_shards(b_idx, big)

    tril = jnp.tril(jnp.ones((CH, CH), bool))
    wc32 = jnp.where(tril[None], gm_ws[0], 0.0)
    wc = wc32.astype(BF16)
    wct = jnp.swapaxes(wc32, 1, 2).astype(BF16)
    bsb = jnp.broadcast_to(gm_bs[0][:, :, None], (HEADS, CH, CH))

    blk = 2 * xi + yi
    near = [blk ^ (2 >> (k % 2)) for k in range(2 * N_PIECE)]
    seq_blk = jnp.stack([blk] * N_PIECE + near + [blk ^ 3] * N_PIECE)
    seq_piece = jnp.asarray(list(range(N_PIECE)) + [k // 2 for k in range(2 * N_PIECE)] + list(range(N_PIECE)))
    seq = jnp.stack([seq_blk, seq_piece, seq_blk * N_PIECE + seq_piece]).astype(jnp.int32)
    proj, hb, win_f, cw8, (wq_f,) = _proj_gather(
        seq, x2d, norm_mix_g, own_blocks[0], pad8(conv_w[0]), [own_blocks[2]])
    mixin, (wout_f, wkv_f, wxo_f) = _mixer_fwd(
        proj, cw8, gm_ln_g, gm_ln_b, wc, bsb, [own_blocks[1], own_blocks[3], own_blocks[4]])
    wout2, wq2, wxo2 = wout_f.reshape(MIX, D), wq_f.reshape(D, D), wxo_f.reshape(D, D)
    k, v = _mem_fwd(mem2d, norm_mem_g, wkv_f)

    (loss_tile, dmix, dx1b, h2b, dq, ob, dx2b, dk, dv, dg2, dg3) = _tail(
        x2d, tgt, mixin, wout2, wq2, wxo2, k, v, norm_x_g, g3)
    dwkv, dwkv_b, dgm = _mem_bwd(mem2d, norm_mem_g, dk, dv, wkv_f)
    dproj, dcw, dlng, dlnb, dwc, dbs8, grad_x, dg1 = _mixer_bwd(
        proj, dmix, cw8, gm_ln_g, gm_ln_b, wc, wct, bsb, win_f, x2d, dx1b, norm_mix_g)

    bc_idx = jnp.concatenate([b_idx, c_idx])
    zero = jnp.zeros((1, D), F32)
    loss_row = jnp.broadcast_to(loss_tile[0:1, 0:1], (1, D))
    sv = jnp.concatenate([dg1[0:1], dg2, dgm, dg3, dlng, dlnb, dbs8[0:1], loss_row, dcw], axis=0)
    sw = dwc.reshape(HEADS * CH, CH)
    dwin_sum, dwin_sum_b, psmall = _grad_matmul_pair(c_idx, hb, dproj, [sv, sw], dgm, name="grad_w_in")
    sums_b = [dwin_sum]
    send_b, recv_b, src_b, land_b, token_b = _exchange_begin([dwin_sum_b], psmall, "exchange_b_begin")

    dwxo, dwxo_b = _grad_matmul(ob, dx2b, token_b, by_cols=False, name="grad_w_xo", tk=2048)
    dwq, dwq_b = _grad_matmul(h2b, dq, token_b, by_cols=False, name="grad_w_q", tk=2048)
    dwout, dwout_b = _grad_matmul(mixin, dx1b, token_b, by_cols=False, name="grad_w_out")
    ps_a = _pair_reduce(c_idx, [dwout, dwkv, dwq, dwxo], [dwout_b, dwkv_b, dwq_b, dwxo_b], [], "pair_reduce_a")
    sums_a, sums_a_b = list(ps_a[:4]), list(ps_a[4:8])
    send_a, recv_a, src_a, land_a, token_a = _exchange_begin(sums_a_b, [], "exchange_a_begin")

    _, rx2b = _exchange_end(send_b, recv_b, src_b, land_b, 1, [0, 1, 2], [token_a], "exchange_b_end")
    gwin, svf, swf = _chip_reduce(bc_idx, sums_b, rx2b[:1], rx2b[1:], psmall, "chip_reduce_b")
    out_b = _adamw_big(big[:1], [gwin], big_m[:1], big_v[:1], "adamw_w_in")[0]

    def vec_pack(a1, a2, am, a3, lg, lb, bs):
        return jnp.concatenate([a1, a2, am, a3.reshape(1, D), lg, lb, bs.reshape(1, D), zero], axis=0)

    wv = vec_pack(norm_mix_g, norm_x_g, norm_mem_g, norm_final_g, gm_ln_g, gm_ln_b, gm_bs)
    mv = vec_pack(m_norm_mix_g, m_norm_x_g, m_norm_mem_g, m_norm_final_g, m_gm_ln_g, m_gm_ln_b, m_gm_bs)
    vv = vec_pack(v_norm_mix_g, v_norm_x_g, v_norm_mem_g, v_norm_final_g, v_gm_ln_g, v_gm_ln_b, v_gm_bs)
    loss = svf[7, 0]
    gcw = lax.dynamic_slice_in_dim(svf[8:16], blk * (D // N_CHIP), D // N_CHIP, axis=1)
    gws = swf
    gv = svf[0:8]
    (dv_, mv_, vv_), (dc_, mc_, vc_), (dws_, mws_, vws_) = _adamw_small([
        (wv, gv, mv, vv),
        (pad8(conv_w[0]), gcw, pad8(m_conv_w[0]), pad8(v_conv_w[0])),
        (gm_ws.reshape(HEADS * CH, CH), gws, m_gm_ws.reshape(HEADS * CH, CH), v_gm_ws.reshape(HEADS * CH, CH))])

    def finish_a(part, src, land, after, tag):
        src, land = _exchange_end(send_a, recv_a, src, land, 4, part, after, "exchange_a%s_end" % tag)
        grads = _chip_reduce(bc_idx, [sums_a[i] for i in part], [land[i] for i in part], [], [], "chip_reduce_a" + tag)
        ids = [(1, 3, 2, 4)[i] for i in part]
        outs = _adamw_big([big[i] for i in ids], grads, [big_m[i] for i in ids], [big_v[i] for i in ids], "adamw_a" + tag)
        return src, land, outs

    src_a, land_a, (out_wout, out_wkv) = finish_a([0, 1], src_a, land_a, [out_b[0], dv_], "1")
    _, _, (out_wq, out_wxo) = finish_a([2, 3], src_a, land_a, [out_wout[0]], "2")
    big_out = [out_b, out_wout, out_wq, out_wkv, out_wxo]

    def unpack(vecs, cw, ws, bigs):
        r = lambda i: vecs[i:i + 1]
        return [r(0), bigs[0][None], cw[0:3][None], r(4), r(5), ws.reshape(1, HEADS, CH, CH), vecs[6].reshape(1, HEADS, CH),
                bigs[1][None], r(1), r(2), bigs[2][None], bigs[3][None], bigs[4][None], vecs[3]]

    grads_out = unpack(gv, gcw, gws, [o[3] for o in big_out])
    delta_out = unpack(dv_, dc_, dws_, [o[0] for o in big_out])
    m_out = unpack(mv_, mc_, mws_, [o[1] for o in big_out])
    v_out = unpack(vv_, vc_, vws_, [o[2] for o in big_out])
    return (loss, grad_x[None], *grads_out, *delta_out, *m_out, *v_out)
```

```python
import functools
import math

import jax
import jax.numpy as jnp
from jax import lax
from jax.experimental import pallas as pl
from jax.experimental.pallas import tpu as pltpu

F32 = jnp.float32
BF16 = jnp.bfloat16
MESH = pl.DeviceIdType.MESH

D = 1024
SLAB = 1024
N_SLAB = 7
IN_DIM = N_SLAB * SLAB
MIX = 2 * SLAB
HEADS = 8
CH = 128
XH = 4
XD = D // XH
EPS = 1e-6
GELU_C = math.sqrt(2.0 / math.pi)
GELU_A = 0.044715
N_CHIP = 4
IN_BLK = IN_DIM // N_CHIP
IN_PIECE = 256
N_PIECE = IN_BLK // IN_PIECE
KV_BLK = 2 * D // N_CHIP

ADAM_LR, ADAM_B1, ADAM_B2, ADAM_EPS, ADAM_WD, ADAM_STEP = 0.001, 0.9, 0.999, 1e-08, 0.01, 10

VMEM_LIMIT = 60 * 1024 * 1024


def _cp(sem=None, vmem=None):
    return pltpu.CompilerParams(dimension_semantics=sem, vmem_limit_bytes=vmem)


def _full(shape, buffers=None):
    n = len(shape)
    if buffers is None:
        return pl.BlockSpec(shape, lambda *_: (0,) * n)
    return pl.BlockSpec(shape, lambda *_: (0,) * n, pipeline_mode=pl.Buffered(buffers))


ANY = pl.BlockSpec(memory_space=pl.ANY)


def _bdot(a, b):
    return jnp.dot(a.astype(BF16), b.astype(BF16), preferred_element_type=F32)


def _bdot_nt(a, b):
    return lax.dot_general(a.astype(BF16), b.astype(BF16), (((1,), (1,)), ((), ())), preferred_element_type=F32)


def _bdot_tn(a, b):
    return lax.dot_general(a.astype(BF16), b.astype(BF16), (((0,), (0,)), ((), ())), preferred_element_type=F32)


def _rms(x, g):
    r = lax.rsqrt(jnp.mean(x * x, axis=-1, keepdims=True) + EPS)
    return x * r * g, r


def _rms_bwd(dy, x, r, g):
    gdy = dy * g
    dx = r * gdy - x * (r * r * r) * jnp.mean(x * gdy, axis=-1, keepdims=True)
    dg = jnp.sum(dy * x * r, axis=0, keepdims=True)
    return dx, dg


def _gelu_parts(x):
    x2 = x * x
    t = jnp.tanh(GELU_C * (x + GELU_A * x * x2))
    val = 0.5 * x * (1.0 + t)
    grad = 0.5 * (1.0 + t) + 0.5 * x * (1.0 - t * t) * (GELU_C * (1.0 + 3.0 * GELU_A * x2))
    return val, grad


def _gelu(x):
    return 0.5 * x * (1.0 + jnp.tanh(GELU_C * (x + GELU_A * x * x * x)))


def _sigmoid(z):
    return 1.0 / (1.0 + jnp.exp(-z))


def _cast_shards(b_idx, arrs):
    n = len(arrs)
    steps = 8

    def body(b_ref, *refs):
        for p in range(N_PIECE):
            refs[n][p] = refs[0][:, pl.ds(p * IN_PIECE, IN_PIECE)].astype(BF16)
        for i in range(1, n):
            refs[n + i][...] = refs[i][...].astype(BF16)

    rows = [a.shape[0] // steps for a in arrs]
    in_specs = [pl.BlockSpec((rows[i], a.shape[1]), lambda i, b: (i, 0)) for i, a in enumerate(arrs)]
    out_specs = [pl.BlockSpec((None, N_PIECE, rows[0], IN_PIECE), lambda i, b: (b[0], 0, i, 0))]
    out_specs += [pl.BlockSpec((None, rows[i], a.shape[1]), lambda i, b: (b[0], i, 0)) for i, a in enumerate(arrs) if i > 0]
    out_shape = [jax.ShapeDtypeStruct((N_CHIP, N_PIECE, arrs[0].shape[0], IN_PIECE), BF16)]
    out_shape += [jax.ShapeDtypeStruct((N_CHIP,) + a.shape, BF16) for a in arrs[1:]]
    return pl.pallas_call(
        body, out_shape=out_shape,
        grid_spec=pltpu.PrefetchScalarGridSpec(num_scalar_prefetch=1, grid=(steps,), in_specs=in_specs, out_specs=out_specs),
        compiler_params=_cp(("arbitrary",)), name="cast_shards")(b_idx, *arrs)


def _proj_gather(seq, x, g, win_own, cw8s, more, tm=1024):
    t = x.shape[0]
    ni = t // tm
    nm = len(more)
    steps = N_CHIP * N_PIECE
    near0, far0 = N_PIECE, 3 * N_PIECE

    def body(*refs):
        seq_ref, x_any, g_ref, win_in, cw_in = refs[:5]
        o_ref, hb_any, win_f, cw_out = refs[5 + nm:9 + nm]
        more_out = refs[9 + nm:9 + 2 * nm]
        hbuf, xbuf, wv, cw_s, cw_r, loc = refs[9 + 2 * nm:15 + 2 * nm]
        g_in = _Gather([win_f.at[:, p] for p in range(N_PIECE)], *refs[15 + 2 * nm:19 + 2 * nm])
        g_more = _Gather(more_out, *refs[19 + 2 * nm:23 + 2 * nm])
        s = pl.program_id(0)
        x, y, c, chips = _coords()
        b = 2 * x + y
        blks = [2 * chip[0] + chip[1] for chip in chips]

        def cw_cols(blk):
            return cw_out.at[:, pl.ds(blk * (D // N_CHIP), D // N_CHIP)]

        def cw_copy(k, blk):
            src = cw_in if blk is None else cw_cols(blk)
            return pltpu.make_async_remote_copy(src_ref=src, dst_ref=cw_cols(b if blk is None else blk), send_sem=cw_s.at[k],
                                                recv_sem=cw_r.at[k], device_id=(*chips[k], c), device_id_type=MESH)

        cw_local = pltpu.make_async_copy(cw_in, cw_cols(b), loc.at[1])
        hb_copy = pltpu.make_async_copy(hbuf, hb_any, loc.at[0])

        def load(step):
            slot = lax.rem(step, 2)
            return pltpu.make_async_copy(win_f.at[seq_ref[0, step], seq_ref[1, step]], wv.at[slot], loc.at[2 + slot])

        def chunk(i):
            return pltpu.make_async_copy(x_any.at[pl.ds(i * tm, tm)], xbuf.at[i % 2], loc.at[4 + i % 2])

        def first():
            g_in.start()
            cw_local.start()
            for k in range(3):
                cw_copy(k, None).start()
            load(0).start()
            chunk(0).start()
            for i in range(ni):
                if i + 1 < ni:
                    chunk(i + 1).start()
                chunk(i).wait()
                h, _ = _rms(xbuf[i % 2], g_ref[...])
                hbuf[pl.ds(i * tm, tm), :] = h.astype(BF16)
            hb_copy.start()

        events = {step: [] for step in range(steps)}
        events[0].append(first)
        for p in range(N_PIECE):
            events[2 * p + 2].append(functools.partial(g_in.hop, [p]))
            events[near0 + 2 * p - 1].append(functools.partial(g_in.near_ready, [p]))
            events[far0 + p - 2].append(functools.partial(g_in.far, [p]))
            events[far0 + p - 1].append(functools.partial(g_in.far_ready, [p]))
        events[2 * N_PIECE + 1].append(g_more.start)
        for step, todo in events.items():
            if todo:
                @pl.when(s == step)
                def _(todo=todo):
                    for do in todo:
                        do()

        @pl.when(s + 1 < steps)
        def _():
            load(s + 1).start()

        load(s).wait()
        for i in range(ni):
            rows = pl.ds(i * tm, tm)
            o_ref[rows, :] = jnp.dot(hbuf[rows, :], wv[lax.rem(s, 2)], preferred_element_type=F32).astype(BF16)

        @pl.when(s == steps - 1)
        def _():
            g_more.hop()
            g_more.far()
            for k in range(3):
                cw_copy(k, blks[k]).wait_recv()
            for k in range(3):
                cw_copy(k, None).wait_send()
            cw_local.wait()
            hb_copy.wait()
            g_more.near_ready()
            g_more.far_ready()
            g_in.drain()
            g_more.drain()

    in_specs = [ANY, pl.BlockSpec((1, D), lambda s, q: (0, 0)), ANY, ANY] + [ANY] * nm
    out_specs = [pl.BlockSpec((t, IN_PIECE), lambda s, q: (0, q[2, s])), ANY, ANY, ANY] + [ANY] * nm
    outs = pl.pallas_call(
        body, out_shape=[jax.ShapeDtypeStruct((t, IN_DIM), BF16), jax.ShapeDtypeStruct((t, D), BF16),
                         jax.ShapeDtypeStruct(win_own.shape, BF16), jax.ShapeDtypeStruct((8, D), F32)]
        + [jax.ShapeDtypeStruct(f.shape, f.dtype) for f in more],
        grid_spec=pltpu.PrefetchScalarGridSpec(
            num_scalar_prefetch=1, grid=(steps,), in_specs=in_specs, out_specs=out_specs,
            scratch_shapes=[pltpu.VMEM((t, D), BF16), pltpu.VMEM((2, tm, D), F32), pltpu.VMEM((2, D, IN_PIECE), BF16)]
            + [pltpu.SemaphoreType.DMA((3,))] * 2 + [pltpu.SemaphoreType.DMA((6,))]
            + _gather_sems(N_PIECE) + _gather_sems(nm)),
        input_output_aliases={3: 2, **{5 + w: 4 + w for w in range(nm)}},
        compiler_params=_cp(("arbitrary",), VMEM_LIMIT), name="proj_gather")(seq, x, g, win_own, cw8s, *more)
    return outs[0], outs[1], outs[2], outs[3], outs[4:]


class _Gather:
    def __init__(self, outs, ici_s, ici_r, d2d_s, d2d_r):
        x, y, c, _ = _coords()
        self.outs, self.c = outs, c
        self.sems = ici_s, ici_r, d2d_s, d2d_r
        self.b, self.bx, self.by, self.bd = 2 * x + y, 2 * (1 - x) + y, 2 * x + (1 - y), 2 * (1 - x) + (1 - y)
        self.xn, self.yn, self.sib = (1 - x, y, c), (x, 1 - y, c), (x, y, 1 - c)

    def piece(self, w, blk, hc, quarter=None):
        hr = self.outs[w].shape[1] // 2
        if quarter is None:
            return self.outs[w].at[blk, pl.ds(hc * hr, hr)]
        return self.outs[w].at[blk, pl.ds(hc * hr + quarter * (hr // 2), hr // 2)]

    def ici(self, w, k, ref, to):
        return pltpu.make_async_remote_copy(src_ref=ref, dst_ref=ref, send_sem=self.sems[0].at[w, k],
                                            recv_sem=self.sems[1].at[w, k], device_id=to, device_id_type=MESH)

    def d2d(self, w, k, ref):
        return pltpu.make_async_remote_copy(src_ref=ref, dst_ref=ref, send_sem=self.sems[2].at[w, k],
                                            recv_sem=self.sems[3].at[w, k], device_id=self.sib, device_id_type=MESH)

    def all(self):
        return range(len(self.outs))

    def start(self):
        for w in self.all():
            mine = self.piece(w, self.b, self.c)
            self.ici(w, 0, mine, self.xn).start()
            self.ici(w, 1, mine, self.yn).start()

    def hop(self, ws=None):
        c = self.c
        for w in ws or self.all():
            self.ici(w, 0, self.piece(w, self.bx, c), self.xn).wait_recv()
            self.ici(w, 1, self.piece(w, self.by, c), self.yn).wait_recv()
            self.ici(w, 2, self.piece(w, self.bx, c, 0), self.yn).start()
            self.ici(w, 3, self.piece(w, self.by, c, 1), self.xn).start()
            self.d2d(w, 0, self.piece(w, self.bx, c)).start()
            self.d2d(w, 1, self.piece(w, self.by, c)).start()

    def near_ready(self, ws=None):
        for w in ws or self.all():
            self.d2d(w, 0, self.piece(w, self.bx, 1 - self.c)).wait_recv()
            self.d2d(w, 1, self.piece(w, self.by, 1 - self.c)).wait_recv()

    def far(self, ws=None):
        c = self.c
        for w in ws or self.all():
            self.ici(w, 2, self.piece(w, self.bd, c, 0), self.yn).wait_recv()
            self.ici(w, 3, self.piece(w, self.bd, c, 1), self.xn).wait_recv()
            self.d2d(w, 2, self.piece(w, self.bd, c, 0)).start()
            self.d2d(w, 3, self.piece(w, self.bd, c, 1)).start()

    def far_ready(self, ws=None):
        for w in ws or self.all():
            self.d2d(w, 2, self.piece(w, self.bd, 1 - self.c, 0)).wait_recv()
            self.d2d(w, 3, self.piece(w, self.bd, 1 - self.c, 1)).wait_recv()

    def drain(self):
        c = self.c
        for w in self.all():
            mine = self.piece(w, self.b, c)
            self.ici(w, 0, mine, self.xn).wait_send()
            self.ici(w, 1, mine, self.yn).wait_send()
            self.ici(w, 2, self.piece(w, self.bx, c, 0), self.yn).wait_send()
            self.ici(w, 3, self.piece(w, self.by, c, 1), self.xn).wait_send()
            self.d2d(w, 0, self.piece(w, self.bx, c)).wait_send()
            self.d2d(w, 1, self.piece(w, self.by, c)).wait_send()
            self.d2d(w, 2, self.piece(w, self.bd, c, 0)).wait_send()
            self.d2d(w, 3, self.piece(w, self.bd, c, 1)).wait_send()


def _gather_sems(nw):
    return [pltpu.SemaphoreType.DMA((max(nw, 1), 4))] * 4


def _mixer_fwd(proj, cw8, lng, lnb, wc, bsb, fulls, tm=256):
    t = proj.shape[0]
    nt = t // tm
    nch = tm // CH
    nw = len(fulls)

    def body(*refs):
        p_ref, cw_ref, lng_ref, lnb_ref, wc_ref, bsb_ref = refs[:6]
        mix_ref = refs[6 + nw]
        w_outs = refs[7 + nw:7 + 2 * nw]
        prev_ref = refs[7 + 2 * nw]
        gather = _Gather(w_outs, *refs[8 + 2 * nw:])

        @pl.when(pl.program_id(0) == 0)
        def _():
            gather.start()
            prev_ref[...] = jnp.zeros_like(prev_ref)

        @pl.when(pl.program_id(0) == nt // 2)
        def _():
            gather.hop()

        @pl.when(pl.program_id(0) == nt - 1)
        def _():
            gather.far()

        rows = lax.broadcasted_iota(jnp.int32, (tm, CH), 0)
        for s in range(HEADS):
            cs = pl.ds(CH * s, CH)

            def slab(k):
                return p_ref[:, pl.ds(k * SLAB + CH * s, CH)].astype(F32)

            gb, gc, xa, za = slab(0), slab(1), slab(2), slab(3)
            cx = gc * xa
            p6 = jnp.broadcast_to(prev_ref[6:7, cs], (tm, CH))
            p7 = jnp.broadcast_to(prev_ref[7:8, cs], (tm, CH))
            c1 = jnp.where(rows == 0, p7, pltpu.roll(cx, 1, 0))
            c2 = jnp.where(rows == 0, p6, jnp.where(rows == 1, p7, pltpu.roll(cx, 2, 0)))
            prev_ref[:, cs] = cx[tm - 8:, :]
            cv = cw_ref[0:1, cs] * c2 + cw_ref[1:2, cs] * c1 + cw_ref[2:3, cs] * cx
            mix_ref[:, cs] = (gb * cv * (za * _sigmoid(za))).astype(BF16)

            u, v, zb = slab(4), slab(5), slab(6)
            ug, vg = _gelu(u), _gelu(v)
            dlt = vg - jnp.mean(vg, axis=-1, keepdims=True)
            vhat = dlt * lax.rsqrt(jnp.mean(dlt * dlt, axis=-1, keepdims=True) + EPS)
            vn = (vhat * lng_ref[:, cs] + lnb_ref[:, cs]).astype(BF16)
            gate = ug * (zb * _sigmoid(zb))
            for c in range(nch):
                rs = slice(CH * c, CH * (c + 1))
                sp = jnp.dot(wc_ref[s], vn[rs], preferred_element_type=F32) + bsb_ref[s]
                mix_ref[rs, pl.ds(SLAB + CH * s, CH)] = (gate[rs] * sp).astype(BF16)

        @pl.when(pl.program_id(0) == nt - 1)
        def _():
            gather.near_ready()
            gather.far_ready()
            gather.drain()

    sems = _gather_sems(nw)
    outs = pl.pallas_call(
        body, grid=(nt,),
        in_specs=[pl.BlockSpec((tm, IN_DIM), lambda i: (i, 0)), _full((8, D)), _full((1, D)), _full((1, D)),
                  _full((HEADS, CH, CH)), _full((HEADS, CH, CH))] + [ANY] * nw,
        out_specs=[pl.BlockSpec((tm, MIX), lambda i: (i, 0))] + [ANY] * nw,
        out_shape=[jax.ShapeDtypeStruct((t, MIX), BF16)] + [jax.ShapeDtypeStruct(f.shape, f.dtype) for f in fulls],
        input_output_aliases={6 + w: 1 + w for w in range(nw)},
        scratch_shapes=[pltpu.VMEM((8, D), F32)] + sems,
        compiler_params=_cp(("arbitrary",), VMEM_LIMIT), name="mixer_fwd")(proj, cw8, lng, lnb, wc, bsb, *fulls)
    return outs[0], outs[1:]


def _mem_fwd(mem, gm, wkv_f):
    n_mem = mem.shape[0]

    def body(mem_ref, gm_ref, w_ref, k_ref, v_ref):
        m, _ = _rms(mem_ref[...], gm_ref[...])
        mb = m.astype(BF16)
        for j in range(N_CHIP):
            dst = k_ref if j < 2 else v_ref
            dst[:, pl.ds(KV_BLK * (j % 2), KV_BLK)] = jnp.dot(mb, w_ref[j], preferred_element_type=F32).astype(BF16)

    return pl.pallas_call(
        body, out_shape=[jax.ShapeDtypeStruct((n_mem, D), BF16), jax.ShapeDtypeStruct((n_mem, D), BF16)],
        compiler_params=_cp(None, VMEM_LIMIT), name="mem_fwd")(mem, gm, wkv_f)


def _tail(x, tgt, mixin, wout, wq, wxo, k, v, g2, g3, tm=512, sub=512):
    t = x.shape[0]
    n_mem = k.shape[0]
    scale = 1.0 / math.sqrt(XD)

    def body(x_ref, tgt_ref, mix_ref, wout_ref, wq_ref, wxo_ref, k_ref, v_ref, g2_ref, g3_ref,
             loss_ref, dmix_ref, dx1b_ref, h2_ref, dq_ref, o_ref, dx2b_ref, dk_ref, dv_ref, dg2_ref, dg3_ref):
        @pl.when(pl.program_id(0) == 0)
        def _():
            loss_ref[...] = jnp.zeros_like(loss_ref)
            dk_ref[...] = jnp.zeros_like(dk_ref)
            dv_ref[...] = jnp.zeros_like(dv_ref)
            dg2_ref[...] = jnp.zeros_like(dg2_ref)
            dg3_ref[...] = jnp.zeros_like(dg3_ref)

        g2, g3 = g2_ref[...], g3_ref[...]
        for sb in range(tm // sub):
            rs = pl.ds(sub * sb, sub)
            x1 = x_ref[rs, :] + jnp.dot(mix_ref[rs, :], wout_ref[...], preferred_element_type=F32)
            h2, r2 = _rms(x1, g2)
            h2b = h2.astype(BF16)
            h2_ref[rs, :] = h2b
            q = jnp.dot(h2b, wq_ref[...], preferred_element_type=F32).astype(BF16)
            probs, outs = [], []
            for hd in range(XH):
                hs = pl.ds(XD * hd, XD)
                s = _bdot_nt(q[:, XD * hd:XD * (hd + 1)], k_ref[:, hs]) * scale
                e = jnp.exp(s - jnp.max(s, axis=-1, keepdims=True))
                p = e / jnp.sum(e, axis=-1, keepdims=True)
                probs.append(p)
                outs.append(_bdot(p, v_ref[:, hs]))
            ob = jnp.concatenate(outs, axis=-1).astype(BF16)
            o_ref[rs, :] = ob
            x2 = x1 + jnp.dot(ob, wxo_ref[...], preferred_element_type=F32)
            y, r3 = _rms(x2, g3)
            diff = y - tgt_ref[rs, :]
            row_loss = jnp.sum(diff * diff, axis=-1, keepdims=True)
            loss_ref[...] += jnp.broadcast_to(jnp.sum(row_loss, axis=0, keepdims=True) * (0.5 / D), loss_ref.shape)

            dx2, dg3 = _rms_bwd(diff * (1.0 / D), x2, r3, g3)
            dg3_ref[...] += dg3
            dx2b = dx2.astype(BF16)
            dx2b_ref[rs, :] = dx2b
            do = _bdot_nt(dx2b, wxo_ref[...])
            dqs = []
            for hd in range(XH):
                hs = pl.ds(XD * hd, XD)
                p = probs[hd]
                do_h = do[:, XD * hd:XD * (hd + 1)]
                dv_ref[:, hs] += _bdot_tn(p, do_h)
                dp = _bdot_nt(do_h, v_ref[:, hs])
                ds = p * (dp - jnp.sum(dp * p, axis=-1, keepdims=True))
                dqs.append(_bdot(ds, k_ref[:, hs]) * scale)
                dk_ref[:, hs] += _bdot_tn(ds, q[:, XD * hd:XD * (hd + 1)]) * scale
            dq = jnp.concatenate(dqs, axis=-1).astype(BF16)
            dq_ref[rs, :] = dq
            dx1n, dg2 = _rms_bwd(_bdot_nt(dq, wq_ref[...]), x1, r2, g2)
            dg2_ref[...] += dg2
            dx1b = (dx2 + dx1n).astype(BF16)
            dx1b_ref[rs, :] = dx1b
            dmix_ref[rs, :] = _bdot_nt(dx1b, wout_ref[...]).astype(BF16)

    tok = lambda w: pl.BlockSpec((tm, w), lambda i: (i, 0))
    return pl.pallas_call(
        body, grid=(t // tm,),
        in_specs=[tok(D), tok(D), tok(MIX), _full((MIX, D), 1), _full((D, D), 1), _full((D, D), 1),
                  _full((n_mem, D), 1), _full((n_mem, D), 1), _full((1, D)), _full((1, D))],
        out_specs=[_full((8, 128)), tok(MIX), tok(D), tok(D), tok(D), tok(D), tok(D),
                   _full((n_mem, D)), _full((n_mem, D)), _full((1, D)), _full((1, D))],
        out_shape=[jax.ShapeDtypeStruct((8, 128), F32), jax.ShapeDtypeStruct((t, MIX), BF16),
                   jax.ShapeDtypeStruct((t, D), BF16),
                   jax.ShapeDtypeStruct((t, D), BF16), jax.ShapeDtypeStruct((t, D), BF16),
                   jax.ShapeDtypeStruct((t, D), BF16), jax.ShapeDtypeStruct((t, D), BF16),
                   jax.ShapeDtypeStruct((n_mem, D), F32), jax.ShapeDtypeStruct((n_mem, D), F32),
                   jax.ShapeDtypeStruct((1, D), F32), jax.ShapeDtypeStruct((1, D), F32)],
        compiler_params=_cp(("arbitrary",), VMEM_LIMIT), name="tail")(x, tgt, mixin, wout, wq, wxo, k, v, g2, g3)


def _mem_bwd(mem, gm, dk, dv, wkv_f):
    def body(mem_ref, gm_ref, dk_ref, dv_ref, w_ref, dw_ref, dwb_ref, dgm_ref):
        mem_v = mem_ref[...]
        m, rm = _rms(mem_v, gm_ref[...])
        mb = m.astype(BF16)
        dm = jnp.zeros_like(mem_v)
        for j in range(N_CHIP):
            src = dk_ref if j < 2 else dv_ref
            dkv = src[:, pl.ds(KV_BLK * (j % 2), KV_BLK)].astype(BF16)
            dw = _bdot_tn(mb, dkv)
            dw_ref[j] = dw
            dwb_ref[j] = dw.astype(BF16)
            dm = dm + _bdot_nt(dkv, w_ref[j])
        dgm_ref[...] = jnp.sum(dm * mem_v * rm, axis=0, keepdims=True)

    return pl.pallas_call(
        body, out_shape=[jax.ShapeDtypeStruct((N_CHIP, D, KV_BLK), F32), jax.ShapeDtypeStruct((N_CHIP, D, KV_BLK), BF16),
                         jax.ShapeDtypeStruct((1, D), F32)],
        compiler_params=_cp(None, VMEM_LIMIT), name="mem_bwd")(mem, gm, dk, dv, wkv_f)


def _mixer_bwd(proj, dmix, cw8, lng, lnb, wc, wct, bsb, win_f, x, dx1, g1, tm=256):
    t = proj.shape[0]
    nt = t // tm
    nch = tm // CH
    hb = 16
    pair = 2 * CH
    assert pair == IN_PIECE

    def body(p_ref, pgc_ref, pxa_ref, dm_ref, cw_ref, lng_ref, lnb_ref, wc_ref, wct_ref, bsb_ref, w_ref, x_ref,
             dx1_ref, g1_ref, dp_ref, dcw_ref, dlng_ref, dlnb_ref, dwc_ref, dbs_ref, gx_ref, dg1_ref,
             next_ref, dh_ref):
        i = pl.program_id(0)

        @pl.when(i == 0)
        def _():
            next_ref[...] = jnp.zeros_like(next_ref)
            dcw_ref[...] = jnp.zeros_like(dcw_ref)
            dlng_ref[...] = jnp.zeros_like(dlng_ref)
            dlnb_ref[...] = jnp.zeros_like(dlnb_ref)
            dwc_ref[...] = jnp.zeros_like(dwc_ref)
            dbs_ref[...] = jnp.zeros_like(dbs_ref)
            dg1_ref[...] = jnp.zeros_like(dg1_ref)

        first_tile = i == nt - 1
        rows = lax.broadcasted_iota(jnp.int32, (tm, CH), 0)
        ones8 = jnp.ones((8, CH), BF16)
        for s in range(HEADS):
            cs = pl.ds(CH * s, CH)

            def slab(k):
                return p_ref[:, pl.ds(k * SLAB + CH * s, CH)].astype(F32)

            gb, gc, xa, za = slab(0), slab(1), slab(2), slab(3)
            da = dm_ref[:, cs].astype(F32)
            cx = gc * xa
            cxp = pgc_ref[:, cs].astype(F32) * pxa_ref[:, cs].astype(F32)
            cxp = jnp.where(first_tile, jnp.zeros_like(cxp), cxp)
            p6 = jnp.broadcast_to(cxp[hb - 2:hb - 1, :], (tm, CH))
            p7 = jnp.broadcast_to(cxp[hb - 1:hb, :], (tm, CH))
            c1 = jnp.where(rows == 0, p7, pltpu.roll(cx, 1, 0))
            c2 = jnp.where(rows == 0, p6, jnp.where(rows == 1, p7, pltpu.roll(cx, 2, 0)))
            w0, w1, w2 = cw_ref[0:1, cs], cw_ref[1:2, cs], cw_ref[2:3, cs]
            cv = w0 * c2 + w1 * c1 + w2 * cx
            sg = _sigmoid(za)
            sa = za * sg
            dcv = da * gb * sa
            dp_ref[:, pl.ds(0 * SLAB + CH * s, CH)] = (da * cv * sa).astype(BF16)
            dp_ref[:, pl.ds(3 * SLAB + CH * s, CH)] = (da * gb * cv * (sg * (1.0 + za * (1.0 - sg)))).astype(BF16)
            n0 = jnp.broadcast_to(next_ref[0:1, cs], (tm, CH))
            n1 = jnp.broadcast_to(next_ref[1:2, cs], (tm, CH))
            u1 = jnp.where(rows == tm - 1, n0, pltpu.roll(dcv, tm - 1, 0))
            u2 = jnp.where(rows == tm - 2, n0, jnp.where(rows == tm - 1, n1, pltpu.roll(dcv, tm - 2, 0)))
            next_ref[:, cs] = dcv[0:8, :]
            dcx = w2 * dcv + w1 * u1 + w0 * u2
            dp_ref[:, pl.ds(1 * SLAB + CH * s, CH)] = (dcx * xa).astype(BF16)
            dp_ref[:, pl.ds(2 * SLAB + CH * s, CH)] = (dcx * gc).astype(BF16)
            dcw_ref[0:1, cs] += jnp.sum(dcv * c2, axis=0, keepdims=True)
            dcw_ref[1:2, cs] += jnp.sum(dcv * c1, axis=0, keepdims=True)
            dcw_ref[2:3, cs] += jnp.sum(dcv * cx, axis=0, keepdims=True)

            u, v, zb = slab(4), slab(5), slab(6)
            db = dm_ref[:, pl.ds(SLAB + CH * s, CH)].astype(F32)
            ug, ugrad = _gelu_parts(u)
            vg, vgrad = _gelu_parts(v)
            dlt = vg - jnp.mean(vg, axis=-1, keepdims=True)
            rstd = lax.rsqrt(jnp.mean(dlt * dlt, axis=-1, keepdims=True) + EPS)
            vhat = dlt * rstd
            lg = lng_ref[:, cs]
            vn = (vhat * lg + lnb_ref[:, cs]).astype(BF16)
            sgb = _sigmoid(zb)
            szb = zb * sgb
            sps, dvns = [], []
            dbs = jnp.zeros((8, CH), F32)
            dwc = jnp.zeros((CH, CH), F32)
            for c in range(nch):
                rs = slice(CH * c, CH * (c + 1))
                sp = jnp.dot(wc_ref[s], vn[rs], preferred_element_type=F32) + bsb_ref[s]
                dsp = (db[rs] * ug[rs] * szb[rs]).astype(BF16)
                dbs = dbs + lax.dot_general(ones8, dsp, (((1,), (1,)), ((), ())), preferred_element_type=F32)
                dwc = dwc + lax.dot_general(dsp, vn[rs], (((1,), (1,)), ((), ())), preferred_element_type=F32)
                dvns.append(jnp.dot(wct_ref[s], dsp, preferred_element_type=F32))
                sps.append(sp)
            sp = jnp.concatenate(sps, axis=0)
            dvn = jnp.concatenate(dvns, axis=0)
            dbs_ref[:, cs] += dbs
            dwc_ref[s] += dwc
            dlng_ref[:, cs] += jnp.sum(dvn * vhat, axis=0, keepdims=True)
            dlnb_ref[:, cs] += jnp.sum(dvn, axis=0, keepdims=True)
            dvhat = dvn * lg
            dvg = rstd * (dvhat - jnp.mean(dvhat, axis=-1, keepdims=True)
                          - vhat * jnp.mean(dvhat * vhat, axis=-1, keepdims=True))
            dp_ref[:, pl.ds(4 * SLAB + CH * s, CH)] = (db * sp * szb * ugrad).astype(BF16)
            dp_ref[:, pl.ds(5 * SLAB + CH * s, CH)] = (dvg * vgrad).astype(BF16)
            dp_ref[:, pl.ds(6 * SLAB + CH * s, CH)] = (db * ug * sp * (sgb * (1.0 + zb * (1.0 - sgb)))).astype(BF16)

            if s % 2 == 1:
                part = None
                for k in range(N_SLAB):
                    col = k * SLAB + pair * (s // 2)
                    blk, off = divmod(col, IN_BLK)
                    term = lax.dot_general(dp_ref[:, pl.ds(col, pair)], w_ref[blk, off // IN_PIECE],
                                           (((1,), (1,)), ((), ())), preferred_element_type=F32)
                    part = term if part is None else part + term
                if s == 1:
                    dh_ref[...] = part
                else:
                    dh_ref[...] += part

        xv = x_ref[...]
        r = lax.rsqrt(jnp.mean(xv * xv, axis=-1, keepdims=True) + EPS)
        dxn, dg = _rms_bwd(dh_ref[...], xv, r, g1_ref[...])
        gx_ref[...] = dx1_ref[...].astype(F32) + dxn
        dg1_ref[0:1, :] += dg

        @pl.when(i == nt - 1)
        def _():
            tril = lax.broadcasted_iota(jnp.int32, (CH, CH), 0) >= lax.broadcasted_iota(jnp.int32, (CH, CH), 1)
            for s in range(HEADS):
                dwc_ref[s] = jnp.where(tril, dwc_ref[s], 0.0)

    rev = lambda i: nt - 1 - i
    halo = lambda col: pl.BlockSpec((hb, SLAB), lambda i: (jnp.maximum(rev(i) * (tm // hb) - 1, 0), col))
    tok = lambda w: pl.BlockSpec((tm, w), lambda i: (rev(i), 0))
    return pl.pallas_call(
        body, grid=(nt,),
        in_specs=[tok(IN_DIM), halo(1), halo(2), tok(MIX), _full((8, D)), _full((1, D)), _full((1, D)),
                  _full((HEADS, CH, CH)), _full((HEADS, CH, CH)), _full((HEADS, CH, CH)),
                  _full((N_CHIP, N_PIECE, D, IN_PIECE), 1), tok(D), tok(D), _full((1, D))],
        out_specs=[tok(IN_DIM), _full((8, D)), _full((1, D)), _full((1, D)), _full((HEADS, CH, CH)), _full((8, D)),
                   tok(D), _full((8, D))],
        out_shape=[jax.ShapeDtypeStruct((t, IN_DIM), BF16), jax.ShapeDtypeStruct((8, D), F32),
                   jax.ShapeDtypeStruct((1, D), F32), jax.ShapeDtypeStruct((1, D), F32),
                   jax.ShapeDtypeStruct((HEADS, CH, CH), F32), jax.ShapeDtypeStruct((8, D), F32),
                   jax.ShapeDtypeStruct((t, D), F32), jax.ShapeDtypeStruct((8, D), F32)],
        scratch_shapes=[pltpu.VMEM((8, D), F32), pltpu.VMEM((tm, D), F32)],
        compiler_params=_cp(("arbitrary",), VMEM_LIMIT), name="mixer_bwd")(
            proj, proj, proj, dmix, cw8, lng, lnb, wc, wct, bsb, win_f, x, dx1, g1)


def _grad_matmul(a, b, after, *, by_cols, name, tk=1024):
    t, m = a.shape
    n = b.shape[1]
    nk = t // tk
    nj = N_CHIP if by_cols else 1
    bn = n // nj

    def body(a_ref, b_ref, after_ref, o_ref, ob_ref):
        kk = pl.program_id(1)
        part = lax.dot_general(a_ref[...], b_ref[...], (((0,), (0,)), ((), ())), preferred_element_type=F32)

        @pl.when(kk == 0)
        def _():
            o_ref[...] = part

        @pl.when(kk > 0)
        def _():
            o_ref[...] += part

        @pl.when(kk == nk - 1)
        def _():
            ob_ref[...] = o_ref[...].astype(BF16)

    a_spec = pl.BlockSpec((tk, m), lambda j, k: (k, 0))
    b_spec = pl.BlockSpec((tk, bn), lambda j, k: (k, j))
    o_spec = pl.BlockSpec((None, m, bn), lambda j, k: (j, 0, 0))
    o32, o16 = pl.pallas_call(
        body, grid=(nj, nk), in_specs=[a_spec, b_spec, ANY], out_specs=[o_spec, o_spec],
        out_shape=[jax.ShapeDtypeStruct((nj, m, bn), F32), jax.ShapeDtypeStruct((nj, m, bn), BF16)],
        compiler_params=_cp(("parallel", "arbitrary"), VMEM_LIMIT), name=name)(a, b, after)
    if by_cols:
        return o32, o16
    return o32.reshape(N_CHIP, m // N_CHIP, n), o16.reshape(N_CHIP, m // N_CHIP, n)


def _coords():
    x, y, c = lax.axis_index("x"), lax.axis_index("y"), lax.axis_index("c")
    chips = [(1 - x, y), (x, 1 - y), (1 - x, 1 - y)]
    return x, y, c, chips


def _pair_reduce(c_idx, grads, grads_b, smalls, name):
    ng, ns = len(grads), len(smalls)
    halves = [g.shape[1] // 2 for g in grads]

    def body(c_ref, *refs):
        g_in, gb_any = refs[:ng], refs[ng:2 * ng]
        s_own, s_any = refs[2 * ng:2 * ng + ns], refs[2 * ng + ns:2 * ng + 2 * ns]
        o = refs[2 * ng + 2 * ns:4 * ng + 3 * ns]
        lands = refs[4 * ng + 3 * ns:5 * ng + 4 * ns]
        send, recv = refs[5 * ng + 4 * ns:]
        x, y, c, _ = _coords()
        j = pl.program_id(0)

        def big(i, blk):
            return pltpu.make_async_remote_copy(
                src_ref=gb_any[i].at[blk, pl.ds((1 - c) * halves[i], halves[i])], dst_ref=lands[i].at[blk],
                send_sem=send.at[i, blk], recv_sem=recv.at[i, blk], device_id=(x, y, 1 - c), device_id_type=MESH)

        def small(i):
            return pltpu.make_async_remote_copy(
                src_ref=s_any[i].at[1 - c], dst_ref=lands[ng + i],
                send_sem=send.at[ng + i, 0], recv_sem=recv.at[ng + i, 0], device_id=(x, y, 1 - c), device_id_type=MESH)

        @pl.when(j == 0)
        def _():
            for blk in range(N_CHIP):
                for i in range(ng):
                    big(i, blk).start()
            for i in range(ns):
                small(i).start()

        for i in range(ng):
            big(i, j).wait_recv()
            tot = g_in[i][...] + lands[i][j].astype(F32)
            o[i][...] = tot
            o[ng + i][...] = tot.astype(BF16)

        @pl.when(j == N_CHIP - 1)
        def _():
            for i in range(ns):
                small(i).wait_recv()
                o[2 * ng + i][...] = s_own[i][...] + lands[ng + i][...]
                small(i).wait_send()
            for blk in range(N_CHIP):
                for i in range(ng):
                    big(i, blk).wait_send()

    in_specs = [pl.BlockSpec((None, None, halves[i], g.shape[2]), lambda b, c: (b, c[0], 0, 0)) for i, g in enumerate(grads)]
    in_specs += [ANY] * ng
    in_specs += [pl.BlockSpec((None, s.shape[0] // 2, s.shape[1]), lambda b, c: (c[0], 0, 0)) for s in smalls]
    in_specs += [ANY] * ns
    blk = [pl.BlockSpec((None, halves[i], g.shape[2]), lambda b, c: (b, 0, 0)) for i, g in enumerate(grads)]
    out_specs = blk + blk + [pl.BlockSpec((s.shape[0] // 2, s.shape[1]), lambda b, c: (0, 0)) for s in smalls]
    out_shape = [jax.ShapeDtypeStruct((N_CHIP, halves[i], g.shape[2]), F32) for i, g in enumerate(grads)]
    out_shape += [jax.ShapeDtypeStruct((N_CHIP, halves[i], g.shape[2]), BF16) for i, g in enumerate(grads)]
    out_shape += [jax.ShapeDtypeStruct((s.shape[0] // 2, s.shape[1]), F32) for s in smalls]
    scratch = [pltpu.VMEM((N_CHIP, halves[i], g.shape[2]), BF16) for i, g in enumerate(grads)]
    scratch += [pltpu.VMEM((s.shape[0] // 2, s.shape[1]), F32) for s in smalls]
    scratch += [pltpu.SemaphoreType.DMA((ng + ns, N_CHIP)), pltpu.SemaphoreType.DMA((ng + ns, N_CHIP))]
    grads4 = [g.reshape(N_CHIP, 2, halves[i], g.shape[2]) for i, g in enumerate(grads)]
    smalls3 = [s.reshape(2, s.shape[0] // 2, s.shape[1]) for s in smalls]
    return pl.pallas_call(
        body, out_shape=out_shape,
        grid_spec=pltpu.PrefetchScalarGridSpec(num_scalar_prefetch=1, grid=(N_CHIP,), in_specs=in_specs,
                                               out_specs=out_specs, scratch_shapes=scratch),
        compiler_params=_cp(("arbitrary",), VMEM_LIMIT), name=name)(c_idx, *grads4, *grads_b, *smalls3, *smalls3)


def _grad_matmul_pair(c_idx, a, b, smalls, after, *, name, tk=2048):
    t, m = a.shape
    bn = b.shape[1] // N_CHIP
    nk = t // tk
    hr = m // 2
    ns = len(smalls)

    def body(c_ref, a_ref, b_ref, *refs):
        s_own, s_any = refs[:ns], refs[ns:2 * ns]
        o32, o16 = refs[2 * ns + 1], refs[2 * ns + 2]
        o_small = refs[2 * ns + 3:3 * ns + 3]
        acc, tb, land, st16 = refs[3 * ns + 3:3 * ns + 7]
        s_land, s_stage = refs[3 * ns + 7:4 * ns + 7], refs[4 * ns + 7:5 * ns + 7]
        send, recv, loc = refs[5 * ns + 7:]
        x, y, c, _ = _coords()
        sibling = dict(device_id=(x, y, 1 - c), device_id_type=MESH)
        j, kk = pl.program_id(0), pl.program_id(1)
        mine = pl.ds(pl.multiple_of(c * hr, hr), hr)
        theirs = pl.ds(pl.multiple_of((1 - c) * hr, hr), hr)

        def to_sibling(blk):
            return pltpu.make_async_remote_copy(src_ref=tb, dst_ref=land.at[blk], send_sem=send.at[blk],
                                                recv_sem=recv.at[blk], **sibling)

        def small(i):
            return pltpu.make_async_remote_copy(src_ref=s_any[i].at[1 - c], dst_ref=s_land[i], send_sem=send.at[N_CHIP + i],
                                                recv_sem=recv.at[N_CHIP + i], **sibling)

        def written(blk):
            return (pltpu.make_async_copy(acc.at[blk % 2, mine], o32.at[blk], loc.at[0]),
                    pltpu.make_async_copy(st16, o16.at[blk], loc.at[1]))

        def finish(blk):
            to_sibling(blk).wait_recv()

            @pl.when(blk > 0)
            def _():
                for cp in written(blk - 1):
                    cp.wait()

            tot = acc[blk % 2, mine, :] + land[blk].astype(F32)
            acc[blk % 2, mine, :] = tot
            st16[...] = tot.astype(BF16)
            for cp in written(blk):
                cp.start()

        def small_out(i):
            return pltpu.make_async_copy(s_stage[i], o_small[i], loc.at[2 + i])

        @pl.when((j == 0) & (kk == 0))
        def _():
            for i in range(ns):
                small(i).start()

        @pl.when((j == 1) & (kk == 0))
        def _():
            for i in range(ns):
                small(i).wait_recv()
                s_stage[i][...] = s_own[i][...] + s_land[i][...]
                small_out(i).start()

        @pl.when((j > 0) & (kk == 0))
        def _():
            finish(j - 1)

        part = lax.dot_general(a_ref[...], b_ref[...], (((0,), (0,)), ((), ())), preferred_element_type=F32)
        slot = lax.rem(j, 2)

        @pl.when(kk == 0)
        def _():
            acc[slot] = part

        @pl.when(kk > 0)
        def _():
            acc[slot] += part

        @pl.when(kk == nk - 1)
        def _():
            @pl.when(j > 0)
            def _():
                to_sibling(j - 1).wait_send()

            tb[...] = acc[slot, theirs, :].astype(BF16)
            to_sibling(j).start()

        @pl.when((j == N_CHIP - 1) & (kk == nk - 1))
        def _():
            finish(j)
            for i in range(ns):
                small_out(i).wait()
                small(i).wait_send()
            for cp in written(j):
                cp.wait()
            to_sibling(j).wait_send()

    halves = [(s.shape[0] // 2, s.shape[1]) for s in smalls]
    in_specs = [pl.BlockSpec((tk, m), lambda j, k, c: (k, 0)), pl.BlockSpec((tk, bn), lambda j, k, c: (k, j))]
    in_specs += [pl.BlockSpec((None,) + h, lambda j, k, c: (c[0], 0, 0)) for h in halves] + [ANY] * ns + [ANY]
    out_shape = [jax.ShapeDtypeStruct((N_CHIP, hr, bn), F32), jax.ShapeDtypeStruct((N_CHIP, hr, bn), BF16)]
    out_shape += [jax.ShapeDtypeStruct(h, F32) for h in halves]
    scratch = [pltpu.VMEM((2, m, bn), F32), pltpu.VMEM((hr, bn), BF16),
               pltpu.VMEM((N_CHIP, hr, bn), BF16), pltpu.VMEM((hr, bn), BF16)]
    scratch += [pltpu.VMEM(h, F32) for h in halves] * 2
    scratch += [pltpu.SemaphoreType.DMA((N_CHIP + ns,)), pltpu.SemaphoreType.DMA((N_CHIP + ns,)),
                pltpu.SemaphoreType.DMA((2 + ns,))]
    smalls3 = [s.reshape((2,) + h) for s, h in zip(smalls, halves)]
    outs = pl.pallas_call(
        body, out_shape=out_shape,
        grid_spec=pltpu.PrefetchScalarGridSpec(num_scalar_prefetch=1, grid=(N_CHIP, nk), in_specs=in_specs,
                                               out_specs=[ANY] * (2 + ns), scratch_shapes=scratch),
        compiler_params=_cp(("arbitrary", "arbitrary"), VMEM_LIMIT), name=name)(c_idx, a, b, *smalls3, *smalls3, after)
    return outs[0], outs[1], list(outs[2:])


_HBM = pl.BlockSpec(memory_space=pltpu.HBM)
_SEM = pl.BlockSpec(memory_space=pltpu.SEMAPHORE)


def _split_copies(ins, lands, ng, send, recv, arriving):
    x, y, c, chips = _coords()
    b = 2 * x + y
    copies = []
    for i in range(len(ins)):
        for k in range(3):
            blk = 2 * chips[k][0] + chips[k][1]
            src, dst, got = (ins[i].at[blk], lands[i].at[k], lands[i].at[k]) if i < ng else (ins[i], lands[i].at[b], lands[i].at[blk])
            sems = dict(send_sem=send.at[3 * i + k], recv_sem=recv.at[3 * i + k], device_id=(*chips[k], c), device_id_type=MESH)
            if arriving:
                copies.append(pltpu.make_async_remote_copy(src_ref=got, dst_ref=got, **sems))
            else:
                copies.append(pltpu.make_async_remote_copy(src_ref=src, dst_ref=dst, **sems))
    return copies


def _exchange_begin(sums_b, smalls, name):
    ng, n = len(sums_b), len(sums_b) + len(smalls)
    srcs = list(sums_b) + list(smalls)
    lands = [lax.empty((3,) + g.shape[1:], g.dtype) for g in sums_b] + [lax.empty((N_CHIP,) + s.shape, s.dtype) for s in smalls]

    def body(*refs):
        ins, land_refs = refs[:n], refs[n:2 * n]
        send, recv = refs[2 * n], refs[2 * n + 1]
        token = refs[4 * n + 2]
        for cp in _split_copies(ins, land_refs, ng, send, recv, False):
            cp.start()
        token[...] = jnp.zeros_like(token)

    hbm = lambda a: pltpu.HBM(a.shape, a.dtype)
    outs = pl.pallas_call(
        body, name=name,
        out_shape=(pltpu.SemaphoreType.DMA((3 * n,)), pltpu.SemaphoreType.DMA((3 * n,)), *[hbm(a) for a in srcs + lands],
                   jax.ShapeDtypeStruct((8, 128), F32)),
        in_specs=[_HBM] * (2 * n), out_specs=(_SEM, _SEM, *[_HBM] * (2 * n), pl.BlockSpec(memory_space=pltpu.VMEM)),
        input_output_aliases={i: 2 + i for i in range(2 * n)},
        compiler_params=pltpu.CompilerParams(has_side_effects=pltpu.SideEffectType.DATAFLOW_SIDE_EFFECTING),
    )(*[pltpu.with_memory_space_constraint(a, pltpu.HBM) for a in srcs + lands])
    return outs[0], outs[1], list(outs[2:2 + n]), list(outs[2 + n:2 + 2 * n]), outs[2 + 2 * n]


def _exchange_end(send, recv, srcs, lands, ng, which, after, name):
    n = len(srcs)
    after = list(after)

    def body(*refs):
        ins, land_refs = refs[:n], refs[n:2 * n]
        send_ref, recv_ref = refs[2 * n], refs[2 * n + 1]
        outgoing = _split_copies(ins, land_refs, ng, send_ref, recv_ref, False)
        arriving = _split_copies(ins, land_refs, ng, send_ref, recv_ref, True)
        for i in which:
            for cp in outgoing[3 * i:3 * i + 3]:
                cp.wait_send()
        for i in which:
            for cp in arriving[3 * i:3 * i + 3]:
                cp.wait_recv()

    hbm = lambda a: pltpu.HBM(a.shape, a.dtype)
    outs = pl.pallas_call(
        body, name=name, out_shape=tuple(hbm(a) for a in list(srcs) + list(lands)),
        in_specs=[_HBM] * (2 * n) + [_SEM, _SEM] + [ANY] * len(after), out_specs=tuple([_HBM] * (2 * n)),
        input_output_aliases={i: i for i in range(2 * n)},
        compiler_params=pltpu.CompilerParams(has_side_effects=pltpu.SideEffectType.DATAFLOW_SIDE_EFFECTING),
    )(*srcs, *lands, send, recv, *after)
    return list(outs[:n]), list(outs[n:])


def _chip_reduce(bc_idx, sums, recvd, smalls_slots, smalls_own, name, steps=4):
    ng, ns = len(sums), len(smalls_slots)
    n = ng + ns
    assert steps >= 2
    halves = [g.shape[1] for g in sums] + [s.shape[1] for s in smalls_slots]
    rows = [g.shape[1] // steps for g in sums]

    def body(bc_ref, *refs):
        own, rx = refs[:ng], refs[ng:2 * ng]
        sl = refs[2 * ng:2 * ng + ns]
        sl_own = refs[2 * ng + ns:2 * ng + 2 * ns]
        o = refs[2 * ng + 2 * ns:2 * ng + 2 * ns + n]
        tiles = refs[2 * ng + 2 * ns + n:2 * ng + 2 * ns + 2 * n]
        keep, send, recv = refs[2 * ng + 2 * ns + 2 * n:]
        x, y, c, _ = _coords()
        sibling = dict(device_id=(x, y, 1 - c), device_id_type=MESH)
        r = pl.program_id(0)

        def writes(i, step, slot):
            dst = o[i].at[pl.ds(c * halves[i] + step * rows[i], rows[i])]
            return (pltpu.make_async_copy(tiles[i].at[slot], dst, keep.at[i, slot]),
                    pltpu.make_async_remote_copy(src_ref=tiles[i].at[slot], dst_ref=dst, send_sem=send.at[i, slot],
                                                 recv_sem=recv.at[i, step], **sibling))

        def small_writes(i):
            dst = o[i].at[pl.ds(c * halves[i], halves[i])]
            return (pltpu.make_async_copy(tiles[i], dst, keep.at[i, 0]),
                    pltpu.make_async_remote_copy(src_ref=tiles[i], dst_ref=dst, send_sem=send.at[i, 0],
                                                 recv_sem=recv.at[i, 0], **sibling))

        def arriving(i, step, nrows):
            dst = o[i].at[pl.ds((1 - c) * halves[i] + step * nrows, nrows)]
            return pltpu.make_async_remote_copy(src_ref=dst, dst_ref=dst, send_sem=send.at[i, 0], recv_sem=recv.at[i, step],
                                                **sibling)

        def finish(step, slot):
            for i in range(ng):
                local, remote = writes(i, step, slot)
                local.wait()
                remote.wait_send()

        @pl.when(r >= 2)
        def _():
            finish(r - 2, r % 2)

        for i in range(ng):
            tot = own[i][...]
            for j in range(3):
                tot = tot + rx[i][j].astype(F32)
            tiles[i][r % 2] = tot
            for cp in writes(i, r, r % 2):
                cp.start()

        @pl.when(r == 0)
        def _():
            for i in range(ns):
                term = [jnp.where(bc_ref[0] == kk, sl_own[i][...], sl[i][kk]) for kk in range(N_CHIP)]
                tiles[ng + i][...] = ((term[0] + term[1]) + term[2]) + term[3]
                for cp in small_writes(ng + i):
                    cp.start()

        @pl.when(r == steps - 1)
        def _():
            finish(steps - 2, (steps - 2) % 2)
            finish(steps - 1, (steps - 1) % 2)
            for i in range(ns):
                local, remote = small_writes(ng + i)
                local.wait()
                remote.wait_send()
                arriving(ng + i, 0, halves[ng + i]).wait_recv()
            for i in range(ng):
                for step in range(steps):
                    arriving(i, step, rows[i]).wait_recv()

    in_specs = [pl.BlockSpec((None, rows[i], g.shape[2]), lambda r, bc: (bc[0], r, 0)) for i, g in enumerate(sums)]
    in_specs += [pl.BlockSpec((3, rows[i], g.shape[2]), lambda r, bc: (0, r, 0)) for i, g in enumerate(sums)]
    in_specs += [pl.BlockSpec(s.shape, lambda r, bc: (0, 0, 0)) for s in smalls_slots]
    in_specs += [pl.BlockSpec(s.shape[1:], lambda r, bc: (0, 0)) for s in smalls_slots]
    out_shape = [jax.ShapeDtypeStruct((2 * g.shape[1], g.shape[2]), F32) for g in sums]
    out_shape += [jax.ShapeDtypeStruct((2 * s.shape[1], s.shape[2]), F32) for s in smalls_slots]
    scratch = [pltpu.VMEM((2, rows[i], g.shape[2]), F32) for i, g in enumerate(sums)]
    scratch += [pltpu.VMEM(s.shape[1:], F32) for s in smalls_slots]
    scratch += [pltpu.SemaphoreType.DMA((n, 2)), pltpu.SemaphoreType.DMA((n, 2)), pltpu.SemaphoreType.DMA((n, steps))]
    return list(pl.pallas_call(
        body, out_shape=out_shape,
        grid_spec=pltpu.PrefetchScalarGridSpec(num_scalar_prefetch=1, grid=(steps,), in_specs=in_specs,
                                               out_specs=[ANY] * n, scratch_shapes=scratch),
        compiler_params=_cp(("arbitrary",), VMEM_LIMIT), name=name)(bc_idx, *sums, *recvd, *smalls_slots, *smalls_own))


def _adamw_math(w, g, m, v):
    m2 = ADAM_B1 * m + (1.0 - ADAM_B1) * g
    v2 = ADAM_B2 * v + (1.0 - ADAM_B2) * (g * g)
    m_hat = m2 / (1.0 - ADAM_B1 ** ADAM_STEP)
    v_hat = v2 / (1.0 - ADAM_B2 ** ADAM_STEP)
    delta = -ADAM_LR * (m_hat / (jnp.sqrt(v_hat) + ADAM_EPS) + ADAM_WD * w)
    return delta, m2, v2


def _adamw_big(ws, gs, ms, vs, name, steps=8):
    n = len(ws)

    def body(*refs):
        for i in range(n):
            w_ref, g_ref, m_ref, v_ref = (refs[k * n + i] for k in range(4))
            d_ref, m2_ref, v2_ref, g2_ref = (refs[(4 + k) * n + i] for k in range(4))
            gv = g_ref[...]
            d_ref[...], m2_ref[...], v2_ref[...] = _adamw_math(w_ref[...], gv, m_ref[...], v_ref[...])
            g2_ref[...] = gv

    specs = [pl.BlockSpec((w.shape[0] // steps, w.shape[1]), lambda i: (i, 0)) for w in ws]
    shapes = [jax.ShapeDtypeStruct(w.shape, F32) for w in ws]
    outs = pl.pallas_call(
        body, grid=(steps,), in_specs=specs * 4, out_specs=specs * 4, out_shape=shapes * 4,
        compiler_params=_cp(("parallel",), VMEM_LIMIT), name=name)(*ws, *gs, *ms, *vs)
    return [tuple(outs[k * n + i] for k in range(4)) for i in range(n)]


def _adamw_small(groups):
    n = len(groups)

    def body(*refs):
        for i in range(n):
            w_ref, g_ref, m_ref, v_ref = refs[4 * i:4 * i + 4]
            d_ref, m2_ref, v2_ref = refs[4 * n + 3 * i:4 * n + 3 * i + 3]
            d_ref[...], m2_ref[...], v2_ref[...] = _adamw_math(w_ref[...], g_ref[...], m_ref[...], v_ref[...])

    flat = [a for grp in groups for a in grp]
    out_shape = [jax.ShapeDtypeStruct(grp[0].shape, F32) for grp in groups for _ in range(3)]
    outs = pl.pallas_call(body, out_shape=out_shape, name="adamw_small")(*flat)
    return [tuple(outs[3 * i:3 * i + 3]) for i in range(n)]


def kernel(x, mem, norm_mix_g, w_in, conv_w, gm_ln_g, gm_ln_b, gm_ws, gm_bs, w_out, norm_x_g, norm_mem_g, w_q, w_kv, w_xo, norm_final_g, loss_target, m_norm_mix_g, m_w_in, m_conv_w, m_gm_ln_g, m_gm_ln_b, m_gm_ws, m_gm_bs, m_w_out, m_norm_x_g, m_norm_mem_g, m_w_q, m_w_kv, m_w_xo, m_norm_final_g, v_norm_mix_g, v_w_in, v_conv_w, v_gm_ln_g, v_gm_ln_b, v_gm_ws, v_gm_bs, v_w_out, v_norm_x_g, v_norm_mem_g, v_w_q, v_w_kv, v_w_xo, v_norm_final_g):
    t = x.shape[1]
    xi = lax.axis_index("x")
    yi = lax.axis_index("y")
    ci = lax.axis_index("c")
    b_idx = jnp.reshape(2 * xi + yi, (1,)).astype(jnp.int32)
    c_idx = jnp.reshape(ci, (1,)).astype(jnp.int32)

    x2d, mem2d, tgt = x[0], mem[0], loss_target[0]
    big = [w_in[0], w_out[0], w_q[0], w_kv[0], w_xo[0]]
    big_m = [m_w_in[0], m_w_out[0], m_w_q[0], m_w_kv[0], m_w_xo[0]]
    big_v = [v_w_in[0], v_w_out[0], v_w_q[0], v_w_kv[0], v_w_xo[0]]
    g3 = norm_final_g.reshape(1, D)

    def pad8(a):
        return jnp.pad(a, ((0, 8 - a.shape[0]), (0, 0)))

    own_blocks = _cast_shards(b_idx, big)

    tril = jnp.tril(jnp.ones((CH, CH), bool))
    wc32 = jnp.where(tril[None], gm_ws[0], 0.0)
    wc = wc32.astype(BF16)
    wct = jnp.swapaxes(wc32, 1, 2).astype(BF16)
    bsb = jnp.broadcast_to(gm_bs[0][:, :, None], (HEADS, CH, CH))

    blk = 2 * xi + yi
    near = [blk ^ (2 >> (k % 2)) for k in range(2 * N_PIECE)]
    seq_blk = jnp.stack([blk] * N_PIECE + near + [blk ^ 3] * N_PIECE)
    seq_piece = jnp.asarray(list(range(N_PIECE)) + [k // 2 for k in range(2 * N_PIECE)] + list(range(N_PIECE)))
    seq = jnp.stack([seq_blk, seq_piece, seq_blk * N_PIECE + seq_piece]).astype(jnp.int32)
    proj, hb, win_f, cw8, (wq_f,) = _proj_gather(
        seq, x2d, norm_mix_g, own_blocks[0], pad8(conv_w[0]), [own_blocks[2]])
    mixin, (wout_f, wkv_f, wxo_f) = _mixer_fwd(
        proj, cw8, gm_ln_g, gm_ln_b, wc, bsb, [own_blocks[1], own_blocks[3], own_blocks[4]])
    wout2, wq2, wxo2 = wout_f.reshape(MIX, D), wq_f.reshape(D, D), wxo_f.reshape(D, D)
    k, v = _mem_fwd(mem2d, norm_mem_g, wkv_f)

    (loss_tile, dmix, dx1b, h2b, dq, ob, dx2b, dk, dv, dg2, dg3) = _tail(
        x2d, tgt, mixin, wout2, wq2, wxo2, k, v, norm_x_g, g3)
    dwkv, dwkv_b, dgm = _mem_bwd(mem2d, norm_mem_g, dk, dv, wkv_f)
    dproj, dcw, dlng, dlnb, dwc, dbs8, grad_x, dg1 = _mixer_bwd(
        proj, dmix, cw8, gm_ln_g, gm_ln_b, wc, wct, bsb, win_f, x2d, dx1b, norm_mix_g)

    bc_idx = jnp.concatenate([b_idx, c_idx])
    zero = jnp.zeros((1, D), F32)
    loss_row = jnp.broadcast_to(loss_tile[0:1, 0:1], (1, D))
    sv = jnp.concatenate([dg1[0:1], dg2, dgm, dg3, dlng, dlnb, dbs8[0:1], loss_row, dcw], axis=0)
    sw = dwc.reshape(HEADS * CH, CH)
    dwin_sum, dwin_sum_b, psmall = _grad_matmul_pair(c_idx, hb, dproj, [sv, sw], dgm, name="grad_w_in")
    sums_b = [dwin_sum]
    send_b, recv_b, src_b, land_b, token_b = _exchange_begin([dwin_sum_b], psmall, "exchange_b_begin")

    dwxo, dwxo_b = _grad_matmul(ob, dx2b, token_b, by_cols=False, name="grad_w_xo", tk=2048)
    dwq, dwq_b = _grad_matmul(h2b, dq, token_b, by_cols=False, name="grad_w_q", tk=2048)
    dwout, dwout_b = _grad_matmul(mixin, dx1b, token_b, by_cols=False, name="grad_w_out")
    ps_a = _pair_reduce(c_idx, [dwout, dwkv, dwq, dwxo], [dwout_b, dwkv_b, dwq_b, dwxo_b], [], "pair_reduce_a")
    sums_a, sums_a_b = list(ps_a[:4]), list(ps_a[4:8])
    send_a, recv_a, src_a, land_a, token_a = _exchange_begin(sums_a_b, [], "exchange_a_begin")

    _, rx2b = _exchange_end(send_b, recv_b, src_b, land_b, 1, [0, 1, 2], [token_a], "exchange_b_end")
    gwin, svf, swf = _chip_reduce(bc_idx, sums_b, rx2b[:1], rx2b[1:], psmall, "chip_reduce_b")
    out_b = _adamw_big(big[:1], [gwin], big_m[:1], big_v[:1], "adamw_w_in")[0]

    def vec_pack(a1, a2, am, a3, lg, lb, bs):
        return jnp.concatenate([a1, a2, am, a3.reshape(1, D), lg, lb, bs.reshape(1, D), zero], axis=0)

    wv = vec_pack(norm_mix_g, norm_x_g, norm_mem_g, norm_final_g, gm_ln_g, gm_ln_b, gm_bs)
    mv = vec_pack(m_norm_mix_g, m_norm_x_g, m_norm_mem_g, m_norm_final_g, m_gm_ln_g, m_gm_ln_b, m_gm_bs)
    vv = vec_pack(v_norm_mix_g, v_norm_x_g, v_norm_mem_g, v_norm_final_g, v_gm_ln_g, v_gm_ln_b, v_gm_bs)
    loss = svf[7, 0]
    gcw = lax.dynamic_slice_in_dim(svf[8:16], blk * (D // N_CHIP), D // N_CHIP, axis=1)
    gws = swf
    gv = svf[0:8]
    (dv_, mv_, vv_), (dc_, mc_, vc_), (dws_, mws_, vws_) = _adamw_small([
        (wv, gv, mv, vv),
        (pad8(conv_w[0]), gcw, pad8(m_conv_w[0]), pad8(v_conv_w[0])),
        (gm_ws.reshape(HEADS * CH, CH), gws, m_gm_ws.reshape(HEADS * CH, CH), v_gm_ws.reshape(HEADS * CH, CH))])

    def finish_a(part, src, land, after, tag):
        src, land = _exchange_end(send_a, recv_a, src, land, 4, part, after, "exchange_a%s_end" % tag)
        grads = _chip_reduce(bc_idx, [sums_a[i] for i in part], [land[i] for i in part], [], [], "chip_reduce_a" + tag,
                             steps=2)
        ids = [(1, 3, 2, 4)[i] for i in part]
        outs = _adamw_big([big[i] for i in ids], grads, [big_m[i] for i in ids], [big_v[i] for i in ids], "adamw_a" + tag,
                          steps=4)
        return src, land, outs

    src_a, land_a, (out_wout, out_wkv) = finish_a([0, 1], src_a, land_a, [out_b[0], dv_], "1")
    _, _, (out_wq, out_wxo) = finish_a([2, 3], src_a, land_a, [out_wout[0]], "2")
    big_out = [out_b, out_wout, out_wq, out_wkv, out_wxo]

    def unpack(vecs, cw, ws, bigs):
        r = lambda i: vecs[i:i + 1]
        return [r(0), bigs[0][None], cw[0:3][None], r(4), r(5), ws.reshape(1, HEADS, CH, CH), vecs[6].reshape(1, HEADS, CH),
                bigs[1][None], r(1), r(2), bigs[2][None], bigs[3][None], bigs[4][None], vecs[3]]

    grads_out = unpack(gv, gcw, gws, [o[3] for o in big_out])
    delta_out = unpack(dv_, dc_, dws_, [o[0] for o in big_out])
    m_out = unpack(mv_, mc_, mws_, [o[1] for o in big_out])
    v_out = unpack(vv_, vc_, vws_, [o[2] for o in big_out])
    return (loss, grad_x[None], *grads_out, *delta_out, *m_out, *v_out)
```

```python
import functools
import math

import jax
import jax.numpy as jnp
from jax import lax
from jax.experimental import pallas as pl
from jax.experimental.pallas import tpu as pltpu

F32 = jnp.float32
BF16 = jnp.bfloat16
MESH = pl.DeviceIdType.MESH

D = 1024
SLAB = 1024
N_SLAB = 7
IN_DIM = N_SLAB * SLAB
MIX = 2 * SLAB
HEADS = 8
CH = 128
XH = 4
XD = D // XH
EPS = 1e-6
GELU_C = math.sqrt(2.0 / math.pi)
GELU_A = 0.044715
N_CHIP = 4
IN_BLK = IN_DIM // N_CHIP
IN_PIECE = 256
N_PIECE = IN_BLK // IN_PIECE
KV_BLK = 2 * D // N_CHIP

ADAM_LR, ADAM_B1, ADAM_B2, ADAM_EPS, ADAM_WD, ADAM_STEP = 0.001, 0.9, 0.999, 1e-08, 0.01, 10

VMEM_LIMIT = 60 * 1024 * 1024


def _cp(sem=None, vmem=None):
    return pltpu.CompilerParams(dimension_semantics=sem, vmem_limit_bytes=vmem)


def _full(shape, buffers=None):
    n = len(shape)
    if buffers is None:
        return pl.BlockSpec(shape, lambda *_: (0,) * n)
    return pl.BlockSpec(shape, lambda *_: (0,) * n, pipeline_mode=pl.Buffered(buffers))


ANY = pl.BlockSpec(memory_space=pl.ANY)


def _bdot(a, b):
    return jnp.dot(a.astype(BF16), b.astype(BF16), preferred_element_type=F32)


def _bdot_nt(a, b):
    return lax.dot_general(a.astype(BF16), b.astype(BF16), (((1,), (1,)), ((), ())), preferred_element_type=F32)


def _bdot_tn(a, b):
    return lax.dot_general(a.astype(BF16), b.astype(BF16), (((0,), (0,)), ((), ())), preferred_element_type=F32)


def _rms(x, g):
    r = lax.rsqrt(jnp.mean(x * x, axis=-1, keepdims=True) + EPS)
    return x * r * g, r


def _rms_bwd(dy, x, r, g):
    gdy = dy * g
    dx = r * gdy - x * (r * r * r) * jnp.mean(x * gdy, axis=-1, keepdims=True)
    dg = jnp.sum(dy * x * r, axis=0, keepdims=True)
    return dx, dg


def _gelu_parts(x):
    x2 = x * x
    t = jnp.tanh(GELU_C * (x + GELU_A * x * x2))
    val = 0.5 * x * (1.0 + t)
    grad = 0.5 * (1.0 + t) + 0.5 * x * (1.0 - t * t) * (GELU_C * (1.0 + 3.0 * GELU_A * x2))
    return val, grad


def _gelu(x):
    return 0.5 * x * (1.0 + jnp.tanh(GELU_C * (x + GELU_A * x * x * x)))


def _sigmoid(z):
    return 1.0 / (1.0 + jnp.exp(-z))


def _cast_shards(b_idx, arrs):
    n = len(arrs)
    steps = 8

    def body(b_ref, *refs):
        for p in range(N_PIECE):
            refs[n][p] = refs[0][:, pl.ds(p * IN_PIECE, IN_PIECE)].astype(BF16)
        for i in range(1, n):
            refs[n + i][...] = refs[i][...].astype(BF16)

    rows = [a.shape[0] // steps for a in arrs]
    in_specs = [pl.BlockSpec((rows[i], a.shape[1]), lambda i, b: (i, 0)) for i, a in enumerate(arrs)]
    out_specs = [pl.BlockSpec((None, N_PIECE, rows[0], IN_PIECE), lambda i, b: (b[0], 0, i, 0))]
    out_specs += [pl.BlockSpec((None, rows[i], a.shape[1]), lambda i, b: (b[0], i, 0)) for i, a in enumerate(arrs) if i > 0]
    out_shape = [jax.ShapeDtypeStruct((N_CHIP, N_PIECE, arrs[0].shape[0], IN_PIECE), BF16)]
    out_shape += [jax.ShapeDtypeStruct((N_CHIP,) + a.shape, BF16) for a in arrs[1:]]
    return pl.pallas_call(
        body, out_shape=out_shape,
        grid_spec=pltpu.PrefetchScalarGridSpec(num_scalar_prefetch=1, grid=(steps,), in_specs=in_specs, out_specs=out_specs),
        compiler_params=_cp(("arbitrary",)), name="cast_shards")(b_idx, *arrs)


def _proj_gather(seq, x, g, win_own, cw8s, more, tm=1024):
    t = x.shape[0]
    ni = t // tm
    nm = len(more)
    steps = N_CHIP * N_PIECE
    near0, far0 = N_PIECE, 3 * N_PIECE

    def body(*refs):
        seq_ref, x_any, g_ref, win_in, cw_in = refs[:5]
        o_ref, hb_any, win_f, cw_out = refs[5 + nm:9 + nm]
        more_out = refs[9 + nm:9 + 2 * nm]
        hbuf, xbuf, wv, cw_s, cw_r, loc = refs[9 + 2 * nm:15 + 2 * nm]
        g_in = _Gather([win_f.at[:, p] for p in range(N_PIECE)], *refs[15 + 2 * nm:19 + 2 * nm])
        g_more = _Gather(more_out, *refs[19 + 2 * nm:23 + 2 * nm])
        s = pl.program_id(0)
        x, y, c, chips = _coords()
        b = 2 * x + y
        blks = [2 * chip[0] + chip[1] for chip in chips]

        def cw_cols(blk):
            return cw_out.at[:, pl.ds(blk * (D // N_CHIP), D // N_CHIP)]

        def cw_copy(k, blk):
            src = cw_in if blk is None else cw_cols(blk)
            return pltpu.make_async_remote_copy(src_ref=src, dst_ref=cw_cols(b if blk is None else blk), send_sem=cw_s.at[k],
                                                recv_sem=cw_r.at[k], device_id=(*chips[k], c), device_id_type=MESH)

        cw_local = pltpu.make_async_copy(cw_in, cw_cols(b), loc.at[1])
        hb_copy = pltpu.make_async_copy(hbuf, hb_any, loc.at[0])

        def load(step):
            slot = lax.rem(step, 2)
            return pltpu.make_async_copy(win_f.at[seq_ref[0, step], seq_ref[1, step]], wv.at[slot], loc.at[2 + slot])

        def chunk(i):
            return pltpu.make_async_copy(x_any.at[pl.ds(i * tm, tm)], xbuf.at[i % 2], loc.at[4 + i % 2])

        def first():
            g_in.start()
            cw_local.start()
            for k in range(3):
                cw_copy(k, None).start()
            load(0).start()
            chunk(0).start()
            for i in range(ni):
                if i + 1 < ni:
                    chunk(i + 1).start()
                chunk(i).wait()
                h, _ = _rms(xbuf[i % 2], g_ref[...])
                hbuf[pl.ds(i * tm, tm), :] = h.astype(BF16)
            hb_copy.start()

        events = {step: [] for step in range(steps)}
        events[0].append(first)
        for p in range(N_PIECE):
            events[2 * p + 2].append(functools.partial(g_in.hop, [p]))
            events[near0 + 2 * p - 1].append(functools.partial(g_in.near_ready, [p]))
            events[far0 + p - 2].append(functools.partial(g_in.far, [p]))
            events[far0 + p - 1].append(functools.partial(g_in.far_ready, [p]))
        events[2 * N_PIECE + 1].append(g_more.start)
        for step, todo in events.items():
            if todo:
                @pl.when(s == step)
                def _(todo=todo):
                    for do in todo:
                        do()

        @pl.when(s + 1 < steps)
        def _():
            load(s + 1).start()

        load(s).wait()
        for i in range(ni):
            rows = pl.ds(i * tm, tm)
            o_ref[rows, :] = jnp.dot(hbuf[rows, :], wv[lax.rem(s, 2)], preferred_element_type=F32).astype(BF16)

        @pl.when(s == steps - 1)
        def _():
            g_more.hop()
            g_more.far()
            for k in range(3):
                cw_copy(k, blks[k]).wait_recv()
            for k in range(3):
                cw_copy(k, None).wait_send()
            cw_local.wait()
            hb_copy.wait()
            g_more.near_ready()
            g_more.far_ready()
            g_in.drain()
            g_more.drain()

    in_specs = [ANY, pl.BlockSpec((1, D), lambda s, q: (0, 0)), ANY, ANY] + [ANY] * nm
    out_specs = [pl.BlockSpec((t, IN_PIECE), lambda s, q: (0, q[2, s])), ANY, ANY, ANY] + [ANY] * nm
    outs = pl.pallas_call(
        body, out_shape=[jax.ShapeDtypeStruct((t, IN_DIM), BF16), jax.ShapeDtypeStruct((t, D), BF16),
                         jax.ShapeDtypeStruct(win_own.shape, BF16), jax.ShapeDtypeStruct((8, D), F32)]
        + [jax.ShapeDtypeStruct(f.shape, f.dtype) for f in more],
        grid_spec=pltpu.PrefetchScalarGridSpec(
            num_scalar_prefetch=1, grid=(steps,), in_specs=in_specs, out_specs=out_specs,
            scratch_shapes=[pltpu.VMEM((t, D), BF16), pltpu.VMEM((2, tm, D), F32), pltpu.VMEM((2, D, IN_PIECE), BF16)]
            + [pltpu.SemaphoreType.DMA((3,))] * 2 + [pltpu.SemaphoreType.DMA((6,))]
            + _gather_sems(N_PIECE) + _gather_sems(nm)),
        input_output_aliases={3: 2, **{5 + w: 4 + w for w in range(nm)}},
        compiler_params=_cp(("arbitrary",), VMEM_LIMIT), name="proj_gather")(seq, x, g, win_own, cw8s, *more)
    return outs[0], outs[1], outs[2], outs[3], outs[4:]


class _Gather:
    def __init__(self, outs, ici_s, ici_r, d2d_s, d2d_r):
        x, y, c, _ = _coords()
        self.outs, self.c = outs, c
        self.sems = ici_s, ici_r, d2d_s, d2d_r
        self.b, self.bx, self.by, self.bd = 2 * x + y, 2 * (1 - x) + y, 2 * x + (1 - y), 2 * (1 - x) + (1 - y)
        self.xn, self.yn, self.sib = (1 - x, y, c), (x, 1 - y, c), (x, y, 1 - c)

    def piece(self, w, blk, hc, quarter=None):
        hr = self.outs[w].shape[1] // 2
        if quarter is None:
            return self.outs[w].at[blk, pl.ds(hc * hr, hr)]
        return self.outs[w].at[blk, pl.ds(hc * hr + quarter * (hr // 2), hr // 2)]

    def ici(self, w, k, ref, to):
        return pltpu.make_async_remote_copy(src_ref=ref, dst_ref=ref, send_sem=self.sems[0].at[w, k],
                                            recv_sem=self.sems[1].at[w, k], device_id=to, device_id_type=MESH)

    def d2d(self, w, k, ref):
        return pltpu.make_async_remote_copy(src_ref=ref, dst_ref=ref, send_sem=self.sems[2].at[w, k],
                                            recv_sem=self.sems[3].at[w, k], device_id=self.sib, device_id_type=MESH)

    def all(self):
        return range(len(self.outs))

    def start(self):
        for w in self.all():
            mine = self.piece(w, self.b, self.c)
            self.ici(w, 0, mine, self.xn).start()
            self.ici(w, 1, mine, self.yn).start()

    def hop(self, ws=None):
        c = self.c
        for w in ws or self.all():
            self.ici(w, 0, self.piece(w, self.bx, c), self.xn).wait_recv()
            self.ici(w, 1, self.piece(w, self.by, c), self.yn).wait_recv()
            self.ici(w, 2, self.piece(w, self.bx, c, 0), self.yn).start()
            self.ici(w, 3, self.piece(w, self.by, c, 1), self.xn).start()
            self.d2d(w, 0, self.piece(w, self.bx, c)).start()
            self.d2d(w, 1, self.piece(w, self.by, c)).start()

    def near_ready(self, ws=None):
        for w in ws or self.all():
            self.d2d(w, 0, self.piece(w, self.bx, 1 - self.c)).wait_recv()
            self.d2d(w, 1, self.piece(w, self.by, 1 - self.c)).wait_recv()

    def far(self, ws=None):
        c = self.c
        for w in ws or self.all():
            self.ici(w, 2, self.piece(w, self.bd, c, 0), self.yn).wait_recv()
            self.ici(w, 3, self.piece(w, self.bd, c, 1), self.xn).wait_recv()
            self.d2d(w, 2, self.piece(w, self.bd, c, 0)).start()
            self.d2d(w, 3, self.piece(w, self.bd, c, 1)).start()

    def far_ready(self, ws=None):
        for w in ws or self.all():
            self.d2d(w, 2, self.piece(w, self.bd, 1 - self.c, 0)).wait_recv()
            self.d2d(w, 3, self.piece(w, self.bd, 1 - self.c, 1)).wait_recv()

    def drain(self):
        c = self.c
        for w in self.all():
            mine = self.piece(w, self.b, c)
            self.ici(w, 0, mine, self.xn).wait_send()
            self.ici(w, 1, mine, self.yn).wait_send()
            self.ici(w, 2, self.piece(w, self.bx, c, 0), self.yn).wait_send()
            self.ici(w, 3, self.piece(w, self.by, c, 1), self.xn).wait_send()
            self.d2d(w, 0, self.piece(w, self.bx, c)).wait_send()
            self.d2d(w, 1, self.piece(w, self.by, c)).wait_send()
            self.d2d(w, 2, self.piece(w, self.bd, c, 0)).wait_send()
            self.d2d(w, 3, self.piece(w, self.bd, c, 1)).wait_send()


def _gather_sems(nw):
    return [pltpu.SemaphoreType.DMA((max(nw, 1), 4))] * 4


def _mixer_fwd(proj, cw8, lng, lnb, wc, bsb, fulls, tm=256):
    t = proj.shape[0]
    nt = t // tm
    nch = tm // CH
    nw = len(fulls)

    def body(*refs):
        p_ref, cw_ref, lng_ref, lnb_ref, wc_ref, bsb_ref = refs[:6]
        mix_ref = refs[6 + nw]
        w_outs = refs[7 + nw:7 + 2 * nw]
        prev_ref = refs[7 + 2 * nw]
        gather = _Gather(w_outs, *refs[8 + 2 * nw:])

        @pl.when(pl.program_id(0) == 0)
        def _():
            gather.start()
            prev_ref[...] = jnp.zeros_like(prev_ref)

        @pl.when(pl.program_id(0) == nt // 2)
        def _():
            gather.hop()

        @pl.when(pl.program_id(0) == nt - 1)
        def _():
            gather.far()

        rows = lax.broadcasted_iota(jnp.int32, (tm, CH), 0)
        for s in range(HEADS):
            cs = pl.ds(CH * s, CH)

            def slab(k):
                return p_ref[:, pl.ds(k * SLAB + CH * s, CH)].astype(F32)

            gb, gc, xa, za = slab(0), slab(1), slab(2), slab(3)
            cx = gc * xa
            p6 = jnp.broadcast_to(prev_ref[6:7, cs], (tm, CH))
            p7 = jnp.broadcast_to(prev_ref[7:8, cs], (tm, CH))
            c1 = jnp.where(rows == 0, p7, pltpu.roll(cx, 1, 0))
            c2 = jnp.where(rows == 0, p6, jnp.where(rows == 1, p7, pltpu.roll(cx, 2, 0)))
            prev_ref[:, cs] = cx[tm - 8:, :]
            cv = cw_ref[0:1, cs] * c2 + cw_ref[1:2, cs] * c1 + cw_ref[2:3, cs] * cx
            mix_ref[:, cs] = (gb * cv * (za * _sigmoid(za))).astype(BF16)

            u, v, zb = slab(4), slab(5), slab(6)
            ug, vg = _gelu(u), _gelu(v)
            dlt = vg - jnp.mean(vg, axis=-1, keepdims=True)
            vhat = dlt * lax.rsqrt(jnp.mean(dlt * dlt, axis=-1, keepdims=True) + EPS)
            vn = (vhat * lng_ref[:, cs] + lnb_ref[:, cs]).astype(BF16)
            gate = ug * (zb * _sigmoid(zb))
            for c in range(nch):
                rs = slice(CH * c, CH * (c + 1))
                sp = jnp.dot(wc_ref[s], vn[rs], preferred_element_type=F32) + bsb_ref[s]
                mix_ref[rs, pl.ds(SLAB + CH * s, CH)] = (gate[rs] * sp).astype(BF16)

        @pl.when(pl.program_id(0) == nt - 1)
        def _():
            gather.near_ready()
            gather.far_ready()
            gather.drain()

    sems = _gather_sems(nw)
    outs = pl.pallas_call(
        body, grid=(nt,),
        in_specs=[pl.BlockSpec((tm, IN_DIM), lambda i: (i, 0)), _full((8, D)), _full((1, D)), _full((1, D)),
                  _full((HEADS, CH, CH)), _full((HEADS, CH, CH))] + [ANY] * nw,
        out_specs=[pl.BlockSpec((tm, MIX), lambda i: (i, 0))] + [ANY] * nw,
        out_shape=[jax.ShapeDtypeStruct((t, MIX), BF16)] + [jax.ShapeDtypeStruct(f.shape, f.dtype) for f in fulls],
        input_output_aliases={6 + w: 1 + w for w in range(nw)},
        scratch_shapes=[pltpu.VMEM((8, D), F32)] + sems,
        compiler_params=_cp(("arbitrary",), VMEM_LIMIT), name="mixer_fwd")(proj, cw8, lng, lnb, wc, bsb, *fulls)
    return outs[0], outs[1:]


def _mem_fwd(mem, gm, wkv_f):
    n_mem = mem.shape[0]

    def body(mem_ref, gm_ref, w_ref, k_ref, v_ref):
        m, _ = _rms(mem_ref[...], gm_ref[...])
        mb = m.astype(BF16)
        for j in range(N_CHIP):
            dst = k_ref if j < 2 else v_ref
            dst[:, pl.ds(KV_BLK * (j % 2), KV_BLK)] = jnp.dot(mb, w_ref[j], preferred_element_type=F32).astype(BF16)

    return pl.pallas_call(
        body, out_shape=[jax.ShapeDtypeStruct((n_mem, D), BF16), jax.ShapeDtypeStruct((n_mem, D), BF16)],
        compiler_params=_cp(None, VMEM_LIMIT), name="mem_fwd")(mem, gm, wkv_f)


def _tail(x, tgt, mixin, wout, wq, wxo, k, v, g2, g3, tm=512, sub=512):
    t = x.shape[0]
    n_mem = k.shape[0]
    scale = 1.0 / math.sqrt(XD)

    def body(x_ref, tgt_ref, mix_ref, wout_ref, wq_ref, wxo_ref, k_ref, v_ref, g2_ref, g3_ref,
             loss_ref, dmix_ref, dx1b_ref, h2_ref, dq_ref, o_ref, dx2b_ref, dk_ref, dv_ref, dg2_ref, dg3_ref):
        @pl.when(pl.program_id(0) == 0)
        def _():
            loss_ref[...] = jnp.zeros_like(loss_ref)
            dk_ref[...] = jnp.zeros_like(dk_ref)
            dv_ref[...] = jnp.zeros_like(dv_ref)
            dg2_ref[...] = jnp.zeros_like(dg2_ref)
            dg3_ref[...] = jnp.zeros_like(dg3_ref)

        g2, g3 = g2_ref[...], g3_ref[...]
        for sb in range(tm // sub):
            rs = pl.ds(sub * sb, sub)
            x1 = x_ref[rs, :] + jnp.dot(mix_ref[rs, :], wout_ref[...], preferred_element_type=F32)
            h2, r2 = _rms(x1, g2)
            h2b = h2.astype(BF16)
            h2_ref[rs, :] = h2b
            q = jnp.dot(h2b, wq_ref[...], preferred_element_type=F32).astype(BF16)
            probs, outs = [], []
            for hd in range(XH):
                hs = pl.ds(XD * hd, XD)
                s = _bdot_nt(q[:, XD * hd:XD * (hd + 1)], k_ref[:, hs]) * scale
                e = jnp.exp(s - jnp.max(s, axis=-1, keepdims=True))
                p = e / jnp.sum(e, axis=-1, keepdims=True)
                probs.append(p)
                outs.append(_bdot(p, v_ref[:, hs]))
            ob = jnp.concatenate(outs, axis=-1).astype(BF16)
            o_ref[rs, :] = ob
            x2 = x1 + jnp.dot(ob, wxo_ref[...], preferred_element_type=F32)
            y, r3 = _rms(x2, g3)
            diff = y - tgt_ref[rs, :]
            row_loss = jnp.sum(diff * diff, axis=-1, keepdims=True)
            loss_ref[...] += jnp.broadcast_to(jnp.sum(row_loss, axis=0, keepdims=True) * (0.5 / D), loss_ref.shape)

            dx2, dg3 = _rms_bwd(diff * (1.0 / D), x2, r3, g3)
            dg3_ref[...] += dg3
            dx2b = dx2.astype(BF16)
            dx2b_ref[rs, :] = dx2b
            do = _bdot_nt(dx2b, wxo_ref[...])
            dqs = []
            for hd in range(XH):
                hs = pl.ds(XD * hd, XD)
                p = probs[hd]
                do_h = do[:, XD * hd:XD * (hd + 1)]
                dv_ref[:, hs] += _bdot_tn(p, do_h)
                dp = _bdot_nt(do_h, v_ref[:, hs])
                ds = p * (dp - jnp.sum(dp * p, axis=-1, keepdims=True))
                dqs.append(_bdot(ds, k_ref[:, hs]) * scale)
                dk_ref[:, hs] += _bdot_tn(ds, q[:, XD * hd:XD * (hd + 1)]) * scale
            dq = jnp.concatenate(dqs, axis=-1).astype(BF16)
            dq_ref[rs, :] = dq
            dx1n, dg2 = _rms_bwd(_bdot_nt(dq, wq_ref[...]), x1, r2, g2)
            dg2_ref[...] += dg2
            dx1b = (dx2 + dx1n).astype(BF16)
            dx1b_ref[rs, :] = dx1b
            dmix_ref[rs, :] = _bdot_nt(dx1b, wout_ref[...]).astype(BF16)

    tok = lambda w: pl.BlockSpec((tm, w), lambda i: (i, 0))
    return pl.pallas_call(
        body, grid=(t // tm,),
        in_specs=[tok(D), tok(D), tok(MIX), _full((MIX, D), 1), _full((D, D), 1), _full((D, D), 1),
                  _full((n_mem, D), 1), _full((n_mem, D), 1), _full((1, D)), _full((1, D))],
        out_specs=[_full((8, 128)), tok(MIX), tok(D), tok(D), tok(D), tok(D), tok(D),
                   _full((n_mem, D)), _full((n_mem, D)), _full((1, D)), _full((1, D))],
        out_shape=[jax.ShapeDtypeStruct((8, 128), F32), jax.ShapeDtypeStruct((t, MIX), BF16),
                   jax.ShapeDtypeStruct((t, D), BF16),
                   jax.ShapeDtypeStruct((t, D), BF16), jax.ShapeDtypeStruct((t, D), BF16),
                   jax.ShapeDtypeStruct((t, D), BF16), jax.ShapeDtypeStruct((t, D), BF16),
                   jax.ShapeDtypeStruct((n_mem, D), F32), jax.ShapeDtypeStruct((n_mem, D), F32),
                   jax.ShapeDtypeStruct((1, D), F32), jax.ShapeDtypeStruct((1, D), F32)],
        compiler_params=_cp(("arbitrary",), VMEM_LIMIT), name="tail")(x, tgt, mixin, wout, wq, wxo, k, v, g2, g3)


def _mem_bwd(mem, gm, dk, dv, wkv_f):
    def body(mem_ref, gm_ref, dk_ref, dv_ref, w_ref, dw_ref, dwb_ref, dgm_ref):
        mem_v = mem_ref[...]
        m, rm = _rms(mem_v, gm_ref[...])
        mb = m.astype(BF16)
        dm = jnp.zeros_like(mem_v)
        for j in range(N_CHIP):
            src = dk_ref if j < 2 else dv_ref
            dkv = src[:, pl.ds(KV_BLK * (j % 2), KV_BLK)].astype(BF16)
            dw = _bdot_tn(mb, dkv)
            dw_ref[j] = dw
            dwb_ref[j] = dw.astype(BF16)
            dm = dm + _bdot_nt(dkv, w_ref[j])
        dgm_ref[...] = jnp.sum(dm * mem_v * rm, axis=0, keepdims=True)

    return pl.pallas_call(
        body, out_shape=[jax.ShapeDtypeStruct((N_CHIP, D, KV_BLK), F32), jax.ShapeDtypeStruct((N_CHIP, D, KV_BLK), BF16),
                         jax.ShapeDtypeStruct((1, D), F32)],
        compiler_params=_cp(None, VMEM_LIMIT), name="mem_bwd")(mem, gm, dk, dv, wkv_f)


def _mixer_bwd(proj, dmix, cw8, lng, lnb, wc, wct, bsb, win_f, x, dx1, g1, tm=256):
    t = proj.shape[0]
    nt = t // tm
    nch = tm // CH
    hb = 16
    pair = 2 * CH
    assert pair == IN_PIECE

    def body(p_ref, pgc_ref, pxa_ref, dm_ref, cw_ref, lng_ref, lnb_ref, wc_ref, wct_ref, bsb_ref, w_ref, x_ref,
             dx1_ref, g1_ref, dp_ref, dcw_ref, dlng_ref, dlnb_ref, dwc_ref, dbs_ref, gx_ref, dg1_ref,
             next_ref, dh_ref):
        i = pl.program_id(0)

        @pl.when(i == 0)
        def _():
            next_ref[...] = jnp.zeros_like(next_ref)
            dcw_ref[...] = jnp.zeros_like(dcw_ref)
            dlng_ref[...] = jnp.zeros_like(dlng_ref)
            dlnb_ref[...] = jnp.zeros_like(dlnb_ref)
            dwc_ref[...] = jnp.zeros_like(dwc_ref)
            dbs_ref[...] = jnp.zeros_like(dbs_ref)
            dg1_ref[...] = jnp.zeros_like(dg1_ref)

        first_tile = i == nt - 1
        rows = lax.broadcasted_iota(jnp.int32, (tm, CH), 0)
        ones8 = jnp.ones((8, CH), BF16)
        for s in range(HEADS):
            cs = pl.ds(CH * s, CH)

            def slab(k):
                return p_ref[:, pl.ds(k * SLAB + CH * s, CH)].astype(F32)

            gb, gc, xa, za = slab(0), slab(1), slab(2), slab(3)
            da = dm_ref[:, cs].astype(F32)
            cx = gc * xa
            cxp = pgc_ref[:, cs].astype(F32) * pxa_ref[:, cs].astype(F32)
            cxp = jnp.where(first_tile, jnp.zeros_like(cxp), cxp)
            p6 = jnp.broadcast_to(cxp[hb - 2:hb - 1, :], (tm, CH))
            p7 = jnp.broadcast_to(cxp[hb - 1:hb, :], (tm, CH))
            c1 = jnp.where(rows == 0, p7, pltpu.roll(cx, 1, 0))
            c2 = jnp.where(rows == 0, p6, jnp.where(rows == 1, p7, pltpu.roll(cx, 2, 0)))
            w0, w1, w2 = cw_ref[0:1, cs], cw_ref[1:2, cs], cw_ref[2:3, cs]
            cv = w0 * c2 + w1 * c1 + w2 * cx
            sg = _sigmoid(za)
            sa = za * sg
            dcv = da * gb * sa
            dp_ref[:, pl.ds(0 * SLAB + CH * s, CH)] = (da * cv * sa).astype(BF16)
            dp_ref[:, pl.ds(3 * SLAB + CH * s, CH)] = (da * gb * cv * (sg * (1.0 + za * (1.0 - sg)))).astype(BF16)
            n0 = jnp.broadcast_to(next_ref[0:1, cs], (tm, CH))
            n1 = jnp.broadcast_to(next_ref[1:2, cs], (tm, CH))
            u1 = jnp.where(rows == tm - 1, n0, pltpu.roll(dcv, tm - 1, 0))
            u2 = jnp.where(rows == tm - 2, n0, jnp.where(rows == tm - 1, n1, pltpu.roll(dcv, tm - 2, 0)))
            next_ref[:, cs] = dcv[0:8, :]
            dcx = w2 * dcv + w1 * u1 + w0 * u2
            dp_ref[:, pl.ds(1 * SLAB + CH * s, CH)] = (dcx * xa).astype(BF16)
            dp_ref[:, pl.ds(2 * SLAB + CH * s, CH)] = (dcx * gc).astype(BF16)
            dcw_ref[0:1, cs] += jnp.sum(dcv * c2, axis=0, keepdims=True)
            dcw_ref[1:2, cs] += jnp.sum(dcv * c1, axis=0, keepdims=True)
            dcw_ref[2:3, cs] += jnp.sum(dcv * cx, axis=0, keepdims=True)

            u, v, zb = slab(4), slab(5), slab(6)
            db = dm_ref[:, pl.ds(SLAB + CH * s, CH)].astype(F32)
            ug, ugrad = _gelu_parts(u)
            vg, vgrad = _gelu_parts(v)
            dlt = vg - jnp.mean(vg, axis=-1, keepdims=True)
            rstd = lax.rsqrt(jnp.mean(dlt * dlt, axis=-1, keepdims=True) + EPS)
            vhat = dlt * rstd
            lg = lng_ref[:, cs]
            vn = (vhat * lg + lnb_ref[:, cs]).astype(BF16)
            sgb = _sigmoid(zb)
            szb = zb * sgb
            sps, dvns = [], []
            dbs = jnp.zeros((8, CH), F32)
            dwc = jnp.zeros((CH, CH), F32)
            for c in range(nch):
                rs = slice(CH * c, CH * (c + 1))
                sp = jnp.dot(wc_ref[s], vn[rs], preferred_element_type=F32) + bsb_ref[s]
                dsp = (db[rs] * ug[rs] * szb[rs]).astype(BF16)
                dbs = dbs + lax.dot_general(ones8, dsp, (((1,), (1,)), ((), ())), preferred_element_type=F32)
                dwc = dwc + lax.dot_general(dsp, vn[rs], (((1,), (1,)), ((), ())), preferred_element_type=F32)
                dvns.append(jnp.dot(wct_ref[s], dsp, preferred_element_type=F32))
                sps.append(sp)
            sp = jnp.concatenate(sps, axis=0)
            dvn = jnp.concatenate(dvns, axis=0)
            dbs_ref[:, cs] += dbs
            dwc_ref[s] += dwc
            dlng_ref[:, cs] += jnp.sum(dvn * vhat, axis=0, keepdims=True)
            dlnb_ref[:, cs] += jnp.sum(dvn, axis=0, keepdims=True)
            dvhat = dvn * lg
            dvg = rstd * (dvhat - jnp.mean(dvhat, axis=-1, keepdims=True)
                          - vhat * jnp.mean(dvhat * vhat, axis=-1, keepdims=True))
            dp_ref[:, pl.ds(4 * SLAB + CH * s, CH)] = (db * sp * szb * ugrad).astype(BF16)
            dp_ref[:, pl.ds(5 * SLAB + CH * s, CH)] = (dvg * vgrad).astype(BF16)
            dp_ref[:, pl.ds(6 * SLAB + CH * s, CH)] = (db * ug * sp * (sgb * (1.0 + zb * (1.0 - sgb)))).astype(BF16)

            if s % 2 == 1:
                part = None
                for k in range(N_SLAB):
                    col = k * SLAB + pair * (s // 2)
                    blk, off = divmod(col, IN_BLK)
                    term = lax.dot_general(dp_ref[:, pl.ds(col, pair)], w_ref[blk, off // IN_PIECE],
                                           (((1,), (1,)), ((), ())), preferred_element_type=F32)
                    part = term if part is None else part + term
                if s == 1:
                    dh_ref[...] = part
                else:
                    dh_ref[...] += part

        xv = x_ref[...]
        r = lax.rsqrt(jnp.mean(xv * xv, axis=-1, keepdims=True) + EPS)
        dxn, dg = _rms_bwd(dh_ref[...], xv, r, g1_ref[...])
        gx_ref[...] = dx1_ref[...].astype(F32) + dxn
        dg1_ref[0:1, :] += dg

        @pl.when(i == nt - 1)
        def _():
            tril = lax.broadcasted_iota(jnp.int32, (CH, CH), 0) >= lax.broadcasted_iota(jnp.int32, (CH, CH), 1)
            for s in range(HEADS):
                dwc_ref[s] = jnp.where(tril, dwc_ref[s], 0.0)

    rev = lambda i: nt - 1 - i
    halo = lambda col: pl.BlockSpec((hb, SLAB), lambda i: (jnp.maximum(rev(i) * (tm // hb) - 1, 0), col))
    tok = lambda w: pl.BlockSpec((tm, w), lambda i: (rev(i), 0))
    return pl.pallas_call(
        body, grid=(nt,),
        in_specs=[tok(IN_DIM), halo(1), halo(2), tok(MIX), _full((8, D)), _full((1, D)), _full((1, D)),
                  _full((HEADS, CH, CH)), _full((HEADS, CH, CH)), _full((HEADS, CH, CH)),
                  _full((N_CHIP, N_PIECE, D, IN_PIECE), 1), tok(D), tok(D), _full((1, D))],
        out_specs=[tok(IN_DIM), _full((8, D)), _full((1, D)), _full((1, D)), _full((HEADS, CH, CH)), _full((8, D)),
                   tok(D), _full((8, D))],
        out_shape=[jax.ShapeDtypeStruct((t, IN_DIM), BF16), jax.ShapeDtypeStruct((8, D), F32),
                   jax.ShapeDtypeStruct((1, D), F32), jax.ShapeDtypeStruct((1, D), F32),
                   jax.ShapeDtypeStruct((HEADS, CH, CH), F32), jax.ShapeDtypeStruct((8, D), F32),
                   jax.ShapeDtypeStruct((t, D), F32), jax.ShapeDtypeStruct((8, D), F32)],
        scratch_shapes=[pltpu.VMEM((8, D), F32), pltpu.VMEM((tm, D), F32)],
        compiler_params=_cp(("arbitrary",), VMEM_LIMIT), name="mixer_bwd")(
            proj, proj, proj, dmix, cw8, lng, lnb, wc, wct, bsb, win_f, x, dx1, g1)


def _grad_matmul(a, b, after, *, by_cols, name, tk=1024):
    t, m = a.shape
    n = b.shape[1]
    nk = t // tk
    nj = N_CHIP if by_cols else 1
    bn = n // nj

    def body(a_ref, b_ref, after_ref, o_ref, ob_ref):
        kk = pl.program_id(1)
        part = lax.dot_general(a_ref[...], b_ref[...], (((0,), (0,)), ((), ())), preferred_element_type=F32)

        @pl.when(kk == 0)
        def _():
            o_ref[...] = part

        @pl.when(kk > 0)
        def _():
            o_ref[...] += part

        @pl.when(kk == nk - 1)
        def _():
            ob_ref[...] = o_ref[...].astype(BF16)

    a_spec = pl.BlockSpec((tk, m), lambda j, k: (k, 0))
    b_spec = pl.BlockSpec((tk, bn), lambda j, k: (k, j))
    o_spec = pl.BlockSpec((None, m, bn), lambda j, k: (j, 0, 0))
    o32, o16 = pl.pallas_call(
        body, grid=(nj, nk), in_specs=[a_spec, b_spec, ANY], out_specs=[o_spec, o_spec],
        out_shape=[jax.ShapeDtypeStruct((nj, m, bn), F32), jax.ShapeDtypeStruct((nj, m, bn), BF16)],
        compiler_params=_cp(("parallel", "arbitrary"), VMEM_LIMIT), name=name)(a, b, after)
    if by_cols:
        return o32, o16
    return o32.reshape(N_CHIP, m // N_CHIP, n), o16.reshape(N_CHIP, m // N_CHIP, n)


def _coords():
    x, y, c = lax.axis_index("x"), lax.axis_index("y"), lax.axis_index("c")
    chips = [(1 - x, y), (x, 1 - y), (1 - x, 1 - y)]
    return x, y, c, chips


def _pair_reduce(c_idx, grads, grads_b, smalls, name):
    ng, ns = len(grads), len(smalls)
    halves = [g.shape[1] // 2 for g in grads]

    def body(c_ref, *refs):
        g_in, gb_any = refs[:ng], refs[ng:2 * ng]
        s_own, s_any = refs[2 * ng:2 * ng + ns], refs[2 * ng + ns:2 * ng + 2 * ns]
        o = refs[2 * ng + 2 * ns:4 * ng + 3 * ns]
        lands = refs[4 * ng + 3 * ns:5 * ng + 4 * ns]
        send, recv = refs[5 * ng + 4 * ns:]
        x, y, c, _ = _coords()
        j = pl.program_id(0)

        def big(i, blk):
            return pltpu.make_async_remote_copy(
                src_ref=gb_any[i].at[blk, pl.ds((1 - c) * halves[i], halves[i])], dst_ref=lands[i].at[blk],
                send_sem=send.at[i, blk], recv_sem=recv.at[i, blk], device_id=(x, y, 1 - c), device_id_type=MESH)

        def small(i):
            return pltpu.make_async_remote_copy(
                src_ref=s_any[i].at[1 - c], dst_ref=lands[ng + i],
                send_sem=send.at[ng + i, 0], recv_sem=recv.at[ng + i, 0], device_id=(x, y, 1 - c), device_id_type=MESH)

        @pl.when(j == 0)
        def _():
            for blk in range(N_CHIP):
                for i in range(ng):
                    big(i, blk).start()
            for i in range(ns):
                small(i).start()

        for i in range(ng):
            big(i, j).wait_recv()
            tot = g_in[i][...] + lands[i][j].astype(F32)
            o[i][...] = tot
            o[ng + i][...] = tot.astype(BF16)

        @pl.when(j == N_CHIP - 1)
        def _():
            for i in range(ns):
                small(i).wait_recv()
                o[2 * ng + i][...] = s_own[i][...] + lands[ng + i][...]
                small(i).wait_send()
            for blk in range(N_CHIP):
                for i in range(ng):
                    big(i, blk).wait_send()

    in_specs = [pl.BlockSpec((None, None, halves[i], g.shape[2]), lambda b, c: (b, c[0], 0, 0)) for i, g in enumerate(grads)]
    in_specs += [ANY] * ng
    in_specs += [pl.BlockSpec((None, s.shape[0] // 2, s.shape[1]), lambda b, c: (c[0], 0, 0)) for s in smalls]
    in_specs += [ANY] * ns
    blk = [pl.BlockSpec((None, halves[i], g.shape[2]), lambda b, c: (b, 0, 0)) for i, g in enumerate(grads)]
    out_specs = blk + blk + [pl.BlockSpec((s.shape[0] // 2, s.shape[1]), lambda b, c: (0, 0)) for s in smalls]
    out_shape = [jax.ShapeDtypeStruct((N_CHIP, halves[i], g.shape[2]), F32) for i, g in enumerate(grads)]
    out_shape += [jax.ShapeDtypeStruct((N_CHIP, halves[i], g.shape[2]), BF16) for i, g in enumerate(grads)]
    out_shape += [jax.ShapeDtypeStruct((s.shape[0] // 2, s.shape[1]), F32) for s in smalls]
    scratch = [pltpu.VMEM((N_CHIP, halves[i], g.shape[2]), BF16) for i, g in enumerate(grads)]
    scratch += [pltpu.VMEM((s.shape[0] // 2, s.shape[1]), F32) for s in smalls]
    scratch += [pltpu.SemaphoreType.DMA((ng + ns, N_CHIP)), pltpu.SemaphoreType.DMA((ng + ns, N_CHIP))]
    grads4 = [g.reshape(N_CHIP, 2, halves[i], g.shape[2]) for i, g in enumerate(grads)]
    smalls3 = [s.reshape(2, s.shape[0] // 2, s.shape[1]) for s in smalls]
    return pl.pallas_call(
        body, out_shape=out_shape,
        grid_spec=pltpu.PrefetchScalarGridSpec(num_scalar_prefetch=1, grid=(N_CHIP,), in_specs=in_specs,
                                               out_specs=out_specs, scratch_shapes=scratch),
        compiler_params=_cp(("arbitrary",), VMEM_LIMIT), name=name)(c_idx, *grads4, *grads_b, *smalls3, *smalls3)


def _grad_matmul_pair(c_idx, a, b, smalls, after, *, name, tk=2048):
    t, m = a.shape
    bn = b.shape[1] // N_CHIP
    nk = t // tk
    hr = m // 2
    ns = len(smalls)

    def body(c_ref, a_ref, b_ref, *refs):
        s_own, s_any = refs[:ns], refs[ns:2 * ns]
        o32, o16 = refs[2 * ns + 1], refs[2 * ns + 2]
        o_small = refs[2 * ns + 3:3 * ns + 3]
        acc, tb, land, st16 = refs[3 * ns + 3:3 * ns + 7]
        s_land, s_stage = refs[3 * ns + 7:4 * ns + 7], refs[4 * ns + 7:5 * ns + 7]
        send, recv, loc = refs[5 * ns + 7:]
        x, y, c, _ = _coords()
        sibling = dict(device_id=(x, y, 1 - c), device_id_type=MESH)
        j, kk = pl.program_id(0), pl.program_id(1)
        mine = pl.ds(pl.multiple_of(c * hr, hr), hr)
        theirs = pl.ds(pl.multiple_of((1 - c) * hr, hr), hr)

        def to_sibling(blk):
            return pltpu.make_async_remote_copy(src_ref=tb, dst_ref=land.at[blk], send_sem=send.at[blk],
                                                recv_sem=recv.at[blk], **sibling)

        def small(i):
            return pltpu.make_async_remote_copy(src_ref=s_any[i].at[1 - c], dst_ref=s_land[i], send_sem=send.at[N_CHIP + i],
                                                recv_sem=recv.at[N_CHIP + i], **sibling)

        def written(blk):
            return (pltpu.make_async_copy(acc.at[blk % 2, mine], o32.at[blk], loc.at[0]),
                    pltpu.make_async_copy(st16, o16.at[blk], loc.at[1]))

        def finish(blk):
            to_sibling(blk).wait_recv()

            @pl.when(blk > 0)
            def _():
                for cp in written(blk - 1):
                    cp.wait()

            tot = acc[blk % 2, mine, :] + land[blk].astype(F32)
            acc[blk % 2, mine, :] = tot
            st16[...] = tot.astype(BF16)
            for cp in written(blk):
                cp.start()

        def small_out(i):
            return pltpu.make_async_copy(s_stage[i], o_small[i], loc.at[2 + i])

        @pl.when((j == 0) & (kk == 0))
        def _():
            for i in range(ns):
                small(i).start()

        @pl.when((j == 1) & (kk == 0))
        def _():
            for i in range(ns):
                small(i).wait_recv()
                s_stage[i][...] = s_own[i][...] + s_land[i][...]
                small_out(i).start()

        @pl.when((j > 0) & (kk == 0))
        def _():
            finish(j - 1)

        part = lax.dot_general(a_ref[...], b_ref[...], (((0,), (0,)), ((), ())), preferred_element_type=F32)
        slot = lax.rem(j, 2)

        @pl.when(kk == 0)
        def _():
            acc[slot] = part

        @pl.when(kk > 0)
        def _():
            acc[slot] += part

        @pl.when(kk == nk - 1)
        def _():
            @pl.when(j > 0)
            def _():
                to_sibling(j - 1).wait_send()

            tb[...] = acc[slot, theirs, :].astype(BF16)
            to_sibling(j).start()

        @pl.when((j == N_CHIP - 1) & (kk == nk - 1))
        def _():
            finish(j)
            for i in range(ns):
                small_out(i).wait()
                small(i).wait_send()
            for cp in written(j):
                cp.wait()
            to_sibling(j).wait_send()

    halves = [(s.shape[0] // 2, s.shape[1]) for s in smalls]
    in_specs = [pl.BlockSpec((tk, m), lambda j, k, c: (k, 0)), pl.BlockSpec((tk, bn), lambda j, k, c: (k, j))]
    in_specs += [pl.BlockSpec((None,) + h, lambda j, k, c: (c[0], 0, 0)) for h in halves] + [ANY] * ns + [ANY]
    out_shape = [jax.ShapeDtypeStruct((N_CHIP, hr, bn), F32), jax.ShapeDtypeStruct((N_CHIP, hr, bn), BF16)]
    out_shape += [jax.ShapeDtypeStruct(h, F32) for h in halves]
    scratch = [pltpu.VMEM((2, m, bn), F32), pltpu.VMEM((hr, bn), BF16),
               pltpu.VMEM((N_CHIP, hr, bn), BF16), pltpu.VMEM((hr, bn), BF16)]
    scratch += [pltpu.VMEM(h, F32) for h in halves] * 2
    scratch += [pltpu.SemaphoreType.DMA((N_CHIP + ns,)), pltpu.SemaphoreType.DMA((N_CHIP + ns,)),
                pltpu.SemaphoreType.DMA((2 + ns,))]
    smalls3 = [s.reshape((2,) + h) for s, h in zip(smalls, halves)]
    outs = pl.pallas_call(
        body, out_shape=out_shape,
        grid_spec=pltpu.PrefetchScalarGridSpec(num_scalar_prefetch=1, grid=(N_CHIP, nk), in_specs=in_specs,
                                               out_specs=[ANY] * (2 + ns), scratch_shapes=scratch),
        compiler_params=_cp(("arbitrary", "arbitrary"), VMEM_LIMIT), name=name)(c_idx, a, b, *smalls3, *smalls3, after)
    return outs[0], outs[1], list(outs[2:])


_HBM = pl.BlockSpec(memory_space=pltpu.HBM)
_SEM = pl.BlockSpec(memory_space=pltpu.SEMAPHORE)


def _split_copies(ins, lands, ng, send, recv, arriving):
    x, y, c, chips = _coords()
    b = 2 * x + y
    copies = []
    for i in range(len(ins)):
        for k in range(3):
            blk = 2 * chips[k][0] + chips[k][1]
            src, dst, got = (ins[i].at[blk], lands[i].at[k], lands[i].at[k]) if i < ng else (ins[i], lands[i].at[b], lands[i].at[blk])
            sems = dict(send_sem=send.at[3 * i + k], recv_sem=recv.at[3 * i + k], device_id=(*chips[k], c), device_id_type=MESH)
            if arriving:
                copies.append(pltpu.make_async_remote_copy(src_ref=got, dst_ref=got, **sems))
            else:
                copies.append(pltpu.make_async_remote_copy(src_ref=src, dst_ref=dst, **sems))
    return copies


def _exchange_begin(sums_b, smalls, name):
    ng, n = len(sums_b), len(sums_b) + len(smalls)
    srcs = list(sums_b) + list(smalls)
    lands = [lax.empty((3,) + g.shape[1:], g.dtype) for g in sums_b] + [lax.empty((N_CHIP,) + s.shape, s.dtype) for s in smalls]

    def body(*refs):
        ins, land_refs = refs[:n], refs[n:2 * n]
        send, recv = refs[2 * n], refs[2 * n + 1]
        token = refs[4 * n + 2]
        for cp in _split_copies(ins, land_refs, ng, send, recv, False):
            cp.start()
        token[...] = jnp.zeros_like(token)

    hbm = lambda a: pltpu.HBM(a.shape, a.dtype)
    outs = pl.pallas_call(
        body, name=name,
        out_shape=(pltpu.SemaphoreType.DMA((3 * n,)), pltpu.SemaphoreType.DMA((3 * n,)), *[hbm(a) for a in srcs + lands],
                   jax.ShapeDtypeStruct((8, 128), F32)),
        in_specs=[_HBM] * (2 * n), out_specs=(_SEM, _SEM, *[_HBM] * (2 * n), pl.BlockSpec(memory_space=pltpu.VMEM)),
        input_output_aliases={i: 2 + i for i in range(2 * n)},
        compiler_params=pltpu.CompilerParams(has_side_effects=pltpu.SideEffectType.DATAFLOW_SIDE_EFFECTING),
    )(*[pltpu.with_memory_space_constraint(a, pltpu.HBM) for a in srcs + lands])
    return outs[0], outs[1], list(outs[2:2 + n]), list(outs[2 + n:2 + 2 * n]), outs[2 + 2 * n]


def _exchange_end(send, recv, srcs, lands, ng, which, after, name):
    n = len(srcs)
    after = list(after)

    def body(*refs):
        ins, land_refs = refs[:n], refs[n:2 * n]
        send_ref, recv_ref = refs[2 * n], refs[2 * n + 1]
        outgoing = _split_copies(ins, land_refs, ng, send_ref, recv_ref, False)
        arriving = _split_copies(ins, land_refs, ng, send_ref, recv_ref, True)
        for i in which:
            for cp in outgoing[3 * i:3 * i + 3]:
                cp.wait_send()
        for i in which:
            for cp in arriving[3 * i:3 * i + 3]:
                cp.wait_recv()

    hbm = lambda a: pltpu.HBM(a.shape, a.dtype)
    outs = pl.pallas_call(
        body, name=name, out_shape=tuple(hbm(a) for a in list(srcs) + list(lands)),
        in_specs=[_HBM] * (2 * n) + [_SEM, _SEM] + [ANY] * len(after), out_specs=tuple([_HBM] * (2 * n)),
        input_output_aliases={i: i for i in range(2 * n)},
        compiler_params=pltpu.CompilerParams(has_side_effects=pltpu.SideEffectType.DATAFLOW_SIDE_EFFECTING),
    )(*srcs, *lands, send, recv, *after)
    return list(outs[:n]), list(outs[n:])


def _chip_reduce(bc_idx, sums, recvd, smalls_slots, smalls_own, name, steps=4):
    ng, ns = len(sums), len(smalls_slots)
    n = ng + ns
    assert steps >= 2
    halves = [g.shape[1] for g in sums] + [s.shape[1] for s in smalls_slots]
    rows = [g.shape[1] // steps for g in sums]

    def body(bc_ref, *refs):
        own, rx = refs[:ng], refs[ng:2 * ng]
        sl = refs[2 * ng:2 * ng + ns]
        sl_own = refs[2 * ng + ns:2 * ng + 2 * ns]
        o = refs[2 * ng + 2 * ns:2 * ng + 2 * ns + n]
        tiles = refs[2 * ng + 2 * ns + n:2 * ng + 2 * ns + 2 * n]
        keep, send, recv = refs[2 * ng + 2 * ns + 2 * n:]
        x, y, c, _ = _coords()
        sibling = dict(device_id=(x, y, 1 - c), device_id_type=MESH)
        r = pl.program_id(0)

        def writes(i, step, slot):
            dst = o[i].at[pl.ds(c * halves[i] + step * rows[i], rows[i])]
            return (pltpu.make_async_copy(tiles[i].at[slot], dst, keep.at[i, slot]),
                    pltpu.make_async_remote_copy(src_ref=tiles[i].at[slot], dst_ref=dst, send_sem=send.at[i, slot],
                                                 recv_sem=recv.at[i, step], **sibling))

        def small_writes(i):
            dst = o[i].at[pl.ds(c * halves[i], halves[i])]
            return (pltpu.make_async_copy(tiles[i], dst, keep.at[i, 0]),
                    pltpu.make_async_remote_copy(src_ref=tiles[i], dst_ref=dst, send_sem=send.at[i, 0],
                                                 recv_sem=recv.at[i, 0], **sibling))

        def arriving(i, step, nrows):
            dst = o[i].at[pl.ds((1 - c) * halves[i] + step * nrows, nrows)]
            return pltpu.make_async_remote_copy(src_ref=dst, dst_ref=dst, send_sem=send.at[i, 0], recv_sem=recv.at[i, step],
                                                **sibling)

        def finish(step, slot):
            for i in range(ng):
                local, remote = writes(i, step, slot)
                local.wait()
                remote.wait_send()

        @pl.when(r >= 2)
        def _():
            finish(r - 2, r % 2)

        for i in range(ng):
            tot = own[i][...]
            for j in range(3):
                tot = tot + rx[i][j].astype(F32)
            tiles[i][r % 2] = tot
            for cp in writes(i, r, r % 2):
                cp.start()

        @pl.when(r == 0)
        def _():
            for i in range(ns):
                term = [jnp.where(bc_ref[0] == kk, sl_own[i][...], sl[i][kk]) for kk in range(N_CHIP)]
                tiles[ng + i][...] = ((term[0] + term[1]) + term[2]) + term[3]
                for cp in small_writes(ng + i):
                    cp.start()

        @pl.when(r == steps - 1)
        def _():
            finish(steps - 2, (steps - 2) % 2)
            finish(steps - 1, (steps - 1) % 2)
            for i in range(ns):
                local, remote = small_writes(ng + i)
                local.wait()
                remote.wait_send()
                arriving(ng + i, 0, halves[ng + i]).wait_recv()
            for i in range(ng):
                for step in range(steps):
                    arriving(i, step, rows[i]).wait_recv()

    in_specs = [pl.BlockSpec((None, rows[i], g.shape[2]), lambda r, bc: (bc[0], r, 0)) for i, g in enumerate(sums)]
    in_specs += [pl.BlockSpec((3, rows[i], g.shape[2]), lambda r, bc: (0, r, 0)) for i, g in enumerate(sums)]
    in_specs += [pl.BlockSpec(s.shape, lambda r, bc: (0, 0, 0)) for s in smalls_slots]
    in_specs += [pl.BlockSpec(s.shape[1:], lambda r, bc: (0, 0)) for s in smalls_slots]
    out_shape = [jax.ShapeDtypeStruct((2 * g.shape[1], g.shape[2]), F32) for g in sums]
    out_shape += [jax.ShapeDtypeStruct((2 * s.shape[1], s.shape[2]), F32) for s in smalls_slots]
    scratch = [pltpu.VMEM((2, rows[i], g.shape[2]), F32) for i, g in enumerate(sums)]
    scratch += [pltpu.VMEM(s.shape[1:], F32) for s in smalls_slots]
    scratch += [pltpu.SemaphoreType.DMA((n, 2)), pltpu.SemaphoreType.DMA((n, 2)), pltpu.SemaphoreType.DMA((n, steps))]
    return list(pl.pallas_call(
        body, out_shape=out_shape,
        grid_spec=pltpu.PrefetchScalarGridSpec(num_scalar_prefetch=1, grid=(steps,), in_specs=in_specs,
                                               out_specs=[ANY] * n, scratch_shapes=scratch),
        compiler_params=_cp(("arbitrary",), VMEM_LIMIT), name=name)(bc_idx, *sums, *recvd, *smalls_slots, *smalls_own))


def _adamw_math(w, g, m, v):
    m2 = ADAM_B1 * m + (1.0 - ADAM_B1) * g
    v2 = ADAM_B2 * v + (1.0 - ADAM_B2) * (g * g)
    m_hat = m2 / (1.0 - ADAM_B1 ** ADAM_STEP)
    v_hat = v2 / (1.0 - ADAM_B2 ** ADAM_STEP)
    delta = -ADAM_LR * (m_hat / (jnp.sqrt(v_hat) + ADAM_EPS) + ADAM_WD * w)
    return delta, m2, v2


def _adamw_big(ws, gs, ms, vs, name, steps=8):
    n = len(ws)

    def body(*refs):
        for i in range(n):
            w_ref, g_ref, m_ref, v_ref = (refs[k * n + i] for k in range(4))
            d_ref, m2_ref, v2_ref, g2_ref = (refs[(4 + k) * n + i] for k in range(4))
            gv = g_ref[...]
            d_ref[...], m2_ref[...], v2_ref[...] = _adamw_math(w_ref[...], gv, m_ref[...], v_ref[...])
            g2_ref[...] = gv

    specs = [pl.BlockSpec((w.shape[0] // steps, w.shape[1]), lambda i: (i, 0)) for w in ws]
    shapes = [jax.ShapeDtypeStruct(w.shape, F32) for w in ws]
    outs = pl.pallas_call(
        body, grid=(steps,), in_specs=specs * 4, out_specs=specs * 4, out_shape=shapes * 4,
        compiler_params=_cp(("parallel",), VMEM_LIMIT), name=name)(*ws, *gs, *ms, *vs)
    return [tuple(outs[k * n + i] for k in range(4)) for i in range(n)]


def _reduce_adamw(bc_idx, sums, recvd, ws, ms, vs, name, steps=2):
    n = len(sums)
    rows = [g.shape[1] // steps for g in sums]

    def body(bc_ref, *refs):
        own, rx = refs[:n], refs[n:2 * n]
        w_in, m_in, v_in = refs[2 * n:3 * n], refs[3 * n:4 * n], refs[4 * n:5 * n]
        d_out, m_out, v_out, g_out = (refs[(5 + k) * n:(6 + k) * n] for k in range(4))
        mine, theirs = refs[9 * n:10 * n], refs[10 * n:11 * n]
        send, recv = refs[11 * n], refs[11 * n + 1]
        x, y, c, _ = _coords()
        r = pl.program_id(0)

        def to_sibling(i, tile):
            return pltpu.make_async_remote_copy(src_ref=mine[i].at[tile], dst_ref=theirs[i].at[tile], send_sem=send.at[i, tile],
                                                recv_sem=recv.at[i, tile], device_id=(x, y, 1 - c), device_id_type=MESH)

        def step(i, g):
            d_out[i][...], m_out[i][...], v_out[i][...] = _adamw_math(w_in[i][...], g, m_in[i][...], v_in[i][...])
            g_out[i][...] = g

        @pl.when(r < steps)
        def _():
            for i in range(n):
                tot = own[i][...]
                for j in range(3):
                    tot = tot + rx[i][j].astype(F32)
                mine[i][r] = tot
                to_sibling(i, r).start()
                step(i, tot)

        @pl.when(r >= steps)
        def _():
            for i in range(n):
                to_sibling(i, r - steps).wait_recv()
                step(i, theirs[i][r - steps])

        @pl.when(r == 2 * steps - 1)
        def _():
            for i in range(n):
                for tile in range(steps):
                    to_sibling(i, tile).wait_send()

    def summed(r):
        return jnp.minimum(r, steps - 1)

    def stepped(r, bc):
        return jnp.where(r < steps, bc[1] * steps + r, (1 - bc[1]) * steps + r - steps)

    in_specs = [pl.BlockSpec((None, rows[i], g.shape[2]), lambda r, bc: (bc[0], summed(r), 0)) for i, g in enumerate(sums)]
    in_specs += [pl.BlockSpec((3, rows[i], g.shape[2]), lambda r, bc: (0, summed(r), 0)) for i, g in enumerate(sums)]
    shard = [pl.BlockSpec((rows[i], g.shape[2]), lambda r, bc: (stepped(r, bc), 0)) for i, g in enumerate(sums)]
    in_specs += shard * 3
    shapes = [jax.ShapeDtypeStruct(w.shape, F32) for w in ws]
    scratch = [pltpu.VMEM((steps, rows[i], g.shape[2]), F32) for i, g in enumerate(sums)] * 2
    scratch += [pltpu.SemaphoreType.DMA((n, steps)), pltpu.SemaphoreType.DMA((n, steps))]
    outs = pl.pallas_call(
        body, out_shape=shapes * 4,
        grid_spec=pltpu.PrefetchScalarGridSpec(num_scalar_prefetch=1, grid=(2 * steps,), in_specs=in_specs,
                                               out_specs=shard * 4, scratch_shapes=scratch),
        compiler_params=_cp(("arbitrary",), VMEM_LIMIT), name=name)(bc_idx, *sums, *recvd, *ws, *ms, *vs)
    return [tuple(outs[k * n + i] for k in range(4)) for i in range(n)]


def _adamw_small(groups):
    n = len(groups)

    def body(*refs):
        for i in range(n):
            w_ref, g_ref, m_ref, v_ref = refs[4 * i:4 * i + 4]
            d_ref, m2_ref, v2_ref = refs[4 * n + 3 * i:4 * n + 3 * i + 3]
            d_ref[...], m2_ref[...], v2_ref[...] = _adamw_math(w_ref[...], g_ref[...], m_ref[...], v_ref[...])

    flat = [a for grp in groups for a in grp]
    out_shape = [jax.ShapeDtypeStruct(grp[0].shape, F32) for grp in groups for _ in range(3)]
    outs = pl.pallas_call(body, out_shape=out_shape, name="adamw_small")(*flat)
    return [tuple(outs[3 * i:3 * i + 3]) for i in range(n)]


def kernel(x, mem, norm_mix_g, w_in, conv_w, gm_ln_g, gm_ln_b, gm_ws, gm_bs, w_out, norm_x_g, norm_mem_g, w_q, w_kv, w_xo, norm_final_g, loss_target, m_norm_mix_g, m_w_in, m_conv_w, m_gm_ln_g, m_gm_ln_b, m_gm_ws, m_gm_bs, m_w_out, m_norm_x_g, m_norm_mem_g, m_w_q, m_w_kv, m_w_xo, m_norm_final_g, v_norm_mix_g, v_w_in, v_conv_w, v_gm_ln_g, v_gm_ln_b, v_gm_ws, v_gm_bs, v_w_out, v_norm_x_g, v_norm_mem_g, v_w_q, v_w_kv, v_w_xo, v_norm_final_g):
    t = x.shape[1]
    xi = lax.axis_index("x")
    yi = lax.axis_index("y")
    ci = lax.axis_index("c")
    b_idx = jnp.reshape(2 * xi + yi, (1,)).astype(jnp.int32)
    c_idx = jnp.reshape(ci, (1,)).astype(jnp.int32)

    x2d, mem2d, tgt = x[0], mem[0], loss_target[0]
    big = [w_in[0], w_out[0], w_q[0], w_kv[0], w_xo[0]]
    big_m = [m_w_in[0], m_w_out[0], m_w_q[0], m_w_kv[0], m_w_xo[0]]
    big_v = [v_w_in[0], v_w_out[0], v_w_q[0], v_w_kv[0], v_w_xo[0]]
    g3 = norm_final_g.reshape(1, D)

    def pad8(a):
        return jnp.pad(a, ((0, 8 - a.shape[0]), (0, 0)))

    own_blocks = _cast_shards(b_idx, big)

    tril = jnp.tril(jnp.ones((CH, CH), bool))
    wc32 = jnp.where(tril[None], gm_ws[0], 0.0)
    wc = wc32.astype(BF16)
    wct = jnp.swapaxes(wc32, 1, 2).astype(BF16)
    bsb = jnp.broadcast_to(gm_bs[0][:, :, None], (HEADS, CH, CH))

    blk = 2 * xi + yi
    near = [blk ^ (2 >> (k % 2)) for k in range(2 * N_PIECE)]
    seq_blk = jnp.stack([blk] * N_PIECE + near + [blk ^ 3] * N_PIECE)
    seq_piece = jnp.asarray(list(range(N_PIECE)) + [k // 2 for k in range(2 * N_PIECE)] + list(range(N_PIECE)))
    seq = jnp.stack([seq_blk, seq_piece, seq_blk * N_PIECE + seq_piece]).astype(jnp.int32)
    proj, hb, win_f, cw8, (wq_f,) = _proj_gather(
        seq, x2d, norm_mix_g, own_blocks[0], pad8(conv_w[0]), [own_blocks[2]])
    mixin, (wout_f, wkv_f, wxo_f) = _mixer_fwd(
        proj, cw8, gm_ln_g, gm_ln_b, wc, bsb, [own_blocks[1], own_blocks[3], own_blocks[4]])
    wout2, wq2, wxo2 = wout_f.reshape(MIX, D), wq_f.reshape(D, D), wxo_f.reshape(D, D)
    k, v = _mem_fwd(mem2d, norm_mem_g, wkv_f)

    (loss_tile, dmix, dx1b, h2b, dq, ob, dx2b, dk, dv, dg2, dg3) = _tail(
        x2d, tgt, mixin, wout2, wq2, wxo2, k, v, norm_x_g, g3)
    dwkv, dwkv_b, dgm = _mem_bwd(mem2d, norm_mem_g, dk, dv, wkv_f)
    dproj, dcw, dlng, dlnb, dwc, dbs8, grad_x, dg1 = _mixer_bwd(
        proj, dmix, cw8, gm_ln_g, gm_ln_b, wc, wct, bsb, win_f, x2d, dx1b, norm_mix_g)

    bc_idx = jnp.concatenate([b_idx, c_idx])
    zero = jnp.zeros((1, D), F32)
    loss_row = jnp.broadcast_to(loss_tile[0:1, 0:1], (1, D))
    sv = jnp.concatenate([dg1[0:1], dg2, dgm, dg3, dlng, dlnb, dbs8[0:1], loss_row, dcw], axis=0)
    sw = dwc.reshape(HEADS * CH, CH)
    dwin_sum, dwin_sum_b, psmall = _grad_matmul_pair(c_idx, hb, dproj, [sv, sw], dgm, name="grad_w_in")
    sums_b = [dwin_sum]
    send_b, recv_b, src_b, land_b, token_b = _exchange_begin([dwin_sum_b], psmall, "exchange_b_begin")

    dwxo, dwxo_b = _grad_matmul(ob, dx2b, token_b, by_cols=False, name="grad_w_xo", tk=2048)
    dwq, dwq_b = _grad_matmul(h2b, dq, token_b, by_cols=False, name="grad_w_q", tk=2048)
    dwout, dwout_b = _grad_matmul(mixin, dx1b, token_b, by_cols=False, name="grad_w_out")
    ps_a = _pair_reduce(c_idx, [dwout, dwkv, dwq, dwxo], [dwout_b, dwkv_b, dwq_b, dwxo_b], [], "pair_reduce_a")
    sums_a, sums_a_b = list(ps_a[:4]), list(ps_a[4:8])
    send_a, recv_a, src_a, land_a, token_a = _exchange_begin(sums_a_b, [], "exchange_a_begin")

    _, rx2b = _exchange_end(send_b, recv_b, src_b, land_b, 1, [0, 1, 2], [token_a], "exchange_b_end")
    gwin, svf, swf = _chip_reduce(bc_idx, sums_b, rx2b[:1], rx2b[1:], psmall, "chip_reduce_b")
    out_b = _adamw_big(big[:1], [gwin], big_m[:1], big_v[:1], "adamw_w_in")[0]

    def vec_pack(a1, a2, am, a3, lg, lb, bs):
        return jnp.concatenate([a1, a2, am, a3.reshape(1, D), lg, lb, bs.reshape(1, D), zero], axis=0)

    wv = vec_pack(norm_mix_g, norm_x_g, norm_mem_g, norm_final_g, gm_ln_g, gm_ln_b, gm_bs)
    mv = vec_pack(m_norm_mix_g, m_norm_x_g, m_norm_mem_g, m_norm_final_g, m_gm_ln_g, m_gm_ln_b, m_gm_bs)
    vv = vec_pack(v_norm_mix_g, v_norm_x_g, v_norm_mem_g, v_norm_final_g, v_gm_ln_g, v_gm_ln_b, v_gm_bs)
    loss = svf[7, 0]
    gcw = lax.dynamic_slice_in_dim(svf[8:16], blk * (D // N_CHIP), D // N_CHIP, axis=1)
    gws = swf
    gv = svf[0:8]
    (dv_, mv_, vv_), (dc_, mc_, vc_), (dws_, mws_, vws_) = _adamw_small([
        (wv, gv, mv, vv),
        (pad8(conv_w[0]), gcw, pad8(m_conv_w[0]), pad8(v_conv_w[0])),
        (gm_ws.reshape(HEADS * CH, CH), gws, m_gm_ws.reshape(HEADS * CH, CH), v_gm_ws.reshape(HEADS * CH, CH))])

    def finish_a(part, src, land, after, tag):
        src, land = _exchange_end(send_a, recv_a, src, land, 4, part, after, "exchange_a%s_end" % tag)
        ids = [(1, 3, 2, 4)[i] for i in part]
        outs = _reduce_adamw(bc_idx, [sums_a[i] for i in part], [land[i] for i in part], [big[i] for i in ids],
                             [big_m[i] for i in ids], [big_v[i] for i in ids], "reduce_adamw_a" + tag)
        return src, land, outs

    src_a, land_a, (out_wout, out_wkv) = finish_a([0, 1], src_a, land_a, [out_b[0], dv_], "1")
    _, _, (out_wq, out_wxo) = finish_a([2, 3], src_a, land_a, [out_wout[0]], "2")
    big_out = [out_b, out_wout, out_wq, out_wkv, out_wxo]

    def unpack(vecs, cw, ws, bigs):
        r = lambda i: vecs[i:i + 1]
        return [r(0), bigs[0][None], cw[0:3][None], r(4), r(5), ws.reshape(1, HEADS, CH, CH), vecs[6].reshape(1, HEADS, CH),
                bigs[1][None], r(1), r(2), bigs[2][None], bigs[3][None], bigs[4][None], vecs[3]]

    grads_out = unpack(gv, gcw, gws, [o[3] for o in big_out])
    delta_out = unpack(dv_, dc_, dws_, [o[0] for o in big_out])
    m_out = unpack(mv_, mc_, mws_, [o[1] for o in big_out])
    v_out = unpack(vv_, vc_, vws_, [o[2] for o in big_out])
    return (loss, grad_x[None], *grads_out, *delta_out, *m_out, *v_out)
```

```python
import functools
import math

import jax
import jax.numpy as jnp
from jax import lax
from jax.experimental import pallas as pl
from jax.experimental.pallas import tpu as pltpu

F32 = jnp.float32
BF16 = jnp.bfloat16
MESH = pl.DeviceIdType.MESH

D = 1024
SLAB = 1024
N_SLAB = 7
IN_DIM = N_SLAB * SLAB
MIX = 2 * SLAB
HEADS = 8
CH = 128
XH = 4
XD = D // XH
EPS = 1e-6
GELU_C = math.sqrt(2.0 / math.pi)
GELU_A = 0.044715
N_CHIP = 4
IN_BLK = IN_DIM // N_CHIP
IN_PIECE = 256
N_PIECE = IN_BLK // IN_PIECE
KV_BLK = 2 * D // N_CHIP

ADAM_LR, ADAM_B1, ADAM_B2, ADAM_EPS, ADAM_WD, ADAM_STEP = 0.001, 0.9, 0.999, 1e-08, 0.01, 10

VMEM_LIMIT = 60 * 1024 * 1024


def _cp(sem=None, vmem=None):
    return pltpu.CompilerParams(dimension_semantics=sem, vmem_limit_bytes=vmem)


def _full(shape, buffers=None):
    n = len(shape)
    if buffers is None:
        return pl.BlockSpec(shape, lambda *_: (0,) * n)
    return pl.BlockSpec(shape, lambda *_: (0,) * n, pipeline_mode=pl.Buffered(buffers))


ANY = pl.BlockSpec(memory_space=pl.ANY)


def _bdot(a, b):
    return jnp.dot(a.astype(BF16), b.astype(BF16), preferred_element_type=F32)


def _bdot_nt(a, b):
    return lax.dot_general(a.astype(BF16), b.astype(BF16), (((1,), (1,)), ((), ())), preferred_element_type=F32)


def _bdot_tn(a, b):
    return lax.dot_general(a.astype(BF16), b.astype(BF16), (((0,), (0,)), ((), ())), preferred_element_type=F32)


def _rms(x, g):
    r = lax.rsqrt(jnp.mean(x * x, axis=-1, keepdims=True) + EPS)
    return x * r * g, r


def _rms_bwd(dy, x, r, g):
    gdy = dy * g
    dx = r * gdy - x * (r * r * r) * jnp.mean(x * gdy, axis=-1, keepdims=True)
    dg = jnp.sum(dy * x * r, axis=0, keepdims=True)
    return dx, dg


def _gelu_parts(x):
    x2 = x * x
    t = jnp.tanh(GELU_C * (x + GELU_A * x * x2))
    val = 0.5 * x * (1.0 + t)
    grad = 0.5 * (1.0 + t) + 0.5 * x * (1.0 - t * t) * (GELU_C * (1.0 + 3.0 * GELU_A * x2))
    return val, grad


def _gelu(x):
    return 0.5 * x * (1.0 + jnp.tanh(GELU_C * (x + GELU_A * x * x * x)))


def _sigmoid(z):
    return 1.0 / (1.0 + jnp.exp(-z))


def _cast_shards(b_idx, arrs):
    n = len(arrs)
    steps = 8

    def body(b_ref, *refs):
        for p in range(N_PIECE):
            refs[n][p] = refs[0][:, pl.ds(p * IN_PIECE, IN_PIECE)].astype(BF16)
        for i in range(1, n):
            refs[n + i][...] = refs[i][...].astype(BF16)

    rows = [a.shape[0] // steps for a in arrs]
    in_specs = [pl.BlockSpec((rows[i], a.shape[1]), lambda i, b: (i, 0)) for i, a in enumerate(arrs)]
    out_specs = [pl.BlockSpec((None, N_PIECE, rows[0], IN_PIECE), lambda i, b: (b[0], 0, i, 0))]
    out_specs += [pl.BlockSpec((None, rows[i], a.shape[1]), lambda i, b: (b[0], i, 0)) for i, a in enumerate(arrs) if i > 0]
    out_shape = [jax.ShapeDtypeStruct((N_CHIP, N_PIECE, arrs[0].shape[0], IN_PIECE), BF16)]
    out_shape += [jax.ShapeDtypeStruct((N_CHIP,) + a.shape, BF16) for a in arrs[1:]]
    return pl.pallas_call(
        body, out_shape=out_shape,
        grid_spec=pltpu.PrefetchScalarGridSpec(num_scalar_prefetch=1, grid=(steps,), in_specs=in_specs, out_specs=out_specs),
        compiler_params=_cp(("arbitrary",)), name="cast_shards")(b_idx, *arrs)


def _proj_gather(seq, x, g, win_own, cw8s, more, tm=1024):
    t = x.shape[0]
    ni = t // tm
    nm = len(more)
    steps = N_CHIP * N_PIECE
    near0, far0 = N_PIECE, 3 * N_PIECE

    def body(*refs):
        seq_ref, x_any, g_ref, win_in, cw_in = refs[:5]
        o_ref, hb_any, win_f, cw_out = refs[5 + nm:9 + nm]
        more_out = refs[9 + nm:9 + 2 * nm]
        hbuf, xbuf, wv, cw_s, cw_r, loc = refs[9 + 2 * nm:15 + 2 * nm]
        g_in = _Gather([win_f.at[:, p] for p in range(N_PIECE)], *refs[15 + 2 * nm:19 + 2 * nm])
        g_more = _Gather(more_out, *refs[19 + 2 * nm:23 + 2 * nm])
        s = pl.program_id(0)
        x, y, c, chips = _coords()
        b = 2 * x + y
        blks = [2 * chip[0] + chip[1] for chip in chips]

        def cw_cols(blk):
            return cw_out.at[:, pl.ds(blk * (D // N_CHIP), D // N_CHIP)]

        def cw_copy(k, blk):
            src = cw_in if blk is None else cw_cols(blk)
            return pltpu.make_async_remote_copy(src_ref=src, dst_ref=cw_cols(b if blk is None else blk), send_sem=cw_s.at[k],
                                                recv_sem=cw_r.at[k], device_id=(*chips[k], c), device_id_type=MESH)

        cw_local = pltpu.make_async_copy(cw_in, cw_cols(b), loc.at[1])
        hb_copy = pltpu.make_async_copy(hbuf, hb_any, loc.at[0])

        def load(step):
            slot = lax.rem(step, 2)
            return pltpu.make_async_copy(win_f.at[seq_ref[0, step], seq_ref[1, step]], wv.at[slot], loc.at[2 + slot])

        def chunk(i):
            return pltpu.make_async_copy(x_any.at[pl.ds(i * tm, tm)], xbuf.at[i % 2], loc.at[4 + i % 2])

        def first():
            g_in.start()
            cw_local.start()
            for k in range(3):
                cw_copy(k, None).start()
            load(0).start()
            chunk(0).start()
            for i in range(ni):
                if i + 1 < ni:
                    chunk(i + 1).start()
                chunk(i).wait()
                h, _ = _rms(xbuf[i % 2], g_ref[...])
                hbuf[pl.ds(i * tm, tm), :] = h.astype(BF16)
            hb_copy.start()

        events = {step: [] for step in range(steps)}
        events[0].append(first)
        for p in range(N_PIECE):
            events[2 * p + 2].append(functools.partial(g_in.hop, [p]))
            events[near0 + 2 * p - 1].append(functools.partial(g_in.near_ready, [p]))
            events[far0 + p - 2].append(functools.partial(g_in.far, [p]))
            events[far0 + p - 1].append(functools.partial(g_in.far_ready, [p]))
        events[2 * N_PIECE + 1].append(g_more.start)
        for step, todo in events.items():
            if todo:
                @pl.when(s == step)
                def _(todo=todo):
                    for do in todo:
                        do()

        @pl.when(s + 1 < steps)
        def _():
            load(s + 1).start()

        load(s).wait()
        for i in range(ni):
            rows = pl.ds(i * tm, tm)
            o_ref[rows, :] = jnp.dot(hbuf[rows, :], wv[lax.rem(s, 2)], preferred_element_type=F32).astype(BF16)

        @pl.when(s == steps - 1)
        def _():
            g_more.hop()
            g_more.far()
            for k in range(3):
                cw_copy(k, blks[k]).wait_recv()
            for k in range(3):
                cw_copy(k, None).wait_send()
            cw_local.wait()
            hb_copy.wait()
            g_more.near_ready()
            g_more.far_ready()
            g_in.drain()
            g_more.drain()

    in_specs = [ANY, pl.BlockSpec((1, D), lambda s, q: (0, 0)), ANY, ANY] + [ANY] * nm
    out_specs = [pl.BlockSpec((t, IN_PIECE), lambda s, q: (0, q[2, s])), ANY, ANY, ANY] + [ANY] * nm
    outs = pl.pallas_call(
        body, out_shape=[jax.ShapeDtypeStruct((t, IN_DIM), BF16), jax.ShapeDtypeStruct((t, D), BF16),
                         jax.ShapeDtypeStruct(win_own.shape, BF16), jax.ShapeDtypeStruct((8, D), F32)]
        + [jax.ShapeDtypeStruct(f.shape, f.dtype) for f in more],
        grid_spec=pltpu.PrefetchScalarGridSpec(
            num_scalar_prefetch=1, grid=(steps,), in_specs=in_specs, out_specs=out_specs,
            scratch_shapes=[pltpu.VMEM((t, D), BF16), pltpu.VMEM((2, tm, D), F32), pltpu.VMEM((2, D, IN_PIECE), BF16)]
            + [pltpu.SemaphoreType.DMA((3,))] * 2 + [pltpu.SemaphoreType.DMA((6,))]
            + _gather_sems(N_PIECE) + _gather_sems(nm)),
        input_output_aliases={3: 2, **{5 + w: 4 + w for w in range(nm)}},
        compiler_params=_cp(("arbitrary",), VMEM_LIMIT), name="proj_gather")(seq, x, g, win_own, cw8s, *more)
    return outs[0], outs[1], outs[2], outs[3], outs[4:]


class _Gather:
    def __init__(self, outs, ici_s, ici_r, d2d_s, d2d_r):
        x, y, c, _ = _coords()
        self.outs, self.c = outs, c
        self.sems = ici_s, ici_r, d2d_s, d2d_r
        self.b, self.bx, self.by, self.bd = 2 * x + y, 2 * (1 - x) + y, 2 * x + (1 - y), 2 * (1 - x) + (1 - y)
        self.xn, self.yn, self.sib = (1 - x, y, c), (x, 1 - y, c), (x, y, 1 - c)

    def piece(self, w, blk, hc, quarter=None):
        hr = self.outs[w].shape[1] // 2
        if quarter is None:
            return self.outs[w].at[blk, pl.ds(hc * hr, hr)]
        return self.outs[w].at[blk, pl.ds(hc * hr + quarter * (hr // 2), hr // 2)]

    def ici(self, w, k, ref, to):
        return pltpu.make_async_remote_copy(src_ref=ref, dst_ref=ref, send_sem=self.sems[0].at[w, k],
                                            recv_sem=self.sems[1].at[w, k], device_id=to, device_id_type=MESH)

    def d2d(self, w, k, ref):
        return pltpu.make_async_remote_copy(src_ref=ref, dst_ref=ref, send_sem=self.sems[2].at[w, k],
                                            recv_sem=self.sems[3].at[w, k], device_id=self.sib, device_id_type=MESH)

    def all(self):
        return range(len(self.outs))

    def start(self):
        for w in self.all():
            mine = self.piece(w, self.b, self.c)
            self.ici(w, 0, mine, self.xn).start()
            self.ici(w, 1, mine, self.yn).start()

    def hop(self, ws=None):
        c = self.c
        for w in ws or self.all():
            self.ici(w, 0, self.piece(w, self.bx, c), self.xn).wait_recv()
            self.ici(w, 1, self.piece(w, self.by, c), self.yn).wait_recv()
            self.ici(w, 2, self.piece(w, self.bx, c, 0), self.yn).start()
            self.ici(w, 3, self.piece(w, self.by, c, 1), self.xn).start()
            self.d2d(w, 0, self.piece(w, self.bx, c)).start()
            self.d2d(w, 1, self.piece(w, self.by, c)).start()

    def near_ready(self, ws=None):
        for w in ws or self.all():
            self.d2d(w, 0, self.piece(w, self.bx, 1 - self.c)).wait_recv()
            self.d2d(w, 1, self.piece(w, self.by, 1 - self.c)).wait_recv()

    def far(self, ws=None):
        c = self.c
        for w in ws or self.all():
            self.ici(w, 2, self.piece(w, self.bd, c, 0), self.yn).wait_recv()
            self.ici(w, 3, self.piece(w, self.bd, c, 1), self.xn).wait_recv()
            self.d2d(w, 2, self.piece(w, self.bd, c, 0)).start()
            self.d2d(w, 3, self.piece(w, self.bd, c, 1)).start()

    def far_ready(self, ws=None):
        for w in ws or self.all():
            self.d2d(w, 2, self.piece(w, self.bd, 1 - self.c, 0)).wait_recv()
            self.d2d(w, 3, self.piece(w, self.bd, 1 - self.c, 1)).wait_recv()

    def drain(self):
        c = self.c
        for w in self.all():
            mine = self.piece(w, self.b, c)
            self.ici(w, 0, mine, self.xn).wait_send()
            self.ici(w, 1, mine, self.yn).wait_send()
            self.ici(w, 2, self.piece(w, self.bx, c, 0), self.yn).wait_send()
            self.ici(w, 3, self.piece(w, self.by, c, 1), self.xn).wait_send()
            self.d2d(w, 0, self.piece(w, self.bx, c)).wait_send()
            self.d2d(w, 1, self.piece(w, self.by, c)).wait_send()
            self.d2d(w, 2, self.piece(w, self.bd, c, 0)).wait_send()
            self.d2d(w, 3, self.piece(w, self.bd, c, 1)).wait_send()


def _gather_sems(nw):
    return [pltpu.SemaphoreType.DMA((max(nw, 1), 4))] * 4


def _mixer_fwd(proj, cw8, lng, lnb, wc, bsb, fulls, tm=256):
    t = proj.shape[0]
    nt = t // tm
    nch = tm // CH
    nw = len(fulls)

    def body(*refs):
        p_ref, cw_ref, lng_ref, lnb_ref, wc_ref, bsb_ref = refs[:6]
        mix_ref = refs[6 + nw]
        w_outs = refs[7 + nw:7 + 2 * nw]
        prev_ref = refs[7 + 2 * nw]
        gather = _Gather(w_outs, *refs[8 + 2 * nw:])

        @pl.when(pl.program_id(0) == 0)
        def _():
            gather.start()
            prev_ref[...] = jnp.zeros_like(prev_ref)

        @pl.when(pl.program_id(0) == nt // 2)
        def _():
            gather.hop()

        @pl.when(pl.program_id(0) == nt - 1)
        def _():
            gather.far()

        rows = lax.broadcasted_iota(jnp.int32, (tm, CH), 0)
        for s in range(HEADS):
            cs = pl.ds(CH * s, CH)

            def slab(k):
                return p_ref[:, pl.ds(k * SLAB + CH * s, CH)].astype(F32)

            gb, gc, xa, za = slab(0), slab(1), slab(2), slab(3)
            cx = gc * xa
            p6 = jnp.broadcast_to(prev_ref[6:7, cs], (tm, CH))
            p7 = jnp.broadcast_to(prev_ref[7:8, cs], (tm, CH))
            c1 = jnp.where(rows == 0, p7, pltpu.roll(cx, 1, 0))
            c2 = jnp.where(rows == 0, p6, jnp.where(rows == 1, p7, pltpu.roll(cx, 2, 0)))
            prev_ref[:, cs] = cx[tm - 8:, :]
            cv = cw_ref[0:1, cs] * c2 + cw_ref[1:2, cs] * c1 + cw_ref[2:3, cs] * cx
            mix_ref[:, cs] = (gb * cv * (za * _sigmoid(za))).astype(BF16)

            u, v, zb = slab(4), slab(5), slab(6)
            ug, vg = _gelu(u), _gelu(v)
            dlt = vg - jnp.mean(vg, axis=-1, keepdims=True)
            vhat = dlt * lax.rsqrt(jnp.mean(dlt * dlt, axis=-1, keepdims=True) + EPS)
            vn = (vhat * lng_ref[:, cs] + lnb_ref[:, cs]).astype(BF16)
            gate = ug * (zb * _sigmoid(zb))
            for c in range(nch):
                rs = slice(CH * c, CH * (c + 1))
                sp = jnp.dot(wc_ref[s], vn[rs], preferred_element_type=F32) + bsb_ref[s]
                mix_ref[rs, pl.ds(SLAB + CH * s, CH)] = (gate[rs] * sp).astype(BF16)

        @pl.when(pl.program_id(0) == nt - 1)
        def _():
            gather.near_ready()
            gather.far_ready()
            gather.drain()

    sems = _gather_sems(nw)
    outs = pl.pallas_call(
        body, grid=(nt,),
        in_specs=[pl.BlockSpec((tm, IN_DIM), lambda i: (i, 0)), _full((8, D)), _full((1, D)), _full((1, D)),
                  _full((HEADS, CH, CH)), _full((HEADS, CH, CH))] + [ANY] * nw,
        out_specs=[pl.BlockSpec((tm, MIX), lambda i: (i, 0))] + [ANY] * nw,
        out_shape=[jax.ShapeDtypeStruct((t, MIX), BF16)] + [jax.ShapeDtypeStruct(f.shape, f.dtype) for f in fulls],
        input_output_aliases={6 + w: 1 + w for w in range(nw)},
        scratch_shapes=[pltpu.VMEM((8, D), F32)] + sems,
        compiler_params=_cp(("arbitrary",), VMEM_LIMIT), name="mixer_fwd")(proj, cw8, lng, lnb, wc, bsb, *fulls)
    return outs[0], outs[1:]


def _mem_fwd(mem, gm, wkv_f):
    n_mem = mem.shape[0]

    def body(mem_ref, gm_ref, w_ref, k_ref, v_ref):
        m, _ = _rms(mem_ref[...], gm_ref[...])
        mb = m.astype(BF16)
        for j in range(N_CHIP):
            dst = k_ref if j < 2 else v_ref
            dst[:, pl.ds(KV_BLK * (j % 2), KV_BLK)] = jnp.dot(mb, w_ref[j], preferred_element_type=F32).astype(BF16)

    return pl.pallas_call(
        body, out_shape=[jax.ShapeDtypeStruct((n_mem, D), BF16), jax.ShapeDtypeStruct((n_mem, D), BF16)],
        compiler_params=_cp(None, VMEM_LIMIT), name="mem_fwd")(mem, gm, wkv_f)


def _tail(x, tgt, mixin, wout, wq, wxo, k, v, g2, g3, tm=512, sub=512):
    t = x.shape[0]
    n_mem = k.shape[0]
    scale = 1.0 / math.sqrt(XD)

    def body(x_ref, tgt_ref, mix_ref, wout_ref, wq_ref, wxo_ref, k_ref, v_ref, g2_ref, g3_ref,
             loss_ref, dmix_ref, dx1b_ref, h2_ref, dq_ref, o_ref, dx2b_ref, dk_ref, dv_ref, dg2_ref, dg3_ref):
        @pl.when(pl.program_id(0) == 0)
        def _():
            loss_ref[...] = jnp.zeros_like(loss_ref)
            dk_ref[...] = jnp.zeros_like(dk_ref)
            dv_ref[...] = jnp.zeros_like(dv_ref)
            dg2_ref[...] = jnp.zeros_like(dg2_ref)
            dg3_ref[...] = jnp.zeros_like(dg3_ref)

        g2, g3 = g2_ref[...], g3_ref[...]
        for sb in range(tm // sub):
            rs = pl.ds(sub * sb, sub)
            x1 = x_ref[rs, :] + jnp.dot(mix_ref[rs, :], wout_ref[...], preferred_element_type=F32)
            h2, r2 = _rms(x1, g2)
            h2b = h2.astype(BF16)
            h2_ref[rs, :] = h2b
            q = jnp.dot(h2b, wq_ref[...], preferred_element_type=F32).astype(BF16)
            probs, outs = [], []
            for hd in range(XH):
                hs = pl.ds(XD * hd, XD)
                s = _bdot_nt(q[:, XD * hd:XD * (hd + 1)], k_ref[:, hs]) * scale
                e = jnp.exp(s - jnp.max(s, axis=-1, keepdims=True))
                p = e / jnp.sum(e, axis=-1, keepdims=True)
                probs.append(p)
                outs.append(_bdot(p, v_ref[:, hs]))
            ob = jnp.concatenate(outs, axis=-1).astype(BF16)
            o_ref[rs, :] = ob
            x2 = x1 + jnp.dot(ob, wxo_ref[...], preferred_element_type=F32)
            y, r3 = _rms(x2, g3)
            diff = y - tgt_ref[rs, :]
            row_loss = jnp.sum(diff * diff, axis=-1, keepdims=True)
            loss_ref[...] += jnp.broadcast_to(jnp.sum(row_loss, axis=0, keepdims=True) * (0.5 / D), loss_ref.shape)

            dx2, dg3 = _rms_bwd(diff * (1.0 / D), x2, r3, g3)
            dg3_ref[...] += dg3
            dx2b = dx2.astype(BF16)
            dx2b_ref[rs, :] = dx2b
            do = _bdot_nt(dx2b, wxo_ref[...])
            dqs = []
            for hd in range(XH):
                hs = pl.ds(XD * hd, XD)
                p = probs[hd]
                do_h = do[:, XD * hd:XD * (hd + 1)]
                dv_ref[:, hs] += _bdot_tn(p, do_h)
                dp = _bdot_nt(do_h, v_ref[:, hs])
                ds = p * (dp - jnp.sum(dp * p, axis=-1, keepdims=True))
                dqs.append(_bdot(ds, k_ref[:, hs]) * scale)
                dk_ref[:, hs] += _bdot_tn(ds, q[:, XD * hd:XD * (hd + 1)]) * scale
            dq = jnp.concatenate(dqs, axis=-1).astype(BF16)
            dq_ref[rs, :] = dq
            dx1n, dg2 = _rms_bwd(_bdot_nt(dq, wq_ref[...]), x1, r2, g2)
            dg2_ref[...] += dg2
            dx1b = (dx2 + dx1n).astype(BF16)
            dx1b_ref[rs, :] = dx1b
            dmix_ref[rs, :] = _bdot_nt(dx1b, wout_ref[...]).astype(BF16)

    tok = lambda w: pl.BlockSpec((tm, w), lambda i: (i, 0))
    return pl.pallas_call(
        body, grid=(t // tm,),
        in_specs=[tok(D), tok(D), tok(MIX), _full((MIX, D), 1), _full((D, D), 1), _full((D, D), 1),
                  _full((n_mem, D), 1), _full((n_mem, D), 1), _full((1, D)), _full((1, D))],
        out_specs=[_full((8, 128)), tok(MIX), tok(D), tok(D), tok(D), tok(D), tok(D),
                   _full((n_mem, D)), _full((n_mem, D)), _full((1, D)), _full((1, D))],
        out_shape=[jax.ShapeDtypeStruct((8, 128), F32), jax.ShapeDtypeStruct((t, MIX), BF16),
                   jax.ShapeDtypeStruct((t, D), BF16),
                   jax.ShapeDtypeStruct((t, D), BF16), jax.ShapeDtypeStruct((t, D), BF16),
                   jax.ShapeDtypeStruct((t, D), BF16), jax.ShapeDtypeStruct((t, D), BF16),
                   jax.ShapeDtypeStruct((n_mem, D), F32), jax.ShapeDtypeStruct((n_mem, D), F32),
                   jax.ShapeDtypeStruct((1, D), F32), jax.ShapeDtypeStruct((1, D), F32)],
        compiler_params=_cp(("arbitrary",), VMEM_LIMIT), name="tail")(x, tgt, mixin, wout, wq, wxo, k, v, g2, g3)


def _mem_bwd(mem, gm, dk, dv, wkv_f):
    def body(mem_ref, gm_ref, dk_ref, dv_ref, w_ref, dw_ref, dwb_ref, dgm_ref):
        mem_v = mem_ref[...]
        m, rm = _rms(mem_v, gm_ref[...])
        mb = m.astype(BF16)
        dm = jnp.zeros_like(mem_v)
        for j in range(N_CHIP):
            src = dk_ref if j < 2 else dv_ref
            dkv = src[:, pl.ds(KV_BLK * (j % 2), KV_BLK)].astype(BF16)
            dw = _bdot_tn(mb, dkv)
            dw_ref[j] = dw
            dwb_ref[j] = dw.astype(BF16)
            dm = dm + _bdot_nt(dkv, w_ref[j])
        dgm_ref[...] = jnp.sum(dm * mem_v * rm, axis=0, keepdims=True)

    return pl.pallas_call(
        body, out_shape=[jax.ShapeDtypeStruct((N_CHIP, D, KV_BLK), F32), jax.ShapeDtypeStruct((N_CHIP, D, KV_BLK), BF16),
                         jax.ShapeDtypeStruct((1, D), F32)],
        compiler_params=_cp(None, VMEM_LIMIT), name="mem_bwd")(mem, gm, dk, dv, wkv_f)


def _mixer_bwd(proj, dmix, cw8, lng, lnb, wc, wct, bsb, win_f, x, dx1, g1, tm=256):
    t = proj.shape[0]
    nt = t // tm
    nch = tm // CH
    hb = 16
    pair = 2 * CH
    assert pair == IN_PIECE

    def body(p_ref, pgc_ref, pxa_ref, dm_ref, cw_ref, lng_ref, lnb_ref, wc_ref, wct_ref, bsb_ref, w_ref, x_ref,
             dx1_ref, g1_ref, dp_ref, dcw_ref, dlng_ref, dlnb_ref, dwc_ref, dbs_ref, gx_ref, dg1_ref,
             next_ref, dh_ref):
        i = pl.program_id(0)

        @pl.when(i == 0)
        def _():
            next_ref[...] = jnp.zeros_like(next_ref)
            dcw_ref[...] = jnp.zeros_like(dcw_ref)
            dlng_ref[...] = jnp.zeros_like(dlng_ref)
            dlnb_ref[...] = jnp.zeros_like(dlnb_ref)
            dwc_ref[...] = jnp.zeros_like(dwc_ref)
            dbs_ref[...] = jnp.zeros_like(dbs_ref)
            dg1_ref[...] = jnp.zeros_like(dg1_ref)

        first_tile = i == nt - 1
        rows = lax.broadcasted_iota(jnp.int32, (tm, CH), 0)
        ones8 = jnp.ones((8, CH), BF16)
        for s in range(HEADS):
            cs = pl.ds(CH * s, CH)

            def slab(k):
                return p_ref[:, pl.ds(k * SLAB + CH * s, CH)].astype(F32)

            gb, gc, xa, za = slab(0), slab(1), slab(2), slab(3)
            da = dm_ref[:, cs].astype(F32)
            cx = gc * xa
            cxp = pgc_ref[:, cs].astype(F32) * pxa_ref[:, cs].astype(F32)
            cxp = jnp.where(first_tile, jnp.zeros_like(cxp), cxp)
            p6 = jnp.broadcast_to(cxp[hb - 2:hb - 1, :], (tm, CH))
            p7 = jnp.broadcast_to(cxp[hb - 1:hb, :], (tm, CH))
            c1 = jnp.where(rows == 0, p7, pltpu.roll(cx, 1, 0))
            c2 = jnp.where(rows == 0, p6, jnp.where(rows == 1, p7, pltpu.roll(cx, 2, 0)))
            w0, w1, w2 = cw_ref[0:1, cs], cw_ref[1:2, cs], cw_ref[2:3, cs]
            cv = w0 * c2 + w1 * c1 + w2 * cx
            sg = _sigmoid(za)
            sa = za * sg
            dcv = da * gb * sa
            dp_ref[:, pl.ds(0 * SLAB + CH * s, CH)] = (da * cv * sa).astype(BF16)
            dp_ref[:, pl.ds(3 * SLAB + CH * s, CH)] = (da * gb * cv * (sg * (1.0 + za * (1.0 - sg)))).astype(BF16)
            n0 = jnp.broadcast_to(next_ref[0:1, cs], (tm, CH))
            n1 = jnp.broadcast_to(next_ref[1:2, cs], (tm, CH))
            u1 = jnp.where(rows == tm - 1, n0, pltpu.roll(dcv, tm - 1, 0))
            u2 = jnp.where(rows == tm - 2, n0, jnp.where(rows == tm - 1, n1, pltpu.roll(dcv, tm - 2, 0)))
            next_ref[:, cs] = dcv[0:8, :]
            dcx = w2 * dcv + w1 * u1 + w0 * u2
            dp_ref[:, pl.ds(1 * SLAB + CH * s, CH)] = (dcx * xa).astype(BF16)
            dp_ref[:, pl.ds(2 * SLAB + CH * s, CH)] = (dcx * gc).astype(BF16)
            dcw_ref[0:1, cs] += jnp.sum(dcv * c2, axis=0, keepdims=True)
            dcw_ref[1:2, cs] += jnp.sum(dcv * c1, axis=0, keepdims=True)
            dcw_ref[2:3, cs] += jnp.sum(dcv * cx, axis=0, keepdims=True)

            u, v, zb = slab(4), slab(5), slab(6)
            db = dm_ref[:, pl.ds(SLAB + CH * s, CH)].astype(F32)
            ug, ugrad = _gelu_parts(u)
            vg, vgrad = _gelu_parts(v)
            dlt = vg - jnp.mean(vg, axis=-1, keepdims=True)
            rstd = lax.rsqrt(jnp.mean(dlt * dlt, axis=-1, keepdims=True) + EPS)
            vhat = dlt * rstd
            lg = lng_ref[:, cs]
            vn = (vhat * lg + lnb_ref[:, cs]).astype(BF16)
            sgb = _sigmoid(zb)
            szb = zb * sgb
            sps, dvns = [], []
            dbs = jnp.zeros((8, CH), F32)
            dwc = jnp.zeros((CH, CH), F32)
            for c in range(nch):
                rs = slice(CH * c, CH * (c + 1))
                sp = jnp.dot(wc_ref[s], vn[rs], preferred_element_type=F32) + bsb_ref[s]
                dsp = (db[rs] * ug[rs] * szb[rs]).astype(BF16)
                dbs = dbs + lax.dot_general(ones8, dsp, (((1,), (1,)), ((), ())), preferred_element_type=F32)
                dwc = dwc + lax.dot_general(dsp, vn[rs], (((1,), (1,)), ((), ())), preferred_element_type=F32)
                dvns.append(jnp.dot(wct_ref[s], dsp, preferred_element_type=F32))
                sps.append(sp)
            sp = jnp.concatenate(sps, axis=0)
            dvn = jnp.concatenate(dvns, axis=0)
            dbs_ref[:, cs] += dbs
            dwc_ref[s] += dwc
            dlng_ref[:, cs] += jnp.sum(dvn * vhat, axis=0, keepdims=True)
            dlnb_ref[:, cs] += jnp.sum(dvn, axis=0, keepdims=True)
            dvhat = dvn * lg
            dvg = rstd * (dvhat - jnp.mean(dvhat, axis=-1, keepdims=True)
                          - vhat * jnp.mean(dvhat * vhat, axis=-1, keepdims=True))
            dp_ref[:, pl.ds(4 * SLAB + CH * s, CH)] = (db * sp * szb * ugrad).astype(BF16)
            dp_ref[:, pl.ds(5 * SLAB + CH * s, CH)] = (dvg * vgrad).astype(BF16)
            dp_ref[:, pl.ds(6 * SLAB + CH * s, CH)] = (db * ug * sp * (sgb * (1.0 + zb * (1.0 - sgb)))).astype(BF16)

            if s % 2 == 1:
                part = None
                for k in range(N_SLAB):
                    col = k * SLAB + pair * (s // 2)
                    blk, off = divmod(col, IN_BLK)
                    term = lax.dot_general(dp_ref[:, pl.ds(col, pair)], w_ref[blk, off // IN_PIECE],
                                           (((1,), (1,)), ((), ())), preferred_element_type=F32)
                    part = term if part is None else part + term
                if s == 1:
                    dh_ref[...] = part
                else:
                    dh_ref[...] += part

        xv = x_ref[...]
        r = lax.rsqrt(jnp.mean(xv * xv, axis=-1, keepdims=True) + EPS)
        dxn, dg = _rms_bwd(dh_ref[...], xv, r, g1_ref[...])
        gx_ref[...] = dx1_ref[...].astype(F32) + dxn
        dg1_ref[0:1, :] += dg

        @pl.when(i == nt - 1)
        def _():
            tril = lax.broadcasted_iota(jnp.int32, (CH, CH), 0) >= lax.broadcasted_iota(jnp.int32, (CH, CH), 1)
            for s in range(HEADS):
                dwc_ref[s] = jnp.where(tril, dwc_ref[s], 0.0)

    rev = lambda i: nt - 1 - i
    halo = lambda col: pl.BlockSpec((hb, SLAB), lambda i: (jnp.maximum(rev(i) * (tm // hb) - 1, 0), col))
    tok = lambda w: pl.BlockSpec((tm, w), lambda i: (rev(i), 0))
    return pl.pallas_call(
        body, grid=(nt,),
        in_specs=[tok(IN_DIM), halo(1), halo(2), tok(MIX), _full((8, D)), _full((1, D)), _full((1, D)),
                  _full((HEADS, CH, CH)), _full((HEADS, CH, CH)), _full((HEADS, CH, CH)),
                  _full((N_CHIP, N_PIECE, D, IN_PIECE), 1), tok(D), tok(D), _full((1, D))],
        out_specs=[tok(IN_DIM), _full((8, D)), _full((1, D)), _full((1, D)), _full((HEADS, CH, CH)), _full((8, D)),
                   tok(D), _full((8, D))],
        out_shape=[jax.ShapeDtypeStruct((t, IN_DIM), BF16), jax.ShapeDtypeStruct((8, D), F32),
                   jax.ShapeDtypeStruct((1, D), F32), jax.ShapeDtypeStruct((1, D), F32),
                   jax.ShapeDtypeStruct((HEADS, CH, CH), F32), jax.ShapeDtypeStruct((8, D), F32),
                   jax.ShapeDtypeStruct((t, D), F32), jax.ShapeDtypeStruct((8, D), F32)],
        scratch_shapes=[pltpu.VMEM((8, D), F32), pltpu.VMEM((tm, D), F32)],
        compiler_params=_cp(("arbitrary",), VMEM_LIMIT), name="mixer_bwd")(
            proj, proj, proj, dmix, cw8, lng, lnb, wc, wct, bsb, win_f, x, dx1, g1)


def _grad_matmul(a, b, after, *, by_cols, name, tk=1024):
    t, m = a.shape
    n = b.shape[1]
    nk = t // tk
    nj = N_CHIP if by_cols else 1
    bn = n // nj

    def body(a_ref, b_ref, after_ref, o_ref, ob_ref):
        kk = pl.program_id(1)
        part = lax.dot_general(a_ref[...], b_ref[...], (((0,), (0,)), ((), ())), preferred_element_type=F32)

        @pl.when(kk == 0)
        def _():
            o_ref[...] = part

        @pl.when(kk > 0)
        def _():
            o_ref[...] += part

        @pl.when(kk == nk - 1)
        def _():
            ob_ref[...] = o_ref[...].astype(BF16)

    a_spec = pl.BlockSpec((tk, m), lambda j, k: (k, 0))
    b_spec = pl.BlockSpec((tk, bn), lambda j, k: (k, j))
    o_spec = pl.BlockSpec((None, m, bn), lambda j, k: (j, 0, 0))
    o32, o16 = pl.pallas_call(
        body, grid=(nj, nk), in_specs=[a_spec, b_spec, ANY], out_specs=[o_spec, o_spec],
        out_shape=[jax.ShapeDtypeStruct((nj, m, bn), F32), jax.ShapeDtypeStruct((nj, m, bn), BF16)],
        compiler_params=_cp(("parallel", "arbitrary"), VMEM_LIMIT), name=name)(a, b, after)
    if by_cols:
        return o32, o16
    return o32.reshape(N_CHIP, m // N_CHIP, n), o16.reshape(N_CHIP, m // N_CHIP, n)


def _coords():
    x, y, c = lax.axis_index("x"), lax.axis_index("y"), lax.axis_index("c")
    chips = [(1 - x, y), (x, 1 - y), (1 - x, 1 - y)]
    return x, y, c, chips


def _pair_reduce(c_idx, grads, grads_b, smalls, name):
    ng, ns = len(grads), len(smalls)
    halves = [g.shape[1] // 2 for g in grads]

    def body(c_ref, *refs):
        g_in, gb_any = refs[:ng], refs[ng:2 * ng]
        s_own, s_any = refs[2 * ng:2 * ng + ns], refs[2 * ng + ns:2 * ng + 2 * ns]
        o = refs[2 * ng + 2 * ns:4 * ng + 3 * ns]
        lands = refs[4 * ng + 3 * ns:5 * ng + 4 * ns]
        send, recv = refs[5 * ng + 4 * ns:]
        x, y, c, _ = _coords()
        j = pl.program_id(0)

        def big(i, blk):
            return pltpu.make_async_remote_copy(
                src_ref=gb_any[i].at[blk, pl.ds((1 - c) * halves[i], halves[i])], dst_ref=lands[i].at[blk],
                send_sem=send.at[i, blk], recv_sem=recv.at[i, blk], device_id=(x, y, 1 - c), device_id_type=MESH)

        def small(i):
            return pltpu.make_async_remote_copy(
                src_ref=s_any[i].at[1 - c], dst_ref=lands[ng + i],
                send_sem=send.at[ng + i, 0], recv_sem=recv.at[ng + i, 0], device_id=(x, y, 1 - c), device_id_type=MESH)

        @pl.when(j == 0)
        def _():
            for blk in range(N_CHIP):
                for i in range(ng):
                    big(i, blk).start()
            for i in range(ns):
                small(i).start()

        for i in range(ng):
            big(i, j).wait_recv()
            tot = g_in[i][...] + lands[i][j].astype(F32)
            o[i][...] = tot
            o[ng + i][...] = tot.astype(BF16)

        @pl.when(j == N_CHIP - 1)
        def _():
            for i in range(ns):
                small(i).wait_recv()
                o[2 * ng + i][...] = s_own[i][...] + lands[ng + i][...]
                small(i).wait_send()
            for blk in range(N_CHIP):
                for i in range(ng):
                    big(i, blk).wait_send()

    in_specs = [pl.BlockSpec((None, None, halves[i], g.shape[2]), lambda b, c: (b, c[0], 0, 0)) for i, g in enumerate(grads)]
    in_specs += [ANY] * ng
    in_specs += [pl.BlockSpec((None, s.shape[0] // 2, s.shape[1]), lambda b, c: (c[0], 0, 0)) for s in smalls]
    in_specs += [ANY] * ns
    blk = [pl.BlockSpec((None, halves[i], g.shape[2]), lambda b, c: (b, 0, 0)) for i, g in enumerate(grads)]
    out_specs = blk + blk + [pl.BlockSpec((s.shape[0] // 2, s.shape[1]), lambda b, c: (0, 0)) for s in smalls]
    out_shape = [jax.ShapeDtypeStruct((N_CHIP, halves[i], g.shape[2]), F32) for i, g in enumerate(grads)]
    out_shape += [jax.ShapeDtypeStruct((N_CHIP, halves[i], g.shape[2]), BF16) for i, g in enumerate(grads)]
    out_shape += [jax.ShapeDtypeStruct((s.shape[0] // 2, s.shape[1]), F32) for s in smalls]
    scratch = [pltpu.VMEM((N_CHIP, halves[i], g.shape[2]), BF16) for i, g in enumerate(grads)]
    scratch += [pltpu.VMEM((s.shape[0] // 2, s.shape[1]), F32) for s in smalls]
    scratch += [pltpu.SemaphoreType.DMA((ng + ns, N_CHIP)), pltpu.SemaphoreType.DMA((ng + ns, N_CHIP))]
    grads4 = [g.reshape(N_CHIP, 2, halves[i], g.shape[2]) for i, g in enumerate(grads)]
    smalls3 = [s.reshape(2, s.shape[0] // 2, s.shape[1]) for s in smalls]
    return pl.pallas_call(
        body, out_shape=out_shape,
        grid_spec=pltpu.PrefetchScalarGridSpec(num_scalar_prefetch=1, grid=(N_CHIP,), in_specs=in_specs,
                                               out_specs=out_specs, scratch_shapes=scratch),
        compiler_params=_cp(("arbitrary",), VMEM_LIMIT), name=name)(c_idx, *grads4, *grads_b, *smalls3, *smalls3)


def _grad_matmul_pair(c_idx, a, b, smalls, after, *, name, tk=2048):
    t, m = a.shape
    bn = b.shape[1] // N_CHIP
    nk = t // tk
    hr = m // 2
    ns = len(smalls)

    def body(c_ref, a_ref, b_ref, *refs):
        s_own, s_any = refs[:ns], refs[ns:2 * ns]
        o32, o16 = refs[2 * ns + 1], refs[2 * ns + 2]
        o_small = refs[2 * ns + 3:3 * ns + 3]
        acc, tb, land, st16 = refs[3 * ns + 3:3 * ns + 7]
        s_land, s_stage = refs[3 * ns + 7:4 * ns + 7], refs[4 * ns + 7:5 * ns + 7]
        send, recv, loc = refs[5 * ns + 7:]
        x, y, c, _ = _coords()
        sibling = dict(device_id=(x, y, 1 - c), device_id_type=MESH)
        j, kk = pl.program_id(0), pl.program_id(1)
        mine = pl.ds(pl.multiple_of(c * hr, hr), hr)
        theirs = pl.ds(pl.multiple_of((1 - c) * hr, hr), hr)

        def to_sibling(blk):
            return pltpu.make_async_remote_copy(src_ref=tb, dst_ref=land.at[blk], send_sem=send.at[blk],
                                                recv_sem=recv.at[blk], **sibling)

        def small(i):
            return pltpu.make_async_remote_copy(src_ref=s_any[i].at[1 - c], dst_ref=s_land[i], send_sem=send.at[N_CHIP + i],
                                                recv_sem=recv.at[N_CHIP + i], **sibling)

        def written(blk):
            return (pltpu.make_async_copy(acc.at[blk % 2, mine], o32.at[blk], loc.at[0]),
                    pltpu.make_async_copy(st16, o16.at[blk], loc.at[1]))

        def finish(blk):
            to_sibling(blk).wait_recv()

            @pl.when(blk > 0)
            def _():
                for cp in written(blk - 1):
                    cp.wait()

            tot = acc[blk % 2, mine, :] + land[blk].astype(F32)
            acc[blk % 2, mine, :] = tot
            st16[...] = tot.astype(BF16)
            for cp in written(blk):
                cp.start()

        def small_out(i):
            return pltpu.make_async_copy(s_stage[i], o_small[i], loc.at[2 + i])

        @pl.when((j == 0) & (kk == 0))
        def _():
            for i in range(ns):
                small(i).start()

        @pl.when((j == 1) & (kk == 0))
        def _():
            for i in range(ns):
                small(i).wait_recv()
                s_stage[i][...] = s_own[i][...] + s_land[i][...]
                small_out(i).start()

        @pl.when((j > 0) & (kk == 0))
        def _():
            finish(j - 1)

        part = lax.dot_general(a_ref[...], b_ref[...], (((0,), (0,)), ((), ())), preferred_element_type=F32)
        slot = lax.rem(j, 2)

        @pl.when(kk == 0)
        def _():
            acc[slot] = part

        @pl.when(kk > 0)
        def _():
            acc[slot] += part

        @pl.when(kk == nk - 1)
        def _():
            @pl.when(j > 0)
            def _():
                to_sibling(j - 1).wait_send()

            tb[...] = acc[slot, theirs, :].astype(BF16)
            to_sibling(j).start()

        @pl.when((j == N_CHIP - 1) & (kk == nk - 1))
        def _():
            finish(j)
            for i in range(ns):
                small_out(i).wait()
                small(i).wait_send()
            for cp in written(j):
                cp.wait()
            to_sibling(j).wait_send()

    halves = [(s.shape[0] // 2, s.shape[1]) for s in smalls]
    in_specs = [pl.BlockSpec((tk, m), lambda j, k, c: (k, 0)), pl.BlockSpec((tk, bn), lambda j, k, c: (k, j))]
    in_specs += [pl.BlockSpec((None,) + h, lambda j, k, c: (c[0], 0, 0)) for h in halves] + [ANY] * ns + [ANY]
    out_shape = [jax.ShapeDtypeStruct((N_CHIP, hr, bn), F32), jax.ShapeDtypeStruct((N_CHIP, hr, bn), BF16)]
    out_shape += [jax.ShapeDtypeStruct(h, F32) for h in halves]
    scratch = [pltpu.VMEM((2, m, bn), F32), pltpu.VMEM((hr, bn), BF16),
               pltpu.VMEM((N_CHIP, hr, bn), BF16), pltpu.VMEM((hr, bn), BF16)]
    scratch += [pltpu.VMEM(h, F32) for h in halves] * 2
    scratch += [pltpu.SemaphoreType.DMA((N_CHIP + ns,)), pltpu.SemaphoreType.DMA((N_CHIP + ns,)),
                pltpu.SemaphoreType.DMA((2 + ns,))]
    smalls3 = [s.reshape((2,) + h) for s, h in zip(smalls, halves)]
    outs = pl.pallas_call(
        body, out_shape=out_shape,
        grid_spec=pltpu.PrefetchScalarGridSpec(num_scalar_prefetch=1, grid=(N_CHIP, nk), in_specs=in_specs,
                                               out_specs=[ANY] * (2 + ns), scratch_shapes=scratch),
        compiler_params=_cp(("arbitrary", "arbitrary"), VMEM_LIMIT), name=name)(c_idx, a, b, *smalls3, *smalls3, after)
    return outs[0], outs[1], list(outs[2:])


_HBM = pl.BlockSpec(memory_space=pltpu.HBM)
_SEM = pl.BlockSpec(memory_space=pltpu.SEMAPHORE)


def _split_copies(ins, lands, ng, send, recv, arriving):
    x, y, c, chips = _coords()
    b = 2 * x + y
    copies = []
    for i in range(len(ins)):
        for k in range(3):
            blk = 2 * chips[k][0] + chips[k][1]
            src, dst, got = (ins[i].at[blk], lands[i].at[k], lands[i].at[k]) if i < ng else (ins[i], lands[i].at[b], lands[i].at[blk])
            sems = dict(send_sem=send.at[3 * i + k], recv_sem=recv.at[3 * i + k], device_id=(*chips[k], c), device_id_type=MESH)
            if arriving:
                copies.append(pltpu.make_async_remote_copy(src_ref=got, dst_ref=got, **sems))
            else:
                copies.append(pltpu.make_async_remote_copy(src_ref=src, dst_ref=dst, **sems))
    return copies


def _exchange_begin(sums_b, smalls, name):
    ng, n = len(sums_b), len(sums_b) + len(smalls)
    srcs = list(sums_b) + list(smalls)
    lands = [lax.empty((3,) + g.shape[1:], g.dtype) for g in sums_b] + [lax.empty((N_CHIP,) + s.shape, s.dtype) for s in smalls]

    def body(*refs):
        ins, land_refs = refs[:n], refs[n:2 * n]
        send, recv = refs[2 * n], refs[2 * n + 1]
        token = refs[4 * n + 2]
        for cp in _split_copies(ins, land_refs, ng, send, recv, False):
            cp.start()
        token[...] = jnp.zeros_like(token)

    hbm = lambda a: pltpu.HBM(a.shape, a.dtype)
    outs = pl.pallas_call(
        body, name=name,
        out_shape=(pltpu.SemaphoreType.DMA((3 * n,)), pltpu.SemaphoreType.DMA((3 * n,)), *[hbm(a) for a in srcs + lands],
                   jax.ShapeDtypeStruct((8, 128), F32)),
        in_specs=[_HBM] * (2 * n), out_specs=(_SEM, _SEM, *[_HBM] * (2 * n), pl.BlockSpec(memory_space=pltpu.VMEM)),
        input_output_aliases={i: 2 + i for i in range(2 * n)},
        compiler_params=pltpu.CompilerParams(has_side_effects=pltpu.SideEffectType.DATAFLOW_SIDE_EFFECTING),
    )(*[pltpu.with_memory_space_constraint(a, pltpu.HBM) for a in srcs + lands])
    return outs[0], outs[1], list(outs[2:2 + n]), list(outs[2 + n:2 + 2 * n]), outs[2 + 2 * n]


def _exchange_end(send, recv, srcs, lands, ng, which, after, name):
    n = len(srcs)
    after = list(after)

    def body(*refs):
        ins, land_refs = refs[:n], refs[n:2 * n]
        send_ref, recv_ref = refs[2 * n], refs[2 * n + 1]
        outgoing = _split_copies(ins, land_refs, ng, send_ref, recv_ref, False)
        arriving = _split_copies(ins, land_refs, ng, send_ref, recv_ref, True)
        for i in which:
            for cp in outgoing[3 * i:3 * i + 3]:
                cp.wait_send()
        for i in which:
            for cp in arriving[3 * i:3 * i + 3]:
                cp.wait_recv()

    hbm = lambda a: pltpu.HBM(a.shape, a.dtype)
    outs = pl.pallas_call(
        body, name=name, out_shape=tuple(hbm(a) for a in list(srcs) + list(lands)),
        in_specs=[_HBM] * (2 * n) + [_SEM, _SEM] + [ANY] * len(after), out_specs=tuple([_HBM] * (2 * n)),
        input_output_aliases={i: i for i in range(2 * n)},
        compiler_params=pltpu.CompilerParams(has_side_effects=pltpu.SideEffectType.DATAFLOW_SIDE_EFFECTING),
    )(*srcs, *lands, send, recv, *after)
    return list(outs[:n]), list(outs[n:])


def _chip_reduce(bc_idx, sums, recvd, smalls_slots, smalls_own, name, steps=4):
    ng, ns = len(sums), len(smalls_slots)
    n = ng + ns
    assert steps >= 2
    halves = [g.shape[1] for g in sums] + [s.shape[1] for s in smalls_slots]
    rows = [g.shape[1] // steps for g in sums]

    def body(bc_ref, *refs):
        own, rx = refs[:ng], refs[ng:2 * ng]
        sl = refs[2 * ng:2 * ng + ns]
        sl_own = refs[2 * ng + ns:2 * ng + 2 * ns]
        o = refs[2 * ng + 2 * ns:2 * ng + 2 * ns + n]
        tiles = refs[2 * ng + 2 * ns + n:2 * ng + 2 * ns + 2 * n]
        keep, send, recv = refs[2 * ng + 2 * ns + 2 * n:]
        x, y, c, _ = _coords()
        sibling = dict(device_id=(x, y, 1 - c), device_id_type=MESH)
        r = pl.program_id(0)

        def writes(i, step, slot):
            dst = o[i].at[pl.ds(c * halves[i] + step * rows[i], rows[i])]
            return (pltpu.make_async_copy(tiles[i].at[slot], dst, keep.at[i, slot]),
                    pltpu.make_async_remote_copy(src_ref=tiles[i].at[slot], dst_ref=dst, send_sem=send.at[i, slot],
                                                 recv_sem=recv.at[i, step], **sibling))

        def small_writes(i):
            dst = o[i].at[pl.ds(c * halves[i], halves[i])]
            return (pltpu.make_async_copy(tiles[i], dst, keep.at[i, 0]),
                    pltpu.make_async_remote_copy(src_ref=tiles[i], dst_ref=dst, send_sem=send.at[i, 0],
                                                 recv_sem=recv.at[i, 0], **sibling))

        def arriving(i, step, nrows):
            dst = o[i].at[pl.ds((1 - c) * halves[i] + step * nrows, nrows)]
            return pltpu.make_async_remote_copy(src_ref=dst, dst_ref=dst, send_sem=send.at[i, 0], recv_sem=recv.at[i, step],
                                                **sibling)

        def finish(step, slot):
            for i in range(ng):
                local, remote = writes(i, step, slot)
                local.wait()
                remote.wait_send()

        @pl.when(r >= 2)
        def _():
            finish(r - 2, r % 2)

        for i in range(ng):
            tot = own[i][...]
            for j in range(3):
                tot = tot + rx[i][j].astype(F32)
            tiles[i][r % 2] = tot
            for cp in writes(i, r, r % 2):
                cp.start()

        @pl.when(r == 0)
        def _():
            for i in range(ns):
                term = [jnp.where(bc_ref[0] == kk, sl_own[i][...], sl[i][kk]) for kk in range(N_CHIP)]
                tiles[ng + i][...] = ((term[0] + term[1]) + term[2]) + term[3]
                for cp in small_writes(ng + i):
                    cp.start()

        @pl.when(r == steps - 1)
        def _():
            finish(steps - 2, (steps - 2) % 2)
            finish(steps - 1, (steps - 1) % 2)
            for i in range(ns):
                local, remote = small_writes(ng + i)
                local.wait()
                remote.wait_send()
                arriving(ng + i, 0, halves[ng + i]).wait_recv()
            for i in range(ng):
                for step in range(steps):
                    arriving(i, step, rows[i]).wait_recv()

    in_specs = [pl.BlockSpec((None, rows[i], g.shape[2]), lambda r, bc: (bc[0], r, 0)) for i, g in enumerate(sums)]
    in_specs += [pl.BlockSpec((3, rows[i], g.shape[2]), lambda r, bc: (0, r, 0)) for i, g in enumerate(sums)]
    in_specs += [pl.BlockSpec(s.shape, lambda r, bc: (0, 0, 0)) for s in smalls_slots]
    in_specs += [pl.BlockSpec(s.shape[1:], lambda r, bc: (0, 0)) for s in smalls_slots]
    out_shape = [jax.ShapeDtypeStruct((2 * g.shape[1], g.shape[2]), F32) for g in sums]
    out_shape += [jax.ShapeDtypeStruct((2 * s.shape[1], s.shape[2]), F32) for s in smalls_slots]
    scratch = [pltpu.VMEM((2, rows[i], g.shape[2]), F32) for i, g in enumerate(sums)]
    scratch += [pltpu.VMEM(s.shape[1:], F32) for s in smalls_slots]
    scratch += [pltpu.SemaphoreType.DMA((n, 2)), pltpu.SemaphoreType.DMA((n, 2)), pltpu.SemaphoreType.DMA((n, steps))]
    return list(pl.pallas_call(
        body, out_shape=out_shape,
        grid_spec=pltpu.PrefetchScalarGridSpec(num_scalar_prefetch=1, grid=(steps,), in_specs=in_specs,
                                               out_specs=[ANY] * n, scratch_shapes=scratch),
        compiler_params=_cp(("arbitrary",), VMEM_LIMIT), name=name)(bc_idx, *sums, *recvd, *smalls_slots, *smalls_own))


def _adamw_math(w, g, m, v):
    m2 = ADAM_B1 * m + (1.0 - ADAM_B1) * g
    v2 = ADAM_B2 * v + (1.0 - ADAM_B2) * (g * g)
    m_hat = m2 / (1.0 - ADAM_B1 ** ADAM_STEP)
    v_hat = v2 / (1.0 - ADAM_B2 ** ADAM_STEP)
    delta = -ADAM_LR * (m_hat / (jnp.sqrt(v_hat) + ADAM_EPS) + ADAM_WD * w)
    return delta, m2, v2


def _adamw_big(ws, gs, ms, vs, name, steps=8):
    n = len(ws)

    def body(*refs):
        for i in range(n):
            w_ref, g_ref, m_ref, v_ref = (refs[k * n + i] for k in range(4))
            d_ref, m2_ref, v2_ref, g2_ref = (refs[(4 + k) * n + i] for k in range(4))
            gv = g_ref[...]
            d_ref[...], m2_ref[...], v2_ref[...] = _adamw_math(w_ref[...], gv, m_ref[...], v_ref[...])
            g2_ref[...] = gv

    specs = [pl.BlockSpec((w.shape[0] // steps, w.shape[1]), lambda i: (i, 0)) for w in ws]
    shapes = [jax.ShapeDtypeStruct(w.shape, F32) for w in ws]
    outs = pl.pallas_call(
        body, grid=(steps,), in_specs=specs * 4, out_specs=specs * 4, out_shape=shapes * 4,
        compiler_params=_cp(("parallel",), VMEM_LIMIT), name=name)(*ws, *gs, *ms, *vs)
    return [tuple(outs[k * n + i] for k in range(4)) for i in range(n)]


def _adamw_small(b_idx, sv, sw, vecs, conv, ws):
    nv = len(vecs)
    cols = conv[0].shape[1]

    def body(b_ref, sv_ref, sw_ref, *refs):
        ins, outs = refs[:3 * nv + 6], refs[3 * nv + 6:]
        for i in range(nv):
            g = sv_ref[i:i + 1, :]
            w_ref, m_ref, v_ref = ins[3 * i:3 * i + 3]
            d_ref, m2_ref, v2_ref, g_ref = outs[4 * i:4 * i + 4]
            d_ref[...], m2_ref[...], v2_ref[...] = _adamw_math(w_ref[...], g, m_ref[...], v_ref[...])
            g_ref[...] = g
        g = sv_ref[8:8 + conv[0].shape[0], pl.ds(pl.multiple_of(b_ref[0] * cols, cols), cols)]
        w_ref, m_ref, v_ref = ins[3 * nv:3 * nv + 3]
        d_ref, m2_ref, v2_ref, g_ref = outs[4 * nv:4 * nv + 4]
        d_ref[...], m2_ref[...], v2_ref[...] = _adamw_math(w_ref[...], g, m_ref[...], v_ref[...])
        g_ref[...] = g
        w_ref, m_ref, v_ref = ins[3 * nv + 3:]
        d_ref, m2_ref, v2_ref = outs[4 * nv + 4:]
        d_ref[...], m2_ref[...], v2_ref[...] = _adamw_math(w_ref[...], sw_ref[...], m_ref[...], v_ref[...])

    flat = [a for grp in vecs for a in grp] + list(conv) + list(ws)
    out_shape = [jax.ShapeDtypeStruct(grp[0].shape, F32) for grp in list(vecs) + [conv] for _ in range(4)]
    out_shape += [jax.ShapeDtypeStruct(ws[0].shape, F32)] * 3
    vmem = pl.BlockSpec(memory_space=pltpu.VMEM)
    outs = pl.pallas_call(
        body, out_shape=out_shape, in_specs=[pl.BlockSpec(memory_space=pltpu.SMEM)] + [vmem] * (2 + len(flat)),
        out_specs=[vmem] * len(out_shape), name="adamw_small")(b_idx, sv, sw, *flat)
    return [tuple(outs[4 * i:4 * i + 4]) for i in range(nv + 1)] + [tuple(outs[4 * nv + 4:])]


def kernel(x, mem, norm_mix_g, w_in, conv_w, gm_ln_g, gm_ln_b, gm_ws, gm_bs, w_out, norm_x_g, norm_mem_g, w_q, w_kv, w_xo, norm_final_g, loss_target, m_norm_mix_g, m_w_in, m_conv_w, m_gm_ln_g, m_gm_ln_b, m_gm_ws, m_gm_bs, m_w_out, m_norm_x_g, m_norm_mem_g, m_w_q, m_w_kv, m_w_xo, m_norm_final_g, v_norm_mix_g, v_w_in, v_conv_w, v_gm_ln_g, v_gm_ln_b, v_gm_ws, v_gm_bs, v_w_out, v_norm_x_g, v_norm_mem_g, v_w_q, v_w_kv, v_w_xo, v_norm_final_g):
    t = x.shape[1]
    xi = lax.axis_index("x")
    yi = lax.axis_index("y")
    ci = lax.axis_index("c")
    b_idx = jnp.reshape(2 * xi + yi, (1,)).astype(jnp.int32)
    c_idx = jnp.reshape(ci, (1,)).astype(jnp.int32)

    x2d, mem2d, tgt = x[0], mem[0], loss_target[0]
    big = [w_in[0], w_out[0], w_q[0], w_kv[0], w_xo[0]]
    big_m = [m_w_in[0], m_w_out[0], m_w_q[0], m_w_kv[0], m_w_xo[0]]
    big_v = [v_w_in[0], v_w_out[0], v_w_q[0], v_w_kv[0], v_w_xo[0]]
    g3 = norm_final_g.reshape(1, D)

    def pad8(a):
        return jnp.pad(a, ((0, 8 - a.shape[0]), (0, 0)))

    own_blocks = _cast_shards(b_idx, big)

    tril = jnp.tril(jnp.ones((CH, CH), bool))
    wc32 = jnp.where(tril[None], gm_ws[0], 0.0)
    wc = wc32.astype(BF16)
    wct = jnp.swapaxes(wc32, 1, 2).astype(BF16)
    bsb = jnp.broadcast_to(gm_bs[0][:, :, None], (HEADS, CH, CH))

    blk = 2 * xi + yi
    near = [blk ^ (2 >> (k % 2)) for k in range(2 * N_PIECE)]
    seq_blk = jnp.stack([blk] * N_PIECE + near + [blk ^ 3] * N_PIECE)
    seq_piece = jnp.asarray(list(range(N_PIECE)) + [k // 2 for k in range(2 * N_PIECE)] + list(range(N_PIECE)))
    seq = jnp.stack([seq_blk, seq_piece, seq_blk * N_PIECE + seq_piece]).astype(jnp.int32)
    proj, hb, win_f, cw8, (wq_f,) = _proj_gather(
        seq, x2d, norm_mix_g, own_blocks[0], pad8(conv_w[0]), [own_blocks[2]])
    mixin, (wout_f, wkv_f, wxo_f) = _mixer_fwd(
        proj, cw8, gm_ln_g, gm_ln_b, wc, bsb, [own_blocks[1], own_blocks[3], own_blocks[4]])
    wout2, wq2, wxo2 = wout_f.reshape(MIX, D), wq_f.reshape(D, D), wxo_f.reshape(D, D)
    k, v = _mem_fwd(mem2d, norm_mem_g, wkv_f)

    (loss_tile, dmix, dx1b, h2b, dq, ob, dx2b, dk, dv, dg2, dg3) = _tail(
        x2d, tgt, mixin, wout2, wq2, wxo2, k, v, norm_x_g, g3)
    dwkv, dwkv_b, dgm = _mem_bwd(mem2d, norm_mem_g, dk, dv, wkv_f)
    dproj, dcw, dlng, dlnb, dwc, dbs8, grad_x, dg1 = _mixer_bwd(
        proj, dmix, cw8, gm_ln_g, gm_ln_b, wc, wct, bsb, win_f, x2d, dx1b, norm_mix_g)

    bc_idx = jnp.concatenate([b_idx, c_idx])
    loss_row = jnp.broadcast_to(loss_tile[0:1, 0:1], (1, D))
    sv = jnp.concatenate([dg1[0:1], dg2, dgm, dg3, dlng, dlnb, dbs8[0:1], loss_row, dcw], axis=0)
    sw = dwc.reshape(HEADS * CH, CH)
    dwin_sum, dwin_sum_b, psmall = _grad_matmul_pair(c_idx, hb, dproj, [sv, sw], dgm, name="grad_w_in")
    sums_b = [dwin_sum]
    send_b, recv_b, src_b, land_b, token_b = _exchange_begin([dwin_sum_b], psmall, "exchange_b_begin")

    dwxo, dwxo_b = _grad_matmul(ob, dx2b, token_b, by_cols=False, name="grad_w_xo", tk=2048)
    dwq, dwq_b = _grad_matmul(h2b, dq, token_b, by_cols=False, name="grad_w_q", tk=2048)
    dwout, dwout_b = _grad_matmul(mixin, dx1b, token_b, by_cols=False, name="grad_w_out")
    ps_a = _pair_reduce(c_idx, [dwout, dwkv, dwq, dwxo], [dwout_b, dwkv_b, dwq_b, dwxo_b], [], "pair_reduce_a")
    sums_a, sums_a_b = list(ps_a[:4]), list(ps_a[4:8])
    send_a, recv_a, src_a, land_a, token_a = _exchange_begin(sums_a_b, [], "exchange_a_begin")

    _, rx2b = _exchange_end(send_b, recv_b, src_b, land_b, 1, [0, 1, 2], [token_a], "exchange_b_end")
    gwin, svf, swf = _chip_reduce(bc_idx, sums_b, rx2b[:1], rx2b[1:], psmall, "chip_reduce_b")
    out_b = _adamw_big(big[:1], [gwin], big_m[:1], big_v[:1], "adamw_w_in")[0]

    loss = svf[7, 0]
    row = lambda a: a.reshape(1, D)
    mat = lambda a: a.reshape(HEADS * CH, CH)
    small = _adamw_small(
        b_idx, svf, swf,
        [(row(norm_mix_g), row(m_norm_mix_g), row(v_norm_mix_g)), (row(norm_x_g), row(m_norm_x_g), row(v_norm_x_g)),
         (row(norm_mem_g), row(m_norm_mem_g), row(v_norm_mem_g)), (row(norm_final_g), row(m_norm_final_g), row(v_norm_final_g)),
         (row(gm_ln_g), row(m_gm_ln_g), row(v_gm_ln_g)), (row(gm_ln_b), row(m_gm_ln_b), row(v_gm_ln_b)),
         (row(gm_bs), row(m_gm_bs), row(v_gm_bs))],
        (conv_w[0], m_conv_w[0], v_conv_w[0]), (mat(gm_ws), mat(m_gm_ws), mat(v_gm_ws)))
    small_ws = small[8] + (swf,)

    def finish_a(part, src, land, after, tag):
        src, land = _exchange_end(send_a, recv_a, src, land, 4, part, after, "exchange_a%s_end" % tag)
        grads = _chip_reduce(bc_idx, [sums_a[i] for i in part], [land[i] for i in part], [], [], "chip_reduce_a" + tag,
                             steps=2)
        ids = [(1, 3, 2, 4)[i] for i in part]
        outs = _adamw_big([big[i] for i in ids], grads, [big_m[i] for i in ids], [big_v[i] for i in ids], "adamw_a" + tag,
                          steps=4)
        return src, land, outs

    src_a, land_a, (out_wout, out_wkv) = finish_a([0, 1], src_a, land_a, [out_b[0], small[0][0]], "1")
    _, _, (out_wq, out_wxo) = finish_a([2, 3], src_a, land_a, [out_wout[0]], "2")

    def unpack(k):
        vec = lambda i: small[i][k]
        return [vec(0), out_b[k][None], small[7][k][None], vec(4), vec(5), small_ws[k].reshape(1, HEADS, CH, CH),
                vec(6).reshape(1, HEADS, CH), out_wout[k][None], vec(1), vec(2), out_wq[k][None], out_wkv[k][None],
                out_wxo[k][None], vec(3).reshape(D)]

    return (loss, grad_x[None], *unpack(3), *unpack(0), *unpack(1), *unpack(2))
```

```python
import functools
import math

import jax
import jax.numpy as jnp
from jax import lax
from jax.experimental import pallas as pl
from jax.experimental.pallas import tpu as pltpu

F32 = jnp.float32
BF16 = jnp.bfloat16
MESH = pl.DeviceIdType.MESH

D = 1024
SLAB = 1024
N_SLAB = 7
IN_DIM = N_SLAB * SLAB
MIX = 2 * SLAB
HEADS = 8
CH = 128
XH = 4
XD = D // XH
EPS = 1e-6
GELU_C = math.sqrt(2.0 / math.pi)
GELU_A = 0.044715
N_CHIP = 4
IN_BLK = IN_DIM // N_CHIP
IN_PIECE = 256
N_PIECE = IN_BLK // IN_PIECE
KV_BLK = 2 * D // N_CHIP

ADAM_LR, ADAM_B1, ADAM_B2, ADAM_EPS, ADAM_WD, ADAM_STEP = 0.001, 0.9, 0.999, 1e-08, 0.01, 10

VMEM_LIMIT = 60 * 1024 * 1024


def _cp(sem=None, vmem=None):
    return pltpu.CompilerParams(dimension_semantics=sem, vmem_limit_bytes=vmem)


def _full(shape, buffers=None):
    n = len(shape)
    if buffers is None:
        return pl.BlockSpec(shape, lambda *_: (0,) * n)
    return pl.BlockSpec(shape, lambda *_: (0,) * n, pipeline_mode=pl.Buffered(buffers))


ANY = pl.BlockSpec(memory_space=pl.ANY)


def _bdot(a, b):
    return jnp.dot(a.astype(BF16), b.astype(BF16), preferred_element_type=F32)


def _bdot_nt(a, b):
    return lax.dot_general(a.astype(BF16), b.astype(BF16), (((1,), (1,)), ((), ())), preferred_element_type=F32)


def _bdot_tn(a, b):
    return lax.dot_general(a.astype(BF16), b.astype(BF16), (((0,), (0,)), ((), ())), preferred_element_type=F32)


def _rms(x, g):
    r = lax.rsqrt(jnp.mean(x * x, axis=-1, keepdims=True) + EPS)
    return x * r * g, r


def _rms_bwd(dy, x, r, g):
    gdy = dy * g
    dx = r * gdy - x * (r * r * r) * jnp.mean(x * gdy, axis=-1, keepdims=True)
    dg = jnp.sum(dy * x * r, axis=0, keepdims=True)
    return dx, dg


def _gelu_parts(x):
    x2 = x * x
    t = jnp.tanh(GELU_C * (x + GELU_A * x * x2))
    val = 0.5 * x * (1.0 + t)
    grad = 0.5 * (1.0 + t) + 0.5 * x * (1.0 - t * t) * (GELU_C * (1.0 + 3.0 * GELU_A * x2))
    return val, grad


def _gelu(x):
    return 0.5 * x * (1.0 + jnp.tanh(GELU_C * (x + GELU_A * x * x * x)))


def _sigmoid(z):
    return 1.0 / (1.0 + jnp.exp(-z))


def _cast_shards(b_idx, arrs):
    n = len(arrs)
    steps = 8

    def body(b_ref, *refs):
        for p in range(N_PIECE):
            refs[n][p] = refs[0][:, pl.ds(p * IN_PIECE, IN_PIECE)].astype(BF16)
        for i in range(1, n):
            refs[n + i][...] = refs[i][...].astype(BF16)

    rows = [a.shape[0] // steps for a in arrs]
    in_specs = [pl.BlockSpec((rows[i], a.shape[1]), lambda i, b: (i, 0)) for i, a in enumerate(arrs)]
    out_specs = [pl.BlockSpec((None, N_PIECE, rows[0], IN_PIECE), lambda i, b: (b[0], 0, i, 0))]
    out_specs += [pl.BlockSpec((None, rows[i], a.shape[1]), lambda i, b: (b[0], i, 0)) for i, a in enumerate(arrs) if i > 0]
    out_shape = [jax.ShapeDtypeStruct((N_CHIP, N_PIECE, arrs[0].shape[0], IN_PIECE), BF16)]
    out_shape += [jax.ShapeDtypeStruct((N_CHIP,) + a.shape, BF16) for a in arrs[1:]]
    return pl.pallas_call(
        body, out_shape=out_shape,
        grid_spec=pltpu.PrefetchScalarGridSpec(num_scalar_prefetch=1, grid=(steps,), in_specs=in_specs, out_specs=out_specs),
        compiler_params=_cp(("arbitrary",)), name="cast_shards")(b_idx, *arrs)


def _proj_gather(seq, x, g, win_own, cw8s, more, tm=1024):
    t = x.shape[0]
    ni = t // tm
    nm = len(more)
    steps = N_CHIP * N_PIECE
    near0, far0 = N_PIECE, 3 * N_PIECE

    def body(*refs):
        seq_ref, x_any, g_ref, win_in, cw_in = refs[:5]
        o_ref, hb_any, win_f, cw_out = refs[5 + nm:9 + nm]
        more_out = refs[9 + nm:9 + 2 * nm]
        hbuf, xbuf, wv, cw_s, cw_r, loc = refs[9 + 2 * nm:15 + 2 * nm]
        g_in = _Gather([win_f.at[:, p] for p in range(N_PIECE)], *refs[15 + 2 * nm:19 + 2 * nm])
        g_more = _Gather(more_out, *refs[19 + 2 * nm:23 + 2 * nm])
        s = pl.program_id(0)
        x, y, c, chips = _coords()
        b = 2 * x + y
        blks = [2 * chip[0] + chip[1] for chip in chips]

        def cw_cols(blk):
            return cw_out.at[:, pl.ds(blk * (D // N_CHIP), D // N_CHIP)]

        def cw_copy(k, blk):
            src = cw_in if blk is None else cw_cols(blk)
            return pltpu.make_async_remote_copy(src_ref=src, dst_ref=cw_cols(b if blk is None else blk), send_sem=cw_s.at[k],
                                                recv_sem=cw_r.at[k], device_id=(*chips[k], c), device_id_type=MESH)

        cw_local = pltpu.make_async_copy(cw_in, cw_cols(b), loc.at[1])
        hb_copy = pltpu.make_async_copy(hbuf, hb_any, loc.at[0])

        def load(step):
            slot = lax.rem(step, 2)
            return pltpu.make_async_copy(win_f.at[seq_ref[0, step], seq_ref[1, step]], wv.at[slot], loc.at[2 + slot])

        def chunk(i):
            return pltpu.make_async_copy(x_any.at[pl.ds(i * tm, tm)], xbuf.at[i % 2], loc.at[4 + i % 2])

        def first():
            g_in.start()
            cw_local.start()
            for k in range(3):
                cw_copy(k, None).start()
            load(0).start()
            chunk(0).start()
            for i in range(ni):
                if i + 1 < ni:
                    chunk(i + 1).start()
                chunk(i).wait()
                h, _ = _rms(xbuf[i % 2], g_ref[...])
                hbuf[pl.ds(i * tm, tm), :] = h.astype(BF16)
            hb_copy.start()

        events = {step: [] for step in range(steps)}
        events[0].append(first)
        for p in range(N_PIECE):
            events[2 * p + 2].append(functools.partial(g_in.hop, [p]))
            events[near0 + 2 * p - 1].append(functools.partial(g_in.near_ready, [p]))
            events[far0 + p - 2].append(functools.partial(g_in.far, [p]))
            events[far0 + p - 1].append(functools.partial(g_in.far_ready, [p]))
        events[2 * N_PIECE + 1].append(g_more.start)
        for step, todo in events.items():
            if todo:
                @pl.when(s == step)
                def _(todo=todo):
                    for do in todo:
                        do()

        @pl.when(s + 1 < steps)
        def _():
            load(s + 1).start()

        load(s).wait()
        for i in range(ni):
            rows = pl.ds(i * tm, tm)
            o_ref[rows, :] = jnp.dot(hbuf[rows, :], wv[lax.rem(s, 2)], preferred_element_type=F32).astype(BF16)

        @pl.when(s == steps - 1)
        def _():
            g_more.hop()
            g_more.far()
            for k in range(3):
                cw_copy(k, blks[k]).wait_recv()
            for k in range(3):
                cw_copy(k, None).wait_send()
            cw_local.wait()
            hb_copy.wait()
            g_more.near_ready()
            g_more.far_ready()
            g_in.drain()
            g_more.drain()

    in_specs = [ANY, pl.BlockSpec((1, D), lambda s, q: (0, 0)), ANY, ANY] + [ANY] * nm
    out_specs = [pl.BlockSpec((t, IN_PIECE), lambda s, q: (0, q[2, s])), ANY, ANY, ANY] + [ANY] * nm
    outs = pl.pallas_call(
        body, out_shape=[jax.ShapeDtypeStruct((t, IN_DIM), BF16), jax.ShapeDtypeStruct((t, D), BF16),
                         jax.ShapeDtypeStruct(win_own.shape, BF16), jax.ShapeDtypeStruct((8, D), F32)]
        + [jax.ShapeDtypeStruct(f.shape, f.dtype) for f in more],
        grid_spec=pltpu.PrefetchScalarGridSpec(
            num_scalar_prefetch=1, grid=(steps,), in_specs=in_specs, out_specs=out_specs,
            scratch_shapes=[pltpu.VMEM((t, D), BF16), pltpu.VMEM((2, tm, D), F32), pltpu.VMEM((2, D, IN_PIECE), BF16)]
            + [pltpu.SemaphoreType.DMA((3,))] * 2 + [pltpu.SemaphoreType.DMA((6,))]
            + _gather_sems(N_PIECE) + _gather_sems(nm)),
        input_output_aliases={3: 2, **{5 + w: 4 + w for w in range(nm)}},
        compiler_params=_cp(("arbitrary",), VMEM_LIMIT), name="proj_gather")(seq, x, g, win_own, cw8s, *more)
    return outs[0], outs[1], outs[2], outs[3], outs[4:]


class _Gather:
    def __init__(self, outs, ici_s, ici_r, d2d_s, d2d_r):
        x, y, c, _ = _coords()
        self.outs, self.c = outs, c
        self.sems = ici_s, ici_r, d2d_s, d2d_r
        self.b, self.bx, self.by, self.bd = 2 * x + y, 2 * (1 - x) + y, 2 * x + (1 - y), 2 * (1 - x) + (1 - y)
        self.xn, self.yn, self.sib = (1 - x, y, c), (x, 1 - y, c), (x, y, 1 - c)

    def piece(self, w, blk, hc, quarter=None):
        hr = self.outs[w].shape[1] // 2
        if quarter is None:
            return self.outs[w].at[blk, pl.ds(hc * hr, hr)]
        return self.outs[w].at[blk, pl.ds(hc * hr + quarter * (hr // 2), hr // 2)]

    def ici(self, w, k, ref, to):
        return pltpu.make_async_remote_copy(src_ref=ref, dst_ref=ref, send_sem=self.sems[0].at[w, k],
                                            recv_sem=self.sems[1].at[w, k], device_id=to, device_id_type=MESH)

    def d2d(self, w, k, ref):
        return pltpu.make_async_remote_copy(src_ref=ref, dst_ref=ref, send_sem=self.sems[2].at[w, k],
                                            recv_sem=self.sems[3].at[w, k], device_id=self.sib, device_id_type=MESH)

    def all(self):
        return range(len(self.outs))

    def start(self):
        for w in self.all():
            mine = self.piece(w, self.b, self.c)
            self.ici(w, 0, mine, self.xn).start()
            self.ici(w, 1, mine, self.yn).start()

    def hop(self, ws=None):
        c = self.c
        for w in ws or self.all():
            self.ici(w, 0, self.piece(w, self.bx, c), self.xn).wait_recv()
            self.ici(w, 1, self.piece(w, self.by, c), self.yn).wait_recv()
            self.ici(w, 2, self.piece(w, self.bx, c, 0), self.yn).start()
            self.ici(w, 3, self.piece(w, self.by, c, 1), self.xn).start()
            self.d2d(w, 0, self.piece(w, self.bx, c)).start()
            self.d2d(w, 1, self.piece(w, self.by, c)).start()

    def near_ready(self, ws=None):
        for w in ws or self.all():
            self.d2d(w, 0, self.piece(w, self.bx, 1 - self.c)).wait_recv()
            self.d2d(w, 1, self.piece(w, self.by, 1 - self.c)).wait_recv()

    def far(self, ws=None):
        c = self.c
        for w in ws or self.all():
            self.ici(w, 2, self.piece(w, self.bd, c, 0), self.yn).wait_recv()
            self.ici(w, 3, self.piece(w, self.bd, c, 1), self.xn).wait_recv()
            self.d2d(w, 2, self.piece(w, self.bd, c, 0)).start()
            self.d2d(w, 3, self.piece(w, self.bd, c, 1)).start()

    def far_ready(self, ws=None):
        for w in ws or self.all():
            self.d2d(w, 2, self.piece(w, self.bd, 1 - self.c, 0)).wait_recv()
            self.d2d(w, 3, self.piece(w, self.bd, 1 - self.c, 1)).wait_recv()

    def drain(self):
        c = self.c
        for w in self.all():
            mine = self.piece(w, self.b, c)
            self.ici(w, 0, mine, self.xn).wait_send()
            self.ici(w, 1, mine, self.yn).wait_send()
            self.ici(w, 2, self.piece(w, self.bx, c, 0), self.yn).wait_send()
            self.ici(w, 3, self.piece(w, self.by, c, 1), self.xn).wait_send()
            self.d2d(w, 0, self.piece(w, self.bx, c)).wait_send()
            self.d2d(w, 1, self.piece(w, self.by, c)).wait_send()
            self.d2d(w, 2, self.piece(w, self.bd, c, 0)).wait_send()
            self.d2d(w, 3, self.piece(w, self.bd, c, 1)).wait_send()


def _gather_sems(nw):
    return [pltpu.SemaphoreType.DMA((max(nw, 1), 4))] * 4


def _mixer_fwd(proj, cw8, lng, lnb, wc, bsb, fulls, tm=256):
    t = proj.shape[0]
    nt = t // tm
    nch = tm // CH
    nw = len(fulls)

    def body(*refs):
        p_ref, cw_ref, lng_ref, lnb_ref, wc_ref, bsb_ref = refs[:6]
        mix_ref = refs[6 + nw]
        w_outs = refs[7 + nw:7 + 2 * nw]
        prev_ref = refs[7 + 2 * nw]
        gather = _Gather(w_outs, *refs[8 + 2 * nw:])

        @pl.when(pl.program_id(0) == 0)
        def _():
            gather.start()
            prev_ref[...] = jnp.zeros_like(prev_ref)

        @pl.when(pl.program_id(0) == nt // 2)
        def _():
            gather.hop()

        @pl.when(pl.program_id(0) == nt - 1)
        def _():
            gather.far()

        rows = lax.broadcasted_iota(jnp.int32, (tm, CH), 0)
        for s in range(HEADS):
            cs = pl.ds(CH * s, CH)

            def slab(k):
                return p_ref[:, pl.ds(k * SLAB + CH * s, CH)].astype(F32)

            gb, gc, xa, za = slab(0), slab(1), slab(2), slab(3)
            cx = gc * xa
            p6 = jnp.broadcast_to(prev_ref[6:7, cs], (tm, CH))
            p7 = jnp.broadcast_to(prev_ref[7:8, cs], (tm, CH))
            c1 = jnp.where(rows == 0, p7, pltpu.roll(cx, 1, 0))
            c2 = jnp.where(rows == 0, p6, jnp.where(rows == 1, p7, pltpu.roll(cx, 2, 0)))
            prev_ref[:, cs] = cx[tm - 8:, :]
            cv = cw_ref[0:1, cs] * c2 + cw_ref[1:2, cs] * c1 + cw_ref[2:3, cs] * cx
            mix_ref[:, cs] = (gb * cv * (za * _sigmoid(za))).astype(BF16)

            u, v, zb = slab(4), slab(5), slab(6)
            ug, vg = _gelu(u), _gelu(v)
            dlt = vg - jnp.mean(vg, axis=-1, keepdims=True)
            vhat = dlt * lax.rsqrt(jnp.mean(dlt * dlt, axis=-1, keepdims=True) + EPS)
            vn = (vhat * lng_ref[:, cs] + lnb_ref[:, cs]).astype(BF16)
            gate = ug * (zb * _sigmoid(zb))
            for c in range(nch):
                rs = slice(CH * c, CH * (c + 1))
                sp = jnp.dot(wc_ref[s], vn[rs], preferred_element_type=F32) + bsb_ref[s]
                mix_ref[rs, pl.ds(SLAB + CH * s, CH)] = (gate[rs] * sp).astype(BF16)

        @pl.when(pl.program_id(0) == nt - 1)
        def _():
            gather.near_ready()
            gather.far_ready()
            gather.drain()

    sems = _gather_sems(nw)
    outs = pl.pallas_call(
        body, grid=(nt,),
        in_specs=[pl.BlockSpec((tm, IN_DIM), lambda i: (i, 0)), _full((8, D)), _full((1, D)), _full((1, D)),
                  _full((HEADS, CH, CH)), _full((HEADS, CH, CH))] + [ANY] * nw,
        out_specs=[pl.BlockSpec((tm, MIX), lambda i: (i, 0))] + [ANY] * nw,
        out_shape=[jax.ShapeDtypeStruct((t, MIX), BF16)] + [jax.ShapeDtypeStruct(f.shape, f.dtype) for f in fulls],
        input_output_aliases={6 + w: 1 + w for w in range(nw)},
        scratch_shapes=[pltpu.VMEM((8, D), F32)] + sems,
        compiler_params=_cp(("arbitrary",), VMEM_LIMIT), name="mixer_fwd")(proj, cw8, lng, lnb, wc, bsb, *fulls)
    return outs[0], outs[1:]


def _mem_fwd(mem, gm, wkv_f):
    n_mem = mem.shape[0]

    def body(mem_ref, gm_ref, w_ref, k_ref, v_ref):
        m, _ = _rms(mem_ref[...], gm_ref[...])
        mb = m.astype(BF16)
        for j in range(N_CHIP):
            dst = k_ref if j < 2 else v_ref
            dst[:, pl.ds(KV_BLK * (j % 2), KV_BLK)] = jnp.dot(mb, w_ref[j], preferred_element_type=F32).astype(BF16)

    return pl.pallas_call(
        body, out_shape=[jax.ShapeDtypeStruct((n_mem, D), BF16), jax.ShapeDtypeStruct((n_mem, D), BF16)],
        compiler_params=_cp(None, VMEM_LIMIT), name="mem_fwd")(mem, gm, wkv_f)


def _tail(x, tgt, mixin, wout, wq, wxo, k, v, g2, g3, tm=512, sub=512):
    t = x.shape[0]
    n_mem = k.shape[0]
    scale = 1.0 / math.sqrt(XD)

    def body(x_ref, tgt_ref, mix_ref, wout_ref, wq_ref, wxo_ref, k_ref, v_ref, g2_ref, g3_ref,
             loss_ref, dmix_ref, dx1b_ref, h2_ref, dq_ref, o_ref, dx2b_ref, dk_ref, dv_ref, dg2_ref, dg3_ref):
        @pl.when(pl.program_id(0) == 0)
        def _():
            loss_ref[...] = jnp.zeros_like(loss_ref)
            dk_ref[...] = jnp.zeros_like(dk_ref)
            dv_ref[...] = jnp.zeros_like(dv_ref)
            dg2_ref[...] = jnp.zeros_like(dg2_ref)
            dg3_ref[...] = jnp.zeros_like(dg3_ref)

        g2, g3 = g2_ref[...], g3_ref[...]
        for sb in range(tm // sub):
            rs = pl.ds(sub * sb, sub)
            x1 = x_ref[rs, :] + jnp.dot(mix_ref[rs, :], wout_ref[...], preferred_element_type=F32)
            h2, r2 = _rms(x1, g2)
            h2b = h2.astype(BF16)
            h2_ref[rs, :] = h2b
            q = jnp.dot(h2b, wq_ref[...], preferred_element_type=F32).astype(BF16)
            probs, outs = [], []
            for hd in range(XH):
                hs = pl.ds(XD * hd, XD)
                s = _bdot_nt(q[:, XD * hd:XD * (hd + 1)], k_ref[:, hs]) * scale
                e = jnp.exp(s - jnp.max(s, axis=-1, keepdims=True))
                p = e / jnp.sum(e, axis=-1, keepdims=True)
                probs.append(p)
                outs.append(_bdot(p, v_ref[:, hs]))
            ob = jnp.concatenate(outs, axis=-1).astype(BF16)
            o_ref[rs, :] = ob
            x2 = x1 + jnp.dot(ob, wxo_ref[...], preferred_element_type=F32)
            y, r3 = _rms(x2, g3)
            diff = y - tgt_ref[rs, :]
            row_loss = jnp.sum(diff * diff, axis=-1, keepdims=True)
            loss_ref[...] += jnp.broadcast_to(jnp.sum(row_loss, axis=0, keepdims=True) * (0.5 / D), loss_ref.shape)

            dx2, dg3 = _rms_bwd(diff * (1.0 / D), x2, r3, g3)
            dg3_ref[...] += dg3
            dx2b = dx2.astype(BF16)
            dx2b_ref[rs, :] = dx2b
            do = _bdot_nt(dx2b, wxo_ref[...])
            dqs = []
            for hd in range(XH):
                hs = pl.ds(XD * hd, XD)
                p = probs[hd]
                do_h = do[:, XD * hd:XD * (hd + 1)]
                dv_ref[:, hs] += _bdot_tn(p, do_h)
                dp = _bdot_nt(do_h, v_ref[:, hs])
                ds = p * (dp - jnp.sum(dp * p, axis=-1, keepdims=True))
                dqs.append(_bdot(ds, k_ref[:, hs]) * scale)
                dk_ref[:, hs] += _bdot_tn(ds, q[:, XD * hd:XD * (hd + 1)]) * scale
            dq = jnp.concatenate(dqs, axis=-1).astype(BF16)
            dq_ref[rs, :] = dq
            dx1n, dg2 = _rms_bwd(_bdot_nt(dq, wq_ref[...]), x1, r2, g2)
            dg2_ref[...] += dg2
            dx1b = (dx2 + dx1n).astype(BF16)
            dx1b_ref[rs, :] = dx1b
            dmix_ref[rs, :] = _bdot_nt(dx1b, wout_ref[...]).astype(BF16)

    tok = lambda w: pl.BlockSpec((tm, w), lambda i: (i, 0))
    return pl.pallas_call(
        body, grid=(t // tm,),
        in_specs=[tok(D), tok(D), tok(MIX), _full((MIX, D), 1), _full((D, D), 1), _full((D, D), 1),
                  _full((n_mem, D), 1), _full((n_mem, D), 1), _full((1, D)), _full((1, D))],
        out_specs=[_full((1, D)), tok(MIX), tok(D), tok(D), tok(D), tok(D), tok(D),
                   _full((n_mem, D)), _full((n_mem, D)), _full((1, D)), _full((1, D))],
        out_shape=[jax.ShapeDtypeStruct((1, D), F32), jax.ShapeDtypeStruct((t, MIX), BF16),
                   jax.ShapeDtypeStruct((t, D), BF16),
                   jax.ShapeDtypeStruct((t, D), BF16), jax.ShapeDtypeStruct((t, D), BF16),
                   jax.ShapeDtypeStruct((t, D), BF16), jax.ShapeDtypeStruct((t, D), BF16),
                   jax.ShapeDtypeStruct((n_mem, D), F32), jax.ShapeDtypeStruct((n_mem, D), F32),
                   jax.ShapeDtypeStruct((1, D), F32), jax.ShapeDtypeStruct((1, D), F32)],
        compiler_params=_cp(("arbitrary",), VMEM_LIMIT), name="tail")(x, tgt, mixin, wout, wq, wxo, k, v, g2, g3)


def _mem_bwd(mem, gm, dk, dv, wkv_f):
    def body(mem_ref, gm_ref, dk_ref, dv_ref, w_ref, dw_ref, dwb_ref, dgm_ref):
        mem_v = mem_ref[...]
        m, rm = _rms(mem_v, gm_ref[...])
        mb = m.astype(BF16)
        dm = jnp.zeros_like(mem_v)
        for j in range(N_CHIP):
            src = dk_ref if j < 2 else dv_ref
            dkv = src[:, pl.ds(KV_BLK * (j % 2), KV_BLK)].astype(BF16)
            dw = _bdot_tn(mb, dkv)
            dw_ref[j] = dw
            dwb_ref[j] = dw.astype(BF16)
            dm = dm + _bdot_nt(dkv, w_ref[j])
        dgm_ref[...] = jnp.sum(dm * mem_v * rm, axis=0, keepdims=True)

    return pl.pallas_call(
        body, out_shape=[jax.ShapeDtypeStruct((N_CHIP, D, KV_BLK), F32), jax.ShapeDtypeStruct((N_CHIP, D, KV_BLK), BF16),
                         jax.ShapeDtypeStruct((1, D), F32)],
        compiler_params=_cp(None, VMEM_LIMIT), name="mem_bwd")(mem, gm, dk, dv, wkv_f)


def _mixer_bwd(proj, dmix, cw8, lng, lnb, wc, wct, bsb, win_f, x, dx1, g1, tm=256):
    t = proj.shape[0]
    nt = t // tm
    nch = tm // CH
    hb = 16
    pair = 2 * CH
    assert pair == IN_PIECE

    def body(p_ref, pgc_ref, pxa_ref, dm_ref, cw_ref, lng_ref, lnb_ref, wc_ref, wct_ref, bsb_ref, w_ref, x_ref,
             dx1_ref, g1_ref, dp_ref, dcw_ref, dlng_ref, dlnb_ref, dwc_ref, dbs_ref, gx_ref, dg1_ref,
             next_ref, dh_ref):
        i = pl.program_id(0)

        @pl.when(i == 0)
        def _():
            next_ref[...] = jnp.zeros_like(next_ref)
            dcw_ref[...] = jnp.zeros_like(dcw_ref)
            dlng_ref[...] = jnp.zeros_like(dlng_ref)
            dlnb_ref[...] = jnp.zeros_like(dlnb_ref)
            dwc_ref[...] = jnp.zeros_like(dwc_ref)
            dbs_ref[...] = jnp.zeros_like(dbs_ref)
            dg1_ref[...] = jnp.zeros_like(dg1_ref)

        first_tile = i == nt - 1
        rows = lax.broadcasted_iota(jnp.int32, (tm, CH), 0)
        ones8 = jnp.ones((8, CH), BF16)
        for s in range(HEADS):
            cs = pl.ds(CH * s, CH)

            def slab(k):
                return p_ref[:, pl.ds(k * SLAB + CH * s, CH)].astype(F32)

            gb, gc, xa, za = slab(0), slab(1), slab(2), slab(3)
            da = dm_ref[:, cs].astype(F32)
            cx = gc * xa
            cxp = pgc_ref[:, cs].astype(F32) * pxa_ref[:, cs].astype(F32)
            cxp = jnp.where(first_tile, jnp.zeros_like(cxp), cxp)
            p6 = jnp.broadcast_to(cxp[hb - 2:hb - 1, :], (tm, CH))
            p7 = jnp.broadcast_to(cxp[hb - 1:hb, :], (tm, CH))
            c1 = jnp.where(rows == 0, p7, pltpu.roll(cx, 1, 0))
            c2 = jnp.where(rows == 0, p6, jnp.where(rows == 1, p7, pltpu.roll(cx, 2, 0)))
            w0, w1, w2 = cw_ref[0:1, cs], cw_ref[1:2, cs], cw_ref[2:3, cs]
            cv = w0 * c2 + w1 * c1 + w2 * cx
            sg = _sigmoid(za)
            sa = za * sg
            dcv = da * gb * sa
            dp_ref[:, pl.ds(0 * SLAB + CH * s, CH)] = (da * cv * sa).astype(BF16)
            dp_ref[:, pl.ds(3 * SLAB + CH * s, CH)] = (da * gb * cv * (sg * (1.0 + za * (1.0 - sg)))).astype(BF16)
            n0 = jnp.broadcast_to(next_ref[0:1, cs], (tm, CH))
            n1 = jnp.broadcast_to(next_ref[1:2, cs], (tm, CH))
            u1 = jnp.where(rows == tm - 1, n0, pltpu.roll(dcv, tm - 1, 0))
            u2 = jnp.where(rows == tm - 2, n0, jnp.where(rows == tm - 1, n1, pltpu.roll(dcv, tm - 2, 0)))
            next_ref[:, cs] = dcv[0:8, :]
            dcx = w2 * dcv + w1 * u1 + w0 * u2
            dp_ref[:, pl.ds(1 * SLAB + CH * s, CH)] = (dcx * xa).astype(BF16)
            dp_ref[:, pl.ds(2 * SLAB + CH * s, CH)] = (dcx * gc).astype(BF16)
            dcw_ref[0:1, cs] += jnp.sum(dcv * c2, axis=0, keepdims=True)
            dcw_ref[1:2, cs] += jnp.sum(dcv * c1, axis=0, keepdims=True)
            dcw_ref[2:3, cs] += jnp.sum(dcv * cx, axis=0, keepdims=True)

            u, v, zb = slab(4), slab(5), slab(6)
            db = dm_ref[:, pl.ds(SLAB + CH * s, CH)].astype(F32)
            ug, ugrad = _gelu_parts(u)
            vg, vgrad = _gelu_parts(v)
            dlt = vg - jnp.mean(vg, axis=-1, keepdims=True)
            rstd = lax.rsqrt(jnp.mean(dlt * dlt, axis=-1, keepdims=True) + EPS)
            vhat = dlt * rstd
            lg = lng_ref[:, cs]
            vn = (vhat * lg + lnb_ref[:, cs]).astype(BF16)
            sgb = _sigmoid(zb)
            szb = zb * sgb
            sps, dvns = [], []
            dbs = jnp.zeros((8, CH), F32)
            dwc = jnp.zeros((CH, CH), F32)
            for c in range(nch):
                rs = slice(CH * c, CH * (c + 1))
                sp = jnp.dot(wc_ref[s], vn[rs], preferred_element_type=F32) + bsb_ref[s]
                dsp = (db[rs] * ug[rs] * szb[rs]).astype(BF16)
                dbs = dbs + lax.dot_general(ones8, dsp, (((1,), (1,)), ((), ())), preferred_element_type=F32)
                dwc = dwc + lax.dot_general(dsp, vn[rs], (((1,), (1,)), ((), ())), preferred_element_type=F32)
                dvns.append(jnp.dot(wct_ref[s], dsp, preferred_element_type=F32))
                sps.append(sp)
            sp = jnp.concatenate(sps, axis=0)
            dvn = jnp.concatenate(dvns, axis=0)
            dbs_ref[:, cs] += dbs[0:1]
            dwc_ref[s] += dwc
            dlng_ref[:, cs] += jnp.sum(dvn * vhat, axis=0, keepdims=True)
            dlnb_ref[:, cs] += jnp.sum(dvn, axis=0, keepdims=True)
            dvhat = dvn * lg
            dvg = rstd * (dvhat - jnp.mean(dvhat, axis=-1, keepdims=True)
                          - vhat * jnp.mean(dvhat * vhat, axis=-1, keepdims=True))
            dp_ref[:, pl.ds(4 * SLAB + CH * s, CH)] = (db * sp * szb * ugrad).astype(BF16)
            dp_ref[:, pl.ds(5 * SLAB + CH * s, CH)] = (dvg * vgrad).astype(BF16)
            dp_ref[:, pl.ds(6 * SLAB + CH * s, CH)] = (db * ug * sp * (sgb * (1.0 + zb * (1.0 - sgb)))).astype(BF16)

            if s % 2 == 1:
                part = None
                for k in range(N_SLAB):
                    col = k * SLAB + pair * (s // 2)
                    blk, off = divmod(col, IN_BLK)
                    term = lax.dot_general(dp_ref[:, pl.ds(col, pair)], w_ref[blk, off // IN_PIECE],
                                           (((1,), (1,)), ((), ())), preferred_element_type=F32)
                    part = term if part is None else part + term
                if s == 1:
                    dh_ref[...] = part
                else:
                    dh_ref[...] += part

        xv = x_ref[...]
        r = lax.rsqrt(jnp.mean(xv * xv, axis=-1, keepdims=True) + EPS)
        dxn, dg = _rms_bwd(dh_ref[...], xv, r, g1_ref[...])
        gx_ref[...] = dx1_ref[...].astype(F32) + dxn
        dg1_ref[...] += dg

        @pl.when(i == nt - 1)
        def _():
            tril = lax.broadcasted_iota(jnp.int32, (CH, CH), 0) >= lax.broadcasted_iota(jnp.int32, (CH, CH), 1)
            for s in range(HEADS):
                dwc_ref[s] = jnp.where(tril, dwc_ref[s], 0.0)

    rev = lambda i: nt - 1 - i
    halo = lambda col: pl.BlockSpec((hb, SLAB), lambda i: (jnp.maximum(rev(i) * (tm // hb) - 1, 0), col))
    tok = lambda w: pl.BlockSpec((tm, w), lambda i: (rev(i), 0))
    return pl.pallas_call(
        body, grid=(nt,),
        in_specs=[tok(IN_DIM), halo(1), halo(2), tok(MIX), _full((8, D)), _full((1, D)), _full((1, D)),
                  _full((HEADS, CH, CH)), _full((HEADS, CH, CH)), _full((HEADS, CH, CH)),
                  _full((N_CHIP, N_PIECE, D, IN_PIECE), 1), tok(D), tok(D), _full((1, D))],
        out_specs=[tok(IN_DIM), _full((8, D)), _full((1, D)), _full((1, D)), _full((HEADS, CH, CH)), _full((1, D)),
                   tok(D), _full((1, D))],
        out_shape=[jax.ShapeDtypeStruct((t, IN_DIM), BF16), jax.ShapeDtypeStruct((8, D), F32),
                   jax.ShapeDtypeStruct((1, D), F32), jax.ShapeDtypeStruct((1, D), F32),
                   jax.ShapeDtypeStruct((HEADS, CH, CH), F32), jax.ShapeDtypeStruct((1, D), F32),
                   jax.ShapeDtypeStruct((t, D), F32), jax.ShapeDtypeStruct((1, D), F32)],
        scratch_shapes=[pltpu.VMEM((8, D), F32), pltpu.VMEM((tm, D), F32)],
        compiler_params=_cp(("arbitrary",), VMEM_LIMIT), name="mixer_bwd")(
            proj, proj, proj, dmix, cw8, lng, lnb, wc, wct, bsb, win_f, x, dx1, g1)


def _grad_matmul(a, b, after, *, by_cols, name, tk=1024):
    t, m = a.shape
    n = b.shape[1]
    nk = t // tk
    nj = N_CHIP if by_cols else 1
    bn = n // nj

    def body(a_ref, b_ref, after_ref, o_ref, ob_ref):
        kk = pl.program_id(1)
        part = lax.dot_general(a_ref[...], b_ref[...], (((0,), (0,)), ((), ())), preferred_element_type=F32)

        @pl.when(kk == 0)
        def _():
            o_ref[...] = part

        @pl.when(kk > 0)
        def _():
            o_ref[...] += part

        @pl.when(kk == nk - 1)
        def _():
            ob_ref[...] = o_ref[...].astype(BF16)

    a_spec = pl.BlockSpec((tk, m), lambda j, k: (k, 0))
    b_spec = pl.BlockSpec((tk, bn), lambda j, k: (k, j))
    o_spec = pl.BlockSpec((None, m, bn), lambda j, k: (j, 0, 0))
    o32, o16 = pl.pallas_call(
        body, grid=(nj, nk), in_specs=[a_spec, b_spec, ANY], out_specs=[o_spec, o_spec],
        out_shape=[jax.ShapeDtypeStruct((nj, m, bn), F32), jax.ShapeDtypeStruct((nj, m, bn), BF16)],
        compiler_params=_cp(("parallel", "arbitrary"), VMEM_LIMIT), name=name)(a, b, after)
    if by_cols:
        return o32, o16
    return o32.reshape(N_CHIP, m // N_CHIP, n), o16.reshape(N_CHIP, m // N_CHIP, n)


def _coords():
    x, y, c = lax.axis_index("x"), lax.axis_index("y"), lax.axis_index("c")
    chips = [(1 - x, y), (x, 1 - y), (1 - x, 1 - y)]
    return x, y, c, chips


def _pair_reduce(c_idx, grads, grads_b, smalls, name):
    ng, ns = len(grads), len(smalls)
    halves = [g.shape[1] // 2 for g in grads]

    def body(c_ref, *refs):
        g_in, gb_any = refs[:ng], refs[ng:2 * ng]
        s_own, s_any = refs[2 * ng:2 * ng + ns], refs[2 * ng + ns:2 * ng + 2 * ns]
        o = refs[2 * ng + 2 * ns:4 * ng + 3 * ns]
        lands = refs[4 * ng + 3 * ns:5 * ng + 4 * ns]
        send, recv = refs[5 * ng + 4 * ns:]
        x, y, c, _ = _coords()
        j = pl.program_id(0)

        def big(i, blk):
            return pltpu.make_async_remote_copy(
                src_ref=gb_any[i].at[blk, pl.ds((1 - c) * halves[i], halves[i])], dst_ref=lands[i].at[blk],
                send_sem=send.at[i, blk], recv_sem=recv.at[i, blk], device_id=(x, y, 1 - c), device_id_type=MESH)

        def small(i):
            return pltpu.make_async_remote_copy(
                src_ref=s_any[i].at[1 - c], dst_ref=lands[ng + i],
                send_sem=send.at[ng + i, 0], recv_sem=recv.at[ng + i, 0], device_id=(x, y, 1 - c), device_id_type=MESH)

        @pl.when(j == 0)
        def _():
            for blk in range(N_CHIP):
                for i in range(ng):
                    big(i, blk).start()
            for i in range(ns):
                small(i).start()

        for i in range(ng):
            big(i, j).wait_recv()
            tot = g_in[i][...] + lands[i][j].astype(F32)
            o[i][...] = tot
            o[ng + i][...] = tot.astype(BF16)

        @pl.when(j == N_CHIP - 1)
        def _():
            for i in range(ns):
                small(i).wait_recv()
                o[2 * ng + i][...] = s_own[i][...] + lands[ng + i][...]
                small(i).wait_send()
            for blk in range(N_CHIP):
                for i in range(ng):
                    big(i, blk).wait_send()

    in_specs = [pl.BlockSpec((None, None, halves[i], g.shape[2]), lambda b, c: (b, c[0], 0, 0)) for i, g in enumerate(grads)]
    in_specs += [ANY] * ng
    in_specs += [pl.BlockSpec((None, s.shape[0] // 2, s.shape[1]), lambda b, c: (c[0], 0, 0)) for s in smalls]
    in_specs += [ANY] * ns
    blk = [pl.BlockSpec((None, halves[i], g.shape[2]), lambda b, c: (b, 0, 0)) for i, g in enumerate(grads)]
    out_specs = blk + blk + [pl.BlockSpec((s.shape[0] // 2, s.shape[1]), lambda b, c: (0, 0)) for s in smalls]
    out_shape = [jax.ShapeDtypeStruct((N_CHIP, halves[i], g.shape[2]), F32) for i, g in enumerate(grads)]
    out_shape += [jax.ShapeDtypeStruct((N_CHIP, halves[i], g.shape[2]), BF16) for i, g in enumerate(grads)]
    out_shape += [jax.ShapeDtypeStruct((s.shape[0] // 2, s.shape[1]), F32) for s in smalls]
    scratch = [pltpu.VMEM((N_CHIP, halves[i], g.shape[2]), BF16) for i, g in enumerate(grads)]
    scratch += [pltpu.VMEM((s.shape[0] // 2, s.shape[1]), F32) for s in smalls]
    scratch += [pltpu.SemaphoreType.DMA((ng + ns, N_CHIP)), pltpu.SemaphoreType.DMA((ng + ns, N_CHIP))]
    grads4 = [g.reshape(N_CHIP, 2, halves[i], g.shape[2]) for i, g in enumerate(grads)]
    smalls3 = [s.reshape(2, s.shape[0] // 2, s.shape[1]) for s in smalls]
    return pl.pallas_call(
        body, out_shape=out_shape,
        grid_spec=pltpu.PrefetchScalarGridSpec(num_scalar_prefetch=1, grid=(N_CHIP,), in_specs=in_specs,
                                               out_specs=out_specs, scratch_shapes=scratch),
        compiler_params=_cp(("arbitrary",), VMEM_LIMIT), name=name)(c_idx, *grads4, *grads_b, *smalls3, *smalls3)


def _grad_matmul_pair(c_idx, a, b, smalls, after, *, name, tk=2048):
    t, m = a.shape
    bn = b.shape[1] // N_CHIP
    nk = t // tk
    hr = m // 2
    ns = len(smalls)

    def body(c_ref, a_ref, b_ref, *refs):
        s_own, s_any = refs[:ns], refs[ns:2 * ns]
        o32, o16 = refs[2 * ns + 1], refs[2 * ns + 2]
        o_small = refs[2 * ns + 3:3 * ns + 3]
        acc, tb, land, st16 = refs[3 * ns + 3:3 * ns + 7]
        s_land, s_stage = refs[3 * ns + 7:4 * ns + 7], refs[4 * ns + 7:5 * ns + 7]
        send, recv, loc = refs[5 * ns + 7:]
        x, y, c, _ = _coords()
        sibling = dict(device_id=(x, y, 1 - c), device_id_type=MESH)
        j, kk = pl.program_id(0), pl.program_id(1)
        mine = pl.ds(pl.multiple_of(c * hr, hr), hr)
        theirs = pl.ds(pl.multiple_of((1 - c) * hr, hr), hr)

        def to_sibling(blk):
            return pltpu.make_async_remote_copy(src_ref=tb, dst_ref=land.at[blk], send_sem=send.at[blk],
                                                recv_sem=recv.at[blk], **sibling)

        def small(i):
            return pltpu.make_async_remote_copy(src_ref=s_any[i].at[1 - c], dst_ref=s_land[i], send_sem=send.at[N_CHIP + i],
                                                recv_sem=recv.at[N_CHIP + i], **sibling)

        def written(blk):
            return (pltpu.make_async_copy(acc.at[blk % 2, mine], o32.at[blk], loc.at[0]),
                    pltpu.make_async_copy(st16, o16.at[blk], loc.at[1]))

        def finish(blk):
            to_sibling(blk).wait_recv()

            @pl.when(blk > 0)
            def _():
                for cp in written(blk - 1):
                    cp.wait()

            tot = acc[blk % 2, mine, :] + land[blk].astype(F32)
            acc[blk % 2, mine, :] = tot
            st16[...] = tot.astype(BF16)
            for cp in written(blk):
                cp.start()

        def small_out(i):
            return pltpu.make_async_copy(s_stage[i], o_small[i], loc.at[2 + i])

        @pl.when((j == 0) & (kk == 0))
        def _():
            for i in range(ns):
                small(i).start()

        @pl.when((j == 1) & (kk == 0))
        def _():
            for i in range(ns):
                small(i).wait_recv()
                s_stage[i][...] = s_own[i][...] + s_land[i][...]
                small_out(i).start()

        @pl.when((j > 0) & (kk == 0))
        def _():
            finish(j - 1)

        part = lax.dot_general(a_ref[...], b_ref[...], (((0,), (0,)), ((), ())), preferred_element_type=F32)
        slot = lax.rem(j, 2)

        @pl.when(kk == 0)
        def _():
            acc[slot] = part

        @pl.when(kk > 0)
        def _():
            acc[slot] += part

        @pl.when(kk == nk - 1)
        def _():
            @pl.when(j > 0)
            def _():
                to_sibling(j - 1).wait_send()

            tb[...] = acc[slot, theirs, :].astype(BF16)
            to_sibling(j).start()

        @pl.when((j == N_CHIP - 1) & (kk == nk - 1))
        def _():
            finish(j)
            for i in range(ns):
                small_out(i).wait()
                small(i).wait_send()
            for cp in written(j):
                cp.wait()
            to_sibling(j).wait_send()

    halves = [(s.shape[0] // 2, s.shape[1]) for s in smalls]
    in_specs = [pl.BlockSpec((tk, m), lambda j, k, c: (k, 0)), pl.BlockSpec((tk, bn), lambda j, k, c: (k, j))]
    in_specs += [pl.BlockSpec((None,) + h, lambda j, k, c: (c[0], 0, 0)) for h in halves] + [ANY] * ns + [ANY]
    out_shape = [jax.ShapeDtypeStruct((N_CHIP, hr, bn), F32), jax.ShapeDtypeStruct((N_CHIP, hr, bn), BF16)]
    out_shape += [pltpu.HBM(h, F32) for h in halves]
    scratch = [pltpu.VMEM((2, m, bn), F32), pltpu.VMEM((hr, bn), BF16),
               pltpu.VMEM((N_CHIP, hr, bn), BF16), pltpu.VMEM((hr, bn), BF16)]
    scratch += [pltpu.VMEM(h, F32) for h in halves] * 2
    scratch += [pltpu.SemaphoreType.DMA((N_CHIP + ns,)), pltpu.SemaphoreType.DMA((N_CHIP + ns,)),
                pltpu.SemaphoreType.DMA((2 + ns,))]
    smalls3 = [s.reshape((2,) + h) for s, h in zip(smalls, halves)]
    outs = pl.pallas_call(
        body, out_shape=out_shape,
        grid_spec=pltpu.PrefetchScalarGridSpec(num_scalar_prefetch=1, grid=(N_CHIP, nk), in_specs=in_specs,
                                               out_specs=[ANY, ANY] + [_HBM] * ns, scratch_shapes=scratch),
        compiler_params=_cp(("arbitrary", "arbitrary"), VMEM_LIMIT), name=name)(c_idx, a, b, *smalls3, *smalls3, after)
    return outs[0], outs[1], list(outs[2:])


_HBM = pl.BlockSpec(memory_space=pltpu.HBM)
_SEM = pl.BlockSpec(memory_space=pltpu.SEMAPHORE)


def _split_copies(ins, lands, ng, send, recv, arriving):
    x, y, c, chips = _coords()
    b = 2 * x + y
    copies = []
    for i in range(len(ins)):
        for k in range(3):
            blk = 2 * chips[k][0] + chips[k][1]
            src, dst, got = (ins[i].at[blk], lands[i].at[k], lands[i].at[k]) if i < ng else (ins[i], lands[i].at[b], lands[i].at[blk])
            sems = dict(send_sem=send.at[3 * i + k], recv_sem=recv.at[3 * i + k], device_id=(*chips[k], c), device_id_type=MESH)
            if arriving:
                copies.append(pltpu.make_async_remote_copy(src_ref=got, dst_ref=got, **sems))
            else:
                copies.append(pltpu.make_async_remote_copy(src_ref=src, dst_ref=dst, **sems))
    return copies


def _exchange_begin(sums_b, smalls, name):
    ng, n = len(sums_b), len(sums_b) + len(smalls)
    srcs = list(sums_b) + list(smalls)
    lands = [lax.empty((3,) + g.shape[1:], g.dtype) for g in sums_b] + [lax.empty((N_CHIP,) + s.shape, s.dtype) for s in smalls]

    def body(*refs):
        ins, land_refs = refs[:n], refs[n:2 * n]
        send, recv = refs[2 * n], refs[2 * n + 1]
        token = refs[4 * n + 2]
        for cp in _split_copies(ins, land_refs, ng, send, recv, False):
            cp.start()
        token[...] = jnp.zeros_like(token)

    hbm = lambda a: pltpu.HBM(a.shape, a.dtype)
    outs = pl.pallas_call(
        body, name=name,
        out_shape=(pltpu.SemaphoreType.DMA((3 * n,)), pltpu.SemaphoreType.DMA((3 * n,)), *[hbm(a) for a in srcs + lands],
                   jax.ShapeDtypeStruct((8, 128), F32)),
        in_specs=[_HBM] * (2 * n), out_specs=(_SEM, _SEM, *[_HBM] * (2 * n), pl.BlockSpec(memory_space=pltpu.VMEM)),
        input_output_aliases={i: 2 + i for i in range(2 * n)},
        compiler_params=pltpu.CompilerParams(has_side_effects=pltpu.SideEffectType.DATAFLOW_SIDE_EFFECTING),
    )(*[pltpu.with_memory_space_constraint(a, pltpu.HBM) for a in srcs + lands])
    return outs[0], outs[1], list(outs[2:2 + n]), list(outs[2 + n:2 + 2 * n]), outs[2 + 2 * n]


def _exchange_end(send, recv, srcs, lands, ng, which, after, name):
    n = len(srcs)
    after = list(after)

    def body(*refs):
        ins, land_refs = refs[:n], refs[n:2 * n]
        send_ref, recv_ref = refs[2 * n], refs[2 * n + 1]
        outgoing = _split_copies(ins, land_refs, ng, send_ref, recv_ref, False)
        arriving = _split_copies(ins, land_refs, ng, send_ref, recv_ref, True)
        for i in which:
            for cp in outgoing[3 * i:3 * i + 3]:
                cp.wait_send()
        for i in which:
            for cp in arriving[3 * i:3 * i + 3]:
                cp.wait_recv()

    hbm = lambda a: pltpu.HBM(a.shape, a.dtype)
    outs = pl.pallas_call(
        body, name=name, out_shape=tuple(hbm(a) for a in list(srcs) + list(lands)),
        in_specs=[_HBM] * (2 * n) + [_SEM, _SEM] + [ANY] * len(after), out_specs=tuple([_HBM] * (2 * n)),
        input_output_aliases={i: i for i in range(2 * n)},
        compiler_params=pltpu.CompilerParams(has_side_effects=pltpu.SideEffectType.DATAFLOW_SIDE_EFFECTING),
    )(*srcs, *lands, send, recv, *after)
    return list(outs[:n]), list(outs[n:])


def _chip_reduce(bc_idx, sums, recvd, smalls_slots, smalls_own, name, steps=4):
    ng, ns = len(sums), len(smalls_slots)
    n = ng + ns
    assert steps >= 2
    halves = [g.shape[1] for g in sums] + [s.shape[1] for s in smalls_slots]
    rows = [g.shape[1] // steps for g in sums]

    def body(bc_ref, *refs):
        own, rx = refs[:ng], refs[ng:2 * ng]
        sl = refs[2 * ng:2 * ng + ns]
        sl_own = refs[2 * ng + ns:2 * ng + 2 * ns]
        o = refs[2 * ng + 2 * ns:2 * ng + 2 * ns + n]
        tiles = refs[2 * ng + 2 * ns + n:2 * ng + 2 * ns + 2 * n]
        keep, send, recv = refs[2 * ng + 2 * ns + 2 * n:]
        x, y, c, _ = _coords()
        sibling = dict(device_id=(x, y, 1 - c), device_id_type=MESH)
        r = pl.program_id(0)

        def writes(i, step, slot):
            dst = o[i].at[pl.ds(c * halves[i] + step * rows[i], rows[i])]
            return (pltpu.make_async_copy(tiles[i].at[slot], dst, keep.at[i, slot]),
                    pltpu.make_async_remote_copy(src_ref=tiles[i].at[slot], dst_ref=dst, send_sem=send.at[i, slot],
                                                 recv_sem=recv.at[i, step], **sibling))

        def small_writes(i):
            dst = o[i].at[pl.ds(c * halves[i], halves[i])]
            return (pltpu.make_async_copy(tiles[i], dst, keep.at[i, 0]),
                    pltpu.make_async_remote_copy(src_ref=tiles[i], dst_ref=dst, send_sem=send.at[i, 0],
                                                 recv_sem=recv.at[i, 0], **sibling))

        def arriving(i, step, nrows):
            dst = o[i].at[pl.ds((1 - c) * halves[i] + step * nrows, nrows)]
            return pltpu.make_async_remote_copy(src_ref=dst, dst_ref=dst, send_sem=send.at[i, 0], recv_sem=recv.at[i, step],
                                                **sibling)

        def finish(step, slot):
            for i in range(ng):
                local, remote = writes(i, step, slot)
                local.wait()
                remote.wait_send()

        @pl.when(r >= 2)
        def _():
            finish(r - 2, r % 2)

        for i in range(ng):
            tot = own[i][...]
            for j in range(3):
                tot = tot + rx[i][j].astype(F32)
            tiles[i][r % 2] = tot
            for cp in writes(i, r, r % 2):
                cp.start()

        @pl.when(r == 0)
        def _():
            for i in range(ns):
                term = [jnp.where(bc_ref[0] == kk, sl_own[i][...], sl[i][kk]) for kk in range(N_CHIP)]
                tiles[ng + i][...] = ((term[0] + term[1]) + term[2]) + term[3]
                for cp in small_writes(ng + i):
                    cp.start()

        @pl.when(r == steps - 1)
        def _():
            finish(steps - 2, (steps - 2) % 2)
            finish(steps - 1, (steps - 1) % 2)
            for i in range(ns):
                local, remote = small_writes(ng + i)
                local.wait()
                remote.wait_send()
                arriving(ng + i, 0, halves[ng + i]).wait_recv()
            for i in range(ng):
                for step in range(steps):
                    arriving(i, step, rows[i]).wait_recv()

    in_specs = [pl.BlockSpec((None, rows[i], g.shape[2]), lambda r, bc: (bc[0], r, 0)) for i, g in enumerate(sums)]
    in_specs += [pl.BlockSpec((3, rows[i], g.shape[2]), lambda r, bc: (0, r, 0)) for i, g in enumerate(sums)]
    in_specs += [pl.BlockSpec(s.shape, lambda r, bc: (0, 0, 0)) for s in smalls_slots]
    in_specs += [pl.BlockSpec(s.shape[1:], lambda r, bc: (0, 0)) for s in smalls_slots]
    out_shape = [jax.ShapeDtypeStruct((2 * g.shape[1], g.shape[2]), F32) for g in sums]
    out_shape += [jax.ShapeDtypeStruct((2 * s.shape[1], s.shape[2]), F32) for s in smalls_slots]
    scratch = [pltpu.VMEM((2, rows[i], g.shape[2]), F32) for i, g in enumerate(sums)]
    scratch += [pltpu.VMEM(s.shape[1:], F32) for s in smalls_slots]
    scratch += [pltpu.SemaphoreType.DMA((n, 2)), pltpu.SemaphoreType.DMA((n, 2)), pltpu.SemaphoreType.DMA((n, steps))]
    return list(pl.pallas_call(
        body, out_shape=out_shape,
        grid_spec=pltpu.PrefetchScalarGridSpec(num_scalar_prefetch=1, grid=(steps,), in_specs=in_specs,
                                               out_specs=[ANY] * n, scratch_shapes=scratch),
        compiler_params=_cp(("arbitrary",), VMEM_LIMIT), name=name)(bc_idx, *sums, *recvd, *smalls_slots, *smalls_own))


def _adamw_math(w, g, m, v):
    m2 = ADAM_B1 * m + (1.0 - ADAM_B1) * g
    v2 = ADAM_B2 * v + (1.0 - ADAM_B2) * (g * g)
    m_hat = m2 / (1.0 - ADAM_B1 ** ADAM_STEP)
    v_hat = v2 / (1.0 - ADAM_B2 ** ADAM_STEP)
    delta = -ADAM_LR * (m_hat / (jnp.sqrt(v_hat) + ADAM_EPS) + ADAM_WD * w)
    return delta, m2, v2


def _adamw_big(ws, gs, ms, vs, name, steps=8):
    n = len(ws)

    def body(*refs):
        for i in range(n):
            w_ref, g_ref, m_ref, v_ref = (refs[k * n + i] for k in range(4))
            d_ref, m2_ref, v2_ref, g2_ref = (refs[(4 + k) * n + i] for k in range(4))
            gv = g_ref[...]
            d_ref[...], m2_ref[...], v2_ref[...] = _adamw_math(w_ref[...], gv, m_ref[...], v_ref[...])
            g2_ref[...] = gv

    specs = [pl.BlockSpec((w.shape[0] // steps, w.shape[1]), lambda i: (i, 0)) for w in ws]
    shapes = [jax.ShapeDtypeStruct(w.shape, F32) for w in ws]
    outs = pl.pallas_call(
        body, grid=(steps,), in_specs=specs * 4, out_specs=specs * 4, out_shape=shapes * 4,
        compiler_params=_cp(("parallel",), VMEM_LIMIT), name=name)(*ws, *gs, *ms, *vs)
    return [tuple(outs[k * n + i] for k in range(4)) for i in range(n)]


def _adamw_small(b_idx, sv, sw, vecs, conv, ws):
    nv = len(vecs)
    cols = conv[0].shape[1]

    def body(b_ref, sv_ref, sw_ref, *refs):
        ins, outs = refs[:3 * nv + 6], refs[3 * nv + 6:]
        for i in range(nv):
            g = sv_ref[i:i + 1, :]
            w_ref, m_ref, v_ref = ins[3 * i:3 * i + 3]
            d_ref, m2_ref, v2_ref, g_ref = outs[4 * i:4 * i + 4]
            d_ref[...], m2_ref[...], v2_ref[...] = _adamw_math(w_ref[...], g, m_ref[...], v_ref[...])
            g_ref[...] = g
        g = sv_ref[8:8 + conv[0].shape[0], pl.ds(pl.multiple_of(b_ref[0] * cols, cols), cols)]
        w_ref, m_ref, v_ref = ins[3 * nv:3 * nv + 3]
        d_ref, m2_ref, v2_ref, g_ref = outs[4 * nv:4 * nv + 4]
        d_ref[...], m2_ref[...], v2_ref[...] = _adamw_math(w_ref[...], g, m_ref[...], v_ref[...])
        g_ref[...] = g
        w_ref, m_ref, v_ref = ins[3 * nv + 3:]
        d_ref, m2_ref, v2_ref = outs[4 * nv + 4:]
        d_ref[...], m2_ref[...], v2_ref[...] = _adamw_math(w_ref[...], sw_ref[...], m_ref[...], v_ref[...])

    flat = [a for grp in vecs for a in grp] + list(conv) + list(ws)
    out_shape = [jax.ShapeDtypeStruct(grp[0].shape, F32) for grp in list(vecs) + [conv] for _ in range(4)]
    out_shape += [jax.ShapeDtypeStruct(ws[0].shape, F32)] * 3
    vmem = pl.BlockSpec(memory_space=pltpu.VMEM)
    outs = pl.pallas_call(
        body, out_shape=out_shape, in_specs=[pl.BlockSpec(memory_space=pltpu.SMEM)] + [vmem] * (2 + len(flat)),
        out_specs=[vmem] * len(out_shape), name="adamw_small")(b_idx, sv, sw, *flat)
    return [tuple(outs[4 * i:4 * i + 4]) for i in range(nv + 1)] + [tuple(outs[4 * nv + 4:])]


def kernel(x, mem, norm_mix_g, w_in, conv_w, gm_ln_g, gm_ln_b, gm_ws, gm_bs, w_out, norm_x_g, norm_mem_g, w_q, w_kv, w_xo, norm_final_g, loss_target, m_norm_mix_g, m_w_in, m_conv_w, m_gm_ln_g, m_gm_ln_b, m_gm_ws, m_gm_bs, m_w_out, m_norm_x_g, m_norm_mem_g, m_w_q, m_w_kv, m_w_xo, m_norm_final_g, v_norm_mix_g, v_w_in, v_conv_w, v_gm_ln_g, v_gm_ln_b, v_gm_ws, v_gm_bs, v_w_out, v_norm_x_g, v_norm_mem_g, v_w_q, v_w_kv, v_w_xo, v_norm_final_g):
    t = x.shape[1]
    xi = lax.axis_index("x")
    yi = lax.axis_index("y")
    ci = lax.axis_index("c")
    b_idx = jnp.reshape(2 * xi + yi, (1,)).astype(jnp.int32)
    c_idx = jnp.reshape(ci, (1,)).astype(jnp.int32)

    x2d, mem2d, tgt = x[0], mem[0], loss_target[0]
    big = [w_in[0], w_out[0], w_q[0], w_kv[0], w_xo[0]]
    big_m = [m_w_in[0], m_w_out[0], m_w_q[0], m_w_kv[0], m_w_xo[0]]
    big_v = [v_w_in[0], v_w_out[0], v_w_q[0], v_w_kv[0], v_w_xo[0]]
    g3 = norm_final_g.reshape(1, D)

    def pad8(a):
        return jnp.pad(a, ((0, 8 - a.shape[0]), (0, 0)))

    own_blocks = _cast_shards(b_idx, big)

    tril = jnp.tril(jnp.ones((CH, CH), bool))
    wc32 = jnp.where(tril[None], gm_ws[0], 0.0)
    wc = wc32.astype(BF16)
    wct = jnp.swapaxes(wc32, 1, 2).astype(BF16)
    bsb = jnp.broadcast_to(gm_bs[0][:, :, None], (HEADS, CH, CH))

    blk = 2 * xi + yi
    near = [blk ^ (2 >> (k % 2)) for k in range(2 * N_PIECE)]
    seq_blk = jnp.stack([blk] * N_PIECE + near + [blk ^ 3] * N_PIECE)
    seq_piece = jnp.asarray(list(range(N_PIECE)) + [k // 2 for k in range(2 * N_PIECE)] + list(range(N_PIECE)))
    seq = jnp.stack([seq_blk, seq_piece, seq_blk * N_PIECE + seq_piece]).astype(jnp.int32)
    proj, hb, win_f, cw8, (wq_f,) = _proj_gather(
        seq, x2d, norm_mix_g, own_blocks[0], pad8(conv_w[0]), [own_blocks[2]])
    mixin, (wout_f, wkv_f, wxo_f) = _mixer_fwd(
        proj, cw8, gm_ln_g, gm_ln_b, wc, bsb, [own_blocks[1], own_blocks[3], own_blocks[4]])
    wout2, wq2, wxo2 = wout_f.reshape(MIX, D), wq_f.reshape(D, D), wxo_f.reshape(D, D)
    k, v = _mem_fwd(mem2d, norm_mem_g, wkv_f)

    (loss_row, dmix, dx1b, h2b, dq, ob, dx2b, dk, dv, dg2, dg3) = _tail(
        x2d, tgt, mixin, wout2, wq2, wxo2, k, v, norm_x_g, g3)
    dwkv, dwkv_b, dgm = _mem_bwd(mem2d, norm_mem_g, dk, dv, wkv_f)
    dproj, dcw, dlng, dlnb, dwc, dbs, grad_x, dg1 = _mixer_bwd(
        proj, dmix, cw8, gm_ln_g, gm_ln_b, wc, wct, bsb, win_f, x2d, dx1b, norm_mix_g)

    bc_idx = jnp.concatenate([b_idx, c_idx])
    sv = jnp.concatenate([dg1, dg2, dgm, dg3, dlng, dlnb, dbs, loss_row, dcw], axis=0)
    sw = dwc.reshape(HEADS * CH, CH)
    dwin_sum, dwin_sum_b, psmall = _grad_matmul_pair(c_idx, hb, dproj, [sv, sw], dgm, name="grad_w_in")
    sums_b = [dwin_sum]
    send_b, recv_b, src_b, land_b, token_b = _exchange_begin([dwin_sum_b], psmall, "exchange_b_begin")

    dwxo, dwxo_b = _grad_matmul(ob, dx2b, token_b, by_cols=False, name="grad_w_xo", tk=2048)
    dwq, dwq_b = _grad_matmul(h2b, dq, token_b, by_cols=False, name="grad_w_q", tk=2048)
    dwout, dwout_b = _grad_matmul(mixin, dx1b, token_b, by_cols=False, name="grad_w_out")
    ps_a = _pair_reduce(c_idx, [dwout, dwkv, dwq, dwxo], [dwout_b, dwkv_b, dwq_b, dwxo_b], [], "pair_reduce_a")
    sums_a, sums_a_b = list(ps_a[:4]), list(ps_a[4:8])
    send_a, recv_a, src_a, land_a, token_a = _exchange_begin(sums_a_b, [], "exchange_a_begin")

    _, rx2b = _exchange_end(send_b, recv_b, src_b, land_b, 1, [0, 1, 2], [token_a], "exchange_b_end")
    gwin, svf, swf = _chip_reduce(bc_idx, sums_b, rx2b[:1], rx2b[1:], psmall, "chip_reduce_b")
    out_b = _adamw_big(big[:1], [gwin], big_m[:1], big_v[:1], "adamw_w_in")[0]

    loss = svf[7, 0]
    row = lambda a: a.reshape(1, D)
    mat = lambda a: a.reshape(HEADS * CH, CH)
    small = _adamw_small(
        b_idx, svf, swf,
        [(row(norm_mix_g), row(m_norm_mix_g), row(v_norm_mix_g)), (row(norm_x_g), row(m_norm_x_g), row(v_norm_x_g)),
         (row(norm_mem_g), row(m_norm_mem_g), row(v_norm_mem_g)), (row(norm_final_g), row(m_norm_final_g), row(v_norm_final_g)),
         (row(gm_ln_g), row(m_gm_ln_g), row(v_gm_ln_g)), (row(gm_ln_b), row(m_gm_ln_b), row(v_gm_ln_b)),
         (row(gm_bs), row(m_gm_bs), row(v_gm_bs))],
        (conv_w[0], m_conv_w[0], v_conv_w[0]), (mat(gm_ws), mat(m_gm_ws), mat(v_gm_ws)))
    small_ws = small[8] + (swf,)

    def finish_a(part, src, land, after, tag):
        src, land = _exchange_end(send_a, recv_a, src, land, 4, part, after, "exchange_a%s_end" % tag)
        grads = _chip_reduce(bc_idx, [sums_a[i] for i in part], [land[i] for i in part], [], [], "chip_reduce_a" + tag,
                             steps=2)
        ids = [(1, 3, 2, 4)[i] for i in part]
        outs = _adamw_big([big[i] for i in ids], grads, [big_m[i] for i in ids], [big_v[i] for i in ids], "adamw_a" + tag,
                          steps=4)
        return src, land, outs

    src_a, land_a, (out_wout, out_wkv) = finish_a([0, 1], src_a, land_a, [out_b[0], small[0][0]], "1")
    _, _, (out_wq, out_wxo) = finish_a([2, 3], src_a, land_a, [out_wout[0]], "2")

    def unpack(k):
        vec = lambda i: small[i][k]
        return [vec(0), out_b[k][None], small[7][k][None], vec(4), vec(5), small_ws[k].reshape(1, HEADS, CH, CH),
                vec(6).reshape(1, HEADS, CH), out_wout[k][None], vec(1), vec(2), out_wq[k][None], out_wkv[k][None],
                out_wxo[k][None], vec(3).reshape(D)]

    return (loss, grad_x[None], *unpack(3), *unpack(0), *unpack(1), *unpack(2))
```

```python
import functools
import math

import jax
import jax.numpy as jnp
from jax import lax
from jax.experimental import pallas as pl
from jax.experimental.pallas import tpu as pltpu

F32 = jnp.float32
BF16 = jnp.bfloat16
MESH = pl.DeviceIdType.MESH

D = 1024
SLAB = 1024
N_SLAB = 7
IN_DIM = N_SLAB * SLAB
MIX = 2 * SLAB
HEADS = 8
CH = 128
XH = 4
XD = D // XH
EPS = 1e-6
GELU_C = math.sqrt(2.0 / math.pi)
GELU_A = 0.044715
N_CHIP = 4
IN_BLK = IN_DIM // N_CHIP
IN_PIECE = 256
N_PIECE = IN_BLK // IN_PIECE
KV_BLK = 2 * D // N_CHIP

ADAM_LR, ADAM_B1, ADAM_B2, ADAM_EPS, ADAM_WD, ADAM_STEP = 0.001, 0.9, 0.999, 1e-08, 0.01, 10

VMEM_LIMIT = 60 * 1024 * 1024


def _cp(sem=None, vmem=None):
    return pltpu.CompilerParams(dimension_semantics=sem, vmem_limit_bytes=vmem)


def _full(shape, buffers=None):
    n = len(shape)
    if buffers is None:
        return pl.BlockSpec(shape, lambda *_: (0,) * n)
    return pl.BlockSpec(shape, lambda *_: (0,) * n, pipeline_mode=pl.Buffered(buffers))


ANY = pl.BlockSpec(memory_space=pl.ANY)


def _bdot(a, b):
    return jnp.dot(a.astype(BF16), b.astype(BF16), preferred_element_type=F32)


def _bdot_nt(a, b):
    return lax.dot_general(a.astype(BF16), b.astype(BF16), (((1,), (1,)), ((), ())), preferred_element_type=F32)


def _bdot_tn(a, b):
    return lax.dot_general(a.astype(BF16), b.astype(BF16), (((0,), (0,)), ((), ())), preferred_element_type=F32)


def _rms(x, g):
    r = lax.rsqrt(jnp.mean(x * x, axis=-1, keepdims=True) + EPS)
    return x * r * g, r


def _rms_bwd(dy, x, r, g):
    gdy = dy * g
    dx = r * gdy - x * (r * r * r) * jnp.mean(x * gdy, axis=-1, keepdims=True)
    dg = jnp.sum(dy * x * r, axis=0, keepdims=True)
    return dx, dg


def _gelu_parts(x):
    x2 = x * x
    t = jnp.tanh(GELU_C * (x + GELU_A * x * x2))
    val = 0.5 * x * (1.0 + t)
    grad = 0.5 * (1.0 + t) + 0.5 * x * (1.0 - t * t) * (GELU_C * (1.0 + 3.0 * GELU_A * x2))
    return val, grad


def _gelu(x):
    return 0.5 * x * (1.0 + jnp.tanh(GELU_C * (x + GELU_A * x * x * x)))


def _sigmoid(z):
    return 1.0 / (1.0 + jnp.exp(-z))


def _cast_shards(b_idx, arrs):
    n = len(arrs)
    steps = 8

    def body(b_ref, *refs):
        for p in range(N_PIECE):
            refs[n][p] = refs[0][:, pl.ds(p * IN_PIECE, IN_PIECE)].astype(BF16)
        for i in range(1, n):
            refs[n + i][...] = refs[i][...].astype(BF16)

    rows = [a.shape[0] // steps for a in arrs]
    in_specs = [pl.BlockSpec((rows[i], a.shape[1]), lambda i, b: (i, 0)) for i, a in enumerate(arrs)]
    out_specs = [pl.BlockSpec((None, N_PIECE, rows[0], IN_PIECE), lambda i, b: (b[0], 0, i, 0))]
    out_specs += [pl.BlockSpec((None, rows[i], a.shape[1]), lambda i, b: (b[0], i, 0)) for i, a in enumerate(arrs) if i > 0]
    out_shape = [jax.ShapeDtypeStruct((N_CHIP, N_PIECE, arrs[0].shape[0], IN_PIECE), BF16)]
    out_shape += [jax.ShapeDtypeStruct((N_CHIP,) + a.shape, BF16) for a in arrs[1:]]
    return pl.pallas_call(
        body, out_shape=out_shape,
        grid_spec=pltpu.PrefetchScalarGridSpec(num_scalar_prefetch=1, grid=(steps,), in_specs=in_specs, out_specs=out_specs),
        compiler_params=_cp(("arbitrary",)), name="cast_shards")(b_idx, *arrs)


def _proj_gather(b_idx, x, g, win_own, cw8s, more, tm=1024):
    t = x.shape[0]
    ni = t // tm
    nm = len(more)
    steps = N_CHIP * N_PIECE
    near0, far0 = N_PIECE, 3 * N_PIECE

    def piece_at(step, own):
        k = step - near0
        near, far = (step >= near0) & (step < far0), step >= far0
        block = jnp.where(far, own ^ 3, jnp.where(near, own ^ jnp.where(lax.rem(k, 2) == 0, 2, 1), own))
        return block, jnp.where(far, step - far0, jnp.where(near, lax.div(k, 2), step))

    def body(*refs):
        b_ref, x_any, g_ref, win_in, cw_in = refs[:5]
        o_ref, hb_any, win_f, cw_out = refs[5 + nm:9 + nm]
        more_out = refs[9 + nm:9 + 2 * nm]
        hbuf, xbuf, wv, cw_s, cw_r, loc = refs[9 + 2 * nm:15 + 2 * nm]
        g_in = _Gather([win_f.at[:, p] for p in range(N_PIECE)], *refs[15 + 2 * nm:19 + 2 * nm])
        g_more = _Gather(more_out, *refs[19 + 2 * nm:23 + 2 * nm])
        s = pl.program_id(0)
        x, y, c, chips = _coords()
        b = 2 * x + y
        blks = [2 * chip[0] + chip[1] for chip in chips]

        def cw_cols(blk):
            return cw_out.at[:, pl.ds(blk * (D // N_CHIP), D // N_CHIP)]

        def cw_copy(k, blk):
            src = cw_in if blk is None else cw_cols(blk)
            return pltpu.make_async_remote_copy(src_ref=src, dst_ref=cw_cols(b if blk is None else blk), send_sem=cw_s.at[k],
                                                recv_sem=cw_r.at[k], device_id=(*chips[k], c), device_id_type=MESH)

        cw_local = pltpu.make_async_copy(cw_in, cw_cols(b), loc.at[1])
        hb_copy = pltpu.make_async_copy(hbuf, hb_any, loc.at[0])

        def load(step):
            slot = lax.rem(step, 2)
            block, piece = piece_at(step, b_ref[0])
            return pltpu.make_async_copy(win_f.at[block, piece], wv.at[slot], loc.at[2 + slot])

        def chunk(i):
            return pltpu.make_async_copy(x_any.at[pl.ds(i * tm, tm)], xbuf.at[i % 2], loc.at[4 + i % 2])

        def first():
            g_in.start()
            cw_local.start()
            for k in range(3):
                cw_copy(k, None).start()
            load(0).start()
            chunk(0).start()
            for i in range(ni):
                if i + 1 < ni:
                    chunk(i + 1).start()
                chunk(i).wait()
                h, _ = _rms(xbuf[i % 2], g_ref[...])
                hbuf[pl.ds(i * tm, tm), :] = h.astype(BF16)
            hb_copy.start()

        events = {step: [] for step in range(steps)}
        events[0].append(first)
        for p in range(N_PIECE):
            events[2 * p + 2].append(functools.partial(g_in.hop, [p]))
            events[near0 + 2 * p - 1].append(functools.partial(g_in.near_ready, [p]))
            events[far0 + p - 2].append(functools.partial(g_in.far, [p]))
            events[far0 + p - 1].append(functools.partial(g_in.far_ready, [p]))
        events[2 * N_PIECE + 1].append(g_more.start)
        for step, todo in events.items():
            if todo:
                @pl.when(s == step)
                def _(todo=todo):
                    for do in todo:
                        do()

        @pl.when(s + 1 < steps)
        def _():
            load(s + 1).start()

        load(s).wait()
        for i in range(ni):
            rows = pl.ds(i * tm, tm)
            o_ref[rows, :] = jnp.dot(hbuf[rows, :], wv[lax.rem(s, 2)], preferred_element_type=F32).astype(BF16)

        @pl.when(s == steps - 1)
        def _():
            g_more.hop()
            g_more.far()
            for k in range(3):
                cw_copy(k, blks[k]).wait_recv()
            for k in range(3):
                cw_copy(k, None).wait_send()
            cw_local.wait()
            hb_copy.wait()
            g_more.near_ready()
            g_more.far_ready()
            g_in.drain()
            g_more.drain()

    def out_col(s, b):
        block, piece = piece_at(s, b[0])
        return 0, block * N_PIECE + piece

    in_specs = [ANY, pl.BlockSpec((1, D), lambda s, b: (0, 0)), ANY, ANY] + [ANY] * nm
    out_specs = [pl.BlockSpec((t, IN_PIECE), out_col), ANY, ANY, ANY] + [ANY] * nm
    outs = pl.pallas_call(
        body, out_shape=[jax.ShapeDtypeStruct((t, IN_DIM), BF16), jax.ShapeDtypeStruct((t, D), BF16),
                         jax.ShapeDtypeStruct(win_own.shape, BF16), jax.ShapeDtypeStruct((8, D), F32)]
        + [jax.ShapeDtypeStruct(f.shape, f.dtype) for f in more],
        grid_spec=pltpu.PrefetchScalarGridSpec(
            num_scalar_prefetch=1, grid=(steps,), in_specs=in_specs, out_specs=out_specs,
            scratch_shapes=[pltpu.VMEM((t, D), BF16), pltpu.VMEM((2, tm, D), F32), pltpu.VMEM((2, D, IN_PIECE), BF16)]
            + [pltpu.SemaphoreType.DMA((3,))] * 2 + [pltpu.SemaphoreType.DMA((6,))]
            + _gather_sems(N_PIECE) + _gather_sems(nm)),
        input_output_aliases={3: 2, **{5 + w: 4 + w for w in range(nm)}},
        compiler_params=_cp(("arbitrary",), VMEM_LIMIT), name="proj_gather")(b_idx, x, g, win_own, cw8s, *more)
    return outs[0], outs[1], outs[2], outs[3], outs[4:]


class _Gather:
    def __init__(self, outs, ici_s, ici_r, d2d_s, d2d_r):
        x, y, c, _ = _coords()
        self.outs, self.c = outs, c
        self.sems = ici_s, ici_r, d2d_s, d2d_r
        self.b, self.bx, self.by, self.bd = 2 * x + y, 2 * (1 - x) + y, 2 * x + (1 - y), 2 * (1 - x) + (1 - y)
        self.xn, self.yn, self.sib = (1 - x, y, c), (x, 1 - y, c), (x, y, 1 - c)

    def piece(self, w, blk, hc, quarter=None):
        hr = self.outs[w].shape[1] // 2
        if quarter is None:
            return self.outs[w].at[blk, pl.ds(hc * hr, hr)]
        return self.outs[w].at[blk, pl.ds(hc * hr + quarter * (hr // 2), hr // 2)]

    def ici(self, w, k, ref, to):
        return pltpu.make_async_remote_copy(src_ref=ref, dst_ref=ref, send_sem=self.sems[0].at[w, k],
                                            recv_sem=self.sems[1].at[w, k], device_id=to, device_id_type=MESH)

    def d2d(self, w, k, ref):
        return pltpu.make_async_remote_copy(src_ref=ref, dst_ref=ref, send_sem=self.sems[2].at[w, k],
                                            recv_sem=self.sems[3].at[w, k], device_id=self.sib, device_id_type=MESH)

    def all(self):
        return range(len(self.outs))

    def start(self):
        for w in self.all():
            mine = self.piece(w, self.b, self.c)
            self.ici(w, 0, mine, self.xn).start()
            self.ici(w, 1, mine, self.yn).start()

    def hop(self, ws=None):
        c = self.c
        for w in ws or self.all():
            self.ici(w, 0, self.piece(w, self.bx, c), self.xn).wait_recv()
            self.ici(w, 1, self.piece(w, self.by, c), self.yn).wait_recv()
            self.ici(w, 2, self.piece(w, self.bx, c, 0), self.yn).start()
            self.ici(w, 3, self.piece(w, self.by, c, 1), self.xn).start()
            self.d2d(w, 0, self.piece(w, self.bx, c)).start()
            self.d2d(w, 1, self.piece(w, self.by, c)).start()

    def near_ready(self, ws=None):
        for w in ws or self.all():
            self.d2d(w, 0, self.piece(w, self.bx, 1 - self.c)).wait_recv()
            self.d2d(w, 1, self.piece(w, self.by, 1 - self.c)).wait_recv()

    def far(self, ws=None):
        c = self.c
        for w in ws or self.all():
            self.ici(w, 2, self.piece(w, self.bd, c, 0), self.yn).wait_recv()
            self.ici(w, 3, self.piece(w, self.bd, c, 1), self.xn).wait_recv()
            self.d2d(w, 2, self.piece(w, self.bd, c, 0)).start()
            self.d2d(w, 3, self.piece(w, self.bd, c, 1)).start()

    def far_ready(self, ws=None):
        for w in ws or self.all():
            self.d2d(w, 2, self.piece(w, self.bd, 1 - self.c, 0)).wait_recv()
            self.d2d(w, 3, self.piece(w, self.bd, 1 - self.c, 1)).wait_recv()

    def drain(self):
        c = self.c
        for w in self.all():
            mine = self.piece(w, self.b, c)
            self.ici(w, 0, mine, self.xn).wait_send()
            self.ici(w, 1, mine, self.yn).wait_send()
            self.ici(w, 2, self.piece(w, self.bx, c, 0), self.yn).wait_send()
            self.ici(w, 3, self.piece(w, self.by, c, 1), self.xn).wait_send()
            self.d2d(w, 0, self.piece(w, self.bx, c)).wait_send()
            self.d2d(w, 1, self.piece(w, self.by, c)).wait_send()
            self.d2d(w, 2, self.piece(w, self.bd, c, 0)).wait_send()
            self.d2d(w, 3, self.piece(w, self.bd, c, 1)).wait_send()


def _gather_sems(nw):
    return [pltpu.SemaphoreType.DMA((max(nw, 1), 4))] * 4


def _mixer_fwd(proj, cw8, lng, lnb, wc, bsb, fulls, tm=256):
    t = proj.shape[0]
    nt = t // tm
    nch = tm // CH
    nw = len(fulls)

    def body(*refs):
        p_ref, cw_ref, lng_ref, lnb_ref, wc_ref, bsb_ref = refs[:6]
        mix_ref = refs[6 + nw]
        w_outs = refs[7 + nw:7 + 2 * nw]
        prev_ref = refs[7 + 2 * nw]
        gather = _Gather(w_outs, *refs[8 + 2 * nw:])

        @pl.when(pl.program_id(0) == 0)
        def _():
            gather.start()
            prev_ref[...] = jnp.zeros_like(prev_ref)

        @pl.when(pl.program_id(0) == nt // 2)
        def _():
            gather.hop()

        @pl.when(pl.program_id(0) == nt - 1)
        def _():
            gather.far()

        rows = lax.broadcasted_iota(jnp.int32, (tm, CH), 0)
        for s in range(HEADS):
            cs = pl.ds(CH * s, CH)

            def slab(k):
                return p_ref[:, pl.ds(k * SLAB + CH * s, CH)].astype(F32)

            gb, gc, xa, za = slab(0), slab(1), slab(2), slab(3)
            cx = gc * xa
            p6 = jnp.broadcast_to(prev_ref[6:7, cs], (tm, CH))
            p7 = jnp.broadcast_to(prev_ref[7:8, cs], (tm, CH))
            c1 = jnp.where(rows == 0, p7, pltpu.roll(cx, 1, 0))
            c2 = jnp.where(rows == 0, p6, jnp.where(rows == 1, p7, pltpu.roll(cx, 2, 0)))
            prev_ref[:, cs] = cx[tm - 8:, :]
            cv = cw_ref[0:1, cs] * c2 + cw_ref[1:2, cs] * c1 + cw_ref[2:3, cs] * cx
            mix_ref[:, cs] = (gb * cv * (za * _sigmoid(za))).astype(BF16)

            u, v, zb = slab(4), slab(5), slab(6)
            ug, vg = _gelu(u), _gelu(v)
            dlt = vg - jnp.mean(vg, axis=-1, keepdims=True)
            vhat = dlt * lax.rsqrt(jnp.mean(dlt * dlt, axis=-1, keepdims=True) + EPS)
            vn = (vhat * lng_ref[:, cs] + lnb_ref[:, cs]).astype(BF16)
            gate = ug * (zb * _sigmoid(zb))
            for c in range(nch):
                rs = slice(CH * c, CH * (c + 1))
                sp = jnp.dot(wc_ref[s], vn[rs], preferred_element_type=F32) + bsb_ref[s]
                mix_ref[rs, pl.ds(SLAB + CH * s, CH)] = (gate[rs] * sp).astype(BF16)

        @pl.when(pl.program_id(0) == nt - 1)
        def _():
            gather.near_ready()
            gather.far_ready()
            gather.drain()

    sems = _gather_sems(nw)
    outs = pl.pallas_call(
        body, grid=(nt,),
        in_specs=[pl.BlockSpec((tm, IN_DIM), lambda i: (i, 0)), _full((8, D)), _full((1, D)), _full((1, D)),
                  _full((HEADS, CH, CH)), _full((HEADS, CH, CH))] + [ANY] * nw,
        out_specs=[pl.BlockSpec((tm, MIX), lambda i: (i, 0))] + [ANY] * nw,
        out_shape=[jax.ShapeDtypeStruct((t, MIX), BF16)] + [jax.ShapeDtypeStruct(f.shape, f.dtype) for f in fulls],
        input_output_aliases={6 + w: 1 + w for w in range(nw)},
        scratch_shapes=[pltpu.VMEM((8, D), F32)] + sems,
        compiler_params=_cp(("arbitrary",), VMEM_LIMIT), name="mixer_fwd")(proj, cw8, lng, lnb, wc, bsb, *fulls)
    return outs[0], outs[1:]


def _mem_fwd(mem, gm, wkv_f):
    n_mem = mem.shape[0]

    def body(mem_ref, gm_ref, w_ref, k_ref, v_ref):
        m, _ = _rms(mem_ref[...], gm_ref[...])
        mb = m.astype(BF16)
        for j in range(N_CHIP):
            dst = k_ref if j < 2 else v_ref
            dst[:, pl.ds(KV_BLK * (j % 2), KV_BLK)] = jnp.dot(mb, w_ref[j], preferred_element_type=F32).astype(BF16)

    return pl.pallas_call(
        body, out_shape=[jax.ShapeDtypeStruct((n_mem, D), BF16), jax.ShapeDtypeStruct((n_mem, D), BF16)],
        compiler_params=_cp(None, VMEM_LIMIT), name="mem_fwd")(mem, gm, wkv_f)


def _tail(x, tgt, mixin, wout, wq, wxo, k, v, g2, g3, tm=512, sub=512):
    t = x.shape[0]
    n_mem = k.shape[0]
    scale = 1.0 / math.sqrt(XD)

    def body(x_ref, tgt_ref, mix_ref, wout_ref, wq_ref, wxo_ref, k_ref, v_ref, g2_ref, g3_ref,
             loss_ref, dmix_ref, dx1b_ref, h2_ref, dq_ref, o_ref, dx2b_ref, dk_ref, dv_ref, dg2_ref, dg3_ref):
        @pl.when(pl.program_id(0) == 0)
        def _():
            loss_ref[...] = jnp.zeros_like(loss_ref)
            dk_ref[...] = jnp.zeros_like(dk_ref)
            dv_ref[...] = jnp.zeros_like(dv_ref)
            dg2_ref[...] = jnp.zeros_like(dg2_ref)
            dg3_ref[...] = jnp.zeros_like(dg3_ref)

        g2, g3 = g2_ref[...], g3_ref[...]
        for sb in range(tm // sub):
            rs = pl.ds(sub * sb, sub)
            x1 = x_ref[rs, :] + jnp.dot(mix_ref[rs, :], wout_ref[...], preferred_element_type=F32)
            h2, r2 = _rms(x1, g2)
            h2b = h2.astype(BF16)
            h2_ref[rs, :] = h2b
            q = jnp.dot(h2b, wq_ref[...], preferred_element_type=F32).astype(BF16)
            probs, outs = [], []
            for hd in range(XH):
                hs = pl.ds(XD * hd, XD)
                s = _bdot_nt(q[:, XD * hd:XD * (hd + 1)], k_ref[:, hs]) * scale
                e = jnp.exp(s - jnp.max(s, axis=-1, keepdims=True))
                p = e / jnp.sum(e, axis=-1, keepdims=True)
                probs.append(p)
                outs.append(_bdot(p, v_ref[:, hs]))
            ob = jnp.concatenate(outs, axis=-1).astype(BF16)
            o_ref[rs, :] = ob
            x2 = x1 + jnp.dot(ob, wxo_ref[...], preferred_element_type=F32)
            y, r3 = _rms(x2, g3)
            diff = y - tgt_ref[rs, :]
            row_loss = jnp.sum(diff * diff, axis=-1, keepdims=True)
            loss_ref[...] += jnp.broadcast_to(jnp.sum(row_loss, axis=0, keepdims=True) * (0.5 / D), loss_ref.shape)

            dx2, dg3 = _rms_bwd(diff * (1.0 / D), x2, r3, g3)
            dg3_ref[...] += dg3
            dx2b = dx2.astype(BF16)
            dx2b_ref[rs, :] = dx2b
            do = _bdot_nt(dx2b, wxo_ref[...])
            dqs = []
            for hd in range(XH):
                hs = pl.ds(XD * hd, XD)
                p = probs[hd]
                do_h = do[:, XD * hd:XD * (hd + 1)]
                dv_ref[:, hs] += _bdot_tn(p, do_h)
                dp = _bdot_nt(do_h, v_ref[:, hs])
                ds = p * (dp - jnp.sum(dp * p, axis=-1, keepdims=True))
                dqs.append(_bdot(ds, k_ref[:, hs]) * scale)
                dk_ref[:, hs] += _bdot_tn(ds, q[:, XD * hd:XD * (hd + 1)]) * scale
            dq = jnp.concatenate(dqs, axis=-1).astype(BF16)
            dq_ref[rs, :] = dq
            dx1n, dg2 = _rms_bwd(_bdot_nt(dq, wq_ref[...]), x1, r2, g2)
            dg2_ref[...] += dg2
            dx1b = (dx2 + dx1n).astype(BF16)
            dx1b_ref[rs, :] = dx1b
            dmix_ref[rs, :] = _bdot_nt(dx1b, wout_ref[...]).astype(BF16)

    tok = lambda w: pl.BlockSpec((tm, w), lambda i: (i, 0))
    return pl.pallas_call(
        body, grid=(t // tm,),
        in_specs=[tok(D), tok(D), tok(MIX), _full((MIX, D), 1), _full((D, D), 1), _full((D, D), 1),
                  _full((n_mem, D), 1), _full((n_mem, D), 1), _full((1, D)), _full((1, D))],
        out_specs=[_full((1, D)), tok(MIX), tok(D), tok(D), tok(D), tok(D), tok(D),
                   _full((n_mem, D)), _full((n_mem, D)), _full((1, D)), _full((1, D))],
        out_shape=[jax.ShapeDtypeStruct((1, D), F32), jax.ShapeDtypeStruct((t, MIX), BF16),
                   jax.ShapeDtypeStruct((t, D), BF16),
                   jax.ShapeDtypeStruct((t, D), BF16), jax.ShapeDtypeStruct((t, D), BF16),
                   jax.ShapeDtypeStruct((t, D), BF16), jax.ShapeDtypeStruct((t, D), BF16),
                   jax.ShapeDtypeStruct((n_mem, D), F32), jax.ShapeDtypeStruct((n_mem, D), F32),
                   jax.ShapeDtypeStruct((1, D), F32), jax.ShapeDtypeStruct((1, D), F32)],
        compiler_params=_cp(("arbitrary",), VMEM_LIMIT), name="tail")(x, tgt, mixin, wout, wq, wxo, k, v, g2, g3)


def _mem_bwd(mem, gm, dk, dv, wkv_f):
    def body(mem_ref, gm_ref, dk_ref, dv_ref, w_ref, dw_ref, dwb_ref, dgm_ref):
        mem_v = mem_ref[...]
        m, rm = _rms(mem_v, gm_ref[...])
        mb = m.astype(BF16)
        dm = jnp.zeros_like(mem_v)
        for j in range(N_CHIP):
            src = dk_ref if j < 2 else dv_ref
            dkv = src[:, pl.ds(KV_BLK * (j % 2), KV_BLK)].astype(BF16)
            dw = _bdot_tn(mb, dkv)
            dw_ref[j] = dw
            dwb_ref[j] = dw.astype(BF16)
            dm = dm + _bdot_nt(dkv, w_ref[j])
        dgm_ref[...] = jnp.sum(dm * mem_v * rm, axis=0, keepdims=True)

    return pl.pallas_call(
        body, out_shape=[jax.ShapeDtypeStruct((N_CHIP, D, KV_BLK), F32), jax.ShapeDtypeStruct((N_CHIP, D, KV_BLK), BF16),
                         jax.ShapeDtypeStruct((1, D), F32)],
        compiler_params=_cp(None, VMEM_LIMIT), name="mem_bwd")(mem, gm, dk, dv, wkv_f)


def _mixer_bwd(proj, dmix, cw8, lng, lnb, wc, wct, bsb, win_f, x, dx1, g1, tm=256):
    t = proj.shape[0]
    nt = t // tm
    nch = tm // CH
    hb = 16
    pair = 2 * CH
    assert pair == IN_PIECE

    def body(p_ref, pgc_ref, pxa_ref, dm_ref, cw_ref, lng_ref, lnb_ref, wc_ref, wct_ref, bsb_ref, w_ref, x_ref,
             dx1_ref, g1_ref, dp_ref, dcw_ref, dlng_ref, dlnb_ref, dwc_ref, dbs_ref, gx_ref, dg1_ref,
             next_ref, dh_ref):
        i = pl.program_id(0)

        @pl.when(i == 0)
        def _():
            next_ref[...] = jnp.zeros_like(next_ref)
            dcw_ref[...] = jnp.zeros_like(dcw_ref)
            dlng_ref[...] = jnp.zeros_like(dlng_ref)
            dlnb_ref[...] = jnp.zeros_like(dlnb_ref)
            dwc_ref[...] = jnp.zeros_like(dwc_ref)
            dbs_ref[...] = jnp.zeros_like(dbs_ref)
            dg1_ref[...] = jnp.zeros_like(dg1_ref)

        first_tile = i == nt - 1
        rows = lax.broadcasted_iota(jnp.int32, (tm, CH), 0)
        ones8 = jnp.ones((8, CH), BF16)
        for s in range(HEADS):
            cs = pl.ds(CH * s, CH)

            def slab(k):
                return p_ref[:, pl.ds(k * SLAB + CH * s, CH)].astype(F32)

            gb, gc, xa, za = slab(0), slab(1), slab(2), slab(3)
            da = dm_ref[:, cs].astype(F32)
            cx = gc * xa
            cxp = pgc_ref[:, cs].astype(F32) * pxa_ref[:, cs].astype(F32)
            cxp = jnp.where(first_tile, jnp.zeros_like(cxp), cxp)
            p6 = jnp.broadcast_to(cxp[hb - 2:hb - 1, :], (tm, CH))
            p7 = jnp.broadcast_to(cxp[hb - 1:hb, :], (tm, CH))
            c1 = jnp.where(rows == 0, p7, pltpu.roll(cx, 1, 0))
            c2 = jnp.where(rows == 0, p6, jnp.where(rows == 1, p7, pltpu.roll(cx, 2, 0)))
            w0, w1, w2 = cw_ref[0:1, cs], cw_ref[1:2, cs], cw_ref[2:3, cs]
            cv = w0 * c2 + w1 * c1 + w2 * cx
            sg = _sigmoid(za)
            sa = za * sg
            dcv = da * gb * sa
            dp_ref[:, pl.ds(0 * SLAB + CH * s, CH)] = (da * cv * sa).astype(BF16)
            dp_ref[:, pl.ds(3 * SLAB + CH * s, CH)] = (da * gb * cv * (sg * (1.0 + za * (1.0 - sg)))).astype(BF16)
            n0 = jnp.broadcast_to(next_ref[0:1, cs], (tm, CH))
            n1 = jnp.broadcast_to(next_ref[1:2, cs], (tm, CH))
            u1 = jnp.where(rows == tm - 1, n0, pltpu.roll(dcv, tm - 1, 0))
            u2 = jnp.where(rows == tm - 2, n0, jnp.where(rows == tm - 1, n1, pltpu.roll(dcv, tm - 2, 0)))
            next_ref[:, cs] = dcv[0:8, :]
            dcx = w2 * dcv + w1 * u1 + w0 * u2
            dp_ref[:, pl.ds(1 * SLAB + CH * s, CH)] = (dcx * xa).astype(BF16)
            dp_ref[:, pl.ds(2 * SLAB + CH * s, CH)] = (dcx * gc).astype(BF16)
            dcw_ref[0:1, cs] += jnp.sum(dcv * c2, axis=0, keepdims=True)
            dcw_ref[1:2, cs] += jnp.sum(dcv * c1, axis=0, keepdims=True)
            dcw_ref[2:3, cs] += jnp.sum(dcv * cx, axis=0, keepdims=True)

            u, v, zb = slab(4), slab(5), slab(6)
            db = dm_ref[:, pl.ds(SLAB + CH * s, CH)].astype(F32)
            ug, ugrad = _gelu_parts(u)
            vg, vgrad = _gelu_parts(v)
            dlt = vg - jnp.mean(vg, axis=-1, keepdims=True)
            rstd = lax.rsqrt(jnp.mean(dlt * dlt, axis=-1, keepdims=True) + EPS)
            vhat = dlt * rstd
            lg = lng_ref[:, cs]
            vn = (vhat * lg + lnb_ref[:, cs]).astype(BF16)
            sgb = _sigmoid(zb)
            szb = zb * sgb
            sps, dvns = [], []
            dbs = jnp.zeros((8, CH), F32)
            dwc = jnp.zeros((CH, CH), F32)
            for c in range(nch):
                rs = slice(CH * c, CH * (c + 1))
                sp = jnp.dot(wc_ref[s], vn[rs], preferred_element_type=F32) + bsb_ref[s]
                dsp = (db[rs] * ug[rs] * szb[rs]).astype(BF16)
                dbs = dbs + lax.dot_general(ones8, dsp, (((1,), (1,)), ((), ())), preferred_element_type=F32)
                dwc = dwc + lax.dot_general(dsp, vn[rs], (((1,), (1,)), ((), ())), preferred_element_type=F32)
                dvns.append(jnp.dot(wct_ref[s], dsp, preferred_element_type=F32))
                sps.append(sp)
            sp = jnp.concatenate(sps, axis=0)
            dvn = jnp.concatenate(dvns, axis=0)
            dbs_ref[:, cs] += dbs[0:1]
            dwc_ref[s] += dwc
            dlng_ref[:, cs] += jnp.sum(dvn * vhat, axis=0, keepdims=True)
            dlnb_ref[:, cs] += jnp.sum(dvn, axis=0, keepdims=True)
            dvhat = dvn * lg
            dvg = rstd * (dvhat - jnp.mean(dvhat, axis=-1, keepdims=True)
                          - vhat * jnp.mean(dvhat * vhat, axis=-1, keepdims=True))
            dp_ref[:, pl.ds(4 * SLAB + CH * s, CH)] = (db * sp * szb * ugrad).astype(BF16)
            dp_ref[:, pl.ds(5 * SLAB + CH * s, CH)] = (dvg * vgrad).astype(BF16)
            dp_ref[:, pl.ds(6 * SLAB + CH * s, CH)] = (db * ug * sp * (sgb * (1.0 + zb * (1.0 - sgb)))).astype(BF16)

            if s % 2 == 1:
                part = None
                for k in range(N_SLAB):
                    col = k * SLAB + pair * (s // 2)
                    blk, off = divmod(col, IN_BLK)
                    term = lax.dot_general(dp_ref[:, pl.ds(col, pair)], w_ref[blk, off // IN_PIECE],
                                           (((1,), (1,)), ((), ())), preferred_element_type=F32)
                    part = term if part is None else part + term
                if s == 1:
                    dh_ref[...] = part
                else:
                    dh_ref[...] += part

        xv = x_ref[...]
        r = lax.rsqrt(jnp.mean(xv * xv, axis=-1, keepdims=True) + EPS)
        dxn, dg = _rms_bwd(dh_ref[...], xv, r, g1_ref[...])
        gx_ref[...] = dx1_ref[...].astype(F32) + dxn
        dg1_ref[...] += dg

        @pl.when(i == nt - 1)
        def _():
            tril = lax.broadcasted_iota(jnp.int32, (CH, CH), 0) >= lax.broadcasted_iota(jnp.int32, (CH, CH), 1)
            for s in range(HEADS):
                dwc_ref[s] = jnp.where(tril, dwc_ref[s], 0.0)

    rev = lambda i: nt - 1 - i
    halo = lambda col: pl.BlockSpec((hb, SLAB), lambda i: (jnp.maximum(rev(i) * (tm // hb) - 1, 0), col))
    tok = lambda w: pl.BlockSpec((tm, w), lambda i: (rev(i), 0))
    return pl.pallas_call(
        body, grid=(nt,),
        in_specs=[tok(IN_DIM), halo(1), halo(2), tok(MIX), _full((8, D)), _full((1, D)), _full((1, D)),
                  _full((HEADS, CH, CH)), _full((HEADS, CH, CH)), _full((HEADS, CH, CH)),
                  _full((N_CHIP, N_PIECE, D, IN_PIECE), 1), tok(D), tok(D), _full((1, D))],
        out_specs=[tok(IN_DIM), _full((8, D)), _full((1, D)), _full((1, D)), _full((HEADS, CH, CH)), _full((1, D)),
                   tok(D), _full((1, D))],
        out_shape=[jax.ShapeDtypeStruct((t, IN_DIM), BF16), jax.ShapeDtypeStruct((8, D), F32),
                   jax.ShapeDtypeStruct((1, D), F32), jax.ShapeDtypeStruct((1, D), F32),
                   jax.ShapeDtypeStruct((HEADS, CH, CH), F32), jax.ShapeDtypeStruct((1, D), F32),
                   jax.ShapeDtypeStruct((t, D), F32), jax.ShapeDtypeStruct((1, D), F32)],
        scratch_shapes=[pltpu.VMEM((8, D), F32), pltpu.VMEM((tm, D), F32)],
        compiler_params=_cp(("arbitrary",), VMEM_LIMIT), name="mixer_bwd")(
            proj, proj, proj, dmix, cw8, lng, lnb, wc, wct, bsb, win_f, x, dx1, g1)


def _grad_matmul(a, b, after, *, by_cols, name, tk=1024):
    t, m = a.shape
    n = b.shape[1]
    nk = t // tk
    nj = N_CHIP if by_cols else 1
    bn = n // nj

    def body(a_ref, b_ref, after_ref, o_ref, ob_ref):
        kk = pl.program_id(1)
        part = lax.dot_general(a_ref[...], b_ref[...], (((0,), (0,)), ((), ())), preferred_element_type=F32)

        @pl.when(kk == 0)
        def _():
            o_ref[...] = part

        @pl.when(kk > 0)
        def _():
            o_ref[...] += part

        @pl.when(kk == nk - 1)
        def _():
            ob_ref[...] = o_ref[...].astype(BF16)

    a_spec = pl.BlockSpec((tk, m), lambda j, k: (k, 0))
    b_spec = pl.BlockSpec((tk, bn), lambda j, k: (k, j))
    o_spec = pl.BlockSpec((None, m, bn), lambda j, k: (j, 0, 0))
    o32, o16 = pl.pallas_call(
        body, grid=(nj, nk), in_specs=[a_spec, b_spec, ANY], out_specs=[o_spec, o_spec],
        out_shape=[jax.ShapeDtypeStruct((nj, m, bn), F32), jax.ShapeDtypeStruct((nj, m, bn), BF16)],
        compiler_params=_cp(("parallel", "arbitrary"), VMEM_LIMIT), name=name)(a, b, after)
    if by_cols:
        return o32, o16
    return o32.reshape(N_CHIP, m // N_CHIP, n), o16.reshape(N_CHIP, m // N_CHIP, n)


def _coords():
    x, y, c = lax.axis_index("x"), lax.axis_index("y"), lax.axis_index("c")
    chips = [(1 - x, y), (x, 1 - y), (1 - x, 1 - y)]
    return x, y, c, chips


def _pair_reduce(c_idx, grads, grads_b, smalls, name):
    ng, ns = len(grads), len(smalls)
    halves = [g.shape[1] // 2 for g in grads]

    def body(c_ref, *refs):
        g_in, gb_any = refs[:ng], refs[ng:2 * ng]
        s_own, s_any = refs[2 * ng:2 * ng + ns], refs[2 * ng + ns:2 * ng + 2 * ns]
        o = refs[2 * ng + 2 * ns:4 * ng + 3 * ns]
        lands = refs[4 * ng + 3 * ns:5 * ng + 4 * ns]
        send, recv = refs[5 * ng + 4 * ns:]
        x, y, c, _ = _coords()
        j = pl.program_id(0)

        def big(i, blk):
            return pltpu.make_async_remote_copy(
                src_ref=gb_any[i].at[blk, pl.ds((1 - c) * halves[i], halves[i])], dst_ref=lands[i].at[blk],
                send_sem=send.at[i, blk], recv_sem=recv.at[i, blk], device_id=(x, y, 1 - c), device_id_type=MESH)

        def small(i):
            return pltpu.make_async_remote_copy(
                src_ref=s_any[i].at[1 - c], dst_ref=lands[ng + i],
                send_sem=send.at[ng + i, 0], recv_sem=recv.at[ng + i, 0], device_id=(x, y, 1 - c), device_id_type=MESH)

        @pl.when(j == 0)
        def _():
            for blk in range(N_CHIP):
                for i in range(ng):
                    big(i, blk).start()
            for i in range(ns):
                small(i).start()

        for i in range(ng):
            big(i, j).wait_recv()
            tot = g_in[i][...] + lands[i][j].astype(F32)
            o[i][...] = tot
            o[ng + i][...] = tot.astype(BF16)

        @pl.when(j == N_CHIP - 1)
        def _():
            for i in range(ns):
                small(i).wait_recv()
                o[2 * ng + i][...] = s_own[i][...] + lands[ng + i][...]
                small(i).wait_send()
            for blk in range(N_CHIP):
                for i in range(ng):
                    big(i, blk).wait_send()

    in_specs = [pl.BlockSpec((None, None, halves[i], g.shape[2]), lambda b, c: (b, c[0], 0, 0)) for i, g in enumerate(grads)]
    in_specs += [ANY] * ng
    in_specs += [pl.BlockSpec((None, s.shape[0] // 2, s.shape[1]), lambda b, c: (c[0], 0, 0)) for s in smalls]
    in_specs += [ANY] * ns
    blk = [pl.BlockSpec((None, halves[i], g.shape[2]), lambda b, c: (b, 0, 0)) for i, g in enumerate(grads)]
    out_specs = blk + blk + [pl.BlockSpec((s.shape[0] // 2, s.shape[1]), lambda b, c: (0, 0)) for s in smalls]
    out_shape = [jax.ShapeDtypeStruct((N_CHIP, halves[i], g.shape[2]), F32) for i, g in enumerate(grads)]
    out_shape += [jax.ShapeDtypeStruct((N_CHIP, halves[i], g.shape[2]), BF16) for i, g in enumerate(grads)]
    out_shape += [jax.ShapeDtypeStruct((s.shape[0] // 2, s.shape[1]), F32) for s in smalls]
    scratch = [pltpu.VMEM((N_CHIP, halves[i], g.shape[2]), BF16) for i, g in enumerate(grads)]
    scratch += [pltpu.VMEM((s.shape[0] // 2, s.shape[1]), F32) for s in smalls]
    scratch += [pltpu.SemaphoreType.DMA((ng + ns, N_CHIP)), pltpu.SemaphoreType.DMA((ng + ns, N_CHIP))]
    grads4 = [g.reshape(N_CHIP, 2, halves[i], g.shape[2]) for i, g in enumerate(grads)]
    smalls3 = [s.reshape(2, s.shape[0] // 2, s.shape[1]) for s in smalls]
    return pl.pallas_call(
        body, out_shape=out_shape,
        grid_spec=pltpu.PrefetchScalarGridSpec(num_scalar_prefetch=1, grid=(N_CHIP,), in_specs=in_specs,
                                               out_specs=out_specs, scratch_shapes=scratch),
        compiler_params=_cp(("arbitrary",), VMEM_LIMIT), name=name)(c_idx, *grads4, *grads_b, *smalls3, *smalls3)


def _grad_matmul_pair(c_idx, a, b, smalls, after, *, name, tk=2048):
    t, m = a.shape
    bn = b.shape[1] // N_CHIP
    nk = t // tk
    hr = m // 2
    ns = len(smalls)

    def body(c_ref, a_ref, b_ref, *refs):
        s_own, s_any = refs[:ns], refs[ns:2 * ns]
        o32, o16 = refs[2 * ns + 1], refs[2 * ns + 2]
        o_small = refs[2 * ns + 3:3 * ns + 3]
        acc, tb, land, st16 = refs[3 * ns + 3:3 * ns + 7]
        s_land, s_stage = refs[3 * ns + 7:4 * ns + 7], refs[4 * ns + 7:5 * ns + 7]
        send, recv, loc = refs[5 * ns + 7:]
        x, y, c, _ = _coords()
        sibling = dict(device_id=(x, y, 1 - c), device_id_type=MESH)
        j, kk = pl.program_id(0), pl.program_id(1)
        mine = pl.ds(pl.multiple_of(c * hr, hr), hr)
        theirs = pl.ds(pl.multiple_of((1 - c) * hr, hr), hr)

        def to_sibling(blk):
            return pltpu.make_async_remote_copy(src_ref=tb, dst_ref=land.at[blk], send_sem=send.at[blk],
                                                recv_sem=recv.at[blk], **sibling)

        def small(i):
            return pltpu.make_async_remote_copy(src_ref=s_any[i].at[1 - c], dst_ref=s_land[i], send_sem=send.at[N_CHIP + i],
                                                recv_sem=recv.at[N_CHIP + i], **sibling)

        def written(blk):
            return (pltpu.make_async_copy(acc.at[blk % 2, mine], o32.at[blk], loc.at[0]),
                    pltpu.make_async_copy(st16, o16.at[blk], loc.at[1]))

        def finish(blk):
            to_sibling(blk).wait_recv()

            @pl.when(blk > 0)
            def _():
                for cp in written(blk - 1):
                    cp.wait()

            tot = acc[blk % 2, mine, :] + land[blk].astype(F32)
            acc[blk % 2, mine, :] = tot
            st16[...] = tot.astype(BF16)
            for cp in written(blk):
                cp.start()

        def small_out(i):
            return pltpu.make_async_copy(s_stage[i], o_small[i], loc.at[2 + i])

        @pl.when((j == 0) & (kk == 0))
        def _():
            for i in range(ns):
                small(i).start()

        @pl.when((j == 1) & (kk == 0))
        def _():
            for i in range(ns):
                small(i).wait_recv()
                s_stage[i][...] = s_own[i][...] + s_land[i][...]
                small_out(i).start()

        @pl.when((j > 0) & (kk == 0))
        def _():
            finish(j - 1)

        part = lax.dot_general(a_ref[...], b_ref[...], (((0,), (0,)), ((), ())), preferred_element_type=F32)
        slot = lax.rem(j, 2)

        @pl.when(kk == 0)
        def _():
            acc[slot] = part

        @pl.when(kk > 0)
        def _():
            acc[slot] += part

        @pl.when(kk == nk - 1)
        def _():
            @pl.when(j > 0)
            def _():
                to_sibling(j - 1).wait_send()

            tb[...] = acc[slot, theirs, :].astype(BF16)
            to_sibling(j).start()

        @pl.when((j == N_CHIP - 1) & (kk == nk - 1))
        def _():
            finish(j)
            for i in range(ns):
                small_out(i).wait()
                small(i).wait_send()
            for cp in written(j):
                cp.wait()
            to_sibling(j).wait_send()

    halves = [(s.shape[0] // 2, s.shape[1]) for s in smalls]
    in_specs = [pl.BlockSpec((tk, m), lambda j, k, c: (k, 0)), pl.BlockSpec((tk, bn), lambda j, k, c: (k, j))]
    in_specs += [pl.BlockSpec((None,) + h, lambda j, k, c: (c[0], 0, 0)) for h in halves] + [ANY] * ns + [ANY]
    out_shape = [jax.ShapeDtypeStruct((N_CHIP, hr, bn), F32), jax.ShapeDtypeStruct((N_CHIP, hr, bn), BF16)]
    out_shape += [pltpu.HBM(h, F32) for h in halves]
    scratch = [pltpu.VMEM((2, m, bn), F32), pltpu.VMEM((hr, bn), BF16),
               pltpu.VMEM((N_CHIP, hr, bn), BF16), pltpu.VMEM((hr, bn), BF16)]
    scratch += [pltpu.VMEM(h, F32) for h in halves] * 2
    scratch += [pltpu.SemaphoreType.DMA((N_CHIP + ns,)), pltpu.SemaphoreType.DMA((N_CHIP + ns,)),
                pltpu.SemaphoreType.DMA((2 + ns,))]
    smalls3 = [s.reshape((2,) + h) for s, h in zip(smalls, halves)]
    outs = pl.pallas_call(
        body, out_shape=out_shape,
        grid_spec=pltpu.PrefetchScalarGridSpec(num_scalar_prefetch=1, grid=(N_CHIP, nk), in_specs=in_specs,
                                               out_specs=[ANY, ANY] + [_HBM] * ns, scratch_shapes=scratch),
        compiler_params=_cp(("arbitrary", "arbitrary"), VMEM_LIMIT), name=name)(c_idx, a, b, *smalls3, *smalls3, after)
    return outs[0], outs[1], list(outs[2:])


_HBM = pl.BlockSpec(memory_space=pltpu.HBM)
_SEM = pl.BlockSpec(memory_space=pltpu.SEMAPHORE)


def _split_copies(ins, lands, ng, send, recv, arriving):
    x, y, c, chips = _coords()
    b = 2 * x + y
    copies = []
    for i in range(len(ins)):
        for k in range(3):
            blk = 2 * chips[k][0] + chips[k][1]
            src, dst, got = (ins[i].at[blk], lands[i].at[k], lands[i].at[k]) if i < ng else (ins[i], lands[i].at[b], lands[i].at[blk])
            sems = dict(send_sem=send.at[3 * i + k], recv_sem=recv.at[3 * i + k], device_id=(*chips[k], c), device_id_type=MESH)
            if arriving:
                copies.append(pltpu.make_async_remote_copy(src_ref=got, dst_ref=got, **sems))
            else:
                copies.append(pltpu.make_async_remote_copy(src_ref=src, dst_ref=dst, **sems))
    return copies


def _exchange_begin(sums_b, smalls, name):
    ng, n = len(sums_b), len(sums_b) + len(smalls)
    srcs = list(sums_b) + list(smalls)
    lands = [lax.empty((3,) + g.shape[1:], g.dtype) for g in sums_b] + [lax.empty((N_CHIP,) + s.shape, s.dtype) for s in smalls]

    def body(*refs):
        ins, land_refs = refs[:n], refs[n:2 * n]
        send, recv = refs[2 * n], refs[2 * n + 1]
        token = refs[4 * n + 2]
        for cp in _split_copies(ins, land_refs, ng, send, recv, False):
            cp.start()
        token[...] = jnp.zeros_like(token)

    hbm = lambda a: pltpu.HBM(a.shape, a.dtype)
    outs = pl.pallas_call(
        body, name=name,
        out_shape=(pltpu.SemaphoreType.DMA((3 * n,)), pltpu.SemaphoreType.DMA((3 * n,)), *[hbm(a) for a in srcs + lands],
                   jax.ShapeDtypeStruct((8, 128), F32)),
        in_specs=[_HBM] * (2 * n), out_specs=(_SEM, _SEM, *[_HBM] * (2 * n), pl.BlockSpec(memory_space=pltpu.VMEM)),
        input_output_aliases={i: 2 + i for i in range(2 * n)},
        compiler_params=pltpu.CompilerParams(has_side_effects=pltpu.SideEffectType.DATAFLOW_SIDE_EFFECTING),
    )(*[pltpu.with_memory_space_constraint(a, pltpu.HBM) for a in srcs + lands])
    return outs[0], outs[1], list(outs[2:2 + n]), list(outs[2 + n:2 + 2 * n]), outs[2 + 2 * n]


def _exchange_end(send, recv, srcs, lands, ng, which, after, name):
    n = len(srcs)
    after = list(after)

    def body(*refs):
        ins, land_refs = refs[:n], refs[n:2 * n]
        send_ref, recv_ref = refs[2 * n], refs[2 * n + 1]
        outgoing = _split_copies(ins, land_refs, ng, send_ref, recv_ref, False)
        arriving = _split_copies(ins, land_refs, ng, send_ref, recv_ref, True)
        for i in which:
            for cp in outgoing[3 * i:3 * i + 3]:
                cp.wait_send()
        for i in which:
            for cp in arriving[3 * i:3 * i + 3]:
                cp.wait_recv()

    hbm = lambda a: pltpu.HBM(a.shape, a.dtype)
    outs = pl.pallas_call(
        body, name=name, out_shape=tuple(hbm(a) for a in list(srcs) + list(lands)),
        in_specs=[_HBM] * (2 * n) + [_SEM, _SEM] + [ANY] * len(after), out_specs=tuple([_HBM] * (2 * n)),
        input_output_aliases={i: i for i in range(2 * n)},
        compiler_params=pltpu.CompilerParams(has_side_effects=pltpu.SideEffectType.DATAFLOW_SIDE_EFFECTING),
    )(*srcs, *lands, send, recv, *after)
    return list(outs[:n]), list(outs[n:])


def _chip_reduce(bc_idx, sums, recvd, smalls_slots, smalls_own, name, steps=4):
    ng, ns = len(sums), len(smalls_slots)
    n = ng + ns
    assert steps >= 2
    halves = [g.shape[1] for g in sums] + [s.shape[1] for s in smalls_slots]
    rows = [g.shape[1] // steps for g in sums]

    def body(bc_ref, *refs):
        own, rx = refs[:ng], refs[ng:2 * ng]
        sl = refs[2 * ng:2 * ng + ns]
        sl_own = refs[2 * ng + ns:2 * ng + 2 * ns]
        o = refs[2 * ng + 2 * ns:2 * ng + 2 * ns + n]
        tiles = refs[2 * ng + 2 * ns + n:2 * ng + 2 * ns + 2 * n]
        keep, send, recv = refs[2 * ng + 2 * ns + 2 * n:]
        x, y, c, _ = _coords()
        sibling = dict(device_id=(x, y, 1 - c), device_id_type=MESH)
        r = pl.program_id(0)

        def writes(i, step, slot):
            dst = o[i].at[pl.ds(c * halves[i] + step * rows[i], rows[i])]
            return (pltpu.make_async_copy(tiles[i].at[slot], dst, keep.at[i, slot]),
                    pltpu.make_async_remote_copy(src_ref=tiles[i].at[slot], dst_ref=dst, send_sem=send.at[i, slot],
                                                 recv_sem=recv.at[i, step], **sibling))

        def small_writes(i):
            dst = o[i].at[pl.ds(c * halves[i], halves[i])]
            return (pltpu.make_async_copy(tiles[i], dst, keep.at[i, 0]),
                    pltpu.make_async_remote_copy(src_ref=tiles[i], dst_ref=dst, send_sem=send.at[i, 0],
                                                 recv_sem=recv.at[i, 0], **sibling))

        def arriving(i, step, nrows):
            dst = o[i].at[pl.ds((1 - c) * halves[i] + step * nrows, nrows)]
            return pltpu.make_async_remote_copy(src_ref=dst, dst_ref=dst, send_sem=send.at[i, 0], recv_sem=recv.at[i, step],
                                                **sibling)

        def finish(step, slot):
            for i in range(ng):
                local, remote = writes(i, step, slot)
                local.wait()
                remote.wait_send()

        @pl.when(r >= 2)
        def _():
            finish(r - 2, r % 2)

        for i in range(ng):
            tot = own[i][...]
            for j in range(3):
                tot = tot + rx[i][j].astype(F32)
            tiles[i][r % 2] = tot
            for cp in writes(i, r, r % 2):
                cp.start()

        @pl.when(r == 0)
        def _():
            for i in range(ns):
                term = [jnp.where(bc_ref[0] == kk, sl_own[i][...], sl[i][kk]) for kk in range(N_CHIP)]
                tiles[ng + i][...] = ((term[0] + term[1]) + term[2]) + term[3]
                for cp in small_writes(ng + i):
                    cp.start()

        @pl.when(r == steps - 1)
        def _():
            finish(steps - 2, (steps - 2) % 2)
            finish(steps - 1, (steps - 1) % 2)
            for i in range(ns):
                local, remote = small_writes(ng + i)
                local.wait()
                remote.wait_send()
                arriving(ng + i, 0, halves[ng + i]).wait_recv()
            for i in range(ng):
                for step in range(steps):
                    arriving(i, step, rows[i]).wait_recv()

    in_specs = [pl.BlockSpec((None, rows[i], g.shape[2]), lambda r, bc: (bc[0], r, 0)) for i, g in enumerate(sums)]
    in_specs += [pl.BlockSpec((3, rows[i], g.shape[2]), lambda r, bc: (0, r, 0)) for i, g in enumerate(sums)]
    in_specs += [pl.BlockSpec(s.shape, lambda r, bc: (0, 0, 0)) for s in smalls_slots]
    in_specs += [pl.BlockSpec(s.shape[1:], lambda r, bc: (0, 0)) for s in smalls_slots]
    out_shape = [jax.ShapeDtypeStruct((2 * g.shape[1], g.shape[2]), F32) for g in sums]
    out_shape += [jax.ShapeDtypeStruct((2 * s.shape[1], s.shape[2]), F32) for s in smalls_slots]
    scratch = [pltpu.VMEM((2, rows[i], g.shape[2]), F32) for i, g in enumerate(sums)]
    scratch += [pltpu.VMEM(s.shape[1:], F32) for s in smalls_slots]
    scratch += [pltpu.SemaphoreType.DMA((n, 2)), pltpu.SemaphoreType.DMA((n, 2)), pltpu.SemaphoreType.DMA((n, steps))]
    return list(pl.pallas_call(
        body, out_shape=out_shape,
        grid_spec=pltpu.PrefetchScalarGridSpec(num_scalar_prefetch=1, grid=(steps,), in_specs=in_specs,
                                               out_specs=[ANY] * n, scratch_shapes=scratch),
        compiler_params=_cp(("arbitrary",), VMEM_LIMIT), name=name)(bc_idx, *sums, *recvd, *smalls_slots, *smalls_own))


def _adamw_math(w, g, m, v):
    m2 = ADAM_B1 * m + (1.0 - ADAM_B1) * g
    v2 = ADAM_B2 * v + (1.0 - ADAM_B2) * (g * g)
    m_hat = m2 / (1.0 - ADAM_B1 ** ADAM_STEP)
    v_hat = v2 / (1.0 - ADAM_B2 ** ADAM_STEP)
    delta = -ADAM_LR * (m_hat / (jnp.sqrt(v_hat) + ADAM_EPS) + ADAM_WD * w)
    return delta, m2, v2


def _adamw_big(ws, gs, ms, vs, name, steps=8):
    n = len(ws)

    def body(*refs):
        for i in range(n):
            w_ref, g_ref, m_ref, v_ref = (refs[k * n + i] for k in range(4))
            d_ref, m2_ref, v2_ref, g2_ref = (refs[(4 + k) * n + i] for k in range(4))
            gv = g_ref[...]
            d_ref[...], m2_ref[...], v2_ref[...] = _adamw_math(w_ref[...], gv, m_ref[...], v_ref[...])
            g2_ref[...] = gv

    specs = [pl.BlockSpec((w.shape[0] // steps, w.shape[1]), lambda i: (i, 0)) for w in ws]
    shapes = [jax.ShapeDtypeStruct(w.shape, F32) for w in ws]
    outs = pl.pallas_call(
        body, grid=(steps,), in_specs=specs * 4, out_specs=specs * 4, out_shape=shapes * 4,
        compiler_params=_cp(("parallel",), VMEM_LIMIT), name=name)(*ws, *gs, *ms, *vs)
    return [tuple(outs[k * n + i] for k in range(4)) for i in range(n)]


def _adamw_small(b_idx, sv, sw, vecs, conv, ws):
    nv = len(vecs)
    cols = conv[0].shape[1]

    def body(b_ref, sv_ref, sw_ref, *refs):
        ins, outs = refs[:3 * nv + 6], refs[3 * nv + 6:]
        for i in range(nv):
            g = sv_ref[i:i + 1, :]
            w_ref, m_ref, v_ref = ins[3 * i:3 * i + 3]
            d_ref, m2_ref, v2_ref, g_ref = outs[4 * i:4 * i + 4]
            d_ref[...], m2_ref[...], v2_ref[...] = _adamw_math(w_ref[...], g, m_ref[...], v_ref[...])
            g_ref[...] = g
        g = sv_ref[8:8 + conv[0].shape[0], pl.ds(pl.multiple_of(b_ref[0] * cols, cols), cols)]
        w_ref, m_ref, v_ref = ins[3 * nv:3 * nv + 3]
        d_ref, m2_ref, v2_ref, g_ref = outs[4 * nv:4 * nv + 4]
        d_ref[...], m2_ref[...], v2_ref[...] = _adamw_math(w_ref[...], g, m_ref[...], v_ref[...])
        g_ref[...] = g
        w_ref, m_ref, v_ref = ins[3 * nv + 3:]
        d_ref, m2_ref, v2_ref = outs[4 * nv + 4:]
        d_ref[...], m2_ref[...], v2_ref[...] = _adamw_math(w_ref[...], sw_ref[...], m_ref[...], v_ref[...])

    flat = [a for grp in vecs for a in grp] + list(conv) + list(ws)
    out_shape = [jax.ShapeDtypeStruct(grp[0].shape, F32) for grp in list(vecs) + [conv] for _ in range(4)]
    out_shape += [jax.ShapeDtypeStruct(ws[0].shape, F32)] * 3
    vmem = pl.BlockSpec(memory_space=pltpu.VMEM)
    outs = pl.pallas_call(
        body, out_shape=out_shape, in_specs=[pl.BlockSpec(memory_space=pltpu.SMEM)] + [vmem] * (2 + len(flat)),
        out_specs=[vmem] * len(out_shape), name="adamw_small")(b_idx, sv, sw, *flat)
    return [tuple(outs[4 * i:4 * i + 4]) for i in range(nv + 1)] + [tuple(outs[4 * nv + 4:])]


def kernel(x, mem, norm_mix_g, w_in, conv_w, gm_ln_g, gm_ln_b, gm_ws, gm_bs, w_out, norm_x_g, norm_mem_g, w_q, w_kv, w_xo, norm_final_g, loss_target, m_norm_mix_g, m_w_in, m_conv_w, m_gm_ln_g, m_gm_ln_b, m_gm_ws, m_gm_bs, m_w_out, m_norm_x_g, m_norm_mem_g, m_w_q, m_w_kv, m_w_xo, m_norm_final_g, v_norm_mix_g, v_w_in, v_conv_w, v_gm_ln_g, v_gm_ln_b, v_gm_ws, v_gm_bs, v_w_out, v_norm_x_g, v_norm_mem_g, v_w_q, v_w_kv, v_w_xo, v_norm_final_g):
    t = x.shape[1]
    xi = lax.axis_index("x")
    yi = lax.axis_index("y")
    ci = lax.axis_index("c")
    b_idx = jnp.reshape(2 * xi + yi, (1,)).astype(jnp.int32)
    c_idx = jnp.reshape(ci, (1,)).astype(jnp.int32)

    x2d, mem2d, tgt = x[0], mem[0], loss_target[0]
    big = [w_in[0], w_out[0], w_q[0], w_kv[0], w_xo[0]]
    big_m = [m_w_in[0], m_w_out[0], m_w_q[0], m_w_kv[0], m_w_xo[0]]
    big_v = [v_w_in[0], v_w_out[0], v_w_q[0], v_w_kv[0], v_w_xo[0]]
    g3 = norm_final_g.reshape(1, D)

    def pad8(a):
        return jnp.pad(a, ((0, 8 - a.shape[0]), (0, 0)))

    own_blocks = _cast_shards(b_idx, big)

    tril = jnp.tril(jnp.ones((CH, CH), bool))
    wc32 = jnp.where(tril[None], gm_ws[0], 0.0)
    wc = wc32.astype(BF16)
    wct = jnp.swapaxes(wc32, 1, 2).astype(BF16)
    bsb = jnp.broadcast_to(gm_bs[0][:, :, None], (HEADS, CH, CH))

    proj, hb, win_f, cw8, (wq_f,) = _proj_gather(
        b_idx, x2d, norm_mix_g, own_blocks[0], pad8(conv_w[0]), [own_blocks[2]])
    mixin, (wout_f, wkv_f, wxo_f) = _mixer_fwd(
        proj, cw8, gm_ln_g, gm_ln_b, wc, bsb, [own_blocks[1], own_blocks[3], own_blocks[4]])
    wout2, wq2, wxo2 = wout_f.reshape(MIX, D), wq_f.reshape(D, D), wxo_f.reshape(D, D)
    k, v = _mem_fwd(mem2d, norm_mem_g, wkv_f)

    (loss_row, dmix, dx1b, h2b, dq, ob, dx2b, dk, dv, dg2, dg3) = _tail(
        x2d, tgt, mixin, wout2, wq2, wxo2, k, v, norm_x_g, g3)
    dwkv, dwkv_b, dgm = _mem_bwd(mem2d, norm_mem_g, dk, dv, wkv_f)
    dproj, dcw, dlng, dlnb, dwc, dbs, grad_x, dg1 = _mixer_bwd(
        proj, dmix, cw8, gm_ln_g, gm_ln_b, wc, wct, bsb, win_f, x2d, dx1b, norm_mix_g)

    bc_idx = jnp.concatenate([b_idx, c_idx])
    sv = jnp.concatenate([dg1, dg2, dgm, dg3, dlng, dlnb, dbs, loss_row, dcw], axis=0)
    sw = dwc.reshape(HEADS * CH, CH)
    dwin_sum, dwin_sum_b, psmall = _grad_matmul_pair(c_idx, hb, dproj, [sv, sw], dgm, name="grad_w_in")
    sums_b = [dwin_sum]
    send_b, recv_b, src_b, land_b, token_b = _exchange_begin([dwin_sum_b], psmall, "exchange_b_begin")

    dwxo, dwxo_b = _grad_matmul(ob, dx2b, token_b, by_cols=False, name="grad_w_xo", tk=2048)
    dwq, dwq_b = _grad_matmul(h2b, dq, token_b, by_cols=False, name="grad_w_q", tk=2048)
    dwout, dwout_b = _grad_matmul(mixin, dx1b, token_b, by_cols=False, name="grad_w_out")
    ps_a = _pair_reduce(c_idx, [dwout, dwkv, dwq, dwxo], [dwout_b, dwkv_b, dwq_b, dwxo_b], [], "pair_reduce_a")
    sums_a, sums_a_b = list(ps_a[:4]), list(ps_a[4:8])
    send_a, recv_a, src_a, land_a, token_a = _exchange_begin(sums_a_b, [], "exchange_a_begin")

    src_b, rx2b = _exchange_end(send_b, recv_b, src_b, land_b, 1, [0, 1, 2], [token_a], "exchange_b_end")
    gwin, svf, swf = _chip_reduce(bc_idx, sums_b, rx2b[:1], rx2b[1:], src_b[1:], "chip_reduce_b")
    out_b = _adamw_big(big[:1], [gwin], big_m[:1], big_v[:1], "adamw_w_in")[0]

    loss = svf[7, 0]
    row = lambda a: a.reshape(1, D)
    mat = lambda a: a.reshape(HEADS * CH, CH)
    small = _adamw_small(
        b_idx, svf, swf,
        [(row(norm_mix_g), row(m_norm_mix_g), row(v_norm_mix_g)), (row(norm_x_g), row(m_norm_x_g), row(v_norm_x_g)),
         (row(norm_mem_g), row(m_norm_mem_g), row(v_norm_mem_g)), (row(norm_final_g), row(m_norm_final_g), row(v_norm_final_g)),
         (row(gm_ln_g), row(m_gm_ln_g), row(v_gm_ln_g)), (row(gm_ln_b), row(m_gm_ln_b), row(v_gm_ln_b)),
         (row(gm_bs), row(m_gm_bs), row(v_gm_bs))],
        (conv_w[0], m_conv_w[0], v_conv_w[0]), (mat(gm_ws), mat(m_gm_ws), mat(v_gm_ws)))
    small_ws = small[8] + (swf,)

    def finish_a(part, src, land, after, tag):
        src, land = _exchange_end(send_a, recv_a, src, land, 4, part, after, "exchange_a%s_end" % tag)
        grads = _chip_reduce(bc_idx, [sums_a[i] for i in part], [land[i] for i in part], [], [], "chip_reduce_a" + tag,
                             steps=2)
        ids = [(1, 3, 2, 4)[i] for i in part]
        outs = _adamw_big([big[i] for i in ids], grads, [big_m[i] for i in ids], [big_v[i] for i in ids], "adamw_a" + tag,
                          steps=4)
        return src, land, outs

    src_a, land_a, (out_wout, out_wkv) = finish_a([0, 1], src_a, land_a, [out_b[0], small[0][0]], "1")
    _, _, (out_wq, out_wxo) = finish_a([2, 3], src_a, land_a, [out_wout[0]], "2")

    def unpack(k):
        vec = lambda i: small[i][k]
        return [vec(0), out_b[k][None], small[7][k][None], vec(4), vec(5), small_ws[k].reshape(1, HEADS, CH, CH),
                vec(6).reshape(1, HEADS, CH), out_wout[k][None], vec(1), vec(2), out_wq[k][None], out_wkv[k][None],
                out_wxo[k][None], vec(3).reshape(D)]

    return (loss, grad_x[None], *unpack(3), *unpack(0), *unpack(1), *unpack(2))
```

```python
import functools
import math

import jax
import jax.numpy as jnp
from jax import lax
from jax.experimental import pallas as pl
from jax.experimental.pallas import tpu as pltpu

F32 = jnp.float32
BF16 = jnp.bfloat16
MESH = pl.DeviceIdType.MESH

D = 1024
SLAB = 1024
N_SLAB = 7
IN_DIM = N_SLAB * SLAB
MIX = 2 * SLAB
HEADS = 8
CH = 128
XH = 4
XD = D // XH
EPS = 1e-6
GELU_C = math.sqrt(2.0 / math.pi)
GELU_A = 0.044715
N_CHIP = 4
IN_BLK = IN_DIM // N_CHIP
IN_PIECE = 256
N_PIECE = IN_BLK // IN_PIECE
KV_BLK = 2 * D // N_CHIP

ADAM_LR, ADAM_B1, ADAM_B2, ADAM_EPS, ADAM_WD, ADAM_STEP = 0.001, 0.9, 0.999, 1e-08, 0.01, 10

VMEM_LIMIT = 60 * 1024 * 1024


def _cp(sem=None, vmem=None):
    return pltpu.CompilerParams(dimension_semantics=sem, vmem_limit_bytes=vmem)


def _full(shape, buffers=None):
    n = len(shape)
    if buffers is None:
        return pl.BlockSpec(shape, lambda *_: (0,) * n)
    return pl.BlockSpec(shape, lambda *_: (0,) * n, pipeline_mode=pl.Buffered(buffers))


ANY = pl.BlockSpec(memory_space=pl.ANY)


def _bdot(a, b):
    return jnp.dot(a.astype(BF16), b.astype(BF16), preferred_element_type=F32)


def _bdot_nt(a, b):
    return lax.dot_general(a.astype(BF16), b.astype(BF16), (((1,), (1,)), ((), ())), preferred_element_type=F32)


def _bdot_tn(a, b):
    return lax.dot_general(a.astype(BF16), b.astype(BF16), (((0,), (0,)), ((), ())), preferred_element_type=F32)


def _rms(x, g):
    r = lax.rsqrt(jnp.mean(x * x, axis=-1, keepdims=True) + EPS)
    return x * r * g, r


def _rms_bwd(dy, x, r, g):
    gdy = dy * g
    dx = r * gdy - x * (r * r * r) * jnp.mean(x * gdy, axis=-1, keepdims=True)
    dg = jnp.sum(dy * x * r, axis=0, keepdims=True)
    return dx, dg


def _gelu_parts(x):
    x2 = x * x
    t = jnp.tanh(GELU_C * (x + GELU_A * x * x2))
    val = 0.5 * x * (1.0 + t)
    grad = 0.5 * (1.0 + t) + 0.5 * x * (1.0 - t * t) * (GELU_C * (1.0 + 3.0 * GELU_A * x2))
    return val, grad


def _gelu(x):
    return 0.5 * x * (1.0 + jnp.tanh(GELU_C * (x + GELU_A * x * x * x)))


def _sigmoid(z):
    return 1.0 / (1.0 + jnp.exp(-z))


def _cast_shards(b_idx, arrs):
    n = len(arrs)
    steps = 8

    def body(b_ref, *refs):
        for p in range(N_PIECE):
            refs[n][p] = refs[0][:, pl.ds(p * IN_PIECE, IN_PIECE)].astype(BF16)
        for i in range(1, n):
            refs[n + i][...] = refs[i][...].astype(BF16)

    rows = [a.shape[0] // steps for a in arrs]
    in_specs = [pl.BlockSpec((rows[i], a.shape[1]), lambda i, b: (i, 0)) for i, a in enumerate(arrs)]
    out_specs = [pl.BlockSpec((None, N_PIECE, rows[0], IN_PIECE), lambda i, b: (b[0], 0, i, 0))]
    out_specs += [pl.BlockSpec((None, rows[i], a.shape[1]), lambda i, b: (b[0], i, 0)) for i, a in enumerate(arrs) if i > 0]
    out_shape = [jax.ShapeDtypeStruct((N_CHIP, N_PIECE, arrs[0].shape[0], IN_PIECE), BF16)]
    out_shape += [jax.ShapeDtypeStruct((N_CHIP,) + a.shape, BF16) for a in arrs[1:]]
    return pl.pallas_call(
        body, out_shape=out_shape,
        grid_spec=pltpu.PrefetchScalarGridSpec(num_scalar_prefetch=1, grid=(steps,), in_specs=in_specs, out_specs=out_specs),
        compiler_params=_cp(("arbitrary",)), name="cast_shards")(b_idx, *arrs)


def _proj_gather(b_idx, x, g, win_own, cw8s, more, tm=1024):
    t = x.shape[0]
    ni = t // tm
    nm = len(more)
    steps = N_CHIP * N_PIECE
    near0, far0 = N_PIECE, 3 * N_PIECE

    def piece_at(step, own):
        k = step - near0
        near, far = (step >= near0) & (step < far0), step >= far0
        block = jnp.where(far, own ^ 3, jnp.where(near, own ^ jnp.where(lax.rem(k, 2) == 0, 2, 1), own))
        return block, jnp.where(far, step - far0, jnp.where(near, lax.div(k, 2), step))

    def body(*refs):
        b_ref, x_any, g_ref, win_in, cw_in = refs[:5]
        o_ref, hb_any, win_f, cw_out = refs[5 + nm:9 + nm]
        more_out = refs[9 + nm:9 + 2 * nm]
        hbuf, xbuf, wv, cw_s, cw_r, loc = refs[9 + 2 * nm:15 + 2 * nm]
        g_in = _Gather([win_f.at[:, p] for p in range(N_PIECE)], *refs[15 + 2 * nm:19 + 2 * nm])
        g_more = _Gather(more_out, *refs[19 + 2 * nm:23 + 2 * nm])
        s = pl.program_id(0)
        x, y, c, chips = _coords()
        b = 2 * x + y
        blks = [2 * chip[0] + chip[1] for chip in chips]

        def cw_cols(blk):
            return cw_out.at[:, pl.ds(blk * (D // N_CHIP), D // N_CHIP)]

        def cw_copy(k, blk):
            src = cw_in if blk is None else cw_cols(blk)
            return pltpu.make_async_remote_copy(src_ref=src, dst_ref=cw_cols(b if blk is None else blk), send_sem=cw_s.at[k],
                                                recv_sem=cw_r.at[k], device_id=(*chips[k], c), device_id_type=MESH)

        cw_local = pltpu.make_async_copy(cw_in, cw_cols(b), loc.at[1])
        hb_copy = pltpu.make_async_copy(hbuf, hb_any, loc.at[0])

        def load(step):
            slot = lax.rem(step, 2)
            block, piece = piece_at(step, b_ref[0])
            return pltpu.make_async_copy(win_f.at[block, piece], wv.at[slot], loc.at[2 + slot])

        def chunk(i):
            return pltpu.make_async_copy(x_any.at[pl.ds(i * tm, tm)], xbuf.at[i % 2], loc.at[4 + i % 2])

        def first():
            g_in.start()
            cw_local.start()
            for k in range(3):
                cw_copy(k, None).start()
            load(0).start()
            chunk(0).start()
            for i in range(ni):
                if i + 1 < ni:
                    chunk(i + 1).start()
                chunk(i).wait()
                h, _ = _rms(xbuf[i % 2], g_ref[...])
                hbuf[pl.ds(i * tm, tm), :] = h.astype(BF16)
            hb_copy.start()

        events = {step: [] for step in range(steps)}
        events[0].append(first)
        for p in range(N_PIECE):
            events[2 * p + 2].append(functools.partial(g_in.hop, [p]))
            events[near0 + 2 * p - 1].append(functools.partial(g_in.near_ready, [p]))
            events[far0 + p - 2].append(functools.partial(g_in.far, [p]))
            events[far0 + p - 1].append(functools.partial(g_in.far_ready, [p]))
        events[2 * N_PIECE + 1].append(g_more.start)
        for step, todo in events.items():
            if todo:
                @pl.when(s == step)
                def _(todo=todo):
                    for do in todo:
                        do()

        @pl.when(s + 1 < steps)
        def _():
            load(s + 1).start()

        load(s).wait()
        for i in range(ni):
            rows = pl.ds(i * tm, tm)
            o_ref[rows, :] = jnp.dot(hbuf[rows, :], wv[lax.rem(s, 2)], preferred_element_type=F32).astype(BF16)

        @pl.when(s == steps - 1)
        def _():
            g_more.hop()
            g_more.far()
            for k in range(3):
                cw_copy(k, blks[k]).wait_recv()
            for k in range(3):
                cw_copy(k, None).wait_send()
            cw_local.wait()
            hb_copy.wait()
            g_more.near_ready()
            g_more.far_ready()
            g_in.drain()
            g_more.drain()

    def out_col(s, b):
        block, piece = piece_at(s, b[0])
        return 0, block * N_PIECE + piece

    in_specs = [ANY, pl.BlockSpec((1, D), lambda s, b: (0, 0)), ANY, ANY] + [ANY] * nm
    out_specs = [pl.BlockSpec((t, IN_PIECE), out_col), ANY, ANY, ANY] + [ANY] * nm
    outs = pl.pallas_call(
        body, out_shape=[jax.ShapeDtypeStruct((t, IN_DIM), BF16), jax.ShapeDtypeStruct((t, D), BF16),
                         jax.ShapeDtypeStruct(win_own.shape, BF16), jax.ShapeDtypeStruct((8, D), F32)]
        + [jax.ShapeDtypeStruct(f.shape, f.dtype) for f in more],
        grid_spec=pltpu.PrefetchScalarGridSpec(
            num_scalar_prefetch=1, grid=(steps,), in_specs=in_specs, out_specs=out_specs,
            scratch_shapes=[pltpu.VMEM((t, D), BF16), pltpu.VMEM((2, tm, D), F32), pltpu.VMEM((2, D, IN_PIECE), BF16)]
            + [pltpu.SemaphoreType.DMA((3,))] * 2 + [pltpu.SemaphoreType.DMA((6,))]
            + _gather_sems(N_PIECE) + _gather_sems(nm)),
        input_output_aliases={3: 2, **{5 + w: 4 + w for w in range(nm)}},
        compiler_params=_cp(("arbitrary",), VMEM_LIMIT), name="proj_gather")(b_idx, x, g, win_own, cw8s, *more)
    return outs[0], outs[1], outs[2], outs[3], outs[4:]


class _Gather:
    def __init__(self, outs, ici_s, ici_r, d2d_s, d2d_r):
        x, y, c, _ = _coords()
        self.outs, self.c = outs, c
        self.sems = ici_s, ici_r, d2d_s, d2d_r
        self.b, self.bx, self.by, self.bd = 2 * x + y, 2 * (1 - x) + y, 2 * x + (1 - y), 2 * (1 - x) + (1 - y)
        self.xn, self.yn, self.sib = (1 - x, y, c), (x, 1 - y, c), (x, y, 1 - c)

    def piece(self, w, blk, hc, quarter=None):
        hr = self.outs[w].shape[1] // 2
        if quarter is None:
            return self.outs[w].at[blk, pl.ds(hc * hr, hr)]
        return self.outs[w].at[blk, pl.ds(hc * hr + quarter * (hr // 2), hr // 2)]

    def ici(self, w, k, ref, to):
        return pltpu.make_async_remote_copy(src_ref=ref, dst_ref=ref, send_sem=self.sems[0].at[w, k],
                                            recv_sem=self.sems[1].at[w, k], device_id=to, device_id_type=MESH)

    def d2d(self, w, k, ref):
        return pltpu.make_async_remote_copy(src_ref=ref, dst_ref=ref, send_sem=self.sems[2].at[w, k],
                                            recv_sem=self.sems[3].at[w, k], device_id=self.sib, device_id_type=MESH)

    def all(self):
        return range(len(self.outs))

    def start(self):
        for w in self.all():
            mine = self.piece(w, self.b, self.c)
            self.ici(w, 0, mine, self.xn).start()
            self.ici(w, 1, mine, self.yn).start()

    def hop(self, ws=None):
        c = self.c
        for w in ws or self.all():
            self.ici(w, 0, self.piece(w, self.bx, c), self.xn).wait_recv()
            self.ici(w, 1, self.piece(w, self.by, c), self.yn).wait_recv()
            self.ici(w, 2, self.piece(w, self.bx, c, 0), self.yn).start()
            self.ici(w, 3, self.piece(w, self.by, c, 1), self.xn).start()
            self.d2d(w, 0, self.piece(w, self.bx, c)).start()
            self.d2d(w, 1, self.piece(w, self.by, c)).start()

    def near_ready(self, ws=None):
        for w in ws or self.all():
            self.d2d(w, 0, self.piece(w, self.bx, 1 - self.c)).wait_recv()
            self.d2d(w, 1, self.piece(w, self.by, 1 - self.c)).wait_recv()

    def far(self, ws=None):
        c = self.c
        for w in ws or self.all():
            self.ici(w, 2, self.piece(w, self.bd, c, 0), self.yn).wait_recv()
            self.ici(w, 3, self.piece(w, self.bd, c, 1), self.xn).wait_recv()
            self.d2d(w, 2, self.piece(w, self.bd, c, 0)).start()
            self.d2d(w, 3, self.piece(w, self.bd, c, 1)).start()

    def far_ready(self, ws=None):
        for w in ws or self.all():
            self.d2d(w, 2, self.piece(w, self.bd, 1 - self.c, 0)).wait_recv()
            self.d2d(w, 3, self.piece(w, self.bd, 1 - self.c, 1)).wait_recv()

    def drain(self):
        c = self.c
        for w in self.all():
            mine = self.piece(w, self.b, c)
            self.ici(w, 0, mine, self.xn).wait_send()
            self.ici(w, 1, mine, self.yn).wait_send()
            self.ici(w, 2, self.piece(w, self.bx, c, 0), self.yn).wait_send()
            self.ici(w, 3, self.piece(w, self.by, c, 1), self.xn).wait_send()
            self.d2d(w, 0, self.piece(w, self.bx, c)).wait_send()
            self.d2d(w, 1, self.piece(w, self.by, c)).wait_send()
            self.d2d(w, 2, self.piece(w, self.bd, c, 0)).wait_send()
            self.d2d(w, 3, self.piece(w, self.bd, c, 1)).wait_send()


def _gather_sems(nw):
    return [pltpu.SemaphoreType.DMA((max(nw, 1), 4))] * 4


def _mixer_fwd(proj, cw8, lng, lnb, wc, bsb, fulls, tm=256):
    t = proj.shape[0]
    nt = t // tm
    nch = tm // CH
    nw = len(fulls)

    def body(*refs):
        p_ref, cw_ref, lng_ref, lnb_ref, wc_ref, bsb_ref = refs[:6]
        mix_ref = refs[6 + nw]
        w_outs = refs[7 + nw:7 + 2 * nw]
        prev_ref = refs[7 + 2 * nw]
        gather = _Gather(w_outs, *refs[8 + 2 * nw:])

        @pl.when(pl.program_id(0) == 0)
        def _():
            gather.start()
            prev_ref[...] = jnp.zeros_like(prev_ref)

        @pl.when(pl.program_id(0) == nt // 2)
        def _():
            gather.hop()

        @pl.when(pl.program_id(0) == nt - 1)
        def _():
            gather.far()

        rows = lax.broadcasted_iota(jnp.int32, (tm, CH), 0)
        for s in range(HEADS):
            cs = pl.ds(CH * s, CH)

            def slab(k):
                return p_ref[:, pl.ds(k * SLAB + CH * s, CH)].astype(F32)

            gb, gc, xa, za = slab(0), slab(1), slab(2), slab(3)
            cx = gc * xa
            p6 = jnp.broadcast_to(prev_ref[6:7, cs], (tm, CH))
            p7 = jnp.broadcast_to(prev_ref[7:8, cs], (tm, CH))
            c1 = jnp.where(rows == 0, p7, pltpu.roll(cx, 1, 0))
            c2 = jnp.where(rows == 0, p6, jnp.where(rows == 1, p7, pltpu.roll(cx, 2, 0)))
            prev_ref[:, cs] = cx[tm - 8:, :]
            cv = cw_ref[0:1, cs] * c2 + cw_ref[1:2, cs] * c1 + cw_ref[2:3, cs] * cx
            mix_ref[:, cs] = (gb * cv * (za * _sigmoid(za))).astype(BF16)

            u, v, zb = slab(4), slab(5), slab(6)
            ug, vg = _gelu(u), _gelu(v)
            dlt = vg - jnp.mean(vg, axis=-1, keepdims=True)
            vhat = dlt * lax.rsqrt(jnp.mean(dlt * dlt, axis=-1, keepdims=True) + EPS)
            vn = (vhat * lng_ref[:, cs] + lnb_ref[:, cs]).astype(BF16)
            gate = ug * (zb * _sigmoid(zb))
            for c in range(nch):
                rs = slice(CH * c, CH * (c + 1))
                sp = jnp.dot(wc_ref[s], vn[rs], preferred_element_type=F32) + bsb_ref[s]
                mix_ref[rs, pl.ds(SLAB + CH * s, CH)] = (gate[rs] * sp).astype(BF16)

        @pl.when(pl.program_id(0) == nt - 1)
        def _():
            gather.near_ready()
            gather.far_ready()
            gather.drain()

    sems = _gather_sems(nw)
    outs = pl.pallas_call(
        body, grid=(nt,),
        in_specs=[pl.BlockSpec((tm, IN_DIM), lambda i: (i, 0)), _full((8, D)), _full((1, D)), _full((1, D)),
                  _full((HEADS, CH, CH)), _full((HEADS, CH, CH))] + [ANY] * nw,
        out_specs=[pl.BlockSpec((tm, MIX), lambda i: (i, 0))] + [ANY] * nw,
        out_shape=[jax.ShapeDtypeStruct((t, MIX), BF16)] + [jax.ShapeDtypeStruct(f.shape, f.dtype) for f in fulls],
        input_output_aliases={6 + w: 1 + w for w in range(nw)},
        scratch_shapes=[pltpu.VMEM((8, D), F32)] + sems,
        compiler_params=_cp(("arbitrary",), VMEM_LIMIT), name="mixer_fwd")(proj, cw8, lng, lnb, wc, bsb, *fulls)
    return outs[0], outs[1:]


def _mem_fwd(mem, gm, wkv_f):
    n_mem = mem.shape[0]

    def body(mem_ref, gm_ref, w_ref, k_ref, v_ref):
        m, _ = _rms(mem_ref[...], gm_ref[...])
        mb = m.astype(BF16)
        for j in range(N_CHIP):
            dst = k_ref if j < 2 else v_ref
            dst[:, pl.ds(KV_BLK * (j % 2), KV_BLK)] = jnp.dot(mb, w_ref[j], preferred_element_type=F32).astype(BF16)

    return pl.pallas_call(
        body, out_shape=[jax.ShapeDtypeStruct((n_mem, D), BF16), jax.ShapeDtypeStruct((n_mem, D), BF16)],
        compiler_params=_cp(None, VMEM_LIMIT), name="mem_fwd")(mem, gm, wkv_f)


def _tail(x, tgt, mixin, wout, wq, wxo, k, v, g2, g3, tm=512, sub=512):
    t = x.shape[0]
    n_mem = k.shape[0]
    scale = 1.0 / math.sqrt(XD)

    def body(x_ref, tgt_ref, mix_ref, wout_ref, wq_ref, wxo_ref, k_ref, v_ref, g2_ref, g3_ref,
             loss_ref, dmix_ref, dx1b_ref, h2_ref, dq_ref, o_ref, dx2b_ref, dk_ref, dv_ref, dg2_ref, dg3_ref):
        @pl.when(pl.program_id(0) == 0)
        def _():
            loss_ref[...] = jnp.zeros_like(loss_ref)
            dk_ref[...] = jnp.zeros_like(dk_ref)
            dv_ref[...] = jnp.zeros_like(dv_ref)
            dg2_ref[...] = jnp.zeros_like(dg2_ref)
            dg3_ref[...] = jnp.zeros_like(dg3_ref)

        g2, g3 = g2_ref[...], g3_ref[...]
        for sb in range(tm // sub):
            rs = pl.ds(sub * sb, sub)
            x1 = x_ref[rs, :] + jnp.dot(mix_ref[rs, :], wout_ref[...], preferred_element_type=F32)
            h2, r2 = _rms(x1, g2)
            h2b = h2.astype(BF16)
            h2_ref[rs, :] = h2b
            q = jnp.dot(h2b, wq_ref[...], preferred_element_type=F32).astype(BF16)
            probs, outs = [], []
            for hd in range(XH):
                hs = pl.ds(XD * hd, XD)
                s = _bdot_nt(q[:, XD * hd:XD * (hd + 1)], k_ref[:, hs]) * scale
                e = jnp.exp(s - jnp.max(s, axis=-1, keepdims=True))
                p = e / jnp.sum(e, axis=-1, keepdims=True)
                probs.append(p)
                outs.append(_bdot(p, v_ref[:, hs]))
            ob = jnp.concatenate(outs, axis=-1).astype(BF16)
            o_ref[rs, :] = ob
            x2 = x1 + jnp.dot(ob, wxo_ref[...], preferred_element_type=F32)
            y, r3 = _rms(x2, g3)
            diff = y - tgt_ref[rs, :]
            row_loss = jnp.sum(diff * diff, axis=-1, keepdims=True)
            loss_ref[...] += jnp.broadcast_to(jnp.sum(row_loss, axis=0, keepdims=True) * (0.5 / D), loss_ref.shape)

            dx2, dg3 = _rms_bwd(diff * (1.0 / D), x2, r3, g3)
            dg3_ref[...] += dg3
            dx2b = dx2.astype(BF16)
            dx2b_ref[rs, :] = dx2b
            do = _bdot_nt(dx2b, wxo_ref[...])
            dqs = []
            for hd in range(XH):
                hs = pl.ds(XD * hd, XD)
                p = probs[hd]
                do_h = do[:, XD * hd:XD * (hd + 1)]
                dv_ref[:, hs] += _bdot_tn(p, do_h)
                dp = _bdot_nt(do_h, v_ref[:, hs])
                ds = p * (dp - jnp.sum(dp * p, axis=-1, keepdims=True))
                dqs.append(_bdot(ds, k_ref[:, hs]) * scale)
                dk_ref[:, hs] += _bdot_tn(ds, q[:, XD * hd:XD * (hd + 1)]) * scale
            dq = jnp.concatenate(dqs, axis=-1).astype(BF16)
            dq_ref[rs, :] = dq
            dx1n, dg2 = _rms_bwd(_bdot_nt(dq, wq_ref[...]), x1, r2, g2)
            dg2_ref[...] += dg2
            dx1b = (dx2 + dx1n).astype(BF16)
            dx1b_ref[rs, :] = dx1b
            dmix_ref[rs, :] = _bdot_nt(dx1b, wout_ref[...]).astype(BF16)

    tok = lambda w: pl.BlockSpec((tm, w), lambda i: (i, 0))
    return pl.pallas_call(
        body, grid=(t // tm,),
        in_specs=[tok(D), tok(D), tok(MIX), _full((MIX, D), 1), _full((D, D), 1), _full((D, D), 1),
                  _full((n_mem, D), 1), _full((n_mem, D), 1), _full((1, D)), _full((1, D))],
        out_specs=[_full((1, D)), tok(MIX), tok(D), tok(D), tok(D), tok(D), tok(D),
                   _full((n_mem, D)), _full((n_mem, D)), _full((1, D)), _full((1, D))],
        out_shape=[jax.ShapeDtypeStruct((1, D), F32), jax.ShapeDtypeStruct((t, MIX), BF16),
                   jax.ShapeDtypeStruct((t, D), BF16),
                   jax.ShapeDtypeStruct((t, D), BF16), jax.ShapeDtypeStruct((t, D), BF16),
                   jax.ShapeDtypeStruct((t, D), BF16), jax.ShapeDtypeStruct((t, D), BF16),
                   jax.ShapeDtypeStruct((n_mem, D), F32), jax.ShapeDtypeStruct((n_mem, D), F32),
                   jax.ShapeDtypeStruct((1, D), F32), jax.ShapeDtypeStruct((1, D), F32)],
        compiler_params=_cp(("arbitrary",), VMEM_LIMIT), name="tail")(x, tgt, mixin, wout, wq, wxo, k, v, g2, g3)


def _mem_bwd(mem, gm, dk, dv, wkv_f):
    def body(mem_ref, gm_ref, dk_ref, dv_ref, w_ref, dw_ref, dwb_ref, dgm_ref):
        mem_v = mem_ref[...]
        m, rm = _rms(mem_v, gm_ref[...])
        mb = m.astype(BF16)
        dm = jnp.zeros_like(mem_v)
        for j in range(N_CHIP):
            src = dk_ref if j < 2 else dv_ref
            dkv = src[:, pl.ds(KV_BLK * (j % 2), KV_BLK)].astype(BF16)
            dw = _bdot_tn(mb, dkv)
            dw_ref[j] = dw
            dwb_ref[j] = dw.astype(BF16)
            dm = dm + _bdot_nt(dkv, w_ref[j])
        dgm_ref[...] = jnp.sum(dm * mem_v * rm, axis=0, keepdims=True)

    return pl.pallas_call(
        body, out_shape=[jax.ShapeDtypeStruct((N_CHIP, D, KV_BLK), F32), jax.ShapeDtypeStruct((N_CHIP, D, KV_BLK), BF16),
                         jax.ShapeDtypeStruct((1, D), F32)],
        compiler_params=_cp(None, VMEM_LIMIT), name="mem_bwd")(mem, gm, dk, dv, wkv_f)


def _mixer_bwd(proj, dmix, cw8, lng, lnb, wc, wct, bsb, win_f, x, dx1, g1, rows123, row7, tm=256):
    t = proj.shape[0]
    nt = t // tm
    nch = tm // CH
    hb = 16
    pair = 2 * CH
    assert pair == IN_PIECE

    def body(p_ref, pgc_ref, pxa_ref, dm_ref, cw_ref, lng_ref, lnb_ref, wc_ref, wct_ref, bsb_ref, w_ref, x_ref,
             dx1_ref, g1_ref, r1_ref, r2_ref, r3_ref, r7_ref, dp_ref, sv_ref, dwc_ref, gx_ref,
             next_ref, dh_ref):
        i = pl.program_id(0)
        dg1_ref, dlng_ref, dlnb_ref, dbs_ref = (sv_ref.at[pl.ds(r, 1)] for r in (0, 4, 5, 6))
        dcw_ref = sv_ref.at[pl.ds(8, 8)]

        @pl.when(i == 0)
        def _():
            next_ref[...] = jnp.zeros_like(next_ref)
            sv_ref[...] = jnp.zeros_like(sv_ref)
            dwc_ref[...] = jnp.zeros_like(dwc_ref)
            for r, ref in ((1, r1_ref), (2, r2_ref), (3, r3_ref), (7, r7_ref)):
                sv_ref[r:r + 1, :] = ref[...]

        first_tile = i == nt - 1
        rows = lax.broadcasted_iota(jnp.int32, (tm, CH), 0)
        ones8 = jnp.ones((8, CH), BF16)
        for s in range(HEADS):
            cs = pl.ds(CH * s, CH)

            def slab(k):
                return p_ref[:, pl.ds(k * SLAB + CH * s, CH)].astype(F32)

            gb, gc, xa, za = slab(0), slab(1), slab(2), slab(3)
            da = dm_ref[:, cs].astype(F32)
            cx = gc * xa
            cxp = pgc_ref[:, cs].astype(F32) * pxa_ref[:, cs].astype(F32)
            cxp = jnp.where(first_tile, jnp.zeros_like(cxp), cxp)
            p6 = jnp.broadcast_to(cxp[hb - 2:hb - 1, :], (tm, CH))
            p7 = jnp.broadcast_to(cxp[hb - 1:hb, :], (tm, CH))
            c1 = jnp.where(rows == 0, p7, pltpu.roll(cx, 1, 0))
            c2 = jnp.where(rows == 0, p6, jnp.where(rows == 1, p7, pltpu.roll(cx, 2, 0)))
            w0, w1, w2 = cw_ref[0:1, cs], cw_ref[1:2, cs], cw_ref[2:3, cs]
            cv = w0 * c2 + w1 * c1 + w2 * cx
            sg = _sigmoid(za)
            sa = za * sg
            dcv = da * gb * sa
            dp_ref[:, pl.ds(0 * SLAB + CH * s, CH)] = (da * cv * sa).astype(BF16)
            dp_ref[:, pl.ds(3 * SLAB + CH * s, CH)] = (da * gb * cv * (sg * (1.0 + za * (1.0 - sg)))).astype(BF16)
            n0 = jnp.broadcast_to(next_ref[0:1, cs], (tm, CH))
            n1 = jnp.broadcast_to(next_ref[1:2, cs], (tm, CH))
            u1 = jnp.where(rows == tm - 1, n0, pltpu.roll(dcv, tm - 1, 0))
            u2 = jnp.where(rows == tm - 2, n0, jnp.where(rows == tm - 1, n1, pltpu.roll(dcv, tm - 2, 0)))
            next_ref[:, cs] = dcv[0:8, :]
            dcx = w2 * dcv + w1 * u1 + w0 * u2
            dp_ref[:, pl.ds(1 * SLAB + CH * s, CH)] = (dcx * xa).astype(BF16)
            dp_ref[:, pl.ds(2 * SLAB + CH * s, CH)] = (dcx * gc).astype(BF16)
            dcw_ref[0:1, cs] += jnp.sum(dcv * c2, axis=0, keepdims=True)
            dcw_ref[1:2, cs] += jnp.sum(dcv * c1, axis=0, keepdims=True)
            dcw_ref[2:3, cs] += jnp.sum(dcv * cx, axis=0, keepdims=True)

            u, v, zb = slab(4), slab(5), slab(6)
            db = dm_ref[:, pl.ds(SLAB + CH * s, CH)].astype(F32)
            ug, ugrad = _gelu_parts(u)
            vg, vgrad = _gelu_parts(v)
            dlt = vg - jnp.mean(vg, axis=-1, keepdims=True)
            rstd = lax.rsqrt(jnp.mean(dlt * dlt, axis=-1, keepdims=True) + EPS)
            vhat = dlt * rstd
            lg = lng_ref[:, cs]
            vn = (vhat * lg + lnb_ref[:, cs]).astype(BF16)
            sgb = _sigmoid(zb)
            szb = zb * sgb
            sps, dvns = [], []
            dbs = jnp.zeros((8, CH), F32)
            dwc = jnp.zeros((CH, CH), F32)
            for c in range(nch):
                rs = slice(CH * c, CH * (c + 1))
                sp = jnp.dot(wc_ref[s], vn[rs], preferred_element_type=F32) + bsb_ref[s]
                dsp = (db[rs] * ug[rs] * szb[rs]).astype(BF16)
                dbs = dbs + lax.dot_general(ones8, dsp, (((1,), (1,)), ((), ())), preferred_element_type=F32)
                dwc = dwc + lax.dot_general(dsp, vn[rs], (((1,), (1,)), ((), ())), preferred_element_type=F32)
                dvns.append(jnp.dot(wct_ref[s], dsp, preferred_element_type=F32))
                sps.append(sp)
            sp = jnp.concatenate(sps, axis=0)
            dvn = jnp.concatenate(dvns, axis=0)
            dbs_ref[:, cs] += dbs[0:1]
            dwc_ref[s] += dwc
            dlng_ref[:, cs] += jnp.sum(dvn * vhat, axis=0, keepdims=True)
            dlnb_ref[:, cs] += jnp.sum(dvn, axis=0, keepdims=True)
            dvhat = dvn * lg
            dvg = rstd * (dvhat - jnp.mean(dvhat, axis=-1, keepdims=True)
                          - vhat * jnp.mean(dvhat * vhat, axis=-1, keepdims=True))
            dp_ref[:, pl.ds(4 * SLAB + CH * s, CH)] = (db * sp * szb * ugrad).astype(BF16)
            dp_ref[:, pl.ds(5 * SLAB + CH * s, CH)] = (dvg * vgrad).astype(BF16)
            dp_ref[:, pl.ds(6 * SLAB + CH * s, CH)] = (db * ug * sp * (sgb * (1.0 + zb * (1.0 - sgb)))).astype(BF16)

            if s % 2 == 1:
                part = None
                for k in range(N_SLAB):
                    col = k * SLAB + pair * (s // 2)
                    blk, off = divmod(col, IN_BLK)
                    term = lax.dot_general(dp_ref[:, pl.ds(col, pair)], w_ref[blk, off // IN_PIECE],
                                           (((1,), (1,)), ((), ())), preferred_element_type=F32)
                    part = term if part is None else part + term
                if s == 1:
                    dh_ref[...] = part
                else:
                    dh_ref[...] += part

        xv = x_ref[...]
        r = lax.rsqrt(jnp.mean(xv * xv, axis=-1, keepdims=True) + EPS)
        dxn, dg = _rms_bwd(dh_ref[...], xv, r, g1_ref[...])
        gx_ref[...] = dx1_ref[...].astype(F32) + dxn
        dg1_ref[...] += dg

        @pl.when(i == nt - 1)
        def _():
            tril = lax.broadcasted_iota(jnp.int32, (CH, CH), 0) >= lax.broadcasted_iota(jnp.int32, (CH, CH), 1)
            for s in range(HEADS):
                dwc_ref[s] = jnp.where(tril, dwc_ref[s], 0.0)

    rev = lambda i: nt - 1 - i
    halo = lambda col: pl.BlockSpec((hb, SLAB), lambda i: (jnp.maximum(rev(i) * (tm // hb) - 1, 0), col))
    tok = lambda w: pl.BlockSpec((tm, w), lambda i: (rev(i), 0))
    return pl.pallas_call(
        body, grid=(nt,),
        in_specs=[tok(IN_DIM), halo(1), halo(2), tok(MIX), _full((8, D)), _full((1, D)), _full((1, D)),
                  _full((HEADS, CH, CH)), _full((HEADS, CH, CH)), _full((HEADS, CH, CH)),
                  _full((N_CHIP, N_PIECE, D, IN_PIECE), 1), tok(D), tok(D)] + [_full((1, D))] * 5,
        out_specs=[tok(IN_DIM), _full((16, D)), _full((HEADS, CH, CH)), tok(D)],
        out_shape=[jax.ShapeDtypeStruct((t, IN_DIM), BF16), jax.ShapeDtypeStruct((16, D), F32),
                   jax.ShapeDtypeStruct((HEADS, CH, CH), F32), jax.ShapeDtypeStruct((t, D), F32)],
        scratch_shapes=[pltpu.VMEM((8, D), F32), pltpu.VMEM((tm, D), F32)],
        compiler_params=_cp(("arbitrary",), VMEM_LIMIT), name="mixer_bwd")(
            proj, proj, proj, dmix, cw8, lng, lnb, wc, wct, bsb, win_f, x, dx1, g1, *rows123, row7)


def _grad_matmul(a, b, after, *, by_cols, name, tk=1024):
    t, m = a.shape
    n = b.shape[1]
    nk = t // tk
    nj = N_CHIP if by_cols else 1
    bn = n // nj

    def body(a_ref, b_ref, after_ref, o_ref, ob_ref):
        kk = pl.program_id(1)
        part = lax.dot_general(a_ref[...], b_ref[...], (((0,), (0,)), ((), ())), preferred_element_type=F32)

        @pl.when(kk == 0)
        def _():
            o_ref[...] = part

        @pl.when(kk > 0)
        def _():
            o_ref[...] += part

        @pl.when(kk == nk - 1)
        def _():
            ob_ref[...] = o_ref[...].astype(BF16)

    a_spec = pl.BlockSpec((tk, m), lambda j, k: (k, 0))
    b_spec = pl.BlockSpec((tk, bn), lambda j, k: (k, j))
    o_spec = pl.BlockSpec((None, m, bn), lambda j, k: (j, 0, 0))
    o32, o16 = pl.pallas_call(
        body, grid=(nj, nk), in_specs=[a_spec, b_spec, ANY], out_specs=[o_spec, o_spec],
        out_shape=[jax.ShapeDtypeStruct((nj, m, bn), F32), jax.ShapeDtypeStruct((nj, m, bn), BF16)],
        compiler_params=_cp(("parallel", "arbitrary"), VMEM_LIMIT), name=name)(a, b, after)
    if by_cols:
        return o32, o16
    return o32.reshape(N_CHIP, m // N_CHIP, n), o16.reshape(N_CHIP, m // N_CHIP, n)


def _coords():
    x, y, c = lax.axis_index("x"), lax.axis_index("y"), lax.axis_index("c")
    chips = [(1 - x, y), (x, 1 - y), (1 - x, 1 - y)]
    return x, y, c, chips


def _pair_reduce(c_idx, grads, grads_b, smalls, name):
    ng, ns = len(grads), len(smalls)
    halves = [g.shape[1] // 2 for g in grads]

    def body(c_ref, *refs):
        g_in, gb_any = refs[:ng], refs[ng:2 * ng]
        s_own, s_any = refs[2 * ng:2 * ng + ns], refs[2 * ng + ns:2 * ng + 2 * ns]
        o = refs[2 * ng + 2 * ns:4 * ng + 3 * ns]
        lands = refs[4 * ng + 3 * ns:5 * ng + 4 * ns]
        send, recv = refs[5 * ng + 4 * ns:]
        x, y, c, _ = _coords()
        j = pl.program_id(0)

        def big(i, blk):
            return pltpu.make_async_remote_copy(
                src_ref=gb_any[i].at[blk, pl.ds((1 - c) * halves[i], halves[i])], dst_ref=lands[i].at[blk],
                send_sem=send.at[i, blk], recv_sem=recv.at[i, blk], device_id=(x, y, 1 - c), device_id_type=MESH)

        def small(i):
            return pltpu.make_async_remote_copy(
                src_ref=s_any[i].at[1 - c], dst_ref=lands[ng + i],
                send_sem=send.at[ng + i, 0], recv_sem=recv.at[ng + i, 0], device_id=(x, y, 1 - c), device_id_type=MESH)

        @pl.when(j == 0)
        def _():
            for blk in range(N_CHIP):
                for i in range(ng):
                    big(i, blk).start()
            for i in range(ns):
                small(i).start()

        for i in range(ng):
            big(i, j).wait_recv()
            tot = g_in[i][...] + lands[i][j].astype(F32)
            o[i][...] = tot
            o[ng + i][...] = tot.astype(BF16)

        @pl.when(j == N_CHIP - 1)
        def _():
            for i in range(ns):
                small(i).wait_recv()
                o[2 * ng + i][...] = s_own[i][...] + lands[ng + i][...]
                small(i).wait_send()
            for blk in range(N_CHIP):
                for i in range(ng):
                    big(i, blk).wait_send()

    in_specs = [pl.BlockSpec((None, None, halves[i], g.shape[2]), lambda b, c: (b, c[0], 0, 0)) for i, g in enumerate(grads)]
    in_specs += [ANY] * ng
    in_specs += [pl.BlockSpec((None, s.shape[0] // 2, s.shape[1]), lambda b, c: (c[0], 0, 0)) for s in smalls]
    in_specs += [ANY] * ns
    blk = [pl.BlockSpec((None, halves[i], g.shape[2]), lambda b, c: (b, 0, 0)) for i, g in enumerate(grads)]
    out_specs = blk + blk + [pl.BlockSpec((s.shape[0] // 2, s.shape[1]), lambda b, c: (0, 0)) for s in smalls]
    out_shape = [jax.ShapeDtypeStruct((N_CHIP, halves[i], g.shape[2]), F32) for i, g in enumerate(grads)]
    out_shape += [jax.ShapeDtypeStruct((N_CHIP, halves[i], g.shape[2]), BF16) for i, g in enumerate(grads)]
    out_shape += [jax.ShapeDtypeStruct((s.shape[0] // 2, s.shape[1]), F32) for s in smalls]
    scratch = [pltpu.VMEM((N_CHIP, halves[i], g.shape[2]), BF16) for i, g in enumerate(grads)]
    scratch += [pltpu.VMEM((s.shape[0] // 2, s.shape[1]), F32) for s in smalls]
    scratch += [pltpu.SemaphoreType.DMA((ng + ns, N_CHIP)), pltpu.SemaphoreType.DMA((ng + ns, N_CHIP))]
    grads4 = [g.reshape(N_CHIP, 2, halves[i], g.shape[2]) for i, g in enumerate(grads)]
    smalls3 = [s.reshape(2, s.shape[0] // 2, s.shape[1]) for s in smalls]
    return pl.pallas_call(
        body, out_shape=out_shape,
        grid_spec=pltpu.PrefetchScalarGridSpec(num_scalar_prefetch=1, grid=(N_CHIP,), in_specs=in_specs,
                                               out_specs=out_specs, scratch_shapes=scratch),
        compiler_params=_cp(("arbitrary",), VMEM_LIMIT), name=name)(c_idx, *grads4, *grads_b, *smalls3, *smalls3)


def _grad_matmul_pair(c_idx, a, b, smalls, after, *, name, tk=2048):
    t, m = a.shape
    bn = b.shape[1] // N_CHIP
    nk = t // tk
    hr = m // 2
    ns = len(smalls)

    def body(c_ref, a_ref, b_ref, *refs):
        s_own, s_any = refs[:ns], refs[ns:2 * ns]
        o32, o16 = refs[2 * ns + 1], refs[2 * ns + 2]
        o_small = refs[2 * ns + 3:3 * ns + 3]
        acc, tb, land, st16 = refs[3 * ns + 3:3 * ns + 7]
        s_land, s_stage = refs[3 * ns + 7:4 * ns + 7], refs[4 * ns + 7:5 * ns + 7]
        send, recv, loc = refs[5 * ns + 7:]
        x, y, c, _ = _coords()
        sibling = dict(device_id=(x, y, 1 - c), device_id_type=MESH)
        j, kk = pl.program_id(0), pl.program_id(1)
        mine = pl.ds(pl.multiple_of(c * hr, hr), hr)
        theirs = pl.ds(pl.multiple_of((1 - c) * hr, hr), hr)

        def to_sibling(blk):
            return pltpu.make_async_remote_copy(src_ref=tb, dst_ref=land.at[blk], send_sem=send.at[blk],
                                                recv_sem=recv.at[blk], **sibling)

        def small(i):
            return pltpu.make_async_remote_copy(src_ref=s_any[i].at[1 - c], dst_ref=s_land[i], send_sem=send.at[N_CHIP + i],
                                                recv_sem=recv.at[N_CHIP + i], **sibling)

        def written(blk):
            return (pltpu.make_async_copy(acc.at[blk % 2, mine], o32.at[blk], loc.at[0]),
                    pltpu.make_async_copy(st16, o16.at[blk], loc.at[1]))

        def finish(blk):
            to_sibling(blk).wait_recv()

            @pl.when(blk > 0)
            def _():
                for cp in written(blk - 1):
                    cp.wait()

            tot = acc[blk % 2, mine, :] + land[blk].astype(F32)
            acc[blk % 2, mine, :] = tot
            st16[...] = tot.astype(BF16)
            for cp in written(blk):
                cp.start()

        def small_out(i):
            return pltpu.make_async_copy(s_stage[i], o_small[i], loc.at[2 + i])

        @pl.when((j == 0) & (kk == 0))
        def _():
            for i in range(ns):
                small(i).start()

        @pl.when((j == 1) & (kk == 0))
        def _():
            for i in range(ns):
                small(i).wait_recv()
                s_stage[i][...] = s_own[i][...] + s_land[i][...]
                small_out(i).start()

        @pl.when((j > 0) & (kk == 0))
        def _():
            finish(j - 1)

        part = lax.dot_general(a_ref[...], b_ref[...], (((0,), (0,)), ((), ())), preferred_element_type=F32)
        slot = lax.rem(j, 2)

        @pl.when(kk == 0)
        def _():
            acc[slot] = part

        @pl.when(kk > 0)
        def _():
            acc[slot] += part

        @pl.when(kk == nk - 1)
        def _():
            @pl.when(j > 0)
            def _():
                to_sibling(j - 1).wait_send()

            tb[...] = acc[slot, theirs, :].astype(BF16)
            to_sibling(j).start()

        @pl.when((j == N_CHIP - 1) & (kk == nk - 1))
        def _():
            finish(j)
            for i in range(ns):
                small_out(i).wait()
                small(i).wait_send()
            for cp in written(j):
                cp.wait()
            to_sibling(j).wait_send()

    halves = [(s.shape[0] // 2, s.shape[1]) for s in smalls]
    in_specs = [pl.BlockSpec((tk, m), lambda j, k, c: (k, 0)), pl.BlockSpec((tk, bn), lambda j, k, c: (k, j))]
    in_specs += [pl.BlockSpec((None,) + h, lambda j, k, c: (c[0], 0, 0)) for h in halves] + [ANY] * ns + [ANY]
    out_shape = [jax.ShapeDtypeStruct((N_CHIP, hr, bn), F32), jax.ShapeDtypeStruct((N_CHIP, hr, bn), BF16)]
    out_shape += [pltpu.HBM(h, F32) for h in halves]
    scratch = [pltpu.VMEM((2, m, bn), F32), pltpu.VMEM((hr, bn), BF16),
               pltpu.VMEM((N_CHIP, hr, bn), BF16), pltpu.VMEM((hr, bn), BF16)]
    scratch += [pltpu.VMEM(h, F32) for h in halves] * 2
    scratch += [pltpu.SemaphoreType.DMA((N_CHIP + ns,)), pltpu.SemaphoreType.DMA((N_CHIP + ns,)),
                pltpu.SemaphoreType.DMA((2 + ns,))]
    smalls3 = [s.reshape((2,) + h) for s, h in zip(smalls, halves)]
    outs = pl.pallas_call(
        body, out_shape=out_shape,
        grid_spec=pltpu.PrefetchScalarGridSpec(num_scalar_prefetch=1, grid=(N_CHIP, nk), in_specs=in_specs,
                                               out_specs=[ANY, ANY] + [_HBM] * ns, scratch_shapes=scratch),
        compiler_params=_cp(("arbitrary", "arbitrary"), VMEM_LIMIT), name=name)(c_idx, a, b, *smalls3, *smalls3, after)
    return outs[0], outs[1], list(outs[2:])


_HBM = pl.BlockSpec(memory_space=pltpu.HBM)
_SEM = pl.BlockSpec(memory_space=pltpu.SEMAPHORE)


def _split_copies(ins, lands, ng, send, recv, arriving):
    x, y, c, chips = _coords()
    b = 2 * x + y
    copies = []
    for i in range(len(ins)):
        for k in range(3):
            blk = 2 * chips[k][0] + chips[k][1]
            src, dst, got = (ins[i].at[blk], lands[i].at[k], lands[i].at[k]) if i < ng else (ins[i], lands[i].at[b], lands[i].at[blk])
            sems = dict(send_sem=send.at[3 * i + k], recv_sem=recv.at[3 * i + k], device_id=(*chips[k], c), device_id_type=MESH)
            if arriving:
                copies.append(pltpu.make_async_remote_copy(src_ref=got, dst_ref=got, **sems))
            else:
                copies.append(pltpu.make_async_remote_copy(src_ref=src, dst_ref=dst, **sems))
    return copies


def _exchange_begin(sums_b, smalls, name):
    ng, n = len(sums_b), len(sums_b) + len(smalls)
    srcs = list(sums_b) + list(smalls)
    lands = [lax.empty((3,) + g.shape[1:], g.dtype) for g in sums_b] + [lax.empty((N_CHIP,) + s.shape, s.dtype) for s in smalls]

    def body(*refs):
        ins, land_refs = refs[:n], refs[n:2 * n]
        send, recv = refs[2 * n], refs[2 * n + 1]
        token = refs[4 * n + 2]
        for cp in _split_copies(ins, land_refs, ng, send, recv, False):
            cp.start()
        token[...] = jnp.zeros_like(token)

    hbm = lambda a: pltpu.HBM(a.shape, a.dtype)
    outs = pl.pallas_call(
        body, name=name,
        out_shape=(pltpu.SemaphoreType.DMA((3 * n,)), pltpu.SemaphoreType.DMA((3 * n,)), *[hbm(a) for a in srcs + lands],
                   jax.ShapeDtypeStruct((8, 128), F32)),
        in_specs=[_HBM] * (2 * n), out_specs=(_SEM, _SEM, *[_HBM] * (2 * n), pl.BlockSpec(memory_space=pltpu.VMEM)),
        input_output_aliases={i: 2 + i for i in range(2 * n)},
        compiler_params=pltpu.CompilerParams(has_side_effects=pltpu.SideEffectType.DATAFLOW_SIDE_EFFECTING),
    )(*[pltpu.with_memory_space_constraint(a, pltpu.HBM) for a in srcs + lands])
    return outs[0], outs[1], list(outs[2:2 + n]), list(outs[2 + n:2 + 2 * n]), outs[2 + 2 * n]


def _exchange_end(send, recv, srcs, lands, ng, which, after, name):
    n = len(srcs)
    after = list(after)

    def body(*refs):
        ins, land_refs = refs[:n], refs[n:2 * n]
        send_ref, recv_ref = refs[2 * n], refs[2 * n + 1]
        outgoing = _split_copies(ins, land_refs, ng, send_ref, recv_ref, False)
        arriving = _split_copies(ins, land_refs, ng, send_ref, recv_ref, True)
        for i in which:
            for cp in outgoing[3 * i:3 * i + 3]:
                cp.wait_send()
        for i in which:
            for cp in arriving[3 * i:3 * i + 3]:
                cp.wait_recv()

    hbm = lambda a: pltpu.HBM(a.shape, a.dtype)
    outs = pl.pallas_call(
        body, name=name, out_shape=tuple(hbm(a) for a in list(srcs) + list(lands)),
        in_specs=[_HBM] * (2 * n) + [_SEM, _SEM] + [ANY] * len(after), out_specs=tuple([_HBM] * (2 * n)),
        input_output_aliases={i: i for i in range(2 * n)},
        compiler_params=pltpu.CompilerParams(has_side_effects=pltpu.SideEffectType.DATAFLOW_SIDE_EFFECTING),
    )(*srcs, *lands, send, recv, *after)
    return list(outs[:n]), list(outs[n:])


def _chip_reduce(bc_idx, sums, recvd, smalls_slots, smalls_own, name, steps=4):
    ng, ns = len(sums), len(smalls_slots)
    n = ng + ns
    assert steps >= 2
    halves = [g.shape[1] for g in sums] + [s.shape[1] for s in smalls_slots]
    rows = [g.shape[1] // steps for g in sums]

    def body(bc_ref, *refs):
        own, rx = refs[:ng], refs[ng:2 * ng]
        sl = refs[2 * ng:2 * ng + ns]
        sl_own = refs[2 * ng + ns:2 * ng + 2 * ns]
        o = refs[2 * ng + 2 * ns:2 * ng + 2 * ns + n]
        tiles = refs[2 * ng + 2 * ns + n:2 * ng + 2 * ns + 2 * n]
        keep, send, recv = refs[2 * ng + 2 * ns + 2 * n:]
        x, y, c, _ = _coords()
        sibling = dict(device_id=(x, y, 1 - c), device_id_type=MESH)
        r = pl.program_id(0)

        def writes(i, step, slot):
            dst = o[i].at[pl.ds(c * halves[i] + step * rows[i], rows[i])]
            return (pltpu.make_async_copy(tiles[i].at[slot], dst, keep.at[i, slot]),
                    pltpu.make_async_remote_copy(src_ref=tiles[i].at[slot], dst_ref=dst, send_sem=send.at[i, slot],
                                                 recv_sem=recv.at[i, step], **sibling))

        def small_writes(i):
            dst = o[i].at[pl.ds(c * halves[i], halves[i])]
            return (pltpu.make_async_copy(tiles[i], dst, keep.at[i, 0]),
                    pltpu.make_async_remote_copy(src_ref=tiles[i], dst_ref=dst, send_sem=send.at[i, 0],
                                                 recv_sem=recv.at[i, 0], **sibling))

        def arriving(i, step, nrows):
            dst = o[i].at[pl.ds((1 - c) * halves[i] + step * nrows, nrows)]
            return pltpu.make_async_remote_copy(src_ref=dst, dst_ref=dst, send_sem=send.at[i, 0], recv_sem=recv.at[i, step],
                                                **sibling)

        def finish(step, slot):
            for i in range(ng):
                local, remote = writes(i, step, slot)
                local.wait()
                remote.wait_send()

        @pl.when(r >= 2)
        def _():
            finish(r - 2, r % 2)

        for i in range(ng):
            tot = own[i][...]
            for j in range(3):
                tot = tot + rx[i][j].astype(F32)
            tiles[i][r % 2] = tot
            for cp in writes(i, r, r % 2):
                cp.start()

        @pl.when(r == 0)
        def _():
            for i in range(ns):
                term = [jnp.where(bc_ref[0] == kk, sl_own[i][...], sl[i][kk]) for kk in range(N_CHIP)]
                tiles[ng + i][...] = ((term[0] + term[1]) + term[2]) + term[3]
                for cp in small_writes(ng + i):
                    cp.start()

        @pl.when(r == steps - 1)
        def _():
            finish(steps - 2, (steps - 2) % 2)
            finish(steps - 1, (steps - 1) % 2)
            for i in range(ns):
                local, remote = small_writes(ng + i)
                local.wait()
                remote.wait_send()
                arriving(ng + i, 0, halves[ng + i]).wait_recv()
            for i in range(ng):
                for step in range(steps):
                    arriving(i, step, rows[i]).wait_recv()

    in_specs = [pl.BlockSpec((None, rows[i], g.shape[2]), lambda r, bc: (bc[0], r, 0)) for i, g in enumerate(sums)]
    in_specs += [pl.BlockSpec((3, rows[i], g.shape[2]), lambda r, bc: (0, r, 0)) for i, g in enumerate(sums)]
    in_specs += [pl.BlockSpec(s.shape, lambda r, bc: (0, 0, 0)) for s in smalls_slots]
    in_specs += [pl.BlockSpec(s.shape[1:], lambda r, bc: (0, 0)) for s in smalls_slots]
    out_shape = [jax.ShapeDtypeStruct((2 * g.shape[1], g.shape[2]), F32) for g in sums]
    out_shape += [jax.ShapeDtypeStruct((2 * s.shape[1], s.shape[2]), F32) for s in smalls_slots]
    scratch = [pltpu.VMEM((2, rows[i], g.shape[2]), F32) for i, g in enumerate(sums)]
    scratch += [pltpu.VMEM(s.shape[1:], F32) for s in smalls_slots]
    scratch += [pltpu.SemaphoreType.DMA((n, 2)), pltpu.SemaphoreType.DMA((n, 2)), pltpu.SemaphoreType.DMA((n, steps))]
    return list(pl.pallas_call(
        body, out_shape=out_shape,
        grid_spec=pltpu.PrefetchScalarGridSpec(num_scalar_prefetch=1, grid=(steps,), in_specs=in_specs,
                                               out_specs=[ANY] * n, scratch_shapes=scratch),
        compiler_params=_cp(("arbitrary",), VMEM_LIMIT), name=name)(bc_idx, *sums, *recvd, *smalls_slots, *smalls_own))


def _adamw_math(w, g, m, v):
    m2 = ADAM_B1 * m + (1.0 - ADAM_B1) * g
    v2 = ADAM_B2 * v + (1.0 - ADAM_B2) * (g * g)
    m_hat = m2 / (1.0 - ADAM_B1 ** ADAM_STEP)
    v_hat = v2 / (1.0 - ADAM_B2 ** ADAM_STEP)
    delta = -ADAM_LR * (m_hat / (jnp.sqrt(v_hat) + ADAM_EPS) + ADAM_WD * w)
    return delta, m2, v2


def _adamw_big(ws, gs, ms, vs, name, steps=8):
    n = len(ws)

    def body(*refs):
        for i in range(n):
            w_ref, g_ref, m_ref, v_ref = (refs[k * n + i] for k in range(4))
            d_ref, m2_ref, v2_ref, g2_ref = (refs[(4 + k) * n + i] for k in range(4))
            gv = g_ref[...]
            d_ref[...], m2_ref[...], v2_ref[...] = _adamw_math(w_ref[...], gv, m_ref[...], v_ref[...])
            g2_ref[...] = gv

    specs = [pl.BlockSpec((w.shape[0] // steps, w.shape[1]), lambda i: (i, 0)) for w in ws]
    shapes = [jax.ShapeDtypeStruct(w.shape, F32) for w in ws]
    outs = pl.pallas_call(
        body, grid=(steps,), in_specs=specs * 4, out_specs=specs * 4, out_shape=shapes * 4,
        compiler_params=_cp(("parallel",), VMEM_LIMIT), name=name)(*ws, *gs, *ms, *vs)
    return [tuple(outs[k * n + i] for k in range(4)) for i in range(n)]


def _adamw_small(b_idx, sv, sw, vecs, conv, ws):
    nv = len(vecs)
    cols = conv[0].shape[1]

    def body(b_ref, sv_ref, sw_ref, *refs):
        ins, outs = refs[:3 * nv + 6], refs[3 * nv + 6:]
        for i in range(nv):
            g = sv_ref[i:i + 1, :]
            w_ref, m_ref, v_ref = ins[3 * i:3 * i + 3]
            d_ref, m2_ref, v2_ref, g_ref = outs[4 * i:4 * i + 4]
            d_ref[...], m2_ref[...], v2_ref[...] = _adamw_math(w_ref[...], g, m_ref[...], v_ref[...])
            g_ref[...] = g
        g = sv_ref[8:8 + conv[0].shape[0], pl.ds(pl.multiple_of(b_ref[0] * cols, cols), cols)]
        w_ref, m_ref, v_ref = ins[3 * nv:3 * nv + 3]
        d_ref, m2_ref, v2_ref, g_ref = outs[4 * nv:4 * nv + 4]
        d_ref[...], m2_ref[...], v2_ref[...] = _adamw_math(w_ref[...], g, m_ref[...], v_ref[...])
        g_ref[...] = g
        w_ref, m_ref, v_ref = ins[3 * nv + 3:]
        d_ref, m2_ref, v2_ref, g_ref, one_ref = outs[4 * nv + 4:]
        g = sw_ref[...]
        d_ref[...], m2_ref[...], v2_ref[...] = _adamw_math(w_ref[...], g, m_ref[...], v_ref[...])
        g_ref[...] = g
        one_ref[...] = sv_ref[nv:nv + 1, 0:1]

    flat = [a for grp in vecs for a in grp] + list(conv) + list(ws)
    out_shape = [jax.ShapeDtypeStruct(grp[0].shape, F32) for grp in list(vecs) + [conv, ws] for _ in range(4)]
    out_shape += [jax.ShapeDtypeStruct((1, 1), F32)]
    vmem = pl.BlockSpec(memory_space=pltpu.VMEM)
    outs = pl.pallas_call(
        body, out_shape=out_shape, in_specs=[pl.BlockSpec(memory_space=pltpu.SMEM)] + [vmem] * (2 + len(flat)),
        out_specs=[vmem] * len(out_shape), name="adamw_small")(b_idx, sv, sw, *flat)
    return [tuple(outs[4 * i:4 * i + 4]) for i in range(nv + 2)], outs[4 * nv + 8]


def kernel(x, mem, norm_mix_g, w_in, conv_w, gm_ln_g, gm_ln_b, gm_ws, gm_bs, w_out, norm_x_g, norm_mem_g, w_q, w_kv, w_xo, norm_final_g, loss_target, m_norm_mix_g, m_w_in, m_conv_w, m_gm_ln_g, m_gm_ln_b, m_gm_ws, m_gm_bs, m_w_out, m_norm_x_g, m_norm_mem_g, m_w_q, m_w_kv, m_w_xo, m_norm_final_g, v_norm_mix_g, v_w_in, v_conv_w, v_gm_ln_g, v_gm_ln_b, v_gm_ws, v_gm_bs, v_w_out, v_norm_x_g, v_norm_mem_g, v_w_q, v_w_kv, v_w_xo, v_norm_final_g):
    t = x.shape[1]
    xi = lax.axis_index("x")
    yi = lax.axis_index("y")
    ci = lax.axis_index("c")
    b_idx = jnp.reshape(2 * xi + yi, (1,)).astype(jnp.int32)
    c_idx = jnp.reshape(ci, (1,)).astype(jnp.int32)

    x2d, mem2d, tgt = x[0], mem[0], loss_target[0]
    big = [w_in[0], w_out[0], w_q[0], w_kv[0], w_xo[0]]
    big_m = [m_w_in[0], m_w_out[0], m_w_q[0], m_w_kv[0], m_w_xo[0]]
    big_v = [v_w_in[0], v_w_out[0], v_w_q[0], v_w_kv[0], v_w_xo[0]]
    g3 = norm_final_g.reshape(1, D)

    def pad8(a):
        return jnp.pad(a, ((0, 8 - a.shape[0]), (0, 0)))

    own_blocks = _cast_shards(b_idx, big)

    tril = jnp.tril(jnp.ones((CH, CH), bool))
    wc32 = jnp.where(tril[None], gm_ws[0], 0.0)
    wc = wc32.astype(BF16)
    wct = jnp.swapaxes(wc32, 1, 2).astype(BF16)
    bsb = jnp.broadcast_to(gm_bs[0][:, :, None], (HEADS, CH, CH))

    proj, hb, win_f, cw8, (wq_f,) = _proj_gather(
        b_idx, x2d, norm_mix_g, own_blocks[0], pad8(conv_w[0]), [own_blocks[2]])
    mixin, (wout_f, wkv_f, wxo_f) = _mixer_fwd(
        proj, cw8, gm_ln_g, gm_ln_b, wc, bsb, [own_blocks[1], own_blocks[3], own_blocks[4]])
    wout2, wq2, wxo2 = wout_f.reshape(MIX, D), wq_f.reshape(D, D), wxo_f.reshape(D, D)
    k, v = _mem_fwd(mem2d, norm_mem_g, wkv_f)

    (loss_row, dmix, dx1b, h2b, dq, ob, dx2b, dk, dv, dg2, dg3) = _tail(
        x2d, tgt, mixin, wout2, wq2, wxo2, k, v, norm_x_g, g3)
    dwkv, dwkv_b, dgm = _mem_bwd(mem2d, norm_mem_g, dk, dv, wkv_f)
    dproj, sv, dwc, grad_x = _mixer_bwd(
        proj, dmix, cw8, gm_ln_g, gm_ln_b, wc, wct, bsb, win_f, x2d, dx1b, norm_mix_g, [dg2, dgm, dg3], loss_row)

    bc_idx = jnp.concatenate([b_idx, c_idx])
    sw = dwc.reshape(HEADS * CH, CH)
    dwin_sum, dwin_sum_b, psmall = _grad_matmul_pair(c_idx, hb, dproj, [sv, sw], dgm, name="grad_w_in")
    sums_b = [dwin_sum]
    send_b, recv_b, src_b, land_b, token_b = _exchange_begin([dwin_sum_b], psmall, "exchange_b_begin")

    dwxo, dwxo_b = _grad_matmul(ob, dx2b, token_b, by_cols=False, name="grad_w_xo", tk=2048)
    dwq, dwq_b = _grad_matmul(h2b, dq, token_b, by_cols=False, name="grad_w_q", tk=2048)
    dwout, dwout_b = _grad_matmul(mixin, dx1b, token_b, by_cols=False, name="grad_w_out")
    ps_a = _pair_reduce(c_idx, [dwout, dwkv, dwq, dwxo], [dwout_b, dwkv_b, dwq_b, dwxo_b], [], "pair_reduce_a")
    sums_a, sums_a_b = list(ps_a[:4]), list(ps_a[4:8])
    send_a, recv_a, src_a, land_a, token_a = _exchange_begin(sums_a_b, [], "exchange_a_begin")

    src_b, rx2b = _exchange_end(send_b, recv_b, src_b, land_b, 1, [0, 1, 2], [token_a], "exchange_b_end")
    gwin, svf, swf = _chip_reduce(bc_idx, sums_b, rx2b[:1], rx2b[1:], src_b[1:], "chip_reduce_b")
    out_b = _adamw_big(big[:1], [gwin], big_m[:1], big_v[:1], "adamw_w_in")[0]

    row = lambda a: a.reshape(1, D)
    mat = lambda a: a.reshape(HEADS * CH, CH)
    small, loss = _adamw_small(
        b_idx, svf, swf,
        [(row(norm_mix_g), row(m_norm_mix_g), row(v_norm_mix_g)), (row(norm_x_g), row(m_norm_x_g), row(v_norm_x_g)),
         (row(norm_mem_g), row(m_norm_mem_g), row(v_norm_mem_g)), (row(norm_final_g), row(m_norm_final_g), row(v_norm_final_g)),
         (row(gm_ln_g), row(m_gm_ln_g), row(v_gm_ln_g)), (row(gm_ln_b), row(m_gm_ln_b), row(v_gm_ln_b)),
         (row(gm_bs), row(m_gm_bs), row(v_gm_bs))],
        (conv_w[0], m_conv_w[0], v_conv_w[0]), (mat(gm_ws), mat(m_gm_ws), mat(v_gm_ws)))

    def finish_a(part, src, land, after, tag):
        src, land = _exchange_end(send_a, recv_a, src, land, 4, part, after, "exchange_a%s_end" % tag)
        grads = _chip_reduce(bc_idx, [sums_a[i] for i in part], [land[i] for i in part], [], [], "chip_reduce_a" + tag,
                             steps=2)
        ids = [(1, 3, 2, 4)[i] for i in part]
        outs = _adamw_big([big[i] for i in ids], grads, [big_m[i] for i in ids], [big_v[i] for i in ids], "adamw_a" + tag,
                          steps=4)
        return src, land, outs

    src_a, land_a, (out_wout, out_wkv) = finish_a([0, 1], src_a, land_a, [out_b[0], small[0][0]], "1")
    _, _, (out_wq, out_wxo) = finish_a([2, 3], src_a, land_a, [out_wout[0]], "2")

    def unpack(k):
        vec = lambda i: small[i][k]
        return [vec(0), out_b[k][None], small[7][k][None], vec(4), vec(5), small[8][k].reshape(1, HEADS, CH, CH),
                vec(6).reshape(1, HEADS, CH), out_wout[k][None], vec(1), vec(2), out_wq[k][None], out_wkv[k][None],
                out_wxo[k][None], vec(3).reshape(D)]

    return (loss.reshape(()), grad_x[None], *unpack(3), *unpack(0), *unpack(1), *unpack(2))
```

```python
import functools
import math

import jax
import jax.numpy as jnp
from jax import lax
from jax.experimental import pallas as pl
from jax.experimental.pallas import tpu as pltpu

F32 = jnp.float32
BF16 = jnp.bfloat16
MESH = pl.DeviceIdType.MESH

D = 1024
SLAB = 1024
N_SLAB = 7
IN_DIM = N_SLAB * SLAB
MIX = 2 * SLAB
HEADS = 8
CH = 128
XH = 4
XD = D // XH
EPS = 1e-6
GELU_C = math.sqrt(2.0 / math.pi)
GELU_A = 0.044715
N_CHIP = 4
IN_BLK = IN_DIM // N_CHIP
IN_PIECE = 256
N_PIECE = IN_BLK // IN_PIECE
KV_BLK = 2 * D // N_CHIP

ADAM_LR, ADAM_B1, ADAM_B2, ADAM_EPS, ADAM_WD, ADAM_STEP = 0.001, 0.9, 0.999, 1e-08, 0.01, 10

VMEM_LIMIT = 60 * 1024 * 1024


def _cp(sem=None, vmem=None):
    return pltpu.CompilerParams(dimension_semantics=sem, vmem_limit_bytes=vmem)


def _full(shape, buffers=None):
    n = len(shape)
    if buffers is None:
        return pl.BlockSpec(shape, lambda *_: (0,) * n)
    return pl.BlockSpec(shape, lambda *_: (0,) * n, pipeline_mode=pl.Buffered(buffers))


ANY = pl.BlockSpec(memory_space=pl.ANY)


def _bdot(a, b):
    return jnp.dot(a.astype(BF16), b.astype(BF16), preferred_element_type=F32)


def _bdot_nt(a, b):
    return lax.dot_general(a.astype(BF16), b.astype(BF16), (((1,), (1,)), ((), ())), preferred_element_type=F32)


def _bdot_tn(a, b):
    return lax.dot_general(a.astype(BF16), b.astype(BF16), (((0,), (0,)), ((), ())), preferred_element_type=F32)


def _rms(x, g):
    r = lax.rsqrt(jnp.mean(x * x, axis=-1, keepdims=True) + EPS)
    return x * r * g, r


def _rms_bwd(dy, x, r, g):
    gdy = dy * g
    dx = r * gdy - x * (r * r * r) * jnp.mean(x * gdy, axis=-1, keepdims=True)
    dg = jnp.sum(dy * x * r, axis=0, keepdims=True)
    return dx, dg


def _gelu_parts(x):
    x2 = x * x
    t = jnp.tanh(GELU_C * (x + GELU_A * x * x2))
    val = 0.5 * x * (1.0 + t)
    grad = 0.5 * (1.0 + t) + 0.5 * x * (1.0 - t * t) * (GELU_C * (1.0 + 3.0 * GELU_A * x2))
    return val, grad


def _gelu(x):
    return 0.5 * x * (1.0 + jnp.tanh(GELU_C * (x + GELU_A * x * x * x)))


def _sigmoid(z):
    return 1.0 / (1.0 + jnp.exp(-z))


def _cast_shards(b_idx, arrs):
    n = len(arrs)
    steps = 4

    def body(b_ref, *refs):
        for p in range(N_PIECE):
            refs[n][p] = refs[0][:, pl.ds(p * IN_PIECE, IN_PIECE)].astype(BF16)
        for i in range(1, n):
            refs[n + i][...] = refs[i][...].astype(BF16)

    rows = [a.shape[0] // steps for a in arrs]
    in_specs = [pl.BlockSpec((rows[i], a.shape[1]), lambda i, b: (i, 0)) for i, a in enumerate(arrs)]
    out_specs = [pl.BlockSpec((None, N_PIECE, rows[0], IN_PIECE), lambda i, b: (b[0], 0, i, 0))]
    out_specs += [pl.BlockSpec((None, rows[i], a.shape[1]), lambda i, b: (b[0], i, 0)) for i, a in enumerate(arrs) if i > 0]
    out_shape = [jax.ShapeDtypeStruct((N_CHIP, N_PIECE, arrs[0].shape[0], IN_PIECE), BF16)]
    out_shape += [jax.ShapeDtypeStruct((N_CHIP,) + a.shape, BF16) for a in arrs[1:]]
    return pl.pallas_call(
        body, out_shape=out_shape,
        grid_spec=pltpu.PrefetchScalarGridSpec(num_scalar_prefetch=1, grid=(steps,), in_specs=in_specs, out_specs=out_specs),
        compiler_params=_cp(("arbitrary",)), name="cast_shards")(b_idx, *arrs)


def _proj_gather(b_idx, x, g, win_own, cw8s, more, tm=1024):
    t = x.shape[0]
    ni = t // tm
    nm = len(more)
    steps = N_CHIP * N_PIECE
    near0, far0 = N_PIECE, 3 * N_PIECE

    def piece_at(step, own):
        k = step - near0
        near, far = (step >= near0) & (step < far0), step >= far0
        block = jnp.where(far, own ^ 3, jnp.where(near, own ^ jnp.where(lax.rem(k, 2) == 0, 2, 1), own))
        return block, jnp.where(far, step - far0, jnp.where(near, lax.div(k, 2), step))

    def body(*refs):
        b_ref, x_any, g_ref, win_in, cw_in = refs[:5]
        o_ref, hb_any, win_f, cw_out = refs[5 + nm:9 + nm]
        more_out = refs[9 + nm:9 + 2 * nm]
        hbuf, xbuf, wv, cw_s, cw_r, loc = refs[9 + 2 * nm:15 + 2 * nm]
        g_in = _Gather([win_f.at[:, p] for p in range(N_PIECE)], *refs[15 + 2 * nm:19 + 2 * nm])
        g_more = _Gather(more_out, *refs[19 + 2 * nm:23 + 2 * nm])
        s = pl.program_id(0)
        x, y, c, chips = _coords()
        b = 2 * x + y
        blks = [2 * chip[0] + chip[1] for chip in chips]

        def cw_cols(blk):
            return cw_out.at[:, pl.ds(blk * (D // N_CHIP), D // N_CHIP)]

        def cw_copy(k, blk):
            src = cw_in if blk is None else cw_cols(blk)
            return pltpu.make_async_remote_copy(src_ref=src, dst_ref=cw_cols(b if blk is None else blk), send_sem=cw_s.at[k],
                                                recv_sem=cw_r.at[k], device_id=(*chips[k], c), device_id_type=MESH)

        cw_local = pltpu.make_async_copy(cw_in, cw_cols(b), loc.at[1])
        hb_copy = pltpu.make_async_copy(hbuf, hb_any, loc.at[0])

        def load(step):
            slot = lax.rem(step, 2)
            block, piece = piece_at(step, b_ref[0])
            return pltpu.make_async_copy(win_f.at[block, piece], wv.at[slot], loc.at[2 + slot])

        def chunk(i):
            return pltpu.make_async_copy(x_any.at[pl.ds(i * tm, tm)], xbuf.at[i % 2], loc.at[4 + i % 2])

        def first():
            g_in.start()
            cw_local.start()
            for k in range(3):
                cw_copy(k, None).start()
            load(0).start()
            chunk(0).start()
            for i in range(ni):
                if i + 1 < ni:
                    chunk(i + 1).start()
                chunk(i).wait()
                h, _ = _rms(xbuf[i % 2], g_ref[...])
                hbuf[pl.ds(i * tm, tm), :] = h.astype(BF16)
            hb_copy.start()

        events = {step: [] for step in range(steps)}
        events[0].append(first)
        for p in range(N_PIECE):
            events[2 * p + 2].append(functools.partial(g_in.hop, [p]))
            events[near0 + 2 * p - 1].append(functools.partial(g_in.near_ready, [p]))
            events[far0 + p - 2].append(functools.partial(g_in.far, [p]))
            events[far0 + p - 1].append(functools.partial(g_in.far_ready, [p]))
        events[2 * N_PIECE + 1].append(g_more.start)
        for step, todo in events.items():
            if todo:
                @pl.when(s == step)
                def _(todo=todo):
                    for do in todo:
                        do()

        @pl.when(s + 1 < steps)
        def _():
            load(s + 1).start()

        load(s).wait()
        for i in range(ni):
            rows = pl.ds(i * tm, tm)
            o_ref[rows, :] = jnp.dot(hbuf[rows, :], wv[lax.rem(s, 2)], preferred_element_type=F32).astype(BF16)

        @pl.when(s == steps - 1)
        def _():
            g_more.hop()
            g_more.far()
            for k in range(3):
                cw_copy(k, blks[k]).wait_recv()
            for k in range(3):
                cw_copy(k, None).wait_send()
            cw_local.wait()
            hb_copy.wait()
            g_more.near_ready()
            g_more.far_ready()
            g_in.drain()
            g_more.drain()

    def out_col(s, b):
        block, piece = piece_at(s, b[0])
        return 0, block * N_PIECE + piece

    in_specs = [ANY, pl.BlockSpec((1, D), lambda s, b: (0, 0)), ANY, ANY] + [ANY] * nm
    out_specs = [pl.BlockSpec((t, IN_PIECE), out_col), ANY, ANY, ANY] + [ANY] * nm
    outs = pl.pallas_call(
        body, out_shape=[jax.ShapeDtypeStruct((t, IN_DIM), BF16), jax.ShapeDtypeStruct((t, D), BF16),
                         jax.ShapeDtypeStruct(win_own.shape, BF16), jax.ShapeDtypeStruct((8, D), F32)]
        + [jax.ShapeDtypeStruct(f.shape, f.dtype) for f in more],
        grid_spec=pltpu.PrefetchScalarGridSpec(
            num_scalar_prefetch=1, grid=(steps,), in_specs=in_specs, out_specs=out_specs,
            scratch_shapes=[pltpu.VMEM((t, D), BF16), pltpu.VMEM((2, tm, D), F32), pltpu.VMEM((2, D, IN_PIECE), BF16)]
            + [pltpu.SemaphoreType.DMA((3,))] * 2 + [pltpu.SemaphoreType.DMA((6,))]
            + _gather_sems(N_PIECE) + _gather_sems(nm)),
        input_output_aliases={3: 2, **{5 + w: 4 + w for w in range(nm)}},
        compiler_params=_cp(("arbitrary",), VMEM_LIMIT), name="proj_gather")(b_idx, x, g, win_own, cw8s, *more)
    return outs[0], outs[1], outs[2], outs[3], outs[4:]


class _Gather:
    def __init__(self, outs, ici_s, ici_r, d2d_s, d2d_r):
        x, y, c, _ = _coords()
        self.outs, self.c = outs, c
        self.sems = ici_s, ici_r, d2d_s, d2d_r
        self.b, self.bx, self.by, self.bd = 2 * x + y, 2 * (1 - x) + y, 2 * x + (1 - y), 2 * (1 - x) + (1 - y)
        self.xn, self.yn, self.sib = (1 - x, y, c), (x, 1 - y, c), (x, y, 1 - c)

    def piece(self, w, blk, hc, quarter=None):
        hr = self.outs[w].shape[1] // 2
        if quarter is None:
            return self.outs[w].at[blk, pl.ds(hc * hr, hr)]
        return self.outs[w].at[blk, pl.ds(hc * hr + quarter * (hr // 2), hr // 2)]

    def ici(self, w, k, ref, to):
        return pltpu.make_async_remote_copy(src_ref=ref, dst_ref=ref, send_sem=self.sems[0].at[w, k],
                                            recv_sem=self.sems[1].at[w, k], device_id=to, device_id_type=MESH)

    def d2d(self, w, k, ref):
        return pltpu.make_async_remote_copy(src_ref=ref, dst_ref=ref, send_sem=self.sems[2].at[w, k],
                                            recv_sem=self.sems[3].at[w, k], device_id=self.sib, device_id_type=MESH)

    def all(self):
        return range(len(self.outs))

    def start(self):
        for w in self.all():
            mine = self.piece(w, self.b, self.c)
            self.ici(w, 0, mine, self.xn).start()
            self.ici(w, 1, mine, self.yn).start()

    def hop(self, ws=None):
        c = self.c
        for w in ws or self.all():
            self.ici(w, 0, self.piece(w, self.bx, c), self.xn).wait_recv()
            self.ici(w, 1, self.piece(w, self.by, c), self.yn).wait_recv()
            self.ici(w, 2, self.piece(w, self.bx, c, 0), self.yn).start()
            self.ici(w, 3, self.piece(w, self.by, c, 1), self.xn).start()
            self.d2d(w, 0, self.piece(w, self.bx, c)).start()
            self.d2d(w, 1, self.piece(w, self.by, c)).start()

    def near_ready(self, ws=None):
        for w in ws or self.all():
            self.d2d(w, 0, self.piece(w, self.bx, 1 - self.c)).wait_recv()
            self.d2d(w, 1, self.piece(w, self.by, 1 - self.c)).wait_recv()

    def far(self, ws=None):
        c = self.c
        for w in ws or self.all():
            self.ici(w, 2, self.piece(w, self.bd, c, 0), self.yn).wait_recv()
            self.ici(w, 3, self.piece(w, self.bd, c, 1), self.xn).wait_recv()
            self.d2d(w, 2, self.piece(w, self.bd, c, 0)).start()
            self.d2d(w, 3, self.piece(w, self.bd, c, 1)).start()

    def far_ready(self, ws=None):
        for w in ws or self.all():
            self.d2d(w, 2, self.piece(w, self.bd, 1 - self.c, 0)).wait_recv()
            self.d2d(w, 3, self.piece(w, self.bd, 1 - self.c, 1)).wait_recv()

    def drain(self):
        c = self.c
        for w in self.all():
            mine = self.piece(w, self.b, c)
            self.ici(w, 0, mine, self.xn).wait_send()
            self.ici(w, 1, mine, self.yn).wait_send()
            self.ici(w, 2, self.piece(w, self.bx, c, 0), self.yn).wait_send()
            self.ici(w, 3, self.piece(w, self.by, c, 1), self.xn).wait_send()
            self.d2d(w, 0, self.piece(w, self.bx, c)).wait_send()
            self.d2d(w, 1, self.piece(w, self.by, c)).wait_send()
            self.d2d(w, 2, self.piece(w, self.bd, c, 0)).wait_send()
            self.d2d(w, 3, self.piece(w, self.bd, c, 1)).wait_send()


def _gather_sems(nw):
    return [pltpu.SemaphoreType.DMA((max(nw, 1), 4))] * 4


def _mixer_fwd(proj, cw8, lng, lnb, wc, bsb, fulls, tm=256):
    t = proj.shape[0]
    nt = t // tm
    nch = tm // CH
    nw = len(fulls)

    def body(*refs):
        p_ref, cw_ref, lng_ref, lnb_ref, wc_ref, bsb_ref = refs[:6]
        mix_ref = refs[6 + nw]
        w_outs = refs[7 + nw:7 + 2 * nw]
        prev_ref = refs[7 + 2 * nw]
        gather = _Gather(w_outs, *refs[8 + 2 * nw:])

        @pl.when(pl.program_id(0) == 0)
        def _():
            gather.start()
            prev_ref[...] = jnp.zeros_like(prev_ref)

        @pl.when(pl.program_id(0) == nt // 2)
        def _():
            gather.hop()

        @pl.when(pl.program_id(0) == nt - 1)
        def _():
            gather.far()

        rows = lax.broadcasted_iota(jnp.int32, (tm, CH), 0)
        for s in range(HEADS):
            cs = pl.ds(CH * s, CH)

            def slab(k):
                return p_ref[:, pl.ds(k * SLAB + CH * s, CH)].astype(F32)

            gb, gc, xa, za = slab(0), slab(1), slab(2), slab(3)
            cx = gc * xa
            p6 = jnp.broadcast_to(prev_ref[6:7, cs], (tm, CH))
            p7 = jnp.broadcast_to(prev_ref[7:8, cs], (tm, CH))
            c1 = jnp.where(rows == 0, p7, pltpu.roll(cx, 1, 0))
            c2 = jnp.where(rows == 0, p6, jnp.where(rows == 1, p7, pltpu.roll(cx, 2, 0)))
            prev_ref[:, cs] = cx[tm - 8:, :]
            cv = cw_ref[0:1, cs] * c2 + cw_ref[1:2, cs] * c1 + cw_ref[2:3, cs] * cx
            mix_ref[:, cs] = (gb * cv * (za * _sigmoid(za))).astype(BF16)

            u, v, zb = slab(4), slab(5), slab(6)
            ug, vg = _gelu(u), _gelu(v)
            dlt = vg - jnp.mean(vg, axis=-1, keepdims=True)
            vhat = dlt * lax.rsqrt(jnp.mean(dlt * dlt, axis=-1, keepdims=True) + EPS)
            vn = (vhat * lng_ref[:, cs] + lnb_ref[:, cs]).astype(BF16)
            gate = ug * (zb * _sigmoid(zb))
            for c in range(nch):
                rs = slice(CH * c, CH * (c + 1))
                sp = jnp.dot(wc_ref[s], vn[rs], preferred_element_type=F32) + bsb_ref[s]
                mix_ref[rs, pl.ds(SLAB + CH * s, CH)] = (gate[rs] * sp).astype(BF16)

        @pl.when(pl.program_id(0) == nt - 1)
        def _():
            gather.near_ready()
            gather.far_ready()
            gather.drain()

    sems = _gather_sems(nw)
    outs = pl.pallas_call(
        body, grid=(nt,),
        in_specs=[pl.BlockSpec((tm, IN_DIM), lambda i: (i, 0)), _full((8, D)), _full((1, D)), _full((1, D)),
                  _full((HEADS, CH, CH)), _full((HEADS, CH, CH))] + [ANY] * nw,
        out_specs=[pl.BlockSpec((tm, MIX), lambda i: (i, 0))] + [ANY] * nw,
        out_shape=[jax.ShapeDtypeStruct((t, MIX), BF16)] + [jax.ShapeDtypeStruct(f.shape, f.dtype) for f in fulls],
        input_output_aliases={6 + w: 1 + w for w in range(nw)},
        scratch_shapes=[pltpu.VMEM((8, D), F32)] + sems,
        compiler_params=_cp(("arbitrary",), VMEM_LIMIT), name="mixer_fwd")(proj, cw8, lng, lnb, wc, bsb, *fulls)
    return outs[0], outs[1:]


def _mem_fwd(mem, gm, wkv_f):
    n_mem = mem.shape[0]

    def body(mem_ref, gm_ref, w_ref, k_ref, v_ref):
        m, _ = _rms(mem_ref[...], gm_ref[...])
        mb = m.astype(BF16)
        for j in range(N_CHIP):
            dst = k_ref if j < 2 else v_ref
            dst[:, pl.ds(KV_BLK * (j % 2), KV_BLK)] = jnp.dot(mb, w_ref[j], preferred_element_type=F32).astype(BF16)

    return pl.pallas_call(
        body, out_shape=[jax.ShapeDtypeStruct((n_mem, D), BF16), jax.ShapeDtypeStruct((n_mem, D), BF16)],
        compiler_params=_cp(None, VMEM_LIMIT), name="mem_fwd")(mem, gm, wkv_f)


def _tail(x, tgt, mixin, wout, wq, wxo, k, v, g2, g3, tm=512, sub=512):
    t = x.shape[0]
    n_mem = k.shape[0]
    scale = 1.0 / math.sqrt(XD)

    def body(x_ref, tgt_ref, mix_ref, wout_ref, wq_ref, wxo_ref, k_ref, v_ref, g2_ref, g3_ref,
             loss_ref, dmix_ref, dx1b_ref, h2_ref, dq_ref, o_ref, dx2b_ref, dk_ref, dv_ref, dg2_ref, dg3_ref):
        @pl.when(pl.program_id(0) == 0)
        def _():
            loss_ref[...] = jnp.zeros_like(loss_ref)
            dk_ref[...] = jnp.zeros_like(dk_ref)
            dv_ref[...] = jnp.zeros_like(dv_ref)
            dg2_ref[...] = jnp.zeros_like(dg2_ref)
            dg3_ref[...] = jnp.zeros_like(dg3_ref)

        g2, g3 = g2_ref[...], g3_ref[...]
        for sb in range(tm // sub):
            rs = pl.ds(sub * sb, sub)
            x1 = x_ref[rs, :] + jnp.dot(mix_ref[rs, :], wout_ref[...], preferred_element_type=F32)
            h2, r2 = _rms(x1, g2)
            h2b = h2.astype(BF16)
            h2_ref[rs, :] = h2b
            q = jnp.dot(h2b, wq_ref[...], preferred_element_type=F32).astype(BF16)
            probs, outs = [], []
            for hd in range(XH):
                hs = pl.ds(XD * hd, XD)
                s = _bdot_nt(q[:, XD * hd:XD * (hd + 1)], k_ref[:, hs]) * scale
                e = jnp.exp(s - jnp.max(s, axis=-1, keepdims=True))
                p = e / jnp.sum(e, axis=-1, keepdims=True)
                probs.append(p)
                outs.append(_bdot(p, v_ref[:, hs]))
            ob = jnp.concatenate(outs, axis=-1).astype(BF16)
            o_ref[rs, :] = ob
            x2 = x1 + jnp.dot(ob, wxo_ref[...], preferred_element_type=F32)
            y, r3 = _rms(x2, g3)
            diff = y - tgt_ref[rs, :]
            row_loss = jnp.sum(diff * diff, axis=-1, keepdims=True)
            loss_ref[...] += jnp.broadcast_to(jnp.sum(row_loss, axis=0, keepdims=True) * (0.5 / D), loss_ref.shape)

            dx2, dg3 = _rms_bwd(diff * (1.0 / D), x2, r3, g3)
            dg3_ref[...] += dg3
            dx2b = dx2.astype(BF16)
            dx2b_ref[rs, :] = dx2b
            do = _bdot_nt(dx2b, wxo_ref[...])
            dqs = []
            for hd in range(XH):
                hs = pl.ds(XD * hd, XD)
                p = probs[hd]
                do_h = do[:, XD * hd:XD * (hd + 1)]
                dv_ref[:, hs] += _bdot_tn(p, do_h)
                dp = _bdot_nt(do_h, v_ref[:, hs])
                ds = p * (dp - jnp.sum(dp * p, axis=-1, keepdims=True))
                dqs.append(_bdot(ds, k_ref[:, hs]) * scale)
                dk_ref[:, hs] += _bdot_tn(ds, q[:, XD * hd:XD * (hd + 1)]) * scale
            dq = jnp.concatenate(dqs, axis=-1).astype(BF16)
            dq_ref[rs, :] = dq
            dx1n, dg2 = _rms_bwd(_bdot_nt(dq, wq_ref[...]), x1, r2, g2)
            dg2_ref[...] += dg2
            dx1b = (dx2 + dx1n).astype(BF16)
            dx1b_ref[rs, :] = dx1b
            dmix_ref[rs, :] = _bdot_nt(dx1b, wout_ref[...]).astype(BF16)

    tok = lambda w: pl.BlockSpec((tm, w), lambda i: (i, 0))
    return pl.pallas_call(
        body, grid=(t // tm,),
        in_specs=[tok(D), tok(D), tok(MIX), _full((MIX, D), 1), _full((D, D), 1), _full((D, D), 1),
                  _full((n_mem, D), 1), _full((n_mem, D), 1), _full((1, D)), _full((1, D))],
        out_specs=[_full((1, D)), tok(MIX), tok(D), tok(D), tok(D), tok(D), tok(D),
                   _full((n_mem, D)), _full((n_mem, D)), _full((1, D)), _full((1, D))],
        out_shape=[jax.ShapeDtypeStruct((1, D), F32), jax.ShapeDtypeStruct((t, MIX), BF16),
                   jax.ShapeDtypeStruct((t, D), BF16),
                   jax.ShapeDtypeStruct((t, D), BF16), jax.ShapeDtypeStruct((t, D), BF16),
                   jax.ShapeDtypeStruct((t, D), BF16), jax.ShapeDtypeStruct((t, D), BF16),
                   jax.ShapeDtypeStruct((n_mem, D), F32), jax.ShapeDtypeStruct((n_mem, D), F32),
                   jax.ShapeDtypeStruct((1, D), F32), jax.ShapeDtypeStruct((1, D), F32)],
        compiler_params=_cp(("arbitrary",), VMEM_LIMIT), name="tail")(x, tgt, mixin, wout, wq, wxo, k, v, g2, g3)


def _mem_bwd(mem, gm, dk, dv, wkv_f):
    def body(mem_ref, gm_ref, dk_ref, dv_ref, w_ref, dw_ref, dwb_ref, dgm_ref):
        mem_v = mem_ref[...]
        m, rm = _rms(mem_v, gm_ref[...])
        mb = m.astype(BF16)
        dm = jnp.zeros_like(mem_v)
        for j in range(N_CHIP):
            src = dk_ref if j < 2 else dv_ref
            dkv = src[:, pl.ds(KV_BLK * (j % 2), KV_BLK)].astype(BF16)
            dw = _bdot_tn(mb, dkv)
            dw_ref[j] = dw
            dwb_ref[j] = dw.astype(BF16)
            dm = dm + _bdot_nt(dkv, w_ref[j])
        dgm_ref[...] = jnp.sum(dm * mem_v * rm, axis=0, keepdims=True)

    return pl.pallas_call(
        body, out_shape=[jax.ShapeDtypeStruct((N_CHIP, D, KV_BLK), F32), jax.ShapeDtypeStruct((N_CHIP, D, KV_BLK), BF16),
                         jax.ShapeDtypeStruct((1, D), F32)],
        compiler_params=_cp(None, VMEM_LIMIT), name="mem_bwd")(mem, gm, dk, dv, wkv_f)


def _mixer_bwd(proj, dmix, cw8, lng, lnb, wc, wct, bsb, win_f, x, dx1, g1, rows123, row7, tm=256):
    t = proj.shape[0]
    nt = t // tm
    nch = tm // CH
    hb = 16
    pair = 2 * CH
    assert pair == IN_PIECE

    def body(p_ref, pgc_ref, pxa_ref, dm_ref, cw_ref, lng_ref, lnb_ref, wc_ref, wct_ref, bsb_ref, w_ref, x_ref,
             dx1_ref, g1_ref, r1_ref, r2_ref, r3_ref, r7_ref, dp_ref, sv_ref, dwc_ref, gx_ref,
             next_ref, dh_ref):
        i = pl.program_id(0)
        dg1_ref, dlng_ref, dlnb_ref, dbs_ref = (sv_ref.at[pl.ds(r, 1)] for r in (0, 4, 5, 6))
        dcw_ref = sv_ref.at[pl.ds(8, 8)]

        @pl.when(i == 0)
        def _():
            next_ref[...] = jnp.zeros_like(next_ref)
            sv_ref[...] = jnp.zeros_like(sv_ref)
            dwc_ref[...] = jnp.zeros_like(dwc_ref)
            for r, ref in ((1, r1_ref), (2, r2_ref), (3, r3_ref), (7, r7_ref)):
                sv_ref[r:r + 1, :] = ref[...]

        first_tile = i == nt - 1
        rows = lax.broadcasted_iota(jnp.int32, (tm, CH), 0)
        ones8 = jnp.ones((8, CH), BF16)
        for s in range(HEADS):
            cs = pl.ds(CH * s, CH)

            def slab(k):
                return p_ref[:, pl.ds(k * SLAB + CH * s, CH)].astype(F32)

            gb, gc, xa, za = slab(0), slab(1), slab(2), slab(3)
            da = dm_ref[:, cs].astype(F32)
            cx = gc * xa
            cxp = pgc_ref[:, cs].astype(F32) * pxa_ref[:, cs].astype(F32)
            cxp = jnp.where(first_tile, jnp.zeros_like(cxp), cxp)
            p6 = jnp.broadcast_to(cxp[hb - 2:hb - 1, :], (tm, CH))
            p7 = jnp.broadcast_to(cxp[hb - 1:hb, :], (tm, CH))
            c1 = jnp.where(rows == 0, p7, pltpu.roll(cx, 1, 0))
            c2 = jnp.where(rows == 0, p6, jnp.where(rows == 1, p7, pltpu.roll(cx, 2, 0)))
            w0, w1, w2 = cw_ref[0:1, cs], cw_ref[1:2, cs], cw_ref[2:3, cs]
            cv = w0 * c2 + w1 * c1 + w2 * cx
            sg = _sigmoid(za)
            sa = za * sg
            dcv = da * gb * sa
            dp_ref[:, pl.ds(0 * SLAB + CH * s, CH)] = (da * cv * sa).astype(BF16)
            dp_ref[:, pl.ds(3 * SLAB + CH * s, CH)] = (da * gb * cv * (sg * (1.0 + za * (1.0 - sg)))).astype(BF16)
            n0 = jnp.broadcast_to(next_ref[0:1, cs], (tm, CH))
            n1 = jnp.broadcast_to(next_ref[1:2, cs], (tm, CH))
            u1 = jnp.where(rows == tm - 1, n0, pltpu.roll(dcv, tm - 1, 0))
            u2 = jnp.where(rows == tm - 2, n0, jnp.where(rows == tm - 1, n1, pltpu.roll(dcv, tm - 2, 0)))
            next_ref[:, cs] = dcv[0:8, :]
            dcx = w2 * dcv + w1 * u1 + w0 * u2
            dp_ref[:, pl.ds(1 * SLAB + CH * s, CH)] = (dcx * xa).astype(BF16)
            dp_ref[:, pl.ds(2 * SLAB + CH * s, CH)] = (dcx * gc).astype(BF16)
            dcw_ref[0:1, cs] += jnp.sum(dcv * c2, axis=0, keepdims=True)
            dcw_ref[1:2, cs] += jnp.sum(dcv * c1, axis=0, keepdims=True)
            dcw_ref[2:3, cs] += jnp.sum(dcv * cx, axis=0, keepdims=True)

            u, v, zb = slab(4), slab(5), slab(6)
            db = dm_ref[:, pl.ds(SLAB + CH * s, CH)].astype(F32)
            ug, ugrad = _gelu_parts(u)
            vg, vgrad = _gelu_parts(v)
            dlt = vg - jnp.mean(vg, axis=-1, keepdims=True)
            rstd = lax.rsqrt(jnp.mean(dlt * dlt, axis=-1, keepdims=True) + EPS)
            vhat = dlt * rstd
            lg = lng_ref[:, cs]
            vn = (vhat * lg + lnb_ref[:, cs]).astype(BF16)
            sgb = _sigmoid(zb)
            szb = zb * sgb
            sps, dvns = [], []
            dbs = jnp.zeros((8, CH), F32)
            dwc = jnp.zeros((CH, CH), F32)
            for c in range(nch):
                rs = slice(CH * c, CH * (c + 1))
                sp = jnp.dot(wc_ref[s], vn[rs], preferred_element_type=F32) + bsb_ref[s]
                dsp = (db[rs] * ug[rs] * szb[rs]).astype(BF16)
                dbs = dbs + lax.dot_general(ones8, dsp, (((1,), (1,)), ((), ())), preferred_element_type=F32)
                dwc = dwc + lax.dot_general(dsp, vn[rs], (((1,), (1,)), ((), ())), preferred_element_type=F32)
                dvns.append(jnp.dot(wct_ref[s], dsp, preferred_element_type=F32))
                sps.append(sp)
            sp = jnp.concatenate(sps, axis=0)
            dvn = jnp.concatenate(dvns, axis=0)
            dbs_ref[:, cs] += dbs[0:1]
            dwc_ref[s] += dwc
            dlng_ref[:, cs] += jnp.sum(dvn * vhat, axis=0, keepdims=True)
            dlnb_ref[:, cs] += jnp.sum(dvn, axis=0, keepdims=True)
            dvhat = dvn * lg
            dvg = rstd * (dvhat - jnp.mean(dvhat, axis=-1, keepdims=True)
                          - vhat * jnp.mean(dvhat * vhat, axis=-1, keepdims=True))
            dp_ref[:, pl.ds(4 * SLAB + CH * s, CH)] = (db * sp * szb * ugrad).astype(BF16)
            dp_ref[:, pl.ds(5 * SLAB + CH * s, CH)] = (dvg * vgrad).astype(BF16)
            dp_ref[:, pl.ds(6 * SLAB + CH * s, CH)] = (db * ug * sp * (sgb * (1.0 + zb * (1.0 - sgb)))).astype(BF16)

            if s % 2 == 1:
                part = None
                for k in range(N_SLAB):
                    col = k * SLAB + pair * (s // 2)
                    blk, off = divmod(col, IN_BLK)
                    term = lax.dot_general(dp_ref[:, pl.ds(col, pair)], w_ref[blk, off // IN_PIECE],
                                           (((1,), (1,)), ((), ())), preferred_element_type=F32)
                    part = term if part is None else part + term
                if s == 1:
                    dh_ref[...] = part
                else:
                    dh_ref[...] += part

        xv = x_ref[...]
        r = lax.rsqrt(jnp.mean(xv * xv, axis=-1, keepdims=True) + EPS)
        dxn, dg = _rms_bwd(dh_ref[...], xv, r, g1_ref[...])
        gx_ref[...] = dx1_ref[...].astype(F32) + dxn
        dg1_ref[...] += dg

        @pl.when(i == nt - 1)
        def _():
            tril = lax.broadcasted_iota(jnp.int32, (CH, CH), 0) >= lax.broadcasted_iota(jnp.int32, (CH, CH), 1)
            for s in range(HEADS):
                dwc_ref[s] = jnp.where(tril, dwc_ref[s], 0.0)

    rev = lambda i: nt - 1 - i
    halo = lambda col: pl.BlockSpec((hb, SLAB), lambda i: (jnp.maximum(rev(i) * (tm // hb) - 1, 0), col))
    tok = lambda w: pl.BlockSpec((tm, w), lambda i: (rev(i), 0))
    return pl.pallas_call(
        body, grid=(nt,),
        in_specs=[tok(IN_DIM), halo(1), halo(2), tok(MIX), _full((8, D)), _full((1, D)), _full((1, D)),
                  _full((HEADS, CH, CH)), _full((HEADS, CH, CH)), _full((HEADS, CH, CH)),
                  _full((N_CHIP, N_PIECE, D, IN_PIECE), 1), tok(D), tok(D)] + [_full((1, D))] * 5,
        out_specs=[tok(IN_DIM), _full((16, D)), _full((HEADS, CH, CH)), tok(D)],
        out_shape=[jax.ShapeDtypeStruct((t, IN_DIM), BF16), jax.ShapeDtypeStruct((16, D), F32),
                   jax.ShapeDtypeStruct((HEADS, CH, CH), F32), jax.ShapeDtypeStruct((t, D), F32)],
        scratch_shapes=[pltpu.VMEM((8, D), F32), pltpu.VMEM((tm, D), F32)],
        compiler_params=_cp(("arbitrary",), VMEM_LIMIT), name="mixer_bwd")(
            proj, proj, proj, dmix, cw8, lng, lnb, wc, wct, bsb, win_f, x, dx1, g1, *rows123, row7)


def _grad_matmul(a, b, after, *, name, tk=1024):
    t, m = a.shape
    n = b.shape[1]
    nk = t // tk

    def body(a_ref, b_ref, after_ref, o_ref, ob_ref):
        kk = pl.program_id(0)
        part = lax.dot_general(a_ref[...], b_ref[...], (((0,), (0,)), ((), ())), preferred_element_type=F32)

        @pl.when(kk == 0)
        def _():
            o_ref[...] = part

        @pl.when(kk > 0)
        def _():
            o_ref[...] += part

        @pl.when(kk == nk - 1)
        def _():
            ob_ref[...] = o_ref[...].astype(BF16)

    o_spec = pl.BlockSpec((m, n), lambda k: (0, 0))
    o32, o16 = pl.pallas_call(
        body, grid=(nk,), in_specs=[pl.BlockSpec((tk, m), lambda k: (k, 0)), pl.BlockSpec((tk, n), lambda k: (k, 0)), ANY],
        out_specs=[o_spec, o_spec], out_shape=[jax.ShapeDtypeStruct((m, n), F32), jax.ShapeDtypeStruct((m, n), BF16)],
        compiler_params=_cp(("arbitrary",), VMEM_LIMIT), name=name)(a, b, after)
    return o32.reshape(N_CHIP, m // N_CHIP, n), o16.reshape(N_CHIP, m // N_CHIP, n)


def _coords():
    x, y, c = lax.axis_index("x"), lax.axis_index("y"), lax.axis_index("c")
    chips = [(1 - x, y), (x, 1 - y), (1 - x, 1 - y)]
    return x, y, c, chips


def _pair_reduce(c_idx, grads, grads_b, smalls, name):
    ng, ns = len(grads), len(smalls)
    halves = [g.shape[1] // 2 for g in grads]

    def body(c_ref, *refs):
        g_in, gb_any = refs[:ng], refs[ng:2 * ng]
        s_own, s_any = refs[2 * ng:2 * ng + ns], refs[2 * ng + ns:2 * ng + 2 * ns]
        o = refs[2 * ng + 2 * ns:4 * ng + 3 * ns]
        lands = refs[4 * ng + 3 * ns:5 * ng + 4 * ns]
        send, recv = refs[5 * ng + 4 * ns:]
        x, y, c, _ = _coords()
        j = pl.program_id(0)

        def big(i, blk):
            return pltpu.make_async_remote_copy(
                src_ref=gb_any[i].at[blk, pl.ds((1 - c) * halves[i], halves[i])], dst_ref=lands[i].at[blk],
                send_sem=send.at[i, blk], recv_sem=recv.at[i, blk], device_id=(x, y, 1 - c), device_id_type=MESH)

        def small(i):
            return pltpu.make_async_remote_copy(
                src_ref=s_any[i].at[1 - c], dst_ref=lands[ng + i],
                send_sem=send.at[ng + i, 0], recv_sem=recv.at[ng + i, 0], device_id=(x, y, 1 - c), device_id_type=MESH)

        @pl.when(j == 0)
        def _():
            for blk in range(N_CHIP):
                for i in range(ng):
                    big(i, blk).start()
            for i in range(ns):
                small(i).start()

        for i in range(ng):
            big(i, j).wait_recv()
            tot = g_in[i][...] + lands[i][j].astype(F32)
            o[i][...] = tot
            o[ng + i][...] = tot.astype(BF16)

        @pl.when(j == N_CHIP - 1)
        def _():
            for i in range(ns):
                small(i).wait_recv()
                o[2 * ng + i][...] = s_own[i][...] + lands[ng + i][...]
                small(i).wait_send()
            for blk in range(N_CHIP):
                for i in range(ng):
                    big(i, blk).wait_send()

    in_specs = [pl.BlockSpec((None, None, halves[i], g.shape[2]), lambda b, c: (b, c[0], 0, 0)) for i, g in enumerate(grads)]
    in_specs += [ANY] * ng
    in_specs += [pl.BlockSpec((None, s.shape[0] // 2, s.shape[1]), lambda b, c: (c[0], 0, 0)) for s in smalls]
    in_specs += [ANY] * ns
    blk = [pl.BlockSpec((None, halves[i], g.shape[2]), lambda b, c: (b, 0, 0)) for i, g in enumerate(grads)]
    out_specs = blk + blk + [pl.BlockSpec((s.shape[0] // 2, s.shape[1]), lambda b, c: (0, 0)) for s in smalls]
    out_shape = [jax.ShapeDtypeStruct((N_CHIP, halves[i], g.shape[2]), F32) for i, g in enumerate(grads)]
    out_shape += [jax.ShapeDtypeStruct((N_CHIP, halves[i], g.shape[2]), BF16) for i, g in enumerate(grads)]
    out_shape += [jax.ShapeDtypeStruct((s.shape[0] // 2, s.shape[1]), F32) for s in smalls]
    scratch = [pltpu.VMEM((N_CHIP, halves[i], g.shape[2]), BF16) for i, g in enumerate(grads)]
    scratch += [pltpu.VMEM((s.shape[0] // 2, s.shape[1]), F32) for s in smalls]
    scratch += [pltpu.SemaphoreType.DMA((ng + ns, N_CHIP)), pltpu.SemaphoreType.DMA((ng + ns, N_CHIP))]
    grads4 = [g.reshape(N_CHIP, 2, halves[i], g.shape[2]) for i, g in enumerate(grads)]
    smalls3 = [s.reshape(2, s.shape[0] // 2, s.shape[1]) for s in smalls]
    return pl.pallas_call(
        body, out_shape=out_shape,
        grid_spec=pltpu.PrefetchScalarGridSpec(num_scalar_prefetch=1, grid=(N_CHIP,), in_specs=in_specs,
                                               out_specs=out_specs, scratch_shapes=scratch),
        compiler_params=_cp(("arbitrary",), VMEM_LIMIT), name=name)(c_idx, *grads4, *grads_b, *smalls3, *smalls3)


def _grad_matmul_pair(c_idx, a, b, smalls, after, *, name, tk=2048):
    t, m = a.shape
    bn = b.shape[1] // N_CHIP
    nk = t // tk
    hr = m // 2
    ns = len(smalls)

    def body(c_ref, a_ref, b_ref, *refs):
        s_own, s_any = refs[:ns], refs[ns:2 * ns]
        o32, o16 = refs[2 * ns + 1], refs[2 * ns + 2]
        o_small = refs[2 * ns + 3:3 * ns + 3]
        acc, tb, land, st16 = refs[3 * ns + 3:3 * ns + 7]
        s_land, s_stage = refs[3 * ns + 7:4 * ns + 7], refs[4 * ns + 7:5 * ns + 7]
        send, recv, loc = refs[5 * ns + 7:]
        x, y, c, _ = _coords()
        sibling = dict(device_id=(x, y, 1 - c), device_id_type=MESH)
        j, kk = pl.program_id(0), pl.program_id(1)
        mine = pl.ds(pl.multiple_of(c * hr, hr), hr)
        theirs = pl.ds(pl.multiple_of((1 - c) * hr, hr), hr)

        def to_sibling(blk):
            return pltpu.make_async_remote_copy(src_ref=tb, dst_ref=land.at[blk], send_sem=send.at[blk],
                                                recv_sem=recv.at[blk], **sibling)

        def small(i):
            return pltpu.make_async_remote_copy(src_ref=s_any[i].at[1 - c], dst_ref=s_land[i], send_sem=send.at[N_CHIP + i],
                                                recv_sem=recv.at[N_CHIP + i], **sibling)

        def written(blk):
            return (pltpu.make_async_copy(acc.at[blk % 2, mine], o32.at[blk], loc.at[0]),
                    pltpu.make_async_copy(st16, o16.at[blk], loc.at[1]))

        def finish(blk):
            to_sibling(blk).wait_recv()

            @pl.when(blk > 0)
            def _():
                for cp in written(blk - 1):
                    cp.wait()

            tot = acc[blk % 2, mine, :] + land[blk].astype(F32)
            acc[blk % 2, mine, :] = tot
            st16[...] = tot.astype(BF16)
            for cp in written(blk):
                cp.start()

        def small_out(i):
            return pltpu.make_async_copy(s_stage[i], o_small[i], loc.at[2 + i])

        @pl.when((j == 0) & (kk == 0))
        def _():
            for i in range(ns):
                small(i).start()

        @pl.when((j == 1) & (kk == 0))
        def _():
            for i in range(ns):
                small(i).wait_recv()
                s_stage[i][...] = s_own[i][...] + s_land[i][...]
                small_out(i).start()

        @pl.when((j > 0) & (kk == 0))
        def _():
            finish(j - 1)

        part = lax.dot_general(a_ref[...], b_ref[...], (((0,), (0,)), ((), ())), preferred_element_type=F32)
        slot = lax.rem(j, 2)

        @pl.when(kk == 0)
        def _():
            acc[slot] = part

        @pl.when(kk > 0)
        def _():
            acc[slot] += part

        @pl.when(kk == nk - 1)
        def _():
            @pl.when(j > 0)
            def _():
                to_sibling(j - 1).wait_send()

            tb[...] = acc[slot, theirs, :].astype(BF16)
            to_sibling(j).start()

        @pl.when((j == N_CHIP - 1) & (kk == nk - 1))
        def _():
            finish(j)
            for i in range(ns):
                small_out(i).wait()
                small(i).wait_send()
            for cp in written(j):
                cp.wait()
            to_sibling(j).wait_send()

    halves = [(s.shape[0] // 2, s.shape[1]) for s in smalls]
    in_specs = [pl.BlockSpec((tk, m), lambda j, k, c: (k, 0)), pl.BlockSpec((tk, bn), lambda j, k, c: (k, j))]
    in_specs += [pl.BlockSpec((None,) + h, lambda j, k, c: (c[0], 0, 0)) for h in halves] + [ANY] * ns + [ANY]
    out_shape = [jax.ShapeDtypeStruct((N_CHIP, hr, bn), F32), jax.ShapeDtypeStruct((N_CHIP, hr, bn), BF16)]
    out_shape += [pltpu.HBM(h, F32) for h in halves]
    scratch = [pltpu.VMEM((2, m, bn), F32), pltpu.VMEM((hr, bn), BF16),
               pltpu.VMEM((N_CHIP, hr, bn), BF16), pltpu.VMEM((hr, bn), BF16)]
    scratch += [pltpu.VMEM(h, F32) for h in halves] * 2
    scratch += [pltpu.SemaphoreType.DMA((N_CHIP + ns,)), pltpu.SemaphoreType.DMA((N_CHIP + ns,)),
                pltpu.SemaphoreType.DMA((2 + ns,))]
    smalls3 = [s.reshape((2,) + h) for s, h in zip(smalls, halves)]
    outs = pl.pallas_call(
        body, out_shape=out_shape,
        grid_spec=pltpu.PrefetchScalarGridSpec(num_scalar_prefetch=1, grid=(N_CHIP, nk), in_specs=in_specs,
                                               out_specs=[ANY, ANY] + [_HBM] * ns, scratch_shapes=scratch),
        compiler_params=_cp(("arbitrary", "arbitrary"), VMEM_LIMIT), name=name)(c_idx, a, b, *smalls3, *smalls3, after)
    return outs[0], outs[1], list(outs[2:])


_HBM = pl.BlockSpec(memory_space=pltpu.HBM)
_SEM = pl.BlockSpec(memory_space=pltpu.SEMAPHORE)


def _split_copies(ins, lands, ng, send, recv, arriving):
    x, y, c, chips = _coords()
    b = 2 * x + y
    copies = []
    for i in range(len(ins)):
        for k in range(3):
            blk = 2 * chips[k][0] + chips[k][1]
            src, dst, got = (ins[i].at[blk], lands[i].at[k], lands[i].at[k]) if i < ng else (ins[i], lands[i].at[b], lands[i].at[blk])
            sems = dict(send_sem=send.at[3 * i + k], recv_sem=recv.at[3 * i + k], device_id=(*chips[k], c), device_id_type=MESH)
            if arriving:
                copies.append(pltpu.make_async_remote_copy(src_ref=got, dst_ref=got, **sems))
            else:
                copies.append(pltpu.make_async_remote_copy(src_ref=src, dst_ref=dst, **sems))
    return copies


def _exchange_begin(sums_b, smalls, name):
    ng, n = len(sums_b), len(sums_b) + len(smalls)
    srcs = list(sums_b) + list(smalls)
    lands = [lax.empty((3,) + g.shape[1:], g.dtype) for g in sums_b] + [lax.empty((N_CHIP,) + s.shape, s.dtype) for s in smalls]

    def body(*refs):
        ins, land_refs = refs[:n], refs[n:2 * n]
        send, recv = refs[2 * n], refs[2 * n + 1]
        token = refs[4 * n + 2]
        for cp in _split_copies(ins, land_refs, ng, send, recv, False):
            cp.start()
        token[...] = jnp.zeros_like(token)

    hbm = lambda a: pltpu.HBM(a.shape, a.dtype)
    outs = pl.pallas_call(
        body, name=name,
        out_shape=(pltpu.SemaphoreType.DMA((3 * n,)), pltpu.SemaphoreType.DMA((3 * n,)), *[hbm(a) for a in srcs + lands],
                   jax.ShapeDtypeStruct((8, 128), F32)),
        in_specs=[_HBM] * (2 * n), out_specs=(_SEM, _SEM, *[_HBM] * (2 * n), pl.BlockSpec(memory_space=pltpu.VMEM)),
        input_output_aliases={i: 2 + i for i in range(2 * n)},
        compiler_params=pltpu.CompilerParams(has_side_effects=pltpu.SideEffectType.DATAFLOW_SIDE_EFFECTING),
    )(*[pltpu.with_memory_space_constraint(a, pltpu.HBM) for a in srcs + lands])
    return outs[0], outs[1], list(outs[2:2 + n]), list(outs[2 + n:2 + 2 * n]), outs[2 + 2 * n]


def _exchange_end(send, recv, srcs, lands, ng, which, after, name):
    n = len(srcs)
    after = list(after)

    def body(*refs):
        ins, land_refs = refs[:n], refs[n:2 * n]
        send_ref, recv_ref = refs[2 * n], refs[2 * n + 1]
        outgoing = _split_copies(ins, land_refs, ng, send_ref, recv_ref, False)
        arriving = _split_copies(ins, land_refs, ng, send_ref, recv_ref, True)
        for i in which:
            for cp in outgoing[3 * i:3 * i + 3]:
                cp.wait_send()
        for i in which:
            for cp in arriving[3 * i:3 * i + 3]:
                cp.wait_recv()

    hbm = lambda a: pltpu.HBM(a.shape, a.dtype)
    outs = pl.pallas_call(
        body, name=name, out_shape=tuple(hbm(a) for a in list(srcs) + list(lands)),
        in_specs=[_HBM] * (2 * n) + [_SEM, _SEM] + [ANY] * len(after), out_specs=tuple([_HBM] * (2 * n)),
        input_output_aliases={i: i for i in range(2 * n)},
        compiler_params=pltpu.CompilerParams(has_side_effects=pltpu.SideEffectType.DATAFLOW_SIDE_EFFECTING),
    )(*srcs, *lands, send, recv, *after)
    return list(outs[:n]), list(outs[n:])


def _chip_reduce(bc_idx, sums, recvd, smalls_slots, smalls_own, name, steps=4):
    ng, ns = len(sums), len(smalls_slots)
    n = ng + ns
    assert steps >= 2
    halves = [g.shape[1] for g in sums] + [s.shape[1] for s in smalls_slots]
    rows = [g.shape[1] // steps for g in sums]

    def body(bc_ref, *refs):
        own, rx = refs[:ng], refs[ng:2 * ng]
        sl = refs[2 * ng:2 * ng + ns]
        sl_own = refs[2 * ng + ns:2 * ng + 2 * ns]
        o = refs[2 * ng + 2 * ns:2 * ng + 2 * ns + n]
        tiles = refs[2 * ng + 2 * ns + n:2 * ng + 2 * ns + 2 * n]
        keep, send, recv = refs[2 * ng + 2 * ns + 2 * n:]
        x, y, c, _ = _coords()
        sibling = dict(device_id=(x, y, 1 - c), device_id_type=MESH)
        r = pl.program_id(0)

        def writes(i, step, slot):
            dst = o[i].at[pl.ds(c * halves[i] + step * rows[i], rows[i])]
            return (pltpu.make_async_copy(tiles[i].at[slot], dst, keep.at[i, slot]),
                    pltpu.make_async_remote_copy(src_ref=tiles[i].at[slot], dst_ref=dst, send_sem=send.at[i, slot],
                                                 recv_sem=recv.at[i, step], **sibling))

        def small_writes(i):
            dst = o[i].at[pl.ds(c * halves[i], halves[i])]
            return (pltpu.make_async_copy(tiles[i], dst, keep.at[i, 0]),
                    pltpu.make_async_remote_copy(src_ref=tiles[i], dst_ref=dst, send_sem=send.at[i, 0],
                                                 recv_sem=recv.at[i, 0], **sibling))

        def arriving(i, step, nrows):
            dst = o[i].at[pl.ds((1 - c) * halves[i] + step * nrows, nrows)]
            return pltpu.make_async_remote_copy(src_ref=dst, dst_ref=dst, send_sem=send.at[i, 0], recv_sem=recv.at[i, step],
                                                **sibling)

        def finish(step, slot):
            for i in range(ng):
                local, remote = writes(i, step, slot)
                local.wait()
                remote.wait_send()

        @pl.when(r >= 2)
        def _():
            finish(r - 2, r % 2)

        for i in range(ng):
            tot = own[i][...]
            for j in range(3):
                tot = tot + rx[i][j].astype(F32)
            tiles[i][r % 2] = tot
            for cp in writes(i, r, r % 2):
                cp.start()

        @pl.when(r == 0)
        def _():
            for i in range(ns):
                term = [jnp.where(bc_ref[0] == kk, sl_own[i][...], sl[i][kk]) for kk in range(N_CHIP)]
                tiles[ng + i][...] = ((term[0] + term[1]) + term[2]) + term[3]
                for cp in small_writes(ng + i):
                    cp.start()

        @pl.when(r == steps - 1)
        def _():
            finish(steps - 2, (steps - 2) % 2)
            finish(steps - 1, (steps - 1) % 2)
            for i in range(ns):
                local, remote = small_writes(ng + i)
                local.wait()
                remote.wait_send()
                arriving(ng + i, 0, halves[ng + i]).wait_recv()
            for i in range(ng):
                for step in range(steps):
                    arriving(i, step, rows[i]).wait_recv()

    in_specs = [pl.BlockSpec((None, rows[i], g.shape[2]), lambda r, bc: (bc[0], r, 0)) for i, g in enumerate(sums)]
    in_specs += [pl.BlockSpec((3, rows[i], g.shape[2]), lambda r, bc: (0, r, 0)) for i, g in enumerate(sums)]
    in_specs += [pl.BlockSpec(s.shape, lambda r, bc: (0, 0, 0)) for s in smalls_slots]
    in_specs += [pl.BlockSpec(s.shape[1:], lambda r, bc: (0, 0)) for s in smalls_slots]
    out_shape = [jax.ShapeDtypeStruct((2 * g.shape[1], g.shape[2]), F32) for g in sums]
    out_shape += [jax.ShapeDtypeStruct((2 * s.shape[1], s.shape[2]), F32) for s in smalls_slots]
    scratch = [pltpu.VMEM((2, rows[i], g.shape[2]), F32) for i, g in enumerate(sums)]
    scratch += [pltpu.VMEM(s.shape[1:], F32) for s in smalls_slots]
    scratch += [pltpu.SemaphoreType.DMA((n, 2)), pltpu.SemaphoreType.DMA((n, 2)), pltpu.SemaphoreType.DMA((n, steps))]
    return list(pl.pallas_call(
        body, out_shape=out_shape,
        grid_spec=pltpu.PrefetchScalarGridSpec(num_scalar_prefetch=1, grid=(steps,), in_specs=in_specs,
                                               out_specs=[ANY] * n, scratch_shapes=scratch),
        compiler_params=_cp(("arbitrary",), VMEM_LIMIT), name=name)(bc_idx, *sums, *recvd, *smalls_slots, *smalls_own))


def _adamw_math(w, g, m, v):
    m2 = ADAM_B1 * m + (1.0 - ADAM_B1) * g
    v2 = ADAM_B2 * v + (1.0 - ADAM_B2) * (g * g)
    m_hat = m2 / (1.0 - ADAM_B1 ** ADAM_STEP)
    v_hat = v2 / (1.0 - ADAM_B2 ** ADAM_STEP)
    delta = -ADAM_LR * (m_hat / (jnp.sqrt(v_hat) + ADAM_EPS) + ADAM_WD * w)
    return delta, m2, v2


def _adamw_big(ws, gs, ms, vs, name, steps=8):
    n = len(ws)

    def body(*refs):
        for i in range(n):
            w_ref, g_ref, m_ref, v_ref = (refs[k * n + i] for k in range(4))
            d_ref, m2_ref, v2_ref, g2_ref = (refs[(4 + k) * n + i] for k in range(4))
            gv = g_ref[...]
            d_ref[...], m2_ref[...], v2_ref[...] = _adamw_math(w_ref[...], gv, m_ref[...], v_ref[...])
            g2_ref[...] = gv

    specs = [pl.BlockSpec((w.shape[0] // steps, w.shape[1]), lambda i: (i, 0)) for w in ws]
    shapes = [jax.ShapeDtypeStruct(w.shape, F32) for w in ws]
    outs = pl.pallas_call(
        body, grid=(steps,), in_specs=specs * 4, out_specs=specs * 4, out_shape=shapes * 4,
        compiler_params=_cp(("parallel",), VMEM_LIMIT), name=name)(*ws, *gs, *ms, *vs)
    return [tuple(outs[k * n + i] for k in range(4)) for i in range(n)]


def _adamw_small(b_idx, sv, sw, vecs, conv, ws):
    nv = len(vecs)
    cols = conv[0].shape[1]

    def body(b_ref, sv_ref, sw_ref, *refs):
        ins, outs = refs[:3 * nv + 6], refs[3 * nv + 6:]
        for i in range(nv):
            g = sv_ref[i:i + 1, :]
            w_ref, m_ref, v_ref = ins[3 * i:3 * i + 3]
            d_ref, m2_ref, v2_ref, g_ref = outs[4 * i:4 * i + 4]
            d_ref[...], m2_ref[...], v2_ref[...] = _adamw_math(w_ref[...], g, m_ref[...], v_ref[...])
            g_ref[...] = g
        g = sv_ref[8:8 + conv[0].shape[0], pl.ds(pl.multiple_of(b_ref[0] * cols, cols), cols)]
        w_ref, m_ref, v_ref = ins[3 * nv:3 * nv + 3]
        d_ref, m2_ref, v2_ref, g_ref = outs[4 * nv:4 * nv + 4]
        d_ref[...], m2_ref[...], v2_ref[...] = _adamw_math(w_ref[...], g, m_ref[...], v_ref[...])
        g_ref[...] = g
        w_ref, m_ref, v_ref = ins[3 * nv + 3:]
        d_ref, m2_ref, v2_ref, g_ref, one_ref = outs[4 * nv + 4:]
        g = sw_ref[...]
        d_ref[...], m2_ref[...], v2_ref[...] = _adamw_math(w_ref[...], g, m_ref[...], v_ref[...])
        g_ref[...] = g
        one_ref[...] = sv_ref[nv:nv + 1, 0:1]

    flat = [a for grp in vecs for a in grp] + list(conv) + list(ws)
    out_shape = [jax.ShapeDtypeStruct(grp[0].shape, F32) for grp in list(vecs) + [conv, ws] for _ in range(4)]
    out_shape += [jax.ShapeDtypeStruct((1, 1), F32)]
    vmem = pl.BlockSpec(memory_space=pltpu.VMEM)
    outs = pl.pallas_call(
        body, out_shape=out_shape, in_specs=[pl.BlockSpec(memory_space=pltpu.SMEM)] + [vmem] * (2 + len(flat)),
        out_specs=[vmem] * len(out_shape), name="adamw_small")(b_idx, sv, sw, *flat)
    return [tuple(outs[4 * i:4 * i + 4]) for i in range(nv + 2)], outs[4 * nv + 8]


def kernel(x, mem, norm_mix_g, w_in, conv_w, gm_ln_g, gm_ln_b, gm_ws, gm_bs, w_out, norm_x_g, norm_mem_g, w_q, w_kv, w_xo, norm_final_g, loss_target, m_norm_mix_g, m_w_in, m_conv_w, m_gm_ln_g, m_gm_ln_b, m_gm_ws, m_gm_bs, m_w_out, m_norm_x_g, m_norm_mem_g, m_w_q, m_w_kv, m_w_xo, m_norm_final_g, v_norm_mix_g, v_w_in, v_conv_w, v_gm_ln_g, v_gm_ln_b, v_gm_ws, v_gm_bs, v_w_out, v_norm_x_g, v_norm_mem_g, v_w_q, v_w_kv, v_w_xo, v_norm_final_g):
    t = x.shape[1]
    xi = lax.axis_index("x")
    yi = lax.axis_index("y")
    ci = lax.axis_index("c")
    b_idx = jnp.reshape(2 * xi + yi, (1,)).astype(jnp.int32)
    c_idx = jnp.reshape(ci, (1,)).astype(jnp.int32)

    x2d, mem2d, tgt = x[0], mem[0], loss_target[0]
    big = [w_in[0], w_out[0], w_q[0], w_kv[0], w_xo[0]]
    big_m = [m_w_in[0], m_w_out[0], m_w_q[0], m_w_kv[0], m_w_xo[0]]
    big_v = [v_w_in[0], v_w_out[0], v_w_q[0], v_w_kv[0], v_w_xo[0]]
    g3 = norm_final_g.reshape(1, D)

    def pad8(a):
        return jnp.pad(a, ((0, 8 - a.shape[0]), (0, 0)))

    own_blocks = _cast_shards(b_idx, big)

    tril = jnp.tril(jnp.ones((CH, CH), bool))
    wc32 = jnp.where(tril[None], gm_ws[0], 0.0)
    wc = wc32.astype(BF16)
    wct = jnp.swapaxes(wc32, 1, 2).astype(BF16)
    bsb = jnp.broadcast_to(gm_bs[0][:, :, None], (HEADS, CH, CH))

    proj, hb, win_f, cw8, (wq_f,) = _proj_gather(
        b_idx, x2d, norm_mix_g, own_blocks[0], pad8(conv_w[0]), [own_blocks[2]])
    mixin, (wout_f, wkv_f, wxo_f) = _mixer_fwd(
        proj, cw8, gm_ln_g, gm_ln_b, wc, bsb, [own_blocks[1], own_blocks[3], own_blocks[4]])
    wout2, wq2, wxo2 = wout_f.reshape(MIX, D), wq_f.reshape(D, D), wxo_f.reshape(D, D)
    k, v = _mem_fwd(mem2d, norm_mem_g, wkv_f)

    (loss_row, dmix, dx1b, h2b, dq, ob, dx2b, dk, dv, dg2, dg3) = _tail(
        x2d, tgt, mixin, wout2, wq2, wxo2, k, v, norm_x_g, g3)
    dwkv, dwkv_b, dgm = _mem_bwd(mem2d, norm_mem_g, dk, dv, wkv_f)
    dproj, sv, dwc, grad_x = _mixer_bwd(
        proj, dmix, cw8, gm_ln_g, gm_ln_b, wc, wct, bsb, win_f, x2d, dx1b, norm_mix_g, [dg2, dgm, dg3], loss_row)

    bc_idx = jnp.concatenate([b_idx, c_idx])
    sw = dwc.reshape(HEADS * CH, CH)
    dwin_sum, dwin_sum_b, psmall = _grad_matmul_pair(c_idx, hb, dproj, [sv, sw], dgm, name="grad_w_in")
    sums_b = [dwin_sum]
    send_b, recv_b, src_b, land_b, token_b = _exchange_begin([dwin_sum_b], psmall, "exchange_b_begin")

    dwxo, dwxo_b = _grad_matmul(ob, dx2b, token_b, name="grad_w_xo", tk=2048)
    dwq, dwq_b = _grad_matmul(h2b, dq, token_b, name="grad_w_q", tk=2048)
    dwout, dwout_b = _grad_matmul(mixin, dx1b, token_b, name="grad_w_out")
    ps_a = _pair_reduce(c_idx, [dwout, dwkv, dwq, dwxo], [dwout_b, dwkv_b, dwq_b, dwxo_b], [], "pair_reduce_a")
    sums_a, sums_a_b = list(ps_a[:4]), list(ps_a[4:8])
    send_a, recv_a, src_a, land_a, token_a = _exchange_begin(sums_a_b, [], "exchange_a_begin")

    src_b, rx2b = _exchange_end(send_b, recv_b, src_b, land_b, 1, [0, 1, 2], [token_a], "exchange_b_end")
    gwin, svf, swf = _chip_reduce(bc_idx, sums_b, rx2b[:1], rx2b[1:], src_b[1:], "chip_reduce_b")
    out_b = _adamw_big(big[:1], [gwin], big_m[:1], big_v[:1], "adamw_w_in")[0]

    row = lambda a: a.reshape(1, D)
    mat = lambda a: a.reshape(HEADS * CH, CH)
    small, loss = _adamw_small(
        b_idx, svf, swf,
        [(row(norm_mix_g), row(m_norm_mix_g), row(v_norm_mix_g)), (row(norm_x_g), row(m_norm_x_g), row(v_norm_x_g)),
         (row(norm_mem_g), row(m_norm_mem_g), row(v_norm_mem_g)), (row(norm_final_g), row(m_norm_final_g), row(v_norm_final_g)),
         (row(gm_ln_g), row(m_gm_ln_g), row(v_gm_ln_g)), (row(gm_ln_b), row(m_gm_ln_b), row(v_gm_ln_b)),
         (row(gm_bs), row(m_gm_bs), row(v_gm_bs))],
        (conv_w[0], m_conv_w[0], v_conv_w[0]), (mat(gm_ws), mat(m_gm_ws), mat(v_gm_ws)))

    def finish_a(part, src, land, after, tag):
        src, land = _exchange_end(send_a, recv_a, src, land, 4, part, after, "exchange_a%s_end" % tag)
        grads = _chip_reduce(bc_idx, [sums_a[i] for i in part], [land[i] for i in part], [], [], "chip_reduce_a" + tag,
                             steps=2)
        ids = [(1, 3, 2, 4)[i] for i in part]
        outs = _adamw_big([big[i] for i in ids], grads, [big_m[i] for i in ids], [big_v[i] for i in ids], "adamw_a" + tag,
                          steps=4)
        return src, land, outs

    src_a, land_a, (out_wout, out_wkv) = finish_a([0, 1], src_a, land_a, [out_b[0], small[0][0]], "1")
    _, _, (out_wq, out_wxo) = finish_a([2, 3], src_a, land_a, [out_wout[0]], "2")

    def unpack(k):
        vec = lambda i: small[i][k]
        return [vec(0), out_b[k][None], small[7][k][None], vec(4), vec(5), small[8][k].reshape(1, HEADS, CH, CH),
                vec(6).reshape(1, HEADS, CH), out_wout[k][None], vec(1), vec(2), out_wq[k][None], out_wkv[k][None],
                out_wxo[k][None], vec(3).reshape(D)]

    return (loss.reshape(()), grad_x[None], *unpack(3), *unpack(0), *unpack(1), *unpack(2))
```

```python
import functools
import math

import jax
import jax.numpy as jnp
from jax import lax
from jax.experimental import pallas as pl
from jax.experimental.pallas import tpu as pltpu

F32 = jnp.float32
BF16 = jnp.bfloat16
MESH = pl.DeviceIdType.MESH

D = 1024
SLAB = 1024
N_SLAB = 7
IN_DIM = N_SLAB * SLAB
MIX = 2 * SLAB
HEADS = 8
CH = 128
XH = 4
XD = D // XH
EPS = 1e-6
GELU_C = math.sqrt(2.0 / math.pi)
GELU_A = 0.044715
N_CHIP = 4
IN_BLK = IN_DIM // N_CHIP
IN_PIECE = 256
N_PIECE = IN_BLK // IN_PIECE
KV_BLK = 2 * D // N_CHIP

ADAM_LR, ADAM_B1, ADAM_B2, ADAM_EPS, ADAM_WD, ADAM_STEP = 0.001, 0.9, 0.999, 1e-08, 0.01, 10

VMEM_LIMIT = 60 * 1024 * 1024


def _cp(sem=None, vmem=None):
    return pltpu.CompilerParams(dimension_semantics=sem, vmem_limit_bytes=vmem)


def _full(shape, buffers=None):
    n = len(shape)
    if buffers is None:
        return pl.BlockSpec(shape, lambda *_: (0,) * n)
    return pl.BlockSpec(shape, lambda *_: (0,) * n, pipeline_mode=pl.Buffered(buffers))


ANY = pl.BlockSpec(memory_space=pl.ANY)


def _bdot(a, b):
    return jnp.dot(a.astype(BF16), b.astype(BF16), preferred_element_type=F32)


def _bdot_nt(a, b):
    return lax.dot_general(a.astype(BF16), b.astype(BF16), (((1,), (1,)), ((), ())), preferred_element_type=F32)


def _bdot_tn(a, b):
    return lax.dot_general(a.astype(BF16), b.astype(BF16), (((0,), (0,)), ((), ())), preferred_element_type=F32)


def _rms(x, g):
    r = lax.rsqrt(jnp.mean(x * x, axis=-1, keepdims=True) + EPS)
    return x * r * g, r


def _rms_bwd(dy, x, r, g):
    gdy = dy * g
    dx = r * gdy - x * (r * r * r) * jnp.mean(x * gdy, axis=-1, keepdims=True)
    dg = jnp.sum(dy * x * r, axis=0, keepdims=True)
    return dx, dg


def _gelu_parts(x):
    x2 = x * x
    t = jnp.tanh(GELU_C * (x + GELU_A * x * x2))
    val = 0.5 * x * (1.0 + t)
    grad = 0.5 * (1.0 + t) + 0.5 * x * (1.0 - t * t) * (GELU_C * (1.0 + 3.0 * GELU_A * x2))
    return val, grad


def _gelu(x):
    return 0.5 * x * (1.0 + jnp.tanh(GELU_C * (x + GELU_A * x * x * x)))


def _sigmoid(z):
    return 1.0 / (1.0 + jnp.exp(-z))


def _cast_shards(b_idx, arrs):
    n = len(arrs)
    steps = 4

    def body(b_ref, *refs):
        for p in range(N_PIECE):
            refs[n][p] = refs[0][:, pl.ds(p * IN_PIECE, IN_PIECE)].astype(BF16)
        for i in range(1, n):
            refs[n + i][...] = refs[i][...].astype(BF16)

    rows = [a.shape[0] // steps for a in arrs]
    in_specs = [pl.BlockSpec((rows[i], a.shape[1]), lambda i, b: (i, 0)) for i, a in enumerate(arrs)]
    out_specs = [pl.BlockSpec((None, N_PIECE, rows[0], IN_PIECE), lambda i, b: (b[0], 0, i, 0))]
    out_specs += [pl.BlockSpec((None, rows[i], a.shape[1]), lambda i, b: (b[0], i, 0)) for i, a in enumerate(arrs) if i > 0]
    out_shape = [jax.ShapeDtypeStruct((N_CHIP, N_PIECE, arrs[0].shape[0], IN_PIECE), BF16)]
    out_shape += [jax.ShapeDtypeStruct((N_CHIP,) + a.shape, BF16) for a in arrs[1:]]
    return pl.pallas_call(
        body, out_shape=out_shape,
        grid_spec=pltpu.PrefetchScalarGridSpec(num_scalar_prefetch=1, grid=(steps,), in_specs=in_specs, out_specs=out_specs),
        compiler_params=_cp(("arbitrary",)), name="cast_shards")(b_idx, *arrs)


def _proj_gather(b_idx, x, g, win_own, cw8s, more, tm=1024):
    t = x.shape[0]
    ni = t // tm
    nm = len(more)
    steps = N_CHIP * N_PIECE
    near0, far0 = N_PIECE, 3 * N_PIECE

    def piece_at(step, own):
        k = step - near0
        near, far = (step >= near0) & (step < far0), step >= far0
        block = jnp.where(far, own ^ 3, jnp.where(near, own ^ jnp.where(lax.rem(k, 2) == 0, 2, 1), own))
        return block, jnp.where(far, step - far0, jnp.where(near, lax.div(k, 2), step))

    def body(*refs):
        b_ref, x_any, g_ref, win_in, cw_in = refs[:5]
        o_ref, hb_any, win_f, cw_out = refs[5 + nm:9 + nm]
        more_out = refs[9 + nm:9 + 2 * nm]
        hbuf, xbuf, wv, cw_s, cw_r, loc = refs[9 + 2 * nm:15 + 2 * nm]
        g_in = _Gather([win_f.at[:, p] for p in range(N_PIECE)], *refs[15 + 2 * nm:19 + 2 * nm])
        g_more = _Gather(more_out, *refs[19 + 2 * nm:23 + 2 * nm])
        s = pl.program_id(0)
        x, y, c, chips = _coords()
        b = 2 * x + y
        blks = [2 * chip[0] + chip[1] for chip in chips]

        def cw_cols(blk):
            return cw_out.at[:, pl.ds(blk * (D // N_CHIP), D // N_CHIP)]

        def cw_copy(k, blk):
            src = cw_in if blk is None else cw_cols(blk)
            return pltpu.make_async_remote_copy(src_ref=src, dst_ref=cw_cols(b if blk is None else blk), send_sem=cw_s.at[k],
                                                recv_sem=cw_r.at[k], device_id=(*chips[k], c), device_id_type=MESH)

        cw_local = pltpu.make_async_copy(cw_in, cw_cols(b), loc.at[1])
        hb_copy = pltpu.make_async_copy(hbuf, hb_any, loc.at[0])

        def load(step):
            slot = lax.rem(step, 2)
            block, piece = piece_at(step, b_ref[0])
            return pltpu.make_async_copy(win_f.at[block, piece], wv.at[slot], loc.at[2 + slot])

        def chunk(i):
            return pltpu.make_async_copy(x_any.at[pl.ds(i * tm, tm)], xbuf.at[i % 2], loc.at[4 + i % 2])

        def first():
            g_in.start()
            cw_local.start()
            for k in range(3):
                cw_copy(k, None).start()
            load(0).start()
            chunk(0).start()
            for i in range(ni):
                if i + 1 < ni:
                    chunk(i + 1).start()
                chunk(i).wait()
                h, _ = _rms(xbuf[i % 2], g_ref[...])
                hbuf[pl.ds(i * tm, tm), :] = h.astype(BF16)
            hb_copy.start()

        events = {step: [] for step in range(steps)}
        events[0].append(first)
        for p in range(N_PIECE):
            events[2 * p + 2].append(functools.partial(g_in.hop, [p]))
            events[near0 + 2 * p - 1].append(functools.partial(g_in.near_ready, [p]))
            events[far0 + p - 2].append(functools.partial(g_in.far, [p]))
            events[far0 + p - 1].append(functools.partial(g_in.far_ready, [p]))
        events[2 * N_PIECE + 1].append(g_more.start)
        for step, todo in events.items():
            if todo:
                @pl.when(s == step)
                def _(todo=todo):
                    for do in todo:
                        do()

        @pl.when(s + 1 < steps)
        def _():
            load(s + 1).start()

        load(s).wait()
        for i in range(ni):
            rows = pl.ds(i * tm, tm)
            o_ref[rows, :] = jnp.dot(hbuf[rows, :], wv[lax.rem(s, 2)], preferred_element_type=F32).astype(BF16)

        @pl.when(s == steps - 1)
        def _():
            g_more.hop()
            g_more.far()
            for k in range(3):
                cw_copy(k, blks[k]).wait_recv()
            for k in range(3):
                cw_copy(k, None).wait_send()
            cw_local.wait()
            hb_copy.wait()
            g_more.near_ready()
            g_more.far_ready()
            g_in.drain()
            g_more.drain()

    def out_col(s, b):
        block, piece = piece_at(s, b[0])
        return 0, block * N_PIECE + piece

    in_specs = [ANY, pl.BlockSpec((1, D), lambda s, b: (0, 0)), ANY, ANY] + [ANY] * nm
    out_specs = [pl.BlockSpec((t, IN_PIECE), out_col), ANY, ANY, ANY] + [ANY] * nm
    outs = pl.pallas_call(
        body, out_shape=[jax.ShapeDtypeStruct((t, IN_DIM), BF16), jax.ShapeDtypeStruct((t, D), BF16),
                         jax.ShapeDtypeStruct(win_own.shape, BF16), jax.ShapeDtypeStruct((8, D), F32)]
        + [jax.ShapeDtypeStruct(f.shape, f.dtype) for f in more],
        grid_spec=pltpu.PrefetchScalarGridSpec(
            num_scalar_prefetch=1, grid=(steps,), in_specs=in_specs, out_specs=out_specs,
            scratch_shapes=[pltpu.VMEM((t, D), BF16), pltpu.VMEM((2, tm, D), F32), pltpu.VMEM((2, D, IN_PIECE), BF16)]
            + [pltpu.SemaphoreType.DMA((3,))] * 2 + [pltpu.SemaphoreType.DMA((6,))]
            + _gather_sems(N_PIECE) + _gather_sems(nm)),
        input_output_aliases={3: 2, **{5 + w: 4 + w for w in range(nm)}},
        compiler_params=_cp(("arbitrary",), VMEM_LIMIT), name="proj_gather")(b_idx, x, g, win_own, cw8s, *more)
    return outs[0], outs[1], outs[2], outs[3], outs[4:]


class _Gather:
    def __init__(self, outs, ici_s, ici_r, d2d_s, d2d_r):
        x, y, c, _ = _coords()
        self.outs, self.c = outs, c
        self.sems = ici_s, ici_r, d2d_s, d2d_r
        self.b, self.bx, self.by, self.bd = 2 * x + y, 2 * (1 - x) + y, 2 * x + (1 - y), 2 * (1 - x) + (1 - y)
        self.xn, self.yn, self.sib = (1 - x, y, c), (x, 1 - y, c), (x, y, 1 - c)

    def piece(self, w, blk, hc, quarter=None):
        hr = self.outs[w].shape[1] // 2
        if quarter is None:
            return self.outs[w].at[blk, pl.ds(hc * hr, hr)]
        return self.outs[w].at[blk, pl.ds(hc * hr + quarter * (hr // 2), hr // 2)]

    def ici(self, w, k, ref, to):
        return pltpu.make_async_remote_copy(src_ref=ref, dst_ref=ref, send_sem=self.sems[0].at[w, k],
                                            recv_sem=self.sems[1].at[w, k], device_id=to, device_id_type=MESH)

    def d2d(self, w, k, ref):
        return pltpu.make_async_remote_copy(src_ref=ref, dst_ref=ref, send_sem=self.sems[2].at[w, k],
                                            recv_sem=self.sems[3].at[w, k], device_id=self.sib, device_id_type=MESH)

    def all(self):
        return range(len(self.outs))

    def start(self):
        for w in self.all():
            mine = self.piece(w, self.b, self.c)
            self.ici(w, 0, mine, self.xn).start()
            self.ici(w, 1, mine, self.yn).start()

    def hop(self, ws=None):
        c = self.c
        for w in ws or self.all():
            self.ici(w, 0, self.piece(w, self.bx, c), self.xn).wait_recv()
            self.ici(w, 1, self.piece(w, self.by, c), self.yn).wait_recv()
            self.ici(w, 2, self.piece(w, self.bx, c, 0), self.yn).start()
            self.ici(w, 3, self.piece(w, self.by, c, 1), self.xn).start()
            self.d2d(w, 0, self.piece(w, self.bx, c)).start()
            self.d2d(w, 1, self.piece(w, self.by, c)).start()

    def near_ready(self, ws=None):
        for w in ws or self.all():
            self.d2d(w, 0, self.piece(w, self.bx, 1 - self.c)).wait_recv()
            self.d2d(w, 1, self.piece(w, self.by, 1 - self.c)).wait_recv()

    def far(self, ws=None):
        c = self.c
        for w in ws or self.all():
            self.ici(w, 2, self.piece(w, self.bd, c, 0), self.yn).wait_recv()
            self.ici(w, 3, self.piece(w, self.bd, c, 1), self.xn).wait_recv()
            self.d2d(w, 2, self.piece(w, self.bd, c, 0)).start()
            self.d2d(w, 3, self.piece(w, self.bd, c, 1)).start()

    def far_ready(self, ws=None):
        for w in ws or self.all():
            self.d2d(w, 2, self.piece(w, self.bd, 1 - self.c, 0)).wait_recv()
            self.d2d(w, 3, self.piece(w, self.bd, 1 - self.c, 1)).wait_recv()

    def drain(self):
        c = self.c
        for w in self.all():
            mine = self.piece(w, self.b, c)
            self.ici(w, 0, mine, self.xn).wait_send()
            self.ici(w, 1, mine, self.yn).wait_send()
            self.ici(w, 2, self.piece(w, self.bx, c, 0), self.yn).wait_send()
            self.ici(w, 3, self.piece(w, self.by, c, 1), self.xn).wait_send()
            self.d2d(w, 0, self.piece(w, self.bx, c)).wait_send()
            self.d2d(w, 1, self.piece(w, self.by, c)).wait_send()
            self.d2d(w, 2, self.piece(w, self.bd, c, 0)).wait_send()
            self.d2d(w, 3, self.piece(w, self.bd, c, 1)).wait_send()


def _gather_sems(nw):
    return [pltpu.SemaphoreType.DMA((max(nw, 1), 4))] * 4


def _mixer_fwd(proj, cw8, lng, lnb, wc, bsb, fulls, tm=256):
    t = proj.shape[0]
    nt = t // tm
    nch = tm // CH
    nw = len(fulls)

    def body(*refs):
        p_ref, cw_ref, lng_ref, lnb_ref, wc_ref, bsb_ref = refs[:6]
        mix_ref = refs[6 + nw]
        w_outs = refs[7 + nw:7 + 2 * nw]
        prev_ref = refs[7 + 2 * nw]
        gather = _Gather(w_outs, *refs[8 + 2 * nw:])

        @pl.when(pl.program_id(0) == 0)
        def _():
            gather.start()
            prev_ref[...] = jnp.zeros_like(prev_ref)

        @pl.when(pl.program_id(0) == nt // 2)
        def _():
            gather.hop()

        @pl.when(pl.program_id(0) == nt - 1)
        def _():
            gather.far()

        rows = lax.broadcasted_iota(jnp.int32, (tm, CH), 0)
        for s in range(HEADS):
            cs = pl.ds(CH * s, CH)

            def slab(k):
                return p_ref[:, pl.ds(k * SLAB + CH * s, CH)].astype(F32)

            gb, gc, xa, za = slab(0), slab(1), slab(2), slab(3)
            cx = gc * xa
            p6 = jnp.broadcast_to(prev_ref[6:7, cs], (tm, CH))
            p7 = jnp.broadcast_to(prev_ref[7:8, cs], (tm, CH))
            c1 = jnp.where(rows == 0, p7, pltpu.roll(cx, 1, 0))
            c2 = jnp.where(rows == 0, p6, jnp.where(rows == 1, p7, pltpu.roll(cx, 2, 0)))
            prev_ref[:, cs] = cx[tm - 8:, :]
            cv = cw_ref[0:1, cs] * c2 + cw_ref[1:2, cs] * c1 + cw_ref[2:3, cs] * cx
            mix_ref[:, cs] = (gb * cv * (za * _sigmoid(za))).astype(BF16)

            u, v, zb = slab(4), slab(5), slab(6)
            ug, vg = _gelu(u), _gelu(v)
            dlt = vg - jnp.mean(vg, axis=-1, keepdims=True)
            vhat = dlt * lax.rsqrt(jnp.mean(dlt * dlt, axis=-1, keepdims=True) + EPS)
            vn = (vhat * lng_ref[:, cs] + lnb_ref[:, cs]).astype(BF16)
            gate = ug * (zb * _sigmoid(zb))
            for c in range(nch):
                rs = slice(CH * c, CH * (c + 1))
                sp = jnp.dot(wc_ref[s], vn[rs], preferred_element_type=F32) + bsb_ref[s]
                mix_ref[rs, pl.ds(SLAB + CH * s, CH)] = (gate[rs] * sp).astype(BF16)

        @pl.when(pl.program_id(0) == nt - 1)
        def _():
            gather.near_ready()
            gather.far_ready()
            gather.drain()

    sems = _gather_sems(nw)
    outs = pl.pallas_call(
        body, grid=(nt,),
        in_specs=[pl.BlockSpec((tm, IN_DIM), lambda i: (i, 0)), _full((8, D)), _full((1, D)), _full((1, D)),
                  _full((HEADS, CH, CH)), _full((HEADS, CH, CH))] + [ANY] * nw,
        out_specs=[pl.BlockSpec((tm, MIX), lambda i: (i, 0))] + [ANY] * nw,
        out_shape=[jax.ShapeDtypeStruct((t, MIX), BF16)] + [jax.ShapeDtypeStruct(f.shape, f.dtype) for f in fulls],
        input_output_aliases={6 + w: 1 + w for w in range(nw)},
        scratch_shapes=[pltpu.VMEM((8, D), F32)] + sems,
        compiler_params=_cp(("arbitrary",), VMEM_LIMIT), name="mixer_fwd")(proj, cw8, lng, lnb, wc, bsb, *fulls)
    return outs[0], outs[1:]


def _mem_fwd(mem, gm, wkv_f):
    n_mem = mem.shape[0]

    def body(mem_ref, gm_ref, w_ref, k_ref, v_ref):
        m, _ = _rms(mem_ref[...], gm_ref[...])
        mb = m.astype(BF16)
        for j in range(N_CHIP):
            dst = k_ref if j < 2 else v_ref
            dst[:, pl.ds(KV_BLK * (j % 2), KV_BLK)] = jnp.dot(mb, w_ref[j], preferred_element_type=F32).astype(BF16)

    return pl.pallas_call(
        body, out_shape=[jax.ShapeDtypeStruct((n_mem, D), BF16), jax.ShapeDtypeStruct((n_mem, D), BF16)],
        compiler_params=_cp(None, VMEM_LIMIT), name="mem_fwd")(mem, gm, wkv_f)


def _tail(x, tgt, mixin, wout, wq, wxo, k, v, g2, g3, tm=512, sub=512):
    t = x.shape[0]
    n_mem = k.shape[0]
    scale = 1.0 / math.sqrt(XD)

    def body(x_ref, tgt_ref, mix_ref, wout_ref, wq_ref, wxo_ref, k_ref, v_ref, g2_ref, g3_ref,
             loss_ref, dmix_ref, dx1b_ref, h2_ref, dq_ref, o_ref, dx2b_ref, dk_ref, dv_ref, dg2_ref, dg3_ref):
        @pl.when(pl.program_id(0) == 0)
        def _():
            loss_ref[...] = jnp.zeros_like(loss_ref)
            dk_ref[...] = jnp.zeros_like(dk_ref)
            dv_ref[...] = jnp.zeros_like(dv_ref)
            dg2_ref[...] = jnp.zeros_like(dg2_ref)
            dg3_ref[...] = jnp.zeros_like(dg3_ref)

        g2, g3 = g2_ref[...], g3_ref[...]
        for sb in range(tm // sub):
            rs = pl.ds(sub * sb, sub)
            x1 = x_ref[rs, :] + jnp.dot(mix_ref[rs, :], wout_ref[...], preferred_element_type=F32)
            h2, r2 = _rms(x1, g2)
            h2b = h2.astype(BF16)
            h2_ref[rs, :] = h2b
            q = jnp.dot(h2b, wq_ref[...], preferred_element_type=F32).astype(BF16)
            probs, outs = [], []
            for hd in range(XH):
                hs = pl.ds(XD * hd, XD)
                s = _bdot_nt(q[:, XD * hd:XD * (hd + 1)], k_ref[:, hs]) * scale
                e = jnp.exp(s - jnp.max(s, axis=-1, keepdims=True))
                p = e / jnp.sum(e, axis=-1, keepdims=True)
                probs.append(p)
                outs.append(_bdot(p, v_ref[:, hs]))
            ob = jnp.concatenate(outs, axis=-1).astype(BF16)
            o_ref[rs, :] = ob
            x2 = x1 + jnp.dot(ob, wxo_ref[...], preferred_element_type=F32)
            y, r3 = _rms(x2, g3)
            diff = y - tgt_ref[rs, :]
            row_loss = jnp.sum(diff * diff, axis=-1, keepdims=True)
            loss_ref[...] += jnp.broadcast_to(jnp.sum(row_loss, axis=0, keepdims=True) * (0.5 / D), loss_ref.shape)

            dx2, dg3 = _rms_bwd(diff * (1.0 / D), x2, r3, g3)
            dg3_ref[...] += dg3
            dx2b = dx2.astype(BF16)
            dx2b_ref[rs, :] = dx2b
            do = _bdot_nt(dx2b, wxo_ref[...])
            dqs = []
            for hd in range(XH):
                hs = pl.ds(XD * hd, XD)
                p = probs[hd]
                do_h = do[:, XD * hd:XD * (hd + 1)]
                dv_ref[:, hs] += _bdot_tn(p, do_h)
                dp = _bdot_nt(do_h, v_ref[:, hs])
                ds = p * (dp - jnp.sum(dp * p, axis=-1, keepdims=True))
                dqs.append(_bdot(ds, k_ref[:, hs]) * scale)
                dk_ref[:, hs] += _bdot_tn(ds, q[:, XD * hd:XD * (hd + 1)]) * scale
            dq = jnp.concatenate(dqs, axis=-1).astype(BF16)
            dq_ref[rs, :] = dq
            dx1n, dg2 = _rms_bwd(_bdot_nt(dq, wq_ref[...]), x1, r2, g2)
            dg2_ref[...] += dg2
            dx1b = (dx2 + dx1n).astype(BF16)
            dx1b_ref[rs, :] = dx1b
            dmix_ref[rs, :] = _bdot_nt(dx1b, wout_ref[...]).astype(BF16)

    tok = lambda w: pl.BlockSpec((tm, w), lambda i: (i, 0))
    return pl.pallas_call(
        body, grid=(t // tm,),
        in_specs=[tok(D), tok(D), tok(MIX), _full((MIX, D), 1), _full((D, D), 1), _full((D, D), 1),
                  _full((n_mem, D), 1), _full((n_mem, D), 1), _full((1, D)), _full((1, D))],
        out_specs=[_full((1, D)), tok(MIX), tok(D), tok(D), tok(D), tok(D), tok(D),
                   _full((n_mem, D)), _full((n_mem, D)), _full((1, D)), _full((1, D))],
        out_shape=[jax.ShapeDtypeStruct((1, D), F32), jax.ShapeDtypeStruct((t, MIX), BF16),
                   jax.ShapeDtypeStruct((t, D), BF16),
                   jax.ShapeDtypeStruct((t, D), BF16), jax.ShapeDtypeStruct((t, D), BF16),
                   jax.ShapeDtypeStruct((t, D), BF16), jax.ShapeDtypeStruct((t, D), BF16),
                   jax.ShapeDtypeStruct((n_mem, D), F32), jax.ShapeDtypeStruct((n_mem, D), F32),
                   jax.ShapeDtypeStruct((1, D), F32), jax.ShapeDtypeStruct((1, D), F32)],
        compiler_params=_cp(("arbitrary",), VMEM_LIMIT), name="tail")(x, tgt, mixin, wout, wq, wxo, k, v, g2, g3)


def _mem_bwd(mem, gm, dk, dv, wkv_f):
    def body(mem_ref, gm_ref, dk_ref, dv_ref, w_ref, dw_ref, dwb_ref, dgm_ref):
        mem_v = mem_ref[...]
        m, rm = _rms(mem_v, gm_ref[...])
        mb = m.astype(BF16)
        dm = jnp.zeros_like(mem_v)
        for j in range(N_CHIP):
            src = dk_ref if j < 2 else dv_ref
            dkv = src[:, pl.ds(KV_BLK * (j % 2), KV_BLK)].astype(BF16)
            dw = _bdot_tn(mb, dkv)
            dw_ref[j] = dw
            dwb_ref[j] = dw.astype(BF16)
            dm = dm + _bdot_nt(dkv, w_ref[j])
        dgm_ref[...] = jnp.sum(dm * mem_v * rm, axis=0, keepdims=True)

    return pl.pallas_call(
        body, out_shape=[jax.ShapeDtypeStruct((N_CHIP, D, KV_BLK), F32), jax.ShapeDtypeStruct((N_CHIP, D, KV_BLK), BF16),
                         jax.ShapeDtypeStruct((1, D), F32)],
        compiler_params=_cp(None, VMEM_LIMIT), name="mem_bwd")(mem, gm, dk, dv, wkv_f)


def _mixer_bwd(proj, dmix, cw8, lng, lnb, wc, wct, bsb, win_f, x, dx1, g1, rows123, row7, tm=256):
    t = proj.shape[0]
    nt = t // tm
    nch = tm // CH
    hb = 16
    pair = 2 * CH
    assert pair == IN_PIECE

    def body(p_ref, pgc_ref, pxa_ref, dm_ref, cw_ref, lng_ref, lnb_ref, wc_ref, wct_ref, bsb_ref, w_ref, x_ref,
             dx1_ref, g1_ref, r1_ref, r2_ref, r3_ref, r7_ref, dp_ref, sv_ref, dwc_ref, gx_ref,
             next_ref, dh_ref):
        i = pl.program_id(0)
        dg1_ref, dlng_ref, dlnb_ref, dbs_ref = (sv_ref.at[pl.ds(r, 1)] for r in (0, 4, 5, 6))
        dcw_ref = sv_ref.at[pl.ds(8, 8)]

        @pl.when(i == 0)
        def _():
            next_ref[...] = jnp.zeros_like(next_ref)
            sv_ref[...] = jnp.zeros_like(sv_ref)
            dwc_ref[...] = jnp.zeros_like(dwc_ref)
            for r, ref in ((1, r1_ref), (2, r2_ref), (3, r3_ref), (7, r7_ref)):
                sv_ref[r:r + 1, :] = ref[...]

        first_tile = i == nt - 1
        rows = lax.broadcasted_iota(jnp.int32, (tm, CH), 0)
        ones8 = jnp.ones((8, CH), BF16)
        for s in range(HEADS):
            cs = pl.ds(CH * s, CH)

            def slab(k):
                return p_ref[:, pl.ds(k * SLAB + CH * s, CH)].astype(F32)

            gb, gc, xa, za = slab(0), slab(1), slab(2), slab(3)
            da = dm_ref[:, cs].astype(F32)
            cx = gc * xa
            cxp = pgc_ref[:, cs].astype(F32) * pxa_ref[:, cs].astype(F32)
            cxp = jnp.where(first_tile, jnp.zeros_like(cxp), cxp)
            p6 = jnp.broadcast_to(cxp[hb - 2:hb - 1, :], (tm, CH))
            p7 = jnp.broadcast_to(cxp[hb - 1:hb, :], (tm, CH))
            c1 = jnp.where(rows == 0, p7, pltpu.roll(cx, 1, 0))
            c2 = jnp.where(rows == 0, p6, jnp.where(rows == 1, p7, pltpu.roll(cx, 2, 0)))
            w0, w1, w2 = cw_ref[0:1, cs], cw_ref[1:2, cs], cw_ref[2:3, cs]
            cv = w0 * c2 + w1 * c1 + w2 * cx
            sg = _sigmoid(za)
            sa = za * sg
            dcv = da * gb * sa
            dp_ref[:, pl.ds(0 * SLAB + CH * s, CH)] = (da * cv * sa).astype(BF16)
            dp_ref[:, pl.ds(3 * SLAB + CH * s, CH)] = (da * gb * cv * (sg * (1.0 + za * (1.0 - sg)))).astype(BF16)
            n0 = jnp.broadcast_to(next_ref[0:1, cs], (tm, CH))
            n1 = jnp.broadcast_to(next_ref[1:2, cs], (tm, CH))
            u1 = jnp.where(rows == tm - 1, n0, pltpu.roll(dcv, tm - 1, 0))
            u2 = jnp.where(rows == tm - 2, n0, jnp.where(rows == tm - 1, n1, pltpu.roll(dcv, tm - 2, 0)))
            next_ref[:, cs] = dcv[0:8, :]
            dcx = w2 * dcv + w1 * u1 + w0 * u2
            dp_ref[:, pl.ds(1 * SLAB + CH * s, CH)] = (dcx * xa).astype(BF16)
            dp_ref[:, pl.ds(2 * SLAB + CH * s, CH)] = (dcx * gc).astype(BF16)
            dcw_ref[0:1, cs] += jnp.sum(dcv * c2, axis=0, keepdims=True)
            dcw_ref[1:2, cs] += jnp.sum(dcv * c1, axis=0, keepdims=True)
            dcw_ref[2:3, cs] += jnp.sum(dcv * cx, axis=0, keepdims=True)

            u, v, zb = slab(4), slab(5), slab(6)
            db = dm_ref[:, pl.ds(SLAB + CH * s, CH)].astype(F32)
            ug, ugrad = _gelu_parts(u)
            vg, vgrad = _gelu_parts(v)
            dlt = vg - jnp.mean(vg, axis=-1, keepdims=True)
            rstd = lax.rsqrt(jnp.mean(dlt * dlt, axis=-1, keepdims=True) + EPS)
            vhat = dlt * rstd
            lg = lng_ref[:, cs]
            vn = (vhat * lg + lnb_ref[:, cs]).astype(BF16)
            sgb = _sigmoid(zb)
            szb = zb * sgb
            sps, dvns = [], []
            dbs = jnp.zeros((8, CH), F32)
            dwc = jnp.zeros((CH, CH), F32)
            for c in range(nch):
                rs = slice(CH * c, CH * (c + 1))
                sp = jnp.dot(wc_ref[s], vn[rs], preferred_element_type=F32) + bsb_ref[s]
                dsp = (db[rs] * ug[rs] * szb[rs]).astype(BF16)
                dbs = dbs + lax.dot_general(ones8, dsp, (((1,), (1,)), ((), ())), preferred_element_type=F32)
                dwc = dwc + lax.dot_general(dsp, vn[rs], (((1,), (1,)), ((), ())), preferred_element_type=F32)
                dvns.append(jnp.dot(wct_ref[s], dsp, preferred_element_type=F32))
                sps.append(sp)
            sp = jnp.concatenate(sps, axis=0)
            dvn = jnp.concatenate(dvns, axis=0)
            dbs_ref[:, cs] += dbs[0:1]
            dwc_ref[s] += dwc
            dlng_ref[:, cs] += jnp.sum(dvn * vhat, axis=0, keepdims=True)
            dlnb_ref[:, cs] += jnp.sum(dvn, axis=0, keepdims=True)
            dvhat = dvn * lg
            dvg = rstd * (dvhat - jnp.mean(dvhat, axis=-1, keepdims=True)
                          - vhat * jnp.mean(dvhat * vhat, axis=-1, keepdims=True))
            dp_ref[:, pl.ds(4 * SLAB + CH * s, CH)] = (db * sp * szb * ugrad).astype(BF16)
            dp_ref[:, pl.ds(5 * SLAB + CH * s, CH)] = (dvg * vgrad).astype(BF16)
            dp_ref[:, pl.ds(6 * SLAB + CH * s, CH)] = (db * ug * sp * (sgb * (1.0 + zb * (1.0 - sgb)))).astype(BF16)

            if s % 2 == 1:
                part = None
                for k in range(N_SLAB):
                    col = k * SLAB + pair * (s // 2)
                    blk, off = divmod(col, IN_BLK)
                    term = lax.dot_general(dp_ref[:, pl.ds(col, pair)], w_ref[blk, off // IN_PIECE],
                                           (((1,), (1,)), ((), ())), preferred_element_type=F32)
                    part = term if part is None else part + term
                if s == 1:
                    dh_ref[...] = part
                else:
                    dh_ref[...] += part

        xv = x_ref[...]
        r = lax.rsqrt(jnp.mean(xv * xv, axis=-1, keepdims=True) + EPS)
        dxn, dg = _rms_bwd(dh_ref[...], xv, r, g1_ref[...])
        gx_ref[...] = dx1_ref[...].astype(F32) + dxn
        dg1_ref[...] += dg

        @pl.when(i == nt - 1)
        def _():
            tril = lax.broadcasted_iota(jnp.int32, (CH, CH), 0) >= lax.broadcasted_iota(jnp.int32, (CH, CH), 1)
            for s in range(HEADS):
                dwc_ref[s] = jnp.where(tril, dwc_ref[s], 0.0)

    rev = lambda i: nt - 1 - i
    halo = lambda col: pl.BlockSpec((hb, SLAB), lambda i: (jnp.maximum(rev(i) * (tm // hb) - 1, 0), col))
    tok = lambda w: pl.BlockSpec((tm, w), lambda i: (rev(i), 0))
    return pl.pallas_call(
        body, grid=(nt,),
        in_specs=[tok(IN_DIM), halo(1), halo(2), tok(MIX), _full((8, D)), _full((1, D)), _full((1, D)),
                  _full((HEADS, CH, CH)), _full((HEADS, CH, CH)), _full((HEADS, CH, CH)),
                  _full((N_CHIP, N_PIECE, D, IN_PIECE), 1), tok(D), tok(D)] + [_full((1, D))] * 5,
        out_specs=[tok(IN_DIM), _full((16, D)), _full((HEADS, CH, CH)), tok(D)],
        out_shape=[jax.ShapeDtypeStruct((t, IN_DIM), BF16), jax.ShapeDtypeStruct((16, D), F32),
                   jax.ShapeDtypeStruct((HEADS, CH, CH), F32), jax.ShapeDtypeStruct((t, D), F32)],
        scratch_shapes=[pltpu.VMEM((8, D), F32), pltpu.VMEM((tm, D), F32)],
        compiler_params=_cp(("arbitrary",), VMEM_LIMIT), name="mixer_bwd")(
            proj, proj, proj, dmix, cw8, lng, lnb, wc, wct, bsb, win_f, x, dx1, g1, *rows123, row7)


def _grad_matmul(a, b, after, *, name, tk=1024):
    t, m = a.shape
    n = b.shape[1]
    nk = t // tk

    def body(a_ref, b_ref, after_ref, o_ref, ob_ref):
        kk = pl.program_id(0)
        part = lax.dot_general(a_ref[...], b_ref[...], (((0,), (0,)), ((), ())), preferred_element_type=F32)

        @pl.when(kk == 0)
        def _():
            o_ref[...] = part

        @pl.when(kk > 0)
        def _():
            o_ref[...] += part

        @pl.when(kk == nk - 1)
        def _():
            ob_ref[...] = o_ref[...].astype(BF16)

    o_spec = pl.BlockSpec((m, n), lambda k: (0, 0))
    o32, o16 = pl.pallas_call(
        body, grid=(nk,), in_specs=[pl.BlockSpec((tk, m), lambda k: (k, 0)), pl.BlockSpec((tk, n), lambda k: (k, 0)), ANY],
        out_specs=[o_spec, o_spec], out_shape=[jax.ShapeDtypeStruct((m, n), F32), jax.ShapeDtypeStruct((m, n), BF16)],
        compiler_params=_cp(("arbitrary",), VMEM_LIMIT), name=name)(a, b, after)
    return o32.reshape(N_CHIP, m // N_CHIP, n), o16.reshape(N_CHIP, m // N_CHIP, n)


def _coords():
    x, y, c = lax.axis_index("x"), lax.axis_index("y"), lax.axis_index("c")
    chips = [(1 - x, y), (x, 1 - y), (1 - x, 1 - y)]
    return x, y, c, chips


def _pair_reduce(c_idx, grads, grads_b, smalls, name):
    ng, ns = len(grads), len(smalls)
    halves = [g.shape[1] // 2 for g in grads]

    def body(c_ref, *refs):
        g_in, gb_any = refs[:ng], refs[ng:2 * ng]
        s_own, s_any = refs[2 * ng:2 * ng + ns], refs[2 * ng + ns:2 * ng + 2 * ns]
        o = refs[2 * ng + 2 * ns:4 * ng + 3 * ns]
        lands = refs[4 * ng + 3 * ns:5 * ng + 4 * ns]
        send, recv = refs[5 * ng + 4 * ns:]
        x, y, c, _ = _coords()
        j = pl.program_id(0)

        def big(i, blk):
            return pltpu.make_async_remote_copy(
                src_ref=gb_any[i].at[blk, pl.ds((1 - c) * halves[i], halves[i])], dst_ref=lands[i].at[blk],
                send_sem=send.at[i, blk], recv_sem=recv.at[i, blk], device_id=(x, y, 1 - c), device_id_type=MESH)

        def small(i):
            return pltpu.make_async_remote_copy(
                src_ref=s_any[i].at[1 - c], dst_ref=lands[ng + i],
                send_sem=send.at[ng + i, 0], recv_sem=recv.at[ng + i, 0], device_id=(x, y, 1 - c), device_id_type=MESH)

        @pl.when(j == 0)
        def _():
            for blk in range(N_CHIP):
                for i in range(ng):
                    big(i, blk).start()
            for i in range(ns):
                small(i).start()

        for i in range(ng):
            big(i, j).wait_recv()
            tot = g_in[i][...] + lands[i][j].astype(F32)
            o[i][...] = tot
            o[ng + i][...] = tot.astype(BF16)

        @pl.when(j == N_CHIP - 1)
        def _():
            for i in range(ns):
                small(i).wait_recv()
                o[2 * ng + i][...] = s_own[i][...] + lands[ng + i][...]
                small(i).wait_send()
            for blk in range(N_CHIP):
                for i in range(ng):
                    big(i, blk).wait_send()

    in_specs = [pl.BlockSpec((None, None, halves[i], g.shape[2]), lambda b, c: (b, c[0], 0, 0)) for i, g in enumerate(grads)]
    in_specs += [ANY] * ng
    in_specs += [pl.BlockSpec((None, s.shape[0] // 2, s.shape[1]), lambda b, c: (c[0], 0, 0)) for s in smalls]
    in_specs += [ANY] * ns
    blk = [pl.BlockSpec((None, halves[i], g.shape[2]), lambda b, c: (b, 0, 0)) for i, g in enumerate(grads)]
    out_specs = blk + blk + [pl.BlockSpec((s.shape[0] // 2, s.shape[1]), lambda b, c: (0, 0)) for s in smalls]
    out_shape = [jax.ShapeDtypeStruct((N_CHIP, halves[i], g.shape[2]), F32) for i, g in enumerate(grads)]
    out_shape += [jax.ShapeDtypeStruct((N_CHIP, halves[i], g.shape[2]), BF16) for i, g in enumerate(grads)]
    out_shape += [jax.ShapeDtypeStruct((s.shape[0] // 2, s.shape[1]), F32) for s in smalls]
    scratch = [pltpu.VMEM((N_CHIP, halves[i], g.shape[2]), BF16) for i, g in enumerate(grads)]
    scratch += [pltpu.VMEM((s.shape[0] // 2, s.shape[1]), F32) for s in smalls]
    scratch += [pltpu.SemaphoreType.DMA((ng + ns, N_CHIP)), pltpu.SemaphoreType.DMA((ng + ns, N_CHIP))]
    grads4 = [g.reshape(N_CHIP, 2, halves[i], g.shape[2]) for i, g in enumerate(grads)]
    smalls3 = [s.reshape(2, s.shape[0] // 2, s.shape[1]) for s in smalls]
    return pl.pallas_call(
        body, out_shape=out_shape,
        grid_spec=pltpu.PrefetchScalarGridSpec(num_scalar_prefetch=1, grid=(N_CHIP,), in_specs=in_specs,
                                               out_specs=out_specs, scratch_shapes=scratch),
        compiler_params=_cp(("arbitrary",), VMEM_LIMIT), name=name)(c_idx, *grads4, *grads_b, *smalls3, *smalls3)


def _grad_matmul_pair(c_idx, a, b, smalls, small_dtypes, after, *, name, tk=2048):
    t, m = a.shape
    bn = b.shape[1] // N_CHIP
    nk = t // tk
    hr = m // 2
    ns = len(smalls)

    def body(c_ref, a_ref, b_ref, *refs):
        s_own, s_any = refs[:ns], refs[ns:2 * ns]
        o32, o16 = refs[2 * ns + 1], refs[2 * ns + 2]
        o_small = refs[2 * ns + 3:3 * ns + 3]
        acc, tb, land, st16 = refs[3 * ns + 3:3 * ns + 7]
        s_land, s_stage = refs[3 * ns + 7:4 * ns + 7], refs[4 * ns + 7:5 * ns + 7]
        send, recv, loc = refs[5 * ns + 7:]
        x, y, c, _ = _coords()
        sibling = dict(device_id=(x, y, 1 - c), device_id_type=MESH)
        j, kk = pl.program_id(0), pl.program_id(1)
        mine = pl.ds(pl.multiple_of(c * hr, hr), hr)
        theirs = pl.ds(pl.multiple_of((1 - c) * hr, hr), hr)

        def to_sibling(blk):
            return pltpu.make_async_remote_copy(src_ref=tb, dst_ref=land.at[blk], send_sem=send.at[blk],
                                                recv_sem=recv.at[blk], **sibling)

        def small(i):
            return pltpu.make_async_remote_copy(src_ref=s_any[i].at[1 - c], dst_ref=s_land[i], send_sem=send.at[N_CHIP + i],
                                                recv_sem=recv.at[N_CHIP + i], **sibling)

        def written(blk):
            return (pltpu.make_async_copy(acc.at[blk % 2, mine], o32.at[blk], loc.at[0]),
                    pltpu.make_async_copy(st16, o16.at[blk], loc.at[1]))

        def finish(blk):
            to_sibling(blk).wait_recv()

            @pl.when(blk > 0)
            def _():
                for cp in written(blk - 1):
                    cp.wait()

            tot = acc[blk % 2, mine, :] + land[blk].astype(F32)
            acc[blk % 2, mine, :] = tot
            st16[...] = tot.astype(BF16)
            for cp in written(blk):
                cp.start()

        def small_out(i):
            return pltpu.make_async_copy(s_stage[i], o_small[i], loc.at[2 + i])

        @pl.when((j == 0) & (kk == 0))
        def _():
            for i in range(ns):
                small(i).start()

        @pl.when((j == 1) & (kk == 0))
        def _():
            for i in range(ns):
                small(i).wait_recv()
                s_stage[i][...] = (s_own[i][...] + s_land[i][...]).astype(small_dtypes[i])
                small_out(i).start()

        @pl.when((j > 0) & (kk == 0))
        def _():
            finish(j - 1)

        part = lax.dot_general(a_ref[...], b_ref[...], (((0,), (0,)), ((), ())), preferred_element_type=F32)
        slot = lax.rem(j, 2)

        @pl.when(kk == 0)
        def _():
            acc[slot] = part

        @pl.when(kk > 0)
        def _():
            acc[slot] += part

        @pl.when(kk == nk - 1)
        def _():
            @pl.when(j > 0)
            def _():
                to_sibling(j - 1).wait_send()

            tb[...] = acc[slot, theirs, :].astype(BF16)
            to_sibling(j).start()

        @pl.when((j == N_CHIP - 1) & (kk == nk - 1))
        def _():
            finish(j)
            for i in range(ns):
                small_out(i).wait()
                small(i).wait_send()
            for cp in written(j):
                cp.wait()
            to_sibling(j).wait_send()

    halves = [(s.shape[0] // 2, s.shape[1]) for s in smalls]
    in_specs = [pl.BlockSpec((tk, m), lambda j, k, c: (k, 0)), pl.BlockSpec((tk, bn), lambda j, k, c: (k, j))]
    in_specs += [pl.BlockSpec((None,) + h, lambda j, k, c: (c[0], 0, 0)) for h in halves] + [ANY] * ns + [ANY]
    out_shape = [jax.ShapeDtypeStruct((N_CHIP, hr, bn), F32), jax.ShapeDtypeStruct((N_CHIP, hr, bn), BF16)]
    out_shape += [pltpu.HBM(h, dt) for h, dt in zip(halves, small_dtypes)]
    scratch = [pltpu.VMEM((2, m, bn), F32), pltpu.VMEM((hr, bn), BF16),
               pltpu.VMEM((N_CHIP, hr, bn), BF16), pltpu.VMEM((hr, bn), BF16)]
    scratch += [pltpu.VMEM(h, F32) for h in halves] + [pltpu.VMEM(h, dt) for h, dt in zip(halves, small_dtypes)]
    scratch += [pltpu.SemaphoreType.DMA((N_CHIP + ns,)), pltpu.SemaphoreType.DMA((N_CHIP + ns,)),
                pltpu.SemaphoreType.DMA((2 + ns,))]
    smalls3 = [s.reshape((2,) + h) for s, h in zip(smalls, halves)]
    outs = pl.pallas_call(
        body, out_shape=out_shape,
        grid_spec=pltpu.PrefetchScalarGridSpec(num_scalar_prefetch=1, grid=(N_CHIP, nk), in_specs=in_specs,
                                               out_specs=[ANY, ANY] + [_HBM] * ns, scratch_shapes=scratch),
        compiler_params=_cp(("arbitrary", "arbitrary"), VMEM_LIMIT), name=name)(c_idx, a, b, *smalls3, *smalls3, after)
    return outs[0], outs[1], list(outs[2:])


_HBM = pl.BlockSpec(memory_space=pltpu.HBM)
_SEM = pl.BlockSpec(memory_space=pltpu.SEMAPHORE)


def _split_copies(ins, lands, ng, send, recv, arriving):
    x, y, c, chips = _coords()
    b = 2 * x + y
    copies = []
    for i in range(len(ins)):
        for k in range(3):
            blk = 2 * chips[k][0] + chips[k][1]
            src, dst, got = (ins[i].at[blk], lands[i].at[k], lands[i].at[k]) if i < ng else (ins[i], lands[i].at[b], lands[i].at[blk])
            sems = dict(send_sem=send.at[3 * i + k], recv_sem=recv.at[3 * i + k], device_id=(*chips[k], c), device_id_type=MESH)
            if arriving:
                copies.append(pltpu.make_async_remote_copy(src_ref=got, dst_ref=got, **sems))
            else:
                copies.append(pltpu.make_async_remote_copy(src_ref=src, dst_ref=dst, **sems))
    return copies


def _exchange_begin(sums_b, smalls, name):
    ng, n = len(sums_b), len(sums_b) + len(smalls)
    srcs = list(sums_b) + list(smalls)
    lands = [lax.empty((3,) + g.shape[1:], g.dtype) for g in sums_b] + [lax.empty((N_CHIP,) + s.shape, s.dtype) for s in smalls]

    def body(*refs):
        ins, land_refs = refs[:n], refs[n:2 * n]
        send, recv = refs[2 * n], refs[2 * n + 1]
        token = refs[4 * n + 2]
        for cp in _split_copies(ins, land_refs, ng, send, recv, False):
            cp.start()
        token[...] = jnp.zeros_like(token)

    hbm = lambda a: pltpu.HBM(a.shape, a.dtype)
    outs = pl.pallas_call(
        body, name=name,
        out_shape=(pltpu.SemaphoreType.DMA((3 * n,)), pltpu.SemaphoreType.DMA((3 * n,)), *[hbm(a) for a in srcs + lands],
                   jax.ShapeDtypeStruct((8, 128), F32)),
        in_specs=[_HBM] * (2 * n), out_specs=(_SEM, _SEM, *[_HBM] * (2 * n), pl.BlockSpec(memory_space=pltpu.VMEM)),
        input_output_aliases={i: 2 + i for i in range(2 * n)},
        compiler_params=pltpu.CompilerParams(has_side_effects=pltpu.SideEffectType.DATAFLOW_SIDE_EFFECTING),
    )(*[pltpu.with_memory_space_constraint(a, pltpu.HBM) for a in srcs + lands])
    return outs[0], outs[1], list(outs[2:2 + n]), list(outs[2 + n:2 + 2 * n]), outs[2 + 2 * n]


def _exchange_end(send, recv, srcs, lands, ng, which, after, name):
    n = len(srcs)
    after = list(after)

    def body(*refs):
        ins, land_refs = refs[:n], refs[n:2 * n]
        send_ref, recv_ref = refs[2 * n], refs[2 * n + 1]
        outgoing = _split_copies(ins, land_refs, ng, send_ref, recv_ref, False)
        arriving = _split_copies(ins, land_refs, ng, send_ref, recv_ref, True)
        for i in which:
            for cp in outgoing[3 * i:3 * i + 3]:
                cp.wait_send()
        for i in which:
            for cp in arriving[3 * i:3 * i + 3]:
                cp.wait_recv()

    hbm = lambda a: pltpu.HBM(a.shape, a.dtype)
    outs = pl.pallas_call(
        body, name=name, out_shape=tuple(hbm(a) for a in list(srcs) + list(lands)),
        in_specs=[_HBM] * (2 * n) + [_SEM, _SEM] + [ANY] * len(after), out_specs=tuple([_HBM] * (2 * n)),
        input_output_aliases={i: i for i in range(2 * n)},
        compiler_params=pltpu.CompilerParams(has_side_effects=pltpu.SideEffectType.DATAFLOW_SIDE_EFFECTING),
    )(*srcs, *lands, send, recv, *after)
    return list(outs[:n]), list(outs[n:])


def _chip_reduce(bc_idx, sums, recvd, smalls_slots, smalls_own, name, steps=4):
    ng, ns = len(sums), len(smalls_slots)
    n = ng + ns
    assert steps >= 2
    halves = [g.shape[1] for g in sums] + [s.shape[1] for s in smalls_slots]
    rows = [g.shape[1] // steps for g in sums]

    def body(bc_ref, *refs):
        own, rx = refs[:ng], refs[ng:2 * ng]
        sl = refs[2 * ng:2 * ng + ns]
        sl_own = refs[2 * ng + ns:2 * ng + 2 * ns]
        o = refs[2 * ng + 2 * ns:2 * ng + 2 * ns + n]
        tiles = refs[2 * ng + 2 * ns + n:2 * ng + 2 * ns + 2 * n]
        keep, send, recv = refs[2 * ng + 2 * ns + 2 * n:]
        x, y, c, _ = _coords()
        sibling = dict(device_id=(x, y, 1 - c), device_id_type=MESH)
        r = pl.program_id(0)

        def writes(i, step, slot):
            dst = o[i].at[pl.ds(c * halves[i] + step * rows[i], rows[i])]
            return (pltpu.make_async_copy(tiles[i].at[slot], dst, keep.at[i, slot]),
                    pltpu.make_async_remote_copy(src_ref=tiles[i].at[slot], dst_ref=dst, send_sem=send.at[i, slot],
                                                 recv_sem=recv.at[i, step], **sibling))

        def small_writes(i):
            dst = o[i].at[pl.ds(c * halves[i], halves[i])]
            return (pltpu.make_async_copy(tiles[i], dst, keep.at[i, 0]),
                    pltpu.make_async_remote_copy(src_ref=tiles[i], dst_ref=dst, send_sem=send.at[i, 0],
                                                 recv_sem=recv.at[i, 0], **sibling))

        def arriving(i, step, nrows):
            dst = o[i].at[pl.ds((1 - c) * halves[i] + step * nrows, nrows)]
            return pltpu.make_async_remote_copy(src_ref=dst, dst_ref=dst, send_sem=send.at[i, 0], recv_sem=recv.at[i, step],
                                                **sibling)

        def finish(step, slot):
            for i in range(ng):
                local, remote = writes(i, step, slot)
                local.wait()
                remote.wait_send()

        @pl.when(r >= 2)
        def _():
            finish(r - 2, r % 2)

        for i in range(ng):
            tot = own[i][...]
            for j in range(3):
                tot = tot + rx[i][j].astype(F32)
            tiles[i][r % 2] = tot
            for cp in writes(i, r, r % 2):
                cp.start()

        @pl.when(r == 0)
        def _():
            for i in range(ns):
                term = [jnp.where(bc_ref[0] == kk, sl_own[i][...], sl[i][kk]).astype(F32) for kk in range(N_CHIP)]
                tiles[ng + i][...] = ((term[0] + term[1]) + term[2]) + term[3]
                for cp in small_writes(ng + i):
                    cp.start()

        @pl.when(r == steps - 1)
        def _():
            finish(steps - 2, (steps - 2) % 2)
            finish(steps - 1, (steps - 1) % 2)
            for i in range(ns):
                local, remote = small_writes(ng + i)
                local.wait()
                remote.wait_send()
                arriving(ng + i, 0, halves[ng + i]).wait_recv()
            for i in range(ng):
                for step in range(steps):
                    arriving(i, step, rows[i]).wait_recv()

    in_specs = [pl.BlockSpec((None, rows[i], g.shape[2]), lambda r, bc: (bc[0], r, 0)) for i, g in enumerate(sums)]
    in_specs += [pl.BlockSpec((3, rows[i], g.shape[2]), lambda r, bc: (0, r, 0)) for i, g in enumerate(sums)]
    in_specs += [pl.BlockSpec(s.shape, lambda r, bc: (0, 0, 0)) for s in smalls_slots]
    in_specs += [pl.BlockSpec(s.shape[1:], lambda r, bc: (0, 0)) for s in smalls_slots]
    out_shape = [jax.ShapeDtypeStruct((2 * g.shape[1], g.shape[2]), F32) for g in sums]
    out_shape += [jax.ShapeDtypeStruct((2 * s.shape[1], s.shape[2]), F32) for s in smalls_slots]
    scratch = [pltpu.VMEM((2, rows[i], g.shape[2]), F32) for i, g in enumerate(sums)]
    scratch += [pltpu.VMEM(s.shape[1:], F32) for s in smalls_slots]
    scratch += [pltpu.SemaphoreType.DMA((n, 2)), pltpu.SemaphoreType.DMA((n, 2)), pltpu.SemaphoreType.DMA((n, steps))]
    return list(pl.pallas_call(
        body, out_shape=out_shape,
        grid_spec=pltpu.PrefetchScalarGridSpec(num_scalar_prefetch=1, grid=(steps,), in_specs=in_specs,
                                               out_specs=[ANY] * n, scratch_shapes=scratch),
        compiler_params=_cp(("arbitrary",), VMEM_LIMIT), name=name)(bc_idx, *sums, *recvd, *smalls_slots, *smalls_own))


def _adamw_math(w, g, m, v):
    m2 = ADAM_B1 * m + (1.0 - ADAM_B1) * g
    v2 = ADAM_B2 * v + (1.0 - ADAM_B2) * (g * g)
    m_hat = m2 / (1.0 - ADAM_B1 ** ADAM_STEP)
    v_hat = v2 / (1.0 - ADAM_B2 ** ADAM_STEP)
    delta = -ADAM_LR * (m_hat / (jnp.sqrt(v_hat) + ADAM_EPS) + ADAM_WD * w)
    return delta, m2, v2


def _adamw_big(ws, gs, ms, vs, name, steps=8):
    n = len(ws)

    def body(*refs):
        for i in range(n):
            w_ref, g_ref, m_ref, v_ref = (refs[k * n + i] for k in range(4))
            d_ref, m2_ref, v2_ref, g2_ref = (refs[(4 + k) * n + i] for k in range(4))
            gv = g_ref[...]
            d_ref[...], m2_ref[...], v2_ref[...] = _adamw_math(w_ref[...], gv, m_ref[...], v_ref[...])
            g2_ref[...] = gv

    specs = [pl.BlockSpec((w.shape[0] // steps, w.shape[1]), lambda i: (i, 0)) for w in ws]
    shapes = [jax.ShapeDtypeStruct(w.shape, F32) for w in ws]
    outs = pl.pallas_call(
        body, grid=(steps,), in_specs=specs * 4, out_specs=specs * 4, out_shape=shapes * 4,
        compiler_params=_cp(("parallel",), VMEM_LIMIT), name=name)(*ws, *gs, *ms, *vs)
    return [tuple(outs[k * n + i] for k in range(4)) for i in range(n)]


def _adamw_small(b_idx, sv, sw, vecs, conv, ws):
    nv = len(vecs)
    cols = conv[0].shape[1]

    def body(b_ref, sv_ref, sw_ref, *refs):
        ins, outs = refs[:3 * nv + 6], refs[3 * nv + 6:]
        for i in range(nv):
            g = sv_ref[i:i + 1, :]
            w_ref, m_ref, v_ref = ins[3 * i:3 * i + 3]
            d_ref, m2_ref, v2_ref, g_ref = outs[4 * i:4 * i + 4]
            d_ref[...], m2_ref[...], v2_ref[...] = _adamw_math(w_ref[...], g, m_ref[...], v_ref[...])
            g_ref[...] = g
        g = sv_ref[8:8 + conv[0].shape[0], pl.ds(pl.multiple_of(b_ref[0] * cols, cols), cols)]
        w_ref, m_ref, v_ref = ins[3 * nv:3 * nv + 3]
        d_ref, m2_ref, v2_ref, g_ref = outs[4 * nv:4 * nv + 4]
        d_ref[...], m2_ref[...], v2_ref[...] = _adamw_math(w_ref[...], g, m_ref[...], v_ref[...])
        g_ref[...] = g
        w_ref, m_ref, v_ref = ins[3 * nv + 3:]
        d_ref, m2_ref, v2_ref, g_ref, one_ref = outs[4 * nv + 4:]
        g = sw_ref[...]
        d_ref[...], m2_ref[...], v2_ref[...] = _adamw_math(w_ref[...], g, m_ref[...], v_ref[...])
        g_ref[...] = g
        one_ref[...] = sv_ref[nv:nv + 1, 0:1]

    flat = [a for grp in vecs for a in grp] + list(conv) + list(ws)
    out_shape = [jax.ShapeDtypeStruct(grp[0].shape, F32) for grp in list(vecs) + [conv, ws] for _ in range(4)]
    out_shape += [jax.ShapeDtypeStruct((1, 1), F32)]
    vmem = pl.BlockSpec(memory_space=pltpu.VMEM)
    outs = pl.pallas_call(
        body, out_shape=out_shape, in_specs=[pl.BlockSpec(memory_space=pltpu.SMEM)] + [vmem] * (2 + len(flat)),
        out_specs=[vmem] * len(out_shape), name="adamw_small")(b_idx, sv, sw, *flat)
    return [tuple(outs[4 * i:4 * i + 4]) for i in range(nv + 2)], outs[4 * nv + 8]


def kernel(x, mem, norm_mix_g, w_in, conv_w, gm_ln_g, gm_ln_b, gm_ws, gm_bs, w_out, norm_x_g, norm_mem_g, w_q, w_kv, w_xo, norm_final_g, loss_target, m_norm_mix_g, m_w_in, m_conv_w, m_gm_ln_g, m_gm_ln_b, m_gm_ws, m_gm_bs, m_w_out, m_norm_x_g, m_norm_mem_g, m_w_q, m_w_kv, m_w_xo, m_norm_final_g, v_norm_mix_g, v_w_in, v_conv_w, v_gm_ln_g, v_gm_ln_b, v_gm_ws, v_gm_bs, v_w_out, v_norm_x_g, v_norm_mem_g, v_w_q, v_w_kv, v_w_xo, v_norm_final_g):
    t = x.shape[1]
    xi = lax.axis_index("x")
    yi = lax.axis_index("y")
    ci = lax.axis_index("c")
    b_idx = jnp.reshape(2 * xi + yi, (1,)).astype(jnp.int32)
    c_idx = jnp.reshape(ci, (1,)).astype(jnp.int32)

    x2d, mem2d, tgt = x[0], mem[0], loss_target[0]
    big = [w_in[0], w_out[0], w_q[0], w_kv[0], w_xo[0]]
    big_m = [m_w_in[0], m_w_out[0], m_w_q[0], m_w_kv[0], m_w_xo[0]]
    big_v = [v_w_in[0], v_w_out[0], v_w_q[0], v_w_kv[0], v_w_xo[0]]
    g3 = norm_final_g.reshape(1, D)

    def pad8(a):
        return jnp.pad(a, ((0, 8 - a.shape[0]), (0, 0)))

    own_blocks = _cast_shards(b_idx, big)

    tril = jnp.tril(jnp.ones((CH, CH), bool))
    wc32 = jnp.where(tril[None], gm_ws[0], 0.0)
    wc = wc32.astype(BF16)
    wct = jnp.swapaxes(wc32, 1, 2).astype(BF16)
    bsb = jnp.broadcast_to(gm_bs[0][:, :, None], (HEADS, CH, CH))

    proj, hb, win_f, cw8, (wq_f,) = _proj_gather(
        b_idx, x2d, norm_mix_g, own_blocks[0], pad8(conv_w[0]), [own_blocks[2]])
    mixin, (wout_f, wkv_f, wxo_f) = _mixer_fwd(
        proj, cw8, gm_ln_g, gm_ln_b, wc, bsb, [own_blocks[1], own_blocks[3], own_blocks[4]])
    wout2, wq2, wxo2 = wout_f.reshape(MIX, D), wq_f.reshape(D, D), wxo_f.reshape(D, D)
    k, v = _mem_fwd(mem2d, norm_mem_g, wkv_f)

    (loss_row, dmix, dx1b, h2b, dq, ob, dx2b, dk, dv, dg2, dg3) = _tail(
        x2d, tgt, mixin, wout2, wq2, wxo2, k, v, norm_x_g, g3)
    dwkv, dwkv_b, dgm = _mem_bwd(mem2d, norm_mem_g, dk, dv, wkv_f)
    dproj, sv, dwc, grad_x = _mixer_bwd(
        proj, dmix, cw8, gm_ln_g, gm_ln_b, wc, wct, bsb, win_f, x2d, dx1b, norm_mix_g, [dg2, dgm, dg3], loss_row)

    bc_idx = jnp.concatenate([b_idx, c_idx])
    sw = dwc.reshape(HEADS * CH, CH)
    dwin_sum, dwin_sum_b, psmall = _grad_matmul_pair(c_idx, hb, dproj, [sv, sw], [F32, BF16], dgm, name="grad_w_in")
    sums_b = [dwin_sum]
    send_b, recv_b, src_b, land_b, token_b = _exchange_begin([dwin_sum_b], psmall, "exchange_b_begin")

    dwxo, dwxo_b = _grad_matmul(ob, dx2b, token_b, name="grad_w_xo", tk=2048)
    dwq, dwq_b = _grad_matmul(h2b, dq, token_b, name="grad_w_q", tk=2048)
    dwout, dwout_b = _grad_matmul(mixin, dx1b, token_b, name="grad_w_out")
    ps_a = _pair_reduce(c_idx, [dwout, dwkv, dwq, dwxo], [dwout_b, dwkv_b, dwq_b, dwxo_b], [], "pair_reduce_a")
    sums_a, sums_a_b = list(ps_a[:4]), list(ps_a[4:8])
    send_a, recv_a, src_a, land_a, token_a = _exchange_begin(sums_a_b, [], "exchange_a_begin")

    src_b, rx2b = _exchange_end(send_b, recv_b, src_b, land_b, 1, [0, 1, 2], [token_a], "exchange_b_end")
    gwin, svf, swf = _chip_reduce(bc_idx, sums_b, rx2b[:1], rx2b[1:], src_b[1:], "chip_reduce_b")
    out_b = _adamw_big(big[:1], [gwin], big_m[:1], big_v[:1], "adamw_w_in")[0]

    row = lambda a: a.reshape(1, D)
    mat = lambda a: a.reshape(HEADS * CH, CH)
    small, loss = _adamw_small(
        b_idx, svf, swf,
        [(row(norm_mix_g), row(m_norm_mix_g), row(v_norm_mix_g)), (row(norm_x_g), row(m_norm_x_g), row(v_norm_x_g)),
         (row(norm_mem_g), row(m_norm_mem_g), row(v_norm_mem_g)), (row(norm_final_g), row(m_norm_final_g), row(v_norm_final_g)),
         (row(gm_ln_g), row(m_gm_ln_g), row(v_gm_ln_g)), (row(gm_ln_b), row(m_gm_ln_b), row(v_gm_ln_b)),
         (row(gm_bs), row(m_gm_bs), row(v_gm_bs))],
        (conv_w[0], m_conv_w[0], v_conv_w[0]), (mat(gm_ws), mat(m_gm_ws), mat(v_gm_ws)))

    def finish_a(part, src, land, after, tag):
        src, land = _exchange_end(send_a, recv_a, src, land, 4, part, after, "exchange_a%s_end" % tag)
        grads = _chip_reduce(bc_idx, [sums_a[i] for i in part], [land[i] for i in part], [], [], "chip_reduce_a" + tag,
                             steps=2)
        ids = [(1, 3, 2, 4)[i] for i in part]
        outs = _adamw_big([big[i] for i in ids], grads, [big_m[i] for i in ids], [big_v[i] for i in ids], "adamw_a" + tag,
                          steps=4)
        return src, land, outs

    src_a, land_a, (out_wout, out_wkv) = finish_a([0, 1], src_a, land_a, [out_b[0], small[0][0]], "1")
    _, _, (out_wq, out_wxo) = finish_a([2, 3], src_a, land_a, [out_wout[0]], "2")

    def unpack(k):
        vec = lambda i: small[i][k]
        return [vec(0), out_b[k][None], small[7][k][None], vec(4), vec(5), small[8][k].reshape(1, HEADS, CH, CH),
                vec(6).reshape(1, HEADS, CH), out_wout[k][None], vec(1), vec(2), out_wq[k][None], out_wkv[k][None],
                out_wxo[k][None], vec(3).reshape(D)]

    return (loss.reshape(()), grad_x[None], *unpack(3), *unpack(0), *unpack(1), *unpack(2))
```

```python
import functools
import math

import jax
import jax.numpy as jnp
from jax import lax
from jax.experimental import pallas as pl
from jax.experimental.pallas import tpu as pltpu

F32 = jnp.float32
BF16 = jnp.bfloat16
MESH = pl.DeviceIdType.MESH

D = 1024
SLAB = 1024
N_SLAB = 7
IN_DIM = N_SLAB * SLAB
MIX = 2 * SLAB
HEADS = 8
CH = 128
XH = 4
XD = D // XH
EPS = 1e-6
GELU_C = math.sqrt(2.0 / math.pi)
GELU_A = 0.044715
N_CHIP = 4
IN_BLK = IN_DIM // N_CHIP
IN_PIECE = 256
N_PIECE = IN_BLK // IN_PIECE
KV_BLK = 2 * D // N_CHIP

ADAM_LR, ADAM_B1, ADAM_B2, ADAM_EPS, ADAM_WD, ADAM_STEP = 0.001, 0.9, 0.999, 1e-08, 0.01, 10

VMEM_LIMIT = 60 * 1024 * 1024


def _cp(sem=None, vmem=None):
    return pltpu.CompilerParams(dimension_semantics=sem, vmem_limit_bytes=vmem)


def _full(shape, buffers=None):
    n = len(shape)
    if buffers is None:
        return pl.BlockSpec(shape, lambda *_: (0,) * n)
    return pl.BlockSpec(shape, lambda *_: (0,) * n, pipeline_mode=pl.Buffered(buffers))


ANY = pl.BlockSpec(memory_space=pl.ANY)


def _bdot(a, b):
    return jnp.dot(a.astype(BF16), b.astype(BF16), preferred_element_type=F32)


def _bdot_nt(a, b):
    return lax.dot_general(a.astype(BF16), b.astype(BF16), (((1,), (1,)), ((), ())), preferred_element_type=F32)


def _bdot_tn(a, b):
    return lax.dot_general(a.astype(BF16), b.astype(BF16), (((0,), (0,)), ((), ())), preferred_element_type=F32)


def _rms(x, g):
    r = lax.rsqrt(jnp.mean(x * x, axis=-1, keepdims=True) + EPS)
    return x * r * g, r


def _rms_bwd(dy, x, r, g):
    gdy = dy * g
    dx = r * gdy - x * (r * r * r) * jnp.mean(x * gdy, axis=-1, keepdims=True)
    dg = jnp.sum(dy * x * r, axis=0, keepdims=True)
    return dx, dg


def _gelu_parts(x):
    x2 = x * x
    t = jnp.tanh(GELU_C * (x + GELU_A * x * x2))
    val = 0.5 * x * (1.0 + t)
    grad = 0.5 * (1.0 + t) + 0.5 * x * (1.0 - t * t) * (GELU_C * (1.0 + 3.0 * GELU_A * x2))
    return val, grad


def _gelu(x):
    return 0.5 * x * (1.0 + jnp.tanh(GELU_C * (x + GELU_A * x * x * x)))


def _sigmoid(z):
    return 1.0 / (1.0 + jnp.exp(-z))


def _cast_shards(b_idx, arrs, gm_ws, gm_bs):
    n = len(arrs)
    steps = 4

    def body(b_ref, *refs):
        ws_ref, bs_ref = refs[n:n + 2]
        outs = refs[n + 2:2 * n + 2]
        wc_ref, wct_ref, bsb_ref = refs[2 * n + 2:]
        for p in range(N_PIECE):
            outs[0][p] = refs[0][:, pl.ds(p * IN_PIECE, IN_PIECE)].astype(BF16)
        for i in range(1, n):
            outs[i][...] = refs[i][...].astype(BF16)

        @pl.when(pl.program_id(0) == 0)
        def _():
            causal = lax.broadcasted_iota(jnp.int32, (CH, CH), 0) >= lax.broadcasted_iota(jnp.int32, (CH, CH), 1)
            for h in range(HEADS):
                w = jnp.where(causal, ws_ref[h], 0.0)
                wc_ref[h] = w.astype(BF16)
                wct_ref[h] = w.T.astype(BF16)
                bsb_ref[h] = jnp.broadcast_to(bs_ref[h:h + 1, :], (CH, CH)).T

    rows = [a.shape[0] // steps for a in arrs]
    in_specs = [pl.BlockSpec((rows[i], a.shape[1]), lambda i, b: (i, 0)) for i, a in enumerate(arrs)]
    in_specs += [pl.BlockSpec((HEADS, CH, CH), lambda i, b: (0, 0, 0)), pl.BlockSpec((HEADS, CH), lambda i, b: (0, 0))]
    out_specs = [pl.BlockSpec((None, N_PIECE, rows[0], IN_PIECE), lambda i, b: (b[0], 0, i, 0))]
    out_specs += [pl.BlockSpec((None, rows[i], a.shape[1]), lambda i, b: (b[0], i, 0)) for i, a in enumerate(arrs) if i > 0]
    out_specs += [pl.BlockSpec((HEADS, CH, CH), lambda i, b: (0, 0, 0))] * 3
    out_shape = [jax.ShapeDtypeStruct((N_CHIP, N_PIECE, arrs[0].shape[0], IN_PIECE), BF16)]
    out_shape += [jax.ShapeDtypeStruct((N_CHIP,) + a.shape, BF16) for a in arrs[1:]]
    out_shape += [jax.ShapeDtypeStruct((HEADS, CH, CH), dt) for dt in (BF16, BF16, F32)]
    outs = pl.pallas_call(
        body, out_shape=out_shape,
        grid_spec=pltpu.PrefetchScalarGridSpec(num_scalar_prefetch=1, grid=(steps,), in_specs=in_specs, out_specs=out_specs),
        compiler_params=_cp(("arbitrary",)), name="cast_shards")(b_idx, *arrs, gm_ws, gm_bs)
    return outs[:n], outs[n:]


def _proj_gather(b_idx, x, g, win_own, cw8s, more, tm=1024):
    t = x.shape[0]
    ni = t // tm
    nm = len(more)
    steps = N_CHIP * N_PIECE
    near0, far0 = N_PIECE, 3 * N_PIECE

    def piece_at(step, own):
        k = step - near0
        near, far = (step >= near0) & (step < far0), step >= far0
        block = jnp.where(far, own ^ 3, jnp.where(near, own ^ jnp.where(lax.rem(k, 2) == 0, 2, 1), own))
        return block, jnp.where(far, step - far0, jnp.where(near, lax.div(k, 2), step))

    def body(*refs):
        b_ref, x_any, g_ref, win_in, cw_in = refs[:5]
        o_ref, hb_any, win_f, cw_out = refs[5 + nm:9 + nm]
        more_out = refs[9 + nm:9 + 2 * nm]
        hbuf, xbuf, wv, cw_s, cw_r, loc = refs[9 + 2 * nm:15 + 2 * nm]
        g_in = _Gather([win_f.at[:, p] for p in range(N_PIECE)], *refs[15 + 2 * nm:19 + 2 * nm])
        g_more = _Gather(more_out, *refs[19 + 2 * nm:23 + 2 * nm])
        s = pl.program_id(0)
        x, y, c, chips = _coords()
        b = 2 * x + y
        blks = [2 * chip[0] + chip[1] for chip in chips]

        def cw_cols(blk):
            return cw_out.at[:, pl.ds(blk * (D // N_CHIP), D // N_CHIP)]

        def cw_copy(k, blk):
            src = cw_in if blk is None else cw_cols(blk)
            return pltpu.make_async_remote_copy(src_ref=src, dst_ref=cw_cols(b if blk is None else blk), send_sem=cw_s.at[k],
                                                recv_sem=cw_r.at[k], device_id=(*chips[k], c), device_id_type=MESH)

        cw_local = pltpu.make_async_copy(cw_in, cw_cols(b), loc.at[1])
        hb_copy = pltpu.make_async_copy(hbuf, hb_any, loc.at[0])

        def load(step):
            slot = lax.rem(step, 2)
            block, piece = piece_at(step, b_ref[0])
            return pltpu.make_async_copy(win_f.at[block, piece], wv.at[slot], loc.at[2 + slot])

        def chunk(i):
            return pltpu.make_async_copy(x_any.at[pl.ds(i * tm, tm)], xbuf.at[i % 2], loc.at[4 + i % 2])

        def first():
            g_in.start()
            cw_local.start()
            for k in range(3):
                cw_copy(k, None).start()
            load(0).start()
            chunk(0).start()
            for i in range(ni):
                if i + 1 < ni:
                    chunk(i + 1).start()
                chunk(i).wait()
                h, _ = _rms(xbuf[i % 2], g_ref[...])
                hbuf[pl.ds(i * tm, tm), :] = h.astype(BF16)
            hb_copy.start()

        events = {step: [] for step in range(steps)}
        events[0].append(first)
        for p in range(N_PIECE):
            events[2 * p + 2].append(functools.partial(g_in.hop, [p]))
            events[near0 + 2 * p - 1].append(functools.partial(g_in.near_ready, [p]))
            events[far0 + p - 2].append(functools.partial(g_in.far, [p]))
            events[far0 + p - 1].append(functools.partial(g_in.far_ready, [p]))
        events[2 * N_PIECE + 1].append(g_more.start)
        for step, todo in events.items():
            if todo:
                @pl.when(s == step)
                def _(todo=todo):
                    for do in todo:
                        do()

        @pl.when(s + 1 < steps)
        def _():
            load(s + 1).start()

        load(s).wait()
        for i in range(ni):
            rows = pl.ds(i * tm, tm)
            o_ref[rows, :] = jnp.dot(hbuf[rows, :], wv[lax.rem(s, 2)], preferred_element_type=F32).astype(BF16)

        @pl.when(s == steps - 1)
        def _():
            g_more.hop()
            g_more.far()
            for k in range(3):
                cw_copy(k, blks[k]).wait_recv()
            for k in range(3):
                cw_copy(k, None).wait_send()
            cw_local.wait()
            hb_copy.wait()
            g_more.near_ready()
            g_more.far_ready()
            g_in.drain()
            g_more.drain()

    def out_col(s, b):
        block, piece = piece_at(s, b[0])
        return 0, block * N_PIECE + piece

    in_specs = [ANY, pl.BlockSpec((1, D), lambda s, b: (0, 0)), ANY, ANY] + [ANY] * nm
    out_specs = [pl.BlockSpec((t, IN_PIECE), out_col), ANY, ANY, ANY] + [ANY] * nm
    outs = pl.pallas_call(
        body, out_shape=[jax.ShapeDtypeStruct((t, IN_DIM), BF16), jax.ShapeDtypeStruct((t, D), BF16),
                         jax.ShapeDtypeStruct(win_own.shape, BF16), jax.ShapeDtypeStruct((8, D), F32)]
        + [jax.ShapeDtypeStruct(f.shape, f.dtype) for f in more],
        grid_spec=pltpu.PrefetchScalarGridSpec(
            num_scalar_prefetch=1, grid=(steps,), in_specs=in_specs, out_specs=out_specs,
            scratch_shapes=[pltpu.VMEM((t, D), BF16), pltpu.VMEM((2, tm, D), F32), pltpu.VMEM((2, D, IN_PIECE), BF16)]
            + [pltpu.SemaphoreType.DMA((3,))] * 2 + [pltpu.SemaphoreType.DMA((6,))]
            + _gather_sems(N_PIECE) + _gather_sems(nm)),
        input_output_aliases={3: 2, **{5 + w: 4 + w for w in range(nm)}},
        compiler_params=_cp(("arbitrary",), VMEM_LIMIT), name="proj_gather")(b_idx, x, g, win_own, cw8s, *more)
    return outs[0], outs[1], outs[2], outs[3], outs[4:]


class _Gather:
    def __init__(self, outs, ici_s, ici_r, d2d_s, d2d_r):
        x, y, c, _ = _coords()
        self.outs, self.c = outs, c
        self.sems = ici_s, ici_r, d2d_s, d2d_r
        self.b, self.bx, self.by, self.bd = 2 * x + y, 2 * (1 - x) + y, 2 * x + (1 - y), 2 * (1 - x) + (1 - y)
        self.xn, self.yn, self.sib = (1 - x, y, c), (x, 1 - y, c), (x, y, 1 - c)

    def piece(self, w, blk, hc, quarter=None):
        hr = self.outs[w].shape[1] // 2
        if quarter is None:
            return self.outs[w].at[blk, pl.ds(hc * hr, hr)]
        return self.outs[w].at[blk, pl.ds(hc * hr + quarter * (hr // 2), hr // 2)]

    def ici(self, w, k, ref, to):
        return pltpu.make_async_remote_copy(src_ref=ref, dst_ref=ref, send_sem=self.sems[0].at[w, k],
                                            recv_sem=self.sems[1].at[w, k], device_id=to, device_id_type=MESH)

    def d2d(self, w, k, ref):
        return pltpu.make_async_remote_copy(src_ref=ref, dst_ref=ref, send_sem=self.sems[2].at[w, k],
                                            recv_sem=self.sems[3].at[w, k], device_id=self.sib, device_id_type=MESH)

    def all(self):
        return range(len(self.outs))

    def start(self):
        for w in self.all():
            mine = self.piece(w, self.b, self.c)
            self.ici(w, 0, mine, self.xn).start()
            self.ici(w, 1, mine, self.yn).start()

    def hop(self, ws=None):
        c = self.c
        for w in ws or self.all():
            self.ici(w, 0, self.piece(w, self.bx, c), self.xn).wait_recv()
            self.ici(w, 1, self.piece(w, self.by, c), self.yn).wait_recv()
            self.ici(w, 2, self.piece(w, self.bx, c, 0), self.yn).start()
            self.ici(w, 3, self.piece(w, self.by, c, 1), self.xn).start()
            self.d2d(w, 0, self.piece(w, self.bx, c)).start()
            self.d2d(w, 1, self.piece(w, self.by, c)).start()

    def near_ready(self, ws=None):
        for w in ws or self.all():
            self.d2d(w, 0, self.piece(w, self.bx, 1 - self.c)).wait_recv()
            self.d2d(w, 1, self.piece(w, self.by, 1 - self.c)).wait_recv()

    def far(self, ws=None):
        c = self.c
        for w in ws or self.all():
            self.ici(w, 2, self.piece(w, self.bd, c, 0), self.yn).wait_recv()
            self.ici(w, 3, self.piece(w, self.bd, c, 1), self.xn).wait_recv()
            self.d2d(w, 2, self.piece(w, self.bd, c, 0)).start()
            self.d2d(w, 3, self.piece(w, self.bd, c, 1)).start()

    def far_ready(self, ws=None):
        for w in ws or self.all():
            self.d2d(w, 2, self.piece(w, self.bd, 1 - self.c, 0)).wait_recv()
            self.d2d(w, 3, self.piece(w, self.bd, 1 - self.c, 1)).wait_recv()

    def drain(self):
        c = self.c
        for w in self.all():
            mine = self.piece(w, self.b, c)
            self.ici(w, 0, mine, self.xn).wait_send()
            self.ici(w, 1, mine, self.yn).wait_send()
            self.ici(w, 2, self.piece(w, self.bx, c, 0), self.yn).wait_send()
            self.ici(w, 3, self.piece(w, self.by, c, 1), self.xn).wait_send()
            self.d2d(w, 0, self.piece(w, self.bx, c)).wait_send()
            self.d2d(w, 1, self.piece(w, self.by, c)).wait_send()
            self.d2d(w, 2, self.piece(w, self.bd, c, 0)).wait_send()
            self.d2d(w, 3, self.piece(w, self.bd, c, 1)).wait_send()


def _gather_sems(nw):
    return [pltpu.SemaphoreType.DMA((max(nw, 1), 4))] * 4


def _mixer_fwd(proj, cw8, lng, lnb, wc, bsb, fulls, tm=256):
    t = proj.shape[0]
    nt = t // tm
    nch = tm // CH
    nw = len(fulls)

    def body(*refs):
        p_ref, cw_ref, lng_ref, lnb_ref, wc_ref, bsb_ref = refs[:6]
        mix_ref = refs[6 + nw]
        w_outs = refs[7 + nw:7 + 2 * nw]
        prev_ref = refs[7 + 2 * nw]
        gather = _Gather(w_outs, *refs[8 + 2 * nw:])

        @pl.when(pl.program_id(0) == 0)
        def _():
            gather.start()
            prev_ref[...] = jnp.zeros_like(prev_ref)

        @pl.when(pl.program_id(0) == nt // 2)
        def _():
            gather.hop()

        @pl.when(pl.program_id(0) == nt - 1)
        def _():
            gather.far()

        rows = lax.broadcasted_iota(jnp.int32, (tm, CH), 0)
        for s in range(HEADS):
            cs = pl.ds(CH * s, CH)

            def slab(k):
                return p_ref[:, pl.ds(k * SLAB + CH * s, CH)].astype(F32)

            gb, gc, xa, za = slab(0), slab(1), slab(2), slab(3)
            cx = gc * xa
            p6 = jnp.broadcast_to(prev_ref[6:7, cs], (tm, CH))
            p7 = jnp.broadcast_to(prev_ref[7:8, cs], (tm, CH))
            c1 = jnp.where(rows == 0, p7, pltpu.roll(cx, 1, 0))
            c2 = jnp.where(rows == 0, p6, jnp.where(rows == 1, p7, pltpu.roll(cx, 2, 0)))
            prev_ref[:, cs] = cx[tm - 8:, :]
            cv = cw_ref[0:1, cs] * c2 + cw_ref[1:2, cs] * c1 + cw_ref[2:3, cs] * cx
            mix_ref[:, cs] = (gb * cv * (za * _sigmoid(za))).astype(BF16)

            u, v, zb = slab(4), slab(5), slab(6)
            ug, vg = _gelu(u), _gelu(v)
            dlt = vg - jnp.mean(vg, axis=-1, keepdims=True)
            vhat = dlt * lax.rsqrt(jnp.mean(dlt * dlt, axis=-1, keepdims=True) + EPS)
            vn = (vhat * lng_ref[:, cs] + lnb_ref[:, cs]).astype(BF16)
            gate = ug * (zb * _sigmoid(zb))
            for c in range(nch):
                rs = slice(CH * c, CH * (c + 1))
                sp = jnp.dot(wc_ref[s], vn[rs], preferred_element_type=F32) + bsb_ref[s]
                mix_ref[rs, pl.ds(SLAB + CH * s, CH)] = (gate[rs] * sp).astype(BF16)

        @pl.when(pl.program_id(0) == nt - 1)
        def _():
            gather.near_ready()
            gather.far_ready()
            gather.drain()

    sems = _gather_sems(nw)
    outs = pl.pallas_call(
        body, grid=(nt,),
        in_specs=[pl.BlockSpec((tm, IN_DIM), lambda i: (i, 0)), _full((8, D)), _full((1, D)), _full((1, D)),
                  _full((HEADS, CH, CH)), _full((HEADS, CH, CH))] + [ANY] * nw,
        out_specs=[pl.BlockSpec((tm, MIX), lambda i: (i, 0))] + [ANY] * nw,
        out_shape=[jax.ShapeDtypeStruct((t, MIX), BF16)] + [jax.ShapeDtypeStruct(f.shape, f.dtype) for f in fulls],
        input_output_aliases={6 + w: 1 + w for w in range(nw)},
        scratch_shapes=[pltpu.VMEM((8, D), F32)] + sems,
        compiler_params=_cp(("arbitrary",), VMEM_LIMIT), name="mixer_fwd")(proj, cw8, lng, lnb, wc, bsb, *fulls)
    return outs[0], outs[1:]


def _mem_fwd(mem, gm, wkv_f):
    n_mem = mem.shape[0]

    def body(mem_ref, gm_ref, w_ref, k_ref, v_ref):
        m, _ = _rms(mem_ref[...], gm_ref[...])
        mb = m.astype(BF16)
        for j in range(N_CHIP):
            dst = k_ref if j < 2 else v_ref
            dst[:, pl.ds(KV_BLK * (j % 2), KV_BLK)] = jnp.dot(mb, w_ref[j], preferred_element_type=F32).astype(BF16)

    return pl.pallas_call(
        body, out_shape=[jax.ShapeDtypeStruct((n_mem, D), BF16), jax.ShapeDtypeStruct((n_mem, D), BF16)],
        compiler_params=_cp(None, VMEM_LIMIT), name="mem_fwd")(mem, gm, wkv_f)


def _tail(x, tgt, mixin, wout, wq, wxo, k, v, g2, g3, tm=512, sub=512):
    t = x.shape[0]
    n_mem = k.shape[0]
    scale = 1.0 / math.sqrt(XD)

    def body(x_ref, tgt_ref, mix_ref, wout_ref, wq_ref, wxo_ref, k_ref, v_ref, g2_ref, g3_ref,
             loss_ref, dmix_ref, dx1b_ref, h2_ref, dq_ref, o_ref, dx2b_ref, dk_ref, dv_ref, dg2_ref, dg3_ref):
        @pl.when(pl.program_id(0) == 0)
        def _():
            loss_ref[...] = jnp.zeros_like(loss_ref)
            dk_ref[...] = jnp.zeros_like(dk_ref)
            dv_ref[...] = jnp.zeros_like(dv_ref)
            dg2_ref[...] = jnp.zeros_like(dg2_ref)
            dg3_ref[...] = jnp.zeros_like(dg3_ref)

        g2, g3 = g2_ref[...], g3_ref[...]
        for sb in range(tm // sub):
            rs = pl.ds(sub * sb, sub)
            x1 = x_ref[rs, :] + jnp.dot(mix_ref[rs, :], wout_ref[...], preferred_element_type=F32)
            h2, r2 = _rms(x1, g2)
            h2b = h2.astype(BF16)
            h2_ref[rs, :] = h2b
            q = jnp.dot(h2b, wq_ref[...], preferred_element_type=F32).astype(BF16)
            probs, outs = [], []
            for hd in range(XH):
                hs = pl.ds(XD * hd, XD)
                s = _bdot_nt(q[:, XD * hd:XD * (hd + 1)], k_ref[:, hs]) * scale
                e = jnp.exp(s - jnp.max(s, axis=-1, keepdims=True))
                p = e / jnp.sum(e, axis=-1, keepdims=True)
                probs.append(p)
                outs.append(_bdot(p, v_ref[:, hs]))
            ob = jnp.concatenate(outs, axis=-1).astype(BF16)
            o_ref[rs, :] = ob
            x2 = x1 + jnp.dot(ob, wxo_ref[...], preferred_element_type=F32)
            y, r3 = _rms(x2, g3)
            diff = y - tgt_ref[rs, :]
            row_loss = jnp.sum(diff * diff, axis=-1, keepdims=True)
            loss_ref[...] += jnp.broadcast_to(jnp.sum(row_loss, axis=0, keepdims=True) * (0.5 / D), loss_ref.shape)

            dx2, dg3 = _rms_bwd(diff * (1.0 / D), x2, r3, g3)
            dg3_ref[...] += dg3
            dx2b = dx2.astype(BF16)
            dx2b_ref[rs, :] = dx2b
            do = _bdot_nt(dx2b, wxo_ref[...])
            dqs = []
            for hd in range(XH):
                hs = pl.ds(XD * hd, XD)
                p = probs[hd]
                do_h = do[:, XD * hd:XD * (hd + 1)]
                dv_ref[:, hs] += _bdot_tn(p, do_h)
                dp = _bdot_nt(do_h, v_ref[:, hs])
                ds = p * (dp - jnp.sum(dp * p, axis=-1, keepdims=True))
                dqs.append(_bdot(ds, k_ref[:, hs]) * scale)
                dk_ref[:, hs] += _bdot_tn(ds, q[:, XD * hd:XD * (hd + 1)]) * scale
            dq = jnp.concatenate(dqs, axis=-1).astype(BF16)
            dq_ref[rs, :] = dq
            dx1n, dg2 = _rms_bwd(_bdot_nt(dq, wq_ref[...]), x1, r2, g2)
            dg2_ref[...] += dg2
            dx1b = (dx2 + dx1n).astype(BF16)
            dx1b_ref[rs, :] = dx1b
            dmix_ref[rs, :] = _bdot_nt(dx1b, wout_ref[...]).astype(BF16)

    tok = lambda w: pl.BlockSpec((tm, w), lambda i: (i, 0))
    return pl.pallas_call(
        body, grid=(t // tm,),
        in_specs=[tok(D), tok(D), tok(MIX), _full((MIX, D), 1), _full((D, D), 1), _full((D, D), 1),
                  _full((n_mem, D), 1), _full((n_mem, D), 1), _full((1, D)), _full((1, D))],
        out_specs=[_full((1, D)), tok(MIX), tok(D), tok(D), tok(D), tok(D), tok(D),
                   _full((n_mem, D)), _full((n_mem, D)), _full((1, D)), _full((1, D))],
        out_shape=[jax.ShapeDtypeStruct((1, D), F32), jax.ShapeDtypeStruct((t, MIX), BF16),
                   jax.ShapeDtypeStruct((t, D), BF16),
                   jax.ShapeDtypeStruct((t, D), BF16), jax.ShapeDtypeStruct((t, D), BF16),
                   jax.ShapeDtypeStruct((t, D), BF16), jax.ShapeDtypeStruct((t, D), BF16),
                   jax.ShapeDtypeStruct((n_mem, D), F32), jax.ShapeDtypeStruct((n_mem, D), F32),
                   jax.ShapeDtypeStruct((1, D), F32), jax.ShapeDtypeStruct((1, D), F32)],
        compiler_params=_cp(("arbitrary",), VMEM_LIMIT), name="tail")(x, tgt, mixin, wout, wq, wxo, k, v, g2, g3)


def _mem_bwd(mem, gm, dk, dv, wkv_f):
    def body(mem_ref, gm_ref, dk_ref, dv_ref, w_ref, dw_ref, dwb_ref, dgm_ref):
        mem_v = mem_ref[...]
        m, rm = _rms(mem_v, gm_ref[...])
        mb = m.astype(BF16)
        dm = jnp.zeros_like(mem_v)
        for j in range(N_CHIP):
            src = dk_ref if j < 2 else dv_ref
            dkv = src[:, pl.ds(KV_BLK * (j % 2), KV_BLK)].astype(BF16)
            dw = _bdot_tn(mb, dkv)
            dw_ref[j] = dw
            dwb_ref[j] = dw.astype(BF16)
            dm = dm + _bdot_nt(dkv, w_ref[j])
        dgm_ref[...] = jnp.sum(dm * mem_v * rm, axis=0, keepdims=True)

    return pl.pallas_call(
        body, out_shape=[jax.ShapeDtypeStruct((N_CHIP, D, KV_BLK), F32), jax.ShapeDtypeStruct((N_CHIP, D, KV_BLK), BF16),
                         jax.ShapeDtypeStruct((1, D), F32)],
        compiler_params=_cp(None, VMEM_LIMIT), name="mem_bwd")(mem, gm, dk, dv, wkv_f)


def _mixer_bwd(proj, dmix, cw8, lng, lnb, wc, wct, bsb, win_f, x, dx1, g1, rows123, row7, tm=256):
    t = proj.shape[0]
    nt = t // tm
    nch = tm // CH
    hb = 16
    pair = 2 * CH
    assert pair == IN_PIECE

    def body(p_ref, pgc_ref, pxa_ref, dm_ref, cw_ref, lng_ref, lnb_ref, wc_ref, wct_ref, bsb_ref, w_ref, x_ref,
             dx1_ref, g1_ref, r1_ref, r2_ref, r3_ref, r7_ref, dp_ref, sv_ref, dwc_ref, gx_ref,
             next_ref, dh_ref):
        i = pl.program_id(0)
        dg1_ref, dlng_ref, dlnb_ref, dbs_ref = (sv_ref.at[pl.ds(r, 1)] for r in (0, 4, 5, 6))
        dcw_ref = sv_ref.at[pl.ds(8, 8)]

        @pl.when(i == 0)
        def _():
            next_ref[...] = jnp.zeros_like(next_ref)
            sv_ref[...] = jnp.zeros_like(sv_ref)
            dwc_ref[...] = jnp.zeros_like(dwc_ref)
            for r, ref in ((1, r1_ref), (2, r2_ref), (3, r3_ref), (7, r7_ref)):
                sv_ref[r:r + 1, :] = ref[...]

        first_tile = i == nt - 1
        rows = lax.broadcasted_iota(jnp.int32, (tm, CH), 0)
        ones8 = jnp.ones((8, CH), BF16)
        for s in range(HEADS):
            cs = pl.ds(CH * s, CH)

            def slab(k):
                return p_ref[:, pl.ds(k * SLAB + CH * s, CH)].astype(F32)

            gb, gc, xa, za = slab(0), slab(1), slab(2), slab(3)
            da = dm_ref[:, cs].astype(F32)
            cx = gc * xa
            cxp = pgc_ref[:, cs].astype(F32) * pxa_ref[:, cs].astype(F32)
            cxp = jnp.where(first_tile, jnp.zeros_like(cxp), cxp)
            p6 = jnp.broadcast_to(cxp[hb - 2:hb - 1, :], (tm, CH))
            p7 = jnp.broadcast_to(cxp[hb - 1:hb, :], (tm, CH))
            c1 = jnp.where(rows == 0, p7, pltpu.roll(cx, 1, 0))
            c2 = jnp.where(rows == 0, p6, jnp.where(rows == 1, p7, pltpu.roll(cx, 2, 0)))
            w0, w1, w2 = cw_ref[0:1, cs], cw_ref[1:2, cs], cw_ref[2:3, cs]
            cv = w0 * c2 + w1 * c1 + w2 * cx
            sg = _sigmoid(za)
            sa = za * sg
            dcv = da * gb * sa
            dp_ref[:, pl.ds(0 * SLAB + CH * s, CH)] = (da * cv * sa).astype(BF16)
            dp_ref[:, pl.ds(3 * SLAB + CH * s, CH)] = (da * gb * cv * (sg * (1.0 + za * (1.0 - sg)))).astype(BF16)
            n0 = jnp.broadcast_to(next_ref[0:1, cs], (tm, CH))
            n1 = jnp.broadcast_to(next_ref[1:2, cs], (tm, CH))
            u1 = jnp.where(rows == tm - 1, n0, pltpu.roll(dcv, tm - 1, 0))
            u2 = jnp.where(rows == tm - 2, n0, jnp.where(rows == tm - 1, n1, pltpu.roll(dcv, tm - 2, 0)))
            next_ref[:, cs] = dcv[0:8, :]
            dcx = w2 * dcv + w1 * u1 + w0 * u2
            dp_ref[:, pl.ds(1 * SLAB + CH * s, CH)] = (dcx * xa).astype(BF16)
            dp_ref[:, pl.ds(2 * SLAB + CH * s, CH)] = (dcx * gc).astype(BF16)
            dcw_ref[0:1, cs] += jnp.sum(dcv * c2, axis=0, keepdims=True)
            dcw_ref[1:2, cs] += jnp.sum(dcv * c1, axis=0, keepdims=True)
            dcw_ref[2:3, cs] += jnp.sum(dcv * cx, axis=0, keepdims=True)

            u, v, zb = slab(4), slab(5), slab(6)
            db = dm_ref[:, pl.ds(SLAB + CH * s, CH)].astype(F32)
            ug, ugrad = _gelu_parts(u)
            vg, vgrad = _gelu_parts(v)
            dlt = vg - jnp.mean(vg, axis=-1, keepdims=True)
            rstd = lax.rsqrt(jnp.mean(dlt * dlt, axis=-1, keepdims=True) + EPS)
            vhat = dlt * rstd
            lg = lng_ref[:, cs]
            vn = (vhat * lg + lnb_ref[:, cs]).astype(BF16)
            sgb = _sigmoid(zb)
            szb = zb * sgb
            sps, dvns = [], []
            dbs = jnp.zeros((8, CH), F32)
            dwc = jnp.zeros((CH, CH), F32)
            for c in range(nch):
                rs = slice(CH * c, CH * (c + 1))
                sp = jnp.dot(wc_ref[s], vn[rs], preferred_element_type=F32) + bsb_ref[s]
                dsp = (db[rs] * ug[rs] * szb[rs]).astype(BF16)
                dbs = dbs + lax.dot_general(ones8, dsp, (((1,), (1,)), ((), ())), preferred_element_type=F32)
                dwc = dwc + lax.dot_general(dsp, vn[rs], (((1,), (1,)), ((), ())), preferred_element_type=F32)
                dvns.append(jnp.dot(wct_ref[s], dsp, preferred_element_type=F32))
                sps.append(sp)
            sp = jnp.concatenate(sps, axis=0)
            dvn = jnp.concatenate(dvns, axis=0)
            dbs_ref[:, cs] += dbs[0:1]
            dwc_ref[s] += dwc
            dlng_ref[:, cs] += jnp.sum(dvn * vhat, axis=0, keepdims=True)
            dlnb_ref[:, cs] += jnp.sum(dvn, axis=0, keepdims=True)
            dvhat = dvn * lg
            dvg = rstd * (dvhat - jnp.mean(dvhat, axis=-1, keepdims=True)
                          - vhat * jnp.mean(dvhat * vhat, axis=-1, keepdims=True))
            dp_ref[:, pl.ds(4 * SLAB + CH * s, CH)] = (db * sp * szb * ugrad).astype(BF16)
            dp_ref[:, pl.ds(5 * SLAB + CH * s, CH)] = (dvg * vgrad).astype(BF16)
            dp_ref[:, pl.ds(6 * SLAB + CH * s, CH)] = (db * ug * sp * (sgb * (1.0 + zb * (1.0 - sgb)))).astype(BF16)

            if s % 2 == 1:
                part = None
                for k in range(N_SLAB):
                    col = k * SLAB + pair * (s // 2)
                    blk, off = divmod(col, IN_BLK)
                    term = lax.dot_general(dp_ref[:, pl.ds(col, pair)], w_ref[blk, off // IN_PIECE],
                                           (((1,), (1,)), ((), ())), preferred_element_type=F32)
                    part = term if part is None else part + term
                if s == 1:
                    dh_ref[...] = part
                else:
                    dh_ref[...] += part

        xv = x_ref[...]
        r = lax.rsqrt(jnp.mean(xv * xv, axis=-1, keepdims=True) + EPS)
        dxn, dg = _rms_bwd(dh_ref[...], xv, r, g1_ref[...])
        gx_ref[...] = dx1_ref[...].astype(F32) + dxn
        dg1_ref[...] += dg

        @pl.when(i == nt - 1)
        def _():
            tril = lax.broadcasted_iota(jnp.int32, (CH, CH), 0) >= lax.broadcasted_iota(jnp.int32, (CH, CH), 1)
            for s in range(HEADS):
                dwc_ref[s] = jnp.where(tril, dwc_ref[s], 0.0)

    rev = lambda i: nt - 1 - i
    halo = lambda col: pl.BlockSpec((hb, SLAB), lambda i: (jnp.maximum(rev(i) * (tm // hb) - 1, 0), col))
    tok = lambda w: pl.BlockSpec((tm, w), lambda i: (rev(i), 0))
    return pl.pallas_call(
        body, grid=(nt,),
        in_specs=[tok(IN_DIM), halo(1), halo(2), tok(MIX), _full((8, D)), _full((1, D)), _full((1, D)),
                  _full((HEADS, CH, CH)), _full((HEADS, CH, CH)), _full((HEADS, CH, CH)),
                  _full((N_CHIP, N_PIECE, D, IN_PIECE), 1), tok(D), tok(D)] + [_full((1, D))] * 5,
        out_specs=[tok(IN_DIM), _full((16, D)), _full((HEADS, CH, CH)), tok(D)],
        out_shape=[jax.ShapeDtypeStruct((t, IN_DIM), BF16), jax.ShapeDtypeStruct((16, D), F32),
                   jax.ShapeDtypeStruct((HEADS, CH, CH), F32), jax.ShapeDtypeStruct((t, D), F32)],
        scratch_shapes=[pltpu.VMEM((8, D), F32), pltpu.VMEM((tm, D), F32)],
        compiler_params=_cp(("arbitrary",), VMEM_LIMIT), name="mixer_bwd")(
            proj, proj, proj, dmix, cw8, lng, lnb, wc, wct, bsb, win_f, x, dx1, g1, *rows123, row7)


def _grad_matmul(a, b, after, *, name, tk=1024):
    t, m = a.shape
    n = b.shape[1]
    nk = t // tk

    def body(a_ref, b_ref, after_ref, o_ref, ob_ref):
        kk = pl.program_id(0)
        part = lax.dot_general(a_ref[...], b_ref[...], (((0,), (0,)), ((), ())), preferred_element_type=F32)

        @pl.when(kk == 0)
        def _():
            o_ref[...] = part

        @pl.when(kk > 0)
        def _():
            o_ref[...] += part

        @pl.when(kk == nk - 1)
        def _():
            ob_ref[...] = o_ref[...].astype(BF16)

    o_spec = pl.BlockSpec((m, n), lambda k: (0, 0))
    o32, o16 = pl.pallas_call(
        body, grid=(nk,), in_specs=[pl.BlockSpec((tk, m), lambda k: (k, 0)), pl.BlockSpec((tk, n), lambda k: (k, 0)), ANY],
        out_specs=[o_spec, o_spec], out_shape=[jax.ShapeDtypeStruct((m, n), F32), jax.ShapeDtypeStruct((m, n), BF16)],
        compiler_params=_cp(("arbitrary",), VMEM_LIMIT), name=name)(a, b, after)
    return o32.reshape(N_CHIP, m // N_CHIP, n), o16.reshape(N_CHIP, m // N_CHIP, n)


def _coords():
    x, y, c = lax.axis_index("x"), lax.axis_index("y"), lax.axis_index("c")
    chips = [(1 - x, y), (x, 1 - y), (1 - x, 1 - y)]
    return x, y, c, chips


def _pair_reduce(c_idx, grads, grads_b, smalls, name):
    ng, ns = len(grads), len(smalls)
    halves = [g.shape[1] // 2 for g in grads]

    def body(c_ref, *refs):
        g_in, gb_any = refs[:ng], refs[ng:2 * ng]
        s_own, s_any = refs[2 * ng:2 * ng + ns], refs[2 * ng + ns:2 * ng + 2 * ns]
        o = refs[2 * ng + 2 * ns:4 * ng + 3 * ns]
        lands = refs[4 * ng + 3 * ns:5 * ng + 4 * ns]
        send, recv = refs[5 * ng + 4 * ns:]
        x, y, c, _ = _coords()
        j = pl.program_id(0)

        def big(i, blk):
            return pltpu.make_async_remote_copy(
                src_ref=gb_any[i].at[blk, pl.ds((1 - c) * halves[i], halves[i])], dst_ref=lands[i].at[blk],
                send_sem=send.at[i, blk], recv_sem=recv.at[i, blk], device_id=(x, y, 1 - c), device_id_type=MESH)

        def small(i):
            return pltpu.make_async_remote_copy(
                src_ref=s_any[i].at[1 - c], dst_ref=lands[ng + i],
                send_sem=send.at[ng + i, 0], recv_sem=recv.at[ng + i, 0], device_id=(x, y, 1 - c), device_id_type=MESH)

        @pl.when(j == 0)
        def _():
            for blk in range(N_CHIP):
                for i in range(ng):
                    big(i, blk).start()
            for i in range(ns):
                small(i).start()

        for i in range(ng):
            big(i, j).wait_recv()
            tot = g_in[i][...] + lands[i][j].astype(F32)
            o[i][...] = tot
            o[ng + i][...] = tot.astype(BF16)

        @pl.when(j == N_CHIP - 1)
        def _():
            for i in range(ns):
                small(i).wait_recv()
                o[2 * ng + i][...] = s_own[i][...] + lands[ng + i][...]
                small(i).wait_send()
            for blk in range(N_CHIP):
                for i in range(ng):
                    big(i, blk).wait_send()

    in_specs = [pl.BlockSpec((None, None, halves[i], g.shape[2]), lambda b, c: (b, c[0], 0, 0)) for i, g in enumerate(grads)]
    in_specs += [ANY] * ng
    in_specs += [pl.BlockSpec((None, s.shape[0] // 2, s.shape[1]), lambda b, c: (c[0], 0, 0)) for s in smalls]
    in_specs += [ANY] * ns
    blk = [pl.BlockSpec((None, halves[i], g.shape[2]), lambda b, c: (b, 0, 0)) for i, g in enumerate(grads)]
    out_specs = blk + blk + [pl.BlockSpec((s.shape[0] // 2, s.shape[1]), lambda b, c: (0, 0)) for s in smalls]
    out_shape = [jax.ShapeDtypeStruct((N_CHIP, halves[i], g.shape[2]), F32) for i, g in enumerate(grads)]
    out_shape += [jax.ShapeDtypeStruct((N_CHIP, halves[i], g.shape[2]), BF16) for i, g in enumerate(grads)]
    out_shape += [jax.ShapeDtypeStruct((s.shape[0] // 2, s.shape[1]), F32) for s in smalls]
    scratch = [pltpu.VMEM((N_CHIP, halves[i], g.shape[2]), BF16) for i, g in enumerate(grads)]
    scratch += [pltpu.VMEM((s.shape[0] // 2, s.shape[1]), F32) for s in smalls]
    scratch += [pltpu.SemaphoreType.DMA((ng + ns, N_CHIP)), pltpu.SemaphoreType.DMA((ng + ns, N_CHIP))]
    grads4 = [g.reshape(N_CHIP, 2, halves[i], g.shape[2]) for i, g in enumerate(grads)]
    smalls3 = [s.reshape(2, s.shape[0] // 2, s.shape[1]) for s in smalls]
    return pl.pallas_call(
        body, out_shape=out_shape,
        grid_spec=pltpu.PrefetchScalarGridSpec(num_scalar_prefetch=1, grid=(N_CHIP,), in_specs=in_specs,
                                               out_specs=out_specs, scratch_shapes=scratch),
        compiler_params=_cp(("arbitrary",), VMEM_LIMIT), name=name)(c_idx, *grads4, *grads_b, *smalls3, *smalls3)


def _grad_matmul_pair(c_idx, a, b, smalls, small_dtypes, after, *, name, tk=2048):
    t, m = a.shape
    bn = b.shape[1] // N_CHIP
    nk = t // tk
    hr = m // 2
    ns = len(smalls)

    def body(c_ref, a_ref, b_ref, *refs):
        s_own, s_any = refs[:ns], refs[ns:2 * ns]
        o32, o16 = refs[2 * ns + 1], refs[2 * ns + 2]
        o_small = refs[2 * ns + 3:3 * ns + 3]
        acc, tb, land, st16 = refs[3 * ns + 3:3 * ns + 7]
        s_land, s_stage = refs[3 * ns + 7:4 * ns + 7], refs[4 * ns + 7:5 * ns + 7]
        send, recv, loc = refs[5 * ns + 7:]
        x, y, c, _ = _coords()
        sibling = dict(device_id=(x, y, 1 - c), device_id_type=MESH)
        j, kk = pl.program_id(0), pl.program_id(1)
        mine = pl.ds(pl.multiple_of(c * hr, hr), hr)
        theirs = pl.ds(pl.multiple_of((1 - c) * hr, hr), hr)

        def to_sibling(blk):
            return pltpu.make_async_remote_copy(src_ref=tb, dst_ref=land.at[blk], send_sem=send.at[blk],
                                                recv_sem=recv.at[blk], **sibling)

        def small(i):
            return pltpu.make_async_remote_copy(src_ref=s_any[i].at[1 - c], dst_ref=s_land[i], send_sem=send.at[N_CHIP + i],
                                                recv_sem=recv.at[N_CHIP + i], **sibling)

        def written(blk):
            return (pltpu.make_async_copy(acc.at[blk % 2, mine], o32.at[blk], loc.at[0]),
                    pltpu.make_async_copy(st16, o16.at[blk], loc.at[1]))

        def finish(blk):
            to_sibling(blk).wait_recv()

            @pl.when(blk > 0)
            def _():
                for cp in written(blk - 1):
                    cp.wait()

            tot = acc[blk % 2, mine, :] + land[blk].astype(F32)
            acc[blk % 2, mine, :] = tot
            st16[...] = tot.astype(BF16)
            for cp in written(blk):
                cp.start()

        def small_out(i):
            return pltpu.make_async_copy(s_stage[i], o_small[i], loc.at[2 + i])

        @pl.when((j == 0) & (kk == 0))
        def _():
            for i in range(ns):
                small(i).start()

        @pl.when((j == 1) & (kk == 0))
        def _():
            for i in range(ns):
                small(i).wait_recv()
                s_stage[i][...] = (s_own[i][...] + s_land[i][...]).astype(small_dtypes[i])
                small_out(i).start()

        @pl.when((j > 0) & (kk == 0))
        def _():
            finish(j - 1)

        part = lax.dot_general(a_ref[...], b_ref[...], (((0,), (0,)), ((), ())), preferred_element_type=F32)
        slot = lax.rem(j, 2)

        @pl.when(kk == 0)
        def _():
            acc[slot] = part

        @pl.when(kk > 0)
        def _():
            acc[slot] += part

        @pl.when(kk == nk - 1)
        def _():
            @pl.when(j > 0)
            def _():
                to_sibling(j - 1).wait_send()

            tb[...] = acc[slot, theirs, :].astype(BF16)
            to_sibling(j).start()

        @pl.when((j == N_CHIP - 1) & (kk == nk - 1))
        def _():
            finish(j)
            for i in range(ns):
                small_out(i).wait()
                small(i).wait_send()
            for cp in written(j):
                cp.wait()
            to_sibling(j).wait_send()

    halves = [(s.shape[0] // 2, s.shape[1]) for s in smalls]
    in_specs = [pl.BlockSpec((tk, m), lambda j, k, c: (k, 0)), pl.BlockSpec((tk, bn), lambda j, k, c: (k, j))]
    in_specs += [pl.BlockSpec((None,) + h, lambda j, k, c: (c[0], 0, 0)) for h in halves] + [ANY] * ns + [ANY]
    out_shape = [jax.ShapeDtypeStruct((N_CHIP, hr, bn), F32), jax.ShapeDtypeStruct((N_CHIP, hr, bn), BF16)]
    out_shape += [pltpu.HBM(h, dt) for h, dt in zip(halves, small_dtypes)]
    scratch = [pltpu.VMEM((2, m, bn), F32), pltpu.VMEM((hr, bn), BF16),
               pltpu.VMEM((N_CHIP, hr, bn), BF16), pltpu.VMEM((hr, bn), BF16)]
    scratch += [pltpu.VMEM(h, F32) for h in halves] + [pltpu.VMEM(h, dt) for h, dt in zip(halves, small_dtypes)]
    scratch += [pltpu.SemaphoreType.DMA((N_CHIP + ns,)), pltpu.SemaphoreType.DMA((N_CHIP + ns,)),
                pltpu.SemaphoreType.DMA((2 + ns,))]
    smalls3 = [s.reshape((2,) + h) for s, h in zip(smalls, halves)]
    outs = pl.pallas_call(
        body, out_shape=out_shape,
        grid_spec=pltpu.PrefetchScalarGridSpec(num_scalar_prefetch=1, grid=(N_CHIP, nk), in_specs=in_specs,
                                               out_specs=[ANY, ANY] + [_HBM] * ns, scratch_shapes=scratch),
        compiler_params=_cp(("arbitrary", "arbitrary"), VMEM_LIMIT), name=name)(c_idx, a, b, *smalls3, *smalls3, after)
    return outs[0], outs[1], list(outs[2:])


_HBM = pl.BlockSpec(memory_space=pltpu.HBM)
_SEM = pl.BlockSpec(memory_space=pltpu.SEMAPHORE)


def _split_copies(ins, lands, ng, send, recv, arriving):
    x, y, c, chips = _coords()
    b = 2 * x + y
    copies = []
    for i in range(len(ins)):
        for k in range(3):
            blk = 2 * chips[k][0] + chips[k][1]
            src, dst, got = (ins[i].at[blk], lands[i].at[k], lands[i].at[k]) if i < ng else (ins[i], lands[i].at[b], lands[i].at[blk])
            sems = dict(send_sem=send.at[3 * i + k], recv_sem=recv.at[3 * i + k], device_id=(*chips[k], c), device_id_type=MESH)
            if arriving:
                copies.append(pltpu.make_async_remote_copy(src_ref=got, dst_ref=got, **sems))
            else:
                copies.append(pltpu.make_async_remote_copy(src_ref=src, dst_ref=dst, **sems))
    return copies


def _exchange_begin(sums_b, smalls, name):
    ng, n = len(sums_b), len(sums_b) + len(smalls)
    srcs = list(sums_b) + list(smalls)
    lands = [lax.empty((3,) + g.shape[1:], g.dtype) for g in sums_b] + [lax.empty((N_CHIP,) + s.shape, s.dtype) for s in smalls]

    def body(*refs):
        ins, land_refs = refs[:n], refs[n:2 * n]
        send, recv = refs[2 * n], refs[2 * n + 1]
        token = refs[4 * n + 2]
        for cp in _split_copies(ins, land_refs, ng, send, recv, False):
            cp.start()
        token[...] = jnp.zeros_like(token)

    hbm = lambda a: pltpu.HBM(a.shape, a.dtype)
    outs = pl.pallas_call(
        body, name=name,
        out_shape=(pltpu.SemaphoreType.DMA((3 * n,)), pltpu.SemaphoreType.DMA((3 * n,)), *[hbm(a) for a in srcs + lands],
                   jax.ShapeDtypeStruct((8, 128), F32)),
        in_specs=[_HBM] * (2 * n), out_specs=(_SEM, _SEM, *[_HBM] * (2 * n), pl.BlockSpec(memory_space=pltpu.VMEM)),
        input_output_aliases={i: 2 + i for i in range(2 * n)},
        compiler_params=pltpu.CompilerParams(has_side_effects=pltpu.SideEffectType.DATAFLOW_SIDE_EFFECTING),
    )(*[pltpu.with_memory_space_constraint(a, pltpu.HBM) for a in srcs + lands])
    return outs[0], outs[1], list(outs[2:2 + n]), list(outs[2 + n:2 + 2 * n]), outs[2 + 2 * n]


def _exchange_end(send, recv, srcs, lands, ng, which, after, name):
    n = len(srcs)
    after = list(after)

    def body(*refs):
        ins, land_refs = refs[:n], refs[n:2 * n]
        send_ref, recv_ref = refs[2 * n], refs[2 * n + 1]
        outgoing = _split_copies(ins, land_refs, ng, send_ref, recv_ref, False)
        arriving = _split_copies(ins, land_refs, ng, send_ref, recv_ref, True)
        for i in which:
            for cp in outgoing[3 * i:3 * i + 3]:
                cp.wait_send()
        for i in which:
            for cp in arriving[3 * i:3 * i + 3]:
                cp.wait_recv()

    hbm = lambda a: pltpu.HBM(a.shape, a.dtype)
    outs = pl.pallas_call(
        body, name=name, out_shape=tuple(hbm(a) for a in list(srcs) + list(lands)),
        in_specs=[_HBM] * (2 * n) + [_SEM, _SEM] + [ANY] * len(after), out_specs=tuple([_HBM] * (2 * n)),
        input_output_aliases={i: i for i in range(2 * n)},
        compiler_params=pltpu.CompilerParams(has_side_effects=pltpu.SideEffectType.DATAFLOW_SIDE_EFFECTING),
    )(*srcs, *lands, send, recv, *after)
    return list(outs[:n]), list(outs[n:])


def _chip_reduce(bc_idx, sums, recvd, smalls_slots, smalls_own, name, steps=4):
    ng, ns = len(sums), len(smalls_slots)
    n = ng + ns
    assert steps >= 2
    halves = [g.shape[1] for g in sums] + [s.shape[1] for s in smalls_slots]
    rows = [g.shape[1] // steps for g in sums]

    def body(bc_ref, *refs):
        own, rx = refs[:ng], refs[ng:2 * ng]
        sl = refs[2 * ng:2 * ng + ns]
        sl_own = refs[2 * ng + ns:2 * ng + 2 * ns]
        o = refs[2 * ng + 2 * ns:2 * ng + 2 * ns + n]
        tiles = refs[2 * ng + 2 * ns + n:2 * ng + 2 * ns + 2 * n]
        keep, send, recv = refs[2 * ng + 2 * ns + 2 * n:]
        x, y, c, _ = _coords()
        sibling = dict(device_id=(x, y, 1 - c), device_id_type=MESH)
        r = pl.program_id(0)

        def writes(i, step, slot):
            dst = o[i].at[pl.ds(c * halves[i] + step * rows[i], rows[i])]
            return (pltpu.make_async_copy(tiles[i].at[slot], dst, keep.at[i, slot]),
                    pltpu.make_async_remote_copy(src_ref=tiles[i].at[slot], dst_ref=dst, send_sem=send.at[i, slot],
                                                 recv_sem=recv.at[i, step], **sibling))

        def small_writes(i):
            dst = o[i].at[pl.ds(c * halves[i], halves[i])]
            return (pltpu.make_async_copy(tiles[i], dst, keep.at[i, 0]),
                    pltpu.make_async_remote_copy(src_ref=tiles[i], dst_ref=dst, send_sem=send.at[i, 0],
                                                 recv_sem=recv.at[i, 0], **sibling))

        def arriving(i, step, nrows):
            dst = o[i].at[pl.ds((1 - c) * halves[i] + step * nrows, nrows)]
            return pltpu.make_async_remote_copy(src_ref=dst, dst_ref=dst, send_sem=send.at[i, 0], recv_sem=recv.at[i, step],
                                                **sibling)

        def finish(step, slot):
            for i in range(ng):
                local, remote = writes(i, step, slot)
                local.wait()
                remote.wait_send()

        @pl.when(r >= 2)
        def _():
            finish(r - 2, r % 2)

        for i in range(ng):
            tot = own[i][...]
            for j in range(3):
                tot = tot + rx[i][j].astype(F32)
            tiles[i][r % 2] = tot
            for cp in writes(i, r, r % 2):
                cp.start()

        @pl.when(r == 0)
        def _():
            for i in range(ns):
                term = [jnp.where(bc_ref[0] == kk, sl_own[i][...], sl[i][kk]).astype(F32) for kk in range(N_CHIP)]
                tiles[ng + i][...] = ((term[0] + term[1]) + term[2]) + term[3]
                for cp in small_writes(ng + i):
                    cp.start()

        @pl.when(r == steps - 1)
        def _():
            finish(steps - 2, (steps - 2) % 2)
            finish(steps - 1, (steps - 1) % 2)
            for i in range(ns):
                local, remote = small_writes(ng + i)
                local.wait()
                remote.wait_send()
                arriving(ng + i, 0, halves[ng + i]).wait_recv()
            for i in range(ng):
                for step in range(steps):
                    arriving(i, step, rows[i]).wait_recv()

    in_specs = [pl.BlockSpec((None, rows[i], g.shape[2]), lambda r, bc: (bc[0], r, 0)) for i, g in enumerate(sums)]
    in_specs += [pl.BlockSpec((3, rows[i], g.shape[2]), lambda r, bc: (0, r, 0)) for i, g in enumerate(sums)]
    in_specs += [pl.BlockSpec(s.shape, lambda r, bc: (0, 0, 0)) for s in smalls_slots]
    in_specs += [pl.BlockSpec(s.shape[1:], lambda r, bc: (0, 0)) for s in smalls_slots]
    out_shape = [jax.ShapeDtypeStruct((2 * g.shape[1], g.shape[2]), F32) for g in sums]
    out_shape += [jax.ShapeDtypeStruct((2 * s.shape[1], s.shape[2]), F32) for s in smalls_slots]
    scratch = [pltpu.VMEM((2, rows[i], g.shape[2]), F32) for i, g in enumerate(sums)]
    scratch += [pltpu.VMEM(s.shape[1:], F32) for s in smalls_slots]
    scratch += [pltpu.SemaphoreType.DMA((n, 2)), pltpu.SemaphoreType.DMA((n, 2)), pltpu.SemaphoreType.DMA((n, steps))]
    return list(pl.pallas_call(
        body, out_shape=out_shape,
        grid_spec=pltpu.PrefetchScalarGridSpec(num_scalar_prefetch=1, grid=(steps,), in_specs=in_specs,
                                               out_specs=[ANY] * n, scratch_shapes=scratch),
        compiler_params=_cp(("arbitrary",), VMEM_LIMIT), name=name)(bc_idx, *sums, *recvd, *smalls_slots, *smalls_own))


def _adamw_math(w, g, m, v):
    m2 = ADAM_B1 * m + (1.0 - ADAM_B1) * g
    v2 = ADAM_B2 * v + (1.0 - ADAM_B2) * (g * g)
    m_hat = m2 / (1.0 - ADAM_B1 ** ADAM_STEP)
    v_hat = v2 / (1.0 - ADAM_B2 ** ADAM_STEP)
    delta = -ADAM_LR * (m_hat / (jnp.sqrt(v_hat) + ADAM_EPS) + ADAM_WD * w)
    return delta, m2, v2


def _adamw_big(ws, gs, ms, vs, name, steps=8):
    n = len(ws)

    def body(*refs):
        for i in range(n):
            w_ref, g_ref, m_ref, v_ref = (refs[k * n + i] for k in range(4))
            d_ref, m2_ref, v2_ref, g2_ref = (refs[(4 + k) * n + i] for k in range(4))
            gv = g_ref[...]
            d_ref[...], m2_ref[...], v2_ref[...] = _adamw_math(w_ref[...], gv, m_ref[...], v_ref[...])
            g2_ref[...] = gv

    specs = [pl.BlockSpec((w.shape[0] // steps, w.shape[1]), lambda i: (i, 0)) for w in ws]
    shapes = [jax.ShapeDtypeStruct(w.shape, F32) for w in ws]
    outs = pl.pallas_call(
        body, grid=(steps,), in_specs=specs * 4, out_specs=specs * 4, out_shape=shapes * 4,
        compiler_params=_cp(("parallel",), VMEM_LIMIT), name=name)(*ws, *gs, *ms, *vs)
    return [tuple(outs[k * n + i] for k in range(4)) for i in range(n)]


def _adamw_small(b_idx, sv, sw, vecs, conv, ws):
    nv = len(vecs)
    cols = conv[0].shape[1]

    def body(b_ref, sv_ref, sw_ref, *refs):
        ins, outs = refs[:3 * nv + 6], refs[3 * nv + 6:]
        for i in range(nv):
            g = sv_ref[i:i + 1, :]
            w_ref, m_ref, v_ref = ins[3 * i:3 * i + 3]
            d_ref, m2_ref, v2_ref, g_ref = outs[4 * i:4 * i + 4]
            d_ref[...], m2_ref[...], v2_ref[...] = _adamw_math(w_ref[...], g, m_ref[...], v_ref[...])
            g_ref[...] = g
        g = sv_ref[8:8 + conv[0].shape[0], pl.ds(pl.multiple_of(b_ref[0] * cols, cols), cols)]
        w_ref, m_ref, v_ref = ins[3 * nv:3 * nv + 3]
        d_ref, m2_ref, v2_ref, g_ref = outs[4 * nv:4 * nv + 4]
        d_ref[...], m2_ref[...], v2_ref[...] = _adamw_math(w_ref[...], g, m_ref[...], v_ref[...])
        g_ref[...] = g
        w_ref, m_ref, v_ref = ins[3 * nv + 3:]
        d_ref, m2_ref, v2_ref, g_ref, one_ref = outs[4 * nv + 4:]
        g = sw_ref[...]
        d_ref[...], m2_ref[...], v2_ref[...] = _adamw_math(w_ref[...], g, m_ref[...], v_ref[...])
        g_ref[...] = g
        one_ref[...] = sv_ref[nv:nv + 1, 0:1]

    flat = [a for grp in vecs for a in grp] + list(conv) + list(ws)
    out_shape = [jax.ShapeDtypeStruct(grp[0].shape, F32) for grp in list(vecs) + [conv, ws] for _ in range(4)]
    out_shape += [jax.ShapeDtypeStruct((1, 1), F32)]
    vmem = pl.BlockSpec(memory_space=pltpu.VMEM)
    outs = pl.pallas_call(
        body, out_shape=out_shape, in_specs=[pl.BlockSpec(memory_space=pltpu.SMEM)] + [vmem] * (2 + len(flat)),
        out_specs=[vmem] * len(out_shape), name="adamw_small")(b_idx, sv, sw, *flat)
    return [tuple(outs[4 * i:4 * i + 4]) for i in range(nv + 2)], outs[4 * nv + 8]


def kernel(x, mem, norm_mix_g, w_in, conv_w, gm_ln_g, gm_ln_b, gm_ws, gm_bs, w_out, norm_x_g, norm_mem_g, w_q, w_kv, w_xo, norm_final_g, loss_target, m_norm_mix_g, m_w_in, m_conv_w, m_gm_ln_g, m_gm_ln_b, m_gm_ws, m_gm_bs, m_w_out, m_norm_x_g, m_norm_mem_g, m_w_q, m_w_kv, m_w_xo, m_norm_final_g, v_norm_mix_g, v_w_in, v_conv_w, v_gm_ln_g, v_gm_ln_b, v_gm_ws, v_gm_bs, v_w_out, v_norm_x_g, v_norm_mem_g, v_w_q, v_w_kv, v_w_xo, v_norm_final_g):
    t = x.shape[1]
    xi = lax.axis_index("x")
    yi = lax.axis_index("y")
    ci = lax.axis_index("c")
    b_idx = jnp.reshape(2 * xi + yi, (1,)).astype(jnp.int32)
    c_idx = jnp.reshape(ci, (1,)).astype(jnp.int32)

    x2d, mem2d, tgt = x[0], mem[0], loss_target[0]
    big = [w_in[0], w_out[0], w_q[0], w_kv[0], w_xo[0]]
    big_m = [m_w_in[0], m_w_out[0], m_w_q[0], m_w_kv[0], m_w_xo[0]]
    big_v = [v_w_in[0], v_w_out[0], v_w_q[0], v_w_kv[0], v_w_xo[0]]
    g3 = norm_final_g.reshape(1, D)

    def pad8(a):
        return jnp.pad(a, ((0, 8 - a.shape[0]), (0, 0)))

    own_blocks, (wc, wct, bsb) = _cast_shards(b_idx, big, gm_ws[0], gm_bs[0])

    proj, hb, win_f, cw8, (wq_f,) = _proj_gather(
        b_idx, x2d, norm_mix_g, own_blocks[0], pad8(conv_w[0]), [own_blocks[2]])
    mixin, (wout_f, wkv_f, wxo_f) = _mixer_fwd(
        proj, cw8, gm_ln_g, gm_ln_b, wc, bsb, [own_blocks[1], own_blocks[3], own_blocks[4]])
    wout2, wq2, wxo2 = wout_f.reshape(MIX, D), wq_f.reshape(D, D), wxo_f.reshape(D, D)
    k, v = _mem_fwd(mem2d, norm_mem_g, wkv_f)

    (loss_row, dmix, dx1b, h2b, dq, ob, dx2b, dk, dv, dg2, dg3) = _tail(
        x2d, tgt, mixin, wout2, wq2, wxo2, k, v, norm_x_g, g3)
    dwkv, dwkv_b, dgm = _mem_bwd(mem2d, norm_mem_g, dk, dv, wkv_f)
    dproj, sv, dwc, grad_x = _mixer_bwd(
        proj, dmix, cw8, gm_ln_g, gm_ln_b, wc, wct, bsb, win_f, x2d, dx1b, norm_mix_g, [dg2, dgm, dg3], loss_row)

    bc_idx = jnp.concatenate([b_idx, c_idx])
    sw = dwc.reshape(HEADS * CH, CH)
    dwin_sum, dwin_sum_b, psmall = _grad_matmul_pair(c_idx, hb, dproj, [sv, sw], [F32, BF16], dgm, name="grad_w_in")
    sums_b = [dwin_sum]
    send_b, recv_b, src_b, land_b, token_b = _exchange_begin([dwin_sum_b], psmall, "exchange_b_begin")

    dwxo, dwxo_b = _grad_matmul(ob, dx2b, token_b, name="grad_w_xo", tk=2048)
    dwq, dwq_b = _grad_matmul(h2b, dq, token_b, name="grad_w_q", tk=2048)
    dwout, dwout_b = _grad_matmul(mixin, dx1b, token_b, name="grad_w_out")
    ps_a = _pair_reduce(c_idx, [dwout, dwkv, dwq, dwxo], [dwout_b, dwkv_b, dwq_b, dwxo_b], [], "pair_reduce_a")
    sums_a, sums_a_b = list(ps_a[:4]), list(ps_a[4:8])
    send_a, recv_a, src_a, land_a, token_a = _exchange_begin(sums_a_b, [], "exchange_a_begin")

    src_b, rx2b = _exchange_end(send_b, recv_b, src_b, land_b, 1, [0, 1, 2], [token_a], "exchange_b_end")
    gwin, svf, swf = _chip_reduce(bc_idx, sums_b, rx2b[:1], rx2b[1:], src_b[1:], "chip_reduce_b")
    out_b = _adamw_big(big[:1], [gwin], big_m[:1], big_v[:1], "adamw_w_in")[0]

    row = lambda a: a.reshape(1, D)
    mat = lambda a: a.reshape(HEADS * CH, CH)
    small, loss = _adamw_small(
        b_idx, svf, swf,
        [(row(norm_mix_g), row(m_norm_mix_g), row(v_norm_mix_g)), (row(norm_x_g), row(m_norm_x_g), row(v_norm_x_g)),
         (row(norm_mem_g), row(m_norm_mem_g), row(v_norm_mem_g)), (row(norm_final_g), row(m_norm_final_g), row(v_norm_final_g)),
         (row(gm_ln_g), row(m_gm_ln_g), row(v_gm_ln_g)), (row(gm_ln_b), row(m_gm_ln_b), row(v_gm_ln_b)),
         (row(gm_bs), row(m_gm_bs), row(v_gm_bs))],
        (conv_w[0], m_conv_w[0], v_conv_w[0]), (mat(gm_ws), mat(m_gm_ws), mat(v_gm_ws)))

    def finish_a(part, src, land, after, tag):
        src, land = _exchange_end(send_a, recv_a, src, land, 4, part, after, "exchange_a%s_end" % tag)
        grads = _chip_reduce(bc_idx, [sums_a[i] for i in part], [land[i] for i in part], [], [], "chip_reduce_a" + tag,
                             steps=2)
        ids = [(1, 3, 2, 4)[i] for i in part]
        outs = _adamw_big([big[i] for i in ids], grads, [big_m[i] for i in ids], [big_v[i] for i in ids], "adamw_a" + tag,
                          steps=4)
        return src, land, outs

    src_a, land_a, (out_wout, out_wkv) = finish_a([0, 1], src_a, land_a, [out_b[0], small[0][0]], "1")
    _, _, (out_wq, out_wxo) = finish_a([2, 3], src_a, land_a, [out_wout[0]], "2")

    def unpack(k):
        vec = lambda i: small[i][k]
        return [vec(0), out_b[k][None], small[7][k][None], vec(4), vec(5), small[8][k].reshape(1, HEADS, CH, CH),
                vec(6).reshape(1, HEADS, CH), out_wout[k][None], vec(1), vec(2), out_wq[k][None], out_wkv[k][None],
                out_wxo[k][None], vec(3).reshape(D)]

    return (loss.reshape(()), grad_x[None], *unpack(3), *unpack(0), *unpack(1), *unpack(2))
```

```python
import functools
import math

import jax
import jax.numpy as jnp
from jax import lax
from jax.experimental import pallas as pl
from jax.experimental.pallas import tpu as pltpu

F32 = jnp.float32
BF16 = jnp.bfloat16
MESH = pl.DeviceIdType.MESH

D = 1024
SLAB = 1024
N_SLAB = 7
IN_DIM = N_SLAB * SLAB
MIX = 2 * SLAB
HEADS = 8
CH = 128
XH = 4
XD = D // XH
EPS = 1e-6
GELU_C = math.sqrt(2.0 / math.pi)
GELU_A = 0.044715
N_CHIP = 4
IN_BLK = IN_DIM // N_CHIP
IN_PIECE = 256
N_PIECE = IN_BLK // IN_PIECE
KV_BLK = 2 * D // N_CHIP

ADAM_LR, ADAM_B1, ADAM_B2, ADAM_EPS, ADAM_WD, ADAM_STEP = 0.001, 0.9, 0.999, 1e-08, 0.01, 10

VMEM_LIMIT = 60 * 1024 * 1024


def _cp(sem=None, vmem=None):
    return pltpu.CompilerParams(dimension_semantics=sem, vmem_limit_bytes=vmem)


def _full(shape, buffers=None):
    n = len(shape)
    if buffers is None:
        return pl.BlockSpec(shape, lambda *_: (0,) * n)
    return pl.BlockSpec(shape, lambda *_: (0,) * n, pipeline_mode=pl.Buffered(buffers))


ANY = pl.BlockSpec(memory_space=pl.ANY)


def _bdot(a, b):
    return jnp.dot(a.astype(BF16), b.astype(BF16), preferred_element_type=F32)


def _bdot_nt(a, b):
    return lax.dot_general(a.astype(BF16), b.astype(BF16), (((1,), (1,)), ((), ())), preferred_element_type=F32)


def _bdot_tn(a, b):
    return lax.dot_general(a.astype(BF16), b.astype(BF16), (((0,), (0,)), ((), ())), preferred_element_type=F32)


def _rms(x, g):
    r = lax.rsqrt(jnp.mean(x * x, axis=-1, keepdims=True) + EPS)
    return x * r * g, r


def _rms_bwd(dy, x, r, g):
    gdy = dy * g
    dx = r * gdy - x * (r * r * r) * jnp.mean(x * gdy, axis=-1, keepdims=True)
    dg = jnp.sum(dy * x * r, axis=0, keepdims=True)
    return dx, dg


def _gelu_parts(x):
    x2 = x * x
    t = jnp.tanh(GELU_C * (x + GELU_A * x * x2))
    val = 0.5 * x * (1.0 + t)
    grad = 0.5 * (1.0 + t) + 0.5 * x * (1.0 - t * t) * (GELU_C * (1.0 + 3.0 * GELU_A * x2))
    return val, grad


def _gelu(x):
    return 0.5 * x * (1.0 + jnp.tanh(GELU_C * (x + GELU_A * x * x * x)))


def _sigmoid(z):
    return 1.0 / (1.0 + jnp.exp(-z))


def _cast_shards(b_idx, arrs, gm_ws, gm_bs):
    n = len(arrs)
    steps = 4

    def body(b_ref, *refs):
        ws_ref, bs_ref = refs[n:n + 2]
        outs = refs[n + 2:2 * n + 2]
        wc_ref, wct_ref, bsb_ref = refs[2 * n + 2:]
        for p in range(N_PIECE):
            outs[0][p] = refs[0][:, pl.ds(p * IN_PIECE, IN_PIECE)].astype(BF16)
        for i in range(1, n):
            outs[i][...] = refs[i][...].astype(BF16)

        @pl.when(pl.program_id(0) == 0)
        def _():
            causal = lax.broadcasted_iota(jnp.int32, (CH, CH), 0) >= lax.broadcasted_iota(jnp.int32, (CH, CH), 1)
            for h in range(HEADS):
                w = jnp.where(causal, ws_ref[h], 0.0)
                wc_ref[h] = w.astype(BF16)
                wct_ref[h] = w.T.astype(BF16)
                bsb_ref[h] = jnp.broadcast_to(bs_ref[h:h + 1, :], (CH, CH)).T

    rows = [a.shape[0] // steps for a in arrs]
    in_specs = [pl.BlockSpec((rows[i], a.shape[1]), lambda i, b: (i, 0)) for i, a in enumerate(arrs)]
    in_specs += [pl.BlockSpec((HEADS, CH, CH), lambda i, b: (0, 0, 0)), pl.BlockSpec((HEADS, CH), lambda i, b: (0, 0))]
    out_specs = [pl.BlockSpec((None, N_PIECE, rows[0], IN_PIECE), lambda i, b: (b[0], 0, i, 0))]
    out_specs += [pl.BlockSpec((None, rows[i], a.shape[1]), lambda i, b: (b[0], i, 0)) for i, a in enumerate(arrs) if i > 0]
    out_specs += [pl.BlockSpec((HEADS, CH, CH), lambda i, b: (0, 0, 0))] * 3
    out_shape = [jax.ShapeDtypeStruct((N_CHIP, N_PIECE, arrs[0].shape[0], IN_PIECE), BF16)]
    out_shape += [jax.ShapeDtypeStruct((N_CHIP,) + a.shape, BF16) for a in arrs[1:]]
    out_shape += [jax.ShapeDtypeStruct((HEADS, CH, CH), dt) for dt in (BF16, BF16, F32)]
    outs = pl.pallas_call(
        body, out_shape=out_shape,
        grid_spec=pltpu.PrefetchScalarGridSpec(num_scalar_prefetch=1, grid=(steps,), in_specs=in_specs, out_specs=out_specs),
        compiler_params=_cp(("arbitrary",)), name="cast_shards")(b_idx, *arrs, gm_ws, gm_bs)
    return outs[:n], outs[n:]


def _proj_gather(b_idx, x, g, win_own, cw8s, more, tm=1024):
    t = x.shape[0]
    ni = t // tm
    nm = len(more)
    steps = N_CHIP * N_PIECE
    near0, far0 = N_PIECE, 3 * N_PIECE

    def piece_at(step, own):
        k = step - near0
        near, far = (step >= near0) & (step < far0), step >= far0
        block = jnp.where(far, own ^ 3, jnp.where(near, own ^ jnp.where(lax.rem(k, 2) == 0, 2, 1), own))
        return block, jnp.where(far, step - far0, jnp.where(near, lax.div(k, 2), step))

    def body(*refs):
        b_ref, x_any, g_ref, win_in, cw_in = refs[:5]
        o_ref, hb_any, win_f, cw_out = refs[5 + nm:9 + nm]
        more_out = refs[9 + nm:9 + 2 * nm]
        hbuf, xbuf, wv, cw_s, cw_r, loc = refs[9 + 2 * nm:15 + 2 * nm]
        g_in = _Gather([win_f.at[:, p] for p in range(N_PIECE)], *refs[15 + 2 * nm:19 + 2 * nm])
        g_more = _Gather(more_out, *refs[19 + 2 * nm:23 + 2 * nm])
        s = pl.program_id(0)
        x, y, c, chips = _coords()
        b = 2 * x + y
        blks = [2 * chip[0] + chip[1] for chip in chips]

        def cw_cols(blk):
            return cw_out.at[:, pl.ds(blk * (D // N_CHIP), D // N_CHIP)]

        def cw_copy(k, blk):
            src = cw_in if blk is None else cw_cols(blk)
            return pltpu.make_async_remote_copy(src_ref=src, dst_ref=cw_cols(b if blk is None else blk), send_sem=cw_s.at[k],
                                                recv_sem=cw_r.at[k], device_id=(*chips[k], c), device_id_type=MESH)

        cw_local = pltpu.make_async_copy(cw_in, cw_cols(b), loc.at[1])
        hb_copy = pltpu.make_async_copy(hbuf, hb_any, loc.at[0])

        def load(step):
            slot = lax.rem(step, 2)
            block, piece = piece_at(step, b_ref[0])
            return pltpu.make_async_copy(win_f.at[block, piece], wv.at[slot], loc.at[2 + slot])

        def chunk(i):
            return pltpu.make_async_copy(x_any.at[pl.ds(i * tm, tm)], xbuf.at[i % 2], loc.at[4 + i % 2])

        local = 1

        def first():
            g_in.start()
            cw_local.start(priority=local)
            for k in range(3):
                cw_copy(k, None).start()
            load(0).start(priority=local)
            chunk(0).start(priority=local)
            for i in range(ni):
                if i + 1 < ni:
                    chunk(i + 1).start(priority=local)
                chunk(i).wait()
                h, _ = _rms(xbuf[i % 2], g_ref[...])
                hbuf[pl.ds(i * tm, tm), :] = h.astype(BF16)
            hb_copy.start(priority=local)

        events = {step: [] for step in range(steps)}
        events[0].append(first)
        for p in range(N_PIECE):
            events[2 * p + 2].append(functools.partial(g_in.hop, [p]))
            events[near0 + 2 * p - 1].append(functools.partial(g_in.near_ready, [p]))
            events[far0 + p - 2].append(functools.partial(g_in.far, [p]))
            events[far0 + p - 1].append(functools.partial(g_in.far_ready, [p]))
        events[2 * N_PIECE + 1].append(g_more.start)
        for step, todo in events.items():
            if todo:
                @pl.when(s == step)
                def _(todo=todo):
                    for do in todo:
                        do()

        @pl.when(s + 1 < steps)
        def _():
            load(s + 1).start(priority=local)

        load(s).wait()
        for i in range(ni):
            rows = pl.ds(i * tm, tm)
            o_ref[rows, :] = jnp.dot(hbuf[rows, :], wv[lax.rem(s, 2)], preferred_element_type=F32).astype(BF16)

        @pl.when(s == steps - 1)
        def _():
            g_more.hop()
            g_more.far()
            for k in range(3):
                cw_copy(k, blks[k]).wait_recv()
            for k in range(3):
                cw_copy(k, None).wait_send()
            cw_local.wait()
            hb_copy.wait()
            g_more.near_ready()
            g_more.far_ready()
            g_in.drain()
            g_more.drain()

    def out_col(s, b):
        block, piece = piece_at(s, b[0])
        return 0, block * N_PIECE + piece

    in_specs = [ANY, pl.BlockSpec((1, D), lambda s, b: (0, 0)), ANY, ANY] + [ANY] * nm
    out_specs = [pl.BlockSpec((t, IN_PIECE), out_col), ANY, ANY, ANY] + [ANY] * nm
    outs = pl.pallas_call(
        body, out_shape=[jax.ShapeDtypeStruct((t, IN_DIM), BF16), jax.ShapeDtypeStruct((t, D), BF16),
                         jax.ShapeDtypeStruct(win_own.shape, BF16), jax.ShapeDtypeStruct((8, D), F32)]
        + [jax.ShapeDtypeStruct(f.shape, f.dtype) for f in more],
        grid_spec=pltpu.PrefetchScalarGridSpec(
            num_scalar_prefetch=1, grid=(steps,), in_specs=in_specs, out_specs=out_specs,
            scratch_shapes=[pltpu.VMEM((t, D), BF16), pltpu.VMEM((2, tm, D), F32), pltpu.VMEM((2, D, IN_PIECE), BF16)]
            + [pltpu.SemaphoreType.DMA((3,))] * 2 + [pltpu.SemaphoreType.DMA((6,))]
            + _gather_sems(N_PIECE) + _gather_sems(nm)),
        input_output_aliases={3: 2, **{5 + w: 4 + w for w in range(nm)}},
        compiler_params=_cp(("arbitrary",), VMEM_LIMIT), name="proj_gather")(b_idx, x, g, win_own, cw8s, *more)
    return outs[0], outs[1], outs[2], outs[3], outs[4:]


class _Gather:
    def __init__(self, outs, ici_s, ici_r, d2d_s, d2d_r):
        x, y, c, _ = _coords()
        self.outs, self.c = outs, c
        self.sems = ici_s, ici_r, d2d_s, d2d_r
        self.b, self.bx, self.by, self.bd = 2 * x + y, 2 * (1 - x) + y, 2 * x + (1 - y), 2 * (1 - x) + (1 - y)
        self.xn, self.yn, self.sib = (1 - x, y, c), (x, 1 - y, c), (x, y, 1 - c)

    def piece(self, w, blk, hc, quarter=None):
        hr = self.outs[w].shape[1] // 2
        if quarter is None:
            return self.outs[w].at[blk, pl.ds(hc * hr, hr)]
        return self.outs[w].at[blk, pl.ds(hc * hr + quarter * (hr // 2), hr // 2)]

    def ici(self, w, k, ref, to):
        return pltpu.make_async_remote_copy(src_ref=ref, dst_ref=ref, send_sem=self.sems[0].at[w, k],
                                            recv_sem=self.sems[1].at[w, k], device_id=to, device_id_type=MESH)

    def d2d(self, w, k, ref):
        return pltpu.make_async_remote_copy(src_ref=ref, dst_ref=ref, send_sem=self.sems[2].at[w, k],
                                            recv_sem=self.sems[3].at[w, k], device_id=self.sib, device_id_type=MESH)

    def all(self):
        return range(len(self.outs))

    def start(self):
        for w in self.all():
            mine = self.piece(w, self.b, self.c)
            self.ici(w, 0, mine, self.xn).start()
            self.ici(w, 1, mine, self.yn).start()

    def hop(self, ws=None):
        c = self.c
        for w in ws or self.all():
            self.ici(w, 0, self.piece(w, self.bx, c), self.xn).wait_recv()
            self.ici(w, 1, self.piece(w, self.by, c), self.yn).wait_recv()
            self.ici(w, 2, self.piece(w, self.bx, c, 0), self.yn).start()
            self.ici(w, 3, self.piece(w, self.by, c, 1), self.xn).start()
            self.d2d(w, 0, self.piece(w, self.bx, c)).start()
            self.d2d(w, 1, self.piece(w, self.by, c)).start()

    def near_ready(self, ws=None):
        for w in ws or self.all():
            self.d2d(w, 0, self.piece(w, self.bx, 1 - self.c)).wait_recv()
            self.d2d(w, 1, self.piece(w, self.by, 1 - self.c)).wait_recv()

    def far(self, ws=None):
        c = self.c
        for w in ws or self.all():
            self.ici(w, 2, self.piece(w, self.bd, c, 0), self.yn).wait_recv()
            self.ici(w, 3, self.piece(w, self.bd, c, 1), self.xn).wait_recv()
            self.d2d(w, 2, self.piece(w, self.bd, c, 0)).start()
            self.d2d(w, 3, self.piece(w, self.bd, c, 1)).start()

    def far_ready(self, ws=None):
        for w in ws or self.all():
            self.d2d(w, 2, self.piece(w, self.bd, 1 - self.c, 0)).wait_recv()
            self.d2d(w, 3, self.piece(w, self.bd, 1 - self.c, 1)).wait_recv()

    def drain(self):
        c = self.c
        for w in self.all():
            mine = self.piece(w, self.b, c)
            self.ici(w, 0, mine, self.xn).wait_send()
            self.ici(w, 1, mine, self.yn).wait_send()
            self.ici(w, 2, self.piece(w, self.bx, c, 0), self.yn).wait_send()
            self.ici(w, 3, self.piece(w, self.by, c, 1), self.xn).wait_send()
            self.d2d(w, 0, self.piece(w, self.bx, c)).wait_send()
            self.d2d(w, 1, self.piece(w, self.by, c)).wait_send()
            self.d2d(w, 2, self.piece(w, self.bd, c, 0)).wait_send()
            self.d2d(w, 3, self.piece(w, self.bd, c, 1)).wait_send()


def _gather_sems(nw):
    return [pltpu.SemaphoreType.DMA((max(nw, 1), 4))] * 4


def _mixer_fwd(proj, cw8, lng, lnb, wc, bsb, fulls, tm=256):
    t = proj.shape[0]
    nt = t // tm
    nch = tm // CH
    nw = len(fulls)

    def body(*refs):
        p_ref, cw_ref, lng_ref, lnb_ref, wc_ref, bsb_ref = refs[:6]
        mix_ref = refs[6 + nw]
        w_outs = refs[7 + nw:7 + 2 * nw]
        prev_ref = refs[7 + 2 * nw]
        gather = _Gather(w_outs, *refs[8 + 2 * nw:])

        @pl.when(pl.program_id(0) == 0)
        def _():
            gather.start()
            prev_ref[...] = jnp.zeros_like(prev_ref)

        @pl.when(pl.program_id(0) == nt // 2)
        def _():
            gather.hop()

        @pl.when(pl.program_id(0) == nt - 1)
        def _():
            gather.far()

        rows = lax.broadcasted_iota(jnp.int32, (tm, CH), 0)
        for s in range(HEADS):
            cs = pl.ds(CH * s, CH)

            def slab(k):
                return p_ref[:, pl.ds(k * SLAB + CH * s, CH)].astype(F32)

            gb, gc, xa, za = slab(0), slab(1), slab(2), slab(3)
            cx = gc * xa
            p6 = jnp.broadcast_to(prev_ref[6:7, cs], (tm, CH))
            p7 = jnp.broadcast_to(prev_ref[7:8, cs], (tm, CH))
            c1 = jnp.where(rows == 0, p7, pltpu.roll(cx, 1, 0))
            c2 = jnp.where(rows == 0, p6, jnp.where(rows == 1, p7, pltpu.roll(cx, 2, 0)))
            prev_ref[:, cs] = cx[tm - 8:, :]
            cv = cw_ref[0:1, cs] * c2 + cw_ref[1:2, cs] * c1 + cw_ref[2:3, cs] * cx
            mix_ref[:, cs] = (gb * cv * (za * _sigmoid(za))).astype(BF16)

            u, v, zb = slab(4), slab(5), slab(6)
            ug, vg = _gelu(u), _gelu(v)
            dlt = vg - jnp.mean(vg, axis=-1, keepdims=True)
            vhat = dlt * lax.rsqrt(jnp.mean(dlt * dlt, axis=-1, keepdims=True) + EPS)
            vn = (vhat * lng_ref[:, cs] + lnb_ref[:, cs]).astype(BF16)
            gate = ug * (zb * _sigmoid(zb))
            for c in range(nch):
                rs = slice(CH * c, CH * (c + 1))
                sp = jnp.dot(wc_ref[s], vn[rs], preferred_element_type=F32) + bsb_ref[s]
                mix_ref[rs, pl.ds(SLAB + CH * s, CH)] = (gate[rs] * sp).astype(BF16)

        @pl.when(pl.program_id(0) == nt - 1)
        def _():
            gather.near_ready()
            gather.far_ready()
            gather.drain()

    sems = _gather_sems(nw)
    outs = pl.pallas_call(
        body, grid=(nt,),
        in_specs=[pl.BlockSpec((tm, IN_DIM), lambda i: (i, 0)), _full((8, D)), _full((1, D)), _full((1, D)),
                  _full((HEADS, CH, CH)), _full((HEADS, CH, CH))] + [ANY] * nw,
        out_specs=[pl.BlockSpec((tm, MIX), lambda i: (i, 0))] + [ANY] * nw,
        out_shape=[jax.ShapeDtypeStruct((t, MIX), BF16)] + [jax.ShapeDtypeStruct(f.shape, f.dtype) for f in fulls],
        input_output_aliases={6 + w: 1 + w for w in range(nw)},
        scratch_shapes=[pltpu.VMEM((8, D), F32)] + sems,
        compiler_params=_cp(("arbitrary",), VMEM_LIMIT), name="mixer_fwd")(proj, cw8, lng, lnb, wc, bsb, *fulls)
    return outs[0], outs[1:]


def _mem_fwd(mem, gm, wkv_f):
    n_mem = mem.shape[0]

    def body(mem_ref, gm_ref, w_ref, k_ref, v_ref):
        m, _ = _rms(mem_ref[...], gm_ref[...])
        mb = m.astype(BF16)
        for j in range(N_CHIP):
            dst = k_ref if j < 2 else v_ref
            dst[:, pl.ds(KV_BLK * (j % 2), KV_BLK)] = jnp.dot(mb, w_ref[j], preferred_element_type=F32).astype(BF16)

    return pl.pallas_call(
        body, out_shape=[jax.ShapeDtypeStruct((n_mem, D), BF16), jax.ShapeDtypeStruct((n_mem, D), BF16)],
        compiler_params=_cp(None, VMEM_LIMIT), name="mem_fwd")(mem, gm, wkv_f)


def _tail(x, tgt, mixin, wout, wq, wxo, k, v, g2, g3, tm=512, sub=512):
    t = x.shape[0]
    n_mem = k.shape[0]
    scale = 1.0 / math.sqrt(XD)

    def body(x_ref, tgt_ref, mix_ref, wout_ref, wq_ref, wxo_ref, k_ref, v_ref, g2_ref, g3_ref,
             loss_ref, dmix_ref, dx1b_ref, h2_ref, dq_ref, o_ref, dx2b_ref, dk_ref, dv_ref, dg2_ref, dg3_ref):
        @pl.when(pl.program_id(0) == 0)
        def _():
            loss_ref[...] = jnp.zeros_like(loss_ref)
            dk_ref[...] = jnp.zeros_like(dk_ref)
            dv_ref[...] = jnp.zeros_like(dv_ref)
            dg2_ref[...] = jnp.zeros_like(dg2_ref)
            dg3_ref[...] = jnp.zeros_like(dg3_ref)

        g2, g3 = g2_ref[...], g3_ref[...]
        for sb in range(tm // sub):
            rs = pl.ds(sub * sb, sub)
            x1 = x_ref[rs, :] + jnp.dot(mix_ref[rs, :], wout_ref[...], preferred_element_type=F32)
            h2, r2 = _rms(x1, g2)
            h2b = h2.astype(BF16)
            h2_ref[rs, :] = h2b
            q = jnp.dot(h2b, wq_ref[...], preferred_element_type=F32).astype(BF16)
            probs, outs = [], []
            for hd in range(XH):
                hs = pl.ds(XD * hd, XD)
                s = _bdot_nt(q[:, XD * hd:XD * (hd + 1)], k_ref[:, hs]) * scale
                e = jnp.exp(s - jnp.max(s, axis=-1, keepdims=True))
                p = e / jnp.sum(e, axis=-1, keepdims=True)
                probs.append(p)
                outs.append(_bdot(p, v_ref[:, hs]))
            ob = jnp.concatenate(outs, axis=-1).astype(BF16)
            o_ref[rs, :] = ob
            x2 = x1 + jnp.dot(ob, wxo_ref[...], preferred_element_type=F32)
            y, r3 = _rms(x2, g3)
            diff = y - tgt_ref[rs, :]
            row_loss = jnp.sum(diff * diff, axis=-1, keepdims=True)
            loss_ref[...] += jnp.broadcast_to(jnp.sum(row_loss, axis=0, keepdims=True) * (0.5 / D), loss_ref.shape)

            dx2, dg3 = _rms_bwd(diff * (1.0 / D), x2, r3, g3)
            dg3_ref[...] += dg3
            dx2b = dx2.astype(BF16)
            dx2b_ref[rs, :] = dx2b
            do = _bdot_nt(dx2b, wxo_ref[...])
            dqs = []
            for hd in range(XH):
                hs = pl.ds(XD * hd, XD)
                p = probs[hd]
                do_h = do[:, XD * hd:XD * (hd + 1)]
                dv_ref[:, hs] += _bdot_tn(p, do_h)
                dp = _bdot_nt(do_h, v_ref[:, hs])
                ds = p * (dp - jnp.sum(dp * p, axis=-1, keepdims=True))
                dqs.append(_bdot(ds, k_ref[:, hs]) * scale)
                dk_ref[:, hs] += _bdot_tn(ds, q[:, XD * hd:XD * (hd + 1)]) * scale
            dq = jnp.concatenate(dqs, axis=-1).astype(BF16)
            dq_ref[rs, :] = dq
            dx1n, dg2 = _rms_bwd(_bdot_nt(dq, wq_ref[...]), x1, r2, g2)
            dg2_ref[...] += dg2
            dx1b = (dx2 + dx1n).astype(BF16)
            dx1b_ref[rs, :] = dx1b
            dmix_ref[rs, :] = _bdot_nt(dx1b, wout_ref[...]).astype(BF16)

    tok = lambda w: pl.BlockSpec((tm, w), lambda i: (i, 0))
    return pl.pallas_call(
        body, grid=(t // tm,),
        in_specs=[tok(D), tok(D), tok(MIX), _full((MIX, D), 1), _full((D, D), 1), _full((D, D), 1),
                  _full((n_mem, D), 1), _full((n_mem, D), 1), _full((1, D)), _full((1, D))],
        out_specs=[_full((1, D)), tok(MIX), tok(D), tok(D), tok(D), tok(D), tok(D),
                   _full((n_mem, D)), _full((n_mem, D)), _full((1, D)), _full((1, D))],
        out_shape=[jax.ShapeDtypeStruct((1, D), F32), jax.ShapeDtypeStruct((t, MIX), BF16),
                   jax.ShapeDtypeStruct((t, D), BF16),
                   jax.ShapeDtypeStruct((t, D), BF16), jax.ShapeDtypeStruct((t, D), BF16),
                   jax.ShapeDtypeStruct((t, D), BF16), jax.ShapeDtypeStruct((t, D), BF16),
                   jax.ShapeDtypeStruct((n_mem, D), F32), jax.ShapeDtypeStruct((n_mem, D), F32),
                   jax.ShapeDtypeStruct((1, D), F32), jax.ShapeDtypeStruct((1, D), F32)],
        compiler_params=_cp(("arbitrary",), VMEM_LIMIT), name="tail")(x, tgt, mixin, wout, wq, wxo, k, v, g2, g3)


def _mem_bwd(mem, gm, dk, dv, wkv_f):
    def body(mem_ref, gm_ref, dk_ref, dv_ref, w_ref, dw_ref, dwb_ref, dgm_ref):
        mem_v = mem_ref[...]
        m, rm = _rms(mem_v, gm_ref[...])
        mb = m.astype(BF16)
        dm = jnp.zeros_like(mem_v)
        for j in range(N_CHIP):
            src = dk_ref if j < 2 else dv_ref
            dkv = src[:, pl.ds(KV_BLK * (j % 2), KV_BLK)].astype(BF16)
            dw = _bdot_tn(mb, dkv)
            dw_ref[j] = dw
            dwb_ref[j] = dw.astype(BF16)
            dm = dm + _bdot_nt(dkv, w_ref[j])
        dgm_ref[...] = jnp.sum(dm * mem_v * rm, axis=0, keepdims=True)

    return pl.pallas_call(
        body, out_shape=[jax.ShapeDtypeStruct((N_CHIP, D, KV_BLK), F32), jax.ShapeDtypeStruct((N_CHIP, D, KV_BLK), BF16),
                         jax.ShapeDtypeStruct((1, D), F32)],
        compiler_params=_cp(None, VMEM_LIMIT), name="mem_bwd")(mem, gm, dk, dv, wkv_f)


def _mixer_bwd(proj, dmix, cw8, lng, lnb, wc, wct, bsb, win_f, x, dx1, g1, rows123, row7, tm=256):
    t = proj.shape[0]
    nt = t // tm
    nch = tm // CH
    hb = 16
    pair = 2 * CH
    assert pair == IN_PIECE

    def body(p_ref, pgc_ref, pxa_ref, dm_ref, cw_ref, lng_ref, lnb_ref, wc_ref, wct_ref, bsb_ref, w_ref, x_ref,
             dx1_ref, g1_ref, r1_ref, r2_ref, r3_ref, r7_ref, dp_ref, sv_ref, dwc_ref, gx_ref,
             next_ref, dh_ref):
        i = pl.program_id(0)
        dg1_ref, dlng_ref, dlnb_ref, dbs_ref = (sv_ref.at[pl.ds(r, 1)] for r in (0, 4, 5, 6))
        dcw_ref = sv_ref.at[pl.ds(8, 8)]

        @pl.when(i == 0)
        def _():
            next_ref[...] = jnp.zeros_like(next_ref)
            sv_ref[...] = jnp.zeros_like(sv_ref)
            dwc_ref[...] = jnp.zeros_like(dwc_ref)
            for r, ref in ((1, r1_ref), (2, r2_ref), (3, r3_ref), (7, r7_ref)):
                sv_ref[r:r + 1, :] = ref[...]

        first_tile = i == nt - 1
        rows = lax.broadcasted_iota(jnp.int32, (tm, CH), 0)
        ones8 = jnp.ones((8, CH), BF16)
        for s in range(HEADS):
            cs = pl.ds(CH * s, CH)

            def slab(k):
                return p_ref[:, pl.ds(k * SLAB + CH * s, CH)].astype(F32)

            gb, gc, xa, za = slab(0), slab(1), slab(2), slab(3)
            da = dm_ref[:, cs].astype(F32)
            cx = gc * xa
            cxp = pgc_ref[:, cs].astype(F32) * pxa_ref[:, cs].astype(F32)
            cxp = jnp.where(first_tile, jnp.zeros_like(cxp), cxp)
            p6 = jnp.broadcast_to(cxp[hb - 2:hb - 1, :], (tm, CH))
            p7 = jnp.broadcast_to(cxp[hb - 1:hb, :], (tm, CH))
            c1 = jnp.where(rows == 0, p7, pltpu.roll(cx, 1, 0))
            c2 = jnp.where(rows == 0, p6, jnp.where(rows == 1, p7, pltpu.roll(cx, 2, 0)))
            w0, w1, w2 = cw_ref[0:1, cs], cw_ref[1:2, cs], cw_ref[2:3, cs]
            cv = w0 * c2 + w1 * c1 + w2 * cx
            sg = _sigmoid(za)
            sa = za * sg
            dcv = da * gb * sa
            dp_ref[:, pl.ds(0 * SLAB + CH * s, CH)] = (da * cv * sa).astype(BF16)
            dp_ref[:, pl.ds(3 * SLAB + CH * s, CH)] = (da * gb * cv * (sg * (1.0 + za * (1.0 - sg)))).astype(BF16)
            n0 = jnp.broadcast_to(next_ref[0:1, cs], (tm, CH))
            n1 = jnp.broadcast_to(next_ref[1:2, cs], (tm, CH))
            u1 = jnp.where(rows == tm - 1, n0, pltpu.roll(dcv, tm - 1, 0))
            u2 = jnp.where(rows == tm - 2, n0, jnp.where(rows == tm - 1, n1, pltpu.roll(dcv, tm - 2, 0)))
            next_ref[:, cs] = dcv[0:8, :]
            dcx = w2 * dcv + w1 * u1 + w0 * u2
            dp_ref[:, pl.ds(1 * SLAB + CH * s, CH)] = (dcx * xa).astype(BF16)
            dp_ref[:, pl.ds(2 * SLAB + CH * s, CH)] = (dcx * gc).astype(BF16)
            dcw_ref[0:1, cs] += jnp.sum(dcv * c2, axis=0, keepdims=True)
            dcw_ref[1:2, cs] += jnp.sum(dcv * c1, axis=0, keepdims=True)
            dcw_ref[2:3, cs] += jnp.sum(dcv * cx, axis=0, keepdims=True)

            u, v, zb = slab(4), slab(5), slab(6)
            db = dm_ref[:, pl.ds(SLAB + CH * s, CH)].astype(F32)
            ug, ugrad = _gelu_parts(u)
            vg, vgrad = _gelu_parts(v)
            dlt = vg - jnp.mean(vg, axis=-1, keepdims=True)
            rstd = lax.rsqrt(jnp.mean(dlt * dlt, axis=-1, keepdims=True) + EPS)
            vhat = dlt * rstd
            lg = lng_ref[:, cs]
            vn = (vhat * lg + lnb_ref[:, cs]).astype(BF16)
            sgb = _sigmoid(zb)
            szb = zb * sgb
            sps, dvns = [], []
            dbs = jnp.zeros((8, CH), F32)
            dwc = jnp.zeros((CH, CH), F32)
            for c in range(nch):
                rs = slice(CH * c, CH * (c + 1))
                sp = jnp.dot(wc_ref[s], vn[rs], preferred_element_type=F32) + bsb_ref[s]
                dsp = (db[rs] * ug[rs] * szb[rs]).astype(BF16)
                dbs = dbs + lax.dot_general(ones8, dsp, (((1,), (1,)), ((), ())), preferred_element_type=F32)
                dwc = dwc + lax.dot_general(dsp, vn[rs], (((1,), (1,)), ((), ())), preferred_element_type=F32)
                dvns.append(jnp.dot(wct_ref[s], dsp, preferred_element_type=F32))
                sps.append(sp)
            sp = jnp.concatenate(sps, axis=0)
            dvn = jnp.concatenate(dvns, axis=0)
            dbs_ref[:, cs] += dbs[0:1]
            dwc_ref[s] += dwc
            dlng_ref[:, cs] += jnp.sum(dvn * vhat, axis=0, keepdims=True)
            dlnb_ref[:, cs] += jnp.sum(dvn, axis=0, keepdims=True)
            dvhat = dvn * lg
            dvg = rstd * (dvhat - jnp.mean(dvhat, axis=-1, keepdims=True)
                          - vhat * jnp.mean(dvhat * vhat, axis=-1, keepdims=True))
            dp_ref[:, pl.ds(4 * SLAB + CH * s, CH)] = (db * sp * szb * ugrad).astype(BF16)
            dp_ref[:, pl.ds(5 * SLAB + CH * s, CH)] = (dvg * vgrad).astype(BF16)
            dp_ref[:, pl.ds(6 * SLAB + CH * s, CH)] = (db * ug * sp * (sgb * (1.0 + zb * (1.0 - sgb)))).astype(BF16)

            if s % 2 == 1:
                part = None
                for k in range(N_SLAB):
                    col = k * SLAB + pair * (s // 2)
                    blk, off = divmod(col, IN_BLK)
                    term = lax.dot_general(dp_ref[:, pl.ds(col, pair)], w_ref[blk, off // IN_PIECE],
                                           (((1,), (1,)), ((), ())), preferred_element_type=F32)
                    part = term if part is None else part + term
                if s == 1:
                    dh_ref[...] = part
                else:
                    dh_ref[...] += part

        xv = x_ref[...]
        r = lax.rsqrt(jnp.mean(xv * xv, axis=-1, keepdims=True) + EPS)
        dxn, dg = _rms_bwd(dh_ref[...], xv, r, g1_ref[...])
        gx_ref[...] = dx1_ref[...].astype(F32) + dxn
        dg1_ref[...] += dg

        @pl.when(i == nt - 1)
        def _():
            tril = lax.broadcasted_iota(jnp.int32, (CH, CH), 0) >= lax.broadcasted_iota(jnp.int32, (CH, CH), 1)
            for s in range(HEADS):
                dwc_ref[s] = jnp.where(tril, dwc_ref[s], 0.0)

    rev = lambda i: nt - 1 - i
    halo = lambda col: pl.BlockSpec((hb, SLAB), lambda i: (jnp.maximum(rev(i) * (tm // hb) - 1, 0), col))
    tok = lambda w: pl.BlockSpec((tm, w), lambda i: (rev(i), 0))
    return pl.pallas_call(
        body, grid=(nt,),
        in_specs=[tok(IN_DIM), halo(1), halo(2), tok(MIX), _full((8, D)), _full((1, D)), _full((1, D)),
                  _full((HEADS, CH, CH)), _full((HEADS, CH, CH)), _full((HEADS, CH, CH)),
                  _full((N_CHIP, N_PIECE, D, IN_PIECE), 1), tok(D), tok(D)] + [_full((1, D))] * 5,
        out_specs=[tok(IN_DIM), _full((16, D)), _full((HEADS, CH, CH)), tok(D)],
        out_shape=[jax.ShapeDtypeStruct((t, IN_DIM), BF16), jax.ShapeDtypeStruct((16, D), F32),
                   jax.ShapeDtypeStruct((HEADS, CH, CH), F32), jax.ShapeDtypeStruct((t, D), F32)],
        scratch_shapes=[pltpu.VMEM((8, D), F32), pltpu.VMEM((tm, D), F32)],
        compiler_params=_cp(("arbitrary",), VMEM_LIMIT), name="mixer_bwd")(
            proj, proj, proj, dmix, cw8, lng, lnb, wc, wct, bsb, win_f, x, dx1, g1, *rows123, row7)


def _grad_matmul(a, b, after, *, name, tk=1024):
    t, m = a.shape
    n = b.shape[1]
    nk = t // tk

    def body(a_ref, b_ref, after_ref, o_ref, ob_ref):
        kk = pl.program_id(0)
        part = lax.dot_general(a_ref[...], b_ref[...], (((0,), (0,)), ((), ())), preferred_element_type=F32)

        @pl.when(kk == 0)
        def _():
            o_ref[...] = part

        @pl.when(kk > 0)
        def _():
            o_ref[...] += part

        @pl.when(kk == nk - 1)
        def _():
            ob_ref[...] = o_ref[...].astype(BF16)

    o_spec = pl.BlockSpec((m, n), lambda k: (0, 0))
    o32, o16 = pl.pallas_call(
        body, grid=(nk,), in_specs=[pl.BlockSpec((tk, m), lambda k: (k, 0)), pl.BlockSpec((tk, n), lambda k: (k, 0)), ANY],
        out_specs=[o_spec, o_spec], out_shape=[jax.ShapeDtypeStruct((m, n), F32), jax.ShapeDtypeStruct((m, n), BF16)],
        compiler_params=_cp(("arbitrary",), VMEM_LIMIT), name=name)(a, b, after)
    return o32.reshape(N_CHIP, m // N_CHIP, n), o16.reshape(N_CHIP, m // N_CHIP, n)


def _coords():
    x, y, c = lax.axis_index("x"), lax.axis_index("y"), lax.axis_index("c")
    chips = [(1 - x, y), (x, 1 - y), (1 - x, 1 - y)]
    return x, y, c, chips


def _pair_reduce(c_idx, grads, grads_b, smalls, name):
    ng, ns = len(grads), len(smalls)
    halves = [g.shape[1] // 2 for g in grads]

    def body(c_ref, *refs):
        g_in, gb_any = refs[:ng], refs[ng:2 * ng]
        s_own, s_any = refs[2 * ng:2 * ng + ns], refs[2 * ng + ns:2 * ng + 2 * ns]
        o = refs[2 * ng + 2 * ns:4 * ng + 3 * ns]
        lands = refs[4 * ng + 3 * ns:5 * ng + 4 * ns]
        send, recv = refs[5 * ng + 4 * ns:]
        x, y, c, _ = _coords()
        j = pl.program_id(0)

        def big(i, blk):
            return pltpu.make_async_remote_copy(
                src_ref=gb_any[i].at[blk, pl.ds((1 - c) * halves[i], halves[i])], dst_ref=lands[i].at[blk],
                send_sem=send.at[i, blk], recv_sem=recv.at[i, blk], device_id=(x, y, 1 - c), device_id_type=MESH)

        def small(i):
            return pltpu.make_async_remote_copy(
                src_ref=s_any[i].at[1 - c], dst_ref=lands[ng + i],
                send_sem=send.at[ng + i, 0], recv_sem=recv.at[ng + i, 0], device_id=(x, y, 1 - c), device_id_type=MESH)

        @pl.when(j == 0)
        def _():
            for blk in range(N_CHIP):
                for i in range(ng):
                    big(i, blk).start()
            for i in range(ns):
                small(i).start()

        for i in range(ng):
            big(i, j).wait_recv()
            tot = g_in[i][...] + lands[i][j].astype(F32)
            o[i][...] = tot
            o[ng + i][...] = tot.astype(BF16)

        @pl.when(j == N_CHIP - 1)
        def _():
            for i in range(ns):
                small(i).wait_recv()
                o[2 * ng + i][...] = s_own[i][...] + lands[ng + i][...]
                small(i).wait_send()
            for blk in range(N_CHIP):
                for i in range(ng):
                    big(i, blk).wait_send()

    in_specs = [pl.BlockSpec((None, None, halves[i], g.shape[2]), lambda b, c: (b, c[0], 0, 0)) for i, g in enumerate(grads)]
    in_specs += [ANY] * ng
    in_specs += [pl.BlockSpec((None, s.shape[0] // 2, s.shape[1]), lambda b, c: (c[0], 0, 0)) for s in smalls]
    in_specs += [ANY] * ns
    blk = [pl.BlockSpec((None, halves[i], g.shape[2]), lambda b, c: (b, 0, 0)) for i, g in enumerate(grads)]
    out_specs = blk + blk + [pl.BlockSpec((s.shape[0] // 2, s.shape[1]), lambda b, c: (0, 0)) for s in smalls]
    out_shape = [jax.ShapeDtypeStruct((N_CHIP, halves[i], g.shape[2]), F32) for i, g in enumerate(grads)]
    out_shape += [jax.ShapeDtypeStruct((N_CHIP, halves[i], g.shape[2]), BF16) for i, g in enumerate(grads)]
    out_shape += [jax.ShapeDtypeStruct((s.shape[0] // 2, s.shape[1]), F32) for s in smalls]
    scratch = [pltpu.VMEM((N_CHIP, halves[i], g.shape[2]), BF16) for i, g in enumerate(grads)]
    scratch += [pltpu.VMEM((s.shape[0] // 2, s.shape[1]), F32) for s in smalls]
    scratch += [pltpu.SemaphoreType.DMA((ng + ns, N_CHIP)), pltpu.SemaphoreType.DMA((ng + ns, N_CHIP))]
    grads4 = [g.reshape(N_CHIP, 2, halves[i], g.shape[2]) for i, g in enumerate(grads)]
    smalls3 = [s.reshape(2, s.shape[0] // 2, s.shape[1]) for s in smalls]
    return pl.pallas_call(
        body, out_shape=out_shape,
        grid_spec=pltpu.PrefetchScalarGridSpec(num_scalar_prefetch=1, grid=(N_CHIP,), in_specs=in_specs,
                                               out_specs=out_specs, scratch_shapes=scratch),
        compiler_params=_cp(("arbitrary",), VMEM_LIMIT), name=name)(c_idx, *grads4, *grads_b, *smalls3, *smalls3)


def _grad_matmul_pair(c_idx, a, b, smalls, small_dtypes, after, *, name, tk=2048):
    t, m = a.shape
    bn = b.shape[1] // N_CHIP
    nk = t // tk
    hr = m // 2
    ns = len(smalls)

    def body(c_ref, a_ref, b_ref, *refs):
        s_own, s_any = refs[:ns], refs[ns:2 * ns]
        o32, o16 = refs[2 * ns + 1], refs[2 * ns + 2]
        o_small = refs[2 * ns + 3:3 * ns + 3]
        acc, tb, land, st16 = refs[3 * ns + 3:3 * ns + 7]
        s_land, s_stage = refs[3 * ns + 7:4 * ns + 7], refs[4 * ns + 7:5 * ns + 7]
        send, recv, loc = refs[5 * ns + 7:]
        x, y, c, _ = _coords()
        sibling = dict(device_id=(x, y, 1 - c), device_id_type=MESH)
        j, kk = pl.program_id(0), pl.program_id(1)
        mine = pl.ds(pl.multiple_of(c * hr, hr), hr)
        theirs = pl.ds(pl.multiple_of((1 - c) * hr, hr), hr)

        def to_sibling(blk):
            return pltpu.make_async_remote_copy(src_ref=tb, dst_ref=land.at[blk], send_sem=send.at[blk],
                                                recv_sem=recv.at[blk], **sibling)

        def small(i):
            return pltpu.make_async_remote_copy(src_ref=s_any[i].at[1 - c], dst_ref=s_land[i], send_sem=send.at[N_CHIP + i],
                                                recv_sem=recv.at[N_CHIP + i], **sibling)

        def written(blk):
            return (pltpu.make_async_copy(acc.at[blk % 2, mine], o32.at[blk], loc.at[0]),
                    pltpu.make_async_copy(st16, o16.at[blk], loc.at[1]))

        def finish(blk):
            to_sibling(blk).wait_recv()

            @pl.when(blk > 0)
            def _():
                for cp in written(blk - 1):
                    cp.wait()

            tot = acc[blk % 2, mine, :] + land[blk].astype(F32)
            acc[blk % 2, mine, :] = tot
            st16[...] = tot.astype(BF16)
            for cp in written(blk):
                cp.start()

        def small_out(i):
            return pltpu.make_async_copy(s_stage[i], o_small[i], loc.at[2 + i])

        @pl.when((j == 0) & (kk == 0))
        def _():
            for i in range(ns):
                small(i).start()

        @pl.when((j == 1) & (kk == 0))
        def _():
            for i in range(ns):
                small(i).wait_recv()
                s_stage[i][...] = (s_own[i][...] + s_land[i][...]).astype(small_dtypes[i])
                small_out(i).start()

        @pl.when((j > 0) & (kk == 0))
        def _():
            finish(j - 1)

        part = lax.dot_general(a_ref[...], b_ref[...], (((0,), (0,)), ((), ())), preferred_element_type=F32)
        slot = lax.rem(j, 2)

        @pl.when(kk == 0)
        def _():
            acc[slot] = part

        @pl.when(kk > 0)
        def _():
            acc[slot] += part

        @pl.when(kk == nk - 1)
        def _():
            @pl.when(j > 0)
            def _():
                to_sibling(j - 1).wait_send()

            tb[...] = acc[slot, theirs, :].astype(BF16)
            to_sibling(j).start()

        @pl.when((j == N_CHIP - 1) & (kk == nk - 1))
        def _():
            finish(j)
            for i in range(ns):
                small_out(i).wait()
                small(i).wait_send()
            for cp in written(j):
                cp.wait()
            to_sibling(j).wait_send()

    halves = [(s.shape[0] // 2, s.shape[1]) for s in smalls]
    in_specs = [pl.BlockSpec((tk, m), lambda j, k, c: (k, 0)), pl.BlockSpec((tk, bn), lambda j, k, c: (k, j))]
    in_specs += [pl.BlockSpec((None,) + h, lambda j, k, c: (c[0], 0, 0)) for h in halves] + [ANY] * ns + [ANY]
    out_shape = [jax.ShapeDtypeStruct((N_CHIP, hr, bn), F32), jax.ShapeDtypeStruct((N_CHIP, hr, bn), BF16)]
    out_shape += [pltpu.HBM(h, dt) for h, dt in zip(halves, small_dtypes)]
    scratch = [pltpu.VMEM((2, m, bn), F32), pltpu.VMEM((hr, bn), BF16),
               pltpu.VMEM((N_CHIP, hr, bn), BF16), pltpu.VMEM((hr, bn), BF16)]
    scratch += [pltpu.VMEM(h, F32) for h in halves] + [pltpu.VMEM(h, dt) for h, dt in zip(halves, small_dtypes)]
    scratch += [pltpu.SemaphoreType.DMA((N_CHIP + ns,)), pltpu.SemaphoreType.DMA((N_CHIP + ns,)),
                pltpu.SemaphoreType.DMA((2 + ns,))]
    smalls3 = [s.reshape((2,) + h) for s, h in zip(smalls, halves)]
    outs = pl.pallas_call(
        body, out_shape=out_shape,
        grid_spec=pltpu.PrefetchScalarGridSpec(num_scalar_prefetch=1, grid=(N_CHIP, nk), in_specs=in_specs,
                                               out_specs=[ANY, ANY] + [_HBM] * ns, scratch_shapes=scratch),
        compiler_params=_cp(("arbitrary", "arbitrary"), VMEM_LIMIT), name=name)(c_idx, a, b, *smalls3, *smalls3, after)
    return outs[0], outs[1], list(outs[2:])


_HBM = pl.BlockSpec(memory_space=pltpu.HBM)
_SEM = pl.BlockSpec(memory_space=pltpu.SEMAPHORE)


def _split_copies(ins, lands, ng, send, recv, arriving):
    x, y, c, chips = _coords()
    b = 2 * x + y
    copies = []
    for i in range(len(ins)):
        for k in range(3):
            blk = 2 * chips[k][0] + chips[k][1]
            src, dst, got = (ins[i].at[blk], lands[i].at[k], lands[i].at[k]) if i < ng else (ins[i], lands[i].at[b], lands[i].at[blk])
            sems = dict(send_sem=send.at[3 * i + k], recv_sem=recv.at[3 * i + k], device_id=(*chips[k], c), device_id_type=MESH)
            if arriving:
                copies.append(pltpu.make_async_remote_copy(src_ref=got, dst_ref=got, **sems))
            else:
                copies.append(pltpu.make_async_remote_copy(src_ref=src, dst_ref=dst, **sems))
    return copies


def _exchange_begin(sums_b, smalls, name):
    ng, n = len(sums_b), len(sums_b) + len(smalls)
    srcs = list(sums_b) + list(smalls)
    lands = [lax.empty((3,) + g.shape[1:], g.dtype) for g in sums_b] + [lax.empty((N_CHIP,) + s.shape, s.dtype) for s in smalls]

    def body(*refs):
        ins, land_refs = refs[:n], refs[n:2 * n]
        send, recv = refs[2 * n], refs[2 * n + 1]
        token = refs[4 * n + 2]
        for cp in _split_copies(ins, land_refs, ng, send, recv, False):
            cp.start()
        token[...] = jnp.zeros_like(token)

    hbm = lambda a: pltpu.HBM(a.shape, a.dtype)
    outs = pl.pallas_call(
        body, name=name,
        out_shape=(pltpu.SemaphoreType.DMA((3 * n,)), pltpu.SemaphoreType.DMA((3 * n,)), *[hbm(a) for a in srcs + lands],
                   jax.ShapeDtypeStruct((8, 128), F32)),
        in_specs=[_HBM] * (2 * n), out_specs=(_SEM, _SEM, *[_HBM] * (2 * n), pl.BlockSpec(memory_space=pltpu.VMEM)),
        input_output_aliases={i: 2 + i for i in range(2 * n)},
        compiler_params=pltpu.CompilerParams(has_side_effects=pltpu.SideEffectType.DATAFLOW_SIDE_EFFECTING),
    )(*[pltpu.with_memory_space_constraint(a, pltpu.HBM) for a in srcs + lands])
    return outs[0], outs[1], list(outs[2:2 + n]), list(outs[2 + n:2 + 2 * n]), outs[2 + 2 * n]


def _exchange_end(send, recv, srcs, lands, ng, which, after, name):
    n = len(srcs)
    after = list(after)

    def body(*refs):
        ins, land_refs = refs[:n], refs[n:2 * n]
        send_ref, recv_ref = refs[2 * n], refs[2 * n + 1]
        outgoing = _split_copies(ins, land_refs, ng, send_ref, recv_ref, False)
        arriving = _split_copies(ins, land_refs, ng, send_ref, recv_ref, True)
        for i in which:
            for cp in outgoing[3 * i:3 * i + 3]:
                cp.wait_send()
        for i in which:
            for cp in arriving[3 * i:3 * i + 3]:
                cp.wait_recv()

    hbm = lambda a: pltpu.HBM(a.shape, a.dtype)
    outs = pl.pallas_call(
        body, name=name, out_shape=tuple(hbm(a) for a in list(srcs) + list(lands)),
        in_specs=[_HBM] * (2 * n) + [_SEM, _SEM] + [ANY] * len(after), out_specs=tuple([_HBM] * (2 * n)),
        input_output_aliases={i: i for i in range(2 * n)},
        compiler_params=pltpu.CompilerParams(has_side_effects=pltpu.SideEffectType.DATAFLOW_SIDE_EFFECTING),
    )(*srcs, *lands, send, recv, *after)
    return list(outs[:n]), list(outs[n:])


def _chip_reduce(bc_idx, sums, recvd, smalls_slots, smalls_own, name, steps=4):
    ng, ns = len(sums), len(smalls_slots)
    n = ng + ns
    assert steps >= 2
    halves = [g.shape[1] for g in sums] + [s.shape[1] for s in smalls_slots]
    rows = [g.shape[1] // steps for g in sums]

    def body(bc_ref, *refs):
        own, rx = refs[:ng], refs[ng:2 * ng]
        sl = refs[2 * ng:2 * ng + ns]
        sl_own = refs[2 * ng + ns:2 * ng + 2 * ns]
        o = refs[2 * ng + 2 * ns:2 * ng + 2 * ns + n]
        tiles = refs[2 * ng + 2 * ns + n:2 * ng + 2 * ns + 2 * n]
        keep, send, recv = refs[2 * ng + 2 * ns + 2 * n:]
        x, y, c, _ = _coords()
        sibling = dict(device_id=(x, y, 1 - c), device_id_type=MESH)
        r = pl.program_id(0)

        def writes(i, step, slot):
            dst = o[i].at[pl.ds(c * halves[i] + step * rows[i], rows[i])]
            return (pltpu.make_async_copy(tiles[i].at[slot], dst, keep.at[i, slot]),
                    pltpu.make_async_remote_copy(src_ref=tiles[i].at[slot], dst_ref=dst, send_sem=send.at[i, slot],
                                                 recv_sem=recv.at[i, step], **sibling))

        def small_writes(i):
            dst = o[i].at[pl.ds(c * halves[i], halves[i])]
            return (pltpu.make_async_copy(tiles[i], dst, keep.at[i, 0]),
                    pltpu.make_async_remote_copy(src_ref=tiles[i], dst_ref=dst, send_sem=send.at[i, 0],
                                                 recv_sem=recv.at[i, 0], **sibling))

        def arriving(i, step, nrows):
            dst = o[i].at[pl.ds((1 - c) * halves[i] + step * nrows, nrows)]
            return pltpu.make_async_remote_copy(src_ref=dst, dst_ref=dst, send_sem=send.at[i, 0], recv_sem=recv.at[i, step],
                                                **sibling)

        def finish(step, slot):
            for i in range(ng):
                local, remote = writes(i, step, slot)
                local.wait()
                remote.wait_send()

        @pl.when(r >= 2)
        def _():
            finish(r - 2, r % 2)

        for i in range(ng):
            tot = own[i][...]
            for j in range(3):
                tot = tot + rx[i][j].astype(F32)
            tiles[i][r % 2] = tot
            for cp in writes(i, r, r % 2):
                cp.start()

        @pl.when(r == 0)
        def _():
            for i in range(ns):
                term = [jnp.where(bc_ref[0] == kk, sl_own[i][...], sl[i][kk]).astype(F32) for kk in range(N_CHIP)]
                tiles[ng + i][...] = ((term[0] + term[1]) + term[2]) + term[3]
                for cp in small_writes(ng + i):
                    cp.start()

        @pl.when(r == steps - 1)
        def _():
            finish(steps - 2, (steps - 2) % 2)
            finish(steps - 1, (steps - 1) % 2)
            for i in range(ns):
                local, remote = small_writes(ng + i)
                local.wait()
                remote.wait_send()
                arriving(ng + i, 0, halves[ng + i]).wait_recv()
            for i in range(ng):
                for step in range(steps):
                    arriving(i, step, rows[i]).wait_recv()

    in_specs = [pl.BlockSpec((None, rows[i], g.shape[2]), lambda r, bc: (bc[0], r, 0)) for i, g in enumerate(sums)]
    in_specs += [pl.BlockSpec((3, rows[i], g.shape[2]), lambda r, bc: (0, r, 0)) for i, g in enumerate(sums)]
    in_specs += [pl.BlockSpec(s.shape, lambda r, bc: (0, 0, 0)) for s in smalls_slots]
    in_specs += [pl.BlockSpec(s.shape[1:], lambda r, bc: (0, 0)) for s in smalls_slots]
    out_shape = [jax.ShapeDtypeStruct((2 * g.shape[1], g.shape[2]), F32) for g in sums]
    out_shape += [jax.ShapeDtypeStruct((2 * s.shape[1], s.shape[2]), F32) for s in smalls_slots]
    scratch = [pltpu.VMEM((2, rows[i], g.shape[2]), F32) for i, g in enumerate(sums)]
    scratch += [pltpu.VMEM(s.shape[1:], F32) for s in smalls_slots]
    scratch += [pltpu.SemaphoreType.DMA((n, 2)), pltpu.SemaphoreType.DMA((n, 2)), pltpu.SemaphoreType.DMA((n, steps))]
    return list(pl.pallas_call(
        body, out_shape=out_shape,
        grid_spec=pltpu.PrefetchScalarGridSpec(num_scalar_prefetch=1, grid=(steps,), in_specs=in_specs,
                                               out_specs=[ANY] * n, scratch_shapes=scratch),
        compiler_params=_cp(("arbitrary",), VMEM_LIMIT), name=name)(bc_idx, *sums, *recvd, *smalls_slots, *smalls_own))


def _adamw_math(w, g, m, v):
    m2 = ADAM_B1 * m + (1.0 - ADAM_B1) * g
    v2 = ADAM_B2 * v + (1.0 - ADAM_B2) * (g * g)
    m_hat = m2 / (1.0 - ADAM_B1 ** ADAM_STEP)
    v_hat = v2 / (1.0 - ADAM_B2 ** ADAM_STEP)
    delta = -ADAM_LR * (m_hat / (jnp.sqrt(v_hat) + ADAM_EPS) + ADAM_WD * w)
    return delta, m2, v2


def _adamw_big(ws, gs, ms, vs, name, steps=8):
    n = len(ws)

    def body(*refs):
        for i in range(n):
            w_ref, g_ref, m_ref, v_ref = (refs[k * n + i] for k in range(4))
            d_ref, m2_ref, v2_ref, g2_ref = (refs[(4 + k) * n + i] for k in range(4))
            gv = g_ref[...]
            d_ref[...], m2_ref[...], v2_ref[...] = _adamw_math(w_ref[...], gv, m_ref[...], v_ref[...])
            g2_ref[...] = gv

    specs = [pl.BlockSpec((w.shape[0] // steps, w.shape[1]), lambda i: (i, 0)) for w in ws]
    shapes = [jax.ShapeDtypeStruct(w.shape, F32) for w in ws]
    outs = pl.pallas_call(
        body, grid=(steps,), in_specs=specs * 4, out_specs=specs * 4, out_shape=shapes * 4,
        compiler_params=_cp(("parallel",), VMEM_LIMIT), name=name)(*ws, *gs, *ms, *vs)
    return [tuple(outs[k * n + i] for k in range(4)) for i in range(n)]


def _adamw_small(b_idx, sv, sw, vecs, conv, ws):
    nv = len(vecs)
    cols = conv[0].shape[1]

    def body(b_ref, sv_ref, sw_ref, *refs):
        ins, outs = refs[:3 * nv + 6], refs[3 * nv + 6:]
        for i in range(nv):
            g = sv_ref[i:i + 1, :]
            w_ref, m_ref, v_ref = ins[3 * i:3 * i + 3]
            d_ref, m2_ref, v2_ref, g_ref = outs[4 * i:4 * i + 4]
            d_ref[...], m2_ref[...], v2_ref[...] = _adamw_math(w_ref[...], g, m_ref[...], v_ref[...])
            g_ref[...] = g
        g = sv_ref[8:8 + conv[0].shape[0], pl.ds(pl.multiple_of(b_ref[0] * cols, cols), cols)]
        w_ref, m_ref, v_ref = ins[3 * nv:3 * nv + 3]
        d_ref, m2_ref, v2_ref, g_ref = outs[4 * nv:4 * nv + 4]
        d_ref[...], m2_ref[...], v2_ref[...] = _adamw_math(w_ref[...], g, m_ref[...], v_ref[...])
        g_ref[...] = g
        w_ref, m_ref, v_ref = ins[3 * nv + 3:]
        d_ref, m2_ref, v2_ref, g_ref, one_ref = outs[4 * nv + 4:]
        g = sw_ref[...]
        d_ref[...], m2_ref[...], v2_ref[...] = _adamw_math(w_ref[...], g, m_ref[...], v_ref[...])
        g_ref[...] = g
        one_ref[...] = sv_ref[nv:nv + 1, 0:1]

    flat = [a for grp in vecs for a in grp] + list(conv) + list(ws)
    out_shape = [jax.ShapeDtypeStruct(grp[0].shape, F32) for grp in list(vecs) + [conv, ws] for _ in range(4)]
    out_shape += [jax.ShapeDtypeStruct((1, 1), F32)]
    vmem = pl.BlockSpec(memory_space=pltpu.VMEM)
    outs = pl.pallas_call(
        body, out_shape=out_shape, in_specs=[pl.BlockSpec(memory_space=pltpu.SMEM)] + [vmem] * (2 + len(flat)),
        out_specs=[vmem] * len(out_shape), name="adamw_small")(b_idx, sv, sw, *flat)
    return [tuple(outs[4 * i:4 * i + 4]) for i in range(nv + 2)], outs[4 * nv + 8]


def kernel(x, mem, norm_mix_g, w_in, conv_w, gm_ln_g, gm_ln_b, gm_ws, gm_bs, w_out, norm_x_g, norm_mem_g, w_q, w_kv, w_xo, norm_final_g, loss_target, m_norm_mix_g, m_w_in, m_conv_w, m_gm_ln_g, m_gm_ln_b, m_gm_ws, m_gm_bs, m_w_out, m_norm_x_g, m_norm_mem_g, m_w_q, m_w_kv, m_w_xo, m_norm_final_g, v_norm_mix_g, v_w_in, v_conv_w, v_gm_ln_g, v_gm_ln_b, v_gm_ws, v_gm_bs, v_w_out, v_norm_x_g, v_norm_mem_g, v_w_q, v_w_kv, v_w_xo, v_norm_final_g):
    t = x.shape[1]
    xi = lax.axis_index("x")
    yi = lax.axis_index("y")
    ci = lax.axis_index("c")
    b_idx = jnp.reshape(2 * xi + yi, (1,)).astype(jnp.int32)
    c_idx = jnp.reshape(ci, (1,)).astype(jnp.int32)

    x2d, mem2d, tgt = x[0], mem[0], loss_target[0]
    big = [w_in[0], w_out[0], w_q[0], w_kv[0], w_xo[0]]
    big_m = [m_w_in[0], m_w_out[0], m_w_q[0], m_w_kv[0], m_w_xo[0]]
    big_v = [v_w_in[0], v_w_out[0], v_w_q[0], v_w_kv[0], v_w_xo[0]]
    g3 = norm_final_g.reshape(1, D)

    def pad8(a):
        return jnp.pad(a, ((0, 8 - a.shape[0]), (0, 0)))

    own_blocks, (wc, wct, bsb) = _cast_shards(b_idx, big, gm_ws[0], gm_bs[0])

    proj, hb, win_f, cw8, (wq_f,) = _proj_gather(
        b_idx, x2d, norm_mix_g, own_blocks[0], pad8(conv_w[0]), [own_blocks[2]])
    mixin, (wout_f, wkv_f, wxo_f) = _mixer_fwd(
        proj, cw8, gm_ln_g, gm_ln_b, wc, bsb, [own_blocks[1], own_blocks[3], own_blocks[4]])
    wout2, wq2, wxo2 = wout_f.reshape(MIX, D), wq_f.reshape(D, D), wxo_f.reshape(D, D)
    k, v = _mem_fwd(mem2d, norm_mem_g, wkv_f)

    (loss_row, dmix, dx1b, h2b, dq, ob, dx2b, dk, dv, dg2, dg3) = _tail(
        x2d, tgt, mixin, wout2, wq2, wxo2, k, v, norm_x_g, g3)
    dwkv, dwkv_b, dgm = _mem_bwd(mem2d, norm_mem_g, dk, dv, wkv_f)
    dproj, sv, dwc, grad_x = _mixer_bwd(
        proj, dmix, cw8, gm_ln_g, gm_ln_b, wc, wct, bsb, win_f, x2d, dx1b, norm_mix_g, [dg2, dgm, dg3], loss_row)

    bc_idx = jnp.concatenate([b_idx, c_idx])
    sw = dwc.reshape(HEADS * CH, CH)
    dwin_sum, dwin_sum_b, psmall = _grad_matmul_pair(c_idx, hb, dproj, [sv, sw], [F32, BF16], dgm, name="grad_w_in")
    sums_b = [dwin_sum]
    send_b, recv_b, src_b, land_b, token_b = _exchange_begin([dwin_sum_b], psmall, "exchange_b_begin")

    dwxo, dwxo_b = _grad_matmul(ob, dx2b, token_b, name="grad_w_xo", tk=2048)
    dwq, dwq_b = _grad_matmul(h2b, dq, token_b, name="grad_w_q", tk=2048)
    dwout, dwout_b = _grad_matmul(mixin, dx1b, token_b, name="grad_w_out")
    ps_a = _pair_reduce(c_idx, [dwout, dwkv, dwq, dwxo], [dwout_b, dwkv_b, dwq_b, dwxo_b], [], "pair_reduce_a")
    sums_a, sums_a_b = list(ps_a[:4]), list(ps_a[4:8])
    send_a, recv_a, src_a, land_a, token_a = _exchange_begin(sums_a_b, [], "exchange_a_begin")

    src_b, rx2b = _exchange_end(send_b, recv_b, src_b, land_b, 1, [0, 1, 2], [token_a], "exchange_b_end")
    gwin, svf, swf = _chip_reduce(bc_idx, sums_b, rx2b[:1], rx2b[1:], src_b[1:], "chip_reduce_b")
    out_b = _adamw_big(big[:1], [gwin], big_m[:1], big_v[:1], "adamw_w_in")[0]

    row = lambda a: a.reshape(1, D)
    mat = lambda a: a.reshape(HEADS * CH, CH)
    small, loss = _adamw_small(
        b_idx, svf, swf,
        [(row(norm_mix_g), row(m_norm_mix_g), row(v_norm_mix_g)), (row(norm_x_g), row(m_norm_x_g), row(v_norm_x_g)),
         (row(norm_mem_g), row(m_norm_mem_g), row(v_norm_mem_g)), (row(norm_final_g), row(m_norm_final_g), row(v_norm_final_g)),
         (row(gm_ln_g), row(m_gm_ln_g), row(v_gm_ln_g)), (row(gm_ln_b), row(m_gm_ln_b), row(v_gm_ln_b)),
         (row(gm_bs), row(m_gm_bs), row(v_gm_bs))],
        (conv_w[0], m_conv_w[0], v_conv_w[0]), (mat(gm_ws), mat(m_gm_ws), mat(v_gm_ws)))

    def finish_a(part, src, land, after, tag):
        src, land = _exchange_end(send_a, recv_a, src, land, 4, part, after, "exchange_a%s_end" % tag)
        grads = _chip_reduce(bc_idx, [sums_a[i] for i in part], [land[i] for i in part], [], [], "chip_reduce_a" + tag,
                             steps=2)
        ids = [(1, 3, 2, 4)[i] for i in part]
        outs = _adamw_big([big[i] for i in ids], grads, [big_m[i] for i in ids], [big_v[i] for i in ids], "adamw_a" + tag,
                          steps=4)
        return src, land, outs

    src_a, land_a, (out_wout, out_wkv) = finish_a([0, 1], src_a, land_a, [out_b[0], small[0][0]], "1")
    _, _, (out_wq, out_wxo) = finish_a([2, 3], src_a, land_a, [out_wout[0]], "2")

    def unpack(k):
        vec = lambda i: small[i][k]
        return [vec(0), out_b[k][None], small[7][k][None], vec(4), vec(5), small[8][k].reshape(1, HEADS, CH, CH),
                vec(6).reshape(1, HEADS, CH), out_wout[k][None], vec(1), vec(2), out_wq[k][None], out_wkv[k][None],
                out_wxo[k][None], vec(3).reshape(D)]

    return (loss.reshape(()), grad_x[None], *unpack(3), *unpack(0), *unpack(1), *unpack(2))
```

```python
import functools
import math

import jax
import jax.numpy as jnp
from jax import lax
from jax.experimental import pallas as pl
from jax.experimental.pallas import tpu as pltpu

F32 = jnp.float32
BF16 = jnp.bfloat16
MESH = pl.DeviceIdType.MESH

D = 1024
SLAB = 1024
N_SLAB = 7
IN_DIM = N_SLAB * SLAB
MIX = 2 * SLAB
HEADS = 8
CH = 128
XH = 4
XD = D // XH
EPS = 1e-6
GELU_C = math.sqrt(2.0 / math.pi)
GELU_A = 0.044715
N_CHIP = 4
IN_BLK = IN_DIM // N_CHIP
IN_PIECE = 256
N_PIECE = IN_BLK // IN_PIECE
KV_BLK = 2 * D // N_CHIP

ADAM_LR, ADAM_B1, ADAM_B2, ADAM_EPS, ADAM_WD, ADAM_STEP = 0.001, 0.9, 0.999, 1e-08, 0.01, 10

VMEM_LIMIT = 60 * 1024 * 1024


def _cp(sem=None, vmem=None):
    return pltpu.CompilerParams(dimension_semantics=sem, vmem_limit_bytes=vmem)


def _full(shape, buffers=None):
    n = len(shape)
    if buffers is None:
        return pl.BlockSpec(shape, lambda *_: (0,) * n)
    return pl.BlockSpec(shape, lambda *_: (0,) * n, pipeline_mode=pl.Buffered(buffers))


ANY = pl.BlockSpec(memory_space=pl.ANY)


def _bdot(a, b):
    return jnp.dot(a.astype(BF16), b.astype(BF16), preferred_element_type=F32)


def _bdot_nt(a, b):
    return lax.dot_general(a.astype(BF16), b.astype(BF16), (((1,), (1,)), ((), ())), preferred_element_type=F32)


def _bdot_tn(a, b):
    return lax.dot_general(a.astype(BF16), b.astype(BF16), (((0,), (0,)), ((), ())), preferred_element_type=F32)


def _rms(x, g):
    r = lax.rsqrt(jnp.mean(x * x, axis=-1, keepdims=True) + EPS)
    return x * r * g, r


def _rms_bwd(dy, x, r, g):
    gdy = dy * g
    dx = r * gdy - x * (r * r * r) * jnp.mean(x * gdy, axis=-1, keepdims=True)
    dg = jnp.sum(dy * x * r, axis=0, keepdims=True)
    return dx, dg


def _gelu_parts(x):
    x2 = x * x
    t = jnp.tanh(GELU_C * (x + GELU_A * x * x2))
    val = 0.5 * x * (1.0 + t)
    grad = 0.5 * (1.0 + t) + 0.5 * x * (1.0 - t * t) * (GELU_C * (1.0 + 3.0 * GELU_A * x2))
    return val, grad


def _gelu(x):
    return 0.5 * x * (1.0 + jnp.tanh(GELU_C * (x + GELU_A * x * x * x)))


def _sigmoid(z):
    return 1.0 / (1.0 + jnp.exp(-z))


def _cast_shards(b_idx, arrs, gm_ws, gm_bs):
    n = len(arrs)
    steps = 4

    def body(b_ref, *refs):
        ws_ref, bs_ref = refs[n:n + 2]
        outs = refs[n + 2:2 * n + 2]
        wc_ref, wct_ref, bsb_ref = refs[2 * n + 2:]
        for p in range(N_PIECE):
            outs[0][p] = refs[0][:, pl.ds(p * IN_PIECE, IN_PIECE)].astype(BF16)
        for i in range(1, n):
            outs[i][...] = refs[i][...].astype(BF16)

        @pl.when(pl.program_id(0) == 0)
        def _():
            causal = lax.broadcasted_iota(jnp.int32, (CH, CH), 0) >= lax.broadcasted_iota(jnp.int32, (CH, CH), 1)
            for h in range(HEADS):
                w = jnp.where(causal, ws_ref[h], 0.0)
                wc_ref[h] = w.astype(BF16)
                wct_ref[h] = w.T.astype(BF16)
                bsb_ref[h] = jnp.broadcast_to(bs_ref[h:h + 1, :], (CH, CH)).T

    rows = [a.shape[0] // steps for a in arrs]
    in_specs = [pl.BlockSpec((rows[i], a.shape[1]), lambda i, b: (i, 0)) for i, a in enumerate(arrs)]
    in_specs += [pl.BlockSpec((HEADS, CH, CH), lambda i, b: (0, 0, 0)), pl.BlockSpec((HEADS, CH), lambda i, b: (0, 0))]
    out_specs = [pl.BlockSpec((None, N_PIECE, rows[0], IN_PIECE), lambda i, b: (b[0], 0, i, 0))]
    out_specs += [pl.BlockSpec((None, rows[i], a.shape[1]), lambda i, b: (b[0], i, 0)) for i, a in enumerate(arrs) if i > 0]
    out_specs += [pl.BlockSpec((HEADS, CH, CH), lambda i, b: (0, 0, 0))] * 3
    out_shape = [jax.ShapeDtypeStruct((N_CHIP, N_PIECE, arrs[0].shape[0], IN_PIECE), BF16)]
    out_shape += [jax.ShapeDtypeStruct((N_CHIP,) + a.shape, BF16) for a in arrs[1:]]
    out_shape += [jax.ShapeDtypeStruct((HEADS, CH, CH), dt) for dt in (BF16, BF16, F32)]
    outs = pl.pallas_call(
        body, out_shape=out_shape,
        grid_spec=pltpu.PrefetchScalarGridSpec(num_scalar_prefetch=1, grid=(steps,), in_specs=in_specs, out_specs=out_specs),
        compiler_params=_cp(("arbitrary",)), name="cast_shards")(b_idx, *arrs, gm_ws, gm_bs)
    return outs[:n], outs[n:]


def _proj_gather(b_idx, x, g, win_own, cw8s, more, tm=1024):
    t = x.shape[0]
    ni = t // tm
    nm = len(more)
    steps = N_CHIP * N_PIECE
    near0, far0 = N_PIECE, 3 * N_PIECE

    def piece_at(step, own):
        k = step - near0
        near, far = (step >= near0) & (step < far0), step >= far0
        block = jnp.where(far, own ^ 3, jnp.where(near, own ^ jnp.where(lax.rem(k, 2) == 0, 2, 1), own))
        return block, jnp.where(far, step - far0, jnp.where(near, lax.div(k, 2), step))

    def body(*refs):
        b_ref, x_any, g_ref, win_in, cw_in = refs[:5]
        o_ref, hb_any, win_f, cw_out = refs[5 + nm:9 + nm]
        more_out = refs[9 + nm:9 + 2 * nm]
        hbuf, xbuf, wv, cw_s, cw_r, loc = refs[9 + 2 * nm:15 + 2 * nm]
        g_in = _Gather([win_f.at[:, p] for p in range(N_PIECE)], *refs[15 + 2 * nm:19 + 2 * nm])
        g_more = _Gather(more_out, *refs[19 + 2 * nm:23 + 2 * nm])
        s = pl.program_id(0)
        x, y, c, chips = _coords()
        b = 2 * x + y
        blks = [2 * chip[0] + chip[1] for chip in chips]

        def cw_cols(blk):
            return cw_out.at[:, pl.ds(blk * (D // N_CHIP), D // N_CHIP)]

        def cw_copy(k, blk):
            src = cw_in if blk is None else cw_cols(blk)
            return pltpu.make_async_remote_copy(src_ref=src, dst_ref=cw_cols(b if blk is None else blk), send_sem=cw_s.at[k],
                                                recv_sem=cw_r.at[k], device_id=(*chips[k], c), device_id_type=MESH)

        cw_local = pltpu.make_async_copy(cw_in, cw_cols(b), loc.at[1])
        hb_copy = pltpu.make_async_copy(hbuf, hb_any, loc.at[0])

        def load(step):
            slot = lax.rem(step, 2)
            block, piece = piece_at(step, b_ref[0])
            return pltpu.make_async_copy(win_f.at[block, piece], wv.at[slot], loc.at[2 + slot])

        def chunk(i):
            return pltpu.make_async_copy(x_any.at[pl.ds(i * tm, tm)], xbuf.at[i % 2], loc.at[4 + i % 2])

        def first():
            g_in.start()
            cw_local.start()
            for k in range(3):
                cw_copy(k, None).start()
            load(0).start()
            chunk(0).start()
            for i in range(ni):
                if i + 1 < ni:
                    chunk(i + 1).start()
                chunk(i).wait()
                h, _ = _rms(xbuf[i % 2], g_ref[...])
                hbuf[pl.ds(i * tm, tm), :] = h.astype(BF16)
            hb_copy.start()

        events = {step: [] for step in range(steps)}
        events[0].append(first)
        for p in range(N_PIECE):
            events[2 * p + 2].append(functools.partial(g_in.hop, [p]))
            events[near0 + 2 * p - 1].append(functools.partial(g_in.near_ready, [p]))
            events[far0 + p - 2].append(functools.partial(g_in.far, [p]))
            events[far0 + p - 1].append(functools.partial(g_in.far_ready, [p]))
        events[2 * N_PIECE + 1].append(g_more.start)
        for step, todo in events.items():
            if todo:
                @pl.when(s == step)
                def _(todo=todo):
                    for do in todo:
                        do()

        @pl.when(s + 1 < steps)
        def _():
            load(s + 1).start()

        load(s).wait()
        for i in range(ni):
            rows = pl.ds(i * tm, tm)
            o_ref[rows, :] = jnp.dot(hbuf[rows, :], wv[lax.rem(s, 2)], preferred_element_type=F32).astype(BF16)

        @pl.when(s == steps - 1)
        def _():
            g_more.hop()
            g_more.far()
            for k in range(3):
                cw_copy(k, blks[k]).wait_recv()
            for k in range(3):
                cw_copy(k, None).wait_send()
            cw_local.wait()
            hb_copy.wait()
            g_more.near_ready()
            g_more.far_ready()
            g_in.drain()
            g_more.drain()

    def out_col(s, b):
        block, piece = piece_at(s, b[0])
        return 0, block * N_PIECE + piece

    in_specs = [ANY, pl.BlockSpec((1, D), lambda s, b: (0, 0)), ANY, ANY] + [ANY] * nm
    out_specs = [pl.BlockSpec((t, IN_PIECE), out_col), ANY, ANY, ANY] + [ANY] * nm
    outs = pl.pallas_call(
        body, out_shape=[jax.ShapeDtypeStruct((t, IN_DIM), BF16), jax.ShapeDtypeStruct((t, D), BF16),
                         jax.ShapeDtypeStruct(win_own.shape, BF16), jax.ShapeDtypeStruct((8, D), F32)]
        + [jax.ShapeDtypeStruct(f.shape, f.dtype) for f in more],
        grid_spec=pltpu.PrefetchScalarGridSpec(
            num_scalar_prefetch=1, grid=(steps,), in_specs=in_specs, out_specs=out_specs,
            scratch_shapes=[pltpu.VMEM((t, D), BF16), pltpu.VMEM((2, tm, D), F32), pltpu.VMEM((2, D, IN_PIECE), BF16)]
            + [pltpu.SemaphoreType.DMA((3,))] * 2 + [pltpu.SemaphoreType.DMA((6,))]
            + _gather_sems(N_PIECE) + _gather_sems(nm)),
        input_output_aliases={3: 2, **{5 + w: 4 + w for w in range(nm)}},
        compiler_params=_cp(("arbitrary",), VMEM_LIMIT), name="proj_gather")(b_idx, x, g, win_own, cw8s, *more)
    return outs[0], outs[1], outs[2], outs[3], outs[4:]


class _Gather:
    def __init__(self, outs, ici_s, ici_r, d2d_s, d2d_r):
        x, y, c, _ = _coords()
        self.outs, self.c = outs, c
        self.sems = ici_s, ici_r, d2d_s, d2d_r
        self.b, self.bx, self.by, self.bd = 2 * x + y, 2 * (1 - x) + y, 2 * x + (1 - y), 2 * (1 - x) + (1 - y)
        self.xn, self.yn, self.sib = (1 - x, y, c), (x, 1 - y, c), (x, y, 1 - c)

    def piece(self, w, blk, hc, quarter=None):
        hr = self.outs[w].shape[1] // 2
        if quarter is None:
            return self.outs[w].at[blk, pl.ds(hc * hr, hr)]
        return self.outs[w].at[blk, pl.ds(hc * hr + quarter * (hr // 2), hr // 2)]

    def ici(self, w, k, ref, to):
        return pltpu.make_async_remote_copy(src_ref=ref, dst_ref=ref, send_sem=self.sems[0].at[w, k],
                                            recv_sem=self.sems[1].at[w, k], device_id=to, device_id_type=MESH)

    def d2d(self, w, k, ref):
        return pltpu.make_async_remote_copy(src_ref=ref, dst_ref=ref, send_sem=self.sems[2].at[w, k],
                                            recv_sem=self.sems[3].at[w, k], device_id=self.sib, device_id_type=MESH)

    def all(self):
        return range(len(self.outs))

    def start(self):
        for w in self.all():
            mine = self.piece(w, self.b, self.c)
            self.ici(w, 0, mine, self.xn).start()
            self.ici(w, 1, mine, self.yn).start()

    def hop(self, ws=None):
        c = self.c
        for w in ws or self.all():
            self.ici(w, 0, self.piece(w, self.bx, c), self.xn).wait_recv()
            self.ici(w, 1, self.piece(w, self.by, c), self.yn).wait_recv()
            self.ici(w, 2, self.piece(w, self.bx, c, 0), self.yn).start()
            self.ici(w, 3, self.piece(w, self.by, c, 1), self.xn).start()
            self.d2d(w, 0, self.piece(w, self.bx, c)).start()
            self.d2d(w, 1, self.piece(w, self.by, c)).start()

    def near_ready(self, ws=None):
        for w in ws or self.all():
            self.d2d(w, 0, self.piece(w, self.bx, 1 - self.c)).wait_recv()
            self.d2d(w, 1, self.piece(w, self.by, 1 - self.c)).wait_recv()

    def far(self, ws=None):
        c = self.c
        for w in ws or self.all():
            self.ici(w, 2, self.piece(w, self.bd, c, 0), self.yn).wait_recv()
            self.ici(w, 3, self.piece(w, self.bd, c, 1), self.xn).wait_recv()
            self.d2d(w, 2, self.piece(w, self.bd, c, 0)).start()
            self.d2d(w, 3, self.piece(w, self.bd, c, 1)).start()

    def far_ready(self, ws=None):
        for w in ws or self.all():
            self.d2d(w, 2, self.piece(w, self.bd, 1 - self.c, 0)).wait_recv()
            self.d2d(w, 3, self.piece(w, self.bd, 1 - self.c, 1)).wait_recv()

    def drain(self):
        c = self.c
        for w in self.all():
            mine = self.piece(w, self.b, c)
            self.ici(w, 0, mine, self.xn).wait_send()
            self.ici(w, 1, mine, self.yn).wait_send()
            self.ici(w, 2, self.piece(w, self.bx, c, 0), self.yn).wait_send()
            self.ici(w, 3, self.piece(w, self.by, c, 1), self.xn).wait_send()
            self.d2d(w, 0, self.piece(w, self.bx, c)).wait_send()
            self.d2d(w, 1, self.piece(w, self.by, c)).wait_send()
            self.d2d(w, 2, self.piece(w, self.bd, c, 0)).wait_send()
            self.d2d(w, 3, self.piece(w, self.bd, c, 1)).wait_send()


def _gather_sems(nw):
    return [pltpu.SemaphoreType.DMA((max(nw, 1), 4))] * 4


def _mixer_fwd(proj, cw8, lng, lnb, wc, bsb, fulls, tm=256):
    t = proj.shape[0]
    nt = t // tm
    nch = tm // CH
    nw = len(fulls)

    def body(*refs):
        p_ref, cw_ref, lng_ref, lnb_ref, wc_ref, bsb_ref = refs[:6]
        mix_ref = refs[6 + nw]
        w_outs = refs[7 + nw:7 + 2 * nw]
        prev_ref = refs[7 + 2 * nw]
        gather = _Gather(w_outs, *refs[8 + 2 * nw:])

        @pl.when(pl.program_id(0) == 0)
        def _():
            gather.start()
            prev_ref[...] = jnp.zeros_like(prev_ref)

        @pl.when(pl.program_id(0) == nt // 2)
        def _():
            gather.hop()

        @pl.when(pl.program_id(0) == nt - 1)
        def _():
            gather.far()

        rows = lax.broadcasted_iota(jnp.int32, (tm, CH), 0)
        for s in range(HEADS):
            cs = pl.ds(CH * s, CH)

            def slab(k):
                return p_ref[:, pl.ds(k * SLAB + CH * s, CH)].astype(F32)

            gb, gc, xa, za = slab(0), slab(1), slab(2), slab(3)
            cx = gc * xa
            p6 = jnp.broadcast_to(prev_ref[6:7, cs], (tm, CH))
            p7 = jnp.broadcast_to(prev_ref[7:8, cs], (tm, CH))
            c1 = jnp.where(rows == 0, p7, pltpu.roll(cx, 1, 0))
            c2 = jnp.where(rows == 0, p6, jnp.where(rows == 1, p7, pltpu.roll(cx, 2, 0)))
            prev_ref[:, cs] = cx[tm - 8:, :]
            cv = cw_ref[0:1, cs] * c2 + cw_ref[1:2, cs] * c1 + cw_ref[2:3, cs] * cx
            mix_ref[:, cs] = (gb * cv * (za * _sigmoid(za))).astype(BF16)

            u, v, zb = slab(4), slab(5), slab(6)
            ug, vg = _gelu(u), _gelu(v)
            dlt = vg - jnp.mean(vg, axis=-1, keepdims=True)
            vhat = dlt * lax.rsqrt(jnp.mean(dlt * dlt, axis=-1, keepdims=True) + EPS)
            vn = (vhat * lng_ref[:, cs] + lnb_ref[:, cs]).astype(BF16)
            gate = ug * (zb * _sigmoid(zb))
            for c in range(nch):
                rs = slice(CH * c, CH * (c + 1))
                sp = jnp.dot(wc_ref[s], vn[rs], preferred_element_type=F32) + bsb_ref[s]
                mix_ref[rs, pl.ds(SLAB + CH * s, CH)] = (gate[rs] * sp).astype(BF16)

        @pl.when(pl.program_id(0) == nt - 1)
        def _():
            gather.near_ready()
            gather.far_ready()
            gather.drain()

    sems = _gather_sems(nw)
    outs = pl.pallas_call(
        body, grid=(nt,),
        in_specs=[pl.BlockSpec((tm, IN_DIM), lambda i: (i, 0)), _full((8, D)), _full((1, D)), _full((1, D)),
                  _full((HEADS, CH, CH)), _full((HEADS, CH, CH))] + [ANY] * nw,
        out_specs=[pl.BlockSpec((tm, MIX), lambda i: (i, 0))] + [ANY] * nw,
        out_shape=[jax.ShapeDtypeStruct((t, MIX), BF16)] + [jax.ShapeDtypeStruct(f.shape, f.dtype) for f in fulls],
        input_output_aliases={6 + w: 1 + w for w in range(nw)},
        scratch_shapes=[pltpu.VMEM((8, D), F32)] + sems,
        compiler_params=_cp(("arbitrary",), VMEM_LIMIT), name="mixer_fwd")(proj, cw8, lng, lnb, wc, bsb, *fulls)
    return outs[0], outs[1:]


def _mem_fwd(mem, gm, wkv_f):
    n_mem = mem.shape[0]

    def body(mem_ref, gm_ref, w_ref, k_ref, v_ref):
        m, _ = _rms(mem_ref[...], gm_ref[...])
        mb = m.astype(BF16)
        for j in range(N_CHIP):
            dst = k_ref if j < 2 else v_ref
            dst[:, pl.ds(KV_BLK * (j % 2), KV_BLK)] = jnp.dot(mb, w_ref[j], preferred_element_type=F32).astype(BF16)

    return pl.pallas_call(
        body, out_shape=[jax.ShapeDtypeStruct((n_mem, D), BF16), jax.ShapeDtypeStruct((n_mem, D), BF16)],
        compiler_params=_cp(None, VMEM_LIMIT), name="mem_fwd")(mem, gm, wkv_f)


def _tail(x, tgt, mixin, wout, wq, wxo, k, v, g2, g3, tm=512, sub=512):
    t = x.shape[0]
    n_mem = k.shape[0]
    scale = 1.0 / math.sqrt(XD)

    def body(x_ref, tgt_ref, mix_ref, wout_ref, wq_ref, wxo_ref, k_ref, v_ref, g2_ref, g3_ref,
             loss_ref, dmix_ref, dx1b_ref, h2_ref, dq_ref, o_ref, dx2b_ref, dk_ref, dv_ref, dg2_ref, dg3_ref):
        @pl.when(pl.program_id(0) == 0)
        def _():
            loss_ref[...] = jnp.zeros_like(loss_ref)
            dk_ref[...] = jnp.zeros_like(dk_ref)
            dv_ref[...] = jnp.zeros_like(dv_ref)
            dg2_ref[...] = jnp.zeros_like(dg2_ref)
            dg3_ref[...] = jnp.zeros_like(dg3_ref)

        g2, g3 = g2_ref[...], g3_ref[...]
        for sb in range(tm // sub):
            rs = pl.ds(sub * sb, sub)
            x1 = x_ref[rs, :] + jnp.dot(mix_ref[rs, :], wout_ref[...], preferred_element_type=F32)
            h2, r2 = _rms(x1, g2)
            h2b = h2.astype(BF16)
            h2_ref[rs, :] = h2b
            q = jnp.dot(h2b, wq_ref[...], preferred_element_type=F32).astype(BF16)
            probs, outs = [], []
            for hd in range(XH):
                hs = pl.ds(XD * hd, XD)
                s = _bdot_nt(q[:, XD * hd:XD * (hd + 1)], k_ref[:, hs]) * scale
                e = jnp.exp(s - jnp.max(s, axis=-1, keepdims=True))
                p = e / jnp.sum(e, axis=-1, keepdims=True)
                probs.append(p)
                outs.append(_bdot(p, v_ref[:, hs]))
            ob = jnp.concatenate(outs, axis=-1).astype(BF16)
            o_ref[rs, :] = ob
            x2 = x1 + jnp.dot(ob, wxo_ref[...], preferred_element_type=F32)
            y, r3 = _rms(x2, g3)
            diff = y - tgt_ref[rs, :]
            row_loss = jnp.sum(diff * diff, axis=-1, keepdims=True)
            loss_ref[...] += jnp.broadcast_to(jnp.sum(row_loss, axis=0, keepdims=True) * (0.5 / D), loss_ref.shape)

            dx2, dg3 = _rms_bwd(diff * (1.0 / D), x2, r3, g3)
            dg3_ref[...] += dg3
            dx2b = dx2.astype(BF16)
            dx2b_ref[rs, :] = dx2b
            do = _bdot_nt(dx2b, wxo_ref[...])
            dqs = []
            for hd in range(XH):
                hs = pl.ds(XD * hd, XD)
                p = probs[hd]
                do_h = do[:, XD * hd:XD * (hd + 1)]
                dv_ref[:, hs] += _bdot_tn(p, do_h)
                dp = _bdot_nt(do_h, v_ref[:, hs])
                ds = p * (dp - jnp.sum(dp * p, axis=-1, keepdims=True))
                dqs.append(_bdot(ds, k_ref[:, hs]) * scale)
                dk_ref[:, hs] += _bdot_tn(ds, q[:, XD * hd:XD * (hd + 1)]) * scale
            dq = jnp.concatenate(dqs, axis=-1).astype(BF16)
            dq_ref[rs, :] = dq
            dx1n, dg2 = _rms_bwd(_bdot_nt(dq, wq_ref[...]), x1, r2, g2)
            dg2_ref[...] += dg2
            dx1b = (dx2 + dx1n).astype(BF16)
            dx1b_ref[rs, :] = dx1b
            dmix_ref[rs, :] = _bdot_nt(dx1b, wout_ref[...]).astype(BF16)

    tok = lambda w: pl.BlockSpec((tm, w), lambda i: (i, 0))
    return pl.pallas_call(
        body, grid=(t // tm,),
        in_specs=[tok(D), tok(D), tok(MIX), _full((MIX, D), 1), _full((D, D), 1), _full((D, D), 1),
                  _full((n_mem, D), 1), _full((n_mem, D), 1), _full((1, D)), _full((1, D))],
        out_specs=[_full((1, D)), tok(MIX), tok(D), tok(D), tok(D), tok(D), tok(D),
                   _full((n_mem, D)), _full((n_mem, D)), _full((1, D)), _full((1, D))],
        out_shape=[jax.ShapeDtypeStruct((1, D), F32), jax.ShapeDtypeStruct((t, MIX), BF16),
                   jax.ShapeDtypeStruct((t, D), BF16),
                   jax.ShapeDtypeStruct((t, D), BF16), jax.ShapeDtypeStruct((t, D), BF16),
                   jax.ShapeDtypeStruct((t, D), BF16), jax.ShapeDtypeStruct((t, D), BF16),
                   jax.ShapeDtypeStruct((n_mem, D), F32), jax.ShapeDtypeStruct((n_mem, D), F32),
                   jax.ShapeDtypeStruct((1, D), F32), jax.ShapeDtypeStruct((1, D), F32)],
        compiler_params=_cp(("arbitrary",), VMEM_LIMIT), name="tail")(x, tgt, mixin, wout, wq, wxo, k, v, g2, g3)


def _mem_bwd(mem, gm, dk, dv, wkv_f):
    def body(mem_ref, gm_ref, dk_ref, dv_ref, w_ref, dw_ref, dwb_ref, dgm_ref):
        mem_v = mem_ref[...]
        m, rm = _rms(mem_v, gm_ref[...])
        mb = m.astype(BF16)
        dm = jnp.zeros_like(mem_v)
        for j in range(N_CHIP):
            src = dk_ref if j < 2 else dv_ref
            dkv = src[:, pl.ds(KV_BLK * (j % 2), KV_BLK)].astype(BF16)
            dw = _bdot_tn(mb, dkv)
            dw_ref[j] = dw
            dwb_ref[j] = dw.astype(BF16)
            dm = dm + _bdot_nt(dkv, w_ref[j])
        dgm_ref[...] = jnp.sum(dm * mem_v * rm, axis=0, keepdims=True)

    return pl.pallas_call(
        body, out_shape=[jax.ShapeDtypeStruct((N_CHIP, D, KV_BLK), F32), jax.ShapeDtypeStruct((N_CHIP, D, KV_BLK), BF16),
                         jax.ShapeDtypeStruct((1, D), F32)],
        compiler_params=_cp(None, VMEM_LIMIT), name="mem_bwd")(mem, gm, dk, dv, wkv_f)


def _mixer_bwd(proj, dmix, cw8, lng, lnb, wc, wct, bsb, win_f, x, dx1, g1, rows123, row7, tm=256):
    t = proj.shape[0]
    nt = t // tm
    nch = tm // CH
    hb = 16
    pair = 2 * CH
    assert pair == IN_PIECE

    def pieces_of(hp):
        return [(blk, off // IN_PIECE) for blk, off in (divmod(k * SLAB + pair * hp, IN_BLK) for k in range(N_SLAB))]

    def body(p_ref, pgc_ref, pxa_ref, dm_ref, cw_ref, lng_ref, lnb_ref, wc_ref, wct_ref, bsb_ref, w_any, x_ref,
             dx1_ref, g1_ref, r1_ref, r2_ref, r3_ref, r7_ref, dp_ref, sv_ref, dwc_ref, gx_ref,
             next_ref, dh_ref, w_ref, wsem):
        i = pl.program_id(0)
        dg1_ref, dlng_ref, dlnb_ref, dbs_ref = (sv_ref.at[pl.ds(r, 1)] for r in (0, 4, 5, 6))
        dcw_ref = sv_ref.at[pl.ds(8, 8)]

        def fetch(hp):
            return [pltpu.make_async_copy(w_any.at[blk, pc], w_ref.at[blk, pc], wsem.at[blk, pc]) for blk, pc in pieces_of(hp)]

        @pl.when(i == 0)
        def _():
            for hp in range(HEADS // 2):
                for cp in fetch(hp):
                    cp.start()
            next_ref[...] = jnp.zeros_like(next_ref)
            sv_ref[...] = jnp.zeros_like(sv_ref)
            dwc_ref[...] = jnp.zeros_like(dwc_ref)
            for r, ref in ((1, r1_ref), (2, r2_ref), (3, r3_ref), (7, r7_ref)):
                sv_ref[r:r + 1, :] = ref[...]

        first_tile = i == nt - 1
        rows = lax.broadcasted_iota(jnp.int32, (tm, CH), 0)
        ones8 = jnp.ones((8, CH), BF16)
        for s in range(HEADS):
            cs = pl.ds(CH * s, CH)

            def slab(k):
                return p_ref[:, pl.ds(k * SLAB + CH * s, CH)].astype(F32)

            gb, gc, xa, za = slab(0), slab(1), slab(2), slab(3)
            da = dm_ref[:, cs].astype(F32)
            cx = gc * xa
            cxp = pgc_ref[:, cs].astype(F32) * pxa_ref[:, cs].astype(F32)
            cxp = jnp.where(first_tile, jnp.zeros_like(cxp), cxp)
            p6 = jnp.broadcast_to(cxp[hb - 2:hb - 1, :], (tm, CH))
            p7 = jnp.broadcast_to(cxp[hb - 1:hb, :], (tm, CH))
            c1 = jnp.where(rows == 0, p7, pltpu.roll(cx, 1, 0))
            c2 = jnp.where(rows == 0, p6, jnp.where(rows == 1, p7, pltpu.roll(cx, 2, 0)))
            w0, w1, w2 = cw_ref[0:1, cs], cw_ref[1:2, cs], cw_ref[2:3, cs]
            cv = w0 * c2 + w1 * c1 + w2 * cx
            sg = _sigmoid(za)
            sa = za * sg
            dcv = da * gb * sa
            dp_ref[:, pl.ds(0 * SLAB + CH * s, CH)] = (da * cv * sa).astype(BF16)
            dp_ref[:, pl.ds(3 * SLAB + CH * s, CH)] = (da * gb * cv * (sg * (1.0 + za * (1.0 - sg)))).astype(BF16)
            n0 = jnp.broadcast_to(next_ref[0:1, cs], (tm, CH))
            n1 = jnp.broadcast_to(next_ref[1:2, cs], (tm, CH))
            u1 = jnp.where(rows == tm - 1, n0, pltpu.roll(dcv, tm - 1, 0))
            u2 = jnp.where(rows == tm - 2, n0, jnp.where(rows == tm - 1, n1, pltpu.roll(dcv, tm - 2, 0)))
            next_ref[:, cs] = dcv[0:8, :]
            dcx = w2 * dcv + w1 * u1 + w0 * u2
            dp_ref[:, pl.ds(1 * SLAB + CH * s, CH)] = (dcx * xa).astype(BF16)
            dp_ref[:, pl.ds(2 * SLAB + CH * s, CH)] = (dcx * gc).astype(BF16)
            dcw_ref[0:1, cs] += jnp.sum(dcv * c2, axis=0, keepdims=True)
            dcw_ref[1:2, cs] += jnp.sum(dcv * c1, axis=0, keepdims=True)
            dcw_ref[2:3, cs] += jnp.sum(dcv * cx, axis=0, keepdims=True)

            u, v, zb = slab(4), slab(5), slab(6)
            db = dm_ref[:, pl.ds(SLAB + CH * s, CH)].astype(F32)
            ug, ugrad = _gelu_parts(u)
            vg, vgrad = _gelu_parts(v)
            dlt = vg - jnp.mean(vg, axis=-1, keepdims=True)
            rstd = lax.rsqrt(jnp.mean(dlt * dlt, axis=-1, keepdims=True) + EPS)
            vhat = dlt * rstd
            lg = lng_ref[:, cs]
            vn = (vhat * lg + lnb_ref[:, cs]).astype(BF16)
            sgb = _sigmoid(zb)
            szb = zb * sgb
            sps, dvns = [], []
            dbs = jnp.zeros((8, CH), F32)
            dwc = jnp.zeros((CH, CH), F32)
            for c in range(nch):
                rs = slice(CH * c, CH * (c + 1))
                sp = jnp.dot(wc_ref[s], vn[rs], preferred_element_type=F32) + bsb_ref[s]
                dsp = (db[rs] * ug[rs] * szb[rs]).astype(BF16)
                dbs = dbs + lax.dot_general(ones8, dsp, (((1,), (1,)), ((), ())), preferred_element_type=F32)
                dwc = dwc + lax.dot_general(dsp, vn[rs], (((1,), (1,)), ((), ())), preferred_element_type=F32)
                dvns.append(jnp.dot(wct_ref[s], dsp, preferred_element_type=F32))
                sps.append(sp)
            sp = jnp.concatenate(sps, axis=0)
            dvn = jnp.concatenate(dvns, axis=0)
            dbs_ref[:, cs] += dbs[0:1]
            dwc_ref[s] += dwc
            dlng_ref[:, cs] += jnp.sum(dvn * vhat, axis=0, keepdims=True)
            dlnb_ref[:, cs] += jnp.sum(dvn, axis=0, keepdims=True)
            dvhat = dvn * lg
            dvg = rstd * (dvhat - jnp.mean(dvhat, axis=-1, keepdims=True)
                          - vhat * jnp.mean(dvhat * vhat, axis=-1, keepdims=True))
            dp_ref[:, pl.ds(4 * SLAB + CH * s, CH)] = (db * sp * szb * ugrad).astype(BF16)
            dp_ref[:, pl.ds(5 * SLAB + CH * s, CH)] = (dvg * vgrad).astype(BF16)
            dp_ref[:, pl.ds(6 * SLAB + CH * s, CH)] = (db * ug * sp * (sgb * (1.0 + zb * (1.0 - sgb)))).astype(BF16)

            if s % 2 == 1:
                @pl.when(i == 0)
                def _(hp=s // 2):
                    for cp in fetch(hp):
                        cp.wait()

                part = None
                for k in range(N_SLAB):
                    col = k * SLAB + pair * (s // 2)
                    blk, off = divmod(col, IN_BLK)
                    term = lax.dot_general(dp_ref[:, pl.ds(col, pair)], w_ref[blk, off // IN_PIECE],
                                           (((1,), (1,)), ((), ())), preferred_element_type=F32)
                    part = term if part is None else part + term
                if s == 1:
                    dh_ref[...] = part
                else:
                    dh_ref[...] += part

        xv = x_ref[...]
        r = lax.rsqrt(jnp.mean(xv * xv, axis=-1, keepdims=True) + EPS)
        dxn, dg = _rms_bwd(dh_ref[...], xv, r, g1_ref[...])
        gx_ref[...] = dx1_ref[...].astype(F32) + dxn
        dg1_ref[...] += dg

        @pl.when(i == nt - 1)
        def _():
            tril = lax.broadcasted_iota(jnp.int32, (CH, CH), 0) >= lax.broadcasted_iota(jnp.int32, (CH, CH), 1)
            for s in range(HEADS):
                dwc_ref[s] = jnp.where(tril, dwc_ref[s], 0.0)

    rev = lambda i: nt - 1 - i
    halo = lambda col: pl.BlockSpec((hb, SLAB), lambda i: (jnp.maximum(rev(i) * (tm // hb) - 1, 0), col))
    tok = lambda w: pl.BlockSpec((tm, w), lambda i: (rev(i), 0))
    return pl.pallas_call(
        body, grid=(nt,),
        in_specs=[tok(IN_DIM), halo(1), halo(2), tok(MIX), _full((8, D)), _full((1, D)), _full((1, D)),
                  _full((HEADS, CH, CH)), _full((HEADS, CH, CH)), _full((HEADS, CH, CH)),
                  ANY, tok(D), tok(D)] + [_full((1, D))] * 5,
        out_specs=[tok(IN_DIM), _full((16, D)), _full((HEADS, CH, CH)), tok(D)],
        out_shape=[jax.ShapeDtypeStruct((t, IN_DIM), BF16), jax.ShapeDtypeStruct((16, D), F32),
                   jax.ShapeDtypeStruct((HEADS, CH, CH), F32), jax.ShapeDtypeStruct((t, D), F32)],
        scratch_shapes=[pltpu.VMEM((8, D), F32), pltpu.VMEM((tm, D), F32),
                        pltpu.VMEM((N_CHIP, N_PIECE, D, IN_PIECE), BF16), pltpu.SemaphoreType.DMA((N_CHIP, N_PIECE))],
        compiler_params=_cp(("arbitrary",), VMEM_LIMIT), name="mixer_bwd")(
            proj, proj, proj, dmix, cw8, lng, lnb, wc, wct, bsb, win_f, x, dx1, g1, *rows123, row7)


def _grad_matmul(a, b, after, *, name, tk=1024):
    t, m = a.shape
    n = b.shape[1]
    nk = t // tk

    def body(a_ref, b_ref, after_ref, o_ref, ob_ref):
        kk = pl.program_id(0)
        part = lax.dot_general(a_ref[...], b_ref[...], (((0,), (0,)), ((), ())), preferred_element_type=F32)

        @pl.when(kk == 0)
        def _():
            o_ref[...] = part

        @pl.when(kk > 0)
        def _():
            o_ref[...] += part

        @pl.when(kk == nk - 1)
        def _():
            ob_ref[...] = o_ref[...].astype(BF16)

    o_spec = pl.BlockSpec((m, n), lambda k: (0, 0))
    o32, o16 = pl.pallas_call(
        body, grid=(nk,), in_specs=[pl.BlockSpec((tk, m), lambda k: (k, 0)), pl.BlockSpec((tk, n), lambda k: (k, 0)), ANY],
        out_specs=[o_spec, o_spec], out_shape=[jax.ShapeDtypeStruct((m, n), F32), jax.ShapeDtypeStruct((m, n), BF16)],
        compiler_params=_cp(("arbitrary",), VMEM_LIMIT), name=name)(a, b, after)
    return o32.reshape(N_CHIP, m // N_CHIP, n), o16.reshape(N_CHIP, m // N_CHIP, n)


def _coords():
    x, y, c = lax.axis_index("x"), lax.axis_index("y"), lax.axis_index("c")
    chips = [(1 - x, y), (x, 1 - y), (1 - x, 1 - y)]
    return x, y, c, chips


def _pair_reduce(c_idx, grads, grads_b, smalls, name):
    ng, ns = len(grads), len(smalls)
    halves = [g.shape[1] // 2 for g in grads]

    def body(c_ref, *refs):
        g_in, gb_any = refs[:ng], refs[ng:2 * ng]
        s_own, s_any = refs[2 * ng:2 * ng + ns], refs[2 * ng + ns:2 * ng + 2 * ns]
        o = refs[2 * ng + 2 * ns:4 * ng + 3 * ns]
        lands = refs[4 * ng + 3 * ns:5 * ng + 4 * ns]
        send, recv = refs[5 * ng + 4 * ns:]
        x, y, c, _ = _coords()
        j = pl.program_id(0)

        def big(i, blk):
            return pltpu.make_async_remote_copy(
                src_ref=gb_any[i].at[blk, pl.ds((1 - c) * halves[i], halves[i])], dst_ref=lands[i].at[blk],
                send_sem=send.at[i, blk], recv_sem=recv.at[i, blk], device_id=(x, y, 1 - c), device_id_type=MESH)

        def small(i):
            return pltpu.make_async_remote_copy(
                src_ref=s_any[i].at[1 - c], dst_ref=lands[ng + i],
                send_sem=send.at[ng + i, 0], recv_sem=recv.at[ng + i, 0], device_id=(x, y, 1 - c), device_id_type=MESH)

        @pl.when(j == 0)
        def _():
            for blk in range(N_CHIP):
                for i in range(ng):
                    big(i, blk).start()
            for i in range(ns):
                small(i).start()

        for i in range(ng):
            big(i, j).wait_recv()
            tot = g_in[i][...] + lands[i][j].astype(F32)
            o[i][...] = tot
            o[ng + i][...] = tot.astype(BF16)

        @pl.when(j == N_CHIP - 1)
        def _():
            for i in range(ns):
                small(i).wait_recv()
                o[2 * ng + i][...] = s_own[i][...] + lands[ng + i][...]
                small(i).wait_send()
            for blk in range(N_CHIP):
                for i in range(ng):
                    big(i, blk).wait_send()

    in_specs = [pl.BlockSpec((None, None, halves[i], g.shape[2]), lambda b, c: (b, c[0], 0, 0)) for i, g in enumerate(grads)]
    in_specs += [ANY] * ng
    in_specs += [pl.BlockSpec((None, s.shape[0] // 2, s.shape[1]), lambda b, c: (c[0], 0, 0)) for s in smalls]
    in_specs += [ANY] * ns
    blk = [pl.BlockSpec((None, halves[i], g.shape[2]), lambda b, c: (b, 0, 0)) for i, g in enumerate(grads)]
    out_specs = blk + blk + [pl.BlockSpec((s.shape[0] // 2, s.shape[1]), lambda b, c: (0, 0)) for s in smalls]
    out_shape = [jax.ShapeDtypeStruct((N_CHIP, halves[i], g.shape[2]), F32) for i, g in enumerate(grads)]
    out_shape += [jax.ShapeDtypeStruct((N_CHIP, halves[i], g.shape[2]), BF16) for i, g in enumerate(grads)]
    out_shape += [jax.ShapeDtypeStruct((s.shape[0] // 2, s.shape[1]), F32) for s in smalls]
    scratch = [pltpu.VMEM((N_CHIP, halves[i], g.shape[2]), BF16) for i, g in enumerate(grads)]
    scratch += [pltpu.VMEM((s.shape[0] // 2, s.shape[1]), F32) for s in smalls]
    scratch += [pltpu.SemaphoreType.DMA((ng + ns, N_CHIP)), pltpu.SemaphoreType.DMA((ng + ns, N_CHIP))]
    grads4 = [g.reshape(N_CHIP, 2, halves[i], g.shape[2]) for i, g in enumerate(grads)]
    smalls3 = [s.reshape(2, s.shape[0] // 2, s.shape[1]) for s in smalls]
    return pl.pallas_call(
        body, out_shape=out_shape,
        grid_spec=pltpu.PrefetchScalarGridSpec(num_scalar_prefetch=1, grid=(N_CHIP,), in_specs=in_specs,
                                               out_specs=out_specs, scratch_shapes=scratch),
        compiler_params=_cp(("arbitrary",), VMEM_LIMIT), name=name)(c_idx, *grads4, *grads_b, *smalls3, *smalls3)


def _grad_matmul_pair(c_idx, a, b, smalls, small_dtypes, after, *, name, tk=2048):
    t, m = a.shape
    bn = b.shape[1] // N_CHIP
    nk = t // tk
    hr = m // 2
    ns = len(smalls)

    def body(c_ref, a_ref, b_ref, *refs):
        s_own, s_any = refs[:ns], refs[ns:2 * ns]
        o32, o16 = refs[2 * ns + 1], refs[2 * ns + 2]
        o_small = refs[2 * ns + 3:3 * ns + 3]
        acc, tb, land, st16 = refs[3 * ns + 3:3 * ns + 7]
        s_land, s_stage = refs[3 * ns + 7:4 * ns + 7], refs[4 * ns + 7:5 * ns + 7]
        send, recv, loc = refs[5 * ns + 7:]
        x, y, c, _ = _coords()
        sibling = dict(device_id=(x, y, 1 - c), device_id_type=MESH)
        j, kk = pl.program_id(0), pl.program_id(1)
        mine = pl.ds(pl.multiple_of(c * hr, hr), hr)
        theirs = pl.ds(pl.multiple_of((1 - c) * hr, hr), hr)

        def to_sibling(blk):
            return pltpu.make_async_remote_copy(src_ref=tb, dst_ref=land.at[blk], send_sem=send.at[blk],
                                                recv_sem=recv.at[blk], **sibling)

        def small(i):
            return pltpu.make_async_remote_copy(src_ref=s_any[i].at[1 - c], dst_ref=s_land[i], send_sem=send.at[N_CHIP + i],
                                                recv_sem=recv.at[N_CHIP + i], **sibling)

        def written(blk):
            return (pltpu.make_async_copy(acc.at[blk % 2, mine], o32.at[blk], loc.at[0]),
                    pltpu.make_async_copy(st16, o16.at[blk], loc.at[1]))

        def finish(blk):
            to_sibling(blk).wait_recv()

            @pl.when(blk > 0)
            def _():
                for cp in written(blk - 1):
                    cp.wait()

            tot = acc[blk % 2, mine, :] + land[blk].astype(F32)
            acc[blk % 2, mine, :] = tot
            st16[...] = tot.astype(BF16)
            for cp in written(blk):
                cp.start()

        def small_out(i):
            return pltpu.make_async_copy(s_stage[i], o_small[i], loc.at[2 + i])

        @pl.when((j == 0) & (kk == 0))
        def _():
            for i in range(ns):
                small(i).start()

        @pl.when((j == 1) & (kk == 0))
        def _():
            for i in range(ns):
                small(i).wait_recv()
                s_stage[i][...] = (s_own[i][...] + s_land[i][...]).astype(small_dtypes[i])
                small_out(i).start()

        @pl.when((j > 0) & (kk == 0))
        def _():
            finish(j - 1)

        part = lax.dot_general(a_ref[...], b_ref[...], (((0,), (0,)), ((), ())), preferred_element_type=F32)
        slot = lax.rem(j, 2)

        @pl.when(kk == 0)
        def _():
            acc[slot] = part

        @pl.when(kk > 0)
        def _():
            acc[slot] += part

        @pl.when(kk == nk - 1)
        def _():
            @pl.when(j > 0)
            def _():
                to_sibling(j - 1).wait_send()

            tb[...] = acc[slot, theirs, :].astype(BF16)
            to_sibling(j).start()

        @pl.when((j == N_CHIP - 1) & (kk == nk - 1))
        def _():
            finish(j)
            for i in range(ns):
                small_out(i).wait()
                small(i).wait_send()
            for cp in written(j):
                cp.wait()
            to_sibling(j).wait_send()

    halves = [(s.shape[0] // 2, s.shape[1]) for s in smalls]
    in_specs = [pl.BlockSpec((tk, m), lambda j, k, c: (k, 0)), pl.BlockSpec((tk, bn), lambda j, k, c: (k, j))]
    in_specs += [pl.BlockSpec((None,) + h, lambda j, k, c: (c[0], 0, 0)) for h in halves] + [ANY] * ns + [ANY]
    out_shape = [jax.ShapeDtypeStruct((N_CHIP, hr, bn), F32), jax.ShapeDtypeStruct((N_CHIP, hr, bn), BF16)]
    out_shape += [pltpu.HBM(h, dt) for h, dt in zip(halves, small_dtypes)]
    scratch = [pltpu.VMEM((2, m, bn), F32), pltpu.VMEM((hr, bn), BF16),
               pltpu.VMEM((N_CHIP, hr, bn), BF16), pltpu.VMEM((hr, bn), BF16)]
    scratch += [pltpu.VMEM(h, F32) for h in halves] + [pltpu.VMEM(h, dt) for h, dt in zip(halves, small_dtypes)]
    scratch += [pltpu.SemaphoreType.DMA((N_CHIP + ns,)), pltpu.SemaphoreType.DMA((N_CHIP + ns,)),
                pltpu.SemaphoreType.DMA((2 + ns,))]
    smalls3 = [s.reshape((2,) + h) for s, h in zip(smalls, halves)]
    outs = pl.pallas_call(
        body, out_shape=out_shape,
        grid_spec=pltpu.PrefetchScalarGridSpec(num_scalar_prefetch=1, grid=(N_CHIP, nk), in_specs=in_specs,
                                               out_specs=[ANY, ANY] + [_HBM] * ns, scratch_shapes=scratch),
        compiler_params=_cp(("arbitrary", "arbitrary"), VMEM_LIMIT), name=name)(c_idx, a, b, *smalls3, *smalls3, after)
    return outs[0], outs[1], list(outs[2:])


_HBM = pl.BlockSpec(memory_space=pltpu.HBM)
_SEM = pl.BlockSpec(memory_space=pltpu.SEMAPHORE)


def _split_copies(ins, lands, ng, send, recv, arriving):
    x, y, c, chips = _coords()
    b = 2 * x + y
    copies = []
    for i in range(len(ins)):
        for k in range(3):
            blk = 2 * chips[k][0] + chips[k][1]
            src, dst, got = (ins[i].at[blk], lands[i].at[k], lands[i].at[k]) if i < ng else (ins[i], lands[i].at[b], lands[i].at[blk])
            sems = dict(send_sem=send.at[3 * i + k], recv_sem=recv.at[3 * i + k], device_id=(*chips[k], c), device_id_type=MESH)
            if arriving:
                copies.append(pltpu.make_async_remote_copy(src_ref=got, dst_ref=got, **sems))
            else:
                copies.append(pltpu.make_async_remote_copy(src_ref=src, dst_ref=dst, **sems))
    return copies


def _exchange_begin(sums_b, smalls, name):
    ng, n = len(sums_b), len(sums_b) + len(smalls)
    srcs = list(sums_b) + list(smalls)
    lands = [lax.empty((3,) + g.shape[1:], g.dtype) for g in sums_b] + [lax.empty((N_CHIP,) + s.shape, s.dtype) for s in smalls]

    def body(*refs):
        ins, land_refs = refs[:n], refs[n:2 * n]
        send, recv = refs[2 * n], refs[2 * n + 1]
        token = refs[4 * n + 2]
        for cp in _split_copies(ins, land_refs, ng, send, recv, False):
            cp.start()
        token[...] = jnp.zeros_like(token)

    hbm = lambda a: pltpu.HBM(a.shape, a.dtype)
    outs = pl.pallas_call(
        body, name=name,
        out_shape=(pltpu.SemaphoreType.DMA((3 * n,)), pltpu.SemaphoreType.DMA((3 * n,)), *[hbm(a) for a in srcs + lands],
                   jax.ShapeDtypeStruct((8, 128), F32)),
        in_specs=[_HBM] * (2 * n), out_specs=(_SEM, _SEM, *[_HBM] * (2 * n), pl.BlockSpec(memory_space=pltpu.VMEM)),
        input_output_aliases={i: 2 + i for i in range(2 * n)},
        compiler_params=pltpu.CompilerParams(has_side_effects=pltpu.SideEffectType.DATAFLOW_SIDE_EFFECTING),
    )(*[pltpu.with_memory_space_constraint(a, pltpu.HBM) for a in srcs + lands])
    return outs[0], outs[1], list(outs[2:2 + n]), list(outs[2 + n:2 + 2 * n]), outs[2 + 2 * n]


def _exchange_end(send, recv, srcs, lands, ng, which, after, name):
    n = len(srcs)
    after = list(after)

    def body(*refs):
        ins, land_refs = refs[:n], refs[n:2 * n]
        send_ref, recv_ref = refs[2 * n], refs[2 * n + 1]
        outgoing = _split_copies(ins, land_refs, ng, send_ref, recv_ref, False)
        arriving = _split_copies(ins, land_refs, ng, send_ref, recv_ref, True)
        for i in which:
            for cp in outgoing[3 * i:3 * i + 3]:
                cp.wait_send()
        for i in which:
            for cp in arriving[3 * i:3 * i + 3]:
                cp.wait_recv()

    hbm = lambda a: pltpu.HBM(a.shape, a.dtype)
    outs = pl.pallas_call(
        body, name=name, out_shape=tuple(hbm(a) for a in list(srcs) + list(lands)),
        in_specs=[_HBM] * (2 * n) + [_SEM, _SEM] + [ANY] * len(after), out_specs=tuple([_HBM] * (2 * n)),
        input_output_aliases={i: i for i in range(2 * n)},
        compiler_params=pltpu.CompilerParams(has_side_effects=pltpu.SideEffectType.DATAFLOW_SIDE_EFFECTING),
    )(*srcs, *lands, send, recv, *after)
    return list(outs[:n]), list(outs[n:])


def _chip_reduce(bc_idx, sums, recvd, smalls_slots, smalls_own, name, steps=4):
    ng, ns = len(sums), len(smalls_slots)
    n = ng + ns
    assert steps >= 2
    halves = [g.shape[1] for g in sums] + [s.shape[1] for s in smalls_slots]
    rows = [g.shape[1] // steps for g in sums]

    def body(bc_ref, *refs):
        own, rx = refs[:ng], refs[ng:2 * ng]
        sl = refs[2 * ng:2 * ng + ns]
        sl_own = refs[2 * ng + ns:2 * ng + 2 * ns]
        o = refs[2 * ng + 2 * ns:2 * ng + 2 * ns + n]
        tiles = refs[2 * ng + 2 * ns + n:2 * ng + 2 * ns + 2 * n]
        keep, send, recv = refs[2 * ng + 2 * ns + 2 * n:]
        x, y, c, _ = _coords()
        sibling = dict(device_id=(x, y, 1 - c), device_id_type=MESH)
        r = pl.program_id(0)

        def writes(i, step, slot):
            dst = o[i].at[pl.ds(c * halves[i] + step * rows[i], rows[i])]
            return (pltpu.make_async_copy(tiles[i].at[slot], dst, keep.at[i, slot]),
                    pltpu.make_async_remote_copy(src_ref=tiles[i].at[slot], dst_ref=dst, send_sem=send.at[i, slot],
                                                 recv_sem=recv.at[i, step], **sibling))

        def small_writes(i):
            dst = o[i].at[pl.ds(c * halves[i], halves[i])]
            return (pltpu.make_async_copy(tiles[i], dst, keep.at[i, 0]),
                    pltpu.make_async_remote_copy(src_ref=tiles[i], dst_ref=dst, send_sem=send.at[i, 0],
                                                 recv_sem=recv.at[i, 0], **sibling))

        def arriving(i, step, nrows):
            dst = o[i].at[pl.ds((1 - c) * halves[i] + step * nrows, nrows)]
            return pltpu.make_async_remote_copy(src_ref=dst, dst_ref=dst, send_sem=send.at[i, 0], recv_sem=recv.at[i, step],
                                                **sibling)

        def finish(step, slot):
            for i in range(ng):
                local, remote = writes(i, step, slot)
                local.wait()
                remote.wait_send()

        @pl.when(r >= 2)
        def _():
            finish(r - 2, r % 2)

        for i in range(ng):
            tot = own[i][...]
            for j in range(3):
                tot = tot + rx[i][j].astype(F32)
            tiles[i][r % 2] = tot
            for cp in writes(i, r, r % 2):
                cp.start()

        @pl.when(r == 0)
        def _():
            for i in range(ns):
                term = [jnp.where(bc_ref[0] == kk, sl_own[i][...], sl[i][kk]).astype(F32) for kk in range(N_CHIP)]
                tiles[ng + i][...] = ((term[0] + term[1]) + term[2]) + term[3]
                for cp in small_writes(ng + i):
                    cp.start()

        @pl.when(r == steps - 1)
        def _():
            finish(steps - 2, (steps - 2) % 2)
            finish(steps - 1, (steps - 1) % 2)
            for i in range(ns):
                local, remote = small_writes(ng + i)
                local.wait()
                remote.wait_send()
                arriving(ng + i, 0, halves[ng + i]).wait_recv()
            for i in range(ng):
                for step in range(steps):
                    arriving(i, step, rows[i]).wait_recv()

    in_specs = [pl.BlockSpec((None, rows[i], g.shape[2]), lambda r, bc: (bc[0], r, 0)) for i, g in enumerate(sums)]
    in_specs += [pl.BlockSpec((3, rows[i], g.shape[2]), lambda r, bc: (0, r, 0)) for i, g in enumerate(sums)]
    in_specs += [pl.BlockSpec(s.shape, lambda r, bc: (0, 0, 0)) for s in smalls_slots]
    in_specs += [pl.BlockSpec(s.shape[1:], lambda r, bc: (0, 0)) for s in smalls_slots]
    out_shape = [jax.ShapeDtypeStruct((2 * g.shape[1], g.shape[2]), F32) for g in sums]
    out_shape += [jax.ShapeDtypeStruct((2 * s.shape[1], s.shape[2]), F32) for s in smalls_slots]
    scratch = [pltpu.VMEM((2, rows[i], g.shape[2]), F32) for i, g in enumerate(sums)]
    scratch += [pltpu.VMEM(s.shape[1:], F32) for s in smalls_slots]
    scratch += [pltpu.SemaphoreType.DMA((n, 2)), pltpu.SemaphoreType.DMA((n, 2)), pltpu.SemaphoreType.DMA((n, steps))]
    return list(pl.pallas_call(
        body, out_shape=out_shape,
        grid_spec=pltpu.PrefetchScalarGridSpec(num_scalar_prefetch=1, grid=(steps,), in_specs=in_specs,
                                               out_specs=[ANY] * n, scratch_shapes=scratch),
        compiler_params=_cp(("arbitrary",), VMEM_LIMIT), name=name)(bc_idx, *sums, *recvd, *smalls_slots, *smalls_own))


def _adamw_math(w, g, m, v):
    m2 = ADAM_B1 * m + (1.0 - ADAM_B1) * g
    v2 = ADAM_B2 * v + (1.0 - ADAM_B2) * (g * g)
    m_hat = m2 / (1.0 - ADAM_B1 ** ADAM_STEP)
    v_hat = v2 / (1.0 - ADAM_B2 ** ADAM_STEP)
    delta = -ADAM_LR * (m_hat / (jnp.sqrt(v_hat) + ADAM_EPS) + ADAM_WD * w)
    return delta, m2, v2


def _adamw_big(ws, gs, ms, vs, name, steps=8):
    n = len(ws)

    def body(*refs):
        for i in range(n):
            w_ref, g_ref, m_ref, v_ref = (refs[k * n + i] for k in range(4))
            d_ref, m2_ref, v2_ref, g2_ref = (refs[(4 + k) * n + i] for k in range(4))
            gv = g_ref[...]
            d_ref[...], m2_ref[...], v2_ref[...] = _adamw_math(w_ref[...], gv, m_ref[...], v_ref[...])
            g2_ref[...] = gv

    specs = [pl.BlockSpec((w.shape[0] // steps, w.shape[1]), lambda i: (i, 0)) for w in ws]
    shapes = [jax.ShapeDtypeStruct(w.shape, F32) for w in ws]
    outs = pl.pallas_call(
        body, grid=(steps,), in_specs=specs * 4, out_specs=specs * 4, out_shape=shapes * 4,
        compiler_params=_cp(("parallel",), VMEM_LIMIT), name=name)(*ws, *gs, *ms, *vs)
    return [tuple(outs[k * n + i] for k in range(4)) for i in range(n)]


def _adamw_small(b_idx, sv, sw, vecs, conv, ws):
    nv = len(vecs)
    cols = conv[0].shape[1]

    def body(b_ref, sv_ref, sw_ref, *refs):
        ins, outs = refs[:3 * nv + 6], refs[3 * nv + 6:]
        for i in range(nv):
            g = sv_ref[i:i + 1, :]
            w_ref, m_ref, v_ref = ins[3 * i:3 * i + 3]
            d_ref, m2_ref, v2_ref, g_ref = outs[4 * i:4 * i + 4]
            d_ref[...], m2_ref[...], v2_ref[...] = _adamw_math(w_ref[...], g, m_ref[...], v_ref[...])
            g_ref[...] = g
        g = sv_ref[8:8 + conv[0].shape[0], pl.ds(pl.multiple_of(b_ref[0] * cols, cols), cols)]
        w_ref, m_ref, v_ref = ins[3 * nv:3 * nv + 3]
        d_ref, m2_ref, v2_ref, g_ref = outs[4 * nv:4 * nv + 4]
        d_ref[...], m2_ref[...], v2_ref[...] = _adamw_math(w_ref[...], g, m_ref[...], v_ref[...])
        g_ref[...] = g
        w_ref, m_ref, v_ref = ins[3 * nv + 3:]
        d_ref, m2_ref, v2_ref, g_ref, one_ref = outs[4 * nv + 4:]
        g = sw_ref[...]
        d_ref[...], m2_ref[...], v2_ref[...] = _adamw_math(w_ref[...], g, m_ref[...], v_ref[...])
        g_ref[...] = g
        one_ref[...] = sv_ref[nv:nv + 1, 0:1]

    flat = [a for grp in vecs for a in grp] + list(conv) + list(ws)
    out_shape = [jax.ShapeDtypeStruct(grp[0].shape, F32) for grp in list(vecs) + [conv, ws] for _ in range(4)]
    out_shape += [jax.ShapeDtypeStruct((1, 1), F32)]
    vmem = pl.BlockSpec(memory_space=pltpu.VMEM)
    outs = pl.pallas_call(
        body, out_shape=out_shape, in_specs=[pl.BlockSpec(memory_space=pltpu.SMEM)] + [vmem] * (2 + len(flat)),
        out_specs=[vmem] * len(out_shape), name="adamw_small")(b_idx, sv, sw, *flat)
    return [tuple(outs[4 * i:4 * i + 4]) for i in range(nv + 2)], outs[4 * nv + 8]


def kernel(x, mem, norm_mix_g, w_in, conv_w, gm_ln_g, gm_ln_b, gm_ws, gm_bs, w_out, norm_x_g, norm_mem_g, w_q, w_kv, w_xo, norm_final_g, loss_target, m_norm_mix_g, m_w_in, m_conv_w, m_gm_ln_g, m_gm_ln_b, m_gm_ws, m_gm_bs, m_w_out, m_norm_x_g, m_norm_mem_g, m_w_q, m_w_kv, m_w_xo, m_norm_final_g, v_norm_mix_g, v_w_in, v_conv_w, v_gm_ln_g, v_gm_ln_b, v_gm_ws, v_gm_bs, v_w_out, v_norm_x_g, v_norm_mem_g, v_w_q, v_w_kv, v_w_xo, v_norm_final_g):
    t = x.shape[1]
    xi = lax.axis_index("x")
    yi = lax.axis_index("y")
    ci = lax.axis_index("c")
    b_idx = jnp.reshape(2 * xi + yi, (1,)).astype(jnp.int32)
    c_idx = jnp.reshape(ci, (1,)).astype(jnp.int32)

    x2d, mem2d, tgt = x[0], mem[0], loss_target[0]
    big = [w_in[0], w_out[0], w_q[0], w_kv[0], w_xo[0]]
    big_m = [m_w_in[0], m_w_out[0], m_w_q[0], m_w_kv[0], m_w_xo[0]]
    big_v = [v_w_in[0], v_w_out[0], v_w_q[0], v_w_kv[0], v_w_xo[0]]
    g3 = norm_final_g.reshape(1, D)

    def pad8(a):
        return jnp.pad(a, ((0, 8 - a.shape[0]), (0, 0)))

    own_blocks, (wc, wct, bsb) = _cast_shards(b_idx, big, gm_ws[0], gm_bs[0])

    proj, hb, win_f, cw8, (wq_f,) = _proj_gather(
        b_idx, x2d, norm_mix_g, own_blocks[0], pad8(conv_w[0]), [own_blocks[2]])
    mixin, (wout_f, wkv_f, wxo_f) = _mixer_fwd(
        proj, cw8, gm_ln_g, gm_ln_b, wc, bsb, [own_blocks[1], own_blocks[3], own_blocks[4]])
    wout2, wq2, wxo2 = wout_f.reshape(MIX, D), wq_f.reshape(D, D), wxo_f.reshape(D, D)
    k, v = _mem_fwd(mem2d, norm_mem_g, wkv_f)

    (loss_row, dmix, dx1b, h2b, dq, ob, dx2b, dk, dv, dg2, dg3) = _tail(
        x2d, tgt, mixin, wout2, wq2, wxo2, k, v, norm_x_g, g3)
    dwkv, dwkv_b, dgm = _mem_bwd(mem2d, norm_mem_g, dk, dv, wkv_f)
    dproj, sv, dwc, grad_x = _mixer_bwd(
        proj, dmix, cw8, gm_ln_g, gm_ln_b, wc, wct, bsb, win_f, x2d, dx1b, norm_mix_g, [dg2, dgm, dg3], loss_row)

    bc_idx = jnp.concatenate([b_idx, c_idx])
    sw = dwc.reshape(HEADS * CH, CH)
    dwin_sum, dwin_sum_b, psmall = _grad_matmul_pair(c_idx, hb, dproj, [sv, sw], [F32, BF16], dgm, name="grad_w_in")
    sums_b = [dwin_sum]
    send_b, recv_b, src_b, land_b, token_b = _exchange_begin([dwin_sum_b], psmall, "exchange_b_begin")

    dwxo, dwxo_b = _grad_matmul(ob, dx2b, token_b, name="grad_w_xo", tk=2048)
    dwq, dwq_b = _grad_matmul(h2b, dq, token_b, name="grad_w_q", tk=2048)
    dwout, dwout_b = _grad_matmul(mixin, dx1b, token_b, name="grad_w_out")
    ps_a = _pair_reduce(c_idx, [dwout, dwkv, dwq, dwxo], [dwout_b, dwkv_b, dwq_b, dwxo_b], [], "pair_reduce_a")
    sums_a, sums_a_b = list(ps_a[:4]), list(ps_a[4:8])
    send_a, recv_a, src_a, land_a, token_a = _exchange_begin(sums_a_b, [], "exchange_a_begin")

    src_b, rx2b = _exchange_end(send_b, recv_b, src_b, land_b, 1, [0, 1, 2], [token_a], "exchange_b_end")
    gwin, svf, swf = _chip_reduce(bc_idx, sums_b, rx2b[:1], rx2b[1:], src_b[1:], "chip_reduce_b")
    out_b = _adamw_big(big[:1], [gwin], big_m[:1], big_v[:1], "adamw_w_in")[0]

    row = lambda a: a.reshape(1, D)
    mat = lambda a: a.reshape(HEADS * CH, CH)
    small, loss = _adamw_small(
        b_idx, svf, swf,
        [(row(norm_mix_g), row(m_norm_mix_g), row(v_norm_mix_g)), (row(norm_x_g), row(m_norm_x_g), row(v_norm_x_g)),
         (row(norm_mem_g), row(m_norm_mem_g), row(v_norm_mem_g)), (row(norm_final_g), row(m_norm_final_g), row(v_norm_final_g)),
         (row(gm_ln_g), row(m_gm_ln_g), row(v_gm_ln_g)), (row(gm_ln_b), row(m_gm_ln_b), row(v_gm_ln_b)),
         (row(gm_bs), row(m_gm_bs), row(v_gm_bs))],
        (conv_w[0], m_conv_w[0], v_conv_w[0]), (mat(gm_ws), mat(m_gm_ws), mat(v_gm_ws)))

    def finish_a(part, src, land, after, tag):
        src, land = _exchange_end(send_a, recv_a, src, land, 4, part, after, "exchange_a%s_end" % tag)
        grads = _chip_reduce(bc_idx, [sums_a[i] for i in part], [land[i] for i in part], [], [], "chip_reduce_a" + tag,
                             steps=2)
        ids = [(1, 3, 2, 4)[i] for i in part]
        outs = _adamw_big([big[i] for i in ids], grads, [big_m[i] for i in ids], [big_v[i] for i in ids], "adamw_a" + tag,
                          steps=4)
        return src, land, outs

    src_a, land_a, (out_wout, out_wkv) = finish_a([0, 1], src_a, land_a, [out_b[0], small[0][0]], "1")
    _, _, (out_wq, out_wxo) = finish_a([2, 3], src_a, land_a, [out_wout[0]], "2")

    def unpack(k):
        vec = lambda i: small[i][k]
        return [vec(0), out_b[k][None], small[7][k][None], vec(4), vec(5), small[8][k].reshape(1, HEADS, CH, CH),
                vec(6).reshape(1, HEADS, CH), out_wout[k][None], vec(1), vec(2), out_wq[k][None], out_wkv[k][None],
                out_wxo[k][None], vec(3).reshape(D)]

    return (loss.reshape(()), grad_x[None], *unpack(3), *unpack(0), *unpack(1), *unpack(2))
```

```python
import functools
import math

import jax
import jax.numpy as jnp
from jax import lax
from jax.experimental import pallas as pl
from jax.experimental.pallas import tpu as pltpu

F32 = jnp.float32
BF16 = jnp.bfloat16
MESH = pl.DeviceIdType.MESH

D = 1024
SLAB = 1024
N_SLAB = 7
IN_DIM = N_SLAB * SLAB
MIX = 2 * SLAB
HEADS = 8
CH = 128
XH = 4
XD = D // XH
EPS = 1e-6
GELU_C = math.sqrt(2.0 / math.pi)
GELU_A = 0.044715
N_CHIP = 4
IN_BLK = IN_DIM // N_CHIP
IN_PIECE = 256
N_PIECE = IN_BLK // IN_PIECE
KV_BLK = 2 * D // N_CHIP

ADAM_LR, ADAM_B1, ADAM_B2, ADAM_EPS, ADAM_WD, ADAM_STEP = 0.001, 0.9, 0.999, 1e-08, 0.01, 10

VMEM_LIMIT = 60 * 1024 * 1024


def _cp(sem=None, vmem=None):
    return pltpu.CompilerParams(dimension_semantics=sem, vmem_limit_bytes=vmem)


def _full(shape, buffers=None):
    n = len(shape)
    if buffers is None:
        return pl.BlockSpec(shape, lambda *_: (0,) * n)
    return pl.BlockSpec(shape, lambda *_: (0,) * n, pipeline_mode=pl.Buffered(buffers))


ANY = pl.BlockSpec(memory_space=pl.ANY)


def _bdot(a, b):
    return jnp.dot(a.astype(BF16), b.astype(BF16), preferred_element_type=F32)


def _bdot_nt(a, b):
    return lax.dot_general(a.astype(BF16), b.astype(BF16), (((1,), (1,)), ((), ())), preferred_element_type=F32)


def _bdot_tn(a, b):
    return lax.dot_general(a.astype(BF16), b.astype(BF16), (((0,), (0,)), ((), ())), preferred_element_type=F32)


def _rms(x, g):
    r = lax.rsqrt(jnp.mean(x * x, axis=-1, keepdims=True) + EPS)
    return x * r * g, r


def _rms_bwd(dy, x, r, g):
    gdy = dy * g
    dx = r * gdy - x * (r * r * r) * jnp.mean(x * gdy, axis=-1, keepdims=True)
    dg = jnp.sum(dy * x * r, axis=0, keepdims=True)
    return dx, dg


def _gelu_parts(x):
    x2 = x * x
    t = jnp.tanh(GELU_C * (x + GELU_A * x * x2))
    val = 0.5 * x * (1.0 + t)
    grad = 0.5 * (1.0 + t) + 0.5 * x * (1.0 - t * t) * (GELU_C * (1.0 + 3.0 * GELU_A * x2))
    return val, grad


def _gelu(x):
    return 0.5 * x * (1.0 + jnp.tanh(GELU_C * (x + GELU_A * x * x * x)))


def _sigmoid(z):
    return 1.0 / (1.0 + jnp.exp(-z))


def _cast_shards(b_idx, arrs, gm_ws, gm_bs):
    n = len(arrs)
    steps = 4

    def body(b_ref, *refs):
        ws_ref, bs_ref = refs[n:n + 2]
        outs = refs[n + 2:2 * n + 2]
        wc_ref, wct_ref, bsb_ref = refs[2 * n + 2:]
        for p in range(N_PIECE):
            outs[0][p] = refs[0][:, pl.ds(p * IN_PIECE, IN_PIECE)].astype(BF16)
        for i in range(1, n):
            outs[i][...] = refs[i][...].astype(BF16)

        @pl.when(pl.program_id(0) == 0)
        def _():
            causal = lax.broadcasted_iota(jnp.int32, (CH, CH), 0) >= lax.broadcasted_iota(jnp.int32, (CH, CH), 1)
            for h in range(HEADS):
                w = jnp.where(causal, ws_ref[h], 0.0)
                wc_ref[h] = w.astype(BF16)
                wct_ref[h] = w.T.astype(BF16)
                bsb_ref[h] = jnp.broadcast_to(bs_ref[h:h + 1, :], (CH, CH)).T

    rows = [a.shape[0] // steps for a in arrs]
    in_specs = [pl.BlockSpec((rows[i], a.shape[1]), lambda i, b: (i, 0)) for i, a in enumerate(arrs)]
    in_specs += [pl.BlockSpec((HEADS, CH, CH), lambda i, b: (0, 0, 0)), pl.BlockSpec((HEADS, CH), lambda i, b: (0, 0))]
    out_specs = [pl.BlockSpec((None, N_PIECE, rows[0], IN_PIECE), lambda i, b: (b[0], 0, i, 0))]
    out_specs += [pl.BlockSpec((None, rows[i], a.shape[1]), lambda i, b: (b[0], i, 0)) for i, a in enumerate(arrs) if i > 0]
    out_specs += [pl.BlockSpec((HEADS, CH, CH), lambda i, b: (0, 0, 0))] * 3
    out_shape = [jax.ShapeDtypeStruct((N_CHIP, N_PIECE, arrs[0].shape[0], IN_PIECE), BF16)]
    out_shape += [jax.ShapeDtypeStruct((N_CHIP,) + a.shape, BF16) for a in arrs[1:]]
    out_shape += [jax.ShapeDtypeStruct((HEADS, CH, CH), dt) for dt in (BF16, BF16, F32)]
    outs = pl.pallas_call(
        body, out_shape=out_shape,
        grid_spec=pltpu.PrefetchScalarGridSpec(num_scalar_prefetch=1, grid=(steps,), in_specs=in_specs, out_specs=out_specs),
        compiler_params=_cp(("arbitrary",)), name="cast_shards")(b_idx, *arrs, gm_ws, gm_bs)
    return outs[:n], outs[n:]


def _proj_gather(b_idx, x, g, win_own, cw8s, more, tm=1024):
    t = x.shape[0]
    ni = t // tm
    nm = len(more)
    steps = N_CHIP * N_PIECE
    near0, far0 = N_PIECE, 3 * N_PIECE

    def piece_at(step, own):
        k = step - near0
        near, far = (step >= near0) & (step < far0), step >= far0
        block = jnp.where(far, own ^ 3, jnp.where(near, own ^ jnp.where(lax.rem(k, 2) == 0, 2, 1), own))
        return block, jnp.where(far, step - far0, jnp.where(near, lax.div(k, 2), step))

    def body(*refs):
        b_ref, x_any, g_ref, win_in, cw_in = refs[:5]
        o_ref, hb_any, win_f, cw_out = refs[5 + nm:9 + nm]
        more_out = refs[9 + nm:9 + 2 * nm]
        hbuf, xbuf, wv, cw_s, cw_r, loc = refs[9 + 2 * nm:15 + 2 * nm]
        g_in = _Gather([win_f.at[:, p] for p in range(N_PIECE)], *refs[15 + 2 * nm:19 + 2 * nm])
        g_more = _Gather(more_out, *refs[19 + 2 * nm:23 + 2 * nm])
        s = pl.program_id(0)
        x, y, c, chips = _coords()
        b = 2 * x + y
        blks = [2 * chip[0] + chip[1] for chip in chips]

        def cw_cols(blk):
            return cw_out.at[:, pl.ds(blk * (D // N_CHIP), D // N_CHIP)]

        def cw_copy(k, blk):
            src = cw_in if blk is None else cw_cols(blk)
            return pltpu.make_async_remote_copy(src_ref=src, dst_ref=cw_cols(b if blk is None else blk), send_sem=cw_s.at[k],
                                                recv_sem=cw_r.at[k], device_id=(*chips[k], c), device_id_type=MESH)

        cw_local = pltpu.make_async_copy(cw_in, cw_cols(b), loc.at[1])
        hb_copy = pltpu.make_async_copy(hbuf, hb_any, loc.at[0])

        def load(step):
            slot = lax.rem(step, 2)
            block, piece = piece_at(step, b_ref[0])
            return pltpu.make_async_copy(win_f.at[block, piece], wv.at[slot], loc.at[2 + slot])

        def chunk(i):
            return pltpu.make_async_copy(x_any.at[pl.ds(i * tm, tm)], xbuf.at[i % 2], loc.at[4 + i % 2])

        def first():
            g_in.start()
            cw_local.start()
            for k in range(3):
                cw_copy(k, None).start()
            load(0).start()
            chunk(0).start()
            for i in range(ni):
                if i + 1 < ni:
                    chunk(i + 1).start()
                chunk(i).wait()
                h, _ = _rms(xbuf[i % 2], g_ref[...])
                hbuf[pl.ds(i * tm, tm), :] = h.astype(BF16)
            hb_copy.start()

        events = {step: [] for step in range(steps)}
        events[0].append(first)
        for p in range(N_PIECE):
            events[2 * p + 2].append(functools.partial(g_in.hop, [p]))
            events[near0 + 2 * p - 1].append(functools.partial(g_in.near_ready, [p]))
            events[far0 + p - 2].append(functools.partial(g_in.far, [p]))
            events[far0 + p - 1].append(functools.partial(g_in.far_ready, [p]))
        events[2 * N_PIECE + 1].append(g_more.start)
        for step, todo in events.items():
            if todo:
                @pl.when(s == step)
                def _(todo=todo):
                    for do in todo:
                        do()

        @pl.when(s + 1 < steps)
        def _():
            load(s + 1).start()

        load(s).wait()
        for i in range(ni):
            rows = pl.ds(i * tm, tm)
            o_ref[rows, :] = jnp.dot(hbuf[rows, :], wv[lax.rem(s, 2)], preferred_element_type=F32).astype(BF16)

        @pl.when(s == steps - 1)
        def _():
            g_more.hop()
            g_more.far()
            for k in range(3):
                cw_copy(k, blks[k]).wait_recv()
            for k in range(3):
                cw_copy(k, None).wait_send()
            cw_local.wait()
            hb_copy.wait()
            g_more.near_ready()
            g_more.far_ready()
            g_in.drain()
            g_more.drain()

    def out_col(s, b):
        block, piece = piece_at(s, b[0])
        return 0, block * N_PIECE + piece

    in_specs = [ANY, pl.BlockSpec((1, D), lambda s, b: (0, 0)), ANY, ANY] + [ANY] * nm
    out_specs = [pl.BlockSpec((t, IN_PIECE), out_col), ANY, ANY, ANY] + [ANY] * nm
    outs = pl.pallas_call(
        body, out_shape=[jax.ShapeDtypeStruct((t, IN_DIM), BF16), jax.ShapeDtypeStruct((t, D), BF16),
                         jax.ShapeDtypeStruct(win_own.shape, BF16), jax.ShapeDtypeStruct((8, D), F32)]
        + [jax.ShapeDtypeStruct(f.shape, f.dtype) for f in more],
        grid_spec=pltpu.PrefetchScalarGridSpec(
            num_scalar_prefetch=1, grid=(steps,), in_specs=in_specs, out_specs=out_specs,
            scratch_shapes=[pltpu.VMEM((t, D), BF16), pltpu.VMEM((2, tm, D), F32), pltpu.VMEM((2, D, IN_PIECE), BF16)]
            + [pltpu.SemaphoreType.DMA((3,))] * 2 + [pltpu.SemaphoreType.DMA((6,))]
            + _gather_sems(N_PIECE) + _gather_sems(nm)),
        input_output_aliases={3: 2, **{5 + w: 4 + w for w in range(nm)}},
        compiler_params=_cp(("arbitrary",), VMEM_LIMIT), name="proj_gather")(b_idx, x, g, win_own, cw8s, *more)
    return outs[0], outs[1], outs[2], outs[3], outs[4:]


class _Gather:
    def __init__(self, outs, ici_s, ici_r, d2d_s, d2d_r):
        x, y, c, _ = _coords()
        self.outs, self.c = outs, c
        self.sems = ici_s, ici_r, d2d_s, d2d_r
        self.b, self.bx, self.by, self.bd = 2 * x + y, 2 * (1 - x) + y, 2 * x + (1 - y), 2 * (1 - x) + (1 - y)
        self.xn, self.yn, self.sib = (1 - x, y, c), (x, 1 - y, c), (x, y, 1 - c)

    def piece(self, w, blk, hc, quarter=None):
        hr = self.outs[w].shape[1] // 2
        if quarter is None:
            return self.outs[w].at[blk, pl.ds(hc * hr, hr)]
        return self.outs[w].at[blk, pl.ds(hc * hr + quarter * (hr // 2), hr // 2)]

    def ici(self, w, k, ref, to):
        return pltpu.make_async_remote_copy(src_ref=ref, dst_ref=ref, send_sem=self.sems[0].at[w, k],
                                            recv_sem=self.sems[1].at[w, k], device_id=to, device_id_type=MESH)

    def d2d(self, w, k, ref):
        return pltpu.make_async_remote_copy(src_ref=ref, dst_ref=ref, send_sem=self.sems[2].at[w, k],
                                            recv_sem=self.sems[3].at[w, k], device_id=self.sib, device_id_type=MESH)

    def all(self):
        return range(len(self.outs))

    def start(self):
        for w in self.all():
            mine = self.piece(w, self.b, self.c)
            self.ici(w, 0, mine, self.xn).start()
            self.ici(w, 1, mine, self.yn).start()

    def hop(self, ws=None):
        c = self.c
        for w in ws or self.all():
            self.ici(w, 0, self.piece(w, self.bx, c), self.xn).wait_recv()
            self.ici(w, 1, self.piece(w, self.by, c), self.yn).wait_recv()
            self.ici(w, 2, self.piece(w, self.bx, c, 0), self.yn).start()
            self.ici(w, 3, self.piece(w, self.by, c, 1), self.xn).start()
            self.d2d(w, 0, self.piece(w, self.bx, c)).start()
            self.d2d(w, 1, self.piece(w, self.by, c)).start()

    def near_ready(self, ws=None):
        for w in ws or self.all():
            self.d2d(w, 0, self.piece(w, self.bx, 1 - self.c)).wait_recv()
            self.d2d(w, 1, self.piece(w, self.by, 1 - self.c)).wait_recv()

    def far(self, ws=None):
        c = self.c
        for w in ws or self.all():
            self.ici(w, 2, self.piece(w, self.bd, c, 0), self.yn).wait_recv()
            self.ici(w, 3, self.piece(w, self.bd, c, 1), self.xn).wait_recv()
            self.d2d(w, 2, self.piece(w, self.bd, c, 0)).start()
            self.d2d(w, 3, self.piece(w, self.bd, c, 1)).start()

    def far_ready(self, ws=None):
        for w in ws or self.all():
            self.d2d(w, 2, self.piece(w, self.bd, 1 - self.c, 0)).wait_recv()
            self.d2d(w, 3, self.piece(w, self.bd, 1 - self.c, 1)).wait_recv()

    def drain(self):
        c = self.c
        for w in self.all():
            mine = self.piece(w, self.b, c)
            self.ici(w, 0, mine, self.xn).wait_send()
            self.ici(w, 1, mine, self.yn).wait_send()
            self.ici(w, 2, self.piece(w, self.bx, c, 0), self.yn).wait_send()
            self.ici(w, 3, self.piece(w, self.by, c, 1), self.xn).wait_send()
            self.d2d(w, 0, self.piece(w, self.bx, c)).wait_send()
            self.d2d(w, 1, self.piece(w, self.by, c)).wait_send()
            self.d2d(w, 2, self.piece(w, self.bd, c, 0)).wait_send()
            self.d2d(w, 3, self.piece(w, self.bd, c, 1)).wait_send()


def _gather_sems(nw):
    return [pltpu.SemaphoreType.DMA((max(nw, 1), 4))] * 4


def _mixer_fwd(proj, cw8, lng, lnb, wc, bsb, fulls, tm=256):
    t = proj.shape[0]
    nt = t // tm
    nch = tm // CH
    nw = len(fulls)

    def body(*refs):
        p_ref, cw_ref, lng_ref, lnb_ref, wc_ref, bsb_ref = refs[:6]
        mix_ref = refs[6 + nw]
        w_outs = refs[7 + nw:7 + 2 * nw]
        prev_ref = refs[7 + 2 * nw]
        gather = _Gather(w_outs, *refs[8 + 2 * nw:])

        @pl.when(pl.program_id(0) == 0)
        def _():
            gather.start()
            prev_ref[...] = jnp.zeros_like(prev_ref)

        @pl.when(pl.program_id(0) == nt // 2)
        def _():
            gather.hop()

        @pl.when(pl.program_id(0) == nt - 1)
        def _():
            gather.far()

        rows = lax.broadcasted_iota(jnp.int32, (tm, CH), 0)
        for s in range(HEADS):
            cs = pl.ds(CH * s, CH)

            def slab(k):
                return p_ref[:, pl.ds(k * SLAB + CH * s, CH)].astype(F32)

            gb, gc, xa, za = slab(0), slab(1), slab(2), slab(3)
            cx = gc * xa
            p6 = jnp.broadcast_to(prev_ref[6:7, cs], (tm, CH))
            p7 = jnp.broadcast_to(prev_ref[7:8, cs], (tm, CH))
            c1 = jnp.where(rows == 0, p7, pltpu.roll(cx, 1, 0))
            c2 = jnp.where(rows == 0, p6, jnp.where(rows == 1, p7, pltpu.roll(cx, 2, 0)))
            prev_ref[:, cs] = cx[tm - 8:, :]
            cv = cw_ref[0:1, cs] * c2 + cw_ref[1:2, cs] * c1 + cw_ref[2:3, cs] * cx
            mix_ref[:, cs] = (gb * cv * (za * _sigmoid(za))).astype(BF16)

            u, v, zb = slab(4), slab(5), slab(6)
            ug, vg = _gelu(u), _gelu(v)
            dlt = vg - jnp.mean(vg, axis=-1, keepdims=True)
            vhat = dlt * lax.rsqrt(jnp.mean(dlt * dlt, axis=-1, keepdims=True) + EPS)
            vn = (vhat * lng_ref[:, cs] + lnb_ref[:, cs]).astype(BF16)
            gate = ug * (zb * _sigmoid(zb))
            for c in range(nch):
                rs = slice(CH * c, CH * (c + 1))
                sp = jnp.dot(wc_ref[s], vn[rs], preferred_element_type=F32) + bsb_ref[s]
                mix_ref[rs, pl.ds(SLAB + CH * s, CH)] = (gate[rs] * sp).astype(BF16)

        @pl.when(pl.program_id(0) == nt - 1)
        def _():
            gather.near_ready()
            gather.far_ready()
            gather.drain()

    sems = _gather_sems(nw)
    outs = pl.pallas_call(
        body, grid=(nt,),
        in_specs=[pl.BlockSpec((tm, IN_DIM), lambda i: (i, 0)), _full((8, D)), _full((1, D)), _full((1, D)),
                  _full((HEADS, CH, CH)), _full((HEADS, CH, CH))] + [ANY] * nw,
        out_specs=[pl.BlockSpec((tm, MIX), lambda i: (i, 0))] + [ANY] * nw,
        out_shape=[jax.ShapeDtypeStruct((t, MIX), BF16)] + [jax.ShapeDtypeStruct(f.shape, f.dtype) for f in fulls],
        input_output_aliases={6 + w: 1 + w for w in range(nw)},
        scratch_shapes=[pltpu.VMEM((8, D), F32)] + sems,
        compiler_params=_cp(("arbitrary",), VMEM_LIMIT), name="mixer_fwd")(proj, cw8, lng, lnb, wc, bsb, *fulls)
    return outs[0], outs[1:]


def _mem_fwd(mem, gm, wkv_f):
    n_mem = mem.shape[0]

    def body(mem_ref, gm_ref, w_ref, k_ref, v_ref):
        m, _ = _rms(mem_ref[...], gm_ref[...])
        mb = m.astype(BF16)
        for j in range(N_CHIP):
            dst = k_ref if j < 2 else v_ref
            dst[:, pl.ds(KV_BLK * (j % 2), KV_BLK)] = jnp.dot(mb, w_ref[j], preferred_element_type=F32).astype(BF16)

    return pl.pallas_call(
        body, out_shape=[jax.ShapeDtypeStruct((n_mem, D), BF16), jax.ShapeDtypeStruct((n_mem, D), BF16)],
        compiler_params=_cp(None, VMEM_LIMIT), name="mem_fwd")(mem, gm, wkv_f)


def _tail(x, tgt, mixin, wout, wq, wxo, k, v, g2, g3, tm=512, sub=512):
    t = x.shape[0]
    n_mem = k.shape[0]
    scale = 1.0 / math.sqrt(XD)

    def body(x_ref, tgt_ref, mix_ref, wout_ref, wq_ref, wxo_ref, k_ref, v_ref, g2_ref, g3_ref,
             loss_ref, dmix_ref, dx1b_ref, h2_ref, dq_ref, o_ref, dx2b_ref, dk_ref, dv_ref, dg2_ref, dg3_ref):
        @pl.when(pl.program_id(0) == 0)
        def _():
            loss_ref[...] = jnp.zeros_like(loss_ref)
            dk_ref[...] = jnp.zeros_like(dk_ref)
            dv_ref[...] = jnp.zeros_like(dv_ref)
            dg2_ref[...] = jnp.zeros_like(dg2_ref)
            dg3_ref[...] = jnp.zeros_like(dg3_ref)

        g2, g3 = g2_ref[...], g3_ref[...]
        for sb in range(tm // sub):
            rs = pl.ds(sub * sb, sub)
            x1 = x_ref[rs, :] + jnp.dot(mix_ref[rs, :], wout_ref[...], preferred_element_type=F32)
            h2, r2 = _rms(x1, g2)
            h2b = h2.astype(BF16)
            h2_ref[rs, :] = h2b
            q = jnp.dot(h2b, wq_ref[...], preferred_element_type=F32).astype(BF16)
            probs, outs = [], []
            for hd in range(XH):
                hs = pl.ds(XD * hd, XD)
                s = _bdot_nt(q[:, XD * hd:XD * (hd + 1)], k_ref[:, hs]) * scale
                e = jnp.exp(s - jnp.max(s, axis=-1, keepdims=True))
                p = e / jnp.sum(e, axis=-1, keepdims=True)
                probs.append(p)
                outs.append(_bdot(p, v_ref[:, hs]))
            ob = jnp.concatenate(outs, axis=-1).astype(BF16)
            o_ref[rs, :] = ob
            x2 = x1 + jnp.dot(ob, wxo_ref[...], preferred_element_type=F32)
            y, r3 = _rms(x2, g3)
            diff = y - tgt_ref[rs, :]
            row_loss = jnp.sum(diff * diff, axis=-1, keepdims=True)
            loss_ref[...] += jnp.broadcast_to(jnp.sum(row_loss, axis=0, keepdims=True) * (0.5 / D), loss_ref.shape)

            dx2, dg3 = _rms_bwd(diff * (1.0 / D), x2, r3, g3)
            dg3_ref[...] += dg3
            dx2b = dx2.astype(BF16)
            dx2b_ref[rs, :] = dx2b
            do = _bdot_nt(dx2b, wxo_ref[...])
            dqs = []
            for hd in range(XH):
                hs = pl.ds(XD * hd, XD)
                p = probs[hd]
                do_h = do[:, XD * hd:XD * (hd + 1)]
                dv_ref[:, hs] += _bdot_tn(p, do_h)
                dp = _bdot_nt(do_h, v_ref[:, hs])
                ds = p * (dp - jnp.sum(dp * p, axis=-1, keepdims=True))
                dqs.append(_bdot(ds, k_ref[:, hs]) * scale)
                dk_ref[:, hs] += _bdot_tn(ds, q[:, XD * hd:XD * (hd + 1)]) * scale
            dq = jnp.concatenate(dqs, axis=-1).astype(BF16)
            dq_ref[rs, :] = dq
            dx1n, dg2 = _rms_bwd(_bdot_nt(dq, wq_ref[...]), x1, r2, g2)
            dg2_ref[...] += dg2
            dx1b = (dx2 + dx1n).astype(BF16)
            dx1b_ref[rs, :] = dx1b
            dmix_ref[rs, :] = _bdot_nt(dx1b, wout_ref[...]).astype(BF16)

    tok = lambda w: pl.BlockSpec((tm, w), lambda i: (i, 0))
    return pl.pallas_call(
        body, grid=(t // tm,),
        in_specs=[tok(D), tok(D), tok(MIX), _full((MIX, D), 1), _full((D, D), 1), _full((D, D), 1),
                  _full((n_mem, D), 1), _full((n_mem, D), 1), _full((1, D)), _full((1, D))],
        out_specs=[_full((1, D)), tok(MIX), tok(D), tok(D), tok(D), tok(D), tok(D),
                   _full((n_mem, D)), _full((n_mem, D)), _full((1, D)), _full((1, D))],
        out_shape=[jax.ShapeDtypeStruct((1, D), F32), jax.ShapeDtypeStruct((t, MIX), BF16),
                   jax.ShapeDtypeStruct((t, D), BF16),
                   jax.ShapeDtypeStruct((t, D), BF16), jax.ShapeDtypeStruct((t, D), BF16),
                   jax.ShapeDtypeStruct((t, D), BF16), jax.ShapeDtypeStruct((t, D), BF16),
                   jax.ShapeDtypeStruct((n_mem, D), F32), jax.ShapeDtypeStruct((n_mem, D), F32),
                   jax.ShapeDtypeStruct((1, D), F32), jax.ShapeDtypeStruct((1, D), F32)],
        compiler_params=_cp(("arbitrary",), VMEM_LIMIT), name="tail")(x, tgt, mixin, wout, wq, wxo, k, v, g2, g3)


def _mem_bwd(mem, gm, dk, dv, wkv_f):
    def body(mem_ref, gm_ref, dk_ref, dv_ref, w_ref, dw_ref, dwb_ref, dgm_ref):
        mem_v = mem_ref[...]
        m, rm = _rms(mem_v, gm_ref[...])
        mb = m.astype(BF16)
        dm = jnp.zeros_like(mem_v)
        for j in range(N_CHIP):
            src = dk_ref if j < 2 else dv_ref
            dkv = src[:, pl.ds(KV_BLK * (j % 2), KV_BLK)].astype(BF16)
            dw = _bdot_tn(mb, dkv)
            dw_ref[j] = dw
            dwb_ref[j] = dw.astype(BF16)
            dm = dm + _bdot_nt(dkv, w_ref[j])
        dgm_ref[...] = jnp.sum(dm * mem_v * rm, axis=0, keepdims=True)

    return pl.pallas_call(
        body, out_shape=[jax.ShapeDtypeStruct((N_CHIP, D, KV_BLK), F32), jax.ShapeDtypeStruct((N_CHIP, D, KV_BLK), BF16),
                         jax.ShapeDtypeStruct((1, D), F32)],
        compiler_params=_cp(None, VMEM_LIMIT), name="mem_bwd")(mem, gm, dk, dv, wkv_f)


def _mixer_bwd(proj, dmix, cw8, lng, lnb, wc, wct, bsb, win_f, x, dx1, g1, rows123, row7, tm=256):
    t = proj.shape[0]
    nt = t // tm
    nch = tm // CH
    hb = 16
    pair = 2 * CH
    assert pair == IN_PIECE

    def body(p_ref, pgc_ref, pxa_ref, dm_ref, cw_ref, lng_ref, lnb_ref, wc_ref, wct_ref, bsb_ref, w_ref, x_ref,
             dx1_ref, g1_ref, r1_ref, r2_ref, r3_ref, r7_ref, dp_ref, sv_ref, dwc_ref, gx_ref,
             next_ref, dh_ref):
        i = pl.program_id(0)
        dg1_ref, dlng_ref, dlnb_ref, dbs_ref = (sv_ref.at[pl.ds(r, 1)] for r in (0, 4, 5, 6))
        dcw_ref = sv_ref.at[pl.ds(8, 8)]

        @pl.when(i == 0)
        def _():
            next_ref[...] = jnp.zeros_like(next_ref)
            sv_ref[...] = jnp.zeros_like(sv_ref)
            dwc_ref[...] = jnp.zeros_like(dwc_ref)
            for r, ref in ((1, r1_ref), (2, r2_ref), (3, r3_ref), (7, r7_ref)):
                sv_ref[r:r + 1, :] = ref[...]

        first_tile = i == nt - 1
        rows = lax.broadcasted_iota(jnp.int32, (tm, CH), 0)
        ones8 = jnp.ones((8, CH), BF16)
        for s in range(HEADS):
            cs = pl.ds(CH * s, CH)

            def slab(k):
                return p_ref[:, pl.ds(k * SLAB + CH * s, CH)].astype(F32)

            gb, gc, xa, za = slab(0), slab(1), slab(2), slab(3)
            da = dm_ref[:, cs].astype(F32)
            cx = gc * xa
            cxp = pgc_ref[:, cs].astype(F32) * pxa_ref[:, cs].astype(F32)
            cxp = jnp.where(first_tile, jnp.zeros_like(cxp), cxp)
            p6 = jnp.broadcast_to(cxp[hb - 2:hb - 1, :], (tm, CH))
            p7 = jnp.broadcast_to(cxp[hb - 1:hb, :], (tm, CH))
            c1 = jnp.where(rows == 0, p7, pltpu.roll(cx, 1, 0))
            c2 = jnp.where(rows == 0, p6, jnp.where(rows == 1, p7, pltpu.roll(cx, 2, 0)))
            w0, w1, w2 = cw_ref[0:1, cs], cw_ref[1:2, cs], cw_ref[2:3, cs]
            cv = w0 * c2 + w1 * c1 + w2 * cx
            sg = _sigmoid(za)
            sa = za * sg
            dcv = da * gb * sa
            dp_ref[:, pl.ds(0 * SLAB + CH * s, CH)] = (da * cv * sa).astype(BF16)
            dp_ref[:, pl.ds(3 * SLAB + CH * s, CH)] = (da * gb * cv * (sg * (1.0 + za * (1.0 - sg)))).astype(BF16)
            n0 = jnp.broadcast_to(next_ref[0:1, cs], (tm, CH))
            n1 = jnp.broadcast_to(next_ref[1:2, cs], (tm, CH))
            u1 = jnp.where(rows == tm - 1, n0, pltpu.roll(dcv, tm - 1, 0))
            u2 = jnp.where(rows == tm - 2, n0, jnp.where(rows == tm - 1, n1, pltpu.roll(dcv, tm - 2, 0)))
            next_ref[:, cs] = dcv[0:8, :]
            dcx = w2 * dcv + w1 * u1 + w0 * u2
            dp_ref[:, pl.ds(1 * SLAB + CH * s, CH)] = (dcx * xa).astype(BF16)
            dp_ref[:, pl.ds(2 * SLAB + CH * s, CH)] = (dcx * gc).astype(BF16)
            dcw_ref[0:1, cs] += jnp.sum(dcv * c2, axis=0, keepdims=True)
            dcw_ref[1:2, cs] += jnp.sum(dcv * c1, axis=0, keepdims=True)
            dcw_ref[2:3, cs] += jnp.sum(dcv * cx, axis=0, keepdims=True)

            u, v, zb = slab(4), slab(5), slab(6)
            db = dm_ref[:, pl.ds(SLAB + CH * s, CH)].astype(F32)
            ug, ugrad = _gelu_parts(u)
            vg, vgrad = _gelu_parts(v)
            dlt = vg - jnp.mean(vg, axis=-1, keepdims=True)
            rstd = lax.rsqrt(jnp.mean(dlt * dlt, axis=-1, keepdims=True) + EPS)
            vhat = dlt * rstd
            lg = lng_ref[:, cs]
            vn = (vhat * lg + lnb_ref[:, cs]).astype(BF16)
            sgb = _sigmoid(zb)
            szb = zb * sgb
            sps, dvns = [], []
            dbs = jnp.zeros((8, CH), F32)
            dwc = jnp.zeros((CH, CH), F32)
            for c in range(nch):
                rs = slice(CH * c, CH * (c + 1))
                sp = jnp.dot(wc_ref[s], vn[rs], preferred_element_type=F32) + bsb_ref[s]
                dsp = (db[rs] * ug[rs] * szb[rs]).astype(BF16)
                dbs = dbs + lax.dot_general(ones8, dsp, (((1,), (1,)), ((), ())), preferred_element_type=F32)
                dwc = dwc + lax.dot_general(dsp, vn[rs], (((1,), (1,)), ((), ())), preferred_element_type=F32)
                dvns.append(jnp.dot(wct_ref[s], dsp, preferred_element_type=F32))
                sps.append(sp)
            sp = jnp.concatenate(sps, axis=0)
            dvn = jnp.concatenate(dvns, axis=0)
            dbs_ref[:, cs] += dbs[0:1]
            dwc_ref[s] += dwc
            dlng_ref[:, cs] += jnp.sum(dvn * vhat, axis=0, keepdims=True)
            dlnb_ref[:, cs] += jnp.sum(dvn, axis=0, keepdims=True)
            dvhat = dvn * lg
            dvg = rstd * (dvhat - jnp.mean(dvhat, axis=-1, keepdims=True)
                          - vhat * jnp.mean(dvhat * vhat, axis=-1, keepdims=True))
            dp_ref[:, pl.ds(4 * SLAB + CH * s, CH)] = (db * sp * szb * ugrad).astype(BF16)
            dp_ref[:, pl.ds(5 * SLAB + CH * s, CH)] = (dvg * vgrad).astype(BF16)
            dp_ref[:, pl.ds(6 * SLAB + CH * s, CH)] = (db * ug * sp * (sgb * (1.0 + zb * (1.0 - sgb)))).astype(BF16)

            if s % 2 == 1:
                part = None
                for k in range(N_SLAB):
                    col = k * SLAB + pair * (s // 2)
                    blk, off = divmod(col, IN_BLK)
                    term = lax.dot_general(dp_ref[:, pl.ds(col, pair)], w_ref[blk, off // IN_PIECE],
                                           (((1,), (1,)), ((), ())), preferred_element_type=F32)
                    part = term if part is None else part + term
                if s == 1:
                    dh_ref[...] = part
                else:
                    dh_ref[...] += part

        xv = x_ref[...]
        r = lax.rsqrt(jnp.mean(xv * xv, axis=-1, keepdims=True) + EPS)
        dxn, dg = _rms_bwd(dh_ref[...], xv, r, g1_ref[...])
        gx_ref[...] = dx1_ref[...].astype(F32) + dxn
        dg1_ref[...] += dg

        @pl.when(i == nt - 1)
        def _():
            tril = lax.broadcasted_iota(jnp.int32, (CH, CH), 0) >= lax.broadcasted_iota(jnp.int32, (CH, CH), 1)
            for s in range(HEADS):
                dwc_ref[s] = jnp.where(tril, dwc_ref[s], 0.0)

    rev = lambda i: nt - 1 - i
    halo = lambda col: pl.BlockSpec((hb, SLAB), lambda i: (jnp.maximum(rev(i) * (tm // hb) - 1, 0), col))
    tok = lambda w: pl.BlockSpec((tm, w), lambda i: (rev(i), 0))
    return pl.pallas_call(
        body, grid=(nt,),
        in_specs=[tok(IN_DIM), halo(1), halo(2), tok(MIX), _full((8, D)), _full((1, D)), _full((1, D)),
                  _full((HEADS, CH, CH)), _full((HEADS, CH, CH)), _full((HEADS, CH, CH)),
                  _full((N_CHIP, N_PIECE, D, IN_PIECE), 1), tok(D), tok(D)] + [_full((1, D))] * 5,
        out_specs=[tok(IN_DIM), _full((16, D)), _full((HEADS, CH, CH)), tok(D)],
        out_shape=[jax.ShapeDtypeStruct((t, IN_DIM), BF16), jax.ShapeDtypeStruct((16, D), F32),
                   jax.ShapeDtypeStruct((HEADS, CH, CH), F32), jax.ShapeDtypeStruct((t, D), F32)],
        scratch_shapes=[pltpu.VMEM((8, D), F32), pltpu.VMEM((tm, D), F32)],
        compiler_params=_cp(("arbitrary",), VMEM_LIMIT), name="mixer_bwd")(
            proj, proj, proj, dmix, cw8, lng, lnb, wc, wct, bsb, win_f, x, dx1, g1, *rows123, row7)


def _grad_matmul(a, b, after, *, name, tk=1024):
    t, m = a.shape
    n = b.shape[1]
    nk = t // tk

    def body(a_ref, b_ref, after_ref, o_ref, ob_ref):
        kk = pl.program_id(0)
        part = lax.dot_general(a_ref[...], b_ref[...], (((0,), (0,)), ((), ())), preferred_element_type=F32)

        @pl.when(kk == 0)
        def _():
            o_ref[...] = part

        @pl.when(kk > 0)
        def _():
            o_ref[...] += part

        @pl.when(kk == nk - 1)
        def _():
            ob_ref[...] = o_ref[...].astype(BF16)

    o_spec = pl.BlockSpec((m, n), lambda k: (0, 0))
    o32, o16 = pl.pallas_call(
        body, grid=(nk,), in_specs=[pl.BlockSpec((tk, m), lambda k: (k, 0)), pl.BlockSpec((tk, n), lambda k: (k, 0)), ANY],
        out_specs=[o_spec, o_spec], out_shape=[jax.ShapeDtypeStruct((m, n), F32), jax.ShapeDtypeStruct((m, n), BF16)],
        compiler_params=_cp(("arbitrary",), VMEM_LIMIT), name=name)(a, b, after)
    return o32.reshape(N_CHIP, m // N_CHIP, n), o16.reshape(N_CHIP, m // N_CHIP, n)


def _coords():
    x, y, c = lax.axis_index("x"), lax.axis_index("y"), lax.axis_index("c")
    chips = [(1 - x, y), (x, 1 - y), (1 - x, 1 - y)]
    return x, y, c, chips


def _pair_reduce(c_idx, grads, grads_b, smalls, name):
    ng, ns = len(grads), len(smalls)
    halves = [g.shape[1] // 2 for g in grads]

    def body(c_ref, *refs):
        g_in, gb_any = refs[:ng], refs[ng:2 * ng]
        s_own, s_any = refs[2 * ng:2 * ng + ns], refs[2 * ng + ns:2 * ng + 2 * ns]
        o = refs[2 * ng + 2 * ns:4 * ng + 3 * ns]
        lands = refs[4 * ng + 3 * ns:5 * ng + 4 * ns]
        send, recv = refs[5 * ng + 4 * ns:]
        x, y, c, _ = _coords()
        j = pl.program_id(0)

        def big(i, blk):
            return pltpu.make_async_remote_copy(
                src_ref=gb_any[i].at[blk, pl.ds((1 - c) * halves[i], halves[i])], dst_ref=lands[i].at[blk],
                send_sem=send.at[i, blk], recv_sem=recv.at[i, blk], device_id=(x, y, 1 - c), device_id_type=MESH)

        def small(i):
            return pltpu.make_async_remote_copy(
                src_ref=s_any[i].at[1 - c], dst_ref=lands[ng + i],
                send_sem=send.at[ng + i, 0], recv_sem=recv.at[ng + i, 0], device_id=(x, y, 1 - c), device_id_type=MESH)

        @pl.when(j == 0)
        def _():
            for blk in range(N_CHIP):
                for i in range(ng):
                    big(i, blk).start()
            for i in range(ns):
                small(i).start()

        for i in range(ng):
            big(i, j).wait_recv()
            tot = g_in[i][...] + lands[i][j].astype(F32)
            o[i][...] = tot
            o[ng + i][...] = tot.astype(BF16)

        @pl.when(j == N_CHIP - 1)
        def _():
            for i in range(ns):
                small(i).wait_recv()
                o[2 * ng + i][...] = s_own[i][...] + lands[ng + i][...]
                small(i).wait_send()
            for blk in range(N_CHIP):
                for i in range(ng):
                    big(i, blk).wait_send()

    in_specs = [pl.BlockSpec((None, None, halves[i], g.shape[2]), lambda b, c: (b, c[0], 0, 0)) for i, g in enumerate(grads)]
    in_specs += [ANY] * ng
    in_specs += [pl.BlockSpec((None, s.shape[0] // 2, s.shape[1]), lambda b, c: (c[0], 0, 0)) for s in smalls]
    in_specs += [ANY] * ns
    blk = [pl.BlockSpec((None, halves[i], g.shape[2]), lambda b, c: (b, 0, 0)) for i, g in enumerate(grads)]
    out_specs = blk + blk + [pl.BlockSpec((s.shape[0] // 2, s.shape[1]), lambda b, c: (0, 0)) for s in smalls]
    out_shape = [jax.ShapeDtypeStruct((N_CHIP, halves[i], g.shape[2]), F32) for i, g in enumerate(grads)]
    out_shape += [jax.ShapeDtypeStruct((N_CHIP, halves[i], g.shape[2]), BF16) for i, g in enumerate(grads)]
    out_shape += [jax.ShapeDtypeStruct((s.shape[0] // 2, s.shape[1]), F32) for s in smalls]
    scratch = [pltpu.VMEM((N_CHIP, halves[i], g.shape[2]), BF16) for i, g in enumerate(grads)]
    scratch += [pltpu.VMEM((s.shape[0] // 2, s.shape[1]), F32) for s in smalls]
    scratch += [pltpu.SemaphoreType.DMA((ng + ns, N_CHIP)), pltpu.SemaphoreType.DMA((ng + ns, N_CHIP))]
    grads4 = [g.reshape(N_CHIP, 2, halves[i], g.shape[2]) for i, g in enumerate(grads)]
    smalls3 = [s.reshape(2, s.shape[0] // 2, s.shape[1]) for s in smalls]
    return pl.pallas_call(
        body, out_shape=out_shape,
        grid_spec=pltpu.PrefetchScalarGridSpec(num_scalar_prefetch=1, grid=(N_CHIP,), in_specs=in_specs,
                                               out_specs=out_specs, scratch_shapes=scratch),
        compiler_params=_cp(("arbitrary",), VMEM_LIMIT), name=name)(c_idx, *grads4, *grads_b, *smalls3, *smalls3)


def _grad_matmul_pair(c_idx, a, b, smalls, small_dtypes, after, *, name, tk=2048):
    t, m = a.shape
    bn = b.shape[1] // N_CHIP
    nk = t // tk
    hr = m // 2
    ns = len(smalls)

    def body(c_ref, a_ref, b_ref, *refs):
        s_own, s_any = refs[:ns], refs[ns:2 * ns]
        o32, o16 = refs[2 * ns + 1], refs[2 * ns + 2]
        o_small = refs[2 * ns + 3:3 * ns + 3]
        acc, tb, land, st16 = refs[3 * ns + 3:3 * ns + 7]
        s_land, s_stage = refs[3 * ns + 7:4 * ns + 7], refs[4 * ns + 7:5 * ns + 7]
        send, recv, loc = refs[5 * ns + 7:]
        x, y, c, _ = _coords()
        sibling = dict(device_id=(x, y, 1 - c), device_id_type=MESH)
        j, kk = pl.program_id(0), pl.program_id(1)
        mine = pl.ds(pl.multiple_of(c * hr, hr), hr)
        theirs = pl.ds(pl.multiple_of((1 - c) * hr, hr), hr)

        def to_sibling(blk):
            return pltpu.make_async_remote_copy(src_ref=tb, dst_ref=land.at[blk], send_sem=send.at[blk],
                                                recv_sem=recv.at[blk], **sibling)

        def small(i):
            return pltpu.make_async_remote_copy(src_ref=s_any[i].at[1 - c], dst_ref=s_land[i], send_sem=send.at[N_CHIP + i],
                                                recv_sem=recv.at[N_CHIP + i], **sibling)

        def written(blk):
            return (pltpu.make_async_copy(acc.at[blk % 2, mine], o32.at[blk], loc.at[0]),
                    pltpu.make_async_copy(st16, o16.at[blk], loc.at[1]))

        def finish(blk):
            to_sibling(blk).wait_recv()

            @pl.when(blk > 0)
            def _():
                for cp in written(blk - 1):
                    cp.wait()

            tot = acc[blk % 2, mine, :] + land[blk].astype(F32)
            acc[blk % 2, mine, :] = tot
            st16[...] = tot.astype(BF16)
            for cp in written(blk):
                cp.start()

        def small_out(i):
            return pltpu.make_async_copy(s_stage[i], o_small[i], loc.at[2 + i])

        @pl.when((j == 0) & (kk == 0))
        def _():
            for i in range(ns):
                small(i).start()

        @pl.when((j == 1) & (kk == 0))
        def _():
            for i in range(ns):
                small(i).wait_recv()
                s_stage[i][...] = (s_own[i][...] + s_land[i][...]).astype(small_dtypes[i])
                small_out(i).start()

        @pl.when((j > 0) & (kk == 0))
        def _():
            finish(j - 1)

        part = lax.dot_general(a_ref[...], b_ref[...], (((0,), (0,)), ((), ())), preferred_element_type=F32)
        slot = lax.rem(j, 2)

        @pl.when(kk == 0)
        def _():
            acc[slot] = part

        @pl.when(kk > 0)
        def _():
            acc[slot] += part

        @pl.when(kk == nk - 1)
        def _():
            @pl.when(j > 0)
            def _():
                to_sibling(j - 1).wait_send()

            tb[...] = acc[slot, theirs, :].astype(BF16)
            to_sibling(j).start()

        @pl.when((j == N_CHIP - 1) & (kk == nk - 1))
        def _():
            finish(j)
            for i in range(ns):
                small_out(i).wait()
                small(i).wait_send()
            for cp in written(j):
                cp.wait()
            to_sibling(j).wait_send()

    halves = [(s.shape[0] // 2, s.shape[1]) for s in smalls]
    in_specs = [pl.BlockSpec((tk, m), lambda j, k, c: (k, 0)), pl.BlockSpec((tk, bn), lambda j, k, c: (k, j))]
    in_specs += [pl.BlockSpec((None,) + h, lambda j, k, c: (c[0], 0, 0)) for h in halves] + [ANY] * ns + [ANY]
    out_shape = [jax.ShapeDtypeStruct((N_CHIP, hr, bn), F32), jax.ShapeDtypeStruct((N_CHIP, hr, bn), BF16)]
    out_shape += [pltpu.HBM(h, dt) for h, dt in zip(halves, small_dtypes)]
    scratch = [pltpu.VMEM((2, m, bn), F32), pltpu.VMEM((hr, bn), BF16),
               pltpu.VMEM((N_CHIP, hr, bn), BF16), pltpu.VMEM((hr, bn), BF16)]
    scratch += [pltpu.VMEM(h, F32) for h in halves] + [pltpu.VMEM(h, dt) for h, dt in zip(halves, small_dtypes)]
    scratch += [pltpu.SemaphoreType.DMA((N_CHIP + ns,)), pltpu.SemaphoreType.DMA((N_CHIP + ns,)),
                pltpu.SemaphoreType.DMA((2 + ns,))]
    smalls3 = [s.reshape((2,) + h) for s, h in zip(smalls, halves)]
    outs = pl.pallas_call(
        body, out_shape=out_shape,
        grid_spec=pltpu.PrefetchScalarGridSpec(num_scalar_prefetch=1, grid=(N_CHIP, nk), in_specs=in_specs,
                                               out_specs=[ANY, ANY] + [_HBM] * ns, scratch_shapes=scratch),
        compiler_params=_cp(("arbitrary", "arbitrary"), VMEM_LIMIT), name=name)(c_idx, a, b, *smalls3, *smalls3, after)
    return outs[0], outs[1], list(outs[2:])


_HBM = pl.BlockSpec(memory_space=pltpu.HBM)
_SEM = pl.BlockSpec(memory_space=pltpu.SEMAPHORE)


def _split_copies(ins, lands, ng, send, recv, arriving):
    x, y, c, chips = _coords()
    b = 2 * x + y
    copies = []
    for i in range(len(ins)):
        for k in range(3):
            blk = 2 * chips[k][0] + chips[k][1]
            src, dst, got = (ins[i].at[blk], lands[i].at[k], lands[i].at[k]) if i < ng else (ins[i], lands[i].at[b], lands[i].at[blk])
            sems = dict(send_sem=send.at[3 * i + k], recv_sem=recv.at[3 * i + k], device_id=(*chips[k], c), device_id_type=MESH)
            if arriving:
                copies.append(pltpu.make_async_remote_copy(src_ref=got, dst_ref=got, **sems))
            else:
                copies.append(pltpu.make_async_remote_copy(src_ref=src, dst_ref=dst, **sems))
    return copies


def _exchange_begin(sums_b, smalls, name):
    ng, n = len(sums_b), len(sums_b) + len(smalls)
    srcs = list(sums_b) + list(smalls)
    lands = [lax.empty((3,) + g.shape[1:], g.dtype) for g in sums_b] + [lax.empty((N_CHIP,) + s.shape, s.dtype) for s in smalls]

    def body(*refs):
        ins, land_refs = refs[:n], refs[n:2 * n]
        send, recv = refs[2 * n], refs[2 * n + 1]
        token = refs[4 * n + 2]
        for cp in _split_copies(ins, land_refs, ng, send, recv, False):
            cp.start()
        token[...] = jnp.zeros_like(token)

    hbm = lambda a: pltpu.HBM(a.shape, a.dtype)
    outs = pl.pallas_call(
        body, name=name,
        out_shape=(pltpu.SemaphoreType.DMA((3 * n,)), pltpu.SemaphoreType.DMA((3 * n,)), *[hbm(a) for a in srcs + lands],
                   jax.ShapeDtypeStruct((8, 128), F32)),
        in_specs=[_HBM] * (2 * n), out_specs=(_SEM, _SEM, *[_HBM] * (2 * n), pl.BlockSpec(memory_space=pltpu.VMEM)),
        input_output_aliases={i: 2 + i for i in range(2 * n)},
        compiler_params=pltpu.CompilerParams(has_side_effects=pltpu.SideEffectType.DATAFLOW_SIDE_EFFECTING),
    )(*[pltpu.with_memory_space_constraint(a, pltpu.HBM) for a in srcs + lands])
    return outs[0], outs[1], list(outs[2:2 + n]), list(outs[2 + n:2 + 2 * n]), outs[2 + 2 * n]


def _exchange_end(send, recv, srcs, lands, ng, which, after, name):
    n = len(srcs)
    after = list(after)

    def body(*refs):
        ins, land_refs = refs[:n], refs[n:2 * n]
        send_ref, recv_ref = refs[2 * n], refs[2 * n + 1]
        outgoing = _split_copies(ins, land_refs, ng, send_ref, recv_ref, False)
        arriving = _split_copies(ins, land_refs, ng, send_ref, recv_ref, True)
        for i in which:
            for cp in outgoing[3 * i:3 * i + 3]:
                cp.wait_send()
        for i in which:
            for cp in arriving[3 * i:3 * i + 3]:
                cp.wait_recv()

    hbm = lambda a: pltpu.HBM(a.shape, a.dtype)
    outs = pl.pallas_call(
        body, name=name, out_shape=tuple(hbm(a) for a in list(srcs) + list(lands)),
        in_specs=[_HBM] * (2 * n) + [_SEM, _SEM] + [ANY] * len(after), out_specs=tuple([_HBM] * (2 * n)),
        input_output_aliases={i: i for i in range(2 * n)},
        compiler_params=pltpu.CompilerParams(has_side_effects=pltpu.SideEffectType.DATAFLOW_SIDE_EFFECTING),
    )(*srcs, *lands, send, recv, *after)
    return list(outs[:n]), list(outs[n:])


def _chip_reduce(bc_idx, sums, recvd, smalls_slots, smalls_own, name, steps=4):
    ng, ns = len(sums), len(smalls_slots)
    n = ng + ns
    assert steps >= 2
    halves = [g.shape[1] for g in sums] + [s.shape[1] for s in smalls_slots]
    rows = [g.shape[1] // steps for g in sums]

    def body(bc_ref, *refs):
        own, rx = refs[:ng], refs[ng:2 * ng]
        sl = refs[2 * ng:2 * ng + ns]
        sl_own = refs[2 * ng + ns:2 * ng + 2 * ns]
        o = refs[2 * ng + 2 * ns:2 * ng + 2 * ns + n]
        tiles = refs[2 * ng + 2 * ns + n:2 * ng + 2 * ns + 2 * n]
        keep, send, recv = refs[2 * ng + 2 * ns + 2 * n:]
        x, y, c, _ = _coords()
        sibling = dict(device_id=(x, y, 1 - c), device_id_type=MESH)
        r = pl.program_id(0)

        def writes(i, step, slot):
            dst = o[i].at[pl.ds(c * halves[i] + step * rows[i], rows[i])]
            return (pltpu.make_async_copy(tiles[i].at[slot], dst, keep.at[i, slot]),
                    pltpu.make_async_remote_copy(src_ref=tiles[i].at[slot], dst_ref=dst, send_sem=send.at[i, slot],
                                                 recv_sem=recv.at[i, step], **sibling))

        def small_writes(i):
            dst = o[i].at[pl.ds(c * halves[i], halves[i])]
            return (pltpu.make_async_copy(tiles[i], dst, keep.at[i, 0]),
                    pltpu.make_async_remote_copy(src_ref=tiles[i], dst_ref=dst, send_sem=send.at[i, 0],
                                                 recv_sem=recv.at[i, 0], **sibling))

        def arriving(i, step, nrows):
            dst = o[i].at[pl.ds((1 - c) * halves[i] + step * nrows, nrows)]
            return pltpu.make_async_remote_copy(src_ref=dst, dst_ref=dst, send_sem=send.at[i, 0], recv_sem=recv.at[i, step],
                                                **sibling)

        def finish(step, slot):
            for i in range(ng):
                local, remote = writes(i, step, slot)
                local.wait()
                remote.wait_send()

        @pl.when(r >= 2)
        def _():
            finish(r - 2, r % 2)

        for i in range(ng):
            tot = own[i][...]
            for j in range(3):
                tot = tot + rx[i][j].astype(F32)
            tiles[i][r % 2] = tot
            for cp in writes(i, r, r % 2):
                cp.start()

        @pl.when(r == 0)
        def _():
            for i in range(ns):
                term = [jnp.where(bc_ref[0] == kk, sl_own[i][...], sl[i][kk]).astype(F32) for kk in range(N_CHIP)]
                tiles[ng + i][...] = ((term[0] + term[1]) + term[2]) + term[3]
                for cp in small_writes(ng + i):
                    cp.start()

        @pl.when(r == steps - 1)
        def _():
            finish(steps - 2, (steps - 2) % 2)
            finish(steps - 1, (steps - 1) % 2)
            for i in range(ns):
                local, remote = small_writes(ng + i)
                local.wait()
                remote.wait_send()
                arriving(ng + i, 0, halves[ng + i]).wait_recv()
            for i in range(ng):
                for step in range(steps):
                    arriving(i, step, rows[i]).wait_recv()

    in_specs = [pl.BlockSpec((None, rows[i], g.shape[2]), lambda r, bc: (bc[0], r, 0)) for i, g in enumerate(sums)]
    in_specs += [pl.BlockSpec((3, rows[i], g.shape[2]), lambda r, bc: (0, r, 0)) for i, g in enumerate(sums)]
    in_specs += [pl.BlockSpec(s.shape, lambda r, bc: (0, 0, 0)) for s in smalls_slots]
    in_specs += [pl.BlockSpec(s.shape[1:], lambda r, bc: (0, 0)) for s in smalls_slots]
    out_shape = [jax.ShapeDtypeStruct((2 * g.shape[1], g.shape[2]), F32) for g in sums]
    out_shape += [jax.ShapeDtypeStruct((2 * s.shape[1], s.shape[2]), F32) for s in smalls_slots]
    scratch = [pltpu.VMEM((2, rows[i], g.shape[2]), F32) for i, g in enumerate(sums)]
    scratch += [pltpu.VMEM(s.shape[1:], F32) for s in smalls_slots]
    scratch += [pltpu.SemaphoreType.DMA((n, 2)), pltpu.SemaphoreType.DMA((n, 2)), pltpu.SemaphoreType.DMA((n, steps))]
    return list(pl.pallas_call(
        body, out_shape=out_shape,
        grid_spec=pltpu.PrefetchScalarGridSpec(num_scalar_prefetch=1, grid=(steps,), in_specs=in_specs,
                                               out_specs=[ANY] * n, scratch_shapes=scratch),
        compiler_params=_cp(("arbitrary",), VMEM_LIMIT), name=name)(bc_idx, *sums, *recvd, *smalls_slots, *smalls_own))


def _adamw_math(w, g, m, v):
    m2 = ADAM_B1 * m + (1.0 - ADAM_B1) * g
    v2 = ADAM_B2 * v + (1.0 - ADAM_B2) * (g * g)
    m_hat = m2 / (1.0 - ADAM_B1 ** ADAM_STEP)
    v_hat = v2 / (1.0 - ADAM_B2 ** ADAM_STEP)
    delta = -ADAM_LR * (m_hat / (jnp.sqrt(v_hat) + ADAM_EPS) + ADAM_WD * w)
    return delta, m2, v2


def _adamw_big(ws, gs, ms, vs, name, steps=8):
    n = len(ws)

    def body(*refs):
        for i in range(n):
            w_ref, g_ref, m_ref, v_ref = (refs[k * n + i] for k in range(4))
            d_ref, m2_ref, v2_ref, g2_ref = (refs[(4 + k) * n + i] for k in range(4))
            gv = g_ref[...]
            d_ref[...], m2_ref[...], v2_ref[...] = _adamw_math(w_ref[...], gv, m_ref[...], v_ref[...])
            g2_ref[...] = gv

    specs = [pl.BlockSpec((w.shape[0] // steps, w.shape[1]), lambda i: (i, 0)) for w in ws]
    ahead = [pl.BlockSpec((w.shape[0] // steps, w.shape[1]), lambda i: (i, 0), pipeline_mode=pl.Buffered(3)) for w in ws]
    shapes = [jax.ShapeDtypeStruct(w.shape, F32) for w in ws]

    def streamed(*refs):
        pltpu.emit_pipeline(body, grid=(steps,), in_specs=ahead * 4, out_specs=specs * 4)(*refs)

    outs = pl.pallas_call(
        streamed, in_specs=[ANY] * (4 * n), out_specs=[ANY] * (4 * n), out_shape=shapes * 4,
        compiler_params=_cp(None, VMEM_LIMIT), name=name)(*ws, *gs, *ms, *vs)
    return [tuple(outs[k * n + i] for k in range(4)) for i in range(n)]


def _adamw_small(b_idx, sv, sw, vecs, conv, ws):
    nv = len(vecs)
    cols = conv[0].shape[1]

    def body(b_ref, sv_ref, sw_ref, *refs):
        ins, outs = refs[:3 * nv + 6], refs[3 * nv + 6:]
        for i in range(nv):
            g = sv_ref[i:i + 1, :]
            w_ref, m_ref, v_ref = ins[3 * i:3 * i + 3]
            d_ref, m2_ref, v2_ref, g_ref = outs[4 * i:4 * i + 4]
            d_ref[...], m2_ref[...], v2_ref[...] = _adamw_math(w_ref[...], g, m_ref[...], v_ref[...])
            g_ref[...] = g
        g = sv_ref[8:8 + conv[0].shape[0], pl.ds(pl.multiple_of(b_ref[0] * cols, cols), cols)]
        w_ref, m_ref, v_ref = ins[3 * nv:3 * nv + 3]
        d_ref, m2_ref, v2_ref, g_ref = outs[4 * nv:4 * nv + 4]
        d_ref[...], m2_ref[...], v2_ref[...] = _adamw_math(w_ref[...], g, m_ref[...], v_ref[...])
        g_ref[...] = g
        w_ref, m_ref, v_ref = ins[3 * nv + 3:]
        d_ref, m2_ref, v2_ref, g_ref, one_ref = outs[4 * nv + 4:]
        g = sw_ref[...]
        d_ref[...], m2_ref[...], v2_ref[...] = _adamw_math(w_ref[...], g, m_ref[...], v_ref[...])
        g_ref[...] = g
        one_ref[...] = sv_ref[nv:nv + 1, 0:1]

    flat = [a for grp in vecs for a in grp] + list(conv) + list(ws)
    out_shape = [jax.ShapeDtypeStruct(grp[0].shape, F32) for grp in list(vecs) + [conv, ws] for _ in range(4)]
    out_shape += [jax.ShapeDtypeStruct((1, 1), F32)]
    vmem = pl.BlockSpec(memory_space=pltpu.VMEM)
    outs = pl.pallas_call(
        body, out_shape=out_shape, in_specs=[pl.BlockSpec(memory_space=pltpu.SMEM)] + [vmem] * (2 + len(flat)),
        out_specs=[vmem] * len(out_shape), name="adamw_small")(b_idx, sv, sw, *flat)
    return [tuple(outs[4 * i:4 * i + 4]) for i in range(nv + 2)], outs[4 * nv + 8]


def kernel(x, mem, norm_mix_g, w_in, conv_w, gm_ln_g, gm_ln_b, gm_ws, gm_bs, w_out, norm_x_g, norm_mem_g, w_q, w_kv, w_xo, norm_final_g, loss_target, m_norm_mix_g, m_w_in, m_conv_w, m_gm_ln_g, m_gm_ln_b, m_gm_ws, m_gm_bs, m_w_out, m_norm_x_g, m_norm_mem_g, m_w_q, m_w_kv, m_w_xo, m_norm_final_g, v_norm_mix_g, v_w_in, v_conv_w, v_gm_ln_g, v_gm_ln_b, v_gm_ws, v_gm_bs, v_w_out, v_norm_x_g, v_norm_mem_g, v_w_q, v_w_kv, v_w_xo, v_norm_final_g):
    t = x.shape[1]
    xi = lax.axis_index("x")
    yi = lax.axis_index("y")
    ci = lax.axis_index("c")
    b_idx = jnp.reshape(2 * xi + yi, (1,)).astype(jnp.int32)
    c_idx = jnp.reshape(ci, (1,)).astype(jnp.int32)

    x2d, mem2d, tgt = x[0], mem[0], loss_target[0]
    big = [w_in[0], w_out[0], w_q[0], w_kv[0], w_xo[0]]
    big_m = [m_w_in[0], m_w_out[0], m_w_q[0], m_w_kv[0], m_w_xo[0]]
    big_v = [v_w_in[0], v_w_out[0], v_w_q[0], v_w_kv[0], v_w_xo[0]]
    g3 = norm_final_g.reshape(1, D)

    def pad8(a):
        return jnp.pad(a, ((0, 8 - a.shape[0]), (0, 0)))

    own_blocks, (wc, wct, bsb) = _cast_shards(b_idx, big, gm_ws[0], gm_bs[0])

    proj, hb, win_f, cw8, (wq_f,) = _proj_gather(
        b_idx, x2d, norm_mix_g, own_blocks[0], pad8(conv_w[0]), [own_blocks[2]])
    mixin, (wout_f, wkv_f, wxo_f) = _mixer_fwd(
        proj, cw8, gm_ln_g, gm_ln_b, wc, bsb, [own_blocks[1], own_blocks[3], own_blocks[4]])
    wout2, wq2, wxo2 = wout_f.reshape(MIX, D), wq_f.reshape(D, D), wxo_f.reshape(D, D)
    k, v = _mem_fwd(mem2d, norm_mem_g, wkv_f)

    (loss_row, dmix, dx1b, h2b, dq, ob, dx2b, dk, dv, dg2, dg3) = _tail(
        x2d, tgt, mixin, wout2, wq2, wxo2, k, v, norm_x_g, g3)
    dwkv, dwkv_b, dgm = _mem_bwd(mem2d, norm_mem_g, dk, dv, wkv_f)
    dproj, sv, dwc, grad_x = _mixer_bwd(
        proj, dmix, cw8, gm_ln_g, gm_ln_b, wc, wct, bsb, win_f, x2d, dx1b, norm_mix_g, [dg2, dgm, dg3], loss_row)

    bc_idx = jnp.concatenate([b_idx, c_idx])
    sw = dwc.reshape(HEADS * CH, CH)
    dwin_sum, dwin_sum_b, psmall = _grad_matmul_pair(c_idx, hb, dproj, [sv, sw], [F32, BF16], dgm, name="grad_w_in")
    sums_b = [dwin_sum]
    send_b, recv_b, src_b, land_b, token_b = _exchange_begin([dwin_sum_b], psmall, "exchange_b_begin")

    dwxo, dwxo_b = _grad_matmul(ob, dx2b, token_b, name="grad_w_xo", tk=2048)
    dwq, dwq_b = _grad_matmul(h2b, dq, token_b, name="grad_w_q", tk=2048)
    dwout, dwout_b = _grad_matmul(mixin, dx1b, token_b, name="grad_w_out")
    ps_a = _pair_reduce(c_idx, [dwout, dwkv, dwq, dwxo], [dwout_b, dwkv_b, dwq_b, dwxo_b], [], "pair_reduce_a")
    sums_a, sums_a_b = list(ps_a[:4]), list(ps_a[4:8])
    send_a, recv_a, src_a, land_a, token_a = _exchange_begin(sums_a_b, [], "exchange_a_begin")

    src_b, rx2b = _exchange_end(send_b, recv_b, src_b, land_b, 1, [0, 1, 2], [token_a], "exchange_b_end")
    gwin, svf, swf = _chip_reduce(bc_idx, sums_b, rx2b[:1], rx2b[1:], src_b[1:], "chip_reduce_b")
    out_b = _adamw_big(big[:1], [gwin], big_m[:1], big_v[:1], "adamw_w_in")[0]

    row = lambda a: a.reshape(1, D)
    mat = lambda a: a.reshape(HEADS * CH, CH)
    small, loss = _adamw_small(
        b_idx, svf, swf,
        [(row(norm_mix_g), row(m_norm_mix_g), row(v_norm_mix_g)), (row(norm_x_g), row(m_norm_x_g), row(v_norm_x_g)),
         (row(norm_mem_g), row(m_norm_mem_g), row(v_norm_mem_g)), (row(norm_final_g), row(m_norm_final_g), row(v_norm_final_g)),
         (row(gm_ln_g), row(m_gm_ln_g), row(v_gm_ln_g)), (row(gm_ln_b), row(m_gm_ln_b), row(v_gm_ln_b)),
         (row(gm_bs), row(m_gm_bs), row(v_gm_bs))],
        (conv_w[0], m_conv_w[0], v_conv_w[0]), (mat(gm_ws), mat(m_gm_ws), mat(v_gm_ws)))

    def finish_a(part, src, land, after, tag):
        src, land = _exchange_end(send_a, recv_a, src, land, 4, part, after, "exchange_a%s_end" % tag)
        grads = _chip_reduce(bc_idx, [sums_a[i] for i in part], [land[i] for i in part], [], [], "chip_reduce_a" + tag,
                             steps=2)
        ids = [(1, 3, 2, 4)[i] for i in part]
        outs = _adamw_big([big[i] for i in ids], grads, [big_m[i] for i in ids], [big_v[i] for i in ids], "adamw_a" + tag,
                          steps=4)
        return src, land, outs

    src_a, land_a, (out_wout, out_wkv) = finish_a([0, 1], src_a, land_a, [out_b[0], small[0][0]], "1")
    _, _, (out_wq, out_wxo) = finish_a([2, 3], src_a, land_a, [out_wout[0]], "2")

    def unpack(k):
        vec = lambda i: small[i][k]
        return [vec(0), out_b[k][None], small[7][k][None], vec(4), vec(5), small[8][k].reshape(1, HEADS, CH, CH),
                vec(6).reshape(1, HEADS, CH), out_wout[k][None], vec(1), vec(2), out_wq[k][None], out_wkv[k][None],
                out_wxo[k][None], vec(3).reshape(D)]

    return (loss.reshape(()), grad_x[None], *unpack(3), *unpack(0), *unpack(1), *unpack(2))
```

```python
import functools
import math

import jax
import jax.numpy as jnp
from jax import lax
from jax.experimental import pallas as pl
from jax.experimental.pallas import tpu as pltpu

F32 = jnp.float32
BF16 = jnp.bfloat16
MESH = pl.DeviceIdType.MESH

D = 1024
SLAB = 1024
N_SLAB = 7
IN_DIM = N_SLAB * SLAB
MIX = 2 * SLAB
HEADS = 8
CH = 128
XH = 4
XD = D // XH
EPS = 1e-6
GELU_C = math.sqrt(2.0 / math.pi)
GELU_A = 0.044715
N_CHIP = 4
IN_BLK = IN_DIM // N_CHIP
IN_PIECE = 256
N_PIECE = IN_BLK // IN_PIECE
KV_BLK = 2 * D // N_CHIP

ADAM_LR, ADAM_B1, ADAM_B2, ADAM_EPS, ADAM_WD, ADAM_STEP = 0.001, 0.9, 0.999, 1e-08, 0.01, 10

VMEM_LIMIT = 60 * 1024 * 1024


def _cp(sem=None, vmem=None):
    return pltpu.CompilerParams(dimension_semantics=sem, vmem_limit_bytes=vmem)


def _full(shape, buffers=None):
    n = len(shape)
    if buffers is None:
        return pl.BlockSpec(shape, lambda *_: (0,) * n)
    return pl.BlockSpec(shape, lambda *_: (0,) * n, pipeline_mode=pl.Buffered(buffers))


ANY = pl.BlockSpec(memory_space=pl.ANY)


def _bdot(a, b):
    return jnp.dot(a.astype(BF16), b.astype(BF16), preferred_element_type=F32)


def _bdot_nt(a, b):
    return lax.dot_general(a.astype(BF16), b.astype(BF16), (((1,), (1,)), ((), ())), preferred_element_type=F32)


def _bdot_tn(a, b):
    return lax.dot_general(a.astype(BF16), b.astype(BF16), (((0,), (0,)), ((), ())), preferred_element_type=F32)


def _rms(x, g):
    r = lax.rsqrt(jnp.mean(x * x, axis=-1, keepdims=True) + EPS)
    return x * r * g, r


def _rms_bwd(dy, x, r, g):
    gdy = dy * g
    dx = r * gdy - x * (r * r * r) * jnp.mean(x * gdy, axis=-1, keepdims=True)
    dg = jnp.sum(dy * x * r, axis=0, keepdims=True)
    return dx, dg


def _gelu_parts(x):
    x2 = x * x
    t = jnp.tanh(GELU_C * (x + GELU_A * x * x2))
    val = 0.5 * x * (1.0 + t)
    grad = 0.5 * (1.0 + t) + 0.5 * x * (1.0 - t * t) * (GELU_C * (1.0 + 3.0 * GELU_A * x2))
    return val, grad


def _gelu(x):
    return 0.5 * x * (1.0 + jnp.tanh(GELU_C * (x + GELU_A * x * x * x)))


def _sigmoid(z):
    return 1.0 / (1.0 + jnp.exp(-z))


def _cast_shards(b_idx, arrs, gm_ws, gm_bs):
    n = len(arrs)
    steps = 4

    rows = [a.shape[0] // steps for a in arrs]

    def cast_tiles(*refs):
        for p in range(N_PIECE):
            refs[n][0, p] = refs[0][:, pl.ds(p * IN_PIECE, IN_PIECE)].astype(BF16)
        for i in range(1, n):
            refs[n + i][0] = refs[i][...].astype(BF16)

    def body(b_ref, *refs):
        ws_ref, bs_ref = refs[n:n + 2]
        wc_ref, wct_ref, bsb_ref = refs[2 * n + 2:]
        b = b_ref[0]
        in_specs = [pl.BlockSpec((rows[i], a.shape[1]), lambda i: (i, 0), pipeline_mode=pl.Buffered(3))
                    for i, a in enumerate(arrs)]
        out_specs = [pl.BlockSpec((1, N_PIECE, rows[0], IN_PIECE), lambda i: (b, 0, i, 0))]
        out_specs += [pl.BlockSpec((1, rows[i], a.shape[1]), lambda i: (b, i, 0)) for i, a in enumerate(arrs) if i > 0]
        pltpu.emit_pipeline(cast_tiles, grid=(steps,), in_specs=in_specs, out_specs=out_specs)(*refs[:n], *refs[n + 2:2 * n + 2])
        causal = lax.broadcasted_iota(jnp.int32, (CH, CH), 0) >= lax.broadcasted_iota(jnp.int32, (CH, CH), 1)
        for h in range(HEADS):
            w = jnp.where(causal, ws_ref[h], 0.0)
            wc_ref[h] = w.astype(BF16)
            wct_ref[h] = w.T.astype(BF16)
            bsb_ref[h] = jnp.broadcast_to(bs_ref[h:h + 1, :], (CH, CH)).T

    vmem = pl.BlockSpec(memory_space=pltpu.VMEM)
    out_shape = [jax.ShapeDtypeStruct((N_CHIP, N_PIECE, arrs[0].shape[0], IN_PIECE), BF16)]
    out_shape += [jax.ShapeDtypeStruct((N_CHIP,) + a.shape, BF16) for a in arrs[1:]]
    out_shape += [jax.ShapeDtypeStruct((HEADS, CH, CH), dt) for dt in (BF16, BF16, F32)]
    outs = pl.pallas_call(
        body, out_shape=out_shape, in_specs=[pl.BlockSpec(memory_space=pltpu.SMEM)] + [ANY] * n + [vmem, vmem],
        out_specs=[ANY] * n + [vmem] * 3, name="cast_shards")(b_idx, *arrs, gm_ws, gm_bs)
    return outs[:n], outs[n:]


def _proj_gather(b_idx, x, g, win_own, cw8s, more, tm=1024):
    t = x.shape[0]
    ni = t // tm
    nm = len(more)
    steps = N_CHIP * N_PIECE
    near0, far0 = N_PIECE, 3 * N_PIECE

    def piece_at(step, own):
        k = step - near0
        near, far = (step >= near0) & (step < far0), step >= far0
        block = jnp.where(far, own ^ 3, jnp.where(near, own ^ jnp.where(lax.rem(k, 2) == 0, 2, 1), own))
        return block, jnp.where(far, step - far0, jnp.where(near, lax.div(k, 2), step))

    def body(*refs):
        b_ref, x_any, g_ref, win_in, cw_in = refs[:5]
        o_ref, hb_any, win_f, cw_out = refs[5 + nm:9 + nm]
        more_out = refs[9 + nm:9 + 2 * nm]
        hbuf, xbuf, wv, cw_s, cw_r, loc = refs[9 + 2 * nm:15 + 2 * nm]
        g_in = _Gather([win_f.at[:, p] for p in range(N_PIECE)], *refs[15 + 2 * nm:19 + 2 * nm])
        g_more = _Gather(more_out, *refs[19 + 2 * nm:23 + 2 * nm])
        s = pl.program_id(0)
        x, y, c, chips = _coords()
        b = 2 * x + y
        blks = [2 * chip[0] + chip[1] for chip in chips]

        def cw_cols(blk):
            return cw_out.at[:, pl.ds(blk * (D // N_CHIP), D // N_CHIP)]

        def cw_copy(k, blk):
            src = cw_in if blk is None else cw_cols(blk)
            return pltpu.make_async_remote_copy(src_ref=src, dst_ref=cw_cols(b if blk is None else blk), send_sem=cw_s.at[k],
                                                recv_sem=cw_r.at[k], device_id=(*chips[k], c), device_id_type=MESH)

        cw_local = pltpu.make_async_copy(cw_in, cw_cols(b), loc.at[1])
        hb_copy = pltpu.make_async_copy(hbuf, hb_any, loc.at[0])

        def load(step):
            slot = lax.rem(step, 2)
            block, piece = piece_at(step, b_ref[0])
            return pltpu.make_async_copy(win_f.at[block, piece], wv.at[slot], loc.at[2 + slot])

        def chunk(i):
            return pltpu.make_async_copy(x_any.at[pl.ds(i * tm, tm)], xbuf.at[i % 2], loc.at[4 + i % 2])

        def first():
            g_in.start()
            cw_local.start()
            for k in range(3):
                cw_copy(k, None).start()
            load(0).start()
            chunk(0).start()
            for i in range(ni):
                if i + 1 < ni:
                    chunk(i + 1).start()
                chunk(i).wait()
                h, _ = _rms(xbuf[i % 2], g_ref[...])
                hbuf[pl.ds(i * tm, tm), :] = h.astype(BF16)
            hb_copy.start()

        events = {step: [] for step in range(steps)}
        events[0].append(first)
        for p in range(N_PIECE):
            events[2 * p + 2].append(functools.partial(g_in.hop, [p]))
            events[near0 + 2 * p - 1].append(functools.partial(g_in.near_ready, [p]))
            events[far0 + p - 2].append(functools.partial(g_in.far, [p]))
            events[far0 + p - 1].append(functools.partial(g_in.far_ready, [p]))
        events[2 * N_PIECE + 1].append(g_more.start)
        for step, todo in events.items():
            if todo:
                @pl.when(s == step)
                def _(todo=todo):
                    for do in todo:
                        do()

        @pl.when(s + 1 < steps)
        def _():
            load(s + 1).start()

        load(s).wait()
        for i in range(ni):
            rows = pl.ds(i * tm, tm)
            o_ref[rows, :] = jnp.dot(hbuf[rows, :], wv[lax.rem(s, 2)], preferred_element_type=F32).astype(BF16)

        @pl.when(s == steps - 1)
        def _():
            g_more.hop()
            g_more.far()
            for k in range(3):
                cw_copy(k, blks[k]).wait_recv()
            for k in range(3):
                cw_copy(k, None).wait_send()
            cw_local.wait()
            hb_copy.wait()
            g_more.near_ready()
            g_more.far_ready()
            g_in.drain()
            g_more.drain()

    def out_col(s, b):
        block, piece = piece_at(s, b[0])
        return 0, block * N_PIECE + piece

    in_specs = [ANY, pl.BlockSpec((1, D), lambda s, b: (0, 0)), ANY, ANY] + [ANY] * nm
    out_specs = [pl.BlockSpec((t, IN_PIECE), out_col), ANY, ANY, ANY] + [ANY] * nm
    outs = pl.pallas_call(
        body, out_shape=[jax.ShapeDtypeStruct((t, IN_DIM), BF16), jax.ShapeDtypeStruct((t, D), BF16),
                         jax.ShapeDtypeStruct(win_own.shape, BF16), jax.ShapeDtypeStruct((8, D), F32)]
        + [jax.ShapeDtypeStruct(f.shape, f.dtype) for f in more],
        grid_spec=pltpu.PrefetchScalarGridSpec(
            num_scalar_prefetch=1, grid=(steps,), in_specs=in_specs, out_specs=out_specs,
            scratch_shapes=[pltpu.VMEM((t, D), BF16), pltpu.VMEM((2, tm, D), F32), pltpu.VMEM((2, D, IN_PIECE), BF16)]
            + [pltpu.SemaphoreType.DMA((3,))] * 2 + [pltpu.SemaphoreType.DMA((6,))]
            + _gather_sems(N_PIECE) + _gather_sems(nm)),
        input_output_aliases={3: 2, **{5 + w: 4 + w for w in range(nm)}},
        compiler_params=_cp(("arbitrary",), VMEM_LIMIT), name="proj_gather")(b_idx, x, g, win_own, cw8s, *more)
    return outs[0], outs[1], outs[2], outs[3], outs[4:]


class _Gather:
    def __init__(self, outs, ici_s, ici_r, d2d_s, d2d_r):
        x, y, c, _ = _coords()
        self.outs, self.c = outs, c
        self.sems = ici_s, ici_r, d2d_s, d2d_r
        self.b, self.bx, self.by, self.bd = 2 * x + y, 2 * (1 - x) + y, 2 * x + (1 - y), 2 * (1 - x) + (1 - y)
        self.xn, self.yn, self.sib = (1 - x, y, c), (x, 1 - y, c), (x, y, 1 - c)

    def piece(self, w, blk, hc, quarter=None):
        hr = self.outs[w].shape[1] // 2
        if quarter is None:
            return self.outs[w].at[blk, pl.ds(hc * hr, hr)]
        return self.outs[w].at[blk, pl.ds(hc * hr + quarter * (hr // 2), hr // 2)]

    def ici(self, w, k, ref, to):
        return pltpu.make_async_remote_copy(src_ref=ref, dst_ref=ref, send_sem=self.sems[0].at[w, k],
                                            recv_sem=self.sems[1].at[w, k], device_id=to, device_id_type=MESH)

    def d2d(self, w, k, ref):
        return pltpu.make_async_remote_copy(src_ref=ref, dst_ref=ref, send_sem=self.sems[2].at[w, k],
                                            recv_sem=self.sems[3].at[w, k], device_id=self.sib, device_id_type=MESH)

    def all(self):
        return range(len(self.outs))

    def start(self):
        for w in self.all():
            mine = self.piece(w, self.b, self.c)
            self.ici(w, 0, mine, self.xn).start()
            self.ici(w, 1, mine, self.yn).start()

    def hop(self, ws=None):
        c = self.c
        for w in ws or self.all():
            self.ici(w, 0, self.piece(w, self.bx, c), self.xn).wait_recv()
            self.ici(w, 1, self.piece(w, self.by, c), self.yn).wait_recv()
            self.ici(w, 2, self.piece(w, self.bx, c, 0), self.yn).start()
            self.ici(w, 3, self.piece(w, self.by, c, 1), self.xn).start()
            self.d2d(w, 0, self.piece(w, self.bx, c)).start()
            self.d2d(w, 1, self.piece(w, self.by, c)).start()

    def near_ready(self, ws=None):
        for w in ws or self.all():
            self.d2d(w, 0, self.piece(w, self.bx, 1 - self.c)).wait_recv()
            self.d2d(w, 1, self.piece(w, self.by, 1 - self.c)).wait_recv()

    def far(self, ws=None):
        c = self.c
        for w in ws or self.all():
            self.ici(w, 2, self.piece(w, self.bd, c, 0), self.yn).wait_recv()
            self.ici(w, 3, self.piece(w, self.bd, c, 1), self.xn).wait_recv()
            self.d2d(w, 2, self.piece(w, self.bd, c, 0)).start()
            self.d2d(w, 3, self.piece(w, self.bd, c, 1)).start()

    def far_ready(self, ws=None):
        for w in ws or self.all():
            self.d2d(w, 2, self.piece(w, self.bd, 1 - self.c, 0)).wait_recv()
            self.d2d(w, 3, self.piece(w, self.bd, 1 - self.c, 1)).wait_recv()

    def drain(self):
        c = self.c
        for w in self.all():
            mine = self.piece(w, self.b, c)
            self.ici(w, 0, mine, self.xn).wait_send()
            self.ici(w, 1, mine, self.yn).wait_send()
            self.ici(w, 2, self.piece(w, self.bx, c, 0), self.yn).wait_send()
            self.ici(w, 3, self.piece(w, self.by, c, 1), self.xn).wait_send()
            self.d2d(w, 0, self.piece(w, self.bx, c)).wait_send()
            self.d2d(w, 1, self.piece(w, self.by, c)).wait_send()
            self.d2d(w, 2, self.piece(w, self.bd, c, 0)).wait_send()
            self.d2d(w, 3, self.piece(w, self.bd, c, 1)).wait_send()


def _gather_sems(nw):
    return [pltpu.SemaphoreType.DMA((max(nw, 1), 4))] * 4


def _mixer_fwd(proj, cw8, lng, lnb, wc, bsb, fulls, tm=256):
    t = proj.shape[0]
    nt = t // tm
    nch = tm // CH
    nw = len(fulls)

    def body(*refs):
        p_ref, cw_ref, lng_ref, lnb_ref, wc_ref, bsb_ref = refs[:6]
        mix_ref = refs[6 + nw]
        w_outs = refs[7 + nw:7 + 2 * nw]
        prev_ref = refs[7 + 2 * nw]
        gather = _Gather(w_outs, *refs[8 + 2 * nw:])

        @pl.when(pl.program_id(0) == 0)
        def _():
            gather.start()
            prev_ref[...] = jnp.zeros_like(prev_ref)

        @pl.when(pl.program_id(0) == nt // 2)
        def _():
            gather.hop()

        @pl.when(pl.program_id(0) == nt - 1)
        def _():
            gather.far()

        rows = lax.broadcasted_iota(jnp.int32, (tm, CH), 0)
        for s in range(HEADS):
            cs = pl.ds(CH * s, CH)

            def slab(k):
                return p_ref[:, pl.ds(k * SLAB + CH * s, CH)].astype(F32)

            gb, gc, xa, za = slab(0), slab(1), slab(2), slab(3)
            cx = gc * xa
            p6 = jnp.broadcast_to(prev_ref[6:7, cs], (tm, CH))
            p7 = jnp.broadcast_to(prev_ref[7:8, cs], (tm, CH))
            c1 = jnp.where(rows == 0, p7, pltpu.roll(cx, 1, 0))
            c2 = jnp.where(rows == 0, p6, jnp.where(rows == 1, p7, pltpu.roll(cx, 2, 0)))
            prev_ref[:, cs] = cx[tm - 8:, :]
            cv = cw_ref[0:1, cs] * c2 + cw_ref[1:2, cs] * c1 + cw_ref[2:3, cs] * cx
            mix_ref[:, cs] = (gb * cv * (za * _sigmoid(za))).astype(BF16)

            u, v, zb = slab(4), slab(5), slab(6)
            ug, vg = _gelu(u), _gelu(v)
            dlt = vg - jnp.mean(vg, axis=-1, keepdims=True)
            vhat = dlt * lax.rsqrt(jnp.mean(dlt * dlt, axis=-1, keepdims=True) + EPS)
            vn = (vhat * lng_ref[:, cs] + lnb_ref[:, cs]).astype(BF16)
            gate = ug * (zb * _sigmoid(zb))
            for c in range(nch):
                rs = slice(CH * c, CH * (c + 1))
                sp = jnp.dot(wc_ref[s], vn[rs], preferred_element_type=F32) + bsb_ref[s]
                mix_ref[rs, pl.ds(SLAB + CH * s, CH)] = (gate[rs] * sp).astype(BF16)

        @pl.when(pl.program_id(0) == nt - 1)
        def _():
            gather.near_ready()
            gather.far_ready()
            gather.drain()

    sems = _gather_sems(nw)
    outs = pl.pallas_call(
        body, grid=(nt,),
        in_specs=[pl.BlockSpec((tm, IN_DIM), lambda i: (i, 0)), _full((8, D)), _full((1, D)), _full((1, D)),
                  _full((HEADS, CH, CH)), _full((HEADS, CH, CH))] + [ANY] * nw,
        out_specs=[pl.BlockSpec((tm, MIX), lambda i: (i, 0))] + [ANY] * nw,
        out_shape=[jax.ShapeDtypeStruct((t, MIX), BF16)] + [jax.ShapeDtypeStruct(f.shape, f.dtype) for f in fulls],
        input_output_aliases={6 + w: 1 + w for w in range(nw)},
        scratch_shapes=[pltpu.VMEM((8, D), F32)] + sems,
        compiler_params=_cp(("arbitrary",), VMEM_LIMIT), name="mixer_fwd")(proj, cw8, lng, lnb, wc, bsb, *fulls)
    return outs[0], outs[1:]


def _mem_fwd(mem, gm, wkv_f):
    n_mem = mem.shape[0]

    def body(mem_ref, gm_ref, w_ref, k_ref, v_ref):
        m, _ = _rms(mem_ref[...], gm_ref[...])
        mb = m.astype(BF16)
        for j in range(N_CHIP):
            dst = k_ref if j < 2 else v_ref
            dst[:, pl.ds(KV_BLK * (j % 2), KV_BLK)] = jnp.dot(mb, w_ref[j], preferred_element_type=F32).astype(BF16)

    return pl.pallas_call(
        body, out_shape=[jax.ShapeDtypeStruct((n_mem, D), BF16), jax.ShapeDtypeStruct((n_mem, D), BF16)],
        compiler_params=_cp(None, VMEM_LIMIT), name="mem_fwd")(mem, gm, wkv_f)


def _tail(x, tgt, mixin, wout, wq, wxo, k, v, g2, g3, tm=512, sub=512):
    t = x.shape[0]
    n_mem = k.shape[0]
    scale = 1.0 / math.sqrt(XD)

    def body(x_ref, tgt_ref, mix_ref, wout_ref, wq_ref, wxo_ref, k_ref, v_ref, g2_ref, g3_ref,
             loss_ref, dmix_ref, dx1b_ref, h2_ref, dq_ref, o_ref, dx2b_ref, dk_ref, dv_ref, dg2_ref, dg3_ref):
        @pl.when(pl.program_id(0) == 0)
        def _():
            loss_ref[...] = jnp.zeros_like(loss_ref)
            dk_ref[...] = jnp.zeros_like(dk_ref)
            dv_ref[...] = jnp.zeros_like(dv_ref)
            dg2_ref[...] = jnp.zeros_like(dg2_ref)
            dg3_ref[...] = jnp.zeros_like(dg3_ref)

        g2, g3 = g2_ref[...], g3_ref[...]
        for sb in range(tm // sub):
            rs = pl.ds(sub * sb, sub)
            x1 = x_ref[rs, :] + jnp.dot(mix_ref[rs, :], wout_ref[...], preferred_element_type=F32)
            h2, r2 = _rms(x1, g2)
            h2b = h2.astype(BF16)
            h2_ref[rs, :] = h2b
            q = jnp.dot(h2b, wq_ref[...], preferred_element_type=F32).astype(BF16)
            probs, outs = [], []
            for hd in range(XH):
                hs = pl.ds(XD * hd, XD)
                s = _bdot_nt(q[:, XD * hd:XD * (hd + 1)], k_ref[:, hs]) * scale
                e = jnp.exp(s - jnp.max(s, axis=-1, keepdims=True))
                p = e / jnp.sum(e, axis=-1, keepdims=True)
                probs.append(p)
                outs.append(_bdot(p, v_ref[:, hs]))
            ob = jnp.concatenate(outs, axis=-1).astype(BF16)
            o_ref[rs, :] = ob
            x2 = x1 + jnp.dot(ob, wxo_ref[...], preferred_element_type=F32)
            y, r3 = _rms(x2, g3)
            diff = y - tgt_ref[rs, :]
            row_loss = jnp.sum(diff * diff, axis=-1, keepdims=True)
            loss_ref[...] += jnp.broadcast_to(jnp.sum(row_loss, axis=0, keepdims=True) * (0.5 / D), loss_ref.shape)

            dx2, dg3 = _rms_bwd(diff * (1.0 / D), x2, r3, g3)
            dg3_ref[...] += dg3
            dx2b = dx2.astype(BF16)
            dx2b_ref[rs, :] = dx2b
            do = _bdot_nt(dx2b, wxo_ref[...])
            dqs = []
            for hd in range(XH):
                hs = pl.ds(XD * hd, XD)
                p = probs[hd]
                do_h = do[:, XD * hd:XD * (hd + 1)]
                dv_ref[:, hs] += _bdot_tn(p, do_h)
                dp = _bdot_nt(do_h, v_ref[:, hs])
                ds = p * (dp - jnp.sum(dp * p, axis=-1, keepdims=True))
                dqs.append(_bdot(ds, k_ref[:, hs]) * scale)
                dk_ref[:, hs] += _bdot_tn(ds, q[:, XD * hd:XD * (hd + 1)]) * scale
            dq = jnp.concatenate(dqs, axis=-1).astype(BF16)
            dq_ref[rs, :] = dq
            dx1n, dg2 = _rms_bwd(_bdot_nt(dq, wq_ref[...]), x1, r2, g2)
            dg2_ref[...] += dg2
            dx1b = (dx2 + dx1n).astype(BF16)
            dx1b_ref[rs, :] = dx1b
            dmix_ref[rs, :] = _bdot_nt(dx1b, wout_ref[...]).astype(BF16)

    tok = lambda w: pl.BlockSpec((tm, w), lambda i: (i, 0))
    return pl.pallas_call(
        body, grid=(t // tm,),
        in_specs=[tok(D), tok(D), tok(MIX), _full((MIX, D), 1), _full((D, D), 1), _full((D, D), 1),
                  _full((n_mem, D), 1), _full((n_mem, D), 1), _full((1, D)), _full((1, D))],
        out_specs=[_full((1, D)), tok(MIX), tok(D), tok(D), tok(D), tok(D), tok(D),
                   _full((n_mem, D)), _full((n_mem, D)), _full((1, D)), _full((1, D))],
        out_shape=[jax.ShapeDtypeStruct((1, D), F32), jax.ShapeDtypeStruct((t, MIX), BF16),
                   jax.ShapeDtypeStruct((t, D), BF16),
                   jax.ShapeDtypeStruct((t, D), BF16), jax.ShapeDtypeStruct((t, D), BF16),
                   jax.ShapeDtypeStruct((t, D), BF16), jax.ShapeDtypeStruct((t, D), BF16),
                   jax.ShapeDtypeStruct((n_mem, D), F32), jax.ShapeDtypeStruct((n_mem, D), F32),
                   jax.ShapeDtypeStruct((1, D), F32), jax.ShapeDtypeStruct((1, D), F32)],
        compiler_params=_cp(("arbitrary",), VMEM_LIMIT), name="tail")(x, tgt, mixin, wout, wq, wxo, k, v, g2, g3)


def _mem_bwd(mem, gm, dk, dv, wkv_f):
    def body(mem_ref, gm_ref, dk_ref, dv_ref, w_ref, dw_ref, dwb_ref, dgm_ref):
        mem_v = mem_ref[...]
        m, rm = _rms(mem_v, gm_ref[...])
        mb = m.astype(BF16)
        dm = jnp.zeros_like(mem_v)
        for j in range(N_CHIP):
            src = dk_ref if j < 2 else dv_ref
            dkv = src[:, pl.ds(KV_BLK * (j % 2), KV_BLK)].astype(BF16)
            dw = _bdot_tn(mb, dkv)
            dw_ref[j] = dw
            dwb_ref[j] = dw.astype(BF16)
            dm = dm + _bdot_nt(dkv, w_ref[j])
        dgm_ref[...] = jnp.sum(dm * mem_v * rm, axis=0, keepdims=True)

    return pl.pallas_call(
        body, out_shape=[jax.ShapeDtypeStruct((N_CHIP, D, KV_BLK), F32), jax.ShapeDtypeStruct((N_CHIP, D, KV_BLK), BF16),
                         jax.ShapeDtypeStruct((1, D), F32)],
        compiler_params=_cp(None, VMEM_LIMIT), name="mem_bwd")(mem, gm, dk, dv, wkv_f)


def _mixer_bwd(proj, dmix, cw8, lng, lnb, wc, wct, bsb, win_f, x, dx1, g1, rows123, row7, tm=256):
    t = proj.shape[0]
    nt = t // tm
    nch = tm // CH
    hb = 16
    pair = 2 * CH
    assert pair == IN_PIECE

    def body(p_ref, pgc_ref, pxa_ref, dm_ref, cw_ref, lng_ref, lnb_ref, wc_ref, wct_ref, bsb_ref, w_ref, x_ref,
             dx1_ref, g1_ref, r1_ref, r2_ref, r3_ref, r7_ref, dp_ref, sv_ref, dwc_ref, gx_ref,
             next_ref, dh_ref):
        i = pl.program_id(0)
        dg1_ref, dlng_ref, dlnb_ref, dbs_ref = (sv_ref.at[pl.ds(r, 1)] for r in (0, 4, 5, 6))
        dcw_ref = sv_ref.at[pl.ds(8, 8)]

        @pl.when(i == 0)
        def _():
            next_ref[...] = jnp.zeros_like(next_ref)
            sv_ref[...] = jnp.zeros_like(sv_ref)
            dwc_ref[...] = jnp.zeros_like(dwc_ref)
            for r, ref in ((1, r1_ref), (2, r2_ref), (3, r3_ref), (7, r7_ref)):
                sv_ref[r:r + 1, :] = ref[...]

        first_tile = i == nt - 1
        rows = lax.broadcasted_iota(jnp.int32, (tm, CH), 0)
        ones8 = jnp.ones((8, CH), BF16)
        for s in range(HEADS):
            cs = pl.ds(CH * s, CH)

            def slab(k):
                return p_ref[:, pl.ds(k * SLAB + CH * s, CH)].astype(F32)

            gb, gc, xa, za = slab(0), slab(1), slab(2), slab(3)
            da = dm_ref[:, cs].astype(F32)
            cx = gc * xa
            cxp = pgc_ref[:, cs].astype(F32) * pxa_ref[:, cs].astype(F32)
            cxp = jnp.where(first_tile, jnp.zeros_like(cxp), cxp)
            p6 = jnp.broadcast_to(cxp[hb - 2:hb - 1, :], (tm, CH))
            p7 = jnp.broadcast_to(cxp[hb - 1:hb, :], (tm, CH))
            c1 = jnp.where(rows == 0, p7, pltpu.roll(cx, 1, 0))
            c2 = jnp.where(rows == 0, p6, jnp.where(rows == 1, p7, pltpu.roll(cx, 2, 0)))
            w0, w1, w2 = cw_ref[0:1, cs], cw_ref[1:2, cs], cw_ref[2:3, cs]
            cv = w0 * c2 + w1 * c1 + w2 * cx
            sg = _sigmoid(za)
            sa = za * sg
            dcv = da * gb * sa
            dp_ref[:, pl.ds(0 * SLAB + CH * s, CH)] = (da * cv * sa).astype(BF16)
            dp_ref[:, pl.ds(3 * SLAB + CH * s, CH)] = (da * gb * cv * (sg * (1.0 + za * (1.0 - sg)))).astype(BF16)
            n0 = jnp.broadcast_to(next_ref[0:1, cs], (tm, CH))
            n1 = jnp.broadcast_to(next_ref[1:2, cs], (tm, CH))
            u1 = jnp.where(rows == tm - 1, n0, pltpu.roll(dcv, tm - 1, 0))
            u2 = jnp.where(rows == tm - 2, n0, jnp.where(rows == tm - 1, n1, pltpu.roll(dcv, tm - 2, 0)))
            next_ref[:, cs] = dcv[0:8, :]
            dcx = w2 * dcv + w1 * u1 + w0 * u2
            dp_ref[:, pl.ds(1 * SLAB + CH * s, CH)] = (dcx * xa).astype(BF16)
            dp_ref[:, pl.ds(2 * SLAB + CH * s, CH)] = (dcx * gc).astype(BF16)
            dcw_ref[0:1, cs] += jnp.sum(dcv * c2, axis=0, keepdims=True)
            dcw_ref[1:2, cs] += jnp.sum(dcv * c1, axis=0, keepdims=True)
            dcw_ref[2:3, cs] += jnp.sum(dcv * cx, axis=0, keepdims=True)

            u, v, zb = slab(4), slab(5), slab(6)
            db = dm_ref[:, pl.ds(SLAB + CH * s, CH)].astype(F32)
            ug, ugrad = _gelu_parts(u)
            vg, vgrad = _gelu_parts(v)
            dlt = vg - jnp.mean(vg, axis=-1, keepdims=True)
            rstd = lax.rsqrt(jnp.mean(dlt * dlt, axis=-1, keepdims=True) + EPS)
            vhat = dlt * rstd
            lg = lng_ref[:, cs]
            vn = (vhat * lg + lnb_ref[:, cs]).astype(BF16)
            sgb = _sigmoid(zb)
            szb = zb * sgb
            sps, dvns = [], []
            dbs = jnp.zeros((8, CH), F32)
            dwc = jnp.zeros((CH, CH), F32)
            for c in range(nch):
                rs = slice(CH * c, CH * (c + 1))
                sp = jnp.dot(wc_ref[s], vn[rs], preferred_element_type=F32) + bsb_ref[s]
                dsp = (db[rs] * ug[rs] * szb[rs]).astype(BF16)
                dbs = dbs + lax.dot_general(ones8, dsp, (((1,), (1,)), ((), ())), preferred_element_type=F32)
                dwc = dwc + lax.dot_general(dsp, vn[rs], (((1,), (1,)), ((), ())), preferred_element_type=F32)
                dvns.append(jnp.dot(wct_ref[s], dsp, preferred_element_type=F32))
                sps.append(sp)
            sp = jnp.concatenate(sps, axis=0)
            dvn = jnp.concatenate(dvns, axis=0)
            dbs_ref[:, cs] += dbs[0:1]
            dwc_ref[s] += dwc
            dlng_ref[:, cs] += jnp.sum(dvn * vhat, axis=0, keepdims=True)
            dlnb_ref[:, cs] += jnp.sum(dvn, axis=0, keepdims=True)
            dvhat = dvn * lg
            dvg = rstd * (dvhat - jnp.mean(dvhat, axis=-1, keepdims=True)
                          - vhat * jnp.mean(dvhat * vhat, axis=-1, keepdims=True))
            dp_ref[:, pl.ds(4 * SLAB + CH * s, CH)] = (db * sp * szb * ugrad).astype(BF16)
            dp_ref[:, pl.ds(5 * SLAB + CH * s, CH)] = (dvg * vgrad).astype(BF16)
            dp_ref[:, pl.ds(6 * SLAB + CH * s, CH)] = (db * ug * sp * (sgb * (1.0 + zb * (1.0 - sgb)))).astype(BF16)

            if s % 2 == 1:
                part = None
                for k in range(N_SLAB):
                    col = k * SLAB + pair * (s // 2)
                    blk, off = divmod(col, IN_BLK)
                    term = lax.dot_general(dp_ref[:, pl.ds(col, pair)], w_ref[blk, off // IN_PIECE],
                                           (((1,), (1,)), ((), ())), preferred_element_type=F32)
                    part = term if part is None else part + term
                if s == 1:
                    dh_ref[...] = part
                else:
                    dh_ref[...] += part

        xv = x_ref[...]
        r = lax.rsqrt(jnp.mean(xv * xv, axis=-1, keepdims=True) + EPS)
        dxn, dg = _rms_bwd(dh_ref[...], xv, r, g1_ref[...])
        gx_ref[...] = dx1_ref[...].astype(F32) + dxn
        dg1_ref[...] += dg

        @pl.when(i == nt - 1)
        def _():
            tril = lax.broadcasted_iota(jnp.int32, (CH, CH), 0) >= lax.broadcasted_iota(jnp.int32, (CH, CH), 1)
            for s in range(HEADS):
                dwc_ref[s] = jnp.where(tril, dwc_ref[s], 0.0)

    rev = lambda i: nt - 1 - i
    halo = lambda col: pl.BlockSpec((hb, SLAB), lambda i: (jnp.maximum(rev(i) * (tm // hb) - 1, 0), col))
    tok = lambda w: pl.BlockSpec((tm, w), lambda i: (rev(i), 0))
    return pl.pallas_call(
        body, grid=(nt,),
        in_specs=[tok(IN_DIM), halo(1), halo(2), tok(MIX), _full((8, D)), _full((1, D)), _full((1, D)),
                  _full((HEADS, CH, CH)), _full((HEADS, CH, CH)), _full((HEADS, CH, CH)),
                  _full((N_CHIP, N_PIECE, D, IN_PIECE), 1), tok(D), tok(D)] + [_full((1, D))] * 5,
        out_specs=[tok(IN_DIM), _full((16, D)), _full((HEADS, CH, CH)), tok(D)],
        out_shape=[jax.ShapeDtypeStruct((t, IN_DIM), BF16), jax.ShapeDtypeStruct((16, D), F32),
                   jax.ShapeDtypeStruct((HEADS, CH, CH), F32), jax.ShapeDtypeStruct((t, D), F32)],
        scratch_shapes=[pltpu.VMEM((8, D), F32), pltpu.VMEM((tm, D), F32)],
        compiler_params=_cp(("arbitrary",), VMEM_LIMIT), name="mixer_bwd")(
            proj, proj, proj, dmix, cw8, lng, lnb, wc, wct, bsb, win_f, x, dx1, g1, *rows123, row7)


def _grad_matmul(a, b, after, *, name, tk=1024):
    t, m = a.shape
    n = b.shape[1]
    nk = t // tk

    def body(a_ref, b_ref, after_ref, o_ref, ob_ref):
        kk = pl.program_id(0)
        part = lax.dot_general(a_ref[...], b_ref[...], (((0,), (0,)), ((), ())), preferred_element_type=F32)

        @pl.when(kk == 0)
        def _():
            o_ref[...] = part

        @pl.when(kk > 0)
        def _():
            o_ref[...] += part

        @pl.when(kk == nk - 1)
        def _():
            ob_ref[...] = o_ref[...].astype(BF16)

    o_spec = pl.BlockSpec((m, n), lambda k: (0, 0))
    o32, o16 = pl.pallas_call(
        body, grid=(nk,), in_specs=[pl.BlockSpec((tk, m), lambda k: (k, 0)), pl.BlockSpec((tk, n), lambda k: (k, 0)), ANY],
        out_specs=[o_spec, o_spec], out_shape=[jax.ShapeDtypeStruct((m, n), F32), jax.ShapeDtypeStruct((m, n), BF16)],
        compiler_params=_cp(("arbitrary",), VMEM_LIMIT), name=name)(a, b, after)
    return o32.reshape(N_CHIP, m // N_CHIP, n), o16.reshape(N_CHIP, m // N_CHIP, n)


def _coords():
    x, y, c = lax.axis_index("x"), lax.axis_index("y"), lax.axis_index("c")
    chips = [(1 - x, y), (x, 1 - y), (1 - x, 1 - y)]
    return x, y, c, chips


def _pair_reduce(c_idx, grads, grads_b, smalls, name):
    ng, ns = len(grads), len(smalls)
    halves = [g.shape[1] // 2 for g in grads]

    def body(c_ref, *refs):
        g_in, gb_any = refs[:ng], refs[ng:2 * ng]
        s_own, s_any = refs[2 * ng:2 * ng + ns], refs[2 * ng + ns:2 * ng + 2 * ns]
        o = refs[2 * ng + 2 * ns:4 * ng + 3 * ns]
        lands = refs[4 * ng + 3 * ns:5 * ng + 4 * ns]
        send, recv = refs[5 * ng + 4 * ns:]
        x, y, c, _ = _coords()
        j = pl.program_id(0)

        def big(i, blk):
            return pltpu.make_async_remote_copy(
                src_ref=gb_any[i].at[blk, pl.ds((1 - c) * halves[i], halves[i])], dst_ref=lands[i].at[blk],
                send_sem=send.at[i, blk], recv_sem=recv.at[i, blk], device_id=(x, y, 1 - c), device_id_type=MESH)

        def small(i):
            return pltpu.make_async_remote_copy(
                src_ref=s_any[i].at[1 - c], dst_ref=lands[ng + i],
                send_sem=send.at[ng + i, 0], recv_sem=recv.at[ng + i, 0], device_id=(x, y, 1 - c), device_id_type=MESH)

        @pl.when(j == 0)
        def _():
            for blk in range(N_CHIP):
                for i in range(ng):
                    big(i, blk).start()
            for i in range(ns):
                small(i).start()

        for i in range(ng):
            big(i, j).wait_recv()
            tot = g_in[i][...] + lands[i][j].astype(F32)
            o[i][...] = tot
            o[ng + i][...] = tot.astype(BF16)

        @pl.when(j == N_CHIP - 1)
        def _():
            for i in range(ns):
                small(i).wait_recv()
                o[2 * ng + i][...] = s_own[i][...] + lands[ng + i][...]
                small(i).wait_send()
            for blk in range(N_CHIP):
                for i in range(ng):
                    big(i, blk).wait_send()

    in_specs = [pl.BlockSpec((None, None, halves[i], g.shape[2]), lambda b, c: (b, c[0], 0, 0)) for i, g in enumerate(grads)]
    in_specs += [ANY] * ng
    in_specs += [pl.BlockSpec((None, s.shape[0] // 2, s.shape[1]), lambda b, c: (c[0], 0, 0)) for s in smalls]
    in_specs += [ANY] * ns
    blk = [pl.BlockSpec((None, halves[i], g.shape[2]), lambda b, c: (b, 0, 0)) for i, g in enumerate(grads)]
    out_specs = blk + blk + [pl.BlockSpec((s.shape[0] // 2, s.shape[1]), lambda b, c: (0, 0)) for s in smalls]
    out_shape = [jax.ShapeDtypeStruct((N_CHIP, halves[i], g.shape[2]), F32) for i, g in enumerate(grads)]
    out_shape += [jax.ShapeDtypeStruct((N_CHIP, halves[i], g.shape[2]), BF16) for i, g in enumerate(grads)]
    out_shape += [jax.ShapeDtypeStruct((s.shape[0] // 2, s.shape[1]), F32) for s in smalls]
    scratch = [pltpu.VMEM((N_CHIP, halves[i], g.shape[2]), BF16) for i, g in enumerate(grads)]
    scratch += [pltpu.VMEM((s.shape[0] // 2, s.shape[1]), F32) for s in smalls]
    scratch += [pltpu.SemaphoreType.DMA((ng + ns, N_CHIP)), pltpu.SemaphoreType.DMA((ng + ns, N_CHIP))]
    grads4 = [g.reshape(N_CHIP, 2, halves[i], g.shape[2]) for i, g in enumerate(grads)]
    smalls3 = [s.reshape(2, s.shape[0] // 2, s.shape[1]) for s in smalls]
    return pl.pallas_call(
        body, out_shape=out_shape,
        grid_spec=pltpu.PrefetchScalarGridSpec(num_scalar_prefetch=1, grid=(N_CHIP,), in_specs=in_specs,
                                               out_specs=out_specs, scratch_shapes=scratch),
        compiler_params=_cp(("arbitrary",), VMEM_LIMIT), name=name)(c_idx, *grads4, *grads_b, *smalls3, *smalls3)


def _grad_matmul_pair(c_idx, a, b, smalls, small_dtypes, after, *, name, tk=2048):
    t, m = a.shape
    bn = b.shape[1] // N_CHIP
    nk = t // tk
    hr = m // 2
    ns = len(smalls)

    def body(c_ref, a_ref, b_ref, *refs):
        s_own, s_any = refs[:ns], refs[ns:2 * ns]
        o32, o16 = refs[2 * ns + 1], refs[2 * ns + 2]
        o_small = refs[2 * ns + 3:3 * ns + 3]
        acc, tb, land, st16 = refs[3 * ns + 3:3 * ns + 7]
        s_land, s_stage = refs[3 * ns + 7:4 * ns + 7], refs[4 * ns + 7:5 * ns + 7]
        send, recv, loc = refs[5 * ns + 7:]
        x, y, c, _ = _coords()
        sibling = dict(device_id=(x, y, 1 - c), device_id_type=MESH)
        j, kk = pl.program_id(0), pl.program_id(1)
        mine = pl.ds(pl.multiple_of(c * hr, hr), hr)
        theirs = pl.ds(pl.multiple_of((1 - c) * hr, hr), hr)

        def to_sibling(blk):
            return pltpu.make_async_remote_copy(src_ref=tb, dst_ref=land.at[blk], send_sem=send.at[blk],
                                                recv_sem=recv.at[blk], **sibling)

        def small(i):
            return pltpu.make_async_remote_copy(src_ref=s_any[i].at[1 - c], dst_ref=s_land[i], send_sem=send.at[N_CHIP + i],
                                                recv_sem=recv.at[N_CHIP + i], **sibling)

        def written(blk):
            return (pltpu.make_async_copy(acc.at[blk % 2, mine], o32.at[blk], loc.at[0]),
                    pltpu.make_async_copy(st16, o16.at[blk], loc.at[1]))

        def finish(blk):
            to_sibling(blk).wait_recv()

            @pl.when(blk > 0)
            def _():
                for cp in written(blk - 1):
                    cp.wait()

            tot = acc[blk % 2, mine, :] + land[blk].astype(F32)
            acc[blk % 2, mine, :] = tot
            st16[...] = tot.astype(BF16)
            for cp in written(blk):
                cp.start()

        def small_out(i):
            return pltpu.make_async_copy(s_stage[i], o_small[i], loc.at[2 + i])

        @pl.when((j == 0) & (kk == 0))
        def _():
            for i in range(ns):
                small(i).start()

        @pl.when((j == 1) & (kk == 0))
        def _():
            for i in range(ns):
                small(i).wait_recv()
                s_stage[i][...] = (s_own[i][...] + s_land[i][...]).astype(small_dtypes[i])
                small_out(i).start()

        @pl.when((j > 0) & (kk == 0))
        def _():
            finish(j - 1)

        part = lax.dot_general(a_ref[...], b_ref[...], (((0,), (0,)), ((), ())), preferred_element_type=F32)
        slot = lax.rem(j, 2)

        @pl.when(kk == 0)
        def _():
            acc[slot] = part

        @pl.when(kk > 0)
        def _():
            acc[slot] += part

        @pl.when(kk == nk - 1)
        def _():
            @pl.when(j > 0)
            def _():
                to_sibling(j - 1).wait_send()

            tb[...] = acc[slot, theirs, :].astype(BF16)
            to_sibling(j).start()

        @pl.when((j == N_CHIP - 1) & (kk == nk - 1))
        def _():
            finish(j)
            for i in range(ns):
                small_out(i).wait()
                small(i).wait_send()
            for cp in written(j):
                cp.wait()
            to_sibling(j).wait_send()

    halves = [(s.shape[0] // 2, s.shape[1]) for s in smalls]
    in_specs = [pl.BlockSpec((tk, m), lambda j, k, c: (k, 0)), pl.BlockSpec((tk, bn), lambda j, k, c: (k, j))]
    in_specs += [pl.BlockSpec((None,) + h, lambda j, k, c: (c[0], 0, 0)) for h in halves] + [ANY] * ns + [ANY]
    out_shape = [jax.ShapeDtypeStruct((N_CHIP, hr, bn), F32), jax.ShapeDtypeStruct((N_CHIP, hr, bn), BF16)]
    out_shape += [pltpu.HBM(h, dt) for h, dt in zip(halves, small_dtypes)]
    scratch = [pltpu.VMEM((2, m, bn), F32), pltpu.VMEM((hr, bn), BF16),
               pltpu.VMEM((N_CHIP, hr, bn), BF16), pltpu.VMEM((hr, bn), BF16)]
    scratch += [pltpu.VMEM(h, F32) for h in halves] + [pltpu.VMEM(h, dt) for h, dt in zip(halves, small_dtypes)]
    scratch += [pltpu.SemaphoreType.DMA((N_CHIP + ns,)), pltpu.SemaphoreType.DMA((N_CHIP + ns,)),
                pltpu.SemaphoreType.DMA((2 + ns,))]
    smalls3 = [s.reshape((2,) + h) for s, h in zip(smalls, halves)]
    outs = pl.pallas_call(
        body, out_shape=out_shape,
        grid_spec=pltpu.PrefetchScalarGridSpec(num_scalar_prefetch=1, grid=(N_CHIP, nk), in_specs=in_specs,
                                               out_specs=[ANY, ANY] + [_HBM] * ns, scratch_shapes=scratch),
        compiler_params=_cp(("arbitrary", "arbitrary"), VMEM_LIMIT), name=name)(c_idx, a, b, *smalls3, *smalls3, after)
    return outs[0], outs[1], list(outs[2:])


_HBM = pl.BlockSpec(memory_space=pltpu.HBM)
_SEM = pl.BlockSpec(memory_space=pltpu.SEMAPHORE)


def _split_copies(ins, lands, ng, send, recv, arriving):
    x, y, c, chips = _coords()
    b = 2 * x + y
    copies = []
    for i in range(len(ins)):
        for k in range(3):
            blk = 2 * chips[k][0] + chips[k][1]
            src, dst, got = (ins[i].at[blk], lands[i].at[k], lands[i].at[k]) if i < ng else (ins[i], lands[i].at[b], lands[i].at[blk])
            sems = dict(send_sem=send.at[3 * i + k], recv_sem=recv.at[3 * i + k], device_id=(*chips[k], c), device_id_type=MESH)
            if arriving:
                copies.append(pltpu.make_async_remote_copy(src_ref=got, dst_ref=got, **sems))
            else:
                copies.append(pltpu.make_async_remote_copy(src_ref=src, dst_ref=dst, **sems))
    return copies


def _exchange_begin(sums_b, smalls, name):
    ng, n = len(sums_b), len(sums_b) + len(smalls)
    srcs = list(sums_b) + list(smalls)
    lands = [lax.empty((3,) + g.shape[1:], g.dtype) for g in sums_b] + [lax.empty((N_CHIP,) + s.shape, s.dtype) for s in smalls]

    def body(*refs):
        ins, land_refs = refs[:n], refs[n:2 * n]
        send, recv = refs[2 * n], refs[2 * n + 1]
        token = refs[4 * n + 2]
        for cp in _split_copies(ins, land_refs, ng, send, recv, False):
            cp.start()
        token[...] = jnp.zeros_like(token)

    hbm = lambda a: pltpu.HBM(a.shape, a.dtype)
    outs = pl.pallas_call(
        body, name=name,
        out_shape=(pltpu.SemaphoreType.DMA((3 * n,)), pltpu.SemaphoreType.DMA((3 * n,)), *[hbm(a) for a in srcs + lands],
                   jax.ShapeDtypeStruct((8, 128), F32)),
        in_specs=[_HBM] * (2 * n), out_specs=(_SEM, _SEM, *[_HBM] * (2 * n), pl.BlockSpec(memory_space=pltpu.VMEM)),
        input_output_aliases={i: 2 + i for i in range(2 * n)},
        compiler_params=pltpu.CompilerParams(has_side_effects=pltpu.SideEffectType.DATAFLOW_SIDE_EFFECTING),
    )(*[pltpu.with_memory_space_constraint(a, pltpu.HBM) for a in srcs + lands])
    return outs[0], outs[1], list(outs[2:2 + n]), list(outs[2 + n:2 + 2 * n]), outs[2 + 2 * n]


def _exchange_end(send, recv, srcs, lands, ng, which, after, name):
    n = len(srcs)
    after = list(after)

    def body(*refs):
        ins, land_refs = refs[:n], refs[n:2 * n]
        send_ref, recv_ref = refs[2 * n], refs[2 * n + 1]
        outgoing = _split_copies(ins, land_refs, ng, send_ref, recv_ref, False)
        arriving = _split_copies(ins, land_refs, ng, send_ref, recv_ref, True)
        for i in which:
            for cp in outgoing[3 * i:3 * i + 3]:
                cp.wait_send()
        for i in which:
            for cp in arriving[3 * i:3 * i + 3]:
                cp.wait_recv()

    hbm = lambda a: pltpu.HBM(a.shape, a.dtype)
    outs = pl.pallas_call(
        body, name=name, out_shape=tuple(hbm(a) for a in list(srcs) + list(lands)),
        in_specs=[_HBM] * (2 * n) + [_SEM, _SEM] + [ANY] * len(after), out_specs=tuple([_HBM] * (2 * n)),
        input_output_aliases={i: i for i in range(2 * n)},
        compiler_params=pltpu.CompilerParams(has_side_effects=pltpu.SideEffectType.DATAFLOW_SIDE_EFFECTING),
    )(*srcs, *lands, send, recv, *after)
    return list(outs[:n]), list(outs[n:])


def _chip_reduce(bc_idx, sums, recvd, smalls_slots, smalls_own, name, steps=4):
    ng, ns = len(sums), len(smalls_slots)
    n = ng + ns
    assert steps >= 2
    halves = [g.shape[1] for g in sums] + [s.shape[1] for s in smalls_slots]
    rows = [g.shape[1] // steps for g in sums]

    def body(bc_ref, *refs):
        own, rx = refs[:ng], refs[ng:2 * ng]
        sl = refs[2 * ng:2 * ng + ns]
        sl_own = refs[2 * ng + ns:2 * ng + 2 * ns]
        o = refs[2 * ng + 2 * ns:2 * ng + 2 * ns + n]
        tiles = refs[2 * ng + 2 * ns + n:2 * ng + 2 * ns + 2 * n]
        keep, send, recv = refs[2 * ng + 2 * ns + 2 * n:]
        x, y, c, _ = _coords()
        sibling = dict(device_id=(x, y, 1 - c), device_id_type=MESH)
        r = pl.program_id(0)

        def writes(i, step, slot):
            dst = o[i].at[pl.ds(c * halves[i] + step * rows[i], rows[i])]
            return (pltpu.make_async_copy(tiles[i].at[slot], dst, keep.at[i, slot]),
                    pltpu.make_async_remote_copy(src_ref=tiles[i].at[slot], dst_ref=dst, send_sem=send.at[i, slot],
                                                 recv_sem=recv.at[i, step], **sibling))

        def small_writes(i):
            dst = o[i].at[pl.ds(c * halves[i], halves[i])]
            return (pltpu.make_async_copy(tiles[i], dst, keep.at[i, 0]),
                    pltpu.make_async_remote_copy(src_ref=tiles[i], dst_ref=dst, send_sem=send.at[i, 0],
                                                 recv_sem=recv.at[i, 0], **sibling))

        def arriving(i, step, nrows):
            dst = o[i].at[pl.ds((1 - c) * halves[i] + step * nrows, nrows)]
            return pltpu.make_async_remote_copy(src_ref=dst, dst_ref=dst, send_sem=send.at[i, 0], recv_sem=recv.at[i, step],
                                                **sibling)

        def finish(step, slot):
            for i in range(ng):
                local, remote = writes(i, step, slot)
                local.wait()
                remote.wait_send()

        @pl.when(r >= 2)
        def _():
            finish(r - 2, r % 2)

        for i in range(ng):
            tot = own[i][...]
            for j in range(3):
                tot = tot + rx[i][j].astype(F32)
            tiles[i][r % 2] = tot
            for cp in writes(i, r, r % 2):
                cp.start()

        @pl.when(r == 0)
        def _():
            for i in range(ns):
                term = [jnp.where(bc_ref[0] == kk, sl_own[i][...], sl[i][kk]).astype(F32) for kk in range(N_CHIP)]
                tiles[ng + i][...] = ((term[0] + term[1]) + term[2]) + term[3]
                for cp in small_writes(ng + i):
                    cp.start()

        @pl.when(r == steps - 1)
        def _():
            finish(steps - 2, (steps - 2) % 2)
            finish(steps - 1, (steps - 1) % 2)
            for i in range(ns):
                local, remote = small_writes(ng + i)
                local.wait()
                remote.wait_send()
                arriving(ng + i, 0, halves[ng + i]).wait_recv()
            for i in range(ng):
                for step in range(steps):
                    arriving(i, step, rows[i]).wait_recv()

    in_specs = [pl.BlockSpec((None, rows[i], g.shape[2]), lambda r, bc: (bc[0], r, 0)) for i, g in enumerate(sums)]
    in_specs += [pl.BlockSpec((3, rows[i], g.shape[2]), lambda r, bc: (0, r, 0)) for i, g in enumerate(sums)]
    in_specs += [pl.BlockSpec(s.shape, lambda r, bc: (0, 0, 0)) for s in smalls_slots]
    in_specs += [pl.BlockSpec(s.shape[1:], lambda r, bc: (0, 0)) for s in smalls_slots]
    out_shape = [jax.ShapeDtypeStruct((2 * g.shape[1], g.shape[2]), F32) for g in sums]
    out_shape += [jax.ShapeDtypeStruct((2 * s.shape[1], s.shape[2]), F32) for s in smalls_slots]
    scratch = [pltpu.VMEM((2, rows[i], g.shape[2]), F32) for i, g in enumerate(sums)]
    scratch += [pltpu.VMEM(s.shape[1:], F32) for s in smalls_slots]
    scratch += [pltpu.SemaphoreType.DMA((n, 2)), pltpu.SemaphoreType.DMA((n, 2)), pltpu.SemaphoreType.DMA((n, steps))]
    return list(pl.pallas_call(
        body, out_shape=out_shape,
        grid_spec=pltpu.PrefetchScalarGridSpec(num_scalar_prefetch=1, grid=(steps,), in_specs=in_specs,
                                               out_specs=[ANY] * n, scratch_shapes=scratch),
        compiler_params=_cp(("arbitrary",), VMEM_LIMIT), name=name)(bc_idx, *sums, *recvd, *smalls_slots, *smalls_own))


def _adamw_math(w, g, m, v):
    m2 = ADAM_B1 * m + (1.0 - ADAM_B1) * g
    v2 = ADAM_B2 * v + (1.0 - ADAM_B2) * (g * g)
    m_hat = m2 / (1.0 - ADAM_B1 ** ADAM_STEP)
    v_hat = v2 / (1.0 - ADAM_B2 ** ADAM_STEP)
    delta = -ADAM_LR * (m_hat / (jnp.sqrt(v_hat) + ADAM_EPS) + ADAM_WD * w)
    return delta, m2, v2


def _adamw_big(ws, gs, ms, vs, name, steps=8):
    n = len(ws)

    def body(*refs):
        for i in range(n):
            w_ref, g_ref, m_ref, v_ref = (refs[k * n + i] for k in range(4))
            d_ref, m2_ref, v2_ref, g2_ref = (refs[(4 + k) * n + i] for k in range(4))
            gv = g_ref[...]
            d_ref[...], m2_ref[...], v2_ref[...] = _adamw_math(w_ref[...], gv, m_ref[...], v_ref[...])
            g2_ref[...] = gv

    specs = [pl.BlockSpec((w.shape[0] // steps, w.shape[1]), lambda i: (i, 0)) for w in ws]
    ahead = [pl.BlockSpec((w.shape[0] // steps, w.shape[1]), lambda i: (i, 0), pipeline_mode=pl.Buffered(3)) for w in ws]
    shapes = [jax.ShapeDtypeStruct(w.shape, F32) for w in ws]

    def streamed(*refs):
        pltpu.emit_pipeline(body, grid=(steps,), in_specs=ahead * 4, out_specs=specs * 4)(*refs)

    outs = pl.pallas_call(
        streamed, in_specs=[ANY] * (4 * n), out_specs=[ANY] * (4 * n), out_shape=shapes * 4,
        compiler_params=_cp(None, VMEM_LIMIT), name=name)(*ws, *gs, *ms, *vs)
    return [tuple(outs[k * n + i] for k in range(4)) for i in range(n)]


def _adamw_small(b_idx, sv, sw, vecs, conv, ws):
    nv = len(vecs)
    cols = conv[0].shape[1]

    def body(b_ref, sv_ref, sw_ref, *refs):
        ins, outs = refs[:3 * nv + 6], refs[3 * nv + 6:]
        for i in range(nv):
            g = sv_ref[i:i + 1, :]
            w_ref, m_ref, v_ref = ins[3 * i:3 * i + 3]
            d_ref, m2_ref, v2_ref, g_ref = outs[4 * i:4 * i + 4]
            d_ref[...], m2_ref[...], v2_ref[...] = _adamw_math(w_ref[...], g, m_ref[...], v_ref[...])
            g_ref[...] = g
        g = sv_ref[8:8 + conv[0].shape[0], pl.ds(pl.multiple_of(b_ref[0] * cols, cols), cols)]
        w_ref, m_ref, v_ref = ins[3 * nv:3 * nv + 3]
        d_ref, m2_ref, v2_ref, g_ref = outs[4 * nv:4 * nv + 4]
        d_ref[...], m2_ref[...], v2_ref[...] = _adamw_math(w_ref[...], g, m_ref[...], v_ref[...])
        g_ref[...] = g
        w_ref, m_ref, v_ref = ins[3 * nv + 3:]
        d_ref, m2_ref, v2_ref, g_ref, one_ref = outs[4 * nv + 4:]
        g = sw_ref[...]
        d_ref[...], m2_ref[...], v2_ref[...] = _adamw_math(w_ref[...], g, m_ref[...], v_ref[...])
        g_ref[...] = g
        one_ref[...] = sv_ref[nv:nv + 1, 0:1]

    flat = [a for grp in vecs for a in grp] + list(conv) + list(ws)
    out_shape = [jax.ShapeDtypeStruct(grp[0].shape, F32) for grp in list(vecs) + [conv, ws] for _ in range(4)]
    out_shape += [jax.ShapeDtypeStruct((1, 1), F32)]
    vmem = pl.BlockSpec(memory_space=pltpu.VMEM)
    outs = pl.pallas_call(
        body, out_shape=out_shape, in_specs=[pl.BlockSpec(memory_space=pltpu.SMEM)] + [vmem] * (2 + len(flat)),
        out_specs=[vmem] * len(out_shape), name="adamw_small")(b_idx, sv, sw, *flat)
    return [tuple(outs[4 * i:4 * i + 4]) for i in range(nv + 2)], outs[4 * nv + 8]


def kernel(x, mem, norm_mix_g, w_in, conv_w, gm_ln_g, gm_ln_b, gm_ws, gm_bs, w_out, norm_x_g, norm_mem_g, w_q, w_kv, w_xo, norm_final_g, loss_target, m_norm_mix_g, m_w_in, m_conv_w, m_gm_ln_g, m_gm_ln_b, m_gm_ws, m_gm_bs, m_w_out, m_norm_x_g, m_norm_mem_g, m_w_q, m_w_kv, m_w_xo, m_norm_final_g, v_norm_mix_g, v_w_in, v_conv_w, v_gm_ln_g, v_gm_ln_b, v_gm_ws, v_gm_bs, v_w_out, v_norm_x_g, v_norm_mem_g, v_w_q, v_w_kv, v_w_xo, v_norm_final_g):
    t = x.shape[1]
    xi = lax.axis_index("x")
    yi = lax.axis_index("y")
    ci = lax.axis_index("c")
    b_idx = jnp.reshape(2 * xi + yi, (1,)).astype(jnp.int32)
    c_idx = jnp.reshape(ci, (1,)).astype(jnp.int32)

    x2d, mem2d, tgt = x[0], mem[0], loss_target[0]
    big = [w_in[0], w_out[0], w_q[0], w_kv[0], w_xo[0]]
    big_m = [m_w_in[0], m_w_out[0], m_w_q[0], m_w_kv[0], m_w_xo[0]]
    big_v = [v_w_in[0], v_w_out[0], v_w_q[0], v_w_kv[0], v_w_xo[0]]
    g3 = norm_final_g.reshape(1, D)

    def pad8(a):
        return jnp.pad(a, ((0, 8 - a.shape[0]), (0, 0)))

    own_blocks, (wc, wct, bsb) = _cast_shards(b_idx, big, gm_ws[0], gm_bs[0])

    proj, hb, win_f, cw8, (wq_f,) = _proj_gather(
        b_idx, x2d, norm_mix_g, own_blocks[0], pad8(conv_w[0]), [own_blocks[2]])
    mixin, (wout_f, wkv_f, wxo_f) = _mixer_fwd(
        proj, cw8, gm_ln_g, gm_ln_b, wc, bsb, [own_blocks[1], own_blocks[3], own_blocks[4]])
    wout2, wq2, wxo2 = wout_f.reshape(MIX, D), wq_f.reshape(D, D), wxo_f.reshape(D, D)
    k, v = _mem_fwd(mem2d, norm_mem_g, wkv_f)

    (loss_row, dmix, dx1b, h2b, dq, ob, dx2b, dk, dv, dg2, dg3) = _tail(
        x2d, tgt, mixin, wout2, wq2, wxo2, k, v, norm_x_g, g3)
    dwkv, dwkv_b, dgm = _mem_bwd(mem2d, norm_mem_g, dk, dv, wkv_f)
    dproj, sv, dwc, grad_x = _mixer_bwd(
        proj, dmix, cw8, gm_ln_g, gm_ln_b, wc, wct, bsb, win_f, x2d, dx1b, norm_mix_g, [dg2, dgm, dg3], loss_row)

    bc_idx = jnp.concatenate([b_idx, c_idx])
    sw = dwc.reshape(HEADS * CH, CH)
    dwin_sum, dwin_sum_b, psmall = _grad_matmul_pair(c_idx, hb, dproj, [sv, sw], [F32, BF16], dgm, name="grad_w_in")
    sums_b = [dwin_sum]
    send_b, recv_b, src_b, land_b, token_b = _exchange_begin([dwin_sum_b], psmall, "exchange_b_begin")

    dwxo, dwxo_b = _grad_matmul(ob, dx2b, token_b, name="grad_w_xo", tk=2048)
    dwq, dwq_b = _grad_matmul(h2b, dq, token_b, name="grad_w_q", tk=2048)
    dwout, dwout_b = _grad_matmul(mixin, dx1b, token_b, name="grad_w_out")
    ps_a = _pair_reduce(c_idx, [dwout, dwkv, dwq, dwxo], [dwout_b, dwkv_b, dwq_b, dwxo_b], [], "pair_reduce_a")
    sums_a, sums_a_b = list(ps_a[:4]), list(ps_a[4:8])
    send_a, recv_a, src_a, land_a, token_a = _exchange_begin(sums_a_b, [], "exchange_a_begin")

    src_b, rx2b = _exchange_end(send_b, recv_b, src_b, land_b, 1, [0, 1, 2], [token_a], "exchange_b_end")
    gwin, svf, swf = _chip_reduce(bc_idx, sums_b, rx2b[:1], rx2b[1:], src_b[1:], "chip_reduce_b")
    out_b = _adamw_big(big[:1], [gwin], big_m[:1], big_v[:1], "adamw_w_in")[0]

    row = lambda a: a.reshape(1, D)
    mat = lambda a: a.reshape(HEADS * CH, CH)
    small, loss = _adamw_small(
        b_idx, svf, swf,
        [(row(norm_mix_g), row(m_norm_mix_g), row(v_norm_mix_g)), (row(norm_x_g), row(m_norm_x_g), row(v_norm_x_g)),
         (row(norm_mem_g), row(m_norm_mem_g), row(v_norm_mem_g)), (row(norm_final_g), row(m_norm_final_g), row(v_norm_final_g)),
         (row(gm_ln_g), row(m_gm_ln_g), row(v_gm_ln_g)), (row(gm_ln_b), row(m_gm_ln_b), row(v_gm_ln_b)),
         (row(gm_bs), row(m_gm_bs), row(v_gm_bs))],
        (conv_w[0], m_conv_w[0], v_conv_w[0]), (mat(gm_ws), mat(m_gm_ws), mat(v_gm_ws)))

    def finish_a(part, src, land, after, tag):
        src, land = _exchange_end(send_a, recv_a, src, land, 4, part, after, "exchange_a%s_end" % tag)
        grads = _chip_reduce(bc_idx, [sums_a[i] for i in part], [land[i] for i in part], [], [], "chip_reduce_a" + tag,
                             steps=2)
        ids = [(1, 3, 2, 4)[i] for i in part]
        outs = _adamw_big([big[i] for i in ids], grads, [big_m[i] for i in ids], [big_v[i] for i in ids], "adamw_a" + tag,
                          steps=4)
        return src, land, outs

    src_a, land_a, (out_wout, out_wkv) = finish_a([0, 1], src_a, land_a, [out_b[0], small[0][0]], "1")
    _, _, (out_wq, out_wxo) = finish_a([2, 3], src_a, land_a, [out_wout[0]], "2")

    def unpack(k):
        vec = lambda i: small[i][k]
        return [vec(0), out_b[k][None], small[7][k][None], vec(4), vec(5), small[8][k].reshape(1, HEADS, CH, CH),
                vec(6).reshape(1, HEADS, CH), out_wout[k][None], vec(1), vec(2), out_wq[k][None], out_wkv[k][None],
                out_wxo[k][None], vec(3).reshape(D)]

    return (loss.reshape(()), grad_x[None], *unpack(3), *unpack(0), *unpack(1), *unpack(2))
```

```python
import functools
import math

import jax
import jax.numpy as jnp
from jax import lax
from jax.experimental import pallas as pl
from jax.experimental.pallas import tpu as pltpu

F32 = jnp.float32
BF16 = jnp.bfloat16
MESH = pl.DeviceIdType.MESH

D = 1024
SLAB = 1024
N_SLAB = 7
IN_DIM = N_SLAB * SLAB
MIX = 2 * SLAB
HEADS = 8
CH = 128
XH = 4
XD = D // XH
EPS = 1e-6
GELU_C = math.sqrt(2.0 / math.pi)
GELU_A = 0.044715
N_CHIP = 4
IN_BLK = IN_DIM // N_CHIP
IN_PIECE = 256
N_PIECE = IN_BLK // IN_PIECE
KV_BLK = 2 * D // N_CHIP

ADAM_LR, ADAM_B1, ADAM_B2, ADAM_EPS, ADAM_WD, ADAM_STEP = 0.001, 0.9, 0.999, 1e-08, 0.01, 10

VMEM_LIMIT = 60 * 1024 * 1024


def _cp(sem=None, vmem=None):
    return pltpu.CompilerParams(dimension_semantics=sem, vmem_limit_bytes=vmem)


def _full(shape, buffers=None):
    n = len(shape)
    if buffers is None:
        return pl.BlockSpec(shape, lambda *_: (0,) * n)
    return pl.BlockSpec(shape, lambda *_: (0,) * n, pipeline_mode=pl.Buffered(buffers))


ANY = pl.BlockSpec(memory_space=pl.ANY)


def _bdot(a, b):
    return jnp.dot(a.astype(BF16), b.astype(BF16), preferred_element_type=F32)


def _bdot_nt(a, b):
    return lax.dot_general(a.astype(BF16), b.astype(BF16), (((1,), (1,)), ((), ())), preferred_element_type=F32)


def _bdot_tn(a, b):
    return lax.dot_general(a.astype(BF16), b.astype(BF16), (((0,), (0,)), ((), ())), preferred_element_type=F32)


def _rms(x, g):
    r = lax.rsqrt(jnp.mean(x * x, axis=-1, keepdims=True) + EPS)
    return x * r * g, r


def _rms_bwd(dy, x, r, g):
    gdy = dy * g
    dx = r * gdy - x * (r * r * r) * jnp.mean(x * gdy, axis=-1, keepdims=True)
    dg = jnp.sum(dy * x * r, axis=0, keepdims=True)
    return dx, dg


def _gelu_parts(x):
    x2 = x * x
    t = jnp.tanh(GELU_C * (x + GELU_A * x * x2))
    val = 0.5 * x * (1.0 + t)
    grad = 0.5 * (1.0 + t) + 0.5 * x * (1.0 - t * t) * (GELU_C * (1.0 + 3.0 * GELU_A * x2))
    return val, grad


def _gelu(x):
    return 0.5 * x * (1.0 + jnp.tanh(GELU_C * (x + GELU_A * x * x * x)))


def _sigmoid(z):
    return 1.0 / (1.0 + jnp.exp(-z))


def _cast_shards(b_idx, arrs, gm_ws, gm_bs):
    n = len(arrs)
    steps = 4

    def body(b_ref, *refs):
        ws_ref, bs_ref = refs[n:n + 2]
        outs = refs[n + 2:2 * n + 2]
        wc_ref, wct_ref, bsb_ref = refs[2 * n + 2:]
        for p in range(N_PIECE):
            outs[0][p] = refs[0][:, pl.ds(p * IN_PIECE, IN_PIECE)].astype(BF16)
        for i in range(1, n):
            outs[i][...] = refs[i][...].astype(BF16)

        @pl.when(pl.program_id(0) == 0)
        def _():
            causal = lax.broadcasted_iota(jnp.int32, (CH, CH), 0) >= lax.broadcasted_iota(jnp.int32, (CH, CH), 1)
            for h in range(HEADS):
                w = jnp.where(causal, ws_ref[h], 0.0)
                wc_ref[h] = w.astype(BF16)
                wct_ref[h] = w.T.astype(BF16)
                bsb_ref[h] = jnp.broadcast_to(bs_ref[h:h + 1, :], (CH, CH)).T

    rows = [a.shape[0] // steps for a in arrs]
    in_specs = [pl.BlockSpec((rows[i], a.shape[1]), lambda i, b: (i, 0)) for i, a in enumerate(arrs)]
    in_specs += [pl.BlockSpec((HEADS, CH, CH), lambda i, b: (0, 0, 0)), pl.BlockSpec((HEADS, CH), lambda i, b: (0, 0))]
    out_specs = [pl.BlockSpec((None, N_PIECE, rows[0], IN_PIECE), lambda i, b: (b[0], 0, i, 0))]
    out_specs += [pl.BlockSpec((None, rows[i], a.shape[1]), lambda i, b: (b[0], i, 0)) for i, a in enumerate(arrs) if i > 0]
    out_specs += [pl.BlockSpec((HEADS, CH, CH), lambda i, b: (0, 0, 0))] * 3
    out_shape = [jax.ShapeDtypeStruct((N_CHIP, N_PIECE, arrs[0].shape[0], IN_PIECE), BF16)]
    out_shape += [jax.ShapeDtypeStruct((N_CHIP,) + a.shape, BF16) for a in arrs[1:]]
    out_shape += [jax.ShapeDtypeStruct((HEADS, CH, CH), dt) for dt in (BF16, BF16, F32)]
    outs = pl.pallas_call(
        body, out_shape=out_shape,
        grid_spec=pltpu.PrefetchScalarGridSpec(num_scalar_prefetch=1, grid=(steps,), in_specs=in_specs, out_specs=out_specs),
        compiler_params=_cp(("arbitrary",)), name="cast_shards")(b_idx, *arrs, gm_ws, gm_bs)
    return outs[:n], outs[n:]


def _proj_gather(b_idx, x, g, win_own, cw8s, more, tm=1024):
    t = x.shape[0]
    ni = t // tm
    nm = len(more)
    steps = N_CHIP * N_PIECE
    near0, far0 = N_PIECE, 3 * N_PIECE

    def piece_at(step, own):
        k = step - near0
        near, far = (step >= near0) & (step < far0), step >= far0
        block = jnp.where(far, own ^ 3, jnp.where(near, own ^ jnp.where(lax.rem(k, 2) == 0, 2, 1), own))
        return block, jnp.where(far, step - far0, jnp.where(near, lax.div(k, 2), step))

    def body(*refs):
        b_ref, x_any, g_ref, win_in, cw_in = refs[:5]
        o_ref, hb_any, win_f, cw_out = refs[5 + nm:9 + nm]
        more_out = refs[9 + nm:9 + 2 * nm]
        hbuf, xbuf, wv, cw_s, cw_r, loc = refs[9 + 2 * nm:15 + 2 * nm]
        g_in = _Gather([win_f.at[:, p] for p in range(N_PIECE)], *refs[15 + 2 * nm:19 + 2 * nm])
        g_more = _Gather(more_out, *refs[19 + 2 * nm:23 + 2 * nm])
        s = pl.program_id(0)
        x, y, c, chips = _coords()
        b = 2 * x + y
        blks = [2 * chip[0] + chip[1] for chip in chips]

        def cw_cols(blk):
            return cw_out.at[:, pl.ds(blk * (D // N_CHIP), D // N_CHIP)]

        def cw_copy(k, blk):
            src = cw_in if blk is None else cw_cols(blk)
            return pltpu.make_async_remote_copy(src_ref=src, dst_ref=cw_cols(b if blk is None else blk), send_sem=cw_s.at[k],
                                                recv_sem=cw_r.at[k], device_id=(*chips[k], c), device_id_type=MESH)

        cw_local = pltpu.make_async_copy(cw_in, cw_cols(b), loc.at[1])
        hb_copy = pltpu.make_async_copy(hbuf, hb_any, loc.at[0])

        def load(step):
            slot = lax.rem(step, 2)
            block, piece = piece_at(step, b_ref[0])
            return pltpu.make_async_copy(win_f.at[block, piece], wv.at[slot], loc.at[2 + slot])

        def chunk(i):
            return pltpu.make_async_copy(x_any.at[pl.ds(i * tm, tm)], xbuf.at[i % 2], loc.at[4 + i % 2])

        def first():
            g_in.start()
            cw_local.start()
            for k in range(3):
                cw_copy(k, None).start()
            load(0).start()
            chunk(0).start()
            for i in range(ni):
                if i + 1 < ni:
                    chunk(i + 1).start()
                chunk(i).wait()
                h, _ = _rms(xbuf[i % 2], g_ref[...])
                hbuf[pl.ds(i * tm, tm), :] = h.astype(BF16)
            hb_copy.start()

        events = {step: [] for step in range(steps)}
        events[0].append(first)
        for p in range(N_PIECE):
            events[2 * p + 2].append(functools.partial(g_in.hop, [p]))
            events[near0 + 2 * p - 1].append(functools.partial(g_in.near_ready, [p]))
            events[far0 + p - 2].append(functools.partial(g_in.far, [p]))
            events[far0 + p - 1].append(functools.partial(g_in.far_ready, [p]))
        events[2 * N_PIECE + 1].append(g_more.start)
        for step, todo in events.items():
            if todo:
                @pl.when(s == step)
                def _(todo=todo):
                    for do in todo:
                        do()

        @pl.when(s + 1 < steps)
        def _():
            load(s + 1).start()

        load(s).wait()
        for i in range(ni):
            rows = pl.ds(i * tm, tm)
            o_ref[rows, :] = jnp.dot(hbuf[rows, :], wv[lax.rem(s, 2)], preferred_element_type=F32).astype(BF16)

        @pl.when(s == steps - 1)
        def _():
            g_more.hop()
            g_more.far()
            for k in range(3):
                cw_copy(k, blks[k]).wait_recv()
            for k in range(3):
                cw_copy(k, None).wait_send()
            cw_local.wait()
            hb_copy.wait()
            g_more.near_ready()
            g_more.far_ready()
            g_in.drain()
            g_more.drain()

    def out_col(s, b):
        block, piece = piece_at(s, b[0])
        return 0, block * N_PIECE + piece

    in_specs = [ANY, pl.BlockSpec((1, D), lambda s, b: (0, 0)), ANY, ANY] + [ANY] * nm
    out_specs = [pl.BlockSpec((t, IN_PIECE), out_col), ANY, ANY, ANY] + [ANY] * nm
    outs = pl.pallas_call(
        body, out_shape=[jax.ShapeDtypeStruct((t, IN_DIM), BF16), jax.ShapeDtypeStruct((t, D), BF16),
                         jax.ShapeDtypeStruct(win_own.shape, BF16), jax.ShapeDtypeStruct((8, D), F32)]
        + [jax.ShapeDtypeStruct(f.shape, f.dtype) for f in more],
        grid_spec=pltpu.PrefetchScalarGridSpec(
            num_scalar_prefetch=1, grid=(steps,), in_specs=in_specs, out_specs=out_specs,
            scratch_shapes=[pltpu.VMEM((t, D), BF16), pltpu.VMEM((2, tm, D), F32), pltpu.VMEM((2, D, IN_PIECE), BF16)]
            + [pltpu.SemaphoreType.DMA((3,))] * 2 + [pltpu.SemaphoreType.DMA((6,))]
            + _gather_sems(N_PIECE) + _gather_sems(nm)),
        input_output_aliases={3: 2, **{5 + w: 4 + w for w in range(nm)}},
        compiler_params=_cp(("arbitrary",), VMEM_LIMIT), name="proj_gather")(b_idx, x, g, win_own, cw8s, *more)
    return outs[0], outs[1], outs[2], outs[3], outs[4:]


class _Gather:
    def __init__(self, outs, ici_s, ici_r, d2d_s, d2d_r):
        x, y, c, _ = _coords()
        self.outs, self.c = outs, c
        self.sems = ici_s, ici_r, d2d_s, d2d_r
        self.b, self.bx, self.by, self.bd = 2 * x + y, 2 * (1 - x) + y, 2 * x + (1 - y), 2 * (1 - x) + (1 - y)
        self.xn, self.yn, self.sib = (1 - x, y, c), (x, 1 - y, c), (x, y, 1 - c)

    def piece(self, w, blk, hc, quarter=None):
        hr = self.outs[w].shape[1] // 2
        if quarter is None:
            return self.outs[w].at[blk, pl.ds(hc * hr, hr)]
        return self.outs[w].at[blk, pl.ds(hc * hr + quarter * (hr // 2), hr // 2)]

    def ici(self, w, k, ref, to):
        return pltpu.make_async_remote_copy(src_ref=ref, dst_ref=ref, send_sem=self.sems[0].at[w, k],
                                            recv_sem=self.sems[1].at[w, k], device_id=to, device_id_type=MESH)

    def d2d(self, w, k, ref):
        return pltpu.make_async_remote_copy(src_ref=ref, dst_ref=ref, send_sem=self.sems[2].at[w, k],
                                            recv_sem=self.sems[3].at[w, k], device_id=self.sib, device_id_type=MESH)

    def all(self):
        return range(len(self.outs))

    def start(self):
        for w in self.all():
            mine = self.piece(w, self.b, self.c)
            self.ici(w, 0, mine, self.xn).start()
            self.ici(w, 1, mine, self.yn).start()

    def hop(self, ws=None):
        c = self.c
        for w in ws or self.all():
            self.ici(w, 0, self.piece(w, self.bx, c), self.xn).wait_recv()
            self.ici(w, 1, self.piece(w, self.by, c), self.yn).wait_recv()
            self.ici(w, 2, self.piece(w, self.bx, c, 0), self.yn).start()
            self.ici(w, 3, self.piece(w, self.by, c, 1), self.xn).start()
            self.d2d(w, 0, self.piece(w, self.bx, c)).start()
            self.d2d(w, 1, self.piece(w, self.by, c)).start()

    def near_ready(self, ws=None):
        for w in ws or self.all():
            self.d2d(w, 0, self.piece(w, self.bx, 1 - self.c)).wait_recv()
            self.d2d(w, 1, self.piece(w, self.by, 1 - self.c)).wait_recv()

    def far(self, ws=None):
        c = self.c
        for w in ws or self.all():
            self.ici(w, 2, self.piece(w, self.bd, c, 0), self.yn).wait_recv()
            self.ici(w, 3, self.piece(w, self.bd, c, 1), self.xn).wait_recv()
            self.d2d(w, 2, self.piece(w, self.bd, c, 0)).start()
            self.d2d(w, 3, self.piece(w, self.bd, c, 1)).start()

    def far_ready(self, ws=None):
        for w in ws or self.all():
            self.d2d(w, 2, self.piece(w, self.bd, 1 - self.c, 0)).wait_recv()
            self.d2d(w, 3, self.piece(w, self.bd, 1 - self.c, 1)).wait_recv()

    def drain(self):
        c = self.c
        for w in self.all():
            mine = self.piece(w, self.b, c)
            self.ici(w, 0, mine, self.xn).wait_send()
            self.ici(w, 1, mine, self.yn).wait_send()
            self.ici(w, 2, self.piece(w, self.bx, c, 0), self.yn).wait_send()
            self.ici(w, 3, self.piece(w, self.by, c, 1), self.xn).wait_send()
            self.d2d(w, 0, self.piece(w, self.bx, c)).wait_send()
            self.d2d(w, 1, self.piece(w, self.by, c)).wait_send()
            self.d2d(w, 2, self.piece(w, self.bd, c, 0)).wait_send()
            self.d2d(w, 3, self.piece(w, self.bd, c, 1)).wait_send()


def _gather_sems(nw):
    return [pltpu.SemaphoreType.DMA((max(nw, 1), 4))] * 4


def _mixer_fwd(proj, cw8, lng, lnb, wc, bsb, fulls, tm=256):
    t = proj.shape[0]
    nt = t // tm
    nch = tm // CH
    nw = len(fulls)

    def body(*refs):
        p_ref, cw_ref, lng_ref, lnb_ref, wc_ref, bsb_ref = refs[:6]
        mix_ref = refs[6 + nw]
        w_outs = refs[7 + nw:7 + 2 * nw]
        prev_ref = refs[7 + 2 * nw]
        gather = _Gather(w_outs, *refs[8 + 2 * nw:])

        @pl.when(pl.program_id(0) == 0)
        def _():
            gather.start()
            prev_ref[...] = jnp.zeros_like(prev_ref)

        @pl.when(pl.program_id(0) == nt // 2)
        def _():
            gather.hop()

        @pl.when(pl.program_id(0) == nt - 1)
        def _():
            gather.far()

        rows = lax.broadcasted_iota(jnp.int32, (tm, CH), 0)
        for s in range(HEADS):
            cs = pl.ds(CH * s, CH)

            def slab(k):
                return p_ref[:, pl.ds(k * SLAB + CH * s, CH)].astype(F32)

            gb, gc, xa, za = slab(0), slab(1), slab(2), slab(3)
            cx = gc * xa
            p6 = jnp.broadcast_to(prev_ref[6:7, cs], (tm, CH))
            p7 = jnp.broadcast_to(prev_ref[7:8, cs], (tm, CH))
            c1 = jnp.where(rows == 0, p7, pltpu.roll(cx, 1, 0))
            c2 = jnp.where(rows == 0, p6, jnp.where(rows == 1, p7, pltpu.roll(cx, 2, 0)))
            prev_ref[:, cs] = cx[tm - 8:, :]
            cv = cw_ref[0:1, cs] * c2 + cw_ref[1:2, cs] * c1 + cw_ref[2:3, cs] * cx
            mix_ref[:, cs] = (gb * cv * (za * _sigmoid(za))).astype(BF16)

            u, v, zb = slab(4), slab(5), slab(6)
            ug, vg = _gelu(u), _gelu(v)
            dlt = vg - jnp.mean(vg, axis=-1, keepdims=True)
            vhat = dlt * lax.rsqrt(jnp.mean(dlt * dlt, axis=-1, keepdims=True) + EPS)
            vn = (vhat * lng_ref[:, cs] + lnb_ref[:, cs]).astype(BF16)
            gate = ug * (zb * _sigmoid(zb))
            for c in range(nch):
                rs = slice(CH * c, CH * (c + 1))
                sp = jnp.dot(wc_ref[s], vn[rs], preferred_element_type=F32) + bsb_ref[s]
                mix_ref[rs, pl.ds(SLAB + CH * s, CH)] = (gate[rs] * sp).astype(BF16)

        @pl.when(pl.program_id(0) == nt - 1)
        def _():
            gather.near_ready()
            gather.far_ready()
            gather.drain()

    sems = _gather_sems(nw)
    outs = pl.pallas_call(
        body, grid=(nt,),
        in_specs=[pl.BlockSpec((tm, IN_DIM), lambda i: (i, 0)), _full((8, D)), _full((1, D)), _full((1, D)),
                  _full((HEADS, CH, CH)), _full((HEADS, CH, CH))] + [ANY] * nw,
        out_specs=[pl.BlockSpec((tm, MIX), lambda i: (i, 0))] + [ANY] * nw,
        out_shape=[jax.ShapeDtypeStruct((t, MIX), BF16)] + [jax.ShapeDtypeStruct(f.shape, f.dtype) for f in fulls],
        input_output_aliases={6 + w: 1 + w for w in range(nw)},
        scratch_shapes=[pltpu.VMEM((8, D), F32)] + sems,
        compiler_params=_cp(("arbitrary",), VMEM_LIMIT), name="mixer_fwd")(proj, cw8, lng, lnb, wc, bsb, *fulls)
    return outs[0], outs[1:]


def _mem_fwd(mem, gm, wkv_f):
    n_mem = mem.shape[0]

    def body(mem_ref, gm_ref, w_ref, k_ref, v_ref):
        m, _ = _rms(mem_ref[...], gm_ref[...])
        mb = m.astype(BF16)
        for j in range(N_CHIP):
            dst = k_ref if j < 2 else v_ref
            dst[:, pl.ds(KV_BLK * (j % 2), KV_BLK)] = jnp.dot(mb, w_ref[j], preferred_element_type=F32).astype(BF16)

    return pl.pallas_call(
        body, out_shape=[jax.ShapeDtypeStruct((n_mem, D), BF16), jax.ShapeDtypeStruct((n_mem, D), BF16)],
        compiler_params=_cp(None, VMEM_LIMIT), name="mem_fwd")(mem, gm, wkv_f)


def _tail(x, tgt, mixin, wout, wq, wxo, k, v, g2, g3, tm=512, sub=512):
    t = x.shape[0]
    n_mem = k.shape[0]
    scale = 1.0 / math.sqrt(XD)

    def body(x_ref, tgt_ref, mix_ref, wout_ref, wq_ref, wxo_ref, k_ref, v_ref, g2_ref, g3_ref,
             loss_ref, dmix_ref, dx1b_ref, h2_ref, dq_ref, o_ref, dx2b_ref, dk_ref, dv_ref, dg2_ref, dg3_ref):
        @pl.when(pl.program_id(0) == 0)
        def _():
            loss_ref[...] = jnp.zeros_like(loss_ref)
            dk_ref[...] = jnp.zeros_like(dk_ref)
            dv_ref[...] = jnp.zeros_like(dv_ref)
            dg2_ref[...] = jnp.zeros_like(dg2_ref)
            dg3_ref[...] = jnp.zeros_like(dg3_ref)

        g2, g3 = g2_ref[...], g3_ref[...]
        for sb in range(tm // sub):
            rs = pl.ds(sub * sb, sub)
            x1 = x_ref[rs, :] + jnp.dot(mix_ref[rs, :], wout_ref[...], preferred_element_type=F32)
            h2, r2 = _rms(x1, g2)
            h2b = h2.astype(BF16)
            h2_ref[rs, :] = h2b
            q = jnp.dot(h2b, wq_ref[...], preferred_element_type=F32).astype(BF16)
            probs, outs = [], []
            for hd in range(XH):
                hs = pl.ds(XD * hd, XD)
                s = _bdot_nt(q[:, XD * hd:XD * (hd + 1)], k_ref[:, hs]) * scale
                e = jnp.exp(s - jnp.max(s, axis=-1, keepdims=True))
                p = e / jnp.sum(e, axis=-1, keepdims=True)
                probs.append(p)
                outs.append(_bdot(p, v_ref[:, hs]))
            ob = jnp.concatenate(outs, axis=-1).astype(BF16)
            o_ref[rs, :] = ob
            x2 = x1 + jnp.dot(ob, wxo_ref[...], preferred_element_type=F32)
            y, r3 = _rms(x2, g3)
            diff = y - tgt_ref[rs, :]
            row_loss = jnp.sum(diff * diff, axis=-1, keepdims=True)
            loss_ref[...] += jnp.broadcast_to(jnp.sum(row_loss, axis=0, keepdims=True) * (0.5 / D), loss_ref.shape)

            dx2, dg3 = _rms_bwd(diff * (1.0 / D), x2, r3, g3)
            dg3_ref[...] += dg3
            dx2b = dx2.astype(BF16)
            dx2b_ref[rs, :] = dx2b
            do = _bdot_nt(dx2b, wxo_ref[...])
            dqs = []
            for hd in range(XH):
                hs = pl.ds(XD * hd, XD)
                p = probs[hd]
                do_h = do[:, XD * hd:XD * (hd + 1)]
                dv_ref[:, hs] += _bdot_tn(p, do_h)
                dp = _bdot_nt(do_h, v_ref[:, hs])
                ds = p * (dp - jnp.sum(dp * p, axis=-1, keepdims=True))
                dqs.append(_bdot(ds, k_ref[:, hs]) * scale)
                dk_ref[:, hs] += _bdot_tn(ds, q[:, XD * hd:XD * (hd + 1)]) * scale
            dq = jnp.concatenate(dqs, axis=-1).astype(BF16)
            dq_ref[rs, :] = dq
            dx1n, dg2 = _rms_bwd(_bdot_nt(dq, wq_ref[...]), x1, r2, g2)
            dg2_ref[...] += dg2
            dx1b = (dx2 + dx1n).astype(BF16)
            dx1b_ref[rs, :] = dx1b
            dmix_ref[rs, :] = _bdot_nt(dx1b, wout_ref[...]).astype(BF16)

    tok = lambda w: pl.BlockSpec((tm, w), lambda i: (i, 0))
    return pl.pallas_call(
        body, grid=(t // tm,),
        in_specs=[tok(D), tok(D), tok(MIX), _full((MIX, D), 1), _full((D, D), 1), _full((D, D), 1),
                  _full((n_mem, D), 1), _full((n_mem, D), 1), _full((1, D)), _full((1, D))],
        out_specs=[_full((1, D)), tok(MIX), tok(D), tok(D), tok(D), tok(D), tok(D),
                   _full((n_mem, D)), _full((n_mem, D)), _full((1, D)), _full((1, D))],
        out_shape=[jax.ShapeDtypeStruct((1, D), F32), jax.ShapeDtypeStruct((t, MIX), BF16),
                   jax.ShapeDtypeStruct((t, D), BF16),
                   jax.ShapeDtypeStruct((t, D), BF16), jax.ShapeDtypeStruct((t, D), BF16),
                   jax.ShapeDtypeStruct((t, D), BF16), jax.ShapeDtypeStruct((t, D), BF16),
                   jax.ShapeDtypeStruct((n_mem, D), F32), jax.ShapeDtypeStruct((n_mem, D), F32),
                   jax.ShapeDtypeStruct((1, D), F32), jax.ShapeDtypeStruct((1, D), F32)],
        compiler_params=_cp(("arbitrary",), VMEM_LIMIT), name="tail")(x, tgt, mixin, wout, wq, wxo, k, v, g2, g3)


def _mem_bwd(mem, gm, dk, dv, wkv_f):
    def body(mem_ref, gm_ref, dk_ref, dv_ref, w_ref, dw_ref, dwb_ref, dgm_ref):
        mem_v = mem_ref[...]
        m, rm = _rms(mem_v, gm_ref[...])
        mb = m.astype(BF16)
        dm = jnp.zeros_like(mem_v)
        for j in range(N_CHIP):
            src = dk_ref if j < 2 else dv_ref
            dkv = src[:, pl.ds(KV_BLK * (j % 2), KV_BLK)].astype(BF16)
            dw = _bdot_tn(mb, dkv)
            dw_ref[j] = dw
            dwb_ref[j] = dw.astype(BF16)
            dm = dm + _bdot_nt(dkv, w_ref[j])
        dgm_ref[...] = jnp.sum(dm * mem_v * rm, axis=0, keepdims=True)

    return pl.pallas_call(
        body, out_shape=[jax.ShapeDtypeStruct((N_CHIP, D, KV_BLK), F32), jax.ShapeDtypeStruct((N_CHIP, D, KV_BLK), BF16),
                         jax.ShapeDtypeStruct((1, D), F32)],
        compiler_params=_cp(None, VMEM_LIMIT), name="mem_bwd")(mem, gm, dk, dv, wkv_f)


def _mixer_bwd(proj, dmix, cw8, lng, lnb, wc, wct, bsb, win_f, x, dx1, g1, rows123, row7, tm=256):
    t = proj.shape[0]
    nt = t // tm
    nch = tm // CH
    hb = 16
    pair = 2 * CH
    assert pair == IN_PIECE

    def body(p_ref, pgc_ref, pxa_ref, dm_ref, cw_ref, lng_ref, lnb_ref, wc_ref, wct_ref, bsb_ref, w_ref, x_ref,
             dx1_ref, g1_ref, r1_ref, r2_ref, r3_ref, r7_ref, dp_ref, sv_ref, dwc_ref, gx_ref,
             next_ref, dh_ref):
        i = pl.program_id(0)
        dg1_ref, dlng_ref, dlnb_ref, dbs_ref = (sv_ref.at[pl.ds(r, 1)] for r in (0, 4, 5, 6))
        dcw_ref = sv_ref.at[pl.ds(8, 8)]

        @pl.when(i == 0)
        def _():
            next_ref[...] = jnp.zeros_like(next_ref)
            sv_ref[...] = jnp.zeros_like(sv_ref)
            dwc_ref[...] = jnp.zeros_like(dwc_ref)
            for r, ref in ((1, r1_ref), (2, r2_ref), (3, r3_ref), (7, r7_ref)):
                sv_ref[r:r + 1, :] = ref[...]

        first_tile = i == nt - 1
        rows = lax.broadcasted_iota(jnp.int32, (tm, CH), 0)
        ones8 = jnp.ones((8, CH), BF16)
        for s in range(HEADS):
            cs = pl.ds(CH * s, CH)

            def slab(k):
                return p_ref[:, pl.ds(k * SLAB + CH * s, CH)].astype(F32)

            gb, gc, xa, za = slab(0), slab(1), slab(2), slab(3)
            da = dm_ref[:, cs].astype(F32)
            cx = gc * xa
            cxp = pgc_ref[:, cs].astype(F32) * pxa_ref[:, cs].astype(F32)
            cxp = jnp.where(first_tile, jnp.zeros_like(cxp), cxp)
            p6 = jnp.broadcast_to(cxp[hb - 2:hb - 1, :], (tm, CH))
            p7 = jnp.broadcast_to(cxp[hb - 1:hb, :], (tm, CH))
            c1 = jnp.where(rows == 0, p7, pltpu.roll(cx, 1, 0))
            c2 = jnp.where(rows == 0, p6, jnp.where(rows == 1, p7, pltpu.roll(cx, 2, 0)))
            w0, w1, w2 = cw_ref[0:1, cs], cw_ref[1:2, cs], cw_ref[2:3, cs]
            cv = w0 * c2 + w1 * c1 + w2 * cx
            sg = _sigmoid(za)
            sa = za * sg
            dcv = da * gb * sa
            dp_ref[:, pl.ds(0 * SLAB + CH * s, CH)] = (da * cv * sa).astype(BF16)
            dp_ref[:, pl.ds(3 * SLAB + CH * s, CH)] = (da * gb * cv * (sg * (1.0 + za * (1.0 - sg)))).astype(BF16)
            n0 = jnp.broadcast_to(next_ref[0:1, cs], (tm, CH))
            n1 = jnp.broadcast_to(next_ref[1:2, cs], (tm, CH))
            u1 = jnp.where(rows == tm - 1, n0, pltpu.roll(dcv, tm - 1, 0))
            u2 = jnp.where(rows == tm - 2, n0, jnp.where(rows == tm - 1, n1, pltpu.roll(dcv, tm - 2, 0)))
            next_ref[:, cs] = dcv[0:8, :]
            dcx = w2 * dcv + w1 * u1 + w0 * u2
            dp_ref[:, pl.ds(1 * SLAB + CH * s, CH)] = (dcx * xa).astype(BF16)
            dp_ref[:, pl.ds(2 * SLAB + CH * s, CH)] = (dcx * gc).astype(BF16)
            dcw_ref[0:1, cs] += jnp.sum(dcv * c2, axis=0, keepdims=True)
            dcw_ref[1:2, cs] += jnp.sum(dcv * c1, axis=0, keepdims=True)
            dcw_ref[2:3, cs] += jnp.sum(dcv * cx, axis=0, keepdims=True)

            u, v, zb = slab(4), slab(5), slab(6)
            db = dm_ref[:, pl.ds(SLAB + CH * s, CH)].astype(F32)
            ug, ugrad = _gelu_parts(u)
            vg, vgrad = _gelu_parts(v)
            dlt = vg - jnp.mean(vg, axis=-1, keepdims=True)
            rstd = lax.rsqrt(jnp.mean(dlt * dlt, axis=-1, keepdims=True) + EPS)
            vhat = dlt * rstd
            lg = lng_ref[:, cs]
            vn = (vhat * lg + lnb_ref[:, cs]).astype(BF16)
            sgb = _sigmoid(zb)
            szb = zb * sgb
            sps, dvns = [], []
            dbs = jnp.zeros((8, CH), F32)
            dwc = jnp.zeros((CH, CH), F32)
            for c in range(nch):
                rs = slice(CH * c, CH * (c + 1))
                sp = jnp.dot(wc_ref[s], vn[rs], preferred_element_type=F32) + bsb_ref[s]
                dsp = (db[rs] * ug[rs] * szb[rs]).astype(BF16)
                dbs = dbs + lax.dot_general(ones8, dsp, (((1,), (1,)), ((), ())), preferred_element_type=F32)
                dwc = dwc + lax.dot_general(dsp, vn[rs], (((1,), (1,)), ((), ())), preferred_element_type=F32)
                dvns.append(jnp.dot(wct_ref[s], dsp, preferred_element_type=F32))
                sps.append(sp)
            sp = jnp.concatenate(sps, axis=0)
            dvn = jnp.concatenate(dvns, axis=0)
            dbs_ref[:, cs] += dbs[0:1]
            dwc_ref[s] += dwc
            dlng_ref[:, cs] += jnp.sum(dvn * vhat, axis=0, keepdims=True)
            dlnb_ref[:, cs] += jnp.sum(dvn, axis=0, keepdims=True)
            dvhat = dvn * lg
            dvg = rstd * (dvhat - jnp.mean(dvhat, axis=-1, keepdims=True)
                          - vhat * jnp.mean(dvhat * vhat, axis=-1, keepdims=True))
            dp_ref[:, pl.ds(4 * SLAB + CH * s, CH)] = (db * sp * szb * ugrad).astype(BF16)
            dp_ref[:, pl.ds(5 * SLAB + CH * s, CH)] = (dvg * vgrad).astype(BF16)
            dp_ref[:, pl.ds(6 * SLAB + CH * s, CH)] = (db * ug * sp * (sgb * (1.0 + zb * (1.0 - sgb)))).astype(BF16)

            if s % 2 == 1:
                part = None
                for k in range(N_SLAB):
                    col = k * SLAB + pair * (s // 2)
                    blk, off = divmod(col, IN_BLK)
                    term = lax.dot_general(dp_ref[:, pl.ds(col, pair)], w_ref[blk, off // IN_PIECE],
                                           (((1,), (1,)), ((), ())), preferred_element_type=F32)
                    part = term if part is None else part + term
                if s == 1:
                    dh_ref[...] = part
                else:
                    dh_ref[...] += part

        xv = x_ref[...]
        r = lax.rsqrt(jnp.mean(xv * xv, axis=-1, keepdims=True) + EPS)
        dxn, dg = _rms_bwd(dh_ref[...], xv, r, g1_ref[...])
        gx_ref[...] = dx1_ref[...].astype(F32) + dxn
        dg1_ref[...] += dg

        @pl.when(i == nt - 1)
        def _():
            tril = lax.broadcasted_iota(jnp.int32, (CH, CH), 0) >= lax.broadcasted_iota(jnp.int32, (CH, CH), 1)
            for s in range(HEADS):
                dwc_ref[s] = jnp.where(tril, dwc_ref[s], 0.0)

    rev = lambda i: nt - 1 - i
    halo = lambda col: pl.BlockSpec((hb, SLAB), lambda i: (jnp.maximum(rev(i) * (tm // hb) - 1, 0), col))
    tok = lambda w: pl.BlockSpec((tm, w), lambda i: (rev(i), 0))
    return pl.pallas_call(
        body, grid=(nt,),
        in_specs=[tok(IN_DIM), halo(1), halo(2), tok(MIX), _full((8, D)), _full((1, D)), _full((1, D)),
                  _full((HEADS, CH, CH)), _full((HEADS, CH, CH)), _full((HEADS, CH, CH)),
                  _full((N_CHIP, N_PIECE, D, IN_PIECE), 1), tok(D), tok(D)] + [_full((1, D))] * 5,
        out_specs=[tok(IN_DIM), _full((16, D)), _full((HEADS, CH, CH)), tok(D)],
        out_shape=[jax.ShapeDtypeStruct((t, IN_DIM), BF16), jax.ShapeDtypeStruct((16, D), F32),
                   jax.ShapeDtypeStruct((HEADS, CH, CH), F32), jax.ShapeDtypeStruct((t, D), F32)],
        scratch_shapes=[pltpu.VMEM((8, D), F32), pltpu.VMEM((tm, D), F32)],
        compiler_params=_cp(("arbitrary",), VMEM_LIMIT), name="mixer_bwd")(
            proj, proj, proj, dmix, cw8, lng, lnb, wc, wct, bsb, win_f, x, dx1, g1, *rows123, row7)


def _grad_matmul(a, b, after, *, name, tk=1024):
    t, m = a.shape
    n = b.shape[1]
    nk = t // tk

    def body(a_ref, b_ref, after_ref, o_ref, ob_ref):
        kk = pl.program_id(0)
        part = lax.dot_general(a_ref[...], b_ref[...], (((0,), (0,)), ((), ())), preferred_element_type=F32)

        @pl.when(kk == 0)
        def _():
            o_ref[...] = part

        @pl.when(kk > 0)
        def _():
            o_ref[...] += part

        @pl.when(kk == nk - 1)
        def _():
            ob_ref[...] = o_ref[...].astype(BF16)

    o_spec = pl.BlockSpec((m, n), lambda k: (0, 0))
    o32, o16 = pl.pallas_call(
        body, grid=(nk,), in_specs=[pl.BlockSpec((tk, m), lambda k: (k, 0)), pl.BlockSpec((tk, n), lambda k: (k, 0)), ANY],
        out_specs=[o_spec, o_spec], out_shape=[jax.ShapeDtypeStruct((m, n), F32), jax.ShapeDtypeStruct((m, n), BF16)],
        compiler_params=_cp(("arbitrary",), VMEM_LIMIT), name=name)(a, b, after)
    return o32.reshape(N_CHIP, m // N_CHIP, n), o16.reshape(N_CHIP, m // N_CHIP, n)


def _coords():
    x, y, c = lax.axis_index("x"), lax.axis_index("y"), lax.axis_index("c")
    chips = [(1 - x, y), (x, 1 - y), (1 - x, 1 - y)]
    return x, y, c, chips


def _pair_reduce(c_idx, grads, grads_b, smalls, name):
    ng, ns = len(grads), len(smalls)
    halves = [g.shape[1] // 2 for g in grads]

    def body(c_ref, *refs):
        g_in, gb_any = refs[:ng], refs[ng:2 * ng]
        s_own, s_any = refs[2 * ng:2 * ng + ns], refs[2 * ng + ns:2 * ng + 2 * ns]
        o = refs[2 * ng + 2 * ns:4 * ng + 3 * ns]
        lands = refs[4 * ng + 3 * ns:5 * ng + 4 * ns]
        send, recv = refs[5 * ng + 4 * ns:]
        x, y, c, _ = _coords()
        j = pl.program_id(0)

        def big(i, blk):
            return pltpu.make_async_remote_copy(
                src_ref=gb_any[i].at[blk, pl.ds((1 - c) * halves[i], halves[i])], dst_ref=lands[i].at[blk],
                send_sem=send.at[i, blk], recv_sem=recv.at[i, blk], device_id=(x, y, 1 - c), device_id_type=MESH)

        def small(i):
            return pltpu.make_async_remote_copy(
                src_ref=s_any[i].at[1 - c], dst_ref=lands[ng + i],
                send_sem=send.at[ng + i, 0], recv_sem=recv.at[ng + i, 0], device_id=(x, y, 1 - c), device_id_type=MESH)

        @pl.when(j == 0)
        def _():
            for blk in range(N_CHIP):
                for i in range(ng):
                    big(i, blk).start()
            for i in range(ns):
                small(i).start()

        for i in range(ng):
            big(i, j).wait_recv()
            tot = g_in[i][...] + lands[i][j].astype(F32)
            o[i][...] = tot
            o[ng + i][...] = tot.astype(BF16)

        @pl.when(j == N_CHIP - 1)
        def _():
            for i in range(ns):
                small(i).wait_recv()
                o[2 * ng + i][...] = s_own[i][...] + lands[ng + i][...]
                small(i).wait_send()
            for blk in range(N_CHIP):
                for i in range(ng):
                    big(i, blk).wait_send()

    in_specs = [pl.BlockSpec((None, None, halves[i], g.shape[2]), lambda b, c: (b, c[0], 0, 0)) for i, g in enumerate(grads)]
    in_specs += [ANY] * ng
    in_specs += [pl.BlockSpec((None, s.shape[0] // 2, s.shape[1]), lambda b, c: (c[0], 0, 0)) for s in smalls]
    in_specs += [ANY] * ns
    blk = [pl.BlockSpec((None, halves[i], g.shape[2]), lambda b, c: (b, 0, 0)) for i, g in enumerate(grads)]
    out_specs = blk + blk + [pl.BlockSpec((s.shape[0] // 2, s.shape[1]), lambda b, c: (0, 0)) for s in smalls]
    out_shape = [jax.ShapeDtypeStruct((N_CHIP, halves[i], g.shape[2]), F32) for i, g in enumerate(grads)]
    out_shape += [jax.ShapeDtypeStruct((N_CHIP, halves[i], g.shape[2]), BF16) for i, g in enumerate(grads)]
    out_shape += [jax.ShapeDtypeStruct((s.shape[0] // 2, s.shape[1]), F32) for s in smalls]
    scratch = [pltpu.VMEM((N_CHIP, halves[i], g.shape[2]), BF16) for i, g in enumerate(grads)]
    scratch += [pltpu.VMEM((s.shape[0] // 2, s.shape[1]), F32) for s in smalls]
    scratch += [pltpu.SemaphoreType.DMA((ng + ns, N_CHIP)), pltpu.SemaphoreType.DMA((ng + ns, N_CHIP))]
    grads4 = [g.reshape(N_CHIP, 2, halves[i], g.shape[2]) for i, g in enumerate(grads)]
    smalls3 = [s.reshape(2, s.shape[0] // 2, s.shape[1]) for s in smalls]
    return pl.pallas_call(
        body, out_shape=out_shape,
        grid_spec=pltpu.PrefetchScalarGridSpec(num_scalar_prefetch=1, grid=(N_CHIP,), in_specs=in_specs,
                                               out_specs=out_specs, scratch_shapes=scratch),
        compiler_params=_cp(("arbitrary",), VMEM_LIMIT), name=name)(c_idx, *grads4, *grads_b, *smalls3, *smalls3)


def _grad_matmul_pair(c_idx, a, b, smalls, small_dtypes, after, *, name, tk=2048):
    t, m = a.shape
    bn = b.shape[1] // N_CHIP
    nk = t // tk
    hr = m // 2
    ns = len(smalls)

    def body(c_ref, a_ref, b_ref, *refs):
        s_own, s_any = refs[:ns], refs[ns:2 * ns]
        o32, o16 = refs[2 * ns + 1], refs[2 * ns + 2]
        o_small = refs[2 * ns + 3:3 * ns + 3]
        acc, tb, land, st16 = refs[3 * ns + 3:3 * ns + 7]
        s_land, s_stage = refs[3 * ns + 7:4 * ns + 7], refs[4 * ns + 7:5 * ns + 7]
        send, recv, loc = refs[5 * ns + 7:]
        x, y, c, _ = _coords()
        sibling = dict(device_id=(x, y, 1 - c), device_id_type=MESH)
        j, kk = pl.program_id(0), pl.program_id(1)
        mine = pl.ds(pl.multiple_of(c * hr, hr), hr)
        theirs = pl.ds(pl.multiple_of((1 - c) * hr, hr), hr)

        def to_sibling(blk):
            return pltpu.make_async_remote_copy(src_ref=tb, dst_ref=land.at[blk], send_sem=send.at[blk],
                                                recv_sem=recv.at[blk], **sibling)

        def small(i):
            return pltpu.make_async_remote_copy(src_ref=s_any[i].at[1 - c], dst_ref=s_land[i], send_sem=send.at[N_CHIP + i],
                                                recv_sem=recv.at[N_CHIP + i], **sibling)

        def written(blk):
            return (pltpu.make_async_copy(acc.at[blk % 2, mine], o32.at[blk], loc.at[0]),
                    pltpu.make_async_copy(st16, o16.at[blk], loc.at[1]))

        def finish(blk):
            to_sibling(blk).wait_recv()

            @pl.when(blk > 0)
            def _():
                for cp in written(blk - 1):
                    cp.wait()

            tot = acc[blk % 2, mine, :] + land[blk].astype(F32)
            acc[blk % 2, mine, :] = tot
            st16[...] = tot.astype(BF16)
            for cp in written(blk):
                cp.start()

        def small_out(i):
            return pltpu.make_async_copy(s_stage[i], o_small[i], loc.at[2 + i])

        @pl.when((j == 0) & (kk == 0))
        def _():
            for i in range(ns):
                small(i).start()

        @pl.when((j == 1) & (kk == 0))
        def _():
            for i in range(ns):
                small(i).wait_recv()
                s_stage[i][...] = (s_own[i][...] + s_land[i][...]).astype(small_dtypes[i])
                small_out(i).start()

        @pl.when((j > 0) & (kk == 0))
        def _():
            finish(j - 1)

        part = lax.dot_general(a_ref[...], b_ref[...], (((0,), (0,)), ((), ())), preferred_element_type=F32)
        slot = lax.rem(j, 2)

        @pl.when(kk == 0)
        def _():
            acc[slot] = part

        @pl.when(kk > 0)
        def _():
            acc[slot] += part

        @pl.when(kk == nk - 1)
        def _():
            @pl.when(j > 0)
            def _():
                to_sibling(j - 1).wait_send()

            tb[...] = acc[slot, theirs, :].astype(BF16)
            to_sibling(j).start()

        @pl.when((j == N_CHIP - 1) & (kk == nk - 1))
        def _():
            finish(j)
            for i in range(ns):
                small_out(i).wait()
                small(i).wait_send()
            for cp in written(j):
                cp.wait()
            to_sibling(j).wait_send()

    halves = [(s.shape[0] // 2, s.shape[1]) for s in smalls]
    in_specs = [pl.BlockSpec((tk, m), lambda j, k, c: (k, 0)), pl.BlockSpec((tk, bn), lambda j, k, c: (k, j))]
    in_specs += [pl.BlockSpec((None,) + h, lambda j, k, c: (c[0], 0, 0)) for h in halves] + [ANY] * ns + [ANY]
    out_shape = [jax.ShapeDtypeStruct((N_CHIP, hr, bn), F32), jax.ShapeDtypeStruct((N_CHIP, hr, bn), BF16)]
    out_shape += [pltpu.HBM(h, dt) for h, dt in zip(halves, small_dtypes)]
    scratch = [pltpu.VMEM((2, m, bn), F32), pltpu.VMEM((hr, bn), BF16),
               pltpu.VMEM((N_CHIP, hr, bn), BF16), pltpu.VMEM((hr, bn), BF16)]
    scratch += [pltpu.VMEM(h, F32) for h in halves] + [pltpu.VMEM(h, dt) for h, dt in zip(halves, small_dtypes)]
    scratch += [pltpu.SemaphoreType.DMA((N_CHIP + ns,)), pltpu.SemaphoreType.DMA((N_CHIP + ns,)),
                pltpu.SemaphoreType.DMA((2 + ns,))]
    smalls3 = [s.reshape((2,) + h) for s, h in zip(smalls, halves)]
    outs = pl.pallas_call(
        body, out_shape=out_shape,
        grid_spec=pltpu.PrefetchScalarGridSpec(num_scalar_prefetch=1, grid=(N_CHIP, nk), in_specs=in_specs,
                                               out_specs=[ANY, ANY] + [_HBM] * ns, scratch_shapes=scratch),
        compiler_params=_cp(("arbitrary", "arbitrary"), VMEM_LIMIT), name=name)(c_idx, a, b, *smalls3, *smalls3, after)
    return outs[0], outs[1], list(outs[2:])


_HBM = pl.BlockSpec(memory_space=pltpu.HBM)
_SEM = pl.BlockSpec(memory_space=pltpu.SEMAPHORE)


def _split_copies(ins, lands, ng, send, recv, arriving):
    x, y, c, chips = _coords()
    b = 2 * x + y
    copies = []
    for i in range(len(ins)):
        for k in range(3):
            blk = 2 * chips[k][0] + chips[k][1]
            src, dst, got = (ins[i].at[blk], lands[i].at[k], lands[i].at[k]) if i < ng else (ins[i], lands[i].at[b], lands[i].at[blk])
            sems = dict(send_sem=send.at[3 * i + k], recv_sem=recv.at[3 * i + k], device_id=(*chips[k], c), device_id_type=MESH)
            if arriving:
                copies.append(pltpu.make_async_remote_copy(src_ref=got, dst_ref=got, **sems))
            else:
                copies.append(pltpu.make_async_remote_copy(src_ref=src, dst_ref=dst, **sems))
    return copies


def _exchange_begin(sums_b, smalls, name):
    ng, n = len(sums_b), len(sums_b) + len(smalls)
    srcs = list(sums_b) + list(smalls)
    lands = [lax.empty((3,) + g.shape[1:], g.dtype) for g in sums_b] + [lax.empty((N_CHIP,) + s.shape, s.dtype) for s in smalls]

    def body(*refs):
        ins, land_refs = refs[:n], refs[n:2 * n]
        send, recv = refs[2 * n], refs[2 * n + 1]
        token = refs[4 * n + 2]
        for cp in _split_copies(ins, land_refs, ng, send, recv, False):
            cp.start()
        token[...] = jnp.zeros_like(token)

    hbm = lambda a: pltpu.HBM(a.shape, a.dtype)
    outs = pl.pallas_call(
        body, name=name,
        out_shape=(pltpu.SemaphoreType.DMA((3 * n,)), pltpu.SemaphoreType.DMA((3 * n,)), *[hbm(a) for a in srcs + lands],
                   jax.ShapeDtypeStruct((8, 128), F32)),
        in_specs=[_HBM] * (2 * n), out_specs=(_SEM, _SEM, *[_HBM] * (2 * n), pl.BlockSpec(memory_space=pltpu.VMEM)),
        input_output_aliases={i: 2 + i for i in range(2 * n)},
        compiler_params=pltpu.CompilerParams(has_side_effects=pltpu.SideEffectType.DATAFLOW_SIDE_EFFECTING),
    )(*[pltpu.with_memory_space_constraint(a, pltpu.HBM) for a in srcs + lands])
    return outs[0], outs[1], list(outs[2:2 + n]), list(outs[2 + n:2 + 2 * n]), outs[2 + 2 * n]


def _exchange_end(send, recv, srcs, lands, ng, which, after, name):
    n = len(srcs)
    after = list(after)

    def body(*refs):
        ins, land_refs = refs[:n], refs[n:2 * n]
        send_ref, recv_ref = refs[2 * n], refs[2 * n + 1]
        outgoing = _split_copies(ins, land_refs, ng, send_ref, recv_ref, False)
        arriving = _split_copies(ins, land_refs, ng, send_ref, recv_ref, True)
        for i in which:
            for cp in outgoing[3 * i:3 * i + 3]:
                cp.wait_send()
        for i in which:
            for cp in arriving[3 * i:3 * i + 3]:
                cp.wait_recv()

    hbm = lambda a: pltpu.HBM(a.shape, a.dtype)
    outs = pl.pallas_call(
        body, name=name, out_shape=tuple(hbm(a) for a in list(srcs) + list(lands)),
        in_specs=[_HBM] * (2 * n) + [_SEM, _SEM] + [ANY] * len(after), out_specs=tuple([_HBM] * (2 * n)),
        input_output_aliases={i: i for i in range(2 * n)},
        compiler_params=pltpu.CompilerParams(has_side_effects=pltpu.SideEffectType.DATAFLOW_SIDE_EFFECTING),
    )(*srcs, *lands, send, recv, *after)
    return list(outs[:n]), list(outs[n:])


def _chip_reduce(bc_idx, sums, recvd, smalls_slots, smalls_own, name, steps=4):
    ng, ns = len(sums), len(smalls_slots)
    n = ng + ns
    assert steps >= 2
    halves = [g.shape[1] for g in sums] + [s.shape[1] for s in smalls_slots]
    rows = [g.shape[1] // steps for g in sums]

    def body(bc_ref, *refs):
        own, rx = refs[:ng], refs[ng:2 * ng]
        sl = refs[2 * ng:2 * ng + ns]
        sl_own = refs[2 * ng + ns:2 * ng + 2 * ns]
        o = refs[2 * ng + 2 * ns:2 * ng + 2 * ns + n]
        tiles = refs[2 * ng + 2 * ns + n:2 * ng + 2 * ns + 2 * n]
        keep, send, recv = refs[2 * ng + 2 * ns + 2 * n:]
        x, y, c, _ = _coords()
        sibling = dict(device_id=(x, y, 1 - c), device_id_type=MESH)
        r = pl.program_id(0)

        def writes(i, step, slot):
            dst = o[i].at[pl.ds(c * halves[i] + step * rows[i], rows[i])]
            return (pltpu.make_async_copy(tiles[i].at[slot], dst, keep.at[i, slot]),
                    pltpu.make_async_remote_copy(src_ref=tiles[i].at[slot], dst_ref=dst, send_sem=send.at[i, slot],
                                                 recv_sem=recv.at[i, step], **sibling))

        def small_writes(i):
            dst = o[i].at[pl.ds(c * halves[i], halves[i])]
            return (pltpu.make_async_copy(tiles[i], dst, keep.at[i, 0]),
                    pltpu.make_async_remote_copy(src_ref=tiles[i], dst_ref=dst, send_sem=send.at[i, 0],
                                                 recv_sem=recv.at[i, 0], **sibling))

        def arriving(i, step, nrows):
            dst = o[i].at[pl.ds((1 - c) * halves[i] + step * nrows, nrows)]
            return pltpu.make_async_remote_copy(src_ref=dst, dst_ref=dst, send_sem=send.at[i, 0], recv_sem=recv.at[i, step],
                                                **sibling)

        def finish(step, slot):
            for i in range(ng):
                local, remote = writes(i, step, slot)
                local.wait()
                remote.wait_send()

        @pl.when(r >= 2)
        def _():
            finish(r - 2, r % 2)

        for i in range(ng):
            tot = own[i][...]
            for j in range(3):
                tot = tot + rx[i][j].astype(F32)
            tiles[i][r % 2] = tot
            for cp in writes(i, r, r % 2):
                cp.start()

        @pl.when(r == 0)
        def _():
            for i in range(ns):
                term = [jnp.where(bc_ref[0] == kk, sl_own[i][...], sl[i][kk]).astype(F32) for kk in range(N_CHIP)]
                tiles[ng + i][...] = ((term[0] + term[1]) + term[2]) + term[3]
                for cp in small_writes(ng + i):
                    cp.start()

        @pl.when(r == steps - 1)
        def _():
            finish(steps - 2, (steps - 2) % 2)
            finish(steps - 1, (steps - 1) % 2)
            for i in range(ns):
                local, remote = small_writes(ng + i)
                local.wait()
                remote.wait_send()
                arriving(ng + i, 0, halves[ng + i]).wait_recv()
            for i in range(ng):
                for step in range(steps):
                    arriving(i, step, rows[i]).wait_recv()

    in_specs = [pl.BlockSpec((None, rows[i], g.shape[2]), lambda r, bc: (bc[0], r, 0)) for i, g in enumerate(sums)]
    in_specs += [pl.BlockSpec((3, rows[i], g.shape[2]), lambda r, bc: (0, r, 0)) for i, g in enumerate(sums)]
    in_specs += [pl.BlockSpec(s.shape, lambda r, bc: (0, 0, 0)) for s in smalls_slots]
    in_specs += [pl.BlockSpec(s.shape[1:], lambda r, bc: (0, 0)) for s in smalls_slots]
    out_shape = [jax.ShapeDtypeStruct((2 * g.shape[1], g.shape[2]), F32) for g in sums]
    out_shape += [jax.ShapeDtypeStruct((2 * s.shape[1], s.shape[2]), F32) for s in smalls_slots]
    scratch = [pltpu.VMEM((2, rows[i], g.shape[2]), F32) for i, g in enumerate(sums)]
    scratch += [pltpu.VMEM(s.shape[1:], F32) for s in smalls_slots]
    scratch += [pltpu.SemaphoreType.DMA((n, 2)), pltpu.SemaphoreType.DMA((n, 2)), pltpu.SemaphoreType.DMA((n, steps))]
    return list(pl.pallas_call(
        body, out_shape=out_shape,
        grid_spec=pltpu.PrefetchScalarGridSpec(num_scalar_prefetch=1, grid=(steps,), in_specs=in_specs,
                                               out_specs=[ANY] * n, scratch_shapes=scratch),
        compiler_params=_cp(("arbitrary",), VMEM_LIMIT), name=name)(bc_idx, *sums, *recvd, *smalls_slots, *smalls_own))


def _adamw_math(w, g, m, v):
    m2 = ADAM_B1 * m + (1.0 - ADAM_B1) * g
    v2 = ADAM_B2 * v + (1.0 - ADAM_B2) * (g * g)
    m_hat = m2 / (1.0 - ADAM_B1 ** ADAM_STEP)
    v_hat = v2 / (1.0 - ADAM_B2 ** ADAM_STEP)
    delta = -ADAM_LR * (m_hat / (jnp.sqrt(v_hat) + ADAM_EPS) + ADAM_WD * w)
    return delta, m2, v2


def _adamw_big(ws, gs, ms, vs, name, steps=8):
    n = len(ws)

    def body(*refs):
        for i in range(n):
            w_ref, g_ref, m_ref, v_ref = (refs[k * n + i] for k in range(4))
            d_ref, m2_ref, v2_ref, g2_ref = (refs[(4 + k) * n + i] for k in range(4))
            gv = g_ref[...]
            d_ref[...], m2_ref[...], v2_ref[...] = _adamw_math(w_ref[...], gv, m_ref[...], v_ref[...])
            g2_ref[...] = gv

    specs = [pl.BlockSpec((w.shape[0] // steps, w.shape[1]), lambda i: (i, 0)) for w in ws]
    ahead = [pl.BlockSpec((w.shape[0] // steps, w.shape[1]), lambda i: (i, 0), pipeline_mode=pl.Buffered(4)) for w in ws]
    shapes = [jax.ShapeDtypeStruct(w.shape, F32) for w in ws]

    def streamed(*refs):
        pltpu.emit_pipeline(body, grid=(steps,), in_specs=ahead * 4, out_specs=specs * 4)(*refs)

    outs = pl.pallas_call(
        streamed, in_specs=[ANY] * (4 * n), out_specs=[ANY] * (4 * n), out_shape=shapes * 4,
        compiler_params=_cp(None, VMEM_LIMIT), name=name)(*ws, *gs, *ms, *vs)
    return [tuple(outs[k * n + i] for k in range(4)) for i in range(n)]


def _adamw_small(b_idx, sv, sw, vecs, conv, ws):
    nv = len(vecs)
    cols = conv[0].shape[1]

    def body(b_ref, sv_ref, sw_ref, *refs):
        ins, outs = refs[:3 * nv + 6], refs[3 * nv + 6:]
        for i in range(nv):
            g = sv_ref[i:i + 1, :]
            w_ref, m_ref, v_ref = ins[3 * i:3 * i + 3]
            d_ref, m2_ref, v2_ref, g_ref = outs[4 * i:4 * i + 4]
            d_ref[...], m2_ref[...], v2_ref[...] = _adamw_math(w_ref[...], g, m_ref[...], v_ref[...])
            g_ref[...] = g
        g = sv_ref[8:8 + conv[0].shape[0], pl.ds(pl.multiple_of(b_ref[0] * cols, cols), cols)]
        w_ref, m_ref, v_ref = ins[3 * nv:3 * nv + 3]
        d_ref, m2_ref, v2_ref, g_ref = outs[4 * nv:4 * nv + 4]
        d_ref[...], m2_ref[...], v2_ref[...] = _adamw_math(w_ref[...], g, m_ref[...], v_ref[...])
        g_ref[...] = g
        w_ref, m_ref, v_ref = ins[3 * nv + 3:]
        d_ref, m2_ref, v2_ref, g_ref, one_ref = outs[4 * nv + 4:]
        g = sw_ref[...]
        d_ref[...], m2_ref[...], v2_ref[...] = _adamw_math(w_ref[...], g, m_ref[...], v_ref[...])
        g_ref[...] = g
        one_ref[...] = sv_ref[nv:nv + 1, 0:1]

    flat = [a for grp in vecs for a in grp] + list(conv) + list(ws)
    out_shape = [jax.ShapeDtypeStruct(grp[0].shape, F32) for grp in list(vecs) + [conv, ws] for _ in range(4)]
    out_shape += [jax.ShapeDtypeStruct((1, 1), F32)]
    vmem = pl.BlockSpec(memory_space=pltpu.VMEM)
    outs = pl.pallas_call(
        body, out_shape=out_shape, in_specs=[pl.BlockSpec(memory_space=pltpu.SMEM)] + [vmem] * (2 + len(flat)),
        out_specs=[vmem] * len(out_shape), name="adamw_small")(b_idx, sv, sw, *flat)
    return [tuple(outs[4 * i:4 * i + 4]) for i in range(nv + 2)], outs[4 * nv + 8]


def kernel(x, mem, norm_mix_g, w_in, conv_w, gm_ln_g, gm_ln_b, gm_ws, gm_bs, w_out, norm_x_g, norm_mem_g, w_q, w_kv, w_xo, norm_final_g, loss_target, m_norm_mix_g, m_w_in, m_conv_w, m_gm_ln_g, m_gm_ln_b, m_gm_ws, m_gm_bs, m_w_out, m_norm_x_g, m_norm_mem_g, m_w_q, m_w_kv, m_w_xo, m_norm_final_g, v_norm_mix_g, v_w_in, v_conv_w, v_gm_ln_g, v_gm_ln_b, v_gm_ws, v_gm_bs, v_w_out, v_norm_x_g, v_norm_mem_g, v_w_q, v_w_kv, v_w_xo, v_norm_final_g):
    t = x.shape[1]
    xi = lax.axis_index("x")
    yi = lax.axis_index("y")
    ci = lax.axis_index("c")
    b_idx = jnp.reshape(2 * xi + yi, (1,)).astype(jnp.int32)
    c_idx = jnp.reshape(ci, (1,)).astype(jnp.int32)

    x2d, mem2d, tgt = x[0], mem[0], loss_target[0]
    big = [w_in[0], w_out[0], w_q[0], w_kv[0], w_xo[0]]
    big_m = [m_w_in[0], m_w_out[0], m_w_q[0], m_w_kv[0], m_w_xo[0]]
    big_v = [v_w_in[0], v_w_out[0], v_w_q[0], v_w_kv[0], v_w_xo[0]]
    g3 = norm_final_g.reshape(1, D)

    def pad8(a):
        return jnp.pad(a, ((0, 8 - a.shape[0]), (0, 0)))

    own_blocks, (wc, wct, bsb) = _cast_shards(b_idx, big, gm_ws[0], gm_bs[0])

    proj, hb, win_f, cw8, (wq_f,) = _proj_gather(
        b_idx, x2d, norm_mix_g, own_blocks[0], pad8(conv_w[0]), [own_blocks[2]])
    mixin, (wout_f, wkv_f, wxo_f) = _mixer_fwd(
        proj, cw8, gm_ln_g, gm_ln_b, wc, bsb, [own_blocks[1], own_blocks[3], own_blocks[4]])
    wout2, wq2, wxo2 = wout_f.reshape(MIX, D), wq_f.reshape(D, D), wxo_f.reshape(D, D)
    k, v = _mem_fwd(mem2d, norm_mem_g, wkv_f)

    (loss_row, dmix, dx1b, h2b, dq, ob, dx2b, dk, dv, dg2, dg3) = _tail(
        x2d, tgt, mixin, wout2, wq2, wxo2, k, v, norm_x_g, g3)
    dwkv, dwkv_b, dgm = _mem_bwd(mem2d, norm_mem_g, dk, dv, wkv_f)
    dproj, sv, dwc, grad_x = _mixer_bwd(
        proj, dmix, cw8, gm_ln_g, gm_ln_b, wc, wct, bsb, win_f, x2d, dx1b, norm_mix_g, [dg2, dgm, dg3], loss_row)

    bc_idx = jnp.concatenate([b_idx, c_idx])
    sw = dwc.reshape(HEADS * CH, CH)
    dwin_sum, dwin_sum_b, psmall = _grad_matmul_pair(c_idx, hb, dproj, [sv, sw], [F32, BF16], dgm, name="grad_w_in")
    sums_b = [dwin_sum]
    send_b, recv_b, src_b, land_b, token_b = _exchange_begin([dwin_sum_b], psmall, "exchange_b_begin")

    dwxo, dwxo_b = _grad_matmul(ob, dx2b, token_b, name="grad_w_xo", tk=2048)
    dwq, dwq_b = _grad_matmul(h2b, dq, token_b, name="grad_w_q", tk=2048)
    dwout, dwout_b = _grad_matmul(mixin, dx1b, token_b, name="grad_w_out")
    ps_a = _pair_reduce(c_idx, [dwout, dwkv, dwq, dwxo], [dwout_b, dwkv_b, dwq_b, dwxo_b], [], "pair_reduce_a")
    sums_a, sums_a_b = list(ps_a[:4]), list(ps_a[4:8])
    send_a, recv_a, src_a, land_a, token_a = _exchange_begin(sums_a_b, [], "exchange_a_begin")

    src_b, rx2b = _exchange_end(send_b, recv_b, src_b, land_b, 1, [0, 1, 2], [token_a], "exchange_b_end")
    gwin, svf, swf = _chip_reduce(bc_idx, sums_b, rx2b[:1], rx2b[1:], src_b[1:], "chip_reduce_b")
    out_b = _adamw_big(big[:1], [gwin], big_m[:1], big_v[:1], "adamw_w_in")[0]

    row = lambda a: a.reshape(1, D)
    mat = lambda a: a.reshape(HEADS * CH, CH)
    small, loss = _adamw_small(
        b_idx, svf, swf,
        [(row(norm_mix_g), row(m_norm_mix_g), row(v_norm_mix_g)), (row(norm_x_g), row(m_norm_x_g), row(v_norm_x_g)),
         (row(norm_mem_g), row(m_norm_mem_g), row(v_norm_mem_g)), (row(norm_final_g), row(m_norm_final_g), row(v_norm_final_g)),
         (row(gm_ln_g), row(m_gm_ln_g), row(v_gm_ln_g)), (row(gm_ln_b), row(m_gm_ln_b), row(v_gm_ln_b)),
         (row(gm_bs), row(m_gm_bs), row(v_gm_bs))],
        (conv_w[0], m_conv_w[0], v_conv_w[0]), (mat(gm_ws), mat(m_gm_ws), mat(v_gm_ws)))

    def finish_a(part, src, land, after, tag):
        src, land = _exchange_end(send_a, recv_a, src, land, 4, part, after, "exchange_a%s_end" % tag)
        grads = _chip_reduce(bc_idx, [sums_a[i] for i in part], [land[i] for i in part], [], [], "chip_reduce_a" + tag,
                             steps=2)
        ids = [(1, 3, 2, 4)[i] for i in part]
        outs = _adamw_big([big[i] for i in ids], grads, [big_m[i] for i in ids], [big_v[i] for i in ids], "adamw_a" + tag,
                          steps=4)
        return src, land, outs

    src_a, land_a, (out_wout, out_wkv) = finish_a([0, 1], src_a, land_a, [out_b[0], small[0][0]], "1")
    _, _, (out_wq, out_wxo) = finish_a([2, 3], src_a, land_a, [out_wout[0]], "2")

    def unpack(k):
        vec = lambda i: small[i][k]
        return [vec(0), out_b[k][None], small[7][k][None], vec(4), vec(5), small[8][k].reshape(1, HEADS, CH, CH),
                vec(6).reshape(1, HEADS, CH), out_wout[k][None], vec(1), vec(2), out_wq[k][None], out_wkv[k][None],
                out_wxo[k][None], vec(3).reshape(D)]

    return (loss.reshape(()), grad_x[None], *unpack(3), *unpack(0), *unpack(1), *unpack(2))
```

```python
import functools
import math

import jax
import jax.numpy as jnp
from jax import lax
from jax.experimental import pallas as pl
from jax.experimental.pallas import tpu as pltpu

F32 = jnp.float32
BF16 = jnp.bfloat16
MESH = pl.DeviceIdType.MESH

D = 1024
SLAB = 1024
N_SLAB = 7
IN_DIM = N_SLAB * SLAB
MIX = 2 * SLAB
HEADS = 8
CH = 128
XH = 4
XD = D // XH
EPS = 1e-6
GELU_C = math.sqrt(2.0 / math.pi)
GELU_A = 0.044715
N_CHIP = 4
IN_BLK = IN_DIM // N_CHIP
IN_PIECE = 256
N_PIECE = IN_BLK // IN_PIECE
KV_BLK = 2 * D // N_CHIP

ADAM_LR, ADAM_B1, ADAM_B2, ADAM_EPS, ADAM_WD, ADAM_STEP = 0.001, 0.9, 0.999, 1e-08, 0.01, 10

VMEM_LIMIT = 60 * 1024 * 1024


def _cp(sem=None, vmem=None):
    return pltpu.CompilerParams(dimension_semantics=sem, vmem_limit_bytes=vmem)


def _full(shape, buffers=None):
    n = len(shape)
    if buffers is None:
        return pl.BlockSpec(shape, lambda *_: (0,) * n)
    return pl.BlockSpec(shape, lambda *_: (0,) * n, pipeline_mode=pl.Buffered(buffers))


ANY = pl.BlockSpec(memory_space=pl.ANY)


def _bdot(a, b):
    return jnp.dot(a.astype(BF16), b.astype(BF16), preferred_element_type=F32)


def _bdot_nt(a, b):
    return lax.dot_general(a.astype(BF16), b.astype(BF16), (((1,), (1,)), ((), ())), preferred_element_type=F32)


def _bdot_tn(a, b):
    return lax.dot_general(a.astype(BF16), b.astype(BF16), (((0,), (0,)), ((), ())), preferred_element_type=F32)


def _rms(x, g):
    r = lax.rsqrt(jnp.mean(x * x, axis=-1, keepdims=True) + EPS)
    return x * r * g, r


def _rms_bwd(dy, x, r, g):
    gdy = dy * g
    dx = r * gdy - x * (r * r * r) * jnp.mean(x * gdy, axis=-1, keepdims=True)
    dg = jnp.sum(dy * x * r, axis=0, keepdims=True)
    return dx, dg


def _gelu_parts(x):
    x2 = x * x
    t = jnp.tanh(GELU_C * (x + GELU_A * x * x2))
    val = 0.5 * x * (1.0 + t)
    grad = 0.5 * (1.0 + t) + 0.5 * x * (1.0 - t * t) * (GELU_C * (1.0 + 3.0 * GELU_A * x2))
    return val, grad


def _gelu(x):
    return 0.5 * x * (1.0 + jnp.tanh(GELU_C * (x + GELU_A * x * x * x)))


def _sigmoid(z):
    return 1.0 / (1.0 + jnp.exp(-z))


def _cast_shards(b_idx, arrs, gm_ws, gm_bs):
    n = len(arrs)
    steps = 4

    def body(b_ref, *refs):
        ws_ref, bs_ref = refs[n:n + 2]
        outs = refs[n + 2:2 * n + 2]
        wc_ref, wct_ref, bsb_ref = refs[2 * n + 2:]
        for p in range(N_PIECE):
            outs[0][p] = refs[0][:, pl.ds(p * IN_PIECE, IN_PIECE)].astype(BF16)
        for i in range(1, n):
            outs[i][...] = refs[i][...].astype(BF16)

        @pl.when(pl.program_id(0) == 0)
        def _():
            causal = lax.broadcasted_iota(jnp.int32, (CH, CH), 0) >= lax.broadcasted_iota(jnp.int32, (CH, CH), 1)
            for h in range(HEADS):
                w = jnp.where(causal, ws_ref[h], 0.0)
                wc_ref[h] = w.astype(BF16)
                wct_ref[h] = w.T.astype(BF16)
                bsb_ref[h] = jnp.broadcast_to(bs_ref[h:h + 1, :], (CH, CH)).T

    rows = [a.shape[0] // steps for a in arrs]
    in_specs = [pl.BlockSpec((rows[i], a.shape[1]), lambda i, b: (i, 0)) for i, a in enumerate(arrs)]
    in_specs += [pl.BlockSpec((HEADS, CH, CH), lambda i, b: (0, 0, 0)), pl.BlockSpec((HEADS, CH), lambda i, b: (0, 0))]
    out_specs = [pl.BlockSpec((None, N_PIECE, rows[0], IN_PIECE), lambda i, b: (b[0], 0, i, 0))]
    out_specs += [pl.BlockSpec((None, rows[i], a.shape[1]), lambda i, b: (b[0], i, 0)) for i, a in enumerate(arrs) if i > 0]
    out_specs += [pl.BlockSpec((HEADS, CH, CH), lambda i, b: (0, 0, 0))] * 3
    out_shape = [jax.ShapeDtypeStruct((N_CHIP, N_PIECE, arrs[0].shape[0], IN_PIECE), BF16)]
    out_shape += [jax.ShapeDtypeStruct((N_CHIP,) + a.shape, BF16) for a in arrs[1:]]
    out_shape += [jax.ShapeDtypeStruct((HEADS, CH, CH), dt) for dt in (BF16, BF16, F32)]
    outs = pl.pallas_call(
        body, out_shape=out_shape,
        grid_spec=pltpu.PrefetchScalarGridSpec(num_scalar_prefetch=1, grid=(steps,), in_specs=in_specs, out_specs=out_specs),
        compiler_params=_cp(("arbitrary",)), name="cast_shards")(b_idx, *arrs, gm_ws, gm_bs)
    return outs[:n], outs[n:]


def _proj_gather(b_idx, x, g, win_own, cw8s, more, tm=1024):
    t = x.shape[0]
    ni = t // tm
    nm = len(more)
    steps = N_CHIP * N_PIECE
    near0, far0 = N_PIECE, 3 * N_PIECE

    def piece_at(step, own):
        k = step - near0
        near, far = (step >= near0) & (step < far0), step >= far0
        block = jnp.where(far, own ^ 3, jnp.where(near, own ^ jnp.where(lax.rem(k, 2) == 0, 2, 1), own))
        return block, jnp.where(far, step - far0, jnp.where(near, lax.div(k, 2), step))

    def body(*refs):
        b_ref, x_any, g_ref, win_in, cw_in = refs[:5]
        o_ref, hb_any, win_f, cw_out = refs[5 + nm:9 + nm]
        more_out = refs[9 + nm:9 + 2 * nm]
        hbuf, xbuf, wv, cw_s, cw_r, loc = refs[9 + 2 * nm:15 + 2 * nm]
        g_in = _Gather([win_f.at[:, p] for p in range(N_PIECE)], *refs[15 + 2 * nm:19 + 2 * nm])
        g_more = _Gather(more_out, *refs[19 + 2 * nm:23 + 2 * nm])
        s = pl.program_id(0)
        x, y, c, chips = _coords()
        b = 2 * x + y
        blks = [2 * chip[0] + chip[1] for chip in chips]

        def cw_cols(blk):
            return cw_out.at[:, pl.ds(blk * (D // N_CHIP), D // N_CHIP)]

        def cw_copy(k, blk):
            src = cw_in if blk is None else cw_cols(blk)
            return pltpu.make_async_remote_copy(src_ref=src, dst_ref=cw_cols(b if blk is None else blk), send_sem=cw_s.at[k],
                                                recv_sem=cw_r.at[k], device_id=(*chips[k], c), device_id_type=MESH)

        cw_local = pltpu.make_async_copy(cw_in, cw_cols(b), loc.at[1])
        hb_copy = pltpu.make_async_copy(hbuf, hb_any, loc.at[0])

        def load(step):
            slot = lax.rem(step, 2)
            block, piece = piece_at(step, b_ref[0])
            return pltpu.make_async_copy(win_f.at[block, piece], wv.at[slot], loc.at[2 + slot])

        def chunk(i):
            return pltpu.make_async_copy(x_any.at[pl.ds(i * tm, tm)], xbuf.at[i % 2], loc.at[4 + i % 2])

        def first():
            g_in.start()
            cw_local.start()
            for k in range(3):
                cw_copy(k, None).start()
            load(0).start()
            chunk(0).start()
            for i in range(ni):
                if i + 1 < ni:
                    chunk(i + 1).start()
                chunk(i).wait()
                h, _ = _rms(xbuf[i % 2], g_ref[...])
                hbuf[pl.ds(i * tm, tm), :] = h.astype(BF16)
            hb_copy.start()

        events = {step: [] for step in range(steps)}
        events[0].append(first)
        for p in range(N_PIECE):
            events[2 * p + 2].append(functools.partial(g_in.hop, [p]))
            events[near0 + 2 * p - 1].append(functools.partial(g_in.near_ready, [p]))
            events[far0 + p - 2].append(functools.partial(g_in.far, [p]))
            events[far0 + p - 1].append(functools.partial(g_in.far_ready, [p]))
        events[2 * N_PIECE + 1].append(g_more.start)
        for step, todo in events.items():
            if todo:
                @pl.when(s == step)
                def _(todo=todo):
                    for do in todo:
                        do()

        @pl.when(s + 1 < steps)
        def _():
            load(s + 1).start()

        load(s).wait()
        for i in range(ni):
            rows = pl.ds(i * tm, tm)
            o_ref[rows, :] = jnp.dot(hbuf[rows, :], wv[lax.rem(s, 2)], preferred_element_type=F32).astype(BF16)

        @pl.when(s == steps - 1)
        def _():
            g_more.hop()
            g_more.far()
            for k in range(3):
                cw_copy(k, blks[k]).wait_recv()
            for k in range(3):
                cw_copy(k, None).wait_send()
            cw_local.wait()
            hb_copy.wait()
            g_more.near_ready()
            g_more.far_ready()
            g_in.drain()
            g_more.drain()

    def out_col(s, b):
        block, piece = piece_at(s, b[0])
        return 0, block * N_PIECE + piece

    in_specs = [ANY, pl.BlockSpec((1, D), lambda s, b: (0, 0)), ANY, ANY] + [ANY] * nm
    out_specs = [pl.BlockSpec((t, IN_PIECE), out_col), ANY, ANY, ANY] + [ANY] * nm
    outs = pl.pallas_call(
        body, out_shape=[jax.ShapeDtypeStruct((t, IN_DIM), BF16), jax.ShapeDtypeStruct((t, D), BF16),
                         jax.ShapeDtypeStruct(win_own.shape, BF16), jax.ShapeDtypeStruct((8, D), F32)]
        + [jax.ShapeDtypeStruct(f.shape, f.dtype) for f in more],
        grid_spec=pltpu.PrefetchScalarGridSpec(
            num_scalar_prefetch=1, grid=(steps,), in_specs=in_specs, out_specs=out_specs,
            scratch_shapes=[pltpu.VMEM((t, D), BF16), pltpu.VMEM((2, tm, D), F32), pltpu.VMEM((2, D, IN_PIECE), BF16)]
            + [pltpu.SemaphoreType.DMA((3,))] * 2 + [pltpu.SemaphoreType.DMA((6,))]
            + _gather_sems(N_PIECE) + _gather_sems(nm)),
        input_output_aliases={3: 2, **{5 + w: 4 + w for w in range(nm)}},
        compiler_params=_cp(("arbitrary",), VMEM_LIMIT), name="proj_gather")(b_idx, x, g, win_own, cw8s, *more)
    return outs[0], outs[1], outs[2], outs[3], outs[4:]


class _Gather:
    def __init__(self, outs, ici_s, ici_r, d2d_s, d2d_r):
        x, y, c, _ = _coords()
        self.outs, self.c = outs, c
        self.sems = ici_s, ici_r, d2d_s, d2d_r
        self.b, self.bx, self.by, self.bd = 2 * x + y, 2 * (1 - x) + y, 2 * x + (1 - y), 2 * (1 - x) + (1 - y)
        self.xn, self.yn, self.sib = (1 - x, y, c), (x, 1 - y, c), (x, y, 1 - c)

    def piece(self, w, blk, hc, quarter=None):
        hr = self.outs[w].shape[1] // 2
        if quarter is None:
            return self.outs[w].at[blk, pl.ds(hc * hr, hr)]
        return self.outs[w].at[blk, pl.ds(hc * hr + quarter * (hr // 2), hr // 2)]

    def ici(self, w, k, ref, to):
        return pltpu.make_async_remote_copy(src_ref=ref, dst_ref=ref, send_sem=self.sems[0].at[w, k],
                                            recv_sem=self.sems[1].at[w, k], device_id=to, device_id_type=MESH)

    def d2d(self, w, k, ref):
        return pltpu.make_async_remote_copy(src_ref=ref, dst_ref=ref, send_sem=self.sems[2].at[w, k],
                                            recv_sem=self.sems[3].at[w, k], device_id=self.sib, device_id_type=MESH)

    def all(self):
        return range(len(self.outs))

    def start(self):
        for w in self.all():
            mine = self.piece(w, self.b, self.c)
            self.ici(w, 0, mine, self.xn).start()
            self.ici(w, 1, mine, self.yn).start()

    def hop(self, ws=None):
        c = self.c
        for w in ws or self.all():
            self.ici(w, 0, self.piece(w, self.bx, c), self.xn).wait_recv()
            self.ici(w, 1, self.piece(w, self.by, c), self.yn).wait_recv()
            self.ici(w, 2, self.piece(w, self.bx, c, 0), self.yn).start()
            self.ici(w, 3, self.piece(w, self.by, c, 1), self.xn).start()
            self.d2d(w, 0, self.piece(w, self.bx, c)).start()
            self.d2d(w, 1, self.piece(w, self.by, c)).start()

    def near_ready(self, ws=None):
        for w in ws or self.all():
            self.d2d(w, 0, self.piece(w, self.bx, 1 - self.c)).wait_recv()
            self.d2d(w, 1, self.piece(w, self.by, 1 - self.c)).wait_recv()

    def far(self, ws=None):
        c = self.c
        for w in ws or self.all():
            self.ici(w, 2, self.piece(w, self.bd, c, 0), self.yn).wait_recv()
            self.ici(w, 3, self.piece(w, self.bd, c, 1), self.xn).wait_recv()
            self.d2d(w, 2, self.piece(w, self.bd, c, 0)).start()
            self.d2d(w, 3, self.piece(w, self.bd, c, 1)).start()

    def far_ready(self, ws=None):
        for w in ws or self.all():
            self.d2d(w, 2, self.piece(w, self.bd, 1 - self.c, 0)).wait_recv()
            self.d2d(w, 3, self.piece(w, self.bd, 1 - self.c, 1)).wait_recv()

    def drain(self):
        c = self.c
        for w in self.all():
            mine = self.piece(w, self.b, c)
            self.ici(w, 0, mine, self.xn).wait_send()
            self.ici(w, 1, mine, self.yn).wait_send()
            self.ici(w, 2, self.piece(w, self.bx, c, 0), self.yn).wait_send()
            self.ici(w, 3, self.piece(w, self.by, c, 1), self.xn).wait_send()
            self.d2d(w, 0, self.piece(w, self.bx, c)).wait_send()
            self.d2d(w, 1, self.piece(w, self.by, c)).wait_send()
            self.d2d(w, 2, self.piece(w, self.bd, c, 0)).wait_send()
            self.d2d(w, 3, self.piece(w, self.bd, c, 1)).wait_send()


def _gather_sems(nw):
    return [pltpu.SemaphoreType.DMA((max(nw, 1), 4))] * 4


def _mixer_fwd(proj, cw8, lng, lnb, wc, bsb, fulls, tm=256):
    t = proj.shape[0]
    nt = t // tm
    nch = tm // CH
    nw = len(fulls)

    def body(*refs):
        p_ref, cw_ref, lng_ref, lnb_ref, wc_ref, bsb_ref = refs[:6]
        mix_ref = refs[6 + nw]
        w_outs = refs[7 + nw:7 + 2 * nw]
        prev_ref = refs[7 + 2 * nw]
        gather = _Gather(w_outs, *refs[8 + 2 * nw:])

        @pl.when(pl.program_id(0) == 0)
        def _():
            gather.start()
            prev_ref[...] = jnp.zeros_like(prev_ref)

        @pl.when(pl.program_id(0) == nt // 2)
        def _():
            gather.hop()

        @pl.when(pl.program_id(0) == nt - 1)
        def _():
            gather.far()

        rows = lax.broadcasted_iota(jnp.int32, (tm, CH), 0)
        for s in range(HEADS):
            cs = pl.ds(CH * s, CH)

            def slab(k):
                return p_ref[:, pl.ds(k * SLAB + CH * s, CH)].astype(F32)

            gb, gc, xa, za = slab(0), slab(1), slab(2), slab(3)
            cx = gc * xa
            p6 = jnp.broadcast_to(prev_ref[6:7, cs], (tm, CH))
            p7 = jnp.broadcast_to(prev_ref[7:8, cs], (tm, CH))
            c1 = jnp.where(rows == 0, p7, pltpu.roll(cx, 1, 0))
            c2 = jnp.where(rows == 0, p6, jnp.where(rows == 1, p7, pltpu.roll(cx, 2, 0)))
            prev_ref[:, cs] = cx[tm - 8:, :]
            cv = cw_ref[0:1, cs] * c2 + cw_ref[1:2, cs] * c1 + cw_ref[2:3, cs] * cx
            mix_ref[:, cs] = (gb * cv * (za * _sigmoid(za))).astype(BF16)

            u, v, zb = slab(4), slab(5), slab(6)
            ug, vg = _gelu(u), _gelu(v)
            dlt = vg - jnp.mean(vg, axis=-1, keepdims=True)
            vhat = dlt * lax.rsqrt(jnp.mean(dlt * dlt, axis=-1, keepdims=True) + EPS)
            vn = (vhat * lng_ref[:, cs] + lnb_ref[:, cs]).astype(BF16)
            gate = ug * (zb * _sigmoid(zb))
            for c in range(nch):
                rs = slice(CH * c, CH * (c + 1))
                sp = jnp.dot(wc_ref[s], vn[rs], preferred_element_type=F32) + bsb_ref[s]
                mix_ref[rs, pl.ds(SLAB + CH * s, CH)] = (gate[rs] * sp).astype(BF16)

        @pl.when(pl.program_id(0) == nt - 1)
        def _():
            gather.near_ready()
            gather.far_ready()
            gather.drain()

    sems = _gather_sems(nw)
    outs = pl.pallas_call(
        body, grid=(nt,),
        in_specs=[pl.BlockSpec((tm, IN_DIM), lambda i: (i, 0)), _full((8, D)), _full((1, D)), _full((1, D)),
                  _full((HEADS, CH, CH)), _full((HEADS, CH, CH))] + [ANY] * nw,
        out_specs=[pl.BlockSpec((tm, MIX), lambda i: (i, 0))] + [ANY] * nw,
        out_shape=[jax.ShapeDtypeStruct((t, MIX), BF16)] + [jax.ShapeDtypeStruct(f.shape, f.dtype) for f in fulls],
        input_output_aliases={6 + w: 1 + w for w in range(nw)},
        scratch_shapes=[pltpu.VMEM((8, D), F32)] + sems,
        compiler_params=_cp(("arbitrary",), VMEM_LIMIT), name="mixer_fwd")(proj, cw8, lng, lnb, wc, bsb, *fulls)
    return outs[0], outs[1:]


def _mem_fwd(mem, gm, wkv_f):
    n_mem = mem.shape[0]

    def body(mem_ref, gm_ref, w_ref, k_ref, v_ref):
        m, _ = _rms(mem_ref[...], gm_ref[...])
        mb = m.astype(BF16)
        for j in range(N_CHIP):
            dst = k_ref if j < 2 else v_ref
            dst[:, pl.ds(KV_BLK * (j % 2), KV_BLK)] = jnp.dot(mb, w_ref[j], preferred_element_type=F32).astype(BF16)

    return pl.pallas_call(
        body, out_shape=[jax.ShapeDtypeStruct((n_mem, D), BF16), jax.ShapeDtypeStruct((n_mem, D), BF16)],
        compiler_params=_cp(None, VMEM_LIMIT), name="mem_fwd")(mem, gm, wkv_f)


def _tail(x, tgt, mixin, wout, wq, wxo, k, v, g2, g3, tm=512, sub=512):
    t = x.shape[0]
    n_mem = k.shape[0]
    scale = 1.0 / math.sqrt(XD)

    def body(x_ref, tgt_ref, mix_ref, wout_ref, wq_ref, wxo_ref, k_ref, v_ref, g2_ref, g3_ref,
             loss_ref, dmix_ref, dx1b_ref, h2_ref, dq_ref, o_ref, dx2b_ref, dk_ref, dv_ref, dg2_ref, dg3_ref):
        @pl.when(pl.program_id(0) == 0)
        def _():
            loss_ref[...] = jnp.zeros_like(loss_ref)
            dk_ref[...] = jnp.zeros_like(dk_ref)
            dv_ref[...] = jnp.zeros_like(dv_ref)
            dg2_ref[...] = jnp.zeros_like(dg2_ref)
            dg3_ref[...] = jnp.zeros_like(dg3_ref)

        g2, g3 = g2_ref[...], g3_ref[...]
        for sb in range(tm // sub):
            rs = pl.ds(sub * sb, sub)
            x1 = x_ref[rs, :] + jnp.dot(mix_ref[rs, :], wout_ref[...], preferred_element_type=F32)
            h2, r2 = _rms(x1, g2)
            h2b = h2.astype(BF16)
            h2_ref[rs, :] = h2b
            q = jnp.dot(h2b, wq_ref[...], preferred_element_type=F32).astype(BF16)
            probs, outs = [], []
            for hd in range(XH):
                hs = pl.ds(XD * hd, XD)
                s = _bdot_nt(q[:, XD * hd:XD * (hd + 1)], k_ref[:, hs]) * scale
                e = jnp.exp(s - jnp.max(s, axis=-1, keepdims=True))
                p = e / jnp.sum(e, axis=-1, keepdims=True)
                probs.append(p)
                outs.append(_bdot(p, v_ref[:, hs]))
            ob = jnp.concatenate(outs, axis=-1).astype(BF16)
            o_ref[rs, :] = ob
            x2 = x1 + jnp.dot(ob, wxo_ref[...], preferred_element_type=F32)
            y, r3 = _rms(x2, g3)
            diff = y - tgt_ref[rs, :]
            row_loss = jnp.sum(diff * diff, axis=-1, keepdims=True)
            loss_ref[...] += jnp.broadcast_to(jnp.sum(row_loss, axis=0, keepdims=True) * (0.5 / D), loss_ref.shape)

            dx2, dg3 = _rms_bwd(diff * (1.0 / D), x2, r3, g3)
            dg3_ref[...] += dg3
            dx2b = dx2.astype(BF16)
            dx2b_ref[rs, :] = dx2b
            do = _bdot_nt(dx2b, wxo_ref[...])
            dqs = []
            for hd in range(XH):
                hs = pl.ds(XD * hd, XD)
                p = probs[hd]
                do_h = do[:, XD * hd:XD * (hd + 1)]
                dv_ref[:, hs] += _bdot_tn(p, do_h)
                dp = _bdot_nt(do_h, v_ref[:, hs])
                ds = p * (dp - jnp.sum(dp * p, axis=-1, keepdims=True))
                dqs.append(_bdot(ds, k_ref[:, hs]) * scale)
                dk_ref[:, hs] += _bdot_tn(ds, q[:, XD * hd:XD * (hd + 1)]) * scale
            dq = jnp.concatenate(dqs, axis=-1).astype(BF16)
            dq_ref[rs, :] = dq
            dx1n, dg2 = _rms_bwd(_bdot_nt(dq, wq_ref[...]), x1, r2, g2)
            dg2_ref[...] += dg2
            dx1b = (dx2 + dx1n).astype(BF16)
            dx1b_ref[rs, :] = dx1b
            dmix_ref[rs, :] = _bdot_nt(dx1b, wout_ref[...]).astype(BF16)

    tok = lambda w: pl.BlockSpec((tm, w), lambda i: (i, 0))
    return pl.pallas_call(
        body, grid=(t // tm,),
        in_specs=[tok(D), tok(D), tok(MIX), _full((MIX, D), 1), _full((D, D), 1), _full((D, D), 1),
                  _full((n_mem, D), 1), _full((n_mem, D), 1), _full((1, D)), _full((1, D))],
        out_specs=[_full((1, D)), tok(MIX), tok(D), tok(D), tok(D), tok(D), tok(D),
                   _full((n_mem, D)), _full((n_mem, D)), _full((1, D)), _full((1, D))],
        out_shape=[jax.ShapeDtypeStruct((1, D), F32), jax.ShapeDtypeStruct((t, MIX), BF16),
                   jax.ShapeDtypeStruct((t, D), BF16),
                   jax.ShapeDtypeStruct((t, D), BF16), jax.ShapeDtypeStruct((t, D), BF16),
                   jax.ShapeDtypeStruct((t, D), BF16), jax.ShapeDtypeStruct((t, D), BF16),
                   jax.ShapeDtypeStruct((n_mem, D), F32), jax.ShapeDtypeStruct((n_mem, D), F32),
                   jax.ShapeDtypeStruct((1, D), F32), jax.ShapeDtypeStruct((1, D), F32)],
        compiler_params=_cp(("arbitrary",), VMEM_LIMIT), name="tail")(x, tgt, mixin, wout, wq, wxo, k, v, g2, g3)


def _mem_bwd(mem, gm, dk, dv, wkv_f):
    def body(mem_ref, gm_ref, dk_ref, dv_ref, w_ref, dw_any, dwb_any, dgm_ref, dw_ref, dwb_ref, sem):
        mem_v = mem_ref[...]
        m, rm = _rms(mem_v, gm_ref[...])
        mb = m.astype(BF16)
        dm = jnp.zeros_like(mem_v)
        written = []
        for j in range(N_CHIP):
            src = dk_ref if j < 2 else dv_ref
            dkv = src[:, pl.ds(KV_BLK * (j % 2), KV_BLK)].astype(BF16)
            dw = _bdot_tn(mb, dkv)
            dw_ref[j] = dw
            dwb_ref[j] = dw.astype(BF16)
            written += [pltpu.make_async_copy(dw_ref.at[j], dw_any.at[j], sem.at[0, j]),
                        pltpu.make_async_copy(dwb_ref.at[j], dwb_any.at[j], sem.at[1, j])]
            for cp in written[-2:]:
                cp.start()
            dm = dm + _bdot_nt(dkv, w_ref[j])
        dgm_ref[...] = jnp.sum(dm * mem_v * rm, axis=0, keepdims=True)
        for cp in written:
            cp.wait()

    vmem = pl.BlockSpec(memory_space=pltpu.VMEM)
    return pl.pallas_call(
        body, out_shape=[jax.ShapeDtypeStruct((N_CHIP, D, KV_BLK), F32), jax.ShapeDtypeStruct((N_CHIP, D, KV_BLK), BF16),
                         jax.ShapeDtypeStruct((1, D), F32)],
        in_specs=[vmem] * 5, out_specs=[ANY, ANY, vmem],
        scratch_shapes=[pltpu.VMEM((N_CHIP, D, KV_BLK), F32), pltpu.VMEM((N_CHIP, D, KV_BLK), BF16),
                        pltpu.SemaphoreType.DMA((2, N_CHIP))],
        compiler_params=_cp(None, VMEM_LIMIT), name="mem_bwd")(mem, gm, dk, dv, wkv_f)


def _mixer_bwd(proj, dmix, cw8, lng, lnb, wc, wct, bsb, win_f, x, dx1, g1, rows123, row7, tm=256):
    t = proj.shape[0]
    nt = t // tm
    nch = tm // CH
    hb = 16
    pair = 2 * CH
    assert pair == IN_PIECE

    def body(p_ref, pgc_ref, pxa_ref, dm_ref, cw_ref, lng_ref, lnb_ref, wc_ref, wct_ref, bsb_ref, w_ref, x_ref,
             dx1_ref, g1_ref, r1_ref, r2_ref, r3_ref, r7_ref, dp_ref, sv_ref, dwc_ref, gx_ref,
             next_ref, dh_ref):
        i = pl.program_id(0)
        dg1_ref, dlng_ref, dlnb_ref, dbs_ref = (sv_ref.at[pl.ds(r, 1)] for r in (0, 4, 5, 6))
        dcw_ref = sv_ref.at[pl.ds(8, 8)]

        @pl.when(i == 0)
        def _():
            next_ref[...] = jnp.zeros_like(next_ref)
            sv_ref[...] = jnp.zeros_like(sv_ref)
            dwc_ref[...] = jnp.zeros_like(dwc_ref)
            for r, ref in ((1, r1_ref), (2, r2_ref), (3, r3_ref), (7, r7_ref)):
                sv_ref[r:r + 1, :] = ref[...]

        first_tile = i == nt - 1
        rows = lax.broadcasted_iota(jnp.int32, (tm, CH), 0)
        ones8 = jnp.ones((8, CH), BF16)
        for s in range(HEADS):
            cs = pl.ds(CH * s, CH)

            def slab(k):
                return p_ref[:, pl.ds(k * SLAB + CH * s, CH)].astype(F32)

            gb, gc, xa, za = slab(0), slab(1), slab(2), slab(3)
            da = dm_ref[:, cs].astype(F32)
            cx = gc * xa
            cxp = pgc_ref[:, cs].astype(F32) * pxa_ref[:, cs].astype(F32)
            cxp = jnp.where(first_tile, jnp.zeros_like(cxp), cxp)
            p6 = jnp.broadcast_to(cxp[hb - 2:hb - 1, :], (tm, CH))
            p7 = jnp.broadcast_to(cxp[hb - 1:hb, :], (tm, CH))
            c1 = jnp.where(rows == 0, p7, pltpu.roll(cx, 1, 0))
            c2 = jnp.where(rows == 0, p6, jnp.where(rows == 1, p7, pltpu.roll(cx, 2, 0)))
            w0, w1, w2 = cw_ref[0:1, cs], cw_ref[1:2, cs], cw_ref[2:3, cs]
            cv = w0 * c2 + w1 * c1 + w2 * cx
            sg = _sigmoid(za)
            sa = za * sg
            dcv = da * gb * sa
            dp_ref[:, pl.ds(0 * SLAB + CH * s, CH)] = (da * cv * sa).astype(BF16)
            dp_ref[:, pl.ds(3 * SLAB + CH * s, CH)] = (da * gb * cv * (sg * (1.0 + za * (1.0 - sg)))).astype(BF16)
            n0 = jnp.broadcast_to(next_ref[0:1, cs], (tm, CH))
            n1 = jnp.broadcast_to(next_ref[1:2, cs], (tm, CH))
            u1 = jnp.where(rows == tm - 1, n0, pltpu.roll(dcv, tm - 1, 0))
            u2 = jnp.where(rows == tm - 2, n0, jnp.where(rows == tm - 1, n1, pltpu.roll(dcv, tm - 2, 0)))
            next_ref[:, cs] = dcv[0:8, :]
            dcx = w2 * dcv + w1 * u1 + w0 * u2
            dp_ref[:, pl.ds(1 * SLAB + CH * s, CH)] = (dcx * xa).astype(BF16)
            dp_ref[:, pl.ds(2 * SLAB + CH * s, CH)] = (dcx * gc).astype(BF16)
            dcw_ref[0:1, cs] += jnp.sum(dcv * c2, axis=0, keepdims=True)
            dcw_ref[1:2, cs] += jnp.sum(dcv * c1, axis=0, keepdims=True)
            dcw_ref[2:3, cs] += jnp.sum(dcv * cx, axis=0, keepdims=True)

            u, v, zb = slab(4), slab(5), slab(6)
            db = dm_ref[:, pl.ds(SLAB + CH * s, CH)].astype(F32)
            ug, ugrad = _gelu_parts(u)
            vg, vgrad = _gelu_parts(v)
            dlt = vg - jnp.mean(vg, axis=-1, keepdims=True)
            rstd = lax.rsqrt(jnp.mean(dlt * dlt, axis=-1, keepdims=True) + EPS)
            vhat = dlt * rstd
            lg = lng_ref[:, cs]
            vn = (vhat * lg + lnb_ref[:, cs]).astype(BF16)
            sgb = _sigmoid(zb)
            szb = zb * sgb
            sps, dvns = [], []
            dbs = jnp.zeros((8, CH), F32)
            dwc = jnp.zeros((CH, CH), F32)
            for c in range(nch):
                rs = slice(CH * c, CH * (c + 1))
                sp = jnp.dot(wc_ref[s], vn[rs], preferred_element_type=F32) + bsb_ref[s]
                dsp = (db[rs] * ug[rs] * szb[rs]).astype(BF16)
                dbs = dbs + lax.dot_general(ones8, dsp, (((1,), (1,)), ((), ())), preferred_element_type=F32)
                dwc = dwc + lax.dot_general(dsp, vn[rs], (((1,), (1,)), ((), ())), preferred_element_type=F32)
                dvns.append(jnp.dot(wct_ref[s], dsp, preferred_element_type=F32))
                sps.append(sp)
            sp = jnp.concatenate(sps, axis=0)
            dvn = jnp.concatenate(dvns, axis=0)
            dbs_ref[:, cs] += dbs[0:1]
            dwc_ref[s] += dwc
            dlng_ref[:, cs] += jnp.sum(dvn * vhat, axis=0, keepdims=True)
            dlnb_ref[:, cs] += jnp.sum(dvn, axis=0, keepdims=True)
            dvhat = dvn * lg
            dvg = rstd * (dvhat - jnp.mean(dvhat, axis=-1, keepdims=True)
                          - vhat * jnp.mean(dvhat * vhat, axis=-1, keepdims=True))
            dp_ref[:, pl.ds(4 * SLAB + CH * s, CH)] = (db * sp * szb * ugrad).astype(BF16)
            dp_ref[:, pl.ds(5 * SLAB + CH * s, CH)] = (dvg * vgrad).astype(BF16)
            dp_ref[:, pl.ds(6 * SLAB + CH * s, CH)] = (db * ug * sp * (sgb * (1.0 + zb * (1.0 - sgb)))).astype(BF16)

            if s % 2 == 1:
                part = None
                for k in range(N_SLAB):
                    col = k * SLAB + pair * (s // 2)
                    blk, off = divmod(col, IN_BLK)
                    term = lax.dot_general(dp_ref[:, pl.ds(col, pair)], w_ref[blk, off // IN_PIECE],
                                           (((1,), (1,)), ((), ())), preferred_element_type=F32)
                    part = term if part is None else part + term
                if s == 1:
                    dh_ref[...] = part
                else:
                    dh_ref[...] += part

        xv = x_ref[...]
        r = lax.rsqrt(jnp.mean(xv * xv, axis=-1, keepdims=True) + EPS)
        dxn, dg = _rms_bwd(dh_ref[...], xv, r, g1_ref[...])
        gx_ref[...] = dx1_ref[...].astype(F32) + dxn
        dg1_ref[...] += dg

        @pl.when(i == nt - 1)
        def _():
            tril = lax.broadcasted_iota(jnp.int32, (CH, CH), 0) >= lax.broadcasted_iota(jnp.int32, (CH, CH), 1)
            for s in range(HEADS):
                dwc_ref[s] = jnp.where(tril, dwc_ref[s], 0.0)

    rev = lambda i: nt - 1 - i
    halo = lambda col: pl.BlockSpec((hb, SLAB), lambda i: (jnp.maximum(rev(i) * (tm // hb) - 1, 0), col))
    tok = lambda w: pl.BlockSpec((tm, w), lambda i: (rev(i), 0))
    return pl.pallas_call(
        body, grid=(nt,),
        in_specs=[tok(IN_DIM), halo(1), halo(2), tok(MIX), _full((8, D)), _full((1, D)), _full((1, D)),
                  _full((HEADS, CH, CH)), _full((HEADS, CH, CH)), _full((HEADS, CH, CH)),
                  _full((N_CHIP, N_PIECE, D, IN_PIECE), 1), tok(D), tok(D)] + [_full((1, D))] * 5,
        out_specs=[tok(IN_DIM), _full((16, D)), _full((HEADS, CH, CH)), tok(D)],
        out_shape=[jax.ShapeDtypeStruct((t, IN_DIM), BF16), jax.ShapeDtypeStruct((16, D), F32),
                   jax.ShapeDtypeStruct((HEADS, CH, CH), F32), jax.ShapeDtypeStruct((t, D), F32)],
        scratch_shapes=[pltpu.VMEM((8, D), F32), pltpu.VMEM((tm, D), F32)],
        compiler_params=_cp(("arbitrary",), VMEM_LIMIT), name="mixer_bwd")(
            proj, proj, proj, dmix, cw8, lng, lnb, wc, wct, bsb, win_f, x, dx1, g1, *rows123, row7)


def _grad_matmul(a, b, after, *, name, tk=1024):
    t, m = a.shape
    n = b.shape[1]
    nk = t // tk

    def body(a_ref, b_ref, after_ref, o_ref, ob_ref):
        kk = pl.program_id(0)
        part = lax.dot_general(a_ref[...], b_ref[...], (((0,), (0,)), ((), ())), preferred_element_type=F32)

        @pl.when(kk == 0)
        def _():
            o_ref[...] = part

        @pl.when(kk > 0)
        def _():
            o_ref[...] += part

        @pl.when(kk == nk - 1)
        def _():
            ob_ref[...] = o_ref[...].astype(BF16)

    o_spec = pl.BlockSpec((m, n), lambda k: (0, 0))
    o32, o16 = pl.pallas_call(
        body, grid=(nk,), in_specs=[pl.BlockSpec((tk, m), lambda k: (k, 0)), pl.BlockSpec((tk, n), lambda k: (k, 0)), ANY],
        out_specs=[o_spec, o_spec], out_shape=[jax.ShapeDtypeStruct((m, n), F32), jax.ShapeDtypeStruct((m, n), BF16)],
        compiler_params=_cp(("arbitrary",), VMEM_LIMIT), name=name)(a, b, after)
    return o32.reshape(N_CHIP, m // N_CHIP, n), o16.reshape(N_CHIP, m // N_CHIP, n)


def _coords():
    x, y, c = lax.axis_index("x"), lax.axis_index("y"), lax.axis_index("c")
    chips = [(1 - x, y), (x, 1 - y), (1 - x, 1 - y)]
    return x, y, c, chips


def _pair_reduce(c_idx, grads, grads_b, smalls, name):
    ng, ns = len(grads), len(smalls)
    halves = [g.shape[1] // 2 for g in grads]

    def body(c_ref, *refs):
        g_in, gb_any = refs[:ng], refs[ng:2 * ng]
        s_own, s_any = refs[2 * ng:2 * ng + ns], refs[2 * ng + ns:2 * ng + 2 * ns]
        o = refs[2 * ng + 2 * ns:4 * ng + 3 * ns]
        lands = refs[4 * ng + 3 * ns:5 * ng + 4 * ns]
        send, recv = refs[5 * ng + 4 * ns:]
        x, y, c, _ = _coords()
        j = pl.program_id(0)

        def big(i, blk):
            return pltpu.make_async_remote_copy(
                src_ref=gb_any[i].at[blk, pl.ds((1 - c) * halves[i], halves[i])], dst_ref=lands[i].at[blk],
                send_sem=send.at[i, blk], recv_sem=recv.at[i, blk], device_id=(x, y, 1 - c), device_id_type=MESH)

        def small(i):
            return pltpu.make_async_remote_copy(
                src_ref=s_any[i].at[1 - c], dst_ref=lands[ng + i],
                send_sem=send.at[ng + i, 0], recv_sem=recv.at[ng + i, 0], device_id=(x, y, 1 - c), device_id_type=MESH)

        @pl.when(j == 0)
        def _():
            for blk in range(N_CHIP):
                for i in range(ng):
                    big(i, blk).start()
            for i in range(ns):
                small(i).start()

        for i in range(ng):
            big(i, j).wait_recv()
            tot = g_in[i][...] + lands[i][j].astype(F32)
            o[i][...] = tot
            o[ng + i][...] = tot.astype(BF16)

        @pl.when(j == N_CHIP - 1)
        def _():
            for i in range(ns):
                small(i).wait_recv()
                o[2 * ng + i][...] = s_own[i][...] + lands[ng + i][...]
                small(i).wait_send()
            for blk in range(N_CHIP):
                for i in range(ng):
                    big(i, blk).wait_send()

    in_specs = [pl.BlockSpec((None, None, halves[i], g.shape[2]), lambda b, c: (b, c[0], 0, 0)) for i, g in enumerate(grads)]
    in_specs += [ANY] * ng
    in_specs += [pl.BlockSpec((None, s.shape[0] // 2, s.shape[1]), lambda b, c: (c[0], 0, 0)) for s in smalls]
    in_specs += [ANY] * ns
    blk = [pl.BlockSpec((None, halves[i], g.shape[2]), lambda b, c: (b, 0, 0)) for i, g in enumerate(grads)]
    out_specs = blk + blk + [pl.BlockSpec((s.shape[0] // 2, s.shape[1]), lambda b, c: (0, 0)) for s in smalls]
    out_shape = [jax.ShapeDtypeStruct((N_CHIP, halves[i], g.shape[2]), F32) for i, g in enumerate(grads)]
    out_shape += [jax.ShapeDtypeStruct((N_CHIP, halves[i], g.shape[2]), BF16) for i, g in enumerate(grads)]
    out_shape += [jax.ShapeDtypeStruct((s.shape[0] // 2, s.shape[1]), F32) for s in smalls]
    scratch = [pltpu.VMEM((N_CHIP, halves[i], g.shape[2]), BF16) for i, g in enumerate(grads)]
    scratch += [pltpu.VMEM((s.shape[0] // 2, s.shape[1]), F32) for s in smalls]
    scratch += [pltpu.SemaphoreType.DMA((ng + ns, N_CHIP)), pltpu.SemaphoreType.DMA((ng + ns, N_CHIP))]
    grads4 = [g.reshape(N_CHIP, 2, halves[i], g.shape[2]) for i, g in enumerate(grads)]
    smalls3 = [s.reshape(2, s.shape[0] // 2, s.shape[1]) for s in smalls]
    return pl.pallas_call(
        body, out_shape=out_shape,
        grid_spec=pltpu.PrefetchScalarGridSpec(num_scalar_prefetch=1, grid=(N_CHIP,), in_specs=in_specs,
                                               out_specs=out_specs, scratch_shapes=scratch),
        compiler_params=_cp(("arbitrary",), VMEM_LIMIT), name=name)(c_idx, *grads4, *grads_b, *smalls3, *smalls3)


def _grad_matmul_pair(c_idx, a, b, smalls, small_dtypes, after, *, name, tk=2048):
    t, m = a.shape
    bn = b.shape[1] // N_CHIP
    nk = t // tk
    hr = m // 2
    ns = len(smalls)

    def body(c_ref, a_ref, b_ref, *refs):
        s_own, s_any = refs[:ns], refs[ns:2 * ns]
        o32, o16 = refs[2 * ns + 1], refs[2 * ns + 2]
        o_small = refs[2 * ns + 3:3 * ns + 3]
        acc, tb, land, st16 = refs[3 * ns + 3:3 * ns + 7]
        s_land, s_stage = refs[3 * ns + 7:4 * ns + 7], refs[4 * ns + 7:5 * ns + 7]
        send, recv, loc = refs[5 * ns + 7:]
        x, y, c, _ = _coords()
        sibling = dict(device_id=(x, y, 1 - c), device_id_type=MESH)
        j, kk = pl.program_id(0), pl.program_id(1)
        mine = pl.ds(pl.multiple_of(c * hr, hr), hr)
        theirs = pl.ds(pl.multiple_of((1 - c) * hr, hr), hr)

        def to_sibling(blk):
            return pltpu.make_async_remote_copy(src_ref=tb, dst_ref=land.at[blk], send_sem=send.at[blk],
                                                recv_sem=recv.at[blk], **sibling)

        def small(i):
            return pltpu.make_async_remote_copy(src_ref=s_any[i].at[1 - c], dst_ref=s_land[i], send_sem=send.at[N_CHIP + i],
                                                recv_sem=recv.at[N_CHIP + i], **sibling)

        def written(blk):
            return (pltpu.make_async_copy(acc.at[blk % 2, mine], o32.at[blk], loc.at[0]),
                    pltpu.make_async_copy(st16, o16.at[blk], loc.at[1]))

        def finish(blk):
            to_sibling(blk).wait_recv()

            @pl.when(blk > 0)
            def _():
                for cp in written(blk - 1):
                    cp.wait()

            tot = acc[blk % 2, mine, :] + land[blk].astype(F32)
            acc[blk % 2, mine, :] = tot
            st16[...] = tot.astype(BF16)
            for cp in written(blk):
                cp.start()

        def small_out(i):
            return pltpu.make_async_copy(s_stage[i], o_small[i], loc.at[2 + i])

        @pl.when((j == 0) & (kk == 0))
        def _():
            for i in range(ns):
                small(i).start()

        @pl.when((j == 1) & (kk == 0))
        def _():
            for i in range(ns):
                small(i).wait_recv()
                s_stage[i][...] = (s_own[i][...] + s_land[i][...]).astype(small_dtypes[i])
                small_out(i).start()

        @pl.when((j > 0) & (kk == 0))
        def _():
            finish(j - 1)

        part = lax.dot_general(a_ref[...], b_ref[...], (((0,), (0,)), ((), ())), preferred_element_type=F32)
        slot = lax.rem(j, 2)

        @pl.when(kk == 0)
        def _():
            acc[slot] = part

        @pl.when(kk > 0)
        def _():
            acc[slot] += part

        @pl.when(kk == nk - 1)
        def _():
            @pl.when(j > 0)
            def _():
                to_sibling(j - 1).wait_send()

            tb[...] = acc[slot, theirs, :].astype(BF16)
            to_sibling(j).start()

        @pl.when((j == N_CHIP - 1) & (kk == nk - 1))
        def _():
            finish(j)
            for i in range(ns):
                small_out(i).wait()
                small(i).wait_send()
            for cp in written(j):
                cp.wait()
            to_sibling(j).wait_send()

    halves = [(s.shape[0] // 2, s.shape[1]) for s in smalls]
    in_specs = [pl.BlockSpec((tk, m), lambda j, k, c: (k, 0)), pl.BlockSpec((tk, bn), lambda j, k, c: (k, j))]
    in_specs += [pl.BlockSpec((None,) + h, lambda j, k, c: (c[0], 0, 0)) for h in halves] + [ANY] * ns + [ANY]
    out_shape = [jax.ShapeDtypeStruct((N_CHIP, hr, bn), F32), jax.ShapeDtypeStruct((N_CHIP, hr, bn), BF16)]
    out_shape += [pltpu.HBM(h, dt) for h, dt in zip(halves, small_dtypes)]
    scratch = [pltpu.VMEM((2, m, bn), F32), pltpu.VMEM((hr, bn), BF16),
               pltpu.VMEM((N_CHIP, hr, bn), BF16), pltpu.VMEM((hr, bn), BF16)]
    scratch += [pltpu.VMEM(h, F32) for h in halves] + [pltpu.VMEM(h, dt) for h, dt in zip(halves, small_dtypes)]
    scratch += [pltpu.SemaphoreType.DMA((N_CHIP + ns,)), pltpu.SemaphoreType.DMA((N_CHIP + ns,)),
                pltpu.SemaphoreType.DMA((2 + ns,))]
    smalls3 = [s.reshape((2,) + h) for s, h in zip(smalls, halves)]
    outs = pl.pallas_call(
        body, out_shape=out_shape,
        grid_spec=pltpu.PrefetchScalarGridSpec(num_scalar_prefetch=1, grid=(N_CHIP, nk), in_specs=in_specs,
                                               out_specs=[ANY, ANY] + [_HBM] * ns, scratch_shapes=scratch),
        compiler_params=_cp(("arbitrary", "arbitrary"), VMEM_LIMIT), name=name)(c_idx, a, b, *smalls3, *smalls3, after)
    return outs[0], outs[1], list(outs[2:])


_HBM = pl.BlockSpec(memory_space=pltpu.HBM)
_SEM = pl.BlockSpec(memory_space=pltpu.SEMAPHORE)


def _split_copies(ins, lands, ng, send, recv, arriving):
    x, y, c, chips = _coords()
    b = 2 * x + y
    copies = []
    for i in range(len(ins)):
        for k in range(3):
            blk = 2 * chips[k][0] + chips[k][1]
            src, dst, got = (ins[i].at[blk], lands[i].at[k], lands[i].at[k]) if i < ng else (ins[i], lands[i].at[b], lands[i].at[blk])
            sems = dict(send_sem=send.at[3 * i + k], recv_sem=recv.at[3 * i + k], device_id=(*chips[k], c), device_id_type=MESH)
            if arriving:
                copies.append(pltpu.make_async_remote_copy(src_ref=got, dst_ref=got, **sems))
            else:
                copies.append(pltpu.make_async_remote_copy(src_ref=src, dst_ref=dst, **sems))
    return copies


def _exchange_begin(sums_b, smalls, name):
    ng, n = len(sums_b), len(sums_b) + len(smalls)
    srcs = list(sums_b) + list(smalls)
    lands = [lax.empty((3,) + g.shape[1:], g.dtype) for g in sums_b] + [lax.empty((N_CHIP,) + s.shape, s.dtype) for s in smalls]

    def body(*refs):
        ins, land_refs = refs[:n], refs[n:2 * n]
        send, recv = refs[2 * n], refs[2 * n + 1]
        token = refs[4 * n + 2]
        for cp in _split_copies(ins, land_refs, ng, send, recv, False):
            cp.start()
        token[...] = jnp.zeros_like(token)

    hbm = lambda a: pltpu.HBM(a.shape, a.dtype)
    outs = pl.pallas_call(
        body, name=name,
        out_shape=(pltpu.SemaphoreType.DMA((3 * n,)), pltpu.SemaphoreType.DMA((3 * n,)), *[hbm(a) for a in srcs + lands],
                   jax.ShapeDtypeStruct((8, 128), F32)),
        in_specs=[_HBM] * (2 * n), out_specs=(_SEM, _SEM, *[_HBM] * (2 * n), pl.BlockSpec(memory_space=pltpu.VMEM)),
        input_output_aliases={i: 2 + i for i in range(2 * n)},
        compiler_params=pltpu.CompilerParams(has_side_effects=pltpu.SideEffectType.DATAFLOW_SIDE_EFFECTING),
    )(*[pltpu.with_memory_space_constraint(a, pltpu.HBM) for a in srcs + lands])
    return outs[0], outs[1], list(outs[2:2 + n]), list(outs[2 + n:2 + 2 * n]), outs[2 + 2 * n]


def _exchange_end(send, recv, srcs, lands, ng, which, after, name):
    n = len(srcs)
    after = list(after)

    def body(*refs):
        ins, land_refs = refs[:n], refs[n:2 * n]
        send_ref, recv_ref = refs[2 * n], refs[2 * n + 1]
        outgoing = _split_copies(ins, land_refs, ng, send_ref, recv_ref, False)
        arriving = _split_copies(ins, land_refs, ng, send_ref, recv_ref, True)
        for i in which:
            for cp in outgoing[3 * i:3 * i + 3]:
                cp.wait_send()
        for i in which:
            for cp in arriving[3 * i:3 * i + 3]:
                cp.wait_recv()

    hbm = lambda a: pltpu.HBM(a.shape, a.dtype)
    outs = pl.pallas_call(
        body, name=name, out_shape=tuple(hbm(a) for a in list(srcs) + list(lands)),
        in_specs=[_HBM] * (2 * n) + [_SEM, _SEM] + [ANY] * len(after), out_specs=tuple([_HBM] * (2 * n)),
        input_output_aliases={i: i for i in range(2 * n)},
        compiler_params=pltpu.CompilerParams(has_side_effects=pltpu.SideEffectType.DATAFLOW_SIDE_EFFECTING),
    )(*srcs, *lands, send, recv, *after)
    return list(outs[:n]), list(outs[n:])


def _chip_reduce(bc_idx, sums, recvd, smalls_slots, smalls_own, name, steps=4):
    ng, ns = len(sums), len(smalls_slots)
    n = ng + ns
    assert steps >= 2
    halves = [g.shape[1] for g in sums] + [s.shape[1] for s in smalls_slots]
    rows = [g.shape[1] // steps for g in sums]

    def body(bc_ref, *refs):
        own, rx = refs[:ng], refs[ng:2 * ng]
        sl = refs[2 * ng:2 * ng + ns]
        sl_own = refs[2 * ng + ns:2 * ng + 2 * ns]
        o = refs[2 * ng + 2 * ns:2 * ng + 2 * ns + n]
        tiles = refs[2 * ng + 2 * ns + n:2 * ng + 2 * ns + 2 * n]
        keep, send, recv = refs[2 * ng + 2 * ns + 2 * n:]
        x, y, c, _ = _coords()
        sibling = dict(device_id=(x, y, 1 - c), device_id_type=MESH)
        r = pl.program_id(0)

        def writes(i, step, slot):
            dst = o[i].at[pl.ds(c * halves[i] + step * rows[i], rows[i])]
            return (pltpu.make_async_copy(tiles[i].at[slot], dst, keep.at[i, slot]),
                    pltpu.make_async_remote_copy(src_ref=tiles[i].at[slot], dst_ref=dst, send_sem=send.at[i, slot],
                                                 recv_sem=recv.at[i, step], **sibling))

        def small_writes(i):
            dst = o[i].at[pl.ds(c * halves[i], halves[i])]
            return (pltpu.make_async_copy(tiles[i], dst, keep.at[i, 0]),
                    pltpu.make_async_remote_copy(src_ref=tiles[i], dst_ref=dst, send_sem=send.at[i, 0],
                                                 recv_sem=recv.at[i, 0], **sibling))

        def arriving(i, step, nrows):
            dst = o[i].at[pl.ds((1 - c) * halves[i] + step * nrows, nrows)]
            return pltpu.make_async_remote_copy(src_ref=dst, dst_ref=dst, send_sem=send.at[i, 0], recv_sem=recv.at[i, step],
                                                **sibling)

        def finish(step, slot):
            for i in range(ng):
                local, remote = writes(i, step, slot)
                local.wait()
                remote.wait_send()

        @pl.when(r >= 2)
        def _():
            finish(r - 2, r % 2)

        for i in range(ng):
            tot = own[i][...]
            for j in range(3):
                tot = tot + rx[i][j].astype(F32)
            tiles[i][r % 2] = tot
            for cp in writes(i, r, r % 2):
                cp.start()

        @pl.when(r == 0)
        def _():
            for i in range(ns):
                term = [jnp.where(bc_ref[0] == kk, sl_own[i][...], sl[i][kk]).astype(F32) for kk in range(N_CHIP)]
                tiles[ng + i][...] = ((term[0] + term[1]) + term[2]) + term[3]
                for cp in small_writes(ng + i):
                    cp.start()

        @pl.when(r == steps - 1)
        def _():
            finish(steps - 2, (steps - 2) % 2)
            finish(steps - 1, (steps - 1) % 2)
            for i in range(ns):
                local, remote = small_writes(ng + i)
                local.wait()
                remote.wait_send()
                arriving(ng + i, 0, halves[ng + i]).wait_recv()
            for i in range(ng):
                for step in range(steps):
                    arriving(i, step, rows[i]).wait_recv()

    in_specs = [pl.BlockSpec((None, rows[i], g.shape[2]), lambda r, bc: (bc[0], r, 0)) for i, g in enumerate(sums)]
    in_specs += [pl.BlockSpec((3, rows[i], g.shape[2]), lambda r, bc: (0, r, 0)) for i, g in enumerate(sums)]
    in_specs += [pl.BlockSpec(s.shape, lambda r, bc: (0, 0, 0)) for s in smalls_slots]
    in_specs += [pl.BlockSpec(s.shape[1:], lambda r, bc: (0, 0)) for s in smalls_slots]
    out_shape = [jax.ShapeDtypeStruct((2 * g.shape[1], g.shape[2]), F32) for g in sums]
    out_shape += [jax.ShapeDtypeStruct((2 * s.shape[1], s.shape[2]), F32) for s in smalls_slots]
    scratch = [pltpu.VMEM((2, rows[i], g.shape[2]), F32) for i, g in enumerate(sums)]
    scratch += [pltpu.VMEM(s.shape[1:], F32) for s in smalls_slots]
    scratch += [pltpu.SemaphoreType.DMA((n, 2)), pltpu.SemaphoreType.DMA((n, 2)), pltpu.SemaphoreType.DMA((n, steps))]
    return list(pl.pallas_call(
        body, out_shape=out_shape,
        grid_spec=pltpu.PrefetchScalarGridSpec(num_scalar_prefetch=1, grid=(steps,), in_specs=in_specs,
                                               out_specs=[ANY] * n, scratch_shapes=scratch),
        compiler_params=_cp(("arbitrary",), VMEM_LIMIT), name=name)(bc_idx, *sums, *recvd, *smalls_slots, *smalls_own))


def _adamw_math(w, g, m, v):
    m2 = ADAM_B1 * m + (1.0 - ADAM_B1) * g
    v2 = ADAM_B2 * v + (1.0 - ADAM_B2) * (g * g)
    m_hat = m2 / (1.0 - ADAM_B1 ** ADAM_STEP)
    v_hat = v2 / (1.0 - ADAM_B2 ** ADAM_STEP)
    delta = -ADAM_LR * (m_hat / (jnp.sqrt(v_hat) + ADAM_EPS) + ADAM_WD * w)
    return delta, m2, v2


def _adamw_big(ws, gs, ms, vs, name, steps=8):
    n = len(ws)

    def body(*refs):
        for i in range(n):
            w_ref, g_ref, m_ref, v_ref = (refs[k * n + i] for k in range(4))
            d_ref, m2_ref, v2_ref, g2_ref = (refs[(4 + k) * n + i] for k in range(4))
            gv = g_ref[...]
            d_ref[...], m2_ref[...], v2_ref[...] = _adamw_math(w_ref[...], gv, m_ref[...], v_ref[...])
            g2_ref[...] = gv

    specs = [pl.BlockSpec((w.shape[0] // steps, w.shape[1]), lambda i: (i, 0)) for w in ws]
    ahead = [pl.BlockSpec((w.shape[0] // steps, w.shape[1]), lambda i: (i, 0), pipeline_mode=pl.Buffered(4)) for w in ws]
    shapes = [jax.ShapeDtypeStruct(w.shape, F32) for w in ws]

    def streamed(*refs):
        pltpu.emit_pipeline(body, grid=(steps,), in_specs=ahead * 4, out_specs=specs * 4)(*refs)

    outs = pl.pallas_call(
        streamed, in_specs=[ANY] * (4 * n), out_specs=[ANY] * (4 * n), out_shape=shapes * 4,
        compiler_params=_cp(None, VMEM_LIMIT), name=name)(*ws, *gs, *ms, *vs)
    return [tuple(outs[k * n + i] for k in range(4)) for i in range(n)]


def _adamw_small(b_idx, sv, sw, vecs, conv, ws):
    nv = len(vecs)
    cols = conv[0].shape[1]

    def body(b_ref, sv_ref, sw_ref, *refs):
        ins, outs = refs[:3 * nv + 6], refs[3 * nv + 6:]
        for i in range(nv):
            g = sv_ref[i:i + 1, :]
            w_ref, m_ref, v_ref = ins[3 * i:3 * i + 3]
            d_ref, m2_ref, v2_ref, g_ref = outs[4 * i:4 * i + 4]
            d_ref[...], m2_ref[...], v2_ref[...] = _adamw_math(w_ref[...], g, m_ref[...], v_ref[...])
            g_ref[...] = g
        g = sv_ref[8:8 + conv[0].shape[0], pl.ds(pl.multiple_of(b_ref[0] * cols, cols), cols)]
        w_ref, m_ref, v_ref = ins[3 * nv:3 * nv + 3]
        d_ref, m2_ref, v2_ref, g_ref = outs[4 * nv:4 * nv + 4]
        d_ref[...], m2_ref[...], v2_ref[...] = _adamw_math(w_ref[...], g, m_ref[...], v_ref[...])
        g_ref[...] = g
        w_ref, m_ref, v_ref = ins[3 * nv + 3:]
        d_ref, m2_ref, v2_ref, g_ref, one_ref = outs[4 * nv + 4:]
        g = sw_ref[...]
        d_ref[...], m2_ref[...], v2_ref[...] = _adamw_math(w_ref[...], g, m_ref[...], v_ref[...])
        g_ref[...] = g
        one_ref[...] = sv_ref[nv:nv + 1, 0:1]

    flat = [a for grp in vecs for a in grp] + list(conv) + list(ws)
    out_shape = [jax.ShapeDtypeStruct(grp[0].shape, F32) for grp in list(vecs) + [conv, ws] for _ in range(4)]
    out_shape += [jax.ShapeDtypeStruct((1, 1), F32)]
    vmem = pl.BlockSpec(memory_space=pltpu.VMEM)
    outs = pl.pallas_call(
        body, out_shape=out_shape, in_specs=[pl.BlockSpec(memory_space=pltpu.SMEM)] + [vmem] * (2 + len(flat)),
        out_specs=[vmem] * len(out_shape), name="adamw_small")(b_idx, sv, sw, *flat)
    return [tuple(outs[4 * i:4 * i + 4]) for i in range(nv + 2)], outs[4 * nv + 8]


def kernel(x, mem, norm_mix_g, w_in, conv_w, gm_ln_g, gm_ln_b, gm_ws, gm_bs, w_out, norm_x_g, norm_mem_g, w_q, w_kv, w_xo, norm_final_g, loss_target, m_norm_mix_g, m_w_in, m_conv_w, m_gm_ln_g, m_gm_ln_b, m_gm_ws, m_gm_bs, m_w_out, m_norm_x_g, m_norm_mem_g, m_w_q, m_w_kv, m_w_xo, m_norm_final_g, v_norm_mix_g, v_w_in, v_conv_w, v_gm_ln_g, v_gm_ln_b, v_gm_ws, v_gm_bs, v_w_out, v_norm_x_g, v_norm_mem_g, v_w_q, v_w_kv, v_w_xo, v_norm_final_g):
    t = x.shape[1]
    xi = lax.axis_index("x")
    yi = lax.axis_index("y")
    ci = lax.axis_index("c")
    b_idx = jnp.reshape(2 * xi + yi, (1,)).astype(jnp.int32)
    c_idx = jnp.reshape(ci, (1,)).astype(jnp.int32)

    x2d, mem2d, tgt = x[0], mem[0], loss_target[0]
    big = [w_in[0], w_out[0], w_q[0], w_kv[0], w_xo[0]]
    big_m = [m_w_in[0], m_w_out[0], m_w_q[0], m_w_kv[0], m_w_xo[0]]
    big_v = [v_w_in[0], v_w_out[0], v_w_q[0], v_w_kv[0], v_w_xo[0]]
    g3 = norm_final_g.reshape(1, D)

    def pad8(a):
        return jnp.pad(a, ((0, 8 - a.shape[0]), (0, 0)))

    own_blocks, (wc, wct, bsb) = _cast_shards(b_idx, big, gm_ws[0], gm_bs[0])

    proj, hb, win_f, cw8, (wq_f,) = _proj_gather(
        b_idx, x2d, norm_mix_g, own_blocks[0], pad8(conv_w[0]), [own_blocks[2]])
    mixin, (wout_f, wkv_f, wxo_f) = _mixer_fwd(
        proj, cw8, gm_ln_g, gm_ln_b, wc, bsb, [own_blocks[1], own_blocks[3], own_blocks[4]])
    wout2, wq2, wxo2 = wout_f.reshape(MIX, D), wq_f.reshape(D, D), wxo_f.reshape(D, D)
    k, v = _mem_fwd(mem2d, norm_mem_g, wkv_f)

    (loss_row, dmix, dx1b, h2b, dq, ob, dx2b, dk, dv, dg2, dg3) = _tail(
        x2d, tgt, mixin, wout2, wq2, wxo2, k, v, norm_x_g, g3)
    dwkv, dwkv_b, dgm = _mem_bwd(mem2d, norm_mem_g, dk, dv, wkv_f)
    dproj, sv, dwc, grad_x = _mixer_bwd(
        proj, dmix, cw8, gm_ln_g, gm_ln_b, wc, wct, bsb, win_f, x2d, dx1b, norm_mix_g, [dg2, dgm, dg3], loss_row)

    bc_idx = jnp.concatenate([b_idx, c_idx])
    sw = dwc.reshape(HEADS * CH, CH)
    dwin_sum, dwin_sum_b, psmall = _grad_matmul_pair(c_idx, hb, dproj, [sv, sw], [F32, BF16], dgm, name="grad_w_in")
    sums_b = [dwin_sum]
    send_b, recv_b, src_b, land_b, token_b = _exchange_begin([dwin_sum_b], psmall, "exchange_b_begin")

    dwxo, dwxo_b = _grad_matmul(ob, dx2b, token_b, name="grad_w_xo", tk=2048)
    dwq, dwq_b = _grad_matmul(h2b, dq, token_b, name="grad_w_q", tk=2048)
    dwout, dwout_b = _grad_matmul(mixin, dx1b, token_b, name="grad_w_out")
    ps_a = _pair_reduce(c_idx, [dwout, dwkv, dwq, dwxo], [dwout_b, dwkv_b, dwq_b, dwxo_b], [], "pair_reduce_a")
    sums_a, sums_a_b = list(ps_a[:4]), list(ps_a[4:8])
    send_a, recv_a, src_a, land_a, token_a = _exchange_begin(sums_a_b, [], "exchange_a_begin")

    src_b, rx2b = _exchange_end(send_b, recv_b, src_b, land_b, 1, [0, 1, 2], [token_a], "exchange_b_end")
    gwin, svf, swf = _chip_reduce(bc_idx, sums_b, rx2b[:1], rx2b[1:], src_b[1:], "chip_reduce_b")
    out_b = _adamw_big(big[:1], [gwin], big_m[:1], big_v[:1], "adamw_w_in")[0]

    row = lambda a: a.reshape(1, D)
    mat = lambda a: a.reshape(HEADS * CH, CH)
    small, loss = _adamw_small(
        b_idx, svf, swf,
        [(row(norm_mix_g), row(m_norm_mix_g), row(v_norm_mix_g)), (row(norm_x_g), row(m_norm_x_g), row(v_norm_x_g)),
         (row(norm_mem_g), row(m_norm_mem_g), row(v_norm_mem_g)), (row(norm_final_g), row(m_norm_final_g), row(v_norm_final_g)),
         (row(gm_ln_g), row(m_gm_ln_g), row(v_gm_ln_g)), (row(gm_ln_b), row(m_gm_ln_b), row(v_gm_ln_b)),
         (row(gm_bs), row(m_gm_bs), row(v_gm_bs))],
        (conv_w[0], m_conv_w[0], v_conv_w[0]), (mat(gm_ws), mat(m_gm_ws), mat(v_gm_ws)))

    def finish_a(part, src, land, after, tag):
        src, land = _exchange_end(send_a, recv_a, src, land, 4, part, after, "exchange_a%s_end" % tag)
        grads = _chip_reduce(bc_idx, [sums_a[i] for i in part], [land[i] for i in part], [], [], "chip_reduce_a" + tag,
                             steps=2)
        ids = [(1, 3, 2, 4)[i] for i in part]
        outs = _adamw_big([big[i] for i in ids], grads, [big_m[i] for i in ids], [big_v[i] for i in ids], "adamw_a" + tag,
                          steps=4)
        return src, land, outs

    src_a, land_a, (out_wout, out_wkv) = finish_a([0, 1], src_a, land_a, [out_b[0], small[0][0]], "1")
    _, _, (out_wq, out_wxo) = finish_a([2, 3], src_a, land_a, [out_wout[0]], "2")

    def unpack(k):
        vec = lambda i: small[i][k]
        return [vec(0), out_b[k][None], small[7][k][None], vec(4), vec(5), small[8][k].reshape(1, HEADS, CH, CH),
                vec(6).reshape(1, HEADS, CH), out_wout[k][None], vec(1), vec(2), out_wq[k][None], out_wkv[k][None],
                out_wxo[k][None], vec(3).reshape(D)]

    return (loss.reshape(()), grad_x[None], *unpack(3), *unpack(0), *unpack(1), *unpack(2))
```

```python
import functools
import math

import jax
import jax.numpy as jnp
from jax import lax
from jax.experimental import pallas as pl
from jax.experimental.pallas import tpu as pltpu

F32 = jnp.float32
BF16 = jnp.bfloat16
MESH = pl.DeviceIdType.MESH

D = 1024
SLAB = 1024
N_SLAB = 7
IN_DIM = N_SLAB * SLAB
MIX = 2 * SLAB
HEADS = 8
CH = 128
XH = 4
XD = D // XH
EPS = 1e-6
GELU_C = math.sqrt(2.0 / math.pi)
GELU_A = 0.044715
N_CHIP = 4
IN_BLK = IN_DIM // N_CHIP
IN_PIECE = 256
N_PIECE = IN_BLK // IN_PIECE
KV_BLK = 2 * D // N_CHIP

ADAM_LR, ADAM_B1, ADAM_B2, ADAM_EPS, ADAM_WD, ADAM_STEP = 0.001, 0.9, 0.999, 1e-08, 0.01, 10

VMEM_LIMIT = 60 * 1024 * 1024


def _cp(sem=None, vmem=None):
    return pltpu.CompilerParams(dimension_semantics=sem, vmem_limit_bytes=vmem)


def _full(shape, buffers=None):
    n = len(shape)
    if buffers is None:
        return pl.BlockSpec(shape, lambda *_: (0,) * n)
    return pl.BlockSpec(shape, lambda *_: (0,) * n, pipeline_mode=pl.Buffered(buffers))


ANY = pl.BlockSpec(memory_space=pl.ANY)


def _bdot(a, b):
    return jnp.dot(a.astype(BF16), b.astype(BF16), preferred_element_type=F32)


def _bdot_nt(a, b):
    return lax.dot_general(a.astype(BF16), b.astype(BF16), (((1,), (1,)), ((), ())), preferred_element_type=F32)


def _bdot_tn(a, b):
    return lax.dot_general(a.astype(BF16), b.astype(BF16), (((0,), (0,)), ((), ())), preferred_element_type=F32)


def _rms(x, g):
    r = lax.rsqrt(jnp.mean(x * x, axis=-1, keepdims=True) + EPS)
    return x * r * g, r


def _rms_bwd(dy, x, r, g):
    gdy = dy * g
    dx = r * gdy - x * (r * r * r) * jnp.mean(x * gdy, axis=-1, keepdims=True)
    dg = jnp.sum(dy * x * r, axis=0, keepdims=True)
    return dx, dg


def _gelu_parts(x):
    x2 = x * x
    t = jnp.tanh(GELU_C * (x + GELU_A * x * x2))
    val = 0.5 * x * (1.0 + t)
    grad = 0.5 * (1.0 + t) + 0.5 * x * (1.0 - t * t) * (GELU_C * (1.0 + 3.0 * GELU_A * x2))
    return val, grad


def _gelu(x):
    return 0.5 * x * (1.0 + jnp.tanh(GELU_C * (x + GELU_A * x * x * x)))


def _sigmoid(z):
    return 1.0 / (1.0 + jnp.exp(-z))


def _cast_shards(b_idx, arrs, gm_ws, gm_bs):
    n = len(arrs)
    steps = 4

    def body(b_ref, *refs):
        ws_ref, bs_ref = refs[n:n + 2]
        outs = refs[n + 2:2 * n + 2]
        wc_ref, wct_ref, bsb_ref = refs[2 * n + 2:]
        for p in range(N_PIECE):
            outs[0][p] = refs[0][:, pl.ds(p * IN_PIECE, IN_PIECE)].astype(BF16)
        for i in range(1, n):
            outs[i][...] = refs[i][...].astype(BF16)

        @pl.when(pl.program_id(0) == 0)
        def _():
            causal = lax.broadcasted_iota(jnp.int32, (CH, CH), 0) >= lax.broadcasted_iota(jnp.int32, (CH, CH), 1)
            for h in range(HEADS):
                w = jnp.where(causal, ws_ref[h], 0.0)
                wc_ref[h] = w.astype(BF16)
                wct_ref[h] = w.T.astype(BF16)
                bsb_ref[h] = jnp.broadcast_to(bs_ref[h:h + 1, :], (CH, CH)).T

    rows = [a.shape[0] // steps for a in arrs]
    in_specs = [pl.BlockSpec((rows[i], a.shape[1]), lambda i, b: (i, 0)) for i, a in enumerate(arrs)]
    in_specs += [pl.BlockSpec((HEADS, CH, CH), lambda i, b: (0, 0, 0)), pl.BlockSpec((HEADS, CH), lambda i, b: (0, 0))]
    out_specs = [pl.BlockSpec((None, N_PIECE, rows[0], IN_PIECE), lambda i, b: (b[0], 0, i, 0))]
    out_specs += [pl.BlockSpec((None, rows[i], a.shape[1]), lambda i, b: (b[0], i, 0)) for i, a in enumerate(arrs) if i > 0]
    out_specs += [pl.BlockSpec((HEADS, CH, CH), lambda i, b: (0, 0, 0))] * 3
    out_shape = [jax.ShapeDtypeStruct((N_CHIP, N_PIECE, arrs[0].shape[0], IN_PIECE), BF16)]
    out_shape += [jax.ShapeDtypeStruct((N_CHIP,) + a.shape, BF16) for a in arrs[1:]]
    out_shape += [jax.ShapeDtypeStruct((HEADS, CH, CH), dt) for dt in (BF16, BF16, F32)]
    outs = pl.pallas_call(
        body, out_shape=out_shape,
        grid_spec=pltpu.PrefetchScalarGridSpec(num_scalar_prefetch=1, grid=(steps,), in_specs=in_specs, out_specs=out_specs),
        compiler_params=_cp(("arbitrary",)), name="cast_shards")(b_idx, *arrs, gm_ws, gm_bs)
    return outs[:n], outs[n:]


def _proj_gather(b_idx, x, g, win_own, cw8s, more, tm=1024):
    t = x.shape[0]
    ni = t // tm
    nm = len(more)
    steps = N_CHIP * N_PIECE
    near0, far0 = N_PIECE, 3 * N_PIECE

    def piece_at(step, own):
        k = step - near0
        near, far = (step >= near0) & (step < far0), step >= far0
        block = jnp.where(far, own ^ 3, jnp.where(near, own ^ jnp.where(lax.rem(k, 2) == 0, 2, 1), own))
        return block, jnp.where(far, step - far0, jnp.where(near, lax.div(k, 2), step))

    def body(*refs):
        b_ref, x_any, g_ref, win_in, cw_in = refs[:5]
        o_ref, hb_any, win_f, cw_out = refs[5 + nm:9 + nm]
        more_out = refs[9 + nm:9 + 2 * nm]
        hbuf, xbuf, wv, cw_s, cw_r, loc = refs[9 + 2 * nm:15 + 2 * nm]
        g_in = _Gather([win_f.at[:, p] for p in range(N_PIECE)], *refs[15 + 2 * nm:19 + 2 * nm])
        g_more = _Gather(more_out, *refs[19 + 2 * nm:23 + 2 * nm])
        s = pl.program_id(0)
        x, y, c, chips = _coords()
        b = 2 * x + y
        blks = [2 * chip[0] + chip[1] for chip in chips]

        def cw_cols(blk):
            return cw_out.at[:, pl.ds(blk * (D // N_CHIP), D // N_CHIP)]

        def cw_copy(k, blk):
            src = cw_in if blk is None else cw_cols(blk)
            return pltpu.make_async_remote_copy(src_ref=src, dst_ref=cw_cols(b if blk is None else blk), send_sem=cw_s.at[k],
                                                recv_sem=cw_r.at[k], device_id=(*chips[k], c), device_id_type=MESH)

        cw_local = pltpu.make_async_copy(cw_in, cw_cols(b), loc.at[1])
        hb_copy = pltpu.make_async_copy(hbuf, hb_any, loc.at[0])

        def load(step):
            slot = lax.rem(step, 2)
            block, piece = piece_at(step, b_ref[0])
            return pltpu.make_async_copy(win_f.at[block, piece], wv.at[slot], loc.at[2 + slot])

        def chunk(i):
            return pltpu.make_async_copy(x_any.at[pl.ds(i * tm, tm)], xbuf.at[i % 2], loc.at[4 + i % 2])

        def first():
            g_in.start()
            cw_local.start()
            for k in range(3):
                cw_copy(k, None).start()
            load(0).start()
            chunk(0).start()
            for i in range(ni):
                if i + 1 < ni:
                    chunk(i + 1).start()
                chunk(i).wait()
                h, _ = _rms(xbuf[i % 2], g_ref[...])
                hbuf[pl.ds(i * tm, tm), :] = h.astype(BF16)
            hb_copy.start()

        events = {step: [] for step in range(steps)}
        events[0].append(first)
        for p in range(N_PIECE):
            events[2 * p + 2].append(functools.partial(g_in.hop, [p]))
            events[near0 + 2 * p - 1].append(functools.partial(g_in.near_ready, [p]))
            events[far0 + p - 2].append(functools.partial(g_in.far, [p]))
            events[far0 + p - 1].append(functools.partial(g_in.far_ready, [p]))
        events[2 * N_PIECE + 1].append(g_more.start)
        for step, todo in events.items():
            if todo:
                @pl.when(s == step)
                def _(todo=todo):
                    for do in todo:
                        do()

        @pl.when(s + 1 < steps)
        def _():
            load(s + 1).start()

        load(s).wait()
        for i in range(ni):
            rows = pl.ds(i * tm, tm)
            o_ref[rows, :] = jnp.dot(hbuf[rows, :], wv[lax.rem(s, 2)], preferred_element_type=F32).astype(BF16)

        @pl.when(s == steps - 1)
        def _():
            g_more.hop()
            g_more.far()
            for k in range(3):
                cw_copy(k, blks[k]).wait_recv()
            for k in range(3):
                cw_copy(k, None).wait_send()
            cw_local.wait()
            hb_copy.wait()
            g_more.near_ready()
            g_more.far_ready()
            g_in.drain()
            g_more.drain()

    def out_col(s, b):
        block, piece = piece_at(s, b[0])
        return 0, block * N_PIECE + piece

    in_specs = [ANY, pl.BlockSpec((1, D), lambda s, b: (0, 0)), ANY, ANY] + [ANY] * nm
    out_specs = [pl.BlockSpec((t, IN_PIECE), out_col), ANY, ANY, ANY] + [ANY] * nm
    outs = pl.pallas_call(
        body, out_shape=[jax.ShapeDtypeStruct((t, IN_DIM), BF16), jax.ShapeDtypeStruct((t, D), BF16),
                         jax.ShapeDtypeStruct(win_own.shape, BF16), jax.ShapeDtypeStruct((8, D), F32)]
        + [jax.ShapeDtypeStruct(f.shape, f.dtype) for f in more],
        grid_spec=pltpu.PrefetchScalarGridSpec(
            num_scalar_prefetch=1, grid=(steps,), in_specs=in_specs, out_specs=out_specs,
            scratch_shapes=[pltpu.VMEM((t, D), BF16), pltpu.VMEM((2, tm, D), F32), pltpu.VMEM((2, D, IN_PIECE), BF16)]
            + [pltpu.SemaphoreType.DMA((3,))] * 2 + [pltpu.SemaphoreType.DMA((6,))]
            + _gather_sems(N_PIECE) + _gather_sems(nm)),
        input_output_aliases={3: 2, **{5 + w: 4 + w for w in range(nm)}},
        compiler_params=_cp(("arbitrary",), VMEM_LIMIT), name="proj_gather")(b_idx, x, g, win_own, cw8s, *more)
    return outs[0], outs[1], outs[2], outs[3], outs[4:]


class _Gather:
    def __init__(self, outs, ici_s, ici_r, d2d_s, d2d_r):
        x, y, c, _ = _coords()
        self.outs, self.c = outs, c
        self.sems = ici_s, ici_r, d2d_s, d2d_r
        self.b, self.bx, self.by, self.bd = 2 * x + y, 2 * (1 - x) + y, 2 * x + (1 - y), 2 * (1 - x) + (1 - y)
        self.xn, self.yn, self.sib = (1 - x, y, c), (x, 1 - y, c), (x, y, 1 - c)

    def piece(self, w, blk, hc, quarter=None):
        hr = self.outs[w].shape[1] // 2
        if quarter is None:
            return self.outs[w].at[blk, pl.ds(hc * hr, hr)]
        return self.outs[w].at[blk, pl.ds(hc * hr + quarter * (hr // 2), hr // 2)]

    def ici(self, w, k, ref, to):
        return pltpu.make_async_remote_copy(src_ref=ref, dst_ref=ref, send_sem=self.sems[0].at[w, k],
                                            recv_sem=self.sems[1].at[w, k], device_id=to, device_id_type=MESH)

    def d2d(self, w, k, ref):
        return pltpu.make_async_remote_copy(src_ref=ref, dst_ref=ref, send_sem=self.sems[2].at[w, k],
                                            recv_sem=self.sems[3].at[w, k], device_id=self.sib, device_id_type=MESH)

    def all(self):
        return range(len(self.outs))

    def start(self):
        for w in self.all():
            mine = self.piece(w, self.b, self.c)
            self.ici(w, 0, mine, self.xn).start()
            self.ici(w, 1, mine, self.yn).start()

    def hop(self, ws=None):
        c = self.c
        for w in ws or self.all():
            self.ici(w, 0, self.piece(w, self.bx, c), self.xn).wait_recv()
            self.ici(w, 1, self.piece(w, self.by, c), self.yn).wait_recv()
            self.ici(w, 2, self.piece(w, self.bx, c, 0), self.yn).start()
            self.ici(w, 3, self.piece(w, self.by, c, 1), self.xn).start()
            self.d2d(w, 0, self.piece(w, self.bx, c)).start()
            self.d2d(w, 1, self.piece(w, self.by, c)).start()

    def near_ready(self, ws=None):
        for w in ws or self.all():
            self.d2d(w, 0, self.piece(w, self.bx, 1 - self.c)).wait_recv()
            self.d2d(w, 1, self.piece(w, self.by, 1 - self.c)).wait_recv()

    def far(self, ws=None):
        c = self.c
        for w in ws or self.all():
            self.ici(w, 2, self.piece(w, self.bd, c, 0), self.yn).wait_recv()
            self.ici(w, 3, self.piece(w, self.bd, c, 1), self.xn).wait_recv()
            self.d2d(w, 2, self.piece(w, self.bd, c, 0)).start()
            self.d2d(w, 3, self.piece(w, self.bd, c, 1)).start()

    def far_ready(self, ws=None):
        for w in ws or self.all():
            self.d2d(w, 2, self.piece(w, self.bd, 1 - self.c, 0)).wait_recv()
            self.d2d(w, 3, self.piece(w, self.bd, 1 - self.c, 1)).wait_recv()

    def drain(self):
        c = self.c
        for w in self.all():
            mine = self.piece(w, self.b, c)
            self.ici(w, 0, mine, self.xn).wait_send()
            self.ici(w, 1, mine, self.yn).wait_send()
            self.ici(w, 2, self.piece(w, self.bx, c, 0), self.yn).wait_send()
            self.ici(w, 3, self.piece(w, self.by, c, 1), self.xn).wait_send()
            self.d2d(w, 0, self.piece(w, self.bx, c)).wait_send()
            self.d2d(w, 1, self.piece(w, self.by, c)).wait_send()
            self.d2d(w, 2, self.piece(w, self.bd, c, 0)).wait_send()
            self.d2d(w, 3, self.piece(w, self.bd, c, 1)).wait_send()


def _gather_sems(nw):
    return [pltpu.SemaphoreType.DMA((max(nw, 1), 4))] * 4


def _mixer_fwd(proj, cw8, lng, lnb, wc, bsb, fulls, tm=256):
    t = proj.shape[0]
    nt = t // tm
    nch = tm // CH
    nw = len(fulls)

    def body(*refs):
        p_ref, cw_ref, lng_ref, lnb_ref, wc_ref, bsb_ref = refs[:6]
        mix_ref = refs[6 + nw]
        w_outs = refs[7 + nw:7 + 2 * nw]
        prev_ref = refs[7 + 2 * nw]
        gather = _Gather(w_outs, *refs[8 + 2 * nw:])

        @pl.when(pl.program_id(0) == 0)
        def _():
            gather.start()
            prev_ref[...] = jnp.zeros_like(prev_ref)

        @pl.when(pl.program_id(0) == nt // 2)
        def _():
            gather.hop()

        @pl.when(pl.program_id(0) == nt - 1)
        def _():
            gather.far()

        rows = lax.broadcasted_iota(jnp.int32, (tm, CH), 0)
        for s in range(HEADS):
            cs = pl.ds(CH * s, CH)

            def slab(k):
                return p_ref[:, pl.ds(k * SLAB + CH * s, CH)].astype(F32)

            gb, gc, xa, za = slab(0), slab(1), slab(2), slab(3)
            cx = gc * xa
            p6 = jnp.broadcast_to(prev_ref[6:7, cs], (tm, CH))
            p7 = jnp.broadcast_to(prev_ref[7:8, cs], (tm, CH))
            c1 = jnp.where(rows == 0, p7, pltpu.roll(cx, 1, 0))
            c2 = jnp.where(rows == 0, p6, jnp.where(rows == 1, p7, pltpu.roll(cx, 2, 0)))
            prev_ref[:, cs] = cx[tm - 8:, :]
            cv = cw_ref[0:1, cs] * c2 + cw_ref[1:2, cs] * c1 + cw_ref[2:3, cs] * cx
            mix_ref[:, cs] = (gb * cv * (za * _sigmoid(za))).astype(BF16)

            u, v, zb = slab(4), slab(5), slab(6)
            ug, vg = _gelu(u), _gelu(v)
            dlt = vg - jnp.mean(vg, axis=-1, keepdims=True)
            vhat = dlt * lax.rsqrt(jnp.mean(dlt * dlt, axis=-1, keepdims=True) + EPS)
            vn = (vhat * lng_ref[:, cs] + lnb_ref[:, cs]).astype(BF16)
            gate = ug * (zb * _sigmoid(zb))
            for c in range(nch):
                rs = slice(CH * c, CH * (c + 1))
                sp = jnp.dot(wc_ref[s], vn[rs], preferred_element_type=F32) + bsb_ref[s]
                mix_ref[rs, pl.ds(SLAB + CH * s, CH)] = (gate[rs] * sp).astype(BF16)

        @pl.when(pl.program_id(0) == nt - 1)
        def _():
            gather.near_ready()
            gather.far_ready()
            gather.drain()

    sems = _gather_sems(nw)
    outs = pl.pallas_call(
        body, grid=(nt,),
        in_specs=[pl.BlockSpec((tm, IN_DIM), lambda i: (i, 0)), _full((8, D)), _full((1, D)), _full((1, D)),
                  _full((HEADS, CH, CH)), _full((HEADS, CH, CH))] + [ANY] * nw,
        out_specs=[pl.BlockSpec((tm, MIX), lambda i: (i, 0))] + [ANY] * nw,
        out_shape=[jax.ShapeDtypeStruct((t, MIX), BF16)] + [jax.ShapeDtypeStruct(f.shape, f.dtype) for f in fulls],
        input_output_aliases={6 + w: 1 + w for w in range(nw)},
        scratch_shapes=[pltpu.VMEM((8, D), F32)] + sems,
        compiler_params=_cp(("arbitrary",), VMEM_LIMIT), name="mixer_fwd")(proj, cw8, lng, lnb, wc, bsb, *fulls)
    return outs[0], outs[1:]


def _mem_fwd(mem, gm, wkv_f):
    n_mem = mem.shape[0]

    def body(mem_ref, gm_ref, w_ref, k_ref, v_ref):
        m, _ = _rms(mem_ref[...], gm_ref[...])
        mb = m.astype(BF16)
        for j in range(N_CHIP):
            dst = k_ref if j < 2 else v_ref
            dst[:, pl.ds(KV_BLK * (j % 2), KV_BLK)] = jnp.dot(mb, w_ref[j], preferred_element_type=F32).astype(BF16)

    return pl.pallas_call(
        body, out_shape=[jax.ShapeDtypeStruct((n_mem, D), BF16), jax.ShapeDtypeStruct((n_mem, D), BF16)],
        compiler_params=_cp(None, VMEM_LIMIT), name="mem_fwd")(mem, gm, wkv_f)


def _tail(x, tgt, mixin, wout, wq, wxo, k, v, g2, g3, tm=512, sub=512):
    t = x.shape[0]
    n_mem = k.shape[0]
    scale = 1.0 / math.sqrt(XD)

    def body(x_ref, tgt_ref, mix_ref, wout_ref, wq_ref, wxo_ref, k_ref, v_ref, g2_ref, g3_ref,
             loss_ref, dmix_ref, dx1b_ref, h2_ref, dq_ref, o_ref, dx2b_ref, dk_ref, dv_ref, dg2_ref, dg3_ref):
        @pl.when(pl.program_id(0) == 0)
        def _():
            loss_ref[...] = jnp.zeros_like(loss_ref)
            dk_ref[...] = jnp.zeros_like(dk_ref)
            dv_ref[...] = jnp.zeros_like(dv_ref)
            dg2_ref[...] = jnp.zeros_like(dg2_ref)
            dg3_ref[...] = jnp.zeros_like(dg3_ref)

        g2, g3 = g2_ref[...], g3_ref[...]
        for sb in range(tm // sub):
            rs = pl.ds(sub * sb, sub)
            x1 = x_ref[rs, :] + jnp.dot(mix_ref[rs, :], wout_ref[...], preferred_element_type=F32)
            h2, r2 = _rms(x1, g2)
            h2b = h2.astype(BF16)
            h2_ref[rs, :] = h2b
            q = jnp.dot(h2b, wq_ref[...], preferred_element_type=F32).astype(BF16)
            probs, outs = [], []
            for hd in range(XH):
                hs = pl.ds(XD * hd, XD)
                s = _bdot_nt(q[:, XD * hd:XD * (hd + 1)], k_ref[:, hs]) * scale
                e = jnp.exp(s - jnp.max(s, axis=-1, keepdims=True))
                p = e / jnp.sum(e, axis=-1, keepdims=True)
                probs.append(p)
                outs.append(_bdot(p, v_ref[:, hs]))
            ob = jnp.concatenate(outs, axis=-1).astype(BF16)
            o_ref[rs, :] = ob
            x2 = x1 + jnp.dot(ob, wxo_ref[...], preferred_element_type=F32)
            y, r3 = _rms(x2, g3)
            diff = y - tgt_ref[rs, :]
            row_loss = jnp.sum(diff * diff, axis=-1, keepdims=True)
            loss_ref[...] += jnp.broadcast_to(jnp.sum(row_loss, axis=0, keepdims=True) * (0.5 / D), loss_ref.shape)

            dx2, dg3 = _rms_bwd(diff * (1.0 / D), x2, r3, g3)
            dg3_ref[...] += dg3
            dx2b = dx2.astype(BF16)
            dx2b_ref[rs, :] = dx2b
            do = _bdot_nt(dx2b, wxo_ref[...])
            dqs = []
            for hd in range(XH):
                hs = pl.ds(XD * hd, XD)
                p = probs[hd]
                do_h = do[:, XD * hd:XD * (hd + 1)]
                dv_ref[:, hs] += _bdot_tn(p, do_h)
                dp = _bdot_nt(do_h, v_ref[:, hs])
                ds = p * (dp - jnp.sum(dp * p, axis=-1, keepdims=True))
                dqs.append(_bdot(ds, k_ref[:, hs]) * scale)
                dk_ref[:, hs] += _bdot_tn(ds, q[:, XD * hd:XD * (hd + 1)]) * scale
            dq = jnp.concatenate(dqs, axis=-1).astype(BF16)
            dq_ref[rs, :] = dq
            dx1n, dg2 = _rms_bwd(_bdot_nt(dq, wq_ref[...]), x1, r2, g2)
            dg2_ref[...] += dg2
            dx1b = (dx2 + dx1n).astype(BF16)
            dx1b_ref[rs, :] = dx1b
            dmix_ref[rs, :] = _bdot_nt(dx1b, wout_ref[...]).astype(BF16)

    tok = lambda w: pl.BlockSpec((tm, w), lambda i: (i, 0))
    return pl.pallas_call(
        body, grid=(t // tm,),
        in_specs=[tok(D), tok(D), tok(MIX), _full((MIX, D), 1), _full((D, D), 1), _full((D, D), 1),
                  _full((n_mem, D), 1), _full((n_mem, D), 1), _full((1, D)), _full((1, D))],
        out_specs=[_full((1, D)), tok(MIX), tok(D), tok(D), tok(D), tok(D), tok(D),
                   _full((n_mem, D)), _full((n_mem, D)), _full((1, D)), _full((1, D))],
        out_shape=[jax.ShapeDtypeStruct((1, D), F32), jax.ShapeDtypeStruct((t, MIX), BF16),
                   jax.ShapeDtypeStruct((t, D), BF16),
                   jax.ShapeDtypeStruct((t, D), BF16), jax.ShapeDtypeStruct((t, D), BF16),
                   jax.ShapeDtypeStruct((t, D), BF16), jax.ShapeDtypeStruct((t, D), BF16),
                   jax.ShapeDtypeStruct((n_mem, D), F32), jax.ShapeDtypeStruct((n_mem, D), F32),
                   jax.ShapeDtypeStruct((1, D), F32), jax.ShapeDtypeStruct((1, D), F32)],
        compiler_params=_cp(("arbitrary",), VMEM_LIMIT), name="tail")(x, tgt, mixin, wout, wq, wxo, k, v, g2, g3)


def _mem_bwd(mem, gm, dk, dv, wkv_f):
    def body(mem_ref, gm_ref, dk_ref, dv_ref, w_any, dw_any, dwb_any, dgm_ref, dw_ref, dwb_ref, w_ref, sem):
        fetched = [pltpu.make_async_copy(w_any.at[j], w_ref.at[j], sem.at[2, j]) for j in range(N_CHIP)]
        for cp in fetched:
            cp.start()
        mem_v = mem_ref[...]
        m, rm = _rms(mem_v, gm_ref[...])
        mb = m.astype(BF16)
        dm = jnp.zeros_like(mem_v)
        written = []
        for j in range(N_CHIP):
            src = dk_ref if j < 2 else dv_ref
            dkv = src[:, pl.ds(KV_BLK * (j % 2), KV_BLK)].astype(BF16)
            dw = _bdot_tn(mb, dkv)
            dw_ref[j] = dw
            dwb_ref[j] = dw.astype(BF16)
            written += [pltpu.make_async_copy(dw_ref.at[j], dw_any.at[j], sem.at[0, j]),
                        pltpu.make_async_copy(dwb_ref.at[j], dwb_any.at[j], sem.at[1, j])]
            for cp in written[-2:]:
                cp.start()
            fetched[j].wait()
            dm = dm + _bdot_nt(dkv, w_ref[j])
        dgm_ref[...] = jnp.sum(dm * mem_v * rm, axis=0, keepdims=True)
        for cp in written:
            cp.wait()

    vmem = pl.BlockSpec(memory_space=pltpu.VMEM)
    return pl.pallas_call(
        body, out_shape=[jax.ShapeDtypeStruct((N_CHIP, D, KV_BLK), F32), jax.ShapeDtypeStruct((N_CHIP, D, KV_BLK), BF16),
                         jax.ShapeDtypeStruct((1, D), F32)],
        in_specs=[vmem] * 4 + [ANY], out_specs=[ANY, ANY, vmem],
        scratch_shapes=[pltpu.VMEM((N_CHIP, D, KV_BLK), F32), pltpu.VMEM((N_CHIP, D, KV_BLK), BF16),
                        pltpu.VMEM((N_CHIP, D, KV_BLK), BF16), pltpu.SemaphoreType.DMA((3, N_CHIP))],
        compiler_params=_cp(None, VMEM_LIMIT), name="mem_bwd")(mem, gm, dk, dv, wkv_f)


def _mixer_bwd(proj, dmix, cw8, lng, lnb, wc, wct, bsb, win_f, x, dx1, g1, rows123, row7, tm=256):
    t = proj.shape[0]
    nt = t // tm
    nch = tm // CH
    hb = 16
    pair = 2 * CH
    assert pair == IN_PIECE

    def body(p_ref, pgc_ref, pxa_ref, dm_ref, cw_ref, lng_ref, lnb_ref, wc_ref, wct_ref, bsb_ref, w_ref, x_ref,
             dx1_ref, g1_ref, r1_ref, r2_ref, r3_ref, r7_ref, dp_ref, sv_ref, dwc_ref, gx_ref,
             next_ref, dh_ref):
        i = pl.program_id(0)
        dg1_ref, dlng_ref, dlnb_ref, dbs_ref = (sv_ref.at[pl.ds(r, 1)] for r in (0, 4, 5, 6))
        dcw_ref = sv_ref.at[pl.ds(8, 8)]

        @pl.when(i == 0)
        def _():
            next_ref[...] = jnp.zeros_like(next_ref)
            sv_ref[...] = jnp.zeros_like(sv_ref)
            dwc_ref[...] = jnp.zeros_like(dwc_ref)
            for r, ref in ((1, r1_ref), (2, r2_ref), (3, r3_ref), (7, r7_ref)):
                sv_ref[r:r + 1, :] = ref[...]

        first_tile = i == nt - 1
        rows = lax.broadcasted_iota(jnp.int32, (tm, CH), 0)
        ones8 = jnp.ones((8, CH), BF16)
        for s in range(HEADS):
            cs = pl.ds(CH * s, CH)

            def slab(k):
                return p_ref[:, pl.ds(k * SLAB + CH * s, CH)].astype(F32)

            gb, gc, xa, za = slab(0), slab(1), slab(2), slab(3)
            da = dm_ref[:, cs].astype(F32)
            cx = gc * xa
            cxp = pgc_ref[:, cs].astype(F32) * pxa_ref[:, cs].astype(F32)
            cxp = jnp.where(first_tile, jnp.zeros_like(cxp), cxp)
            p6 = jnp.broadcast_to(cxp[hb - 2:hb - 1, :], (tm, CH))
            p7 = jnp.broadcast_to(cxp[hb - 1:hb, :], (tm, CH))
            c1 = jnp.where(rows == 0, p7, pltpu.roll(cx, 1, 0))
            c2 = jnp.where(rows == 0, p6, jnp.where(rows == 1, p7, pltpu.roll(cx, 2, 0)))
            w0, w1, w2 = cw_ref[0:1, cs], cw_ref[1:2, cs], cw_ref[2:3, cs]
            cv = w0 * c2 + w1 * c1 + w2 * cx
            sg = _sigmoid(za)
            sa = za * sg
            dcv = da * gb * sa
            dp_ref[:, pl.ds(0 * SLAB + CH * s, CH)] = (da * cv * sa).astype(BF16)
            dp_ref[:, pl.ds(3 * SLAB + CH * s, CH)] = (da * gb * cv * (sg * (1.0 + za * (1.0 - sg)))).astype(BF16)
            n0 = jnp.broadcast_to(next_ref[0:1, cs], (tm, CH))
            n1 = jnp.broadcast_to(next_ref[1:2, cs], (tm, CH))
            u1 = jnp.where(rows == tm - 1, n0, pltpu.roll(dcv, tm - 1, 0))
            u2 = jnp.where(rows == tm - 2, n0, jnp.where(rows == tm - 1, n1, pltpu.roll(dcv, tm - 2, 0)))
            next_ref[:, cs] = dcv[0:8, :]
            dcx = w2 * dcv + w1 * u1 + w0 * u2
            dp_ref[:, pl.ds(1 * SLAB + CH * s, CH)] = (dcx * xa).astype(BF16)
            dp_ref[:, pl.ds(2 * SLAB + CH * s, CH)] = (dcx * gc).astype(BF16)
            dcw_ref[0:1, cs] += jnp.sum(dcv * c2, axis=0, keepdims=True)
            dcw_ref[1:2, cs] += jnp.sum(dcv * c1, axis=0, keepdims=True)
            dcw_ref[2:3, cs] += jnp.sum(dcv * cx, axis=0, keepdims=True)

            u, v, zb = slab(4), slab(5), slab(6)
            db = dm_ref[:, pl.ds(SLAB + CH * s, CH)].astype(F32)
            ug, ugrad = _gelu_parts(u)
            vg, vgrad = _gelu_parts(v)
            dlt = vg - jnp.mean(vg, axis=-1, keepdims=True)
            rstd = lax.rsqrt(jnp.mean(dlt * dlt, axis=-1, keepdims=True) + EPS)
            vhat = dlt * rstd
            lg = lng_ref[:, cs]
            vn = (vhat * lg + lnb_ref[:, cs]).astype(BF16)
            sgb = _sigmoid(zb)
            szb = zb * sgb
            sps, dvns = [], []
            dbs = jnp.zeros((8, CH), F32)
            dwc = jnp.zeros((CH, CH), F32)
            for c in range(nch):
                rs = slice(CH * c, CH * (c + 1))
                sp = jnp.dot(wc_ref[s], vn[rs], preferred_element_type=F32) + bsb_ref[s]
                dsp = (db[rs] * ug[rs] * szb[rs]).astype(BF16)
                dbs = dbs + lax.dot_general(ones8, dsp, (((1,), (1,)), ((), ())), preferred_element_type=F32)
                dwc = dwc + lax.dot_general(dsp, vn[rs], (((1,), (1,)), ((), ())), preferred_element_type=F32)
                dvns.append(jnp.dot(wct_ref[s], dsp, preferred_element_type=F32))
                sps.append(sp)
            sp = jnp.concatenate(sps, axis=0)
            dvn = jnp.concatenate(dvns, axis=0)
            dbs_ref[:, cs] += dbs[0:1]
            dwc_ref[s] += dwc
            dlng_ref[:, cs] += jnp.sum(dvn * vhat, axis=0, keepdims=True)
            dlnb_ref[:, cs] += jnp.sum(dvn, axis=0, keepdims=True)
            dvhat = dvn * lg
            dvg = rstd * (dvhat - jnp.mean(dvhat, axis=-1, keepdims=True)
                          - vhat * jnp.mean(dvhat * vhat, axis=-1, keepdims=True))
            dp_ref[:, pl.ds(4 * SLAB + CH * s, CH)] = (db * sp * szb * ugrad).astype(BF16)
            dp_ref[:, pl.ds(5 * SLAB + CH * s, CH)] = (dvg * vgrad).astype(BF16)
            dp_ref[:, pl.ds(6 * SLAB + CH * s, CH)] = (db * ug * sp * (sgb * (1.0 + zb * (1.0 - sgb)))).astype(BF16)

            if s % 2 == 1:
                part = None
                for k in range(N_SLAB):
                    col = k * SLAB + pair * (s // 2)
                    blk, off = divmod(col, IN_BLK)
                    term = lax.dot_general(dp_ref[:, pl.ds(col, pair)], w_ref[blk, off // IN_PIECE],
                                           (((1,), (1,)), ((), ())), preferred_element_type=F32)
                    part = term if part is None else part + term
                if s == 1:
                    dh_ref[...] = part
                else:
                    dh_ref[...] += part

        xv = x_ref[...]
        r = lax.rsqrt(jnp.mean(xv * xv, axis=-1, keepdims=True) + EPS)
        dxn, dg = _rms_bwd(dh_ref[...], xv, r, g1_ref[...])
        gx_ref[...] = dx1_ref[...].astype(F32) + dxn
        dg1_ref[...] += dg

        @pl.when(i == nt - 1)
        def _():
            tril = lax.broadcasted_iota(jnp.int32, (CH, CH), 0) >= lax.broadcasted_iota(jnp.int32, (CH, CH), 1)
            for s in range(HEADS):
                dwc_ref[s] = jnp.where(tril, dwc_ref[s], 0.0)

    rev = lambda i: nt - 1 - i
    halo = lambda col: pl.BlockSpec((hb, SLAB), lambda i: (jnp.maximum(rev(i) * (tm // hb) - 1, 0), col))
    tok = lambda w: pl.BlockSpec((tm, w), lambda i: (rev(i), 0))
    return pl.pallas_call(
        body, grid=(nt,),
        in_specs=[tok(IN_DIM), halo(1), halo(2), tok(MIX), _full((8, D)), _full((1, D)), _full((1, D)),
                  _full((HEADS, CH, CH)), _full((HEADS, CH, CH)), _full((HEADS, CH, CH)),
                  _full((N_CHIP, N_PIECE, D, IN_PIECE), 1), tok(D), tok(D)] + [_full((1, D))] * 5,
        out_specs=[tok(IN_DIM), _full((16, D)), _full((HEADS, CH, CH)), tok(D)],
        out_shape=[jax.ShapeDtypeStruct((t, IN_DIM), BF16), jax.ShapeDtypeStruct((16, D), F32),
                   jax.ShapeDtypeStruct((HEADS, CH, CH), F32), jax.ShapeDtypeStruct((t, D), F32)],
        scratch_shapes=[pltpu.VMEM((8, D), F32), pltpu.VMEM((tm, D), F32)],
        compiler_params=_cp(("arbitrary",), VMEM_LIMIT), name="mixer_bwd")(
            proj, proj, proj, dmix, cw8, lng, lnb, wc, wct, bsb, win_f, x, dx1, g1, *rows123, row7)


def _grad_matmul(a, b, after, *, name, tk=1024):
    t, m = a.shape
    n = b.shape[1]
    nk = t // tk

    def body(a_ref, b_ref, after_ref, o_ref, ob_ref):
        kk = pl.program_id(0)
        part = lax.dot_general(a_ref[...], b_ref[...], (((0,), (0,)), ((), ())), preferred_element_type=F32)

        @pl.when(kk == 0)
        def _():
            o_ref[...] = part

        @pl.when(kk > 0)
        def _():
            o_ref[...] += part

        @pl.when(kk == nk - 1)
        def _():
            ob_ref[...] = o_ref[...].astype(BF16)

    o_spec = pl.BlockSpec((m, n), lambda k: (0, 0))
    o32, o16 = pl.pallas_call(
        body, grid=(nk,), in_specs=[pl.BlockSpec((tk, m), lambda k: (k, 0)), pl.BlockSpec((tk, n), lambda k: (k, 0)), ANY],
        out_specs=[o_spec, o_spec], out_shape=[jax.ShapeDtypeStruct((m, n), F32), jax.ShapeDtypeStruct((m, n), BF16)],
        compiler_params=_cp(("arbitrary",), VMEM_LIMIT), name=name)(a, b, after)
    return o32.reshape(N_CHIP, m // N_CHIP, n), o16.reshape(N_CHIP, m // N_CHIP, n)


def _coords():
    x, y, c = lax.axis_index("x"), lax.axis_index("y"), lax.axis_index("c")
    chips = [(1 - x, y), (x, 1 - y), (1 - x, 1 - y)]
    return x, y, c, chips


def _pair_reduce(c_idx, grads, grads_b, smalls, name):
    ng, ns = len(grads), len(smalls)
    halves = [g.shape[1] // 2 for g in grads]

    def body(c_ref, *refs):
        g_in, gb_any = refs[:ng], refs[ng:2 * ng]
        s_own, s_any = refs[2 * ng:2 * ng + ns], refs[2 * ng + ns:2 * ng + 2 * ns]
        o = refs[2 * ng + 2 * ns:4 * ng + 3 * ns]
        lands = refs[4 * ng + 3 * ns:5 * ng + 4 * ns]
        send, recv = refs[5 * ng + 4 * ns:]
        x, y, c, _ = _coords()
        j = pl.program_id(0)

        def big(i, blk):
            return pltpu.make_async_remote_copy(
                src_ref=gb_any[i].at[blk, pl.ds((1 - c) * halves[i], halves[i])], dst_ref=lands[i].at[blk],
                send_sem=send.at[i, blk], recv_sem=recv.at[i, blk], device_id=(x, y, 1 - c), device_id_type=MESH)

        def small(i):
            return pltpu.make_async_remote_copy(
                src_ref=s_any[i].at[1 - c], dst_ref=lands[ng + i],
                send_sem=send.at[ng + i, 0], recv_sem=recv.at[ng + i, 0], device_id=(x, y, 1 - c), device_id_type=MESH)

        @pl.when(j == 0)
        def _():
            for blk in range(N_CHIP):
                for i in range(ng):
                    big(i, blk).start()
            for i in range(ns):
                small(i).start()

        for i in range(ng):
            big(i, j).wait_recv()
            tot = g_in[i][...] + lands[i][j].astype(F32)
            o[i][...] = tot
            o[ng + i][...] = tot.astype(BF16)

        @pl.when(j == N_CHIP - 1)
        def _():
            for i in range(ns):
                small(i).wait_recv()
                o[2 * ng + i][...] = s_own[i][...] + lands[ng + i][...]
                small(i).wait_send()
            for blk in range(N_CHIP):
                for i in range(ng):
                    big(i, blk).wait_send()

    in_specs = [pl.BlockSpec((None, None, halves[i], g.shape[2]), lambda b, c: (b, c[0], 0, 0)) for i, g in enumerate(grads)]
    in_specs += [ANY] * ng
    in_specs += [pl.BlockSpec((None, s.shape[0] // 2, s.shape[1]), lambda b, c: (c[0], 0, 0)) for s in smalls]
    in_specs += [ANY] * ns
    blk = [pl.BlockSpec((None, halves[i], g.shape[2]), lambda b, c: (b, 0, 0)) for i, g in enumerate(grads)]
    out_specs = blk + blk + [pl.BlockSpec((s.shape[0] // 2, s.shape[1]), lambda b, c: (0, 0)) for s in smalls]
    out_shape = [jax.ShapeDtypeStruct((N_CHIP, halves[i], g.shape[2]), F32) for i, g in enumerate(grads)]
    out_shape += [jax.ShapeDtypeStruct((N_CHIP, halves[i], g.shape[2]), BF16) for i, g in enumerate(grads)]
    out_shape += [jax.ShapeDtypeStruct((s.shape[0] // 2, s.shape[1]), F32) for s in smalls]
    scratch = [pltpu.VMEM((N_CHIP, halves[i], g.shape[2]), BF16) for i, g in enumerate(grads)]
    scratch += [pltpu.VMEM((s.shape[0] // 2, s.shape[1]), F32) for s in smalls]
    scratch += [pltpu.SemaphoreType.DMA((ng + ns, N_CHIP)), pltpu.SemaphoreType.DMA((ng + ns, N_CHIP))]
    grads4 = [g.reshape(N_CHIP, 2, halves[i], g.shape[2]) for i, g in enumerate(grads)]
    smalls3 = [s.reshape(2, s.shape[0] // 2, s.shape[1]) for s in smalls]
    return pl.pallas_call(
        body, out_shape=out_shape,
        grid_spec=pltpu.PrefetchScalarGridSpec(num_scalar_prefetch=1, grid=(N_CHIP,), in_specs=in_specs,
                                               out_specs=out_specs, scratch_shapes=scratch),
        compiler_params=_cp(("arbitrary",), VMEM_LIMIT), name=name)(c_idx, *grads4, *grads_b, *smalls3, *smalls3)


def _grad_matmul_pair(c_idx, a, b, smalls, small_dtypes, after, *, name, tk=2048):
    t, m = a.shape
    bn = b.shape[1] // N_CHIP
    nk = t // tk
    hr = m // 2
    ns = len(smalls)

    def body(c_ref, a_ref, b_ref, *refs):
        s_own, s_any = refs[:ns], refs[ns:2 * ns]
        o32, o16 = refs[2 * ns + 1], refs[2 * ns + 2]
        o_small = refs[2 * ns + 3:3 * ns + 3]
        acc, tb, land, st16 = refs[3 * ns + 3:3 * ns + 7]
        s_land, s_stage = refs[3 * ns + 7:4 * ns + 7], refs[4 * ns + 7:5 * ns + 7]
        send, recv, loc = refs[5 * ns + 7:]
        x, y, c, _ = _coords()
        sibling = dict(device_id=(x, y, 1 - c), device_id_type=MESH)
        j, kk = pl.program_id(0), pl.program_id(1)
        mine = pl.ds(pl.multiple_of(c * hr, hr), hr)
        theirs = pl.ds(pl.multiple_of((1 - c) * hr, hr), hr)

        def to_sibling(blk):
            return pltpu.make_async_remote_copy(src_ref=tb, dst_ref=land.at[blk], send_sem=send.at[blk],
                                                recv_sem=recv.at[blk], **sibling)

        def small(i):
            return pltpu.make_async_remote_copy(src_ref=s_any[i].at[1 - c], dst_ref=s_land[i], send_sem=send.at[N_CHIP + i],
                                                recv_sem=recv.at[N_CHIP + i], **sibling)

        def written(blk):
            return (pltpu.make_async_copy(acc.at[blk % 2, mine], o32.at[blk], loc.at[0]),
                    pltpu.make_async_copy(st16, o16.at[blk], loc.at[1]))

        def finish(blk):
            to_sibling(blk).wait_recv()

            @pl.when(blk > 0)
            def _():
                for cp in written(blk - 1):
                    cp.wait()

            tot = acc[blk % 2, mine, :] + land[blk].astype(F32)
            acc[blk % 2, mine, :] = tot
            st16[...] = tot.astype(BF16)
            for cp in written(blk):
                cp.start()

        def small_out(i):
            return pltpu.make_async_copy(s_stage[i], o_small[i], loc.at[2 + i])

        @pl.when((j == 0) & (kk == 0))
        def _():
            for i in range(ns):
                small(i).start()

        @pl.when((j == 1) & (kk == 0))
        def _():
            for i in range(ns):
                small(i).wait_recv()
                s_stage[i][...] = (s_own[i][...] + s_land[i][...]).astype(small_dtypes[i])
                small_out(i).start()

        @pl.when((j > 0) & (kk == 0))
        def _():
            finish(j - 1)

        part = lax.dot_general(a_ref[...], b_ref[...], (((0,), (0,)), ((), ())), preferred_element_type=F32)
        slot = lax.rem(j, 2)

        @pl.when(kk == 0)
        def _():
            acc[slot] = part

        @pl.when(kk > 0)
        def _():
            acc[slot] += part

        @pl.when(kk == nk - 1)
        def _():
            @pl.when(j > 0)
            def _():
                to_sibling(j - 1).wait_send()

            tb[...] = acc[slot, theirs, :].astype(BF16)
            to_sibling(j).start()

        @pl.when((j == N_CHIP - 1) & (kk == nk - 1))
        def _():
            finish(j)
            for i in range(ns):
                small_out(i).wait()
                small(i).wait_send()
            for cp in written(j):
                cp.wait()
            to_sibling(j).wait_send()

    halves = [(s.shape[0] // 2, s.shape[1]) for s in smalls]
    in_specs = [pl.BlockSpec((tk, m), lambda j, k, c: (k, 0)), pl.BlockSpec((tk, bn), lambda j, k, c: (k, j))]
    in_specs += [pl.BlockSpec((None,) + h, lambda j, k, c: (c[0], 0, 0)) for h in halves] + [ANY] * ns + [ANY]
    out_shape = [jax.ShapeDtypeStruct((N_CHIP, hr, bn), F32), jax.ShapeDtypeStruct((N_CHIP, hr, bn), BF16)]
    out_shape += [pltpu.HBM(h, dt) for h, dt in zip(halves, small_dtypes)]
    scratch = [pltpu.VMEM((2, m, bn), F32), pltpu.VMEM((hr, bn), BF16),
               pltpu.VMEM((N_CHIP, hr, bn), BF16), pltpu.VMEM((hr, bn), BF16)]
    scratch += [pltpu.VMEM(h, F32) for h in halves] + [pltpu.VMEM(h, dt) for h, dt in zip(halves, small_dtypes)]
    scratch += [pltpu.SemaphoreType.DMA((N_CHIP + ns,)), pltpu.SemaphoreType.DMA((N_CHIP + ns,)),
                pltpu.SemaphoreType.DMA((2 + ns,))]
    smalls3 = [s.reshape((2,) + h) for s, h in zip(smalls, halves)]
    outs = pl.pallas_call(
        body, out_shape=out_shape,
        grid_spec=pltpu.PrefetchScalarGridSpec(num_scalar_prefetch=1, grid=(N_CHIP, nk), in_specs=in_specs,
                                               out_specs=[ANY, ANY] + [_HBM] * ns, scratch_shapes=scratch),
        compiler_params=_cp(("arbitrary", "arbitrary"), VMEM_LIMIT), name=name)(c_idx, a, b, *smalls3, *smalls3, after)
    return outs[0], outs[1], list(outs[2:])


_HBM = pl.BlockSpec(memory_space=pltpu.HBM)
_SEM = pl.BlockSpec(memory_space=pltpu.SEMAPHORE)


def _split_copies(ins, lands, ng, send, recv, arriving):
    x, y, c, chips = _coords()
    b = 2 * x + y
    copies = []
    for i in range(len(ins)):
        for k in range(3):
            blk = 2 * chips[k][0] + chips[k][1]
            src, dst, got = (ins[i].at[blk], lands[i].at[k], lands[i].at[k]) if i < ng else (ins[i], lands[i].at[b], lands[i].at[blk])
            sems = dict(send_sem=send.at[3 * i + k], recv_sem=recv.at[3 * i + k], device_id=(*chips[k], c), device_id_type=MESH)
            if arriving:
                copies.append(pltpu.make_async_remote_copy(src_ref=got, dst_ref=got, **sems))
            else:
                copies.append(pltpu.make_async_remote_copy(src_ref=src, dst_ref=dst, **sems))
    return copies


def _exchange_begin(sums_b, smalls, name):
    ng, n = len(sums_b), len(sums_b) + len(smalls)
    srcs = list(sums_b) + list(smalls)
    lands = [lax.empty((3,) + g.shape[1:], g.dtype) for g in sums_b] + [lax.empty((N_CHIP,) + s.shape, s.dtype) for s in smalls]

    def body(*refs):
        ins, land_refs = refs[:n], refs[n:2 * n]
        send, recv = refs[2 * n], refs[2 * n + 1]
        token = refs[4 * n + 2]
        for cp in _split_copies(ins, land_refs, ng, send, recv, False):
            cp.start()
        token[...] = jnp.zeros_like(token)

    hbm = lambda a: pltpu.HBM(a.shape, a.dtype)
    outs = pl.pallas_call(
        body, name=name,
        out_shape=(pltpu.SemaphoreType.DMA((3 * n,)), pltpu.SemaphoreType.DMA((3 * n,)), *[hbm(a) for a in srcs + lands],
                   jax.ShapeDtypeStruct((8, 128), F32)),
        in_specs=[_HBM] * (2 * n), out_specs=(_SEM, _SEM, *[_HBM] * (2 * n), pl.BlockSpec(memory_space=pltpu.VMEM)),
        input_output_aliases={i: 2 + i for i in range(2 * n)},
        compiler_params=pltpu.CompilerParams(has_side_effects=pltpu.SideEffectType.DATAFLOW_SIDE_EFFECTING),
    )(*[pltpu.with_memory_space_constraint(a, pltpu.HBM) for a in srcs + lands])
    return outs[0], outs[1], list(outs[2:2 + n]), list(outs[2 + n:2 + 2 * n]), outs[2 + 2 * n]


def _exchange_end(send, recv, srcs, lands, ng, which, after, name):
    n = len(srcs)
    after = list(after)

    def body(*refs):
        ins, land_refs = refs[:n], refs[n:2 * n]
        send_ref, recv_ref = refs[2 * n], refs[2 * n + 1]
        outgoing = _split_copies(ins, land_refs, ng, send_ref, recv_ref, False)
        arriving = _split_copies(ins, land_refs, ng, send_ref, recv_ref, True)
        for i in which:
            for cp in outgoing[3 * i:3 * i + 3]:
                cp.wait_send()
        for i in which:
            for cp in arriving[3 * i:3 * i + 3]:
                cp.wait_recv()

    hbm = lambda a: pltpu.HBM(a.shape, a.dtype)
    outs = pl.pallas_call(
        body, name=name, out_shape=tuple(hbm(a) for a in list(srcs) + list(lands)),
        in_specs=[_HBM] * (2 * n) + [_SEM, _SEM] + [ANY] * len(after), out_specs=tuple([_HBM] * (2 * n)),
        input_output_aliases={i: i for i in range(2 * n)},
        compiler_params=pltpu.CompilerParams(has_side_effects=pltpu.SideEffectType.DATAFLOW_SIDE_EFFECTING),
    )(*srcs, *lands, send, recv, *after)
    return list(outs[:n]), list(outs[n:])


def _chip_reduce(bc_idx, sums, recvd, smalls_slots, smalls_own, name, steps=4):
    ng, ns = len(sums), len(smalls_slots)
    n = ng + ns
    assert steps >= 2
    halves = [g.shape[1] for g in sums] + [s.shape[1] for s in smalls_slots]
    rows = [g.shape[1] // steps for g in sums]

    def body(bc_ref, *refs):
        own, rx = refs[:ng], refs[ng:2 * ng]
        sl = refs[2 * ng:2 * ng + ns]
        sl_own = refs[2 * ng + ns:2 * ng + 2 * ns]
        o = refs[2 * ng + 2 * ns:2 * ng + 2 * ns + n]
        tiles = refs[2 * ng + 2 * ns + n:2 * ng + 2 * ns + 2 * n]
        keep, send, recv = refs[2 * ng + 2 * ns + 2 * n:]
        x, y, c, _ = _coords()
        sibling = dict(device_id=(x, y, 1 - c), device_id_type=MESH)
        r = pl.program_id(0)

        def writes(i, step, slot):
            dst = o[i].at[pl.ds(c * halves[i] + step * rows[i], rows[i])]
            return (pltpu.make_async_copy(tiles[i].at[slot], dst, keep.at[i, slot]),
                    pltpu.make_async_remote_copy(src_ref=tiles[i].at[slot], dst_ref=dst, send_sem=send.at[i, slot],
                                                 recv_sem=recv.at[i, step], **sibling))

        def small_writes(i):
            dst = o[i].at[pl.ds(c * halves[i], halves[i])]
            return (pltpu.make_async_copy(tiles[i], dst, keep.at[i, 0]),
                    pltpu.make_async_remote_copy(src_ref=tiles[i], dst_ref=dst, send_sem=send.at[i, 0],
                                                 recv_sem=recv.at[i, 0], **sibling))

        def arriving(i, step, nrows):
            dst = o[i].at[pl.ds((1 - c) * halves[i] + step * nrows, nrows)]
            return pltpu.make_async_remote_copy(src_ref=dst, dst_ref=dst, send_sem=send.at[i, 0], recv_sem=recv.at[i, step],
                                                **sibling)

        def finish(step, slot):
            for i in range(ng):
                local, remote = writes(i, step, slot)
                local.wait()
                remote.wait_send()

        @pl.when(r >= 2)
        def _():
            finish(r - 2, r % 2)

        for i in range(ng):
            tot = own[i][...]
            for j in range(3):
                tot = tot + rx[i][j].astype(F32)
            tiles[i][r % 2] = tot
            for cp in writes(i, r, r % 2):
                cp.start()

        @pl.when(r == 0)
        def _():
            for i in range(ns):
                term = [jnp.where(bc_ref[0] == kk, sl_own[i][...], sl[i][kk]).astype(F32) for kk in range(N_CHIP)]
                tiles[ng + i][...] = ((term[0] + term[1]) + term[2]) + term[3]
                for cp in small_writes(ng + i):
                    cp.start()

        @pl.when(r == steps - 1)
        def _():
            finish(steps - 2, (steps - 2) % 2)
            finish(steps - 1, (steps - 1) % 2)
            for i in range(ns):
                local, remote = small_writes(ng + i)
                local.wait()
                remote.wait_send()
                arriving(ng + i, 0, halves[ng + i]).wait_recv()
            for i in range(ng):
                for step in range(steps):
                    arriving(i, step, rows[i]).wait_recv()

    in_specs = [pl.BlockSpec((None, rows[i], g.shape[2]), lambda r, bc: (bc[0], r, 0)) for i, g in enumerate(sums)]
    in_specs += [pl.BlockSpec((3, rows[i], g.shape[2]), lambda r, bc: (0, r, 0)) for i, g in enumerate(sums)]
    in_specs += [pl.BlockSpec(s.shape, lambda r, bc: (0, 0, 0)) for s in smalls_slots]
    in_specs += [pl.BlockSpec(s.shape[1:], lambda r, bc: (0, 0)) for s in smalls_slots]
    out_shape = [jax.ShapeDtypeStruct((2 * g.shape[1], g.shape[2]), F32) for g in sums]
    out_shape += [jax.ShapeDtypeStruct((2 * s.shape[1], s.shape[2]), F32) for s in smalls_slots]
    scratch = [pltpu.VMEM((2, rows[i], g.shape[2]), F32) for i, g in enumerate(sums)]
    scratch += [pltpu.VMEM(s.shape[1:], F32) for s in smalls_slots]
    scratch += [pltpu.SemaphoreType.DMA((n, 2)), pltpu.SemaphoreType.DMA((n, 2)), pltpu.SemaphoreType.DMA((n, steps))]
    return list(pl.pallas_call(
        body, out_shape=out_shape,
        grid_spec=pltpu.PrefetchScalarGridSpec(num_scalar_prefetch=1, grid=(steps,), in_specs=in_specs,
                                               out_specs=[ANY] * n, scratch_shapes=scratch),
        compiler_params=_cp(("arbitrary",), VMEM_LIMIT), name=name)(bc_idx, *sums, *recvd, *smalls_slots, *smalls_own))


def _adamw_math(w, g, m, v):
    m2 = ADAM_B1 * m + (1.0 - ADAM_B1) * g
    v2 = ADAM_B2 * v + (1.0 - ADAM_B2) * (g * g)
    m_hat = m2 / (1.0 - ADAM_B1 ** ADAM_STEP)
    v_hat = v2 / (1.0 - ADAM_B2 ** ADAM_STEP)
    delta = -ADAM_LR * (m_hat / (jnp.sqrt(v_hat) + ADAM_EPS) + ADAM_WD * w)
    return delta, m2, v2


def _adamw_big(ws, gs, ms, vs, name, steps=8):
    n = len(ws)

    def body(*refs):
        for i in range(n):
            w_ref, g_ref, m_ref, v_ref = (refs[k * n + i] for k in range(4))
            d_ref, m2_ref, v2_ref, g2_ref = (refs[(4 + k) * n + i] for k in range(4))
            gv = g_ref[...]
            d_ref[...], m2_ref[...], v2_ref[...] = _adamw_math(w_ref[...], gv, m_ref[...], v_ref[...])
            g2_ref[...] = gv

    specs = [pl.BlockSpec((w.shape[0] // steps, w.shape[1]), lambda i: (i, 0)) for w in ws]
    ahead = [pl.BlockSpec((w.shape[0] // steps, w.shape[1]), lambda i: (i, 0), pipeline_mode=pl.Buffered(4)) for w in ws]
    shapes = [jax.ShapeDtypeStruct(w.shape, F32) for w in ws]

    def streamed(*refs):
        pltpu.emit_pipeline(body, grid=(steps,), in_specs=ahead * 4, out_specs=specs * 4)(*refs)

    outs = pl.pallas_call(
        streamed, in_specs=[ANY] * (4 * n), out_specs=[ANY] * (4 * n), out_shape=shapes * 4,
        compiler_params=_cp(None, VMEM_LIMIT), name=name)(*ws, *gs, *ms, *vs)
    return [tuple(outs[k * n + i] for k in range(4)) for i in range(n)]


def _adamw_small(b_idx, sv, sw, vecs, conv, ws):
    nv = len(vecs)
    cols = conv[0].shape[1]

    def body(b_ref, sv_ref, sw_ref, *refs):
        ins, outs = refs[:3 * nv + 6], refs[3 * nv + 6:]
        for i in range(nv):
            g = sv_ref[i:i + 1, :]
            w_ref, m_ref, v_ref = ins[3 * i:3 * i + 3]
            d_ref, m2_ref, v2_ref, g_ref = outs[4 * i:4 * i + 4]
            d_ref[...], m2_ref[...], v2_ref[...] = _adamw_math(w_ref[...], g, m_ref[...], v_ref[...])
            g_ref[...] = g
        g = sv_ref[8:8 + conv[0].shape[0], pl.ds(pl.multiple_of(b_ref[0] * cols, cols), cols)]
        w_ref, m_ref, v_ref = ins[3 * nv:3 * nv + 3]
        d_ref, m2_ref, v2_ref, g_ref = outs[4 * nv:4 * nv + 4]
        d_ref[...], m2_ref[...], v2_ref[...] = _adamw_math(w_ref[...], g, m_ref[...], v_ref[...])
        g_ref[...] = g
        w_ref, m_ref, v_ref = ins[3 * nv + 3:]
        d_ref, m2_ref, v2_ref, g_ref, one_ref = outs[4 * nv + 4:]
        g = sw_ref[...]
        d_ref[...], m2_ref[...], v2_ref[...] = _adamw_math(w_ref[...], g, m_ref[...], v_ref[...])
        g_ref[...] = g
        one_ref[...] = sv_ref[nv:nv + 1, 0:1]

    flat = [a for grp in vecs for a in grp] + list(conv) + list(ws)
    out_shape = [jax.ShapeDtypeStruct(grp[0].shape, F32) for grp in list(vecs) + [conv, ws] for _ in range(4)]
    out_shape += [jax.ShapeDtypeStruct((1, 1), F32)]
    vmem = pl.BlockSpec(memory_space=pltpu.VMEM)
    outs = pl.pallas_call(
        body, out_shape=out_shape, in_specs=[pl.BlockSpec(memory_space=pltpu.SMEM)] + [vmem] * (2 + len(flat)),
        out_specs=[vmem] * len(out_shape), name="adamw_small")(b_idx, sv, sw, *flat)
    return [tuple(outs[4 * i:4 * i + 4]) for i in range(nv + 2)], outs[4 * nv + 8]


def kernel(x, mem, norm_mix_g, w_in, conv_w, gm_ln_g, gm_ln_b, gm_ws, gm_bs, w_out, norm_x_g, norm_mem_g, w_q, w_kv, w_xo, norm_final_g, loss_target, m_norm_mix_g, m_w_in, m_conv_w, m_gm_ln_g, m_gm_ln_b, m_gm_ws, m_gm_bs, m_w_out, m_norm_x_g, m_norm_mem_g, m_w_q, m_w_kv, m_w_xo, m_norm_final_g, v_norm_mix_g, v_w_in, v_conv_w, v_gm_ln_g, v_gm_ln_b, v_gm_ws, v_gm_bs, v_w_out, v_norm_x_g, v_norm_mem_g, v_w_q, v_w_kv, v_w_xo, v_norm_final_g):
    t = x.shape[1]
    xi = lax.axis_index("x")
    yi = lax.axis_index("y")
    ci = lax.axis_index("c")
    b_idx = jnp.reshape(2 * xi + yi, (1,)).astype(jnp.int32)
    c_idx = jnp.reshape(ci, (1,)).astype(jnp.int32)

    x2d, mem2d, tgt = x[0], mem[0], loss_target[0]
    big = [w_in[0], w_out[0], w_q[0], w_kv[0], w_xo[0]]
    big_m = [m_w_in[0], m_w_out[0], m_w_q[0], m_w_kv[0], m_w_xo[0]]
    big_v = [v_w_in[0], v_w_out[0], v_w_q[0], v_w_kv[0], v_w_xo[0]]
    g3 = norm_final_g.reshape(1, D)

    def pad8(a):
        return jnp.pad(a, ((0, 8 - a.shape[0]), (0, 0)))

    own_blocks, (wc, wct, bsb) = _cast_shards(b_idx, big, gm_ws[0], gm_bs[0])

    proj, hb, win_f, cw8, (wq_f,) = _proj_gather(
        b_idx, x2d, norm_mix_g, own_blocks[0], pad8(conv_w[0]), [own_blocks[2]])
    mixin, (wout_f, wkv_f, wxo_f) = _mixer_fwd(
        proj, cw8, gm_ln_g, gm_ln_b, wc, bsb, [own_blocks[1], own_blocks[3], own_blocks[4]])
    wout2, wq2, wxo2 = wout_f.reshape(MIX, D), wq_f.reshape(D, D), wxo_f.reshape(D, D)
    k, v = _mem_fwd(mem2d, norm_mem_g, wkv_f)

    (loss_row, dmix, dx1b, h2b, dq, ob, dx2b, dk, dv, dg2, dg3) = _tail(
        x2d, tgt, mixin, wout2, wq2, wxo2, k, v, norm_x_g, g3)
    dwkv, dwkv_b, dgm = _mem_bwd(mem2d, norm_mem_g, dk, dv, wkv_f)
    dproj, sv, dwc, grad_x = _mixer_bwd(
        proj, dmix, cw8, gm_ln_g, gm_ln_b, wc, wct, bsb, win_f, x2d, dx1b, norm_mix_g, [dg2, dgm, dg3], loss_row)

    bc_idx = jnp.concatenate([b_idx, c_idx])
    sw = dwc.reshape(HEADS * CH, CH)
    dwin_sum, dwin_sum_b, psmall = _grad_matmul_pair(c_idx, hb, dproj, [sv, sw], [F32, BF16], dgm, name="grad_w_in")
    sums_b = [dwin_sum]
    send_b, recv_b, src_b, land_b, token_b = _exchange_begin([dwin_sum_b], psmall, "exchange_b_begin")

    dwxo, dwxo_b = _grad_matmul(ob, dx2b, token_b, name="grad_w_xo", tk=2048)
    dwq, dwq_b = _grad_matmul(h2b, dq, token_b, name="grad_w_q", tk=2048)
    dwout, dwout_b = _grad_matmul(mixin, dx1b, token_b, name="grad_w_out")
    ps_a = _pair_reduce(c_idx, [dwout, dwkv, dwq, dwxo], [dwout_b, dwkv_b, dwq_b, dwxo_b], [], "pair_reduce_a")
    sums_a, sums_a_b = list(ps_a[:4]), list(ps_a[4:8])
    send_a, recv_a, src_a, land_a, token_a = _exchange_begin(sums_a_b, [], "exchange_a_begin")

    src_b, rx2b = _exchange_end(send_b, recv_b, src_b, land_b, 1, [0, 1, 2], [token_a], "exchange_b_end")
    gwin, svf, swf = _chip_reduce(bc_idx, sums_b, rx2b[:1], rx2b[1:], src_b[1:], "chip_reduce_b")
    out_b = _adamw_big(big[:1], [gwin], big_m[:1], big_v[:1], "adamw_w_in")[0]

    row = lambda a: a.reshape(1, D)
    mat = lambda a: a.reshape(HEADS * CH, CH)
    small, loss = _adamw_small(
        b_idx, svf, swf,
        [(row(norm_mix_g), row(m_norm_mix_g), row(v_norm_mix_g)), (row(norm_x_g), row(m_norm_x_g), row(v_norm_x_g)),
         (row(norm_mem_g), row(m_norm_mem_g), row(v_norm_mem_g)), (row(norm_final_g), row(m_norm_final_g), row(v_norm_final_g)),
         (row(gm_ln_g), row(m_gm_ln_g), row(v_gm_ln_g)), (row(gm_ln_b), row(m_gm_ln_b), row(v_gm_ln_b)),
         (row(gm_bs), row(m_gm_bs), row(v_gm_bs))],
        (conv_w[0], m_conv_w[0], v_conv_w[0]), (mat(gm_ws), mat(m_gm_ws), mat(v_gm_ws)))

    def finish_a(part, src, land, after, tag):
        src, land = _exchange_end(send_a, recv_a, src, land, 4, part, after, "exchange_a%s_end" % tag)
        grads = _chip_reduce(bc_idx, [sums_a[i] for i in part], [land[i] for i in part], [], [], "chip_reduce_a" + tag,
                             steps=2)
        ids = [(1, 3, 2, 4)[i] for i in part]
        outs = _adamw_big([big[i] for i in ids], grads, [big_m[i] for i in ids], [big_v[i] for i in ids], "adamw_a" + tag,
                          steps=4)
        return src, land, outs

    src_a, land_a, (out_wout, out_wkv) = finish_a([0, 1], src_a, land_a, [out_b[0], small[0][0]], "1")
    _, _, (out_wq, out_wxo) = finish_a([2, 3], src_a, land_a, [out_wout[0]], "2")

    def unpack(k):
        vec = lambda i: small[i][k]
        return [vec(0), out_b[k][None], small[7][k][None], vec(4), vec(5), small[8][k].reshape(1, HEADS, CH, CH),
                vec(6).reshape(1, HEADS, CH), out_wout[k][None], vec(1), vec(2), out_wq[k][None], out_wkv[k][None],
                out_wxo[k][None], vec(3).reshape(D)]

    return (loss.reshape(()), grad_x[None], *unpack(3), *unpack(0), *unpack(1), *unpack(2))
```

```python
import functools
import math

import jax
import jax.numpy as jnp
from jax import lax
from jax.experimental import pallas as pl
from jax.experimental.pallas import tpu as pltpu

F32 = jnp.float32
BF16 = jnp.bfloat16
MESH = pl.DeviceIdType.MESH

D = 1024
SLAB = 1024
N_SLAB = 7
IN_DIM = N_SLAB * SLAB
MIX = 2 * SLAB
HEADS = 8
CH = 128
XH = 4
XD = D // XH
EPS = 1e-6
GELU_C = math.sqrt(2.0 / math.pi)
GELU_A = 0.044715
N_CHIP = 4
IN_BLK = IN_DIM // N_CHIP
IN_PIECE = 256
N_PIECE = IN_BLK // IN_PIECE
KV_BLK = 2 * D // N_CHIP

ADAM_LR, ADAM_B1, ADAM_B2, ADAM_EPS, ADAM_WD, ADAM_STEP = 0.001, 0.9, 0.999, 1e-08, 0.01, 10

VMEM_LIMIT = 60 * 1024 * 1024


def _cp(sem=None, vmem=None):
    return pltpu.CompilerParams(dimension_semantics=sem, vmem_limit_bytes=vmem)


def _full(shape, buffers=None):
    n = len(shape)
    if buffers is None:
        return pl.BlockSpec(shape, lambda *_: (0,) * n)
    return pl.BlockSpec(shape, lambda *_: (0,) * n, pipeline_mode=pl.Buffered(buffers))


ANY = pl.BlockSpec(memory_space=pl.ANY)


def _bdot(a, b):
    return jnp.dot(a.astype(BF16), b.astype(BF16), preferred_element_type=F32)


def _bdot_nt(a, b):
    return lax.dot_general(a.astype(BF16), b.astype(BF16), (((1,), (1,)), ((), ())), preferred_element_type=F32)


def _bdot_tn(a, b):
    return lax.dot_general(a.astype(BF16), b.astype(BF16), (((0,), (0,)), ((), ())), preferred_element_type=F32)


def _rms(x, g):
    r = lax.rsqrt(jnp.mean(x * x, axis=-1, keepdims=True) + EPS)
    return x * r * g, r


def _rms_bwd(dy, x, r, g):
    gdy = dy * g
    dx = r * gdy - x * (r * r * r) * jnp.mean(x * gdy, axis=-1, keepdims=True)
    dg = jnp.sum(dy * x * r, axis=0, keepdims=True)
    return dx, dg


def _gelu_parts(x):
    x2 = x * x
    t = jnp.tanh(GELU_C * (x + GELU_A * x * x2))
    val = 0.5 * x * (1.0 + t)
    grad = 0.5 * (1.0 + t) + 0.5 * x * (1.0 - t * t) * (GELU_C * (1.0 + 3.0 * GELU_A * x2))
    return val, grad


def _gelu(x):
    return 0.5 * x * (1.0 + jnp.tanh(GELU_C * (x + GELU_A * x * x * x)))


def _sigmoid(z):
    return 1.0 / (1.0 + jnp.exp(-z))


def _cast_shards(b_idx, arrs, gm_ws, gm_bs):
    n = len(arrs)
    steps = 4

    def body(b_ref, *refs):
        ws_ref, bs_ref = refs[n:n + 2]
        outs = refs[n + 2:2 * n + 2]
        wc_ref, wct_ref, bsb_ref = refs[2 * n + 2:]
        for p in range(N_PIECE):
            outs[0][p] = refs[0][:, pl.ds(p * IN_PIECE, IN_PIECE)].astype(BF16)
        for i in range(1, n):
            outs[i][...] = refs[i][...].astype(BF16)

        @pl.when(pl.program_id(0) == 0)
        def _():
            causal = lax.broadcasted_iota(jnp.int32, (CH, CH), 0) >= lax.broadcasted_iota(jnp.int32, (CH, CH), 1)
            for h in range(HEADS):
                w = jnp.where(causal, ws_ref[h], 0.0)
                wc_ref[h] = w.astype(BF16)
                wct_ref[h] = w.T.astype(BF16)
                bsb_ref[h] = jnp.broadcast_to(bs_ref[h:h + 1, :], (CH, CH)).T

    rows = [a.shape[0] // steps for a in arrs]
    in_specs = [pl.BlockSpec((rows[i], a.shape[1]), lambda i, b: (i, 0)) for i, a in enumerate(arrs)]
    in_specs += [pl.BlockSpec((HEADS, CH, CH), lambda i, b: (0, 0, 0)), pl.BlockSpec((HEADS, CH), lambda i, b: (0, 0))]
    out_specs = [pl.BlockSpec((None, N_PIECE, rows[0], IN_PIECE), lambda i, b: (b[0], 0, i, 0))]
    out_specs += [pl.BlockSpec((None, rows[i], a.shape[1]), lambda i, b: (b[0], i, 0)) for i, a in enumerate(arrs) if i > 0]
    out_specs += [pl.BlockSpec((HEADS, CH, CH), lambda i, b: (0, 0, 0))] * 3
    out_shape = [jax.ShapeDtypeStruct((N_CHIP, N_PIECE, arrs[0].shape[0], IN_PIECE), BF16)]
    out_shape += [jax.ShapeDtypeStruct((N_CHIP,) + a.shape, BF16) for a in arrs[1:]]
    out_shape += [jax.ShapeDtypeStruct((HEADS, CH, CH), dt) for dt in (BF16, BF16, F32)]
    outs = pl.pallas_call(
        body, out_shape=out_shape,
        grid_spec=pltpu.PrefetchScalarGridSpec(num_scalar_prefetch=1, grid=(steps,), in_specs=in_specs, out_specs=out_specs),
        compiler_params=_cp(("arbitrary",)), name="cast_shards")(b_idx, *arrs, gm_ws, gm_bs)
    return outs[:n], outs[n:]


def _proj_gather(b_idx, x, g, win_own, cw8s, more, tm=1024):
    t = x.shape[0]
    ni = t // tm
    nm = len(more)
    steps = N_CHIP * N_PIECE
    near0, far0 = N_PIECE, 3 * N_PIECE

    def piece_at(step, own):
        k = step - near0
        near, far = (step >= near0) & (step < far0), step >= far0
        block = jnp.where(far, own ^ 3, jnp.where(near, own ^ jnp.where(lax.rem(k, 2) == 0, 2, 1), own))
        return block, jnp.where(far, step - far0, jnp.where(near, lax.div(k, 2), step))

    def body(*refs):
        b_ref, x_any, g_ref, win_in, cw_in = refs[:5]
        o_ref, hb_any, win_f, cw_out = refs[5 + nm:9 + nm]
        more_out = refs[9 + nm:9 + 2 * nm]
        hbuf, xbuf, wv, cw_s, cw_r, loc = refs[9 + 2 * nm:15 + 2 * nm]
        g_in = _Gather([win_f.at[:, p] for p in range(N_PIECE)], *refs[15 + 2 * nm:19 + 2 * nm])
        g_more = _Gather(more_out, *refs[19 + 2 * nm:23 + 2 * nm])
        s = pl.program_id(0)
        x, y, c, chips = _coords()
        b = 2 * x + y
        blks = [2 * chip[0] + chip[1] for chip in chips]

        def cw_cols(blk):
            return cw_out.at[:, pl.ds(blk * (D // N_CHIP), D // N_CHIP)]

        def cw_copy(k, blk):
            src = cw_in if blk is None else cw_cols(blk)
            return pltpu.make_async_remote_copy(src_ref=src, dst_ref=cw_cols(b if blk is None else blk), send_sem=cw_s.at[k],
                                                recv_sem=cw_r.at[k], device_id=(*chips[k], c), device_id_type=MESH)

        cw_local = pltpu.make_async_copy(cw_in, cw_cols(b), loc.at[1])
        hb_copy = pltpu.make_async_copy(hbuf, hb_any, loc.at[0])

        def load(step):
            slot = lax.rem(step, 2)
            block, piece = piece_at(step, b_ref[0])
            return pltpu.make_async_copy(win_f.at[block, piece], wv.at[slot], loc.at[2 + slot])

        def chunk(i):
            return pltpu.make_async_copy(x_any.at[pl.ds(i * tm, tm)], xbuf.at[i % 2], loc.at[4 + i % 2])

        def first():
            g_in.start()
            cw_local.start()
            for k in range(3):
                cw_copy(k, None).start()
            load(0).start()
            chunk(0).start()
            for i in range(ni):
                if i + 1 < ni:
                    chunk(i + 1).start()
                chunk(i).wait()
                h, _ = _rms(xbuf[i % 2], g_ref[...])
                hbuf[pl.ds(i * tm, tm), :] = h.astype(BF16)
            hb_copy.start()

        events = {step: [] for step in range(steps)}
        events[0].append(first)
        for p in range(N_PIECE):
            events[2 * p + 2].append(functools.partial(g_in.hop, [p]))
            events[near0 + 2 * p - 1].append(functools.partial(g_in.near_ready, [p]))
            events[far0 + p - 2].append(functools.partial(g_in.far, [p]))
            events[far0 + p - 1].append(functools.partial(g_in.far_ready, [p]))
        events[2 * N_PIECE + 1].append(g_more.start)
        for step, todo in events.items():
            if todo:
                @pl.when(s == step)
                def _(todo=todo):
                    for do in todo:
                        do()

        @pl.when(s + 1 < steps)
        def _():
            load(s + 1).start()

        load(s).wait()
        for i in range(ni):
            rows = pl.ds(i * tm, tm)
            o_ref[rows, :] = jnp.dot(hbuf[rows, :], wv[lax.rem(s, 2)], preferred_element_type=F32).astype(BF16)

        @pl.when(s == steps - 1)
        def _():
            g_more.hop()
            g_more.far()
            for k in range(3):
                cw_copy(k, blks[k]).wait_recv()
            for k in range(3):
                cw_copy(k, None).wait_send()
            cw_local.wait()
            hb_copy.wait()
            g_more.near_ready()
            g_more.far_ready()
            g_in.drain()
            g_more.drain()

    def out_col(s, b):
        block, piece = piece_at(s, b[0])
        return 0, block * N_PIECE + piece

    in_specs = [ANY, pl.BlockSpec((1, D), lambda s, b: (0, 0)), ANY, ANY] + [ANY] * nm
    out_specs = [pl.BlockSpec((t, IN_PIECE), out_col), ANY, ANY, ANY] + [ANY] * nm
    outs = pl.pallas_call(
        body, out_shape=[jax.ShapeDtypeStruct((t, IN_DIM), BF16), jax.ShapeDtypeStruct((t, D), BF16),
                         jax.ShapeDtypeStruct(win_own.shape, BF16), jax.ShapeDtypeStruct((8, D), F32)]
        + [jax.ShapeDtypeStruct(f.shape, f.dtype) for f in more],
        grid_spec=pltpu.PrefetchScalarGridSpec(
            num_scalar_prefetch=1, grid=(steps,), in_specs=in_specs, out_specs=out_specs,
            scratch_shapes=[pltpu.VMEM((t, D), BF16), pltpu.VMEM((2, tm, D), F32), pltpu.VMEM((2, D, IN_PIECE), BF16)]
            + [pltpu.SemaphoreType.DMA((3,))] * 2 + [pltpu.SemaphoreType.DMA((6,))]
            + _gather_sems(N_PIECE) + _gather_sems(nm)),
        input_output_aliases={3: 2, **{5 + w: 4 + w for w in range(nm)}},
        compiler_params=_cp(("arbitrary",), VMEM_LIMIT), name="proj_gather")(b_idx, x, g, win_own, cw8s, *more)
    return outs[0], outs[1], outs[2], outs[3], outs[4:]


class _Gather:
    def __init__(self, outs, ici_s, ici_r, d2d_s, d2d_r):
        x, y, c, _ = _coords()
        self.outs, self.c = outs, c
        self.sems = ici_s, ici_r, d2d_s, d2d_r
        self.b, self.bx, self.by, self.bd = 2 * x + y, 2 * (1 - x) + y, 2 * x + (1 - y), 2 * (1 - x) + (1 - y)
        self.xn, self.yn, self.sib = (1 - x, y, c), (x, 1 - y, c), (x, y, 1 - c)

    def piece(self, w, blk, hc, quarter=None):
        hr = self.outs[w].shape[1] // 2
        if quarter is None:
            return self.outs[w].at[blk, pl.ds(hc * hr, hr)]
        return self.outs[w].at[blk, pl.ds(hc * hr + quarter * (hr // 2), hr // 2)]

    def ici(self, w, k, ref, to):
        return pltpu.make_async_remote_copy(src_ref=ref, dst_ref=ref, send_sem=self.sems[0].at[w, k],
                                            recv_sem=self.sems[1].at[w, k], device_id=to, device_id_type=MESH)

    def d2d(self, w, k, ref):
        return pltpu.make_async_remote_copy(src_ref=ref, dst_ref=ref, send_sem=self.sems[2].at[w, k],
                                            recv_sem=self.sems[3].at[w, k], device_id=self.sib, device_id_type=MESH)

    def all(self):
        return range(len(self.outs))

    def start(self):
        for w in self.all():
            mine = self.piece(w, self.b, self.c)
            self.ici(w, 0, mine, self.xn).start()
            self.ici(w, 1, mine, self.yn).start()

    def hop(self, ws=None):
        c = self.c
        for w in ws or self.all():
            self.ici(w, 0, self.piece(w, self.bx, c), self.xn).wait_recv()
            self.ici(w, 1, self.piece(w, self.by, c), self.yn).wait_recv()
            self.ici(w, 2, self.piece(w, self.bx, c, 0), self.yn).start()
            self.ici(w, 3, self.piece(w, self.by, c, 1), self.xn).start()
            self.d2d(w, 0, self.piece(w, self.bx, c)).start()
            self.d2d(w, 1, self.piece(w, self.by, c)).start()

    def near_ready(self, ws=None):
        for w in ws or self.all():
            self.d2d(w, 0, self.piece(w, self.bx, 1 - self.c)).wait_recv()
            self.d2d(w, 1, self.piece(w, self.by, 1 - self.c)).wait_recv()

    def far(self, ws=None):
        c = self.c
        for w in ws or self.all():
            self.ici(w, 2, self.piece(w, self.bd, c, 0), self.yn).wait_recv()
            self.ici(w, 3, self.piece(w, self.bd, c, 1), self.xn).wait_recv()
            self.d2d(w, 2, self.piece(w, self.bd, c, 0)).start()
            self.d2d(w, 3, self.piece(w, self.bd, c, 1)).start()

    def far_ready(self, ws=None):
        for w in ws or self.all():
            self.d2d(w, 2, self.piece(w, self.bd, 1 - self.c, 0)).wait_recv()
            self.d2d(w, 3, self.piece(w, self.bd, 1 - self.c, 1)).wait_recv()

    def drain(self):
        c = self.c
        for w in self.all():
            mine = self.piece(w, self.b, c)
            self.ici(w, 0, mine, self.xn).wait_send()
            self.ici(w, 1, mine, self.yn).wait_send()
            self.ici(w, 2, self.piece(w, self.bx, c, 0), self.yn).wait_send()
            self.ici(w, 3, self.piece(w, self.by, c, 1), self.xn).wait_send()
            self.d2d(w, 0, self.piece(w, self.bx, c)).wait_send()
            self.d2d(w, 1, self.piece(w, self.by, c)).wait_send()
            self.d2d(w, 2, self.piece(w, self.bd, c, 0)).wait_send()
            self.d2d(w, 3, self.piece(w, self.bd, c, 1)).wait_send()


def _gather_sems(nw):
    return [pltpu.SemaphoreType.DMA((max(nw, 1), 4))] * 4


def _mixer_fwd(proj, cw8, lng, lnb, wc, bsb, fulls, tm=256):
    t = proj.shape[0]
    nt = t // tm
    nch = tm // CH
    nw = len(fulls)

    def body(*refs):
        p_ref, cw_ref, lng_ref, lnb_ref, wc_ref, bsb_ref = refs[:6]
        mix_ref = refs[6 + nw]
        w_outs = refs[7 + nw:7 + 2 * nw]
        prev_ref = refs[7 + 2 * nw]
        gather = _Gather(w_outs, *refs[8 + 2 * nw:])

        @pl.when(pl.program_id(0) == 0)
        def _():
            gather.start()
            prev_ref[...] = jnp.zeros_like(prev_ref)

        @pl.when(pl.program_id(0) == nt // 2)
        def _():
            gather.hop()

        @pl.when(pl.program_id(0) == nt - 1)
        def _():
            gather.far()

        rows = lax.broadcasted_iota(jnp.int32, (tm, CH), 0)
        for s in range(HEADS):
            cs = pl.ds(CH * s, CH)

            def slab(k):
                return p_ref[:, pl.ds(k * SLAB + CH * s, CH)].astype(F32)

            gb, gc, xa, za = slab(0), slab(1), slab(2), slab(3)
            cx = gc * xa
            p6 = jnp.broadcast_to(prev_ref[6:7, cs], (tm, CH))
            p7 = jnp.broadcast_to(prev_ref[7:8, cs], (tm, CH))
            c1 = jnp.where(rows == 0, p7, pltpu.roll(cx, 1, 0))
            c2 = jnp.where(rows == 0, p6, jnp.where(rows == 1, p7, pltpu.roll(cx, 2, 0)))
            prev_ref[:, cs] = cx[tm - 8:, :]
            cv = cw_ref[0:1, cs] * c2 + cw_ref[1:2, cs] * c1 + cw_ref[2:3, cs] * cx
            mix_ref[:, cs] = (gb * cv * (za * _sigmoid(za))).astype(BF16)

            u, v, zb = slab(4), slab(5), slab(6)
            ug, vg = _gelu(u), _gelu(v)
            dlt = vg - jnp.mean(vg, axis=-1, keepdims=True)
            vhat = dlt * lax.rsqrt(jnp.mean(dlt * dlt, axis=-1, keepdims=True) + EPS)
            vn = (vhat * lng_ref[:, cs] + lnb_ref[:, cs]).astype(BF16)
            gate = ug * (zb * _sigmoid(zb))
            for c in range(nch):
                rs = slice(CH * c, CH * (c + 1))
                sp = jnp.dot(wc_ref[s], vn[rs], preferred_element_type=F32) + bsb_ref[s]
                mix_ref[rs, pl.ds(SLAB + CH * s, CH)] = (gate[rs] * sp).astype(BF16)

        @pl.when(pl.program_id(0) == nt - 1)
        def _():
            gather.near_ready()
            gather.far_ready()
            gather.drain()

    sems = _gather_sems(nw)
    outs = pl.pallas_call(
        body, grid=(nt,),
        in_specs=[pl.BlockSpec((tm, IN_DIM), lambda i: (i, 0)), _full((8, D)), _full((1, D)), _full((1, D)),
                  _full((HEADS, CH, CH)), _full((HEADS, CH, CH))] + [ANY] * nw,
        out_specs=[pl.BlockSpec((tm, MIX), lambda i: (i, 0))] + [ANY] * nw,
        out_shape=[jax.ShapeDtypeStruct((t, MIX), BF16)] + [jax.ShapeDtypeStruct(f.shape, f.dtype) for f in fulls],
        input_output_aliases={6 + w: 1 + w for w in range(nw)},
        scratch_shapes=[pltpu.VMEM((8, D), F32)] + sems,
        compiler_params=_cp(("arbitrary",), VMEM_LIMIT), name="mixer_fwd")(proj, cw8, lng, lnb, wc, bsb, *fulls)
    return outs[0], outs[1:]


def _mem_fwd(mem, gm, wkv_f):
    n_mem = mem.shape[0]

    def body(mem_ref, gm_ref, w_any, k_ref, v_ref, w_ref, sem):
        fetched = [pltpu.make_async_copy(w_any.at[j], w_ref.at[j], sem.at[j]) for j in range(N_CHIP)]
        for cp in fetched:
            cp.start()
        m, _ = _rms(mem_ref[...], gm_ref[...])
        mb = m.astype(BF16)
        for j in range(N_CHIP):
            dst = k_ref if j < 2 else v_ref
            fetched[j].wait()
            dst[:, pl.ds(KV_BLK * (j % 2), KV_BLK)] = jnp.dot(mb, w_ref[j], preferred_element_type=F32).astype(BF16)

    vmem = pl.BlockSpec(memory_space=pltpu.VMEM)
    return pl.pallas_call(
        body, out_shape=[jax.ShapeDtypeStruct((n_mem, D), BF16), jax.ShapeDtypeStruct((n_mem, D), BF16)],
        in_specs=[vmem, vmem, ANY], out_specs=[vmem, vmem],
        scratch_shapes=[pltpu.VMEM((N_CHIP, D, KV_BLK), BF16), pltpu.SemaphoreType.DMA((N_CHIP,))],
        compiler_params=_cp(None, VMEM_LIMIT), name="mem_fwd")(mem, gm, wkv_f)


def _tail(x, tgt, mixin, wout, wq, wxo, k, v, g2, g3, tm=512, sub=512):
    t = x.shape[0]
    n_mem = k.shape[0]
    scale = 1.0 / math.sqrt(XD)

    def body(x_ref, tgt_ref, mix_ref, wout_ref, wq_ref, wxo_ref, k_ref, v_ref, g2_ref, g3_ref,
             loss_ref, dmix_ref, dx1b_ref, h2_ref, dq_ref, o_ref, dx2b_ref, dk_ref, dv_ref, dg2_ref, dg3_ref):
        @pl.when(pl.program_id(0) == 0)
        def _():
            loss_ref[...] = jnp.zeros_like(loss_ref)
            dk_ref[...] = jnp.zeros_like(dk_ref)
            dv_ref[...] = jnp.zeros_like(dv_ref)
            dg2_ref[...] = jnp.zeros_like(dg2_ref)
            dg3_ref[...] = jnp.zeros_like(dg3_ref)

        g2, g3 = g2_ref[...], g3_ref[...]
        for sb in range(tm // sub):
            rs = pl.ds(sub * sb, sub)
            x1 = x_ref[rs, :] + jnp.dot(mix_ref[rs, :], wout_ref[...], preferred_element_type=F32)
            h2, r2 = _rms(x1, g2)
            h2b = h2.astype(BF16)
            h2_ref[rs, :] = h2b
            q = jnp.dot(h2b, wq_ref[...], preferred_element_type=F32).astype(BF16)
            probs, outs = [], []
            for hd in range(XH):
                hs = pl.ds(XD * hd, XD)
                s = _bdot_nt(q[:, XD * hd:XD * (hd + 1)], k_ref[:, hs]) * scale
                e = jnp.exp(s - jnp.max(s, axis=-1, keepdims=True))
                p = e / jnp.sum(e, axis=-1, keepdims=True)
                probs.append(p)
                outs.append(_bdot(p, v_ref[:, hs]))
            ob = jnp.concatenate(outs, axis=-1).astype(BF16)
            o_ref[rs, :] = ob
            x2 = x1 + jnp.dot(ob, wxo_ref[...], preferred_element_type=F32)
            y, r3 = _rms(x2, g3)
            diff = y - tgt_ref[rs, :]
            row_loss = jnp.sum(diff * diff, axis=-1, keepdims=True)
            loss_ref[...] += jnp.broadcast_to(jnp.sum(row_loss, axis=0, keepdims=True) * (0.5 / D), loss_ref.shape)

            dx2, dg3 = _rms_bwd(diff * (1.0 / D), x2, r3, g3)
            dg3_ref[...] += dg3
            dx2b = dx2.astype(BF16)
            dx2b_ref[rs, :] = dx2b
            do = _bdot_nt(dx2b, wxo_ref[...])
            dqs = []
            for hd in range(XH):
                hs = pl.ds(XD * hd, XD)
                p = probs[hd]
                do_h = do[:, XD * hd:XD * (hd + 1)]
                dv_ref[:, hs] += _bdot_tn(p, do_h)
                dp = _bdot_nt(do_h, v_ref[:, hs])
                ds = p * (dp - jnp.sum(dp * p, axis=-1, keepdims=True))
                dqs.append(_bdot(ds, k_ref[:, hs]) * scale)
                dk_ref[:, hs] += _bdot_tn(ds, q[:, XD * hd:XD * (hd + 1)]) * scale
            dq = jnp.concatenate(dqs, axis=-1).astype(BF16)
            dq_ref[rs, :] = dq
            dx1n, dg2 = _rms_bwd(_bdot_nt(dq, wq_ref[...]), x1, r2, g2)
            dg2_ref[...] += dg2
            dx1b = (dx2 + dx1n).astype(BF16)
            dx1b_ref[rs, :] = dx1b
            dmix_ref[rs, :] = _bdot_nt(dx1b, wout_ref[...]).astype(BF16)

    tok = lambda w: pl.BlockSpec((tm, w), lambda i: (i, 0))
    return pl.pallas_call(
        body, grid=(t // tm,),
        in_specs=[tok(D), tok(D), tok(MIX), _full((MIX, D), 1), _full((D, D), 1), _full((D, D), 1),
                  _full((n_mem, D), 1), _full((n_mem, D), 1), _full((1, D)), _full((1, D))],
        out_specs=[_full((1, D)), tok(MIX), tok(D), tok(D), tok(D), tok(D), tok(D),
                   _full((n_mem, D)), _full((n_mem, D)), _full((1, D)), _full((1, D))],
        out_shape=[jax.ShapeDtypeStruct((1, D), F32), jax.ShapeDtypeStruct((t, MIX), BF16),
                   jax.ShapeDtypeStruct((t, D), BF16),
                   jax.ShapeDtypeStruct((t, D), BF16), jax.ShapeDtypeStruct((t, D), BF16),
                   jax.ShapeDtypeStruct((t, D), BF16), jax.ShapeDtypeStruct((t, D), BF16),
                   jax.ShapeDtypeStruct((n_mem, D), F32), jax.ShapeDtypeStruct((n_mem, D), F32),
                   jax.ShapeDtypeStruct((1, D), F32), jax.ShapeDtypeStruct((1, D), F32)],
        compiler_params=_cp(("arbitrary",), VMEM_LIMIT), name="tail")(x, tgt, mixin, wout, wq, wxo, k, v, g2, g3)


def _mem_bwd(mem, gm, dk, dv, wkv_f):
    def body(mem_ref, gm_ref, dk_ref, dv_ref, w_any, dw_any, dwb_any, dgm_ref, dw_ref, dwb_ref, w_ref, sem):
        fetched = [pltpu.make_async_copy(w_any.at[j], w_ref.at[j], sem.at[2, j]) for j in range(N_CHIP)]
        for cp in fetched:
            cp.start()
        mem_v = mem_ref[...]
        m, rm = _rms(mem_v, gm_ref[...])
        mb = m.astype(BF16)
        dm = jnp.zeros_like(mem_v)
        written = []
        for j in range(N_CHIP):
            src = dk_ref if j < 2 else dv_ref
            dkv = src[:, pl.ds(KV_BLK * (j % 2), KV_BLK)].astype(BF16)
            dw = _bdot_tn(mb, dkv)
            dw_ref[j] = dw
            dwb_ref[j] = dw.astype(BF16)
            written += [pltpu.make_async_copy(dw_ref.at[j], dw_any.at[j], sem.at[0, j]),
                        pltpu.make_async_copy(dwb_ref.at[j], dwb_any.at[j], sem.at[1, j])]
            for cp in written[-2:]:
                cp.start()
            fetched[j].wait()
            dm = dm + _bdot_nt(dkv, w_ref[j])
        dgm_ref[...] = jnp.sum(dm * mem_v * rm, axis=0, keepdims=True)
        for cp in written:
            cp.wait()

    vmem = pl.BlockSpec(memory_space=pltpu.VMEM)
    return pl.pallas_call(
        body, out_shape=[jax.ShapeDtypeStruct((N_CHIP, D, KV_BLK), F32), jax.ShapeDtypeStruct((N_CHIP, D, KV_BLK), BF16),
                         jax.ShapeDtypeStruct((1, D), F32)],
        in_specs=[vmem] * 4 + [ANY], out_specs=[ANY, ANY, vmem],
        scratch_shapes=[pltpu.VMEM((N_CHIP, D, KV_BLK), F32), pltpu.VMEM((N_CHIP, D, KV_BLK), BF16),
                        pltpu.VMEM((N_CHIP, D, KV_BLK), BF16), pltpu.SemaphoreType.DMA((3, N_CHIP))],
        compiler_params=_cp(None, VMEM_LIMIT), name="mem_bwd")(mem, gm, dk, dv, wkv_f)


def _mixer_bwd(proj, dmix, cw8, lng, lnb, wc, wct, bsb, win_f, x, dx1, g1, rows123, row7, tm=256):
    t = proj.shape[0]
    nt = t // tm
    nch = tm // CH
    hb = 16
    pair = 2 * CH
    assert pair == IN_PIECE

    def body(p_ref, pgc_ref, pxa_ref, dm_ref, cw_ref, lng_ref, lnb_ref, wc_ref, wct_ref, bsb_ref, w_ref, x_ref,
             dx1_ref, g1_ref, r1_ref, r2_ref, r3_ref, r7_ref, dp_ref, sv_ref, dwc_ref, gx_ref,
             next_ref, dh_ref):
        i = pl.program_id(0)
        dg1_ref, dlng_ref, dlnb_ref, dbs_ref = (sv_ref.at[pl.ds(r, 1)] for r in (0, 4, 5, 6))
        dcw_ref = sv_ref.at[pl.ds(8, 8)]

        @pl.when(i == 0)
        def _():
            next_ref[...] = jnp.zeros_like(next_ref)
            sv_ref[...] = jnp.zeros_like(sv_ref)
            dwc_ref[...] = jnp.zeros_like(dwc_ref)
            for r, ref in ((1, r1_ref), (2, r2_ref), (3, r3_ref), (7, r7_ref)):
                sv_ref[r:r + 1, :] = ref[...]

        first_tile = i == nt - 1
        rows = lax.broadcasted_iota(jnp.int32, (tm, CH), 0)
        ones8 = jnp.ones((8, CH), BF16)
        for s in range(HEADS):
            cs = pl.ds(CH * s, CH)

            def slab(k):
                return p_ref[:, pl.ds(k * SLAB + CH * s, CH)].astype(F32)

            gb, gc, xa, za = slab(0), slab(1), slab(2), slab(3)
            da = dm_ref[:, cs].astype(F32)
            cx = gc * xa
            cxp = pgc_ref[:, cs].astype(F32) * pxa_ref[:, cs].astype(F32)
            cxp = jnp.where(first_tile, jnp.zeros_like(cxp), cxp)
            p6 = jnp.broadcast_to(cxp[hb - 2:hb - 1, :], (tm, CH))
            p7 = jnp.broadcast_to(cxp[hb - 1:hb, :], (tm, CH))
            c1 = jnp.where(rows == 0, p7, pltpu.roll(cx, 1, 0))
            c2 = jnp.where(rows == 0, p6, jnp.where(rows == 1, p7, pltpu.roll(cx, 2, 0)))
            w0, w1, w2 = cw_ref[0:1, cs], cw_ref[1:2, cs], cw_ref[2:3, cs]
            cv = w0 * c2 + w1 * c1 + w2 * cx
            sg = _sigmoid(za)
            sa = za * sg
            dcv = da * gb * sa
            dp_ref[:, pl.ds(0 * SLAB + CH * s, CH)] = (da * cv * sa).astype(BF16)
            dp_ref[:, pl.ds(3 * SLAB + CH * s, CH)] = (da * gb * cv * (sg * (1.0 + za * (1.0 - sg)))).astype(BF16)
            n0 = jnp.broadcast_to(next_ref[0:1, cs], (tm, CH))
            n1 = jnp.broadcast_to(next_ref[1:2, cs], (tm, CH))
            u1 = jnp.where(rows == tm - 1, n0, pltpu.roll(dcv, tm - 1, 0))
            u2 = jnp.where(rows == tm - 2, n0, jnp.where(rows == tm - 1, n1, pltpu.roll(dcv, tm - 2, 0)))
            next_ref[:, cs] = dcv[0:8, :]
            dcx = w2 * dcv + w1 * u1 + w0 * u2
            dp_ref[:, pl.ds(1 * SLAB + CH * s, CH)] = (dcx * xa).astype(BF16)
            dp_ref[:, pl.ds(2 * SLAB + CH * s, CH)] = (dcx * gc).astype(BF16)
            dcw_ref[0:1, cs] += jnp.sum(dcv * c2, axis=0, keepdims=True)
            dcw_ref[1:2, cs] += jnp.sum(dcv * c1, axis=0, keepdims=True)
            dcw_ref[2:3, cs] += jnp.sum(dcv * cx, axis=0, keepdims=True)

            u, v, zb = slab(4), slab(5), slab(6)
            db = dm_ref[:, pl.ds(SLAB + CH * s, CH)].astype(F32)
            ug, ugrad = _gelu_parts(u)
            vg, vgrad = _gelu_parts(v)
            dlt = vg - jnp.mean(vg, axis=-1, keepdims=True)
            rstd = lax.rsqrt(jnp.mean(dlt * dlt, axis=-1, keepdims=True) + EPS)
            vhat = dlt * rstd
            lg = lng_ref[:, cs]
            vn = (vhat * lg + lnb_ref[:, cs]).astype(BF16)
            sgb = _sigmoid(zb)
            szb = zb * sgb
            sps, dvns = [], []
            dbs = jnp.zeros((8, CH), F32)
            dwc = jnp.zeros((CH, CH), F32)
            for c in range(nch):
                rs = slice(CH * c, CH * (c + 1))
                sp = jnp.dot(wc_ref[s], vn[rs], preferred_element_type=F32) + bsb_ref[s]
                dsp = (db[rs] * ug[rs] * szb[rs]).astype(BF16)
                dbs = dbs + lax.dot_general(ones8, dsp, (((1,), (1,)), ((), ())), preferred_element_type=F32)
                dwc = dwc + lax.dot_general(dsp, vn[rs], (((1,), (1,)), ((), ())), preferred_element_type=F32)
                dvns.append(jnp.dot(wct_ref[s], dsp, preferred_element_type=F32))
                sps.append(sp)
            sp = jnp.concatenate(sps, axis=0)
            dvn = jnp.concatenate(dvns, axis=0)
            dbs_ref[:, cs] += dbs[0:1]
            dwc_ref[s] += dwc
            dlng_ref[:, cs] += jnp.sum(dvn * vhat, axis=0, keepdims=True)
            dlnb_ref[:, cs] += jnp.sum(dvn, axis=0, keepdims=True)
            dvhat = dvn * lg
            dvg = rstd * (dvhat - jnp.mean(dvhat, axis=-1, keepdims=True)
                          - vhat * jnp.mean(dvhat * vhat, axis=-1, keepdims=True))
            dp_ref[:, pl.ds(4 * SLAB + CH * s, CH)] = (db * sp * szb * ugrad).astype(BF16)
            dp_ref[:, pl.ds(5 * SLAB + CH * s, CH)] = (dvg * vgrad).astype(BF16)
            dp_ref[:, pl.ds(6 * SLAB + CH * s, CH)] = (db * ug * sp * (sgb * (1.0 + zb * (1.0 - sgb)))).astype(BF16)

            if s % 2 == 1:
                part = None
                for k in range(N_SLAB):
                    col = k * SLAB + pair * (s // 2)
                    blk, off = divmod(col, IN_BLK)
                    term = lax.dot_general(dp_ref[:, pl.ds(col, pair)], w_ref[blk, off // IN_PIECE],
                                           (((1,), (1,)), ((), ())), preferred_element_type=F32)
                    part = term if part is None else part + term
                if s == 1:
                    dh_ref[...] = part
                else:
                    dh_ref[...] += part

        xv = x_ref[...]
        r = lax.rsqrt(jnp.mean(xv * xv, axis=-1, keepdims=True) + EPS)
        dxn, dg = _rms_bwd(dh_ref[...], xv, r, g1_ref[...])
        gx_ref[...] = dx1_ref[...].astype(F32) + dxn
        dg1_ref[...] += dg

        @pl.when(i == nt - 1)
        def _():
            tril = lax.broadcasted_iota(jnp.int32, (CH, CH), 0) >= lax.broadcasted_iota(jnp.int32, (CH, CH), 1)
            for s in range(HEADS):
                dwc_ref[s] = jnp.where(tril, dwc_ref[s], 0.0)

    rev = lambda i: nt - 1 - i
    halo = lambda col: pl.BlockSpec((hb, SLAB), lambda i: (jnp.maximum(rev(i) * (tm // hb) - 1, 0), col))
    tok = lambda w: pl.BlockSpec((tm, w), lambda i: (rev(i), 0))
    return pl.pallas_call(
        body, grid=(nt,),
        in_specs=[tok(IN_DIM), halo(1), halo(2), tok(MIX), _full((8, D)), _full((1, D)), _full((1, D)),
                  _full((HEADS, CH, CH)), _full((HEADS, CH, CH)), _full((HEADS, CH, CH)),
                  _full((N_CHIP, N_PIECE, D, IN_PIECE), 1), tok(D), tok(D)] + [_full((1, D))] * 5,
        out_specs=[tok(IN_DIM), _full((16, D)), _full((HEADS, CH, CH)), tok(D)],
        out_shape=[jax.ShapeDtypeStruct((t, IN_DIM), BF16), jax.ShapeDtypeStruct((16, D), F32),
                   jax.ShapeDtypeStruct((HEADS, CH, CH), F32), jax.ShapeDtypeStruct((t, D), F32)],
        scratch_shapes=[pltpu.VMEM((8, D), F32), pltpu.VMEM((tm, D), F32)],
        compiler_params=_cp(("arbitrary",), VMEM_LIMIT), name="mixer_bwd")(
            proj, proj, proj, dmix, cw8, lng, lnb, wc, wct, bsb, win_f, x, dx1, g1, *rows123, row7)


def _grad_matmul(a, b, after, *, name, tk=1024):
    t, m = a.shape
    n = b.shape[1]
    nk = t // tk

    def body(a_ref, b_ref, after_ref, o_ref, ob_ref):
        kk = pl.program_id(0)
        part = lax.dot_general(a_ref[...], b_ref[...], (((0,), (0,)), ((), ())), preferred_element_type=F32)

        @pl.when(kk == 0)
        def _():
            o_ref[...] = part

        @pl.when(kk > 0)
        def _():
            o_ref[...] += part

        @pl.when(kk == nk - 1)
        def _():
            ob_ref[...] = o_ref[...].astype(BF16)

    o_spec = pl.BlockSpec((m, n), lambda k: (0, 0))
    o32, o16 = pl.pallas_call(
        body, grid=(nk,), in_specs=[pl.BlockSpec((tk, m), lambda k: (k, 0)), pl.BlockSpec((tk, n), lambda k: (k, 0)), ANY],
        out_specs=[o_spec, o_spec], out_shape=[jax.ShapeDtypeStruct((m, n), F32), jax.ShapeDtypeStruct((m, n), BF16)],
        compiler_params=_cp(("arbitrary",), VMEM_LIMIT), name=name)(a, b, after)
    return o32.reshape(N_CHIP, m // N_CHIP, n), o16.reshape(N_CHIP, m // N_CHIP, n)


def _coords():
    x, y, c = lax.axis_index("x"), lax.axis_index("y"), lax.axis_index("c")
    chips = [(1 - x, y), (x, 1 - y), (1 - x, 1 - y)]
    return x, y, c, chips


def _pair_reduce(c_idx, grads, grads_b, smalls, name):
    ng, ns = len(grads), len(smalls)
    halves = [g.shape[1] // 2 for g in grads]

    def body(c_ref, *refs):
        g_in, gb_any = refs[:ng], refs[ng:2 * ng]
        s_own, s_any = refs[2 * ng:2 * ng + ns], refs[2 * ng + ns:2 * ng + 2 * ns]
        o = refs[2 * ng + 2 * ns:4 * ng + 3 * ns]
        lands = refs[4 * ng + 3 * ns:5 * ng + 4 * ns]
        send, recv = refs[5 * ng + 4 * ns:]
        x, y, c, _ = _coords()
        j = pl.program_id(0)

        def big(i, blk):
            return pltpu.make_async_remote_copy(
                src_ref=gb_any[i].at[blk, pl.ds((1 - c) * halves[i], halves[i])], dst_ref=lands[i].at[blk],
                send_sem=send.at[i, blk], recv_sem=recv.at[i, blk], device_id=(x, y, 1 - c), device_id_type=MESH)

        def small(i):
            return pltpu.make_async_remote_copy(
                src_ref=s_any[i].at[1 - c], dst_ref=lands[ng + i],
                send_sem=send.at[ng + i, 0], recv_sem=recv.at[ng + i, 0], device_id=(x, y, 1 - c), device_id_type=MESH)

        @pl.when(j == 0)
        def _():
            for blk in range(N_CHIP):
                for i in range(ng):
                    big(i, blk).start()
            for i in range(ns):
                small(i).start()

        for i in range(ng):
            big(i, j).wait_recv()
            tot = g_in[i][...] + lands[i][j].astype(F32)
            o[i][...] = tot
            o[ng + i][...] = tot.astype(BF16)

        @pl.when(j == N_CHIP - 1)
        def _():
            for i in range(ns):
                small(i).wait_recv()
                o[2 * ng + i][...] = s_own[i][...] + lands[ng + i][...]
                small(i).wait_send()
            for blk in range(N_CHIP):
                for i in range(ng):
                    big(i, blk).wait_send()

    in_specs = [pl.BlockSpec((None, None, halves[i], g.shape[2]), lambda b, c: (b, c[0], 0, 0)) for i, g in enumerate(grads)]
    in_specs += [ANY] * ng
    in_specs += [pl.BlockSpec((None, s.shape[0] // 2, s.shape[1]), lambda b, c: (c[0], 0, 0)) for s in smalls]
    in_specs += [ANY] * ns
    blk = [pl.BlockSpec((None, halves[i], g.shape[2]), lambda b, c: (b, 0, 0)) for i, g in enumerate(grads)]
    out_specs = blk + blk + [pl.BlockSpec((s.shape[0] // 2, s.shape[1]), lambda b, c: (0, 0)) for s in smalls]
    out_shape = [jax.ShapeDtypeStruct((N_CHIP, halves[i], g.shape[2]), F32) for i, g in enumerate(grads)]
    out_shape += [jax.ShapeDtypeStruct((N_CHIP, halves[i], g.shape[2]), BF16) for i, g in enumerate(grads)]
    out_shape += [jax.ShapeDtypeStruct((s.shape[0] // 2, s.shape[1]), F32) for s in smalls]
    scratch = [pltpu.VMEM((N_CHIP, halves[i], g.shape[2]), BF16) for i, g in enumerate(grads)]
    scratch += [pltpu.VMEM((s.shape[0] // 2, s.shape[1]), F32) for s in smalls]
    scratch += [pltpu.SemaphoreType.DMA((ng + ns, N_CHIP)), pltpu.SemaphoreType.DMA((ng + ns, N_CHIP))]
    grads4 = [g.reshape(N_CHIP, 2, halves[i], g.shape[2]) for i, g in enumerate(grads)]
    smalls3 = [s.reshape(2, s.shape[0] // 2, s.shape[1]) for s in smalls]
    return pl.pallas_call(
        body, out_shape=out_shape,
        grid_spec=pltpu.PrefetchScalarGridSpec(num_scalar_prefetch=1, grid=(N_CHIP,), in_specs=in_specs,
                                               out_specs=out_specs, scratch_shapes=scratch),
        compiler_params=_cp(("arbitrary",), VMEM_LIMIT), name=name)(c_idx, *grads4, *grads_b, *smalls3, *smalls3)


def _grad_matmul_pair(c_idx, a, b, smalls, small_dtypes, after, *, name, tk=2048):
    t, m = a.shape
    bn = b.shape[1] // N_CHIP
    nk = t // tk
    hr = m // 2
    ns = len(smalls)

    def body(c_ref, a_ref, b_ref, *refs):
        s_own, s_any = refs[:ns], refs[ns:2 * ns]
        o32, o16 = refs[2 * ns + 1], refs[2 * ns + 2]
        o_small = refs[2 * ns + 3:3 * ns + 3]
        acc, tb, land, st16 = refs[3 * ns + 3:3 * ns + 7]
        s_land, s_stage = refs[3 * ns + 7:4 * ns + 7], refs[4 * ns + 7:5 * ns + 7]
        send, recv, loc = refs[5 * ns + 7:]
        x, y, c, _ = _coords()
        sibling = dict(device_id=(x, y, 1 - c), device_id_type=MESH)
        j, kk = pl.program_id(0), pl.program_id(1)
        mine = pl.ds(pl.multiple_of(c * hr, hr), hr)
        theirs = pl.ds(pl.multiple_of((1 - c) * hr, hr), hr)

        def to_sibling(blk):
            return pltpu.make_async_remote_copy(src_ref=tb, dst_ref=land.at[blk], send_sem=send.at[blk],
                                                recv_sem=recv.at[blk], **sibling)

        def small(i):
            return pltpu.make_async_remote_copy(src_ref=s_any[i].at[1 - c], dst_ref=s_land[i], send_sem=send.at[N_CHIP + i],
                                                recv_sem=recv.at[N_CHIP + i], **sibling)

        def written(blk):
            return (pltpu.make_async_copy(acc.at[blk % 2, mine], o32.at[blk], loc.at[0]),
                    pltpu.make_async_copy(st16, o16.at[blk], loc.at[1]))

        def finish(blk):
            to_sibling(blk).wait_recv()

            @pl.when(blk > 0)
            def _():
                for cp in written(blk - 1):
                    cp.wait()

            tot = acc[blk % 2, mine, :] + land[blk].astype(F32)
            acc[blk % 2, mine, :] = tot
            st16[...] = tot.astype(BF16)
            for cp in written(blk):
                cp.start()

        def small_out(i):
            return pltpu.make_async_copy(s_stage[i], o_small[i], loc.at[2 + i])

        @pl.when((j == 0) & (kk == 0))
        def _():
            for i in range(ns):
                small(i).start()

        @pl.when((j == 1) & (kk == 0))
        def _():
            for i in range(ns):
                small(i).wait_recv()
                s_stage[i][...] = (s_own[i][...] + s_land[i][...]).astype(small_dtypes[i])
                small_out(i).start()

        @pl.when((j > 0) & (kk == 0))
        def _():
            finish(j - 1)

        part = lax.dot_general(a_ref[...], b_ref[...], (((0,), (0,)), ((), ())), preferred_element_type=F32)
        slot = lax.rem(j, 2)

        @pl.when(kk == 0)
        def _():
            acc[slot] = part

        @pl.when(kk > 0)
        def _():
            acc[slot] += part

        @pl.when(kk == nk - 1)
        def _():
            @pl.when(j > 0)
            def _():
                to_sibling(j - 1).wait_send()

            tb[...] = acc[slot, theirs, :].astype(BF16)
            to_sibling(j).start()

        @pl.when((j == N_CHIP - 1) & (kk == nk - 1))
        def _():
            finish(j)
            for i in range(ns):
                small_out(i).wait()
                small(i).wait_send()
            for cp in written(j):
                cp.wait()
            to_sibling(j).wait_send()

    halves = [(s.shape[0] // 2, s.shape[1]) for s in smalls]
    in_specs = [pl.BlockSpec((tk, m), lambda j, k, c: (k, 0)), pl.BlockSpec((tk, bn), lambda j, k, c: (k, j))]
    in_specs += [pl.BlockSpec((None,) + h, lambda j, k, c: (c[0], 0, 0)) for h in halves] + [ANY] * ns + [ANY]
    out_shape = [jax.ShapeDtypeStruct((N_CHIP, hr, bn), F32), jax.ShapeDtypeStruct((N_CHIP, hr, bn), BF16)]
    out_shape += [pltpu.HBM(h, dt) for h, dt in zip(halves, small_dtypes)]
    scratch = [pltpu.VMEM((2, m, bn), F32), pltpu.VMEM((hr, bn), BF16),
               pltpu.VMEM((N_CHIP, hr, bn), BF16), pltpu.VMEM((hr, bn), BF16)]
    scratch += [pltpu.VMEM(h, F32) for h in halves] + [pltpu.VMEM(h, dt) for h, dt in zip(halves, small_dtypes)]
    scratch += [pltpu.SemaphoreType.DMA((N_CHIP + ns,)), pltpu.SemaphoreType.DMA((N_CHIP + ns,)),
                pltpu.SemaphoreType.DMA((2 + ns,))]
    smalls3 = [s.reshape((2,) + h) for s, h in zip(smalls, halves)]
    outs = pl.pallas_call(
        body, out_shape=out_shape,
        grid_spec=pltpu.PrefetchScalarGridSpec(num_scalar_prefetch=1, grid=(N_CHIP, nk), in_specs=in_specs,
                                               out_specs=[ANY, ANY] + [_HBM] * ns, scratch_shapes=scratch),
        compiler_params=_cp(("arbitrary", "arbitrary"), VMEM_LIMIT), name=name)(c_idx, a, b, *smalls3, *smalls3, after)
    return outs[0], outs[1], list(outs[2:])


_HBM = pl.BlockSpec(memory_space=pltpu.HBM)
_SEM = pl.BlockSpec(memory_space=pltpu.SEMAPHORE)


def _split_copies(ins, lands, ng, send, recv, arriving):
    x, y, c, chips = _coords()
    b = 2 * x + y
    copies = []
    for i in range(len(ins)):
        for k in range(3):
            blk = 2 * chips[k][0] + chips[k][1]
            src, dst, got = (ins[i].at[blk], lands[i].at[k], lands[i].at[k]) if i < ng else (ins[i], lands[i].at[b], lands[i].at[blk])
            sems = dict(send_sem=send.at[3 * i + k], recv_sem=recv.at[3 * i + k], device_id=(*chips[k], c), device_id_type=MESH)
            if arriving:
                copies.append(pltpu.make_async_remote_copy(src_ref=got, dst_ref=got, **sems))
            else:
                copies.append(pltpu.make_async_remote_copy(src_ref=src, dst_ref=dst, **sems))
    return copies


def _exchange_begin(sums_b, smalls, name):
    ng, n = len(sums_b), len(sums_b) + len(smalls)
    srcs = list(sums_b) + list(smalls)
    lands = [lax.empty((3,) + g.shape[1:], g.dtype) for g in sums_b] + [lax.empty((N_CHIP,) + s.shape, s.dtype) for s in smalls]

    def body(*refs):
        ins, land_refs = refs[:n], refs[n:2 * n]
        send, recv = refs[2 * n], refs[2 * n + 1]
        token = refs[4 * n + 2]
        for cp in _split_copies(ins, land_refs, ng, send, recv, False):
            cp.start()
        token[...] = jnp.zeros_like(token)

    hbm = lambda a: pltpu.HBM(a.shape, a.dtype)
    outs = pl.pallas_call(
        body, name=name,
        out_shape=(pltpu.SemaphoreType.DMA((3 * n,)), pltpu.SemaphoreType.DMA((3 * n,)), *[hbm(a) for a in srcs + lands],
                   jax.ShapeDtypeStruct((8, 128), F32)),
        in_specs=[_HBM] * (2 * n), out_specs=(_SEM, _SEM, *[_HBM] * (2 * n), pl.BlockSpec(memory_space=pltpu.VMEM)),
        input_output_aliases={i: 2 + i for i in range(2 * n)},
        compiler_params=pltpu.CompilerParams(has_side_effects=pltpu.SideEffectType.DATAFLOW_SIDE_EFFECTING),
    )(*[pltpu.with_memory_space_constraint(a, pltpu.HBM) for a in srcs + lands])
    return outs[0], outs[1], list(outs[2:2 + n]), list(outs[2 + n:2 + 2 * n]), outs[2 + 2 * n]


def _exchange_end(send, recv, srcs, lands, ng, which, after, name):
    n = len(srcs)
    after = list(after)

    def body(*refs):
        ins, land_refs = refs[:n], refs[n:2 * n]
        send_ref, recv_ref = refs[2 * n], refs[2 * n + 1]
        outgoing = _split_copies(ins, land_refs, ng, send_ref, recv_ref, False)
        arriving = _split_copies(ins, land_refs, ng, send_ref, recv_ref, True)
        for i in which:
            for cp in outgoing[3 * i:3 * i + 3]:
                cp.wait_send()
        for i in which:
            for cp in arriving[3 * i:3 * i + 3]:
                cp.wait_recv()

    hbm = lambda a: pltpu.HBM(a.shape, a.dtype)
    outs = pl.pallas_call(
        body, name=name, out_shape=tuple(hbm(a) for a in list(srcs) + list(lands)),
        in_specs=[_HBM] * (2 * n) + [_SEM, _SEM] + [ANY] * len(after), out_specs=tuple([_HBM] * (2 * n)),
        input_output_aliases={i: i for i in range(2 * n)},
        compiler_params=pltpu.CompilerParams(has_side_effects=pltpu.SideEffectType.DATAFLOW_SIDE_EFFECTING),
    )(*srcs, *lands, send, recv, *after)
    return list(outs[:n]), list(outs[n:])


def _chip_reduce(bc_idx, sums, recvd, smalls_slots, smalls_own, name, steps=4):
    ng, ns = len(sums), len(smalls_slots)
    n = ng + ns
    assert steps >= 2
    halves = [g.shape[1] for g in sums] + [s.shape[1] for s in smalls_slots]
    rows = [g.shape[1] // steps for g in sums]

    def body(bc_ref, *refs):
        own, rx = refs[:ng], refs[ng:2 * ng]
        sl = refs[2 * ng:2 * ng + ns]
        sl_own = refs[2 * ng + ns:2 * ng + 2 * ns]
        o = refs[2 * ng + 2 * ns:2 * ng + 2 * ns + n]
        tiles = refs[2 * ng + 2 * ns + n:2 * ng + 2 * ns + 2 * n]
        keep, send, recv = refs[2 * ng + 2 * ns + 2 * n:]
        x, y, c, _ = _coords()
        sibling = dict(device_id=(x, y, 1 - c), device_id_type=MESH)
        r = pl.program_id(0)

        def writes(i, step, slot):
            dst = o[i].at[pl.ds(c * halves[i] + step * rows[i], rows[i])]
            return (pltpu.make_async_copy(tiles[i].at[slot], dst, keep.at[i, slot]),
                    pltpu.make_async_remote_copy(src_ref=tiles[i].at[slot], dst_ref=dst, send_sem=send.at[i, slot],
                                                 recv_sem=recv.at[i, step], **sibling))

        def small_writes(i):
            dst = o[i].at[pl.ds(c * halves[i], halves[i])]
            return (pltpu.make_async_copy(tiles[i], dst, keep.at[i, 0]),
                    pltpu.make_async_remote_copy(src_ref=tiles[i], dst_ref=dst, send_sem=send.at[i, 0],
                                                 recv_sem=recv.at[i, 0], **sibling))

        def arriving(i, step, nrows):
            dst = o[i].at[pl.ds((1 - c) * halves[i] + step * nrows, nrows)]
            return pltpu.make_async_remote_copy(src_ref=dst, dst_ref=dst, send_sem=send.at[i, 0], recv_sem=recv.at[i, step],
                                                **sibling)

        def finish(step, slot):
            for i in range(ng):
                local, remote = writes(i, step, slot)
                local.wait()
                remote.wait_send()

        @pl.when(r >= 2)
        def _():
            finish(r - 2, r % 2)

        for i in range(ng):
            tot = own[i][...]
            for j in range(3):
                tot = tot + rx[i][j].astype(F32)
            tiles[i][r % 2] = tot
            for cp in writes(i, r, r % 2):
                cp.start()

        @pl.when(r == 0)
        def _():
            for i in range(ns):
                term = [jnp.where(bc_ref[0] == kk, sl_own[i][...], sl[i][kk]).astype(F32) for kk in range(N_CHIP)]
                tiles[ng + i][...] = ((term[0] + term[1]) + term[2]) + term[3]
                for cp in small_writes(ng + i):
                    cp.start()

        @pl.when(r == steps - 1)
        def _():
            finish(steps - 2, (steps - 2) % 2)
            finish(steps - 1, (steps - 1) % 2)
            for i in range(ns):
                local, remote = small_writes(ng + i)
                local.wait()
                remote.wait_send()
                arriving(ng + i, 0, halves[ng + i]).wait_recv()
            for i in range(ng):
                for step in range(steps):
                    arriving(i, step, rows[i]).wait_recv()

    in_specs = [pl.BlockSpec((None, rows[i], g.shape[2]), lambda r, bc: (bc[0], r, 0)) for i, g in enumerate(sums)]
    in_specs += [pl.BlockSpec((3, rows[i], g.shape[2]), lambda r, bc: (0, r, 0)) for i, g in enumerate(sums)]
    in_specs += [pl.BlockSpec(s.shape, lambda r, bc: (0, 0, 0)) for s in smalls_slots]
    in_specs += [pl.BlockSpec(s.shape[1:], lambda r, bc: (0, 0)) for s in smalls_slots]
    out_shape = [jax.ShapeDtypeStruct((2 * g.shape[1], g.shape[2]), F32) for g in sums]
    out_shape += [jax.ShapeDtypeStruct((2 * s.shape[1], s.shape[2]), F32) for s in smalls_slots]
    scratch = [pltpu.VMEM((2, rows[i], g.shape[2]), F32) for i, g in enumerate(sums)]
    scratch += [pltpu.VMEM(s.shape[1:], F32) for s in smalls_slots]
    scratch += [pltpu.SemaphoreType.DMA((n, 2)), pltpu.SemaphoreType.DMA((n, 2)), pltpu.SemaphoreType.DMA((n, steps))]
    return list(pl.pallas_call(
        body, out_shape=out_shape,
        grid_spec=pltpu.PrefetchScalarGridSpec(num_scalar_prefetch=1, grid=(steps,), in_specs=in_specs,
                                               out_specs=[ANY] * n, scratch_shapes=scratch),
        compiler_params=_cp(("arbitrary",), VMEM_LIMIT), name=name)(bc_idx, *sums, *recvd, *smalls_slots, *smalls_own))


def _adamw_math(w, g, m, v):
    m2 = ADAM_B1 * m + (1.0 - ADAM_B1) * g
    v2 = ADAM_B2 * v + (1.0 - ADAM_B2) * (g * g)
    m_hat = m2 / (1.0 - ADAM_B1 ** ADAM_STEP)
    v_hat = v2 / (1.0 - ADAM_B2 ** ADAM_STEP)
    delta = -ADAM_LR * (m_hat / (jnp.sqrt(v_hat) + ADAM_EPS) + ADAM_WD * w)
    return delta, m2, v2


def _adamw_big(ws, gs, ms, vs, name, steps=8):
    n = len(ws)

    def body(*refs):
        for i in range(n):
            w_ref, g_ref, m_ref, v_ref = (refs[k * n + i] for k in range(4))
            d_ref, m2_ref, v2_ref, g2_ref = (refs[(4 + k) * n + i] for k in range(4))
            gv = g_ref[...]
            d_ref[...], m2_ref[...], v2_ref[...] = _adamw_math(w_ref[...], gv, m_ref[...], v_ref[...])
            g2_ref[...] = gv

    specs = [pl.BlockSpec((w.shape[0] // steps, w.shape[1]), lambda i: (i, 0)) for w in ws]
    ahead = [pl.BlockSpec((w.shape[0] // steps, w.shape[1]), lambda i: (i, 0), pipeline_mode=pl.Buffered(4)) for w in ws]
    shapes = [jax.ShapeDtypeStruct(w.shape, F32) for w in ws]

    def streamed(*refs):
        pltpu.emit_pipeline(body, grid=(steps,), in_specs=ahead * 4, out_specs=specs * 4)(*refs)

    outs = pl.pallas_call(
        streamed, in_specs=[ANY] * (4 * n), out_specs=[ANY] * (4 * n), out_shape=shapes * 4,
        compiler_params=_cp(None, VMEM_LIMIT), name=name)(*ws, *gs, *ms, *vs)
    return [tuple(outs[k * n + i] for k in range(4)) for i in range(n)]


def _adamw_small(b_idx, sv, sw, vecs, conv, ws):
    nv = len(vecs)
    cols = conv[0].shape[1]

    def body(b_ref, sv_ref, sw_ref, *refs):
        ins, outs = refs[:3 * nv + 6], refs[3 * nv + 6:]
        for i in range(nv):
            g = sv_ref[i:i + 1, :]
            w_ref, m_ref, v_ref = ins[3 * i:3 * i + 3]
            d_ref, m2_ref, v2_ref, g_ref = outs[4 * i:4 * i + 4]
            d_ref[...], m2_ref[...], v2_ref[...] = _adamw_math(w_ref[...], g, m_ref[...], v_ref[...])
            g_ref[...] = g
        g = sv_ref[8:8 + conv[0].shape[0], pl.ds(pl.multiple_of(b_ref[0] * cols, cols), cols)]
        w_ref, m_ref, v_ref = ins[3 * nv:3 * nv + 3]
        d_ref, m2_ref, v2_ref, g_ref = outs[4 * nv:4 * nv + 4]
        d_ref[...], m2_ref[...], v2_ref[...] = _adamw_math(w_ref[...], g, m_ref[...], v_ref[...])
        g_ref[...] = g
        w_ref, m_ref, v_ref = ins[3 * nv + 3:]
        d_ref, m2_ref, v2_ref, g_ref, one_ref = outs[4 * nv + 4:]
        g = sw_ref[...]
        d_ref[...], m2_ref[...], v2_ref[...] = _adamw_math(w_ref[...], g, m_ref[...], v_ref[...])
        g_ref[...] = g
        one_ref[...] = sv_ref[nv:nv + 1, 0:1]

    flat = [a for grp in vecs for a in grp] + list(conv) + list(ws)
    out_shape = [jax.ShapeDtypeStruct(grp[0].shape, F32) for grp in list(vecs) + [conv, ws] for _ in range(4)]
    out_shape += [jax.ShapeDtypeStruct((1, 1), F32)]
    vmem = pl.BlockSpec(memory_space=pltpu.VMEM)
    outs = pl.pallas_call(
        body, out_shape=out_shape, in_specs=[pl.BlockSpec(memory_space=pltpu.SMEM)] + [vmem] * (2 + len(flat)),
        out_specs=[vmem] * len(out_shape), name="adamw_small")(b_idx, sv, sw, *flat)
    return [tuple(outs[4 * i:4 * i + 4]) for i in range(nv + 2)], outs[4 * nv + 8]


def kernel(x, mem, norm_mix_g, w_in, conv_w, gm_ln_g, gm_ln_b, gm_ws, gm_bs, w_out, norm_x_g, norm_mem_g, w_q, w_kv, w_xo, norm_final_g, loss_target, m_norm_mix_g, m_w_in, m_conv_w, m_gm_ln_g, m_gm_ln_b, m_gm_ws, m_gm_bs, m_w_out, m_norm_x_g, m_norm_mem_g, m_w_q, m_w_kv, m_w_xo, m_norm_final_g, v_norm_mix_g, v_w_in, v_conv_w, v_gm_ln_g, v_gm_ln_b, v_gm_ws, v_gm_bs, v_w_out, v_norm_x_g, v_norm_mem_g, v_w_q, v_w_kv, v_w_xo, v_norm_final_g):
    t = x.shape[1]
    xi = lax.axis_index("x")
    yi = lax.axis_index("y")
    ci = lax.axis_index("c")
    b_idx = jnp.reshape(2 * xi + yi, (1,)).astype(jnp.int32)
    c_idx = jnp.reshape(ci, (1,)).astype(jnp.int32)

    x2d, mem2d, tgt = x[0], mem[0], loss_target[0]
    big = [w_in[0], w_out[0], w_q[0], w_kv[0], w_xo[0]]
    big_m = [m_w_in[0], m_w_out[0], m_w_q[0], m_w_kv[0], m_w_xo[0]]
    big_v = [v_w_in[0], v_w_out[0], v_w_q[0], v_w_kv[0], v_w_xo[0]]
    g3 = norm_final_g.reshape(1, D)

    def pad8(a):
        return jnp.pad(a, ((0, 8 - a.shape[0]), (0, 0)))

    own_blocks, (wc, wct, bsb) = _cast_shards(b_idx, big, gm_ws[0], gm_bs[0])

    proj, hb, win_f, cw8, (wq_f,) = _proj_gather(
        b_idx, x2d, norm_mix_g, own_blocks[0], pad8(conv_w[0]), [own_blocks[2]])
    mixin, (wout_f, wkv_f, wxo_f) = _mixer_fwd(
        proj, cw8, gm_ln_g, gm_ln_b, wc, bsb, [own_blocks[1], own_blocks[3], own_blocks[4]])
    wout2, wq2, wxo2 = wout_f.reshape(MIX, D), wq_f.reshape(D, D), wxo_f.reshape(D, D)
    k, v = _mem_fwd(mem2d, norm_mem_g, wkv_f)

    (loss_row, dmix, dx1b, h2b, dq, ob, dx2b, dk, dv, dg2, dg3) = _tail(
        x2d, tgt, mixin, wout2, wq2, wxo2, k, v, norm_x_g, g3)
    dwkv, dwkv_b, dgm = _mem_bwd(mem2d, norm_mem_g, dk, dv, wkv_f)
    dproj, sv, dwc, grad_x = _mixer_bwd(
        proj, dmix, cw8, gm_ln_g, gm_ln_b, wc, wct, bsb, win_f, x2d, dx1b, norm_mix_g, [dg2, dgm, dg3], loss_row)

    bc_idx = jnp.concatenate([b_idx, c_idx])
    sw = dwc.reshape(HEADS * CH, CH)
    dwin_sum, dwin_sum_b, psmall = _grad_matmul_pair(c_idx, hb, dproj, [sv, sw], [F32, BF16], dgm, name="grad_w_in")
    sums_b = [dwin_sum]
    send_b, recv_b, src_b, land_b, token_b = _exchange_begin([dwin_sum_b], psmall, "exchange_b_begin")

    dwxo, dwxo_b = _grad_matmul(ob, dx2b, token_b, name="grad_w_xo", tk=2048)
    dwq, dwq_b = _grad_matmul(h2b, dq, token_b, name="grad_w_q", tk=2048)
    dwout, dwout_b = _grad_matmul(mixin, dx1b, token_b, name="grad_w_out")
    ps_a = _pair_reduce(c_idx, [dwout, dwkv, dwq, dwxo], [dwout_b, dwkv_b, dwq_b, dwxo_b], [], "pair_reduce_a")
    sums_a, sums_a_b = list(ps_a[:4]), list(ps_a[4:8])
    send_a, recv_a, src_a, land_a, token_a = _exchange_begin(sums_a_b, [], "exchange_a_begin")

    src_b, rx2b = _exchange_end(send_b, recv_b, src_b, land_b, 1, [0, 1, 2], [token_a], "exchange_b_end")
    gwin, svf, swf = _chip_reduce(bc_idx, sums_b, rx2b[:1], rx2b[1:], src_b[1:], "chip_reduce_b")
    out_b = _adamw_big(big[:1], [gwin], big_m[:1], big_v[:1], "adamw_w_in")[0]

    row = lambda a: a.reshape(1, D)
    mat = lambda a: a.reshape(HEADS * CH, CH)
    small, loss = _adamw_small(
        b_idx, svf, swf,
        [(row(norm_mix_g), row(m_norm_mix_g), row(v_norm_mix_g)), (row(norm_x_g), row(m_norm_x_g), row(v_norm_x_g)),
         (row(norm_mem_g), row(m_norm_mem_g), row(v_norm_mem_g)), (row(norm_final_g), row(m_norm_final_g), row(v_norm_final_g)),
         (row(gm_ln_g), row(m_gm_ln_g), row(v_gm_ln_g)), (row(gm_ln_b), row(m_gm_ln_b), row(v_gm_ln_b)),
         (row(gm_bs), row(m_gm_bs), row(v_gm_bs))],
        (conv_w[0], m_conv_w[0], v_conv_w[0]), (mat(gm_ws), mat(m_gm_ws), mat(v_gm_ws)))

    def finish_a(part, src, land, after, tag):
        src, land = _exchange_end(send_a, recv_a, src, land, 4, part, after, "exchange_a%s_end" % tag)
        grads = _chip_reduce(bc_idx, [sums_a[i] for i in part], [land[i] for i in part], [], [], "chip_reduce_a" + tag,
                             steps=2)
        ids = [(1, 3, 2, 4)[i] for i in part]
        outs = _adamw_big([big[i] for i in ids], grads, [big_m[i] for i in ids], [big_v[i] for i in ids], "adamw_a" + tag,
                          steps=4)
        return src, land, outs

    src_a, land_a, (out_wout, out_wkv) = finish_a([0, 1], src_a, land_a, [out_b[0], small[0][0]], "1")
    _, _, (out_wq, out_wxo) = finish_a([2, 3], src_a, land_a, [out_wout[0]], "2")

    def unpack(k):
        vec = lambda i: small[i][k]
        return [vec(0), out_b[k][None], small[7][k][None], vec(4), vec(5), small[8][k].reshape(1, HEADS, CH, CH),
                vec(6).reshape(1, HEADS, CH), out_wout[k][None], vec(1), vec(2), out_wq[k][None], out_wkv[k][None],
                out_wxo[k][None], vec(3).reshape(D)]

    return (loss.reshape(()), grad_x[None], *unpack(3), *unpack(0), *unpack(1), *unpack(2))
```
